```python
import jax, jax.numpy as jnp
from jax import lax
import numpy as np

D_MODEL = 1024
BATCH = 8
SEQ = 4096
DEPTH = 2

HEAD_DIM = 64
A_HEADS = 6
DILATED_PATTERNS = ((128, 1), (512, 4), (2048, 16))
CONV_CH = 256
CONV_K = 3
C_Q_HEADS = 6
C_KV_HEADS = 2
C_GROUP = C_Q_HEADS // C_KV_HEADS
C_WINDOW = 128
BLOCK = 128
D_FF = 4 * D_MODEL
EPS = 1e-6

A_WIDTH = A_HEADS * HEAD_DIM
C_WIDTH = C_Q_HEADS * HEAD_DIM
KV_WIDTH = C_KV_HEADS * HEAD_DIM
MIX_WIDTH = A_WIDTH + CONV_CH + C_WIDTH
IN_SPLITS = (A_WIDTH, A_WIDTH, A_WIDTH, CONV_CH, CONV_CH, CONV_CH, C_WIDTH, KV_WIDTH, KV_WIDTH)
IN_WIDTH = sum(IN_SPLITS)
SPLIT_POINTS = tuple(int(p) for p in np.cumsum(IN_SPLITS)[:-1])

kernel_name = "hymba_dilated_conv_swa_sink_trunk"


def rms_normalize(t):
    t32 = t.astype(jnp.float32)
    return (t32 * lax.rsqrt(jnp.mean(t32 * t32, axis=-1, keepdims=True) + EPS)).astype(t.dtype)


def rmsnorm(t, g):
    return rms_normalize(t) * g


def banded_attention(q, k, v, max_dist, sink_logits=None):
    n, L, hkv, g, dh = q.shape
    nb = -(-L // BLOCK)
    lp = nb * BLOCK
    pad = lp - L
    q = jnp.pad(q, ((0, 0), (0, pad), (0, 0), (0, 0), (0, 0)))
    kv_pad = ((0, 0), (BLOCK, pad), (0, 0), (0, 0))
    k = jnp.pad(k, kv_pad).reshape(n, nb + 1, BLOCK, hkv, dh)
    v = jnp.pad(v, kv_pad).reshape(n, nb + 1, BLOCK, hkv, dh)
    k2 = jnp.concatenate([k[:, :-1], k[:, 1:]], axis=2)
    v2 = jnp.concatenate([v[:, :-1], v[:, 1:]], axis=2)
    qb = q.reshape(n, nb, BLOCK, hkv, g, dh)
    s = jnp.einsum('nbqhgd,nbkhd->nbhgqk', qb, k2).astype(jnp.float32) * (dh ** -0.5)
    qi = jnp.arange(BLOCK)[:, None]
    kj = jnp.arange(2 * BLOCK)[None, :]
    dist = BLOCK + qi - kj
    band = (dist >= 0) & (dist <= max_dist)
    first = jnp.arange(nb)[:, None, None] == 0
    mask = band[None] & ~(first & (kj < BLOCK)[None])
    s = jnp.where(mask[None, :, None, None], s, -jnp.inf)
    if sink_logits is not None:
        sink = jnp.broadcast_to(sink_logits.astype(jnp.float32)[None, None, :, :, None, None], s.shape[:-1] + (1,))
        lse = jax.nn.logsumexp(jnp.concatenate([s, sink], axis=-1), axis=-1)
    else:
        lse = jax.nn.logsumexp(s, axis=-1)
    p = jnp.exp(s - lse[..., None]).astype(v2.dtype)
    o = jnp.einsum('nbhgqk,nbkhd->nbqhgd', p, v2).reshape(n, lp, hkv, g, dh)[:, :L]
    lse = lse.transpose(0, 1, 4, 2, 3).reshape(n, lp, hkv, g)[:, :L]
    return o, lse


def to_residues(t, dil):
    b, s, h, dh = t.shape
    return t.reshape(b, s // dil, dil, h, dh).transpose(0, 2, 1, 3, 4).reshape(b * dil, s // dil, h, dh)


def from_residues(t, dil, b):
    sub = t.shape[1]
    rest = t.shape[2:]
    t = t.reshape((b, dil, sub) + rest)
    t = jnp.moveaxis(t, 1, 2)
    return t.reshape((b, sub * dil) + rest)


def dilated_attention(q, k, v):
    b = q.shape[0]
    outs, lses = [], []
    for window, dil in DILATED_PATTERNS:
        o, lse = banded_attention(to_residues(q, dil)[:, :, :, None], to_residues(k, dil),
                                  to_residues(v, dil), window // dil)
        outs.append(from_residues(o[:, :, :, 0], dil, b))
        lses.append(from_residues(lse[..., 0], dil, b))
    wts = jax.nn.softmax(jnp.stack(lses), axis=0)
    return jnp.einsum('pbsh,pbshd->bshd', wts.astype(q.dtype), jnp.stack(outs))


def short_gated_conv(gate_b, gate_c, xb, w):
    s = xb.shape[1]
    u = gate_c * xb
    up = jnp.pad(u, ((0, 0), (CONV_K - 1, 0), (0, 0)))
    y = sum(w[i] * up[:, i:i + s] for i in range(CONV_K))
    return gate_b * y


def _fwd_setup_inputs(seed: int = 0) -> dict:
    key = jax.random.key(seed)
    ks = jax.random.split(key, 12)
    nrm = jax.random.normal
    x = nrm(ks[0], (BATCH, SEQ, D_MODEL), jnp.float32)
    w_in = nrm(ks[1], (DEPTH, D_MODEL, IN_WIDTH), jnp.float32) * D_MODEL ** -0.5
    conv_w = nrm(ks[2], (DEPTH, CONV_K, CONV_CH), jnp.float32) * CONV_K ** -0.5
    sinks = nrm(ks[3], (DEPTH, C_KV_HEADS, C_GROUP), jnp.float32) * 0.5
    g_mix = 1.0 + 0.02 * nrm(ks[4], (DEPTH, D_MODEL), jnp.float32)
    g_group = 1.0 + 0.02 * nrm(ks[5], (DEPTH, MIX_WIDTH), jnp.float32)
    w_o = nrm(ks[6], (DEPTH, MIX_WIDTH, D_MODEL), jnp.float32) * MIX_WIDTH ** -0.5
    g_mlp = 1.0 + 0.02 * nrm(ks[7], (DEPTH, D_MODEL), jnp.float32)
    w_ff_in = nrm(ks[8], (DEPTH, D_MODEL, D_FF), jnp.float32) * D_MODEL ** -0.5
    w_ff_out = nrm(ks[9], (DEPTH, D_FF, D_MODEL), jnp.float32) * D_FF ** -0.5
    g_final = 1.0 + 0.02 * nrm(ks[10], (D_MODEL,), jnp.float32)
    return {"x": x, "w_in": w_in, "conv_w": conv_w, "sinks": sinks, "g_mix": g_mix,
            "g_group": g_group, "w_o": w_o, "g_mlp": g_mlp, "w_ff_in": w_ff_in,
            "w_ff_out": w_ff_out, "g_final": g_final}


def _fwd_reference(x, w_in, conv_w, sinks, g_mix, g_group, w_o, g_mlp, w_ff_in, w_ff_out, g_final):
    b, s, _ = x.shape
    for l in range(DEPTH):
        h = rmsnorm(x, g_mix[l])
        z = jnp.einsum('bsd,de->bse', h, w_in[l])
        qa, ka, va, gb, gc, xb, qc, kc, vc = jnp.split(z, SPLIT_POINTS, axis=-1)
        ya = dilated_attention(qa.reshape(b, s, A_HEADS, HEAD_DIM),
                               ka.reshape(b, s, A_HEADS, HEAD_DIM),
                               va.reshape(b, s, A_HEADS, HEAD_DIM)).reshape(b, s, A_WIDTH)
        yb = short_gated_conv(gb, gc, xb, conv_w[l])
        oc, _ = banded_attention(qc.reshape(b, s, C_KV_HEADS, C_GROUP, HEAD_DIM),
                                 kc.reshape(b, s, C_KV_HEADS, HEAD_DIM),
                                 vc.reshape(b, s, C_KV_HEADS, HEAD_DIM),
                                 C_WINDOW - 1, sinks[l])
        yc = oc.reshape(b, s, C_WIDTH)
        y = jnp.concatenate([rms_normalize(ya), rms_normalize(yb), rms_normalize(yc)], axis=-1) * g_group[l]
        x = x + jnp.einsum('bse,ed->bsd', y, w_o[l])
        h2 = rmsnorm(x, g_mlp[l])
        a = jnp.square(jax.nn.relu(jnp.einsum('bsd,df->bsf', h2, w_ff_in[l])))
        x = x + jnp.einsum('bsf,fd->bsd', a, w_ff_out[l])
    return rmsnorm(x, g_final)


import jax as _jax
import jax.numpy as _jnp

TWIN_FORMAT = 'train_step'
FWD_PARAMS = ['x', 'w_in', 'conv_w', 'sinks', 'g_mix', 'g_group', 'w_o', 'g_mlp', 'w_ff_in', 'w_ff_out', 'g_final']
TWIN_WEIGHTS = ['w_in', 'conv_w', 'sinks', 'g_mix', 'g_group', 'w_o', 'g_mlp', 'w_ff_in', 'w_ff_out', 'g_final']
TWIN_DIFF_INPUT = 'x'
TWIN_INPUTS = ['x', 'w_in', 'conv_w', 'sinks', 'g_mix', 'g_group', 'w_o', 'g_mlp', 'w_ff_in', 'w_ff_out', 'g_final', 'loss_target', 'm_w_in', 'm_conv_w', 'm_sinks', 'm_g_mix', 'm_g_group', 'm_w_o', 'm_g_mlp', 'm_w_ff_in', 'm_w_ff_out', 'm_g_final', 'v_w_in', 'v_conv_w', 'v_sinks', 'v_g_mix', 'v_g_group', 'v_w_o', 'v_g_mlp', 'v_w_ff_in', 'v_w_ff_out', 'v_g_final']
TWIN_OUTPUTS = ['loss', 'grad_x', 'grad_w_in', 'grad_conv_w', 'grad_sinks', 'grad_g_mix', 'grad_g_group', 'grad_w_o', 'grad_g_mlp', 'grad_w_ff_in', 'grad_w_ff_out', 'grad_g_final', 'delta_w_in', 'delta_conv_w', 'delta_sinks', 'delta_g_mix', 'delta_g_group', 'delta_w_o', 'delta_g_mlp', 'delta_w_ff_in', 'delta_w_ff_out', 'delta_g_final', 'new_m_w_in', 'new_m_conv_w', 'new_m_sinks', 'new_m_g_mix', 'new_m_g_group', 'new_m_w_o', 'new_m_g_mlp', 'new_m_w_ff_in', 'new_m_w_ff_out', 'new_m_g_final', 'new_v_w_in', 'new_v_conv_w', 'new_v_sinks', 'new_v_g_mix', 'new_v_g_group', 'new_v_w_o', 'new_v_g_mlp', 'new_v_w_ff_in', 'new_v_w_ff_out', 'new_v_g_final']
TWIN_LEAF_KINDS = {'loss': 'loss', 'grad_x': 'grad_x', 'grad_w_in': 'grad_w', 'grad_conv_w': 'grad_w', 'grad_sinks': 'grad_w', 'grad_g_mix': 'grad_w', 'grad_g_group': 'grad_w', 'grad_w_o': 'grad_w', 'grad_g_mlp': 'grad_w', 'grad_w_ff_in': 'grad_w', 'grad_w_ff_out': 'grad_w', 'grad_g_final': 'grad_w', 'delta_w_in': 'delta_w', 'delta_conv_w': 'delta_w', 'delta_sinks': 'delta_w', 'delta_g_mix': 'delta_w', 'delta_g_group': 'delta_w', 'delta_w_o': 'delta_w', 'delta_g_mlp': 'delta_w', 'delta_w_ff_in': 'delta_w', 'delta_w_ff_out': 'delta_w', 'delta_g_final': 'delta_w', 'new_m_w_in': 'new_m', 'new_m_conv_w': 'new_m', 'new_m_sinks': 'new_m', 'new_m_g_mix': 'new_m', 'new_m_g_group': 'new_m', 'new_m_w_o': 'new_m', 'new_m_g_mlp': 'new_m', 'new_m_w_ff_in': 'new_m', 'new_m_w_ff_out': 'new_m', 'new_m_g_final': 'new_m', 'new_v_w_in': 'new_v', 'new_v_conv_w': 'new_v', 'new_v_sinks': 'new_v', 'new_v_g_mix': 'new_v', 'new_v_g_group': 'new_v', 'new_v_w_o': 'new_v', 'new_v_g_mlp': 'new_v', 'new_v_w_ff_in': 'new_v', 'new_v_w_ff_out': 'new_v', 'new_v_g_final': 'new_v'}


def _forward(args):
    return _fwd_reference(*[args[k] for k in FWD_PARAMS])


def _output_shape():
    out = _jax.eval_shape(lambda: _forward(_fwd_setup_inputs(0)))
    return out.shape, out.dtype

N_MICROBATCH = 1
ADAM_LR = 0.001
ADAM_B1 = 0.9
ADAM_B2 = 0.999
ADAM_EPS = 1e-08
ADAM_WD = 0.01
ADAM_STEP = 10
PER_EXAMPLE_BATCH_AXIS = {'x': 0, 'loss_target': 0}
SHARED_INPUTS = []
_WEIGHT_DTYPES = {'w_in': _jnp.float32, 'conv_w': _jnp.float32, 'sinks': _jnp.float32, 'g_mix': _jnp.float32, 'g_group': _jnp.float32, 'w_o': _jnp.float32, 'g_mlp': _jnp.float32, 'w_ff_in': _jnp.float32, 'w_ff_out': _jnp.float32, 'g_final': _jnp.float32}
MOMENT_SCALE = {'w_in': 1.285608e-01, 'conv_w': 1.276104e-01, 'sinks': 5.350309e-02, 'g_mix': 2.113290e-01, 'g_group': 1.279154e-01, 'w_o': 1.324466e-01, 'g_mlp': 1.353708e-01, 'w_ff_in': 6.387023e-02, 'w_ff_out': 1.470869e-01, 'g_final': 3.264706e+01}


def _to_microbatches(a, axis):
    t = _jnp.moveaxis(a, axis, 0)
    t = t.reshape((N_MICROBATCH, t.shape[0] // N_MICROBATCH) + t.shape[1:])
    return _jnp.moveaxis(t, 1, axis + 1)


def setup_inputs(seed: int = 0) -> dict:
    inp = _fwd_setup_inputs(seed)
    key = _jax.random.fold_in(_jax.random.key(seed), 7919)
    shape, _ = _output_shape()
    out = dict(inp)
    out["loss_target"] = _jax.random.normal(_jax.random.fold_in(key, 0), shape, _jnp.float32)
    for i, name in enumerate(TWIN_WEIGHTS):
        w = inp[name].astype(_jnp.float32)
        if MOMENT_SCALE is None:
            s = _jnp.sqrt(_jnp.mean(_jnp.square(w)) + 1e-30)
        else:
            s = MOMENT_SCALE[name]
        km, kv = _jax.random.split(_jax.random.fold_in(key, i + 1))
        out[name] = w
        out["m_" + name] = s * _jax.random.normal(km, w.shape, _jnp.float32)
        out["v_" + name] = (s * s) * _jax.random.uniform(kv, w.shape, _jnp.float32, 0.5, 1.5)
    if N_MICROBATCH > 1:
        for name, axis in PER_EXAMPLE_BATCH_AXIS.items():
            out[name] = _to_microbatches(out[name], axis)
    return {'x': out['x'], 'w_in': out['w_in'], 'conv_w': out['conv_w'], 'sinks': out['sinks'], 'g_mix': out['g_mix'], 'g_group': out['g_group'], 'w_o': out['w_o'], 'g_mlp': out['g_mlp'], 'w_ff_in': out['w_ff_in'], 'w_ff_out': out['w_ff_out'], 'g_final': out['g_final'], 'loss_target': out['loss_target'], 'm_w_in': out['m_w_in'], 'm_conv_w': out['m_conv_w'], 'm_sinks': out['m_sinks'], 'm_g_mix': out['m_g_mix'], 'm_g_group': out['m_g_group'], 'm_w_o': out['m_w_o'], 'm_g_mlp': out['m_g_mlp'], 'm_w_ff_in': out['m_w_ff_in'], 'm_w_ff_out': out['m_w_ff_out'], 'm_g_final': out['m_g_final'], 'v_w_in': out['v_w_in'], 'v_conv_w': out['v_conv_w'], 'v_sinks': out['v_sinks'], 'v_g_mix': out['v_g_mix'], 'v_g_group': out['v_g_group'], 'v_w_o': out['v_w_o'], 'v_g_mlp': out['v_g_mlp'], 'v_w_ff_in': out['v_w_ff_in'], 'v_w_ff_out': out['v_w_ff_out'], 'v_g_final': out['v_g_final']}


def _loss(weights, diff, rest, loss_target):
    with _jax.named_scope("forward"):
        args = {**rest, TWIN_DIFF_INPUT: diff, **{k: w.astype(_WEIGHT_DTYPES[k]) for k, w in weights.items()}}
        y = _forward(args)
    with _jax.named_scope("loss_head"):
        err = _jnp.square(y.astype(_jnp.float32) - loss_target)
        return 0.5 * _jnp.sum(_jnp.mean(err, axis=-1)) if err.ndim else 0.5 * err


def _adamw(w, g, m, v):
    m = ADAM_B1 * m + (1.0 - ADAM_B1) * g
    v = ADAM_B2 * v + (1.0 - ADAM_B2) * _jnp.square(g)
    m_hat = m / (1.0 - ADAM_B1 ** ADAM_STEP)
    v_hat = v / (1.0 - ADAM_B2 ** ADAM_STEP)
    delta = -ADAM_LR * (m_hat / (_jnp.sqrt(v_hat) + ADAM_EPS) + ADAM_WD * w)
    return delta, m, v


def reference(x, w_in, conv_w, sinks, g_mix, g_group, w_o, g_mlp, w_ff_in, w_ff_out, g_final, loss_target, m_w_in, m_conv_w, m_sinks, m_g_mix, m_g_group, m_w_o, m_g_mlp, m_w_ff_in, m_w_ff_out, m_g_final, v_w_in, v_conv_w, v_sinks, v_g_mix, v_g_group, v_w_o, v_g_mlp, v_w_ff_in, v_w_ff_out, v_g_final):
    given = dict(x=x, w_in=w_in, conv_w=conv_w, sinks=sinks, g_mix=g_mix, g_group=g_group, w_o=w_o, g_mlp=g_mlp, w_ff_in=w_ff_in, w_ff_out=w_ff_out, g_final=g_final, loss_target=loss_target, m_w_in=m_w_in, m_conv_w=m_conv_w, m_sinks=m_sinks, m_g_mix=m_g_mix, m_g_group=m_g_group, m_w_o=m_w_o, m_g_mlp=m_g_mlp, m_w_ff_in=m_w_ff_in, m_w_ff_out=m_w_ff_out, m_g_final=m_g_final, v_w_in=v_w_in, v_conv_w=v_conv_w, v_sinks=v_sinks, v_g_mix=v_g_mix, v_g_group=v_g_group, v_w_o=v_w_o, v_g_mlp=v_g_mlp, v_w_ff_in=v_w_ff_in, v_w_ff_out=v_w_ff_out, v_g_final=v_g_final)
    weights = {n: given[n] for n in TWIN_WEIGHTS}
    shared = {n: given[n] for n in SHARED_INPUTS}
    per_example = {n: given[n] for n in ['x']}
    grad_fn = _jax.value_and_grad(_loss, argnums=(0, 1))

    def one_microbatch(ex, loss_target):
        ex = dict(ex)
        diff = ex.pop(TWIN_DIFF_INPUT)
        return grad_fn(weights, diff, {**shared, **ex}, loss_target)

    if N_MICROBATCH == 1:
        loss, (grad_w, grad_x) = one_microbatch(per_example, given["loss_target"])
    else:
        def body(carry, xs):
            loss_sum, grad_sum = carry
            l_k, (gw_k, gx_k) = one_microbatch(xs[0], xs[1])
            with _jax.named_scope("update"):
                return (loss_sum + l_k, _jax.tree.map(_jnp.add, grad_sum, gw_k)), gx_k

        init = (_jnp.zeros((), _jnp.float32), _jax.tree.map(_jnp.zeros_like, weights))
        (loss, grad_w), grad_x = _jax.lax.scan(body, init, (per_example, given["loss_target"]))
    with _jax.named_scope("update"):
        delta_w, new_m, new_v = {}, {}, {}
        for n in TWIN_WEIGHTS:
            delta_w[n], new_m[n], new_v[n] = _adamw(weights[n], grad_w[n], given["m_" + n], given["v_" + n])
    return (loss, grad_x, *[grad_w[n] for n in TWIN_WEIGHTS], *[delta_w[n] for n in TWIN_WEIGHTS],
            *[new_m[n] for n in TWIN_WEIGHTS], *[new_v[n] for n in TWIN_WEIGHTS])
```

```python
import functools

import jax
import jax.numpy as jnp
from jax import lax
from jax.experimental import pallas as pl
from jax.experimental.pallas import tpu as pltpu

F32 = jnp.float32
BF16 = jnp.bfloat16
MESH = pl.DeviceIdType.MESH

N_DEV = 8
SEQ = 4096
D_MODEL = 1024
DEPTH = 2
HEAD_DIM = 64
LANES = 128
A_WIDTH = 384
CONV_CH = 256
C_WIDTH = 384
KV_WIDTH = 128
IN_WIDTH = 2560
D_FF = 4096
BLOCK = 128
DILATED_PATTERNS = (1, 4, 16)
A_MAX_DIST = 128
C_MAX_DIST = 127
EPS = 1e-6
SCALE = HEAD_DIM ** -0.5
NEG_BIG = -1e30

QA_BLK, KA_BLK, VA_BLK = 0, 3, 6
GB_BLK, GC_BLK, XB_BLK = 9, 11, 13
QC_BLK, KC_BLK, VC_BLK = 15, 18, 19

ADAM_LR = 0.001
ADAM_B1 = 0.9
ADAM_B2 = 0.999
ADAM_EPS = 1e-08
ADAM_WD = 0.01
ADAM_STEP = 10

VMEM_LIMIT = 56 * 1024 * 1024
ROW_TILE = 512
SMALL_ROWS = 16


def _dot_nn(a, b):
    return lax.dot_general(a, b, (((1,), (0,)), ((), ())), preferred_element_type=F32)


def _dot_nt(a, b):
    return lax.dot_general(a, b, (((1,), (1,)), ((), ())), preferred_element_type=F32)


def _dot_tn(a, b):
    return lax.dot_general(a, b, (((0,), (0,)), ((), ())), preferred_element_type=F32)


def _params(*sem):
    return pltpu.CompilerParams(dimension_semantics=sem, vmem_limit_bytes=VMEM_LIMIT)


def _rms_scale(t):
    return lax.rsqrt(jnp.mean(t * t, axis=-1, keepdims=True) + EPS)


def _rms_bwd(n, r, dn):
    return r * (dn - n * jnp.mean(dn * n, axis=-1, keepdims=True))


def _norm_mm(x, g, wt, name):
    s, d = x.shape
    n = wt.shape[0]
    tm, tn = ROW_TILE, 512

    def body(x_ref, g_ref, w_ref, z_ref, h_ref):
        @pl.when(pl.program_id(1) == 0)
        def _():
            xx = x_ref[...]
            h_ref[...] = ((xx * _rms_scale(xx)) * g_ref[...]).astype(BF16)

        z_ref[...] = _dot_nt(h_ref[...], w_ref[...])

    return pl.pallas_call(
        body,
        grid=(s // tm, n // tn),
        in_specs=[pl.BlockSpec((tm, d), lambda i, j: (i, 0)),
                  pl.BlockSpec((1, d), lambda i, j: (0, 0)),
                  pl.BlockSpec((tn, d), lambda i, j: (j, 0))],
        out_specs=[pl.BlockSpec((tm, tn), lambda i, j: (i, j)),
                   pl.BlockSpec((tm, d), lambda i, j: (i, 0))],
        out_shape=[jax.ShapeDtypeStruct((s, n), F32), jax.ShapeDtypeStruct((s, d), BF16)],
        compiler_params=_params("parallel", "arbitrary"),
        name=name,
    )(x, g, wt)


def _act_mm_res(u, w2, x1, name):
    s, f = u.shape
    d = w2.shape[1]
    tm, tk = ROW_TILE, 1024

    def body(u_ref, w_ref, x_ref, o_ref):
        @pl.when(pl.program_id(1) == 0)
        def _():
            o_ref[...] = x_ref[...]

        a = jnp.square(jnp.maximum(u_ref[...], 0.0)).astype(BF16)
        o_ref[...] += _dot_nn(a, w_ref[...])

    return pl.pallas_call(
        body,
        grid=(s // tm, f // tk),
        in_specs=[pl.BlockSpec((tm, tk), lambda i, k: (i, k)),
                  pl.BlockSpec((tk, d), lambda i, k: (k, 0)),
                  pl.BlockSpec((tm, d), lambda i, k: (i, 0))],
        out_specs=pl.BlockSpec((tm, d), lambda i, k: (i, 0)),
        out_shape=jax.ShapeDtypeStruct((s, d), F32),
        compiler_params=_params("parallel", "arbitrary"),
        name=name,
    )(u, w2, x1)


def _mix_out(ya, yb, yc, gg, wo, x0, name):
    s = ya.shape[0]
    d = wo.shape[1]
    tm = ROW_TILE

    def body(ya_ref, yb_ref, yc_ref, g_ref, w_ref, x_ref, y_ref, o_ref):
        parts = []
        for ref in (ya_ref, yb_ref, yc_ref):
            t = ref[...]
            parts.append(t * _rms_scale(t))
        y = (jnp.concatenate(parts, axis=1) * g_ref[...]).astype(BF16)
        y_ref[...] = y
        o_ref[...] = x_ref[...] + _dot_nn(y, w_ref[...])

    return pl.pallas_call(
        body,
        grid=(s // tm,),
        in_specs=[pl.BlockSpec((tm, A_WIDTH), lambda i: (i, 0)),
                  pl.BlockSpec((tm, CONV_CH), lambda i: (i, 0)),
                  pl.BlockSpec((tm, C_WIDTH), lambda i: (i, 0)),
                  pl.BlockSpec((1, d), lambda i: (0, 0)),
                  pl.BlockSpec((d, d), lambda i: (0, 0)),
                  pl.BlockSpec((tm, d), lambda i: (i, 0))],
        out_specs=[pl.BlockSpec((tm, d), lambda i: (i, 0)),
                   pl.BlockSpec((tm, d), lambda i: (i, 0))],
        out_shape=[jax.ShapeDtypeStruct((s, d), BF16), jax.ShapeDtypeStruct((s, d), F32)],
        compiler_params=_params("parallel"),
        name=name,
    )(ya, yb, yc, gg, wo, x0)


def _loss_head(x, g, target, name):
    s, d = x.shape
    tm = ROW_TILE

    def body(x_ref, g_ref, t_ref, loss_ref, dx_ref, dg_ref):
        @pl.when(pl.program_id(0) == 0)
        def _():
            loss_ref[...] = jnp.zeros_like(loss_ref)
            dg_ref[...] = jnp.zeros_like(dg_ref)

        xx = x_ref[...]
        r = _rms_scale(xx)
        n = xx * r
        gv = g_ref[...]
        err = n * gv - t_ref[...]
        per_tok = jnp.sum(err * err, axis=1, keepdims=True) * (1.0 / d)
        loss_ref[...] += 0.5 * jnp.sum(per_tok, axis=0, keepdims=True)
        dout = err * (1.0 / d)
        dg_ref[...] += jnp.sum(dout * n, axis=0, keepdims=True)
        dx_ref[...] = _rms_bwd(n, r, dout * gv)

    return pl.pallas_call(
        body,
        grid=(s // tm,),
        in_specs=[pl.BlockSpec((tm, d), lambda i: (i, 0)),
                  pl.BlockSpec((1, d), lambda i: (0, 0)),
                  pl.BlockSpec((tm, d), lambda i: (i, 0))],
        out_specs=[pl.BlockSpec((8, LANES), lambda i: (0, 0)),
                   pl.BlockSpec((tm, d), lambda i: (i, 0)),
                   pl.BlockSpec((1, d), lambda i: (0, 0))],
        out_shape=[jax.ShapeDtypeStruct((8, LANES), F32), jax.ShapeDtypeStruct((s, d), F32),
                   jax.ShapeDtypeStruct((1, d), F32)],
        compiler_params=_params("arbitrary"),
        name=name,
    )(x, g, target)


def _mlp_bwd_act(dx, w2, u, name):
    s, d = dx.shape
    f = w2.shape[0]
    tm, tn = ROW_TILE, 512

    def body(dx_ref, w_ref, u_ref, du_ref, a_ref, dxb):
        @pl.when(pl.program_id(1) == 0)
        def _():
            dxb[...] = dx_ref[...].astype(BF16)

        da = _dot_nt(dxb[...], w_ref[...])
        rl = jnp.maximum(u_ref[...], 0.0)
        a_ref[...] = (rl * rl).astype(BF16)
        du_ref[...] = (da * (2.0 * rl)).astype(BF16)

    return pl.pallas_call(
        body,
        grid=(s // tm, f // tn),
        in_specs=[pl.BlockSpec((tm, d), lambda i, j: (i, 0)),
                  pl.BlockSpec((tn, d), lambda i, j: (j, 0)),
                  pl.BlockSpec((tm, tn), lambda i, j: (i, j))],
        out_specs=[pl.BlockSpec((tm, tn), lambda i, j: (i, j)),
                   pl.BlockSpec((tm, tn), lambda i, j: (i, j))],
        out_shape=[jax.ShapeDtypeStruct((s, f), BF16), jax.ShapeDtypeStruct((s, f), BF16)],
        scratch_shapes=[pltpu.VMEM((tm, d), BF16)],
        compiler_params=_params("parallel", "arbitrary"),
        name=name,
    )(dx, w2, u)


def _mm_tn(a, b, name):
    s, n = a.shape
    d = b.shape[1]
    tn, tk = 512, 512
    nk = s // tk

    def body(a_ref, b_ref, o_ref, acc):
        k = pl.program_id(1)

        @pl.when(k == 0)
        def _():
            acc[...] = jnp.zeros_like(acc)

        acc[...] += _dot_tn(a_ref[...], b_ref[...].astype(BF16))

        @pl.when(k == nk - 1)
        def _():
            o_ref[...] = acc[...].astype(BF16)

    return pl.pallas_call(
        body,
        grid=(n // tn, nk),
        in_specs=[pl.BlockSpec((tk, tn), lambda j, k: (k, j)),
                  pl.BlockSpec((tk, d), lambda j, k: (k, 0))],
        out_specs=pl.BlockSpec((tn, d), lambda j, k: (j, 0)),
        out_shape=jax.ShapeDtypeStruct((n, d), BF16),
        scratch_shapes=[pltpu.VMEM((tn, d), F32)],
        compiler_params=_params("parallel", "arbitrary"),
        name=name,
    )(a, b)


def _mm_nn_normbwd(dact, wt, x, dres, g, name):
    s, kdim = dact.shape
    d = wt.shape[1]
    tm, tk = ROW_TILE, 512
    nk = kdim // tk

    def body(a_ref, w_ref, x_ref, r_ref, g_ref, o_ref, dg_ref):
        i, k = pl.program_id(0), pl.program_id(1)

        @pl.when((i == 0) & (k == 0))
        def _():
            dg_ref[...] = jnp.zeros_like(dg_ref)

        @pl.when(k == 0)
        def _():
            o_ref[...] = jnp.zeros_like(o_ref)

        o_ref[...] += _dot_nn(a_ref[...], w_ref[...])

        @pl.when(k == nk - 1)
        def _():
            dh = o_ref[...]
            xx = x_ref[...]
            r = _rms_scale(xx)
            n = xx * r
            dg_ref[...] += jnp.sum(dh * n, axis=0, keepdims=True)
            o_ref[...] = r_ref[...] + _rms_bwd(n, r, dh * g_ref[...])

    return pl.pallas_call(
        body,
        grid=(s // tm, nk),
        in_specs=[pl.BlockSpec((tm, tk), lambda i, k: (i, k)),
                  pl.BlockSpec((tk, d), lambda i, k: (k, 0)),
                  pl.BlockSpec((tm, d), lambda i, k: (i, 0)),
                  pl.BlockSpec((tm, d), lambda i, k: (i, 0)),
                  pl.BlockSpec((1, d), lambda i, k: (0, 0))],
        out_specs=[pl.BlockSpec((tm, d), lambda i, k: (i, 0)),
                   pl.BlockSpec((1, d), lambda i, k: (0, 0))],
        out_shape=[jax.ShapeDtypeStruct((s, d), F32), jax.ShapeDtypeStruct((1, d), F32)],
        compiler_params=_params("arbitrary", "arbitrary"),
        name=name,
    )(dact, wt, x, dres, g)


def _mix_bwd(dx1, wo, ya, yb, yc, gg, name):
    s, d = dx1.shape
    tm = ROW_TILE
    widths = (A_WIDTH, CONV_CH, C_WIDTH)

    def body(dx_ref, w_ref, ya_ref, yb_ref, yc_ref, g_ref, da_ref, db_ref, dc_ref, dg_ref):
        @pl.when(pl.program_id(0) == 0)
        def _():
            dg_ref[...] = jnp.zeros_like(dg_ref)

        dy = _dot_nt(dx_ref[...].astype(BF16), w_ref[...])
        gv = g_ref[...]
        off = 0
        dgs = []
        for ref, out, w in zip((ya_ref, yb_ref, yc_ref), (da_ref, db_ref, dc_ref), widths):
            t = ref[...]
            r = _rms_scale(t)
            n = t * r
            dyg = dy[:, off:off + w]
            dgs.append(jnp.sum(dyg * n, axis=0, keepdims=True))
            out[...] = _rms_bwd(n, r, dyg * gv[:, off:off + w])
            off += w
        dg_ref[...] += jnp.concatenate(dgs, axis=1)

    return pl.pallas_call(
        body,
        grid=(s // tm,),
        in_specs=[pl.BlockSpec((tm, d), lambda i: (i, 0)),
                  pl.BlockSpec((d, d), lambda i: (0, 0)),
                  pl.BlockSpec((tm, A_WIDTH), lambda i: (i, 0)),
                  pl.BlockSpec((tm, CONV_CH), lambda i: (i, 0)),
                  pl.BlockSpec((tm, C_WIDTH), lambda i: (i, 0)),
                  pl.BlockSpec((1, d), lambda i: (0, 0))],
        out_specs=[pl.BlockSpec((tm, A_WIDTH), lambda i: (i, 0)),
                   pl.BlockSpec((tm, CONV_CH), lambda i: (i, 0)),
                   pl.BlockSpec((tm, C_WIDTH), lambda i: (i, 0)),
                   pl.BlockSpec((1, d), lambda i: (0, 0))],
        out_shape=[jax.ShapeDtypeStruct((s, A_WIDTH), F32), jax.ShapeDtypeStruct((s, CONV_CH), F32),
                   jax.ShapeDtypeStruct((s, C_WIDTH), F32), jax.ShapeDtypeStruct((1, d), F32)],
        compiler_params=_params("arbitrary"),
        name=name,
    )(dx1, wo, ya, yb, yc, gg)


CONV_CHUNK = 256
CONV_HALO = 8


def _conv_fwd(z, cw, name):
    s = z.shape[0]
    nch = s // CONV_CHUNK

    def body(gb_ref, gc_ref, xb_ref, w_ref, o_ref, us):
        us[pl.ds(0, CONV_HALO), :] = jnp.zeros((CONV_HALO, LANES), F32)
        us[pl.ds(CONV_HALO, s), :] = gc_ref[...] * xb_ref[...]
        w0, w1, w2 = w_ref[0:1, :], w_ref[1:2, :], w_ref[2:3, :]

        def chunk(c, carry):
            st = pl.multiple_of(c * CONV_CHUNK, CONV_CHUNK)
            ext = us[pl.ds(st, CONV_CHUNK + CONV_HALO), :]
            y = (w0 * ext[CONV_HALO - 2:CONV_HALO - 2 + CONV_CHUNK]
                 + w1 * ext[CONV_HALO - 1:CONV_HALO - 1 + CONV_CHUNK]
                 + w2 * ext[CONV_HALO:])
            o_ref[pl.ds(st, CONV_CHUNK), :] = gb_ref[pl.ds(st, CONV_CHUNK), :] * y
            return carry

        lax.fori_loop(0, nch, chunk, 0)

    col = lambda blk: pl.BlockSpec((s, LANES), lambda j, blk=blk: (0, blk + j))
    return pl.pallas_call(
        body,
        grid=(CONV_CH // LANES,),
        in_specs=[col(GB_BLK), col(GC_BLK), col(XB_BLK), pl.BlockSpec((3, LANES), lambda j: (0, j))],
        out_specs=pl.BlockSpec((s, LANES), lambda j: (0, j)),
        out_shape=jax.ShapeDtypeStruct((s, CONV_CH), F32),
        scratch_shapes=[pltpu.VMEM((s + CONV_HALO, LANES), F32)],
        compiler_params=_params("parallel"),
        name=name,
    )(z, z, z, cw)


def _conv_bwd(z, cw, dyb, name):
    s = z.shape[0]
    nch = s // CONV_CHUNK

    def body(gb_ref, gc_ref, xb_ref, w_ref, dy_ref, dgb_ref, dgc_ref, dxb_ref, dw_ref, us, ds_):
        us[pl.ds(0, CONV_HALO), :] = jnp.zeros((CONV_HALO, LANES), F32)
        us[pl.ds(CONV_HALO, s), :] = gc_ref[...] * xb_ref[...]
        ds_[pl.ds(s, CONV_HALO), :] = jnp.zeros((CONV_HALO, LANES), F32)
        ds_[pl.ds(0, s), :] = dy_ref[...] * gb_ref[...]
        w0, w1, w2 = w_ref[0:1, :], w_ref[1:2, :], w_ref[2:3, :]
        zero = jnp.zeros((1, LANES), F32)

        def chunk(c, carry):
            a0, a1, a2 = carry
            st = pl.multiple_of(c * CONV_CHUNK, CONV_CHUNK)
            rows = pl.ds(st, CONV_CHUNK)
            ext = us[pl.ds(st, CONV_CHUNK + CONV_HALO), :]
            um2 = ext[CONV_HALO - 2:CONV_HALO - 2 + CONV_CHUNK]
            um1 = ext[CONV_HALO - 1:CONV_HALO - 1 + CONV_CHUNK]
            u0 = ext[CONV_HALO:]
            dext = ds_[pl.ds(st, CONV_CHUNK + CONV_HALO), :]
            dc0 = dext[:CONV_CHUNK]
            du = w2 * dc0 + w1 * dext[1:1 + CONV_CHUNK] + w0 * dext[2:2 + CONV_CHUNK]
            yconv = w0 * um2 + w1 * um1 + w2 * u0
            dgb_ref[rows, :] = (dy_ref[rows, :] * yconv).astype(BF16)
            dgc_ref[rows, :] = (du * xb_ref[rows, :]).astype(BF16)
            dxb_ref[rows, :] = (du * gc_ref[rows, :]).astype(BF16)
            a0 = a0 + jnp.sum(dc0 * um2, axis=0, keepdims=True)
            a1 = a1 + jnp.sum(dc0 * um1, axis=0, keepdims=True)
            a2 = a2 + jnp.sum(dc0 * u0, axis=0, keepdims=True)
            return a0, a1, a2

        a0, a1, a2 = lax.fori_loop(0, nch, chunk, (zero, zero, zero))
        dw_ref[...] = jnp.concatenate([a0, a1, a2, jnp.zeros((5, LANES), F32)], axis=0)

    col = lambda blk: pl.BlockSpec((s, LANES), lambda j, blk=blk: (0, blk + j))
    own = pl.BlockSpec((s, LANES), lambda j: (0, j))
    return pl.pallas_call(
        body,
        grid=(CONV_CH // LANES,),
        in_specs=[col(GB_BLK), col(GC_BLK), col(XB_BLK), pl.BlockSpec((3, LANES), lambda j: (0, j)), own],
        out_specs=[own, own, own, pl.BlockSpec((8, LANES), lambda j: (0, j))],
        out_shape=[jax.ShapeDtypeStruct((s, CONV_CH), BF16)] * 3 + [jax.ShapeDtypeStruct((8, CONV_CH), F32)],
        scratch_shapes=[pltpu.VMEM((s + CONV_HALO, LANES), F32), pltpu.VMEM((s + CONV_HALO, LANES), F32)],
        compiler_params=_params("parallel"),
        name=name,
    )(z, z, z, cw, dyb)


ATTN_ROWS = 512


def _band_rows(b, d, r):
    base = pl.multiple_of(b * (BLOCK * d), BLOCK)
    prev = jnp.maximum(base - BLOCK * d, 0)
    if d == 1:
        return pl.ds(base, BLOCK), pl.ds(pl.multiple_of(prev, BLOCK), BLOCK)
    return pl.ds(base + r, BLOCK, stride=d), pl.ds(prev + r, BLOCK, stride=d)


def _band_mask(b, max_dist):
    qi = lax.broadcasted_iota(jnp.int32, (BLOCK, 2 * BLOCK), 0)
    kj = lax.broadcasted_iota(jnp.int32, (BLOCK, 2 * BLOCK), 1)
    dist = BLOCK + qi - kj
    first_key = jnp.where(b > 0, 0, BLOCK)
    return (dist >= 0) & (dist <= max_dist) & (kj >= first_key)


def _lane_half():
    return (lax.broadcasted_iota(jnp.int32, (1, LANES), 1) >= HEAD_DIM).astype(jnp.int32)


def _kv_for_pair(t, pair):
    half = _lane_half()
    want = (pair + half) >> 1
    return jnp.where(want != half, pltpu.roll(t, HEAD_DIM, 1), t)


def _kv_grad_from_pair(t, pair):
    half = _lane_half()
    mine = ((pair + half) >> 1) == half
    other = ((pair + 1 - half) >> 1) == half
    fold = t + pltpu.roll(t, HEAD_DIM, 1)
    return jnp.where(mine & other, fold, jnp.where(mine, t, 0.0))


def _attn_fwd(z, m_init, l_init, q_blk, k_blk, v_blk, patterns, max_dist, gqa, name):
    s = z.shape[0]
    npair = 3

    def body(q_ref, k_ref, v_ref, mi_ref, o_ref, m_ref, l_scr):
        pair = pl.program_id(0)
        head0 = lax.broadcasted_iota(jnp.int32, (1, LANES), 1) < HEAD_DIM

        def init(c, carry):
            rows = pl.ds(pl.multiple_of(c * ATTN_ROWS, ATTN_ROWS), ATTN_ROWS)
            m_ref[rows, :] = jnp.broadcast_to(mi_ref[...], (ATTN_ROWS, LANES))
            l_scr[rows, :] = jnp.full((ATTN_ROWS, LANES), l_init, F32)
            o_ref[rows, :] = jnp.zeros((ATTN_ROWS, LANES), F32)
            return carry

        lax.fori_loop(0, s // ATTN_ROWS, init, 0)

        for d in patterns:
            for r in range(d):
                def blk(b, carry, d=d, r=r):
                    rq, rp = _band_rows(b, d, r)
                    mask = _band_mask(b, max_dist)
                    q = (q_ref[rq, :] * SCALE).astype(BF16)
                    k2 = jnp.concatenate([k_ref[rp, :], k_ref[rq, :]], axis=0)
                    v2 = jnp.concatenate([v_ref[rp, :], v_ref[rq, :]], axis=0)
                    if gqa:
                        k2 = _kv_for_pair(k2, pair)
                        v2 = _kv_for_pair(v2, pair)
                    k2 = k2.astype(BF16)
                    v2 = v2.astype(BF16)
                    per_head = []
                    for hh in range(2):
                        mh = head0 if hh == 0 else jnp.logical_not(head0)
                        sc = _dot_nt(jnp.where(mh, q, jnp.zeros_like(q)), k2)
                        sc = jnp.where(mask, sc, -jnp.inf)
                        mb = jnp.max(sc, axis=1, keepdims=True)
                        p = jnp.exp(sc - mb)
                        lb = jnp.sum(p, axis=1, keepdims=True)
                        per_head.append((mb, lb, _dot_nn(p.astype(BF16), v2)))
                    (m0, l0, o0), (m1, l1, o1) = per_head
                    m2 = jnp.where(head0, m0, m1)
                    l2 = jnp.where(head0, l0, l1)
                    o2 = jnp.where(head0, o0, o1)
                    m_old = m_ref[rq, :]
                    m_new = jnp.maximum(m_old, m2)
                    a_old = jnp.exp(m_old - m_new)
                    a_blk = jnp.exp(m2 - m_new)
                    o_ref[rq, :] = o_ref[rq, :] * a_old + o2 * a_blk
                    l_scr[rq, :] = l_scr[rq, :] * a_old + l2 * a_blk
                    m_ref[rq, :] = m_new
                    return carry

                lax.fori_loop(0, (s // BLOCK) // d, blk, 0)

        def fin(c, carry):
            rows = pl.ds(pl.multiple_of(c * ATTN_ROWS, ATTN_ROWS), ATTN_ROWS)
            l = l_scr[rows, :]
            o_ref[rows, :] = o_ref[rows, :] / l
            m_ref[rows, :] = m_ref[rows, :] + jnp.log(l)
            return carry

        lax.fori_loop(0, s // ATTN_ROWS, fin, 0)

    kv = (lambda blk: pl.BlockSpec((s, LANES), lambda j, blk=blk: (0, blk))) if gqa else \
         (lambda blk: pl.BlockSpec((s, LANES), lambda j, blk=blk: (0, blk + j)))
    own = pl.BlockSpec((s, LANES), lambda j: (0, j))
    return pl.pallas_call(
        body,
        grid=(npair,),
        in_specs=[pl.BlockSpec((s, LANES), lambda j: (0, q_blk + j)), kv(k_blk), kv(v_blk),
                  pl.BlockSpec((1, LANES), lambda j: (0, j))],
        out_specs=[own, own],
        out_shape=[jax.ShapeDtypeStruct((s, npair * LANES), F32)] * 2,
        scratch_shapes=[pltpu.VMEM((s, LANES), F32)],
        compiler_params=_params("parallel"),
        name=name,
    )(z, z, z, m_init)


def _attn_bwd(z, do, o, lse, m_init, q_blk, k_blk, v_blk, patterns, max_dist, gqa, name):
    s = z.shape[0]
    npair = 3

    def body(q_ref, k_ref, v_ref, do_ref, o_ref, lse_ref, mi_ref,
             dq_ref, dk_ref, dv_ref, dm_ref, dq_acc, dk_acc, dv_acc, dl_scr):
        pair = pl.program_id(0)
        head0 = lax.broadcasted_iota(jnp.int32, (1, LANES), 1) < HEAD_DIM
        not0 = jnp.logical_not(head0)

        def zero_kv():
            def f(c, carry):
                rows = pl.ds(pl.multiple_of(c * ATTN_ROWS, ATTN_ROWS), ATTN_ROWS)
                dk_acc[rows, :] = jnp.zeros((ATTN_ROWS, LANES), F32)
                dv_acc[rows, :] = jnp.zeros((ATTN_ROWS, LANES), F32)
                return carry
            lax.fori_loop(0, s // ATTN_ROWS, f, 0)

        if gqa:
            pl.when(pair == 0)(zero_kv)
        else:
            zero_kv()

        def prep(c, dm):
            rows = pl.ds(pl.multiple_of(c * ATTN_ROWS, ATTN_ROWS), ATTN_ROWS)
            dq_acc[rows, :] = jnp.zeros((ATTN_ROWS, LANES), F32)
            prod = do_ref[rows, :] * o_ref[rows, :]
            d0 = jnp.sum(jnp.where(head0, prod, 0.0), axis=1, keepdims=True)
            d1 = jnp.sum(jnp.where(not0, prod, 0.0), axis=1, keepdims=True)
            delta = jnp.where(head0, d0, d1)
            dl_scr[rows, :] = delta
            psink = jnp.exp(mi_ref[...] - lse_ref[rows, :])
            return dm - jnp.sum(psink * delta, axis=0, keepdims=True)

        dm_ref[...] = lax.fori_loop(0, s // ATTN_ROWS, prep, jnp.zeros((1, LANES), F32))

        for d in patterns:
            for r in range(d):
                def blk(b, carry, d=d, r=r):
                    rq, rp = _band_rows(b, d, r)
                    mask = _band_mask(b, max_dist)
                    q = (q_ref[rq, :] * SCALE).astype(BF16)
                    k2 = jnp.concatenate([k_ref[rp, :], k_ref[rq, :]], axis=0)
                    v2 = jnp.concatenate([v_ref[rp, :], v_ref[rq, :]], axis=0)
                    if gqa:
                        k2 = _kv_for_pair(k2, pair)
                        v2 = _kv_for_pair(v2, pair)
                    k2 = k2.astype(BF16)
                    v2 = v2.astype(BF16)
                    dob = do_ref[rq, :].astype(BF16)
                    lse_t = lse_ref[rq, :]
                    dl_t = dl_scr[rq, :]
                    dq2 = None
                    dk2 = jnp.zeros((2 * BLOCK, LANES), F32)
                    dv2 = jnp.zeros((2 * BLOCK, LANES), F32)
                    for hh in range(2):
                        mh = head0 if hh == 0 else not0
                        qh = jnp.where(mh, q, jnp.zeros_like(q))
                        doh = jnp.where(mh, dob, jnp.zeros_like(dob))
                        lse_h = jnp.max(jnp.where(mh, lse_t, -jnp.inf), axis=1, keepdims=True)
                        dl_h = jnp.max(jnp.where(mh, dl_t, -jnp.inf), axis=1, keepdims=True)
                        sc = _dot_nt(qh, k2)
                        p = jnp.where(mask, jnp.exp(sc - lse_h), 0.0)
                        dp = _dot_nt(doh, v2)
                        dsc = (p * (dp - dl_h)).astype(BF16)
                        dqh = _dot_nn(dsc, k2)
                        dq2 = dqh if hh == 0 else jnp.where(head0, dq2, dqh)
                        dk2 = dk2 + _dot_tn(dsc, qh)
                        dv2 = dv2 + _dot_tn(p.astype(BF16), doh)
                    if gqa:
                        dk2 = _kv_grad_from_pair(dk2, pair)
                        dv2 = _kv_grad_from_pair(dv2, pair)
                    dq_acc[rq, :] += dq2 * SCALE
                    dk_acc[rp, :] += dk2[:BLOCK]
                    dk_acc[rq, :] += dk2[BLOCK:]
                    dv_acc[rp, :] += dv2[:BLOCK]
                    dv_acc[rq, :] += dv2[BLOCK:]
                    return carry

                lax.fori_loop(0, (s // BLOCK) // d, blk, 0)

        def out_q(c, carry):
            rows = pl.ds(pl.multiple_of(c * ATTN_ROWS, ATTN_ROWS), ATTN_ROWS)
            dq_ref[rows, :] = dq_acc[rows, :].astype(BF16)
            return carry

        lax.fori_loop(0, s // ATTN_ROWS, out_q, 0)

        def out_kv():
            def f(c, carry):
                rows = pl.ds(pl.multiple_of(c * ATTN_ROWS, ATTN_ROWS), ATTN_ROWS)
                dk_ref[rows, :] = dk_acc[rows, :].astype(BF16)
                dv_ref[rows, :] = dv_acc[rows, :].astype(BF16)
                return carry
            lax.fori_loop(0, s // ATTN_ROWS, f, 0)

        if gqa:
            pl.when(pair == npair - 1)(out_kv)
        else:
            out_kv()

    own = pl.BlockSpec((s, LANES), lambda j: (0, j))
    if gqa:
        kv = lambda blk: pl.BlockSpec((s, LANES), lambda j, blk=blk: (0, blk))
        kv_out = pl.BlockSpec((s, LANES), lambda j: (0, 0))
        kv_shape = jax.ShapeDtypeStruct((s, LANES), BF16)
    else:
        kv = lambda blk: pl.BlockSpec((s, LANES), lambda j, blk=blk: (0, blk + j))
        kv_out = own
        kv_shape = jax.ShapeDtypeStruct((s, npair * LANES), BF16)
    return pl.pallas_call(
        body,
        grid=(npair,),
        in_specs=[pl.BlockSpec((s, LANES), lambda j: (0, q_blk + j)), kv(k_blk), kv(v_blk), own, own, own,
                  pl.BlockSpec((1, LANES), lambda j: (0, j))],
        out_specs=[own, kv_out, kv_out, pl.BlockSpec((1, LANES), lambda j: (0, j))],
        out_shape=[jax.ShapeDtypeStruct((s, npair * LANES), BF16), kv_shape, kv_shape,
                   jax.ShapeDtypeStruct((1, npair * LANES), F32)],
        scratch_shapes=[pltpu.VMEM((s, LANES), F32)] * 4,
        compiler_params=_params("arbitrary" if gqa else "parallel"),
        name=name,
    )(z, z, z, do, o, lse, m_init)


def _adamw_math(w, g, m, v):
    m = ADAM_B1 * m + (1.0 - ADAM_B1) * g
    v = ADAM_B2 * v + (1.0 - ADAM_B2) * (g * g)
    m_hat = m / (1.0 - ADAM_B1 ** ADAM_STEP)
    v_hat = v / (1.0 - ADAM_B2 ** ADAM_STEP)
    delta = -ADAM_LR * (m_hat / (jnp.sqrt(v_hat) + ADAM_EPS) + ADAM_WD * w)
    return delta, m, v


def _adamw(w, g, m, v, name):
    rows, cols = w.shape
    tr = min(rows, 256)

    def body(w_ref, g_ref, m_ref, v_ref, d_ref, nm_ref, nv_ref):
        d_ref[...], nm_ref[...], nv_ref[...] = _adamw_math(w_ref[...], g_ref[...], m_ref[...], v_ref[...])

    spec = pl.BlockSpec((tr, cols), lambda i: (i, 0))
    return pl.pallas_call(
        body,
        grid=(rows // tr,),
        in_specs=[spec] * 4,
        out_specs=[spec] * 3,
        out_shape=[jax.ShapeDtypeStruct((rows, cols), F32)] * 3,
        compiler_params=_params("parallel"),
        name=name,
    )(w, g, m, v)


def _small_sum_adamw(gathered, w, m, v, name):
    _, rows, cols = gathered.shape

    def body(ga_ref, w_ref, m_ref, v_ref, g_ref, d_ref, nm_ref, nv_ref):
        g = ga_ref[0]
        for i in range(1, N_DEV):
            g = g + ga_ref[i]
        g_ref[...] = g
        d_ref[...], nm_ref[...], nv_ref[...] = _adamw_math(w_ref[...], g, m_ref[...], v_ref[...])

    return pl.pallas_call(
        body,
        out_shape=[jax.ShapeDtypeStruct((rows, cols), F32)] * 4,
        name=name,
    )(gathered, w, m, v)


def _pair_sum(g4, r1, pos, name):
    _, _, rows, cols = g4.shape
    tr = min(rows, 512)

    def body(pos_ref, g_ref, r_ref, o_ref):
        o_ref[...] = (g_ref[...].astype(F32) + r_ref[...].astype(F32)).astype(BF16)

    return pl.pallas_call(
        body,
        grid_spec=pltpu.PrefetchScalarGridSpec(
            num_scalar_prefetch=1,
            grid=(4, rows // tr),
            in_specs=[pl.BlockSpec((None, None, tr, cols), lambda i, j, p: (i, p[1], j, 0)),
                      pl.BlockSpec((None, tr, cols), lambda i, j, p: (i, j, 0))],
            out_specs=pl.BlockSpec((None, tr, cols), lambda i, j, p: (i, j, 0)),
        ),
        out_shape=jax.ShapeDtypeStruct((4, rows, cols), BF16),
        compiler_params=_params("parallel", "parallel"),
        name=name,
    )(pos, g4, r1)


def _final_sum(p, r2, pos, name):
    _, rows, cols = p.shape
    tr = min(rows, 512)

    def body(pos_ref, p_ref, r_ref, o_ref):
        o_ref[...] = ((p_ref[...].astype(F32) + r_ref[0].astype(F32)) + r_ref[1].astype(F32)) + r_ref[2].astype(F32)

    return pl.pallas_call(
        body,
        grid_spec=pltpu.PrefetchScalarGridSpec(
            num_scalar_prefetch=1,
            grid=(rows // tr,),
            in_specs=[pl.BlockSpec((None, tr, cols), lambda j, q: (q[0], j, 0)),
                      pl.BlockSpec((3, tr, cols), lambda j, q: (0, j, 0))],
            out_specs=pl.BlockSpec((tr, cols), lambda j, q: (j, 0)),
        ),
        out_shape=jax.ShapeDtypeStruct((rows, cols), F32),
        compiler_params=_params("parallel"),
        name=name,
    )(pos, p, r2)


def _place():
    return lax.axis_index("x"), lax.axis_index("y"), lax.axis_index("c")


def _all_gather(shards, name):
    na = len(shards)
    hbm = pl.BlockSpec(memory_space=pl.ANY)

    def body(*refs):
        ins, outs = refs[:na], refs[na:2 * na]
        send_sems, recv_sems, local_sems = refs[2 * na:]
        x, y, c = _place()
        me, sibling = (x, y, c), (x, y, 1 - c)
        chips = [(1 - x, y), (x, 1 - y), (1 - x, 1 - y)]

        def rows(a, px, py, pc):
            m = ins[a].shape[0]
            return outs[a].at[pl.ds((4 * px + 2 * py + pc) * m, m), :]

        def copy(a, k, block, to, src=None):
            return pltpu.make_async_remote_copy(
                src_ref=rows(a, *block) if src is None else src, dst_ref=rows(a, *block),
                send_sem=send_sems.at[a, k], recv_sem=recv_sems.at[a, k], device_id=to, device_id_type=MESH)

        mine = [pltpu.make_async_copy(ins[a], rows(a, *me), local_sems.at[a]) for a in range(na)]
        for cp in mine:
            cp.start()
        first = []
        for a in range(na):
            first.append(copy(a, 0, me, sibling, src=ins[a]))
            first += [copy(a, 1 + j, me, (*chip, c), src=ins[a]) for j, chip in enumerate(chips)]
        for cp in first:
            cp.start()
        passed = []
        for j, chip in enumerate(chips):
            for a in range(na):
                copy(a, 1 + j, (*chip, c), me).wait_recv()
                cp = copy(a, 4 + j, (*chip, c), sibling)
                cp.start()
                passed.append(cp)
        for a in range(na):
            copy(a, 0, sibling, me).wait_recv()
            for j, chip in enumerate(chips):
                copy(a, 4 + j, (*chip, 1 - c), me).wait_recv()
        for cp in first + passed:
            cp.wait_send()
        for cp in mine:
            cp.wait()

    return pl.pallas_call(
        body,
        out_shape=[jax.ShapeDtypeStruct((N_DEV * t.shape[0], t.shape[1]), t.dtype) for t in shards],
        in_specs=[hbm] * na,
        out_specs=[hbm] * na,
        scratch_shapes=[pltpu.SemaphoreType.DMA((na, 7)), pltpu.SemaphoreType.DMA((na, 7)),
                        pltpu.SemaphoreType.DMA((na,))],
        name=name,
    )(*shards)


def _sibling_exchange(grads, name):
    na = len(grads)
    hbm = pl.BlockSpec(memory_space=pl.ANY)

    def body(*refs):
        ins, outs = refs[:na], refs[na:2 * na]
        send_sems, recv_sems = refs[2 * na:]
        x, y, c = _place()
        copies = []
        for a in range(na):
            for chip in range(4):
                copies.append(pltpu.make_async_remote_copy(
                    src_ref=ins[a].at[chip, 1 - c], dst_ref=outs[a].at[chip],
                    send_sem=send_sems.at[a, chip], recv_sem=recv_sems.at[a, chip],
                    device_id=(x, y, 1 - c), device_id_type=MESH))
        for cp in copies:
            cp.start()
        for cp in copies:
            cp.wait()

    return pl.pallas_call(
        body,
        out_shape=[jax.ShapeDtypeStruct((4,) + t.shape[2:], t.dtype) for t in grads],
        in_specs=[hbm] * na,
        out_specs=[hbm] * na,
        scratch_shapes=[pltpu.SemaphoreType.DMA((na, 4)), pltpu.SemaphoreType.DMA((na, 4))],
        name=name,
    )(*grads)


def _chip_exchange(partials, name):
    na = len(partials)
    hbm = pl.BlockSpec(memory_space=pl.ANY)

    def body(*refs):
        ins, outs = refs[:na], refs[na:2 * na]
        send_sems, recv_sems = refs[2 * na:]
        x, y, c = _place()
        chips = [(1 - x, y), (x, 1 - y), (1 - x, 1 - y)]
        copies = []
        for a in range(na):
            for k, (cx, cy) in enumerate(chips):
                copies.append(pltpu.make_async_remote_copy(
                    src_ref=ins[a].at[2 * cx + cy], dst_ref=outs[a].at[k],
                    send_sem=send_sems.at[a, k], recv_sem=recv_sems.at[a, k],
                    device_id=(cx, cy, c), device_id_type=MESH))
        for cp in copies:
            cp.start()
        for cp in copies:
            cp.wait()

    return pl.pallas_call(
        body,
        out_shape=[jax.ShapeDtypeStruct((3,) + t.shape[1:], t.dtype) for t in partials],
        in_specs=[hbm] * na,
        out_specs=[hbm] * na,
        scratch_shapes=[pltpu.SemaphoreType.DMA((na, 3)), pltpu.SemaphoreType.DMA((na, 3))],
        name=name,
    )(*partials)


def _pad_rows(t, rows):
    return jnp.pad(t, ((0, rows - t.shape[0]), (0, D_MODEL - t.shape[1])))


def _pack_small(g_mix, g_group, g_mlp, g_final, conv, sinks):
    return jnp.concatenate([
        g_mix, g_group, g_mlp, g_final.reshape(1, D_MODEL),
        _pad_rows(conv.reshape(DEPTH * 3, CONV_CH), DEPTH * 3),
        _pad_rows(sinks.reshape(1, DEPTH * 6), 1),
        jnp.zeros((SMALL_ROWS - 14, D_MODEL), F32)], axis=0)


def _unpack_small(slab):
    return (slab[0:2], slab[2:4], slab[4:6], slab[6], slab[7:13, :CONV_CH].reshape(DEPTH, 3, CONV_CH),
            slab[13, :DEPTH * 6].reshape(DEPTH, 2, 3))


def kernel(x, w_in, conv_w, sinks, g_mix, g_group, w_o, g_mlp, w_ff_in, w_ff_out, g_final, loss_target, m_w_in, m_conv_w, m_sinks, m_g_mix, m_g_group, m_w_o, m_g_mlp, m_w_ff_in, m_w_ff_out, m_g_final, v_w_in, v_conv_w, v_sinks, v_g_mix, v_g_group, v_w_o, v_g_mlp, v_w_ff_in, v_w_ff_out, v_g_final):
    ax, ay, ac = _place()
    chip = 2 * ax + ay
    dev = 4 * ax + 2 * ay + ac
    pos = jnp.stack([chip, ac]).astype(jnp.int32)

    x0 = x.reshape(SEQ, D_MODEL)
    target = loss_target.reshape(SEQ, D_MODEL)

    shards = []
    for l in range(DEPTH):
        shards += [w_in[l].T.astype(BF16), w_o[l].astype(BF16), w_ff_in[l].T.astype(BF16), w_ff_out[l].astype(BF16)]
    conv_tile = jnp.pad(conv_w.reshape(DEPTH * 3, CONV_CH // N_DEV), ((0, 2), (0, LANES - CONV_CH // N_DEV)))
    *full, conv_all = _all_gather(shards + [conv_tile], "gather_weights")
    conv_full = conv_all.reshape(N_DEV, 8, LANES)[:, :DEPTH * 3, :CONV_CH // N_DEV]
    conv_full = conv_full.transpose(1, 0, 2).reshape(DEPTH, 3, CONV_CH)

    loss_slab, dx, grads, small = _local_step(x0, target, full, conv_full, sinks, g_mix, g_group, g_mlp, g_final)
    loss = lax.psum(loss_slab[0, 0], ("x", "y", "c"))
    return _finish(loss, dx, grads, small, pos, dev, w_in, conv_w, sinks, g_mix, g_group, w_o, g_mlp, w_ff_in, w_ff_out, g_final, m_w_in, m_conv_w, m_sinks, m_g_mix, m_g_group, m_w_o, m_g_mlp, m_w_ff_in, m_w_ff_out, m_g_final, v_w_in, v_conv_w, v_sinks, v_g_mix, v_g_group, v_w_o, v_g_mlp, v_w_ff_in, v_w_ff_out, v_g_final)


def _local_step(x0, target, full, conv_full, sinks, g_mix, g_group, g_mlp, g_final):
    sink_lanes = jnp.repeat(sinks.reshape(DEPTH, 6), HEAD_DIM, axis=1)
    no_sink = jnp.full((1, A_WIDTH), NEG_BIG, F32)

    saved = []
    xc = x0
    for l in range(DEPTH):
        wt_in, wo, wt_ff1, w_ff2 = full[4 * l:4 * l + 4]
        z, h = _norm_mm(xc, g_mix[l:l + 1], wt_in, f"in_proj_{l}")
        ya, lse_a = _attn_fwd(z, no_sink, 0.0, QA_BLK, KA_BLK, VA_BLK, DILATED_PATTERNS, A_MAX_DIST, False,
                              f"dilated_attn_{l}")
        yb = _conv_fwd(z, conv_full[l], f"conv_{l}")
        sink_l = sink_lanes[l:l + 1]
        yc, lse_c = _attn_fwd(z, sink_l, 1.0, QC_BLK, KC_BLK, VC_BLK, (1,), C_MAX_DIST, True, f"window_attn_{l}")
        y, x1 = _mix_out(ya, yb, yc, g_group[l:l + 1], wo, xc, f"mix_out_{l}")
        u, h2 = _norm_mm(x1, g_mlp[l:l + 1], wt_ff1, f"ff_in_{l}")
        x2 = _act_mm_res(u, w_ff2, x1, f"ff_out_{l}")
        saved.append((xc, z, h, ya, lse_a, yb, yc, lse_c, sink_l, y, x1, u, h2))
        xc = x2

    loss_slab, dx, dg_final = _loss_head(xc, g_final.reshape(1, D_MODEL), target, "loss_head")

    grads = [None] * (4 * DEPTH)
    dg_mix, dg_group, dg_mlp, dconv, dsinks = [None] * DEPTH, [None] * DEPTH, [None] * DEPTH, [None] * DEPTH, [None] * DEPTH
    for l in reversed(range(DEPTH)):
        wt_in, wo, wt_ff1, w_ff2 = full[4 * l:4 * l + 4]
        xin, z, h, ya, lse_a, yb, yc, lse_c, sink_l, y, x1, u, h2 = saved[l]
        du, a = _mlp_bwd_act(dx, w_ff2, u, f"ff_out_bwd_{l}")
        grads[4 * l + 3] = _mm_tn(a, dx, f"grad_w_ff_out_{l}")
        grads[4 * l + 2] = _mm_tn(du, h2, f"grad_w_ff_in_{l}")
        dx1, dg_mlp[l] = _mm_nn_normbwd(du, wt_ff1, x1, dx, g_mlp[l:l + 1], f"ff_in_bwd_{l}")
        grads[4 * l + 1] = _mm_tn(y, dx1, f"grad_w_o_{l}")
        dya, dyb, dyc, dg_group[l] = _mix_bwd(dx1, wo, ya, yb, yc, g_group[l:l + 1], f"mix_out_bwd_{l}")
        dqa, dka, dva, _ = _attn_bwd(z, dya, ya, lse_a, no_sink, QA_BLK, KA_BLK, VA_BLK, DILATED_PATTERNS,
                                     A_MAX_DIST, False, f"dilated_attn_bwd_{l}")
        dgb, dgc, dxb, dcw = _conv_bwd(z, conv_full[l], dyb, f"conv_bwd_{l}")
        dqc, dkc, dvc, dsink = _attn_bwd(z, dyc, yc, lse_c, sink_l, QC_BLK, KC_BLK, VC_BLK, (1,), C_MAX_DIST, True,
                                         f"window_attn_bwd_{l}")
        dz = jnp.concatenate([dqa, dka, dva, dgb, dgc, dxb, dqc, dkc, dvc], axis=1)
        grads[4 * l] = _mm_tn(dz, h, f"grad_w_in_{l}")
        dx, dg_mix[l] = _mm_nn_normbwd(dz, wt_in, xin, dx1, g_mix[l:l + 1], f"in_proj_bwd_{l}")
        dconv[l] = dcw[:3]
        dsinks[l] = dsink[0, ::HEAD_DIM]
    small = _pack_small(jnp.concatenate(dg_mix), jnp.concatenate(dg_group), jnp.concatenate(dg_mlp),
                        dg_final, jnp.stack(dconv), jnp.stack(dsinks))
    return loss_slab, dx, grads, small


def _finish(loss, dx, grads, small, pos, dev, w_in, conv_w, sinks, g_mix, g_group, w_o, g_mlp, w_ff_in, w_ff_out, g_final, m_w_in, m_conv_w, m_sinks, m_g_mix, m_g_group, m_w_o, m_g_mlp, m_w_ff_in, m_w_ff_out, m_g_final, v_w_in, v_conv_w, v_sinks, v_g_mix, v_g_group, v_w_o, v_g_mlp, v_w_ff_in, v_w_ff_out, v_g_final):
    grad_x = dx.reshape(1, SEQ, D_MODEL)

    g4 = [g.reshape(4, 2, g.shape[0] // N_DEV, D_MODEL) for g in grads]
    r1 = _sibling_exchange(g4, "grad_sibling_exchange")
    partial = [_pair_sum(g4[i], r1[i], pos, f"grad_pair_sum_{i}") for i in range(4 * DEPTH)]
    r2 = _chip_exchange(partial, "grad_chip_exchange")
    gsum = [_final_sum(partial[i], r2[i], pos, f"grad_final_sum_{i}") for i in range(4 * DEPTH)]
    grad_w_in = jnp.stack([gsum[4 * l].T for l in range(DEPTH)])
    grad_w_o = jnp.stack([gsum[4 * l + 1] for l in range(DEPTH)])
    grad_w_ff_in = jnp.stack([gsum[4 * l + 2].T for l in range(DEPTH)])
    grad_w_ff_out = jnp.stack([gsum[4 * l + 3] for l in range(DEPTH)])

    (small_all,) = _all_gather([small], "gather_small_grads")
    zeros_conv = jnp.zeros((DEPTH, 3, CONV_CH), F32)
    sw = _pack_small(g_mix, g_group, g_mlp, g_final, zeros_conv, sinks)
    sm = _pack_small(m_g_mix, m_g_group, m_g_mlp, m_g_final, zeros_conv, m_sinks)
    sv = _pack_small(v_g_mix, v_g_group, v_g_mlp, v_g_final, zeros_conv, v_sinks)
    sg, sd, snm, snv = _small_sum_adamw(small_all.reshape(N_DEV, SMALL_ROWS, D_MODEL), sw, sm, sv, "small_adamw")
    grad_g_mix, grad_g_group, grad_g_mlp, grad_g_final, conv_grad_full, grad_sinks = _unpack_small(sg)
    delta_g_mix, delta_g_group, delta_g_mlp, delta_g_final, _, delta_sinks = _unpack_small(sd)
    new_m_g_mix, new_m_g_group, new_m_g_mlp, new_m_g_final, _, new_m_sinks = _unpack_small(snm)
    new_v_g_mix, new_v_g_group, new_v_g_mlp, new_v_g_final, _, new_v_sinks = _unpack_small(snv)
    cs = CONV_CH // N_DEV
    grad_conv_w = lax.dynamic_slice_in_dim(conv_grad_full, dev * cs, cs, axis=2)

    def tile_of(t):
        return jnp.pad(t.reshape(1, DEPTH * 3 * cs), ((0, 7), (0, 256 - DEPTH * 3 * cs)))

    cd, cm, cv = _adamw(tile_of(conv_w), tile_of(grad_conv_w), tile_of(m_conv_w), tile_of(v_conv_w), "conv_adamw")
    untile = lambda t: t[0, :DEPTH * 3 * cs].reshape(DEPTH, 3, cs)
    delta_conv_w, new_m_conv_w, new_v_conv_w = untile(cd), untile(cm), untile(cv)

    def big(w, g, m, v, name):
        shp = w.shape
        flat = lambda t: t.reshape(shp[0] * shp[1], shp[2])
        return [t.reshape(shp) for t in _adamw(flat(w), flat(g), flat(m), flat(v), name)]

    delta_w_in, new_m_w_in, new_v_w_in = big(w_in, grad_w_in, m_w_in, v_w_in, "adamw_w_in")
    delta_w_o, new_m_w_o, new_v_w_o = big(w_o, grad_w_o, m_w_o, v_w_o, "adamw_w_o")
    delta_w_ff_in, new_m_w_ff_in, new_v_w_ff_in = big(w_ff_in, grad_w_ff_in, m_w_ff_in, v_w_ff_in, "adamw_w_ff_in")
    delta_w_ff_out, new_m_w_ff_out, new_v_w_ff_out = big(w_ff_out, grad_w_ff_out, m_w_ff_out, v_w_ff_out, "adamw_w_ff_out")

    return (loss, grad_x, grad_w_in, grad_conv_w, grad_sinks, grad_g_mix, grad_g_group, grad_w_o, grad_g_mlp,
            grad_w_ff_in, grad_w_ff_out, grad_g_final,
            delta_w_in, delta_conv_w, delta_sinks, delta_g_mix, delta_g_group, delta_w_o, delta_g_mlp,
            delta_w_ff_in, delta_w_ff_out, delta_g_final,
            new_m_w_in, new_m_conv_w, new_m_sinks, new_m_g_mix, new_m_g_group, new_m_w_o, new_m_g_mlp,
            new_m_w_ff_in, new_m_w_ff_out, new_m_g_final,
            new_v_w_in, new_v_conv_w, new_v_sinks, new_v_g_mix, new_v_g_group, new_v_w_o, new_v_g_mlp,
            new_v_w_ff_in, new_v_w_ff_out, new_v_g_final)
```

```python
import functools

import jax
import jax.numpy as jnp
from jax import lax
from jax.experimental import pallas as pl
from jax.experimental.pallas import tpu as pltpu

F32 = jnp.float32
BF16 = jnp.bfloat16
MESH = pl.DeviceIdType.MESH

N_DEV = 8
SEQ = 4096
D_MODEL = 1024
DEPTH = 2
HEAD_DIM = 64
LANES = 128
A_WIDTH = 384
CONV_CH = 256
C_WIDTH = 384
KV_WIDTH = 128
IN_WIDTH = 2560
D_FF = 4096
BLOCK = 128
DILATED_PATTERNS = (1, 4, 16)
A_MAX_DIST = 128
C_MAX_DIST = 127
EPS = 1e-6
SCALE = HEAD_DIM ** -0.5
NEG_BIG = -1e30

QA_BLK, KA_BLK, VA_BLK = 0, 3, 6
GB_BLK, GC_BLK, XB_BLK = 9, 11, 13
QC_BLK, KC_BLK, VC_BLK = 15, 18, 19

ADAM_LR = 0.001
ADAM_B1 = 0.9
ADAM_B2 = 0.999
ADAM_EPS = 1e-08
ADAM_WD = 0.01
ADAM_STEP = 10

VMEM_LIMIT = 56 * 1024 * 1024
ROW_TILE = 512
COL_CHUNK = 512
SMALL_ROWS = 48


def _dot_nn(a, b):
    return lax.dot_general(a, b, (((1,), (0,)), ((), ())), preferred_element_type=F32)


def _dot_nt(a, b):
    return lax.dot_general(a, b, (((1,), (1,)), ((), ())), preferred_element_type=F32)


def _dot_tn(a, b):
    return lax.dot_general(a, b, (((0,), (0,)), ((), ())), preferred_element_type=F32)


def _params(*sem):
    return pltpu.CompilerParams(dimension_semantics=sem, vmem_limit_bytes=VMEM_LIMIT)


def _rms_scale(t):
    return lax.rsqrt(jnp.mean(t * t, axis=-1, keepdims=True) + EPS)


def _rms_bwd(n, r, dn):
    return r * (dn - n * jnp.mean(dn * n, axis=-1, keepdims=True))


def _norm_mm(x, g, wt, relu2, name):
    s, d = x.shape
    n = wt.shape[0]
    tm = ROW_TILE

    def body(x_ref, g_ref, w_ref, o_ref, h_ref):
        xx = x_ref[...]
        h = ((xx * _rms_scale(xx)) * g_ref[...]).astype(BF16)
        h_ref[...] = h
        for n0 in range(0, n, COL_CHUNK):
            zc = _dot_nt(h, w_ref[n0:n0 + COL_CHUNK, :])
            if relu2:
                zc = jnp.square(jnp.maximum(zc, 0.0)).astype(BF16)
            o_ref[:, n0:n0 + COL_CHUNK] = zc

    return pl.pallas_call(
        body,
        grid=(s // tm,),
        in_specs=[pl.BlockSpec((tm, d), lambda i: (i, 0)),
                  pl.BlockSpec((1, d), lambda i: (0, 0)),
                  pl.BlockSpec((n, d), lambda i: (0, 0))],
        out_specs=[pl.BlockSpec((tm, n), lambda i: (i, 0)),
                   pl.BlockSpec((tm, d), lambda i: (i, 0))],
        out_shape=[jax.ShapeDtypeStruct((s, n), BF16 if relu2 else F32), jax.ShapeDtypeStruct((s, d), BF16)],
        compiler_params=_params("parallel"),
        name=name,
    )(x, g, wt)


def _mm_res(a, w2, x1, name):
    s, f = a.shape
    d = w2.shape[1]
    tm = ROW_TILE

    def body(a_ref, w_ref, x_ref, o_ref):
        o_ref[...] = x_ref[...] + _dot_nn(a_ref[...], w_ref[...])

    return pl.pallas_call(
        body,
        grid=(s // tm,),
        in_specs=[pl.BlockSpec((tm, f), lambda i: (i, 0)),
                  pl.BlockSpec((f, d), lambda i: (0, 0)),
                  pl.BlockSpec((tm, d), lambda i: (i, 0))],
        out_specs=pl.BlockSpec((tm, d), lambda i: (i, 0)),
        out_shape=jax.ShapeDtypeStruct((s, d), F32),
        compiler_params=_params("parallel"),
        name=name,
    )(a, w2, x1)


def _mix_out(ya, yb, yc, gg, wo, x0, name):
    s = ya.shape[0]
    d = wo.shape[1]
    tm = ROW_TILE

    def body(ya_ref, yb_ref, yc_ref, g_ref, w_ref, x_ref, y_ref, o_ref):
        parts = []
        for ref in (ya_ref, yb_ref, yc_ref):
            t = ref[...]
            parts.append(t * _rms_scale(t))
        y = (jnp.concatenate(parts, axis=1) * g_ref[...]).astype(BF16)
        y_ref[...] = y
        o_ref[...] = x_ref[...] + _dot_nn(y, w_ref[...])

    return pl.pallas_call(
        body,
        grid=(s // tm,),
        in_specs=[pl.BlockSpec((tm, A_WIDTH), lambda i: (i, 0)),
                  pl.BlockSpec((tm, CONV_CH), lambda i: (i, 0)),
                  pl.BlockSpec((tm, C_WIDTH), lambda i: (i, 0)),
                  pl.BlockSpec((1, d), lambda i: (0, 0)),
                  pl.BlockSpec((d, d), lambda i: (0, 0)),
                  pl.BlockSpec((tm, d), lambda i: (i, 0))],
        out_specs=[pl.BlockSpec((tm, d), lambda i: (i, 0)),
                   pl.BlockSpec((tm, d), lambda i: (i, 0))],
        out_shape=[jax.ShapeDtypeStruct((s, d), BF16), jax.ShapeDtypeStruct((s, d), F32)],
        compiler_params=_params("parallel"),
        name=name,
    )(ya, yb, yc, gg, wo, x0)


def _loss_head(x, g, target, name):
    s, d = x.shape
    tm = ROW_TILE

    def body(x_ref, g_ref, t_ref, loss_ref, dx_ref, dxb_ref, dg_ref):
        @pl.when(pl.program_id(0) == 0)
        def _():
            loss_ref[...] = jnp.zeros_like(loss_ref)
            dg_ref[...] = jnp.zeros_like(dg_ref)

        xx = x_ref[...]
        r = _rms_scale(xx)
        n = xx * r
        gv = g_ref[...]
        err = n * gv - t_ref[...]
        per_tok = jnp.sum(err * err, axis=1, keepdims=True) * (1.0 / d)
        loss_ref[...] += 0.5 * jnp.sum(per_tok, axis=0, keepdims=True)
        dout = err * (1.0 / d)
        dg_ref[...] += jnp.sum(dout * n, axis=0, keepdims=True)
        dx = _rms_bwd(n, r, dout * gv)
        dx_ref[...] = dx
        dxb_ref[...] = dx.astype(BF16)

    return pl.pallas_call(
        body,
        grid=(s // tm,),
        in_specs=[pl.BlockSpec((tm, d), lambda i: (i, 0)),
                  pl.BlockSpec((1, d), lambda i: (0, 0)),
                  pl.BlockSpec((tm, d), lambda i: (i, 0))],
        out_specs=[pl.BlockSpec((8, LANES), lambda i: (0, 0)),
                   pl.BlockSpec((tm, d), lambda i: (i, 0)),
                   pl.BlockSpec((tm, d), lambda i: (i, 0)),
                   pl.BlockSpec((1, d), lambda i: (0, 0))],
        out_shape=[jax.ShapeDtypeStruct((8, LANES), F32), jax.ShapeDtypeStruct((s, d), F32),
                   jax.ShapeDtypeStruct((s, d), BF16), jax.ShapeDtypeStruct((1, d), F32)],
        compiler_params=_params("arbitrary"),
        name=name,
    )(x, g, target)


def _mlp_bwd_act(dxb, w2, a, name):
    s, d = dxb.shape
    f = w2.shape[0]
    tm = ROW_TILE

    def body(dx_ref, w_ref, a_ref, du_ref):
        dx = dx_ref[...]
        for n0 in range(0, f, COL_CHUNK):
            da = _dot_nt(dx, w_ref[n0:n0 + COL_CHUNK, :])
            rl = jnp.sqrt(a_ref[:, n0:n0 + COL_CHUNK].astype(F32))
            du_ref[:, n0:n0 + COL_CHUNK] = (da * (2.0 * rl)).astype(BF16)

    return pl.pallas_call(
        body,
        grid=(s // tm,),
        in_specs=[pl.BlockSpec((tm, d), lambda i: (i, 0)),
                  pl.BlockSpec((f, d), lambda i: (0, 0)),
                  pl.BlockSpec((tm, f), lambda i: (i, 0))],
        out_specs=pl.BlockSpec((tm, f), lambda i: (i, 0)),
        out_shape=jax.ShapeDtypeStruct((s, f), BF16),
        compiler_params=_params("parallel"),
        name=name,
    )(dxb, w2, a)


def _mm_tn(a, b, name):
    s, n = a.shape
    d = b.shape[1]
    tn = 512

    def body(a_ref, b_ref, o_ref, acc):
        for k0 in range(0, s, ROW_TILE):
            part = _dot_tn(a_ref[k0:k0 + ROW_TILE, :], b_ref[k0:k0 + ROW_TILE, :])
            if k0 == 0:
                acc[...] = part
            else:
                acc[...] += part
        o_ref[...] = acc[...].astype(BF16)

    return pl.pallas_call(
        body,
        grid=(n // tn,),
        in_specs=[pl.BlockSpec((s, tn), lambda j: (0, j)),
                  pl.BlockSpec((s, d), lambda j: (0, 0))],
        out_specs=pl.BlockSpec((tn, d), lambda j: (j, 0)),
        out_shape=jax.ShapeDtypeStruct((n, d), BF16),
        scratch_shapes=[pltpu.VMEM((tn, d), F32)],
        compiler_params=_params("parallel"),
        name=name,
    )(a, b)


def _mm_nn_normbwd(dact, wt, x, dres, g, name):
    s, kdim = dact.shape
    d = wt.shape[1]
    tm = ROW_TILE

    def body(a_ref, w_ref, x_ref, r_ref, g_ref, o_ref, ob_ref, dg_ref):
        @pl.when(pl.program_id(0) == 0)
        def _():
            dg_ref[...] = jnp.zeros_like(dg_ref)

        dh = _dot_nn(a_ref[...], w_ref[...])
        xx = x_ref[...]
        r = _rms_scale(xx)
        n = xx * r
        dg_ref[...] += jnp.sum(dh * n, axis=0, keepdims=True)
        dx = r_ref[...] + _rms_bwd(n, r, dh * g_ref[...])
        o_ref[...] = dx
        ob_ref[...] = dx.astype(BF16)

    return pl.pallas_call(
        body,
        grid=(s // tm,),
        in_specs=[pl.BlockSpec((tm, kdim), lambda i: (i, 0)),
                  pl.BlockSpec((kdim, d), lambda i: (0, 0)),
                  pl.BlockSpec((tm, d), lambda i: (i, 0)),
                  pl.BlockSpec((tm, d), lambda i: (i, 0)),
                  pl.BlockSpec((1, d), lambda i: (0, 0))],
        out_specs=[pl.BlockSpec((tm, d), lambda i: (i, 0)),
                   pl.BlockSpec((tm, d), lambda i: (i, 0)),
                   pl.BlockSpec((1, d), lambda i: (0, 0))],
        out_shape=[jax.ShapeDtypeStruct((s, d), F32), jax.ShapeDtypeStruct((s, d), BF16),
                   jax.ShapeDtypeStruct((1, d), F32)],
        compiler_params=_params("arbitrary"),
        name=name,
    )(dact, wt, x, dres, g)


def _mix_bwd(dx1, wo, ya, yb, yc, gg, name):
    s, d = dx1.shape
    tm = ROW_TILE
    widths = (A_WIDTH, CONV_CH, C_WIDTH)

    def body(dx_ref, w_ref, ya_ref, yb_ref, yc_ref, g_ref, da_ref, db_ref, dc_ref, dg_ref):
        @pl.when(pl.program_id(0) == 0)
        def _():
            dg_ref[...] = jnp.zeros_like(dg_ref)

        dy = _dot_nt(dx_ref[...], w_ref[...])
        gv = g_ref[...]
        off = 0
        dgs = []
        for ref, out, w in zip((ya_ref, yb_ref, yc_ref), (da_ref, db_ref, dc_ref), widths):
            t = ref[...]
            r = _rms_scale(t)
            n = t * r
            dyg = dy[:, off:off + w]
            dgs.append(jnp.sum(dyg * n, axis=0, keepdims=True))
            out[...] = _rms_bwd(n, r, dyg * gv[:, off:off + w])
            off += w
        dg_ref[...] += jnp.concatenate(dgs, axis=1)

    return pl.pallas_call(
        body,
        grid=(s // tm,),
        in_specs=[pl.BlockSpec((tm, d), lambda i: (i, 0)),
                  pl.BlockSpec((d, d), lambda i: (0, 0)),
                  pl.BlockSpec((tm, A_WIDTH), lambda i: (i, 0)),
                  pl.BlockSpec((tm, CONV_CH), lambda i: (i, 0)),
                  pl.BlockSpec((tm, C_WIDTH), lambda i: (i, 0)),
                  pl.BlockSpec((1, d), lambda i: (0, 0))],
        out_specs=[pl.BlockSpec((tm, A_WIDTH), lambda i: (i, 0)),
                   pl.BlockSpec((tm, CONV_CH), lambda i: (i, 0)),
                   pl.BlockSpec((tm, C_WIDTH), lambda i: (i, 0)),
                   pl.BlockSpec((1, d), lambda i: (0, 0))],
        out_shape=[jax.ShapeDtypeStruct((s, A_WIDTH), F32), jax.ShapeDtypeStruct((s, CONV_CH), F32),
                   jax.ShapeDtypeStruct((s, C_WIDTH), F32), jax.ShapeDtypeStruct((1, d), F32)],
        compiler_params=_params("arbitrary"),
        name=name,
    )(dx1, wo, ya, yb, yc, gg)


CONV_CHUNK = 256
CONV_HALO = 8


def _conv_fwd(z, cw, name):
    s = z.shape[0]
    nch = s // CONV_CHUNK

    def body(gb_ref, gc_ref, xb_ref, w_ref, o_ref, us):
        us[pl.ds(0, CONV_HALO), :] = jnp.zeros((CONV_HALO, LANES), F32)
        us[pl.ds(CONV_HALO, s), :] = gc_ref[...] * xb_ref[...]
        w0, w1, w2 = w_ref[0:1, :], w_ref[1:2, :], w_ref[2:3, :]

        def chunk(c, carry):
            st = pl.multiple_of(c * CONV_CHUNK, CONV_CHUNK)
            ext = us[pl.ds(st, CONV_CHUNK + CONV_HALO), :]
            y = (w0 * ext[CONV_HALO - 2:CONV_HALO - 2 + CONV_CHUNK]
                 + w1 * ext[CONV_HALO - 1:CONV_HALO - 1 + CONV_CHUNK]
                 + w2 * ext[CONV_HALO:])
            o_ref[pl.ds(st, CONV_CHUNK), :] = gb_ref[pl.ds(st, CONV_CHUNK), :] * y
            return carry

        lax.fori_loop(0, nch, chunk, 0)

    col = lambda blk: pl.BlockSpec((s, LANES), lambda j, blk=blk: (0, blk + j))
    return pl.pallas_call(
        body,
        grid=(CONV_CH // LANES,),
        in_specs=[col(GB_BLK), col(GC_BLK), col(XB_BLK), pl.BlockSpec((3, LANES), lambda j: (0, j))],
        out_specs=pl.BlockSpec((s, LANES), lambda j: (0, j)),
        out_shape=jax.ShapeDtypeStruct((s, CONV_CH), F32),
        scratch_shapes=[pltpu.VMEM((s + CONV_HALO, LANES), F32)],
        compiler_params=_params("parallel"),
        name=name,
    )(z, z, z, cw)


def _conv_bwd(z, cw, dyb, name):
    s = z.shape[0]
    nch = s // CONV_CHUNK

    def body(gb_ref, gc_ref, xb_ref, w_ref, dy_ref, dgb_ref, dgc_ref, dxb_ref, dw_ref, us, ds_):
        us[pl.ds(0, CONV_HALO), :] = jnp.zeros((CONV_HALO, LANES), F32)
        us[pl.ds(CONV_HALO, s), :] = gc_ref[...] * xb_ref[...]
        ds_[pl.ds(s, CONV_HALO), :] = jnp.zeros((CONV_HALO, LANES), F32)
        ds_[pl.ds(0, s), :] = dy_ref[...] * gb_ref[...]
        w0, w1, w2 = w_ref[0:1, :], w_ref[1:2, :], w_ref[2:3, :]
        zero = jnp.zeros((1, LANES), F32)

        def chunk(c, carry):
            a0, a1, a2 = carry
            st = pl.multiple_of(c * CONV_CHUNK, CONV_CHUNK)
            rows = pl.ds(st, CONV_CHUNK)
            ext = us[pl.ds(st, CONV_CHUNK + CONV_HALO), :]
            um2 = ext[CONV_HALO - 2:CONV_HALO - 2 + CONV_CHUNK]
            um1 = ext[CONV_HALO - 1:CONV_HALO - 1 + CONV_CHUNK]
            u0 = ext[CONV_HALO:]
            dext = ds_[pl.ds(st, CONV_CHUNK + CONV_HALO), :]
            dc0 = dext[:CONV_CHUNK]
            du = w2 * dc0 + w1 * dext[1:1 + CONV_CHUNK] + w0 * dext[2:2 + CONV_CHUNK]
            yconv = w0 * um2 + w1 * um1 + w2 * u0
            dgb_ref[rows, :] = (dy_ref[rows, :] * yconv).astype(BF16)
            dgc_ref[rows, :] = (du * xb_ref[rows, :]).astype(BF16)
            dxb_ref[rows, :] = (du * gc_ref[rows, :]).astype(BF16)
            a0 = a0 + jnp.sum(dc0 * um2, axis=0, keepdims=True)
            a1 = a1 + jnp.sum(dc0 * um1, axis=0, keepdims=True)
            a2 = a2 + jnp.sum(dc0 * u0, axis=0, keepdims=True)
            return a0, a1, a2

        a0, a1, a2 = lax.fori_loop(0, nch, chunk, (zero, zero, zero))
        dw_ref[...] = jnp.concatenate([a0, a1, a2, jnp.zeros((5, LANES), F32)], axis=0)

    col = lambda blk: pl.BlockSpec((s, LANES), lambda j, blk=blk: (0, blk + j))
    own = pl.BlockSpec((s, LANES), lambda j: (0, j))
    return pl.pallas_call(
        body,
        grid=(CONV_CH // LANES,),
        in_specs=[col(GB_BLK), col(GC_BLK), col(XB_BLK), pl.BlockSpec((3, LANES), lambda j: (0, j)), own],
        out_specs=[own, own, own, pl.BlockSpec((8, LANES), lambda j: (0, j))],
        out_shape=[jax.ShapeDtypeStruct((s, CONV_CH), BF16)] * 3 + [jax.ShapeDtypeStruct((8, CONV_CH), F32)],
        scratch_shapes=[pltpu.VMEM((s + CONV_HALO, LANES), F32), pltpu.VMEM((s + CONV_HALO, LANES), F32)],
        compiler_params=_params("parallel"),
        name=name,
    )(z, z, z, cw, dyb)


ATTN_ROWS = 512
ATTN_UNROLL = 4


def _band_rows(b, d, r):
    base = pl.multiple_of(b * (BLOCK * d), BLOCK)
    prev = jnp.maximum(base - BLOCK * d, 0)
    if d == 1:
        return pl.ds(base, BLOCK), pl.ds(pl.multiple_of(prev, BLOCK), BLOCK)
    return pl.ds(base + r, BLOCK, stride=d), pl.ds(prev + r, BLOCK, stride=d)


def _band_mask(b, max_dist):
    qi = lax.broadcasted_iota(jnp.int32, (BLOCK, 2 * BLOCK), 0)
    kj = lax.broadcasted_iota(jnp.int32, (BLOCK, 2 * BLOCK), 1)
    dist = BLOCK + qi - kj
    first_key = jnp.where(b > 0, 0, BLOCK)
    return (dist >= 0) & (dist <= max_dist) & (kj >= first_key)


def _lane_half():
    return (lax.broadcasted_iota(jnp.int32, (1, LANES), 1) >= HEAD_DIM).astype(jnp.int32)


def _kv_for_pair(t, pair):
    half = _lane_half()
    want = (pair + half) >> 1
    return jnp.where(want != half, pltpu.roll(t, HEAD_DIM, 1), t)


def _kv_grad_from_pair(t, pair):
    half = _lane_half()
    mine = ((pair + half) >> 1) == half
    other = ((pair + 1 - half) >> 1) == half
    fold = t + pltpu.roll(t, HEAD_DIM, 1)
    return jnp.where(mine & other, fold, jnp.where(mine, t, 0.0))


def _stack_heads(t, head0):
    zero = jnp.zeros_like(t)
    return jnp.concatenate([jnp.where(head0, t, zero), jnp.where(head0, zero, t)], axis=0)


def _unstack_heads(t, head0):
    return jnp.where(head0, t[:BLOCK], t[BLOCK:])


def _block_loops(s, patterns, unroll, one_block):
    for d in patterns:
        nb = (s // BLOCK) // d
        if d == 1:
            def trip(i, carry):
                for u in range(unroll):
                    one_block(i * unroll + u, 1, 0)
                return carry
            lax.fori_loop(0, nb // unroll, trip, 0)
        else:
            for r0 in range(0, d, unroll):
                def trip(b, carry, d=d, r0=r0):
                    for u in range(unroll):
                        one_block(b, d, r0 + u)
                    return carry
                lax.fori_loop(0, nb, trip, 0)


def _attn_fwd(z, m_init, l_init, q_blk, k_blk, v_blk, patterns, max_dist, gqa, name):
    s = z.shape[0]
    npair = 3

    def body(q_ref, k_ref, v_ref, mi_ref, o_ref, lse_ref, m_scr, l_scr):
        pair = pl.program_id(0)
        head0 = lax.broadcasted_iota(jnp.int32, (1, LANES), 1) < HEAD_DIM

        def init(c, carry):
            rows = pl.ds(pl.multiple_of(c * ATTN_ROWS, ATTN_ROWS), ATTN_ROWS)
            m_scr[rows, :] = jnp.broadcast_to(mi_ref[...], (ATTN_ROWS, LANES))
            l_scr[rows, :] = jnp.full((ATTN_ROWS, LANES), l_init, F32)
            o_ref[rows, :] = jnp.zeros((ATTN_ROWS, LANES), F32)
            return carry

        lax.fori_loop(0, s // ATTN_ROWS, init, 0)

        def one_block(b, d, r):
            rq, rp = _band_rows(b, d, r)
            mask = _band_mask(b, max_dist)
            mask2 = jnp.concatenate([mask, mask], axis=0)
            q2 = _stack_heads((q_ref[rq, :] * SCALE).astype(BF16), head0)
            k2 = jnp.concatenate([k_ref[rp, :], k_ref[rq, :]], axis=0)
            v2 = jnp.concatenate([v_ref[rp, :], v_ref[rq, :]], axis=0)
            if gqa:
                k2 = _kv_for_pair(k2, pair)
                v2 = _kv_for_pair(v2, pair)
            sc = jnp.where(mask2, _dot_nt(q2, k2.astype(BF16)), -jnp.inf)
            mb = jnp.max(sc, axis=1, keepdims=True)
            p = jnp.exp(sc - mb)
            lb = jnp.sum(p, axis=1, keepdims=True)
            ob = _dot_nn(p.astype(BF16), v2.astype(BF16))
            m2 = _unstack_heads(jnp.broadcast_to(mb, (2 * BLOCK, LANES)), head0)
            l2 = _unstack_heads(jnp.broadcast_to(lb, (2 * BLOCK, LANES)), head0)
            o2 = _unstack_heads(ob, head0)
            m_old = m_scr[rq, :]
            m_new = jnp.maximum(m_old, m2)
            a_old = jnp.exp(m_old - m_new)
            a_blk = jnp.exp(m2 - m_new)
            o_ref[rq, :] = o_ref[rq, :] * a_old + o2 * a_blk
            l_scr[rq, :] = l_scr[rq, :] * a_old + l2 * a_blk
            m_scr[rq, :] = m_new

        _block_loops(s, patterns, ATTN_UNROLL, one_block)

        def fin(c, carry):
            rows = pl.ds(pl.multiple_of(c * ATTN_ROWS, ATTN_ROWS), ATTN_ROWS)
            l = l_scr[rows, :]
            o_ref[rows, :] = o_ref[rows, :] / l
            lse = m_scr[rows, :] + jnp.log(l)
            swapped = pltpu.roll(lse, HEAD_DIM, 1)
            lse_ref[rows, 0:LANES] = jnp.where(head0, lse, swapped)
            lse_ref[rows, LANES:2 * LANES] = jnp.where(head0, swapped, lse)
            return carry

        lax.fori_loop(0, s // ATTN_ROWS, fin, 0)

    kv = (lambda blk: pl.BlockSpec((s, LANES), lambda j, blk=blk: (0, blk))) if gqa else \
         (lambda blk: pl.BlockSpec((s, LANES), lambda j, blk=blk: (0, blk + j)))
    return pl.pallas_call(
        body,
        grid=(npair,),
        in_specs=[pl.BlockSpec((s, LANES), lambda j: (0, q_blk + j)), kv(k_blk), kv(v_blk),
                  pl.BlockSpec((1, LANES), lambda j: (0, j))],
        out_specs=[pl.BlockSpec((s, LANES), lambda j: (0, j)), pl.BlockSpec((s, 2 * LANES), lambda j: (0, j))],
        out_shape=[jax.ShapeDtypeStruct((s, npair * LANES), F32), jax.ShapeDtypeStruct((s, 2 * npair * LANES), F32)],
        scratch_shapes=[pltpu.VMEM((s, LANES), F32)] * 2,
        compiler_params=_params("parallel"),
        name=name,
    )(z, z, z, m_init)


def _attn_bwd(z, do, o, lse, m_init, q_blk, k_blk, v_blk, patterns, max_dist, gqa, name):
    s = z.shape[0]
    npair = 3

    def body(q_ref, k_ref, v_ref, do_ref, o_ref, lse0_ref, lse1_ref, mi_ref,
             dq_ref, dk_ref, dv_ref, dm_ref, dq_acc, dk_acc, dv_acc, dl0_scr, dl1_scr):
        pair = pl.program_id(0)
        head0 = lax.broadcasted_iota(jnp.int32, (1, LANES), 1) < HEAD_DIM

        def zero_kv():
            def f(c, carry):
                rows = pl.ds(pl.multiple_of(c * ATTN_ROWS, ATTN_ROWS), ATTN_ROWS)
                dk_acc[rows, :] = jnp.zeros((ATTN_ROWS, LANES), F32)
                dv_acc[rows, :] = jnp.zeros((ATTN_ROWS, LANES), F32)
                return carry
            lax.fori_loop(0, s // ATTN_ROWS, f, 0)

        if gqa:
            pl.when(pair == 0)(zero_kv)
        else:
            zero_kv()

        def prep(c, dm):
            rows = pl.ds(pl.multiple_of(c * ATTN_ROWS, ATTN_ROWS), ATTN_ROWS)
            dq_acc[rows, :] = jnp.zeros((ATTN_ROWS, LANES), F32)
            prod = do_ref[rows, :] * o_ref[rows, :]
            d0 = jnp.sum(jnp.where(head0, prod, 0.0), axis=1, keepdims=True)
            d1 = jnp.sum(jnp.where(head0, 0.0, prod), axis=1, keepdims=True)
            dl0_scr[rows, :] = jnp.broadcast_to(d0, (ATTN_ROWS, LANES))
            dl1_scr[rows, :] = jnp.broadcast_to(d1, (ATTN_ROWS, LANES))
            lse_own = jnp.where(head0, lse0_ref[rows, :], lse1_ref[rows, :])
            psink = jnp.exp(mi_ref[...] - lse_own)
            return dm - jnp.sum(psink * jnp.where(head0, d0, d1), axis=0, keepdims=True)

        dm_ref[...] = lax.fori_loop(0, s // ATTN_ROWS, prep, jnp.zeros((1, LANES), F32))

        def one_block(b, d, r):
            rq, rp = _band_rows(b, d, r)
            mask = _band_mask(b, max_dist)
            mask2 = jnp.concatenate([mask, mask], axis=0)
            q2 = _stack_heads((q_ref[rq, :] * SCALE).astype(BF16), head0)
            do2 = _stack_heads(do_ref[rq, :].astype(BF16), head0)
            k2 = jnp.concatenate([k_ref[rp, :], k_ref[rq, :]], axis=0)
            v2 = jnp.concatenate([v_ref[rp, :], v_ref[rq, :]], axis=0)
            if gqa:
                k2 = _kv_for_pair(k2, pair)
                v2 = _kv_for_pair(v2, pair)
            k2 = k2.astype(BF16)
            v2 = v2.astype(BF16)
            lse2 = jnp.concatenate([lse0_ref[rq, :], lse1_ref[rq, :]], axis=0)
            dl2 = jnp.concatenate([dl0_scr[rq, :], dl1_scr[rq, :]], axis=0)
            lse2 = jnp.concatenate([lse2, lse2], axis=1)
            dl2 = jnp.concatenate([dl2, dl2], axis=1)
            sc = _dot_nt(q2, k2)
            p = jnp.where(mask2, jnp.exp(sc - lse2), 0.0)
            dp = _dot_nt(do2, v2)
            dsc = (p * (dp - dl2)).astype(BF16)
            dq2 = _unstack_heads(_dot_nn(dsc, k2), head0)
            dk2 = _dot_tn(dsc, q2)
            dv2 = _dot_tn(p.astype(BF16), do2)
            if gqa:
                dk2 = _kv_grad_from_pair(dk2, pair)
                dv2 = _kv_grad_from_pair(dv2, pair)
            dq_acc[rq, :] += dq2 * SCALE
            dk_acc[rp, :] += dk2[:BLOCK]
            dk_acc[rq, :] += dk2[BLOCK:]
            dv_acc[rp, :] += dv2[:BLOCK]
            dv_acc[rq, :] += dv2[BLOCK:]

        _block_loops(s, patterns, ATTN_UNROLL, one_block)

        def out_q(c, carry):
            rows = pl.ds(pl.multiple_of(c * ATTN_ROWS, ATTN_ROWS), ATTN_ROWS)
            dq_ref[rows, :] = dq_acc[rows, :].astype(BF16)
            return carry

        lax.fori_loop(0, s // ATTN_ROWS, out_q, 0)

        def out_kv():
            def f(c, carry):
                rows = pl.ds(pl.multiple_of(c * ATTN_ROWS, ATTN_ROWS), ATTN_ROWS)
                dk_ref[rows, :] = dk_acc[rows, :].astype(BF16)
                dv_ref[rows, :] = dv_acc[rows, :].astype(BF16)
                return carry
            lax.fori_loop(0, s // ATTN_ROWS, f, 0)

        if gqa:
            pl.when(pair == npair - 1)(out_kv)
        else:
            out_kv()

    own = pl.BlockSpec((s, LANES), lambda j: (0, j))
    if gqa:
        kv = lambda blk: pl.BlockSpec((s, LANES), lambda j, blk=blk: (0, blk))
        kv_out = pl.BlockSpec((s, LANES), lambda j: (0, 0))
        kv_shape = jax.ShapeDtypeStruct((s, LANES), BF16)
    else:
        kv = lambda blk: pl.BlockSpec((s, LANES), lambda j, blk=blk: (0, blk + j))
        kv_out = own
        kv_shape = jax.ShapeDtypeStruct((s, npair * LANES), BF16)
    return pl.pallas_call(
        body,
        grid=(npair,),
        in_specs=[pl.BlockSpec((s, LANES), lambda j: (0, q_blk + j)), kv(k_blk), kv(v_blk), own, own,
                  pl.BlockSpec((s, LANES), lambda j: (0, 2 * j)), pl.BlockSpec((s, LANES), lambda j: (0, 2 * j + 1)),
                  pl.BlockSpec((1, LANES), lambda j: (0, j))],
        out_specs=[own, kv_out, kv_out, pl.BlockSpec((1, LANES), lambda j: (0, j))],
        out_shape=[jax.ShapeDtypeStruct((s, npair * LANES), BF16), kv_shape, kv_shape,
                   jax.ShapeDtypeStruct((1, npair * LANES), F32)],
        scratch_shapes=[pltpu.VMEM((s, LANES), F32)] * 5,
        compiler_params=_params("arbitrary" if gqa else "parallel"),
        name=name,
    )(z, z, z, do, o, lse, lse, m_init)


def _adamw_math(w, g, m, v):
    m = ADAM_B1 * m + (1.0 - ADAM_B1) * g
    v = ADAM_B2 * v + (1.0 - ADAM_B2) * (g * g)
    m_hat = m / (1.0 - ADAM_B1 ** ADAM_STEP)
    v_hat = v / (1.0 - ADAM_B2 ** ADAM_STEP)
    delta = -ADAM_LR * (m_hat / (jnp.sqrt(v_hat) + ADAM_EPS) + ADAM_WD * w)
    return delta, m, v


def _adamw(w, g, m, v, name):
    rows, cols = w.shape
    tr = min(rows, 256)

    def body(w_ref, g_ref, m_ref, v_ref, d_ref, nm_ref, nv_ref):
        d_ref[...], nm_ref[...], nv_ref[...] = _adamw_math(w_ref[...], g_ref[...], m_ref[...], v_ref[...])

    spec = pl.BlockSpec((tr, cols), lambda i: (i, 0))
    return pl.pallas_call(
        body,
        grid=(rows // tr,),
        in_specs=[spec] * 4,
        out_specs=[spec] * 3,
        out_shape=[jax.ShapeDtypeStruct((rows, cols), F32)] * 3,
        compiler_params=_params("parallel"),
        name=name,
    )(w, g, m, v)


def _small_sum_adamw(gathered, w, m, v, name):
    _, rows, cols = gathered.shape

    def body(ga_ref, w_ref, m_ref, v_ref, g_ref, d_ref, nm_ref, nv_ref):
        g = ga_ref[0]
        for i in range(1, N_DEV):
            g = g + ga_ref[i]
        g_ref[...] = g
        d_ref[...], nm_ref[...], nv_ref[...] = _adamw_math(w_ref[...], g, m_ref[...], v_ref[...])

    return pl.pallas_call(
        body,
        out_shape=[jax.ShapeDtypeStruct((rows, cols), F32)] * 4,
        name=name,
    )(gathered, w, m, v)


def _pair_sum(g4, r1, pos, name):
    _, _, rows, cols = g4.shape
    tr = min(rows, 512)

    def body(pos_ref, g_ref, r_ref, o_ref):
        o_ref[...] = (g_ref[...].astype(F32) + r_ref[...].astype(F32)).astype(BF16)

    return pl.pallas_call(
        body,
        grid_spec=pltpu.PrefetchScalarGridSpec(
            num_scalar_prefetch=1,
            grid=(4, rows // tr),
            in_specs=[pl.BlockSpec((None, None, tr, cols), lambda i, j, p: (i, p[1], j, 0)),
                      pl.BlockSpec((None, tr, cols), lambda i, j, p: (i, j, 0))],
            out_specs=pl.BlockSpec((None, tr, cols), lambda i, j, p: (i, j, 0)),
        ),
        out_shape=jax.ShapeDtypeStruct((4, rows, cols), BF16),
        compiler_params=_params("parallel", "parallel"),
        name=name,
    )(pos, g4, r1)


def _final_sum(p, r2, pos, name):
    _, rows, cols = p.shape
    tr = min(rows, 512)

    def body(pos_ref, p_ref, r_ref, o_ref):
        o_ref[...] = ((p_ref[...].astype(F32) + r_ref[0].astype(F32)) + r_ref[1].astype(F32)) + r_ref[2].astype(F32)

    return pl.pallas_call(
        body,
        grid_spec=pltpu.PrefetchScalarGridSpec(
            num_scalar_prefetch=1,
            grid=(rows // tr,),
            in_specs=[pl.BlockSpec((None, tr, cols), lambda j, q: (q[0], j, 0)),
                      pl.BlockSpec((3, tr, cols), lambda j, q: (0, j, 0))],
            out_specs=pl.BlockSpec((tr, cols), lambda j, q: (j, 0)),
        ),
        out_shape=jax.ShapeDtypeStruct((rows, cols), F32),
        compiler_params=_params("parallel"),
        name=name,
    )(pos, p, r2)


def _place():
    return lax.axis_index("x"), lax.axis_index("y"), lax.axis_index("c")


def _all_gather(shards, name):
    na = len(shards)
    hbm = pl.BlockSpec(memory_space=pl.ANY)

    def body(*refs):
        ins, outs = refs[:na], refs[na:2 * na]
        send_sems, recv_sems, local_sems = refs[2 * na:]
        x, y, c = _place()
        me, sibling = (x, y, c), (x, y, 1 - c)
        chips = [(1 - x, y), (x, 1 - y), (1 - x, 1 - y)]

        def rows(a, px, py, pc):
            m = ins[a].shape[0]
            return outs[a].at[pl.ds((4 * px + 2 * py + pc) * m, m), :]

        def copy(a, k, block, to, src=None):
            return pltpu.make_async_remote_copy(
                src_ref=rows(a, *block) if src is None else src, dst_ref=rows(a, *block),
                send_sem=send_sems.at[a, k], recv_sem=recv_sems.at[a, k], device_id=to, device_id_type=MESH)

        mine = [pltpu.make_async_copy(ins[a], rows(a, *me), local_sems.at[a]) for a in range(na)]
        for cp in mine:
            cp.start()
        first = []
        for a in range(na):
            first.append(copy(a, 0, me, sibling, src=ins[a]))
            first += [copy(a, 1 + j, me, (*chip, c), src=ins[a]) for j, chip in enumerate(chips)]
        for cp in first:
            cp.start()
        passed = []
        for j, chip in enumerate(chips):
            for a in range(na):
                copy(a, 1 + j, (*chip, c), me).wait_recv()
                cp = copy(a, 4 + j, (*chip, c), sibling)
                cp.start()
                passed.append(cp)
        for a in range(na):
            copy(a, 0, sibling, me).wait_recv()
            for j, chip in enumerate(chips):
                copy(a, 4 + j, (*chip, 1 - c), me).wait_recv()
        for cp in first + passed:
            cp.wait_send()
        for cp in mine:
            cp.wait()

    return pl.pallas_call(
        body,
        out_shape=[jax.ShapeDtypeStruct((N_DEV * t.shape[0], t.shape[1]), t.dtype) for t in shards],
        in_specs=[hbm] * na,
        out_specs=[hbm] * na,
        scratch_shapes=[pltpu.SemaphoreType.DMA((na, 7)), pltpu.SemaphoreType.DMA((na, 7)),
                        pltpu.SemaphoreType.DMA((na,))],
        name=name,
    )(*shards)


def _sibling_exchange(grads, name):
    na = len(grads)
    hbm = pl.BlockSpec(memory_space=pl.ANY)

    def body(*refs):
        ins, outs = refs[:na], refs[na:2 * na]
        send_sems, recv_sems = refs[2 * na:]
        x, y, c = _place()
        copies = []
        for a in range(na):
            for chip in range(4):
                copies.append(pltpu.make_async_remote_copy(
                    src_ref=ins[a].at[chip, 1 - c], dst_ref=outs[a].at[chip],
                    send_sem=send_sems.at[a, chip], recv_sem=recv_sems.at[a, chip],
                    device_id=(x, y, 1 - c), device_id_type=MESH))
        for cp in copies:
            cp.start()
        for cp in copies:
            cp.wait()

    return pl.pallas_call(
        body,
        out_shape=[jax.ShapeDtypeStruct((4,) + t.shape[2:], t.dtype) for t in grads],
        in_specs=[hbm] * na,
        out_specs=[hbm] * na,
        scratch_shapes=[pltpu.SemaphoreType.DMA((na, 4)), pltpu.SemaphoreType.DMA((na, 4))],
        name=name,
    )(*grads)


def _chip_exchange(partials, name):
    na = len(partials)
    hbm = pl.BlockSpec(memory_space=pl.ANY)

    def body(*refs):
        ins, outs = refs[:na], refs[na:2 * na]
        send_sems, recv_sems = refs[2 * na:]
        x, y, c = _place()
        chips = [(1 - x, y), (x, 1 - y), (1 - x, 1 - y)]
        copies = []
        for a in range(na):
            for k, (cx, cy) in enumerate(chips):
                copies.append(pltpu.make_async_remote_copy(
                    src_ref=ins[a].at[2 * cx + cy], dst_ref=outs[a].at[k],
                    send_sem=send_sems.at[a, k], recv_sem=recv_sems.at[a, k],
                    device_id=(cx, cy, c), device_id_type=MESH))
        for cp in copies:
            cp.start()
        for cp in copies:
            cp.wait()

    return pl.pallas_call(
        body,
        out_shape=[jax.ShapeDtypeStruct((3,) + t.shape[1:], t.dtype) for t in partials],
        in_specs=[hbm] * na,
        out_specs=[hbm] * na,
        scratch_shapes=[pltpu.SemaphoreType.DMA((na, 3)), pltpu.SemaphoreType.DMA((na, 3))],
        name=name,
    )(*partials)


def _pad_rows(t, rows):
    return jnp.pad(t, ((0, rows - t.shape[0]), (0, D_MODEL - t.shape[1])))


def _pack_small(g_mix, g_group, g_mlp, g_final, conv, sinks):
    return jnp.concatenate([
        _pad_rows(g_mix, 8), _pad_rows(g_group, 8), _pad_rows(g_mlp, 8), _pad_rows(g_final.reshape(1, D_MODEL), 8),
        _pad_rows(conv.reshape(DEPTH * 3, CONV_CH), 8), _pad_rows(sinks.reshape(1, DEPTH * 6), 8)], axis=0)


def _unpack_small(slab):
    return (slab[0:2], slab[8:10], slab[16:18], slab[24], slab[32:38, :CONV_CH].reshape(DEPTH, 3, CONV_CH),
            slab[40, :DEPTH * 6].reshape(DEPTH, 2, 3))


def kernel(x, w_in, conv_w, sinks, g_mix, g_group, w_o, g_mlp, w_ff_in, w_ff_out, g_final, loss_target, m_w_in, m_conv_w, m_sinks, m_g_mix, m_g_group, m_w_o, m_g_mlp, m_w_ff_in, m_w_ff_out, m_g_final, v_w_in, v_conv_w, v_sinks, v_g_mix, v_g_group, v_w_o, v_g_mlp, v_w_ff_in, v_w_ff_out, v_g_final):
    ax, ay, ac = _place()
    chip = 2 * ax + ay
    dev = 4 * ax + 2 * ay + ac
    pos = jnp.stack([chip, ac]).astype(jnp.int32)

    x0 = x.reshape(SEQ, D_MODEL)
    target = loss_target.reshape(SEQ, D_MODEL)

    shards = []
    for l in range(DEPTH):
        shards += [w_in[l].T.astype(BF16), w_o[l].astype(BF16), w_ff_in[l].T.astype(BF16), w_ff_out[l].astype(BF16)]
    conv_tile = jnp.pad(conv_w.reshape(DEPTH * 3, CONV_CH // N_DEV), ((0, 2), (0, LANES - CONV_CH // N_DEV)))
    *full, conv_all = _all_gather(shards + [conv_tile], "gather_weights")
    conv_full = conv_all.reshape(N_DEV, 8, LANES)[:, :DEPTH * 3, :CONV_CH // N_DEV]
    conv_full = conv_full.transpose(1, 0, 2).reshape(DEPTH, 3, CONV_CH)

    loss_slab, dx, grads, small = _local_step(x0, target, full, conv_full, sinks, g_mix, g_group, g_mlp, g_final)
    loss = lax.psum(loss_slab[0, 0], ("x", "y", "c"))
    return _finish(loss, dx, grads, small, pos, dev, w_in, conv_w, sinks, g_mix, g_group, w_o, g_mlp, w_ff_in, w_ff_out, g_final, m_w_in, m_conv_w, m_sinks, m_g_mix, m_g_group, m_w_o, m_g_mlp, m_w_ff_in, m_w_ff_out, m_g_final, v_w_in, v_conv_w, v_sinks, v_g_mix, v_g_group, v_w_o, v_g_mlp, v_w_ff_in, v_w_ff_out, v_g_final)


def _local_step(x0, target, full, conv_full, sinks, g_mix, g_group, g_mlp, g_final):
    sink_lanes = jnp.repeat(sinks.reshape(DEPTH, 6), HEAD_DIM, axis=1)
    no_sink = jnp.full((1, A_WIDTH), NEG_BIG, F32)

    saved = []
    xc = x0
    for l in range(DEPTH):
        wt_in, wo, wt_ff1, w_ff2 = full[4 * l:4 * l + 4]
        z, h = _norm_mm(xc, g_mix[l:l + 1], wt_in, False, f"in_proj_{l}")
        ya, lse_a = _attn_fwd(z, no_sink, 0.0, QA_BLK, KA_BLK, VA_BLK, DILATED_PATTERNS, A_MAX_DIST, False,
                              f"dilated_attn_{l}")
        yb = _conv_fwd(z, conv_full[l], f"conv_{l}")
        sink_l = sink_lanes[l:l + 1]
        yc, lse_c = _attn_fwd(z, sink_l, 1.0, QC_BLK, KC_BLK, VC_BLK, (1,), C_MAX_DIST, True, f"window_attn_{l}")
        y, x1 = _mix_out(ya, yb, yc, g_group[l:l + 1], wo, xc, f"mix_out_{l}")
        a, h2 = _norm_mm(x1, g_mlp[l:l + 1], wt_ff1, True, f"ff_in_{l}")
        x2 = _mm_res(a, w_ff2, x1, f"ff_out_{l}")
        saved.append((xc, z, h, ya, lse_a, yb, yc, lse_c, sink_l, y, x1, a, h2))
        xc = x2

    loss_slab, dx, dxb, dg_final = _loss_head(xc, g_final.reshape(1, D_MODEL), target, "loss_head")

    grads = [None] * (4 * DEPTH)
    dg_mix, dg_group, dg_mlp, dconv, dsinks = [None] * DEPTH, [None] * DEPTH, [None] * DEPTH, [None] * DEPTH, [None] * DEPTH
    for l in reversed(range(DEPTH)):
        wt_in, wo, wt_ff1, w_ff2 = full[4 * l:4 * l + 4]
        xin, z, h, ya, lse_a, yb, yc, lse_c, sink_l, y, x1, a, h2 = saved[l]
        du = _mlp_bwd_act(dxb, w_ff2, a, f"ff_out_bwd_{l}")
        grads[4 * l + 3] = _mm_tn(a, dxb, f"grad_w_ff_out_{l}")
        grads[4 * l + 2] = _mm_tn(du, h2, f"grad_w_ff_in_{l}")
        dx1, dx1b, dg_mlp[l] = _mm_nn_normbwd(du, wt_ff1, x1, dx, g_mlp[l:l + 1], f"ff_in_bwd_{l}")
        grads[4 * l + 1] = _mm_tn(y, dx1b, f"grad_w_o_{l}")
        dya, dyb, dyc, dg_group[l] = _mix_bwd(dx1b, wo, ya, yb, yc, g_group[l:l + 1], f"mix_out_bwd_{l}")
        dqa, dka, dva, _ = _attn_bwd(z, dya, ya, lse_a, no_sink, QA_BLK, KA_BLK, VA_BLK, DILATED_PATTERNS,
                                     A_MAX_DIST, False, f"dilated_attn_bwd_{l}")
        dgb, dgc, dxb, dcw = _conv_bwd(z, conv_full[l], dyb, f"conv_bwd_{l}")
        dqc, dkc, dvc, dsink = _attn_bwd(z, dyc, yc, lse_c, sink_l, QC_BLK, KC_BLK, VC_BLK, (1,), C_MAX_DIST, True,
                                         f"window_attn_bwd_{l}")
        dz = jnp.concatenate([dqa, dka, dva, dgb, dgc, dxb, dqc, dkc, dvc], axis=1)
        grads[4 * l] = _mm_tn(dz, h, f"grad_w_in_{l}")
        dx, dxb, dg_mix[l] = _mm_nn_normbwd(dz, wt_in, xin, dx1, g_mix[l:l + 1], f"in_proj_bwd_{l}")
        dconv[l] = dcw[:3]
        dsinks[l] = dsink[0, ::HEAD_DIM]
    small = _pack_small(jnp.concatenate(dg_mix), jnp.concatenate(dg_group), jnp.concatenate(dg_mlp),
                        dg_final, jnp.stack(dconv), jnp.stack(dsinks))
    return loss_slab, dx, grads, small


def _finish(loss, dx, grads, small, pos, dev, w_in, conv_w, sinks, g_mix, g_group, w_o, g_mlp, w_ff_in, w_ff_out, g_final, m_w_in, m_conv_w, m_sinks, m_g_mix, m_g_group, m_w_o, m_g_mlp, m_w_ff_in, m_w_ff_out, m_g_final, v_w_in, v_conv_w, v_sinks, v_g_mix, v_g_group, v_w_o, v_g_mlp, v_w_ff_in, v_w_ff_out, v_g_final):
    grad_x = dx.reshape(1, SEQ, D_MODEL)

    g4 = [g.reshape(4, 2, g.shape[0] // N_DEV, D_MODEL) for g in grads]
    r1 = _sibling_exchange(g4, "grad_sibling_exchange")
    partial = [_pair_sum(g4[i], r1[i], pos, f"grad_pair_sum_{i}") for i in range(4 * DEPTH)]
    r2 = _chip_exchange(partial, "grad_chip_exchange")
    gsum = [_final_sum(partial[i], r2[i], pos, f"grad_final_sum_{i}") for i in range(4 * DEPTH)]
    grad_w_in = jnp.stack([gsum[4 * l].T for l in range(DEPTH)])
    grad_w_o = jnp.stack([gsum[4 * l + 1] for l in range(DEPTH)])
    grad_w_ff_in = jnp.stack([gsum[4 * l + 2].T for l in range(DEPTH)])
    grad_w_ff_out = jnp.stack([gsum[4 * l + 3] for l in range(DEPTH)])

    (small_all,) = _all_gather([small], "gather_small_grads")
    zeros_conv = jnp.zeros((DEPTH, 3, CONV_CH), F32)
    sw = _pack_small(g_mix, g_group, g_mlp, g_final, zeros_conv, sinks)
    sm = _pack_small(m_g_mix, m_g_group, m_g_mlp, m_g_final, zeros_conv, m_sinks)
    sv = _pack_small(v_g_mix, v_g_group, v_g_mlp, v_g_final, zeros_conv, v_sinks)
    sg, sd, snm, snv = _small_sum_adamw(small_all.reshape(N_DEV, SMALL_ROWS, D_MODEL), sw, sm, sv, "small_adamw")
    grad_g_mix, grad_g_group, grad_g_mlp, grad_g_final, conv_grad_full, grad_sinks = _unpack_small(sg)
    delta_g_mix, delta_g_group, delta_g_mlp, delta_g_final, _, delta_sinks = _unpack_small(sd)
    new_m_g_mix, new_m_g_group, new_m_g_mlp, new_m_g_final, _, new_m_sinks = _unpack_small(snm)
    new_v_g_mix, new_v_g_group, new_v_g_mlp, new_v_g_final, _, new_v_sinks = _unpack_small(snv)
    cs = CONV_CH // N_DEV
    grad_conv_w = lax.dynamic_slice_in_dim(conv_grad_full, dev * cs, cs, axis=2)

    def tile_of(t):
        return jnp.pad(t.reshape(1, DEPTH * 3 * cs), ((0, 7), (0, 256 - DEPTH * 3 * cs)))

    cd, cm, cv = _adamw(tile_of(conv_w), tile_of(grad_conv_w), tile_of(m_conv_w), tile_of(v_conv_w), "conv_adamw")
    untile = lambda t: t[0, :DEPTH * 3 * cs].reshape(DEPTH, 3, cs)
    delta_conv_w, new_m_conv_w, new_v_conv_w = untile(cd), untile(cm), untile(cv)

    def big(w, g, m, v, name):
        shp = w.shape
        flat = lambda t: t.reshape(shp[0] * shp[1], shp[2])
        return [t.reshape(shp) for t in _adamw(flat(w), flat(g), flat(m), flat(v), name)]

    delta_w_in, new_m_w_in, new_v_w_in = big(w_in, grad_w_in, m_w_in, v_w_in, "adamw_w_in")
    delta_w_o, new_m_w_o, new_v_w_o = big(w_o, grad_w_o, m_w_o, v_w_o, "adamw_w_o")
    delta_w_ff_in, new_m_w_ff_in, new_v_w_ff_in = big(w_ff_in, grad_w_ff_in, m_w_ff_in, v_w_ff_in, "adamw_w_ff_in")
    delta_w_ff_out, new_m_w_ff_out, new_v_w_ff_out = big(w_ff_out, grad_w_ff_out, m_w_ff_out, v_w_ff_out, "adamw_w_ff_out")

    return (loss, grad_x, grad_w_in, grad_conv_w, grad_sinks, grad_g_mix, grad_g_group, grad_w_o, grad_g_mlp,
            grad_w_ff_in, grad_w_ff_out, grad_g_final,
            delta_w_in, delta_conv_w, delta_sinks, delta_g_mix, delta_g_group, delta_w_o, delta_g_mlp,
            delta_w_ff_in, delta_w_ff_out, delta_g_final,
            new_m_w_in, new_m_conv_w, new_m_sinks, new_m_g_mix, new_m_g_group, new_m_w_o, new_m_g_mlp,
            new_m_w_ff_in, new_m_w_ff_out, new_m_g_final,
            new_v_w_in, new_v_conv_w, new_v_sinks, new_v_g_mix, new_v_g_group, new_v_w_o, new_v_g_mlp,
            new_v_w_ff_in, new_v_w_ff_out, new_v_g_final)
```

```python
from typing import Callable, NamedTuple

import jax
import jax.numpy as jnp
from jax import lax
from jax.experimental import pallas as pl
from jax.experimental.pallas import tpu as pltpu

F32 = jnp.float32
BF16 = jnp.bfloat16
MESH = pl.DeviceIdType.MESH

N_DEV = 8
SEQ = 4096
D_MODEL = 1024
DEPTH = 2
HEAD_DIM = 64
LANES = 128
A_WIDTH = 384
CONV_CH = 256
C_WIDTH = 384
KV_WIDTH = 128
IN_WIDTH = 2560
D_FF = 4096
BLOCK = 128
DILATED_PATTERNS = (1, 4, 16)
A_MAX_DIST = 128
C_MAX_DIST = 127
EPS = 1e-6
SCALE = HEAD_DIM ** -0.5
NEG_BIG = -1e30

QA_BLK, KA_BLK, VA_BLK = 0, 3, 6
GB_BLK, GC_BLK, XB_BLK = 9, 11, 13
QC_BLK, KC_BLK, VC_BLK = 15, 18, 19

ADAM_LR = 0.001
ADAM_B1 = 0.9
ADAM_B2 = 0.999
ADAM_EPS = 1e-08
ADAM_WD = 0.01
ADAM_STEP = 10

VMEM_LIMIT = 56 * 1024 * 1024
ROW_TILE = 512
COL_CHUNK = 512
SMALL_ROWS = 48


def _dot_nn(a, b):
    return lax.dot_general(a, b, (((1,), (0,)), ((), ())), preferred_element_type=F32)


def _dot_nt(a, b):
    return lax.dot_general(a, b, (((1,), (1,)), ((), ())), preferred_element_type=F32)


def _dot_tn(a, b):
    return lax.dot_general(a, b, (((0,), (0,)), ((), ())), preferred_element_type=F32)


def _params(*sem):
    return pltpu.CompilerParams(dimension_semantics=sem, vmem_limit_bytes=VMEM_LIMIT)


def _rms_scale(t):
    return lax.rsqrt(jnp.mean(t * t, axis=-1, keepdims=True) + EPS)


def _rms_bwd(n, r, dn):
    return r * (dn - n * jnp.mean(dn * n, axis=-1, keepdims=True))


class _Comm(NamedTuple):
    arrays: tuple
    out_shape: tuple
    sems: tuple
    start: Callable
    finish: Callable


def _join(*comms):
    comms = [c for c in comms if c is not None]
    if not comms:
        return None

    def run(which):
        def f(ins, outs, sems):
            i = o = s = 0
            for c in comms:
                ni, no, ns = len(c.arrays), len(c.out_shape), len(c.sems)
                getattr(c, which)(ins[i:i + ni], outs[o:o + no], sems[s:s + ns])
                i, o, s = i + ni, o + no, s + ns
        return f

    return _Comm(sum((tuple(c.arrays) for c in comms), ()), sum((tuple(c.out_shape) for c in comms), ()),
                 sum((tuple(c.sems) for c in comms), ()), run("start"), run("finish"))


def _call(body, grid, in_specs, out_specs, out_shape, operands, name, scratch_shapes=(), comm=None):
    n_in, n_out, n_scr = len(in_specs), len(out_shape), len(scratch_shapes)
    if comm is None:
        res = pl.pallas_call(body, grid=grid, in_specs=list(in_specs), out_specs=list(out_specs),
                             out_shape=list(out_shape), scratch_shapes=list(scratch_shapes),
                             compiler_params=_params("arbitrary"), name=name)(*operands)
        return list(res), []
    c_in, c_out = len(comm.arrays), len(comm.out_shape)
    hbm = pl.BlockSpec(memory_space=pl.ANY)
    last = grid[0] - 1

    def carried(*refs):
        ins, cins = refs[:n_in], refs[n_in:n_in + c_in]
        o0 = n_in + c_in
        outs, couts = refs[o0:o0 + n_out], refs[o0 + n_out:o0 + n_out + c_out]
        s0 = o0 + n_out + c_out
        scr, sems = refs[s0:s0 + n_scr], refs[s0 + n_scr:]
        pl.when(pl.program_id(0) == 0)(lambda: comm.start(cins, couts, sems))
        body(*ins, *outs, *scr)
        pl.when(pl.program_id(0) == last)(lambda: comm.finish(cins, couts, sems))

    res = pl.pallas_call(carried, grid=grid, in_specs=list(in_specs) + [hbm] * c_in,
                         out_specs=list(out_specs) + [hbm] * c_out, out_shape=list(out_shape) + list(comm.out_shape),
                         scratch_shapes=list(scratch_shapes) + list(comm.sems),
                         compiler_params=_params("arbitrary"), name=name)(*operands, *comm.arrays)
    return list(res[:n_out]), list(res[n_out:])


def _comm_only(comm, name):
    hbm = pl.BlockSpec(memory_space=pl.ANY)
    c_in, c_out = len(comm.arrays), len(comm.out_shape)

    def body(*refs):
        ins, outs, sems = refs[:c_in], refs[c_in:c_in + c_out], refs[c_in + c_out:]
        comm.start(ins, outs, sems)
        comm.finish(ins, outs, sems)

    return pl.pallas_call(body, in_specs=[hbm] * c_in, out_specs=[hbm] * c_out, out_shape=list(comm.out_shape),
                          scratch_shapes=list(comm.sems), name=name)(*comm.arrays)


def _norm_mm(x, g, wt, relu2, name, comm=None):
    s, d = x.shape
    n = wt.shape[0]
    tm = ROW_TILE

    def body(x_ref, g_ref, w_ref, o_ref, h_ref):
        xx = x_ref[...]
        h = ((xx * _rms_scale(xx)) * g_ref[...]).astype(BF16)
        h_ref[...] = h
        for n0 in range(0, n, COL_CHUNK):
            zc = _dot_nt(h, w_ref[n0:n0 + COL_CHUNK, :])
            if relu2:
                zc = jnp.square(jnp.maximum(zc, 0.0)).astype(BF16)
            o_ref[:, n0:n0 + COL_CHUNK] = zc

    return _call(
        body,
        grid=(s // tm,),
        in_specs=[pl.BlockSpec((tm, d), lambda i: (i, 0)),
                  pl.BlockSpec((1, d), lambda i: (0, 0)),
                  pl.BlockSpec((n, d), lambda i: (0, 0))],
        out_specs=[pl.BlockSpec((tm, n), lambda i: (i, 0)),
                   pl.BlockSpec((tm, d), lambda i: (i, 0))],
        out_shape=[jax.ShapeDtypeStruct((s, n), BF16 if relu2 else F32), jax.ShapeDtypeStruct((s, d), BF16)],
        operands=(x, g, wt), name=name, comm=comm)


def _mm_res(a, w2, x1, name, comm=None):
    s, f = a.shape
    d = w2.shape[1]
    tm = ROW_TILE

    def body(a_ref, w_ref, x_ref, o_ref):
        o_ref[...] = x_ref[...] + _dot_nn(a_ref[...], w_ref[...])

    return _call(
        body,
        grid=(s // tm,),
        in_specs=[pl.BlockSpec((tm, f), lambda i: (i, 0)),
                  pl.BlockSpec((f, d), lambda i: (0, 0)),
                  pl.BlockSpec((tm, d), lambda i: (i, 0))],
        out_specs=[pl.BlockSpec((tm, d), lambda i: (i, 0))],
        out_shape=[jax.ShapeDtypeStruct((s, d), F32)],
        operands=(a, w2, x1), name=name, comm=comm)


def _mix_out(ya, yb, yc, gg, wo, x0, name, comm=None):
    s = ya.shape[0]
    d = wo.shape[1]
    tm = ROW_TILE

    def body(ya_ref, yb_ref, yc_ref, g_ref, w_ref, x_ref, y_ref, o_ref):
        parts = []
        for ref in (ya_ref, yb_ref, yc_ref):
            t = ref[...]
            parts.append(t * _rms_scale(t))
        y = (jnp.concatenate(parts, axis=1) * g_ref[...]).astype(BF16)
        y_ref[...] = y
        o_ref[...] = x_ref[...] + _dot_nn(y, w_ref[...])

    return _call(
        body,
        grid=(s // tm,),
        in_specs=[pl.BlockSpec((tm, A_WIDTH), lambda i: (i, 0)),
                  pl.BlockSpec((tm, CONV_CH), lambda i: (i, 0)),
                  pl.BlockSpec((tm, C_WIDTH), lambda i: (i, 0)),
                  pl.BlockSpec((1, d), lambda i: (0, 0)),
                  pl.BlockSpec((d, d), lambda i: (0, 0)),
                  pl.BlockSpec((tm, d), lambda i: (i, 0))],
        out_specs=[pl.BlockSpec((tm, d), lambda i: (i, 0)),
                   pl.BlockSpec((tm, d), lambda i: (i, 0))],
        out_shape=[jax.ShapeDtypeStruct((s, d), BF16), jax.ShapeDtypeStruct((s, d), F32)],
        operands=(ya, yb, yc, gg, wo, x0), name=name, comm=comm)


def _loss_head(x, g, target, name):
    s, d = x.shape
    tm = ROW_TILE

    def body(x_ref, g_ref, t_ref, loss_ref, dx_ref, dxb_ref, dg_ref):
        @pl.when(pl.program_id(0) == 0)
        def _():
            loss_ref[...] = jnp.zeros_like(loss_ref)
            dg_ref[...] = jnp.zeros_like(dg_ref)

        xx = x_ref[...]
        r = _rms_scale(xx)
        n = xx * r
        gv = g_ref[...]
        err = n * gv - t_ref[...]
        per_tok = jnp.sum(err * err, axis=1, keepdims=True) * (1.0 / d)
        loss_ref[...] += 0.5 * jnp.sum(per_tok, axis=0, keepdims=True)
        dout = err * (1.0 / d)
        dg_ref[...] += jnp.sum(dout * n, axis=0, keepdims=True)
        dx = _rms_bwd(n, r, dout * gv)
        dx_ref[...] = dx
        dxb_ref[...] = dx.astype(BF16)

    return pl.pallas_call(
        body,
        grid=(s // tm,),
        in_specs=[pl.BlockSpec((tm, d), lambda i: (i, 0)),
                  pl.BlockSpec((1, d), lambda i: (0, 0)),
                  pl.BlockSpec((tm, d), lambda i: (i, 0))],
        out_specs=[pl.BlockSpec((8, LANES), lambda i: (0, 0)),
                   pl.BlockSpec((tm, d), lambda i: (i, 0)),
                   pl.BlockSpec((tm, d), lambda i: (i, 0)),
                   pl.BlockSpec((1, d), lambda i: (0, 0))],
        out_shape=[jax.ShapeDtypeStruct((8, LANES), F32), jax.ShapeDtypeStruct((s, d), F32),
                   jax.ShapeDtypeStruct((s, d), BF16), jax.ShapeDtypeStruct((1, d), F32)],
        compiler_params=_params("arbitrary"),
        name=name,
    )(x, g, target)


def _mlp_bwd_act(dxb, w2, a, name, comm=None):
    s, d = dxb.shape
    f = w2.shape[0]
    tm = ROW_TILE

    def body(dx_ref, w_ref, a_ref, du_ref):
        dx = dx_ref[...]
        for n0 in range(0, f, COL_CHUNK):
            da = _dot_nt(dx, w_ref[n0:n0 + COL_CHUNK, :])
            rl = jnp.sqrt(a_ref[:, n0:n0 + COL_CHUNK].astype(F32))
            du_ref[:, n0:n0 + COL_CHUNK] = (da * (2.0 * rl)).astype(BF16)

    return _call(
        body,
        grid=(s // tm,),
        in_specs=[pl.BlockSpec((tm, d), lambda i: (i, 0)),
                  pl.BlockSpec((f, d), lambda i: (0, 0)),
                  pl.BlockSpec((tm, f), lambda i: (i, 0))],
        out_specs=[pl.BlockSpec((tm, f), lambda i: (i, 0))],
        out_shape=[jax.ShapeDtypeStruct((s, f), BF16)],
        operands=(dxb, w2, a), name=name, comm=comm)


def _mm_tn(a, b, name, comm=None):
    s, n = a.shape
    d = b.shape[1]
    tn = 512

    def body(a_ref, b_ref, o_ref, acc):
        for k0 in range(0, s, ROW_TILE):
            part = _dot_tn(a_ref[k0:k0 + ROW_TILE, :], b_ref[k0:k0 + ROW_TILE, :])
            if k0 == 0:
                acc[...] = part
            else:
                acc[...] += part
        o_ref[...] = acc[...].astype(BF16)

    return _call(
        body,
        grid=(n // tn,),
        in_specs=[pl.BlockSpec((s, tn), lambda j: (0, j)),
                  pl.BlockSpec((s, d), lambda j: (0, 0))],
        out_specs=[pl.BlockSpec((tn, d), lambda j: (j, 0))],
        out_shape=[jax.ShapeDtypeStruct((n, d), BF16)],
        operands=(a, b), name=name, scratch_shapes=[pltpu.VMEM((tn, d), F32)], comm=comm)


def _mm_nn_normbwd(dact, wt, x, dres, g, name, comm=None):
    s, kdim = dact.shape
    d = wt.shape[1]
    tm = ROW_TILE

    def body(a_ref, w_ref, x_ref, r_ref, g_ref, o_ref, ob_ref, dg_ref):
        @pl.when(pl.program_id(0) == 0)
        def _():
            dg_ref[...] = jnp.zeros_like(dg_ref)

        dh = _dot_nn(a_ref[...], w_ref[...])
        xx = x_ref[...]
        r = _rms_scale(xx)
        n = xx * r
        dg_ref[...] += jnp.sum(dh * n, axis=0, keepdims=True)
        dx = r_ref[...] + _rms_bwd(n, r, dh * g_ref[...])
        o_ref[...] = dx
        ob_ref[...] = dx.astype(BF16)

    return _call(
        body,
        grid=(s // tm,),
        in_specs=[pl.BlockSpec((tm, kdim), lambda i: (i, 0)),
                  pl.BlockSpec((kdim, d), lambda i: (0, 0)),
                  pl.BlockSpec((tm, d), lambda i: (i, 0)),
                  pl.BlockSpec((tm, d), lambda i: (i, 0)),
                  pl.BlockSpec((1, d), lambda i: (0, 0))],
        out_specs=[pl.BlockSpec((tm, d), lambda i: (i, 0)),
                   pl.BlockSpec((tm, d), lambda i: (i, 0)),
                   pl.BlockSpec((1, d), lambda i: (0, 0))],
        out_shape=[jax.ShapeDtypeStruct((s, d), F32), jax.ShapeDtypeStruct((s, d), BF16),
                   jax.ShapeDtypeStruct((1, d), F32)],
        operands=(dact, wt, x, dres, g), name=name, comm=comm)


def _mix_bwd(dx1, wo, ya, yb, yc, gg, name, comm=None):
    s, d = dx1.shape
    tm = ROW_TILE
    widths = (A_WIDTH, CONV_CH, C_WIDTH)

    def body(dx_ref, w_ref, ya_ref, yb_ref, yc_ref, g_ref, da_ref, db_ref, dc_ref, dg_ref):
        @pl.when(pl.program_id(0) == 0)
        def _():
            dg_ref[...] = jnp.zeros_like(dg_ref)

        dy = _dot_nt(dx_ref[...], w_ref[...])
        gv = g_ref[...]
        off = 0
        dgs = []
        for ref, out, w in zip((ya_ref, yb_ref, yc_ref), (da_ref, db_ref, dc_ref), widths):
            t = ref[...]
            r = _rms_scale(t)
            n = t * r
            dyg = dy[:, off:off + w]
            dgs.append(jnp.sum(dyg * n, axis=0, keepdims=True))
            out[...] = _rms_bwd(n, r, dyg * gv[:, off:off + w])
            off += w
        dg_ref[...] += jnp.concatenate(dgs, axis=1)

    return _call(
        body,
        grid=(s // tm,),
        in_specs=[pl.BlockSpec((tm, d), lambda i: (i, 0)),
                  pl.BlockSpec((d, d), lambda i: (0, 0)),
                  pl.BlockSpec((tm, A_WIDTH), lambda i: (i, 0)),
                  pl.BlockSpec((tm, CONV_CH), lambda i: (i, 0)),
                  pl.BlockSpec((tm, C_WIDTH), lambda i: (i, 0)),
                  pl.BlockSpec((1, d), lambda i: (0, 0))],
        out_specs=[pl.BlockSpec((tm, A_WIDTH), lambda i: (i, 0)),
                   pl.BlockSpec((tm, CONV_CH), lambda i: (i, 0)),
                   pl.BlockSpec((tm, C_WIDTH), lambda i: (i, 0)),
                   pl.BlockSpec((1, d), lambda i: (0, 0))],
        out_shape=[jax.ShapeDtypeStruct((s, A_WIDTH), F32), jax.ShapeDtypeStruct((s, CONV_CH), F32),
                   jax.ShapeDtypeStruct((s, C_WIDTH), F32), jax.ShapeDtypeStruct((1, d), F32)],
        operands=(dx1, wo, ya, yb, yc, gg), name=name, comm=comm)


CONV_CHUNK = 256
CONV_HALO = 8


def _conv_fwd(z, cw, name):
    s = z.shape[0]
    nch = s // CONV_CHUNK

    def body(gb_ref, gc_ref, xb_ref, w_ref, o_ref, us):
        us[pl.ds(0, CONV_HALO), :] = jnp.zeros((CONV_HALO, LANES), F32)
        us[pl.ds(CONV_HALO, s), :] = gc_ref[...] * xb_ref[...]
        w0, w1, w2 = w_ref[0:1, :], w_ref[1:2, :], w_ref[2:3, :]

        def chunk(c, carry):
            st = pl.multiple_of(c * CONV_CHUNK, CONV_CHUNK)
            ext = us[pl.ds(st, CONV_CHUNK + CONV_HALO), :]
            y = (w0 * ext[CONV_HALO - 2:CONV_HALO - 2 + CONV_CHUNK]
                 + w1 * ext[CONV_HALO - 1:CONV_HALO - 1 + CONV_CHUNK]
                 + w2 * ext[CONV_HALO:])
            o_ref[pl.ds(st, CONV_CHUNK), :] = gb_ref[pl.ds(st, CONV_CHUNK), :] * y
            return carry

        lax.fori_loop(0, nch, chunk, 0)

    col = lambda blk: pl.BlockSpec((s, LANES), lambda j, blk=blk: (0, blk + j))
    return pl.pallas_call(
        body,
        grid=(CONV_CH // LANES,),
        in_specs=[col(GB_BLK), col(GC_BLK), col(XB_BLK), pl.BlockSpec((3, LANES), lambda j: (0, j))],
        out_specs=pl.BlockSpec((s, LANES), lambda j: (0, j)),
        out_shape=jax.ShapeDtypeStruct((s, CONV_CH), F32),
        scratch_shapes=[pltpu.VMEM((s + CONV_HALO, LANES), F32)],
        compiler_params=_params("parallel"),
        name=name,
    )(z, z, z, cw)


def _conv_bwd(z, cw, dyb, name):
    s = z.shape[0]
    nch = s // CONV_CHUNK

    def body(gb_ref, gc_ref, xb_ref, w_ref, dy_ref, dgb_ref, dgc_ref, dxb_ref, dw_ref, us, ds_):
        us[pl.ds(0, CONV_HALO), :] = jnp.zeros((CONV_HALO, LANES), F32)
        us[pl.ds(CONV_HALO, s), :] = gc_ref[...] * xb_ref[...]
        ds_[pl.ds(s, CONV_HALO), :] = jnp.zeros((CONV_HALO, LANES), F32)
        ds_[pl.ds(0, s), :] = dy_ref[...] * gb_ref[...]
        w0, w1, w2 = w_ref[0:1, :], w_ref[1:2, :], w_ref[2:3, :]
        zero = jnp.zeros((1, LANES), F32)

        def chunk(c, carry):
            a0, a1, a2 = carry
            st = pl.multiple_of(c * CONV_CHUNK, CONV_CHUNK)
            rows = pl.ds(st, CONV_CHUNK)
            ext = us[pl.ds(st, CONV_CHUNK + CONV_HALO), :]
            um2 = ext[CONV_HALO - 2:CONV_HALO - 2 + CONV_CHUNK]
            um1 = ext[CONV_HALO - 1:CONV_HALO - 1 + CONV_CHUNK]
            u0 = ext[CONV_HALO:]
            dext = ds_[pl.ds(st, CONV_CHUNK + CONV_HALO), :]
            dc0 = dext[:CONV_CHUNK]
            du = w2 * dc0 + w1 * dext[1:1 + CONV_CHUNK] + w0 * dext[2:2 + CONV_CHUNK]
            yconv = w0 * um2 + w1 * um1 + w2 * u0
            dgb_ref[rows, :] = (dy_ref[rows, :] * yconv).astype(BF16)
            dgc_ref[rows, :] = (du * xb_ref[rows, :]).astype(BF16)
            dxb_ref[rows, :] = (du * gc_ref[rows, :]).astype(BF16)
            a0 = a0 + jnp.sum(dc0 * um2, axis=0, keepdims=True)
            a1 = a1 + jnp.sum(dc0 * um1, axis=0, keepdims=True)
            a2 = a2 + jnp.sum(dc0 * u0, axis=0, keepdims=True)
            return a0, a1, a2

        a0, a1, a2 = lax.fori_loop(0, nch, chunk, (zero, zero, zero))
        dw_ref[...] = jnp.concatenate([a0, a1, a2, jnp.zeros((5, LANES), F32)], axis=0)

    col = lambda blk: pl.BlockSpec((s, LANES), lambda j, blk=blk: (0, blk + j))
    own = pl.BlockSpec((s, LANES), lambda j: (0, j))
    return pl.pallas_call(
        body,
        grid=(CONV_CH // LANES,),
        in_specs=[col(GB_BLK), col(GC_BLK), col(XB_BLK), pl.BlockSpec((3, LANES), lambda j: (0, j)), own],
        out_specs=[own, own, own, pl.BlockSpec((8, LANES), lambda j: (0, j))],
        out_shape=[jax.ShapeDtypeStruct((s, CONV_CH), BF16)] * 3 + [jax.ShapeDtypeStruct((8, CONV_CH), F32)],
        scratch_shapes=[pltpu.VMEM((s + CONV_HALO, LANES), F32), pltpu.VMEM((s + CONV_HALO, LANES), F32)],
        compiler_params=_params("parallel"),
        name=name,
    )(z, z, z, cw, dyb)


ATTN_ROWS = 512
ATTN_UNROLL = 4


def _band_rows(b, d, r):
    base = pl.multiple_of(b * (BLOCK * d), BLOCK)
    prev = jnp.maximum(base - BLOCK * d, 0)
    if d == 1:
        return pl.ds(base, BLOCK), pl.ds(pl.multiple_of(prev, BLOCK), BLOCK)
    return pl.ds(base + r, BLOCK, stride=d), pl.ds(prev + r, BLOCK, stride=d)


def _band_mask(b, max_dist):
    qi = lax.broadcasted_iota(jnp.int32, (BLOCK, 2 * BLOCK), 0)
    kj = lax.broadcasted_iota(jnp.int32, (BLOCK, 2 * BLOCK), 1)
    dist = BLOCK + qi - kj
    first_key = jnp.where(b > 0, 0, BLOCK)
    return (dist >= 0) & (dist <= max_dist) & (kj >= first_key)


def _lane_half():
    return (lax.broadcasted_iota(jnp.int32, (1, LANES), 1) >= HEAD_DIM).astype(jnp.int32)


def _kv_for_pair(t, pair):
    half = _lane_half()
    want = (pair + half) >> 1
    return jnp.where(want != half, pltpu.roll(t, HEAD_DIM, 1), t)


def _kv_grad_from_pair(t, pair):
    half = _lane_half()
    mine = ((pair + half) >> 1) == half
    other = ((pair + 1 - half) >> 1) == half
    fold = t + pltpu.roll(t, HEAD_DIM, 1)
    return jnp.where(mine & other, fold, jnp.where(mine, t, 0.0))


def _stack_heads(t, head0):
    zero = jnp.zeros_like(t)
    return jnp.concatenate([jnp.where(head0, t, zero), jnp.where(head0, zero, t)], axis=0)


def _unstack_heads(t, head0):
    return jnp.where(head0, t[:BLOCK], t[BLOCK:])


def _block_loops(s, patterns, unroll, one_block):
    for d in patterns:
        nb = (s // BLOCK) // d
        if d == 1:
            def trip(i, carry):
                for u in range(unroll):
                    one_block(i * unroll + u, 1, 0)
                return carry
            lax.fori_loop(0, nb // unroll, trip, 0)
        else:
            for r0 in range(0, d, unroll):
                def trip(b, carry, d=d, r0=r0):
                    for u in range(unroll):
                        one_block(b, d, r0 + u)
                    return carry
                lax.fori_loop(0, nb, trip, 0)


def _attn_fwd(z, m_init, l_init, q_blk, k_blk, v_blk, patterns, max_dist, gqa, name, comm=None):
    s = z.shape[0]
    npair = 3

    def body(q_ref, k_ref, v_ref, mi_ref, o_ref, lse_ref, m_scr, l_scr):
        pair = pl.program_id(0)
        head0 = lax.broadcasted_iota(jnp.int32, (1, LANES), 1) < HEAD_DIM

        def init(c, carry):
            rows = pl.ds(pl.multiple_of(c * ATTN_ROWS, ATTN_ROWS), ATTN_ROWS)
            m_scr[rows, :] = jnp.broadcast_to(mi_ref[...], (ATTN_ROWS, LANES))
            l_scr[rows, :] = jnp.full((ATTN_ROWS, LANES), l_init, F32)
            o_ref[rows, :] = jnp.zeros((ATTN_ROWS, LANES), F32)
            return carry

        lax.fori_loop(0, s // ATTN_ROWS, init, 0)

        def one_block(b, d, r):
            rq, rp = _band_rows(b, d, r)
            mask = _band_mask(b, max_dist)
            mask2 = jnp.concatenate([mask, mask], axis=0)
            q2 = _stack_heads((q_ref[rq, :] * SCALE).astype(BF16), head0)
            k2 = jnp.concatenate([k_ref[rp, :], k_ref[rq, :]], axis=0)
            v2 = jnp.concatenate([v_ref[rp, :], v_ref[rq, :]], axis=0)
            if gqa:
                k2 = _kv_for_pair(k2, pair)
                v2 = _kv_for_pair(v2, pair)
            sc = jnp.where(mask2, _dot_nt(q2, k2.astype(BF16)), -jnp.inf)
            mb = jnp.max(sc, axis=1, keepdims=True)
            p = jnp.exp(sc - mb)
            lb = jnp.sum(p, axis=1, keepdims=True)
            ob = _dot_nn(p.astype(BF16), v2.astype(BF16))
            m2 = _unstack_heads(jnp.broadcast_to(mb, (2 * BLOCK, LANES)), head0)
            l2 = _unstack_heads(jnp.broadcast_to(lb, (2 * BLOCK, LANES)), head0)
            o2 = _unstack_heads(ob, head0)
            m_old = m_scr[rq, :]
            m_new = jnp.maximum(m_old, m2)
            a_old = jnp.exp(m_old - m_new)
            a_blk = jnp.exp(m2 - m_new)
            o_ref[rq, :] = o_ref[rq, :] * a_old + o2 * a_blk
            l_scr[rq, :] = l_scr[rq, :] * a_old + l2 * a_blk
            m_scr[rq, :] = m_new

        _block_loops(s, patterns, ATTN_UNROLL, one_block)

        def fin(c, carry):
            rows = pl.ds(pl.multiple_of(c * ATTN_ROWS, ATTN_ROWS), ATTN_ROWS)
            l = l_scr[rows, :]
            o_ref[rows, :] = o_ref[rows, :] / l
            lse = m_scr[rows, :] + jnp.log(l)
            swapped = pltpu.roll(lse, HEAD_DIM, 1)
            lse_ref[rows, 0:LANES] = jnp.where(head0, lse, swapped)
            lse_ref[rows, LANES:2 * LANES] = jnp.where(head0, swapped, lse)
            return carry

        lax.fori_loop(0, s // ATTN_ROWS, fin, 0)

    kv = (lambda blk: pl.BlockSpec((s, LANES), lambda j, blk=blk: (0, blk))) if gqa else \
         (lambda blk: pl.BlockSpec((s, LANES), lambda j, blk=blk: (0, blk + j)))
    return _call(
        body,
        grid=(npair,),
        in_specs=[pl.BlockSpec((s, LANES), lambda j: (0, q_blk + j)), kv(k_blk), kv(v_blk),
                  pl.BlockSpec((1, LANES), lambda j: (0, j))],
        out_specs=[pl.BlockSpec((s, LANES), lambda j: (0, j)), pl.BlockSpec((s, 2 * LANES), lambda j: (0, j))],
        out_shape=[jax.ShapeDtypeStruct((s, npair * LANES), F32), jax.ShapeDtypeStruct((s, 2 * npair * LANES), F32)],
        operands=(z, z, z, m_init), name=name, scratch_shapes=[pltpu.VMEM((s, LANES), F32)] * 2, comm=comm)


def _attn_bwd(z, do, o, lse, m_init, q_blk, k_blk, v_blk, patterns, max_dist, gqa, name, comm=None):
    s = z.shape[0]
    npair = 3

    def body(q_ref, k_ref, v_ref, do_ref, o_ref, lse0_ref, lse1_ref, mi_ref,
             dq_ref, dk_ref, dv_ref, dm_ref, dq_acc, dk_acc, dv_acc, dl0_scr, dl1_scr):
        pair = pl.program_id(0)
        head0 = lax.broadcasted_iota(jnp.int32, (1, LANES), 1) < HEAD_DIM

        def zero_kv():
            def f(c, carry):
                rows = pl.ds(pl.multiple_of(c * ATTN_ROWS, ATTN_ROWS), ATTN_ROWS)
                dk_acc[rows, :] = jnp.zeros((ATTN_ROWS, LANES), F32)
                dv_acc[rows, :] = jnp.zeros((ATTN_ROWS, LANES), F32)
                return carry
            lax.fori_loop(0, s // ATTN_ROWS, f, 0)

        if gqa:
            pl.when(pair == 0)(zero_kv)
        else:
            zero_kv()

        def prep(c, dm):
            rows = pl.ds(pl.multiple_of(c * ATTN_ROWS, ATTN_ROWS), ATTN_ROWS)
            dq_acc[rows, :] = jnp.zeros((ATTN_ROWS, LANES), F32)
            prod = do_ref[rows, :] * o_ref[rows, :]
            d0 = jnp.sum(jnp.where(head0, prod, 0.0), axis=1, keepdims=True)
            d1 = jnp.sum(jnp.where(head0, 0.0, prod), axis=1, keepdims=True)
            dl0_scr[rows, :] = jnp.broadcast_to(d0, (ATTN_ROWS, LANES))
            dl1_scr[rows, :] = jnp.broadcast_to(d1, (ATTN_ROWS, LANES))
            lse_own = jnp.where(head0, lse0_ref[rows, :], lse1_ref[rows, :])
            psink = jnp.exp(mi_ref[...] - lse_own)
            return dm - jnp.sum(psink * jnp.where(head0, d0, d1), axis=0, keepdims=True)

        dm_ref[...] = lax.fori_loop(0, s // ATTN_ROWS, prep, jnp.zeros((1, LANES), F32))

        def one_block(b, d, r):
            rq, rp = _band_rows(b, d, r)
            mask = _band_mask(b, max_dist)
            mask2 = jnp.concatenate([mask, mask], axis=0)
            q2 = _stack_heads((q_ref[rq, :] * SCALE).astype(BF16), head0)
            do2 = _stack_heads(do_ref[rq, :].astype(BF16), head0)
            k2 = jnp.concatenate([k_ref[rp, :], k_ref[rq, :]], axis=0)
            v2 = jnp.concatenate([v_ref[rp, :], v_ref[rq, :]], axis=0)
            if gqa:
                k2 = _kv_for_pair(k2, pair)
                v2 = _kv_for_pair(v2, pair)
            k2 = k2.astype(BF16)
            v2 = v2.astype(BF16)
            lse2 = jnp.concatenate([lse0_ref[rq, :], lse1_ref[rq, :]], axis=0)
            dl2 = jnp.concatenate([dl0_scr[rq, :], dl1_scr[rq, :]], axis=0)
            lse2 = jnp.concatenate([lse2, lse2], axis=1)
            dl2 = jnp.concatenate([dl2, dl2], axis=1)
            sc = _dot_nt(q2, k2)
            p = jnp.where(mask2, jnp.exp(sc - lse2), 0.0)
            dp = _dot_nt(do2, v2)
            dsc = (p * (dp - dl2)).astype(BF16)
            dq2 = _unstack_heads(_dot_nn(dsc, k2), head0)
            dk2 = _dot_tn(dsc, q2)
            dv2 = _dot_tn(p.astype(BF16), do2)
            if gqa:
                dk2 = _kv_grad_from_pair(dk2, pair)
                dv2 = _kv_grad_from_pair(dv2, pair)
            dq_acc[rq, :] += dq2 * SCALE
            dk_acc[rp, :] += dk2[:BLOCK]
            dk_acc[rq, :] += dk2[BLOCK:]
            dv_acc[rp, :] += dv2[:BLOCK]
            dv_acc[rq, :] += dv2[BLOCK:]

        _block_loops(s, patterns, ATTN_UNROLL, one_block)

        def out_q(c, carry):
            rows = pl.ds(pl.multiple_of(c * ATTN_ROWS, ATTN_ROWS), ATTN_ROWS)
            dq_ref[rows, :] = dq_acc[rows, :].astype(BF16)
            return carry

        lax.fori_loop(0, s // ATTN_ROWS, out_q, 0)

        def out_kv():
            def f(c, carry):
                rows = pl.ds(pl.multiple_of(c * ATTN_ROWS, ATTN_ROWS), ATTN_ROWS)
                dk_ref[rows, :] = dk_acc[rows, :].astype(BF16)
                dv_ref[rows, :] = dv_acc[rows, :].astype(BF16)
                return carry
            lax.fori_loop(0, s // ATTN_ROWS, f, 0)

        if gqa:
            pl.when(pair == npair - 1)(out_kv)
        else:
            out_kv()

    own = pl.BlockSpec((s, LANES), lambda j: (0, j))
    if gqa:
        kv = lambda blk: pl.BlockSpec((s, LANES), lambda j, blk=blk: (0, blk))
        kv_out = pl.BlockSpec((s, LANES), lambda j: (0, 0))
        kv_shape = jax.ShapeDtypeStruct((s, LANES), BF16)
    else:
        kv = lambda blk: pl.BlockSpec((s, LANES), lambda j, blk=blk: (0, blk + j))
        kv_out = own
        kv_shape = jax.ShapeDtypeStruct((s, npair * LANES), BF16)
    return _call(
        body,
        grid=(npair,),
        in_specs=[pl.BlockSpec((s, LANES), lambda j: (0, q_blk + j)), kv(k_blk), kv(v_blk), own, own,
                  pl.BlockSpec((s, LANES), lambda j: (0, 2 * j)), pl.BlockSpec((s, LANES), lambda j: (0, 2 * j + 1)),
                  pl.BlockSpec((1, LANES), lambda j: (0, j))],
        out_specs=[own, kv_out, kv_out, pl.BlockSpec((1, LANES), lambda j: (0, j))],
        out_shape=[jax.ShapeDtypeStruct((s, npair * LANES), BF16), kv_shape, kv_shape,
                   jax.ShapeDtypeStruct((1, npair * LANES), F32)],
        operands=(z, z, z, do, o, lse, lse, m_init), name=name, scratch_shapes=[pltpu.VMEM((s, LANES), F32)] * 5,
        comm=comm)


def _adamw_math(w, g, m, v):
    m = ADAM_B1 * m + (1.0 - ADAM_B1) * g
    v = ADAM_B2 * v + (1.0 - ADAM_B2) * (g * g)
    m_hat = m / (1.0 - ADAM_B1 ** ADAM_STEP)
    v_hat = v / (1.0 - ADAM_B2 ** ADAM_STEP)
    delta = -ADAM_LR * (m_hat / (jnp.sqrt(v_hat) + ADAM_EPS) + ADAM_WD * w)
    return delta, m, v


def _adamw(w, g, m, v, name):
    rows, cols = w.shape
    tr = min(rows, 256)

    def body(w_ref, g_ref, m_ref, v_ref, d_ref, nm_ref, nv_ref):
        d_ref[...], nm_ref[...], nv_ref[...] = _adamw_math(w_ref[...], g_ref[...], m_ref[...], v_ref[...])

    spec = pl.BlockSpec((tr, cols), lambda i: (i, 0))
    return pl.pallas_call(
        body,
        grid=(rows // tr,),
        in_specs=[spec] * 4,
        out_specs=[spec] * 3,
        out_shape=[jax.ShapeDtypeStruct((rows, cols), F32)] * 3,
        compiler_params=_params("parallel"),
        name=name,
    )(w, g, m, v)


def _small_sum_adamw(gathered, w, m, v, name):
    _, rows, cols = gathered.shape

    def body(ga_ref, w_ref, m_ref, v_ref, g_ref, d_ref, nm_ref, nv_ref):
        g = ga_ref[0]
        for i in range(1, N_DEV):
            g = g + ga_ref[i]
        g_ref[...] = g
        d_ref[...], nm_ref[...], nv_ref[...] = _adamw_math(w_ref[...], g, m_ref[...], v_ref[...])

    return pl.pallas_call(
        body,
        out_shape=[jax.ShapeDtypeStruct((rows, cols), F32)] * 4,
        name=name,
    )(gathered, w, m, v)


def _pair_sum(g4, r1, pos, name):
    _, _, rows, cols = g4.shape
    tr = min(rows, 512)

    def body(pos_ref, g_ref, r_ref, o_ref):
        o_ref[...] = (g_ref[...].astype(F32) + r_ref[...].astype(F32)).astype(BF16)

    return pl.pallas_call(
        body,
        grid_spec=pltpu.PrefetchScalarGridSpec(
            num_scalar_prefetch=1,
            grid=(4, rows // tr),
            in_specs=[pl.BlockSpec((None, None, tr, cols), lambda i, j, p: (i, p[1], j, 0)),
                      pl.BlockSpec((None, tr, cols), lambda i, j, p: (i, j, 0))],
            out_specs=pl.BlockSpec((None, tr, cols), lambda i, j, p: (i, j, 0)),
        ),
        out_shape=jax.ShapeDtypeStruct((4, rows, cols), BF16),
        compiler_params=_params("parallel", "parallel"),
        name=name,
    )(pos, g4, r1)


def _final_sum(p, r2, pos, name):
    _, rows, cols = p.shape
    tr = min(rows, 512)

    def body(pos_ref, p_ref, r_ref, o_ref):
        o_ref[...] = ((p_ref[...].astype(F32) + r_ref[0].astype(F32)) + r_ref[1].astype(F32)) + r_ref[2].astype(F32)

    return pl.pallas_call(
        body,
        grid_spec=pltpu.PrefetchScalarGridSpec(
            num_scalar_prefetch=1,
            grid=(rows // tr,),
            in_specs=[pl.BlockSpec((None, tr, cols), lambda j, q: (q[0], j, 0)),
                      pl.BlockSpec((3, tr, cols), lambda j, q: (0, j, 0))],
            out_specs=pl.BlockSpec((tr, cols), lambda j, q: (j, 0)),
        ),
        out_shape=jax.ShapeDtypeStruct((rows, cols), F32),
        compiler_params=_params("parallel"),
        name=name,
    )(pos, p, r2)


def _place():
    return lax.axis_index("x"), lax.axis_index("y"), lax.axis_index("c")


def _gather_comm(shards):
    na = len(shards)

    def plan(ins, outs, sems):
        send_sems, recv_sems, local_sems = sems
        x, y, c = _place()
        me, sibling = (x, y, c), (x, y, 1 - c)
        chips = [(1 - x, y), (x, 1 - y), (1 - x, 1 - y)]

        def rows(a, px, py, pc):
            m = ins[a].shape[0]
            return outs[a].at[pl.ds((4 * px + 2 * py + pc) * m, m), :]

        def copy(a, k, block, to, src=None):
            return pltpu.make_async_remote_copy(
                src_ref=rows(a, *block) if src is None else src, dst_ref=rows(a, *block),
                send_sem=send_sems.at[a, k], recv_sem=recv_sems.at[a, k], device_id=to, device_id_type=MESH)

        mine = [pltpu.make_async_copy(ins[a], rows(a, *me), local_sems.at[a]) for a in range(na)]
        first = []
        for a in range(na):
            first.append(copy(a, 0, me, sibling, src=ins[a]))
            first += [copy(a, 1 + j, me, (*chip, c), src=ins[a]) for j, chip in enumerate(chips)]
        return me, sibling, chips, c, copy, mine, first

    def start(ins, outs, sems):
        *_, mine, first = plan(ins, outs, sems)
        for cp in mine + first:
            cp.start()

    def finish(ins, outs, sems):
        me, sibling, chips, c, copy, mine, first = plan(ins, outs, sems)
        passed = []
        for j, chip in enumerate(chips):
            for a in range(na):
                copy(a, 1 + j, (*chip, c), me).wait_recv()
                cp = copy(a, 4 + j, (*chip, c), sibling)
                cp.start()
                passed.append(cp)
        for a in range(na):
            copy(a, 0, sibling, me).wait_recv()
            for j, chip in enumerate(chips):
                copy(a, 4 + j, (*chip, 1 - c), me).wait_recv()
        for cp in first + passed:
            cp.wait_send()
        for cp in mine:
            cp.wait()

    return _Comm(tuple(shards),
                 tuple(jax.ShapeDtypeStruct((N_DEV * t.shape[0], t.shape[1]), t.dtype) for t in shards),
                 (pltpu.SemaphoreType.DMA((na, 7)), pltpu.SemaphoreType.DMA((na, 7)), pltpu.SemaphoreType.DMA((na,))),
                 start, finish)


def _exchange_comm(arrays, out_shape, n_copies, copies_of):
    na = len(arrays)

    def every(ins, outs, sems):
        send_sems, recv_sems = sems
        return [cp for a in range(na) for cp in copies_of(ins, outs, a, send_sems, recv_sems)]

    def start(ins, outs, sems):
        for cp in every(ins, outs, sems):
            cp.start()

    def finish(ins, outs, sems):
        for cp in every(ins, outs, sems):
            cp.wait()

    return _Comm(tuple(arrays), tuple(out_shape),
                 (pltpu.SemaphoreType.DMA((na, n_copies)), pltpu.SemaphoreType.DMA((na, n_copies))), start, finish)


def _sibling_comm(grads):
    def copies_of(ins, outs, a, send_sems, recv_sems):
        x, y, c = _place()
        return [pltpu.make_async_remote_copy(
            src_ref=ins[a].at[chip, 1 - c], dst_ref=outs[a].at[chip],
            send_sem=send_sems.at[a, chip], recv_sem=recv_sems.at[a, chip],
            device_id=(x, y, 1 - c), device_id_type=MESH) for chip in range(4)]

    return _exchange_comm(grads, [jax.ShapeDtypeStruct((4,) + t.shape[2:], t.dtype) for t in grads], 4, copies_of)


def _chip_comm(partials):
    def copies_of(ins, outs, a, send_sems, recv_sems):
        x, y, c = _place()
        chips = [(1 - x, y), (x, 1 - y), (1 - x, 1 - y)]
        return [pltpu.make_async_remote_copy(
            src_ref=ins[a].at[2 * cx + cy], dst_ref=outs[a].at[k],
            send_sem=send_sems.at[a, k], recv_sem=recv_sems.at[a, k],
            device_id=(cx, cy, c), device_id_type=MESH) for k, (cx, cy) in enumerate(chips)]

    return _exchange_comm(partials, [jax.ShapeDtypeStruct((3,) + t.shape[1:], t.dtype) for t in partials], 3, copies_of)


def _pad_rows(t, rows):
    return jnp.pad(t, ((0, rows - t.shape[0]), (0, D_MODEL - t.shape[1])))


def _pack_small(g_mix, g_group, g_mlp, g_final, conv, sinks):
    return jnp.concatenate([
        _pad_rows(g_mix, 8), _pad_rows(g_group, 8), _pad_rows(g_mlp, 8), _pad_rows(g_final.reshape(1, D_MODEL), 8),
        _pad_rows(conv.reshape(DEPTH * 3, CONV_CH), 8), _pad_rows(sinks.reshape(1, DEPTH * 6), 8)], axis=0)


def _unpack_small(slab):
    return (slab[0:2], slab[8:10], slab[16:18], slab[24], slab[32:38, :CONV_CH].reshape(DEPTH, 3, CONV_CH),
            slab[40, :DEPTH * 6].reshape(DEPTH, 2, 3))


def kernel(x, w_in, conv_w, sinks, g_mix, g_group, w_o, g_mlp, w_ff_in, w_ff_out, g_final, loss_target, m_w_in, m_conv_w, m_sinks, m_g_mix, m_g_group, m_w_o, m_g_mlp, m_w_ff_in, m_w_ff_out, m_g_final, v_w_in, v_conv_w, v_sinks, v_g_mix, v_g_group, v_w_o, v_g_mlp, v_w_ff_in, v_w_ff_out, v_g_final):
    ax, ay, ac = _place()
    chip = 2 * ax + ay
    dev = 4 * ax + 2 * ay + ac
    pos = jnp.stack([chip, ac]).astype(jnp.int32)

    x0 = x.reshape(SEQ, D_MODEL)
    target = loss_target.reshape(SEQ, D_MODEL)

    shards = {}
    for l in range(DEPTH):
        shards[l, 0], shards[l, 1] = w_in[l].T.astype(BF16), w_o[l].astype(BF16)
        shards[l, 2], shards[l, 3] = w_ff_in[l].T.astype(BF16), w_ff_out[l].astype(BF16)
    conv_tile = jnp.pad(conv_w.reshape(DEPTH * 3, CONV_CH // N_DEV), ((0, 2), (0, LANES - CONV_CH // N_DEV)))
    wt_in0, conv_all = _comm_only(_gather_comm([shards[0, 0], conv_tile]), "gather_first")
    conv_full = conv_all.reshape(N_DEV, 8, LANES)[:, :DEPTH * 3, :CONV_CH // N_DEV]
    conv_full = conv_full.transpose(1, 0, 2).reshape(DEPTH, 3, CONV_CH)

    loss_slab, dx, gsum, small = _step(x0, target, shards, wt_in0, conv_full, sinks, g_mix, g_group, g_mlp, g_final, pos)
    loss = lax.psum(loss_slab[0, 0], ("x", "y", "c"))
    return _finish(loss, dx, gsum, small, dev, w_in, conv_w, sinks, g_mix, g_group, w_o, g_mlp, w_ff_in, w_ff_out, g_final, m_w_in, m_conv_w, m_sinks, m_g_mix, m_g_group, m_w_o, m_g_mlp, m_w_ff_in, m_w_ff_out, m_g_final, v_w_in, v_conv_w, v_sinks, v_g_mix, v_g_group, v_w_o, v_g_mlp, v_w_ff_in, v_w_ff_out, v_g_final)


FWD_CARRY = {"in_proj": ((0, 1),), "dilated": ((0, 2), (1, 0)), "window": ((0, 3),),
             "mix_out": ((1, 1),), "ff_in": ((1, 2),), "ff_out": ((1, 3),)}


def _step(x0, target, shards, wt_in0, conv_full, sinks, g_mix, g_group, g_mlp, g_final, pos):
    sink_lanes = jnp.repeat(sinks.reshape(DEPTH, 6), HEAD_DIM, axis=1)
    no_sink = jnp.full((1, A_WIDTH), NEG_BIG, F32)
    full = {(0, 0): wt_in0}

    def gather(stage, l):
        keys = FWD_CARRY[stage] if l == 0 else ()
        return keys, (_gather_comm([shards[k] for k in keys]) if keys else None)

    def landed(keys, got):
        full.update(zip(keys, got))

    saved = []
    xc = x0
    for l in range(DEPTH):
        keys, comm = gather("in_proj", l)
        (z, h), got = _norm_mm(xc, g_mix[l:l + 1], full[l, 0], False, f"in_proj_{l}", comm)
        landed(keys, got)
        keys, comm = gather("dilated", l)
        (ya, lse_a), got = _attn_fwd(z, no_sink, 0.0, QA_BLK, KA_BLK, VA_BLK, DILATED_PATTERNS, A_MAX_DIST, False,
                                     f"dilated_attn_{l}", comm)
        landed(keys, got)
        yb = _conv_fwd(z, conv_full[l], f"conv_{l}")
        sink_l = sink_lanes[l:l + 1]
        keys, comm = gather("window", l)
        (yc, lse_c), got = _attn_fwd(z, sink_l, 1.0, QC_BLK, KC_BLK, VC_BLK, (1,), C_MAX_DIST, True,
                                     f"window_attn_{l}", comm)
        landed(keys, got)
        keys, comm = gather("mix_out", l)
        (y, x1), got = _mix_out(ya, yb, yc, g_group[l:l + 1], full[l, 1], xc, f"mix_out_{l}", comm)
        landed(keys, got)
        keys, comm = gather("ff_in", l)
        (a, h2), got = _norm_mm(x1, g_mlp[l:l + 1], full[l, 2], True, f"ff_in_{l}", comm)
        landed(keys, got)
        keys, comm = gather("ff_out", l)
        (x2,), got = _mm_res(a, full[l, 3], x1, f"ff_out_{l}", comm)
        landed(keys, got)
        saved.append((xc, z, h, ya, lse_a, yb, yc, lse_c, sink_l, y, x1, a, h2))
        xc = x2

    loss_slab, dx, dxb, dg_final = _loss_head(xc, g_final.reshape(1, D_MODEL), target, "loss_head")

    def by_owner(t):
        return t.reshape(4, 2, t.shape[0] // N_DEV, D_MODEL)

    def pair(key, g, r1):
        return _pair_sum(g, r1, pos, f"grad_pair_sum_{key[0]}_{key[1]}")

    partial, r2 = {}, {}
    dg_mix, dg_group, dg_mlp, dconv, dsinks = [None] * DEPTH, [None] * DEPTH, [None] * DEPTH, [None] * DEPTH, [None] * DEPTH
    for l in reversed(range(DEPTH)):
        xin, z, h, ya, lse_a, yb, yc, lse_c, sink_l, y, x1, a, h2 = saved[l]
        above = (l + 1, 0)
        (du,), got = _mlp_bwd_act(dxb, full[l, 3], a, f"ff_out_bwd_{l}",
                                  _chip_comm([partial[above]]) if above in partial else None)
        if got:
            r2[above] = got[0]
        (g3,), _ = _mm_tn(a, dxb, f"grad_w_ff_out_{l}")
        g3 = by_owner(g3)
        (g2,), got = _mm_tn(du, h2, f"grad_w_ff_in_{l}", _sibling_comm([g3]))
        g2 = by_owner(g2)
        partial[l, 3] = pair((l, 3), g3, got[0])
        (dx1, dx1b, dg_mlp[l]), got = _mm_nn_normbwd(du, full[l, 2], x1, dx, g_mlp[l:l + 1], f"ff_in_bwd_{l}",
                                                    _join(_chip_comm([partial[l, 3]]), _sibling_comm([g2])))
        r2[l, 3] = got[0]
        partial[l, 2] = pair((l, 2), g2, got[1])
        (g1,), _ = _mm_tn(y, dx1b, f"grad_w_o_{l}")
        g1 = by_owner(g1)
        (dya, dyb, dyc, dg_group[l]), got = _mix_bwd(dx1b, full[l, 1], ya, yb, yc, g_group[l:l + 1],
                                                     f"mix_out_bwd_{l}", _sibling_comm([g1]))
        partial[l, 1] = pair((l, 1), g1, got[0])
        (dqa, dka, dva, _), got = _attn_bwd(z, dya, ya, lse_a, no_sink, QA_BLK, KA_BLK, VA_BLK, DILATED_PATTERNS,
                                            A_MAX_DIST, False, f"dilated_attn_bwd_{l}",
                                            _chip_comm([partial[l, 2], partial[l, 1]]))
        r2[l, 2], r2[l, 1] = got
        dgate_b, dgate_c, dconv_x, dcw = _conv_bwd(z, conv_full[l], dyb, f"conv_bwd_{l}")
        (dqc, dkc, dvc, dsink), _ = _attn_bwd(z, dyc, yc, lse_c, sink_l, QC_BLK, KC_BLK, VC_BLK, (1,), C_MAX_DIST,
                                              True, f"window_attn_bwd_{l}")
        dz = jnp.concatenate([dqa, dka, dva, dgate_b, dgate_c, dconv_x, dqc, dkc, dvc], axis=1)
        (g0,), _ = _mm_tn(dz, h, f"grad_w_in_{l}")
        g0 = by_owner(g0)
        if l > 0:
            (dx, dxb, dg_mix[l]), got = _mm_nn_normbwd(dz, full[l, 0], xin, dx1, g_mix[l:l + 1], f"in_proj_bwd_{l}",
                                                      _sibling_comm([g0]))
            partial[l, 0] = pair((l, 0), g0, got[0])
        else:
            (r1,) = _comm_only(_sibling_comm([g0]), "grad_sibling_exchange_last")
            partial[l, 0] = pair((l, 0), g0, r1)
            (dx, dxb, dg_mix[l]), got = _mm_nn_normbwd(dz, full[l, 0], xin, dx1, g_mix[l:l + 1], f"in_proj_bwd_{l}",
                                                      _chip_comm([partial[l, 0]]))
            r2[l, 0] = got[0]
        dconv[l] = dcw[:3]
        dsinks[l] = dsink[0, ::HEAD_DIM]
    gsum = {key: _final_sum(partial[key], r2[key], pos, f"grad_final_sum_{key[0]}_{key[1]}") for key in partial}
    small = _pack_small(jnp.concatenate(dg_mix), jnp.concatenate(dg_group), jnp.concatenate(dg_mlp),
                        dg_final, jnp.stack(dconv), jnp.stack(dsinks))
    return loss_slab, dx, gsum, small


def _finish(loss, dx, gsum, small, dev, w_in, conv_w, sinks, g_mix, g_group, w_o, g_mlp, w_ff_in, w_ff_out, g_final, m_w_in, m_conv_w, m_sinks, m_g_mix, m_g_group, m_w_o, m_g_mlp, m_w_ff_in, m_w_ff_out, m_g_final, v_w_in, v_conv_w, v_sinks, v_g_mix, v_g_group, v_w_o, v_g_mlp, v_w_ff_in, v_w_ff_out, v_g_final):
    grad_x = dx.reshape(1, SEQ, D_MODEL)
    grad_w_in = jnp.stack([gsum[l, 0].T for l in range(DEPTH)])
    grad_w_o = jnp.stack([gsum[l, 1] for l in range(DEPTH)])
    grad_w_ff_in = jnp.stack([gsum[l, 2].T for l in range(DEPTH)])
    grad_w_ff_out = jnp.stack([gsum[l, 3] for l in range(DEPTH)])

    (small_all,) = _comm_only(_gather_comm([small]), "gather_small_grads")
    zeros_conv = jnp.zeros((DEPTH, 3, CONV_CH), F32)
    sw = _pack_small(g_mix, g_group, g_mlp, g_final, zeros_conv, sinks)
    sm = _pack_small(m_g_mix, m_g_group, m_g_mlp, m_g_final, zeros_conv, m_sinks)
    sv = _pack_small(v_g_mix, v_g_group, v_g_mlp, v_g_final, zeros_conv, v_sinks)
    sg, sd, snm, snv = _small_sum_adamw(small_all.reshape(N_DEV, SMALL_ROWS, D_MODEL), sw, sm, sv, "small_adamw")
    grad_g_mix, grad_g_group, grad_g_mlp, grad_g_final, conv_grad_full, grad_sinks = _unpack_small(sg)
    delta_g_mix, delta_g_group, delta_g_mlp, delta_g_final, _, delta_sinks = _unpack_small(sd)
    new_m_g_mix, new_m_g_group, new_m_g_mlp, new_m_g_final, _, new_m_sinks = _unpack_small(snm)
    new_v_g_mix, new_v_g_group, new_v_g_mlp, new_v_g_final, _, new_v_sinks = _unpack_small(snv)
    cs = CONV_CH // N_DEV
    grad_conv_w = lax.dynamic_slice_in_dim(conv_grad_full, dev * cs, cs, axis=2)

    def tile_of(t):
        return jnp.pad(t.reshape(1, DEPTH * 3 * cs), ((0, 7), (0, 256 - DEPTH * 3 * cs)))

    cd, cm, cv = _adamw(tile_of(conv_w), tile_of(grad_conv_w), tile_of(m_conv_w), tile_of(v_conv_w), "conv_adamw")
    untile = lambda t: t[0, :DEPTH * 3 * cs].reshape(DEPTH, 3, cs)
    delta_conv_w, new_m_conv_w, new_v_conv_w = untile(cd), untile(cm), untile(cv)

    def big(w, g, m, v, name):
        shp = w.shape
        flat = lambda t: t.reshape(shp[0] * shp[1], shp[2])
        return [t.reshape(shp) for t in _adamw(flat(w), flat(g), flat(m), flat(v), name)]

    delta_w_in, new_m_w_in, new_v_w_in = big(w_in, grad_w_in, m_w_in, v_w_in, "adamw_w_in")
    delta_w_o, new_m_w_o, new_v_w_o = big(w_o, grad_w_o, m_w_o, v_w_o, "adamw_w_o")
    delta_w_ff_in, new_m_w_ff_in, new_v_w_ff_in = big(w_ff_in, grad_w_ff_in, m_w_ff_in, v_w_ff_in, "adamw_w_ff_in")
    delta_w_ff_out, new_m_w_ff_out, new_v_w_ff_out = big(w_ff_out, grad_w_ff_out, m_w_ff_out, v_w_ff_out, "adamw_w_ff_out")

    return (loss, grad_x, grad_w_in, grad_conv_w, grad_sinks, grad_g_mix, grad_g_group, grad_w_o, grad_g_mlp,
            grad_w_ff_in, grad_w_ff_out, grad_g_final,
            delta_w_in, delta_conv_w, delta_sinks, delta_g_mix, delta_g_group, delta_w_o, delta_g_mlp,
            delta_w_ff_in, delta_w_ff_out, delta_g_final,
            new_m_w_in, new_m_conv_w, new_m_sinks, new_m_g_mix, new_m_g_group, new_m_w_o, new_m_g_mlp,
            new_m_w_ff_in, new_m_w_ff_out, new_m_g_final,
            new_v_w_in, new_v_conv_w, new_v_sinks, new_v_g_mix, new_v_g_group, new_v_w_o, new_v_g_mlp,
            new_v_w_ff_in, new_v_w_ff_out, new_v_g_final)
```

```python
from typing import Callable, NamedTuple

import jax
import jax.numpy as jnp
from jax import lax
from jax.experimental import pallas as pl
from jax.experimental.pallas import tpu as pltpu

F32 = jnp.float32
BF16 = jnp.bfloat16
MESH = pl.DeviceIdType.MESH

N_DEV = 8
SEQ = 4096
D_MODEL = 1024
DEPTH = 2
HEAD_DIM = 64
LANES = 128
A_WIDTH = 384
CONV_CH = 256
C_WIDTH = 384
KV_WIDTH = 128
IN_WIDTH = 2560
D_FF = 4096
BLOCK = 128
DILATED_PATTERNS = (1, 4, 16)
A_MAX_DIST = 128
C_MAX_DIST = 127
EPS = 1e-6
SCALE = HEAD_DIM ** -0.5
NEG_BIG = -1e30

QA_BLK, KA_BLK, VA_BLK = 0, 3, 6
GB_BLK, GC_BLK, XB_BLK = 9, 11, 13
QC_BLK, KC_BLK, VC_BLK = 15, 18, 19

ADAM_LR = 0.001
ADAM_B1 = 0.9
ADAM_B2 = 0.999
ADAM_EPS = 1e-08
ADAM_WD = 0.01
ADAM_STEP = 10

VMEM_LIMIT = 56 * 1024 * 1024
ROW_TILE = 512
COL_CHUNK = 512
SMALL_ROWS = 48


def _dot_nn(a, b):
    return lax.dot_general(a, b, (((1,), (0,)), ((), ())), preferred_element_type=F32)


def _dot_nt(a, b):
    return lax.dot_general(a, b, (((1,), (1,)), ((), ())), preferred_element_type=F32)


def _dot_tn(a, b):
    return lax.dot_general(a, b, (((0,), (0,)), ((), ())), preferred_element_type=F32)


def _params(*sem):
    return pltpu.CompilerParams(dimension_semantics=sem, vmem_limit_bytes=VMEM_LIMIT)


def _rms_scale(t):
    return lax.rsqrt(jnp.mean(t * t, axis=-1, keepdims=True) + EPS)


def _rms_bwd(n, r, dn):
    return r * (dn - n * jnp.mean(dn * n, axis=-1, keepdims=True))


class _Comm(NamedTuple):
    arrays: tuple
    out_shape: tuple
    sems: tuple
    start: Callable
    finish: Callable


def _join(*comms):
    comms = [c for c in comms if c is not None]
    if not comms:
        return None

    def run(which):
        def f(ins, outs, sems):
            i = o = s = 0
            for c in comms:
                ni, no, ns = len(c.arrays), len(c.out_shape), len(c.sems)
                getattr(c, which)(ins[i:i + ni], outs[o:o + no], sems[s:s + ns])
                i, o, s = i + ni, o + no, s + ns
        return f

    return _Comm(sum((tuple(c.arrays) for c in comms), ()), sum((tuple(c.out_shape) for c in comms), ()),
                 sum((tuple(c.sems) for c in comms), ()), run("start"), run("finish"))


def _call(body, grid, in_specs, out_specs, out_shape, operands, name, scratch_shapes=(), comm=None):
    n_in, n_out, n_scr = len(in_specs), len(out_shape), len(scratch_shapes)
    if comm is None:
        res = pl.pallas_call(body, grid=grid, in_specs=list(in_specs), out_specs=list(out_specs),
                             out_shape=list(out_shape), scratch_shapes=list(scratch_shapes),
                             compiler_params=_params("arbitrary"), name=name)(*operands)
        return list(res), []
    c_in, c_out = len(comm.arrays), len(comm.out_shape)
    hbm = pl.BlockSpec(memory_space=pl.ANY)
    last = grid[0] - 1

    def carried(*refs):
        ins, cins = refs[:n_in], refs[n_in:n_in + c_in]
        o0 = n_in + c_in
        outs, couts = refs[o0:o0 + n_out], refs[o0 + n_out:o0 + n_out + c_out]
        s0 = o0 + n_out + c_out
        scr, sems = refs[s0:s0 + n_scr], refs[s0 + n_scr:]
        pl.when(pl.program_id(0) == 0)(lambda: comm.start(cins, couts, sems))
        body(*ins, *outs, *scr)
        pl.when(pl.program_id(0) == last)(lambda: comm.finish(cins, couts, sems))

    res = pl.pallas_call(carried, grid=grid, in_specs=list(in_specs) + [hbm] * c_in,
                         out_specs=list(out_specs) + [hbm] * c_out, out_shape=list(out_shape) + list(comm.out_shape),
                         scratch_shapes=list(scratch_shapes) + list(comm.sems),
                         compiler_params=_params("arbitrary"), name=name)(*operands, *comm.arrays)
    return list(res[:n_out]), list(res[n_out:])


def _comm_only(comm, name):
    hbm = pl.BlockSpec(memory_space=pl.ANY)
    c_in, c_out = len(comm.arrays), len(comm.out_shape)

    def body(*refs):
        ins, outs, sems = refs[:c_in], refs[c_in:c_in + c_out], refs[c_in + c_out:]
        comm.start(ins, outs, sems)
        comm.finish(ins, outs, sems)

    return pl.pallas_call(body, in_specs=[hbm] * c_in, out_specs=[hbm] * c_out, out_shape=list(comm.out_shape),
                          scratch_shapes=list(comm.sems), name=name)(*comm.arrays)


def _norm_mm(x, g, wt, relu2, name, comm=None):
    s, d = x.shape
    n = wt.shape[0]
    tm = ROW_TILE

    def body(x_ref, g_ref, w_ref, o_ref, h_ref):
        xx = x_ref[...]
        h = ((xx * _rms_scale(xx)) * g_ref[...]).astype(BF16)
        h_ref[...] = h
        for n0 in range(0, n, COL_CHUNK):
            zc = _dot_nt(h, w_ref[n0:n0 + COL_CHUNK, :])
            if relu2:
                zc = jnp.square(jnp.maximum(zc, 0.0)).astype(BF16)
            o_ref[:, n0:n0 + COL_CHUNK] = zc

    return _call(
        body,
        grid=(s // tm,),
        in_specs=[pl.BlockSpec((tm, d), lambda i: (i, 0)),
                  pl.BlockSpec((1, d), lambda i: (0, 0)),
                  pl.BlockSpec((n, d), lambda i: (0, 0))],
        out_specs=[pl.BlockSpec((tm, n), lambda i: (i, 0)),
                   pl.BlockSpec((tm, d), lambda i: (i, 0))],
        out_shape=[jax.ShapeDtypeStruct((s, n), BF16 if relu2 else F32), jax.ShapeDtypeStruct((s, d), BF16)],
        operands=(x, g, wt), name=name, comm=comm)


def _mm_res(a, w2, x1, name, comm=None):
    s, f = a.shape
    d = w2.shape[1]
    tm = ROW_TILE

    def body(a_ref, w_ref, x_ref, o_ref):
        o_ref[...] = x_ref[...] + _dot_nn(a_ref[...], w_ref[...])

    return _call(
        body,
        grid=(s // tm,),
        in_specs=[pl.BlockSpec((tm, f), lambda i: (i, 0)),
                  pl.BlockSpec((f, d), lambda i: (0, 0)),
                  pl.BlockSpec((tm, d), lambda i: (i, 0))],
        out_specs=[pl.BlockSpec((tm, d), lambda i: (i, 0))],
        out_shape=[jax.ShapeDtypeStruct((s, d), F32)],
        operands=(a, w2, x1), name=name, comm=comm)


def _mix_out(ya, yb, yc, gg, wo, x0, name, comm=None):
    s = ya.shape[0]
    d = wo.shape[1]
    tm = ROW_TILE

    def body(ya_ref, yb_ref, yc_ref, g_ref, w_ref, x_ref, y_ref, o_ref):
        parts = []
        for ref in (ya_ref, yb_ref, yc_ref):
            t = ref[...]
            parts.append(t * _rms_scale(t))
        y = (jnp.concatenate(parts, axis=1) * g_ref[...]).astype(BF16)
        y_ref[...] = y
        o_ref[...] = x_ref[...] + _dot_nn(y, w_ref[...])

    return _call(
        body,
        grid=(s // tm,),
        in_specs=[pl.BlockSpec((tm, A_WIDTH), lambda i: (i, 0)),
                  pl.BlockSpec((tm, CONV_CH), lambda i: (i, 0)),
                  pl.BlockSpec((tm, C_WIDTH), lambda i: (i, 0)),
                  pl.BlockSpec((1, d), lambda i: (0, 0)),
                  pl.BlockSpec((d, d), lambda i: (0, 0)),
                  pl.BlockSpec((tm, d), lambda i: (i, 0))],
        out_specs=[pl.BlockSpec((tm, d), lambda i: (i, 0)),
                   pl.BlockSpec((tm, d), lambda i: (i, 0))],
        out_shape=[jax.ShapeDtypeStruct((s, d), BF16), jax.ShapeDtypeStruct((s, d), F32)],
        operands=(ya, yb, yc, gg, wo, x0), name=name, comm=comm)


def _loss_head(x, g, target, name):
    s, d = x.shape
    tm = ROW_TILE

    def body(x_ref, g_ref, t_ref, loss_ref, dx_ref, dxb_ref, dg_ref):
        @pl.when(pl.program_id(0) == 0)
        def _():
            loss_ref[...] = jnp.zeros_like(loss_ref)
            dg_ref[...] = jnp.zeros_like(dg_ref)

        xx = x_ref[...]
        r = _rms_scale(xx)
        n = xx * r
        gv = g_ref[...]
        err = n * gv - t_ref[...]
        per_tok = jnp.sum(err * err, axis=1, keepdims=True) * (1.0 / d)
        loss_ref[...] += 0.5 * jnp.sum(per_tok, axis=0, keepdims=True)
        dout = err * (1.0 / d)
        dg_ref[...] += jnp.sum(dout * n, axis=0, keepdims=True)
        dx = _rms_bwd(n, r, dout * gv)
        dx_ref[...] = dx
        dxb_ref[...] = dx.astype(BF16)

    return pl.pallas_call(
        body,
        grid=(s // tm,),
        in_specs=[pl.BlockSpec((tm, d), lambda i: (i, 0)),
                  pl.BlockSpec((1, d), lambda i: (0, 0)),
                  pl.BlockSpec((tm, d), lambda i: (i, 0))],
        out_specs=[pl.BlockSpec((8, LANES), lambda i: (0, 0)),
                   pl.BlockSpec((tm, d), lambda i: (i, 0)),
                   pl.BlockSpec((tm, d), lambda i: (i, 0)),
                   pl.BlockSpec((1, d), lambda i: (0, 0))],
        out_shape=[jax.ShapeDtypeStruct((8, LANES), F32), jax.ShapeDtypeStruct((s, d), F32),
                   jax.ShapeDtypeStruct((s, d), BF16), jax.ShapeDtypeStruct((1, d), F32)],
        compiler_params=_params("arbitrary"),
        name=name,
    )(x, g, target)


def _mlp_bwd_act(dxb, w2, a, name, comm=None):
    s, d = dxb.shape
    f = w2.shape[0]
    tm = ROW_TILE

    def body(dx_ref, w_ref, a_ref, du_ref):
        dx = dx_ref[...]
        for n0 in range(0, f, COL_CHUNK):
            da = _dot_nt(dx, w_ref[n0:n0 + COL_CHUNK, :])
            rl = jnp.sqrt(a_ref[:, n0:n0 + COL_CHUNK].astype(F32))
            du_ref[:, n0:n0 + COL_CHUNK] = (da * (2.0 * rl)).astype(BF16)

    return _call(
        body,
        grid=(s // tm,),
        in_specs=[pl.BlockSpec((tm, d), lambda i: (i, 0)),
                  pl.BlockSpec((f, d), lambda i: (0, 0)),
                  pl.BlockSpec((tm, f), lambda i: (i, 0))],
        out_specs=[pl.BlockSpec((tm, f), lambda i: (i, 0))],
        out_shape=[jax.ShapeDtypeStruct((s, f), BF16)],
        operands=(dxb, w2, a), name=name, comm=comm)


def _mm_tn(a, b, name, comm=None):
    s, n = a.shape
    d = b.shape[1]
    tn = 512

    def body(a_ref, b_ref, o_ref, acc):
        for k0 in range(0, s, ROW_TILE):
            part = _dot_tn(a_ref[k0:k0 + ROW_TILE, :], b_ref[k0:k0 + ROW_TILE, :])
            if k0 == 0:
                acc[...] = part
            else:
                acc[...] += part
        o_ref[...] = acc[...].astype(BF16)

    return _call(
        body,
        grid=(n // tn,),
        in_specs=[pl.BlockSpec((s, tn), lambda j: (0, j)),
                  pl.BlockSpec((s, d), lambda j: (0, 0))],
        out_specs=[pl.BlockSpec((tn, d), lambda j: (j, 0))],
        out_shape=[jax.ShapeDtypeStruct((n, d), BF16)],
        operands=(a, b), name=name, scratch_shapes=[pltpu.VMEM((tn, d), F32)], comm=comm)


def _mm_nn_normbwd(dact, wt, x, dres, g, name, comm=None):
    s, kdim = dact.shape
    d = wt.shape[1]
    tm = ROW_TILE

    def body(a_ref, w_ref, x_ref, r_ref, g_ref, o_ref, ob_ref, dg_ref):
        @pl.when(pl.program_id(0) == 0)
        def _():
            dg_ref[...] = jnp.zeros_like(dg_ref)

        dh = _dot_nn(a_ref[...], w_ref[...])
        xx = x_ref[...]
        r = _rms_scale(xx)
        n = xx * r
        dg_ref[...] += jnp.sum(dh * n, axis=0, keepdims=True)
        dx = r_ref[...] + _rms_bwd(n, r, dh * g_ref[...])
        o_ref[...] = dx
        ob_ref[...] = dx.astype(BF16)

    return _call(
        body,
        grid=(s // tm,),
        in_specs=[pl.BlockSpec((tm, kdim), lambda i: (i, 0)),
                  pl.BlockSpec((kdim, d), lambda i: (0, 0)),
                  pl.BlockSpec((tm, d), lambda i: (i, 0)),
                  pl.BlockSpec((tm, d), lambda i: (i, 0)),
                  pl.BlockSpec((1, d), lambda i: (0, 0))],
        out_specs=[pl.BlockSpec((tm, d), lambda i: (i, 0)),
                   pl.BlockSpec((tm, d), lambda i: (i, 0)),
                   pl.BlockSpec((1, d), lambda i: (0, 0))],
        out_shape=[jax.ShapeDtypeStruct((s, d), F32), jax.ShapeDtypeStruct((s, d), BF16),
                   jax.ShapeDtypeStruct((1, d), F32)],
        operands=(dact, wt, x, dres, g), name=name, comm=comm)


def _mix_bwd(dx1, wo, ya, yb, yc, gg, name, comm=None):
    s, d = dx1.shape
    tm = ROW_TILE
    widths = (A_WIDTH, CONV_CH, C_WIDTH)

    def body(dx_ref, w_ref, ya_ref, yb_ref, yc_ref, g_ref, da_ref, db_ref, dc_ref, dg_ref):
        @pl.when(pl.program_id(0) == 0)
        def _():
            dg_ref[...] = jnp.zeros_like(dg_ref)

        dy = _dot_nt(dx_ref[...], w_ref[...])
        gv = g_ref[...]
        off = 0
        dgs = []
        for ref, out, w in zip((ya_ref, yb_ref, yc_ref), (da_ref, db_ref, dc_ref), widths):
            t = ref[...]
            r = _rms_scale(t)
            n = t * r
            dyg = dy[:, off:off + w]
            dgs.append(jnp.sum(dyg * n, axis=0, keepdims=True))
            out[...] = _rms_bwd(n, r, dyg * gv[:, off:off + w])
            off += w
        dg_ref[...] += jnp.concatenate(dgs, axis=1)

    return _call(
        body,
        grid=(s // tm,),
        in_specs=[pl.BlockSpec((tm, d), lambda i: (i, 0)),
                  pl.BlockSpec((d, d), lambda i: (0, 0)),
                  pl.BlockSpec((tm, A_WIDTH), lambda i: (i, 0)),
                  pl.BlockSpec((tm, CONV_CH), lambda i: (i, 0)),
                  pl.BlockSpec((tm, C_WIDTH), lambda i: (i, 0)),
                  pl.BlockSpec((1, d), lambda i: (0, 0))],
        out_specs=[pl.BlockSpec((tm, A_WIDTH), lambda i: (i, 0)),
                   pl.BlockSpec((tm, CONV_CH), lambda i: (i, 0)),
                   pl.BlockSpec((tm, C_WIDTH), lambda i: (i, 0)),
                   pl.BlockSpec((1, d), lambda i: (0, 0))],
        out_shape=[jax.ShapeDtypeStruct((s, A_WIDTH), F32), jax.ShapeDtypeStruct((s, CONV_CH), F32),
                   jax.ShapeDtypeStruct((s, C_WIDTH), F32), jax.ShapeDtypeStruct((1, d), F32)],
        operands=(dx1, wo, ya, yb, yc, gg), name=name, comm=comm)


CONV_CHUNK = 256
CONV_HALO = 8


def _conv_fwd(z, cw, name):
    s = z.shape[0]
    nch = s // CONV_CHUNK

    def body(gb_ref, gc_ref, xb_ref, w_ref, o_ref, us):
        us[pl.ds(0, CONV_HALO), :] = jnp.zeros((CONV_HALO, LANES), F32)
        us[pl.ds(CONV_HALO, s), :] = gc_ref[...] * xb_ref[...]
        w0, w1, w2 = w_ref[0:1, :], w_ref[1:2, :], w_ref[2:3, :]

        def chunk(c, carry):
            st = pl.multiple_of(c * CONV_CHUNK, CONV_CHUNK)
            ext = us[pl.ds(st, CONV_CHUNK + CONV_HALO), :]
            y = (w0 * ext[CONV_HALO - 2:CONV_HALO - 2 + CONV_CHUNK]
                 + w1 * ext[CONV_HALO - 1:CONV_HALO - 1 + CONV_CHUNK]
                 + w2 * ext[CONV_HALO:])
            o_ref[pl.ds(st, CONV_CHUNK), :] = gb_ref[pl.ds(st, CONV_CHUNK), :] * y
            return carry

        lax.fori_loop(0, nch, chunk, 0)

    col = lambda blk: pl.BlockSpec((s, LANES), lambda j, blk=blk: (0, blk + j))
    return pl.pallas_call(
        body,
        grid=(CONV_CH // LANES,),
        in_specs=[col(GB_BLK), col(GC_BLK), col(XB_BLK), pl.BlockSpec((3, LANES), lambda j: (0, j))],
        out_specs=pl.BlockSpec((s, LANES), lambda j: (0, j)),
        out_shape=jax.ShapeDtypeStruct((s, CONV_CH), F32),
        scratch_shapes=[pltpu.VMEM((s + CONV_HALO, LANES), F32)],
        compiler_params=_params("parallel"),
        name=name,
    )(z, z, z, cw)


def _conv_bwd(z, cw, dyb, name):
    s = z.shape[0]
    nch = s // CONV_CHUNK

    def body(gb_ref, gc_ref, xb_ref, w_ref, dy_ref, dgb_ref, dgc_ref, dxb_ref, dw_ref, us, ds_):
        us[pl.ds(0, CONV_HALO), :] = jnp.zeros((CONV_HALO, LANES), F32)
        us[pl.ds(CONV_HALO, s), :] = gc_ref[...] * xb_ref[...]
        ds_[pl.ds(s, CONV_HALO), :] = jnp.zeros((CONV_HALO, LANES), F32)
        ds_[pl.ds(0, s), :] = dy_ref[...] * gb_ref[...]
        w0, w1, w2 = w_ref[0:1, :], w_ref[1:2, :], w_ref[2:3, :]
        zero = jnp.zeros((1, LANES), F32)

        def chunk(c, carry):
            a0, a1, a2 = carry
            st = pl.multiple_of(c * CONV_CHUNK, CONV_CHUNK)
            rows = pl.ds(st, CONV_CHUNK)
            ext = us[pl.ds(st, CONV_CHUNK + CONV_HALO), :]
            um2 = ext[CONV_HALO - 2:CONV_HALO - 2 + CONV_CHUNK]
            um1 = ext[CONV_HALO - 1:CONV_HALO - 1 + CONV_CHUNK]
            u0 = ext[CONV_HALO:]
            dext = ds_[pl.ds(st, CONV_CHUNK + CONV_HALO), :]
            dc0 = dext[:CONV_CHUNK]
            du = w2 * dc0 + w1 * dext[1:1 + CONV_CHUNK] + w0 * dext[2:2 + CONV_CHUNK]
            yconv = w0 * um2 + w1 * um1 + w2 * u0
            dgb_ref[rows, :] = (dy_ref[rows, :] * yconv).astype(BF16)
            dgc_ref[rows, :] = (du * xb_ref[rows, :]).astype(BF16)
            dxb_ref[rows, :] = (du * gc_ref[rows, :]).astype(BF16)
            a0 = a0 + jnp.sum(dc0 * um2, axis=0, keepdims=True)
            a1 = a1 + jnp.sum(dc0 * um1, axis=0, keepdims=True)
            a2 = a2 + jnp.sum(dc0 * u0, axis=0, keepdims=True)
            return a0, a1, a2

        a0, a1, a2 = lax.fori_loop(0, nch, chunk, (zero, zero, zero))
        dw_ref[...] = jnp.concatenate([a0, a1, a2, jnp.zeros((5, LANES), F32)], axis=0)

    col = lambda blk: pl.BlockSpec((s, LANES), lambda j, blk=blk: (0, blk + j))
    own = pl.BlockSpec((s, LANES), lambda j: (0, j))
    return pl.pallas_call(
        body,
        grid=(CONV_CH // LANES,),
        in_specs=[col(GB_BLK), col(GC_BLK), col(XB_BLK), pl.BlockSpec((3, LANES), lambda j: (0, j)), own],
        out_specs=[own, own, own, pl.BlockSpec((8, LANES), lambda j: (0, j))],
        out_shape=[jax.ShapeDtypeStruct((s, CONV_CH), BF16)] * 3 + [jax.ShapeDtypeStruct((8, CONV_CH), F32)],
        scratch_shapes=[pltpu.VMEM((s + CONV_HALO, LANES), F32), pltpu.VMEM((s + CONV_HALO, LANES), F32)],
        compiler_params=_params("parallel"),
        name=name,
    )(z, z, z, cw, dyb)


ATTN_ROWS = 512
ATTN_UNROLL = 8


def _band_rows(b, d, r):
    base = pl.multiple_of(b * (BLOCK * d), BLOCK)
    prev = jnp.maximum(base - BLOCK * d, 0)
    if d == 1:
        return pl.ds(base, BLOCK), pl.ds(pl.multiple_of(prev, BLOCK), BLOCK)
    return pl.ds(base + r, BLOCK, stride=d), pl.ds(prev + r, BLOCK, stride=d)


def _write_band_bias(bias_ref, max_dist):
    qi = lax.broadcasted_iota(jnp.int32, (BLOCK, 2 * BLOCK), 0)
    kj = lax.broadcasted_iota(jnp.int32, (BLOCK, 2 * BLOCK), 1)
    dist = BLOCK + qi - kj
    band = (dist >= 0) & (dist <= max_dist)
    bias_ref[0:BLOCK, :] = jnp.where(band, 0.0, -jnp.inf)
    bias_ref[BLOCK:2 * BLOCK, :] = jnp.where(band & (kj >= BLOCK), 0.0, -jnp.inf)


def _band_bias(bias_ref, b):
    bias = bias_ref[pl.ds(pl.multiple_of(jnp.where(b > 0, 0, BLOCK), BLOCK), BLOCK), :]
    return jnp.concatenate([bias, bias], axis=0)


def _lane_half():
    return (lax.broadcasted_iota(jnp.int32, (1, LANES), 1) >= HEAD_DIM).astype(jnp.int32)


def _kv_for_pair(t, pair):
    half = _lane_half()
    want = (pair + half) >> 1
    return jnp.where(want != half, pltpu.roll(t, HEAD_DIM, 1), t)


def _kv_grad_from_pair(t, pair):
    half = _lane_half()
    mine = ((pair + half) >> 1) == half
    other = ((pair + 1 - half) >> 1) == half
    fold = t + pltpu.roll(t, HEAD_DIM, 1)
    return jnp.where(mine & other, fold, jnp.where(mine, t, 0.0))


def _stack_heads(t, head0):
    zero = jnp.zeros_like(t)
    return jnp.concatenate([jnp.where(head0, t, zero), jnp.where(head0, zero, t)], axis=0)


def _unstack_heads(t, head0):
    return jnp.where(head0, t[:BLOCK], t[BLOCK:])


def _block_loops(s, patterns, unroll, one_block):
    for d in patterns:
        nb = (s // BLOCK) // d
        ur = min(unroll, d)
        ub = unroll // ur
        for r0 in range(0, d, ur):
            def trip(i, carry, d=d, r0=r0, ur=ur, ub=ub):
                for u in range(ub):
                    for r in range(r0, r0 + ur):
                        one_block(i * ub + u, d, r)
                return carry
            lax.fori_loop(0, nb // ub, trip, 0)


def _attn_fwd(z, m_init, l_init, q_blk, k_blk, v_blk, patterns, max_dist, gqa, name, comm=None):
    s = z.shape[0]
    npair = 3

    def body(q_ref, k_ref, v_ref, mi_ref, o_ref, lse_ref, m_scr, l_scr, bias_scr, *kv_scr):
        pair = pl.program_id(0)
        head0 = lax.broadcasted_iota(jnp.int32, (1, LANES), 1) < HEAD_DIM
        _write_band_bias(bias_scr, max_dist)
        k_src, v_src = kv_scr if gqa else (k_ref, v_ref)

        def init(c, carry):
            rows = pl.ds(pl.multiple_of(c * ATTN_ROWS, ATTN_ROWS), ATTN_ROWS)
            m_scr[rows, :] = jnp.broadcast_to(mi_ref[...], (ATTN_ROWS, LANES))
            l_scr[rows, :] = jnp.full((ATTN_ROWS, LANES), l_init, F32)
            o_ref[rows, :] = jnp.zeros((ATTN_ROWS, LANES), F32)
            if gqa:
                k_src[rows, :] = _kv_for_pair(k_ref[rows, :], pair)
                v_src[rows, :] = _kv_for_pair(v_ref[rows, :], pair)
            return carry

        lax.fori_loop(0, s // ATTN_ROWS, init, 0)
        ones = jnp.ones((2 * BLOCK, LANES), BF16)

        def one_block(b, d, r):
            rq, rp = _band_rows(b, d, r)
            q2 = _stack_heads((q_ref[rq, :] * SCALE).astype(BF16), head0)
            k2 = jnp.concatenate([k_src[rp, :], k_src[rq, :]], axis=0)
            v2 = jnp.concatenate([v_src[rp, :], v_src[rq, :]], axis=0)
            sc = _dot_nt(q2, k2.astype(BF16)) + _band_bias(bias_scr, b)
            mb = jnp.max(sc, axis=1, keepdims=True)
            p = jnp.exp(sc - mb).astype(BF16)
            ob = _dot_nn(p, jnp.concatenate([v2.astype(BF16), ones], axis=1))
            m2 = _unstack_heads(jnp.broadcast_to(mb, (2 * BLOCK, LANES)), head0)
            l2 = _unstack_heads(ob[:, LANES:], head0)
            o2 = _unstack_heads(ob[:, :LANES], head0)
            m_old = m_scr[rq, :]
            m_new = jnp.maximum(m_old, m2)
            a_old = jnp.exp(m_old - m_new)
            a_blk = jnp.exp(m2 - m_new)
            o_ref[rq, :] = o_ref[rq, :] * a_old + o2 * a_blk
            l_scr[rq, :] = l_scr[rq, :] * a_old + l2 * a_blk
            m_scr[rq, :] = m_new

        _block_loops(s, patterns, ATTN_UNROLL, one_block)

        def fin(c, carry):
            rows = pl.ds(pl.multiple_of(c * ATTN_ROWS, ATTN_ROWS), ATTN_ROWS)
            l = l_scr[rows, :]
            o_ref[rows, :] = o_ref[rows, :] / l
            lse = m_scr[rows, :] + jnp.log(l)
            swapped = pltpu.roll(lse, HEAD_DIM, 1)
            lse_ref[rows, 0:LANES] = jnp.where(head0, lse, swapped)
            lse_ref[rows, LANES:2 * LANES] = jnp.where(head0, swapped, lse)
            return carry

        lax.fori_loop(0, s // ATTN_ROWS, fin, 0)

    kv = (lambda blk: pl.BlockSpec((s, LANES), lambda j, blk=blk: (0, blk), pipeline_mode=pl.Buffered(1))) if gqa \
        else (lambda blk: pl.BlockSpec((s, LANES), lambda j, blk=blk: (0, blk + j)))
    return _call(
        body,
        grid=(npair,),
        in_specs=[pl.BlockSpec((s, LANES), lambda j: (0, q_blk + j)), kv(k_blk), kv(v_blk),
                  pl.BlockSpec((1, LANES), lambda j: (0, j))],
        out_specs=[pl.BlockSpec((s, LANES), lambda j: (0, j)), pl.BlockSpec((s, 2 * LANES), lambda j: (0, j))],
        out_shape=[jax.ShapeDtypeStruct((s, npair * LANES), F32), jax.ShapeDtypeStruct((s, 2 * npair * LANES), F32)],
        operands=(z, z, z, m_init), name=name,
        scratch_shapes=[pltpu.VMEM((s, LANES), F32)] * 2 + [pltpu.VMEM((2 * BLOCK, 2 * BLOCK), F32)]
        + [pltpu.VMEM((s, LANES), F32)] * (2 if gqa else 0), comm=comm)


def _attn_bwd(z, do, o, lse, m_init, q_blk, k_blk, v_blk, patterns, max_dist, gqa, name, comm=None):
    s = z.shape[0]
    npair = 3

    def body(q_ref, k_ref, v_ref, do_ref, o_ref, lse0_ref, lse1_ref, mi_ref,
             dq_ref, dk_ref, dv_ref, dm_ref, dq_acc, dk_acc, dv_acc, dl0_scr, dl1_scr, bias_scr, *gqa_scr):
        pair = pl.program_id(0)
        head0 = lax.broadcasted_iota(jnp.int32, (1, LANES), 1) < HEAD_DIM
        _write_band_bias(bias_scr, max_dist)
        k_src, v_src, dk_sum, dv_sum = gqa_scr if gqa else (k_ref, v_ref, None, None)

        def prep(c, dm):
            rows = pl.ds(pl.multiple_of(c * ATTN_ROWS, ATTN_ROWS), ATTN_ROWS)
            dq_acc[rows, :] = jnp.zeros((ATTN_ROWS, LANES), F32)
            dk_acc[rows, :] = jnp.zeros((ATTN_ROWS, LANES), F32)
            dv_acc[rows, :] = jnp.zeros((ATTN_ROWS, LANES), F32)
            if gqa:
                k_src[rows, :] = _kv_for_pair(k_ref[rows, :], pair)
                v_src[rows, :] = _kv_for_pair(v_ref[rows, :], pair)
            prod = do_ref[rows, :] * o_ref[rows, :]
            d0 = jnp.sum(jnp.where(head0, prod, 0.0), axis=1, keepdims=True)
            d1 = jnp.sum(jnp.where(head0, 0.0, prod), axis=1, keepdims=True)
            dl0_scr[rows, :] = jnp.broadcast_to(d0, (ATTN_ROWS, LANES))
            dl1_scr[rows, :] = jnp.broadcast_to(d1, (ATTN_ROWS, LANES))
            lse_own = jnp.where(head0, lse0_ref[rows, :], lse1_ref[rows, :])
            psink = jnp.exp(mi_ref[...] - lse_own)
            return dm - jnp.sum(psink * jnp.where(head0, d0, d1), axis=0, keepdims=True)

        dm_ref[...] = lax.fori_loop(0, s // ATTN_ROWS, prep, jnp.zeros((1, LANES), F32))

        def one_block(b, d, r):
            rq, rp = _band_rows(b, d, r)
            q2 = _stack_heads((q_ref[rq, :] * SCALE).astype(BF16), head0)
            do2 = _stack_heads(do_ref[rq, :].astype(BF16), head0)
            k2 = jnp.concatenate([k_src[rp, :], k_src[rq, :]], axis=0).astype(BF16)
            v2 = jnp.concatenate([v_src[rp, :], v_src[rq, :]], axis=0).astype(BF16)
            lse2 = jnp.concatenate([lse0_ref[rq, :], lse1_ref[rq, :]], axis=0)
            dl2 = jnp.concatenate([dl0_scr[rq, :], dl1_scr[rq, :]], axis=0)
            lse2 = jnp.concatenate([lse2, lse2], axis=1)
            dl2 = jnp.concatenate([dl2, dl2], axis=1)
            p = jnp.exp(_dot_nt(q2, k2) + _band_bias(bias_scr, b) - lse2)
            dp = _dot_nt(do2, v2)
            dsc = (p * (dp - dl2)).astype(BF16)
            dq2 = _unstack_heads(_dot_nn(dsc, k2), head0)
            dk2 = _dot_tn(dsc, q2)
            dv2 = _dot_tn(p.astype(BF16), do2)
            dq_acc[rq, :] += dq2 * SCALE
            dk_acc[rp, :] += dk2[:BLOCK]
            dk_acc[rq, :] += dk2[BLOCK:]
            dv_acc[rp, :] += dv2[:BLOCK]
            dv_acc[rq, :] += dv2[BLOCK:]

        _block_loops(s, patterns, ATTN_UNROLL, one_block)

        def out(c, carry):
            rows = pl.ds(pl.multiple_of(c * ATTN_ROWS, ATTN_ROWS), ATTN_ROWS)
            dq_ref[rows, :] = dq_acc[rows, :].astype(BF16)
            if not gqa:
                dk_ref[rows, :] = dk_acc[rows, :].astype(BF16)
                dv_ref[rows, :] = dv_acc[rows, :].astype(BF16)
                return carry
            dk_t = _kv_grad_from_pair(dk_acc[rows, :], pair)
            dv_t = _kv_grad_from_pair(dv_acc[rows, :], pair)

            @pl.when(pair == 0)
            def _():
                dk_sum[rows, :] = dk_t
                dv_sum[rows, :] = dv_t

            @pl.when((pair > 0) & (pair < npair - 1))
            def _():
                dk_sum[rows, :] += dk_t
                dv_sum[rows, :] += dv_t

            @pl.when(pair == npair - 1)
            def _():
                dk_ref[rows, :] = (dk_sum[rows, :] + dk_t).astype(BF16)
                dv_ref[rows, :] = (dv_sum[rows, :] + dv_t).astype(BF16)

            return carry

        lax.fori_loop(0, s // ATTN_ROWS, out, 0)

    own = pl.BlockSpec((s, LANES), lambda j: (0, j))
    if gqa:
        kv = lambda blk: pl.BlockSpec((s, LANES), lambda j, blk=blk: (0, blk), pipeline_mode=pl.Buffered(1))
        kv_out = pl.BlockSpec((s, LANES), lambda j: (0, 0))
        kv_shape = jax.ShapeDtypeStruct((s, LANES), BF16)
    else:
        kv = lambda blk: pl.BlockSpec((s, LANES), lambda j, blk=blk: (0, blk + j))
        kv_out = own
        kv_shape = jax.ShapeDtypeStruct((s, npair * LANES), BF16)
    return _call(
        body,
        grid=(npair,),
        in_specs=[pl.BlockSpec((s, LANES), lambda j: (0, q_blk + j)), kv(k_blk), kv(v_blk), own, own,
                  pl.BlockSpec((s, LANES), lambda j: (0, 2 * j)), pl.BlockSpec((s, LANES), lambda j: (0, 2 * j + 1)),
                  pl.BlockSpec((1, LANES), lambda j: (0, j))],
        out_specs=[own, kv_out, kv_out, pl.BlockSpec((1, LANES), lambda j: (0, j))],
        out_shape=[jax.ShapeDtypeStruct((s, npair * LANES), BF16), kv_shape, kv_shape,
                   jax.ShapeDtypeStruct((1, npair * LANES), F32)],
        operands=(z, z, z, do, o, lse, lse, m_init), name=name,
        scratch_shapes=[pltpu.VMEM((s, LANES), F32)] * 5 + [pltpu.VMEM((2 * BLOCK, 2 * BLOCK), F32)]
        + [pltpu.VMEM((s, LANES), F32)] * (4 if gqa else 0), comm=comm)


def _adamw_math(w, g, m, v):
    m = ADAM_B1 * m + (1.0 - ADAM_B1) * g
    v = ADAM_B2 * v + (1.0 - ADAM_B2) * (g * g)
    m_hat = m / (1.0 - ADAM_B1 ** ADAM_STEP)
    v_hat = v / (1.0 - ADAM_B2 ** ADAM_STEP)
    delta = -ADAM_LR * (m_hat / (jnp.sqrt(v_hat) + ADAM_EPS) + ADAM_WD * w)
    return delta, m, v


def _adamw(w, g, m, v, name):
    rows, cols = w.shape
    tr = min(rows, 256)

    def body(w_ref, g_ref, m_ref, v_ref, d_ref, nm_ref, nv_ref):
        d_ref[...], nm_ref[...], nv_ref[...] = _adamw_math(w_ref[...], g_ref[...], m_ref[...], v_ref[...])

    spec = pl.BlockSpec((tr, cols), lambda i: (i, 0))
    return pl.pallas_call(
        body,
        grid=(rows // tr,),
        in_specs=[spec] * 4,
        out_specs=[spec] * 3,
        out_shape=[jax.ShapeDtypeStruct((rows, cols), F32)] * 3,
        compiler_params=_params("parallel"),
        name=name,
    )(w, g, m, v)


def _small_sum_adamw(gathered, w, m, v, name):
    _, rows, cols = gathered.shape

    def body(ga_ref, w_ref, m_ref, v_ref, g_ref, d_ref, nm_ref, nv_ref):
        g = ga_ref[0]
        for i in range(1, N_DEV):
            g = g + ga_ref[i]
        g_ref[...] = g
        d_ref[...], nm_ref[...], nv_ref[...] = _adamw_math(w_ref[...], g, m_ref[...], v_ref[...])

    return pl.pallas_call(
        body,
        out_shape=[jax.ShapeDtypeStruct((rows, cols), F32)] * 4,
        name=name,
    )(gathered, w, m, v)


def _pair_sum(g4, r1, pos, name):
    _, _, rows, cols = g4.shape
    tr = min(rows, 512)

    def body(pos_ref, g_ref, r_ref, o_ref):
        o_ref[...] = (g_ref[...].astype(F32) + r_ref[...].astype(F32)).astype(BF16)

    return pl.pallas_call(
        body,
        grid_spec=pltpu.PrefetchScalarGridSpec(
            num_scalar_prefetch=1,
            grid=(4, rows // tr),
            in_specs=[pl.BlockSpec((None, None, tr, cols), lambda i, j, p: (i, p[1], j, 0)),
                      pl.BlockSpec((None, tr, cols), lambda i, j, p: (i, j, 0))],
            out_specs=pl.BlockSpec((None, tr, cols), lambda i, j, p: (i, j, 0)),
        ),
        out_shape=jax.ShapeDtypeStruct((4, rows, cols), BF16),
        compiler_params=_params("parallel", "parallel"),
        name=name,
    )(pos, g4, r1)


def _final_sum(p, r2, pos, name):
    _, rows, cols = p.shape
    tr = min(rows, 512)

    def body(pos_ref, p_ref, r_ref, o_ref):
        o_ref[...] = ((p_ref[...].astype(F32) + r_ref[0].astype(F32)) + r_ref[1].astype(F32)) + r_ref[2].astype(F32)

    return pl.pallas_call(
        body,
        grid_spec=pltpu.PrefetchScalarGridSpec(
            num_scalar_prefetch=1,
            grid=(rows // tr,),
            in_specs=[pl.BlockSpec((None, tr, cols), lambda j, q: (q[0], j, 0)),
                      pl.BlockSpec((3, tr, cols), lambda j, q: (0, j, 0))],
            out_specs=pl.BlockSpec((tr, cols), lambda j, q: (j, 0)),
        ),
        out_shape=jax.ShapeDtypeStruct((rows, cols), F32),
        compiler_params=_params("parallel"),
        name=name,
    )(pos, p, r2)


def _place():
    return lax.axis_index("x"), lax.axis_index("y"), lax.axis_index("c")


def _gather_comm(shards):
    na = len(shards)

    def plan(ins, outs, sems):
        send_sems, recv_sems, local_sems = sems
        x, y, c = _place()
        me, sibling = (x, y, c), (x, y, 1 - c)
        chips = [(1 - x, y), (x, 1 - y), (1 - x, 1 - y)]

        def rows(a, px, py, pc):
            m = ins[a].shape[0]
            return outs[a].at[pl.ds((4 * px + 2 * py + pc) * m, m), :]

        def copy(a, k, block, to, src=None):
            return pltpu.make_async_remote_copy(
                src_ref=rows(a, *block) if src is None else src, dst_ref=rows(a, *block),
                send_sem=send_sems.at[a, k], recv_sem=recv_sems.at[a, k], device_id=to, device_id_type=MESH)

        mine = [pltpu.make_async_copy(ins[a], rows(a, *me), local_sems.at[a]) for a in range(na)]
        first = []
        for a in range(na):
            first.append(copy(a, 0, me, sibling, src=ins[a]))
            first += [copy(a, 1 + j, me, (*chip, c), src=ins[a]) for j, chip in enumerate(chips)]
        return me, sibling, chips, c, copy, mine, first

    def start(ins, outs, sems):
        *_, mine, first = plan(ins, outs, sems)
        for cp in mine + first:
            cp.start()

    def finish(ins, outs, sems):
        me, sibling, chips, c, copy, mine, first = plan(ins, outs, sems)
        passed = []
        for j, chip in enumerate(chips):
            for a in range(na):
                copy(a, 1 + j, (*chip, c), me).wait_recv()
                cp = copy(a, 4 + j, (*chip, c), sibling)
                cp.start()
                passed.append(cp)
        for a in range(na):
            copy(a, 0, sibling, me).wait_recv()
            for j, chip in enumerate(chips):
                copy(a, 4 + j, (*chip, 1 - c), me).wait_recv()
        for cp in first + passed:
            cp.wait_send()
        for cp in mine:
            cp.wait()

    return _Comm(tuple(shards),
                 tuple(jax.ShapeDtypeStruct((N_DEV * t.shape[0], t.shape[1]), t.dtype) for t in shards),
                 (pltpu.SemaphoreType.DMA((na, 7)), pltpu.SemaphoreType.DMA((na, 7)), pltpu.SemaphoreType.DMA((na,))),
                 start, finish)


def _exchange_comm(arrays, out_shape, n_copies, copies_of):
    na = len(arrays)

    def every(ins, outs, sems):
        send_sems, recv_sems = sems
        return [cp for a in range(na) for cp in copies_of(ins, outs, a, send_sems, recv_sems)]

    def start(ins, outs, sems):
        for cp in every(ins, outs, sems):
            cp.start()

    def finish(ins, outs, sems):
        for cp in every(ins, outs, sems):
            cp.wait()

    return _Comm(tuple(arrays), tuple(out_shape),
                 (pltpu.SemaphoreType.DMA((na, n_copies)), pltpu.SemaphoreType.DMA((na, n_copies))), start, finish)


def _sibling_comm(grads):
    def copies_of(ins, outs, a, send_sems, recv_sems):
        x, y, c = _place()
        return [pltpu.make_async_remote_copy(
            src_ref=ins[a].at[chip, 1 - c], dst_ref=outs[a].at[chip],
            send_sem=send_sems.at[a, chip], recv_sem=recv_sems.at[a, chip],
            device_id=(x, y, 1 - c), device_id_type=MESH) for chip in range(4)]

    return _exchange_comm(grads, [jax.ShapeDtypeStruct((4,) + t.shape[2:], t.dtype) for t in grads], 4, copies_of)


def _chip_comm(partials):
    def copies_of(ins, outs, a, send_sems, recv_sems):
        x, y, c = _place()
        chips = [(1 - x, y), (x, 1 - y), (1 - x, 1 - y)]
        return [pltpu.make_async_remote_copy(
            src_ref=ins[a].at[2 * cx + cy], dst_ref=outs[a].at[k],
            send_sem=send_sems.at[a, k], recv_sem=recv_sems.at[a, k],
            device_id=(cx, cy, c), device_id_type=MESH) for k, (cx, cy) in enumerate(chips)]

    return _exchange_comm(partials, [jax.ShapeDtypeStruct((3,) + t.shape[1:], t.dtype) for t in partials], 3, copies_of)


def _pad_rows(t, rows):
    return jnp.pad(t, ((0, rows - t.shape[0]), (0, D_MODEL - t.shape[1])))


def _pack_small(g_mix, g_group, g_mlp, g_final, conv, sinks):
    return jnp.concatenate([
        _pad_rows(g_mix, 8), _pad_rows(g_group, 8), _pad_rows(g_mlp, 8), _pad_rows(g_final.reshape(1, D_MODEL), 8),
        _pad_rows(conv.reshape(DEPTH * 3, CONV_CH), 8), _pad_rows(sinks.reshape(1, DEPTH * 6), 8)], axis=0)


def _unpack_small(slab):
    return (slab[0:2], slab[8:10], slab[16:18], slab[24], slab[32:38, :CONV_CH].reshape(DEPTH, 3, CONV_CH),
            slab[40, :DEPTH * 6].reshape(DEPTH, 2, 3))


def kernel(x, w_in, conv_w, sinks, g_mix, g_group, w_o, g_mlp, w_ff_in, w_ff_out, g_final, loss_target, m_w_in, m_conv_w, m_sinks, m_g_mix, m_g_group, m_w_o, m_g_mlp, m_w_ff_in, m_w_ff_out, m_g_final, v_w_in, v_conv_w, v_sinks, v_g_mix, v_g_group, v_w_o, v_g_mlp, v_w_ff_in, v_w_ff_out, v_g_final):
    ax, ay, ac = _place()
    chip = 2 * ax + ay
    dev = 4 * ax + 2 * ay + ac
    pos = jnp.stack([chip, ac]).astype(jnp.int32)

    x0 = x.reshape(SEQ, D_MODEL)
    target = loss_target.reshape(SEQ, D_MODEL)

    shards = {}
    for l in range(DEPTH):
        shards[l, 0], shards[l, 1] = w_in[l].T.astype(BF16), w_o[l].astype(BF16)
        shards[l, 2], shards[l, 3] = w_ff_in[l].T.astype(BF16), w_ff_out[l].astype(BF16)
    conv_tile = jnp.pad(conv_w.reshape(DEPTH * 3, CONV_CH // N_DEV), ((0, 2), (0, LANES - CONV_CH // N_DEV)))
    wt_in0, conv_all = _comm_only(_gather_comm([shards[0, 0], conv_tile]), "gather_first")
    conv_full = conv_all.reshape(N_DEV, 8, LANES)[:, :DEPTH * 3, :CONV_CH // N_DEV]
    conv_full = conv_full.transpose(1, 0, 2).reshape(DEPTH, 3, CONV_CH)

    loss_slab, dx, gsum, small = _step(x0, target, shards, wt_in0, conv_full, sinks, g_mix, g_group, g_mlp, g_final, pos)
    loss = lax.psum(loss_slab[0, 0], ("x", "y", "c"))
    return _finish(loss, dx, gsum, small, dev, w_in, conv_w, sinks, g_mix, g_group, w_o, g_mlp, w_ff_in, w_ff_out, g_final, m_w_in, m_conv_w, m_sinks, m_g_mix, m_g_group, m_w_o, m_g_mlp, m_w_ff_in, m_w_ff_out, m_g_final, v_w_in, v_conv_w, v_sinks, v_g_mix, v_g_group, v_w_o, v_g_mlp, v_w_ff_in, v_w_ff_out, v_g_final)


FWD_CARRY = {"in_proj": ((0, 1),), "dilated": ((0, 2), (1, 0)), "window": ((0, 3),),
             "mix_out": ((1, 1),), "ff_in": ((1, 2),), "ff_out": ((1, 3),)}


def _step(x0, target, shards, wt_in0, conv_full, sinks, g_mix, g_group, g_mlp, g_final, pos):
    sink_lanes = jnp.repeat(sinks.reshape(DEPTH, 6), HEAD_DIM, axis=1)
    no_sink = jnp.full((1, A_WIDTH), NEG_BIG, F32)
    full = {(0, 0): wt_in0}

    def gather(stage, l):
        keys = FWD_CARRY[stage] if l == 0 else ()
        return keys, (_gather_comm([shards[k] for k in keys]) if keys else None)

    def landed(keys, got):
        full.update(zip(keys, got))

    saved = []
    xc = x0
    for l in range(DEPTH):
        keys, comm = gather("in_proj", l)
        (z, h), got = _norm_mm(xc, g_mix[l:l + 1], full[l, 0], False, f"in_proj_{l}", comm)
        landed(keys, got)
        keys, comm = gather("dilated", l)
        (ya, lse_a), got = _attn_fwd(z, no_sink, 0.0, QA_BLK, KA_BLK, VA_BLK, DILATED_PATTERNS, A_MAX_DIST, False,
                                     f"dilated_attn_{l}", comm)
        landed(keys, got)
        yb = _conv_fwd(z, conv_full[l], f"conv_{l}")
        sink_l = sink_lanes[l:l + 1]
        keys, comm = gather("window", l)
        (yc, lse_c), got = _attn_fwd(z, sink_l, 1.0, QC_BLK, KC_BLK, VC_BLK, (1,), C_MAX_DIST, True,
                                     f"window_attn_{l}", comm)
        landed(keys, got)
        keys, comm = gather("mix_out", l)
        (y, x1), got = _mix_out(ya, yb, yc, g_group[l:l + 1], full[l, 1], xc, f"mix_out_{l}", comm)
        landed(keys, got)
        keys, comm = gather("ff_in", l)
        (a, h2), got = _norm_mm(x1, g_mlp[l:l + 1], full[l, 2], True, f"ff_in_{l}", comm)
        landed(keys, got)
        keys, comm = gather("ff_out", l)
        (x2,), got = _mm_res(a, full[l, 3], x1, f"ff_out_{l}", comm)
        landed(keys, got)
        saved.append((xc, z, h, ya, lse_a, yb, yc, lse_c, sink_l, y, x1, a, h2))
        xc = x2

    loss_slab, dx, dxb, dg_final = _loss_head(xc, g_final.reshape(1, D_MODEL), target, "loss_head")

    def by_owner(t):
        return t.reshape(4, 2, t.shape[0] // N_DEV, D_MODEL)

    def pair(key, g, r1):
        return _pair_sum(g, r1, pos, f"grad_pair_sum_{key[0]}_{key[1]}")

    partial, r2 = {}, {}
    dg_mix, dg_group, dg_mlp, dconv, dsinks = [None] * DEPTH, [None] * DEPTH, [None] * DEPTH, [None] * DEPTH, [None] * DEPTH
    for l in reversed(range(DEPTH)):
        xin, z, h, ya, lse_a, yb, yc, lse_c, sink_l, y, x1, a, h2 = saved[l]
        above = (l + 1, 0)
        (du,), got = _mlp_bwd_act(dxb, full[l, 3], a, f"ff_out_bwd_{l}",
                                  _chip_comm([partial[above]]) if above in partial else None)
        if got:
            r2[above] = got[0]
        (g3,), _ = _mm_tn(a, dxb, f"grad_w_ff_out_{l}")
        g3 = by_owner(g3)
        (g2,), got = _mm_tn(du, h2, f"grad_w_ff_in_{l}", _sibling_comm([g3]))
        g2 = by_owner(g2)
        partial[l, 3] = pair((l, 3), g3, got[0])
        (dx1, dx1b, dg_mlp[l]), got = _mm_nn_normbwd(du, full[l, 2], x1, dx, g_mlp[l:l + 1], f"ff_in_bwd_{l}",
                                                    _join(_chip_comm([partial[l, 3]]), _sibling_comm([g2])))
        r2[l, 3] = got[0]
        partial[l, 2] = pair((l, 2), g2, got[1])
        (g1,), _ = _mm_tn(y, dx1b, f"grad_w_o_{l}")
        g1 = by_owner(g1)
        (dya, dyb, dyc, dg_group[l]), got = _mix_bwd(dx1b, full[l, 1], ya, yb, yc, g_group[l:l + 1],
                                                     f"mix_out_bwd_{l}", _sibling_comm([g1]))
        partial[l, 1] = pair((l, 1), g1, got[0])
        (dqa, dka, dva, _), got = _attn_bwd(z, dya, ya, lse_a, no_sink, QA_BLK, KA_BLK, VA_BLK, DILATED_PATTERNS,
                                            A_MAX_DIST, False, f"dilated_attn_bwd_{l}",
                                            _chip_comm([partial[l, 2], partial[l, 1]]))
        r2[l, 2], r2[l, 1] = got
        dgate_b, dgate_c, dconv_x, dcw = _conv_bwd(z, conv_full[l], dyb, f"conv_bwd_{l}")
        (dqc, dkc, dvc, dsink), _ = _attn_bwd(z, dyc, yc, lse_c, sink_l, QC_BLK, KC_BLK, VC_BLK, (1,), C_MAX_DIST,
                                              True, f"window_attn_bwd_{l}")
        dz = jnp.concatenate([dqa, dka, dva, dgate_b, dgate_c, dconv_x, dqc, dkc, dvc], axis=1)
        (g0,), _ = _mm_tn(dz, h, f"grad_w_in_{l}")
        g0 = by_owner(g0)
        if l > 0:
            (dx, dxb, dg_mix[l]), got = _mm_nn_normbwd(dz, full[l, 0], xin, dx1, g_mix[l:l + 1], f"in_proj_bwd_{l}",
                                                      _sibling_comm([g0]))
            partial[l, 0] = pair((l, 0), g0, got[0])
        else:
            (r1,) = _comm_only(_sibling_comm([g0]), "grad_sibling_exchange_last")
            partial[l, 0] = pair((l, 0), g0, r1)
            (dx, dxb, dg_mix[l]), got = _mm_nn_normbwd(dz, full[l, 0], xin, dx1, g_mix[l:l + 1], f"in_proj_bwd_{l}",
                                                      _chip_comm([partial[l, 0]]))
            r2[l, 0] = got[0]
        dconv[l] = dcw[:3]
        dsinks[l] = dsink[0, ::HEAD_DIM]
    gsum = {key: _final_sum(partial[key], r2[key], pos, f"grad_final_sum_{key[0]}_{key[1]}") for key in partial}
    small = _pack_small(jnp.concatenate(dg_mix), jnp.concatenate(dg_group), jnp.concatenate(dg_mlp),
                        dg_final, jnp.stack(dconv), jnp.stack(dsinks))
    return loss_slab, dx, gsum, small


def _finish(loss, dx, gsum, small, dev, w_in, conv_w, sinks, g_mix, g_group, w_o, g_mlp, w_ff_in, w_ff_out, g_final, m_w_in, m_conv_w, m_sinks, m_g_mix, m_g_group, m_w_o, m_g_mlp, m_w_ff_in, m_w_ff_out, m_g_final, v_w_in, v_conv_w, v_sinks, v_g_mix, v_g_group, v_w_o, v_g_mlp, v_w_ff_in, v_w_ff_out, v_g_final):
    grad_x = dx.reshape(1, SEQ, D_MODEL)
    grad_w_in = jnp.stack([gsum[l, 0].T for l in range(DEPTH)])
    grad_w_o = jnp.stack([gsum[l, 1] for l in range(DEPTH)])
    grad_w_ff_in = jnp.stack([gsum[l, 2].T for l in range(DEPTH)])
    grad_w_ff_out = jnp.stack([gsum[l, 3] for l in range(DEPTH)])

    (small_all,) = _comm_only(_gather_comm([small]), "gather_small_grads")
    zeros_conv = jnp.zeros((DEPTH, 3, CONV_CH), F32)
    sw = _pack_small(g_mix, g_group, g_mlp, g_final, zeros_conv, sinks)
    sm = _pack_small(m_g_mix, m_g_group, m_g_mlp, m_g_final, zeros_conv, m_sinks)
    sv = _pack_small(v_g_mix, v_g_group, v_g_mlp, v_g_final, zeros_conv, v_sinks)
    sg, sd, snm, snv = _small_sum_adamw(small_all.reshape(N_DEV, SMALL_ROWS, D_MODEL), sw, sm, sv, "small_adamw")
    grad_g_mix, grad_g_group, grad_g_mlp, grad_g_final, conv_grad_full, grad_sinks = _unpack_small(sg)
    delta_g_mix, delta_g_group, delta_g_mlp, delta_g_final, _, delta_sinks = _unpack_small(sd)
    new_m_g_mix, new_m_g_group, new_m_g_mlp, new_m_g_final, _, new_m_sinks = _unpack_small(snm)
    new_v_g_mix, new_v_g_group, new_v_g_mlp, new_v_g_final, _, new_v_sinks = _unpack_small(snv)
    cs = CONV_CH // N_DEV
    grad_conv_w = lax.dynamic_slice_in_dim(conv_grad_full, dev * cs, cs, axis=2)

    def tile_of(t):
        return jnp.pad(t.reshape(1, DEPTH * 3 * cs), ((0, 7), (0, 256 - DEPTH * 3 * cs)))

    cd, cm, cv = _adamw(tile_of(conv_w), tile_of(grad_conv_w), tile_of(m_conv_w), tile_of(v_conv_w), "conv_adamw")
    untile = lambda t: t[0, :DEPTH * 3 * cs].reshape(DEPTH, 3, cs)
    delta_conv_w, new_m_conv_w, new_v_conv_w = untile(cd), untile(cm), untile(cv)

    def big(w, g, m, v, name):
        shp = w.shape
        flat = lambda t: t.reshape(shp[0] * shp[1], shp[2])
        return [t.reshape(shp) for t in _adamw(flat(w), flat(g), flat(m), flat(v), name)]

    delta_w_in, new_m_w_in, new_v_w_in = big(w_in, grad_w_in, m_w_in, v_w_in, "adamw_w_in")
    delta_w_o, new_m_w_o, new_v_w_o = big(w_o, grad_w_o, m_w_o, v_w_o, "adamw_w_o")
    delta_w_ff_in, new_m_w_ff_in, new_v_w_ff_in = big(w_ff_in, grad_w_ff_in, m_w_ff_in, v_w_ff_in, "adamw_w_ff_in")
    delta_w_ff_out, new_m_w_ff_out, new_v_w_ff_out = big(w_ff_out, grad_w_ff_out, m_w_ff_out, v_w_ff_out, "adamw_w_ff_out")

    return (loss, grad_x, grad_w_in, grad_conv_w, grad_sinks, grad_g_mix, grad_g_group, grad_w_o, grad_g_mlp,
            grad_w_ff_in, grad_w_ff_out, grad_g_final,
            delta_w_in, delta_conv_w, delta_sinks, delta_g_mix, delta_g_group, delta_w_o, delta_g_mlp,
            delta_w_ff_in, delta_w_ff_out, delta_g_final,
            new_m_w_in, new_m_conv_w, new_m_sinks, new_m_g_mix, new_m_g_group, new_m_w_o, new_m_g_mlp,
            new_m_w_ff_in, new_m_w_ff_out, new_m_g_final,
            new_v_w_in, new_v_conv_w, new_v_sinks, new_v_g_mix, new_v_g_group, new_v_w_o, new_v_g_mlp,
            new_v_w_ff_in, new_v_w_ff_out, new_v_g_final)
```

```python
from typing import Callable, NamedTuple

import jax
import jax.numpy as jnp
from jax import lax
from jax.experimental import pallas as pl
from jax.experimental.pallas import tpu as pltpu

F32 = jnp.float32
BF16 = jnp.bfloat16
MESH = pl.DeviceIdType.MESH

N_DEV = 8
SEQ = 4096
D_MODEL = 1024
DEPTH = 2
HEAD_DIM = 64
LANES = 128
A_WIDTH = 384
CONV_CH = 256
C_WIDTH = 384
KV_WIDTH = 128
IN_WIDTH = 2560
D_FF = 4096
BLOCK = 128
DILATED_PATTERNS = (1, 4, 16)
A_MAX_DIST = 128
C_MAX_DIST = 127
EPS = 1e-6
SCALE = HEAD_DIM ** -0.5
NEG_BIG = -1e30

QA_BLK, KA_BLK, VA_BLK = 0, 3, 6
GB_BLK, GC_BLK, XB_BLK = 9, 11, 13
QC_BLK, KC_BLK, VC_BLK = 15, 18, 19

ADAM_LR = 0.001
ADAM_B1 = 0.9
ADAM_B2 = 0.999
ADAM_EPS = 1e-08
ADAM_WD = 0.01
ADAM_STEP = 10

VMEM_LIMIT = 56 * 1024 * 1024
ROW_TILE = 512
COL_CHUNK = 512
SMALL_ROWS = 48


def _dot_nn(a, b):
    return lax.dot_general(a, b, (((1,), (0,)), ((), ())), preferred_element_type=F32)


def _dot_nt(a, b):
    return lax.dot_general(a, b, (((1,), (1,)), ((), ())), preferred_element_type=F32)


def _dot_tn(a, b):
    return lax.dot_general(a, b, (((0,), (0,)), ((), ())), preferred_element_type=F32)


def _params(*sem):
    return pltpu.CompilerParams(dimension_semantics=sem, vmem_limit_bytes=VMEM_LIMIT)


def _rms_scale(t):
    return lax.rsqrt(jnp.mean(t * t, axis=-1, keepdims=True) + EPS)


def _rms_bwd(n, r, dn):
    return r * (dn - n * jnp.mean(dn * n, axis=-1, keepdims=True))


class _Comm(NamedTuple):
    arrays: tuple
    out_shape: tuple
    sems: tuple
    start: Callable
    finish: Callable


def _join(*comms):
    comms = [c for c in comms if c is not None]
    if not comms:
        return None

    def run(which):
        def f(ins, outs, sems):
            i = o = s = 0
            for c in comms:
                ni, no, ns = len(c.arrays), len(c.out_shape), len(c.sems)
                getattr(c, which)(ins[i:i + ni], outs[o:o + no], sems[s:s + ns])
                i, o, s = i + ni, o + no, s + ns
        return f

    return _Comm(sum((tuple(c.arrays) for c in comms), ()), sum((tuple(c.out_shape) for c in comms), ()),
                 sum((tuple(c.sems) for c in comms), ()), run("start"), run("finish"))


def _call(body, grid, in_specs, out_specs, out_shape, operands, name, scratch_shapes=(), comm=None, aliases=None):
    n_in, n_out, n_scr = len(in_specs), len(out_shape), len(scratch_shapes)
    aliases = dict(aliases or {})
    if comm is None:
        res = pl.pallas_call(body, grid=grid, in_specs=list(in_specs), out_specs=list(out_specs),
                             out_shape=list(out_shape), scratch_shapes=list(scratch_shapes),
                             input_output_aliases=aliases,
                             compiler_params=_params("arbitrary"), name=name)(*operands)
        return list(res), []
    c_in, c_out = len(comm.arrays), len(comm.out_shape)
    hbm = pl.BlockSpec(memory_space=pl.ANY)
    last = grid[0] - 1

    def carried(*refs):
        ins, cins = refs[:n_in], refs[n_in:n_in + c_in]
        o0 = n_in + c_in
        outs, couts = refs[o0:o0 + n_out], refs[o0 + n_out:o0 + n_out + c_out]
        s0 = o0 + n_out + c_out
        scr, sems = refs[s0:s0 + n_scr], refs[s0 + n_scr:]
        pl.when(pl.program_id(0) == 0)(lambda: comm.start(cins, couts, sems))
        body(*ins, *outs, *scr)
        pl.when(pl.program_id(0) == last)(lambda: comm.finish(cins, couts, sems))

    res = pl.pallas_call(carried, grid=grid, in_specs=list(in_specs) + [hbm] * c_in,
                         out_specs=list(out_specs) + [hbm] * c_out, out_shape=list(out_shape) + list(comm.out_shape),
                         scratch_shapes=list(scratch_shapes) + list(comm.sems), input_output_aliases=aliases,
                         compiler_params=_params("arbitrary"), name=name)(*operands, *comm.arrays)
    return list(res[:n_out]), list(res[n_out:])


def _comm_only(comm, name):
    hbm = pl.BlockSpec(memory_space=pl.ANY)
    c_in, c_out = len(comm.arrays), len(comm.out_shape)

    def body(*refs):
        ins, outs, sems = refs[:c_in], refs[c_in:c_in + c_out], refs[c_in + c_out:]
        comm.start(ins, outs, sems)
        comm.finish(ins, outs, sems)

    return pl.pallas_call(body, in_specs=[hbm] * c_in, out_specs=[hbm] * c_out, out_shape=list(comm.out_shape),
                          scratch_shapes=list(comm.sems), name=name)(*comm.arrays)


def _norm_mm(x, g, wt, relu2, name, comm=None):
    s, d = x.shape
    n = wt.shape[0]
    tm = ROW_TILE

    def body(x_ref, g_ref, w_ref, o_ref, h_ref):
        xx = x_ref[...]
        h = ((xx * _rms_scale(xx)) * g_ref[...]).astype(BF16)
        h_ref[...] = h
        for n0 in range(0, n, COL_CHUNK):
            zc = _dot_nt(h, w_ref[n0:n0 + COL_CHUNK, :])
            if relu2:
                zc = jnp.square(jnp.maximum(zc, 0.0)).astype(BF16)
            o_ref[:, n0:n0 + COL_CHUNK] = zc

    return _call(
        body,
        grid=(s // tm,),
        in_specs=[pl.BlockSpec((tm, d), lambda i: (i, 0)),
                  pl.BlockSpec((1, d), lambda i: (0, 0)),
                  pl.BlockSpec((n, d), lambda i: (0, 0))],
        out_specs=[pl.BlockSpec((tm, n), lambda i: (i, 0)),
                   pl.BlockSpec((tm, d), lambda i: (i, 0))],
        out_shape=[jax.ShapeDtypeStruct((s, n), BF16 if relu2 else F32), jax.ShapeDtypeStruct((s, d), BF16)],
        operands=(x, g, wt), name=name, comm=comm)


def _mm_res(a, w2, x1, name, comm=None):
    s, f = a.shape
    d = w2.shape[1]
    tm = ROW_TILE

    def body(a_ref, w_ref, x_ref, o_ref):
        o_ref[...] = x_ref[...] + _dot_nn(a_ref[...], w_ref[...])

    return _call(
        body,
        grid=(s // tm,),
        in_specs=[pl.BlockSpec((tm, f), lambda i: (i, 0)),
                  pl.BlockSpec((f, d), lambda i: (0, 0)),
                  pl.BlockSpec((tm, d), lambda i: (i, 0))],
        out_specs=[pl.BlockSpec((tm, d), lambda i: (i, 0))],
        out_shape=[jax.ShapeDtypeStruct((s, d), F32)],
        operands=(a, w2, x1), name=name, comm=comm)


def _mix_out(ya, yb, yc, gg, wo, x0, name, comm=None):
    s = ya.shape[0]
    d = wo.shape[1]
    tm = ROW_TILE

    def body(ya_ref, yb_ref, yc_ref, g_ref, w_ref, x_ref, y_ref, o_ref):
        parts = []
        for ref in (ya_ref, yb_ref, yc_ref):
            t = ref[...]
            parts.append(t * _rms_scale(t))
        y = (jnp.concatenate(parts, axis=1) * g_ref[...]).astype(BF16)
        y_ref[...] = y
        o_ref[...] = x_ref[...] + _dot_nn(y, w_ref[...])

    return _call(
        body,
        grid=(s // tm,),
        in_specs=[pl.BlockSpec((tm, A_WIDTH), lambda i: (i, 0)),
                  pl.BlockSpec((tm, CONV_CH), lambda i: (i, 0)),
                  pl.BlockSpec((tm, C_WIDTH), lambda i: (i, 0)),
                  pl.BlockSpec((1, d), lambda i: (0, 0)),
                  pl.BlockSpec((d, d), lambda i: (0, 0)),
                  pl.BlockSpec((tm, d), lambda i: (i, 0))],
        out_specs=[pl.BlockSpec((tm, d), lambda i: (i, 0)),
                   pl.BlockSpec((tm, d), lambda i: (i, 0))],
        out_shape=[jax.ShapeDtypeStruct((s, d), BF16), jax.ShapeDtypeStruct((s, d), F32)],
        operands=(ya, yb, yc, gg, wo, x0), name=name, comm=comm)


def _loss_head(x, g, target, name):
    s, d = x.shape
    tm = ROW_TILE

    def body(x_ref, g_ref, t_ref, loss_ref, dx_ref, dxb_ref, dg_ref):
        @pl.when(pl.program_id(0) == 0)
        def _():
            loss_ref[...] = jnp.zeros_like(loss_ref)
            dg_ref[...] = jnp.zeros_like(dg_ref)

        xx = x_ref[...]
        r = _rms_scale(xx)
        n = xx * r
        gv = g_ref[...]
        err = n * gv - t_ref[...]
        per_tok = jnp.sum(err * err, axis=1, keepdims=True) * (1.0 / d)
        loss_ref[...] += 0.5 * jnp.sum(per_tok, axis=0, keepdims=True)
        dout = err * (1.0 / d)
        dg_ref[...] += jnp.sum(dout * n, axis=0, keepdims=True)
        dx = _rms_bwd(n, r, dout * gv)
        dx_ref[...] = dx
        dxb_ref[...] = dx.astype(BF16)

    return pl.pallas_call(
        body,
        grid=(s // tm,),
        in_specs=[pl.BlockSpec((tm, d), lambda i: (i, 0)),
                  pl.BlockSpec((1, d), lambda i: (0, 0)),
                  pl.BlockSpec((tm, d), lambda i: (i, 0))],
        out_specs=[pl.BlockSpec((8, LANES), lambda i: (0, 0)),
                   pl.BlockSpec((tm, d), lambda i: (i, 0)),
                   pl.BlockSpec((tm, d), lambda i: (i, 0)),
                   pl.BlockSpec((1, d), lambda i: (0, 0))],
        out_shape=[jax.ShapeDtypeStruct((8, LANES), F32), jax.ShapeDtypeStruct((s, d), F32),
                   jax.ShapeDtypeStruct((s, d), BF16), jax.ShapeDtypeStruct((1, d), F32)],
        compiler_params=_params("arbitrary"),
        name=name,
    )(x, g, target)


def _mlp_bwd_act(dxb, w2, a, name, comm=None):
    s, d = dxb.shape
    f = w2.shape[0]
    tm = ROW_TILE

    def body(dx_ref, w_ref, a_ref, du_ref):
        dx = dx_ref[...]
        for n0 in range(0, f, COL_CHUNK):
            da = _dot_nt(dx, w_ref[n0:n0 + COL_CHUNK, :])
            rl = jnp.sqrt(a_ref[:, n0:n0 + COL_CHUNK].astype(F32))
            du_ref[:, n0:n0 + COL_CHUNK] = (da * (2.0 * rl)).astype(BF16)

    return _call(
        body,
        grid=(s // tm,),
        in_specs=[pl.BlockSpec((tm, d), lambda i: (i, 0)),
                  pl.BlockSpec((f, d), lambda i: (0, 0)),
                  pl.BlockSpec((tm, f), lambda i: (i, 0))],
        out_specs=[pl.BlockSpec((tm, f), lambda i: (i, 0))],
        out_shape=[jax.ShapeDtypeStruct((s, f), BF16)],
        operands=(dxb, w2, a), name=name, comm=comm)


def _mm_tn(a, b, name, comm=None):
    s, n = a.shape
    d = b.shape[1]
    tn = 512

    def body(a_ref, b_ref, o_ref, acc):
        for k0 in range(0, s, ROW_TILE):
            part = _dot_tn(a_ref[k0:k0 + ROW_TILE, :], b_ref[k0:k0 + ROW_TILE, :])
            if k0 == 0:
                acc[...] = part
            else:
                acc[...] += part
        o_ref[...] = acc[...].astype(BF16)

    return _call(
        body,
        grid=(n // tn,),
        in_specs=[pl.BlockSpec((s, tn), lambda j: (0, j)),
                  pl.BlockSpec((s, d), lambda j: (0, 0))],
        out_specs=[pl.BlockSpec((tn, d), lambda j: (j, 0))],
        out_shape=[jax.ShapeDtypeStruct((n, d), BF16)],
        operands=(a, b), name=name, scratch_shapes=[pltpu.VMEM((tn, d), F32)], comm=comm)


def _mm_nn_normbwd(dact, wt, x, dres, g, name, comm=None):
    s, kdim = dact.shape
    d = wt.shape[1]
    tm = ROW_TILE

    def body(a_ref, w_ref, x_ref, r_ref, g_ref, o_ref, ob_ref, dg_ref):
        @pl.when(pl.program_id(0) == 0)
        def _():
            dg_ref[...] = jnp.zeros_like(dg_ref)

        dh = _dot_nn(a_ref[...], w_ref[...])
        xx = x_ref[...]
        r = _rms_scale(xx)
        n = xx * r
        dg_ref[...] += jnp.sum(dh * n, axis=0, keepdims=True)
        dx = r_ref[...] + _rms_bwd(n, r, dh * g_ref[...])
        o_ref[...] = dx
        ob_ref[...] = dx.astype(BF16)

    return _call(
        body,
        grid=(s // tm,),
        in_specs=[pl.BlockSpec((tm, kdim), lambda i: (i, 0)),
                  pl.BlockSpec((kdim, d), lambda i: (0, 0)),
                  pl.BlockSpec((tm, d), lambda i: (i, 0)),
                  pl.BlockSpec((tm, d), lambda i: (i, 0)),
                  pl.BlockSpec((1, d), lambda i: (0, 0))],
        out_specs=[pl.BlockSpec((tm, d), lambda i: (i, 0)),
                   pl.BlockSpec((tm, d), lambda i: (i, 0)),
                   pl.BlockSpec((1, d), lambda i: (0, 0))],
        out_shape=[jax.ShapeDtypeStruct((s, d), F32), jax.ShapeDtypeStruct((s, d), BF16),
                   jax.ShapeDtypeStruct((1, d), F32)],
        operands=(dact, wt, x, dres, g), name=name, comm=comm)


def _mix_bwd(dx1, wo, ya, yb, yc, gg, name, comm=None):
    s, d = dx1.shape
    tm = ROW_TILE
    widths = (A_WIDTH, CONV_CH, C_WIDTH)

    def body(dx_ref, w_ref, ya_ref, yb_ref, yc_ref, g_ref, da_ref, db_ref, dc_ref, dg_ref):
        @pl.when(pl.program_id(0) == 0)
        def _():
            dg_ref[...] = jnp.zeros_like(dg_ref)

        dy = _dot_nt(dx_ref[...], w_ref[...])
        gv = g_ref[...]
        off = 0
        dgs = []
        for ref, out, w in zip((ya_ref, yb_ref, yc_ref), (da_ref, db_ref, dc_ref), widths):
            t = ref[...]
            r = _rms_scale(t)
            n = t * r
            dyg = dy[:, off:off + w]
            dgs.append(jnp.sum(dyg * n, axis=0, keepdims=True))
            out[...] = _rms_bwd(n, r, dyg * gv[:, off:off + w])
            off += w
        dg_ref[...] += jnp.concatenate(dgs, axis=1)

    return _call(
        body,
        grid=(s // tm,),
        in_specs=[pl.BlockSpec((tm, d), lambda i: (i, 0)),
                  pl.BlockSpec((d, d), lambda i: (0, 0)),
                  pl.BlockSpec((tm, A_WIDTH), lambda i: (i, 0)),
                  pl.BlockSpec((tm, CONV_CH), lambda i: (i, 0)),
                  pl.BlockSpec((tm, C_WIDTH), lambda i: (i, 0)),
                  pl.BlockSpec((1, d), lambda i: (0, 0))],
        out_specs=[pl.BlockSpec((tm, A_WIDTH), lambda i: (i, 0)),
                   pl.BlockSpec((tm, CONV_CH), lambda i: (i, 0)),
                   pl.BlockSpec((tm, C_WIDTH), lambda i: (i, 0)),
                   pl.BlockSpec((1, d), lambda i: (0, 0))],
        out_shape=[jax.ShapeDtypeStruct((s, A_WIDTH), F32), jax.ShapeDtypeStruct((s, CONV_CH), F32),
                   jax.ShapeDtypeStruct((s, C_WIDTH), F32), jax.ShapeDtypeStruct((1, d), F32)],
        operands=(dx1, wo, ya, yb, yc, gg), name=name, comm=comm)


CONV_CHUNK = 256
CONV_HALO = 8


def _conv_fwd(z, cw, name):
    s = z.shape[0]
    nch = s // CONV_CHUNK

    def body(gb_ref, gc_ref, xb_ref, w_ref, o_ref, us):
        us[pl.ds(0, CONV_HALO), :] = jnp.zeros((CONV_HALO, LANES), F32)
        us[pl.ds(CONV_HALO, s), :] = gc_ref[...] * xb_ref[...]
        w0, w1, w2 = w_ref[0:1, :], w_ref[1:2, :], w_ref[2:3, :]

        def chunk(c, carry):
            st = pl.multiple_of(c * CONV_CHUNK, CONV_CHUNK)
            ext = us[pl.ds(st, CONV_CHUNK + CONV_HALO), :]
            y = (w0 * ext[CONV_HALO - 2:CONV_HALO - 2 + CONV_CHUNK]
                 + w1 * ext[CONV_HALO - 1:CONV_HALO - 1 + CONV_CHUNK]
                 + w2 * ext[CONV_HALO:])
            o_ref[pl.ds(st, CONV_CHUNK), :] = gb_ref[pl.ds(st, CONV_CHUNK), :] * y
            return carry

        lax.fori_loop(0, nch, chunk, 0)

    col = lambda blk: pl.BlockSpec((s, LANES), lambda j, blk=blk: (0, blk + j))
    return pl.pallas_call(
        body,
        grid=(CONV_CH // LANES,),
        in_specs=[col(GB_BLK), col(GC_BLK), col(XB_BLK), pl.BlockSpec((3, LANES), lambda j: (0, j))],
        out_specs=pl.BlockSpec((s, LANES), lambda j: (0, j)),
        out_shape=jax.ShapeDtypeStruct((s, CONV_CH), F32),
        scratch_shapes=[pltpu.VMEM((s + CONV_HALO, LANES), F32)],
        compiler_params=_params("parallel"),
        name=name,
    )(z, z, z, cw)


def _conv_bwd(z, cw, dyb, dz, name):
    s = z.shape[0]
    nch = s // CONV_CHUNK
    ncol = CONV_CH // LANES

    def body(gb_ref, gc_ref, xb_ref, w_ref, dy_ref, dz_in, dz_ref, dw_ref, us, ds_, dgb_ref, dgc_ref, dxb_ref, sems):
        j = pl.program_id(0)

        def to_dz(staged, blk, k):
            cols = pl.ds(pl.multiple_of((blk + j) * LANES, LANES), LANES)
            return pltpu.make_async_copy(staged, dz_ref.at[:, cols], sems.at[k])

        copies = [to_dz(dgb_ref, GB_BLK, 0), to_dz(dgc_ref, GC_BLK, 1), to_dz(dxb_ref, XB_BLK, 2)]

        @pl.when(j > 0)
        def _():
            for cp in copies:
                cp.wait()

        us[pl.ds(0, CONV_HALO), :] = jnp.zeros((CONV_HALO, LANES), F32)
        us[pl.ds(CONV_HALO, s), :] = gc_ref[...] * xb_ref[...]
        ds_[pl.ds(s, CONV_HALO), :] = jnp.zeros((CONV_HALO, LANES), F32)
        ds_[pl.ds(0, s), :] = dy_ref[...] * gb_ref[...]
        w0, w1, w2 = w_ref[0:1, :], w_ref[1:2, :], w_ref[2:3, :]
        zero = jnp.zeros((1, LANES), F32)

        def chunk(c, carry):
            a0, a1, a2 = carry
            st = pl.multiple_of(c * CONV_CHUNK, CONV_CHUNK)
            rows = pl.ds(st, CONV_CHUNK)
            ext = us[pl.ds(st, CONV_CHUNK + CONV_HALO), :]
            um2 = ext[CONV_HALO - 2:CONV_HALO - 2 + CONV_CHUNK]
            um1 = ext[CONV_HALO - 1:CONV_HALO - 1 + CONV_CHUNK]
            u0 = ext[CONV_HALO:]
            dext = ds_[pl.ds(st, CONV_CHUNK + CONV_HALO), :]
            dc0 = dext[:CONV_CHUNK]
            du = w2 * dc0 + w1 * dext[1:1 + CONV_CHUNK] + w0 * dext[2:2 + CONV_CHUNK]
            yconv = w0 * um2 + w1 * um1 + w2 * u0
            dgb_ref[rows, :] = (dy_ref[rows, :] * yconv).astype(BF16)
            dgc_ref[rows, :] = (du * xb_ref[rows, :]).astype(BF16)
            dxb_ref[rows, :] = (du * gc_ref[rows, :]).astype(BF16)
            a0 = a0 + jnp.sum(dc0 * um2, axis=0, keepdims=True)
            a1 = a1 + jnp.sum(dc0 * um1, axis=0, keepdims=True)
            a2 = a2 + jnp.sum(dc0 * u0, axis=0, keepdims=True)
            return a0, a1, a2

        a0, a1, a2 = lax.fori_loop(0, nch, chunk, (zero, zero, zero))
        dw_ref[...] = jnp.concatenate([a0, a1, a2, jnp.zeros((5, LANES), F32)], axis=0)
        for cp in copies:
            cp.start()

        @pl.when(j == ncol - 1)
        def _():
            for cp in copies:
                cp.wait()

    col = lambda blk: pl.BlockSpec((s, LANES), lambda j, blk=blk: (0, blk + j))
    hbm = pl.BlockSpec(memory_space=pl.ANY)
    return pl.pallas_call(
        body,
        grid=(ncol,),
        in_specs=[col(GB_BLK), col(GC_BLK), col(XB_BLK), pl.BlockSpec((3, LANES), lambda j: (0, j)),
                  pl.BlockSpec((s, LANES), lambda j: (0, j)), hbm],
        out_specs=[hbm, pl.BlockSpec((8, LANES), lambda j: (0, j))],
        out_shape=[jax.ShapeDtypeStruct(dz.shape, dz.dtype), jax.ShapeDtypeStruct((8, CONV_CH), F32)],
        scratch_shapes=[pltpu.VMEM((s + CONV_HALO, LANES), F32), pltpu.VMEM((s + CONV_HALO, LANES), F32)]
        + [pltpu.VMEM((s, LANES), BF16)] * 3 + [pltpu.SemaphoreType.DMA((3,))],
        input_output_aliases={5: 0},
        compiler_params=_params("arbitrary"),
        name=name,
    )(z, z, z, cw, dyb, dz)


ATTN_ROWS = 512
ATTN_UNROLL = 8


def _band_rows(b, d, r):
    base = pl.multiple_of(b * (BLOCK * d), BLOCK)
    prev = jnp.maximum(base - BLOCK * d, 0)
    if d == 1:
        return pl.ds(base, BLOCK), pl.ds(pl.multiple_of(prev, BLOCK), BLOCK)
    return pl.ds(base + r, BLOCK, stride=d), pl.ds(prev + r, BLOCK, stride=d)


def _write_band_bias(bias_ref, max_dist):
    qi = lax.broadcasted_iota(jnp.int32, (BLOCK, 2 * BLOCK), 0)
    kj = lax.broadcasted_iota(jnp.int32, (BLOCK, 2 * BLOCK), 1)
    dist = BLOCK + qi - kj
    band = (dist >= 0) & (dist <= max_dist)
    bias_ref[0:BLOCK, :] = jnp.where(band, 0.0, -jnp.inf)
    bias_ref[BLOCK:2 * BLOCK, :] = jnp.where(band & (kj >= BLOCK), 0.0, -jnp.inf)


def _band_bias(bias_ref, b):
    bias = bias_ref[pl.ds(pl.multiple_of(jnp.where(b > 0, 0, BLOCK), BLOCK), BLOCK), :]
    return jnp.concatenate([bias, bias], axis=0)


def _lane_half():
    return (lax.broadcasted_iota(jnp.int32, (1, LANES), 1) >= HEAD_DIM).astype(jnp.int32)


def _kv_for_pair(t, pair):
    half = _lane_half()
    want = (pair + half) >> 1
    return jnp.where(want != half, pltpu.roll(t, HEAD_DIM, 1), t)


def _kv_grad_from_pair(t, pair):
    half = _lane_half()
    mine = ((pair + half) >> 1) == half
    other = ((pair + 1 - half) >> 1) == half
    fold = t + pltpu.roll(t, HEAD_DIM, 1)
    return jnp.where(mine & other, fold, jnp.where(mine, t, 0.0))


def _stack_heads(t, head0):
    zero = jnp.zeros_like(t)
    return jnp.concatenate([jnp.where(head0, t, zero), jnp.where(head0, zero, t)], axis=0)


def _unstack_heads(t, head0):
    return jnp.where(head0, t[:BLOCK], t[BLOCK:])


def _block_loops(s, patterns, unroll, one_block):
    for d in patterns:
        nb = (s // BLOCK) // d
        ur = min(unroll, d)
        ub = unroll // ur
        for r0 in range(0, d, ur):
            def trip(i, carry, d=d, r0=r0, ur=ur, ub=ub):
                for u in range(ub):
                    for r in range(r0, r0 + ur):
                        one_block(i * ub + u, d, r)
                return carry
            lax.fori_loop(0, nb // ub, trip, 0)


def _attn_fwd(z, m_init, l_init, q_blk, k_blk, v_blk, patterns, max_dist, gqa, name, comm=None):
    s = z.shape[0]
    npair = 3

    def body(q_ref, k_ref, v_ref, mi_ref, o_ref, lse_ref, m_scr, l_scr, bias_scr, *kv_scr):
        pair = pl.program_id(0)
        head0 = lax.broadcasted_iota(jnp.int32, (1, LANES), 1) < HEAD_DIM
        _write_band_bias(bias_scr, max_dist)
        k_src, v_src = kv_scr if gqa else (k_ref, v_ref)

        def init(c, carry):
            rows = pl.ds(pl.multiple_of(c * ATTN_ROWS, ATTN_ROWS), ATTN_ROWS)
            m_scr[rows, :] = jnp.broadcast_to(mi_ref[...], (ATTN_ROWS, LANES))
            l_scr[rows, :] = jnp.full((ATTN_ROWS, LANES), l_init, F32)
            o_ref[rows, :] = jnp.zeros((ATTN_ROWS, LANES), F32)
            if gqa:
                k_src[rows, :] = _kv_for_pair(k_ref[rows, :], pair)
                v_src[rows, :] = _kv_for_pair(v_ref[rows, :], pair)
            return carry

        lax.fori_loop(0, s // ATTN_ROWS, init, 0)
        ones = jnp.ones((2 * BLOCK, LANES), BF16)

        def one_block(b, d, r):
            rq, rp = _band_rows(b, d, r)
            q2 = _stack_heads((q_ref[rq, :] * SCALE).astype(BF16), head0)
            k2 = jnp.concatenate([k_src[rp, :], k_src[rq, :]], axis=0)
            v2 = jnp.concatenate([v_src[rp, :], v_src[rq, :]], axis=0)
            sc = _dot_nt(q2, k2.astype(BF16)) + _band_bias(bias_scr, b)
            mb = jnp.max(sc, axis=1, keepdims=True)
            p = jnp.exp(sc - mb).astype(BF16)
            ob = _dot_nn(p, jnp.concatenate([v2.astype(BF16), ones], axis=1))
            m2 = _unstack_heads(jnp.broadcast_to(mb, (2 * BLOCK, LANES)), head0)
            l2 = _unstack_heads(ob[:, LANES:], head0)
            o2 = _unstack_heads(ob[:, :LANES], head0)
            m_old = m_scr[rq, :]
            m_new = jnp.maximum(m_old, m2)
            a_old = jnp.exp(m_old - m_new)
            a_blk = jnp.exp(m2 - m_new)
            o_ref[rq, :] = o_ref[rq, :] * a_old + o2 * a_blk
            l_scr[rq, :] = l_scr[rq, :] * a_old + l2 * a_blk
            m_scr[rq, :] = m_new

        _block_loops(s, patterns, ATTN_UNROLL, one_block)

        def fin(c, carry):
            rows = pl.ds(pl.multiple_of(c * ATTN_ROWS, ATTN_ROWS), ATTN_ROWS)
            l = l_scr[rows, :]
            o_ref[rows, :] = o_ref[rows, :] / l
            lse = m_scr[rows, :] + jnp.log(l)
            swapped = pltpu.roll(lse, HEAD_DIM, 1)
            lse_ref[rows, 0:LANES] = jnp.where(head0, lse, swapped)
            lse_ref[rows, LANES:2 * LANES] = jnp.where(head0, swapped, lse)
            return carry

        lax.fori_loop(0, s // ATTN_ROWS, fin, 0)

    kv = (lambda blk: pl.BlockSpec((s, LANES), lambda j, blk=blk: (0, blk), pipeline_mode=pl.Buffered(1))) if gqa \
        else (lambda blk: pl.BlockSpec((s, LANES), lambda j, blk=blk: (0, blk + j)))
    return _call(
        body,
        grid=(npair,),
        in_specs=[pl.BlockSpec((s, LANES), lambda j: (0, q_blk + j)), kv(k_blk), kv(v_blk),
                  pl.BlockSpec((1, LANES), lambda j: (0, j))],
        out_specs=[pl.BlockSpec((s, LANES), lambda j: (0, j)), pl.BlockSpec((s, 2 * LANES), lambda j: (0, j))],
        out_shape=[jax.ShapeDtypeStruct((s, npair * LANES), F32), jax.ShapeDtypeStruct((s, 2 * npair * LANES), F32)],
        operands=(z, z, z, m_init), name=name,
        scratch_shapes=[pltpu.VMEM((s, LANES), F32)] * 2 + [pltpu.VMEM((2 * BLOCK, 2 * BLOCK), F32)]
        + [pltpu.VMEM((s, LANES), F32)] * (2 if gqa else 0), comm=comm)


def _attn_bwd(z, do, o, lse, m_init, dz, q_blk, k_blk, v_blk, patterns, max_dist, gqa, name, comm=None):
    s = z.shape[0]
    npair = 3
    n_dz_in = 0 if dz is None else 1

    def body(q_ref, k_ref, v_ref, do_ref, o_ref, lse0_ref, lse1_ref, mi_ref, *rest):
        (dz_ref, dm_ref, dq_acc, dk_acc, dv_acc, dl0_scr, dl1_scr, bias_scr,
         dq_out, dk_out, dv_out, out_sems, *gqa_scr) = rest[n_dz_in:]
        pair = pl.program_id(0)
        head0 = lax.broadcasted_iota(jnp.int32, (1, LANES), 1) < HEAD_DIM
        _write_band_bias(bias_scr, max_dist)
        k_src, v_src, dk_sum, dv_sum = gqa_scr if gqa else (k_ref, v_ref, None, None)

        def prep(c, dm):
            rows = pl.ds(pl.multiple_of(c * ATTN_ROWS, ATTN_ROWS), ATTN_ROWS)
            dq_acc[rows, :] = jnp.zeros((ATTN_ROWS, LANES), F32)
            dk_acc[rows, :] = jnp.zeros((ATTN_ROWS, LANES), F32)
            dv_acc[rows, :] = jnp.zeros((ATTN_ROWS, LANES), F32)
            if gqa:
                k_src[rows, :] = _kv_for_pair(k_ref[rows, :], pair)
                v_src[rows, :] = _kv_for_pair(v_ref[rows, :], pair)
            prod = do_ref[rows, :] * o_ref[rows, :]
            d0 = jnp.sum(jnp.where(head0, prod, 0.0), axis=1, keepdims=True)
            d1 = jnp.sum(jnp.where(head0, 0.0, prod), axis=1, keepdims=True)
            dl0_scr[rows, :] = jnp.broadcast_to(d0, (ATTN_ROWS, LANES))
            dl1_scr[rows, :] = jnp.broadcast_to(d1, (ATTN_ROWS, LANES))
            lse_own = jnp.where(head0, lse0_ref[rows, :], lse1_ref[rows, :])
            psink = jnp.exp(mi_ref[...] - lse_own)
            return dm - jnp.sum(psink * jnp.where(head0, d0, d1), axis=0, keepdims=True)

        dm_ref[...] = lax.fori_loop(0, s // ATTN_ROWS, prep, jnp.zeros((1, LANES), F32))

        def one_block(b, d, r):
            rq, rp = _band_rows(b, d, r)
            q2 = _stack_heads((q_ref[rq, :] * SCALE).astype(BF16), head0)
            do2 = _stack_heads(do_ref[rq, :].astype(BF16), head0)
            k2 = jnp.concatenate([k_src[rp, :], k_src[rq, :]], axis=0).astype(BF16)
            v2 = jnp.concatenate([v_src[rp, :], v_src[rq, :]], axis=0).astype(BF16)
            lse2 = jnp.concatenate([lse0_ref[rq, :], lse1_ref[rq, :]], axis=0)
            dl2 = jnp.concatenate([dl0_scr[rq, :], dl1_scr[rq, :]], axis=0)
            lse2 = jnp.concatenate([lse2, lse2], axis=1)
            dl2 = jnp.concatenate([dl2, dl2], axis=1)
            p = jnp.exp(_dot_nt(q2, k2) + _band_bias(bias_scr, b) - lse2)
            dp = _dot_nt(do2, v2)
            dsc = (p * (dp - dl2)).astype(BF16)
            dq2 = _unstack_heads(_dot_nn(dsc, k2), head0)
            dk2 = _dot_tn(dsc, q2)
            dv2 = _dot_tn(p.astype(BF16), do2)
            dq_acc[rq, :] += dq2 * SCALE
            dk_acc[rp, :] += dk2[:BLOCK]
            dk_acc[rq, :] += dk2[BLOCK:]
            dv_acc[rp, :] += dv2[:BLOCK]
            dv_acc[rq, :] += dv2[BLOCK:]

        _block_loops(s, patterns, ATTN_UNROLL, one_block)

        def to_dz(staged, blk, k):
            cols = pl.ds(pl.multiple_of(blk * LANES, LANES), LANES)
            return pltpu.make_async_copy(staged, dz_ref.at[:, cols], out_sems.at[k])

        last = pair == npair - 1
        q_copy = to_dz(dq_out, q_blk + pair, 0)
        kv_copies = [to_dz(dk_out, k_blk + (0 if gqa else pair), 1), to_dz(dv_out, v_blk + (0 if gqa else pair), 2)]

        @pl.when(pair > 0)
        def _():
            for cp in [q_copy] + ([] if gqa else kv_copies):
                cp.wait()

        def out(c, carry):
            rows = pl.ds(pl.multiple_of(c * ATTN_ROWS, ATTN_ROWS), ATTN_ROWS)
            dq_out[rows, :] = dq_acc[rows, :].astype(BF16)
            if not gqa:
                dk_out[rows, :] = dk_acc[rows, :].astype(BF16)
                dv_out[rows, :] = dv_acc[rows, :].astype(BF16)
                return carry
            dk_t = _kv_grad_from_pair(dk_acc[rows, :], pair)
            dv_t = _kv_grad_from_pair(dv_acc[rows, :], pair)

            @pl.when(pair == 0)
            def _():
                dk_sum[rows, :] = dk_t
                dv_sum[rows, :] = dv_t

            @pl.when((pair > 0) & (pair < npair - 1))
            def _():
                dk_sum[rows, :] += dk_t
                dv_sum[rows, :] += dv_t

            @pl.when(last)
            def _():
                dk_out[rows, :] = (dk_sum[rows, :] + dk_t).astype(BF16)
                dv_out[rows, :] = (dv_sum[rows, :] + dv_t).astype(BF16)

            return carry

        lax.fori_loop(0, s // ATTN_ROWS, out, 0)
        q_copy.start()
        if not gqa:
            for cp in kv_copies:
                cp.start()

        @pl.when(last)
        def _():
            if gqa:
                for cp in kv_copies:
                    cp.start()
            for cp in [q_copy] + kv_copies:
                cp.wait()

    own = pl.BlockSpec((s, LANES), lambda j: (0, j))
    hbm = pl.BlockSpec(memory_space=pl.ANY)
    if gqa:
        kv = lambda blk: pl.BlockSpec((s, LANES), lambda j, blk=blk: (0, blk), pipeline_mode=pl.Buffered(1))
    else:
        kv = lambda blk: pl.BlockSpec((s, LANES), lambda j, blk=blk: (0, blk + j))
    in_specs = [pl.BlockSpec((s, LANES), lambda j: (0, q_blk + j)), kv(k_blk), kv(v_blk), own, own,
                pl.BlockSpec((s, LANES), lambda j: (0, 2 * j)), pl.BlockSpec((s, LANES), lambda j: (0, 2 * j + 1)),
                pl.BlockSpec((1, LANES), lambda j: (0, j))]
    operands = (z, z, z, do, o, lse, lse, m_init)
    return _call(
        body,
        grid=(npair,),
        in_specs=in_specs + [hbm] * n_dz_in,
        out_specs=[hbm, pl.BlockSpec((1, LANES), lambda j: (0, j))],
        out_shape=[jax.ShapeDtypeStruct((s, IN_WIDTH), BF16), jax.ShapeDtypeStruct((1, npair * LANES), F32)],
        operands=operands + (() if dz is None else (dz,)), name=name,
        scratch_shapes=[pltpu.VMEM((s, LANES), F32)] * 5 + [pltpu.VMEM((2 * BLOCK, 2 * BLOCK), F32)]
        + [pltpu.VMEM((s, LANES), BF16)] * 3 + [pltpu.SemaphoreType.DMA((3,))]
        + [pltpu.VMEM((s, LANES), F32)] * (4 if gqa else 0),
        comm=comm, aliases={} if dz is None else {len(in_specs): 0})


def _adamw_math(w, g, m, v):
    m = ADAM_B1 * m + (1.0 - ADAM_B1) * g
    v = ADAM_B2 * v + (1.0 - ADAM_B2) * (g * g)
    m_hat = m / (1.0 - ADAM_B1 ** ADAM_STEP)
    v_hat = v / (1.0 - ADAM_B2 ** ADAM_STEP)
    delta = -ADAM_LR * (m_hat / (jnp.sqrt(v_hat) + ADAM_EPS) + ADAM_WD * w)
    return delta, m, v


def _adamw(w, g, m, v, name):
    rows, cols = w.shape
    tr = min(rows, 256)

    def body(w_ref, g_ref, m_ref, v_ref, d_ref, nm_ref, nv_ref):
        d_ref[...], nm_ref[...], nv_ref[...] = _adamw_math(w_ref[...], g_ref[...], m_ref[...], v_ref[...])

    spec = pl.BlockSpec((tr, cols), lambda i: (i, 0))
    return pl.pallas_call(
        body,
        grid=(rows // tr,),
        in_specs=[spec] * 4,
        out_specs=[spec] * 3,
        out_shape=[jax.ShapeDtypeStruct((rows, cols), F32)] * 3,
        compiler_params=_params("parallel"),
        name=name,
    )(w, g, m, v)


def _small_sum_adamw(gathered, w, m, v, name):
    _, rows, cols = gathered.shape

    def body(ga_ref, w_ref, m_ref, v_ref, g_ref, d_ref, nm_ref, nv_ref):
        g = ga_ref[0]
        for i in range(1, N_DEV):
            g = g + ga_ref[i]
        g_ref[...] = g
        d_ref[...], nm_ref[...], nv_ref[...] = _adamw_math(w_ref[...], g, m_ref[...], v_ref[...])

    return pl.pallas_call(
        body,
        out_shape=[jax.ShapeDtypeStruct((rows, cols), F32)] * 4,
        name=name,
    )(gathered, w, m, v)


def _pair_sum(g4, r1, pos, name):
    _, _, rows, cols = g4.shape
    tr = min(rows, 512)

    def body(pos_ref, g_ref, r_ref, o_ref):
        o_ref[...] = (g_ref[...].astype(F32) + r_ref[...].astype(F32)).astype(BF16)

    return pl.pallas_call(
        body,
        grid_spec=pltpu.PrefetchScalarGridSpec(
            num_scalar_prefetch=1,
            grid=(4, rows // tr),
            in_specs=[pl.BlockSpec((None, None, tr, cols), lambda i, j, p: (i, p[1], j, 0)),
                      pl.BlockSpec((None, tr, cols), lambda i, j, p: (i, j, 0))],
            out_specs=pl.BlockSpec((None, tr, cols), lambda i, j, p: (i, j, 0)),
        ),
        out_shape=jax.ShapeDtypeStruct((4, rows, cols), BF16),
        compiler_params=_params("parallel", "parallel"),
        name=name,
    )(pos, g4, r1)


def _final_sum(p, r2, pos, name):
    _, rows, cols = p.shape
    tr = min(rows, 512)

    def body(pos_ref, p_ref, r_ref, o_ref):
        o_ref[...] = ((p_ref[...].astype(F32) + r_ref[0].astype(F32)) + r_ref[1].astype(F32)) + r_ref[2].astype(F32)

    return pl.pallas_call(
        body,
        grid_spec=pltpu.PrefetchScalarGridSpec(
            num_scalar_prefetch=1,
            grid=(rows // tr,),
            in_specs=[pl.BlockSpec((None, tr, cols), lambda j, q: (q[0], j, 0)),
                      pl.BlockSpec((3, tr, cols), lambda j, q: (0, j, 0))],
            out_specs=pl.BlockSpec((tr, cols), lambda j, q: (j, 0)),
        ),
        out_shape=jax.ShapeDtypeStruct((rows, cols), F32),
        compiler_params=_params("parallel"),
        name=name,
    )(pos, p, r2)


def _place():
    return lax.axis_index("x"), lax.axis_index("y"), lax.axis_index("c")


def _gather_comm(shards):
    na = len(shards)

    def plan(ins, outs, sems):
        send_sems, recv_sems, local_sems = sems
        x, y, c = _place()
        me, sibling = (x, y, c), (x, y, 1 - c)
        chips = [(1 - x, y), (x, 1 - y), (1 - x, 1 - y)]

        def rows(a, px, py, pc):
            m = ins[a].shape[0]
            return outs[a].at[pl.ds((4 * px + 2 * py + pc) * m, m), :]

        def copy(a, k, block, to, src=None):
            return pltpu.make_async_remote_copy(
                src_ref=rows(a, *block) if src is None else src, dst_ref=rows(a, *block),
                send_sem=send_sems.at[a, k], recv_sem=recv_sems.at[a, k], device_id=to, device_id_type=MESH)

        mine = [pltpu.make_async_copy(ins[a], rows(a, *me), local_sems.at[a]) for a in range(na)]
        first = []
        for a in range(na):
            first.append(copy(a, 0, me, sibling, src=ins[a]))
            first += [copy(a, 1 + j, me, (*chip, c), src=ins[a]) for j, chip in enumerate(chips)]
        return me, sibling, chips, c, copy, mine, first

    def start(ins, outs, sems):
        *_, mine, first = plan(ins, outs, sems)
        for cp in mine + first:
            cp.start()

    def finish(ins, outs, sems):
        me, sibling, chips, c, copy, mine, first = plan(ins, outs, sems)
        passed = []
        for j, chip in enumerate(chips):
            for a in range(na):
                copy(a, 1 + j, (*chip, c), me).wait_recv()
                cp = copy(a, 4 + j, (*chip, c), sibling)
                cp.start()
                passed.append(cp)
        for a in range(na):
            copy(a, 0, sibling, me).wait_recv()
            for j, chip in enumerate(chips):
                copy(a, 4 + j, (*chip, 1 - c), me).wait_recv()
        for cp in first + passed:
            cp.wait_send()
        for cp in mine:
            cp.wait()

    return _Comm(tuple(shards),
                 tuple(jax.ShapeDtypeStruct((N_DEV * t.shape[0], t.shape[1]), t.dtype) for t in shards),
                 (pltpu.SemaphoreType.DMA((na, 7)), pltpu.SemaphoreType.DMA((na, 7)), pltpu.SemaphoreType.DMA((na,))),
                 start, finish)


def _exchange_comm(arrays, out_shape, n_copies, copies_of):
    na = len(arrays)

    def every(ins, outs, sems):
        send_sems, recv_sems = sems
        return [cp for a in range(na) for cp in copies_of(ins, outs, a, send_sems, recv_sems)]

    def start(ins, outs, sems):
        for cp in every(ins, outs, sems):
            cp.start()

    def finish(ins, outs, sems):
        for cp in every(ins, outs, sems):
            cp.wait()

    return _Comm(tuple(arrays), tuple(out_shape),
                 (pltpu.SemaphoreType.DMA((na, n_copies)), pltpu.SemaphoreType.DMA((na, n_copies))), start, finish)


def _sibling_comm(grads):
    def copies_of(ins, outs, a, send_sems, recv_sems):
        x, y, c = _place()
        return [pltpu.make_async_remote_copy(
            src_ref=ins[a].at[chip, 1 - c], dst_ref=outs[a].at[chip],
            send_sem=send_sems.at[a, chip], recv_sem=recv_sems.at[a, chip],
            device_id=(x, y, 1 - c), device_id_type=MESH) for chip in range(4)]

    return _exchange_comm(grads, [jax.ShapeDtypeStruct((4,) + t.shape[2:], t.dtype) for t in grads], 4, copies_of)


def _chip_comm(partials):
    def copies_of(ins, outs, a, send_sems, recv_sems):
        x, y, c = _place()
        chips = [(1 - x, y), (x, 1 - y), (1 - x, 1 - y)]
        return [pltpu.make_async_remote_copy(
            src_ref=ins[a].at[2 * cx + cy], dst_ref=outs[a].at[k],
            send_sem=send_sems.at[a, k], recv_sem=recv_sems.at[a, k],
            device_id=(cx, cy, c), device_id_type=MESH) for k, (cx, cy) in enumerate(chips)]

    return _exchange_comm(partials, [jax.ShapeDtypeStruct((3,) + t.shape[1:], t.dtype) for t in partials], 3, copies_of)


def _pad_rows(t, rows):
    return jnp.pad(t, ((0, rows - t.shape[0]), (0, D_MODEL - t.shape[1])))


def _pack_small(g_mix, g_group, g_mlp, g_final, conv, sinks):
    return jnp.concatenate([
        _pad_rows(g_mix, 8), _pad_rows(g_group, 8), _pad_rows(g_mlp, 8), _pad_rows(g_final.reshape(1, D_MODEL), 8),
        _pad_rows(conv.reshape(DEPTH * 3, CONV_CH), 8), _pad_rows(sinks.reshape(1, DEPTH * 6), 8)], axis=0)


def _unpack_small(slab):
    return (slab[0:2], slab[8:10], slab[16:18], slab[24], slab[32:38, :CONV_CH].reshape(DEPTH, 3, CONV_CH),
            slab[40, :DEPTH * 6].reshape(DEPTH, 2, 3))


def kernel(x, w_in, conv_w, sinks, g_mix, g_group, w_o, g_mlp, w_ff_in, w_ff_out, g_final, loss_target, m_w_in, m_conv_w, m_sinks, m_g_mix, m_g_group, m_w_o, m_g_mlp, m_w_ff_in, m_w_ff_out, m_g_final, v_w_in, v_conv_w, v_sinks, v_g_mix, v_g_group, v_w_o, v_g_mlp, v_w_ff_in, v_w_ff_out, v_g_final):
    ax, ay, ac = _place()
    chip = 2 * ax + ay
    dev = 4 * ax + 2 * ay + ac
    pos = jnp.stack([chip, ac]).astype(jnp.int32)

    x0 = x.reshape(SEQ, D_MODEL)
    target = loss_target.reshape(SEQ, D_MODEL)

    shards = {}
    for l in range(DEPTH):
        shards[l, 0], shards[l, 1] = w_in[l].T.astype(BF16), w_o[l].astype(BF16)
        shards[l, 2], shards[l, 3] = w_ff_in[l].T.astype(BF16), w_ff_out[l].astype(BF16)
    conv_tile = jnp.pad(conv_w.reshape(DEPTH * 3, CONV_CH // N_DEV), ((0, 2), (0, LANES - CONV_CH // N_DEV)))
    wt_in0, conv_all = _comm_only(_gather_comm([shards[0, 0], conv_tile]), "gather_first")
    conv_full = conv_all.reshape(N_DEV, 8, LANES)[:, :DEPTH * 3, :CONV_CH // N_DEV]
    conv_full = conv_full.transpose(1, 0, 2).reshape(DEPTH, 3, CONV_CH)

    loss_slab, dx, gsum, small = _step(x0, target, shards, wt_in0, conv_full, sinks, g_mix, g_group, g_mlp, g_final, pos)
    loss = lax.psum(loss_slab[0, 0], ("x", "y", "c"))
    return _finish(loss, dx, gsum, small, dev, w_in, conv_w, sinks, g_mix, g_group, w_o, g_mlp, w_ff_in, w_ff_out, g_final, m_w_in, m_conv_w, m_sinks, m_g_mix, m_g_group, m_w_o, m_g_mlp, m_w_ff_in, m_w_ff_out, m_g_final, v_w_in, v_conv_w, v_sinks, v_g_mix, v_g_group, v_w_o, v_g_mlp, v_w_ff_in, v_w_ff_out, v_g_final)


FWD_CARRY = {(0, "in_proj"): ((0, 1),), (0, "window"): ((1, 0),), (0, "dilated"): ((0, 2),),
             (0, "mix_out"): ((1, 1),), (0, "ff_in"): ((0, 3),), (0, "ff_out"): ((1, 3),),
             (1, "dilated"): ((1, 2),)}


def _step(x0, target, shards, wt_in0, conv_full, sinks, g_mix, g_group, g_mlp, g_final, pos):
    sink_lanes = jnp.repeat(sinks.reshape(DEPTH, 6), HEAD_DIM, axis=1)
    no_sink = jnp.full((1, A_WIDTH), NEG_BIG, F32)
    full = {(0, 0): wt_in0}

    def gather(stage, l):
        keys = FWD_CARRY.get((l, stage), ())
        return keys, (_gather_comm([shards[k] for k in keys]) if keys else None)

    def landed(keys, got):
        full.update(zip(keys, got))

    saved = []
    xc = x0
    for l in range(DEPTH):
        keys, comm = gather("in_proj", l)
        (z, h), got = _norm_mm(xc, g_mix[l:l + 1], full[l, 0], False, f"in_proj_{l}", comm)
        landed(keys, got)
        sink_l = sink_lanes[l:l + 1]
        keys, comm = gather("window", l)
        (yc, lse_c), got = _attn_fwd(z, sink_l, 1.0, QC_BLK, KC_BLK, VC_BLK, (1,), C_MAX_DIST, True,
                                     f"window_attn_{l}", comm)
        landed(keys, got)
        yb = _conv_fwd(z, conv_full[l], f"conv_{l}")
        keys, comm = gather("dilated", l)
        (ya, lse_a), got = _attn_fwd(z, no_sink, 0.0, QA_BLK, KA_BLK, VA_BLK, DILATED_PATTERNS, A_MAX_DIST, False,
                                     f"dilated_attn_{l}", comm)
        landed(keys, got)
        keys, comm = gather("mix_out", l)
        (y, x1), got = _mix_out(ya, yb, yc, g_group[l:l + 1], full[l, 1], xc, f"mix_out_{l}", comm)
        landed(keys, got)
        keys, comm = gather("ff_in", l)
        (a, h2), got = _norm_mm(x1, g_mlp[l:l + 1], full[l, 2], True, f"ff_in_{l}", comm)
        landed(keys, got)
        keys, comm = gather("ff_out", l)
        (x2,), got = _mm_res(a, full[l, 3], x1, f"ff_out_{l}", comm)
        landed(keys, got)
        saved.append((xc, z, h, ya, lse_a, yb, yc, lse_c, sink_l, y, x1, a, h2))
        xc = x2

    loss_slab, dx, dxb, dg_final = _loss_head(xc, g_final.reshape(1, D_MODEL), target, "loss_head")

    def by_owner(t):
        return t.reshape(4, 2, t.shape[0] // N_DEV, D_MODEL)

    def pair(key, g, r1):
        return _pair_sum(g, r1, pos, f"grad_pair_sum_{key[0]}_{key[1]}")

    partial, r2 = {}, {}
    dg_mix, dg_group, dg_mlp, dconv, dsinks = [None] * DEPTH, [None] * DEPTH, [None] * DEPTH, [None] * DEPTH, [None] * DEPTH
    for l in reversed(range(DEPTH)):
        xin, z, h, ya, lse_a, yb, yc, lse_c, sink_l, y, x1, a, h2 = saved[l]
        above = (l + 1, 0)
        (du,), got = _mlp_bwd_act(dxb, full[l, 3], a, f"ff_out_bwd_{l}",
                                  _chip_comm([partial[above]]) if above in partial else None)
        if got:
            r2[above] = got[0]
        (g3,), _ = _mm_tn(a, dxb, f"grad_w_ff_out_{l}")
        g3 = by_owner(g3)
        (g2,), got = _mm_tn(du, h2, f"grad_w_ff_in_{l}", _sibling_comm([g3]))
        g2 = by_owner(g2)
        partial[l, 3] = pair((l, 3), g3, got[0])
        (dx1, dx1b, dg_mlp[l]), got = _mm_nn_normbwd(du, full[l, 2], x1, dx, g_mlp[l:l + 1], f"ff_in_bwd_{l}",
                                                    _join(_chip_comm([partial[l, 3]]), _sibling_comm([g2])))
        r2[l, 3] = got[0]
        partial[l, 2] = pair((l, 2), g2, got[1])
        (g1,), _ = _mm_tn(y, dx1b, f"grad_w_o_{l}")
        g1 = by_owner(g1)
        (dya, dyb, dyc, dg_group[l]), got = _mix_bwd(dx1b, full[l, 1], ya, yb, yc, g_group[l:l + 1],
                                                     f"mix_out_bwd_{l}", _sibling_comm([g1]))
        partial[l, 1] = pair((l, 1), g1, got[0])
        (dz, _), got = _attn_bwd(z, dya, ya, lse_a, no_sink, None, QA_BLK, KA_BLK, VA_BLK, DILATED_PATTERNS,
                                 A_MAX_DIST, False, f"dilated_attn_bwd_{l}",
                                 _chip_comm([partial[l, 2], partial[l, 1]]))
        r2[l, 2], r2[l, 1] = got
        dz, dcw = _conv_bwd(z, conv_full[l], dyb, dz, f"conv_bwd_{l}")
        (dz, dsink), _ = _attn_bwd(z, dyc, yc, lse_c, sink_l, dz, QC_BLK, KC_BLK, VC_BLK, (1,), C_MAX_DIST,
                                   True, f"window_attn_bwd_{l}")
        (g0,), _ = _mm_tn(dz, h, f"grad_w_in_{l}")
        g0 = by_owner(g0)
        if l > 0:
            (dx, dxb, dg_mix[l]), got = _mm_nn_normbwd(dz, full[l, 0], xin, dx1, g_mix[l:l + 1], f"in_proj_bwd_{l}",
                                                      _sibling_comm([g0]))
            partial[l, 0] = pair((l, 0), g0, got[0])
        else:
            (r1,) = _comm_only(_sibling_comm([g0]), "grad_sibling_exchange_last")
            partial[l, 0] = pair((l, 0), g0, r1)
            (dx, dxb, dg_mix[l]), got = _mm_nn_normbwd(dz, full[l, 0], xin, dx1, g_mix[l:l + 1], f"in_proj_bwd_{l}",
                                                      _chip_comm([partial[l, 0]]))
            r2[l, 0] = got[0]
        dconv[l] = dcw[:3]
        dsinks[l] = dsink[0, ::HEAD_DIM]
    gsum = {key: _final_sum(partial[key], r2[key], pos, f"grad_final_sum_{key[0]}_{key[1]}") for key in partial}
    small = _pack_small(jnp.concatenate(dg_mix), jnp.concatenate(dg_group), jnp.concatenate(dg_mlp),
                        dg_final, jnp.stack(dconv), jnp.stack(dsinks))
    return loss_slab, dx, gsum, small


def _finish(loss, dx, gsum, small, dev, w_in, conv_w, sinks, g_mix, g_group, w_o, g_mlp, w_ff_in, w_ff_out, g_final, m_w_in, m_conv_w, m_sinks, m_g_mix, m_g_group, m_w_o, m_g_mlp, m_w_ff_in, m_w_ff_out, m_g_final, v_w_in, v_conv_w, v_sinks, v_g_mix, v_g_group, v_w_o, v_g_mlp, v_w_ff_in, v_w_ff_out, v_g_final):
    grad_x = dx.reshape(1, SEQ, D_MODEL)
    grad_w_in = jnp.stack([gsum[l, 0].T for l in range(DEPTH)])
    grad_w_o = jnp.stack([gsum[l, 1] for l in range(DEPTH)])
    grad_w_ff_in = jnp.stack([gsum[l, 2].T for l in range(DEPTH)])
    grad_w_ff_out = jnp.stack([gsum[l, 3] for l in range(DEPTH)])

    (small_all,) = _comm_only(_gather_comm([small]), "gather_small_grads")
    zeros_conv = jnp.zeros((DEPTH, 3, CONV_CH), F32)
    sw = _pack_small(g_mix, g_group, g_mlp, g_final, zeros_conv, sinks)
    sm = _pack_small(m_g_mix, m_g_group, m_g_mlp, m_g_final, zeros_conv, m_sinks)
    sv = _pack_small(v_g_mix, v_g_group, v_g_mlp, v_g_final, zeros_conv, v_sinks)
    sg, sd, snm, snv = _small_sum_adamw(small_all.reshape(N_DEV, SMALL_ROWS, D_MODEL), sw, sm, sv, "small_adamw")
    grad_g_mix, grad_g_group, grad_g_mlp, grad_g_final, conv_grad_full, grad_sinks = _unpack_small(sg)
    delta_g_mix, delta_g_group, delta_g_mlp, delta_g_final, _, delta_sinks = _unpack_small(sd)
    new_m_g_mix, new_m_g_group, new_m_g_mlp, new_m_g_final, _, new_m_sinks = _unpack_small(snm)
    new_v_g_mix, new_v_g_group, new_v_g_mlp, new_v_g_final, _, new_v_sinks = _unpack_small(snv)
    cs = CONV_CH // N_DEV
    grad_conv_w = lax.dynamic_slice_in_dim(conv_grad_full, dev * cs, cs, axis=2)

    def tile_of(t):
        return jnp.pad(t.reshape(1, DEPTH * 3 * cs), ((0, 7), (0, 256 - DEPTH * 3 * cs)))

    cd, cm, cv = _adamw(tile_of(conv_w), tile_of(grad_conv_w), tile_of(m_conv_w), tile_of(v_conv_w), "conv_adamw")
    untile = lambda t: t[0, :DEPTH * 3 * cs].reshape(DEPTH, 3, cs)
    delta_conv_w, new_m_conv_w, new_v_conv_w = untile(cd), untile(cm), untile(cv)

    def big(w, g, m, v, name):
        shp = w.shape
        flat = lambda t: t.reshape(shp[0] * shp[1], shp[2])
        return [t.reshape(shp) for t in _adamw(flat(w), flat(g), flat(m), flat(v), name)]

    delta_w_in, new_m_w_in, new_v_w_in = big(w_in, grad_w_in, m_w_in, v_w_in, "adamw_w_in")
    delta_w_o, new_m_w_o, new_v_w_o = big(w_o, grad_w_o, m_w_o, v_w_o, "adamw_w_o")
    delta_w_ff_in, new_m_w_ff_in, new_v_w_ff_in = big(w_ff_in, grad_w_ff_in, m_w_ff_in, v_w_ff_in, "adamw_w_ff_in")
    delta_w_ff_out, new_m_w_ff_out, new_v_w_ff_out = big(w_ff_out, grad_w_ff_out, m_w_ff_out, v_w_ff_out, "adamw_w_ff_out")

    return (loss, grad_x, grad_w_in, grad_conv_w, grad_sinks, grad_g_mix, grad_g_group, grad_w_o, grad_g_mlp,
            grad_w_ff_in, grad_w_ff_out, grad_g_final,
            delta_w_in, delta_conv_w, delta_sinks, delta_g_mix, delta_g_group, delta_w_o, delta_g_mlp,
            delta_w_ff_in, delta_w_ff_out, delta_g_final,
            new_m_w_in, new_m_conv_w, new_m_sinks, new_m_g_mix, new_m_g_group, new_m_w_o, new_m_g_mlp,
            new_m_w_ff_in, new_m_w_ff_out, new_m_g_final,
            new_v_w_in, new_v_conv_w, new_v_sinks, new_v_g_mix, new_v_g_group, new_v_w_o, new_v_g_mlp,
            new_v_w_ff_in, new_v_w_ff_out, new_v_g_final)
```

```python
from typing import Callable, NamedTuple

import jax
import jax.numpy as jnp
from jax import lax
from jax.experimental import pallas as pl
from jax.experimental.pallas import tpu as pltpu

F32 = jnp.float32
BF16 = jnp.bfloat16
MESH = pl.DeviceIdType.MESH

N_DEV = 8
SEQ = 4096
D_MODEL = 1024
DEPTH = 2
HEAD_DIM = 64
LANES = 128
A_WIDTH = 384
CONV_CH = 256
C_WIDTH = 384
KV_WIDTH = 128
IN_WIDTH = 2560
D_FF = 4096
BLOCK = 128
DILATED_PATTERNS = (1, 4, 16)
A_MAX_DIST = 128
C_MAX_DIST = 127
EPS = 1e-6
SCALE = HEAD_DIM ** -0.5
NEG_BIG = -1e30

QA_BLK, KA_BLK, VA_BLK = 0, 3, 6
GB_BLK, GC_BLK, XB_BLK = 9, 11, 13
QC_BLK, KC_BLK, VC_BLK = 15, 18, 19

ADAM_LR = 0.001
ADAM_B1 = 0.9
ADAM_B2 = 0.999
ADAM_EPS = 1e-08
ADAM_WD = 0.01
ADAM_STEP = 10

VMEM_LIMIT = 56 * 1024 * 1024
ROW_TILE = 512
COL_CHUNK = 512
SMALL_ROWS = 48


def _dot_nn(a, b):
    return lax.dot_general(a, b, (((1,), (0,)), ((), ())), preferred_element_type=F32)


def _dot_nt(a, b):
    return lax.dot_general(a, b, (((1,), (1,)), ((), ())), preferred_element_type=F32)


def _dot_tn(a, b):
    return lax.dot_general(a, b, (((0,), (0,)), ((), ())), preferred_element_type=F32)


def _params(*sem):
    return pltpu.CompilerParams(dimension_semantics=sem, vmem_limit_bytes=VMEM_LIMIT)


def _rms_scale(t):
    return lax.rsqrt(jnp.mean(t * t, axis=-1, keepdims=True) + EPS)


def _rms_bwd(n, r, dn):
    return r * (dn - n * jnp.mean(dn * n, axis=-1, keepdims=True))


class _Comm(NamedTuple):
    arrays: tuple
    out_shape: tuple
    sems: tuple
    start: Callable
    finish: Callable


def _join(*comms):
    comms = [c for c in comms if c is not None]
    if not comms:
        return None

    def run(which):
        def f(ins, outs, sems):
            i = o = s = 0
            for c in comms:
                ni, no, ns = len(c.arrays), len(c.out_shape), len(c.sems)
                getattr(c, which)(ins[i:i + ni], outs[o:o + no], sems[s:s + ns])
                i, o, s = i + ni, o + no, s + ns
        return f

    return _Comm(sum((tuple(c.arrays) for c in comms), ()), sum((tuple(c.out_shape) for c in comms), ()),
                 sum((tuple(c.sems) for c in comms), ()), run("start"), run("finish"))


def _call(body, grid, in_specs, out_specs, out_shape, operands, name, scratch_shapes=(), comm=None, aliases=None):
    n_in, n_out, n_scr = len(in_specs), len(out_shape), len(scratch_shapes)
    aliases = dict(aliases or {})
    if comm is None:
        res = pl.pallas_call(body, grid=grid, in_specs=list(in_specs), out_specs=list(out_specs),
                             out_shape=list(out_shape), scratch_shapes=list(scratch_shapes),
                             input_output_aliases=aliases,
                             compiler_params=_params("arbitrary"), name=name)(*operands)
        return list(res), []
    c_in, c_out = len(comm.arrays), len(comm.out_shape)
    hbm = pl.BlockSpec(memory_space=pl.ANY)
    last = grid[0] - 1

    def carried(*refs):
        ins, cins = refs[:n_in], refs[n_in:n_in + c_in]
        o0 = n_in + c_in
        outs, couts = refs[o0:o0 + n_out], refs[o0 + n_out:o0 + n_out + c_out]
        s0 = o0 + n_out + c_out
        scr, sems = refs[s0:s0 + n_scr], refs[s0 + n_scr:]
        pl.when(pl.program_id(0) == 0)(lambda: comm.start(cins, couts, sems))
        body(*ins, *outs, *scr)
        pl.when(pl.program_id(0) == last)(lambda: comm.finish(cins, couts, sems))

    res = pl.pallas_call(carried, grid=grid, in_specs=list(in_specs) + [hbm] * c_in,
                         out_specs=list(out_specs) + [hbm] * c_out, out_shape=list(out_shape) + list(comm.out_shape),
                         scratch_shapes=list(scratch_shapes) + list(comm.sems), input_output_aliases=aliases,
                         compiler_params=_params("arbitrary"), name=name)(*operands, *comm.arrays)
    return list(res[:n_out]), list(res[n_out:])


def _comm_only(comm, name):
    hbm = pl.BlockSpec(memory_space=pl.ANY)
    c_in, c_out = len(comm.arrays), len(comm.out_shape)

    def body(*refs):
        ins, outs, sems = refs[:c_in], refs[c_in:c_in + c_out], refs[c_in + c_out:]
        comm.start(ins, outs, sems)
        comm.finish(ins, outs, sems)

    return pl.pallas_call(body, in_specs=[hbm] * c_in, out_specs=[hbm] * c_out, out_shape=list(comm.out_shape),
                          scratch_shapes=list(comm.sems), name=name)(*comm.arrays)


def _norm_mm(x, g, wt, relu2, name, comm=None):
    s, d = x.shape
    n = wt.shape[0]
    tm = ROW_TILE

    def body(x_ref, g_ref, w_ref, o_ref, h_ref):
        xx = x_ref[...]
        h = ((xx * _rms_scale(xx)) * g_ref[...]).astype(BF16)
        h_ref[...] = h
        for n0 in range(0, n, COL_CHUNK):
            zc = _dot_nt(h, w_ref[n0:n0 + COL_CHUNK, :])
            if relu2:
                zc = jnp.square(jnp.maximum(zc, 0.0)).astype(BF16)
            o_ref[:, n0:n0 + COL_CHUNK] = zc

    return _call(
        body,
        grid=(s // tm,),
        in_specs=[pl.BlockSpec((tm, d), lambda i: (i, 0)),
                  pl.BlockSpec((1, d), lambda i: (0, 0)),
                  pl.BlockSpec((n, d), lambda i: (0, 0))],
        out_specs=[pl.BlockSpec((tm, n), lambda i: (i, 0)),
                   pl.BlockSpec((tm, d), lambda i: (i, 0))],
        out_shape=[jax.ShapeDtypeStruct((s, n), BF16 if relu2 else F32), jax.ShapeDtypeStruct((s, d), BF16)],
        operands=(x, g, wt), name=name, comm=comm)


def _mm_res(a, w2, x1, name, comm=None):
    s, f = a.shape
    d = w2.shape[1]
    tm = ROW_TILE

    def body(a_ref, w_ref, x_ref, o_ref):
        o_ref[...] = x_ref[...] + _dot_nn(a_ref[...], w_ref[...])

    return _call(
        body,
        grid=(s // tm,),
        in_specs=[pl.BlockSpec((tm, f), lambda i: (i, 0)),
                  pl.BlockSpec((f, d), lambda i: (0, 0)),
                  pl.BlockSpec((tm, d), lambda i: (i, 0))],
        out_specs=[pl.BlockSpec((tm, d), lambda i: (i, 0))],
        out_shape=[jax.ShapeDtypeStruct((s, d), F32)],
        operands=(a, w2, x1), name=name, comm=comm)


def _mix_out(ya, yb, yc, gg, wo, x0, name, comm=None):
    s = ya.shape[0]
    d = wo.shape[1]
    tm = ROW_TILE

    def body(ya_ref, yb_ref, yc_ref, g_ref, w_ref, x_ref, y_ref, o_ref):
        parts = []
        for ref in (ya_ref, yb_ref, yc_ref):
            t = ref[...]
            parts.append(t * _rms_scale(t))
        y = (jnp.concatenate(parts, axis=1) * g_ref[...]).astype(BF16)
        y_ref[...] = y
        o_ref[...] = x_ref[...] + _dot_nn(y, w_ref[...])

    return _call(
        body,
        grid=(s // tm,),
        in_specs=[pl.BlockSpec((tm, A_WIDTH), lambda i: (i, 0)),
                  pl.BlockSpec((tm, CONV_CH), lambda i: (i, 0)),
                  pl.BlockSpec((tm, C_WIDTH), lambda i: (i, 0)),
                  pl.BlockSpec((1, d), lambda i: (0, 0)),
                  pl.BlockSpec((d, d), lambda i: (0, 0)),
                  pl.BlockSpec((tm, d), lambda i: (i, 0))],
        out_specs=[pl.BlockSpec((tm, d), lambda i: (i, 0)),
                   pl.BlockSpec((tm, d), lambda i: (i, 0))],
        out_shape=[jax.ShapeDtypeStruct((s, d), BF16), jax.ShapeDtypeStruct((s, d), F32)],
        operands=(ya, yb, yc, gg, wo, x0), name=name, comm=comm)


def _loss_head(x, g, target, name):
    s, d = x.shape
    tm = ROW_TILE

    def body(x_ref, g_ref, t_ref, loss_ref, dx_ref, dxb_ref, dg_ref):
        @pl.when(pl.program_id(0) == 0)
        def _():
            loss_ref[...] = jnp.zeros_like(loss_ref)
            dg_ref[...] = jnp.zeros_like(dg_ref)

        xx = x_ref[...]
        r = _rms_scale(xx)
        n = xx * r
        gv = g_ref[...]
        err = n * gv - t_ref[...]
        per_tok = jnp.sum(err * err, axis=1, keepdims=True) * (1.0 / d)
        loss_ref[...] += 0.5 * jnp.sum(per_tok, axis=0, keepdims=True)
        dout = err * (1.0 / d)
        dg_ref[...] += jnp.sum(dout * n, axis=0, keepdims=True)
        dx = _rms_bwd(n, r, dout * gv)
        dx_ref[...] = dx
        dxb_ref[...] = dx.astype(BF16)

    return pl.pallas_call(
        body,
        grid=(s // tm,),
        in_specs=[pl.BlockSpec((tm, d), lambda i: (i, 0)),
                  pl.BlockSpec((1, d), lambda i: (0, 0)),
                  pl.BlockSpec((tm, d), lambda i: (i, 0))],
        out_specs=[pl.BlockSpec((8, LANES), lambda i: (0, 0)),
                   pl.BlockSpec((tm, d), lambda i: (i, 0)),
                   pl.BlockSpec((tm, d), lambda i: (i, 0)),
                   pl.BlockSpec((1, d), lambda i: (0, 0))],
        out_shape=[jax.ShapeDtypeStruct((8, LANES), F32), jax.ShapeDtypeStruct((s, d), F32),
                   jax.ShapeDtypeStruct((s, d), BF16), jax.ShapeDtypeStruct((1, d), F32)],
        compiler_params=_params("arbitrary"),
        name=name,
    )(x, g, target)


def _mlp_bwd_act(dxb, w2, a, name, comm=None):
    s, d = dxb.shape
    f = w2.shape[0]
    tm = ROW_TILE

    def body(dx_ref, w_ref, a_ref, du_ref):
        dx = dx_ref[...]
        for n0 in range(0, f, COL_CHUNK):
            da = _dot_nt(dx, w_ref[n0:n0 + COL_CHUNK, :])
            rl = jnp.sqrt(a_ref[:, n0:n0 + COL_CHUNK].astype(F32))
            du_ref[:, n0:n0 + COL_CHUNK] = (da * (2.0 * rl)).astype(BF16)

    return _call(
        body,
        grid=(s // tm,),
        in_specs=[pl.BlockSpec((tm, d), lambda i: (i, 0)),
                  pl.BlockSpec((f, d), lambda i: (0, 0)),
                  pl.BlockSpec((tm, f), lambda i: (i, 0))],
        out_specs=[pl.BlockSpec((tm, f), lambda i: (i, 0))],
        out_shape=[jax.ShapeDtypeStruct((s, f), BF16)],
        operands=(dxb, w2, a), name=name, comm=comm)


def _mm_tn(a, b, name, comm=None):
    s, n = a.shape
    d = b.shape[1]
    tn = 512

    def body(a_ref, b_ref, o_ref, acc):
        for k0 in range(0, s, ROW_TILE):
            part = _dot_tn(a_ref[k0:k0 + ROW_TILE, :], b_ref[k0:k0 + ROW_TILE, :])
            if k0 == 0:
                acc[...] = part
            else:
                acc[...] += part
        o_ref[...] = acc[...].astype(BF16)

    return _call(
        body,
        grid=(n // tn,),
        in_specs=[pl.BlockSpec((s, tn), lambda j: (0, j)),
                  pl.BlockSpec((s, d), lambda j: (0, 0))],
        out_specs=[pl.BlockSpec((tn, d), lambda j: (j, 0))],
        out_shape=[jax.ShapeDtypeStruct((n, d), BF16)],
        operands=(a, b), name=name, scratch_shapes=[pltpu.VMEM((tn, d), F32)], comm=comm)


def _mm_nn_normbwd(dact, wt, x, dres, g, name, comm=None):
    s, kdim = dact.shape
    d = wt.shape[1]
    tm = ROW_TILE

    def body(a_ref, w_ref, x_ref, r_ref, g_ref, o_ref, ob_ref, dg_ref):
        @pl.when(pl.program_id(0) == 0)
        def _():
            dg_ref[...] = jnp.zeros_like(dg_ref)

        dh = _dot_nn(a_ref[...], w_ref[...])
        xx = x_ref[...]
        r = _rms_scale(xx)
        n = xx * r
        dg_ref[...] += jnp.sum(dh * n, axis=0, keepdims=True)
        dx = r_ref[...] + _rms_bwd(n, r, dh * g_ref[...])
        o_ref[...] = dx
        ob_ref[...] = dx.astype(BF16)

    return _call(
        body,
        grid=(s // tm,),
        in_specs=[pl.BlockSpec((tm, kdim), lambda i: (i, 0)),
                  pl.BlockSpec((kdim, d), lambda i: (0, 0)),
                  pl.BlockSpec((tm, d), lambda i: (i, 0)),
                  pl.BlockSpec((tm, d), lambda i: (i, 0)),
                  pl.BlockSpec((1, d), lambda i: (0, 0))],
        out_specs=[pl.BlockSpec((tm, d), lambda i: (i, 0)),
                   pl.BlockSpec((tm, d), lambda i: (i, 0)),
                   pl.BlockSpec((1, d), lambda i: (0, 0))],
        out_shape=[jax.ShapeDtypeStruct((s, d), F32), jax.ShapeDtypeStruct((s, d), BF16),
                   jax.ShapeDtypeStruct((1, d), F32)],
        operands=(dact, wt, x, dres, g), name=name, comm=comm)


def _mix_bwd(dx1, wo, ya, yb, yc, gg, name, comm=None):
    s, d = dx1.shape
    tm = ROW_TILE
    widths = (A_WIDTH, CONV_CH, C_WIDTH)

    def body(dx_ref, w_ref, ya_ref, yb_ref, yc_ref, g_ref, da_ref, db_ref, dc_ref, dg_ref):
        @pl.when(pl.program_id(0) == 0)
        def _():
            dg_ref[...] = jnp.zeros_like(dg_ref)

        dy = _dot_nt(dx_ref[...], w_ref[...])
        gv = g_ref[...]
        off = 0
        dgs = []
        for ref, out, w in zip((ya_ref, yb_ref, yc_ref), (da_ref, db_ref, dc_ref), widths):
            t = ref[...]
            r = _rms_scale(t)
            n = t * r
            dyg = dy[:, off:off + w]
            dgs.append(jnp.sum(dyg * n, axis=0, keepdims=True))
            out[...] = _rms_bwd(n, r, dyg * gv[:, off:off + w])
            off += w
        dg_ref[...] += jnp.concatenate(dgs, axis=1)

    return _call(
        body,
        grid=(s // tm,),
        in_specs=[pl.BlockSpec((tm, d), lambda i: (i, 0)),
                  pl.BlockSpec((d, d), lambda i: (0, 0)),
                  pl.BlockSpec((tm, A_WIDTH), lambda i: (i, 0)),
                  pl.BlockSpec((tm, CONV_CH), lambda i: (i, 0)),
                  pl.BlockSpec((tm, C_WIDTH), lambda i: (i, 0)),
                  pl.BlockSpec((1, d), lambda i: (0, 0))],
        out_specs=[pl.BlockSpec((tm, A_WIDTH), lambda i: (i, 0)),
                   pl.BlockSpec((tm, CONV_CH), lambda i: (i, 0)),
                   pl.BlockSpec((tm, C_WIDTH), lambda i: (i, 0)),
                   pl.BlockSpec((1, d), lambda i: (0, 0))],
        out_shape=[jax.ShapeDtypeStruct((s, A_WIDTH), F32), jax.ShapeDtypeStruct((s, CONV_CH), F32),
                   jax.ShapeDtypeStruct((s, C_WIDTH), F32), jax.ShapeDtypeStruct((1, d), F32)],
        operands=(dx1, wo, ya, yb, yc, gg), name=name, comm=comm)


CONV_CHUNK = 256
CONV_HALO = 8


def _conv_fwd(z, cw, name):
    s = z.shape[0]
    nch = s // CONV_CHUNK

    def body(gb_ref, gc_ref, xb_ref, w_ref, o_ref, us):
        us[pl.ds(0, CONV_HALO), :] = jnp.zeros((CONV_HALO, LANES), F32)
        us[pl.ds(CONV_HALO, s), :] = gc_ref[...] * xb_ref[...]
        w0, w1, w2 = w_ref[0:1, :], w_ref[1:2, :], w_ref[2:3, :]

        def chunk(c, carry):
            st = pl.multiple_of(c * CONV_CHUNK, CONV_CHUNK)
            ext = us[pl.ds(st, CONV_CHUNK + CONV_HALO), :]
            y = (w0 * ext[CONV_HALO - 2:CONV_HALO - 2 + CONV_CHUNK]
                 + w1 * ext[CONV_HALO - 1:CONV_HALO - 1 + CONV_CHUNK]
                 + w2 * ext[CONV_HALO:])
            o_ref[pl.ds(st, CONV_CHUNK), :] = gb_ref[pl.ds(st, CONV_CHUNK), :] * y
            return carry

        lax.fori_loop(0, nch, chunk, 0)

    col = lambda blk: pl.BlockSpec((s, LANES), lambda j, blk=blk: (0, blk + j))
    return pl.pallas_call(
        body,
        grid=(CONV_CH // LANES,),
        in_specs=[col(GB_BLK), col(GC_BLK), col(XB_BLK), pl.BlockSpec((3, LANES), lambda j: (0, j))],
        out_specs=pl.BlockSpec((s, LANES), lambda j: (0, j)),
        out_shape=jax.ShapeDtypeStruct((s, CONV_CH), F32),
        scratch_shapes=[pltpu.VMEM((s + CONV_HALO, LANES), F32)],
        compiler_params=_params("parallel"),
        name=name,
    )(z, z, z, cw)


def _conv_bwd(z, cw, dyb, dz, name):
    s = z.shape[0]
    nch = s // CONV_CHUNK
    ncol = CONV_CH // LANES

    def body(gb_ref, gc_ref, xb_ref, w_ref, dy_ref, dz_in, dz_ref, dw_ref, us, ds_, dgb_ref, dgc_ref, dxb_ref, sems):
        j = pl.program_id(0)

        def to_dz(staged, blk, k):
            cols = pl.ds(pl.multiple_of((blk + j) * LANES, LANES), LANES)
            return pltpu.make_async_copy(staged, dz_ref.at[:, cols], sems.at[k])

        copies = [to_dz(dgb_ref, GB_BLK, 0), to_dz(dgc_ref, GC_BLK, 1), to_dz(dxb_ref, XB_BLK, 2)]

        @pl.when(j > 0)
        def _():
            for cp in copies:
                cp.wait()

        us[pl.ds(0, CONV_HALO), :] = jnp.zeros((CONV_HALO, LANES), F32)
        us[pl.ds(CONV_HALO, s), :] = gc_ref[...] * xb_ref[...]
        ds_[pl.ds(s, CONV_HALO), :] = jnp.zeros((CONV_HALO, LANES), F32)
        ds_[pl.ds(0, s), :] = dy_ref[...] * gb_ref[...]
        w0, w1, w2 = w_ref[0:1, :], w_ref[1:2, :], w_ref[2:3, :]
        zero = jnp.zeros((1, LANES), F32)

        def chunk(c, carry):
            a0, a1, a2 = carry
            st = pl.multiple_of(c * CONV_CHUNK, CONV_CHUNK)
            rows = pl.ds(st, CONV_CHUNK)
            ext = us[pl.ds(st, CONV_CHUNK + CONV_HALO), :]
            um2 = ext[CONV_HALO - 2:CONV_HALO - 2 + CONV_CHUNK]
            um1 = ext[CONV_HALO - 1:CONV_HALO - 1 + CONV_CHUNK]
            u0 = ext[CONV_HALO:]
            dext = ds_[pl.ds(st, CONV_CHUNK + CONV_HALO), :]
            dc0 = dext[:CONV_CHUNK]
            du = w2 * dc0 + w1 * dext[1:1 + CONV_CHUNK] + w0 * dext[2:2 + CONV_CHUNK]
            yconv = w0 * um2 + w1 * um1 + w2 * u0
            dgb_ref[rows, :] = (dy_ref[rows, :] * yconv).astype(BF16)
            dgc_ref[rows, :] = (du * xb_ref[rows, :]).astype(BF16)
            dxb_ref[rows, :] = (du * gc_ref[rows, :]).astype(BF16)
            a0 = a0 + jnp.sum(dc0 * um2, axis=0, keepdims=True)
            a1 = a1 + jnp.sum(dc0 * um1, axis=0, keepdims=True)
            a2 = a2 + jnp.sum(dc0 * u0, axis=0, keepdims=True)
            return a0, a1, a2

        a0, a1, a2 = lax.fori_loop(0, nch, chunk, (zero, zero, zero))
        dw_ref[...] = jnp.concatenate([a0, a1, a2, jnp.zeros((5, LANES), F32)], axis=0)
        for cp in copies:
            cp.start()

        @pl.when(j == ncol - 1)
        def _():
            for cp in copies:
                cp.wait()

    col = lambda blk: pl.BlockSpec((s, LANES), lambda j, blk=blk: (0, blk + j))
    hbm = pl.BlockSpec(memory_space=pl.ANY)
    return pl.pallas_call(
        body,
        grid=(ncol,),
        in_specs=[col(GB_BLK), col(GC_BLK), col(XB_BLK), pl.BlockSpec((3, LANES), lambda j: (0, j)),
                  pl.BlockSpec((s, LANES), lambda j: (0, j)), hbm],
        out_specs=[hbm, pl.BlockSpec((8, LANES), lambda j: (0, j))],
        out_shape=[jax.ShapeDtypeStruct(dz.shape, dz.dtype), jax.ShapeDtypeStruct((8, CONV_CH), F32)],
        scratch_shapes=[pltpu.VMEM((s + CONV_HALO, LANES), F32), pltpu.VMEM((s + CONV_HALO, LANES), F32)]
        + [pltpu.VMEM((s, LANES), BF16)] * 3 + [pltpu.SemaphoreType.DMA((3,))],
        input_output_aliases={5: 0},
        compiler_params=_params("arbitrary"),
        name=name,
    )(z, z, z, cw, dyb, dz)


ATTN_ROWS = 512
ATTN_UNROLL = 8


def _band_rows(b, d, r):
    base = pl.multiple_of(b * (BLOCK * d), BLOCK)
    prev = jnp.maximum(base - BLOCK * d, 0)
    if d == 1:
        return pl.ds(base, BLOCK), pl.ds(pl.multiple_of(prev, BLOCK), BLOCK)
    return pl.ds(base + r, BLOCK, stride=d), pl.ds(prev + r, BLOCK, stride=d)


def _write_band_bias(bias_ref, max_dist):
    qi = lax.broadcasted_iota(jnp.int32, (BLOCK, 2 * BLOCK), 0)
    kj = lax.broadcasted_iota(jnp.int32, (BLOCK, 2 * BLOCK), 1)
    dist = BLOCK + qi - kj
    band = (dist >= 0) & (dist <= max_dist)
    bias_ref[0:BLOCK, :] = jnp.where(band, 0.0, -jnp.inf)
    bias_ref[BLOCK:2 * BLOCK, :] = jnp.where(band & (kj >= BLOCK), 0.0, -jnp.inf)


def _band_bias(bias_ref, b):
    bias = bias_ref[pl.ds(pl.multiple_of(jnp.where(b > 0, 0, BLOCK), BLOCK), BLOCK), :]
    return jnp.concatenate([bias, bias], axis=0)


def _lane_half():
    return (lax.broadcasted_iota(jnp.int32, (1, LANES), 1) >= HEAD_DIM).astype(jnp.int32)


def _kv_for_pair(t, pair):
    half = _lane_half()
    want = (pair + half) >> 1
    return jnp.where(want != half, pltpu.roll(t, HEAD_DIM, 1), t)


def _kv_grad_from_pair(t, pair):
    half = _lane_half()
    mine = ((pair + half) >> 1) == half
    other = ((pair + 1 - half) >> 1) == half
    fold = t + pltpu.roll(t, HEAD_DIM, 1)
    return jnp.where(mine & other, fold, jnp.where(mine, t, 0.0))


def _stack_heads(t, head0):
    zero = jnp.zeros_like(t)
    return jnp.concatenate([jnp.where(head0, t, zero), jnp.where(head0, zero, t)], axis=0)


def _unstack_heads(t, head0):
    return jnp.where(head0, t[:BLOCK], t[BLOCK:])


def _block_loops(s, patterns, unroll, one_block):
    for d in patterns:
        nb = (s // BLOCK) // d
        ur = min(unroll, d)
        ub = unroll // ur
        for r0 in range(0, d, ur):
            def trip(i, carry, d=d, r0=r0, ur=ur, ub=ub):
                for u in range(ub):
                    for r in range(r0, r0 + ur):
                        one_block(i * ub + u, d, r)
                return carry
            lax.fori_loop(0, nb // ub, trip, 0)


def _attn_fwd(z, m_init, l_init, q_blk, k_blk, v_blk, patterns, max_dist, gqa, name, comm=None):
    s = z.shape[0]
    npair = 3

    def body(q_ref, k_ref, v_ref, mi_ref, o_ref, lse_ref, m_scr, l_scr, bias_scr, *kv_scr):
        pair = pl.program_id(0)
        head0 = lax.broadcasted_iota(jnp.int32, (1, LANES), 1) < HEAD_DIM
        _write_band_bias(bias_scr, max_dist)
        k_src, v_src = kv_scr if gqa else (k_ref, v_ref)

        def init(c, carry):
            rows = pl.ds(pl.multiple_of(c * ATTN_ROWS, ATTN_ROWS), ATTN_ROWS)
            m_scr[rows, :] = jnp.broadcast_to(mi_ref[...], (ATTN_ROWS, LANES))
            l_scr[rows, :] = jnp.full((ATTN_ROWS, LANES), l_init, F32)
            o_ref[rows, :] = jnp.zeros((ATTN_ROWS, LANES), F32)
            if gqa:
                k_src[rows, :] = _kv_for_pair(k_ref[rows, :], pair)
                v_src[rows, :] = _kv_for_pair(v_ref[rows, :], pair)
            return carry

        lax.fori_loop(0, s // ATTN_ROWS, init, 0)
        ones = jnp.ones((2 * BLOCK, LANES), BF16)

        def one_block(b, d, r):
            rq, rp = _band_rows(b, d, r)
            q2 = _stack_heads((q_ref[rq, :] * SCALE).astype(BF16), head0)
            k2 = jnp.concatenate([k_src[rp, :], k_src[rq, :]], axis=0)
            v2 = jnp.concatenate([v_src[rp, :], v_src[rq, :]], axis=0)
            sc = _dot_nt(q2, k2.astype(BF16)) + _band_bias(bias_scr, b)
            mb = jnp.max(sc, axis=1, keepdims=True)
            p = jnp.exp(sc - mb).astype(BF16)
            ob = _dot_nn(p, jnp.concatenate([v2.astype(BF16), ones], axis=1))
            m2 = _unstack_heads(jnp.broadcast_to(mb, (2 * BLOCK, LANES)), head0)
            l2 = _unstack_heads(ob[:, LANES:], head0)
            o2 = _unstack_heads(ob[:, :LANES], head0)
            m_old = m_scr[rq, :]
            m_new = jnp.maximum(m_old, m2)
            a_old = jnp.exp(m_old - m_new)
            a_blk = jnp.exp(m2 - m_new)
            o_ref[rq, :] = o_ref[rq, :] * a_old + o2 * a_blk
            l_scr[rq, :] = l_scr[rq, :] * a_old + l2 * a_blk
            m_scr[rq, :] = m_new

        _block_loops(s, patterns, ATTN_UNROLL, one_block)

        def fin(c, carry):
            rows = pl.ds(pl.multiple_of(c * ATTN_ROWS, ATTN_ROWS), ATTN_ROWS)
            l = l_scr[rows, :]
            o_ref[rows, :] = o_ref[rows, :] / l
            lse = m_scr[rows, :] + jnp.log(l)
            swapped = pltpu.roll(lse, HEAD_DIM, 1)
            lse_ref[rows, 0:LANES] = jnp.where(head0, lse, swapped)
            lse_ref[rows, LANES:2 * LANES] = jnp.where(head0, swapped, lse)
            return carry

        lax.fori_loop(0, s // ATTN_ROWS, fin, 0)

    kv = (lambda blk: pl.BlockSpec((s, LANES), lambda j, blk=blk: (0, blk), pipeline_mode=pl.Buffered(1))) if gqa \
        else (lambda blk: pl.BlockSpec((s, LANES), lambda j, blk=blk: (0, blk + j)))
    return _call(
        body,
        grid=(npair,),
        in_specs=[pl.BlockSpec((s, LANES), lambda j: (0, q_blk + j)), kv(k_blk), kv(v_blk),
                  pl.BlockSpec((1, LANES), lambda j: (0, j))],
        out_specs=[pl.BlockSpec((s, LANES), lambda j: (0, j)), pl.BlockSpec((s, 2 * LANES), lambda j: (0, j))],
        out_shape=[jax.ShapeDtypeStruct((s, npair * LANES), F32), jax.ShapeDtypeStruct((s, 2 * npair * LANES), F32)],
        operands=(z, z, z, m_init), name=name,
        scratch_shapes=[pltpu.VMEM((s, LANES), F32)] * 2 + [pltpu.VMEM((2 * BLOCK, 2 * BLOCK), F32)]
        + [pltpu.VMEM((s, LANES), F32)] * (2 if gqa else 0), comm=comm)


def _attn_bwd(z, do, o, lse, m_init, dz, q_blk, k_blk, v_blk, patterns, max_dist, gqa, name, comm=None):
    s = z.shape[0]
    npair = 3
    n_dz_in = 0 if dz is None else 1

    def body(q_ref, k_ref, v_ref, do_ref, o_ref, lse0_ref, lse1_ref, mi_ref, *rest):
        (dz_ref, dm_ref, dq_acc, dk_acc, dv_acc, dl0_scr, dl1_scr, bias_scr,
         dq_out, dk_out, dv_out, out_sems, *gqa_scr) = rest[n_dz_in:]
        pair = pl.program_id(0)
        head0 = lax.broadcasted_iota(jnp.int32, (1, LANES), 1) < HEAD_DIM
        _write_band_bias(bias_scr, max_dist)
        k_src, v_src, dk_sum, dv_sum = gqa_scr if gqa else (k_ref, v_ref, None, None)

        def prep(c, dm):
            rows = pl.ds(pl.multiple_of(c * ATTN_ROWS, ATTN_ROWS), ATTN_ROWS)
            dq_acc[rows, :] = jnp.zeros((ATTN_ROWS, LANES), F32)
            dk_acc[rows, :] = jnp.zeros((ATTN_ROWS, LANES), F32)
            dv_acc[rows, :] = jnp.zeros((ATTN_ROWS, LANES), F32)
            if gqa:
                k_src[rows, :] = _kv_for_pair(k_ref[rows, :], pair)
                v_src[rows, :] = _kv_for_pair(v_ref[rows, :], pair)
            prod = do_ref[rows, :] * o_ref[rows, :]
            d0 = jnp.sum(jnp.where(head0, prod, 0.0), axis=1, keepdims=True)
            d1 = jnp.sum(jnp.where(head0, 0.0, prod), axis=1, keepdims=True)
            dl0_scr[rows, :] = jnp.broadcast_to(d0, (ATTN_ROWS, LANES))
            dl1_scr[rows, :] = jnp.broadcast_to(d1, (ATTN_ROWS, LANES))
            lse_own = jnp.where(head0, lse0_ref[rows, :], lse1_ref[rows, :])
            psink = jnp.exp(mi_ref[...] - lse_own)
            return dm - jnp.sum(psink * jnp.where(head0, d0, d1), axis=0, keepdims=True)

        dm_ref[...] = lax.fori_loop(0, s // ATTN_ROWS, prep, jnp.zeros((1, LANES), F32))

        def one_block(b, d, r):
            rq, rp = _band_rows(b, d, r)
            q2 = _stack_heads((q_ref[rq, :] * SCALE).astype(BF16), head0)
            do2 = _stack_heads(do_ref[rq, :].astype(BF16), head0)
            k2 = jnp.concatenate([k_src[rp, :], k_src[rq, :]], axis=0).astype(BF16)
            v2 = jnp.concatenate([v_src[rp, :], v_src[rq, :]], axis=0).astype(BF16)
            lse2 = jnp.concatenate([lse0_ref[rq, :], lse1_ref[rq, :]], axis=0)
            dl2 = jnp.concatenate([dl0_scr[rq, :], dl1_scr[rq, :]], axis=0)
            lse2 = jnp.concatenate([lse2, lse2], axis=1)
            dl2 = jnp.concatenate([dl2, dl2], axis=1)
            p = jnp.exp(_dot_nt(q2, k2) + _band_bias(bias_scr, b) - lse2)
            dp = _dot_nt(do2, v2)
            dsc = (p * (dp - dl2)).astype(BF16)
            dq2 = _unstack_heads(_dot_nn(dsc, k2), head0)
            dk2 = _dot_tn(dsc, q2)
            dv2 = _dot_tn(p.astype(BF16), do2)
            dq_acc[rq, :] += dq2 * SCALE
            dk_acc[rp, :] += dk2[:BLOCK]
            dk_acc[rq, :] += dk2[BLOCK:]
            dv_acc[rp, :] += dv2[:BLOCK]
            dv_acc[rq, :] += dv2[BLOCK:]

        _block_loops(s, patterns, ATTN_UNROLL, one_block)

        def to_dz(staged, blk, k):
            cols = pl.ds(pl.multiple_of(blk * LANES, LANES), LANES)
            return pltpu.make_async_copy(staged, dz_ref.at[:, cols], out_sems.at[k])

        last = pair == npair - 1
        q_copy = to_dz(dq_out, q_blk + pair, 0)
        kv_copies = [to_dz(dk_out, k_blk + (0 if gqa else pair), 1), to_dz(dv_out, v_blk + (0 if gqa else pair), 2)]

        @pl.when(pair > 0)
        def _():
            for cp in [q_copy] + ([] if gqa else kv_copies):
                cp.wait()

        def out(c, carry):
            rows = pl.ds(pl.multiple_of(c * ATTN_ROWS, ATTN_ROWS), ATTN_ROWS)
            dq_out[rows, :] = dq_acc[rows, :].astype(BF16)
            if not gqa:
                dk_out[rows, :] = dk_acc[rows, :].astype(BF16)
                dv_out[rows, :] = dv_acc[rows, :].astype(BF16)
                return carry
            dk_t = _kv_grad_from_pair(dk_acc[rows, :], pair)
            dv_t = _kv_grad_from_pair(dv_acc[rows, :], pair)

            @pl.when(pair == 0)
            def _():
                dk_sum[rows, :] = dk_t
                dv_sum[rows, :] = dv_t

            @pl.when((pair > 0) & (pair < npair - 1))
            def _():
                dk_sum[rows, :] += dk_t
                dv_sum[rows, :] += dv_t

            @pl.when(last)
            def _():
                dk_out[rows, :] = (dk_sum[rows, :] + dk_t).astype(BF16)
                dv_out[rows, :] = (dv_sum[rows, :] + dv_t).astype(BF16)

            return carry

        lax.fori_loop(0, s // ATTN_ROWS, out, 0)
        q_copy.start()
        if not gqa:
            for cp in kv_copies:
                cp.start()

        @pl.when(last)
        def _():
            if gqa:
                for cp in kv_copies:
                    cp.start()
            for cp in [q_copy] + kv_copies:
                cp.wait()

    own = pl.BlockSpec((s, LANES), lambda j: (0, j))
    hbm = pl.BlockSpec(memory_space=pl.ANY)
    if gqa:
        kv = lambda blk: pl.BlockSpec((s, LANES), lambda j, blk=blk: (0, blk), pipeline_mode=pl.Buffered(1))
    else:
        kv = lambda blk: pl.BlockSpec((s, LANES), lambda j, blk=blk: (0, blk + j))
    in_specs = [pl.BlockSpec((s, LANES), lambda j: (0, q_blk + j)), kv(k_blk), kv(v_blk), own, own,
                pl.BlockSpec((s, LANES), lambda j: (0, 2 * j)), pl.BlockSpec((s, LANES), lambda j: (0, 2 * j + 1)),
                pl.BlockSpec((1, LANES), lambda j: (0, j))]
    operands = (z, z, z, do, o, lse, lse, m_init)
    return _call(
        body,
        grid=(npair,),
        in_specs=in_specs + [hbm] * n_dz_in,
        out_specs=[hbm, pl.BlockSpec((1, LANES), lambda j: (0, j))],
        out_shape=[jax.ShapeDtypeStruct((s, IN_WIDTH), BF16), jax.ShapeDtypeStruct((1, npair * LANES), F32)],
        operands=operands + (() if dz is None else (dz,)), name=name,
        scratch_shapes=[pltpu.VMEM((s, LANES), F32)] * 5 + [pltpu.VMEM((2 * BLOCK, 2 * BLOCK), F32)]
        + [pltpu.VMEM((s, LANES), BF16)] * 3 + [pltpu.SemaphoreType.DMA((3,))]
        + [pltpu.VMEM((s, LANES), F32)] * (4 if gqa else 0),
        comm=comm, aliases={} if dz is None else {len(in_specs): 0})


def _adamw_math(w, g, m, v):
    m = ADAM_B1 * m + (1.0 - ADAM_B1) * g
    v = ADAM_B2 * v + (1.0 - ADAM_B2) * (g * g)
    m_hat = m / (1.0 - ADAM_B1 ** ADAM_STEP)
    v_hat = v / (1.0 - ADAM_B2 ** ADAM_STEP)
    delta = -ADAM_LR * (m_hat / (jnp.sqrt(v_hat) + ADAM_EPS) + ADAM_WD * w)
    return delta, m, v


def _adamw(w, g, m, v, name):
    rows, cols = w.shape
    tr = min(rows, 256)

    def body(w_ref, g_ref, m_ref, v_ref, d_ref, nm_ref, nv_ref):
        d_ref[...], nm_ref[...], nv_ref[...] = _adamw_math(w_ref[...], g_ref[...], m_ref[...], v_ref[...])

    spec = pl.BlockSpec((tr, cols), lambda i: (i, 0))
    return pl.pallas_call(
        body,
        grid=(rows // tr,),
        in_specs=[spec] * 4,
        out_specs=[spec] * 3,
        out_shape=[jax.ShapeDtypeStruct((rows, cols), F32)] * 3,
        compiler_params=_params("parallel"),
        name=name,
    )(w, g, m, v)


def _small_sum_adamw(gathered, w, m, v, name):
    _, rows, cols = gathered.shape

    def body(ga_ref, w_ref, m_ref, v_ref, g_ref, d_ref, nm_ref, nv_ref):
        g = ga_ref[0]
        for i in range(1, N_DEV):
            g = g + ga_ref[i]
        g_ref[...] = g
        d_ref[...], nm_ref[...], nv_ref[...] = _adamw_math(w_ref[...], g, m_ref[...], v_ref[...])

    return pl.pallas_call(
        body,
        out_shape=[jax.ShapeDtypeStruct((rows, cols), F32)] * 4,
        name=name,
    )(gathered, w, m, v)


def _pair_sum(g4, r1, pos, name):
    _, _, rows, cols = g4.shape
    tr = min(rows, 512)

    def body(pos_ref, g_ref, r_ref, o_ref):
        o_ref[...] = (g_ref[...].astype(F32) + r_ref[...].astype(F32)).astype(BF16)

    return pl.pallas_call(
        body,
        grid_spec=pltpu.PrefetchScalarGridSpec(
            num_scalar_prefetch=1,
            grid=(4, rows // tr),
            in_specs=[pl.BlockSpec((None, None, tr, cols), lambda i, j, p: (i, p[1], j, 0)),
                      pl.BlockSpec((None, tr, cols), lambda i, j, p: (i, j, 0))],
            out_specs=pl.BlockSpec((None, tr, cols), lambda i, j, p: (i, j, 0)),
        ),
        out_shape=jax.ShapeDtypeStruct((4, rows, cols), BF16),
        compiler_params=_params("parallel", "parallel"),
        name=name,
    )(pos, g4, r1)


def _final_sum(p, r2, pos, name):
    _, rows, cols = p.shape
    tr = min(rows, 512)

    def body(pos_ref, p_ref, r_ref, o_ref):
        o_ref[...] = ((p_ref[...].astype(F32) + r_ref[0].astype(F32)) + r_ref[1].astype(F32)) + r_ref[2].astype(F32)

    return pl.pallas_call(
        body,
        grid_spec=pltpu.PrefetchScalarGridSpec(
            num_scalar_prefetch=1,
            grid=(rows // tr,),
            in_specs=[pl.BlockSpec((None, tr, cols), lambda j, q: (q[0], j, 0)),
                      pl.BlockSpec((3, tr, cols), lambda j, q: (0, j, 0))],
            out_specs=pl.BlockSpec((tr, cols), lambda j, q: (j, 0)),
        ),
        out_shape=jax.ShapeDtypeStruct((rows, cols), F32),
        compiler_params=_params("parallel"),
        name=name,
    )(pos, p, r2)


def _place():
    return lax.axis_index("x"), lax.axis_index("y"), lax.axis_index("c")


def _gather_comm(shards):
    na = len(shards)

    def plan(ins, outs, sems):
        send_sems, recv_sems, local_sems = sems
        x, y, c = _place()
        me, sibling = (x, y, c), (x, y, 1 - c)
        chips = [(1 - x, y), (x, 1 - y), (1 - x, 1 - y)]

        def rows(a, px, py, pc):
            m = ins[a].shape[0]
            return outs[a].at[pl.ds((4 * px + 2 * py + pc) * m, m), :]

        def copy(a, k, block, to, src=None):
            return pltpu.make_async_remote_copy(
                src_ref=rows(a, *block) if src is None else src, dst_ref=rows(a, *block),
                send_sem=send_sems.at[a, k], recv_sem=recv_sems.at[a, k], device_id=to, device_id_type=MESH)

        mine = [pltpu.make_async_copy(ins[a], rows(a, *me), local_sems.at[a]) for a in range(na)]
        first = []
        for a in range(na):
            first.append(copy(a, 0, me, sibling, src=ins[a]))
            first += [copy(a, 1 + j, me, (*chip, c), src=ins[a]) for j, chip in enumerate(chips)]
        return me, sibling, chips, c, copy, mine, first

    def start(ins, outs, sems):
        *_, mine, first = plan(ins, outs, sems)
        for cp in mine + first:
            cp.start()

    def finish(ins, outs, sems):
        me, sibling, chips, c, copy, mine, first = plan(ins, outs, sems)
        passed = []
        for j, chip in enumerate(chips):
            for a in range(na):
                copy(a, 1 + j, (*chip, c), me).wait_recv()
                cp = copy(a, 4 + j, (*chip, c), sibling)
                cp.start()
                passed.append(cp)
        for a in range(na):
            copy(a, 0, sibling, me).wait_recv()
            for j, chip in enumerate(chips):
                copy(a, 4 + j, (*chip, 1 - c), me).wait_recv()
        for cp in first + passed:
            cp.wait_send()
        for cp in mine:
            cp.wait()

    return _Comm(tuple(shards),
                 tuple(jax.ShapeDtypeStruct((N_DEV * t.shape[0], t.shape[1]), t.dtype) for t in shards),
                 (pltpu.SemaphoreType.DMA((na, 7)), pltpu.SemaphoreType.DMA((na, 7)), pltpu.SemaphoreType.DMA((na,))),
                 start, finish)


def _exchange_comm(arrays, out_shape, n_copies, copies_of):
    na = len(arrays)

    def every(ins, outs, sems):
        send_sems, recv_sems = sems
        return [cp for a in range(na) for cp in copies_of(ins, outs, a, send_sems, recv_sems)]

    def start(ins, outs, sems):
        for cp in every(ins, outs, sems):
            cp.start()

    def finish(ins, outs, sems):
        for cp in every(ins, outs, sems):
            cp.wait()

    return _Comm(tuple(arrays), tuple(out_shape),
                 (pltpu.SemaphoreType.DMA((na, n_copies)), pltpu.SemaphoreType.DMA((na, n_copies))), start, finish)


def _sibling_comm(grads):
    def copies_of(ins, outs, a, send_sems, recv_sems):
        x, y, c = _place()
        return [pltpu.make_async_remote_copy(
            src_ref=ins[a].at[chip, 1 - c], dst_ref=outs[a].at[chip],
            send_sem=send_sems.at[a, chip], recv_sem=recv_sems.at[a, chip],
            device_id=(x, y, 1 - c), device_id_type=MESH) for chip in range(4)]

    return _exchange_comm(grads, [jax.ShapeDtypeStruct((4,) + t.shape[2:], t.dtype) for t in grads], 4, copies_of)


def _chip_comm(partials):
    def copies_of(ins, outs, a, send_sems, recv_sems):
        x, y, c = _place()
        chips = [(1 - x, y), (x, 1 - y), (1 - x, 1 - y)]
        return [pltpu.make_async_remote_copy(
            src_ref=ins[a].at[2 * cx + cy], dst_ref=outs[a].at[k],
            send_sem=send_sems.at[a, k], recv_sem=recv_sems.at[a, k],
            device_id=(cx, cy, c), device_id_type=MESH) for k, (cx, cy) in enumerate(chips)]

    return _exchange_comm(partials, [jax.ShapeDtypeStruct((3,) + t.shape[1:], t.dtype) for t in partials], 3, copies_of)


def _pad_rows(t, rows):
    return jnp.pad(t, ((0, rows - t.shape[0]), (0, D_MODEL - t.shape[1])))


LOSS_ROW = 25


def _pack_small(g_mix, g_group, g_mlp, g_final, conv, sinks, loss=None):
    loss_row = jnp.zeros((1, LANES), F32) if loss is None else loss
    final_and_loss = jnp.concatenate([g_final.reshape(1, D_MODEL), _pad_rows(loss_row, 1)], axis=0)
    return jnp.concatenate([
        _pad_rows(g_mix, 8), _pad_rows(g_group, 8), _pad_rows(g_mlp, 8), _pad_rows(final_and_loss, 8),
        _pad_rows(conv.reshape(DEPTH * 3, CONV_CH), 8), _pad_rows(sinks.reshape(1, DEPTH * 6), 8)], axis=0)


def _unpack_small(slab):
    return (slab[0:2], slab[8:10], slab[16:18], slab[24], slab[32:38, :CONV_CH].reshape(DEPTH, 3, CONV_CH),
            slab[40, :DEPTH * 6].reshape(DEPTH, 2, 3))


def kernel(x, w_in, conv_w, sinks, g_mix, g_group, w_o, g_mlp, w_ff_in, w_ff_out, g_final, loss_target, m_w_in, m_conv_w, m_sinks, m_g_mix, m_g_group, m_w_o, m_g_mlp, m_w_ff_in, m_w_ff_out, m_g_final, v_w_in, v_conv_w, v_sinks, v_g_mix, v_g_group, v_w_o, v_g_mlp, v_w_ff_in, v_w_ff_out, v_g_final):
    ax, ay, ac = _place()
    chip = 2 * ax + ay
    dev = 4 * ax + 2 * ay + ac
    pos = jnp.stack([chip, ac]).astype(jnp.int32)

    x0 = x.reshape(SEQ, D_MODEL)
    target = loss_target.reshape(SEQ, D_MODEL)

    shards = {}
    for l in range(DEPTH):
        shards[l, 0], shards[l, 1] = w_in[l].T.astype(BF16), w_o[l].astype(BF16)
        shards[l, 2], shards[l, 3] = w_ff_in[l].T.astype(BF16), w_ff_out[l].astype(BF16)
    conv_tile = jnp.pad(conv_w.reshape(DEPTH * 3, CONV_CH // N_DEV), ((0, 2), (0, LANES - CONV_CH // N_DEV)))
    wt_in0, conv_all = _comm_only(_gather_comm([shards[0, 0], conv_tile]), "gather_first")
    conv_full = conv_all.reshape(N_DEV, 8, LANES)[:, :DEPTH * 3, :CONV_CH // N_DEV]
    conv_full = conv_full.transpose(1, 0, 2).reshape(DEPTH, 3, CONV_CH)

    dx, gsum, small = _step(x0, target, shards, wt_in0, conv_full, sinks, g_mix, g_group, g_mlp, g_final, pos)
    return _finish(dx, gsum, small, dev, w_in, conv_w, sinks, g_mix, g_group, w_o, g_mlp, w_ff_in, w_ff_out, g_final, m_w_in, m_conv_w, m_sinks, m_g_mix, m_g_group, m_w_o, m_g_mlp, m_w_ff_in, m_w_ff_out, m_g_final, v_w_in, v_conv_w, v_sinks, v_g_mix, v_g_group, v_w_o, v_g_mlp, v_w_ff_in, v_w_ff_out, v_g_final)


FWD_CARRY = {(0, "in_proj"): ((0, 1),), (0, "window"): ((1, 0),), (0, "dilated"): ((0, 2),),
             (0, "mix_out"): ((1, 1),), (0, "ff_in"): ((0, 3),), (0, "ff_out"): ((1, 3),),
             (1, "dilated"): ((1, 2),)}


def _step(x0, target, shards, wt_in0, conv_full, sinks, g_mix, g_group, g_mlp, g_final, pos):
    sink_lanes = jnp.repeat(sinks.reshape(DEPTH, 6), HEAD_DIM, axis=1)
    no_sink = jnp.full((1, A_WIDTH), NEG_BIG, F32)
    full = {(0, 0): wt_in0}

    def gather(stage, l):
        keys = FWD_CARRY.get((l, stage), ())
        return keys, (_gather_comm([shards[k] for k in keys]) if keys else None)

    def landed(keys, got):
        full.update(zip(keys, got))

    saved = []
    xc = x0
    for l in range(DEPTH):
        keys, comm = gather("in_proj", l)
        (z, h), got = _norm_mm(xc, g_mix[l:l + 1], full[l, 0], False, f"in_proj_{l}", comm)
        landed(keys, got)
        sink_l = sink_lanes[l:l + 1]
        keys, comm = gather("window", l)
        (yc, lse_c), got = _attn_fwd(z, sink_l, 1.0, QC_BLK, KC_BLK, VC_BLK, (1,), C_MAX_DIST, True,
                                     f"window_attn_{l}", comm)
        landed(keys, got)
        yb = _conv_fwd(z, conv_full[l], f"conv_{l}")
        keys, comm = gather("dilated", l)
        (ya, lse_a), got = _attn_fwd(z, no_sink, 0.0, QA_BLK, KA_BLK, VA_BLK, DILATED_PATTERNS, A_MAX_DIST, False,
                                     f"dilated_attn_{l}", comm)
        landed(keys, got)
        keys, comm = gather("mix_out", l)
        (y, x1), got = _mix_out(ya, yb, yc, g_group[l:l + 1], full[l, 1], xc, f"mix_out_{l}", comm)
        landed(keys, got)
        keys, comm = gather("ff_in", l)
        (a, h2), got = _norm_mm(x1, g_mlp[l:l + 1], full[l, 2], True, f"ff_in_{l}", comm)
        landed(keys, got)
        keys, comm = gather("ff_out", l)
        (x2,), got = _mm_res(a, full[l, 3], x1, f"ff_out_{l}", comm)
        landed(keys, got)
        saved.append((xc, z, h, ya, lse_a, yb, yc, lse_c, sink_l, y, x1, a, h2))
        xc = x2

    loss_slab, dx, dxb, dg_final = _loss_head(xc, g_final.reshape(1, D_MODEL), target, "loss_head")

    def by_owner(t):
        return t.reshape(4, 2, t.shape[0] // N_DEV, D_MODEL)

    def pair(key, g, r1):
        return _pair_sum(g, r1, pos, f"grad_pair_sum_{key[0]}_{key[1]}")

    partial, r2 = {}, {}
    dg_mix, dg_group, dg_mlp, dconv, dsinks = [None] * DEPTH, [None] * DEPTH, [None] * DEPTH, [None] * DEPTH, [None] * DEPTH
    for l in reversed(range(DEPTH)):
        xin, z, h, ya, lse_a, yb, yc, lse_c, sink_l, y, x1, a, h2 = saved[l]
        late = [(l + 1, 1), (l + 1, 0)] if l + 1 < DEPTH else []
        (du,), got = _mlp_bwd_act(dxb, full[l, 3], a, f"ff_out_bwd_{l}",
                                  _chip_comm([partial[k] for k in late]) if late else None)
        r2.update(zip(late, got))
        (g3,), _ = _mm_tn(a, dxb, f"grad_w_ff_out_{l}")
        (g2,), _ = _mm_tn(du, h2, f"grad_w_ff_in_{l}")
        g3, g2 = by_owner(g3), by_owner(g2)
        (dx1, dx1b, dg_mlp[l]), got = _mm_nn_normbwd(du, full[l, 2], x1, dx, g_mlp[l:l + 1], f"ff_in_bwd_{l}",
                                                    _sibling_comm([g3, g2]))
        partial[l, 3], partial[l, 2] = pair((l, 3), g3, got[0]), pair((l, 2), g2, got[1])
        (g1,), _ = _mm_tn(y, dx1b, f"grad_w_o_{l}")
        g1 = by_owner(g1)
        (dya, dyb, dyc, dg_group[l]), got = _mix_bwd(dx1b, full[l, 1], ya, yb, yc, g_group[l:l + 1],
                                                     f"mix_out_bwd_{l}", _sibling_comm([g1]) if l == 0 else None)
        if l == 0:
            partial[l, 1] = pair((l, 1), g1, got[0])
        early = [(l, 3), (l, 2)] + ([(l, 1)] if l == 0 else [])
        (dz, _), got = _attn_bwd(z, dya, ya, lse_a, no_sink, None, QA_BLK, KA_BLK, VA_BLK, DILATED_PATTERNS,
                                 A_MAX_DIST, False, f"dilated_attn_bwd_{l}", _chip_comm([partial[k] for k in early]))
        r2.update(zip(early, got))
        dz, dcw = _conv_bwd(z, conv_full[l], dyb, dz, f"conv_bwd_{l}")
        (dz, dsink), _ = _attn_bwd(z, dyc, yc, lse_c, sink_l, dz, QC_BLK, KC_BLK, VC_BLK, (1,), C_MAX_DIST,
                                   True, f"window_attn_bwd_{l}")
        (g0,), _ = _mm_tn(dz, h, f"grad_w_in_{l}")
        g0 = by_owner(g0)
        if l > 0:
            (dx, dxb, dg_mix[l]), got = _mm_nn_normbwd(dz, full[l, 0], xin, dx1, g_mix[l:l + 1], f"in_proj_bwd_{l}",
                                                      _sibling_comm([g1, g0]))
            partial[l, 1], partial[l, 0] = pair((l, 1), g1, got[0]), pair((l, 0), g0, got[1])
        else:
            (r1,) = _comm_only(_sibling_comm([g0]), "grad_sibling_exchange_last")
            partial[l, 0] = pair((l, 0), g0, r1)
            (dx, dxb, dg_mix[l]), got = _mm_nn_normbwd(dz, full[l, 0], xin, dx1, g_mix[l:l + 1], f"in_proj_bwd_{l}",
                                                      _chip_comm([partial[l, 0]]))
            r2[l, 0] = got[0]
        dconv[l] = dcw[:3]
        dsinks[l] = dsink[0, ::HEAD_DIM]
    gsum = {key: _final_sum(partial[key], r2[key], pos, f"grad_final_sum_{key[0]}_{key[1]}") for key in partial}
    small = _pack_small(jnp.concatenate(dg_mix), jnp.concatenate(dg_group), jnp.concatenate(dg_mlp),
                        dg_final, jnp.stack(dconv), jnp.stack(dsinks), loss_slab[0:1])
    return dx, gsum, small


def _finish(dx, gsum, small, dev, w_in, conv_w, sinks, g_mix, g_group, w_o, g_mlp, w_ff_in, w_ff_out, g_final, m_w_in, m_conv_w, m_sinks, m_g_mix, m_g_group, m_w_o, m_g_mlp, m_w_ff_in, m_w_ff_out, m_g_final, v_w_in, v_conv_w, v_sinks, v_g_mix, v_g_group, v_w_o, v_g_mlp, v_w_ff_in, v_w_ff_out, v_g_final):
    grad_x = dx.reshape(1, SEQ, D_MODEL)
    grad_w_in = jnp.stack([gsum[l, 0].T for l in range(DEPTH)])
    grad_w_o = jnp.stack([gsum[l, 1] for l in range(DEPTH)])
    grad_w_ff_in = jnp.stack([gsum[l, 2].T for l in range(DEPTH)])
    grad_w_ff_out = jnp.stack([gsum[l, 3] for l in range(DEPTH)])

    (small_all,) = _comm_only(_gather_comm([small]), "gather_small_grads")
    zeros_conv = jnp.zeros((DEPTH, 3, CONV_CH), F32)
    sw = _pack_small(g_mix, g_group, g_mlp, g_final, zeros_conv, sinks)
    sm = _pack_small(m_g_mix, m_g_group, m_g_mlp, m_g_final, zeros_conv, m_sinks)
    sv = _pack_small(v_g_mix, v_g_group, v_g_mlp, v_g_final, zeros_conv, v_sinks)
    sg, sd, snm, snv = _small_sum_adamw(small_all.reshape(N_DEV, SMALL_ROWS, D_MODEL), sw, sm, sv, "small_adamw")
    loss = sg[LOSS_ROW, 0]
    grad_g_mix, grad_g_group, grad_g_mlp, grad_g_final, conv_grad_full, grad_sinks = _unpack_small(sg)
    delta_g_mix, delta_g_group, delta_g_mlp, delta_g_final, _, delta_sinks = _unpack_small(sd)
    new_m_g_mix, new_m_g_group, new_m_g_mlp, new_m_g_final, _, new_m_sinks = _unpack_small(snm)
    new_v_g_mix, new_v_g_group, new_v_g_mlp, new_v_g_final, _, new_v_sinks = _unpack_small(snv)
    cs = CONV_CH // N_DEV
    grad_conv_w = lax.dynamic_slice_in_dim(conv_grad_full, dev * cs, cs, axis=2)

    def tile_of(t):
        return jnp.pad(t.reshape(1, DEPTH * 3 * cs), ((0, 7), (0, 256 - DEPTH * 3 * cs)))

    cd, cm, cv = _adamw(tile_of(conv_w), tile_of(grad_conv_w), tile_of(m_conv_w), tile_of(v_conv_w), "conv_adamw")
    untile = lambda t: t[0, :DEPTH * 3 * cs].reshape(DEPTH, 3, cs)
    delta_conv_w, new_m_conv_w, new_v_conv_w = untile(cd), untile(cm), untile(cv)

    def big(w, g, m, v, name):
        shp = w.shape
        flat = lambda t: t.reshape(shp[0] * shp[1], shp[2])
        return [t.reshape(shp) for t in _adamw(flat(w), flat(g), flat(m), flat(v), name)]

    delta_w_in, new_m_w_in, new_v_w_in = big(w_in, grad_w_in, m_w_in, v_w_in, "adamw_w_in")
    delta_w_o, new_m_w_o, new_v_w_o = big(w_o, grad_w_o, m_w_o, v_w_o, "adamw_w_o")
    delta_w_ff_in, new_m_w_ff_in, new_v_w_ff_in = big(w_ff_in, grad_w_ff_in, m_w_ff_in, v_w_ff_in, "adamw_w_ff_in")
    delta_w_ff_out, new_m_w_ff_out, new_v_w_ff_out = big(w_ff_out, grad_w_ff_out, m_w_ff_out, v_w_ff_out, "adamw_w_ff_out")

    return (loss, grad_x, grad_w_in, grad_conv_w, grad_sinks, grad_g_mix, grad_g_group, grad_w_o, grad_g_mlp,
            grad_w_ff_in, grad_w_ff_out, grad_g_final,
            delta_w_in, delta_conv_w, delta_sinks, delta_g_mix, delta_g_group, delta_w_o, delta_g_mlp,
            delta_w_ff_in, delta_w_ff_out, delta_g_final,
            new_m_w_in, new_m_conv_w, new_m_sinks, new_m_g_mix, new_m_g_group, new_m_w_o, new_m_g_mlp,
            new_m_w_ff_in, new_m_w_ff_out, new_m_g_final,
            new_v_w_in, new_v_conv_w, new_v_sinks, new_v_g_mix, new_v_g_group, new_v_w_o, new_v_g_mlp,
            new_v_w_ff_in, new_v_w_ff_out, new_v_g_final)
```

```python
from typing import Callable, NamedTuple

import jax
import jax.numpy as jnp
from jax import lax
from jax.experimental import pallas as pl
from jax.experimental.pallas import tpu as pltpu

F32 = jnp.float32
BF16 = jnp.bfloat16
MESH = pl.DeviceIdType.MESH

N_DEV = 8
SEQ = 4096
D_MODEL = 1024
DEPTH = 2
HEAD_DIM = 64
LANES = 128
A_WIDTH = 384
CONV_CH = 256
C_WIDTH = 384
KV_WIDTH = 128
IN_WIDTH = 2560
D_FF = 4096
BLOCK = 128
DILATED_PATTERNS = (1, 4, 16)
A_MAX_DIST = 128
C_MAX_DIST = 127
EPS = 1e-6
SCALE = HEAD_DIM ** -0.5
NEG_BIG = -1e30
F32_TINY = 1.1754944e-38

QA_BLK, KA_BLK, VA_BLK = 0, 3, 6
GB_BLK, GC_BLK, XB_BLK = 9, 11, 13
QC_BLK, KC_BLK, VC_BLK = 15, 18, 19

ADAM_LR = 0.001
ADAM_B1 = 0.9
ADAM_B2 = 0.999
ADAM_EPS = 1e-08
ADAM_WD = 0.01
ADAM_STEP = 10

VMEM_LIMIT = 56 * 1024 * 1024
ROW_TILE = 512
COL_CHUNK = 512
SMALL_ROWS = 48


def _dot_nn(a, b):
    return lax.dot_general(a, b, (((1,), (0,)), ((), ())), preferred_element_type=F32)


def _dot_nt(a, b):
    return lax.dot_general(a, b, (((1,), (1,)), ((), ())), preferred_element_type=F32)


def _dot_tn(a, b):
    return lax.dot_general(a, b, (((0,), (0,)), ((), ())), preferred_element_type=F32)


def _params(*sem):
    return pltpu.CompilerParams(dimension_semantics=sem, vmem_limit_bytes=VMEM_LIMIT)


def _rms_scale(t):
    return lax.rsqrt(jnp.mean(t * t, axis=-1, keepdims=True) + EPS)


def _rms_bwd(n, r, dn):
    return r * (dn - n * jnp.mean(dn * n, axis=-1, keepdims=True))


class _Comm(NamedTuple):
    arrays: tuple
    out_shape: tuple
    sems: tuple
    start: Callable
    finish: Callable


def _call(body, grid, in_specs, out_specs, out_shape, operands, name, scratch_shapes=(), comm=None, aliases=None):
    n_in, n_out, n_scr = len(in_specs), len(out_shape), len(scratch_shapes)
    aliases = dict(aliases or {})
    if comm is None:
        res = pl.pallas_call(body, grid=grid, in_specs=list(in_specs), out_specs=list(out_specs),
                             out_shape=list(out_shape), scratch_shapes=list(scratch_shapes),
                             input_output_aliases=aliases,
                             compiler_params=_params("arbitrary"), name=name)(*operands)
        return list(res), []
    c_in, c_out = len(comm.arrays), len(comm.out_shape)
    hbm = pl.BlockSpec(memory_space=pl.ANY)
    last = grid[0] - 1

    def carried(*refs):
        ins, cins = refs[:n_in], refs[n_in:n_in + c_in]
        o0 = n_in + c_in
        outs, couts = refs[o0:o0 + n_out], refs[o0 + n_out:o0 + n_out + c_out]
        s0 = o0 + n_out + c_out
        scr, sems = refs[s0:s0 + n_scr], refs[s0 + n_scr:]
        pl.when(pl.program_id(0) == 0)(lambda: comm.start(cins, couts, sems))
        body(*ins, *outs, *scr)
        pl.when(pl.program_id(0) == last)(lambda: comm.finish(cins, couts, sems))

    res = pl.pallas_call(carried, grid=grid, in_specs=list(in_specs) + [hbm] * c_in,
                         out_specs=list(out_specs) + [hbm] * c_out, out_shape=list(out_shape) + list(comm.out_shape),
                         scratch_shapes=list(scratch_shapes) + list(comm.sems), input_output_aliases=aliases,
                         compiler_params=_params("arbitrary"), name=name)(*operands, *comm.arrays)
    return list(res[:n_out]), list(res[n_out:])


def _comm_only(comm, name):
    hbm = pl.BlockSpec(memory_space=pl.ANY)
    c_in, c_out = len(comm.arrays), len(comm.out_shape)

    def body(*refs):
        ins, outs, sems = refs[:c_in], refs[c_in:c_in + c_out], refs[c_in + c_out:]
        comm.start(ins, outs, sems)
        comm.finish(ins, outs, sems)

    return pl.pallas_call(body, in_specs=[hbm] * c_in, out_specs=[hbm] * c_out, out_shape=list(comm.out_shape),
                          scratch_shapes=list(comm.sems), name=name)(*comm.arrays)


def _norm_mm(x, g, wt, relu2, name, comm=None):
    s, d = x.shape
    n = wt.shape[0]
    tm = ROW_TILE

    def body(x_ref, g_ref, w_ref, o_ref, h_ref):
        xx = x_ref[...]
        h = ((xx * _rms_scale(xx)) * g_ref[...]).astype(BF16)
        h_ref[...] = h
        for n0 in range(0, n, COL_CHUNK):
            zc = _dot_nt(h, w_ref[n0:n0 + COL_CHUNK, :])
            if relu2:
                zc = jnp.square(jnp.maximum(zc, 0.0)).astype(BF16)
            o_ref[:, n0:n0 + COL_CHUNK] = zc

    return _call(
        body,
        grid=(s // tm,),
        in_specs=[pl.BlockSpec((tm, d), lambda i: (i, 0)),
                  pl.BlockSpec((1, d), lambda i: (0, 0)),
                  pl.BlockSpec((n, d), lambda i: (0, 0))],
        out_specs=[pl.BlockSpec((tm, n), lambda i: (i, 0)),
                   pl.BlockSpec((tm, d), lambda i: (i, 0))],
        out_shape=[jax.ShapeDtypeStruct((s, n), BF16 if relu2 else F32), jax.ShapeDtypeStruct((s, d), BF16)],
        operands=(x, g, wt), name=name, comm=comm)


def _mm_res(a, w2, x1, name, comm=None):
    s, f = a.shape
    d = w2.shape[1]
    tm = ROW_TILE

    def body(a_ref, w_ref, x_ref, o_ref):
        o_ref[...] = x_ref[...] + _dot_nn(a_ref[...], w_ref[...])

    return _call(
        body,
        grid=(s // tm,),
        in_specs=[pl.BlockSpec((tm, f), lambda i: (i, 0)),
                  pl.BlockSpec((f, d), lambda i: (0, 0)),
                  pl.BlockSpec((tm, d), lambda i: (i, 0))],
        out_specs=[pl.BlockSpec((tm, d), lambda i: (i, 0))],
        out_shape=[jax.ShapeDtypeStruct((s, d), F32)],
        operands=(a, w2, x1), name=name, comm=comm)


def _mix_out(ya, yb, yc, gg, wo, x0, name, comm=None):
    s = ya.shape[0]
    d = wo.shape[1]
    tm = ROW_TILE

    def body(ya_ref, yb_ref, yc_ref, g_ref, w_ref, x_ref, y_ref, o_ref):
        parts = []
        for ref in (ya_ref, yb_ref, yc_ref):
            t = ref[...]
            parts.append(t * _rms_scale(t))
        y = (jnp.concatenate(parts, axis=1) * g_ref[...]).astype(BF16)
        y_ref[...] = y
        o_ref[...] = x_ref[...] + _dot_nn(y, w_ref[...])

    return _call(
        body,
        grid=(s // tm,),
        in_specs=[pl.BlockSpec((tm, A_WIDTH), lambda i: (i, 0)),
                  pl.BlockSpec((tm, CONV_CH), lambda i: (i, 0)),
                  pl.BlockSpec((tm, C_WIDTH), lambda i: (i, 0)),
                  pl.BlockSpec((1, d), lambda i: (0, 0)),
                  pl.BlockSpec((d, d), lambda i: (0, 0)),
                  pl.BlockSpec((tm, d), lambda i: (i, 0))],
        out_specs=[pl.BlockSpec((tm, d), lambda i: (i, 0)),
                   pl.BlockSpec((tm, d), lambda i: (i, 0))],
        out_shape=[jax.ShapeDtypeStruct((s, d), BF16), jax.ShapeDtypeStruct((s, d), F32)],
        operands=(ya, yb, yc, gg, wo, x0), name=name, comm=comm)


def _loss_head(x, g, target, name):
    s, d = x.shape
    tm = ROW_TILE

    def body(x_ref, g_ref, t_ref, loss_ref, dx_ref, dxb_ref, dg_ref):
        @pl.when(pl.program_id(0) == 0)
        def _():
            loss_ref[...] = jnp.zeros_like(loss_ref)
            dg_ref[...] = jnp.zeros_like(dg_ref)

        xx = x_ref[...]
        r = _rms_scale(xx)
        n = xx * r
        gv = g_ref[...]
        err = n * gv - t_ref[...]
        per_tok = jnp.sum(err * err, axis=1, keepdims=True) * (1.0 / d)
        loss_ref[...] += 0.5 * jnp.sum(per_tok, axis=0, keepdims=True)
        dout = err * (1.0 / d)
        dg_ref[...] += jnp.sum(dout * n, axis=0, keepdims=True)
        dx = _rms_bwd(n, r, dout * gv)
        dx_ref[...] = dx
        dxb_ref[...] = dx.astype(BF16)

    return pl.pallas_call(
        body,
        grid=(s // tm,),
        in_specs=[pl.BlockSpec((tm, d), lambda i: (i, 0)),
                  pl.BlockSpec((1, d), lambda i: (0, 0)),
                  pl.BlockSpec((tm, d), lambda i: (i, 0))],
        out_specs=[pl.BlockSpec((8, LANES), lambda i: (0, 0)),
                   pl.BlockSpec((tm, d), lambda i: (i, 0)),
                   pl.BlockSpec((tm, d), lambda i: (i, 0)),
                   pl.BlockSpec((1, d), lambda i: (0, 0))],
        out_shape=[jax.ShapeDtypeStruct((8, LANES), F32), jax.ShapeDtypeStruct((s, d), F32),
                   jax.ShapeDtypeStruct((s, d), BF16), jax.ShapeDtypeStruct((1, d), F32)],
        compiler_params=_params("arbitrary"),
        name=name,
    )(x, g, target)


def _mlp_bwd_act(dxb, w2, a, name, comm=None):
    s, d = dxb.shape
    f = w2.shape[0]
    tm = ROW_TILE

    def body(dx_ref, w_ref, a_ref, du_ref):
        dx = dx_ref[...]
        for n0 in range(0, f, COL_CHUNK):
            da = _dot_nt(dx, w_ref[n0:n0 + COL_CHUNK, :])
            av = a_ref[:, n0:n0 + COL_CHUNK].astype(F32)
            rl = av * lax.rsqrt(jnp.maximum(av, F32_TINY))
            du_ref[:, n0:n0 + COL_CHUNK] = (da * (2.0 * rl)).astype(BF16)

    return _call(
        body,
        grid=(s // tm,),
        in_specs=[pl.BlockSpec((tm, d), lambda i: (i, 0)),
                  pl.BlockSpec((f, d), lambda i: (0, 0)),
                  pl.BlockSpec((tm, f), lambda i: (i, 0))],
        out_specs=[pl.BlockSpec((tm, f), lambda i: (i, 0))],
        out_shape=[jax.ShapeDtypeStruct((s, f), BF16)],
        operands=(dxb, w2, a), name=name, comm=comm)


def _mm_tn(a, b, name, comm=None):
    s, n = a.shape
    d = b.shape[1]
    tn = 512

    def body(a_ref, b_ref, o_ref, acc):
        for k0 in range(0, s, ROW_TILE):
            part = _dot_tn(a_ref[k0:k0 + ROW_TILE, :], b_ref[k0:k0 + ROW_TILE, :])
            if k0 == 0:
                acc[...] = part
            else:
                acc[...] += part
        o_ref[...] = acc[...].astype(BF16)

    return _call(
        body,
        grid=(n // tn,),
        in_specs=[pl.BlockSpec((s, tn), lambda j: (0, j)),
                  pl.BlockSpec((s, d), lambda j: (0, 0))],
        out_specs=[pl.BlockSpec((tn, d), lambda j: (j, 0))],
        out_shape=[jax.ShapeDtypeStruct((n, d), BF16)],
        operands=(a, b), name=name, scratch_shapes=[pltpu.VMEM((tn, d), F32)], comm=comm)


def _mm_nn_normbwd(dact, wt, x, dres, g, name, comm=None):
    s, kdim = dact.shape
    d = wt.shape[1]
    tm = ROW_TILE

    def body(a_ref, w_ref, x_ref, r_ref, g_ref, o_ref, ob_ref, dg_ref):
        @pl.when(pl.program_id(0) == 0)
        def _():
            dg_ref[...] = jnp.zeros_like(dg_ref)

        dh = _dot_nn(a_ref[...], w_ref[...])
        xx = x_ref[...]
        r = _rms_scale(xx)
        n = xx * r
        dg_ref[...] += jnp.sum(dh * n, axis=0, keepdims=True)
        dx = r_ref[...] + _rms_bwd(n, r, dh * g_ref[...])
        o_ref[...] = dx
        ob_ref[...] = dx.astype(BF16)

    return _call(
        body,
        grid=(s // tm,),
        in_specs=[pl.BlockSpec((tm, kdim), lambda i: (i, 0)),
                  pl.BlockSpec((kdim, d), lambda i: (0, 0)),
                  pl.BlockSpec((tm, d), lambda i: (i, 0)),
                  pl.BlockSpec((tm, d), lambda i: (i, 0)),
                  pl.BlockSpec((1, d), lambda i: (0, 0))],
        out_specs=[pl.BlockSpec((tm, d), lambda i: (i, 0)),
                   pl.BlockSpec((tm, d), lambda i: (i, 0)),
                   pl.BlockSpec((1, d), lambda i: (0, 0))],
        out_shape=[jax.ShapeDtypeStruct((s, d), F32), jax.ShapeDtypeStruct((s, d), BF16),
                   jax.ShapeDtypeStruct((1, d), F32)],
        operands=(dact, wt, x, dres, g), name=name, comm=comm)


def _mix_bwd(dx1, wo, ya, yb, yc, gg, name, comm=None):
    s, d = dx1.shape
    tm = ROW_TILE
    widths = (A_WIDTH, CONV_CH, C_WIDTH)

    def body(dx_ref, w_ref, ya_ref, yb_ref, yc_ref, g_ref, da_ref, db_ref, dc_ref, dg_ref):
        @pl.when(pl.program_id(0) == 0)
        def _():
            dg_ref[...] = jnp.zeros_like(dg_ref)

        dy = _dot_nt(dx_ref[...], w_ref[...])
        gv = g_ref[...]
        off = 0
        dgs = []
        for ref, out, w in zip((ya_ref, yb_ref, yc_ref), (da_ref, db_ref, dc_ref), widths):
            t = ref[...]
            r = _rms_scale(t)
            n = t * r
            dyg = dy[:, off:off + w]
            dgs.append(jnp.sum(dyg * n, axis=0, keepdims=True))
            out[...] = _rms_bwd(n, r, dyg * gv[:, off:off + w])
            off += w
        dg_ref[...] += jnp.concatenate(dgs, axis=1)

    return _call(
        body,
        grid=(s // tm,),
        in_specs=[pl.BlockSpec((tm, d), lambda i: (i, 0)),
                  pl.BlockSpec((d, d), lambda i: (0, 0)),
                  pl.BlockSpec((tm, A_WIDTH), lambda i: (i, 0)),
                  pl.BlockSpec((tm, CONV_CH), lambda i: (i, 0)),
                  pl.BlockSpec((tm, C_WIDTH), lambda i: (i, 0)),
                  pl.BlockSpec((1, d), lambda i: (0, 0))],
        out_specs=[pl.BlockSpec((tm, A_WIDTH), lambda i: (i, 0)),
                   pl.BlockSpec((tm, CONV_CH), lambda i: (i, 0)),
                   pl.BlockSpec((tm, C_WIDTH), lambda i: (i, 0)),
                   pl.BlockSpec((1, d), lambda i: (0, 0))],
        out_shape=[jax.ShapeDtypeStruct((s, A_WIDTH), F32), jax.ShapeDtypeStruct((s, CONV_CH), F32),
                   jax.ShapeDtypeStruct((s, C_WIDTH), F32), jax.ShapeDtypeStruct((1, d), F32)],
        operands=(dx1, wo, ya, yb, yc, gg), name=name, comm=comm)


CONV_CHUNK = 256
CONV_HALO = 8


def _conv_fwd(z, cw, name):
    s = z.shape[0]
    nch = s // CONV_CHUNK

    def body(gb_ref, gc_ref, xb_ref, w_ref, o_ref, us):
        us[pl.ds(0, CONV_HALO), :] = jnp.zeros((CONV_HALO, LANES), F32)
        us[pl.ds(CONV_HALO, s), :] = gc_ref[...] * xb_ref[...]
        w0, w1, w2 = w_ref[0:1, :], w_ref[1:2, :], w_ref[2:3, :]

        def chunk(c, carry):
            st = pl.multiple_of(c * CONV_CHUNK, CONV_CHUNK)
            ext = us[pl.ds(st, CONV_CHUNK + CONV_HALO), :]
            y = (w0 * ext[CONV_HALO - 2:CONV_HALO - 2 + CONV_CHUNK]
                 + w1 * ext[CONV_HALO - 1:CONV_HALO - 1 + CONV_CHUNK]
                 + w2 * ext[CONV_HALO:])
            o_ref[pl.ds(st, CONV_CHUNK), :] = gb_ref[pl.ds(st, CONV_CHUNK), :] * y
            return carry

        lax.fori_loop(0, nch, chunk, 0)

    col = lambda blk: pl.BlockSpec((s, LANES), lambda j, blk=blk: (0, blk + j))
    return pl.pallas_call(
        body,
        grid=(CONV_CH // LANES,),
        in_specs=[col(GB_BLK), col(GC_BLK), col(XB_BLK), pl.BlockSpec((3, LANES), lambda j: (0, j))],
        out_specs=pl.BlockSpec((s, LANES), lambda j: (0, j)),
        out_shape=jax.ShapeDtypeStruct((s, CONV_CH), F32),
        scratch_shapes=[pltpu.VMEM((s + CONV_HALO, LANES), F32)],
        compiler_params=_params("parallel"),
        name=name,
    )(z, z, z, cw)


def _conv_bwd(z, cw, dyb, dz, name):
    s = z.shape[0]
    nch = s // CONV_CHUNK
    ncol = CONV_CH // LANES

    def body(gb_ref, gc_ref, xb_ref, w_ref, dy_ref, dz_in, dz_ref, dw_ref, us, ds_, dgb_ref, dgc_ref, dxb_ref, sems):
        j = pl.program_id(0)

        def to_dz(staged, blk, k):
            cols = pl.ds(pl.multiple_of((blk + j) * LANES, LANES), LANES)
            return pltpu.make_async_copy(staged, dz_ref.at[:, cols], sems.at[k])

        copies = [to_dz(dgb_ref, GB_BLK, 0), to_dz(dgc_ref, GC_BLK, 1), to_dz(dxb_ref, XB_BLK, 2)]

        @pl.when(j > 0)
        def _():
            for cp in copies:
                cp.wait()

        us[pl.ds(0, CONV_HALO), :] = jnp.zeros((CONV_HALO, LANES), F32)
        us[pl.ds(CONV_HALO, s), :] = gc_ref[...] * xb_ref[...]
        ds_[pl.ds(s, CONV_HALO), :] = jnp.zeros((CONV_HALO, LANES), F32)
        ds_[pl.ds(0, s), :] = dy_ref[...] * gb_ref[...]
        w0, w1, w2 = w_ref[0:1, :], w_ref[1:2, :], w_ref[2:3, :]
        zero = jnp.zeros((1, LANES), F32)

        def chunk(c, carry):
            a0, a1, a2 = carry
            st = pl.multiple_of(c * CONV_CHUNK, CONV_CHUNK)
            rows = pl.ds(st, CONV_CHUNK)
            ext = us[pl.ds(st, CONV_CHUNK + CONV_HALO), :]
            um2 = ext[CONV_HALO - 2:CONV_HALO - 2 + CONV_CHUNK]
            um1 = ext[CONV_HALO - 1:CONV_HALO - 1 + CONV_CHUNK]
            u0 = ext[CONV_HALO:]
            dext = ds_[pl.ds(st, CONV_CHUNK + CONV_HALO), :]
            dc0 = dext[:CONV_CHUNK]
            du = w2 * dc0 + w1 * dext[1:1 + CONV_CHUNK] + w0 * dext[2:2 + CONV_CHUNK]
            yconv = w0 * um2 + w1 * um1 + w2 * u0
            dgb_ref[rows, :] = (dy_ref[rows, :] * yconv).astype(BF16)
            dgc_ref[rows, :] = (du * xb_ref[rows, :]).astype(BF16)
            dxb_ref[rows, :] = (du * gc_ref[rows, :]).astype(BF16)
            a0 = a0 + jnp.sum(dc0 * um2, axis=0, keepdims=True)
            a1 = a1 + jnp.sum(dc0 * um1, axis=0, keepdims=True)
            a2 = a2 + jnp.sum(dc0 * u0, axis=0, keepdims=True)
            return a0, a1, a2

        a0, a1, a2 = lax.fori_loop(0, nch, chunk, (zero, zero, zero))
        dw_ref[...] = jnp.concatenate([a0, a1, a2, jnp.zeros((5, LANES), F32)], axis=0)
        for cp in copies:
            cp.start()

        @pl.when(j == ncol - 1)
        def _():
            for cp in copies:
                cp.wait()

    col = lambda blk: pl.BlockSpec((s, LANES), lambda j, blk=blk: (0, blk + j))
    hbm = pl.BlockSpec(memory_space=pl.ANY)
    return pl.pallas_call(
        body,
        grid=(ncol,),
        in_specs=[col(GB_BLK), col(GC_BLK), col(XB_BLK), pl.BlockSpec((3, LANES), lambda j: (0, j)),
                  pl.BlockSpec((s, LANES), lambda j: (0, j)), hbm],
        out_specs=[hbm, pl.BlockSpec((8, LANES), lambda j: (0, j))],
        out_shape=[jax.ShapeDtypeStruct(dz.shape, dz.dtype), jax.ShapeDtypeStruct((8, CONV_CH), F32)],
        scratch_shapes=[pltpu.VMEM((s + CONV_HALO, LANES), F32), pltpu.VMEM((s + CONV_HALO, LANES), F32)]
        + [pltpu.VMEM((s, LANES), BF16)] * 3 + [pltpu.SemaphoreType.DMA((3,))],
        input_output_aliases={5: 0},
        compiler_params=_params("arbitrary"),
        name=name,
    )(z, z, z, cw, dyb, dz)


ATTN_ROWS = 512
ATTN_UNROLL = 8


def _band_rows(b, d, r):
    base = pl.multiple_of(b * (BLOCK * d), BLOCK)
    prev = jnp.maximum(base - BLOCK * d, 0)
    if d == 1:
        return pl.ds(base, BLOCK), pl.ds(pl.multiple_of(prev, BLOCK), BLOCK)
    return pl.ds(base + r, BLOCK, stride=d), pl.ds(prev + r, BLOCK, stride=d)


def _write_band_bias(bias_ref, max_dist):
    qi = lax.broadcasted_iota(jnp.int32, (BLOCK, 2 * BLOCK), 0)
    kj = lax.broadcasted_iota(jnp.int32, (BLOCK, 2 * BLOCK), 1)
    dist = BLOCK + qi - kj
    band = (dist >= 0) & (dist <= max_dist)
    bias_ref[0:BLOCK, :] = jnp.where(band, 0.0, -jnp.inf)
    bias_ref[BLOCK:2 * BLOCK, :] = jnp.where(band & (kj >= BLOCK), 0.0, -jnp.inf)


def _band_bias(bias_ref, b):
    bias = bias_ref[pl.ds(pl.multiple_of(jnp.where(b > 0, 0, BLOCK), BLOCK), BLOCK), :]
    return jnp.concatenate([bias, bias], axis=0)


def _lane_half():
    return (lax.broadcasted_iota(jnp.int32, (1, LANES), 1) >= HEAD_DIM).astype(jnp.int32)


def _kv_for_pair(t, pair):
    half = _lane_half()
    want = (pair + half) >> 1
    return jnp.where(want != half, pltpu.roll(t, HEAD_DIM, 1), t)


def _kv_grad_from_pair(t, pair):
    half = _lane_half()
    mine = ((pair + half) >> 1) == half
    other = ((pair + 1 - half) >> 1) == half
    fold = t + pltpu.roll(t, HEAD_DIM, 1)
    return jnp.where(mine & other, fold, jnp.where(mine, t, 0.0))


def _stack_heads(t, head0):
    zero = jnp.zeros_like(t)
    return jnp.concatenate([jnp.where(head0, t, zero), jnp.where(head0, zero, t)], axis=0)


def _unstack_heads(t, head0):
    return jnp.where(head0, t[:BLOCK], t[BLOCK:])


def _block_loops(s, patterns, unroll, one_block):
    for d in patterns:
        nb = (s // BLOCK) // d
        ur = min(unroll, d)
        ub = unroll // ur
        for r0 in range(0, d, ur):
            def trip(i, carry, d=d, r0=r0, ur=ur, ub=ub):
                for u in range(ub):
                    for r in range(r0, r0 + ur):
                        one_block(i * ub + u, d, r)
                return carry
            lax.fori_loop(0, nb // ub, trip, 0)


def _attn_fwd(z, m_init, l_init, q_blk, k_blk, v_blk, patterns, max_dist, gqa, name, comm=None):
    s = z.shape[0]
    npair = 3

    def body(q_ref, k_ref, v_ref, mi_ref, o_ref, lse_ref, m_scr, l_scr, bias_scr, *kv_scr):
        pair = pl.program_id(0)
        head0 = lax.broadcasted_iota(jnp.int32, (1, LANES), 1) < HEAD_DIM
        _write_band_bias(bias_scr, max_dist)
        k_src, v_src = kv_scr if gqa else (k_ref, v_ref)

        def init(c, carry):
            rows = pl.ds(pl.multiple_of(c * ATTN_ROWS, ATTN_ROWS), ATTN_ROWS)
            m_scr[rows, :] = jnp.broadcast_to(mi_ref[...], (ATTN_ROWS, LANES))
            l_scr[rows, :] = jnp.full((ATTN_ROWS, LANES), l_init, F32)
            o_ref[rows, :] = jnp.zeros((ATTN_ROWS, LANES), F32)
            if gqa:
                k_src[rows, :] = _kv_for_pair(k_ref[rows, :], pair)
                v_src[rows, :] = _kv_for_pair(v_ref[rows, :], pair)
            return carry

        lax.fori_loop(0, s // ATTN_ROWS, init, 0)
        ones = jnp.ones((2 * BLOCK, LANES), BF16)

        def one_block(b, d, r):
            rq, rp = _band_rows(b, d, r)
            q2 = _stack_heads((q_ref[rq, :] * SCALE).astype(BF16), head0)
            k2 = jnp.concatenate([k_src[rp, :], k_src[rq, :]], axis=0)
            v2 = jnp.concatenate([v_src[rp, :], v_src[rq, :]], axis=0)
            sc = _dot_nt(q2, k2.astype(BF16)) + _band_bias(bias_scr, b)
            mb = jnp.max(sc, axis=1, keepdims=True)
            p = jnp.exp(sc - mb).astype(BF16)
            ob = _dot_nn(p, jnp.concatenate([v2.astype(BF16), ones], axis=1))
            m2 = _unstack_heads(jnp.broadcast_to(mb, (2 * BLOCK, LANES)), head0)
            l2 = _unstack_heads(ob[:, LANES:], head0)
            o2 = _unstack_heads(ob[:, :LANES], head0)
            m_old = m_scr[rq, :]
            m_new = jnp.maximum(m_old, m2)
            a_old = jnp.exp(m_old - m_new)
            a_blk = jnp.exp(m2 - m_new)
            o_ref[rq, :] = o_ref[rq, :] * a_old + o2 * a_blk
            l_scr[rq, :] = l_scr[rq, :] * a_old + l2 * a_blk
            m_scr[rq, :] = m_new

        _block_loops(s, patterns, ATTN_UNROLL, one_block)

        def fin(c, carry):
            rows = pl.ds(pl.multiple_of(c * ATTN_ROWS, ATTN_ROWS), ATTN_ROWS)
            l = l_scr[rows, :]
            o_ref[rows, :] = o_ref[rows, :] / l
            lse = m_scr[rows, :] + jnp.log(l)
            swapped = pltpu.roll(lse, HEAD_DIM, 1)
            lse_ref[rows, 0:LANES] = jnp.where(head0, lse, swapped)
            lse_ref[rows, LANES:2 * LANES] = jnp.where(head0, swapped, lse)
            return carry

        lax.fori_loop(0, s // ATTN_ROWS, fin, 0)

    kv = (lambda blk: pl.BlockSpec((s, LANES), lambda j, blk=blk: (0, blk), pipeline_mode=pl.Buffered(1))) if gqa \
        else (lambda blk: pl.BlockSpec((s, LANES), lambda j, blk=blk: (0, blk + j)))
    return _call(
        body,
        grid=(npair,),
        in_specs=[pl.BlockSpec((s, LANES), lambda j: (0, q_blk + j)), kv(k_blk), kv(v_blk),
                  pl.BlockSpec((1, LANES), lambda j: (0, j))],
        out_specs=[pl.BlockSpec((s, LANES), lambda j: (0, j)), pl.BlockSpec((s, 2 * LANES), lambda j: (0, j))],
        out_shape=[jax.ShapeDtypeStruct((s, npair * LANES), F32), jax.ShapeDtypeStruct((s, 2 * npair * LANES), F32)],
        operands=(z, z, z, m_init), name=name,
        scratch_shapes=[pltpu.VMEM((s, LANES), F32)] * 2 + [pltpu.VMEM((2 * BLOCK, 2 * BLOCK), F32)]
        + [pltpu.VMEM((s, LANES), F32)] * (2 if gqa else 0), comm=comm)


def _attn_bwd(z, do, o, lse, m_init, dz, q_blk, k_blk, v_blk, patterns, max_dist, gqa, name, comm=None):
    s = z.shape[0]
    npair = 3
    n_dz_in = 0 if dz is None else 1

    def body(q_ref, k_ref, v_ref, do_ref, o_ref, lse0_ref, lse1_ref, mi_ref, *rest):
        (dz_ref, dm_ref, dq_acc, dk_acc, dv_acc, dl0_scr, dl1_scr, bias_scr,
         dq_out, dk_out, dv_out, out_sems, *gqa_scr) = rest[n_dz_in:]
        pair = pl.program_id(0)
        head0 = lax.broadcasted_iota(jnp.int32, (1, LANES), 1) < HEAD_DIM
        _write_band_bias(bias_scr, max_dist)
        k_src, v_src, dk_sum, dv_sum = gqa_scr if gqa else (k_ref, v_ref, None, None)

        def prep(c, dm):
            rows = pl.ds(pl.multiple_of(c * ATTN_ROWS, ATTN_ROWS), ATTN_ROWS)
            dq_acc[rows, :] = jnp.zeros((ATTN_ROWS, LANES), F32)
            dk_acc[rows, :] = jnp.zeros((ATTN_ROWS, LANES), F32)
            dv_acc[rows, :] = jnp.zeros((ATTN_ROWS, LANES), F32)
            if gqa:
                k_src[rows, :] = _kv_for_pair(k_ref[rows, :], pair)
                v_src[rows, :] = _kv_for_pair(v_ref[rows, :], pair)
            prod = do_ref[rows, :] * o_ref[rows, :]
            d0 = jnp.sum(jnp.where(head0, prod, 0.0), axis=1, keepdims=True)
            d1 = jnp.sum(jnp.where(head0, 0.0, prod), axis=1, keepdims=True)
            dl0_scr[rows, :] = jnp.broadcast_to(d0, (ATTN_ROWS, LANES))
            dl1_scr[rows, :] = jnp.broadcast_to(d1, (ATTN_ROWS, LANES))
            lse_own = jnp.where(head0, lse0_ref[rows, :], lse1_ref[rows, :])
            psink = jnp.exp(mi_ref[...] - lse_own)
            return dm - jnp.sum(psink * jnp.where(head0, d0, d1), axis=0, keepdims=True)

        dm_ref[...] = lax.fori_loop(0, s // ATTN_ROWS, prep, jnp.zeros((1, LANES), F32))

        def one_block(b, d, r):
            rq, rp = _band_rows(b, d, r)
            q2 = _stack_heads((q_ref[rq, :] * SCALE).astype(BF16), head0)
            do2 = _stack_heads(do_ref[rq, :].astype(BF16), head0)
            k2 = jnp.concatenate([k_src[rp, :], k_src[rq, :]], axis=0).astype(BF16)
            v2 = jnp.concatenate([v_src[rp, :], v_src[rq, :]], axis=0).astype(BF16)
            lse2 = jnp.concatenate([lse0_ref[rq, :], lse1_ref[rq, :]], axis=0)
            dl2 = jnp.concatenate([dl0_scr[rq, :], dl1_scr[rq, :]], axis=0)
            lse2 = jnp.concatenate([lse2, lse2], axis=1)
            dl2 = jnp.concatenate([dl2, dl2], axis=1)
            p = jnp.exp(_dot_nt(q2, k2) + _band_bias(bias_scr, b) - lse2)
            dp = _dot_nt(do2, v2)
            dsc = (p * (dp - dl2)).astype(BF16)
            dq2 = _unstack_heads(_dot_nn(dsc, k2), head0)
            dk2 = _dot_tn(dsc, q2)
            dv2 = _dot_tn(p.astype(BF16), do2)
            dq_acc[rq, :] += dq2 * SCALE
            dk_acc[rp, :] += dk2[:BLOCK]
            dk_acc[rq, :] += dk2[BLOCK:]
            dv_acc[rp, :] += dv2[:BLOCK]
            dv_acc[rq, :] += dv2[BLOCK:]

        _block_loops(s, patterns, ATTN_UNROLL, one_block)

        def to_dz(staged, blk, k):
            cols = pl.ds(pl.multiple_of(blk * LANES, LANES), LANES)
            return pltpu.make_async_copy(staged, dz_ref.at[:, cols], out_sems.at[k])

        last = pair == npair - 1
        q_copy = to_dz(dq_out, q_blk + pair, 0)
        kv_copies = [to_dz(dk_out, k_blk + (0 if gqa else pair), 1), to_dz(dv_out, v_blk + (0 if gqa else pair), 2)]

        @pl.when(pair > 0)
        def _():
            for cp in [q_copy] + ([] if gqa else kv_copies):
                cp.wait()

        def out(c, carry):
            rows = pl.ds(pl.multiple_of(c * ATTN_ROWS, ATTN_ROWS), ATTN_ROWS)
            dq_out[rows, :] = dq_acc[rows, :].astype(BF16)
            if not gqa:
                dk_out[rows, :] = dk_acc[rows, :].astype(BF16)
                dv_out[rows, :] = dv_acc[rows, :].astype(BF16)
                return carry
            dk_t = _kv_grad_from_pair(dk_acc[rows, :], pair)
            dv_t = _kv_grad_from_pair(dv_acc[rows, :], pair)

            @pl.when(pair == 0)
            def _():
                dk_sum[rows, :] = dk_t
                dv_sum[rows, :] = dv_t

            @pl.when((pair > 0) & (pair < npair - 1))
            def _():
                dk_sum[rows, :] += dk_t
                dv_sum[rows, :] += dv_t

            @pl.when(last)
            def _():
                dk_out[rows, :] = (dk_sum[rows, :] + dk_t).astype(BF16)
                dv_out[rows, :] = (dv_sum[rows, :] + dv_t).astype(BF16)

            return carry

        lax.fori_loop(0, s // ATTN_ROWS, out, 0)
        q_copy.start()
        if not gqa:
            for cp in kv_copies:
                cp.start()

        @pl.when(last)
        def _():
            if gqa:
                for cp in kv_copies:
                    cp.start()
            for cp in [q_copy] + kv_copies:
                cp.wait()

    own = pl.BlockSpec((s, LANES), lambda j: (0, j))
    hbm = pl.BlockSpec(memory_space=pl.ANY)
    if gqa:
        kv = lambda blk: pl.BlockSpec((s, LANES), lambda j, blk=blk: (0, blk), pipeline_mode=pl.Buffered(1))
    else:
        kv = lambda blk: pl.BlockSpec((s, LANES), lambda j, blk=blk: (0, blk + j))
    in_specs = [pl.BlockSpec((s, LANES), lambda j: (0, q_blk + j)), kv(k_blk), kv(v_blk), own, own,
                pl.BlockSpec((s, LANES), lambda j: (0, 2 * j)), pl.BlockSpec((s, LANES), lambda j: (0, 2 * j + 1)),
                pl.BlockSpec((1, LANES), lambda j: (0, j))]
    operands = (z, z, z, do, o, lse, lse, m_init)
    return _call(
        body,
        grid=(npair,),
        in_specs=in_specs + [hbm] * n_dz_in,
        out_specs=[hbm, pl.BlockSpec((1, LANES), lambda j: (0, j))],
        out_shape=[jax.ShapeDtypeStruct((s, IN_WIDTH), BF16), jax.ShapeDtypeStruct((1, npair * LANES), F32)],
        operands=operands + (() if dz is None else (dz,)), name=name,
        scratch_shapes=[pltpu.VMEM((s, LANES), F32)] * 5 + [pltpu.VMEM((2 * BLOCK, 2 * BLOCK), F32)]
        + [pltpu.VMEM((s, LANES), BF16)] * 3 + [pltpu.SemaphoreType.DMA((3,))]
        + [pltpu.VMEM((s, LANES), F32)] * (4 if gqa else 0),
        comm=comm, aliases={} if dz is None else {len(in_specs): 0})


def _adamw_math(w, g, m, v):
    m = ADAM_B1 * m + (1.0 - ADAM_B1) * g
    v = ADAM_B2 * v + (1.0 - ADAM_B2) * (g * g)
    m_hat = m / (1.0 - ADAM_B1 ** ADAM_STEP)
    v_hat = v / (1.0 - ADAM_B2 ** ADAM_STEP)
    delta = -ADAM_LR * (m_hat / (jnp.sqrt(v_hat) + ADAM_EPS) + ADAM_WD * w)
    return delta, m, v


def _adamw(w, g, m, v, name):
    rows, cols = w.shape
    tr = min(rows, 256)

    def body(w_ref, g_ref, m_ref, v_ref, d_ref, nm_ref, nv_ref):
        d_ref[...], nm_ref[...], nv_ref[...] = _adamw_math(w_ref[...], g_ref[...], m_ref[...], v_ref[...])

    spec = pl.BlockSpec((tr, cols), lambda i: (i, 0))
    return pl.pallas_call(
        body,
        grid=(rows // tr,),
        in_specs=[spec] * 4,
        out_specs=[spec] * 3,
        out_shape=[jax.ShapeDtypeStruct((rows, cols), F32)] * 3,
        compiler_params=_params("parallel"),
        name=name,
    )(w, g, m, v)


def _sum_adamw(parts, w, m, v, pos, transpose, name):
    assert len(parts) == DEPTH == 2
    (p0, r0), (p1, r1) = parts
    _, rows, cols = p0.shape
    tr = 256 if rows % 256 == 0 else rows
    nt = rows // tr

    def body(pos_ref, p0_ref, r0_ref, p1_ref, r1_ref, w_ref, m_ref, v_ref, g_ref, d_ref, nm_ref, nv_ref):
        def run(p_ref, r_ref):
            g = ((p_ref[...].astype(F32) + r_ref[0].astype(F32)) + r_ref[1].astype(F32)) + r_ref[2].astype(F32)
            if transpose:
                g = g.T
            g_ref[...] = g
            d_ref[...], nm_ref[...], nv_ref[...] = _adamw_math(w_ref[...], g, m_ref[...], v_ref[...])

        layer0 = pl.program_id(0) < nt
        pl.when(layer0)(lambda: run(p0_ref, r0_ref))
        pl.when(jnp.logical_not(layer0))(lambda: run(p1_ref, r1_ref))

    def tile0(i):
        return jnp.minimum(i, nt - 1)

    def tile1(i):
        return jnp.maximum(i - nt, 0)

    if transpose:
        w_spec = pl.BlockSpec((None, cols, tr), lambda i, q: (i // nt, 0, i % nt))
    else:
        w_spec = pl.BlockSpec((None, tr, cols), lambda i, q: (i // nt, i % nt, 0))
    return pl.pallas_call(
        body,
        grid_spec=pltpu.PrefetchScalarGridSpec(
            num_scalar_prefetch=1,
            grid=(DEPTH * nt,),
            in_specs=[pl.BlockSpec((None, tr, cols), lambda i, q: (q[0], tile0(i), 0)),
                      pl.BlockSpec((3, tr, cols), lambda i, q: (0, tile0(i), 0)),
                      pl.BlockSpec((None, tr, cols), lambda i, q: (q[0], tile1(i), 0)),
                      pl.BlockSpec((3, tr, cols), lambda i, q: (0, tile1(i), 0)),
                      w_spec, w_spec, w_spec],
            out_specs=[w_spec] * 4,
        ),
        out_shape=[jax.ShapeDtypeStruct(w.shape, F32)] * 4,
        compiler_params=_params("arbitrary"),
        name=name,
    )(pos, p0, r0, p1, r1, w, m, v)


def _small_sum_adamw(gathered, w, m, v, name):
    _, rows, cols = gathered.shape

    def body(ga_ref, w_ref, m_ref, v_ref, g_ref, d_ref, nm_ref, nv_ref):
        g = ga_ref[0]
        for i in range(1, N_DEV):
            g = g + ga_ref[i]
        g_ref[...] = g
        d_ref[...], nm_ref[...], nv_ref[...] = _adamw_math(w_ref[...], g, m_ref[...], v_ref[...])

    return pl.pallas_call(
        body,
        out_shape=[jax.ShapeDtypeStruct((rows, cols), F32)] * 4,
        name=name,
    )(gathered, w, m, v)


def _pair_sum(g4, r1, pos, name):
    _, _, rows, cols = g4.shape
    tr = min(rows, 512)

    def body(pos_ref, g_ref, r_ref, o_ref):
        o_ref[...] = (g_ref[...].astype(F32) + r_ref[...].astype(F32)).astype(BF16)

    return pl.pallas_call(
        body,
        grid_spec=pltpu.PrefetchScalarGridSpec(
            num_scalar_prefetch=1,
            grid=(4, rows // tr),
            in_specs=[pl.BlockSpec((None, None, tr, cols), lambda i, j, p: (i, p[1], j, 0)),
                      pl.BlockSpec((None, tr, cols), lambda i, j, p: (i, j, 0))],
            out_specs=pl.BlockSpec((None, tr, cols), lambda i, j, p: (i, j, 0)),
        ),
        out_shape=jax.ShapeDtypeStruct((4, rows, cols), BF16),
        compiler_params=_params("parallel", "parallel"),
        name=name,
    )(pos, g4, r1)


def _place():
    return lax.axis_index("x"), lax.axis_index("y"), lax.axis_index("c")


def _gather_comm(shards):
    na = len(shards)

    def plan(ins, outs, sems):
        send_sems, recv_sems, local_sems = sems
        x, y, c = _place()
        me, sibling = (x, y, c), (x, y, 1 - c)
        chips = [(1 - x, y), (x, 1 - y), (1 - x, 1 - y)]

        def rows(a, px, py, pc):
            m = ins[a].shape[0]
            return outs[a].at[pl.ds((4 * px + 2 * py + pc) * m, m), :]

        def copy(a, k, block, to, src=None):
            return pltpu.make_async_remote_copy(
                src_ref=rows(a, *block) if src is None else src, dst_ref=rows(a, *block),
                send_sem=send_sems.at[a, k], recv_sem=recv_sems.at[a, k], device_id=to, device_id_type=MESH)

        mine = [pltpu.make_async_copy(ins[a], rows(a, *me), local_sems.at[a]) for a in range(na)]
        first = []
        for a in range(na):
            first.append(copy(a, 0, me, sibling, src=ins[a]))
            first += [copy(a, 1 + j, me, (*chip, c), src=ins[a]) for j, chip in enumerate(chips)]
        return me, sibling, chips, c, copy, mine, first

    def start(ins, outs, sems):
        *_, mine, first = plan(ins, outs, sems)
        for cp in mine + first:
            cp.start()

    def finish(ins, outs, sems):
        me, sibling, chips, c, copy, mine, first = plan(ins, outs, sems)
        passed = []
        for j, chip in enumerate(chips):
            for a in range(na):
                copy(a, 1 + j, (*chip, c), me).wait_recv()
                cp = copy(a, 4 + j, (*chip, c), sibling)
                cp.start()
                passed.append(cp)
        for a in range(na):
            copy(a, 0, sibling, me).wait_recv()
            for j, chip in enumerate(chips):
                copy(a, 4 + j, (*chip, 1 - c), me).wait_recv()
        for cp in first + passed:
            cp.wait_send()
        for cp in mine:
            cp.wait()

    return _Comm(tuple(shards),
                 tuple(jax.ShapeDtypeStruct((N_DEV * t.shape[0], t.shape[1]), t.dtype) for t in shards),
                 (pltpu.SemaphoreType.DMA((na, 7)), pltpu.SemaphoreType.DMA((na, 7)), pltpu.SemaphoreType.DMA((na,))),
                 start, finish)


def _exchange_comm(arrays, out_shape, n_copies, copies_of):
    na = len(arrays)

    def every(ins, outs, sems):
        send_sems, recv_sems = sems
        return [cp for a in range(na) for cp in copies_of(ins, outs, a, send_sems, recv_sems)]

    def start(ins, outs, sems):
        for cp in every(ins, outs, sems):
            cp.start()

    def finish(ins, outs, sems):
        for cp in every(ins, outs, sems):
            cp.wait()

    return _Comm(tuple(arrays), tuple(out_shape),
                 (pltpu.SemaphoreType.DMA((na, n_copies)), pltpu.SemaphoreType.DMA((na, n_copies))), start, finish)


def _sibling_comm(grads):
    def copies_of(ins, outs, a, send_sems, recv_sems):
        x, y, c = _place()
        return [pltpu.make_async_remote_copy(
            src_ref=ins[a].at[chip, 1 - c], dst_ref=outs[a].at[chip],
            send_sem=send_sems.at[a, chip], recv_sem=recv_sems.at[a, chip],
            device_id=(x, y, 1 - c), device_id_type=MESH) for chip in range(4)]

    return _exchange_comm(grads, [jax.ShapeDtypeStruct((4,) + t.shape[2:], t.dtype) for t in grads], 4, copies_of)


def _chip_comm(partials):
    def copies_of(ins, outs, a, send_sems, recv_sems):
        x, y, c = _place()
        chips = [(1 - x, y), (x, 1 - y), (1 - x, 1 - y)]
        return [pltpu.make_async_remote_copy(
            src_ref=ins[a].at[2 * cx + cy], dst_ref=outs[a].at[k],
            send_sem=send_sems.at[a, k], recv_sem=recv_sems.at[a, k],
            device_id=(cx, cy, c), device_id_type=MESH) for k, (cx, cy) in enumerate(chips)]

    return _exchange_comm(partials, [jax.ShapeDtypeStruct((3,) + t.shape[1:], t.dtype) for t in partials], 3, copies_of)


def _pad_rows(t, rows):
    return jnp.pad(t, ((0, rows - t.shape[0]), (0, D_MODEL - t.shape[1])))


LOSS_ROW = 25


def _pack_small(g_mix, g_group, g_mlp, g_final, conv, sinks, loss=None):
    loss_row = jnp.zeros((1, LANES), F32) if loss is None else loss
    final_and_loss = jnp.concatenate([g_final.reshape(1, D_MODEL), _pad_rows(loss_row, 1)], axis=0)
    return jnp.concatenate([
        _pad_rows(g_mix, 8), _pad_rows(g_group, 8), _pad_rows(g_mlp, 8), _pad_rows(final_and_loss, 8),
        _pad_rows(conv.reshape(DEPTH * 3, CONV_CH), 8), _pad_rows(sinks.reshape(1, DEPTH * 6), 8)], axis=0)


def _unpack_small(slab):
    return (slab[0:2], slab[8:10], slab[16:18], slab[24], slab[32:38, :CONV_CH].reshape(DEPTH, 3, CONV_CH),
            slab[40, :DEPTH * 6].reshape(DEPTH, 2, 3))


def kernel(x, w_in, conv_w, sinks, g_mix, g_group, w_o, g_mlp, w_ff_in, w_ff_out, g_final, loss_target, m_w_in, m_conv_w, m_sinks, m_g_mix, m_g_group, m_w_o, m_g_mlp, m_w_ff_in, m_w_ff_out, m_g_final, v_w_in, v_conv_w, v_sinks, v_g_mix, v_g_group, v_w_o, v_g_mlp, v_w_ff_in, v_w_ff_out, v_g_final):
    ax, ay, ac = _place()
    chip = 2 * ax + ay
    dev = 4 * ax + 2 * ay + ac
    pos = jnp.stack([chip, ac]).astype(jnp.int32)

    x0 = x.reshape(SEQ, D_MODEL)
    target = loss_target.reshape(SEQ, D_MODEL)

    shards = {}
    for l in range(DEPTH):
        shards[l, 0], shards[l, 1] = w_in[l].T.astype(BF16), w_o[l].astype(BF16)
        shards[l, 2], shards[l, 3] = w_ff_in[l].T.astype(BF16), w_ff_out[l].astype(BF16)
    conv_tile = jnp.pad(conv_w.reshape(DEPTH * 3, CONV_CH // N_DEV), ((0, 2), (0, LANES - CONV_CH // N_DEV)))
    wt_in0, conv_all = _comm_only(_gather_comm([shards[0, 0], conv_tile]), "gather_first")
    conv_full = conv_all.reshape(N_DEV, 8, LANES)[:, :DEPTH * 3, :CONV_CH // N_DEV]
    conv_full = conv_full.transpose(1, 0, 2).reshape(DEPTH, 3, CONV_CH)

    dx, parts, small = _step(x0, target, shards, wt_in0, conv_full, sinks, g_mix, g_group, g_mlp, g_final, pos)
    return _finish(dx, parts, small, pos, dev, w_in, conv_w, sinks, g_mix, g_group, w_o, g_mlp, w_ff_in, w_ff_out, g_final, m_w_in, m_conv_w, m_sinks, m_g_mix, m_g_group, m_w_o, m_g_mlp, m_w_ff_in, m_w_ff_out, m_g_final, v_w_in, v_conv_w, v_sinks, v_g_mix, v_g_group, v_w_o, v_g_mlp, v_w_ff_in, v_w_ff_out, v_g_final)


FWD_CARRY = {(0, "in_proj"): ((0, 1),), (0, "window"): ((1, 0),), (0, "dilated"): ((0, 2),),
             (0, "mix_out"): ((1, 1),), (0, "ff_in"): ((0, 3),), (0, "ff_out"): ((1, 3),),
             (1, "dilated"): ((1, 2),)}


def _step(x0, target, shards, wt_in0, conv_full, sinks, g_mix, g_group, g_mlp, g_final, pos):
    sink_lanes = jnp.repeat(sinks.reshape(DEPTH, 6), HEAD_DIM, axis=1)
    no_sink = jnp.full((1, A_WIDTH), NEG_BIG, F32)
    full = {(0, 0): wt_in0}

    def gather(stage, l):
        keys = FWD_CARRY.get((l, stage), ())
        return keys, (_gather_comm([shards[k] for k in keys]) if keys else None)

    def landed(keys, got):
        full.update(zip(keys, got))

    saved = []
    xc = x0
    for l in range(DEPTH):
        keys, comm = gather("in_proj", l)
        (z, h), got = _norm_mm(xc, g_mix[l:l + 1], full[l, 0], False, f"in_proj_{l}", comm)
        landed(keys, got)
        sink_l = sink_lanes[l:l + 1]
        keys, comm = gather("window", l)
        (yc, lse_c), got = _attn_fwd(z, sink_l, 1.0, QC_BLK, KC_BLK, VC_BLK, (1,), C_MAX_DIST, True,
                                     f"window_attn_{l}", comm)
        landed(keys, got)
        yb = _conv_fwd(z, conv_full[l], f"conv_{l}")
        keys, comm = gather("dilated", l)
        (ya, lse_a), got = _attn_fwd(z, no_sink, 0.0, QA_BLK, KA_BLK, VA_BLK, DILATED_PATTERNS, A_MAX_DIST, False,
                                     f"dilated_attn_{l}", comm)
        landed(keys, got)
        keys, comm = gather("mix_out", l)
        (y, x1), got = _mix_out(ya, yb, yc, g_group[l:l + 1], full[l, 1], xc, f"mix_out_{l}", comm)
        landed(keys, got)
        keys, comm = gather("ff_in", l)
        (a, h2), got = _norm_mm(x1, g_mlp[l:l + 1], full[l, 2], True, f"ff_in_{l}", comm)
        landed(keys, got)
        keys, comm = gather("ff_out", l)
        (x2,), got = _mm_res(a, full[l, 3], x1, f"ff_out_{l}", comm)
        landed(keys, got)
        saved.append((xc, z, h, ya, lse_a, yb, yc, lse_c, sink_l, y, x1, a, h2))
        xc = x2

    loss_slab, dx, dxb, dg_final = _loss_head(xc, g_final.reshape(1, D_MODEL), target, "loss_head")

    def by_owner(t):
        return t.reshape(4, 2, t.shape[0] // N_DEV, D_MODEL)

    def pair(key, g, r1):
        return _pair_sum(g, r1, pos, f"grad_pair_sum_{key[0]}_{key[1]}")

    partial, r2 = {}, {}
    dg_mix, dg_group, dg_mlp, dconv, dsinks = [None] * DEPTH, [None] * DEPTH, [None] * DEPTH, [None] * DEPTH, [None] * DEPTH
    for l in reversed(range(DEPTH)):
        xin, z, h, ya, lse_a, yb, yc, lse_c, sink_l, y, x1, a, h2 = saved[l]
        late = [(l + 1, 1), (l + 1, 0)] if l + 1 < DEPTH else []
        (du,), got = _mlp_bwd_act(dxb, full[l, 3], a, f"ff_out_bwd_{l}",
                                  _chip_comm([partial[k] for k in late]) if late else None)
        r2.update(zip(late, got))
        (g3,), _ = _mm_tn(a, dxb, f"grad_w_ff_out_{l}")
        (g2,), _ = _mm_tn(du, h2, f"grad_w_ff_in_{l}")
        g3, g2 = by_owner(g3), by_owner(g2)
        (dx1, dx1b, dg_mlp[l]), got = _mm_nn_normbwd(du, full[l, 2], x1, dx, g_mlp[l:l + 1], f"ff_in_bwd_{l}",
                                                    _sibling_comm([g3, g2]))
        partial[l, 3], partial[l, 2] = pair((l, 3), g3, got[0]), pair((l, 2), g2, got[1])
        (g1,), _ = _mm_tn(y, dx1b, f"grad_w_o_{l}")
        g1 = by_owner(g1)
        (dya, dyb, dyc, dg_group[l]), got = _mix_bwd(dx1b, full[l, 1], ya, yb, yc, g_group[l:l + 1],
                                                     f"mix_out_bwd_{l}", _sibling_comm([g1]) if l == 0 else None)
        if l == 0:
            partial[l, 1] = pair((l, 1), g1, got[0])
        early = [(l, 3), (l, 2)] + ([(l, 1)] if l == 0 else [])
        (dz, _), got = _attn_bwd(z, dya, ya, lse_a, no_sink, None, QA_BLK, KA_BLK, VA_BLK, DILATED_PATTERNS,
                                 A_MAX_DIST, False, f"dilated_attn_bwd_{l}", _chip_comm([partial[k] for k in early]))
        r2.update(zip(early, got))
        dz, dcw = _conv_bwd(z, conv_full[l], dyb, dz, f"conv_bwd_{l}")
        (dz, dsink), _ = _attn_bwd(z, dyc, yc, lse_c, sink_l, dz, QC_BLK, KC_BLK, VC_BLK, (1,), C_MAX_DIST,
                                   True, f"window_attn_bwd_{l}")
        (g0,), _ = _mm_tn(dz, h, f"grad_w_in_{l}")
        g0 = by_owner(g0)
        if l > 0:
            (dx, dxb, dg_mix[l]), got = _mm_nn_normbwd(dz, full[l, 0], xin, dx1, g_mix[l:l + 1], f"in_proj_bwd_{l}",
                                                      _sibling_comm([g1, g0]))
            partial[l, 1], partial[l, 0] = pair((l, 1), g1, got[0]), pair((l, 0), g0, got[1])
        else:
            (r1,) = _comm_only(_sibling_comm([g0]), "grad_sibling_exchange_last")
            partial[l, 0] = pair((l, 0), g0, r1)
            (dx, dxb, dg_mix[l]), got = _mm_nn_normbwd(dz, full[l, 0], xin, dx1, g_mix[l:l + 1], f"in_proj_bwd_{l}",
                                                      _chip_comm([partial[l, 0]]))
            r2[l, 0] = got[0]
        dconv[l] = dcw[:3]
        dsinks[l] = dsink[0, ::HEAD_DIM]
    parts = {key: (partial[key], r2[key]) for key in partial}
    small = _pack_small(jnp.concatenate(dg_mix), jnp.concatenate(dg_group), jnp.concatenate(dg_mlp),
                        dg_final, jnp.stack(dconv), jnp.stack(dsinks), loss_slab[0:1])
    return dx, parts, small


def _finish(dx, parts, small, pos, dev, w_in, conv_w, sinks, g_mix, g_group, w_o, g_mlp, w_ff_in, w_ff_out, g_final, m_w_in, m_conv_w, m_sinks, m_g_mix, m_g_group, m_w_o, m_g_mlp, m_w_ff_in, m_w_ff_out, m_g_final, v_w_in, v_conv_w, v_sinks, v_g_mix, v_g_group, v_w_o, v_g_mlp, v_w_ff_in, v_w_ff_out, v_g_final):
    grad_x = dx.reshape(1, SEQ, D_MODEL)

    (small_all,) = _comm_only(_gather_comm([small]), "gather_small_grads")
    zeros_conv = jnp.zeros((DEPTH, 3, CONV_CH), F32)
    sw = _pack_small(g_mix, g_group, g_mlp, g_final, zeros_conv, sinks)
    sm = _pack_small(m_g_mix, m_g_group, m_g_mlp, m_g_final, zeros_conv, m_sinks)
    sv = _pack_small(v_g_mix, v_g_group, v_g_mlp, v_g_final, zeros_conv, v_sinks)
    sg, sd, snm, snv = _small_sum_adamw(small_all.reshape(N_DEV, SMALL_ROWS, D_MODEL), sw, sm, sv, "small_adamw")
    loss = sg[LOSS_ROW, 0]
    grad_g_mix, grad_g_group, grad_g_mlp, grad_g_final, conv_grad_full, grad_sinks = _unpack_small(sg)
    delta_g_mix, delta_g_group, delta_g_mlp, delta_g_final, _, delta_sinks = _unpack_small(sd)
    new_m_g_mix, new_m_g_group, new_m_g_mlp, new_m_g_final, _, new_m_sinks = _unpack_small(snm)
    new_v_g_mix, new_v_g_group, new_v_g_mlp, new_v_g_final, _, new_v_sinks = _unpack_small(snv)
    cs = CONV_CH // N_DEV
    grad_conv_w = lax.dynamic_slice_in_dim(conv_grad_full, dev * cs, cs, axis=2)

    def tile_of(t):
        return jnp.pad(t.reshape(1, DEPTH * 3 * cs), ((0, 7), (0, 256 - DEPTH * 3 * cs)))

    cd, cm, cv = _adamw(tile_of(conv_w), tile_of(grad_conv_w), tile_of(m_conv_w), tile_of(v_conv_w), "conv_adamw")
    untile = lambda t: t[0, :DEPTH * 3 * cs].reshape(DEPTH, 3, cs)
    delta_conv_w, new_m_conv_w, new_v_conv_w = untile(cd), untile(cm), untile(cv)

    def big(kind, w, m, v, transpose, name):
        return _sum_adamw([parts[l, kind] for l in range(DEPTH)], w, m, v, pos, transpose, name)

    grad_w_in, delta_w_in, new_m_w_in, new_v_w_in = big(0, w_in, m_w_in, v_w_in, True, "adamw_w_in")
    grad_w_o, delta_w_o, new_m_w_o, new_v_w_o = big(1, w_o, m_w_o, v_w_o, False, "adamw_w_o")
    grad_w_ff_in, delta_w_ff_in, new_m_w_ff_in, new_v_w_ff_in = big(2, w_ff_in, m_w_ff_in, v_w_ff_in, True, "adamw_w_ff_in")
    grad_w_ff_out, delta_w_ff_out, new_m_w_ff_out, new_v_w_ff_out = big(3, w_ff_out, m_w_ff_out, v_w_ff_out, False,
                                                                         "adamw_w_ff_out")

    return (loss, grad_x, grad_w_in, grad_conv_w, grad_sinks, grad_g_mix, grad_g_group, grad_w_o, grad_g_mlp,
            grad_w_ff_in, grad_w_ff_out, grad_g_final,
            delta_w_in, delta_conv_w, delta_sinks, delta_g_mix, delta_g_group, delta_w_o, delta_g_mlp,
            delta_w_ff_in, delta_w_ff_out, delta_g_final,
            new_m_w_in, new_m_conv_w, new_m_sinks, new_m_g_mix, new_m_g_group, new_m_w_o, new_m_g_mlp,
            new_m_w_ff_in, new_m_w_ff_out, new_m_g_final,
            new_v_w_in, new_v_conv_w, new_v_sinks, new_v_g_mix, new_v_g_group, new_v_w_o, new_v_g_mlp,
            new_v_w_ff_in, new_v_w_ff_out, new_v_g_final)
```

```python
from typing import Callable, NamedTuple

import jax
import jax.numpy as jnp
from jax import lax
from jax.experimental import pallas as pl
from jax.experimental.pallas import tpu as pltpu

F32 = jnp.float32
BF16 = jnp.bfloat16
MESH = pl.DeviceIdType.MESH

N_DEV = 8
SEQ = 4096
D_MODEL = 1024
DEPTH = 2
HEAD_DIM = 64
LANES = 128
A_WIDTH = 384
CONV_CH = 256
C_WIDTH = 384
KV_WIDTH = 128
IN_WIDTH = 2560
D_FF = 4096
BLOCK = 128
DILATED_PATTERNS = (1, 4, 16)
A_MAX_DIST = 128
C_MAX_DIST = 127
EPS = 1e-6
SCALE = HEAD_DIM ** -0.5
NEG_BIG = -1e30
F32_TINY = 1.1754944e-38

QA_BLK, KA_BLK, VA_BLK = 0, 3, 6
GB_BLK, GC_BLK, XB_BLK = 9, 11, 13
QC_BLK, KC_BLK, VC_BLK = 15, 18, 19

ADAM_LR = 0.001
ADAM_B1 = 0.9
ADAM_B2 = 0.999
ADAM_EPS = 1e-08
ADAM_WD = 0.01
ADAM_STEP = 10

VMEM_LIMIT = 56 * 1024 * 1024
ROW_TILE = 512
COL_CHUNK = 512
SMALL_ROWS = 48


def _dot_nn(a, b):
    return lax.dot_general(a, b, (((1,), (0,)), ((), ())), preferred_element_type=F32)


def _dot_nt(a, b):
    return lax.dot_general(a, b, (((1,), (1,)), ((), ())), preferred_element_type=F32)


def _dot_tn(a, b):
    return lax.dot_general(a, b, (((0,), (0,)), ((), ())), preferred_element_type=F32)


def _params(*sem):
    return pltpu.CompilerParams(dimension_semantics=sem, vmem_limit_bytes=VMEM_LIMIT)


def _rms_scale(t):
    return lax.rsqrt(jnp.mean(t * t, axis=-1, keepdims=True) + EPS)


def _rms_bwd(n, r, dn):
    return r * (dn - n * jnp.mean(dn * n, axis=-1, keepdims=True))


class _Comm(NamedTuple):
    arrays: tuple
    out_shape: tuple
    sems: tuple
    start: Callable
    finish: Callable


def _call(body, grid, in_specs, out_specs, out_shape, operands, name, scratch_shapes=(), comm=None, aliases=None):
    n_in, n_out, n_scr = len(in_specs), len(out_shape), len(scratch_shapes)
    aliases = dict(aliases or {})
    if comm is None:
        res = pl.pallas_call(body, grid=grid, in_specs=list(in_specs), out_specs=list(out_specs),
                             out_shape=list(out_shape), scratch_shapes=list(scratch_shapes),
                             input_output_aliases=aliases,
                             compiler_params=_params("arbitrary"), name=name)(*operands)
        return list(res), []
    c_in, c_out = len(comm.arrays), len(comm.out_shape)
    hbm = pl.BlockSpec(memory_space=pl.ANY)
    last = grid[0] - 1

    def carried(*refs):
        ins, cins = refs[:n_in], refs[n_in:n_in + c_in]
        o0 = n_in + c_in
        outs, couts = refs[o0:o0 + n_out], refs[o0 + n_out:o0 + n_out + c_out]
        s0 = o0 + n_out + c_out
        scr, sems = refs[s0:s0 + n_scr], refs[s0 + n_scr:]
        pl.when(pl.program_id(0) == 0)(lambda: comm.start(cins, couts, sems))
        body(*ins, *outs, *scr)
        pl.when(pl.program_id(0) == last)(lambda: comm.finish(cins, couts, sems))

    res = pl.pallas_call(carried, grid=grid, in_specs=list(in_specs) + [hbm] * c_in,
                         out_specs=list(out_specs) + [hbm] * c_out, out_shape=list(out_shape) + list(comm.out_shape),
                         scratch_shapes=list(scratch_shapes) + list(comm.sems), input_output_aliases=aliases,
                         compiler_params=_params("arbitrary"), name=name)(*operands, *comm.arrays)
    return list(res[:n_out]), list(res[n_out:])


def _comm_only(comm, name):
    hbm = pl.BlockSpec(memory_space=pl.ANY)
    c_in, c_out = len(comm.arrays), len(comm.out_shape)

    def body(*refs):
        ins, outs, sems = refs[:c_in], refs[c_in:c_in + c_out], refs[c_in + c_out:]
        comm.start(ins, outs, sems)
        comm.finish(ins, outs, sems)

    return pl.pallas_call(body, in_specs=[hbm] * c_in, out_specs=[hbm] * c_out, out_shape=list(comm.out_shape),
                          scratch_shapes=list(comm.sems), name=name)(*comm.arrays)


def _norm_mm(x, g, wt, relu2, name, comm=None):
    s, d = x.shape
    n = wt.shape[0]
    tm = ROW_TILE

    def body(x_ref, g_ref, w_ref, o_ref, h_ref):
        xx = x_ref[...]
        h = ((xx * _rms_scale(xx)) * g_ref[...]).astype(BF16)
        h_ref[...] = h
        for n0 in range(0, n, COL_CHUNK):
            zc = _dot_nt(h, w_ref[n0:n0 + COL_CHUNK, :])
            if relu2:
                zc = jnp.square(jnp.maximum(zc, 0.0)).astype(BF16)
            o_ref[:, n0:n0 + COL_CHUNK] = zc

    return _call(
        body,
        grid=(s // tm,),
        in_specs=[pl.BlockSpec((tm, d), lambda i: (i, 0)),
                  pl.BlockSpec((1, d), lambda i: (0, 0)),
                  pl.BlockSpec((n, d), lambda i: (0, 0))],
        out_specs=[pl.BlockSpec((tm, n), lambda i: (i, 0)),
                   pl.BlockSpec((tm, d), lambda i: (i, 0))],
        out_shape=[jax.ShapeDtypeStruct((s, n), BF16 if relu2 else F32), jax.ShapeDtypeStruct((s, d), BF16)],
        operands=(x, g, wt), name=name, comm=comm)


def _mm_res(a, w2, x1, name, comm=None):
    s, f = a.shape
    d = w2.shape[1]
    tm = ROW_TILE

    def body(a_ref, w_ref, x_ref, o_ref):
        o_ref[...] = x_ref[...] + _dot_nn(a_ref[...], w_ref[...])

    return _call(
        body,
        grid=(s // tm,),
        in_specs=[pl.BlockSpec((tm, f), lambda i: (i, 0)),
                  pl.BlockSpec((f, d), lambda i: (0, 0)),
                  pl.BlockSpec((tm, d), lambda i: (i, 0))],
        out_specs=[pl.BlockSpec((tm, d), lambda i: (i, 0))],
        out_shape=[jax.ShapeDtypeStruct((s, d), F32)],
        operands=(a, w2, x1), name=name, comm=comm)


def _mix_out(ya, yb, yc, gg, wo, x0, name, comm=None):
    s = ya.shape[0]
    d = wo.shape[1]
    tm = ROW_TILE

    def body(ya_ref, yb_ref, yc_ref, g_ref, w_ref, x_ref, y_ref, o_ref):
        parts = []
        for ref in (ya_ref, yb_ref, yc_ref):
            t = ref[...]
            parts.append(t * _rms_scale(t))
        y = (jnp.concatenate(parts, axis=1) * g_ref[...]).astype(BF16)
        y_ref[...] = y
        o_ref[...] = x_ref[...] + _dot_nn(y, w_ref[...])

    return _call(
        body,
        grid=(s // tm,),
        in_specs=[pl.BlockSpec((tm, A_WIDTH), lambda i: (i, 0)),
                  pl.BlockSpec((tm, CONV_CH), lambda i: (i, 0)),
                  pl.BlockSpec((tm, C_WIDTH), lambda i: (i, 0)),
                  pl.BlockSpec((1, d), lambda i: (0, 0)),
                  pl.BlockSpec((d, d), lambda i: (0, 0)),
                  pl.BlockSpec((tm, d), lambda i: (i, 0))],
        out_specs=[pl.BlockSpec((tm, d), lambda i: (i, 0)),
                   pl.BlockSpec((tm, d), lambda i: (i, 0))],
        out_shape=[jax.ShapeDtypeStruct((s, d), BF16), jax.ShapeDtypeStruct((s, d), F32)],
        operands=(ya, yb, yc, gg, wo, x0), name=name, comm=comm)


def _loss_head(x, g, target, name):
    s, d = x.shape
    tm = ROW_TILE

    def body(x_ref, g_ref, t_ref, loss_ref, dx_ref, dxb_ref, dg_ref):
        @pl.when(pl.program_id(0) == 0)
        def _():
            loss_ref[...] = jnp.zeros_like(loss_ref)
            dg_ref[...] = jnp.zeros_like(dg_ref)

        xx = x_ref[...]
        r = _rms_scale(xx)
        n = xx * r
        gv = g_ref[...]
        err = n * gv - t_ref[...]
        per_tok = jnp.sum(err * err, axis=1, keepdims=True) * (1.0 / d)
        loss_ref[...] += 0.5 * jnp.sum(per_tok, axis=0, keepdims=True)
        dout = err * (1.0 / d)
        dg_ref[...] += jnp.sum(dout * n, axis=0, keepdims=True)
        dx = _rms_bwd(n, r, dout * gv)
        dx_ref[...] = dx
        dxb_ref[...] = dx.astype(BF16)

    return pl.pallas_call(
        body,
        grid=(s // tm,),
        in_specs=[pl.BlockSpec((tm, d), lambda i: (i, 0)),
                  pl.BlockSpec((1, d), lambda i: (0, 0)),
                  pl.BlockSpec((tm, d), lambda i: (i, 0))],
        out_specs=[pl.BlockSpec((8, LANES), lambda i: (0, 0)),
                   pl.BlockSpec((tm, d), lambda i: (i, 0)),
                   pl.BlockSpec((tm, d), lambda i: (i, 0)),
                   pl.BlockSpec((1, d), lambda i: (0, 0))],
        out_shape=[jax.ShapeDtypeStruct((8, LANES), F32), jax.ShapeDtypeStruct((s, d), F32),
                   jax.ShapeDtypeStruct((s, d), BF16), jax.ShapeDtypeStruct((1, d), F32)],
        compiler_params=_params("arbitrary"),
        name=name,
    )(x, g, target)


def _mlp_bwd_act(dxb, w2, a, name, comm=None):
    s, d = dxb.shape
    f = w2.shape[0]
    tm = ROW_TILE

    def body(dx_ref, w_ref, a_ref, du_ref):
        dx = dx_ref[...]
        for n0 in range(0, f, COL_CHUNK):
            da = _dot_nt(dx, w_ref[n0:n0 + COL_CHUNK, :])
            av = a_ref[:, n0:n0 + COL_CHUNK].astype(F32)
            rl = av * lax.rsqrt(jnp.maximum(av, F32_TINY))
            du_ref[:, n0:n0 + COL_CHUNK] = (da * (2.0 * rl)).astype(BF16)

    return _call(
        body,
        grid=(s // tm,),
        in_specs=[pl.BlockSpec((tm, d), lambda i: (i, 0)),
                  pl.BlockSpec((f, d), lambda i: (0, 0)),
                  pl.BlockSpec((tm, f), lambda i: (i, 0))],
        out_specs=[pl.BlockSpec((tm, f), lambda i: (i, 0))],
        out_shape=[jax.ShapeDtypeStruct((s, f), BF16)],
        operands=(dxb, w2, a), name=name, comm=comm)


def _mm_tn(a, b, name, comm=None):
    s, n = a.shape
    d = b.shape[1]
    tn = 512

    def body(a_ref, b_ref, o_ref, acc):
        for k0 in range(0, s, ROW_TILE):
            part = _dot_tn(a_ref[k0:k0 + ROW_TILE, :], b_ref[k0:k0 + ROW_TILE, :])
            if k0 == 0:
                acc[...] = part
            else:
                acc[...] += part
        o_ref[...] = acc[...].astype(BF16)

    return _call(
        body,
        grid=(n // tn,),
        in_specs=[pl.BlockSpec((s, tn), lambda j: (0, j)),
                  pl.BlockSpec((s, d), lambda j: (0, 0))],
        out_specs=[pl.BlockSpec((tn, d), lambda j: (j, 0))],
        out_shape=[jax.ShapeDtypeStruct((n, d), BF16)],
        operands=(a, b), name=name, scratch_shapes=[pltpu.VMEM((tn, d), F32)], comm=comm)


def _mm_nn_normbwd(dact, wt, x, dres, g, name, comm=None):
    s, kdim = dact.shape
    d = wt.shape[1]
    tm = ROW_TILE

    def body(a_ref, w_ref, x_ref, r_ref, g_ref, o_ref, ob_ref, dg_ref):
        @pl.when(pl.program_id(0) == 0)
        def _():
            dg_ref[...] = jnp.zeros_like(dg_ref)

        dh = _dot_nn(a_ref[...], w_ref[...])
        xx = x_ref[...]
        r = _rms_scale(xx)
        n = xx * r
        dg_ref[...] += jnp.sum(dh * n, axis=0, keepdims=True)
        dx = r_ref[...] + _rms_bwd(n, r, dh * g_ref[...])
        o_ref[...] = dx
        ob_ref[...] = dx.astype(BF16)

    return _call(
        body,
        grid=(s // tm,),
        in_specs=[pl.BlockSpec((tm, kdim), lambda i: (i, 0)),
                  pl.BlockSpec((kdim, d), lambda i: (0, 0)),
                  pl.BlockSpec((tm, d), lambda i: (i, 0)),
                  pl.BlockSpec((tm, d), lambda i: (i, 0)),
                  pl.BlockSpec((1, d), lambda i: (0, 0))],
        out_specs=[pl.BlockSpec((tm, d), lambda i: (i, 0)),
                   pl.BlockSpec((tm, d), lambda i: (i, 0)),
                   pl.BlockSpec((1, d), lambda i: (0, 0))],
        out_shape=[jax.ShapeDtypeStruct((s, d), F32), jax.ShapeDtypeStruct((s, d), BF16),
                   jax.ShapeDtypeStruct((1, d), F32)],
        operands=(dact, wt, x, dres, g), name=name, comm=comm)


def _mix_bwd(dx1, wo, ya, yb, yc, gg, name, comm=None):
    s, d = dx1.shape
    tm = ROW_TILE
    widths = (A_WIDTH, CONV_CH, C_WIDTH)

    def body(dx_ref, w_ref, ya_ref, yb_ref, yc_ref, g_ref, da_ref, db_ref, dc_ref, dg_ref):
        @pl.when(pl.program_id(0) == 0)
        def _():
            dg_ref[...] = jnp.zeros_like(dg_ref)

        dy = _dot_nt(dx_ref[...], w_ref[...])
        gv = g_ref[...]
        off = 0
        dgs = []
        for ref, out, w in zip((ya_ref, yb_ref, yc_ref), (da_ref, db_ref, dc_ref), widths):
            t = ref[...]
            r = _rms_scale(t)
            n = t * r
            dyg = dy[:, off:off + w]
            dgs.append(jnp.sum(dyg * n, axis=0, keepdims=True))
            out[...] = _rms_bwd(n, r, dyg * gv[:, off:off + w])
            off += w
        dg_ref[...] += jnp.concatenate(dgs, axis=1)

    return _call(
        body,
        grid=(s // tm,),
        in_specs=[pl.BlockSpec((tm, d), lambda i: (i, 0)),
                  pl.BlockSpec((d, d), lambda i: (0, 0)),
                  pl.BlockSpec((tm, A_WIDTH), lambda i: (i, 0)),
                  pl.BlockSpec((tm, CONV_CH), lambda i: (i, 0)),
                  pl.BlockSpec((tm, C_WIDTH), lambda i: (i, 0)),
                  pl.BlockSpec((1, d), lambda i: (0, 0))],
        out_specs=[pl.BlockSpec((tm, A_WIDTH), lambda i: (i, 0)),
                   pl.BlockSpec((tm, CONV_CH), lambda i: (i, 0)),
                   pl.BlockSpec((tm, C_WIDTH), lambda i: (i, 0)),
                   pl.BlockSpec((1, d), lambda i: (0, 0))],
        out_shape=[jax.ShapeDtypeStruct((s, A_WIDTH), F32), jax.ShapeDtypeStruct((s, CONV_CH), F32),
                   jax.ShapeDtypeStruct((s, C_WIDTH), F32), jax.ShapeDtypeStruct((1, d), F32)],
        operands=(dx1, wo, ya, yb, yc, gg), name=name, comm=comm)


CONV_CHUNK = 256
CONV_HALO = 8


def _conv_fwd(z, cw, name):
    s = z.shape[0]
    nch = s // CONV_CHUNK

    def body(gb_ref, gc_ref, xb_ref, w_ref, o_ref, us):
        us[pl.ds(0, CONV_HALO), :] = jnp.zeros((CONV_HALO, LANES), F32)
        us[pl.ds(CONV_HALO, s), :] = gc_ref[...] * xb_ref[...]
        w0, w1, w2 = w_ref[0:1, :], w_ref[1:2, :], w_ref[2:3, :]

        def chunk(c, carry):
            st = pl.multiple_of(c * CONV_CHUNK, CONV_CHUNK)
            ext = us[pl.ds(st, CONV_CHUNK + CONV_HALO), :]
            y = (w0 * ext[CONV_HALO - 2:CONV_HALO - 2 + CONV_CHUNK]
                 + w1 * ext[CONV_HALO - 1:CONV_HALO - 1 + CONV_CHUNK]
                 + w2 * ext[CONV_HALO:])
            o_ref[pl.ds(st, CONV_CHUNK), :] = gb_ref[pl.ds(st, CONV_CHUNK), :] * y
            return carry

        lax.fori_loop(0, nch, chunk, 0)

    col = lambda blk: pl.BlockSpec((s, LANES), lambda j, blk=blk: (0, blk + j))
    return pl.pallas_call(
        body,
        grid=(CONV_CH // LANES,),
        in_specs=[col(GB_BLK), col(GC_BLK), col(XB_BLK), pl.BlockSpec((3, LANES), lambda j: (0, j))],
        out_specs=pl.BlockSpec((s, LANES), lambda j: (0, j)),
        out_shape=jax.ShapeDtypeStruct((s, CONV_CH), F32),
        scratch_shapes=[pltpu.VMEM((s + CONV_HALO, LANES), F32)],
        compiler_params=_params("parallel"),
        name=name,
    )(z, z, z, cw)


def _conv_bwd(z, cw, dyb, dz, name):
    s = z.shape[0]
    nch = s // CONV_CHUNK
    ncol = CONV_CH // LANES

    def body(gb_ref, gc_ref, xb_ref, w_ref, dy_ref, dz_in, dz_ref, dw_ref, us, ds_, dgb_ref, dgc_ref, dxb_ref, sems):
        j = pl.program_id(0)

        def to_dz(staged, blk, k):
            cols = pl.ds(pl.multiple_of((blk + j) * LANES, LANES), LANES)
            return pltpu.make_async_copy(staged, dz_ref.at[:, cols], sems.at[k])

        copies = [to_dz(dgb_ref, GB_BLK, 0), to_dz(dgc_ref, GC_BLK, 1), to_dz(dxb_ref, XB_BLK, 2)]

        @pl.when(j > 0)
        def _():
            for cp in copies:
                cp.wait()

        us[pl.ds(0, CONV_HALO), :] = jnp.zeros((CONV_HALO, LANES), F32)
        us[pl.ds(CONV_HALO, s), :] = gc_ref[...] * xb_ref[...]
        ds_[pl.ds(s, CONV_HALO), :] = jnp.zeros((CONV_HALO, LANES), F32)
        ds_[pl.ds(0, s), :] = dy_ref[...] * gb_ref[...]
        w0, w1, w2 = w_ref[0:1, :], w_ref[1:2, :], w_ref[2:3, :]
        zero = jnp.zeros((1, LANES), F32)

        def chunk(c, carry):
            a0, a1, a2 = carry
            st = pl.multiple_of(c * CONV_CHUNK, CONV_CHUNK)
            rows = pl.ds(st, CONV_CHUNK)
            ext = us[pl.ds(st, CONV_CHUNK + CONV_HALO), :]
            um2 = ext[CONV_HALO - 2:CONV_HALO - 2 + CONV_CHUNK]
            um1 = ext[CONV_HALO - 1:CONV_HALO - 1 + CONV_CHUNK]
            u0 = ext[CONV_HALO:]
            dext = ds_[pl.ds(st, CONV_CHUNK + CONV_HALO), :]
            dc0 = dext[:CONV_CHUNK]
            du = w2 * dc0 + w1 * dext[1:1 + CONV_CHUNK] + w0 * dext[2:2 + CONV_CHUNK]
            yconv = w0 * um2 + w1 * um1 + w2 * u0
            dgb_ref[rows, :] = (dy_ref[rows, :] * yconv).astype(BF16)
            dgc_ref[rows, :] = (du * xb_ref[rows, :]).astype(BF16)
            dxb_ref[rows, :] = (du * gc_ref[rows, :]).astype(BF16)
            a0 = a0 + jnp.sum(dc0 * um2, axis=0, keepdims=True)
            a1 = a1 + jnp.sum(dc0 * um1, axis=0, keepdims=True)
            a2 = a2 + jnp.sum(dc0 * u0, axis=0, keepdims=True)
            return a0, a1, a2

        a0, a1, a2 = lax.fori_loop(0, nch, chunk, (zero, zero, zero))
        dw_ref[...] = jnp.concatenate([a0, a1, a2, jnp.zeros((5, LANES), F32)], axis=0)
        for cp in copies:
            cp.start()

        @pl.when(j == ncol - 1)
        def _():
            for cp in copies:
                cp.wait()

    col = lambda blk: pl.BlockSpec((s, LANES), lambda j, blk=blk: (0, blk + j))
    hbm = pl.BlockSpec(memory_space=pl.ANY)
    return pl.pallas_call(
        body,
        grid=(ncol,),
        in_specs=[col(GB_BLK), col(GC_BLK), col(XB_BLK), pl.BlockSpec((3, LANES), lambda j: (0, j)),
                  pl.BlockSpec((s, LANES), lambda j: (0, j)), hbm],
        out_specs=[hbm, pl.BlockSpec((8, LANES), lambda j: (0, j))],
        out_shape=[jax.ShapeDtypeStruct(dz.shape, dz.dtype), jax.ShapeDtypeStruct((8, CONV_CH), F32)],
        scratch_shapes=[pltpu.VMEM((s + CONV_HALO, LANES), F32), pltpu.VMEM((s + CONV_HALO, LANES), F32)]
        + [pltpu.VMEM((s, LANES), BF16)] * 3 + [pltpu.SemaphoreType.DMA((3,))],
        input_output_aliases={5: 0},
        compiler_params=_params("arbitrary"),
        name=name,
    )(z, z, z, cw, dyb, dz)


ATTN_ROWS = 512
ATTN_UNROLL = 8


def _band_rows(b, d, r):
    base = pl.multiple_of(b * (BLOCK * d), BLOCK)
    prev = jnp.maximum(base - BLOCK * d, 0)
    if d == 1:
        return pl.ds(base, BLOCK), pl.ds(pl.multiple_of(prev, BLOCK), BLOCK)
    return pl.ds(base + r, BLOCK, stride=d), pl.ds(prev + r, BLOCK, stride=d)


def _write_band_bias(bias_ref, max_dist):
    qi = lax.broadcasted_iota(jnp.int32, (BLOCK, 2 * BLOCK), 0)
    kj = lax.broadcasted_iota(jnp.int32, (BLOCK, 2 * BLOCK), 1)
    dist = BLOCK + qi - kj
    band = (dist >= 0) & (dist <= max_dist)
    bias_ref[0:BLOCK, :] = jnp.where(band, 0.0, -jnp.inf)
    bias_ref[BLOCK:2 * BLOCK, :] = jnp.where(band & (kj >= BLOCK), 0.0, -jnp.inf)


def _band_bias(bias_ref, b):
    bias = bias_ref[pl.ds(pl.multiple_of(jnp.where(b > 0, 0, BLOCK), BLOCK), BLOCK), :]
    return jnp.concatenate([bias, bias], axis=0)


def _lane_half():
    return (lax.broadcasted_iota(jnp.int32, (1, LANES), 1) >= HEAD_DIM).astype(jnp.int32)


def _kv_for_pair(t, pair):
    half = _lane_half()
    want = (pair + half) >> 1
    return jnp.where(want != half, pltpu.roll(t, HEAD_DIM, 1), t)


def _kv_grad_from_pair(t, pair):
    half = _lane_half()
    mine = ((pair + half) >> 1) == half
    other = ((pair + 1 - half) >> 1) == half
    fold = t + pltpu.roll(t, HEAD_DIM, 1)
    return jnp.where(mine & other, fold, jnp.where(mine, t, 0.0))


def _stack_heads(t, head0):
    zero = jnp.zeros_like(t)
    return jnp.concatenate([jnp.where(head0, t, zero), jnp.where(head0, zero, t)], axis=0)


def _unstack_heads(t, head0):
    return jnp.where(head0, t[:BLOCK], t[BLOCK:])


def _block_loops(s, patterns, unroll, one_block):
    for d in patterns:
        nb = (s // BLOCK) // d
        ur = min(unroll, d)
        ub = unroll // ur
        for r0 in range(0, d, ur):
            def trip(i, carry, d=d, r0=r0, ur=ur, ub=ub):
                for u in range(ub):
                    for r in range(r0, r0 + ur):
                        one_block(i * ub + u, d, r)
                return carry
            lax.fori_loop(0, nb // ub, trip, 0)


def _attn_fwd(z, m_init, l_init, q_blk, k_blk, v_blk, patterns, max_dist, gqa, name, comm=None):
    s = z.shape[0]
    npair = 3

    def body(q_ref, k_ref, v_ref, mi_ref, o_ref, lse_ref, m_scr, l_scr, bias_scr, *kv_scr):
        pair = pl.program_id(0)
        head0 = lax.broadcasted_iota(jnp.int32, (1, LANES), 1) < HEAD_DIM
        _write_band_bias(bias_scr, max_dist)
        k_src, v_src = kv_scr if gqa else (k_ref, v_ref)

        def init(c, carry):
            rows = pl.ds(pl.multiple_of(c * ATTN_ROWS, ATTN_ROWS), ATTN_ROWS)
            m_scr[rows, :] = jnp.broadcast_to(mi_ref[...], (ATTN_ROWS, LANES))
            l_scr[rows, :] = jnp.full((ATTN_ROWS, LANES), l_init, F32)
            o_ref[rows, :] = jnp.zeros((ATTN_ROWS, LANES), F32)
            if gqa:
                k_src[rows, :] = _kv_for_pair(k_ref[rows, :], pair)
                v_src[rows, :] = _kv_for_pair(v_ref[rows, :], pair)
            return carry

        lax.fori_loop(0, s // ATTN_ROWS, init, 0)
        ones = jnp.ones((2 * BLOCK, LANES), BF16)

        def one_block(b, d, r):
            rq, rp = _band_rows(b, d, r)
            q2 = _stack_heads((q_ref[rq, :] * SCALE).astype(BF16), head0)
            k2 = jnp.concatenate([k_src[rp, :], k_src[rq, :]], axis=0)
            v2 = jnp.concatenate([v_src[rp, :], v_src[rq, :]], axis=0)
            sc = _dot_nt(q2, k2.astype(BF16)) + _band_bias(bias_scr, b)
            mb = jnp.max(sc, axis=1, keepdims=True)
            p = jnp.exp(sc - mb).astype(BF16)
            ob = _dot_nn(p, jnp.concatenate([v2.astype(BF16), ones], axis=1))
            m2 = _unstack_heads(jnp.broadcast_to(mb, (2 * BLOCK, LANES)), head0)
            l2 = _unstack_heads(ob[:, LANES:], head0)
            o2 = _unstack_heads(ob[:, :LANES], head0)
            m_old = m_scr[rq, :]
            m_new = jnp.maximum(m_old, m2)
            a_old = jnp.exp(m_old - m_new)
            a_blk = jnp.exp(m2 - m_new)
            o_ref[rq, :] = o_ref[rq, :] * a_old + o2 * a_blk
            l_scr[rq, :] = l_scr[rq, :] * a_old + l2 * a_blk
            m_scr[rq, :] = m_new

        _block_loops(s, patterns, ATTN_UNROLL, one_block)

        def fin(c, carry):
            rows = pl.ds(pl.multiple_of(c * ATTN_ROWS, ATTN_ROWS), ATTN_ROWS)
            l = l_scr[rows, :]
            o_ref[rows, :] = o_ref[rows, :] / l
            lse = m_scr[rows, :] + jnp.log(l)
            swapped = pltpu.roll(lse, HEAD_DIM, 1)
            lse_ref[rows, 0:LANES] = jnp.where(head0, lse, swapped)
            lse_ref[rows, LANES:2 * LANES] = jnp.where(head0, swapped, lse)
            return carry

        lax.fori_loop(0, s // ATTN_ROWS, fin, 0)

    kv = (lambda blk: pl.BlockSpec((s, LANES), lambda j, blk=blk: (0, blk), pipeline_mode=pl.Buffered(1))) if gqa \
        else (lambda blk: pl.BlockSpec((s, LANES), lambda j, blk=blk: (0, blk + j)))
    return _call(
        body,
        grid=(npair,),
        in_specs=[pl.BlockSpec((s, LANES), lambda j: (0, q_blk + j)), kv(k_blk), kv(v_blk),
                  pl.BlockSpec((1, LANES), lambda j: (0, j))],
        out_specs=[pl.BlockSpec((s, LANES), lambda j: (0, j)), pl.BlockSpec((s, 2 * LANES), lambda j: (0, j))],
        out_shape=[jax.ShapeDtypeStruct((s, npair * LANES), F32), jax.ShapeDtypeStruct((s, 2 * npair * LANES), F32)],
        operands=(z, z, z, m_init), name=name,
        scratch_shapes=[pltpu.VMEM((s, LANES), F32)] * 2 + [pltpu.VMEM((2 * BLOCK, 2 * BLOCK), F32)]
        + [pltpu.VMEM((s, LANES), F32)] * (2 if gqa else 0), comm=comm)


def _attn_bwd(z, do, o, lse, m_init, dz, q_blk, k_blk, v_blk, patterns, max_dist, gqa, name, comm=None):
    s = z.shape[0]
    npair = 3
    n_dz_in = 0 if dz is None else 1

    def body(q_ref, k_ref, v_ref, do_ref, o_ref, lse0_ref, lse1_ref, mi_ref, *rest):
        (dz_ref, dm_ref, dq_acc, dk_acc, dv_acc, dl0_scr, dl1_scr, bias_scr,
         dq_out, dk_out, dv_out, out_sems, *gqa_scr) = rest[n_dz_in:]
        pair = pl.program_id(0)
        head0 = lax.broadcasted_iota(jnp.int32, (1, LANES), 1) < HEAD_DIM
        _write_band_bias(bias_scr, max_dist)
        k_src, v_src, dk_sum, dv_sum = gqa_scr if gqa else (k_ref, v_ref, None, None)

        def prep(c, dm):
            rows = pl.ds(pl.multiple_of(c * ATTN_ROWS, ATTN_ROWS), ATTN_ROWS)
            dq_acc[rows, :] = jnp.zeros((ATTN_ROWS, LANES), F32)
            dk_acc[rows, :] = jnp.zeros((ATTN_ROWS, LANES), F32)
            dv_acc[rows, :] = jnp.zeros((ATTN_ROWS, LANES), F32)
            if gqa:
                k_src[rows, :] = _kv_for_pair(k_ref[rows, :], pair)
                v_src[rows, :] = _kv_for_pair(v_ref[rows, :], pair)
            prod = do_ref[rows, :] * o_ref[rows, :]
            d0 = jnp.sum(jnp.where(head0, prod, 0.0), axis=1, keepdims=True)
            d1 = jnp.sum(jnp.where(head0, 0.0, prod), axis=1, keepdims=True)
            dl0_scr[rows, :] = jnp.broadcast_to(d0, (ATTN_ROWS, LANES))
            dl1_scr[rows, :] = jnp.broadcast_to(d1, (ATTN_ROWS, LANES))
            lse_own = jnp.where(head0, lse0_ref[rows, :], lse1_ref[rows, :])
            psink = jnp.exp(mi_ref[...] - lse_own)
            return dm - jnp.sum(psink * jnp.where(head0, d0, d1), axis=0, keepdims=True)

        dm_ref[...] = lax.fori_loop(0, s // ATTN_ROWS, prep, jnp.zeros((1, LANES), F32))

        def one_block(b, d, r):
            rq, rp = _band_rows(b, d, r)
            q2 = _stack_heads((q_ref[rq, :] * SCALE).astype(BF16), head0)
            do2 = _stack_heads(do_ref[rq, :].astype(BF16), head0)
            k2 = jnp.concatenate([k_src[rp, :], k_src[rq, :]], axis=0).astype(BF16)
            v2 = jnp.concatenate([v_src[rp, :], v_src[rq, :]], axis=0).astype(BF16)
            lse2 = jnp.concatenate([lse0_ref[rq, :], lse1_ref[rq, :]], axis=0)
            dl2 = jnp.concatenate([dl0_scr[rq, :], dl1_scr[rq, :]], axis=0)
            lse2 = jnp.concatenate([lse2, lse2], axis=1)
            dl2 = jnp.concatenate([dl2, dl2], axis=1)
            p = jnp.exp(_dot_nt(q2, k2) + _band_bias(bias_scr, b) - lse2)
            dp = _dot_nt(do2, v2)
            dsc = (p * (dp - dl2)).astype(BF16)
            dq2 = _unstack_heads(_dot_nn(dsc, k2), head0)
            dk2 = _dot_tn(dsc, q2)
            dv2 = _dot_tn(p.astype(BF16), do2)
            dq_acc[rq, :] += dq2 * SCALE
            dk_acc[rp, :] += dk2[:BLOCK]
            dk_acc[rq, :] += dk2[BLOCK:]
            dv_acc[rp, :] += dv2[:BLOCK]
            dv_acc[rq, :] += dv2[BLOCK:]

        _block_loops(s, patterns, ATTN_UNROLL, one_block)

        def to_dz(staged, blk, k):
            cols = pl.ds(pl.multiple_of(blk * LANES, LANES), LANES)
            return pltpu.make_async_copy(staged, dz_ref.at[:, cols], out_sems.at[k])

        last = pair == npair - 1
        q_copy = to_dz(dq_out, q_blk + pair, 0)
        kv_copies = [to_dz(dk_out, k_blk + (0 if gqa else pair), 1), to_dz(dv_out, v_blk + (0 if gqa else pair), 2)]

        @pl.when(pair > 0)
        def _():
            for cp in [q_copy] + ([] if gqa else kv_copies):
                cp.wait()

        def out(c, carry):
            rows = pl.ds(pl.multiple_of(c * ATTN_ROWS, ATTN_ROWS), ATTN_ROWS)
            dq_out[rows, :] = dq_acc[rows, :].astype(BF16)
            if not gqa:
                dk_out[rows, :] = dk_acc[rows, :].astype(BF16)
                dv_out[rows, :] = dv_acc[rows, :].astype(BF16)
                return carry
            dk_t = _kv_grad_from_pair(dk_acc[rows, :], pair)
            dv_t = _kv_grad_from_pair(dv_acc[rows, :], pair)

            @pl.when(pair == 0)
            def _():
                dk_sum[rows, :] = dk_t
                dv_sum[rows, :] = dv_t

            @pl.when((pair > 0) & (pair < npair - 1))
            def _():
                dk_sum[rows, :] += dk_t
                dv_sum[rows, :] += dv_t

            @pl.when(last)
            def _():
                dk_out[rows, :] = (dk_sum[rows, :] + dk_t).astype(BF16)
                dv_out[rows, :] = (dv_sum[rows, :] + dv_t).astype(BF16)

            return carry

        lax.fori_loop(0, s // ATTN_ROWS, out, 0)
        q_copy.start()
        if not gqa:
            for cp in kv_copies:
                cp.start()

        @pl.when(last)
        def _():
            if gqa:
                for cp in kv_copies:
                    cp.start()
            for cp in [q_copy] + kv_copies:
                cp.wait()

    own = pl.BlockSpec((s, LANES), lambda j: (0, j))
    hbm = pl.BlockSpec(memory_space=pl.ANY)
    if gqa:
        kv = lambda blk: pl.BlockSpec((s, LANES), lambda j, blk=blk: (0, blk), pipeline_mode=pl.Buffered(1))
    else:
        kv = lambda blk: pl.BlockSpec((s, LANES), lambda j, blk=blk: (0, blk + j))
    in_specs = [pl.BlockSpec((s, LANES), lambda j: (0, q_blk + j)), kv(k_blk), kv(v_blk), own, own,
                pl.BlockSpec((s, LANES), lambda j: (0, 2 * j)), pl.BlockSpec((s, LANES), lambda j: (0, 2 * j + 1)),
                pl.BlockSpec((1, LANES), lambda j: (0, j))]
    operands = (z, z, z, do, o, lse, lse, m_init)
    return _call(
        body,
        grid=(npair,),
        in_specs=in_specs + [hbm] * n_dz_in,
        out_specs=[hbm, pl.BlockSpec((1, LANES), lambda j: (0, j))],
        out_shape=[jax.ShapeDtypeStruct((s, IN_WIDTH), BF16), jax.ShapeDtypeStruct((1, npair * LANES), F32)],
        operands=operands + (() if dz is None else (dz,)), name=name,
        scratch_shapes=[pltpu.VMEM((s, LANES), F32)] * 5 + [pltpu.VMEM((2 * BLOCK, 2 * BLOCK), F32)]
        + [pltpu.VMEM((s, LANES), BF16)] * 3 + [pltpu.SemaphoreType.DMA((3,))]
        + [pltpu.VMEM((s, LANES), F32)] * (4 if gqa else 0),
        comm=comm, aliases={} if dz is None else {len(in_specs): 0})


def _adamw_math(w, g, m, v):
    m = ADAM_B1 * m + (1.0 - ADAM_B1) * g
    v = ADAM_B2 * v + (1.0 - ADAM_B2) * (g * g)
    m_hat = m / (1.0 - ADAM_B1 ** ADAM_STEP)
    v_hat = v / (1.0 - ADAM_B2 ** ADAM_STEP)
    delta = -ADAM_LR * (m_hat / (jnp.sqrt(v_hat) + ADAM_EPS) + ADAM_WD * w)
    return delta, m, v


def _adamw(w, g, m, v, name):
    rows, cols = w.shape
    tr = min(rows, 256)

    def body(w_ref, g_ref, m_ref, v_ref, d_ref, nm_ref, nv_ref):
        d_ref[...], nm_ref[...], nv_ref[...] = _adamw_math(w_ref[...], g_ref[...], m_ref[...], v_ref[...])

    spec = pl.BlockSpec((tr, cols), lambda i: (i, 0))
    return pl.pallas_call(
        body,
        grid=(rows // tr,),
        in_specs=[spec] * 4,
        out_specs=[spec] * 3,
        out_shape=[jax.ShapeDtypeStruct((rows, cols), F32)] * 3,
        compiler_params=_params("parallel"),
        name=name,
    )(w, g, m, v)


def _sum_adamw(parts, w, m, v, pos, transpose, name):
    assert len(parts) == DEPTH == 2
    (p0, r0), (p1, r1) = parts
    _, rows, cols = p0.shape
    tr = 256 if rows % 256 == 0 else rows
    nt = rows // tr

    def body(pos_ref, p0_ref, r0_ref, p1_ref, r1_ref, w_ref, m_ref, v_ref, g_ref, d_ref, nm_ref, nv_ref):
        def run(p_ref, r_ref):
            g = ((p_ref[...].astype(F32) + r_ref[0].astype(F32)) + r_ref[1].astype(F32)) + r_ref[2].astype(F32)
            if transpose:
                g = g.T
            g_ref[...] = g
            d_ref[...], nm_ref[...], nv_ref[...] = _adamw_math(w_ref[...], g, m_ref[...], v_ref[...])

        layer0 = pl.program_id(0) < nt
        pl.when(layer0)(lambda: run(p0_ref, r0_ref))
        pl.when(jnp.logical_not(layer0))(lambda: run(p1_ref, r1_ref))

    def tile0(i):
        return jnp.minimum(i, nt - 1)

    def tile1(i):
        return jnp.maximum(i - nt, 0)

    if transpose:
        w_spec = pl.BlockSpec((None, cols, tr), lambda i, q: (i // nt, 0, i % nt))
    else:
        w_spec = pl.BlockSpec((None, tr, cols), lambda i, q: (i // nt, i % nt, 0))
    return pl.pallas_call(
        body,
        grid_spec=pltpu.PrefetchScalarGridSpec(
            num_scalar_prefetch=1,
            grid=(DEPTH * nt,),
            in_specs=[pl.BlockSpec((None, tr, cols), lambda i, q: (q[0], tile0(i), 0)),
                      pl.BlockSpec((3, tr, cols), lambda i, q: (0, tile0(i), 0)),
                      pl.BlockSpec((None, tr, cols), lambda i, q: (q[0], tile1(i), 0)),
                      pl.BlockSpec((3, tr, cols), lambda i, q: (0, tile1(i), 0)),
                      w_spec, w_spec, w_spec],
            out_specs=[w_spec] * 4,
        ),
        out_shape=[jax.ShapeDtypeStruct(w.shape, F32)] * 4,
        compiler_params=_params("arbitrary"),
        name=name,
    )(pos, p0, r0, p1, r1, w, m, v)


def _small_sum_adamw(gathered, w, m, v, name):
    _, rows, cols = gathered.shape

    def body(ga_ref, w_ref, m_ref, v_ref, g_ref, d_ref, nm_ref, nv_ref):
        g = ga_ref[0]
        for i in range(1, N_DEV):
            g = g + ga_ref[i]
        g_ref[...] = g
        d_ref[...], nm_ref[...], nv_ref[...] = _adamw_math(w_ref[...], g, m_ref[...], v_ref[...])

    return pl.pallas_call(
        body,
        out_shape=[jax.ShapeDtypeStruct((rows, cols), F32)] * 4,
        name=name,
    )(gathered, w, m, v)


def _pair_sum(g4, r1, pos, name):
    _, _, rows, cols = g4.shape
    tr = min(rows, 512)

    def body(pos_ref, g_ref, r_ref, o_ref):
        o_ref[...] = (g_ref[...].astype(F32) + r_ref[...].astype(F32)).astype(BF16)

    return pl.pallas_call(
        body,
        grid_spec=pltpu.PrefetchScalarGridSpec(
            num_scalar_prefetch=1,
            grid=(4, rows // tr),
            in_specs=[pl.BlockSpec((None, None, tr, cols), lambda i, j, p: (i, p[1], j, 0)),
                      pl.BlockSpec((None, tr, cols), lambda i, j, p: (i, j, 0))],
            out_specs=pl.BlockSpec((None, tr, cols), lambda i, j, p: (i, j, 0)),
        ),
        out_shape=jax.ShapeDtypeStruct((4, rows, cols), BF16),
        compiler_params=_params("parallel", "parallel"),
        name=name,
    )(pos, g4, r1)


def _place():
    return lax.axis_index("x"), lax.axis_index("y"), lax.axis_index("c")


def _gather_comm(shards):
    na = len(shards)

    def plan(ins, outs, sems):
        send_sems, recv_sems, local_sems = sems
        x, y, c = _place()
        me, sibling = (x, y, c), (x, y, 1 - c)
        north = c == 1
        near = (jnp.where(north, 1 - x, x), jnp.where(north, y, 1 - y))
        far = (jnp.where(north, x, 1 - x), jnp.where(north, 1 - y, y))
        chips = [near, far, (1 - x, 1 - y)]

        def rows(a, px, py, pc):
            m = ins[a].shape[0]
            return outs[a].at[pl.ds((4 * px + 2 * py + pc) * m, m), :]

        def copy(a, k, block, to, src=None):
            return pltpu.make_async_remote_copy(
                src_ref=rows(a, *block) if src is None else src, dst_ref=rows(a, *block),
                send_sem=send_sems.at[a, k], recv_sem=recv_sems.at[a, k], device_id=to, device_id_type=MESH)

        mine = [pltpu.make_async_copy(ins[a], rows(a, *me), local_sems.at[a]) for a in range(na)]
        first = []
        for a in range(na):
            first.append(copy(a, 0, me, sibling, src=ins[a]))
            first += [copy(a, 1 + j, me, (*chip, c), src=ins[a]) for j, chip in enumerate(chips[:2])]
        return me, sibling, chips, c, copy, mine, first

    def start(ins, outs, sems):
        *_, mine, first = plan(ins, outs, sems)
        for cp in mine + first:
            cp.start()

    def finish(ins, outs, sems):
        me, sibling, chips, c, copy, mine, first = plan(ins, outs, sems)
        near, far, _ = chips
        passed = []
        for j, chip in enumerate(chips):
            for a in range(na):
                copy(a, 1 + j, (*chip, c), me).wait_recv()
                onward = [copy(a, 4 + j, (*chip, c), sibling)]
                if j == 0:
                    onward.append(copy(a, 3, (*near, c), (*far, c)))
                for cp in onward:
                    cp.start()
                passed += onward
        for a in range(na):
            copy(a, 0, sibling, me).wait_recv()
            for j, chip in enumerate([far, near, chips[2]]):
                copy(a, 4 + j, (*chip, 1 - c), me).wait_recv()
        for cp in first + passed:
            cp.wait_send()
        for cp in mine:
            cp.wait()

    return _Comm(tuple(shards),
                 tuple(jax.ShapeDtypeStruct((N_DEV * t.shape[0], t.shape[1]), t.dtype) for t in shards),
                 (pltpu.SemaphoreType.DMA((na, 7)), pltpu.SemaphoreType.DMA((na, 7)), pltpu.SemaphoreType.DMA((na,))),
                 start, finish)


def _exchange_comm(arrays, out_shape, n_copies, copies_of):
    na = len(arrays)

    def every(ins, outs, sems):
        send_sems, recv_sems = sems
        return [cp for a in range(na) for cp in copies_of(ins, outs, a, send_sems, recv_sems)]

    def start(ins, outs, sems):
        for cp in every(ins, outs, sems):
            cp.start()

    def finish(ins, outs, sems):
        for cp in every(ins, outs, sems):
            cp.wait()

    return _Comm(tuple(arrays), tuple(out_shape),
                 (pltpu.SemaphoreType.DMA((na, n_copies)), pltpu.SemaphoreType.DMA((na, n_copies))), start, finish)


def _sibling_comm(grads):
    def copies_of(ins, outs, a, send_sems, recv_sems):
        x, y, c = _place()
        return [pltpu.make_async_remote_copy(
            src_ref=ins[a].at[chip, 1 - c], dst_ref=outs[a].at[chip],
            send_sem=send_sems.at[a, chip], recv_sem=recv_sems.at[a, chip],
            device_id=(x, y, 1 - c), device_id_type=MESH) for chip in range(4)]

    return _exchange_comm(grads, [jax.ShapeDtypeStruct((4,) + t.shape[2:], t.dtype) for t in grads], 4, copies_of)


def _chip_comm(partials):
    def copies_of(ins, outs, a, send_sems, recv_sems):
        x, y, c = _place()
        chips = [(1 - x, y), (x, 1 - y), (1 - x, 1 - y)]
        return [pltpu.make_async_remote_copy(
            src_ref=ins[a].at[2 * cx + cy], dst_ref=outs[a].at[k],
            send_sem=send_sems.at[a, k], recv_sem=recv_sems.at[a, k],
            device_id=(cx, cy, c), device_id_type=MESH) for k, (cx, cy) in enumerate(chips)]

    return _exchange_comm(partials, [jax.ShapeDtypeStruct((3,) + t.shape[1:], t.dtype) for t in partials], 3, copies_of)


def _pad_rows(t, rows):
    return jnp.pad(t, ((0, rows - t.shape[0]), (0, D_MODEL - t.shape[1])))


LOSS_ROW = 25


def _pack_small(g_mix, g_group, g_mlp, g_final, conv, sinks, loss=None):
    loss_row = jnp.zeros((1, LANES), F32) if loss is None else loss
    final_and_loss = jnp.concatenate([g_final.reshape(1, D_MODEL), _pad_rows(loss_row, 1)], axis=0)
    return jnp.concatenate([
        _pad_rows(g_mix, 8), _pad_rows(g_group, 8), _pad_rows(g_mlp, 8), _pad_rows(final_and_loss, 8),
        _pad_rows(conv.reshape(DEPTH * 3, CONV_CH), 8), _pad_rows(sinks.reshape(1, DEPTH * 6), 8)], axis=0)


def _unpack_small(slab):
    return (slab[0:2], slab[8:10], slab[16:18], slab[24], slab[32:38, :CONV_CH].reshape(DEPTH, 3, CONV_CH),
            slab[40, :DEPTH * 6].reshape(DEPTH, 2, 3))


def kernel(x, w_in, conv_w, sinks, g_mix, g_group, w_o, g_mlp, w_ff_in, w_ff_out, g_final, loss_target, m_w_in, m_conv_w, m_sinks, m_g_mix, m_g_group, m_w_o, m_g_mlp, m_w_ff_in, m_w_ff_out, m_g_final, v_w_in, v_conv_w, v_sinks, v_g_mix, v_g_group, v_w_o, v_g_mlp, v_w_ff_in, v_w_ff_out, v_g_final):
    ax, ay, ac = _place()
    chip = 2 * ax + ay
    dev = 4 * ax + 2 * ay + ac
    pos = jnp.stack([chip, ac]).astype(jnp.int32)

    x0 = x.reshape(SEQ, D_MODEL)
    target = loss_target.reshape(SEQ, D_MODEL)

    shards = {}
    for l in range(DEPTH):
        shards[l, 0], shards[l, 1] = w_in[l].T.astype(BF16), w_o[l].astype(BF16)
        shards[l, 2], shards[l, 3] = w_ff_in[l].T.astype(BF16), w_ff_out[l].astype(BF16)
    conv_tile = jnp.pad(conv_w.reshape(DEPTH * 3, CONV_CH // N_DEV), ((0, 2), (0, LANES - CONV_CH // N_DEV)))
    wt_in0, conv_all = _comm_only(_gather_comm([shards[0, 0], conv_tile]), "gather_first")
    conv_full = conv_all.reshape(N_DEV, 8, LANES)[:, :DEPTH * 3, :CONV_CH // N_DEV]
    conv_full = conv_full.transpose(1, 0, 2).reshape(DEPTH, 3, CONV_CH)

    dx, parts, small = _step(x0, target, shards, wt_in0, conv_full, sinks, g_mix, g_group, g_mlp, g_final, pos)
    return _finish(dx, parts, small, pos, dev, w_in, conv_w, sinks, g_mix, g_group, w_o, g_mlp, w_ff_in, w_ff_out, g_final, m_w_in, m_conv_w, m_sinks, m_g_mix, m_g_group, m_w_o, m_g_mlp, m_w_ff_in, m_w_ff_out, m_g_final, v_w_in, v_conv_w, v_sinks, v_g_mix, v_g_group, v_w_o, v_g_mlp, v_w_ff_in, v_w_ff_out, v_g_final)


FWD_CARRY = {(0, "in_proj"): ((0, 1),), (0, "window"): ((1, 0),), (0, "dilated"): ((0, 2),),
             (0, "mix_out"): ((1, 1),), (0, "ff_in"): ((0, 3),), (0, "ff_out"): ((1, 3),),
             (1, "dilated"): ((1, 2),)}


def _step(x0, target, shards, wt_in0, conv_full, sinks, g_mix, g_group, g_mlp, g_final, pos):
    sink_lanes = jnp.repeat(sinks.reshape(DEPTH, 6), HEAD_DIM, axis=1)
    no_sink = jnp.full((1, A_WIDTH), NEG_BIG, F32)
    full = {(0, 0): wt_in0}

    def gather(stage, l):
        keys = FWD_CARRY.get((l, stage), ())
        return keys, (_gather_comm([shards[k] for k in keys]) if keys else None)

    def landed(keys, got):
        full.update(zip(keys, got))

    saved = []
    xc = x0
    for l in range(DEPTH):
        keys, comm = gather("in_proj", l)
        (z, h), got = _norm_mm(xc, g_mix[l:l + 1], full[l, 0], False, f"in_proj_{l}", comm)
        landed(keys, got)
        sink_l = sink_lanes[l:l + 1]
        keys, comm = gather("window", l)
        (yc, lse_c), got = _attn_fwd(z, sink_l, 1.0, QC_BLK, KC_BLK, VC_BLK, (1,), C_MAX_DIST, True,
                                     f"window_attn_{l}", comm)
        landed(keys, got)
        yb = _conv_fwd(z, conv_full[l], f"conv_{l}")
        keys, comm = gather("dilated", l)
        (ya, lse_a), got = _attn_fwd(z, no_sink, 0.0, QA_BLK, KA_BLK, VA_BLK, DILATED_PATTERNS, A_MAX_DIST, False,
                                     f"dilated_attn_{l}", comm)
        landed(keys, got)
        keys, comm = gather("mix_out", l)
        (y, x1), got = _mix_out(ya, yb, yc, g_group[l:l + 1], full[l, 1], xc, f"mix_out_{l}", comm)
        landed(keys, got)
        keys, comm = gather("ff_in", l)
        (a, h2), got = _norm_mm(x1, g_mlp[l:l + 1], full[l, 2], True, f"ff_in_{l}", comm)
        landed(keys, got)
        keys, comm = gather("ff_out", l)
        (x2,), got = _mm_res(a, full[l, 3], x1, f"ff_out_{l}", comm)
        landed(keys, got)
        saved.append((xc, z, h, ya, lse_a, yb, yc, lse_c, sink_l, y, x1, a, h2))
        xc = x2

    loss_slab, dx, dxb, dg_final = _loss_head(xc, g_final.reshape(1, D_MODEL), target, "loss_head")

    def by_owner(t):
        return t.reshape(4, 2, t.shape[0] // N_DEV, D_MODEL)

    def pair(key, g, r1):
        return _pair_sum(g, r1, pos, f"grad_pair_sum_{key[0]}_{key[1]}")

    partial, r2 = {}, {}
    dg_mix, dg_group, dg_mlp, dconv, dsinks = [None] * DEPTH, [None] * DEPTH, [None] * DEPTH, [None] * DEPTH, [None] * DEPTH
    for l in reversed(range(DEPTH)):
        xin, z, h, ya, lse_a, yb, yc, lse_c, sink_l, y, x1, a, h2 = saved[l]
        late = [(l + 1, 1), (l + 1, 0)] if l + 1 < DEPTH else []
        (du,), got = _mlp_bwd_act(dxb, full[l, 3], a, f"ff_out_bwd_{l}",
                                  _chip_comm([partial[k] for k in late]) if late else None)
        r2.update(zip(late, got))
        (g3,), _ = _mm_tn(a, dxb, f"grad_w_ff_out_{l}")
        (g2,), _ = _mm_tn(du, h2, f"grad_w_ff_in_{l}")
        g3, g2 = by_owner(g3), by_owner(g2)
        (dx1, dx1b, dg_mlp[l]), got = _mm_nn_normbwd(du, full[l, 2], x1, dx, g_mlp[l:l + 1], f"ff_in_bwd_{l}",
                                                    _sibling_comm([g3, g2]))
        partial[l, 3], partial[l, 2] = pair((l, 3), g3, got[0]), pair((l, 2), g2, got[1])
        (g1,), _ = _mm_tn(y, dx1b, f"grad_w_o_{l}")
        g1 = by_owner(g1)
        (dya, dyb, dyc, dg_group[l]), got = _mix_bwd(dx1b, full[l, 1], ya, yb, yc, g_group[l:l + 1],
                                                     f"mix_out_bwd_{l}", _sibling_comm([g1]) if l == 0 else None)
        if l == 0:
            partial[l, 1] = pair((l, 1), g1, got[0])
        early = [(l, 3), (l, 2)] + ([(l, 1)] if l == 0 else [])
        (dz, _), got = _attn_bwd(z, dya, ya, lse_a, no_sink, None, QA_BLK, KA_BLK, VA_BLK, DILATED_PATTERNS,
                                 A_MAX_DIST, False, f"dilated_attn_bwd_{l}", _chip_comm([partial[k] for k in early]))
        r2.update(zip(early, got))
        dz, dcw = _conv_bwd(z, conv_full[l], dyb, dz, f"conv_bwd_{l}")
        (dz, dsink), _ = _attn_bwd(z, dyc, yc, lse_c, sink_l, dz, QC_BLK, KC_BLK, VC_BLK, (1,), C_MAX_DIST,
                                   True, f"window_attn_bwd_{l}")
        (g0,), _ = _mm_tn(dz, h, f"grad_w_in_{l}")
        g0 = by_owner(g0)
        if l > 0:
            (dx, dxb, dg_mix[l]), got = _mm_nn_normbwd(dz, full[l, 0], xin, dx1, g_mix[l:l + 1], f"in_proj_bwd_{l}",
                                                      _sibling_comm([g1, g0]))
            partial[l, 1], partial[l, 0] = pair((l, 1), g1, got[0]), pair((l, 0), g0, got[1])
        else:
            (r1,) = _comm_only(_sibling_comm([g0]), "grad_sibling_exchange_last")
            partial[l, 0] = pair((l, 0), g0, r1)
            (dx, dxb, dg_mix[l]), got = _mm_nn_normbwd(dz, full[l, 0], xin, dx1, g_mix[l:l + 1], f"in_proj_bwd_{l}",
                                                      _chip_comm([partial[l, 0]]))
            r2[l, 0] = got[0]
        dconv[l] = dcw[:3]
        dsinks[l] = dsink[0, ::HEAD_DIM]
    parts = {key: (partial[key], r2[key]) for key in partial}
    small = _pack_small(jnp.concatenate(dg_mix), jnp.concatenate(dg_group), jnp.concatenate(dg_mlp),
                        dg_final, jnp.stack(dconv), jnp.stack(dsinks), loss_slab[0:1])
    return dx, parts, small


def _finish(dx, parts, small, pos, dev, w_in, conv_w, sinks, g_mix, g_group, w_o, g_mlp, w_ff_in, w_ff_out, g_final, m_w_in, m_conv_w, m_sinks, m_g_mix, m_g_group, m_w_o, m_g_mlp, m_w_ff_in, m_w_ff_out, m_g_final, v_w_in, v_conv_w, v_sinks, v_g_mix, v_g_group, v_w_o, v_g_mlp, v_w_ff_in, v_w_ff_out, v_g_final):
    grad_x = dx.reshape(1, SEQ, D_MODEL)

    (small_all,) = _comm_only(_gather_comm([small]), "gather_small_grads")
    zeros_conv = jnp.zeros((DEPTH, 3, CONV_CH), F32)
    sw = _pack_small(g_mix, g_group, g_mlp, g_final, zeros_conv, sinks)
    sm = _pack_small(m_g_mix, m_g_group, m_g_mlp, m_g_final, zeros_conv, m_sinks)
    sv = _pack_small(v_g_mix, v_g_group, v_g_mlp, v_g_final, zeros_conv, v_sinks)
    sg, sd, snm, snv = _small_sum_adamw(small_all.reshape(N_DEV, SMALL_ROWS, D_MODEL), sw, sm, sv, "small_adamw")
    loss = sg[LOSS_ROW, 0]
    grad_g_mix, grad_g_group, grad_g_mlp, grad_g_final, conv_grad_full, grad_sinks = _unpack_small(sg)
    delta_g_mix, delta_g_group, delta_g_mlp, delta_g_final, _, delta_sinks = _unpack_small(sd)
    new_m_g_mix, new_m_g_group, new_m_g_mlp, new_m_g_final, _, new_m_sinks = _unpack_small(snm)
    new_v_g_mix, new_v_g_group, new_v_g_mlp, new_v_g_final, _, new_v_sinks = _unpack_small(snv)
    cs = CONV_CH // N_DEV
    grad_conv_w = lax.dynamic_slice_in_dim(conv_grad_full, dev * cs, cs, axis=2)

    def tile_of(t):
        return jnp.pad(t.reshape(1, DEPTH * 3 * cs), ((0, 7), (0, 256 - DEPTH * 3 * cs)))

    cd, cm, cv = _adamw(tile_of(conv_w), tile_of(grad_conv_w), tile_of(m_conv_w), tile_of(v_conv_w), "conv_adamw")
    untile = lambda t: t[0, :DEPTH * 3 * cs].reshape(DEPTH, 3, cs)
    delta_conv_w, new_m_conv_w, new_v_conv_w = untile(cd), untile(cm), untile(cv)

    def big(kind, w, m, v, transpose, name):
        return _sum_adamw([parts[l, kind] for l in range(DEPTH)], w, m, v, pos, transpose, name)

    grad_w_in, delta_w_in, new_m_w_in, new_v_w_in = big(0, w_in, m_w_in, v_w_in, True, "adamw_w_in")
    grad_w_o, delta_w_o, new_m_w_o, new_v_w_o = big(1, w_o, m_w_o, v_w_o, False, "adamw_w_o")
    grad_w_ff_in, delta_w_ff_in, new_m_w_ff_in, new_v_w_ff_in = big(2, w_ff_in, m_w_ff_in, v_w_ff_in, True, "adamw_w_ff_in")
    grad_w_ff_out, delta_w_ff_out, new_m_w_ff_out, new_v_w_ff_out = big(3, w_ff_out, m_w_ff_out, v_w_ff_out, False,
                                                                         "adamw_w_ff_out")

    return (loss, grad_x, grad_w_in, grad_conv_w, grad_sinks, grad_g_mix, grad_g_group, grad_w_o, grad_g_mlp,
            grad_w_ff_in, grad_w_ff_out, grad_g_final,
            delta_w_in, delta_conv_w, delta_sinks, delta_g_mix, delta_g_group, delta_w_o, delta_g_mlp,
            delta_w_ff_in, delta_w_ff_out, delta_g_final,
            new_m_w_in, new_m_conv_w, new_m_sinks, new_m_g_mix, new_m_g_group, new_m_w_o, new_m_g_mlp,
            new_m_w_ff_in, new_m_w_ff_out, new_m_g_final,
            new_v_w_in, new_v_conv_w, new_v_sinks, new_v_g_mix, new_v_g_group, new_v_w_o, new_v_g_mlp,
            new_v_w_ff_in, new_v_w_ff_out, new_v_g_final)
```

```python
from typing import Callable, NamedTuple

import jax
import jax.numpy as jnp
from jax import lax
from jax.experimental import pallas as pl
from jax.experimental.pallas import tpu as pltpu

F32 = jnp.float32
BF16 = jnp.bfloat16
MESH = pl.DeviceIdType.MESH

N_DEV = 8
SEQ = 4096
D_MODEL = 1024
DEPTH = 2
HEAD_DIM = 64
LANES = 128
A_WIDTH = 384
CONV_CH = 256
C_WIDTH = 384
KV_WIDTH = 128
IN_WIDTH = 2560
D_FF = 4096
BLOCK = 128
DILATED_PATTERNS = (1, 4, 16)
A_MAX_DIST = 128
C_MAX_DIST = 127
EPS = 1e-6
SCALE = HEAD_DIM ** -0.5
NEG_BIG = -1e30
F32_TINY = 1.1754944e-38

QA_BLK, KA_BLK, VA_BLK = 0, 3, 6
GB_BLK, GC_BLK, XB_BLK = 9, 11, 13
QC_BLK, KC_BLK, VC_BLK = 15, 18, 19

ADAM_LR = 0.001
ADAM_B1 = 0.9
ADAM_B2 = 0.999
ADAM_EPS = 1e-08
ADAM_WD = 0.01
ADAM_STEP = 10

VMEM_LIMIT = 56 * 1024 * 1024
ROW_TILE = 512
COL_CHUNK = 512
SMALL_ROWS = 48


def _dot_nn(a, b):
    return lax.dot_general(a, b, (((1,), (0,)), ((), ())), preferred_element_type=F32)


def _dot_nt(a, b):
    return lax.dot_general(a, b, (((1,), (1,)), ((), ())), preferred_element_type=F32)


def _dot_tn(a, b):
    return lax.dot_general(a, b, (((0,), (0,)), ((), ())), preferred_element_type=F32)


def _params(*sem):
    return pltpu.CompilerParams(dimension_semantics=sem, vmem_limit_bytes=VMEM_LIMIT)


def _rms_scale(t):
    return lax.rsqrt(jnp.mean(t * t, axis=-1, keepdims=True) + EPS)


def _rms_bwd(n, r, dn):
    return r * (dn - n * jnp.mean(dn * n, axis=-1, keepdims=True))


class _Comm(NamedTuple):
    arrays: tuple
    out_shape: tuple
    sems: tuple
    start: Callable
    finish: Callable


def _call(body, grid, in_specs, out_specs, out_shape, operands, name, scratch_shapes=(), comm=None, aliases=None):
    n_in, n_out, n_scr = len(in_specs), len(out_shape), len(scratch_shapes)
    aliases = dict(aliases or {})
    if comm is None:
        res = pl.pallas_call(body, grid=grid, in_specs=list(in_specs), out_specs=list(out_specs),
                             out_shape=list(out_shape), scratch_shapes=list(scratch_shapes),
                             input_output_aliases=aliases,
                             compiler_params=_params("arbitrary"), name=name)(*operands)
        return list(res), []
    c_in, c_out = len(comm.arrays), len(comm.out_shape)
    hbm = pl.BlockSpec(memory_space=pl.ANY)
    last = grid[0] - 1

    def carried(*refs):
        ins, cins = refs[:n_in], refs[n_in:n_in + c_in]
        o0 = n_in + c_in
        outs, couts = refs[o0:o0 + n_out], refs[o0 + n_out:o0 + n_out + c_out]
        s0 = o0 + n_out + c_out
        scr, sems = refs[s0:s0 + n_scr], refs[s0 + n_scr:]
        pl.when(pl.program_id(0) == 0)(lambda: comm.start(cins, couts, sems))
        body(*ins, *outs, *scr)
        pl.when(pl.program_id(0) == last)(lambda: comm.finish(cins, couts, sems))

    res = pl.pallas_call(carried, grid=grid, in_specs=list(in_specs) + [hbm] * c_in,
                         out_specs=list(out_specs) + [hbm] * c_out, out_shape=list(out_shape) + list(comm.out_shape),
                         scratch_shapes=list(scratch_shapes) + list(comm.sems), input_output_aliases=aliases,
                         compiler_params=_params("arbitrary"), name=name)(*operands, *comm.arrays)
    return list(res[:n_out]), list(res[n_out:])


def _comm_only(comm, name):
    hbm = pl.BlockSpec(memory_space=pl.ANY)
    c_in, c_out = len(comm.arrays), len(comm.out_shape)

    def body(*refs):
        ins, outs, sems = refs[:c_in], refs[c_in:c_in + c_out], refs[c_in + c_out:]
        comm.start(ins, outs, sems)
        comm.finish(ins, outs, sems)

    return pl.pallas_call(body, in_specs=[hbm] * c_in, out_specs=[hbm] * c_out, out_shape=list(comm.out_shape),
                          scratch_shapes=list(comm.sems), name=name)(*comm.arrays)


def _norm_mm(x, g, wt, relu2, name, comm=None):
    s, d = x.shape
    n = wt.shape[0]
    tm = ROW_TILE

    def body(x_ref, g_ref, w_ref, o_ref, h_ref):
        xx = x_ref[...]
        h = ((xx * _rms_scale(xx)) * g_ref[...]).astype(BF16)
        h_ref[...] = h
        for n0 in range(0, n, COL_CHUNK):
            zc = _dot_nt(h, w_ref[n0:n0 + COL_CHUNK, :])
            if relu2:
                zc = jnp.square(jnp.maximum(zc, 0.0)).astype(BF16)
            o_ref[:, n0:n0 + COL_CHUNK] = zc

    return _call(
        body,
        grid=(s // tm,),
        in_specs=[pl.BlockSpec((tm, d), lambda i: (i, 0)),
                  pl.BlockSpec((1, d), lambda i: (0, 0)),
                  pl.BlockSpec((n, d), lambda i: (0, 0))],
        out_specs=[pl.BlockSpec((tm, n), lambda i: (i, 0)),
                   pl.BlockSpec((tm, d), lambda i: (i, 0))],
        out_shape=[jax.ShapeDtypeStruct((s, n), BF16 if relu2 else F32), jax.ShapeDtypeStruct((s, d), BF16)],
        operands=(x, g, wt), name=name, comm=comm)


def _mm_res(a, w2, x1, name, comm=None):
    s, f = a.shape
    d = w2.shape[1]
    tm = ROW_TILE

    def body(a_ref, w_ref, x_ref, o_ref):
        o_ref[...] = x_ref[...] + _dot_nn(a_ref[...], w_ref[...])

    return _call(
        body,
        grid=(s // tm,),
        in_specs=[pl.BlockSpec((tm, f), lambda i: (i, 0)),
                  pl.BlockSpec((f, d), lambda i: (0, 0)),
                  pl.BlockSpec((tm, d), lambda i: (i, 0))],
        out_specs=[pl.BlockSpec((tm, d), lambda i: (i, 0))],
        out_shape=[jax.ShapeDtypeStruct((s, d), F32)],
        operands=(a, w2, x1), name=name, comm=comm)


def _mix_out(ya, yb, yc, gg, wo, x0, name, comm=None):
    s = ya.shape[0]
    d = wo.shape[1]
    tm = ROW_TILE

    def body(ya_ref, yb_ref, yc_ref, g_ref, w_ref, x_ref, y_ref, o_ref):
        parts = []
        for ref in (ya_ref, yb_ref, yc_ref):
            t = ref[...]
            parts.append(t * _rms_scale(t))
        y = (jnp.concatenate(parts, axis=1) * g_ref[...]).astype(BF16)
        y_ref[...] = y
        o_ref[...] = x_ref[...] + _dot_nn(y, w_ref[...])

    return _call(
        body,
        grid=(s // tm,),
        in_specs=[pl.BlockSpec((tm, A_WIDTH), lambda i: (i, 0)),
                  pl.BlockSpec((tm, CONV_CH), lambda i: (i, 0)),
                  pl.BlockSpec((tm, C_WIDTH), lambda i: (i, 0)),
                  pl.BlockSpec((1, d), lambda i: (0, 0)),
                  pl.BlockSpec((d, d), lambda i: (0, 0)),
                  pl.BlockSpec((tm, d), lambda i: (i, 0))],
        out_specs=[pl.BlockSpec((tm, d), lambda i: (i, 0)),
                   pl.BlockSpec((tm, d), lambda i: (i, 0))],
        out_shape=[jax.ShapeDtypeStruct((s, d), BF16), jax.ShapeDtypeStruct((s, d), F32)],
        operands=(ya, yb, yc, gg, wo, x0), name=name, comm=comm)


def _loss_head(x, g, target, name):
    s, d = x.shape
    tm = ROW_TILE

    def body(x_ref, g_ref, t_ref, loss_ref, dx_ref, dxb_ref, dg_ref):
        @pl.when(pl.program_id(0) == 0)
        def _():
            loss_ref[...] = jnp.zeros_like(loss_ref)
            dg_ref[...] = jnp.zeros_like(dg_ref)

        xx = x_ref[...]
        r = _rms_scale(xx)
        n = xx * r
        gv = g_ref[...]
        err = n * gv - t_ref[...]
        per_tok = jnp.sum(err * err, axis=1, keepdims=True) * (1.0 / d)
        loss_ref[...] += 0.5 * jnp.sum(per_tok, axis=0, keepdims=True)
        dout = err * (1.0 / d)
        dg_ref[...] += jnp.sum(dout * n, axis=0, keepdims=True)
        dx = _rms_bwd(n, r, dout * gv)
        dx_ref[...] = dx
        dxb_ref[...] = dx.astype(BF16)

    return pl.pallas_call(
        body,
        grid=(s // tm,),
        in_specs=[pl.BlockSpec((tm, d), lambda i: (i, 0)),
                  pl.BlockSpec((1, d), lambda i: (0, 0)),
                  pl.BlockSpec((tm, d), lambda i: (i, 0))],
        out_specs=[pl.BlockSpec((8, LANES), lambda i: (0, 0)),
                   pl.BlockSpec((tm, d), lambda i: (i, 0)),
                   pl.BlockSpec((tm, d), lambda i: (i, 0)),
                   pl.BlockSpec((1, d), lambda i: (0, 0))],
        out_shape=[jax.ShapeDtypeStruct((8, LANES), F32), jax.ShapeDtypeStruct((s, d), F32),
                   jax.ShapeDtypeStruct((s, d), BF16), jax.ShapeDtypeStruct((1, d), F32)],
        compiler_params=_params("arbitrary"),
        name=name,
    )(x, g, target)


def _mlp_bwd_act(dxb, w2, a, name, comm=None):
    s, d = dxb.shape
    f = w2.shape[0]
    tm = ROW_TILE

    def body(dx_ref, w_ref, a_ref, du_ref):
        dx = dx_ref[...]
        for n0 in range(0, f, COL_CHUNK):
            da = _dot_nt(dx, w_ref[n0:n0 + COL_CHUNK, :])
            av = a_ref[:, n0:n0 + COL_CHUNK].astype(F32)
            rl = av * lax.rsqrt(jnp.maximum(av, F32_TINY))
            du_ref[:, n0:n0 + COL_CHUNK] = (da * (2.0 * rl)).astype(BF16)

    return _call(
        body,
        grid=(s // tm,),
        in_specs=[pl.BlockSpec((tm, d), lambda i: (i, 0)),
                  pl.BlockSpec((f, d), lambda i: (0, 0)),
                  pl.BlockSpec((tm, f), lambda i: (i, 0))],
        out_specs=[pl.BlockSpec((tm, f), lambda i: (i, 0))],
        out_shape=[jax.ShapeDtypeStruct((s, f), BF16)],
        operands=(dxb, w2, a), name=name, comm=comm)


def _mm_tn(a, b, name, comm=None):
    s, n = a.shape
    d = b.shape[1]
    tn = 512

    def body(a_ref, b_ref, o_ref, acc):
        for k0 in range(0, s, ROW_TILE):
            part = _dot_tn(a_ref[k0:k0 + ROW_TILE, :], b_ref[k0:k0 + ROW_TILE, :])
            if k0 == 0:
                acc[...] = part
            else:
                acc[...] += part
        o_ref[...] = acc[...].astype(BF16)

    return _call(
        body,
        grid=(n // tn,),
        in_specs=[pl.BlockSpec((s, tn), lambda j: (0, j)),
                  pl.BlockSpec((s, d), lambda j: (0, 0))],
        out_specs=[pl.BlockSpec((tn, d), lambda j: (j, 0))],
        out_shape=[jax.ShapeDtypeStruct((n, d), BF16)],
        operands=(a, b), name=name, scratch_shapes=[pltpu.VMEM((tn, d), F32)], comm=comm)


def _mm_nn_normbwd(dact, wt, x, dres, g, name, comm=None):
    s, kdim = dact.shape
    d = wt.shape[1]
    tm = ROW_TILE

    def body(a_ref, w_ref, x_ref, r_ref, g_ref, o_ref, ob_ref, dg_ref):
        @pl.when(pl.program_id(0) == 0)
        def _():
            dg_ref[...] = jnp.zeros_like(dg_ref)

        dh = _dot_nn(a_ref[...], w_ref[...])
        xx = x_ref[...]
        r = _rms_scale(xx)
        n = xx * r
        dg_ref[...] += jnp.sum(dh * n, axis=0, keepdims=True)
        dx = r_ref[...] + _rms_bwd(n, r, dh * g_ref[...])
        o_ref[...] = dx
        ob_ref[...] = dx.astype(BF16)

    return _call(
        body,
        grid=(s // tm,),
        in_specs=[pl.BlockSpec((tm, kdim), lambda i: (i, 0)),
                  pl.BlockSpec((kdim, d), lambda i: (0, 0)),
                  pl.BlockSpec((tm, d), lambda i: (i, 0)),
                  pl.BlockSpec((tm, d), lambda i: (i, 0)),
                  pl.BlockSpec((1, d), lambda i: (0, 0))],
        out_specs=[pl.BlockSpec((tm, d), lambda i: (i, 0)),
                   pl.BlockSpec((tm, d), lambda i: (i, 0)),
                   pl.BlockSpec((1, d), lambda i: (0, 0))],
        out_shape=[jax.ShapeDtypeStruct((s, d), F32), jax.ShapeDtypeStruct((s, d), BF16),
                   jax.ShapeDtypeStruct((1, d), F32)],
        operands=(dact, wt, x, dres, g), name=name, comm=comm)


def _mix_bwd(dx1, wo, ya, yb, yc, gg, name, comm=None):
    s, d = dx1.shape
    tm = ROW_TILE
    widths = (A_WIDTH, CONV_CH, C_WIDTH)

    def body(dx_ref, w_ref, ya_ref, yb_ref, yc_ref, g_ref, da_ref, db_ref, dc_ref, dg_ref):
        @pl.when(pl.program_id(0) == 0)
        def _():
            dg_ref[...] = jnp.zeros_like(dg_ref)

        dy = _dot_nt(dx_ref[...], w_ref[...])
        gv = g_ref[...]
        off = 0
        dgs = []
        for ref, out, w in zip((ya_ref, yb_ref, yc_ref), (da_ref, db_ref, dc_ref), widths):
            t = ref[...]
            r = _rms_scale(t)
            n = t * r
            dyg = dy[:, off:off + w]
            dgs.append(jnp.sum(dyg * n, axis=0, keepdims=True))
            out[...] = _rms_bwd(n, r, dyg * gv[:, off:off + w])
            off += w
        dg_ref[...] += jnp.concatenate(dgs, axis=1)

    return _call(
        body,
        grid=(s // tm,),
        in_specs=[pl.BlockSpec((tm, d), lambda i: (i, 0)),
                  pl.BlockSpec((d, d), lambda i: (0, 0)),
                  pl.BlockSpec((tm, A_WIDTH), lambda i: (i, 0)),
                  pl.BlockSpec((tm, CONV_CH), lambda i: (i, 0)),
                  pl.BlockSpec((tm, C_WIDTH), lambda i: (i, 0)),
                  pl.BlockSpec((1, d), lambda i: (0, 0))],
        out_specs=[pl.BlockSpec((tm, A_WIDTH), lambda i: (i, 0)),
                   pl.BlockSpec((tm, CONV_CH), lambda i: (i, 0)),
                   pl.BlockSpec((tm, C_WIDTH), lambda i: (i, 0)),
                   pl.BlockSpec((1, d), lambda i: (0, 0))],
        out_shape=[jax.ShapeDtypeStruct((s, A_WIDTH), F32), jax.ShapeDtypeStruct((s, CONV_CH), F32),
                   jax.ShapeDtypeStruct((s, C_WIDTH), F32), jax.ShapeDtypeStruct((1, d), F32)],
        operands=(dx1, wo, ya, yb, yc, gg), name=name, comm=comm)


CONV_CHUNK = 256
CONV_HALO = 8


def _conv_fwd(z, cw, name):
    s = z.shape[0]
    nch = s // CONV_CHUNK

    def body(gb_ref, gc_ref, xb_ref, w_ref, o_ref, us):
        us[pl.ds(0, CONV_HALO), :] = jnp.zeros((CONV_HALO, LANES), F32)
        us[pl.ds(CONV_HALO, s), :] = gc_ref[...] * xb_ref[...]
        w0, w1, w2 = w_ref[0:1, :], w_ref[1:2, :], w_ref[2:3, :]

        def chunk(c, carry):
            st = pl.multiple_of(c * CONV_CHUNK, CONV_CHUNK)
            ext = us[pl.ds(st, CONV_CHUNK + CONV_HALO), :]
            y = (w0 * ext[CONV_HALO - 2:CONV_HALO - 2 + CONV_CHUNK]
                 + w1 * ext[CONV_HALO - 1:CONV_HALO - 1 + CONV_CHUNK]
                 + w2 * ext[CONV_HALO:])
            o_ref[pl.ds(st, CONV_CHUNK), :] = gb_ref[pl.ds(st, CONV_CHUNK), :] * y
            return carry

        lax.fori_loop(0, nch, chunk, 0)

    col = lambda blk: pl.BlockSpec((s, LANES), lambda j, blk=blk: (0, blk + j))
    return pl.pallas_call(
        body,
        grid=(CONV_CH // LANES,),
        in_specs=[col(GB_BLK), col(GC_BLK), col(XB_BLK), pl.BlockSpec((3, LANES), lambda j: (0, j))],
        out_specs=pl.BlockSpec((s, LANES), lambda j: (0, j)),
        out_shape=jax.ShapeDtypeStruct((s, CONV_CH), F32),
        scratch_shapes=[pltpu.VMEM((s + CONV_HALO, LANES), F32)],
        compiler_params=_params("parallel"),
        name=name,
    )(z, z, z, cw)


def _conv_bwd(z, cw, dyb, dz, name):
    s = z.shape[0]
    nch = s // CONV_CHUNK
    ncol = CONV_CH // LANES

    def body(gb_ref, gc_ref, xb_ref, w_ref, dy_ref, dz_in, dz_ref, dw_ref, us, ds_, dgb_ref, dgc_ref, dxb_ref, sems):
        j = pl.program_id(0)

        def to_dz(staged, blk, k):
            cols = pl.ds(pl.multiple_of((blk + j) * LANES, LANES), LANES)
            return pltpu.make_async_copy(staged, dz_ref.at[:, cols], sems.at[k])

        copies = [to_dz(dgb_ref, GB_BLK, 0), to_dz(dgc_ref, GC_BLK, 1), to_dz(dxb_ref, XB_BLK, 2)]

        @pl.when(j > 0)
        def _():
            for cp in copies:
                cp.wait()

        us[pl.ds(0, CONV_HALO), :] = jnp.zeros((CONV_HALO, LANES), F32)
        us[pl.ds(CONV_HALO, s), :] = gc_ref[...] * xb_ref[...]
        ds_[pl.ds(s, CONV_HALO), :] = jnp.zeros((CONV_HALO, LANES), F32)
        ds_[pl.ds(0, s), :] = dy_ref[...] * gb_ref[...]
        w0, w1, w2 = w_ref[0:1, :], w_ref[1:2, :], w_ref[2:3, :]
        zero = jnp.zeros((1, LANES), F32)

        def chunk(c, carry):
            a0, a1, a2 = carry
            st = pl.multiple_of(c * CONV_CHUNK, CONV_CHUNK)
            rows = pl.ds(st, CONV_CHUNK)
            ext = us[pl.ds(st, CONV_CHUNK + CONV_HALO), :]
            um2 = ext[CONV_HALO - 2:CONV_HALO - 2 + CONV_CHUNK]
            um1 = ext[CONV_HALO - 1:CONV_HALO - 1 + CONV_CHUNK]
            u0 = ext[CONV_HALO:]
            dext = ds_[pl.ds(st, CONV_CHUNK + CONV_HALO), :]
            dc0 = dext[:CONV_CHUNK]
            du = w2 * dc0 + w1 * dext[1:1 + CONV_CHUNK] + w0 * dext[2:2 + CONV_CHUNK]
            yconv = w0 * um2 + w1 * um1 + w2 * u0
            dgb_ref[rows, :] = (dy_ref[rows, :] * yconv).astype(BF16)
            dgc_ref[rows, :] = (du * xb_ref[rows, :]).astype(BF16)
            dxb_ref[rows, :] = (du * gc_ref[rows, :]).astype(BF16)
            a0 = a0 + jnp.sum(dc0 * um2, axis=0, keepdims=True)
            a1 = a1 + jnp.sum(dc0 * um1, axis=0, keepdims=True)
            a2 = a2 + jnp.sum(dc0 * u0, axis=0, keepdims=True)
            return a0, a1, a2

        a0, a1, a2 = lax.fori_loop(0, nch, chunk, (zero, zero, zero))
        dw_ref[...] = jnp.concatenate([a0, a1, a2, jnp.zeros((5, LANES), F32)], axis=0)
        for cp in copies:
            cp.start()

        @pl.when(j == ncol - 1)
        def _():
            for cp in copies:
                cp.wait()

    col = lambda blk: pl.BlockSpec((s, LANES), lambda j, blk=blk: (0, blk + j))
    hbm = pl.BlockSpec(memory_space=pl.ANY)
    return pl.pallas_call(
        body,
        grid=(ncol,),
        in_specs=[col(GB_BLK), col(GC_BLK), col(XB_BLK), pl.BlockSpec((3, LANES), lambda j: (0, j)),
                  pl.BlockSpec((s, LANES), lambda j: (0, j)), hbm],
        out_specs=[hbm, pl.BlockSpec((8, LANES), lambda j: (0, j))],
        out_shape=[jax.ShapeDtypeStruct(dz.shape, dz.dtype), jax.ShapeDtypeStruct((8, CONV_CH), F32)],
        scratch_shapes=[pltpu.VMEM((s + CONV_HALO, LANES), F32), pltpu.VMEM((s + CONV_HALO, LANES), F32)]
        + [pltpu.VMEM((s, LANES), BF16)] * 3 + [pltpu.SemaphoreType.DMA((3,))],
        input_output_aliases={5: 0},
        compiler_params=_params("arbitrary"),
        name=name,
    )(z, z, z, cw, dyb, dz)


ATTN_ROWS = 512
ATTN_UNROLL = 8


def _band_rows(b, d, r):
    base = pl.multiple_of(b * (BLOCK * d), BLOCK)
    prev = jnp.maximum(base - BLOCK * d, 0)
    if d == 1:
        return pl.ds(base, BLOCK), pl.ds(pl.multiple_of(prev, BLOCK), BLOCK)
    return pl.ds(base + r, BLOCK, stride=d), pl.ds(prev + r, BLOCK, stride=d)


def _write_band_bias(bias_ref, max_dist):
    qi = lax.broadcasted_iota(jnp.int32, (BLOCK, 2 * BLOCK), 0)
    kj = lax.broadcasted_iota(jnp.int32, (BLOCK, 2 * BLOCK), 1)
    dist = BLOCK + qi - kj
    band = (dist >= 0) & (dist <= max_dist)
    bias_ref[0:BLOCK, :] = jnp.where(band, 0.0, -jnp.inf)
    bias_ref[BLOCK:2 * BLOCK, :] = jnp.where(band & (kj >= BLOCK), 0.0, -jnp.inf)


def _band_bias(bias_ref, b):
    bias = bias_ref[pl.ds(pl.multiple_of(jnp.where(b > 0, 0, BLOCK), BLOCK), BLOCK), :]
    return jnp.concatenate([bias, bias], axis=0)


def _lane_half():
    return (lax.broadcasted_iota(jnp.int32, (1, LANES), 1) >= HEAD_DIM).astype(jnp.int32)


def _kv_for_pair(t, pair):
    half = _lane_half()
    want = (pair + half) >> 1
    return jnp.where(want != half, pltpu.roll(t, HEAD_DIM, 1), t)


def _kv_grad_from_pair(t, pair):
    half = _lane_half()
    mine = ((pair + half) >> 1) == half
    other = ((pair + 1 - half) >> 1) == half
    fold = t + pltpu.roll(t, HEAD_DIM, 1)
    return jnp.where(mine & other, fold, jnp.where(mine, t, 0.0))


def _stack_heads(t, head0):
    zero = jnp.zeros_like(t)
    return jnp.concatenate([jnp.where(head0, t, zero), jnp.where(head0, zero, t)], axis=0)


def _unstack_heads(t, head0):
    return jnp.where(head0, t[:BLOCK], t[BLOCK:])


def _block_loops(s, patterns, unroll, one_block):
    for d in patterns:
        nb = (s // BLOCK) // d
        ur = min(unroll, d)
        ub = unroll // ur
        for r0 in range(0, d, ur):
            def trip(i, carry, d=d, r0=r0, ur=ur, ub=ub):
                for u in range(ub):
                    for r in range(r0, r0 + ur):
                        one_block(i * ub + u, d, r)
                return carry
            lax.fori_loop(0, nb // ub, trip, 0)


def _attn_fwd(z, m_init, l_init, q_blk, k_blk, v_blk, patterns, max_dist, gqa, name, comm=None):
    s = z.shape[0]
    npair = 3

    def body(q_ref, k_ref, v_ref, mi_ref, o_ref, lse_ref, m_scr, l_scr, bias_scr, *kv_scr):
        pair = pl.program_id(0)
        head0 = lax.broadcasted_iota(jnp.int32, (1, LANES), 1) < HEAD_DIM
        _write_band_bias(bias_scr, max_dist)
        k_src, v_src = kv_scr if gqa else (k_ref, v_ref)

        def init(c, carry):
            rows = pl.ds(pl.multiple_of(c * ATTN_ROWS, ATTN_ROWS), ATTN_ROWS)
            m_scr[rows, :] = jnp.broadcast_to(mi_ref[...], (ATTN_ROWS, LANES))
            l_scr[rows, :] = jnp.full((ATTN_ROWS, LANES), l_init, F32)
            o_ref[rows, :] = jnp.zeros((ATTN_ROWS, LANES), F32)
            if gqa:
                k_src[rows, :] = _kv_for_pair(k_ref[rows, :], pair)
                v_src[rows, :] = _kv_for_pair(v_ref[rows, :], pair)
            return carry

        lax.fori_loop(0, s // ATTN_ROWS, init, 0)
        ones = jnp.ones((2 * BLOCK, LANES), BF16)

        def one_block(b, d, r):
            rq, rp = _band_rows(b, d, r)
            q2 = _stack_heads((q_ref[rq, :] * SCALE).astype(BF16), head0)
            k2 = jnp.concatenate([k_src[rp, :], k_src[rq, :]], axis=0)
            v2 = jnp.concatenate([v_src[rp, :], v_src[rq, :]], axis=0)
            sc = _dot_nt(q2, k2.astype(BF16)) + _band_bias(bias_scr, b)
            mb = jnp.max(sc, axis=1, keepdims=True)
            p = jnp.exp(sc - mb).astype(BF16)
            ob = _dot_nn(p, jnp.concatenate([v2.astype(BF16), ones], axis=1))
            m2 = _unstack_heads(jnp.broadcast_to(mb, (2 * BLOCK, LANES)), head0)
            l2 = _unstack_heads(ob[:, LANES:], head0)
            o2 = _unstack_heads(ob[:, :LANES], head0)
            m_old = m_scr[rq, :]
            m_new = jnp.maximum(m_old, m2)
            a_old = jnp.exp(m_old - m_new)
            a_blk = jnp.exp(m2 - m_new)
            o_ref[rq, :] = o_ref[rq, :] * a_old + o2 * a_blk
            l_scr[rq, :] = l_scr[rq, :] * a_old + l2 * a_blk
            m_scr[rq, :] = m_new

        _block_loops(s, patterns, ATTN_UNROLL, one_block)

        def fin(c, carry):
            rows = pl.ds(pl.multiple_of(c * ATTN_ROWS, ATTN_ROWS), ATTN_ROWS)
            l = l_scr[rows, :]
            o_ref[rows, :] = o_ref[rows, :] / l
            lse = m_scr[rows, :] + jnp.log(l)
            swapped = pltpu.roll(lse, HEAD_DIM, 1)
            lse_ref[rows, 0:LANES] = jnp.where(head0, lse, swapped)
            lse_ref[rows, LANES:2 * LANES] = jnp.where(head0, swapped, lse)
            return carry

        lax.fori_loop(0, s // ATTN_ROWS, fin, 0)

    kv = (lambda blk: pl.BlockSpec((s, LANES), lambda j, blk=blk: (0, blk), pipeline_mode=pl.Buffered(1))) if gqa \
        else (lambda blk: pl.BlockSpec((s, LANES), lambda j, blk=blk: (0, blk + j)))
    return _call(
        body,
        grid=(npair,),
        in_specs=[pl.BlockSpec((s, LANES), lambda j: (0, q_blk + j)), kv(k_blk), kv(v_blk),
                  pl.BlockSpec((1, LANES), lambda j: (0, j))],
        out_specs=[pl.BlockSpec((s, LANES), lambda j: (0, j)), pl.BlockSpec((s, 2 * LANES), lambda j: (0, j))],
        out_shape=[jax.ShapeDtypeStruct((s, npair * LANES), F32), jax.ShapeDtypeStruct((s, 2 * npair * LANES), F32)],
        operands=(z, z, z, m_init), name=name,
        scratch_shapes=[pltpu.VMEM((s, LANES), F32)] * 2 + [pltpu.VMEM((2 * BLOCK, 2 * BLOCK), F32)]
        + [pltpu.VMEM((s, LANES), F32)] * (2 if gqa else 0), comm=comm)


def _attn_bwd(z, do, o, lse, m_init, dz, q_blk, k_blk, v_blk, patterns, max_dist, gqa, name, comm=None):
    s = z.shape[0]
    npair = 3
    n_dz_in = 0 if dz is None else 1

    def body(q_ref, k_ref, v_ref, do_ref, o_ref, lse0_ref, lse1_ref, mi_ref, *rest):
        (dz_ref, dm_ref, dq_acc, dk_acc, dv_acc, dl0_scr, dl1_scr, bias_scr,
         dq_out, dk_out, dv_out, out_sems, *gqa_scr) = rest[n_dz_in:]
        pair = pl.program_id(0)
        head0 = lax.broadcasted_iota(jnp.int32, (1, LANES), 1) < HEAD_DIM
        _write_band_bias(bias_scr, max_dist)
        k_src, v_src, dk_sum, dv_sum = gqa_scr if gqa else (k_ref, v_ref, None, None)

        def prep(c, dm):
            rows = pl.ds(pl.multiple_of(c * ATTN_ROWS, ATTN_ROWS), ATTN_ROWS)
            dq_acc[rows, :] = jnp.zeros((ATTN_ROWS, LANES), F32)
            dk_acc[rows, :] = jnp.zeros((ATTN_ROWS, LANES), F32)
            dv_acc[rows, :] = jnp.zeros((ATTN_ROWS, LANES), F32)
            if gqa:
                k_src[rows, :] = _kv_for_pair(k_ref[rows, :], pair)
                v_src[rows, :] = _kv_for_pair(v_ref[rows, :], pair)
            prod = do_ref[rows, :] * o_ref[rows, :]
            d0 = jnp.sum(jnp.where(head0, prod, 0.0), axis=1, keepdims=True)
            d1 = jnp.sum(jnp.where(head0, 0.0, prod), axis=1, keepdims=True)
            dl0_scr[rows, :] = jnp.broadcast_to(d0, (ATTN_ROWS, LANES))
            dl1_scr[rows, :] = jnp.broadcast_to(d1, (ATTN_ROWS, LANES))
            lse_own = jnp.where(head0, lse0_ref[rows, :], lse1_ref[rows, :])
            psink = jnp.exp(mi_ref[...] - lse_own)
            return dm - jnp.sum(psink * jnp.where(head0, d0, d1), axis=0, keepdims=True)

        dm_ref[...] = lax.fori_loop(0, s // ATTN_ROWS, prep, jnp.zeros((1, LANES), F32))

        def one_block(b, d, r):
            rq, rp = _band_rows(b, d, r)
            q2 = _stack_heads((q_ref[rq, :] * SCALE).astype(BF16), head0)
            do2 = _stack_heads(do_ref[rq, :].astype(BF16), head0)
            k2 = jnp.concatenate([k_src[rp, :], k_src[rq, :]], axis=0).astype(BF16)
            v2 = jnp.concatenate([v_src[rp, :], v_src[rq, :]], axis=0).astype(BF16)
            lse2 = jnp.concatenate([lse0_ref[rq, :], lse1_ref[rq, :]], axis=0)
            dl2 = jnp.concatenate([dl0_scr[rq, :], dl1_scr[rq, :]], axis=0)
            lse2 = jnp.concatenate([lse2, lse2], axis=1)
            dl2 = jnp.concatenate([dl2, dl2], axis=1)
            p = jnp.exp(_dot_nt(q2, k2) + _band_bias(bias_scr, b) - lse2)
            dp = _dot_nt(do2, v2)
            dsc = (p * (dp - dl2)).astype(BF16)
            dq2 = _unstack_heads(_dot_nn(dsc, k2), head0)
            dk2 = _dot_tn(dsc, q2)
            dv2 = _dot_tn(p.astype(BF16), do2)
            dq_acc[rq, :] += dq2 * SCALE
            dk_acc[rp, :] += dk2[:BLOCK]
            dk_acc[rq, :] += dk2[BLOCK:]
            dv_acc[rp, :] += dv2[:BLOCK]
            dv_acc[rq, :] += dv2[BLOCK:]

        _block_loops(s, patterns, ATTN_UNROLL, one_block)

        def to_dz(staged, blk, k):
            cols = pl.ds(pl.multiple_of(blk * LANES, LANES), LANES)
            return pltpu.make_async_copy(staged, dz_ref.at[:, cols], out_sems.at[k])

        last = pair == npair - 1
        q_copy = to_dz(dq_out, q_blk + pair, 0)
        kv_copies = [to_dz(dk_out, k_blk + (0 if gqa else pair), 1), to_dz(dv_out, v_blk + (0 if gqa else pair), 2)]

        @pl.when(pair > 0)
        def _():
            for cp in [q_copy] + ([] if gqa else kv_copies):
                cp.wait()

        def out(c, carry):
            rows = pl.ds(pl.multiple_of(c * ATTN_ROWS, ATTN_ROWS), ATTN_ROWS)
            dq_out[rows, :] = dq_acc[rows, :].astype(BF16)
            if not gqa:
                dk_out[rows, :] = dk_acc[rows, :].astype(BF16)
                dv_out[rows, :] = dv_acc[rows, :].astype(BF16)
                return carry
            dk_t = _kv_grad_from_pair(dk_acc[rows, :], pair)
            dv_t = _kv_grad_from_pair(dv_acc[rows, :], pair)

            @pl.when(pair == 0)
            def _():
                dk_sum[rows, :] = dk_t
                dv_sum[rows, :] = dv_t

            @pl.when((pair > 0) & (pair < npair - 1))
            def _():
                dk_sum[rows, :] += dk_t
                dv_sum[rows, :] += dv_t

            @pl.when(last)
            def _():
                dk_out[rows, :] = (dk_sum[rows, :] + dk_t).astype(BF16)
                dv_out[rows, :] = (dv_sum[rows, :] + dv_t).astype(BF16)

            return carry

        lax.fori_loop(0, s // ATTN_ROWS, out, 0)
        q_copy.start()
        if not gqa:
            for cp in kv_copies:
                cp.start()

        @pl.when(last)
        def _():
            if gqa:
                for cp in kv_copies:
                    cp.start()
            for cp in [q_copy] + kv_copies:
                cp.wait()

    own = pl.BlockSpec((s, LANES), lambda j: (0, j))
    hbm = pl.BlockSpec(memory_space=pl.ANY)
    if gqa:
        kv = lambda blk: pl.BlockSpec((s, LANES), lambda j, blk=blk: (0, blk), pipeline_mode=pl.Buffered(1))
    else:
        kv = lambda blk: pl.BlockSpec((s, LANES), lambda j, blk=blk: (0, blk + j))
    in_specs = [pl.BlockSpec((s, LANES), lambda j: (0, q_blk + j)), kv(k_blk), kv(v_blk), own, own,
                pl.BlockSpec((s, LANES), lambda j: (0, 2 * j)), pl.BlockSpec((s, LANES), lambda j: (0, 2 * j + 1)),
                pl.BlockSpec((1, LANES), lambda j: (0, j))]
    operands = (z, z, z, do, o, lse, lse, m_init)
    return _call(
        body,
        grid=(npair,),
        in_specs=in_specs + [hbm] * n_dz_in,
        out_specs=[hbm, pl.BlockSpec((1, LANES), lambda j: (0, j))],
        out_shape=[jax.ShapeDtypeStruct((s, IN_WIDTH), BF16), jax.ShapeDtypeStruct((1, npair * LANES), F32)],
        operands=operands + (() if dz is None else (dz,)), name=name,
        scratch_shapes=[pltpu.VMEM((s, LANES), F32)] * 5 + [pltpu.VMEM((2 * BLOCK, 2 * BLOCK), F32)]
        + [pltpu.VMEM((s, LANES), BF16)] * 3 + [pltpu.SemaphoreType.DMA((3,))]
        + [pltpu.VMEM((s, LANES), F32)] * (4 if gqa else 0),
        comm=comm, aliases={} if dz is None else {len(in_specs): 0})


def _adamw_math(w, g, m, v):
    m = ADAM_B1 * m + (1.0 - ADAM_B1) * g
    v = ADAM_B2 * v + (1.0 - ADAM_B2) * (g * g)
    m_hat = m / (1.0 - ADAM_B1 ** ADAM_STEP)
    v_hat = v / (1.0 - ADAM_B2 ** ADAM_STEP)
    delta = -ADAM_LR * (m_hat / (jnp.sqrt(v_hat) + ADAM_EPS) + ADAM_WD * w)
    return delta, m, v


def _adamw(w, g, m, v, name):
    rows, cols = w.shape
    tr = min(rows, 256)

    def body(w_ref, g_ref, m_ref, v_ref, d_ref, nm_ref, nv_ref):
        d_ref[...], nm_ref[...], nv_ref[...] = _adamw_math(w_ref[...], g_ref[...], m_ref[...], v_ref[...])

    spec = pl.BlockSpec((tr, cols), lambda i: (i, 0))
    return pl.pallas_call(
        body,
        grid=(rows // tr,),
        in_specs=[spec] * 4,
        out_specs=[spec] * 3,
        out_shape=[jax.ShapeDtypeStruct((rows, cols), F32)] * 3,
        compiler_params=_params("parallel"),
        name=name,
    )(w, g, m, v)


def _sum_adamw(parts, w, m, v, pos, transpose, name):
    assert len(parts) == DEPTH == 2
    (p0, r0), (p1, r1) = parts
    _, rows, cols = p0.shape
    tr = 256 if rows % 256 == 0 else rows
    nt = rows // tr

    def body(pos_ref, p0_ref, r0_ref, p1_ref, r1_ref, w_ref, m_ref, v_ref, g_ref, d_ref, nm_ref, nv_ref):
        def run(p_ref, r_ref):
            g = ((p_ref[...].astype(F32) + r_ref[0].astype(F32)) + r_ref[1].astype(F32)) + r_ref[2].astype(F32)
            if transpose:
                g = g.T
            g_ref[...] = g
            d_ref[...], nm_ref[...], nv_ref[...] = _adamw_math(w_ref[...], g, m_ref[...], v_ref[...])

        layer0 = pl.program_id(0) < nt
        pl.when(layer0)(lambda: run(p0_ref, r0_ref))
        pl.when(jnp.logical_not(layer0))(lambda: run(p1_ref, r1_ref))

    def tile0(i):
        return jnp.minimum(i, nt - 1)

    def tile1(i):
        return jnp.maximum(i - nt, 0)

    if transpose:
        w_spec = pl.BlockSpec((None, cols, tr), lambda i, q: (i // nt, 0, i % nt))
    else:
        w_spec = pl.BlockSpec((None, tr, cols), lambda i, q: (i // nt, i % nt, 0))
    return pl.pallas_call(
        body,
        grid_spec=pltpu.PrefetchScalarGridSpec(
            num_scalar_prefetch=1,
            grid=(DEPTH * nt,),
            in_specs=[pl.BlockSpec((None, tr, cols), lambda i, q: (q[0], tile0(i), 0)),
                      pl.BlockSpec((3, tr, cols), lambda i, q: (0, tile0(i), 0)),
                      pl.BlockSpec((None, tr, cols), lambda i, q: (q[0], tile1(i), 0)),
                      pl.BlockSpec((3, tr, cols), lambda i, q: (0, tile1(i), 0)),
                      w_spec, w_spec, w_spec],
            out_specs=[w_spec] * 4,
        ),
        out_shape=[jax.ShapeDtypeStruct(w.shape, F32)] * 4,
        compiler_params=_params("arbitrary"),
        name=name,
    )(pos, p0, r0, p1, r1, w, m, v)


def _small_sum_adamw(gathered, params, name):
    _, rows, cols = gathered.shape
    n = len(params)

    def body(ga_ref, *refs):
        ins, outs, (g_scr,) = refs[:3 * n], refs[3 * n:7 * n + 2], refs[7 * n + 2:]
        g = ga_ref[0]
        for i in range(1, N_DEV):
            g = g + ga_ref[i]
        g_scr[...] = g
        for k, (row0, w, _, _) in enumerate(params):
            w_ref, m_ref, v_ref = ins[3 * k:3 * k + 3]
            gk = g_scr[row0:row0 + w.shape[0], :]
            outs[4 * k][...] = gk
            outs[4 * k + 1][...], outs[4 * k + 2][...], outs[4 * k + 3][...] = _adamw_math(
                w_ref[...], gk, m_ref[...], v_ref[...])
        outs[4 * n][...] = g_scr[CONV_ROW:CONV_ROW + 8, :]
        outs[4 * n + 1][...] = g_scr[LOSS_ROW:LOSS_ROW + 1, :]

    out_shape = []
    for _, w, _, _ in params:
        out_shape += [jax.ShapeDtypeStruct(w.shape, F32)] * 4
    out_shape += [jax.ShapeDtypeStruct((8, cols), F32), jax.ShapeDtypeStruct((1, cols), F32)]
    res = pl.pallas_call(
        body,
        out_shape=out_shape,
        scratch_shapes=[pltpu.VMEM((rows, cols), F32)],
        name=name,
    )(gathered, *[t for _, w, m, v in params for t in (w, m, v)])
    return [res[4 * k:4 * k + 4] for k in range(n)], res[4 * n], res[4 * n + 1]


def _pair_sum(g4, r1, pos, name):
    _, _, rows, cols = g4.shape
    tr = min(rows, 512)

    def body(pos_ref, g_ref, r_ref, o_ref):
        o_ref[...] = (g_ref[...].astype(F32) + r_ref[...].astype(F32)).astype(BF16)

    return pl.pallas_call(
        body,
        grid_spec=pltpu.PrefetchScalarGridSpec(
            num_scalar_prefetch=1,
            grid=(4, rows // tr),
            in_specs=[pl.BlockSpec((None, None, tr, cols), lambda i, j, p: (i, p[1], j, 0)),
                      pl.BlockSpec((None, tr, cols), lambda i, j, p: (i, j, 0))],
            out_specs=pl.BlockSpec((None, tr, cols), lambda i, j, p: (i, j, 0)),
        ),
        out_shape=jax.ShapeDtypeStruct((4, rows, cols), BF16),
        compiler_params=_params("parallel", "parallel"),
        name=name,
    )(pos, g4, r1)


def _place():
    return lax.axis_index("x"), lax.axis_index("y"), lax.axis_index("c")


def _gather_comm(shards):
    na = len(shards)

    def plan(ins, outs, sems):
        send_sems, recv_sems, local_sems = sems
        x, y, c = _place()
        me, sibling = (x, y, c), (x, y, 1 - c)
        chips = [(1 - x, y), (x, 1 - y), (1 - x, 1 - y)]

        def rows(a, px, py, pc):
            m = ins[a].shape[0]
            return outs[a].at[pl.ds((4 * px + 2 * py + pc) * m, m), :]

        def copy(a, k, block, to, src=None):
            return pltpu.make_async_remote_copy(
                src_ref=rows(a, *block) if src is None else src, dst_ref=rows(a, *block),
                send_sem=send_sems.at[a, k], recv_sem=recv_sems.at[a, k], device_id=to, device_id_type=MESH)

        mine = [pltpu.make_async_copy(ins[a], rows(a, *me), local_sems.at[a]) for a in range(na)]
        first = []
        for a in range(na):
            first.append(copy(a, 0, me, sibling, src=ins[a]))
            first += [copy(a, 1 + j, me, (*chip, c), src=ins[a]) for j, chip in enumerate(chips)]
        return me, sibling, chips, c, copy, mine, first

    def start(ins, outs, sems):
        *_, mine, first = plan(ins, outs, sems)
        for cp in mine + first:
            cp.start()

    def finish(ins, outs, sems):
        me, sibling, chips, c, copy, mine, first = plan(ins, outs, sems)
        passed = []
        for j, chip in enumerate(chips):
            for a in range(na):
                copy(a, 1 + j, (*chip, c), me).wait_recv()
                cp = copy(a, 4 + j, (*chip, c), sibling)
                cp.start()
                passed.append(cp)
        for a in range(na):
            copy(a, 0, sibling, me).wait_recv()
            for j, chip in enumerate(chips):
                copy(a, 4 + j, (*chip, 1 - c), me).wait_recv()
        for cp in first + passed:
            cp.wait_send()
        for cp in mine:
            cp.wait()

    return _Comm(tuple(shards),
                 tuple(jax.ShapeDtypeStruct((N_DEV * t.shape[0], t.shape[1]), t.dtype) for t in shards),
                 (pltpu.SemaphoreType.DMA((na, 7)), pltpu.SemaphoreType.DMA((na, 7)), pltpu.SemaphoreType.DMA((na,))),
                 start, finish)


def _exchange_comm(arrays, out_shape, n_copies, copies_of):
    na = len(arrays)

    def every(ins, outs, sems):
        send_sems, recv_sems = sems
        return [cp for a in range(na) for cp in copies_of(ins, outs, a, send_sems, recv_sems)]

    def start(ins, outs, sems):
        for cp in every(ins, outs, sems):
            cp.start()

    def finish(ins, outs, sems):
        for cp in every(ins, outs, sems):
            cp.wait()

    return _Comm(tuple(arrays), tuple(out_shape),
                 (pltpu.SemaphoreType.DMA((na, n_copies)), pltpu.SemaphoreType.DMA((na, n_copies))), start, finish)


def _sibling_comm(grads):
    def copies_of(ins, outs, a, send_sems, recv_sems):
        x, y, c = _place()
        return [pltpu.make_async_remote_copy(
            src_ref=ins[a].at[chip, 1 - c], dst_ref=outs[a].at[chip],
            send_sem=send_sems.at[a, chip], recv_sem=recv_sems.at[a, chip],
            device_id=(x, y, 1 - c), device_id_type=MESH) for chip in range(4)]

    return _exchange_comm(grads, [jax.ShapeDtypeStruct((4,) + t.shape[2:], t.dtype) for t in grads], 4, copies_of)


def _chip_comm(partials):
    def copies_of(ins, outs, a, send_sems, recv_sems):
        x, y, c = _place()
        chips = [(1 - x, y), (x, 1 - y), (1 - x, 1 - y)]
        return [pltpu.make_async_remote_copy(
            src_ref=ins[a].at[2 * cx + cy], dst_ref=outs[a].at[k],
            send_sem=send_sems.at[a, k], recv_sem=recv_sems.at[a, k],
            device_id=(cx, cy, c), device_id_type=MESH) for k, (cx, cy) in enumerate(chips)]

    return _exchange_comm(partials, [jax.ShapeDtypeStruct((3,) + t.shape[1:], t.dtype) for t in partials], 3, copies_of)


def _pad_rows(t, rows):
    return jnp.pad(t, ((0, rows - t.shape[0]), (0, D_MODEL - t.shape[1])))


MIX_ROW, GROUP_ROW, MLP_ROW, FINAL_ROW, CONV_ROW, SINK_ROW = 0, 8, 16, 24, 32, 40
LOSS_ROW = FINAL_ROW + 1


def _pack_small(g_mix, g_group, g_mlp, g_final, conv, sinks, loss):
    final_and_loss = jnp.concatenate([g_final.reshape(1, D_MODEL), _pad_rows(loss, 1)], axis=0)
    return jnp.concatenate([
        _pad_rows(g_mix, 8), _pad_rows(g_group, 8), _pad_rows(g_mlp, 8), _pad_rows(final_and_loss, 8),
        _pad_rows(conv.reshape(DEPTH * 3, CONV_CH), 8), _pad_rows(sinks.reshape(1, DEPTH * 6), 8)], axis=0)


def kernel(x, w_in, conv_w, sinks, g_mix, g_group, w_o, g_mlp, w_ff_in, w_ff_out, g_final, loss_target, m_w_in, m_conv_w, m_sinks, m_g_mix, m_g_group, m_w_o, m_g_mlp, m_w_ff_in, m_w_ff_out, m_g_final, v_w_in, v_conv_w, v_sinks, v_g_mix, v_g_group, v_w_o, v_g_mlp, v_w_ff_in, v_w_ff_out, v_g_final):
    ax, ay, ac = _place()
    chip = 2 * ax + ay
    dev = 4 * ax + 2 * ay + ac
    pos = jnp.stack([chip, ac]).astype(jnp.int32)

    x0 = x.reshape(SEQ, D_MODEL)
    target = loss_target.reshape(SEQ, D_MODEL)

    shards = {}
    for l in range(DEPTH):
        shards[l, 0], shards[l, 1] = w_in[l].T.astype(BF16), w_o[l].astype(BF16)
        shards[l, 2], shards[l, 3] = w_ff_in[l].T.astype(BF16), w_ff_out[l].astype(BF16)
    conv_tile = jnp.pad(conv_w.reshape(DEPTH * 3, CONV_CH // N_DEV), ((0, 2), (0, LANES - CONV_CH // N_DEV)))
    wt_in0, conv_all = _comm_only(_gather_comm([shards[0, 0], conv_tile]), "gather_first")
    conv_full = conv_all.reshape(N_DEV, 8, LANES)[:, :DEPTH * 3, :CONV_CH // N_DEV]
    conv_full = conv_full.transpose(1, 0, 2).reshape(DEPTH, 3, CONV_CH)

    dx, parts, small = _step(x0, target, shards, wt_in0, conv_full, sinks, g_mix, g_group, g_mlp, g_final, pos)
    return _finish(dx, parts, small, pos, dev, w_in, conv_w, sinks, g_mix, g_group, w_o, g_mlp, w_ff_in, w_ff_out, g_final, m_w_in, m_conv_w, m_sinks, m_g_mix, m_g_group, m_w_o, m_g_mlp, m_w_ff_in, m_w_ff_out, m_g_final, v_w_in, v_conv_w, v_sinks, v_g_mix, v_g_group, v_w_o, v_g_mlp, v_w_ff_in, v_w_ff_out, v_g_final)


FWD_CARRY = {(0, "in_proj"): ((0, 1),), (0, "window"): ((1, 0),), (0, "dilated"): ((0, 2),),
             (0, "mix_out"): ((1, 1),), (0, "ff_in"): ((0, 3),), (0, "ff_out"): ((1, 3),),
             (1, "dilated"): ((1, 2),)}


def _step(x0, target, shards, wt_in0, conv_full, sinks, g_mix, g_group, g_mlp, g_final, pos):
    sink_lanes = jnp.repeat(sinks.reshape(DEPTH, 6), HEAD_DIM, axis=1)
    no_sink = jnp.full((1, A_WIDTH), NEG_BIG, F32)
    full = {(0, 0): wt_in0}

    def gather(stage, l):
        keys = FWD_CARRY.get((l, stage), ())
        return keys, (_gather_comm([shards[k] for k in keys]) if keys else None)

    def landed(keys, got):
        full.update(zip(keys, got))

    saved = []
    xc = x0
    for l in range(DEPTH):
        keys, comm = gather("in_proj", l)
        (z, h), got = _norm_mm(xc, g_mix[l:l + 1], full[l, 0], False, f"in_proj_{l}", comm)
        landed(keys, got)
        sink_l = sink_lanes[l:l + 1]
        keys, comm = gather("window", l)
        (yc, lse_c), got = _attn_fwd(z, sink_l, 1.0, QC_BLK, KC_BLK, VC_BLK, (1,), C_MAX_DIST, True,
                                     f"window_attn_{l}", comm)
        landed(keys, got)
        yb = _conv_fwd(z, conv_full[l], f"conv_{l}")
        keys, comm = gather("dilated", l)
        (ya, lse_a), got = _attn_fwd(z, no_sink, 0.0, QA_BLK, KA_BLK, VA_BLK, DILATED_PATTERNS, A_MAX_DIST, False,
                                     f"dilated_attn_{l}", comm)
        landed(keys, got)
        keys, comm = gather("mix_out", l)
        (y, x1), got = _mix_out(ya, yb, yc, g_group[l:l + 1], full[l, 1], xc, f"mix_out_{l}", comm)
        landed(keys, got)
        keys, comm = gather("ff_in", l)
        (a, h2), got = _norm_mm(x1, g_mlp[l:l + 1], full[l, 2], True, f"ff_in_{l}", comm)
        landed(keys, got)
        keys, comm = gather("ff_out", l)
        (x2,), got = _mm_res(a, full[l, 3], x1, f"ff_out_{l}", comm)
        landed(keys, got)
        saved.append((xc, z, h, ya, lse_a, yb, yc, lse_c, sink_l, y, x1, a, h2))
        xc = x2

    loss_slab, dx, dxb, dg_final = _loss_head(xc, g_final.reshape(1, D_MODEL), target, "loss_head")

    def by_owner(t):
        return t.reshape(4, 2, t.shape[0] // N_DEV, D_MODEL)

    def pair(key, g, r1):
        return _pair_sum(g, r1, pos, f"grad_pair_sum_{key[0]}_{key[1]}")

    partial, r2 = {}, {}
    dg_mix, dg_group, dg_mlp, dconv, dsinks = [None] * DEPTH, [None] * DEPTH, [None] * DEPTH, [None] * DEPTH, [None] * DEPTH
    for l in reversed(range(DEPTH)):
        xin, z, h, ya, lse_a, yb, yc, lse_c, sink_l, y, x1, a, h2 = saved[l]
        late = [(l + 1, 1), (l + 1, 0)] if l + 1 < DEPTH else []
        (du,), got = _mlp_bwd_act(dxb, full[l, 3], a, f"ff_out_bwd_{l}",
                                  _chip_comm([partial[k] for k in late]) if late else None)
        r2.update(zip(late, got))
        (g3,), _ = _mm_tn(a, dxb, f"grad_w_ff_out_{l}")
        (g2,), _ = _mm_tn(du, h2, f"grad_w_ff_in_{l}")
        g3, g2 = by_owner(g3), by_owner(g2)
        (dx1, dx1b, dg_mlp[l]), got = _mm_nn_normbwd(du, full[l, 2], x1, dx, g_mlp[l:l + 1], f"ff_in_bwd_{l}",
                                                    _sibling_comm([g3, g2]))
        partial[l, 3], partial[l, 2] = pair((l, 3), g3, got[0]), pair((l, 2), g2, got[1])
        (g1,), _ = _mm_tn(y, dx1b, f"grad_w_o_{l}")
        g1 = by_owner(g1)
        (dya, dyb, dyc, dg_group[l]), got = _mix_bwd(dx1b, full[l, 1], ya, yb, yc, g_group[l:l + 1],
                                                     f"mix_out_bwd_{l}", _sibling_comm([g1]) if l == 0 else None)
        if l == 0:
            partial[l, 1] = pair((l, 1), g1, got[0])
        early = [(l, 3), (l, 2)] + ([(l, 1)] if l == 0 else [])
        (dz, _), got = _attn_bwd(z, dya, ya, lse_a, no_sink, None, QA_BLK, KA_BLK, VA_BLK, DILATED_PATTERNS,
                                 A_MAX_DIST, False, f"dilated_attn_bwd_{l}", _chip_comm([partial[k] for k in early]))
        r2.update(zip(early, got))
        dz, dcw = _conv_bwd(z, conv_full[l], dyb, dz, f"conv_bwd_{l}")
        (dz, dsink), _ = _attn_bwd(z, dyc, yc, lse_c, sink_l, dz, QC_BLK, KC_BLK, VC_BLK, (1,), C_MAX_DIST,
                                   True, f"window_attn_bwd_{l}")
        (g0,), _ = _mm_tn(dz, h, f"grad_w_in_{l}")
        g0 = by_owner(g0)
        if l > 0:
            (dx, dxb, dg_mix[l]), got = _mm_nn_normbwd(dz, full[l, 0], xin, dx1, g_mix[l:l + 1], f"in_proj_bwd_{l}",
                                                      _sibling_comm([g1, g0]))
            partial[l, 1], partial[l, 0] = pair((l, 1), g1, got[0]), pair((l, 0), g0, got[1])
        else:
            (r1,) = _comm_only(_sibling_comm([g0]), "grad_sibling_exchange_last")
            partial[l, 0] = pair((l, 0), g0, r1)
            (dx, dxb, dg_mix[l]), got = _mm_nn_normbwd(dz, full[l, 0], xin, dx1, g_mix[l:l + 1], f"in_proj_bwd_{l}",
                                                      _chip_comm([partial[l, 0]]))
            r2[l, 0] = got[0]
        dconv[l] = dcw[:3]
        dsinks[l] = dsink[0, ::HEAD_DIM]
    parts = {key: (partial[key], r2[key]) for key in partial}
    small = _pack_small(jnp.concatenate(dg_mix), jnp.concatenate(dg_group), jnp.concatenate(dg_mlp),
                        dg_final, jnp.stack(dconv), jnp.stack(dsinks), loss_slab[0:1])
    return dx, parts, small


def _finish(dx, parts, small, pos, dev, w_in, conv_w, sinks, g_mix, g_group, w_o, g_mlp, w_ff_in, w_ff_out, g_final, m_w_in, m_conv_w, m_sinks, m_g_mix, m_g_group, m_w_o, m_g_mlp, m_w_ff_in, m_w_ff_out, m_g_final, v_w_in, v_conv_w, v_sinks, v_g_mix, v_g_group, v_w_o, v_g_mlp, v_w_ff_in, v_w_ff_out, v_g_final):
    grad_x = dx.reshape(1, SEQ, D_MODEL)

    (small_all,) = _comm_only(_gather_comm([small]), "gather_small_grads")
    row = lambda t: t.reshape(1, D_MODEL)
    sink_row = lambda t: _pad_rows(t.reshape(1, DEPTH * 6), 1)
    params = [(MIX_ROW, g_mix, m_g_mix, v_g_mix), (GROUP_ROW, g_group, m_g_group, v_g_group),
              (MLP_ROW, g_mlp, m_g_mlp, v_g_mlp), (FINAL_ROW, row(g_final), row(m_g_final), row(v_g_final)),
              (SINK_ROW, sink_row(sinks), sink_row(m_sinks), sink_row(v_sinks))]
    updated, conv_rows, loss_row = _small_sum_adamw(small_all.reshape(N_DEV, SMALL_ROWS, D_MODEL), params, "small_adamw")
    loss = loss_row[0, 0]
    (grad_g_mix, delta_g_mix, new_m_g_mix, new_v_g_mix), (grad_g_group, delta_g_group, new_m_g_group, new_v_g_group), \
        (grad_g_mlp, delta_g_mlp, new_m_g_mlp, new_v_g_mlp), final4, sinks4 = updated
    grad_g_final, delta_g_final, new_m_g_final, new_v_g_final = [t.reshape(D_MODEL) for t in final4]
    grad_sinks, delta_sinks, new_m_sinks, new_v_sinks = [t[0, :DEPTH * 6].reshape(DEPTH, 2, 3) for t in sinks4]
    conv_grad_full = conv_rows[:DEPTH * 3, :CONV_CH].reshape(DEPTH, 3, CONV_CH)
    cs = CONV_CH // N_DEV
    grad_conv_w = lax.dynamic_slice_in_dim(conv_grad_full, dev * cs, cs, axis=2)

    def tile_of(t):
        return jnp.pad(t.reshape(1, DEPTH * 3 * cs), ((0, 7), (0, 256 - DEPTH * 3 * cs)))

    cd, cm, cv = _adamw(tile_of(conv_w), tile_of(grad_conv_w), tile_of(m_conv_w), tile_of(v_conv_w), "conv_adamw")
    untile = lambda t: t[0, :DEPTH * 3 * cs].reshape(DEPTH, 3, cs)
    delta_conv_w, new_m_conv_w, new_v_conv_w = untile(cd), untile(cm), untile(cv)

    def big(kind, w, m, v, transpose, name):
        return _sum_adamw([parts[l, kind] for l in range(DEPTH)], w, m, v, pos, transpose, name)

    grad_w_in, delta_w_in, new_m_w_in, new_v_w_in = big(0, w_in, m_w_in, v_w_in, True, "adamw_w_in")
    grad_w_o, delta_w_o, new_m_w_o, new_v_w_o = big(1, w_o, m_w_o, v_w_o, False, "adamw_w_o")
    grad_w_ff_in, delta_w_ff_in, new_m_w_ff_in, new_v_w_ff_in = big(2, w_ff_in, m_w_ff_in, v_w_ff_in, True, "adamw_w_ff_in")
    grad_w_ff_out, delta_w_ff_out, new_m_w_ff_out, new_v_w_ff_out = big(3, w_ff_out, m_w_ff_out, v_w_ff_out, False,
                                                                         "adamw_w_ff_out")

    return (loss, grad_x, grad_w_in, grad_conv_w, grad_sinks, grad_g_mix, grad_g_group, grad_w_o, grad_g_mlp,
            grad_w_ff_in, grad_w_ff_out, grad_g_final,
            delta_w_in, delta_conv_w, delta_sinks, delta_g_mix, delta_g_group, delta_w_o, delta_g_mlp,
            delta_w_ff_in, delta_w_ff_out, delta_g_final,
            new_m_w_in, new_m_conv_w, new_m_sinks, new_m_g_mix, new_m_g_group, new_m_w_o, new_m_g_mlp,
            new_m_w_ff_in, new_m_w_ff_out, new_m_g_final,
            new_v_w_in, new_v_conv_w, new_v_sinks, new_v_g_mix, new_v_g_group, new_v_w_o, new_v_g_mlp,
            new_v_w_ff_in, new_v_w_ff_out, new_v_g_final)
```

```python
from typing import Callable, NamedTuple

import jax
import jax.numpy as jnp
from jax import lax
from jax.experimental import pallas as pl
from jax.experimental.pallas import tpu as pltpu

F32 = jnp.float32
BF16 = jnp.bfloat16
MESH = pl.DeviceIdType.MESH

N_DEV = 8
SEQ = 4096
D_MODEL = 1024
DEPTH = 2
HEAD_DIM = 64
LANES = 128
A_WIDTH = 384
CONV_CH = 256
C_WIDTH = 384
KV_WIDTH = 128
IN_WIDTH = 2560
D_FF = 4096
BLOCK = 128
DILATED_PATTERNS = (1, 4, 16)
A_MAX_DIST = 128
C_MAX_DIST = 127
EPS = 1e-6
SCALE = HEAD_DIM ** -0.5
NEG_BIG = -1e30
F32_TINY = 1.1754944e-38

QA_BLK, KA_BLK, VA_BLK = 0, 3, 6
GB_BLK, GC_BLK, XB_BLK = 9, 11, 13
QC_BLK, KC_BLK, VC_BLK = 15, 18, 19

ADAM_LR = 0.001
ADAM_B1 = 0.9
ADAM_B2 = 0.999
ADAM_EPS = 1e-08
ADAM_WD = 0.01
ADAM_STEP = 10

VMEM_LIMIT = 56 * 1024 * 1024
ROW_TILE = 512
COL_CHUNK = 512
SMALL_ROWS = 48


def _dot_nn(a, b):
    return lax.dot_general(a, b, (((1,), (0,)), ((), ())), preferred_element_type=F32)


def _dot_nt(a, b):
    return lax.dot_general(a, b, (((1,), (1,)), ((), ())), preferred_element_type=F32)


def _dot_tn(a, b):
    return lax.dot_general(a, b, (((0,), (0,)), ((), ())), preferred_element_type=F32)


def _params(*sem):
    return pltpu.CompilerParams(dimension_semantics=sem, vmem_limit_bytes=VMEM_LIMIT)


def _rms_scale(t):
    return lax.rsqrt(jnp.mean(t * t, axis=-1, keepdims=True) + EPS)


def _rms_bwd(n, r, dn):
    return r * (dn - n * jnp.mean(dn * n, axis=-1, keepdims=True))


class _Comm(NamedTuple):
    arrays: tuple
    out_shape: tuple
    sems: tuple
    start: Callable
    finish: Callable


def _call(body, grid, in_specs, out_specs, out_shape, operands, name, scratch_shapes=(), comm=None, aliases=None):
    n_in, n_out, n_scr = len(in_specs), len(out_shape), len(scratch_shapes)
    aliases = dict(aliases or {})
    if comm is None:
        res = pl.pallas_call(body, grid=grid, in_specs=list(in_specs), out_specs=list(out_specs),
                             out_shape=list(out_shape), scratch_shapes=list(scratch_shapes),
                             input_output_aliases=aliases,
                             compiler_params=_params("arbitrary"), name=name)(*operands)
        return list(res), []
    c_in, c_out = len(comm.arrays), len(comm.out_shape)
    hbm = pl.BlockSpec(memory_space=pl.ANY)
    last = grid[0] - 1

    def carried(*refs):
        ins, cins = refs[:n_in], refs[n_in:n_in + c_in]
        o0 = n_in + c_in
        outs, couts = refs[o0:o0 + n_out], refs[o0 + n_out:o0 + n_out + c_out]
        s0 = o0 + n_out + c_out
        scr, sems = refs[s0:s0 + n_scr], refs[s0 + n_scr:]
        pl.when(pl.program_id(0) == 0)(lambda: comm.start(cins, couts, sems))
        body(*ins, *outs, *scr)
        pl.when(pl.program_id(0) == last)(lambda: comm.finish(cins, couts, sems))

    res = pl.pallas_call(carried, grid=grid, in_specs=list(in_specs) + [hbm] * c_in,
                         out_specs=list(out_specs) + [hbm] * c_out, out_shape=list(out_shape) + list(comm.out_shape),
                         scratch_shapes=list(scratch_shapes) + list(comm.sems), input_output_aliases=aliases,
                         compiler_params=_params("arbitrary"), name=name)(*operands, *comm.arrays)
    return list(res[:n_out]), list(res[n_out:])


def _comm_only(comm, name):
    hbm = pl.BlockSpec(memory_space=pl.ANY)
    c_in, c_out = len(comm.arrays), len(comm.out_shape)

    def body(*refs):
        ins, outs, sems = refs[:c_in], refs[c_in:c_in + c_out], refs[c_in + c_out:]
        comm.start(ins, outs, sems)
        comm.finish(ins, outs, sems)

    return pl.pallas_call(body, in_specs=[hbm] * c_in, out_specs=[hbm] * c_out, out_shape=list(comm.out_shape),
                          scratch_shapes=list(comm.sems), name=name)(*comm.arrays)


def _norm_mm(x, g, wt, relu2, name, comm=None):
    s, d = x.shape
    n = wt.shape[0]
    tm = ROW_TILE

    def body(x_ref, g_ref, w_ref, o_ref, h_ref):
        xx = x_ref[...]
        h = ((xx * _rms_scale(xx)) * g_ref[...]).astype(BF16)
        h_ref[...] = h
        for n0 in range(0, n, COL_CHUNK):
            zc = _dot_nt(h, w_ref[n0:n0 + COL_CHUNK, :])
            if relu2:
                zc = jnp.square(jnp.maximum(zc, 0.0)).astype(BF16)
            o_ref[:, n0:n0 + COL_CHUNK] = zc

    return _call(
        body,
        grid=(s // tm,),
        in_specs=[pl.BlockSpec((tm, d), lambda i: (i, 0)),
                  pl.BlockSpec((1, d), lambda i: (0, 0)),
                  pl.BlockSpec((n, d), lambda i: (0, 0))],
        out_specs=[pl.BlockSpec((tm, n), lambda i: (i, 0)),
                   pl.BlockSpec((tm, d), lambda i: (i, 0))],
        out_shape=[jax.ShapeDtypeStruct((s, n), BF16 if relu2 else F32), jax.ShapeDtypeStruct((s, d), BF16)],
        operands=(x, g, wt), name=name, comm=comm)


def _mm_res(a, w2, x1, name, comm=None):
    s, f = a.shape
    d = w2.shape[1]
    tm = ROW_TILE

    def body(a_ref, w_ref, x_ref, o_ref):
        o_ref[...] = x_ref[...] + _dot_nn(a_ref[...], w_ref[...])

    return _call(
        body,
        grid=(s // tm,),
        in_specs=[pl.BlockSpec((tm, f), lambda i: (i, 0)),
                  pl.BlockSpec((f, d), lambda i: (0, 0)),
                  pl.BlockSpec((tm, d), lambda i: (i, 0))],
        out_specs=[pl.BlockSpec((tm, d), lambda i: (i, 0))],
        out_shape=[jax.ShapeDtypeStruct((s, d), F32)],
        operands=(a, w2, x1), name=name, comm=comm)


def _mix_out(ya, yb, yc, gg, wo, x0, name, comm=None):
    s = ya.shape[0]
    d = wo.shape[1]
    tm = ROW_TILE

    def body(ya_ref, yb_ref, yc_ref, g_ref, w_ref, x_ref, y_ref, o_ref):
        parts = []
        for ref in (ya_ref, yb_ref, yc_ref):
            t = ref[...]
            parts.append(t * _rms_scale(t))
        y = (jnp.concatenate(parts, axis=1) * g_ref[...]).astype(BF16)
        y_ref[...] = y
        o_ref[...] = x_ref[...] + _dot_nn(y, w_ref[...])

    return _call(
        body,
        grid=(s // tm,),
        in_specs=[pl.BlockSpec((tm, A_WIDTH), lambda i: (i, 0)),
                  pl.BlockSpec((tm, CONV_CH), lambda i: (i, 0)),
                  pl.BlockSpec((tm, C_WIDTH), lambda i: (i, 0)),
                  pl.BlockSpec((1, d), lambda i: (0, 0)),
                  pl.BlockSpec((d, d), lambda i: (0, 0)),
                  pl.BlockSpec((tm, d), lambda i: (i, 0))],
        out_specs=[pl.BlockSpec((tm, d), lambda i: (i, 0)),
                   pl.BlockSpec((tm, d), lambda i: (i, 0))],
        out_shape=[jax.ShapeDtypeStruct((s, d), BF16), jax.ShapeDtypeStruct((s, d), F32)],
        operands=(ya, yb, yc, gg, wo, x0), name=name, comm=comm)


def _loss_head(x, g, target, name):
    s, d = x.shape
    tm = ROW_TILE

    def body(x_ref, g_ref, t_ref, loss_ref, dx_ref, dxb_ref, dg_ref):
        @pl.when(pl.program_id(0) == 0)
        def _():
            loss_ref[...] = jnp.zeros_like(loss_ref)
            dg_ref[...] = jnp.zeros_like(dg_ref)

        xx = x_ref[...]
        r = _rms_scale(xx)
        n = xx * r
        gv = g_ref[...]
        err = n * gv - t_ref[...]
        per_tok = jnp.sum(err * err, axis=1, keepdims=True) * (1.0 / d)
        loss_ref[...] += 0.5 * jnp.sum(per_tok, axis=0, keepdims=True)
        dout = err * (1.0 / d)
        dg_ref[...] += jnp.sum(dout * n, axis=0, keepdims=True)
        dx = _rms_bwd(n, r, dout * gv)
        dx_ref[...] = dx
        dxb_ref[...] = dx.astype(BF16)

    return pl.pallas_call(
        body,
        grid=(s // tm,),
        in_specs=[pl.BlockSpec((tm, d), lambda i: (i, 0)),
                  pl.BlockSpec((1, d), lambda i: (0, 0)),
                  pl.BlockSpec((tm, d), lambda i: (i, 0))],
        out_specs=[pl.BlockSpec((8, LANES), lambda i: (0, 0)),
                   pl.BlockSpec((tm, d), lambda i: (i, 0)),
                   pl.BlockSpec((tm, d), lambda i: (i, 0)),
                   pl.BlockSpec((1, d), lambda i: (0, 0))],
        out_shape=[jax.ShapeDtypeStruct((8, LANES), F32), jax.ShapeDtypeStruct((s, d), F32),
                   jax.ShapeDtypeStruct((s, d), BF16), jax.ShapeDtypeStruct((1, d), F32)],
        compiler_params=_params("arbitrary"),
        name=name,
    )(x, g, target)


def _mlp_bwd_act(dxb, w2, a, name, comm=None):
    s, d = dxb.shape
    f = w2.shape[0]
    tm = ROW_TILE

    def body(dx_ref, w_ref, a_ref, du_ref):
        dx = dx_ref[...]
        for n0 in range(0, f, COL_CHUNK):
            da = _dot_nt(dx, w_ref[n0:n0 + COL_CHUNK, :])
            av = a_ref[:, n0:n0 + COL_CHUNK].astype(F32)
            rl = av * lax.rsqrt(jnp.maximum(av, F32_TINY))
            du_ref[:, n0:n0 + COL_CHUNK] = (da * (2.0 * rl)).astype(BF16)

    return _call(
        body,
        grid=(s // tm,),
        in_specs=[pl.BlockSpec((tm, d), lambda i: (i, 0)),
                  pl.BlockSpec((f, d), lambda i: (0, 0)),
                  pl.BlockSpec((tm, f), lambda i: (i, 0))],
        out_specs=[pl.BlockSpec((tm, f), lambda i: (i, 0))],
        out_shape=[jax.ShapeDtypeStruct((s, f), BF16)],
        operands=(dxb, w2, a), name=name, comm=comm)


def _mm_tn(a, b, name, comm=None):
    s, n = a.shape
    d = b.shape[1]
    tn = 512

    def body(a_ref, b_ref, o_ref, acc):
        for k0 in range(0, s, ROW_TILE):
            part = _dot_tn(a_ref[k0:k0 + ROW_TILE, :], b_ref[k0:k0 + ROW_TILE, :])
            if k0 == 0:
                acc[...] = part
            else:
                acc[...] += part
        o_ref[...] = acc[...].astype(BF16)

    return _call(
        body,
        grid=(n // tn,),
        in_specs=[pl.BlockSpec((s, tn), lambda j: (0, j)),
                  pl.BlockSpec((s, d), lambda j: (0, 0))],
        out_specs=[pl.BlockSpec((tn, d), lambda j: (j, 0))],
        out_shape=[jax.ShapeDtypeStruct((n, d), BF16)],
        operands=(a, b), name=name, scratch_shapes=[pltpu.VMEM((tn, d), F32)], comm=comm)


def _mm_nn_normbwd(dact, wt, x, dres, g, name, comm=None):
    s, kdim = dact.shape
    d = wt.shape[1]
    tm = ROW_TILE

    def body(a_ref, w_ref, x_ref, r_ref, g_ref, o_ref, ob_ref, dg_ref):
        @pl.when(pl.program_id(0) == 0)
        def _():
            dg_ref[...] = jnp.zeros_like(dg_ref)

        dh = _dot_nn(a_ref[...], w_ref[...])
        xx = x_ref[...]
        r = _rms_scale(xx)
        n = xx * r
        dg_ref[...] += jnp.sum(dh * n, axis=0, keepdims=True)
        dx = r_ref[...] + _rms_bwd(n, r, dh * g_ref[...])
        o_ref[...] = dx
        ob_ref[...] = dx.astype(BF16)

    return _call(
        body,
        grid=(s // tm,),
        in_specs=[pl.BlockSpec((tm, kdim), lambda i: (i, 0)),
                  pl.BlockSpec((kdim, d), lambda i: (0, 0)),
                  pl.BlockSpec((tm, d), lambda i: (i, 0)),
                  pl.BlockSpec((tm, d), lambda i: (i, 0)),
                  pl.BlockSpec((1, d), lambda i: (0, 0))],
        out_specs=[pl.BlockSpec((tm, d), lambda i: (i, 0)),
                   pl.BlockSpec((tm, d), lambda i: (i, 0)),
                   pl.BlockSpec((1, d), lambda i: (0, 0))],
        out_shape=[jax.ShapeDtypeStruct((s, d), F32), jax.ShapeDtypeStruct((s, d), BF16),
                   jax.ShapeDtypeStruct((1, d), F32)],
        operands=(dact, wt, x, dres, g), name=name, comm=comm)


def _mix_bwd(dx1, wo, ya, yb, yc, gg, name, comm=None):
    s, d = dx1.shape
    tm = ROW_TILE
    widths = (A_WIDTH, CONV_CH, C_WIDTH)

    def body(dx_ref, w_ref, ya_ref, yb_ref, yc_ref, g_ref, da_ref, db_ref, dc_ref, dg_ref):
        @pl.when(pl.program_id(0) == 0)
        def _():
            dg_ref[...] = jnp.zeros_like(dg_ref)

        dy = _dot_nt(dx_ref[...], w_ref[...])
        gv = g_ref[...]
        off = 0
        dgs = []
        for ref, out, w in zip((ya_ref, yb_ref, yc_ref), (da_ref, db_ref, dc_ref), widths):
            t = ref[...]
            r = _rms_scale(t)
            n = t * r
            dyg = dy[:, off:off + w]
            dgs.append(jnp.sum(dyg * n, axis=0, keepdims=True))
            out[...] = _rms_bwd(n, r, dyg * gv[:, off:off + w])
            off += w
        dg_ref[...] += jnp.concatenate(dgs, axis=1)

    return _call(
        body,
        grid=(s // tm,),
        in_specs=[pl.BlockSpec((tm, d), lambda i: (i, 0)),
                  pl.BlockSpec((d, d), lambda i: (0, 0)),
                  pl.BlockSpec((tm, A_WIDTH), lambda i: (i, 0)),
                  pl.BlockSpec((tm, CONV_CH), lambda i: (i, 0)),
                  pl.BlockSpec((tm, C_WIDTH), lambda i: (i, 0)),
                  pl.BlockSpec((1, d), lambda i: (0, 0))],
        out_specs=[pl.BlockSpec((tm, A_WIDTH), lambda i: (i, 0)),
                   pl.BlockSpec((tm, CONV_CH), lambda i: (i, 0)),
                   pl.BlockSpec((tm, C_WIDTH), lambda i: (i, 0)),
                   pl.BlockSpec((1, d), lambda i: (0, 0))],
        out_shape=[jax.ShapeDtypeStruct((s, A_WIDTH), F32), jax.ShapeDtypeStruct((s, CONV_CH), F32),
                   jax.ShapeDtypeStruct((s, C_WIDTH), F32), jax.ShapeDtypeStruct((1, d), F32)],
        operands=(dx1, wo, ya, yb, yc, gg), name=name, comm=comm)


CONV_CHUNK = 256
CONV_HALO = 8


def _conv_fwd(z, cw, name):
    s = z.shape[0]
    nch = s // CONV_CHUNK

    def body(gb_ref, gc_ref, xb_ref, w_ref, o_ref, us):
        us[pl.ds(0, CONV_HALO), :] = jnp.zeros((CONV_HALO, LANES), F32)
        us[pl.ds(CONV_HALO, s), :] = gc_ref[...] * xb_ref[...]
        w0, w1, w2 = w_ref[0:1, :], w_ref[1:2, :], w_ref[2:3, :]

        def chunk(c, carry):
            st = pl.multiple_of(c * CONV_CHUNK, CONV_CHUNK)
            ext = us[pl.ds(st, CONV_CHUNK + CONV_HALO), :]
            y = (w0 * ext[CONV_HALO - 2:CONV_HALO - 2 + CONV_CHUNK]
                 + w1 * ext[CONV_HALO - 1:CONV_HALO - 1 + CONV_CHUNK]
                 + w2 * ext[CONV_HALO:])
            o_ref[pl.ds(st, CONV_CHUNK), :] = gb_ref[pl.ds(st, CONV_CHUNK), :] * y
            return carry

        lax.fori_loop(0, nch, chunk, 0)

    col = lambda blk: pl.BlockSpec((s, LANES), lambda j, blk=blk: (0, blk + j))
    return pl.pallas_call(
        body,
        grid=(CONV_CH // LANES,),
        in_specs=[col(GB_BLK), col(GC_BLK), col(XB_BLK), pl.BlockSpec((3, LANES), lambda j: (0, j))],
        out_specs=pl.BlockSpec((s, LANES), lambda j: (0, j)),
        out_shape=jax.ShapeDtypeStruct((s, CONV_CH), F32),
        scratch_shapes=[pltpu.VMEM((s + CONV_HALO, LANES), F32)],
        compiler_params=_params("parallel"),
        name=name,
    )(z, z, z, cw)


def _conv_bwd(z, cw, dyb, dz, name):
    s = z.shape[0]
    nch = s // CONV_CHUNK
    ncol = CONV_CH // LANES

    def body(gb_ref, gc_ref, xb_ref, w_ref, dy_ref, dz_in, dz_ref, dw_ref, us, ds_, dgb_ref, dgc_ref, dxb_ref, sems):
        j = pl.program_id(0)

        def to_dz(staged, blk, k):
            cols = pl.ds(pl.multiple_of((blk + j) * LANES, LANES), LANES)
            return pltpu.make_async_copy(staged, dz_ref.at[:, cols], sems.at[k])

        copies = [to_dz(dgb_ref, GB_BLK, 0), to_dz(dgc_ref, GC_BLK, 1), to_dz(dxb_ref, XB_BLK, 2)]

        @pl.when(j > 0)
        def _():
            for cp in copies:
                cp.wait()

        us[pl.ds(0, CONV_HALO), :] = jnp.zeros((CONV_HALO, LANES), F32)
        us[pl.ds(CONV_HALO, s), :] = gc_ref[...] * xb_ref[...]
        ds_[pl.ds(s, CONV_HALO), :] = jnp.zeros((CONV_HALO, LANES), F32)
        ds_[pl.ds(0, s), :] = dy_ref[...] * gb_ref[...]
        w0, w1, w2 = w_ref[0:1, :], w_ref[1:2, :], w_ref[2:3, :]
        zero = jnp.zeros((1, LANES), F32)

        def chunk(c, carry):
            a0, a1, a2 = carry
            st = pl.multiple_of(c * CONV_CHUNK, CONV_CHUNK)
            rows = pl.ds(st, CONV_CHUNK)
            ext = us[pl.ds(st, CONV_CHUNK + CONV_HALO), :]
            um2 = ext[CONV_HALO - 2:CONV_HALO - 2 + CONV_CHUNK]
            um1 = ext[CONV_HALO - 1:CONV_HALO - 1 + CONV_CHUNK]
            u0 = ext[CONV_HALO:]
            dext = ds_[pl.ds(st, CONV_CHUNK + CONV_HALO), :]
            dc0 = dext[:CONV_CHUNK]
            du = w2 * dc0 + w1 * dext[1:1 + CONV_CHUNK] + w0 * dext[2:2 + CONV_CHUNK]
            yconv = w0 * um2 + w1 * um1 + w2 * u0
            dgb_ref[rows, :] = (dy_ref[rows, :] * yconv).astype(BF16)
            dgc_ref[rows, :] = (du * xb_ref[rows, :]).astype(BF16)
            dxb_ref[rows, :] = (du * gc_ref[rows, :]).astype(BF16)
            a0 = a0 + jnp.sum(dc0 * um2, axis=0, keepdims=True)
            a1 = a1 + jnp.sum(dc0 * um1, axis=0, keepdims=True)
            a2 = a2 + jnp.sum(dc0 * u0, axis=0, keepdims=True)
            return a0, a1, a2

        a0, a1, a2 = lax.fori_loop(0, nch, chunk, (zero, zero, zero))
        dw_ref[...] = jnp.concatenate([a0, a1, a2, jnp.zeros((5, LANES), F32)], axis=0)
        for cp in copies:
            cp.start()

        @pl.when(j == ncol - 1)
        def _():
            for cp in copies:
                cp.wait()

    col = lambda blk: pl.BlockSpec((s, LANES), lambda j, blk=blk: (0, blk + j))
    hbm = pl.BlockSpec(memory_space=pl.ANY)
    return pl.pallas_call(
        body,
        grid=(ncol,),
        in_specs=[col(GB_BLK), col(GC_BLK), col(XB_BLK), pl.BlockSpec((3, LANES), lambda j: (0, j)),
                  pl.BlockSpec((s, LANES), lambda j: (0, j)), hbm],
        out_specs=[hbm, pl.BlockSpec((8, LANES), lambda j: (0, j))],
        out_shape=[jax.ShapeDtypeStruct(dz.shape, dz.dtype), jax.ShapeDtypeStruct((8, CONV_CH), F32)],
        scratch_shapes=[pltpu.VMEM((s + CONV_HALO, LANES), F32), pltpu.VMEM((s + CONV_HALO, LANES), F32)]
        + [pltpu.VMEM((s, LANES), BF16)] * 3 + [pltpu.SemaphoreType.DMA((3,))],
        input_output_aliases={5: 0},
        compiler_params=_params("arbitrary"),
        name=name,
    )(z, z, z, cw, dyb, dz)


ATTN_ROWS = 512
ATTN_UNROLL = 8


def _band_rows(b, d, r):
    base = pl.multiple_of(b * (BLOCK * d), BLOCK)
    prev = jnp.maximum(base - BLOCK * d, 0)
    if d == 1:
        return pl.ds(base, BLOCK), pl.ds(pl.multiple_of(prev, BLOCK), BLOCK)
    return pl.ds(base + r, BLOCK, stride=d), pl.ds(prev + r, BLOCK, stride=d)


def _write_band_bias(bias_ref, max_dist):
    qi = lax.broadcasted_iota(jnp.int32, (BLOCK, 2 * BLOCK), 0)
    kj = lax.broadcasted_iota(jnp.int32, (BLOCK, 2 * BLOCK), 1)
    dist = BLOCK + qi - kj
    band = (dist >= 0) & (dist <= max_dist)
    bias_ref[0:BLOCK, :] = jnp.where(band, 0.0, -jnp.inf)
    bias_ref[BLOCK:2 * BLOCK, :] = jnp.where(band & (kj >= BLOCK), 0.0, -jnp.inf)


def _band_bias(bias_ref, b):
    bias = bias_ref[pl.ds(pl.multiple_of(jnp.where(b > 0, 0, BLOCK), BLOCK), BLOCK), :]
    return jnp.concatenate([bias, bias], axis=0)


def _kv_halves(pair):
    zero = jnp.zeros((1, LANES), jnp.int32)
    return zero + (pair >> 1), zero + ((pair + 1) >> 1)


def _stack_heads(t, head0, halves=None):
    top, bottom = jnp.where(head0, t, 0.0), jnp.where(head0, 0.0, t)
    if halves is not None:
        top = jnp.where(halves[0] == 1, pltpu.roll(top, HEAD_DIM, 1), top)
        bottom = jnp.where(halves[1] == 0, pltpu.roll(bottom, HEAD_DIM, 1), bottom)
    return jnp.concatenate([top, bottom], axis=0).astype(BF16)


def _unstack_heads(t, head0, halves=None):
    top, bottom = t[:BLOCK], t[BLOCK:]
    if halves is not None:
        top = jnp.where(halves[0] == 1, pltpu.roll(top, HEAD_DIM, 1), top)
        bottom = jnp.where(halves[1] == 0, pltpu.roll(bottom, HEAD_DIM, 1), bottom)
    return jnp.where(head0, top, bottom)


def _block_loops(s, patterns, unroll, one_block):
    for n, d in enumerate(patterns):
        nb = (s // BLOCK) // d
        ur = min(unroll, d)
        ub = unroll // ur
        for r0 in range(0, d, ur):
            def trip(i, carry, n=n, d=d, r0=r0, ur=ur, ub=ub):
                for u in range(ub):
                    for r in range(r0, r0 + ur):
                        one_block(i * ub + u, d, r, n == 0)
                return carry
            lax.fori_loop(0, nb // ub, trip, 0)


def _attn_fwd(z, m_init, l_init, q_blk, k_blk, v_blk, patterns, max_dist, gqa, name, comm=None):
    s = z.shape[0]
    npair = 3

    def body(q_ref, k_ref, v_ref, mi_ref, o_ref, lse0_ref, lse1_ref, bias_scr, m_scr, l_scr, *kv_scr):
        head0 = lax.broadcasted_iota(jnp.int32, (1, LANES), 1) < HEAD_DIM
        _write_band_bias(bias_scr, max_dist)
        ones = jnp.ones((2 * BLOCK, LANES), BF16)
        k_src, v_src = kv_scr if gqa else (k_ref, v_ref)
        if gqa:
            half = (lax.broadcasted_iota(jnp.int32, (1, LANES), 1) >= HEAD_DIM).astype(jnp.int32)
            swap = ((pl.program_id(0) + half) >> 1) != half

            def expand(c, carry):
                rows = pl.ds(pl.multiple_of(c * ATTN_ROWS, ATTN_ROWS), ATTN_ROWS)
                k_src[rows, :] = jnp.where(swap, pltpu.roll(k_ref[rows, :], HEAD_DIM, 1), k_ref[rows, :])
                v_src[rows, :] = jnp.where(swap, pltpu.roll(v_ref[rows, :], HEAD_DIM, 1), v_ref[rows, :])
                return carry

            lax.fori_loop(0, s // ATTN_ROWS, expand, 0)

        def one_block(b, d, r, first):
            rq, rp = _band_rows(b, d, r)
            q2 = _stack_heads(q_ref[rq, :] * SCALE, head0)
            k2 = jnp.concatenate([k_src[rp, :], k_src[rq, :]], axis=0).astype(BF16)
            v2 = jnp.concatenate([v_src[rp, :], v_src[rq, :]], axis=0).astype(BF16)
            sc = _dot_nt(q2, k2) + _band_bias(bias_scr, b)
            mb = jnp.max(sc, axis=1, keepdims=True)
            p = jnp.exp(sc - mb).astype(BF16)
            ob = _dot_nn(p, jnp.concatenate([v2, ones], axis=1))
            m_blk = _unstack_heads(jnp.broadcast_to(mb, (2 * BLOCK, LANES)), head0)
            l_blk = _unstack_heads(ob[:, LANES:], head0)
            o_blk = _unstack_heads(ob[:, :LANES], head0)
            if first and l_init == 0.0:
                m_new, l_new, o_new = m_blk, l_blk, o_blk
            else:
                if first:
                    m_old, l_old, o_old = jnp.broadcast_to(mi_ref[...], (BLOCK, LANES)), l_init, 0.0
                else:
                    m_old, l_old, o_old = m_scr[rq, :], l_scr[rq, :], o_ref[rq, :]
                m_new = jnp.maximum(m_old, m_blk)
                a_old = jnp.exp(m_old - m_new)
                a_blk = jnp.exp(m_blk - m_new)
                l_new = l_old * a_old + l_blk * a_blk
                o_new = o_old * a_old + o_blk * a_blk
            o_ref[rq, :], l_scr[rq, :], m_scr[rq, :] = o_new, l_new, m_new

        _block_loops(s, patterns, ATTN_UNROLL, one_block)

        def fin(c, carry):
            rows = pl.ds(pl.multiple_of(c * ATTN_ROWS, ATTN_ROWS), ATTN_ROWS)
            l = l_scr[rows, :]
            o_ref[rows, :] = o_ref[rows, :] / l
            lse = m_scr[rows, :] + jnp.log(l)
            swapped = pltpu.roll(lse, HEAD_DIM, 1)
            lse0_ref[rows, :] = jnp.where(head0, lse, swapped)
            lse1_ref[rows, :] = jnp.where(head0, swapped, lse)
            return carry

        lax.fori_loop(0, s // ATTN_ROWS, fin, 0)

    kv = (lambda blk: pl.BlockSpec((s, LANES), lambda j, blk=blk: (0, blk), pipeline_mode=pl.Buffered(1))) if gqa \
        else (lambda blk: pl.BlockSpec((s, LANES), lambda j, blk=blk: (0, blk + j)))
    own = pl.BlockSpec((s, LANES), lambda j: (0, j))
    return _call(
        body,
        grid=(npair,),
        in_specs=[pl.BlockSpec((s, LANES), lambda j: (0, q_blk + j)), kv(k_blk), kv(v_blk),
                  pl.BlockSpec((1, LANES), lambda j: (0, j))],
        out_specs=[own, own, own],
        out_shape=[jax.ShapeDtypeStruct((s, npair * LANES), F32)] * 3,
        operands=(z, z, z, m_init), name=name,
        scratch_shapes=[pltpu.VMEM((2 * BLOCK, 2 * BLOCK), F32)] + [pltpu.VMEM((s, LANES), F32)] * (4 if gqa else 2),
        comm=comm)


def _attn_bwd(z, do, o, lse, m_init, dz, q_blk, k_blk, v_blk, patterns, max_dist, gqa, name, comm=None):
    s = z.shape[0]
    npair = 3
    n_dz_in = 0 if dz is None else 1

    def body(q_ref, k_ref, v_ref, do_ref, o_ref, lse0_ref, lse1_ref, mi_ref, *rest):
        (dz_ref, dm_ref, dq_acc, dk_acc, dv_acc, dl0_scr, dl1_scr, bias_scr,
         dq_out, dk_out, dv_out, out_sems) = rest[n_dz_in:]
        pair = pl.program_id(0)
        head0 = lax.broadcasted_iota(jnp.int32, (1, LANES), 1) < HEAD_DIM
        halves = _kv_halves(pair) if gqa else None
        _write_band_bias(bias_scr, max_dist)

        def zero_kv():
            def f(c, carry):
                rows = pl.ds(pl.multiple_of(c * ATTN_ROWS, ATTN_ROWS), ATTN_ROWS)
                dk_acc[rows, :] = jnp.zeros((ATTN_ROWS, LANES), F32)
                dv_acc[rows, :] = jnp.zeros((ATTN_ROWS, LANES), F32)
                return carry
            lax.fori_loop(0, s // ATTN_ROWS, f, 0)

        if gqa:
            pl.when(pair == 0)(zero_kv)
        else:
            zero_kv()

        def prep(c, dm):
            rows = pl.ds(pl.multiple_of(c * ATTN_ROWS, ATTN_ROWS), ATTN_ROWS)
            dq_acc[rows, :] = jnp.zeros((ATTN_ROWS, LANES), F32)
            prod = do_ref[rows, :] * o_ref[rows, :]
            d0 = jnp.sum(jnp.where(head0, prod, 0.0), axis=1, keepdims=True)
            d1 = jnp.sum(jnp.where(head0, 0.0, prod), axis=1, keepdims=True)
            dl0_scr[rows, :] = jnp.broadcast_to(d0, (ATTN_ROWS, LANES))
            dl1_scr[rows, :] = jnp.broadcast_to(d1, (ATTN_ROWS, LANES))
            lse_own = jnp.where(head0, lse0_ref[rows, :], lse1_ref[rows, :])
            psink = jnp.exp(mi_ref[...] - lse_own)
            return dm - jnp.sum(psink * jnp.where(head0, d0, d1), axis=0, keepdims=True)

        dm_ref[...] = lax.fori_loop(0, s // ATTN_ROWS, prep, jnp.zeros((1, LANES), F32))

        def one_block(b, d, r, first):
            rq, rp = _band_rows(b, d, r)
            q2 = _stack_heads(q_ref[rq, :] * SCALE, head0, halves)
            do2 = _stack_heads(do_ref[rq, :], head0, halves)
            k2 = jnp.concatenate([k_ref[rp, :], k_ref[rq, :]], axis=0).astype(BF16)
            v2 = jnp.concatenate([v_ref[rp, :], v_ref[rq, :]], axis=0).astype(BF16)
            lse2 = jnp.concatenate([lse0_ref[rq, :], lse1_ref[rq, :]], axis=0)
            dl2 = jnp.concatenate([dl0_scr[rq, :], dl1_scr[rq, :]], axis=0)
            lse2 = jnp.concatenate([lse2, lse2], axis=1)
            dl2 = jnp.concatenate([dl2, dl2], axis=1)
            p = jnp.exp(_dot_nt(q2, k2) + _band_bias(bias_scr, b) - lse2)
            dp = _dot_nt(do2, v2)
            dsc = (p * (dp - dl2)).astype(BF16)
            dq2 = _unstack_heads(_dot_nn(dsc, k2), head0, halves)
            dk2 = _dot_tn(dsc, q2)
            dv2 = _dot_tn(p.astype(BF16), do2)
            dq_acc[rq, :] += dq2 * SCALE
            dk_acc[rp, :] += dk2[:BLOCK]
            dk_acc[rq, :] += dk2[BLOCK:]
            dv_acc[rp, :] += dv2[:BLOCK]
            dv_acc[rq, :] += dv2[BLOCK:]

        _block_loops(s, patterns, ATTN_UNROLL, one_block)

        def to_dz(staged, blk, k):
            cols = pl.ds(pl.multiple_of(blk * LANES, LANES), LANES)
            return pltpu.make_async_copy(staged, dz_ref.at[:, cols], out_sems.at[k])

        last_pair = pair == npair - 1
        q_copy = to_dz(dq_out, q_blk + pair, 0)
        kv_copies = [to_dz(dk_out, k_blk + (0 if gqa else pair), 1), to_dz(dv_out, v_blk + (0 if gqa else pair), 2)]

        @pl.when(pair > 0)
        def _():
            for cp in [q_copy] + ([] if gqa else kv_copies):
                cp.wait()

        def stage(acc, out):
            def f(c, carry):
                rows = pl.ds(pl.multiple_of(c * ATTN_ROWS, ATTN_ROWS), ATTN_ROWS)
                out[rows, :] = acc[rows, :].astype(BF16)
                return carry
            lax.fori_loop(0, s // ATTN_ROWS, f, 0)

        def stage_kv():
            stage(dk_acc, dk_out)
            stage(dv_acc, dv_out)
            for cp in kv_copies:
                cp.start()

        stage(dq_acc, dq_out)
        q_copy.start()
        if gqa:
            pl.when(last_pair)(stage_kv)
        else:
            stage_kv()

        @pl.when(last_pair)
        def _():
            for cp in [q_copy] + kv_copies:
                cp.wait()

    own = pl.BlockSpec((s, LANES), lambda j: (0, j))
    hbm = pl.BlockSpec(memory_space=pl.ANY)
    if gqa:
        kv = lambda blk: pl.BlockSpec((s, LANES), lambda j, blk=blk: (0, blk), pipeline_mode=pl.Buffered(1))
    else:
        kv = lambda blk: pl.BlockSpec((s, LANES), lambda j, blk=blk: (0, blk + j))
    in_specs = [pl.BlockSpec((s, LANES), lambda j: (0, q_blk + j)), kv(k_blk), kv(v_blk), own, own, own, own,
                pl.BlockSpec((1, LANES), lambda j: (0, j))]
    operands = (z, z, z, do, o, lse[0], lse[1], m_init)
    return _call(
        body,
        grid=(npair,),
        in_specs=in_specs + [hbm] * n_dz_in,
        out_specs=[hbm, pl.BlockSpec((1, LANES), lambda j: (0, j))],
        out_shape=[jax.ShapeDtypeStruct((s, IN_WIDTH), BF16), jax.ShapeDtypeStruct((1, npair * LANES), F32)],
        operands=operands + (() if dz is None else (dz,)), name=name,
        scratch_shapes=[pltpu.VMEM((s, LANES), F32)] * 5 + [pltpu.VMEM((2 * BLOCK, 2 * BLOCK), F32)]
        + [pltpu.VMEM((s, LANES), BF16)] * 3 + [pltpu.SemaphoreType.DMA((3,))],
        comm=comm, aliases={} if dz is None else {len(in_specs): 0})


def _adamw_math(w, g, m, v):
    m = ADAM_B1 * m + (1.0 - ADAM_B1) * g
    v = ADAM_B2 * v + (1.0 - ADAM_B2) * (g * g)
    m_hat = m / (1.0 - ADAM_B1 ** ADAM_STEP)
    v_hat = v / (1.0 - ADAM_B2 ** ADAM_STEP)
    delta = -ADAM_LR * (m_hat / (jnp.sqrt(v_hat) + ADAM_EPS) + ADAM_WD * w)
    return delta, m, v


def _adamw(w, g, m, v, name):
    rows, cols = w.shape
    tr = min(rows, 256)

    def body(w_ref, g_ref, m_ref, v_ref, d_ref, nm_ref, nv_ref):
        d_ref[...], nm_ref[...], nv_ref[...] = _adamw_math(w_ref[...], g_ref[...], m_ref[...], v_ref[...])

    spec = pl.BlockSpec((tr, cols), lambda i: (i, 0))
    return pl.pallas_call(
        body,
        grid=(rows // tr,),
        in_specs=[spec] * 4,
        out_specs=[spec] * 3,
        out_shape=[jax.ShapeDtypeStruct((rows, cols), F32)] * 3,
        compiler_params=_params("parallel"),
        name=name,
    )(w, g, m, v)


def _sum_adamw(parts, w, m, v, pos, transpose, name):
    assert len(parts) == DEPTH == 2
    (p0, r0), (p1, r1) = parts
    _, rows, cols = p0.shape
    tr = 256 if rows % 256 == 0 else rows
    nt = rows // tr

    def body(pos_ref, p0_ref, r0_ref, p1_ref, r1_ref, w_ref, m_ref, v_ref, g_ref, d_ref, nm_ref, nv_ref):
        def run(p_ref, r_ref):
            g = ((p_ref[...].astype(F32) + r_ref[0].astype(F32)) + r_ref[1].astype(F32)) + r_ref[2].astype(F32)
            if transpose:
                g = g.T
            g_ref[...] = g
            d_ref[...], nm_ref[...], nv_ref[...] = _adamw_math(w_ref[...], g, m_ref[...], v_ref[...])

        layer0 = pl.program_id(0) < nt
        pl.when(layer0)(lambda: run(p0_ref, r0_ref))
        pl.when(jnp.logical_not(layer0))(lambda: run(p1_ref, r1_ref))

    def tile0(i):
        return jnp.minimum(i, nt - 1)

    def tile1(i):
        return jnp.maximum(i - nt, 0)

    if transpose:
        w_spec = pl.BlockSpec((None, cols, tr), lambda i, q: (i // nt, 0, i % nt))
    else:
        w_spec = pl.BlockSpec((None, tr, cols), lambda i, q: (i // nt, i % nt, 0))
    return pl.pallas_call(
        body,
        grid_spec=pltpu.PrefetchScalarGridSpec(
            num_scalar_prefetch=1,
            grid=(DEPTH * nt,),
            in_specs=[pl.BlockSpec((None, tr, cols), lambda i, q: (q[0], tile0(i), 0)),
                      pl.BlockSpec((3, tr, cols), lambda i, q: (0, tile0(i), 0)),
                      pl.BlockSpec((None, tr, cols), lambda i, q: (q[0], tile1(i), 0)),
                      pl.BlockSpec((3, tr, cols), lambda i, q: (0, tile1(i), 0)),
                      w_spec, w_spec, w_spec],
            out_specs=[w_spec] * 4,
        ),
        out_shape=[jax.ShapeDtypeStruct(w.shape, F32)] * 4,
        compiler_params=_params("arbitrary"),
        name=name,
    )(pos, p0, r0, p1, r1, w, m, v)


def _small_sum_adamw(gathered, params, name):
    _, rows, cols = gathered.shape
    n = len(params)

    def body(ga_ref, *refs):
        ins, outs, (g_scr,) = refs[:3 * n], refs[3 * n:7 * n + 2], refs[7 * n + 2:]
        g = ga_ref[0]
        for i in range(1, N_DEV):
            g = g + ga_ref[i]
        g_scr[...] = g
        for k, (row0, w, _, _) in enumerate(params):
            w_ref, m_ref, v_ref = ins[3 * k:3 * k + 3]
            gk = g_scr[row0:row0 + w.shape[0], :]
            outs[4 * k][...] = gk
            outs[4 * k + 1][...], outs[4 * k + 2][...], outs[4 * k + 3][...] = _adamw_math(
                w_ref[...], gk, m_ref[...], v_ref[...])
        outs[4 * n][...] = g_scr[CONV_ROW:CONV_ROW + 8, :]
        outs[4 * n + 1][...] = g_scr[LOSS_ROW:LOSS_ROW + 1, :]

    out_shape = []
    for _, w, _, _ in params:
        out_shape += [jax.ShapeDtypeStruct(w.shape, F32)] * 4
    out_shape += [jax.ShapeDtypeStruct((8, cols), F32), jax.ShapeDtypeStruct((1, cols), F32)]
    res = pl.pallas_call(
        body,
        out_shape=out_shape,
        scratch_shapes=[pltpu.VMEM((rows, cols), F32)],
        name=name,
    )(gathered, *[t for _, w, m, v in params for t in (w, m, v)])
    return [res[4 * k:4 * k + 4] for k in range(n)], res[4 * n], res[4 * n + 1]


def _pair_sum(g4, r1, pos, name):
    _, _, rows, cols = g4.shape
    tr = min(rows, 512)

    def body(pos_ref, g_ref, r_ref, o_ref):
        o_ref[...] = (g_ref[...].astype(F32) + r_ref[...].astype(F32)).astype(BF16)

    return pl.pallas_call(
        body,
        grid_spec=pltpu.PrefetchScalarGridSpec(
            num_scalar_prefetch=1,
            grid=(4, rows // tr),
            in_specs=[pl.BlockSpec((None, None, tr, cols), lambda i, j, p: (i, p[1], j, 0)),
                      pl.BlockSpec((None, tr, cols), lambda i, j, p: (i, j, 0))],
            out_specs=pl.BlockSpec((None, tr, cols), lambda i, j, p: (i, j, 0)),
        ),
        out_shape=jax.ShapeDtypeStruct((4, rows, cols), BF16),
        compiler_params=_params("parallel", "parallel"),
        name=name,
    )(pos, g4, r1)


def _place():
    return lax.axis_index("x"), lax.axis_index("y"), lax.axis_index("c")


def _gather_comm(shards):
    na = len(shards)

    def plan(ins, outs, sems):
        send_sems, recv_sems, local_sems = sems
        x, y, c = _place()
        me, sibling = (x, y, c), (x, y, 1 - c)
        chips = [(1 - x, y), (x, 1 - y), (1 - x, 1 - y)]

        def rows(a, px, py, pc):
            m = ins[a].shape[0]
            return outs[a].at[pl.ds((4 * px + 2 * py + pc) * m, m), :]

        def copy(a, k, block, to, src=None):
            return pltpu.make_async_remote_copy(
                src_ref=rows(a, *block) if src is None else src, dst_ref=rows(a, *block),
                send_sem=send_sems.at[a, k], recv_sem=recv_sems.at[a, k], device_id=to, device_id_type=MESH)

        mine = [pltpu.make_async_copy(ins[a], rows(a, *me), local_sems.at[a]) for a in range(na)]
        first = []
        for a in range(na):
            first.append(copy(a, 0, me, sibling, src=ins[a]))
            first += [copy(a, 1 + j, me, (*chip, c), src=ins[a]) for j, chip in enumerate(chips)]
        return me, sibling, chips, c, copy, mine, first

    def start(ins, outs, sems):
        *_, mine, first = plan(ins, outs, sems)
        for cp in mine + first:
            cp.start()

    def finish(ins, outs, sems):
        me, sibling, chips, c, copy, mine, first = plan(ins, outs, sems)
        passed = []
        for j, chip in enumerate(chips):
            for a in range(na):
                copy(a, 1 + j, (*chip, c), me).wait_recv()
                cp = copy(a, 4 + j, (*chip, c), sibling)
                cp.start()
                passed.append(cp)
        for a in range(na):
            copy(a, 0, sibling, me).wait_recv()
            for j, chip in enumerate(chips):
                copy(a, 4 + j, (*chip, 1 - c), me).wait_recv()
        for cp in first + passed:
            cp.wait_send()
        for cp in mine:
            cp.wait()

    return _Comm(tuple(shards),
                 tuple(jax.ShapeDtypeStruct((N_DEV * t.shape[0], t.shape[1]), t.dtype) for t in shards),
                 (pltpu.SemaphoreType.DMA((na, 7)), pltpu.SemaphoreType.DMA((na, 7)), pltpu.SemaphoreType.DMA((na,))),
                 start, finish)


def _exchange_comm(arrays, out_shape, n_copies, copies_of):
    na = len(arrays)

    def every(ins, outs, sems):
        send_sems, recv_sems = sems
        return [cp for a in range(na) for cp in copies_of(ins, outs, a, send_sems, recv_sems)]

    def start(ins, outs, sems):
        for cp in every(ins, outs, sems):
            cp.start()

    def finish(ins, outs, sems):
        for cp in every(ins, outs, sems):
            cp.wait()

    return _Comm(tuple(arrays), tuple(out_shape),
                 (pltpu.SemaphoreType.DMA((na, n_copies)), pltpu.SemaphoreType.DMA((na, n_copies))), start, finish)


def _sibling_comm(grads):
    def copies_of(ins, outs, a, send_sems, recv_sems):
        x, y, c = _place()
        return [pltpu.make_async_remote_copy(
            src_ref=ins[a].at[chip, 1 - c], dst_ref=outs[a].at[chip],
            send_sem=send_sems.at[a, chip], recv_sem=recv_sems.at[a, chip],
            device_id=(x, y, 1 - c), device_id_type=MESH) for chip in range(4)]

    return _exchange_comm(grads, [jax.ShapeDtypeStruct((4,) + t.shape[2:], t.dtype) for t in grads], 4, copies_of)


def _chip_comm(partials):
    def copies_of(ins, outs, a, send_sems, recv_sems):
        x, y, c = _place()
        chips = [(1 - x, y), (x, 1 - y), (1 - x, 1 - y)]
        return [pltpu.make_async_remote_copy(
            src_ref=ins[a].at[2 * cx + cy], dst_ref=outs[a].at[k],
            send_sem=send_sems.at[a, k], recv_sem=recv_sems.at[a, k],
            device_id=(cx, cy, c), device_id_type=MESH) for k, (cx, cy) in enumerate(chips)]

    return _exchange_comm(partials, [jax.ShapeDtypeStruct((3,) + t.shape[1:], t.dtype) for t in partials], 3, copies_of)


def _pad_rows(t, rows):
    return jnp.pad(t, ((0, rows - t.shape[0]), (0, D_MODEL - t.shape[1])))


MIX_ROW, GROUP_ROW, MLP_ROW, FINAL_ROW, CONV_ROW, SINK_ROW = 0, 8, 16, 24, 32, 40
LOSS_ROW = FINAL_ROW + 1


def _pack_small(g_mix, g_group, g_mlp, g_final, conv, sinks, loss):
    final_and_loss = jnp.concatenate([g_final.reshape(1, D_MODEL), _pad_rows(loss, 1)], axis=0)
    return jnp.concatenate([
        _pad_rows(g_mix, 8), _pad_rows(g_group, 8), _pad_rows(g_mlp, 8), _pad_rows(final_and_loss, 8),
        _pad_rows(conv.reshape(DEPTH * 3, CONV_CH), 8), _pad_rows(sinks.reshape(1, DEPTH * 6), 8)], axis=0)


def kernel(x, w_in, conv_w, sinks, g_mix, g_group, w_o, g_mlp, w_ff_in, w_ff_out, g_final, loss_target, m_w_in, m_conv_w, m_sinks, m_g_mix, m_g_group, m_w_o, m_g_mlp, m_w_ff_in, m_w_ff_out, m_g_final, v_w_in, v_conv_w, v_sinks, v_g_mix, v_g_group, v_w_o, v_g_mlp, v_w_ff_in, v_w_ff_out, v_g_final):
    ax, ay, ac = _place()
    chip = 2 * ax + ay
    dev = 4 * ax + 2 * ay + ac
    pos = jnp.stack([chip, ac]).astype(jnp.int32)

    x0 = x.reshape(SEQ, D_MODEL)
    target = loss_target.reshape(SEQ, D_MODEL)

    shards = {}
    for l in range(DEPTH):
        shards[l, 0], shards[l, 1] = w_in[l].T.astype(BF16), w_o[l].astype(BF16)
        shards[l, 2], shards[l, 3] = w_ff_in[l].T.astype(BF16), w_ff_out[l].astype(BF16)
    conv_tile = jnp.pad(conv_w.reshape(DEPTH * 3, CONV_CH // N_DEV), ((0, 2), (0, LANES - CONV_CH // N_DEV)))
    wt_in0, conv_all = _comm_only(_gather_comm([shards[0, 0], conv_tile]), "gather_first")
    conv_full = conv_all.reshape(N_DEV, 8, LANES)[:, :DEPTH * 3, :CONV_CH // N_DEV]
    conv_full = conv_full.transpose(1, 0, 2).reshape(DEPTH, 3, CONV_CH)

    dx, parts, small = _step(x0, target, shards, wt_in0, conv_full, sinks, g_mix, g_group, g_mlp, g_final, pos)
    return _finish(dx, parts, small, pos, dev, w_in, conv_w, sinks, g_mix, g_group, w_o, g_mlp, w_ff_in, w_ff_out, g_final, m_w_in, m_conv_w, m_sinks, m_g_mix, m_g_group, m_w_o, m_g_mlp, m_w_ff_in, m_w_ff_out, m_g_final, v_w_in, v_conv_w, v_sinks, v_g_mix, v_g_group, v_w_o, v_g_mlp, v_w_ff_in, v_w_ff_out, v_g_final)


FWD_CARRY = {(0, "in_proj"): ((0, 1),), (0, "window"): ((1, 0),), (0, "dilated"): ((0, 2),),
             (0, "mix_out"): ((1, 1),), (0, "ff_in"): ((0, 3),), (0, "ff_out"): ((1, 3),),
             (1, "dilated"): ((1, 2),)}


def _step(x0, target, shards, wt_in0, conv_full, sinks, g_mix, g_group, g_mlp, g_final, pos):
    sink_lanes = jnp.repeat(sinks.reshape(DEPTH, 6), HEAD_DIM, axis=1)
    no_sink = jnp.full((1, A_WIDTH), NEG_BIG, F32)
    full = {(0, 0): wt_in0}

    def gather(stage, l):
        keys = FWD_CARRY.get((l, stage), ())
        return keys, (_gather_comm([shards[k] for k in keys]) if keys else None)

    def landed(keys, got):
        full.update(zip(keys, got))

    saved = []
    xc = x0
    for l in range(DEPTH):
        keys, comm = gather("in_proj", l)
        (z, h), got = _norm_mm(xc, g_mix[l:l + 1], full[l, 0], False, f"in_proj_{l}", comm)
        landed(keys, got)
        sink_l = sink_lanes[l:l + 1]
        keys, comm = gather("window", l)
        (yc, *lse_c), got = _attn_fwd(z, sink_l, 1.0, QC_BLK, KC_BLK, VC_BLK, (1,), C_MAX_DIST, True,
                                     f"window_attn_{l}", comm)
        landed(keys, got)
        yb = _conv_fwd(z, conv_full[l], f"conv_{l}")
        keys, comm = gather("dilated", l)
        (ya, *lse_a), got = _attn_fwd(z, no_sink, 0.0, QA_BLK, KA_BLK, VA_BLK, DILATED_PATTERNS, A_MAX_DIST, False,
                                     f"dilated_attn_{l}", comm)
        landed(keys, got)
        keys, comm = gather("mix_out", l)
        (y, x1), got = _mix_out(ya, yb, yc, g_group[l:l + 1], full[l, 1], xc, f"mix_out_{l}", comm)
        landed(keys, got)
        keys, comm = gather("ff_in", l)
        (a, h2), got = _norm_mm(x1, g_mlp[l:l + 1], full[l, 2], True, f"ff_in_{l}", comm)
        landed(keys, got)
        keys, comm = gather("ff_out", l)
        (x2,), got = _mm_res(a, full[l, 3], x1, f"ff_out_{l}", comm)
        landed(keys, got)
        saved.append((xc, z, h, ya, lse_a, yb, yc, lse_c, sink_l, y, x1, a, h2))
        xc = x2

    loss_slab, dx, dxb, dg_final = _loss_head(xc, g_final.reshape(1, D_MODEL), target, "loss_head")

    def by_owner(t):
        return t.reshape(4, 2, t.shape[0] // N_DEV, D_MODEL)

    def pair(key, g, r1):
        return _pair_sum(g, r1, pos, f"grad_pair_sum_{key[0]}_{key[1]}")

    partial, r2 = {}, {}
    dg_mix, dg_group, dg_mlp, dconv, dsinks = [None] * DEPTH, [None] * DEPTH, [None] * DEPTH, [None] * DEPTH, [None] * DEPTH
    for l in reversed(range(DEPTH)):
        xin, z, h, ya, lse_a, yb, yc, lse_c, sink_l, y, x1, a, h2 = saved[l]
        late = [(l + 1, 1), (l + 1, 0)] if l + 1 < DEPTH else []
        (du,), got = _mlp_bwd_act(dxb, full[l, 3], a, f"ff_out_bwd_{l}",
                                  _chip_comm([partial[k] for k in late]) if late else None)
        r2.update(zip(late, got))
        (g3,), _ = _mm_tn(a, dxb, f"grad_w_ff_out_{l}")
        (g2,), _ = _mm_tn(du, h2, f"grad_w_ff_in_{l}")
        g3, g2 = by_owner(g3), by_owner(g2)
        (dx1, dx1b, dg_mlp[l]), got = _mm_nn_normbwd(du, full[l, 2], x1, dx, g_mlp[l:l + 1], f"ff_in_bwd_{l}",
                                                    _sibling_comm([g3, g2]))
        partial[l, 3], partial[l, 2] = pair((l, 3), g3, got[0]), pair((l, 2), g2, got[1])
        (g1,), _ = _mm_tn(y, dx1b, f"grad_w_o_{l}")
        g1 = by_owner(g1)
        (dya, dyb, dyc, dg_group[l]), got = _mix_bwd(dx1b, full[l, 1], ya, yb, yc, g_group[l:l + 1],
                                                     f"mix_out_bwd_{l}", _sibling_comm([g1]) if l == 0 else None)
        if l == 0:
            partial[l, 1] = pair((l, 1), g1, got[0])
        early = [(l, 3), (l, 2)] + ([(l, 1)] if l == 0 else [])
        (dz, _), got = _attn_bwd(z, dya, ya, lse_a, no_sink, None, QA_BLK, KA_BLK, VA_BLK, DILATED_PATTERNS,
                                 A_MAX_DIST, False, f"dilated_attn_bwd_{l}", _chip_comm([partial[k] for k in early]))
        r2.update(zip(early, got))
        dz, dcw = _conv_bwd(z, conv_full[l], dyb, dz, f"conv_bwd_{l}")
        (dz, dsink), _ = _attn_bwd(z, dyc, yc, lse_c, sink_l, dz, QC_BLK, KC_BLK, VC_BLK, (1,), C_MAX_DIST,
                                   True, f"window_attn_bwd_{l}")
        (g0,), _ = _mm_tn(dz, h, f"grad_w_in_{l}")
        g0 = by_owner(g0)
        if l > 0:
            (dx, dxb, dg_mix[l]), got = _mm_nn_normbwd(dz, full[l, 0], xin, dx1, g_mix[l:l + 1], f"in_proj_bwd_{l}",
                                                      _sibling_comm([g1, g0]))
            partial[l, 1], partial[l, 0] = pair((l, 1), g1, got[0]), pair((l, 0), g0, got[1])
        else:
            (r1,) = _comm_only(_sibling_comm([g0]), "grad_sibling_exchange_last")
            partial[l, 0] = pair((l, 0), g0, r1)
            (dx, dxb, dg_mix[l]), got = _mm_nn_normbwd(dz, full[l, 0], xin, dx1, g_mix[l:l + 1], f"in_proj_bwd_{l}",
                                                      _chip_comm([partial[l, 0]]))
            r2[l, 0] = got[0]
        dconv[l] = dcw[:3]
        dsinks[l] = dsink[0, ::HEAD_DIM]
    parts = {key: (partial[key], r2[key]) for key in partial}
    small = _pack_small(jnp.concatenate(dg_mix), jnp.concatenate(dg_group), jnp.concatenate(dg_mlp),
                        dg_final, jnp.stack(dconv), jnp.stack(dsinks), loss_slab[0:1])
    return dx, parts, small


def _finish(dx, parts, small, pos, dev, w_in, conv_w, sinks, g_mix, g_group, w_o, g_mlp, w_ff_in, w_ff_out, g_final, m_w_in, m_conv_w, m_sinks, m_g_mix, m_g_group, m_w_o, m_g_mlp, m_w_ff_in, m_w_ff_out, m_g_final, v_w_in, v_conv_w, v_sinks, v_g_mix, v_g_group, v_w_o, v_g_mlp, v_w_ff_in, v_w_ff_out, v_g_final):
    grad_x = dx.reshape(1, SEQ, D_MODEL)

    (small_all,) = _comm_only(_gather_comm([small]), "gather_small_grads")
    row = lambda t: t.reshape(1, D_MODEL)
    sink_row = lambda t: _pad_rows(t.reshape(1, DEPTH * 6), 1)
    params = [(MIX_ROW, g_mix, m_g_mix, v_g_mix), (GROUP_ROW, g_group, m_g_group, v_g_group),
              (MLP_ROW, g_mlp, m_g_mlp, v_g_mlp), (FINAL_ROW, row(g_final), row(m_g_final), row(v_g_final)),
              (SINK_ROW, sink_row(sinks), sink_row(m_sinks), sink_row(v_sinks))]
    updated, conv_rows, loss_row = _small_sum_adamw(small_all.reshape(N_DEV, SMALL_ROWS, D_MODEL), params, "small_adamw")
    loss = loss_row[0, 0]
    (grad_g_mix, delta_g_mix, new_m_g_mix, new_v_g_mix), (grad_g_group, delta_g_group, new_m_g_group, new_v_g_group), \
        (grad_g_mlp, delta_g_mlp, new_m_g_mlp, new_v_g_mlp), final4, sinks4 = updated
    grad_g_final, delta_g_final, new_m_g_final, new_v_g_final = [t.reshape(D_MODEL) for t in final4]
    grad_sinks, delta_sinks, new_m_sinks, new_v_sinks = [t[0, :DEPTH * 6].reshape(DEPTH, 2, 3) for t in sinks4]
    conv_grad_full = conv_rows[:DEPTH * 3, :CONV_CH].reshape(DEPTH, 3, CONV_CH)
    cs = CONV_CH // N_DEV
    grad_conv_w = lax.dynamic_slice_in_dim(conv_grad_full, dev * cs, cs, axis=2)

    def tile_of(t):
        return jnp.pad(t.reshape(1, DEPTH * 3 * cs), ((0, 7), (0, 256 - DEPTH * 3 * cs)))

    cd, cm, cv = _adamw(tile_of(conv_w), tile_of(grad_conv_w), tile_of(m_conv_w), tile_of(v_conv_w), "conv_adamw")
    untile = lambda t: t[0, :DEPTH * 3 * cs].reshape(DEPTH, 3, cs)
    delta_conv_w, new_m_conv_w, new_v_conv_w = untile(cd), untile(cm), untile(cv)

    def big(kind, w, m, v, transpose, name):
        return _sum_adamw([parts[l, kind] for l in range(DEPTH)], w, m, v, pos, transpose, name)

    grad_w_in, delta_w_in, new_m_w_in, new_v_w_in = big(0, w_in, m_w_in, v_w_in, True, "adamw_w_in")
    grad_w_o, delta_w_o, new_m_w_o, new_v_w_o = big(1, w_o, m_w_o, v_w_o, False, "adamw_w_o")
    grad_w_ff_in, delta_w_ff_in, new_m_w_ff_in, new_v_w_ff_in = big(2, w_ff_in, m_w_ff_in, v_w_ff_in, True, "adamw_w_ff_in")
    grad_w_ff_out, delta_w_ff_out, new_m_w_ff_out, new_v_w_ff_out = big(3, w_ff_out, m_w_ff_out, v_w_ff_out, False,
                                                                         "adamw_w_ff_out")

    return (loss, grad_x, grad_w_in, grad_conv_w, grad_sinks, grad_g_mix, grad_g_group, grad_w_o, grad_g_mlp,
            grad_w_ff_in, grad_w_ff_out, grad_g_final,
            delta_w_in, delta_conv_w, delta_sinks, delta_g_mix, delta_g_group, delta_w_o, delta_g_mlp,
            delta_w_ff_in, delta_w_ff_out, delta_g_final,
            new_m_w_in, new_m_conv_w, new_m_sinks, new_m_g_mix, new_m_g_group, new_m_w_o, new_m_g_mlp,
            new_m_w_ff_in, new_m_w_ff_out, new_m_g_final,
            new_v_w_in, new_v_conv_w, new_v_sinks, new_v_g_mix, new_v_g_group, new_v_w_o, new_v_g_mlp,
            new_v_w_ff_in, new_v_w_ff_out, new_v_g_final)
```

```python
from typing import Callable, NamedTuple

import jax
import jax.numpy as jnp
from jax import lax
from jax.experimental import pallas as pl
from jax.experimental.pallas import tpu as pltpu

F32 = jnp.float32
BF16 = jnp.bfloat16
MESH = pl.DeviceIdType.MESH

N_DEV = 8
SEQ = 4096
D_MODEL = 1024
DEPTH = 2
HEAD_DIM = 64
LANES = 128
A_WIDTH = 384
CONV_CH = 256
C_WIDTH = 384
KV_WIDTH = 128
IN_WIDTH = 2560
D_FF = 4096
BLOCK = 128
DILATED_PATTERNS = (1, 4, 16)
A_MAX_DIST = 128
C_MAX_DIST = 127
EPS = 1e-6
SCALE = HEAD_DIM ** -0.5
NEG_BIG = -1e30
F32_TINY = 1.1754944e-38

QA_BLK, KA_BLK, VA_BLK = 0, 3, 6
GB_BLK, GC_BLK, XB_BLK = 9, 11, 13
QC_BLK, KC_BLK, VC_BLK = 15, 18, 19

ADAM_LR = 0.001
ADAM_B1 = 0.9
ADAM_B2 = 0.999
ADAM_EPS = 1e-08
ADAM_WD = 0.01
ADAM_STEP = 10

VMEM_LIMIT = 56 * 1024 * 1024
TILE_BUDGET = 46 * 1024 * 1024
ROW_TILE = 1024
COL_CHUNK = 512
SMALL_ROWS = 48


def _dot_nn(a, b):
    return lax.dot_general(a, b, (((1,), (0,)), ((), ())), preferred_element_type=F32)


def _dot_nt(a, b):
    return lax.dot_general(a, b, (((1,), (1,)), ((), ())), preferred_element_type=F32)


def _dot_tn(a, b):
    return lax.dot_general(a, b, (((0,), (0,)), ((), ())), preferred_element_type=F32)


def _params(*sem):
    return pltpu.CompilerParams(dimension_semantics=sem, vmem_limit_bytes=VMEM_LIMIT)


def _resident(shape):
    return pl.BlockSpec(shape, lambda i: (0,) * len(shape), pipeline_mode=pl.Buffered(1))


def _row_tile(row_bytes, resident_bytes):
    for tm in (ROW_TILE, ROW_TILE // 2):
        if 2 * tm * row_bytes + resident_bytes <= TILE_BUDGET:
            return tm
    return ROW_TILE // 4


def _rms_scale(t):
    return lax.rsqrt(jnp.mean(t * t, axis=-1, keepdims=True) + EPS)


def _rms_bwd(n, r, dn):
    return r * (dn - n * jnp.mean(dn * n, axis=-1, keepdims=True))


class _Comm(NamedTuple):
    arrays: tuple
    out_shape: tuple
    sems: tuple
    start: Callable
    finish: Callable


def _call(body, grid, in_specs, out_specs, out_shape, operands, name, scratch_shapes=(), comm=None, aliases=None):
    n_in, n_out, n_scr = len(in_specs), len(out_shape), len(scratch_shapes)
    aliases = dict(aliases or {})
    if comm is None:
        res = pl.pallas_call(body, grid=grid, in_specs=list(in_specs), out_specs=list(out_specs),
                             out_shape=list(out_shape), scratch_shapes=list(scratch_shapes),
                             input_output_aliases=aliases,
                             compiler_params=_params("arbitrary"), name=name)(*operands)
        return list(res), []
    c_in, c_out = len(comm.arrays), len(comm.out_shape)
    hbm = pl.BlockSpec(memory_space=pl.ANY)
    last = grid[0] - 1

    def carried(*refs):
        ins, cins = refs[:n_in], refs[n_in:n_in + c_in]
        o0 = n_in + c_in
        outs, couts = refs[o0:o0 + n_out], refs[o0 + n_out:o0 + n_out + c_out]
        s0 = o0 + n_out + c_out
        scr, sems = refs[s0:s0 + n_scr], refs[s0 + n_scr:]
        pl.when(pl.program_id(0) == 0)(lambda: comm.start(cins, couts, sems))
        body(*ins, *outs, *scr)
        pl.when(pl.program_id(0) == last)(lambda: comm.finish(cins, couts, sems))

    res = pl.pallas_call(carried, grid=grid, in_specs=list(in_specs) + [hbm] * c_in,
                         out_specs=list(out_specs) + [hbm] * c_out, out_shape=list(out_shape) + list(comm.out_shape),
                         scratch_shapes=list(scratch_shapes) + list(comm.sems), input_output_aliases=aliases,
                         compiler_params=_params("arbitrary"), name=name)(*operands, *comm.arrays)
    return list(res[:n_out]), list(res[n_out:])


def _comm_only(comm, name):
    hbm = pl.BlockSpec(memory_space=pl.ANY)
    c_in, c_out = len(comm.arrays), len(comm.out_shape)

    def body(*refs):
        ins, outs, sems = refs[:c_in], refs[c_in:c_in + c_out], refs[c_in + c_out:]
        comm.start(ins, outs, sems)
        comm.finish(ins, outs, sems)

    return pl.pallas_call(body, in_specs=[hbm] * c_in, out_specs=[hbm] * c_out, out_shape=list(comm.out_shape),
                          scratch_shapes=list(comm.sems), name=name)(*comm.arrays)


def _norm_mm(x, g, wt, relu2, name, comm=None):
    s, d = x.shape
    n = wt.shape[0]
    tm = _row_tile(4 * d + (2 if relu2 else 4) * n + 2 * d, 2 * n * d)

    def body(x_ref, g_ref, w_ref, o_ref, h_ref):
        xx = x_ref[...]
        h = ((xx * _rms_scale(xx)) * g_ref[...]).astype(BF16)
        h_ref[...] = h
        for n0 in range(0, n, COL_CHUNK):
            zc = _dot_nt(h, w_ref[n0:n0 + COL_CHUNK, :])
            if relu2:
                zc = jnp.square(jnp.maximum(zc, 0.0)).astype(BF16)
            o_ref[:, n0:n0 + COL_CHUNK] = zc

    return _call(
        body,
        grid=(s // tm,),
        in_specs=[pl.BlockSpec((tm, d), lambda i: (i, 0)),
                  pl.BlockSpec((1, d), lambda i: (0, 0)),
                  _resident((n, d))],
        out_specs=[pl.BlockSpec((tm, n), lambda i: (i, 0)),
                   pl.BlockSpec((tm, d), lambda i: (i, 0))],
        out_shape=[jax.ShapeDtypeStruct((s, n), BF16 if relu2 else F32), jax.ShapeDtypeStruct((s, d), BF16)],
        operands=(x, g, wt), name=name, comm=comm)


def _mm_res(a, w2, x1, name, comm=None):
    s, f = a.shape
    d = w2.shape[1]
    tm = _row_tile(2 * f + 4 * d + 4 * d, 2 * f * d)

    def body(a_ref, w_ref, x_ref, o_ref):
        o_ref[...] = x_ref[...] + _dot_nn(a_ref[...], w_ref[...])

    return _call(
        body,
        grid=(s // tm,),
        in_specs=[pl.BlockSpec((tm, f), lambda i: (i, 0)),
                  _resident((f, d)),
                  pl.BlockSpec((tm, d), lambda i: (i, 0))],
        out_specs=[pl.BlockSpec((tm, d), lambda i: (i, 0))],
        out_shape=[jax.ShapeDtypeStruct((s, d), F32)],
        operands=(a, w2, x1), name=name, comm=comm)


def _mix_out(ya, yb, yc, gg, wo, x0, name, comm=None):
    s = ya.shape[0]
    d = wo.shape[1]
    tm = ROW_TILE

    def body(ya_ref, yb_ref, yc_ref, g_ref, w_ref, x_ref, y_ref, o_ref):
        parts = []
        for ref in (ya_ref, yb_ref, yc_ref):
            t = ref[...]
            parts.append(t * _rms_scale(t))
        y = (jnp.concatenate(parts, axis=1) * g_ref[...]).astype(BF16)
        y_ref[...] = y
        o_ref[...] = x_ref[...] + _dot_nn(y, w_ref[...])

    return _call(
        body,
        grid=(s // tm,),
        in_specs=[pl.BlockSpec((tm, A_WIDTH), lambda i: (i, 0)),
                  pl.BlockSpec((tm, CONV_CH), lambda i: (i, 0)),
                  pl.BlockSpec((tm, C_WIDTH), lambda i: (i, 0)),
                  pl.BlockSpec((1, d), lambda i: (0, 0)),
                  _resident((d, d)),
                  pl.BlockSpec((tm, d), lambda i: (i, 0))],
        out_specs=[pl.BlockSpec((tm, d), lambda i: (i, 0)),
                   pl.BlockSpec((tm, d), lambda i: (i, 0))],
        out_shape=[jax.ShapeDtypeStruct((s, d), BF16), jax.ShapeDtypeStruct((s, d), F32)],
        operands=(ya, yb, yc, gg, wo, x0), name=name, comm=comm)


def _loss_head(x, g, target, name):
    s, d = x.shape
    tm = ROW_TILE

    def body(x_ref, g_ref, t_ref, loss_ref, dx_ref, dxb_ref, dg_ref):
        @pl.when(pl.program_id(0) == 0)
        def _():
            loss_ref[...] = jnp.zeros_like(loss_ref)
            dg_ref[...] = jnp.zeros_like(dg_ref)

        xx = x_ref[...]
        r = _rms_scale(xx)
        n = xx * r
        gv = g_ref[...]
        err = n * gv - t_ref[...]
        per_tok = jnp.sum(err * err, axis=1, keepdims=True) * (1.0 / d)
        loss_ref[...] += 0.5 * jnp.sum(per_tok, axis=0, keepdims=True)
        dout = err * (1.0 / d)
        dg_ref[...] += jnp.sum(dout * n, axis=0, keepdims=True)
        dx = _rms_bwd(n, r, dout * gv)
        dx_ref[...] = dx
        dxb_ref[...] = dx.astype(BF16)

    return pl.pallas_call(
        body,
        grid=(s // tm,),
        in_specs=[pl.BlockSpec((tm, d), lambda i: (i, 0)),
                  pl.BlockSpec((1, d), lambda i: (0, 0)),
                  pl.BlockSpec((tm, d), lambda i: (i, 0))],
        out_specs=[pl.BlockSpec((8, LANES), lambda i: (0, 0)),
                   pl.BlockSpec((tm, d), lambda i: (i, 0)),
                   pl.BlockSpec((tm, d), lambda i: (i, 0)),
                   pl.BlockSpec((1, d), lambda i: (0, 0))],
        out_shape=[jax.ShapeDtypeStruct((8, LANES), F32), jax.ShapeDtypeStruct((s, d), F32),
                   jax.ShapeDtypeStruct((s, d), BF16), jax.ShapeDtypeStruct((1, d), F32)],
        compiler_params=_params("arbitrary"),
        name=name,
    )(x, g, target)


def _mlp_bwd_act(dxb, w2, a, name, comm=None):
    s, d = dxb.shape
    f = w2.shape[0]
    tm = _row_tile(2 * d + 2 * f + 2 * f, 2 * f * d)

    def body(dx_ref, w_ref, a_ref, du_ref):
        dx = dx_ref[...]
        for n0 in range(0, f, COL_CHUNK):
            da = _dot_nt(dx, w_ref[n0:n0 + COL_CHUNK, :])
            av = a_ref[:, n0:n0 + COL_CHUNK].astype(F32)
            rl = av * lax.rsqrt(jnp.maximum(av, F32_TINY))
            du_ref[:, n0:n0 + COL_CHUNK] = (da * (2.0 * rl)).astype(BF16)

    return _call(
        body,
        grid=(s // tm,),
        in_specs=[pl.BlockSpec((tm, d), lambda i: (i, 0)),
                  _resident((f, d)),
                  pl.BlockSpec((tm, f), lambda i: (i, 0))],
        out_specs=[pl.BlockSpec((tm, f), lambda i: (i, 0))],
        out_shape=[jax.ShapeDtypeStruct((s, f), BF16)],
        operands=(dxb, w2, a), name=name, comm=comm)


def _mm_tn(a, b, name, comm=None):
    s, n = a.shape
    d = b.shape[1]
    tn = 512

    def body(a_ref, b_ref, o_ref, acc):
        for k0 in range(0, s, ROW_TILE):
            part = _dot_tn(a_ref[k0:k0 + ROW_TILE, :], b_ref[k0:k0 + ROW_TILE, :])
            if k0 == 0:
                acc[...] = part
            else:
                acc[...] += part
        o_ref[...] = acc[...].astype(BF16)

    return _call(
        body,
        grid=(n // tn,),
        in_specs=[pl.BlockSpec((s, tn), lambda j: (0, j)),
                  _resident((s, d))],
        out_specs=[pl.BlockSpec((tn, d), lambda j: (j, 0))],
        out_shape=[jax.ShapeDtypeStruct((n, d), BF16)],
        operands=(a, b), name=name, scratch_shapes=[pltpu.VMEM((tn, d), F32)], comm=comm)


def _mm_nn_normbwd(dact, wt, x, dres, g, name, comm=None):
    s, kdim = dact.shape
    d = wt.shape[1]
    tm = _row_tile(2 * kdim + 4 * d + 4 * d + 4 * d + 2 * d, 2 * kdim * d)

    def body(a_ref, w_ref, x_ref, r_ref, g_ref, o_ref, ob_ref, dg_ref):
        @pl.when(pl.program_id(0) == 0)
        def _():
            dg_ref[...] = jnp.zeros_like(dg_ref)

        dh = _dot_nn(a_ref[...], w_ref[...])
        xx = x_ref[...]
        r = _rms_scale(xx)
        n = xx * r
        dg_ref[...] += jnp.sum(dh * n, axis=0, keepdims=True)
        dx = r_ref[...] + _rms_bwd(n, r, dh * g_ref[...])
        o_ref[...] = dx
        ob_ref[...] = dx.astype(BF16)

    return _call(
        body,
        grid=(s // tm,),
        in_specs=[pl.BlockSpec((tm, kdim), lambda i: (i, 0)),
                  _resident((kdim, d)),
                  pl.BlockSpec((tm, d), lambda i: (i, 0)),
                  pl.BlockSpec((tm, d), lambda i: (i, 0)),
                  pl.BlockSpec((1, d), lambda i: (0, 0))],
        out_specs=[pl.BlockSpec((tm, d), lambda i: (i, 0)),
                   pl.BlockSpec((tm, d), lambda i: (i, 0)),
                   pl.BlockSpec((1, d), lambda i: (0, 0))],
        out_shape=[jax.ShapeDtypeStruct((s, d), F32), jax.ShapeDtypeStruct((s, d), BF16),
                   jax.ShapeDtypeStruct((1, d), F32)],
        operands=(dact, wt, x, dres, g), name=name, comm=comm)


def _mix_bwd(dx1, wo, ya, yb, yc, gg, name, comm=None):
    s, d = dx1.shape
    tm = ROW_TILE
    widths = (A_WIDTH, CONV_CH, C_WIDTH)

    def body(dx_ref, w_ref, ya_ref, yb_ref, yc_ref, g_ref, da_ref, db_ref, dc_ref, dg_ref):
        @pl.when(pl.program_id(0) == 0)
        def _():
            dg_ref[...] = jnp.zeros_like(dg_ref)

        dy = _dot_nt(dx_ref[...], w_ref[...])
        gv = g_ref[...]
        off = 0
        dgs = []
        for ref, out, w in zip((ya_ref, yb_ref, yc_ref), (da_ref, db_ref, dc_ref), widths):
            t = ref[...]
            r = _rms_scale(t)
            n = t * r
            dyg = dy[:, off:off + w]
            dgs.append(jnp.sum(dyg * n, axis=0, keepdims=True))
            out[...] = _rms_bwd(n, r, dyg * gv[:, off:off + w])
            off += w
        dg_ref[...] += jnp.concatenate(dgs, axis=1)

    return _call(
        body,
        grid=(s // tm,),
        in_specs=[pl.BlockSpec((tm, d), lambda i: (i, 0)),
                  _resident((d, d)),
                  pl.BlockSpec((tm, A_WIDTH), lambda i: (i, 0)),
                  pl.BlockSpec((tm, CONV_CH), lambda i: (i, 0)),
                  pl.BlockSpec((tm, C_WIDTH), lambda i: (i, 0)),
                  pl.BlockSpec((1, d), lambda i: (0, 0))],
        out_specs=[pl.BlockSpec((tm, A_WIDTH), lambda i: (i, 0)),
                   pl.BlockSpec((tm, CONV_CH), lambda i: (i, 0)),
                   pl.BlockSpec((tm, C_WIDTH), lambda i: (i, 0)),
                   pl.BlockSpec((1, d), lambda i: (0, 0))],
        out_shape=[jax.ShapeDtypeStruct((s, A_WIDTH), F32), jax.ShapeDtypeStruct((s, CONV_CH), F32),
                   jax.ShapeDtypeStruct((s, C_WIDTH), F32), jax.ShapeDtypeStruct((1, d), F32)],
        operands=(dx1, wo, ya, yb, yc, gg), name=name, comm=comm)


CONV_CHUNK = 256
CONV_HALO = 8


def _conv_fwd(z, cw, name):
    s = z.shape[0]
    nch = s // CONV_CHUNK

    def body(gb_ref, gc_ref, xb_ref, w_ref, o_ref, us):
        us[pl.ds(0, CONV_HALO), :] = jnp.zeros((CONV_HALO, LANES), F32)
        us[pl.ds(CONV_HALO, s), :] = gc_ref[...] * xb_ref[...]
        w0, w1, w2 = w_ref[0:1, :], w_ref[1:2, :], w_ref[2:3, :]

        def chunk(c, carry):
            st = pl.multiple_of(c * CONV_CHUNK, CONV_CHUNK)
            ext = us[pl.ds(st, CONV_CHUNK + CONV_HALO), :]
            y = (w0 * ext[CONV_HALO - 2:CONV_HALO - 2 + CONV_CHUNK]
                 + w1 * ext[CONV_HALO - 1:CONV_HALO - 1 + CONV_CHUNK]
                 + w2 * ext[CONV_HALO:])
            o_ref[pl.ds(st, CONV_CHUNK), :] = gb_ref[pl.ds(st, CONV_CHUNK), :] * y
            return carry

        lax.fori_loop(0, nch, chunk, 0)

    col = lambda blk: pl.BlockSpec((s, LANES), lambda j, blk=blk: (0, blk + j))
    return pl.pallas_call(
        body,
        grid=(CONV_CH // LANES,),
        in_specs=[col(GB_BLK), col(GC_BLK), col(XB_BLK), pl.BlockSpec((3, LANES), lambda j: (0, j))],
        out_specs=pl.BlockSpec((s, LANES), lambda j: (0, j)),
        out_shape=jax.ShapeDtypeStruct((s, CONV_CH), F32),
        scratch_shapes=[pltpu.VMEM((s + CONV_HALO, LANES), F32)],
        compiler_params=_params("parallel"),
        name=name,
    )(z, z, z, cw)


def _conv_bwd(z, cw, dyb, dz, name):
    s = z.shape[0]
    nch = s // CONV_CHUNK
    ncol = CONV_CH // LANES

    def body(gb_ref, gc_ref, xb_ref, w_ref, dy_ref, dz_in, dz_ref, dw_ref, us, ds_, dgb_ref, dgc_ref, dxb_ref, sems):
        j = pl.program_id(0)

        def to_dz(staged, blk, k):
            cols = pl.ds(pl.multiple_of((blk + j) * LANES, LANES), LANES)
            return pltpu.make_async_copy(staged, dz_ref.at[:, cols], sems.at[k])

        copies = [to_dz(dgb_ref, GB_BLK, 0), to_dz(dgc_ref, GC_BLK, 1), to_dz(dxb_ref, XB_BLK, 2)]

        @pl.when(j > 0)
        def _():
            for cp in copies:
                cp.wait()

        us[pl.ds(0, CONV_HALO), :] = jnp.zeros((CONV_HALO, LANES), F32)
        us[pl.ds(CONV_HALO, s), :] = gc_ref[...] * xb_ref[...]
        ds_[pl.ds(s, CONV_HALO), :] = jnp.zeros((CONV_HALO, LANES), F32)
        ds_[pl.ds(0, s), :] = dy_ref[...] * gb_ref[...]
        w0, w1, w2 = w_ref[0:1, :], w_ref[1:2, :], w_ref[2:3, :]
        zero = jnp.zeros((1, LANES), F32)

        def chunk(c, carry):
            a0, a1, a2 = carry
            st = pl.multiple_of(c * CONV_CHUNK, CONV_CHUNK)
            rows = pl.ds(st, CONV_CHUNK)
            ext = us[pl.ds(st, CONV_CHUNK + CONV_HALO), :]
            um2 = ext[CONV_HALO - 2:CONV_HALO - 2 + CONV_CHUNK]
            um1 = ext[CONV_HALO - 1:CONV_HALO - 1 + CONV_CHUNK]
            u0 = ext[CONV_HALO:]
            dext = ds_[pl.ds(st, CONV_CHUNK + CONV_HALO), :]
            dc0 = dext[:CONV_CHUNK]
            du = w2 * dc0 + w1 * dext[1:1 + CONV_CHUNK] + w0 * dext[2:2 + CONV_CHUNK]
            yconv = w0 * um2 + w1 * um1 + w2 * u0
            dgb_ref[rows, :] = (dy_ref[rows, :] * yconv).astype(BF16)
            dgc_ref[rows, :] = (du * xb_ref[rows, :]).astype(BF16)
            dxb_ref[rows, :] = (du * gc_ref[rows, :]).astype(BF16)
            a0 = a0 + jnp.sum(dc0 * um2, axis=0, keepdims=True)
            a1 = a1 + jnp.sum(dc0 * um1, axis=0, keepdims=True)
            a2 = a2 + jnp.sum(dc0 * u0, axis=0, keepdims=True)
            return a0, a1, a2

        a0, a1, a2 = lax.fori_loop(0, nch, chunk, (zero, zero, zero))
        dw_ref[...] = jnp.concatenate([a0, a1, a2, jnp.zeros((5, LANES), F32)], axis=0)
        for cp in copies:
            cp.start()

        @pl.when(j == ncol - 1)
        def _():
            for cp in copies:
                cp.wait()

    col = lambda blk: pl.BlockSpec((s, LANES), lambda j, blk=blk: (0, blk + j))
    hbm = pl.BlockSpec(memory_space=pl.ANY)
    return pl.pallas_call(
        body,
        grid=(ncol,),
        in_specs=[col(GB_BLK), col(GC_BLK), col(XB_BLK), pl.BlockSpec((3, LANES), lambda j: (0, j)),
                  pl.BlockSpec((s, LANES), lambda j: (0, j)), hbm],
        out_specs=[hbm, pl.BlockSpec((8, LANES), lambda j: (0, j))],
        out_shape=[jax.ShapeDtypeStruct(dz.shape, dz.dtype), jax.ShapeDtypeStruct((8, CONV_CH), F32)],
        scratch_shapes=[pltpu.VMEM((s + CONV_HALO, LANES), F32), pltpu.VMEM((s + CONV_HALO, LANES), F32)]
        + [pltpu.VMEM((s, LANES), BF16)] * 3 + [pltpu.SemaphoreType.DMA((3,))],
        input_output_aliases={5: 0},
        compiler_params=_params("arbitrary"),
        name=name,
    )(z, z, z, cw, dyb, dz)


ATTN_ROWS = 512
ATTN_UNROLL = 8


def _band_rows(b, d, r):
    base = pl.multiple_of(b * (BLOCK * d), BLOCK)
    prev = jnp.maximum(base - BLOCK * d, 0)
    if d == 1:
        return pl.ds(base, BLOCK), pl.ds(pl.multiple_of(prev, BLOCK), BLOCK)
    return pl.ds(base + r, BLOCK, stride=d), pl.ds(prev + r, BLOCK, stride=d)


def _write_band_bias(bias_ref, max_dist):
    qi = lax.broadcasted_iota(jnp.int32, (BLOCK, 2 * BLOCK), 0)
    kj = lax.broadcasted_iota(jnp.int32, (BLOCK, 2 * BLOCK), 1)
    dist = BLOCK + qi - kj
    band = (dist >= 0) & (dist <= max_dist)
    bias_ref[0:BLOCK, :] = jnp.where(band, 0.0, -jnp.inf)
    bias_ref[BLOCK:2 * BLOCK, :] = jnp.where(band & (kj >= BLOCK), 0.0, -jnp.inf)


def _band_bias(bias_ref, b):
    bias = bias_ref[pl.ds(pl.multiple_of(jnp.where(b > 0, 0, BLOCK), BLOCK), BLOCK), :]
    return jnp.concatenate([bias, bias], axis=0)


def _kv_halves(pair):
    zero = jnp.zeros((1, LANES), jnp.int32)
    return zero + (pair >> 1), zero + ((pair + 1) >> 1)


def _stack_heads(t, head0, halves=None):
    top, bottom = jnp.where(head0, t, 0.0), jnp.where(head0, 0.0, t)
    if halves is not None:
        top = jnp.where(halves[0] == 1, pltpu.roll(top, HEAD_DIM, 1), top)
        bottom = jnp.where(halves[1] == 0, pltpu.roll(bottom, HEAD_DIM, 1), bottom)
    return jnp.concatenate([top, bottom], axis=0).astype(BF16)


def _unstack_heads(t, head0, halves=None):
    top, bottom = t[:BLOCK], t[BLOCK:]
    if halves is not None:
        top = jnp.where(halves[0] == 1, pltpu.roll(top, HEAD_DIM, 1), top)
        bottom = jnp.where(halves[1] == 0, pltpu.roll(bottom, HEAD_DIM, 1), bottom)
    return jnp.where(head0, top, bottom)


def _block_loops(s, patterns, unroll, one_block):
    for n, d in enumerate(patterns):
        nb = (s // BLOCK) // d
        ur = min(unroll, d)
        ub = unroll // ur
        for r0 in range(0, d, ur):
            def trip(i, carry, n=n, d=d, r0=r0, ur=ur, ub=ub):
                for u in range(ub):
                    for r in range(r0, r0 + ur):
                        one_block(i * ub + u, d, r, n == 0)
                return carry
            lax.fori_loop(0, nb // ub, trip, 0)


def _attn_fwd(z, m_init, l_init, q_blk, k_blk, v_blk, patterns, max_dist, gqa, name, comm=None):
    s = z.shape[0]
    npair = 3

    def body(q_ref, k_ref, v_ref, mi_ref, o_ref, lse0_ref, lse1_ref, bias_scr, m_scr, l_scr, *kv_scr):
        head0 = lax.broadcasted_iota(jnp.int32, (1, LANES), 1) < HEAD_DIM
        _write_band_bias(bias_scr, max_dist)
        ones = jnp.ones((2 * BLOCK, LANES), BF16)
        k_src, v_src = kv_scr if gqa else (k_ref, v_ref)
        if gqa:
            half = (lax.broadcasted_iota(jnp.int32, (1, LANES), 1) >= HEAD_DIM).astype(jnp.int32)
            swap = ((pl.program_id(0) + half) >> 1) != half

            def expand(c, carry):
                rows = pl.ds(pl.multiple_of(c * ATTN_ROWS, ATTN_ROWS), ATTN_ROWS)
                k_src[rows, :] = jnp.where(swap, pltpu.roll(k_ref[rows, :], HEAD_DIM, 1), k_ref[rows, :])
                v_src[rows, :] = jnp.where(swap, pltpu.roll(v_ref[rows, :], HEAD_DIM, 1), v_ref[rows, :])
                return carry

            lax.fori_loop(0, s // ATTN_ROWS, expand, 0)

        def one_block(b, d, r, first):
            rq, rp = _band_rows(b, d, r)
            q2 = _stack_heads(q_ref[rq, :] * SCALE, head0)
            k2 = jnp.concatenate([k_src[rp, :], k_src[rq, :]], axis=0).astype(BF16)
            v2 = jnp.concatenate([v_src[rp, :], v_src[rq, :]], axis=0).astype(BF16)
            sc = _dot_nt(q2, k2) + _band_bias(bias_scr, b)
            mb = jnp.max(sc, axis=1, keepdims=True)
            p = jnp.exp(sc - mb).astype(BF16)
            ob = _dot_nn(p, jnp.concatenate([v2, ones], axis=1))
            m_blk = _unstack_heads(jnp.broadcast_to(mb, (2 * BLOCK, LANES)), head0)
            l_blk = _unstack_heads(ob[:, LANES:], head0)
            o_blk = _unstack_heads(ob[:, :LANES], head0)
            if first and l_init == 0.0:
                m_new, l_new, o_new = m_blk, l_blk, o_blk
            else:
                if first:
                    m_old, l_old, o_old = jnp.broadcast_to(mi_ref[...], (BLOCK, LANES)), l_init, 0.0
                else:
                    m_old, l_old, o_old = m_scr[rq, :], l_scr[rq, :], o_ref[rq, :]
                m_new = jnp.maximum(m_old, m_blk)
                a_old = jnp.exp(m_old - m_new)
                a_blk = jnp.exp(m_blk - m_new)
                l_new = l_old * a_old + l_blk * a_blk
                o_new = o_old * a_old + o_blk * a_blk
            o_ref[rq, :], l_scr[rq, :], m_scr[rq, :] = o_new, l_new, m_new

        _block_loops(s, patterns, ATTN_UNROLL, one_block)

        def fin(c, carry):
            rows = pl.ds(pl.multiple_of(c * ATTN_ROWS, ATTN_ROWS), ATTN_ROWS)
            l = l_scr[rows, :]
            o_ref[rows, :] = o_ref[rows, :] / l
            lse = m_scr[rows, :] + jnp.log(l)
            swapped = pltpu.roll(lse, HEAD_DIM, 1)
            lse0_ref[rows, :] = jnp.where(head0, lse, swapped)
            lse1_ref[rows, :] = jnp.where(head0, swapped, lse)
            return carry

        lax.fori_loop(0, s // ATTN_ROWS, fin, 0)

    kv = (lambda blk: pl.BlockSpec((s, LANES), lambda j, blk=blk: (0, blk), pipeline_mode=pl.Buffered(1))) if gqa \
        else (lambda blk: pl.BlockSpec((s, LANES), lambda j, blk=blk: (0, blk + j)))
    own = pl.BlockSpec((s, LANES), lambda j: (0, j))
    return _call(
        body,
        grid=(npair,),
        in_specs=[pl.BlockSpec((s, LANES), lambda j: (0, q_blk + j)), kv(k_blk), kv(v_blk),
                  pl.BlockSpec((1, LANES), lambda j: (0, j))],
        out_specs=[own, own, own],
        out_shape=[jax.ShapeDtypeStruct((s, npair * LANES), F32)] * 3,
        operands=(z, z, z, m_init), name=name,
        scratch_shapes=[pltpu.VMEM((2 * BLOCK, 2 * BLOCK), F32)] + [pltpu.VMEM((s, LANES), F32)] * (4 if gqa else 2),
        comm=comm)


def _attn_bwd(z, do, o, lse, m_init, dz, q_blk, k_blk, v_blk, patterns, max_dist, gqa, name, comm=None):
    s = z.shape[0]
    npair = 3
    n_dz_in = 0 if dz is None else 1

    def body(q_ref, k_ref, v_ref, do_ref, o_ref, lse0_ref, lse1_ref, mi_ref, *rest):
        (dz_ref, dm_ref, dq_acc, dk_acc, dv_acc, dl0_scr, dl1_scr, bias_scr,
         dq_out, dk_out, dv_out, out_sems) = rest[n_dz_in:]
        pair = pl.program_id(0)
        head0 = lax.broadcasted_iota(jnp.int32, (1, LANES), 1) < HEAD_DIM
        halves = _kv_halves(pair) if gqa else None
        _write_band_bias(bias_scr, max_dist)

        def zero_kv():
            def f(c, carry):
                rows = pl.ds(pl.multiple_of(c * ATTN_ROWS, ATTN_ROWS), ATTN_ROWS)
                dk_acc[rows, :] = jnp.zeros((ATTN_ROWS, LANES), F32)
                dv_acc[rows, :] = jnp.zeros((ATTN_ROWS, LANES), F32)
                return carry
            lax.fori_loop(0, s // ATTN_ROWS, f, 0)

        if gqa:
            pl.when(pair == 0)(zero_kv)
        else:
            zero_kv()

        def prep(c, dm):
            rows = pl.ds(pl.multiple_of(c * ATTN_ROWS, ATTN_ROWS), ATTN_ROWS)
            dq_acc[rows, :] = jnp.zeros((ATTN_ROWS, LANES), F32)
            prod = do_ref[rows, :] * o_ref[rows, :]
            d0 = jnp.sum(jnp.where(head0, prod, 0.0), axis=1, keepdims=True)
            d1 = jnp.sum(jnp.where(head0, 0.0, prod), axis=1, keepdims=True)
            dl0_scr[rows, :] = jnp.broadcast_to(d0, (ATTN_ROWS, LANES))
            dl1_scr[rows, :] = jnp.broadcast_to(d1, (ATTN_ROWS, LANES))
            lse_own = jnp.where(head0, lse0_ref[rows, :], lse1_ref[rows, :])
            psink = jnp.exp(mi_ref[...] - lse_own)
            return dm - jnp.sum(psink * jnp.where(head0, d0, d1), axis=0, keepdims=True)

        dm_ref[...] = lax.fori_loop(0, s // ATTN_ROWS, prep, jnp.zeros((1, LANES), F32))

        def one_block(b, d, r, first):
            rq, rp = _band_rows(b, d, r)
            q2 = _stack_heads(q_ref[rq, :] * SCALE, head0, halves)
            do2 = _stack_heads(do_ref[rq, :], head0, halves)
            k2 = jnp.concatenate([k_ref[rp, :], k_ref[rq, :]], axis=0).astype(BF16)
            v2 = jnp.concatenate([v_ref[rp, :], v_ref[rq, :]], axis=0).astype(BF16)
            lse2 = jnp.concatenate([lse0_ref[rq, :], lse1_ref[rq, :]], axis=0)
            dl2 = jnp.concatenate([dl0_scr[rq, :], dl1_scr[rq, :]], axis=0)
            lse2 = jnp.concatenate([lse2, lse2], axis=1)
            dl2 = jnp.concatenate([dl2, dl2], axis=1)
            p = jnp.exp(_dot_nt(q2, k2) + _band_bias(bias_scr, b) - lse2)
            dp = _dot_nt(do2, v2)
            dsc = (p * (dp - dl2)).astype(BF16)
            dq2 = _unstack_heads(_dot_nn(dsc, k2), head0, halves)
            dk2 = _dot_tn(dsc, q2)
            dv2 = _dot_tn(p.astype(BF16), do2)
            dq_acc[rq, :] += dq2 * SCALE
            dk_acc[rp, :] += dk2[:BLOCK]
            dk_acc[rq, :] += dk2[BLOCK:]
            dv_acc[rp, :] += dv2[:BLOCK]
            dv_acc[rq, :] += dv2[BLOCK:]

        _block_loops(s, patterns, ATTN_UNROLL, one_block)

        def to_dz(staged, blk, k):
            cols = pl.ds(pl.multiple_of(blk * LANES, LANES), LANES)
            return pltpu.make_async_copy(staged, dz_ref.at[:, cols], out_sems.at[k])

        last_pair = pair == npair - 1
        q_copy = to_dz(dq_out, q_blk + pair, 0)
        kv_copies = [to_dz(dk_out, k_blk + (0 if gqa else pair), 1), to_dz(dv_out, v_blk + (0 if gqa else pair), 2)]

        @pl.when(pair > 0)
        def _():
            for cp in [q_copy] + ([] if gqa else kv_copies):
                cp.wait()

        def stage(acc, out):
            def f(c, carry):
                rows = pl.ds(pl.multiple_of(c * ATTN_ROWS, ATTN_ROWS), ATTN_ROWS)
                out[rows, :] = acc[rows, :].astype(BF16)
                return carry
            lax.fori_loop(0, s // ATTN_ROWS, f, 0)

        def stage_kv():
            stage(dk_acc, dk_out)
            stage(dv_acc, dv_out)
            for cp in kv_copies:
                cp.start()

        stage(dq_acc, dq_out)
        q_copy.start()
        if gqa:
            pl.when(last_pair)(stage_kv)
        else:
            stage_kv()

        @pl.when(last_pair)
        def _():
            for cp in [q_copy] + kv_copies:
                cp.wait()

    own = pl.BlockSpec((s, LANES), lambda j: (0, j))
    hbm = pl.BlockSpec(memory_space=pl.ANY)
    if gqa:
        kv = lambda blk: pl.BlockSpec((s, LANES), lambda j, blk=blk: (0, blk), pipeline_mode=pl.Buffered(1))
    else:
        kv = lambda blk: pl.BlockSpec((s, LANES), lambda j, blk=blk: (0, blk + j))
    in_specs = [pl.BlockSpec((s, LANES), lambda j: (0, q_blk + j)), kv(k_blk), kv(v_blk), own, own, own, own,
                pl.BlockSpec((1, LANES), lambda j: (0, j))]
    operands = (z, z, z, do, o, lse[0], lse[1], m_init)
    return _call(
        body,
        grid=(npair,),
        in_specs=in_specs + [hbm] * n_dz_in,
        out_specs=[hbm, pl.BlockSpec((1, LANES), lambda j: (0, j))],
        out_shape=[jax.ShapeDtypeStruct((s, IN_WIDTH), BF16), jax.ShapeDtypeStruct((1, npair * LANES), F32)],
        operands=operands + (() if dz is None else (dz,)), name=name,
        scratch_shapes=[pltpu.VMEM((s, LANES), F32)] * 5 + [pltpu.VMEM((2 * BLOCK, 2 * BLOCK), F32)]
        + [pltpu.VMEM((s, LANES), BF16)] * 3 + [pltpu.SemaphoreType.DMA((3,))],
        comm=comm, aliases={} if dz is None else {len(in_specs): 0})


def _adamw_math(w, g, m, v):
    m = ADAM_B1 * m + (1.0 - ADAM_B1) * g
    v = ADAM_B2 * v + (1.0 - ADAM_B2) * (g * g)
    m_hat = m / (1.0 - ADAM_B1 ** ADAM_STEP)
    v_hat = v / (1.0 - ADAM_B2 ** ADAM_STEP)
    delta = -ADAM_LR * (m_hat / (jnp.sqrt(v_hat) + ADAM_EPS) + ADAM_WD * w)
    return delta, m, v


def _adamw(w, g, m, v, name):
    rows, cols = w.shape
    tr = min(rows, 256)

    def body(w_ref, g_ref, m_ref, v_ref, d_ref, nm_ref, nv_ref):
        d_ref[...], nm_ref[...], nv_ref[...] = _adamw_math(w_ref[...], g_ref[...], m_ref[...], v_ref[...])

    spec = pl.BlockSpec((tr, cols), lambda i: (i, 0))
    return pl.pallas_call(
        body,
        grid=(rows // tr,),
        in_specs=[spec] * 4,
        out_specs=[spec] * 3,
        out_shape=[jax.ShapeDtypeStruct((rows, cols), F32)] * 3,
        compiler_params=_params("parallel"),
        name=name,
    )(w, g, m, v)


def _sum_adamw(parts, w, m, v, pos, transpose, name):
    assert len(parts) == DEPTH == 2
    (p0, r0), (p1, r1) = parts
    _, rows, cols = p0.shape
    tr = 256 if rows % 256 == 0 else rows
    nt = rows // tr

    def body(pos_ref, p0_ref, r0_ref, p1_ref, r1_ref, w_ref, m_ref, v_ref, g_ref, d_ref, nm_ref, nv_ref):
        def run(p_ref, r_ref):
            g = ((p_ref[...].astype(F32) + r_ref[0].astype(F32)) + r_ref[1].astype(F32)) + r_ref[2].astype(F32)
            if transpose:
                g = g.T
            g_ref[...] = g
            d_ref[...], nm_ref[...], nv_ref[...] = _adamw_math(w_ref[...], g, m_ref[...], v_ref[...])

        layer0 = pl.program_id(0) < nt
        pl.when(layer0)(lambda: run(p0_ref, r0_ref))
        pl.when(jnp.logical_not(layer0))(lambda: run(p1_ref, r1_ref))

    def tile0(i):
        return jnp.minimum(i, nt - 1)

    def tile1(i):
        return jnp.maximum(i - nt, 0)

    if transpose:
        w_spec = pl.BlockSpec((None, cols, tr), lambda i, q: (i // nt, 0, i % nt))
    else:
        w_spec = pl.BlockSpec((None, tr, cols), lambda i, q: (i // nt, i % nt, 0))
    return pl.pallas_call(
        body,
        grid_spec=pltpu.PrefetchScalarGridSpec(
            num_scalar_prefetch=1,
            grid=(DEPTH * nt,),
            in_specs=[pl.BlockSpec((None, tr, cols), lambda i, q: (q[0], tile0(i), 0)),
                      pl.BlockSpec((3, tr, cols), lambda i, q: (0, tile0(i), 0)),
                      pl.BlockSpec((None, tr, cols), lambda i, q: (q[0], tile1(i), 0)),
                      pl.BlockSpec((3, tr, cols), lambda i, q: (0, tile1(i), 0)),
                      w_spec, w_spec, w_spec],
            out_specs=[w_spec] * 4,
        ),
        out_shape=[jax.ShapeDtypeStruct(w.shape, F32)] * 4,
        compiler_params=_params("arbitrary"),
        name=name,
    )(pos, p0, r0, p1, r1, w, m, v)


def _small_sum_adamw(gathered, params, name):
    _, rows, cols = gathered.shape
    n = len(params)

    def body(ga_ref, *refs):
        ins, outs, (g_scr,) = refs[:3 * n], refs[3 * n:7 * n + 2], refs[7 * n + 2:]
        g = ga_ref[0]
        for i in range(1, N_DEV):
            g = g + ga_ref[i]
        g_scr[...] = g
        for k, (row0, w, _, _) in enumerate(params):
            w_ref, m_ref, v_ref = ins[3 * k:3 * k + 3]
            gk = g_scr[row0:row0 + w.shape[0], :]
            outs[4 * k][...] = gk
            outs[4 * k + 1][...], outs[4 * k + 2][...], outs[4 * k + 3][...] = _adamw_math(
                w_ref[...], gk, m_ref[...], v_ref[...])
        outs[4 * n][...] = g_scr[CONV_ROW:CONV_ROW + 8, :]
        outs[4 * n + 1][...] = g_scr[LOSS_ROW:LOSS_ROW + 1, :]

    out_shape = []
    for _, w, _, _ in params:
        out_shape += [jax.ShapeDtypeStruct(w.shape, F32)] * 4
    out_shape += [jax.ShapeDtypeStruct((8, cols), F32), jax.ShapeDtypeStruct((1, cols), F32)]
    res = pl.pallas_call(
        body,
        out_shape=out_shape,
        scratch_shapes=[pltpu.VMEM((rows, cols), F32)],
        name=name,
    )(gathered, *[t for _, w, m, v in params for t in (w, m, v)])
    return [res[4 * k:4 * k + 4] for k in range(n)], res[4 * n], res[4 * n + 1]


def _pair_sum(g4, r1, pos, name):
    _, _, rows, cols = g4.shape
    tr = min(rows, 512)

    def body(pos_ref, g_ref, r_ref, o_ref):
        o_ref[...] = (g_ref[...].astype(F32) + r_ref[...].astype(F32)).astype(BF16)

    return pl.pallas_call(
        body,
        grid_spec=pltpu.PrefetchScalarGridSpec(
            num_scalar_prefetch=1,
            grid=(4, rows // tr),
            in_specs=[pl.BlockSpec((None, None, tr, cols), lambda i, j, p: (i, p[1], j, 0)),
                      pl.BlockSpec((None, tr, cols), lambda i, j, p: (i, j, 0))],
            out_specs=pl.BlockSpec((None, tr, cols), lambda i, j, p: (i, j, 0)),
        ),
        out_shape=jax.ShapeDtypeStruct((4, rows, cols), BF16),
        compiler_params=_params("parallel", "parallel"),
        name=name,
    )(pos, g4, r1)


def _place():
    return lax.axis_index("x"), lax.axis_index("y"), lax.axis_index("c")


def _gather_comm(shards):
    na = len(shards)

    def plan(ins, outs, sems):
        send_sems, recv_sems, local_sems = sems
        x, y, c = _place()
        me, sibling = (x, y, c), (x, y, 1 - c)
        chips = [(1 - x, y), (x, 1 - y), (1 - x, 1 - y)]

        def rows(a, px, py, pc):
            m = ins[a].shape[0]
            return outs[a].at[pl.ds((4 * px + 2 * py + pc) * m, m), :]

        def copy(a, k, block, to, src=None):
            return pltpu.make_async_remote_copy(
                src_ref=rows(a, *block) if src is None else src, dst_ref=rows(a, *block),
                send_sem=send_sems.at[a, k], recv_sem=recv_sems.at[a, k], device_id=to, device_id_type=MESH)

        mine = [pltpu.make_async_copy(ins[a], rows(a, *me), local_sems.at[a]) for a in range(na)]
        first = []
        for a in range(na):
            first.append(copy(a, 0, me, sibling, src=ins[a]))
            first += [copy(a, 1 + j, me, (*chip, c), src=ins[a]) for j, chip in enumerate(chips)]
        return me, sibling, chips, c, copy, mine, first

    def start(ins, outs, sems):
        *_, mine, first = plan(ins, outs, sems)
        for cp in mine + first:
            cp.start()

    def finish(ins, outs, sems):
        me, sibling, chips, c, copy, mine, first = plan(ins, outs, sems)
        passed = []
        for j, chip in enumerate(chips):
            for a in range(na):
                copy(a, 1 + j, (*chip, c), me).wait_recv()
                cp = copy(a, 4 + j, (*chip, c), sibling)
                cp.start()
                passed.append(cp)
        for a in range(na):
            copy(a, 0, sibling, me).wait_recv()
            for j, chip in enumerate(chips):
                copy(a, 4 + j, (*chip, 1 - c), me).wait_recv()
        for cp in first + passed:
            cp.wait_send()
        for cp in mine:
            cp.wait()

    return _Comm(tuple(shards),
                 tuple(jax.ShapeDtypeStruct((N_DEV * t.shape[0], t.shape[1]), t.dtype) for t in shards),
                 (pltpu.SemaphoreType.DMA((na, 7)), pltpu.SemaphoreType.DMA((na, 7)), pltpu.SemaphoreType.DMA((na,))),
                 start, finish)


def _exchange_comm(arrays, out_shape, n_copies, copies_of):
    na = len(arrays)

    def every(ins, outs, sems):
        send_sems, recv_sems = sems
        return [cp for a in range(na) for cp in copies_of(ins, outs, a, send_sems, recv_sems)]

    def start(ins, outs, sems):
        for cp in every(ins, outs, sems):
            cp.start()

    def finish(ins, outs, sems):
        for cp in every(ins, outs, sems):
            cp.wait()

    return _Comm(tuple(arrays), tuple(out_shape),
                 (pltpu.SemaphoreType.DMA((na, n_copies)), pltpu.SemaphoreType.DMA((na, n_copies))), start, finish)


def _sibling_comm(grads):
    def copies_of(ins, outs, a, send_sems, recv_sems):
        x, y, c = _place()
        return [pltpu.make_async_remote_copy(
            src_ref=ins[a].at[chip, 1 - c], dst_ref=outs[a].at[chip],
            send_sem=send_sems.at[a, chip], recv_sem=recv_sems.at[a, chip],
            device_id=(x, y, 1 - c), device_id_type=MESH) for chip in range(4)]

    return _exchange_comm(grads, [jax.ShapeDtypeStruct((4,) + t.shape[2:], t.dtype) for t in grads], 4, copies_of)


def _chip_comm(partials):
    def copies_of(ins, outs, a, send_sems, recv_sems):
        x, y, c = _place()
        chips = [(1 - x, y), (x, 1 - y), (1 - x, 1 - y)]
        return [pltpu.make_async_remote_copy(
            src_ref=ins[a].at[2 * cx + cy], dst_ref=outs[a].at[k],
            send_sem=send_sems.at[a, k], recv_sem=recv_sems.at[a, k],
            device_id=(cx, cy, c), device_id_type=MESH) for k, (cx, cy) in enumerate(chips)]

    return _exchange_comm(partials, [jax.ShapeDtypeStruct((3,) + t.shape[1:], t.dtype) for t in partials], 3, copies_of)


def _pad_rows(t, rows):
    return jnp.pad(t, ((0, rows - t.shape[0]), (0, D_MODEL - t.shape[1])))


MIX_ROW, GROUP_ROW, MLP_ROW, FINAL_ROW, CONV_ROW, SINK_ROW = 0, 8, 16, 24, 32, 40
LOSS_ROW = FINAL_ROW + 1


def _pack_small(g_mix, g_group, g_mlp, g_final, conv, sinks, loss):
    final_and_loss = jnp.concatenate([g_final.reshape(1, D_MODEL), _pad_rows(loss, 1)], axis=0)
    return jnp.concatenate([
        _pad_rows(g_mix, 8), _pad_rows(g_group, 8), _pad_rows(g_mlp, 8), _pad_rows(final_and_loss, 8),
        _pad_rows(conv.reshape(DEPTH * 3, CONV_CH), 8), _pad_rows(sinks.reshape(1, DEPTH * 6), 8)], axis=0)


def kernel(x, w_in, conv_w, sinks, g_mix, g_group, w_o, g_mlp, w_ff_in, w_ff_out, g_final, loss_target, m_w_in, m_conv_w, m_sinks, m_g_mix, m_g_group, m_w_o, m_g_mlp, m_w_ff_in, m_w_ff_out, m_g_final, v_w_in, v_conv_w, v_sinks, v_g_mix, v_g_group, v_w_o, v_g_mlp, v_w_ff_in, v_w_ff_out, v_g_final):
    ax, ay, ac = _place()
    chip = 2 * ax + ay
    dev = 4 * ax + 2 * ay + ac
    pos = jnp.stack([chip, ac]).astype(jnp.int32)

    x0 = x.reshape(SEQ, D_MODEL)
    target = loss_target.reshape(SEQ, D_MODEL)

    shards = {}
    for l in range(DEPTH):
        shards[l, 0], shards[l, 1] = w_in[l].T.astype(BF16), w_o[l].astype(BF16)
        shards[l, 2], shards[l, 3] = w_ff_in[l].T.astype(BF16), w_ff_out[l].astype(BF16)
    conv_tile = jnp.pad(conv_w.reshape(DEPTH * 3, CONV_CH // N_DEV), ((0, 2), (0, LANES - CONV_CH // N_DEV)))
    wt_in0, conv_all = _comm_only(_gather_comm([shards[0, 0], conv_tile]), "gather_first")
    conv_full = conv_all.reshape(N_DEV, 8, LANES)[:, :DEPTH * 3, :CONV_CH // N_DEV]
    conv_full = conv_full.transpose(1, 0, 2).reshape(DEPTH, 3, CONV_CH)

    dx, parts, small = _step(x0, target, shards, wt_in0, conv_full, sinks, g_mix, g_group, g_mlp, g_final, pos)
    return _finish(dx, parts, small, pos, dev, w_in, conv_w, sinks, g_mix, g_group, w_o, g_mlp, w_ff_in, w_ff_out, g_final, m_w_in, m_conv_w, m_sinks, m_g_mix, m_g_group, m_w_o, m_g_mlp, m_w_ff_in, m_w_ff_out, m_g_final, v_w_in, v_conv_w, v_sinks, v_g_mix, v_g_group, v_w_o, v_g_mlp, v_w_ff_in, v_w_ff_out, v_g_final)


FWD_CARRY = {(0, "in_proj"): ((0, 1),), (0, "window"): ((1, 0),), (0, "dilated"): ((0, 2),),
             (0, "mix_out"): ((1, 1),), (0, "ff_in"): ((0, 3),), (0, "ff_out"): ((1, 3),),
             (1, "dilated"): ((1, 2),)}


def _step(x0, target, shards, wt_in0, conv_full, sinks, g_mix, g_group, g_mlp, g_final, pos):
    sink_lanes = jnp.repeat(sinks.reshape(DEPTH, 6), HEAD_DIM, axis=1)
    no_sink = jnp.full((1, A_WIDTH), NEG_BIG, F32)
    full = {(0, 0): wt_in0}

    def gather(stage, l):
        keys = FWD_CARRY.get((l, stage), ())
        return keys, (_gather_comm([shards[k] for k in keys]) if keys else None)

    def landed(keys, got):
        full.update(zip(keys, got))

    saved = []
    xc = x0
    for l in range(DEPTH):
        keys, comm = gather("in_proj", l)
        (z, h), got = _norm_mm(xc, g_mix[l:l + 1], full[l, 0], False, f"in_proj_{l}", comm)
        landed(keys, got)
        sink_l = sink_lanes[l:l + 1]
        keys, comm = gather("window", l)
        (yc, *lse_c), got = _attn_fwd(z, sink_l, 1.0, QC_BLK, KC_BLK, VC_BLK, (1,), C_MAX_DIST, True,
                                     f"window_attn_{l}", comm)
        landed(keys, got)
        yb = _conv_fwd(z, conv_full[l], f"conv_{l}")
        keys, comm = gather("dilated", l)
        (ya, *lse_a), got = _attn_fwd(z, no_sink, 0.0, QA_BLK, KA_BLK, VA_BLK, DILATED_PATTERNS, A_MAX_DIST, False,
                                     f"dilated_attn_{l}", comm)
        landed(keys, got)
        keys, comm = gather("mix_out", l)
        (y, x1), got = _mix_out(ya, yb, yc, g_group[l:l + 1], full[l, 1], xc, f"mix_out_{l}", comm)
        landed(keys, got)
        keys, comm = gather("ff_in", l)
        (a, h2), got = _norm_mm(x1, g_mlp[l:l + 1], full[l, 2], True, f"ff_in_{l}", comm)
        landed(keys, got)
        keys, comm = gather("ff_out", l)
        (x2,), got = _mm_res(a, full[l, 3], x1, f"ff_out_{l}", comm)
        landed(keys, got)
        saved.append((xc, z, h, ya, lse_a, yb, yc, lse_c, sink_l, y, x1, a, h2))
        xc = x2

    loss_slab, dx, dxb, dg_final = _loss_head(xc, g_final.reshape(1, D_MODEL), target, "loss_head")

    def by_owner(t):
        return t.reshape(4, 2, t.shape[0] // N_DEV, D_MODEL)

    def pair(key, g, r1):
        return _pair_sum(g, r1, pos, f"grad_pair_sum_{key[0]}_{key[1]}")

    partial, r2 = {}, {}
    dg_mix, dg_group, dg_mlp, dconv, dsinks = [None] * DEPTH, [None] * DEPTH, [None] * DEPTH, [None] * DEPTH, [None] * DEPTH
    for l in reversed(range(DEPTH)):
        xin, z, h, ya, lse_a, yb, yc, lse_c, sink_l, y, x1, a, h2 = saved[l]
        late = [(l + 1, 1), (l + 1, 0)] if l + 1 < DEPTH else []
        (du,), got = _mlp_bwd_act(dxb, full[l, 3], a, f"ff_out_bwd_{l}",
                                  _chip_comm([partial[k] for k in late]) if late else None)
        r2.update(zip(late, got))
        (g3,), _ = _mm_tn(a, dxb, f"grad_w_ff_out_{l}")
        (g2,), _ = _mm_tn(du, h2, f"grad_w_ff_in_{l}")
        g3, g2 = by_owner(g3), by_owner(g2)
        (dx1, dx1b, dg_mlp[l]), got = _mm_nn_normbwd(du, full[l, 2], x1, dx, g_mlp[l:l + 1], f"ff_in_bwd_{l}",
                                                    _sibling_comm([g3, g2]))
        partial[l, 3], partial[l, 2] = pair((l, 3), g3, got[0]), pair((l, 2), g2, got[1])
        (g1,), _ = _mm_tn(y, dx1b, f"grad_w_o_{l}")
        g1 = by_owner(g1)
        (dya, dyb, dyc, dg_group[l]), got = _mix_bwd(dx1b, full[l, 1], ya, yb, yc, g_group[l:l + 1],
                                                     f"mix_out_bwd_{l}", _sibling_comm([g1]) if l == 0 else None)
        if l == 0:
            partial[l, 1] = pair((l, 1), g1, got[0])
        early = [(l, 3), (l, 2)] + ([(l, 1)] if l == 0 else [])
        (dz, _), got = _attn_bwd(z, dya, ya, lse_a, no_sink, None, QA_BLK, KA_BLK, VA_BLK, DILATED_PATTERNS,
                                 A_MAX_DIST, False, f"dilated_attn_bwd_{l}", _chip_comm([partial[k] for k in early]))
        r2.update(zip(early, got))
        dz, dcw = _conv_bwd(z, conv_full[l], dyb, dz, f"conv_bwd_{l}")
        (dz, dsink), _ = _attn_bwd(z, dyc, yc, lse_c, sink_l, dz, QC_BLK, KC_BLK, VC_BLK, (1,), C_MAX_DIST,
                                   True, f"window_attn_bwd_{l}")
        (g0,), _ = _mm_tn(dz, h, f"grad_w_in_{l}")
        g0 = by_owner(g0)
        if l > 0:
            (dx, dxb, dg_mix[l]), got = _mm_nn_normbwd(dz, full[l, 0], xin, dx1, g_mix[l:l + 1], f"in_proj_bwd_{l}",
                                                      _sibling_comm([g1, g0]))
            partial[l, 1], partial[l, 0] = pair((l, 1), g1, got[0]), pair((l, 0), g0, got[1])
        else:
            (r1,) = _comm_only(_sibling_comm([g0]), "grad_sibling_exchange_last")
            partial[l, 0] = pair((l, 0), g0, r1)
            (dx, dxb, dg_mix[l]), got = _mm_nn_normbwd(dz, full[l, 0], xin, dx1, g_mix[l:l + 1], f"in_proj_bwd_{l}",
                                                      _chip_comm([partial[l, 0]]))
            r2[l, 0] = got[0]
        dconv[l] = dcw[:3]
        dsinks[l] = dsink[0, ::HEAD_DIM]
    parts = {key: (partial[key], r2[key]) for key in partial}
    small = _pack_small(jnp.concatenate(dg_mix), jnp.concatenate(dg_group), jnp.concatenate(dg_mlp),
                        dg_final, jnp.stack(dconv), jnp.stack(dsinks), loss_slab[0:1])
    return dx, parts, small


def _finish(dx, parts, small, pos, dev, w_in, conv_w, sinks, g_mix, g_group, w_o, g_mlp, w_ff_in, w_ff_out, g_final, m_w_in, m_conv_w, m_sinks, m_g_mix, m_g_group, m_w_o, m_g_mlp, m_w_ff_in, m_w_ff_out, m_g_final, v_w_in, v_conv_w, v_sinks, v_g_mix, v_g_group, v_w_o, v_g_mlp, v_w_ff_in, v_w_ff_out, v_g_final):
    grad_x = dx.reshape(1, SEQ, D_MODEL)

    (small_all,) = _comm_only(_gather_comm([small]), "gather_small_grads")
    row = lambda t: t.reshape(1, D_MODEL)
    sink_row = lambda t: _pad_rows(t.reshape(1, DEPTH * 6), 1)
    params = [(MIX_ROW, g_mix, m_g_mix, v_g_mix), (GROUP_ROW, g_group, m_g_group, v_g_group),
              (MLP_ROW, g_mlp, m_g_mlp, v_g_mlp), (FINAL_ROW, row(g_final), row(m_g_final), row(v_g_final)),
              (SINK_ROW, sink_row(sinks), sink_row(m_sinks), sink_row(v_sinks))]
    updated, conv_rows, loss_row = _small_sum_adamw(small_all.reshape(N_DEV, SMALL_ROWS, D_MODEL), params, "small_adamw")
    loss = loss_row[0, 0]
    (grad_g_mix, delta_g_mix, new_m_g_mix, new_v_g_mix), (grad_g_group, delta_g_group, new_m_g_group, new_v_g_group), \
        (grad_g_mlp, delta_g_mlp, new_m_g_mlp, new_v_g_mlp), final4, sinks4 = updated
    grad_g_final, delta_g_final, new_m_g_final, new_v_g_final = [t.reshape(D_MODEL) for t in final4]
    grad_sinks, delta_sinks, new_m_sinks, new_v_sinks = [t[0, :DEPTH * 6].reshape(DEPTH, 2, 3) for t in sinks4]
    conv_grad_full = conv_rows[:DEPTH * 3, :CONV_CH].reshape(DEPTH, 3, CONV_CH)
    cs = CONV_CH // N_DEV
    grad_conv_w = lax.dynamic_slice_in_dim(conv_grad_full, dev * cs, cs, axis=2)

    def tile_of(t):
        return jnp.pad(t.reshape(1, DEPTH * 3 * cs), ((0, 7), (0, 256 - DEPTH * 3 * cs)))

    cd, cm, cv = _adamw(tile_of(conv_w), tile_of(grad_conv_w), tile_of(m_conv_w), tile_of(v_conv_w), "conv_adamw")
    untile = lambda t: t[0, :DEPTH * 3 * cs].reshape(DEPTH, 3, cs)
    delta_conv_w, new_m_conv_w, new_v_conv_w = untile(cd), untile(cm), untile(cv)

    def big(kind, w, m, v, transpose, name):
        return _sum_adamw([parts[l, kind] for l in range(DEPTH)], w, m, v, pos, transpose, name)

    grad_w_in, delta_w_in, new_m_w_in, new_v_w_in = big(0, w_in, m_w_in, v_w_in, True, "adamw_w_in")
    grad_w_o, delta_w_o, new_m_w_o, new_v_w_o = big(1, w_o, m_w_o, v_w_o, False, "adamw_w_o")
    grad_w_ff_in, delta_w_ff_in, new_m_w_ff_in, new_v_w_ff_in = big(2, w_ff_in, m_w_ff_in, v_w_ff_in, True, "adamw_w_ff_in")
    grad_w_ff_out, delta_w_ff_out, new_m_w_ff_out, new_v_w_ff_out = big(3, w_ff_out, m_w_ff_out, v_w_ff_out, False,
                                                                         "adamw_w_ff_out")

    return (loss, grad_x, grad_w_in, grad_conv_w, grad_sinks, grad_g_mix, grad_g_group, grad_w_o, grad_g_mlp,
            grad_w_ff_in, grad_w_ff_out, grad_g_final,
            delta_w_in, delta_conv_w, delta_sinks, delta_g_mix, delta_g_group, delta_w_o, delta_g_mlp,
            delta_w_ff_in, delta_w_ff_out, delta_g_final,
            new_m_w_in, new_m_conv_w, new_m_sinks, new_m_g_mix, new_m_g_group, new_m_w_o, new_m_g_mlp,
            new_m_w_ff_in, new_m_w_ff_out, new_m_g_final,
            new_v_w_in, new_v_conv_w, new_v_sinks, new_v_g_mix, new_v_g_group, new_v_w_o, new_v_g_mlp,
            new_v_w_ff_in, new_v_w_ff_out, new_v_g_final)
```

```python
from typing import Callable, NamedTuple

import jax
import jax.numpy as jnp
from jax import lax
from jax.experimental import pallas as pl
from jax.experimental.pallas import tpu as pltpu

F32 = jnp.float32
BF16 = jnp.bfloat16
MESH = pl.DeviceIdType.MESH

N_DEV = 8
SEQ = 4096
D_MODEL = 1024
DEPTH = 2
HEAD_DIM = 64
LANES = 128
A_WIDTH = 384
CONV_CH = 256
C_WIDTH = 384
KV_WIDTH = 128
IN_WIDTH = 2560
D_FF = 4096
BLOCK = 128
DILATED_PATTERNS = (1, 4, 16)
A_MAX_DIST = 128
C_MAX_DIST = 127
EPS = 1e-6
SCALE = HEAD_DIM ** -0.5
NEG_BIG = -1e30
F32_TINY = 1.1754944e-38

QA_BLK, KA_BLK, VA_BLK = 0, 3, 6
GB_BLK, GC_BLK, XB_BLK = 9, 11, 13
QC_BLK, KC_BLK, VC_BLK = 15, 18, 19

ADAM_LR = 0.001
ADAM_B1 = 0.9
ADAM_B2 = 0.999
ADAM_EPS = 1e-08
ADAM_WD = 0.01
ADAM_STEP = 10

VMEM_LIMIT = 56 * 1024 * 1024
TILE_BUDGET = 46 * 1024 * 1024
ROW_TILE = 512
COL_CHUNK = 512
SMALL_ROWS = 48


def _dot_nn(a, b):
    return lax.dot_general(a, b, (((1,), (0,)), ((), ())), preferred_element_type=F32)


def _dot_nt(a, b):
    return lax.dot_general(a, b, (((1,), (1,)), ((), ())), preferred_element_type=F32)


def _dot_tn(a, b):
    return lax.dot_general(a, b, (((0,), (0,)), ((), ())), preferred_element_type=F32)


def _params(*sem):
    return pltpu.CompilerParams(dimension_semantics=sem, vmem_limit_bytes=VMEM_LIMIT)


def _resident(shape):
    return pl.BlockSpec(shape, lambda i: (0,) * len(shape), pipeline_mode=pl.Buffered(1))


def _row_tile(row_bytes, resident_bytes):
    for tm in (ROW_TILE, ROW_TILE // 2):
        if 2 * tm * row_bytes + resident_bytes <= TILE_BUDGET:
            return tm
    return ROW_TILE // 4


def _rms_scale(t):
    return lax.rsqrt(jnp.mean(t * t, axis=-1, keepdims=True) + EPS)


def _rms_bwd(n, r, dn):
    return r * (dn - n * jnp.mean(dn * n, axis=-1, keepdims=True))


class _Comm(NamedTuple):
    arrays: tuple
    out_shape: tuple
    sems: tuple
    start: Callable
    finish: Callable


def _call(body, grid, in_specs, out_specs, out_shape, operands, name, scratch_shapes=(), comm=None, aliases=None):
    n_in, n_out, n_scr = len(in_specs), len(out_shape), len(scratch_shapes)
    aliases = dict(aliases or {})
    if comm is None:
        res = pl.pallas_call(body, grid=grid, in_specs=list(in_specs), out_specs=list(out_specs),
                             out_shape=list(out_shape), scratch_shapes=list(scratch_shapes),
                             input_output_aliases=aliases,
                             compiler_params=_params("arbitrary"), name=name)(*operands)
        return list(res), []
    c_in, c_out = len(comm.arrays), len(comm.out_shape)
    hbm = pl.BlockSpec(memory_space=pl.ANY)
    last = grid[0] - 1

    def carried(*refs):
        ins, cins = refs[:n_in], refs[n_in:n_in + c_in]
        o0 = n_in + c_in
        outs, couts = refs[o0:o0 + n_out], refs[o0 + n_out:o0 + n_out + c_out]
        s0 = o0 + n_out + c_out
        scr, sems = refs[s0:s0 + n_scr], refs[s0 + n_scr:]
        pl.when(pl.program_id(0) == 0)(lambda: comm.start(cins, couts, sems))
        body(*ins, *outs, *scr)
        pl.when(pl.program_id(0) == last)(lambda: comm.finish(cins, couts, sems))

    res = pl.pallas_call(carried, grid=grid, in_specs=list(in_specs) + [hbm] * c_in,
                         out_specs=list(out_specs) + [hbm] * c_out, out_shape=list(out_shape) + list(comm.out_shape),
                         scratch_shapes=list(scratch_shapes) + list(comm.sems), input_output_aliases=aliases,
                         compiler_params=_params("arbitrary"), name=name)(*operands, *comm.arrays)
    return list(res[:n_out]), list(res[n_out:])


def _comm_only(comm, name):
    hbm = pl.BlockSpec(memory_space=pl.ANY)
    c_in, c_out = len(comm.arrays), len(comm.out_shape)

    def body(*refs):
        ins, outs, sems = refs[:c_in], refs[c_in:c_in + c_out], refs[c_in + c_out:]
        comm.start(ins, outs, sems)
        comm.finish(ins, outs, sems)

    return pl.pallas_call(body, in_specs=[hbm] * c_in, out_specs=[hbm] * c_out, out_shape=list(comm.out_shape),
                          scratch_shapes=list(comm.sems), name=name)(*comm.arrays)


def _norm_mm(x, g, wt, relu2, name, comm=None):
    s, d = x.shape
    n = wt.shape[0]
    tm = _row_tile(4 * d + (2 if relu2 else 4) * n + 2 * d, 2 * n * d)

    def body(x_ref, g_ref, w_ref, o_ref, h_ref):
        xx = x_ref[...]
        h = ((xx * _rms_scale(xx)) * g_ref[...]).astype(BF16)
        h_ref[...] = h
        for n0 in range(0, n, COL_CHUNK):
            zc = _dot_nt(h, w_ref[n0:n0 + COL_CHUNK, :])
            if relu2:
                zc = jnp.square(jnp.maximum(zc, 0.0)).astype(BF16)
            o_ref[:, n0:n0 + COL_CHUNK] = zc

    return _call(
        body,
        grid=(s // tm,),
        in_specs=[pl.BlockSpec((tm, d), lambda i: (i, 0)),
                  pl.BlockSpec((1, d), lambda i: (0, 0)),
                  _resident((n, d))],
        out_specs=[pl.BlockSpec((tm, n), lambda i: (i, 0)),
                   pl.BlockSpec((tm, d), lambda i: (i, 0))],
        out_shape=[jax.ShapeDtypeStruct((s, n), BF16 if relu2 else F32), jax.ShapeDtypeStruct((s, d), BF16)],
        operands=(x, g, wt), name=name, comm=comm)


def _mm_res(a, w2, x1, name, comm=None):
    s, f = a.shape
    d = w2.shape[1]
    tm = _row_tile(2 * f + 4 * d + 4 * d, 2 * f * d)

    def body(a_ref, w_ref, x_ref, o_ref):
        o_ref[...] = x_ref[...] + _dot_nn(a_ref[...], w_ref[...])

    return _call(
        body,
        grid=(s // tm,),
        in_specs=[pl.BlockSpec((tm, f), lambda i: (i, 0)),
                  _resident((f, d)),
                  pl.BlockSpec((tm, d), lambda i: (i, 0))],
        out_specs=[pl.BlockSpec((tm, d), lambda i: (i, 0))],
        out_shape=[jax.ShapeDtypeStruct((s, d), F32)],
        operands=(a, w2, x1), name=name, comm=comm)


def _mix_out(ya, yb, yc, gg, wo, x0, name, comm=None):
    s = ya.shape[0]
    d = wo.shape[1]
    tm = 2 * ROW_TILE

    def body(ya_ref, yb_ref, yc_ref, g_ref, w_ref, x_ref, y_ref, o_ref):
        parts = []
        for ref in (ya_ref, yb_ref, yc_ref):
            t = ref[...]
            parts.append(t * _rms_scale(t))
        y = (jnp.concatenate(parts, axis=1) * g_ref[...]).astype(BF16)
        y_ref[...] = y
        o_ref[...] = x_ref[...] + _dot_nn(y, w_ref[...])

    return _call(
        body,
        grid=(s // tm,),
        in_specs=[pl.BlockSpec((tm, A_WIDTH), lambda i: (i, 0)),
                  pl.BlockSpec((tm, CONV_CH), lambda i: (i, 0)),
                  pl.BlockSpec((tm, C_WIDTH), lambda i: (i, 0)),
                  pl.BlockSpec((1, d), lambda i: (0, 0)),
                  _resident((d, d)),
                  pl.BlockSpec((tm, d), lambda i: (i, 0))],
        out_specs=[pl.BlockSpec((tm, d), lambda i: (i, 0)),
                   pl.BlockSpec((tm, d), lambda i: (i, 0))],
        out_shape=[jax.ShapeDtypeStruct((s, d), BF16), jax.ShapeDtypeStruct((s, d), F32)],
        operands=(ya, yb, yc, gg, wo, x0), name=name, comm=comm)


def _loss_head(x, g, target, name):
    s, d = x.shape
    tm = ROW_TILE

    def body(x_ref, g_ref, t_ref, loss_ref, dx_ref, dxb_ref, dg_ref):
        @pl.when(pl.program_id(0) == 0)
        def _():
            loss_ref[...] = jnp.zeros_like(loss_ref)
            dg_ref[...] = jnp.zeros_like(dg_ref)

        xx = x_ref[...]
        r = _rms_scale(xx)
        n = xx * r
        gv = g_ref[...]
        err = n * gv - t_ref[...]
        per_tok = jnp.sum(err * err, axis=1, keepdims=True) * (1.0 / d)
        loss_ref[...] += 0.5 * jnp.sum(per_tok, axis=0, keepdims=True)
        dout = err * (1.0 / d)
        dg_ref[...] += jnp.sum(dout * n, axis=0, keepdims=True)
        dx = _rms_bwd(n, r, dout * gv)
        dx_ref[...] = dx
        dxb_ref[...] = dx.astype(BF16)

    return pl.pallas_call(
        body,
        grid=(s // tm,),
        in_specs=[pl.BlockSpec((tm, d), lambda i: (i, 0)),
                  pl.BlockSpec((1, d), lambda i: (0, 0)),
                  pl.BlockSpec((tm, d), lambda i: (i, 0))],
        out_specs=[pl.BlockSpec((8, LANES), lambda i: (0, 0)),
                   pl.BlockSpec((tm, d), lambda i: (i, 0)),
                   pl.BlockSpec((tm, d), lambda i: (i, 0)),
                   pl.BlockSpec((1, d), lambda i: (0, 0))],
        out_shape=[jax.ShapeDtypeStruct((8, LANES), F32), jax.ShapeDtypeStruct((s, d), F32),
                   jax.ShapeDtypeStruct((s, d), BF16), jax.ShapeDtypeStruct((1, d), F32)],
        compiler_params=_params("arbitrary"),
        name=name,
    )(x, g, target)


def _mlp_bwd_act(dxb, w2, a, name, comm=None):
    s, d = dxb.shape
    f = w2.shape[0]
    tm = _row_tile(2 * d + 2 * f + 2 * f, 2 * f * d)

    def body(dx_ref, w_ref, a_ref, du_ref):
        dx = dx_ref[...]
        for n0 in range(0, f, COL_CHUNK):
            da = _dot_nt(dx, w_ref[n0:n0 + COL_CHUNK, :])
            av = a_ref[:, n0:n0 + COL_CHUNK].astype(F32)
            rl = av * lax.rsqrt(jnp.maximum(av, F32_TINY))
            du_ref[:, n0:n0 + COL_CHUNK] = (da * (2.0 * rl)).astype(BF16)

    return _call(
        body,
        grid=(s // tm,),
        in_specs=[pl.BlockSpec((tm, d), lambda i: (i, 0)),
                  _resident((f, d)),
                  pl.BlockSpec((tm, f), lambda i: (i, 0))],
        out_specs=[pl.BlockSpec((tm, f), lambda i: (i, 0))],
        out_shape=[jax.ShapeDtypeStruct((s, f), BF16)],
        operands=(dxb, w2, a), name=name, comm=comm)


def _mm_tn(a, b, name, comm=None):
    s, n = a.shape
    d = b.shape[1]
    tn = 512

    def body(a_ref, b_ref, o_ref, acc):
        for k0 in range(0, s, ROW_TILE):
            part = _dot_tn(a_ref[k0:k0 + ROW_TILE, :], b_ref[k0:k0 + ROW_TILE, :])
            if k0 == 0:
                acc[...] = part
            else:
                acc[...] += part
        o_ref[...] = acc[...].astype(BF16)

    return _call(
        body,
        grid=(n // tn,),
        in_specs=[pl.BlockSpec((s, tn), lambda j: (0, j)),
                  _resident((s, d))],
        out_specs=[pl.BlockSpec((tn, d), lambda j: (j, 0))],
        out_shape=[jax.ShapeDtypeStruct((n, d), BF16)],
        operands=(a, b), name=name, scratch_shapes=[pltpu.VMEM((tn, d), F32)], comm=comm)


def _mm_nn_normbwd(dact, wt, x, dres, g, name, comm=None):
    s, kdim = dact.shape
    d = wt.shape[1]
    tm = _row_tile(2 * kdim + 4 * d + 4 * d + 4 * d + 2 * d, 2 * kdim * d)

    def body(a_ref, w_ref, x_ref, r_ref, g_ref, o_ref, ob_ref, dg_ref):
        @pl.when(pl.program_id(0) == 0)
        def _():
            dg_ref[...] = jnp.zeros_like(dg_ref)

        dh = _dot_nn(a_ref[...], w_ref[...])
        xx = x_ref[...]
        r = _rms_scale(xx)
        n = xx * r
        dg_ref[...] += jnp.sum(dh * n, axis=0, keepdims=True)
        dx = r_ref[...] + _rms_bwd(n, r, dh * g_ref[...])
        o_ref[...] = dx
        ob_ref[...] = dx.astype(BF16)

    return _call(
        body,
        grid=(s // tm,),
        in_specs=[pl.BlockSpec((tm, kdim), lambda i: (i, 0)),
                  _resident((kdim, d)),
                  pl.BlockSpec((tm, d), lambda i: (i, 0)),
                  pl.BlockSpec((tm, d), lambda i: (i, 0)),
                  pl.BlockSpec((1, d), lambda i: (0, 0))],
        out_specs=[pl.BlockSpec((tm, d), lambda i: (i, 0)),
                   pl.BlockSpec((tm, d), lambda i: (i, 0)),
                   pl.BlockSpec((1, d), lambda i: (0, 0))],
        out_shape=[jax.ShapeDtypeStruct((s, d), F32), jax.ShapeDtypeStruct((s, d), BF16),
                   jax.ShapeDtypeStruct((1, d), F32)],
        operands=(dact, wt, x, dres, g), name=name, comm=comm)


def _mix_bwd(dx1, wo, ya, yb, yc, gg, name, comm=None):
    s, d = dx1.shape
    tm = 2 * ROW_TILE
    widths = (A_WIDTH, CONV_CH, C_WIDTH)

    def body(dx_ref, w_ref, ya_ref, yb_ref, yc_ref, g_ref, da_ref, db_ref, dc_ref, dg_ref):
        @pl.when(pl.program_id(0) == 0)
        def _():
            dg_ref[...] = jnp.zeros_like(dg_ref)

        dy = _dot_nt(dx_ref[...], w_ref[...])
        gv = g_ref[...]
        off = 0
        dgs = []
        for ref, out, w in zip((ya_ref, yb_ref, yc_ref), (da_ref, db_ref, dc_ref), widths):
            t = ref[...]
            r = _rms_scale(t)
            n = t * r
            dyg = dy[:, off:off + w]
            dgs.append(jnp.sum(dyg * n, axis=0, keepdims=True))
            out[...] = _rms_bwd(n, r, dyg * gv[:, off:off + w])
            off += w
        dg_ref[...] += jnp.concatenate(dgs, axis=1)

    return _call(
        body,
        grid=(s // tm,),
        in_specs=[pl.BlockSpec((tm, d), lambda i: (i, 0)),
                  _resident((d, d)),
                  pl.BlockSpec((tm, A_WIDTH), lambda i: (i, 0)),
                  pl.BlockSpec((tm, CONV_CH), lambda i: (i, 0)),
                  pl.BlockSpec((tm, C_WIDTH), lambda i: (i, 0)),
                  pl.BlockSpec((1, d), lambda i: (0, 0))],
        out_specs=[pl.BlockSpec((tm, A_WIDTH), lambda i: (i, 0)),
                   pl.BlockSpec((tm, CONV_CH), lambda i: (i, 0)),
                   pl.BlockSpec((tm, C_WIDTH), lambda i: (i, 0)),
                   pl.BlockSpec((1, d), lambda i: (0, 0))],
        out_shape=[jax.ShapeDtypeStruct((s, A_WIDTH), F32), jax.ShapeDtypeStruct((s, CONV_CH), F32),
                   jax.ShapeDtypeStruct((s, C_WIDTH), F32), jax.ShapeDtypeStruct((1, d), F32)],
        operands=(dx1, wo, ya, yb, yc, gg), name=name, comm=comm)


CONV_CHUNK = 256
CONV_HALO = 8


def _conv_fwd(z, cw, name):
    s = z.shape[0]
    nch = s // CONV_CHUNK

    def body(gb_ref, gc_ref, xb_ref, w_ref, o_ref, us):
        us[pl.ds(0, CONV_HALO), :] = jnp.zeros((CONV_HALO, LANES), F32)
        us[pl.ds(CONV_HALO, s), :] = gc_ref[...] * xb_ref[...]
        w0, w1, w2 = w_ref[0:1, :], w_ref[1:2, :], w_ref[2:3, :]

        def chunk(c, carry):
            st = pl.multiple_of(c * CONV_CHUNK, CONV_CHUNK)
            ext = us[pl.ds(st, CONV_CHUNK + CONV_HALO), :]
            y = (w0 * ext[CONV_HALO - 2:CONV_HALO - 2 + CONV_CHUNK]
                 + w1 * ext[CONV_HALO - 1:CONV_HALO - 1 + CONV_CHUNK]
                 + w2 * ext[CONV_HALO:])
            o_ref[pl.ds(st, CONV_CHUNK), :] = gb_ref[pl.ds(st, CONV_CHUNK), :] * y
            return carry

        lax.fori_loop(0, nch, chunk, 0)

    col = lambda blk: pl.BlockSpec((s, LANES), lambda j, blk=blk: (0, blk + j))
    return pl.pallas_call(
        body,
        grid=(CONV_CH // LANES,),
        in_specs=[col(GB_BLK), col(GC_BLK), col(XB_BLK), pl.BlockSpec((3, LANES), lambda j: (0, j))],
        out_specs=pl.BlockSpec((s, LANES), lambda j: (0, j)),
        out_shape=jax.ShapeDtypeStruct((s, CONV_CH), F32),
        scratch_shapes=[pltpu.VMEM((s + CONV_HALO, LANES), F32)],
        compiler_params=_params("parallel"),
        name=name,
    )(z, z, z, cw)


def _conv_bwd(z, cw, dyb, dz, name):
    s = z.shape[0]
    nch = s // CONV_CHUNK
    ncol = CONV_CH // LANES

    def body(gb_ref, gc_ref, xb_ref, w_ref, dy_ref, dz_in, dz_ref, dw_ref, us, ds_, dgb_ref, dgc_ref, dxb_ref, sems):
        j = pl.program_id(0)

        def to_dz(staged, blk, k):
            cols = pl.ds(pl.multiple_of((blk + j) * LANES, LANES), LANES)
            return pltpu.make_async_copy(staged, dz_ref.at[:, cols], sems.at[k])

        copies = [to_dz(dgb_ref, GB_BLK, 0), to_dz(dgc_ref, GC_BLK, 1), to_dz(dxb_ref, XB_BLK, 2)]

        @pl.when(j > 0)
        def _():
            for cp in copies:
                cp.wait()

        us[pl.ds(0, CONV_HALO), :] = jnp.zeros((CONV_HALO, LANES), F32)
        us[pl.ds(CONV_HALO, s), :] = gc_ref[...] * xb_ref[...]
        ds_[pl.ds(s, CONV_HALO), :] = jnp.zeros((CONV_HALO, LANES), F32)
        ds_[pl.ds(0, s), :] = dy_ref[...] * gb_ref[...]
        w0, w1, w2 = w_ref[0:1, :], w_ref[1:2, :], w_ref[2:3, :]
        zero = jnp.zeros((1, LANES), F32)

        def chunk(c, carry):
            a0, a1, a2 = carry
            st = pl.multiple_of(c * CONV_CHUNK, CONV_CHUNK)
            rows = pl.ds(st, CONV_CHUNK)
            ext = us[pl.ds(st, CONV_CHUNK + CONV_HALO), :]
            um2 = ext[CONV_HALO - 2:CONV_HALO - 2 + CONV_CHUNK]
            um1 = ext[CONV_HALO - 1:CONV_HALO - 1 + CONV_CHUNK]
            u0 = ext[CONV_HALO:]
            dext = ds_[pl.ds(st, CONV_CHUNK + CONV_HALO), :]
            dc0 = dext[:CONV_CHUNK]
            du = w2 * dc0 + w1 * dext[1:1 + CONV_CHUNK] + w0 * dext[2:2 + CONV_CHUNK]
            yconv = w0 * um2 + w1 * um1 + w2 * u0
            dgb_ref[rows, :] = (dy_ref[rows, :] * yconv).astype(BF16)
            dgc_ref[rows, :] = (du * xb_ref[rows, :]).astype(BF16)
            dxb_ref[rows, :] = (du * gc_ref[rows, :]).astype(BF16)
            a0 = a0 + jnp.sum(dc0 * um2, axis=0, keepdims=True)
            a1 = a1 + jnp.sum(dc0 * um1, axis=0, keepdims=True)
            a2 = a2 + jnp.sum(dc0 * u0, axis=0, keepdims=True)
            return a0, a1, a2

        a0, a1, a2 = lax.fori_loop(0, nch, chunk, (zero, zero, zero))
        dw_ref[...] = jnp.concatenate([a0, a1, a2, jnp.zeros((5, LANES), F32)], axis=0)
        for cp in copies:
            cp.start()

        @pl.when(j == ncol - 1)
        def _():
            for cp in copies:
                cp.wait()

    col = lambda blk: pl.BlockSpec((s, LANES), lambda j, blk=blk: (0, blk + j))
    hbm = pl.BlockSpec(memory_space=pl.ANY)
    return pl.pallas_call(
        body,
        grid=(ncol,),
        in_specs=[col(GB_BLK), col(GC_BLK), col(XB_BLK), pl.BlockSpec((3, LANES), lambda j: (0, j)),
                  pl.BlockSpec((s, LANES), lambda j: (0, j)), hbm],
        out_specs=[hbm, pl.BlockSpec((8, LANES), lambda j: (0, j))],
        out_shape=[jax.ShapeDtypeStruct(dz.shape, dz.dtype), jax.ShapeDtypeStruct((8, CONV_CH), F32)],
        scratch_shapes=[pltpu.VMEM((s + CONV_HALO, LANES), F32), pltpu.VMEM((s + CONV_HALO, LANES), F32)]
        + [pltpu.VMEM((s, LANES), BF16)] * 3 + [pltpu.SemaphoreType.DMA((3,))],
        input_output_aliases={5: 0},
        compiler_params=_params("arbitrary"),
        name=name,
    )(z, z, z, cw, dyb, dz)


ATTN_ROWS = 512
ATTN_UNROLL = 8


def _band_rows(b, d, r):
    base = pl.multiple_of(b * (BLOCK * d), BLOCK)
    prev = jnp.maximum(base - BLOCK * d, 0)
    if d == 1:
        return pl.ds(base, BLOCK), pl.ds(pl.multiple_of(prev, BLOCK), BLOCK)
    return pl.ds(base + r, BLOCK, stride=d), pl.ds(prev + r, BLOCK, stride=d)


def _write_band_bias(bias_ref, max_dist):
    qi = lax.broadcasted_iota(jnp.int32, (BLOCK, 2 * BLOCK), 0)
    kj = lax.broadcasted_iota(jnp.int32, (BLOCK, 2 * BLOCK), 1)
    dist = BLOCK + qi - kj
    band = (dist >= 0) & (dist <= max_dist)
    bias_ref[0:BLOCK, :] = jnp.where(band, 0.0, -jnp.inf)
    bias_ref[BLOCK:2 * BLOCK, :] = jnp.where(band & (kj >= BLOCK), 0.0, -jnp.inf)


def _band_bias(bias_ref, b):
    bias = bias_ref[pl.ds(pl.multiple_of(jnp.where(b > 0, 0, BLOCK), BLOCK), BLOCK), :]
    return jnp.concatenate([bias, bias], axis=0)


def _kv_halves(pair):
    zero = jnp.zeros((1, LANES), jnp.int32)
    return zero + (pair >> 1), zero + ((pair + 1) >> 1)


def _stack_heads(t, head0, halves=None):
    top, bottom = jnp.where(head0, t, 0.0), jnp.where(head0, 0.0, t)
    if halves is not None:
        top = jnp.where(halves[0] == 1, pltpu.roll(top, HEAD_DIM, 1), top)
        bottom = jnp.where(halves[1] == 0, pltpu.roll(bottom, HEAD_DIM, 1), bottom)
    return jnp.concatenate([top, bottom], axis=0).astype(BF16)


def _unstack_heads(t, head0, halves=None):
    top, bottom = t[:BLOCK], t[BLOCK:]
    if halves is not None:
        top = jnp.where(halves[0] == 1, pltpu.roll(top, HEAD_DIM, 1), top)
        bottom = jnp.where(halves[1] == 0, pltpu.roll(bottom, HEAD_DIM, 1), bottom)
    return jnp.where(head0, top, bottom)


def _block_loops(s, patterns, unroll, one_block):
    for n, d in enumerate(patterns):
        nb = (s // BLOCK) // d
        ur = min(unroll, d)
        ub = unroll // ur
        for r0 in range(0, d, ur):
            def trip(i, carry, n=n, d=d, r0=r0, ur=ur, ub=ub):
                for u in range(ub):
                    for r in range(r0, r0 + ur):
                        one_block(i * ub + u, d, r, n == 0)
                return carry
            lax.fori_loop(0, nb // ub, trip, 0)


def _attn_fwd(z, m_init, l_init, q_blk, k_blk, v_blk, patterns, max_dist, gqa, name, comm=None):
    s = z.shape[0]
    npair = 3

    def body(q_ref, k_ref, v_ref, mi_ref, o_ref, lse0_ref, lse1_ref, bias_scr, m_scr, l_scr, *kv_scr):
        head0 = lax.broadcasted_iota(jnp.int32, (1, LANES), 1) < HEAD_DIM
        _write_band_bias(bias_scr, max_dist)
        ones = jnp.ones((2 * BLOCK, LANES), BF16)
        k_src, v_src = kv_scr if gqa else (k_ref, v_ref)
        if gqa:
            half = (lax.broadcasted_iota(jnp.int32, (1, LANES), 1) >= HEAD_DIM).astype(jnp.int32)
            swap = ((pl.program_id(0) + half) >> 1) != half

            def expand(c, carry):
                rows = pl.ds(pl.multiple_of(c * ATTN_ROWS, ATTN_ROWS), ATTN_ROWS)
                k_src[rows, :] = jnp.where(swap, pltpu.roll(k_ref[rows, :], HEAD_DIM, 1), k_ref[rows, :])
                v_src[rows, :] = jnp.where(swap, pltpu.roll(v_ref[rows, :], HEAD_DIM, 1), v_ref[rows, :])
                return carry

            lax.fori_loop(0, s // ATTN_ROWS, expand, 0)

        def one_block(b, d, r, first):
            rq, rp = _band_rows(b, d, r)
            q2 = _stack_heads(q_ref[rq, :] * SCALE, head0)
            k2 = jnp.concatenate([k_src[rp, :], k_src[rq, :]], axis=0).astype(BF16)
            v2 = jnp.concatenate([v_src[rp, :], v_src[rq, :]], axis=0).astype(BF16)
            sc = _dot_nt(q2, k2) + _band_bias(bias_scr, b)
            mb = jnp.max(sc, axis=1, keepdims=True)
            p = jnp.exp(sc - mb).astype(BF16)
            ob = _dot_nn(p, jnp.concatenate([v2, ones], axis=1))
            m_blk = _unstack_heads(jnp.broadcast_to(mb, (2 * BLOCK, LANES)), head0)
            l_blk = _unstack_heads(ob[:, LANES:], head0)
            o_blk = _unstack_heads(ob[:, :LANES], head0)
            if first and l_init == 0.0:
                m_new, l_new, o_new = m_blk, l_blk, o_blk
            else:
                if first:
                    m_old, l_old, o_old = jnp.broadcast_to(mi_ref[...], (BLOCK, LANES)), l_init, 0.0
                else:
                    m_old, l_old, o_old = m_scr[rq, :], l_scr[rq, :], o_ref[rq, :]
                m_new = jnp.maximum(m_old, m_blk)
                a_old = jnp.exp(m_old - m_new)
                a_blk = jnp.exp(m_blk - m_new)
                l_new = l_old * a_old + l_blk * a_blk
                o_new = o_old * a_old + o_blk * a_blk
            o_ref[rq, :], l_scr[rq, :], m_scr[rq, :] = o_new, l_new, m_new

        _block_loops(s, patterns, ATTN_UNROLL, one_block)

        def fin(c, carry):
            rows = pl.ds(pl.multiple_of(c * ATTN_ROWS, ATTN_ROWS), ATTN_ROWS)
            l = l_scr[rows, :]
            o_ref[rows, :] = o_ref[rows, :] / l
            lse = m_scr[rows, :] + jnp.log(l)
            swapped = pltpu.roll(lse, HEAD_DIM, 1)
            lse0_ref[rows, :] = jnp.where(head0, lse, swapped)
            lse1_ref[rows, :] = jnp.where(head0, swapped, lse)
            return carry

        lax.fori_loop(0, s // ATTN_ROWS, fin, 0)

    kv = (lambda blk: pl.BlockSpec((s, LANES), lambda j, blk=blk: (0, blk), pipeline_mode=pl.Buffered(1))) if gqa \
        else (lambda blk: pl.BlockSpec((s, LANES), lambda j, blk=blk: (0, blk + j)))
    own = pl.BlockSpec((s, LANES), lambda j: (0, j))
    return _call(
        body,
        grid=(npair,),
        in_specs=[pl.BlockSpec((s, LANES), lambda j: (0, q_blk + j)), kv(k_blk), kv(v_blk),
                  pl.BlockSpec((1, LANES), lambda j: (0, j))],
        out_specs=[own, own, own],
        out_shape=[jax.ShapeDtypeStruct((s, npair * LANES), F32)] * 3,
        operands=(z, z, z, m_init), name=name,
        scratch_shapes=[pltpu.VMEM((2 * BLOCK, 2 * BLOCK), F32)] + [pltpu.VMEM((s, LANES), F32)] * (4 if gqa else 2),
        comm=comm)


def _attn_bwd(z, do, o, lse, m_init, dz, q_blk, k_blk, v_blk, patterns, max_dist, gqa, name, comm=None):
    s = z.shape[0]
    npair = 3
    n_dz_in = 0 if dz is None else 1

    def body(q_ref, k_ref, v_ref, do_ref, o_ref, lse0_ref, lse1_ref, mi_ref, *rest):
        (dz_ref, dm_ref, dq_acc, dk_acc, dv_acc, dl0_scr, dl1_scr, bias_scr,
         dq_out, dk_out, dv_out, out_sems) = rest[n_dz_in:]
        pair = pl.program_id(0)
        head0 = lax.broadcasted_iota(jnp.int32, (1, LANES), 1) < HEAD_DIM
        halves = _kv_halves(pair) if gqa else None
        _write_band_bias(bias_scr, max_dist)

        def zero_kv():
            def f(c, carry):
                rows = pl.ds(pl.multiple_of(c * ATTN_ROWS, ATTN_ROWS), ATTN_ROWS)
                dk_acc[rows, :] = jnp.zeros((ATTN_ROWS, LANES), F32)
                dv_acc[rows, :] = jnp.zeros((ATTN_ROWS, LANES), F32)
                return carry
            lax.fori_loop(0, s // ATTN_ROWS, f, 0)

        if gqa:
            pl.when(pair == 0)(zero_kv)
        else:
            zero_kv()

        def prep(c, dm):
            rows = pl.ds(pl.multiple_of(c * ATTN_ROWS, ATTN_ROWS), ATTN_ROWS)
            dq_acc[rows, :] = jnp.zeros((ATTN_ROWS, LANES), F32)
            prod = do_ref[rows, :] * o_ref[rows, :]
            d0 = jnp.sum(jnp.where(head0, prod, 0.0), axis=1, keepdims=True)
            d1 = jnp.sum(jnp.where(head0, 0.0, prod), axis=1, keepdims=True)
            dl0_scr[rows, :] = jnp.broadcast_to(d0, (ATTN_ROWS, LANES))
            dl1_scr[rows, :] = jnp.broadcast_to(d1, (ATTN_ROWS, LANES))
            lse_own = jnp.where(head0, lse0_ref[rows, :], lse1_ref[rows, :])
            psink = jnp.exp(mi_ref[...] - lse_own)
            return dm - jnp.sum(psink * jnp.where(head0, d0, d1), axis=0, keepdims=True)

        dm_ref[...] = lax.fori_loop(0, s // ATTN_ROWS, prep, jnp.zeros((1, LANES), F32))

        def one_block(b, d, r, first):
            rq, rp = _band_rows(b, d, r)
            q2 = _stack_heads(q_ref[rq, :] * SCALE, head0, halves)
            do2 = _stack_heads(do_ref[rq, :], head0, halves)
            k2 = jnp.concatenate([k_ref[rp, :], k_ref[rq, :]], axis=0).astype(BF16)
            v2 = jnp.concatenate([v_ref[rp, :], v_ref[rq, :]], axis=0).astype(BF16)
            lse2 = jnp.concatenate([lse0_ref[rq, :], lse1_ref[rq, :]], axis=0)
            dl2 = jnp.concatenate([dl0_scr[rq, :], dl1_scr[rq, :]], axis=0)
            lse2 = jnp.concatenate([lse2, lse2], axis=1)
            dl2 = jnp.concatenate([dl2, dl2], axis=1)
            p = jnp.exp(_dot_nt(q2, k2) + _band_bias(bias_scr, b) - lse2)
            dp = _dot_nt(do2, v2)
            dsc = (p * (dp - dl2)).astype(BF16)
            dq2 = _unstack_heads(_dot_nn(dsc, k2), head0, halves)
            dk2 = _dot_tn(dsc, q2)
            dv2 = _dot_tn(p.astype(BF16), do2)
            dq_acc[rq, :] += dq2 * SCALE
            dk_acc[rp, :] += dk2[:BLOCK]
            dk_acc[rq, :] += dk2[BLOCK:]
            dv_acc[rp, :] += dv2[:BLOCK]
            dv_acc[rq, :] += dv2[BLOCK:]

        _block_loops(s, patterns, ATTN_UNROLL, one_block)

        def to_dz(staged, blk, k):
            cols = pl.ds(pl.multiple_of(blk * LANES, LANES), LANES)
            return pltpu.make_async_copy(staged, dz_ref.at[:, cols], out_sems.at[k])

        last_pair = pair == npair - 1
        q_copy = to_dz(dq_out, q_blk + pair, 0)
        kv_copies = [to_dz(dk_out, k_blk + (0 if gqa else pair), 1), to_dz(dv_out, v_blk + (0 if gqa else pair), 2)]

        @pl.when(pair > 0)
        def _():
            for cp in [q_copy] + ([] if gqa else kv_copies):
                cp.wait()

        def stage(acc, out):
            def f(c, carry):
                rows = pl.ds(pl.multiple_of(c * ATTN_ROWS, ATTN_ROWS), ATTN_ROWS)
                out[rows, :] = acc[rows, :].astype(BF16)
                return carry
            lax.fori_loop(0, s // ATTN_ROWS, f, 0)

        def stage_kv():
            stage(dk_acc, dk_out)
            stage(dv_acc, dv_out)
            for cp in kv_copies:
                cp.start()

        stage(dq_acc, dq_out)
        q_copy.start()
        if gqa:
            pl.when(last_pair)(stage_kv)
        else:
            stage_kv()

        @pl.when(last_pair)
        def _():
            for cp in [q_copy] + kv_copies:
                cp.wait()

    own = pl.BlockSpec((s, LANES), lambda j: (0, j))
    hbm = pl.BlockSpec(memory_space=pl.ANY)
    if gqa:
        kv = lambda blk: pl.BlockSpec((s, LANES), lambda j, blk=blk: (0, blk), pipeline_mode=pl.Buffered(1))
    else:
        kv = lambda blk: pl.BlockSpec((s, LANES), lambda j, blk=blk: (0, blk + j))
    in_specs = [pl.BlockSpec((s, LANES), lambda j: (0, q_blk + j)), kv(k_blk), kv(v_blk), own, own, own, own,
                pl.BlockSpec((1, LANES), lambda j: (0, j))]
    operands = (z, z, z, do, o, lse[0], lse[1], m_init)
    return _call(
        body,
        grid=(npair,),
        in_specs=in_specs + [hbm] * n_dz_in,
        out_specs=[hbm, pl.BlockSpec((1, LANES), lambda j: (0, j))],
        out_shape=[jax.ShapeDtypeStruct((s, IN_WIDTH), BF16), jax.ShapeDtypeStruct((1, npair * LANES), F32)],
        operands=operands + (() if dz is None else (dz,)), name=name,
        scratch_shapes=[pltpu.VMEM((s, LANES), F32)] * 5 + [pltpu.VMEM((2 * BLOCK, 2 * BLOCK), F32)]
        + [pltpu.VMEM((s, LANES), BF16)] * 3 + [pltpu.SemaphoreType.DMA((3,))],
        comm=comm, aliases={} if dz is None else {len(in_specs): 0})


def _adamw_math(w, g, m, v):
    m = ADAM_B1 * m + (1.0 - ADAM_B1) * g
    v = ADAM_B2 * v + (1.0 - ADAM_B2) * (g * g)
    m_hat = m / (1.0 - ADAM_B1 ** ADAM_STEP)
    v_hat = v / (1.0 - ADAM_B2 ** ADAM_STEP)
    delta = -ADAM_LR * (m_hat / (jnp.sqrt(v_hat) + ADAM_EPS) + ADAM_WD * w)
    return delta, m, v


def _adamw(w, g, m, v, name):
    rows, cols = w.shape
    tr = min(rows, 256)

    def body(w_ref, g_ref, m_ref, v_ref, d_ref, nm_ref, nv_ref):
        d_ref[...], nm_ref[...], nv_ref[...] = _adamw_math(w_ref[...], g_ref[...], m_ref[...], v_ref[...])

    spec = pl.BlockSpec((tr, cols), lambda i: (i, 0))
    return pl.pallas_call(
        body,
        grid=(rows // tr,),
        in_specs=[spec] * 4,
        out_specs=[spec] * 3,
        out_shape=[jax.ShapeDtypeStruct((rows, cols), F32)] * 3,
        compiler_params=_params("parallel"),
        name=name,
    )(w, g, m, v)


def _sum_adamw(parts, w, m, v, pos, transpose, name):
    assert len(parts) == DEPTH == 2
    (p0, r0), (p1, r1) = parts
    _, rows, cols = p0.shape
    tr = 256 if rows % 256 == 0 else rows
    nt = rows // tr

    def body(pos_ref, p0_ref, r0_ref, p1_ref, r1_ref, w_ref, m_ref, v_ref, g_ref, d_ref, nm_ref, nv_ref):
        def run(p_ref, r_ref):
            g = ((p_ref[...].astype(F32) + r_ref[0].astype(F32)) + r_ref[1].astype(F32)) + r_ref[2].astype(F32)
            if transpose:
                g = g.T
            g_ref[...] = g
            d_ref[...], nm_ref[...], nv_ref[...] = _adamw_math(w_ref[...], g, m_ref[...], v_ref[...])

        layer0 = pl.program_id(0) < nt
        pl.when(layer0)(lambda: run(p0_ref, r0_ref))
        pl.when(jnp.logical_not(layer0))(lambda: run(p1_ref, r1_ref))

    def tile0(i):
        return jnp.minimum(i, nt - 1)

    def tile1(i):
        return jnp.maximum(i - nt, 0)

    if transpose:
        w_spec = pl.BlockSpec((None, cols, tr), lambda i, q: (i // nt, 0, i % nt))
    else:
        w_spec = pl.BlockSpec((None, tr, cols), lambda i, q: (i // nt, i % nt, 0))
    return pl.pallas_call(
        body,
        grid_spec=pltpu.PrefetchScalarGridSpec(
            num_scalar_prefetch=1,
            grid=(DEPTH * nt,),
            in_specs=[pl.BlockSpec((None, tr, cols), lambda i, q: (q[0], tile0(i), 0)),
                      pl.BlockSpec((3, tr, cols), lambda i, q: (0, tile0(i), 0)),
                      pl.BlockSpec((None, tr, cols), lambda i, q: (q[0], tile1(i), 0)),
                      pl.BlockSpec((3, tr, cols), lambda i, q: (0, tile1(i), 0)),
                      w_spec, w_spec, w_spec],
            out_specs=[w_spec] * 4,
        ),
        out_shape=[jax.ShapeDtypeStruct(w.shape, F32)] * 4,
        compiler_params=_params("arbitrary"),
        name=name,
    )(pos, p0, r0, p1, r1, w, m, v)


def _small_sum_adamw(gathered, params, name):
    _, rows, cols = gathered.shape
    n = len(params)

    def body(ga_ref, *refs):
        ins, outs, (g_scr,) = refs[:3 * n], refs[3 * n:7 * n + 2], refs[7 * n + 2:]
        g = ga_ref[0]
        for i in range(1, N_DEV):
            g = g + ga_ref[i]
        g_scr[...] = g
        for k, (row0, w, _, _) in enumerate(params):
            w_ref, m_ref, v_ref = ins[3 * k:3 * k + 3]
            gk = g_scr[row0:row0 + w.shape[0], :]
            outs[4 * k][...] = gk
            outs[4 * k + 1][...], outs[4 * k + 2][...], outs[4 * k + 3][...] = _adamw_math(
                w_ref[...], gk, m_ref[...], v_ref[...])
        outs[4 * n][...] = g_scr[CONV_ROW:CONV_ROW + 8, :]
        outs[4 * n + 1][...] = g_scr[LOSS_ROW:LOSS_ROW + 1, :]

    out_shape = []
    for _, w, _, _ in params:
        out_shape += [jax.ShapeDtypeStruct(w.shape, F32)] * 4
    out_shape += [jax.ShapeDtypeStruct((8, cols), F32), jax.ShapeDtypeStruct((1, cols), F32)]
    res = pl.pallas_call(
        body,
        out_shape=out_shape,
        scratch_shapes=[pltpu.VMEM((rows, cols), F32)],
        name=name,
    )(gathered, *[t for _, w, m, v in params for t in (w, m, v)])
    return [res[4 * k:4 * k + 4] for k in range(n)], res[4 * n], res[4 * n + 1]


def _pair_sum(g4, r1, pos, name):
    _, _, rows, cols = g4.shape
    tr = min(rows, 512)

    def body(pos_ref, g_ref, r_ref, o_ref):
        o_ref[...] = (g_ref[...].astype(F32) + r_ref[...].astype(F32)).astype(BF16)

    return pl.pallas_call(
        body,
        grid_spec=pltpu.PrefetchScalarGridSpec(
            num_scalar_prefetch=1,
            grid=(4, rows // tr),
            in_specs=[pl.BlockSpec((None, None, tr, cols), lambda i, j, p: (i, p[1], j, 0)),
                      pl.BlockSpec((None, tr, cols), lambda i, j, p: (i, j, 0))],
            out_specs=pl.BlockSpec((None, tr, cols), lambda i, j, p: (i, j, 0)),
        ),
        out_shape=jax.ShapeDtypeStruct((4, rows, cols), BF16),
        compiler_params=_params("parallel", "parallel"),
        name=name,
    )(pos, g4, r1)


def _place():
    return lax.axis_index("x"), lax.axis_index("y"), lax.axis_index("c")


def _gather_comm(shards):
    na = len(shards)

    def plan(ins, outs, sems):
        send_sems, recv_sems, local_sems = sems
        x, y, c = _place()
        me, sibling = (x, y, c), (x, y, 1 - c)
        chips = [(1 - x, y), (x, 1 - y), (1 - x, 1 - y)]

        def rows(a, px, py, pc):
            m = ins[a].shape[0]
            return outs[a].at[pl.ds((4 * px + 2 * py + pc) * m, m), :]

        def copy(a, k, block, to, src=None):
            return pltpu.make_async_remote_copy(
                src_ref=rows(a, *block) if src is None else src, dst_ref=rows(a, *block),
                send_sem=send_sems.at[a, k], recv_sem=recv_sems.at[a, k], device_id=to, device_id_type=MESH)

        mine = [pltpu.make_async_copy(ins[a], rows(a, *me), local_sems.at[a]) for a in range(na)]
        first = []
        for a in range(na):
            first.append(copy(a, 0, me, sibling, src=ins[a]))
            first += [copy(a, 1 + j, me, (*chip, c), src=ins[a]) for j, chip in enumerate(chips)]
        return me, sibling, chips, c, copy, mine, first

    def start(ins, outs, sems):
        *_, mine, first = plan(ins, outs, sems)
        for cp in mine + first:
            cp.start()

    def finish(ins, outs, sems):
        me, sibling, chips, c, copy, mine, first = plan(ins, outs, sems)
        passed = []
        for j, chip in enumerate(chips):
            for a in range(na):
                copy(a, 1 + j, (*chip, c), me).wait_recv()
                cp = copy(a, 4 + j, (*chip, c), sibling)
                cp.start()
                passed.append(cp)
        for a in range(na):
            copy(a, 0, sibling, me).wait_recv()
            for j, chip in enumerate(chips):
                copy(a, 4 + j, (*chip, 1 - c), me).wait_recv()
        for cp in first + passed:
            cp.wait_send()
        for cp in mine:
            cp.wait()

    return _Comm(tuple(shards),
                 tuple(jax.ShapeDtypeStruct((N_DEV * t.shape[0], t.shape[1]), t.dtype) for t in shards),
                 (pltpu.SemaphoreType.DMA((na, 7)), pltpu.SemaphoreType.DMA((na, 7)), pltpu.SemaphoreType.DMA((na,))),
                 start, finish)


def _exchange_comm(arrays, out_shape, n_copies, copies_of):
    na = len(arrays)

    def every(ins, outs, sems):
        send_sems, recv_sems = sems
        return [cp for a in range(na) for cp in copies_of(ins, outs, a, send_sems, recv_sems)]

    def start(ins, outs, sems):
        for cp in every(ins, outs, sems):
            cp.start()

    def finish(ins, outs, sems):
        for cp in every(ins, outs, sems):
            cp.wait()

    return _Comm(tuple(arrays), tuple(out_shape),
                 (pltpu.SemaphoreType.DMA((na, n_copies)), pltpu.SemaphoreType.DMA((na, n_copies))), start, finish)


def _sibling_comm(grads):
    def copies_of(ins, outs, a, send_sems, recv_sems):
        x, y, c = _place()
        return [pltpu.make_async_remote_copy(
            src_ref=ins[a].at[chip, 1 - c], dst_ref=outs[a].at[chip],
            send_sem=send_sems.at[a, chip], recv_sem=recv_sems.at[a, chip],
            device_id=(x, y, 1 - c), device_id_type=MESH) for chip in range(4)]

    return _exchange_comm(grads, [jax.ShapeDtypeStruct((4,) + t.shape[2:], t.dtype) for t in grads], 4, copies_of)


def _chip_comm(partials):
    def copies_of(ins, outs, a, send_sems, recv_sems):
        x, y, c = _place()
        chips = [(1 - x, y), (x, 1 - y), (1 - x, 1 - y)]
        return [pltpu.make_async_remote_copy(
            src_ref=ins[a].at[2 * cx + cy], dst_ref=outs[a].at[k],
            send_sem=send_sems.at[a, k], recv_sem=recv_sems.at[a, k],
            device_id=(cx, cy, c), device_id_type=MESH) for k, (cx, cy) in enumerate(chips)]

    return _exchange_comm(partials, [jax.ShapeDtypeStruct((3,) + t.shape[1:], t.dtype) for t in partials], 3, copies_of)


def _pad_rows(t, rows):
    return jnp.pad(t, ((0, rows - t.shape[0]), (0, D_MODEL - t.shape[1])))


MIX_ROW, GROUP_ROW, MLP_ROW, FINAL_ROW, CONV_ROW, SINK_ROW = 0, 8, 16, 24, 32, 40
LOSS_ROW = FINAL_ROW + 1


def _pack_small(g_mix, g_group, g_mlp, g_final, conv, sinks, loss):
    final_and_loss = jnp.concatenate([g_final.reshape(1, D_MODEL), _pad_rows(loss, 1)], axis=0)
    return jnp.concatenate([
        _pad_rows(g_mix, 8), _pad_rows(g_group, 8), _pad_rows(g_mlp, 8), _pad_rows(final_and_loss, 8),
        _pad_rows(conv.reshape(DEPTH * 3, CONV_CH), 8), _pad_rows(sinks.reshape(1, DEPTH * 6), 8)], axis=0)


def kernel(x, w_in, conv_w, sinks, g_mix, g_group, w_o, g_mlp, w_ff_in, w_ff_out, g_final, loss_target, m_w_in, m_conv_w, m_sinks, m_g_mix, m_g_group, m_w_o, m_g_mlp, m_w_ff_in, m_w_ff_out, m_g_final, v_w_in, v_conv_w, v_sinks, v_g_mix, v_g_group, v_w_o, v_g_mlp, v_w_ff_in, v_w_ff_out, v_g_final):
    ax, ay, ac = _place()
    chip = 2 * ax + ay
    dev = 4 * ax + 2 * ay + ac
    pos = jnp.stack([chip, ac]).astype(jnp.int32)

    x0 = x.reshape(SEQ, D_MODEL)
    target = loss_target.reshape(SEQ, D_MODEL)

    shards = {}
    for l in range(DEPTH):
        shards[l, 0], shards[l, 1] = w_in[l].T.astype(BF16), w_o[l].astype(BF16)
        shards[l, 2], shards[l, 3] = w_ff_in[l].T.astype(BF16), w_ff_out[l].astype(BF16)
    conv_tile = jnp.pad(conv_w.reshape(DEPTH * 3, CONV_CH // N_DEV), ((0, 2), (0, LANES - CONV_CH // N_DEV)))
    wt_in0, conv_all = _comm_only(_gather_comm([shards[0, 0], conv_tile]), "gather_first")
    conv_full = conv_all.reshape(N_DEV, 8, LANES)[:, :DEPTH * 3, :CONV_CH // N_DEV]
    conv_full = conv_full.transpose(1, 0, 2).reshape(DEPTH, 3, CONV_CH)

    dx, parts, small = _step(x0, target, shards, wt_in0, conv_full, sinks, g_mix, g_group, g_mlp, g_final, pos)
    return _finish(dx, parts, small, pos, dev, w_in, conv_w, sinks, g_mix, g_group, w_o, g_mlp, w_ff_in, w_ff_out, g_final, m_w_in, m_conv_w, m_sinks, m_g_mix, m_g_group, m_w_o, m_g_mlp, m_w_ff_in, m_w_ff_out, m_g_final, v_w_in, v_conv_w, v_sinks, v_g_mix, v_g_group, v_w_o, v_g_mlp, v_w_ff_in, v_w_ff_out, v_g_final)


FWD_CARRY = {(0, "in_proj"): ((0, 1),), (0, "window"): ((1, 0),), (0, "dilated"): ((0, 2),),
             (0, "mix_out"): ((1, 1),), (0, "ff_in"): ((0, 3),), (0, "ff_out"): ((1, 3),),
             (1, "dilated"): ((1, 2),)}


def _step(x0, target, shards, wt_in0, conv_full, sinks, g_mix, g_group, g_mlp, g_final, pos):
    sink_lanes = jnp.repeat(sinks.reshape(DEPTH, 6), HEAD_DIM, axis=1)
    no_sink = jnp.full((1, A_WIDTH), NEG_BIG, F32)
    full = {(0, 0): wt_in0}

    def gather(stage, l):
        keys = FWD_CARRY.get((l, stage), ())
        return keys, (_gather_comm([shards[k] for k in keys]) if keys else None)

    def landed(keys, got):
        full.update(zip(keys, got))

    saved = []
    xc = x0
    for l in range(DEPTH):
        keys, comm = gather("in_proj", l)
        (z, h), got = _norm_mm(xc, g_mix[l:l + 1], full[l, 0], False, f"in_proj_{l}", comm)
        landed(keys, got)
        sink_l = sink_lanes[l:l + 1]
        keys, comm = gather("window", l)
        (yc, *lse_c), got = _attn_fwd(z, sink_l, 1.0, QC_BLK, KC_BLK, VC_BLK, (1,), C_MAX_DIST, True,
                                     f"window_attn_{l}", comm)
        landed(keys, got)
        yb = _conv_fwd(z, conv_full[l], f"conv_{l}")
        keys, comm = gather("dilated", l)
        (ya, *lse_a), got = _attn_fwd(z, no_sink, 0.0, QA_BLK, KA_BLK, VA_BLK, DILATED_PATTERNS, A_MAX_DIST, False,
                                     f"dilated_attn_{l}", comm)
        landed(keys, got)
        keys, comm = gather("mix_out", l)
        (y, x1), got = _mix_out(ya, yb, yc, g_group[l:l + 1], full[l, 1], xc, f"mix_out_{l}", comm)
        landed(keys, got)
        keys, comm = gather("ff_in", l)
        (a, h2), got = _norm_mm(x1, g_mlp[l:l + 1], full[l, 2], True, f"ff_in_{l}", comm)
        landed(keys, got)
        keys, comm = gather("ff_out", l)
        (x2,), got = _mm_res(a, full[l, 3], x1, f"ff_out_{l}", comm)
        landed(keys, got)
        saved.append((xc, z, h, ya, lse_a, yb, yc, lse_c, sink_l, y, x1, a, h2))
        xc = x2

    loss_slab, dx, dxb, dg_final = _loss_head(xc, g_final.reshape(1, D_MODEL), target, "loss_head")

    def by_owner(t):
        return t.reshape(4, 2, t.shape[0] // N_DEV, D_MODEL)

    def pair(key, g, r1):
        return _pair_sum(g, r1, pos, f"grad_pair_sum_{key[0]}_{key[1]}")

    partial, r2 = {}, {}
    dg_mix, dg_group, dg_mlp, dconv, dsinks = [None] * DEPTH, [None] * DEPTH, [None] * DEPTH, [None] * DEPTH, [None] * DEPTH
    for l in reversed(range(DEPTH)):
        xin, z, h, ya, lse_a, yb, yc, lse_c, sink_l, y, x1, a, h2 = saved[l]
        late = [(l + 1, 1), (l + 1, 0)] if l + 1 < DEPTH else []
        (du,), got = _mlp_bwd_act(dxb, full[l, 3], a, f"ff_out_bwd_{l}",
                                  _chip_comm([partial[k] for k in late]) if late else None)
        r2.update(zip(late, got))
        (g3,), _ = _mm_tn(a, dxb, f"grad_w_ff_out_{l}")
        (g2,), _ = _mm_tn(du, h2, f"grad_w_ff_in_{l}")
        g3, g2 = by_owner(g3), by_owner(g2)
        (dx1, dx1b, dg_mlp[l]), got = _mm_nn_normbwd(du, full[l, 2], x1, dx, g_mlp[l:l + 1], f"ff_in_bwd_{l}",
                                                    _sibling_comm([g3, g2]))
        partial[l, 3], partial[l, 2] = pair((l, 3), g3, got[0]), pair((l, 2), g2, got[1])
        (g1,), _ = _mm_tn(y, dx1b, f"grad_w_o_{l}")
        g1 = by_owner(g1)
        (dya, dyb, dyc, dg_group[l]), got = _mix_bwd(dx1b, full[l, 1], ya, yb, yc, g_group[l:l + 1],
                                                     f"mix_out_bwd_{l}", _sibling_comm([g1]) if l == 0 else None)
        if l == 0:
            partial[l, 1] = pair((l, 1), g1, got[0])
        early = [(l, 3), (l, 2)] + ([(l, 1)] if l == 0 else [])
        (dz, _), got = _attn_bwd(z, dya, ya, lse_a, no_sink, None, QA_BLK, KA_BLK, VA_BLK, DILATED_PATTERNS,
                                 A_MAX_DIST, False, f"dilated_attn_bwd_{l}", _chip_comm([partial[k] for k in early]))
        r2.update(zip(early, got))
        dz, dcw = _conv_bwd(z, conv_full[l], dyb, dz, f"conv_bwd_{l}")
        (dz, dsink), _ = _attn_bwd(z, dyc, yc, lse_c, sink_l, dz, QC_BLK, KC_BLK, VC_BLK, (1,), C_MAX_DIST,
                                   True, f"window_attn_bwd_{l}")
        (g0,), _ = _mm_tn(dz, h, f"grad_w_in_{l}")
        g0 = by_owner(g0)
        if l > 0:
            (dx, dxb, dg_mix[l]), got = _mm_nn_normbwd(dz, full[l, 0], xin, dx1, g_mix[l:l + 1], f"in_proj_bwd_{l}",
                                                      _sibling_comm([g1, g0]))
            partial[l, 1], partial[l, 0] = pair((l, 1), g1, got[0]), pair((l, 0), g0, got[1])
        else:
            (r1,) = _comm_only(_sibling_comm([g0]), "grad_sibling_exchange_last")
            partial[l, 0] = pair((l, 0), g0, r1)
            (dx, dxb, dg_mix[l]), got = _mm_nn_normbwd(dz, full[l, 0], xin, dx1, g_mix[l:l + 1], f"in_proj_bwd_{l}",
                                                      _chip_comm([partial[l, 0]]))
            r2[l, 0] = got[0]
        dconv[l] = dcw[:3]
        dsinks[l] = dsink[0, ::HEAD_DIM]
    parts = {key: (partial[key], r2[key]) for key in partial}
    small = _pack_small(jnp.concatenate(dg_mix), jnp.concatenate(dg_group), jnp.concatenate(dg_mlp),
                        dg_final, jnp.stack(dconv), jnp.stack(dsinks), loss_slab[0:1])
    return dx, parts, small


def _finish(dx, parts, small, pos, dev, w_in, conv_w, sinks, g_mix, g_group, w_o, g_mlp, w_ff_in, w_ff_out, g_final, m_w_in, m_conv_w, m_sinks, m_g_mix, m_g_group, m_w_o, m_g_mlp, m_w_ff_in, m_w_ff_out, m_g_final, v_w_in, v_conv_w, v_sinks, v_g_mix, v_g_group, v_w_o, v_g_mlp, v_w_ff_in, v_w_ff_out, v_g_final):
    grad_x = dx.reshape(1, SEQ, D_MODEL)

    (small_all,) = _comm_only(_gather_comm([small]), "gather_small_grads")
    row = lambda t: t.reshape(1, D_MODEL)
    sink_row = lambda t: _pad_rows(t.reshape(1, DEPTH * 6), 1)
    params = [(MIX_ROW, g_mix, m_g_mix, v_g_mix), (GROUP_ROW, g_group, m_g_group, v_g_group),
              (MLP_ROW, g_mlp, m_g_mlp, v_g_mlp), (FINAL_ROW, row(g_final), row(m_g_final), row(v_g_final)),
              (SINK_ROW, sink_row(sinks), sink_row(m_sinks), sink_row(v_sinks))]
    updated, conv_rows, loss_row = _small_sum_adamw(small_all.reshape(N_DEV, SMALL_ROWS, D_MODEL), params, "small_adamw")
    loss = loss_row[0, 0]
    (grad_g_mix, delta_g_mix, new_m_g_mix, new_v_g_mix), (grad_g_group, delta_g_group, new_m_g_group, new_v_g_group), \
        (grad_g_mlp, delta_g_mlp, new_m_g_mlp, new_v_g_mlp), final4, sinks4 = updated
    grad_g_final, delta_g_final, new_m_g_final, new_v_g_final = [t.reshape(D_MODEL) for t in final4]
    grad_sinks, delta_sinks, new_m_sinks, new_v_sinks = [t[0, :DEPTH * 6].reshape(DEPTH, 2, 3) for t in sinks4]
    conv_grad_full = conv_rows[:DEPTH * 3, :CONV_CH].reshape(DEPTH, 3, CONV_CH)
    cs = CONV_CH // N_DEV
    grad_conv_w = lax.dynamic_slice_in_dim(conv_grad_full, dev * cs, cs, axis=2)

    def tile_of(t):
        return jnp.pad(t.reshape(1, DEPTH * 3 * cs), ((0, 7), (0, 256 - DEPTH * 3 * cs)))

    cd, cm, cv = _adamw(tile_of(conv_w), tile_of(grad_conv_w), tile_of(m_conv_w), tile_of(v_conv_w), "conv_adamw")
    untile = lambda t: t[0, :DEPTH * 3 * cs].reshape(DEPTH, 3, cs)
    delta_conv_w, new_m_conv_w, new_v_conv_w = untile(cd), untile(cm), untile(cv)

    def big(kind, w, m, v, transpose, name):
        return _sum_adamw([parts[l, kind] for l in range(DEPTH)], w, m, v, pos, transpose, name)

    grad_w_in, delta_w_in, new_m_w_in, new_v_w_in = big(0, w_in, m_w_in, v_w_in, True, "adamw_w_in")
    grad_w_o, delta_w_o, new_m_w_o, new_v_w_o = big(1, w_o, m_w_o, v_w_o, False, "adamw_w_o")
    grad_w_ff_in, delta_w_ff_in, new_m_w_ff_in, new_v_w_ff_in = big(2, w_ff_in, m_w_ff_in, v_w_ff_in, True, "adamw_w_ff_in")
    grad_w_ff_out, delta_w_ff_out, new_m_w_ff_out, new_v_w_ff_out = big(3, w_ff_out, m_w_ff_out, v_w_ff_out, False,
                                                                         "adamw_w_ff_out")

    return (loss, grad_x, grad_w_in, grad_conv_w, grad_sinks, grad_g_mix, grad_g_group, grad_w_o, grad_g_mlp,
            grad_w_ff_in, grad_w_ff_out, grad_g_final,
            delta_w_in, delta_conv_w, delta_sinks, delta_g_mix, delta_g_group, delta_w_o, delta_g_mlp,
            delta_w_ff_in, delta_w_ff_out, delta_g_final,
            new_m_w_in, new_m_conv_w, new_m_sinks, new_m_g_mix, new_m_g_group, new_m_w_o, new_m_g_mlp,
            new_m_w_ff_in, new_m_w_ff_out, new_m_g_final,
            new_v_w_in, new_v_conv_w, new_v_sinks, new_v_g_mix, new_v_g_group, new_v_w_o, new_v_g_mlp,
            new_v_w_ff_in, new_v_w_ff_out, new_v_g_final)
```

```python
from typing import Callable, NamedTuple

import jax
import jax.numpy as jnp
from jax import lax
from jax.experimental import pallas as pl
from jax.experimental.pallas import tpu as pltpu

F32 = jnp.float32
BF16 = jnp.bfloat16
MESH = pl.DeviceIdType.MESH

N_DEV = 8
SEQ = 4096
D_MODEL = 1024
DEPTH = 2
HEAD_DIM = 64
LANES = 128
A_WIDTH = 384
CONV_CH = 256
C_WIDTH = 384
KV_WIDTH = 128
IN_WIDTH = 2560
D_FF = 4096
BLOCK = 128
DILATED_PATTERNS = (1, 4, 16)
A_MAX_DIST = 128
C_MAX_DIST = 127
EPS = 1e-6
SCALE = HEAD_DIM ** -0.5
NEG_BIG = -1e30
F32_TINY = 1.1754944e-38

QA_BLK, KA_BLK, VA_BLK = 0, 3, 6
GB_BLK, GC_BLK, XB_BLK = 9, 11, 13
QC_BLK, KC_BLK, VC_BLK = 15, 18, 19

ADAM_LR = 0.001
ADAM_B1 = 0.9
ADAM_B2 = 0.999
ADAM_EPS = 1e-08
ADAM_WD = 0.01
ADAM_STEP = 10

VMEM_LIMIT = 56 * 1024 * 1024
TILE_BUDGET = 46 * 1024 * 1024
ROW_TILE = 512
COL_CHUNK = 512
SMALL_ROWS = 48


def _dot_nn(a, b):
    return lax.dot_general(a, b, (((1,), (0,)), ((), ())), preferred_element_type=F32)


def _dot_nt(a, b):
    return lax.dot_general(a, b, (((1,), (1,)), ((), ())), preferred_element_type=F32)


def _dot_tn(a, b):
    return lax.dot_general(a, b, (((0,), (0,)), ((), ())), preferred_element_type=F32)


def _params(*sem):
    return pltpu.CompilerParams(dimension_semantics=sem, vmem_limit_bytes=VMEM_LIMIT)


def _resident(shape):
    return pl.BlockSpec(shape, lambda i: (0,) * len(shape), pipeline_mode=pl.Buffered(1))


def _row_tile(row_bytes, resident_bytes):
    for tm in (ROW_TILE, ROW_TILE // 2):
        if 2 * tm * row_bytes + resident_bytes <= TILE_BUDGET:
            return tm
    return ROW_TILE // 4


def _rms_scale(t):
    return lax.rsqrt(jnp.mean(t * t, axis=-1, keepdims=True) + EPS)


def _rms_bwd(n, r, dn):
    return r * (dn - n * jnp.mean(dn * n, axis=-1, keepdims=True))


class _Comm(NamedTuple):
    arrays: tuple
    out_shape: tuple
    sems: tuple
    start: Callable
    finish: Callable


def _call(body, grid, in_specs, out_specs, out_shape, operands, name, scratch_shapes=(), comm=None, aliases=None):
    n_in, n_out, n_scr = len(in_specs), len(out_shape), len(scratch_shapes)
    aliases = dict(aliases or {})
    if comm is None:
        res = pl.pallas_call(body, grid=grid, in_specs=list(in_specs), out_specs=list(out_specs),
                             out_shape=list(out_shape), scratch_shapes=list(scratch_shapes),
                             input_output_aliases=aliases,
                             compiler_params=_params("arbitrary"), name=name)(*operands)
        return list(res), []
    c_in, c_out = len(comm.arrays), len(comm.out_shape)
    hbm = pl.BlockSpec(memory_space=pl.ANY)
    last = grid[0] - 1

    def carried(*refs):
        ins, cins = refs[:n_in], refs[n_in:n_in + c_in]
        o0 = n_in + c_in
        outs, couts = refs[o0:o0 + n_out], refs[o0 + n_out:o0 + n_out + c_out]
        s0 = o0 + n_out + c_out
        scr, sems = refs[s0:s0 + n_scr], refs[s0 + n_scr:]
        pl.when(pl.program_id(0) == 0)(lambda: comm.start(cins, couts, sems))
        body(*ins, *outs, *scr)
        pl.when(pl.program_id(0) == last)(lambda: comm.finish(cins, couts, sems))

    res = pl.pallas_call(carried, grid=grid, in_specs=list(in_specs) + [hbm] * c_in,
                         out_specs=list(out_specs) + [hbm] * c_out, out_shape=list(out_shape) + list(comm.out_shape),
                         scratch_shapes=list(scratch_shapes) + list(comm.sems), input_output_aliases=aliases,
                         compiler_params=_params("arbitrary"), name=name)(*operands, *comm.arrays)
    return list(res[:n_out]), list(res[n_out:])


def _comm_only(comm, name):
    hbm = pl.BlockSpec(memory_space=pl.ANY)
    c_in, c_out = len(comm.arrays), len(comm.out_shape)

    def body(*refs):
        ins, outs, sems = refs[:c_in], refs[c_in:c_in + c_out], refs[c_in + c_out:]
        comm.start(ins, outs, sems)
        comm.finish(ins, outs, sems)

    return pl.pallas_call(body, in_specs=[hbm] * c_in, out_specs=[hbm] * c_out, out_shape=list(comm.out_shape),
                          scratch_shapes=list(comm.sems), name=name)(*comm.arrays)


def _norm_mm(x, g, wt, relu2, name, comm=None):
    s, d = x.shape
    n = wt.shape[0]
    tm = _row_tile(4 * d + (2 if relu2 else 4) * n + 2 * d, 2 * n * d)

    def body(x_ref, g_ref, w_ref, o_ref, h_ref):
        xx = x_ref[...]
        h = ((xx * _rms_scale(xx)) * g_ref[...]).astype(BF16)
        h_ref[...] = h
        for n0 in range(0, n, COL_CHUNK):
            zc = _dot_nt(h, w_ref[n0:n0 + COL_CHUNK, :])
            if relu2:
                zc = jnp.square(jnp.maximum(zc, 0.0)).astype(BF16)
            o_ref[:, n0:n0 + COL_CHUNK] = zc

    return _call(
        body,
        grid=(s // tm,),
        in_specs=[pl.BlockSpec((tm, d), lambda i: (i, 0)),
                  pl.BlockSpec((1, d), lambda i: (0, 0)),
                  _resident((n, d))],
        out_specs=[pl.BlockSpec((tm, n), lambda i: (i, 0)),
                   pl.BlockSpec((tm, d), lambda i: (i, 0))],
        out_shape=[jax.ShapeDtypeStruct((s, n), BF16 if relu2 else F32), jax.ShapeDtypeStruct((s, d), BF16)],
        operands=(x, g, wt), name=name, comm=comm)


def _mm_res(a, w2, x1, name, comm=None):
    s, f = a.shape
    d = w2.shape[1]
    tm = _row_tile(2 * f + 4 * d + 4 * d, 2 * f * d)

    def body(a_ref, w_ref, x_ref, o_ref):
        o_ref[...] = x_ref[...] + _dot_nn(a_ref[...], w_ref[...])

    return _call(
        body,
        grid=(s // tm,),
        in_specs=[pl.BlockSpec((tm, f), lambda i: (i, 0)),
                  _resident((f, d)),
                  pl.BlockSpec((tm, d), lambda i: (i, 0))],
        out_specs=[pl.BlockSpec((tm, d), lambda i: (i, 0))],
        out_shape=[jax.ShapeDtypeStruct((s, d), F32)],
        operands=(a, w2, x1), name=name, comm=comm)


def _mix_out(ya, yb, yc, gg, wo, x0, name, comm=None):
    s = ya.shape[0]
    d = wo.shape[1]
    tm = 2 * ROW_TILE

    def body(ya_ref, yb_ref, yc_ref, g_ref, w_ref, x_ref, y_ref, o_ref):
        parts = []
        for ref in (ya_ref, yb_ref, yc_ref):
            t = ref[...]
            parts.append(t * _rms_scale(t))
        y = (jnp.concatenate(parts, axis=1) * g_ref[...]).astype(BF16)
        y_ref[...] = y
        o_ref[...] = x_ref[...] + _dot_nn(y, w_ref[...])

    return _call(
        body,
        grid=(s // tm,),
        in_specs=[pl.BlockSpec((tm, A_WIDTH), lambda i: (i, 0)),
                  pl.BlockSpec((tm, CONV_CH), lambda i: (i, 0)),
                  pl.BlockSpec((tm, C_WIDTH), lambda i: (i, 0)),
                  pl.BlockSpec((1, d), lambda i: (0, 0)),
                  _resident((d, d)),
                  pl.BlockSpec((tm, d), lambda i: (i, 0))],
        out_specs=[pl.BlockSpec((tm, d), lambda i: (i, 0)),
                   pl.BlockSpec((tm, d), lambda i: (i, 0))],
        out_shape=[jax.ShapeDtypeStruct((s, d), BF16), jax.ShapeDtypeStruct((s, d), F32)],
        operands=(ya, yb, yc, gg, wo, x0), name=name, comm=comm)


def _mm_res_loss(a, w2, x1, g, target, name):
    s, f = a.shape
    d = w2.shape[1]
    tm = _row_tile(2 * f + 4 * d + 4 * d + 4 * d + 2 * d, 2 * f * d)

    def body(a_ref, w_ref, x_ref, g_ref, t_ref, loss_ref, dx_ref, dxb_ref, dg_ref):
        @pl.when(pl.program_id(0) == 0)
        def _():
            loss_ref[...] = jnp.zeros_like(loss_ref)
            dg_ref[...] = jnp.zeros_like(dg_ref)

        xx = x_ref[...] + _dot_nn(a_ref[...], w_ref[...])
        r = _rms_scale(xx)
        n = xx * r
        gv = g_ref[...]
        err = n * gv - t_ref[...]
        per_tok = jnp.sum(err * err, axis=1, keepdims=True) * (1.0 / d)
        loss_ref[...] += 0.5 * jnp.sum(per_tok, axis=0, keepdims=True)
        dout = err * (1.0 / d)
        dg_ref[...] += jnp.sum(dout * n, axis=0, keepdims=True)
        dx = _rms_bwd(n, r, dout * gv)
        dx_ref[...] = dx
        dxb_ref[...] = dx.astype(BF16)

    return pl.pallas_call(
        body,
        grid=(s // tm,),
        in_specs=[pl.BlockSpec((tm, f), lambda i: (i, 0)),
                  _resident((f, d)),
                  pl.BlockSpec((tm, d), lambda i: (i, 0)),
                  pl.BlockSpec((1, d), lambda i: (0, 0)),
                  pl.BlockSpec((tm, d), lambda i: (i, 0))],
        out_specs=[pl.BlockSpec((8, LANES), lambda i: (0, 0)),
                   pl.BlockSpec((tm, d), lambda i: (i, 0)),
                   pl.BlockSpec((tm, d), lambda i: (i, 0)),
                   pl.BlockSpec((1, d), lambda i: (0, 0))],
        out_shape=[jax.ShapeDtypeStruct((8, LANES), F32), jax.ShapeDtypeStruct((s, d), F32),
                   jax.ShapeDtypeStruct((s, d), BF16), jax.ShapeDtypeStruct((1, d), F32)],
        compiler_params=_params("arbitrary"),
        name=name,
    )(a, w2, x1, g, target)


def _mlp_bwd_act(dxb, w2, a, name, comm=None):
    s, d = dxb.shape
    f = w2.shape[0]
    tm = _row_tile(2 * d + 2 * f + 2 * f, 2 * f * d)

    def body(dx_ref, w_ref, a_ref, du_ref):
        dx = dx_ref[...]
        for n0 in range(0, f, COL_CHUNK):
            da = _dot_nt(dx, w_ref[n0:n0 + COL_CHUNK, :])
            av = a_ref[:, n0:n0 + COL_CHUNK].astype(F32)
            rl = av * lax.rsqrt(jnp.maximum(av, F32_TINY))
            du_ref[:, n0:n0 + COL_CHUNK] = (da * (2.0 * rl)).astype(BF16)

    return _call(
        body,
        grid=(s // tm,),
        in_specs=[pl.BlockSpec((tm, d), lambda i: (i, 0)),
                  _resident((f, d)),
                  pl.BlockSpec((tm, f), lambda i: (i, 0))],
        out_specs=[pl.BlockSpec((tm, f), lambda i: (i, 0))],
        out_shape=[jax.ShapeDtypeStruct((s, f), BF16)],
        operands=(dxb, w2, a), name=name, comm=comm)


def _mm_tn(a, b, name, comm=None):
    s, n = a.shape
    d = b.shape[1]
    tn = 512

    def body(a_ref, b_ref, o_ref, acc):
        for k0 in range(0, s, ROW_TILE):
            part = _dot_tn(a_ref[k0:k0 + ROW_TILE, :], b_ref[k0:k0 + ROW_TILE, :])
            if k0 == 0:
                acc[...] = part
            else:
                acc[...] += part
        o_ref[...] = acc[...].astype(BF16)

    return _call(
        body,
        grid=(n // tn,),
        in_specs=[pl.BlockSpec((s, tn), lambda j: (0, j)),
                  _resident((s, d))],
        out_specs=[pl.BlockSpec((tn, d), lambda j: (j, 0))],
        out_shape=[jax.ShapeDtypeStruct((n, d), BF16)],
        operands=(a, b), name=name, scratch_shapes=[pltpu.VMEM((tn, d), F32)], comm=comm)


def _mm_nn_normbwd(dact, wt, x, dres, g, name, comm=None):
    s, kdim = dact.shape
    d = wt.shape[1]
    tm = _row_tile(2 * kdim + 4 * d + 4 * d + 4 * d + 2 * d, 2 * kdim * d)

    def body(a_ref, w_ref, x_ref, r_ref, g_ref, o_ref, ob_ref, dg_ref):
        @pl.when(pl.program_id(0) == 0)
        def _():
            dg_ref[...] = jnp.zeros_like(dg_ref)

        dh = _dot_nn(a_ref[...], w_ref[...])
        xx = x_ref[...]
        r = _rms_scale(xx)
        n = xx * r
        dg_ref[...] += jnp.sum(dh * n, axis=0, keepdims=True)
        dx = r_ref[...] + _rms_bwd(n, r, dh * g_ref[...])
        o_ref[...] = dx
        ob_ref[...] = dx.astype(BF16)

    return _call(
        body,
        grid=(s // tm,),
        in_specs=[pl.BlockSpec((tm, kdim), lambda i: (i, 0)),
                  _resident((kdim, d)),
                  pl.BlockSpec((tm, d), lambda i: (i, 0)),
                  pl.BlockSpec((tm, d), lambda i: (i, 0)),
                  pl.BlockSpec((1, d), lambda i: (0, 0))],
        out_specs=[pl.BlockSpec((tm, d), lambda i: (i, 0)),
                   pl.BlockSpec((tm, d), lambda i: (i, 0)),
                   pl.BlockSpec((1, d), lambda i: (0, 0))],
        out_shape=[jax.ShapeDtypeStruct((s, d), F32), jax.ShapeDtypeStruct((s, d), BF16),
                   jax.ShapeDtypeStruct((1, d), F32)],
        operands=(dact, wt, x, dres, g), name=name, comm=comm)


def _mix_bwd(dx1, wo, ya, yb, yc, gg, name, comm=None):
    s, d = dx1.shape
    tm = 2 * ROW_TILE
    widths = (A_WIDTH, CONV_CH, C_WIDTH)

    def body(dx_ref, w_ref, ya_ref, yb_ref, yc_ref, g_ref, da_ref, db_ref, dc_ref, dg_ref):
        @pl.when(pl.program_id(0) == 0)
        def _():
            dg_ref[...] = jnp.zeros_like(dg_ref)

        dy = _dot_nt(dx_ref[...], w_ref[...])
        gv = g_ref[...]
        off = 0
        dgs = []
        for ref, out, w in zip((ya_ref, yb_ref, yc_ref), (da_ref, db_ref, dc_ref), widths):
            t = ref[...]
            r = _rms_scale(t)
            n = t * r
            dyg = dy[:, off:off + w]
            dgs.append(jnp.sum(dyg * n, axis=0, keepdims=True))
            out[...] = _rms_bwd(n, r, dyg * gv[:, off:off + w])
            off += w
        dg_ref[...] += jnp.concatenate(dgs, axis=1)

    return _call(
        body,
        grid=(s // tm,),
        in_specs=[pl.BlockSpec((tm, d), lambda i: (i, 0)),
                  _resident((d, d)),
                  pl.BlockSpec((tm, A_WIDTH), lambda i: (i, 0)),
                  pl.BlockSpec((tm, CONV_CH), lambda i: (i, 0)),
                  pl.BlockSpec((tm, C_WIDTH), lambda i: (i, 0)),
                  pl.BlockSpec((1, d), lambda i: (0, 0))],
        out_specs=[pl.BlockSpec((tm, A_WIDTH), lambda i: (i, 0)),
                   pl.BlockSpec((tm, CONV_CH), lambda i: (i, 0)),
                   pl.BlockSpec((tm, C_WIDTH), lambda i: (i, 0)),
                   pl.BlockSpec((1, d), lambda i: (0, 0))],
        out_shape=[jax.ShapeDtypeStruct((s, A_WIDTH), F32), jax.ShapeDtypeStruct((s, CONV_CH), F32),
                   jax.ShapeDtypeStruct((s, C_WIDTH), F32), jax.ShapeDtypeStruct((1, d), F32)],
        operands=(dx1, wo, ya, yb, yc, gg), name=name, comm=comm)


CONV_CHUNK = 256
CONV_HALO = 8


def _conv_fwd(z, cw, name):
    s = z.shape[0]
    nch = s // CONV_CHUNK

    def body(gb_ref, gc_ref, xb_ref, w_ref, o_ref, us):
        us[pl.ds(0, CONV_HALO), :] = jnp.zeros((CONV_HALO, LANES), F32)
        us[pl.ds(CONV_HALO, s), :] = gc_ref[...] * xb_ref[...]
        w0, w1, w2 = w_ref[0:1, :], w_ref[1:2, :], w_ref[2:3, :]

        def chunk(c, carry):
            st = pl.multiple_of(c * CONV_CHUNK, CONV_CHUNK)
            ext = us[pl.ds(st, CONV_CHUNK + CONV_HALO), :]
            y = (w0 * ext[CONV_HALO - 2:CONV_HALO - 2 + CONV_CHUNK]
                 + w1 * ext[CONV_HALO - 1:CONV_HALO - 1 + CONV_CHUNK]
                 + w2 * ext[CONV_HALO:])
            o_ref[pl.ds(st, CONV_CHUNK), :] = gb_ref[pl.ds(st, CONV_CHUNK), :] * y
            return carry

        lax.fori_loop(0, nch, chunk, 0)

    col = lambda blk: pl.BlockSpec((s, LANES), lambda j, blk=blk: (0, blk + j))
    return pl.pallas_call(
        body,
        grid=(CONV_CH // LANES,),
        in_specs=[col(GB_BLK), col(GC_BLK), col(XB_BLK), pl.BlockSpec((3, LANES), lambda j: (0, j))],
        out_specs=pl.BlockSpec((s, LANES), lambda j: (0, j)),
        out_shape=jax.ShapeDtypeStruct((s, CONV_CH), F32),
        scratch_shapes=[pltpu.VMEM((s + CONV_HALO, LANES), F32)],
        compiler_params=_params("parallel"),
        name=name,
    )(z, z, z, cw)


def _conv_bwd(z, cw, dyb, dz, name):
    s = z.shape[0]
    nch = s // CONV_CHUNK
    ncol = CONV_CH // LANES

    def body(gb_ref, gc_ref, xb_ref, w_ref, dy_ref, dz_in, dz_ref, dw_ref, us, ds_, dgb_ref, dgc_ref, dxb_ref, sems):
        j = pl.program_id(0)

        def to_dz(staged, blk, k):
            cols = pl.ds(pl.multiple_of((blk + j) * LANES, LANES), LANES)
            return pltpu.make_async_copy(staged, dz_ref.at[:, cols], sems.at[k])

        copies = [to_dz(dgb_ref, GB_BLK, 0), to_dz(dgc_ref, GC_BLK, 1), to_dz(dxb_ref, XB_BLK, 2)]

        @pl.when(j > 0)
        def _():
            for cp in copies:
                cp.wait()

        us[pl.ds(0, CONV_HALO), :] = jnp.zeros((CONV_HALO, LANES), F32)
        us[pl.ds(CONV_HALO, s), :] = gc_ref[...] * xb_ref[...]
        ds_[pl.ds(s, CONV_HALO), :] = jnp.zeros((CONV_HALO, LANES), F32)
        ds_[pl.ds(0, s), :] = dy_ref[...] * gb_ref[...]
        w0, w1, w2 = w_ref[0:1, :], w_ref[1:2, :], w_ref[2:3, :]
        zero = jnp.zeros((1, LANES), F32)

        def chunk(c, carry):
            a0, a1, a2 = carry
            st = pl.multiple_of(c * CONV_CHUNK, CONV_CHUNK)
            rows = pl.ds(st, CONV_CHUNK)
            ext = us[pl.ds(st, CONV_CHUNK + CONV_HALO), :]
            um2 = ext[CONV_HALO - 2:CONV_HALO - 2 + CONV_CHUNK]
            um1 = ext[CONV_HALO - 1:CONV_HALO - 1 + CONV_CHUNK]
            u0 = ext[CONV_HALO:]
            dext = ds_[pl.ds(st, CONV_CHUNK + CONV_HALO), :]
            dc0 = dext[:CONV_CHUNK]
            du = w2 * dc0 + w1 * dext[1:1 + CONV_CHUNK] + w0 * dext[2:2 + CONV_CHUNK]
            yconv = w0 * um2 + w1 * um1 + w2 * u0
            dgb_ref[rows, :] = (dy_ref[rows, :] * yconv).astype(BF16)
            dgc_ref[rows, :] = (du * xb_ref[rows, :]).astype(BF16)
            dxb_ref[rows, :] = (du * gc_ref[rows, :]).astype(BF16)
            a0 = a0 + jnp.sum(dc0 * um2, axis=0, keepdims=True)
            a1 = a1 + jnp.sum(dc0 * um1, axis=0, keepdims=True)
            a2 = a2 + jnp.sum(dc0 * u0, axis=0, keepdims=True)
            return a0, a1, a2

        a0, a1, a2 = lax.fori_loop(0, nch, chunk, (zero, zero, zero))
        dw_ref[...] = jnp.concatenate([a0, a1, a2, jnp.zeros((5, LANES), F32)], axis=0)
        for cp in copies:
            cp.start()

        @pl.when(j == ncol - 1)
        def _():
            for cp in copies:
                cp.wait()

    col = lambda blk: pl.BlockSpec((s, LANES), lambda j, blk=blk: (0, blk + j))
    hbm = pl.BlockSpec(memory_space=pl.ANY)
    return pl.pallas_call(
        body,
        grid=(ncol,),
        in_specs=[col(GB_BLK), col(GC_BLK), col(XB_BLK), pl.BlockSpec((3, LANES), lambda j: (0, j)),
                  pl.BlockSpec((s, LANES), lambda j: (0, j)), hbm],
        out_specs=[hbm, pl.BlockSpec((8, LANES), lambda j: (0, j))],
        out_shape=[jax.ShapeDtypeStruct(dz.shape, dz.dtype), jax.ShapeDtypeStruct((8, CONV_CH), F32)],
        scratch_shapes=[pltpu.VMEM((s + CONV_HALO, LANES), F32), pltpu.VMEM((s + CONV_HALO, LANES), F32)]
        + [pltpu.VMEM((s, LANES), BF16)] * 3 + [pltpu.SemaphoreType.DMA((3,))],
        input_output_aliases={5: 0},
        compiler_params=_params("arbitrary"),
        name=name,
    )(z, z, z, cw, dyb, dz)


ATTN_ROWS = 512
ATTN_UNROLL = 8


def _band_rows(b, d, r):
    base = pl.multiple_of(b * (BLOCK * d), BLOCK)
    prev = jnp.maximum(base - BLOCK * d, 0)
    if d == 1:
        return pl.ds(base, BLOCK), pl.ds(pl.multiple_of(prev, BLOCK), BLOCK)
    return pl.ds(base + r, BLOCK, stride=d), pl.ds(prev + r, BLOCK, stride=d)


def _write_band_bias(bias_ref, max_dist):
    qi = lax.broadcasted_iota(jnp.int32, (BLOCK, 2 * BLOCK), 0)
    kj = lax.broadcasted_iota(jnp.int32, (BLOCK, 2 * BLOCK), 1)
    dist = BLOCK + qi - kj
    band = (dist >= 0) & (dist <= max_dist)
    bias_ref[0:BLOCK, :] = jnp.where(band, 0.0, -jnp.inf)
    bias_ref[BLOCK:2 * BLOCK, :] = jnp.where(band & (kj >= BLOCK), 0.0, -jnp.inf)


def _band_bias(bias_ref, b):
    bias = bias_ref[pl.ds(pl.multiple_of(jnp.where(b > 0, 0, BLOCK), BLOCK), BLOCK), :]
    return jnp.concatenate([bias, bias], axis=0)


def _kv_halves(pair):
    zero = jnp.zeros((1, LANES), jnp.int32)
    return zero + (pair >> 1), zero + ((pair + 1) >> 1)


def _stack_heads(t, head0, halves=None):
    top, bottom = jnp.where(head0, t, 0.0), jnp.where(head0, 0.0, t)
    if halves is not None:
        top = jnp.where(halves[0] == 1, pltpu.roll(top, HEAD_DIM, 1), top)
        bottom = jnp.where(halves[1] == 0, pltpu.roll(bottom, HEAD_DIM, 1), bottom)
    return jnp.concatenate([top, bottom], axis=0).astype(BF16)


def _unstack_heads(t, head0, halves=None):
    top, bottom = t[:BLOCK], t[BLOCK:]
    if halves is not None:
        top = jnp.where(halves[0] == 1, pltpu.roll(top, HEAD_DIM, 1), top)
        bottom = jnp.where(halves[1] == 0, pltpu.roll(bottom, HEAD_DIM, 1), bottom)
    return jnp.where(head0, top, bottom)


def _block_loops(s, patterns, unroll, one_block):
    for n, d in enumerate(patterns):
        nb = (s // BLOCK) // d
        ur = min(unroll, d)
        ub = unroll // ur
        for r0 in range(0, d, ur):
            def trip(i, carry, n=n, d=d, r0=r0, ur=ur, ub=ub):
                for u in range(ub):
                    for r in range(r0, r0 + ur):
                        one_block(i * ub + u, d, r, n == 0)
                return carry
            lax.fori_loop(0, nb // ub, trip, 0)


def _attn_fwd(z, m_init, l_init, q_blk, k_blk, v_blk, patterns, max_dist, gqa, name, comm=None):
    s = z.shape[0]
    npair = 3

    def body(q_ref, k_ref, v_ref, mi_ref, o_ref, lse0_ref, lse1_ref, bias_scr, m_scr, l_scr, *kv_scr):
        head0 = lax.broadcasted_iota(jnp.int32, (1, LANES), 1) < HEAD_DIM
        _write_band_bias(bias_scr, max_dist)
        ones = jnp.ones((2 * BLOCK, LANES), BF16)
        k_src, v_src = kv_scr if gqa else (k_ref, v_ref)
        if gqa:
            half = (lax.broadcasted_iota(jnp.int32, (1, LANES), 1) >= HEAD_DIM).astype(jnp.int32)
            swap = ((pl.program_id(0) + half) >> 1) != half

            def expand(c, carry):
                rows = pl.ds(pl.multiple_of(c * ATTN_ROWS, ATTN_ROWS), ATTN_ROWS)
                k_src[rows, :] = jnp.where(swap, pltpu.roll(k_ref[rows, :], HEAD_DIM, 1), k_ref[rows, :])
                v_src[rows, :] = jnp.where(swap, pltpu.roll(v_ref[rows, :], HEAD_DIM, 1), v_ref[rows, :])
                return carry

            lax.fori_loop(0, s // ATTN_ROWS, expand, 0)

        def one_block(b, d, r, first):
            rq, rp = _band_rows(b, d, r)
            q2 = _stack_heads(q_ref[rq, :] * SCALE, head0)
            k2 = jnp.concatenate([k_src[rp, :], k_src[rq, :]], axis=0).astype(BF16)
            v2 = jnp.concatenate([v_src[rp, :], v_src[rq, :]], axis=0).astype(BF16)
            sc = _dot_nt(q2, k2) + _band_bias(bias_scr, b)
            mb = jnp.max(sc, axis=1, keepdims=True)
            p = jnp.exp(sc - mb).astype(BF16)
            ob = _dot_nn(p, jnp.concatenate([v2, ones], axis=1))
            m_blk = _unstack_heads(jnp.broadcast_to(mb, (2 * BLOCK, LANES)), head0)
            l_blk = _unstack_heads(ob[:, LANES:], head0)
            o_blk = _unstack_heads(ob[:, :LANES], head0)
            if first and l_init == 0.0:
                m_new, l_new, o_new = m_blk, l_blk, o_blk
            else:
                if first:
                    m_old, l_old, o_old = jnp.broadcast_to(mi_ref[...], (BLOCK, LANES)), l_init, 0.0
                else:
                    m_old, l_old, o_old = m_scr[rq, :], l_scr[rq, :], o_ref[rq, :]
                m_new = jnp.maximum(m_old, m_blk)
                a_old = jnp.exp(m_old - m_new)
                a_blk = jnp.exp(m_blk - m_new)
                l_new = l_old * a_old + l_blk * a_blk
                o_new = o_old * a_old + o_blk * a_blk
            o_ref[rq, :], l_scr[rq, :], m_scr[rq, :] = o_new, l_new, m_new

        _block_loops(s, patterns, ATTN_UNROLL, one_block)

        def fin(c, carry):
            rows = pl.ds(pl.multiple_of(c * ATTN_ROWS, ATTN_ROWS), ATTN_ROWS)
            l = l_scr[rows, :]
            o_ref[rows, :] = o_ref[rows, :] / l
            lse = m_scr[rows, :] + jnp.log(l)
            swapped = pltpu.roll(lse, HEAD_DIM, 1)
            lse0_ref[rows, :] = jnp.where(head0, lse, swapped)
            lse1_ref[rows, :] = jnp.where(head0, swapped, lse)
            return carry

        lax.fori_loop(0, s // ATTN_ROWS, fin, 0)

    kv = (lambda blk: pl.BlockSpec((s, LANES), lambda j, blk=blk: (0, blk), pipeline_mode=pl.Buffered(1))) if gqa \
        else (lambda blk: pl.BlockSpec((s, LANES), lambda j, blk=blk: (0, blk + j)))
    own = pl.BlockSpec((s, LANES), lambda j: (0, j))
    return _call(
        body,
        grid=(npair,),
        in_specs=[pl.BlockSpec((s, LANES), lambda j: (0, q_blk + j)), kv(k_blk), kv(v_blk),
                  pl.BlockSpec((1, LANES), lambda j: (0, j))],
        out_specs=[own, own, own],
        out_shape=[jax.ShapeDtypeStruct((s, npair * LANES), F32)] * 3,
        operands=(z, z, z, m_init), name=name,
        scratch_shapes=[pltpu.VMEM((2 * BLOCK, 2 * BLOCK), F32)] + [pltpu.VMEM((s, LANES), F32)] * (4 if gqa else 2),
        comm=comm)


def _attn_bwd(z, do, o, lse, m_init, dz, q_blk, k_blk, v_blk, patterns, max_dist, gqa, name, comm=None):
    s = z.shape[0]
    npair = 3
    n_dz_in = 0 if dz is None else 1

    def body(q_ref, k_ref, v_ref, do_ref, o_ref, lse0_ref, lse1_ref, mi_ref, *rest):
        (dz_ref, dm_ref, dq_acc, dk_acc, dv_acc, dl0_scr, dl1_scr, bias_scr,
         dq_out, dk_out, dv_out, out_sems) = rest[n_dz_in:]
        pair = pl.program_id(0)
        head0 = lax.broadcasted_iota(jnp.int32, (1, LANES), 1) < HEAD_DIM
        halves = _kv_halves(pair) if gqa else None
        _write_band_bias(bias_scr, max_dist)

        def zero_kv():
            def f(c, carry):
                rows = pl.ds(pl.multiple_of(c * ATTN_ROWS, ATTN_ROWS), ATTN_ROWS)
                dk_acc[rows, :] = jnp.zeros((ATTN_ROWS, LANES), F32)
                dv_acc[rows, :] = jnp.zeros((ATTN_ROWS, LANES), F32)
                return carry
            lax.fori_loop(0, s // ATTN_ROWS, f, 0)

        if gqa:
            pl.when(pair == 0)(zero_kv)
        else:
            zero_kv()

        def prep(c, dm):
            rows = pl.ds(pl.multiple_of(c * ATTN_ROWS, ATTN_ROWS), ATTN_ROWS)
            dq_acc[rows, :] = jnp.zeros((ATTN_ROWS, LANES), F32)
            prod = do_ref[rows, :] * o_ref[rows, :]
            d0 = jnp.sum(jnp.where(head0, prod, 0.0), axis=1, keepdims=True)
            d1 = jnp.sum(jnp.where(head0, 0.0, prod), axis=1, keepdims=True)
            dl0_scr[rows, :] = jnp.broadcast_to(d0, (ATTN_ROWS, LANES))
            dl1_scr[rows, :] = jnp.broadcast_to(d1, (ATTN_ROWS, LANES))
            lse_own = jnp.where(head0, lse0_ref[rows, :], lse1_ref[rows, :])
            psink = jnp.exp(mi_ref[...] - lse_own)
            return dm - jnp.sum(psink * jnp.where(head0, d0, d1), axis=0, keepdims=True)

        dm_ref[...] = lax.fori_loop(0, s // ATTN_ROWS, prep, jnp.zeros((1, LANES), F32))

        def one_block(b, d, r, first):
            rq, rp = _band_rows(b, d, r)
            q2 = _stack_heads(q_ref[rq, :] * SCALE, head0, halves)
            do2 = _stack_heads(do_ref[rq, :], head0, halves)
            k2 = jnp.concatenate([k_ref[rp, :], k_ref[rq, :]], axis=0).astype(BF16)
            v2 = jnp.concatenate([v_ref[rp, :], v_ref[rq, :]], axis=0).astype(BF16)
            lse2 = jnp.concatenate([lse0_ref[rq, :], lse1_ref[rq, :]], axis=0)
            dl2 = jnp.concatenate([dl0_scr[rq, :], dl1_scr[rq, :]], axis=0)
            lse2 = jnp.concatenate([lse2, lse2], axis=1)
            dl2 = jnp.concatenate([dl2, dl2], axis=1)
            p = jnp.exp(_dot_nt(q2, k2) + _band_bias(bias_scr, b) - lse2)
            dp = _dot_nt(do2, v2)
            dsc = (p * (dp - dl2)).astype(BF16)
            dq2 = _unstack_heads(_dot_nn(dsc, k2), head0, halves)
            dk2 = _dot_tn(dsc, q2)
            dv2 = _dot_tn(p.astype(BF16), do2)
            dq_acc[rq, :] += dq2 * SCALE
            dk_acc[rp, :] += dk2[:BLOCK]
            dk_acc[rq, :] += dk2[BLOCK:]
            dv_acc[rp, :] += dv2[:BLOCK]
            dv_acc[rq, :] += dv2[BLOCK:]

        _block_loops(s, patterns, ATTN_UNROLL, one_block)

        def to_dz(staged, blk, k):
            cols = pl.ds(pl.multiple_of(blk * LANES, LANES), LANES)
            return pltpu.make_async_copy(staged, dz_ref.at[:, cols], out_sems.at[k])

        last_pair = pair == npair - 1
        q_copy = to_dz(dq_out, q_blk + pair, 0)
        kv_copies = [to_dz(dk_out, k_blk + (0 if gqa else pair), 1), to_dz(dv_out, v_blk + (0 if gqa else pair), 2)]

        @pl.when(pair > 0)
        def _():
            for cp in [q_copy] + ([] if gqa else kv_copies):
                cp.wait()

        def stage(acc, out):
            def f(c, carry):
                rows = pl.ds(pl.multiple_of(c * ATTN_ROWS, ATTN_ROWS), ATTN_ROWS)
                out[rows, :] = acc[rows, :].astype(BF16)
                return carry
            lax.fori_loop(0, s // ATTN_ROWS, f, 0)

        def stage_kv():
            stage(dk_acc, dk_out)
            stage(dv_acc, dv_out)
            for cp in kv_copies:
                cp.start()

        stage(dq_acc, dq_out)
        q_copy.start()
        if gqa:
            pl.when(last_pair)(stage_kv)
        else:
            stage_kv()

        @pl.when(last_pair)
        def _():
            for cp in [q_copy] + kv_copies:
                cp.wait()

    own = pl.BlockSpec((s, LANES), lambda j: (0, j))
    hbm = pl.BlockSpec(memory_space=pl.ANY)
    if gqa:
        kv = lambda blk: pl.BlockSpec((s, LANES), lambda j, blk=blk: (0, blk), pipeline_mode=pl.Buffered(1))
    else:
        kv = lambda blk: pl.BlockSpec((s, LANES), lambda j, blk=blk: (0, blk + j))
    in_specs = [pl.BlockSpec((s, LANES), lambda j: (0, q_blk + j)), kv(k_blk), kv(v_blk), own, own, own, own,
                pl.BlockSpec((1, LANES), lambda j: (0, j))]
    operands = (z, z, z, do, o, lse[0], lse[1], m_init)
    return _call(
        body,
        grid=(npair,),
        in_specs=in_specs + [hbm] * n_dz_in,
        out_specs=[hbm, pl.BlockSpec((1, LANES), lambda j: (0, j))],
        out_shape=[jax.ShapeDtypeStruct((s, IN_WIDTH), BF16), jax.ShapeDtypeStruct((1, npair * LANES), F32)],
        operands=operands + (() if dz is None else (dz,)), name=name,
        scratch_shapes=[pltpu.VMEM((s, LANES), F32)] * 5 + [pltpu.VMEM((2 * BLOCK, 2 * BLOCK), F32)]
        + [pltpu.VMEM((s, LANES), BF16)] * 3 + [pltpu.SemaphoreType.DMA((3,))],
        comm=comm, aliases={} if dz is None else {len(in_specs): 0})


def _adamw_math(w, g, m, v):
    m = ADAM_B1 * m + (1.0 - ADAM_B1) * g
    v = ADAM_B2 * v + (1.0 - ADAM_B2) * (g * g)
    m_hat = m / (1.0 - ADAM_B1 ** ADAM_STEP)
    v_hat = v / (1.0 - ADAM_B2 ** ADAM_STEP)
    delta = -ADAM_LR * (m_hat / (jnp.sqrt(v_hat) + ADAM_EPS) + ADAM_WD * w)
    return delta, m, v


def _adamw(w, g, m, v, name):
    rows, cols = w.shape
    tr = min(rows, 256)

    def body(w_ref, g_ref, m_ref, v_ref, d_ref, nm_ref, nv_ref):
        d_ref[...], nm_ref[...], nv_ref[...] = _adamw_math(w_ref[...], g_ref[...], m_ref[...], v_ref[...])

    spec = pl.BlockSpec((tr, cols), lambda i: (i, 0))
    return pl.pallas_call(
        body,
        grid=(rows // tr,),
        in_specs=[spec] * 4,
        out_specs=[spec] * 3,
        out_shape=[jax.ShapeDtypeStruct((rows, cols), F32)] * 3,
        compiler_params=_params("parallel"),
        name=name,
    )(w, g, m, v)


def _sum_adamw(parts, w, m, v, pos, transpose, name):
    assert len(parts) == DEPTH == 2
    (p0, r0), (p1, r1) = parts
    _, rows, cols = p0.shape
    tr = 256 if rows % 256 == 0 else rows
    nt = rows // tr

    def body(pos_ref, p0_ref, r0_ref, p1_ref, r1_ref, w_ref, m_ref, v_ref, g_ref, d_ref, nm_ref, nv_ref):
        def run(p_ref, r_ref):
            g = ((p_ref[...].astype(F32) + r_ref[0].astype(F32)) + r_ref[1].astype(F32)) + r_ref[2].astype(F32)
            if transpose:
                g = g.T
            g_ref[...] = g
            d_ref[...], nm_ref[...], nv_ref[...] = _adamw_math(w_ref[...], g, m_ref[...], v_ref[...])

        layer0 = pl.program_id(0) < nt
        pl.when(layer0)(lambda: run(p0_ref, r0_ref))
        pl.when(jnp.logical_not(layer0))(lambda: run(p1_ref, r1_ref))

    def tile0(i):
        return jnp.minimum(i, nt - 1)

    def tile1(i):
        return jnp.maximum(i - nt, 0)

    if transpose:
        w_spec = pl.BlockSpec((None, cols, tr), lambda i, q: (i // nt, 0, i % nt))
    else:
        w_spec = pl.BlockSpec((None, tr, cols), lambda i, q: (i // nt, i % nt, 0))
    return pl.pallas_call(
        body,
        grid_spec=pltpu.PrefetchScalarGridSpec(
            num_scalar_prefetch=1,
            grid=(DEPTH * nt,),
            in_specs=[pl.BlockSpec((None, tr, cols), lambda i, q: (q[0], tile0(i), 0)),
                      pl.BlockSpec((3, tr, cols), lambda i, q: (0, tile0(i), 0)),
                      pl.BlockSpec((None, tr, cols), lambda i, q: (q[0], tile1(i), 0)),
                      pl.BlockSpec((3, tr, cols), lambda i, q: (0, tile1(i), 0)),
                      w_spec, w_spec, w_spec],
            out_specs=[w_spec] * 4,
        ),
        out_shape=[jax.ShapeDtypeStruct(w.shape, F32)] * 4,
        compiler_params=_params("arbitrary"),
        name=name,
    )(pos, p0, r0, p1, r1, w, m, v)


def _small_sum_adamw(gathered, params, name):
    _, rows, cols = gathered.shape
    n = len(params)

    def body(ga_ref, *refs):
        ins, outs, (g_scr,) = refs[:3 * n], refs[3 * n:7 * n + 2], refs[7 * n + 2:]
        g = ga_ref[0]
        for i in range(1, N_DEV):
            g = g + ga_ref[i]
        g_scr[...] = g
        for k, (row0, w, _, _) in enumerate(params):
            w_ref, m_ref, v_ref = ins[3 * k:3 * k + 3]
            gk = g_scr[row0:row0 + w.shape[0], :]
            outs[4 * k][...] = gk
            outs[4 * k + 1][...], outs[4 * k + 2][...], outs[4 * k + 3][...] = _adamw_math(
                w_ref[...], gk, m_ref[...], v_ref[...])
        outs[4 * n][...] = g_scr[CONV_ROW:CONV_ROW + 8, :]
        outs[4 * n + 1][...] = g_scr[LOSS_ROW:LOSS_ROW + 1, :]

    out_shape = []
    for _, w, _, _ in params:
        out_shape += [jax.ShapeDtypeStruct(w.shape, F32)] * 4
    out_shape += [jax.ShapeDtypeStruct((8, cols), F32), jax.ShapeDtypeStruct((1, cols), F32)]
    res = pl.pallas_call(
        body,
        out_shape=out_shape,
        scratch_shapes=[pltpu.VMEM((rows, cols), F32)],
        name=name,
    )(gathered, *[t for _, w, m, v in params for t in (w, m, v)])
    return [res[4 * k:4 * k + 4] for k in range(n)], res[4 * n], res[4 * n + 1]


def _pair_sum(g4, r1, pos, name):
    _, _, rows, cols = g4.shape
    tr = min(rows, 512)

    def body(pos_ref, g_ref, r_ref, o_ref):
        o_ref[...] = (g_ref[...].astype(F32) + r_ref[...].astype(F32)).astype(BF16)

    return pl.pallas_call(
        body,
        grid_spec=pltpu.PrefetchScalarGridSpec(
            num_scalar_prefetch=1,
            grid=(4, rows // tr),
            in_specs=[pl.BlockSpec((None, None, tr, cols), lambda i, j, p: (i, p[1], j, 0)),
                      pl.BlockSpec((None, tr, cols), lambda i, j, p: (i, j, 0))],
            out_specs=pl.BlockSpec((None, tr, cols), lambda i, j, p: (i, j, 0)),
        ),
        out_shape=jax.ShapeDtypeStruct((4, rows, cols), BF16),
        compiler_params=_params("parallel", "parallel"),
        name=name,
    )(pos, g4, r1)


def _place():
    return lax.axis_index("x"), lax.axis_index("y"), lax.axis_index("c")


def _gather_comm(shards):
    na = len(shards)

    def plan(ins, outs, sems):
        send_sems, recv_sems, local_sems = sems
        x, y, c = _place()
        me, sibling = (x, y, c), (x, y, 1 - c)
        chips = [(1 - x, y), (x, 1 - y), (1 - x, 1 - y)]

        def rows(a, px, py, pc):
            m = ins[a].shape[0]
            return outs[a].at[pl.ds((4 * px + 2 * py + pc) * m, m), :]

        def copy(a, k, block, to, src=None):
            return pltpu.make_async_remote_copy(
                src_ref=rows(a, *block) if src is None else src, dst_ref=rows(a, *block),
                send_sem=send_sems.at[a, k], recv_sem=recv_sems.at[a, k], device_id=to, device_id_type=MESH)

        mine = [pltpu.make_async_copy(ins[a], rows(a, *me), local_sems.at[a]) for a in range(na)]
        first = []
        for a in range(na):
            first.append(copy(a, 0, me, sibling, src=ins[a]))
            first += [copy(a, 1 + j, me, (*chip, c), src=ins[a]) for j, chip in enumerate(chips)]
        return me, sibling, chips, c, copy, mine, first

    def start(ins, outs, sems):
        *_, mine, first = plan(ins, outs, sems)
        for cp in mine + first:
            cp.start()

    def finish(ins, outs, sems):
        me, sibling, chips, c, copy, mine, first = plan(ins, outs, sems)
        passed = []
        for j, chip in enumerate(chips):
            for a in range(na):
                copy(a, 1 + j, (*chip, c), me).wait_recv()
                cp = copy(a, 4 + j, (*chip, c), sibling)
                cp.start()
                passed.append(cp)
        for a in range(na):
            copy(a, 0, sibling, me).wait_recv()
            for j, chip in enumerate(chips):
                copy(a, 4 + j, (*chip, 1 - c), me).wait_recv()
        for cp in first + passed:
            cp.wait_send()
        for cp in mine:
            cp.wait()

    return _Comm(tuple(shards),
                 tuple(jax.ShapeDtypeStruct((N_DEV * t.shape[0], t.shape[1]), t.dtype) for t in shards),
                 (pltpu.SemaphoreType.DMA((na, 7)), pltpu.SemaphoreType.DMA((na, 7)), pltpu.SemaphoreType.DMA((na,))),
                 start, finish)


def _exchange_comm(arrays, out_shape, n_copies, copies_of):
    na = len(arrays)

    def every(ins, outs, sems):
        send_sems, recv_sems = sems
        return [cp for a in range(na) for cp in copies_of(ins, outs, a, send_sems, recv_sems)]

    def start(ins, outs, sems):
        for cp in every(ins, outs, sems):
            cp.start()

    def finish(ins, outs, sems):
        for cp in every(ins, outs, sems):
            cp.wait()

    return _Comm(tuple(arrays), tuple(out_shape),
                 (pltpu.SemaphoreType.DMA((na, n_copies)), pltpu.SemaphoreType.DMA((na, n_copies))), start, finish)


def _sibling_comm(grads):
    def copies_of(ins, outs, a, send_sems, recv_sems):
        x, y, c = _place()
        return [pltpu.make_async_remote_copy(
            src_ref=ins[a].at[chip, 1 - c], dst_ref=outs[a].at[chip],
            send_sem=send_sems.at[a, chip], recv_sem=recv_sems.at[a, chip],
            device_id=(x, y, 1 - c), device_id_type=MESH) for chip in range(4)]

    return _exchange_comm(grads, [jax.ShapeDtypeStruct((4,) + t.shape[2:], t.dtype) for t in grads], 4, copies_of)


def _chip_comm(partials):
    def copies_of(ins, outs, a, send_sems, recv_sems):
        x, y, c = _place()
        chips = [(1 - x, y), (x, 1 - y), (1 - x, 1 - y)]
        return [pltpu.make_async_remote_copy(
            src_ref=ins[a].at[2 * cx + cy], dst_ref=outs[a].at[k],
            send_sem=send_sems.at[a, k], recv_sem=recv_sems.at[a, k],
            device_id=(cx, cy, c), device_id_type=MESH) for k, (cx, cy) in enumerate(chips)]

    return _exchange_comm(partials, [jax.ShapeDtypeStruct((3,) + t.shape[1:], t.dtype) for t in partials], 3, copies_of)


def _pad_rows(t, rows):
    return jnp.pad(t, ((0, rows - t.shape[0]), (0, D_MODEL - t.shape[1])))


MIX_ROW, GROUP_ROW, MLP_ROW, FINAL_ROW, CONV_ROW, SINK_ROW = 0, 8, 16, 24, 32, 40
LOSS_ROW = FINAL_ROW + 1


def _pack_small(g_mix, g_group, g_mlp, g_final, conv, sinks, loss):
    final_and_loss = jnp.concatenate([g_final.reshape(1, D_MODEL), _pad_rows(loss, 1)], axis=0)
    return jnp.concatenate([
        _pad_rows(g_mix, 8), _pad_rows(g_group, 8), _pad_rows(g_mlp, 8), _pad_rows(final_and_loss, 8),
        _pad_rows(conv.reshape(DEPTH * 3, CONV_CH), 8), _pad_rows(sinks.reshape(1, DEPTH * 6), 8)], axis=0)


def kernel(x, w_in, conv_w, sinks, g_mix, g_group, w_o, g_mlp, w_ff_in, w_ff_out, g_final, loss_target, m_w_in, m_conv_w, m_sinks, m_g_mix, m_g_group, m_w_o, m_g_mlp, m_w_ff_in, m_w_ff_out, m_g_final, v_w_in, v_conv_w, v_sinks, v_g_mix, v_g_group, v_w_o, v_g_mlp, v_w_ff_in, v_w_ff_out, v_g_final):
    ax, ay, ac = _place()
    chip = 2 * ax + ay
    dev = 4 * ax + 2 * ay + ac
    pos = jnp.stack([chip, ac]).astype(jnp.int32)

    x0 = x.reshape(SEQ, D_MODEL)
    target = loss_target.reshape(SEQ, D_MODEL)

    shards = {}
    for l in range(DEPTH):
        shards[l, 0], shards[l, 1] = w_in[l].T.astype(BF16), w_o[l].astype(BF16)
        shards[l, 2], shards[l, 3] = w_ff_in[l].T.astype(BF16), w_ff_out[l].astype(BF16)
    conv_tile = jnp.pad(conv_w.reshape(DEPTH * 3, CONV_CH // N_DEV), ((0, 2), (0, LANES - CONV_CH // N_DEV)))
    wt_in0, conv_all = _comm_only(_gather_comm([shards[0, 0], conv_tile]), "gather_first")
    conv_full = conv_all.reshape(N_DEV, 8, LANES)[:, :DEPTH * 3, :CONV_CH // N_DEV]
    conv_full = conv_full.transpose(1, 0, 2).reshape(DEPTH, 3, CONV_CH)

    dx, parts, small = _step(x0, target, shards, wt_in0, conv_full, sinks, g_mix, g_group, g_mlp, g_final, pos)
    return _finish(dx, parts, small, pos, dev, w_in, conv_w, sinks, g_mix, g_group, w_o, g_mlp, w_ff_in, w_ff_out, g_final, m_w_in, m_conv_w, m_sinks, m_g_mix, m_g_group, m_w_o, m_g_mlp, m_w_ff_in, m_w_ff_out, m_g_final, v_w_in, v_conv_w, v_sinks, v_g_mix, v_g_group, v_w_o, v_g_mlp, v_w_ff_in, v_w_ff_out, v_g_final)


FWD_CARRY = {(0, "in_proj"): ((0, 1),), (0, "window"): ((1, 0),), (0, "dilated"): ((0, 2),),
             (0, "mix_out"): ((1, 1),), (0, "ff_in"): ((0, 3),), (0, "ff_out"): ((1, 3),),
             (1, "dilated"): ((1, 2),)}


def _step(x0, target, shards, wt_in0, conv_full, sinks, g_mix, g_group, g_mlp, g_final, pos):
    sink_lanes = jnp.repeat(sinks.reshape(DEPTH, 6), HEAD_DIM, axis=1)
    no_sink = jnp.full((1, A_WIDTH), NEG_BIG, F32)
    full = {(0, 0): wt_in0}

    def gather(stage, l):
        keys = FWD_CARRY.get((l, stage), ())
        return keys, (_gather_comm([shards[k] for k in keys]) if keys else None)

    def landed(keys, got):
        full.update(zip(keys, got))

    saved = []
    xc = x0
    for l in range(DEPTH):
        keys, comm = gather("in_proj", l)
        (z, h), got = _norm_mm(xc, g_mix[l:l + 1], full[l, 0], False, f"in_proj_{l}", comm)
        landed(keys, got)
        sink_l = sink_lanes[l:l + 1]
        keys, comm = gather("window", l)
        (yc, *lse_c), got = _attn_fwd(z, sink_l, 1.0, QC_BLK, KC_BLK, VC_BLK, (1,), C_MAX_DIST, True,
                                     f"window_attn_{l}", comm)
        landed(keys, got)
        yb = _conv_fwd(z, conv_full[l], f"conv_{l}")
        keys, comm = gather("dilated", l)
        (ya, *lse_a), got = _attn_fwd(z, no_sink, 0.0, QA_BLK, KA_BLK, VA_BLK, DILATED_PATTERNS, A_MAX_DIST, False,
                                     f"dilated_attn_{l}", comm)
        landed(keys, got)
        keys, comm = gather("mix_out", l)
        (y, x1), got = _mix_out(ya, yb, yc, g_group[l:l + 1], full[l, 1], xc, f"mix_out_{l}", comm)
        landed(keys, got)
        keys, comm = gather("ff_in", l)
        (a, h2), got = _norm_mm(x1, g_mlp[l:l + 1], full[l, 2], True, f"ff_in_{l}", comm)
        landed(keys, got)
        saved.append((xc, z, h, ya, lse_a, yb, yc, lse_c, sink_l, y, x1, a, h2))
        if l + 1 < DEPTH:
            keys, comm = gather("ff_out", l)
            (xc,), got = _mm_res(a, full[l, 3], x1, f"ff_out_{l}", comm)
            landed(keys, got)

    loss_slab, dx, dxb, dg_final = _mm_res_loss(a, full[DEPTH - 1, 3], x1, g_final.reshape(1, D_MODEL), target,
                                                f"ff_out_{DEPTH - 1}_loss")

    def by_owner(t):
        return t.reshape(4, 2, t.shape[0] // N_DEV, D_MODEL)

    def pair(key, g, r1):
        return _pair_sum(g, r1, pos, f"grad_pair_sum_{key[0]}_{key[1]}")

    partial, r2 = {}, {}
    dg_mix, dg_group, dg_mlp, dconv, dsinks = [None] * DEPTH, [None] * DEPTH, [None] * DEPTH, [None] * DEPTH, [None] * DEPTH
    for l in reversed(range(DEPTH)):
        xin, z, h, ya, lse_a, yb, yc, lse_c, sink_l, y, x1, a, h2 = saved[l]
        late = [(l + 1, 1), (l + 1, 0)] if l + 1 < DEPTH else []
        (du,), got = _mlp_bwd_act(dxb, full[l, 3], a, f"ff_out_bwd_{l}",
                                  _chip_comm([partial[k] for k in late]) if late else None)
        r2.update(zip(late, got))
        (g3,), _ = _mm_tn(a, dxb, f"grad_w_ff_out_{l}")
        (g2,), _ = _mm_tn(du, h2, f"grad_w_ff_in_{l}")
        g3, g2 = by_owner(g3), by_owner(g2)
        (dx1, dx1b, dg_mlp[l]), got = _mm_nn_normbwd(du, full[l, 2], x1, dx, g_mlp[l:l + 1], f"ff_in_bwd_{l}",
                                                    _sibling_comm([g3, g2]))
        partial[l, 3], partial[l, 2] = pair((l, 3), g3, got[0]), pair((l, 2), g2, got[1])
        (g1,), _ = _mm_tn(y, dx1b, f"grad_w_o_{l}")
        g1 = by_owner(g1)
        (dya, dyb, dyc, dg_group[l]), got = _mix_bwd(dx1b, full[l, 1], ya, yb, yc, g_group[l:l + 1],
                                                     f"mix_out_bwd_{l}", _sibling_comm([g1]) if l == 0 else None)
        if l == 0:
            partial[l, 1] = pair((l, 1), g1, got[0])
        early = [(l, 3), (l, 2)] + ([(l, 1)] if l == 0 else [])
        (dz, _), got = _attn_bwd(z, dya, ya, lse_a, no_sink, None, QA_BLK, KA_BLK, VA_BLK, DILATED_PATTERNS,
                                 A_MAX_DIST, False, f"dilated_attn_bwd_{l}", _chip_comm([partial[k] for k in early]))
        r2.update(zip(early, got))
        dz, dcw = _conv_bwd(z, conv_full[l], dyb, dz, f"conv_bwd_{l}")
        (dz, dsink), _ = _attn_bwd(z, dyc, yc, lse_c, sink_l, dz, QC_BLK, KC_BLK, VC_BLK, (1,), C_MAX_DIST,
                                   True, f"window_attn_bwd_{l}")
        (g0,), _ = _mm_tn(dz, h, f"grad_w_in_{l}")
        g0 = by_owner(g0)
        if l > 0:
            (dx, dxb, dg_mix[l]), got = _mm_nn_normbwd(dz, full[l, 0], xin, dx1, g_mix[l:l + 1], f"in_proj_bwd_{l}",
                                                      _sibling_comm([g1, g0]))
            partial[l, 1], partial[l, 0] = pair((l, 1), g1, got[0]), pair((l, 0), g0, got[1])
        else:
            (r1,) = _comm_only(_sibling_comm([g0]), "grad_sibling_exchange_last")
            partial[l, 0] = pair((l, 0), g0, r1)
            (dx, dxb, dg_mix[l]), got = _mm_nn_normbwd(dz, full[l, 0], xin, dx1, g_mix[l:l + 1], f"in_proj_bwd_{l}",
                                                      _chip_comm([partial[l, 0]]))
            r2[l, 0] = got[0]
        dconv[l] = dcw[:3]
        dsinks[l] = dsink[0, ::HEAD_DIM]
    parts = {key: (partial[key], r2[key]) for key in partial}
    small = _pack_small(jnp.concatenate(dg_mix), jnp.concatenate(dg_group), jnp.concatenate(dg_mlp),
                        dg_final, jnp.stack(dconv), jnp.stack(dsinks), loss_slab[0:1])
    return dx, parts, small


def _finish(dx, parts, small, pos, dev, w_in, conv_w, sinks, g_mix, g_group, w_o, g_mlp, w_ff_in, w_ff_out, g_final, m_w_in, m_conv_w, m_sinks, m_g_mix, m_g_group, m_w_o, m_g_mlp, m_w_ff_in, m_w_ff_out, m_g_final, v_w_in, v_conv_w, v_sinks, v_g_mix, v_g_group, v_w_o, v_g_mlp, v_w_ff_in, v_w_ff_out, v_g_final):
    grad_x = dx.reshape(1, SEQ, D_MODEL)

    (small_all,) = _comm_only(_gather_comm([small]), "gather_small_grads")
    row = lambda t: t.reshape(1, D_MODEL)
    sink_row = lambda t: _pad_rows(t.reshape(1, DEPTH * 6), 1)
    params = [(MIX_ROW, g_mix, m_g_mix, v_g_mix), (GROUP_ROW, g_group, m_g_group, v_g_group),
              (MLP_ROW, g_mlp, m_g_mlp, v_g_mlp), (FINAL_ROW, row(g_final), row(m_g_final), row(v_g_final)),
              (SINK_ROW, sink_row(sinks), sink_row(m_sinks), sink_row(v_sinks))]
    updated, conv_rows, loss_row = _small_sum_adamw(small_all.reshape(N_DEV, SMALL_ROWS, D_MODEL), params, "small_adamw")
    loss = loss_row[0, 0]
    (grad_g_mix, delta_g_mix, new_m_g_mix, new_v_g_mix), (grad_g_group, delta_g_group, new_m_g_group, new_v_g_group), \
        (grad_g_mlp, delta_g_mlp, new_m_g_mlp, new_v_g_mlp), final4, sinks4 = updated
    grad_g_final, delta_g_final, new_m_g_final, new_v_g_final = [t.reshape(D_MODEL) for t in final4]
    grad_sinks, delta_sinks, new_m_sinks, new_v_sinks = [t[0, :DEPTH * 6].reshape(DEPTH, 2, 3) for t in sinks4]
    conv_grad_full = conv_rows[:DEPTH * 3, :CONV_CH].reshape(DEPTH, 3, CONV_CH)
    cs = CONV_CH // N_DEV
    grad_conv_w = lax.dynamic_slice_in_dim(conv_grad_full, dev * cs, cs, axis=2)

    def tile_of(t):
        return jnp.pad(t.reshape(1, DEPTH * 3 * cs), ((0, 7), (0, 256 - DEPTH * 3 * cs)))

    cd, cm, cv = _adamw(tile_of(conv_w), tile_of(grad_conv_w), tile_of(m_conv_w), tile_of(v_conv_w), "conv_adamw")
    untile = lambda t: t[0, :DEPTH * 3 * cs].reshape(DEPTH, 3, cs)
    delta_conv_w, new_m_conv_w, new_v_conv_w = untile(cd), untile(cm), untile(cv)

    def big(kind, w, m, v, transpose, name):
        return _sum_adamw([parts[l, kind] for l in range(DEPTH)], w, m, v, pos, transpose, name)

    grad_w_in, delta_w_in, new_m_w_in, new_v_w_in = big(0, w_in, m_w_in, v_w_in, True, "adamw_w_in")
    grad_w_o, delta_w_o, new_m_w_o, new_v_w_o = big(1, w_o, m_w_o, v_w_o, False, "adamw_w_o")
    grad_w_ff_in, delta_w_ff_in, new_m_w_ff_in, new_v_w_ff_in = big(2, w_ff_in, m_w_ff_in, v_w_ff_in, True, "adamw_w_ff_in")
    grad_w_ff_out, delta_w_ff_out, new_m_w_ff_out, new_v_w_ff_out = big(3, w_ff_out, m_w_ff_out, v_w_ff_out, False,
                                                                         "adamw_w_ff_out")

    return (loss, grad_x, grad_w_in, grad_conv_w, grad_sinks, grad_g_mix, grad_g_group, grad_w_o, grad_g_mlp,
            grad_w_ff_in, grad_w_ff_out, grad_g_final,
            delta_w_in, delta_conv_w, delta_sinks, delta_g_mix, delta_g_group, delta_w_o, delta_g_mlp,
            delta_w_ff_in, delta_w_ff_out, delta_g_final,
            new_m_w_in, new_m_conv_w, new_m_sinks, new_m_g_mix, new_m_g_group, new_m_w_o, new_m_g_mlp,
            new_m_w_ff_in, new_m_w_ff_out, new_m_g_final,
            new_v_w_in, new_v_conv_w, new_v_sinks, new_v_g_mix, new_v_g_group, new_v_w_o, new_v_g_mlp,
            new_v_w_ff_in, new_v_w_ff_out, new_v_g_final)
```

```python
from typing import Callable, NamedTuple

import jax
import jax.numpy as jnp
from jax import lax
from jax.experimental import pallas as pl
from jax.experimental.pallas import tpu as pltpu

F32 = jnp.float32
BF16 = jnp.bfloat16
MESH = pl.DeviceIdType.MESH

N_DEV = 8
SEQ = 4096
D_MODEL = 1024
DEPTH = 2
HEAD_DIM = 64
LANES = 128
A_WIDTH = 384
CONV_CH = 256
C_WIDTH = 384
KV_WIDTH = 128
IN_WIDTH = 2560
D_FF = 4096
BLOCK = 128
DILATED_PATTERNS = (1, 4, 16)
A_MAX_DIST = 128
C_MAX_DIST = 127
EPS = 1e-6
SCALE = HEAD_DIM ** -0.5
NEG_BIG = -1e30
F32_TINY = 1.1754944e-38

QA_BLK, KA_BLK, VA_BLK = 0, 3, 6
GB_BLK, GC_BLK, XB_BLK = 9, 11, 13
QC_BLK, KC_BLK, VC_BLK = 15, 18, 19

ADAM_LR = 0.001
ADAM_B1 = 0.9
ADAM_B2 = 0.999
ADAM_EPS = 1e-08
ADAM_WD = 0.01
ADAM_STEP = 10

VMEM_LIMIT = 56 * 1024 * 1024
TILE_BUDGET = 46 * 1024 * 1024
ROW_TILE = 512
COL_CHUNK = 512
SMALL_ROWS = 48


def _dot_nn(a, b):
    return lax.dot_general(a, b, (((1,), (0,)), ((), ())), preferred_element_type=F32)


def _dot_nt(a, b):
    return lax.dot_general(a, b, (((1,), (1,)), ((), ())), preferred_element_type=F32)


def _dot_tn(a, b):
    return lax.dot_general(a, b, (((0,), (0,)), ((), ())), preferred_element_type=F32)


def _params(*sem):
    return pltpu.CompilerParams(dimension_semantics=sem, vmem_limit_bytes=VMEM_LIMIT)


def _resident(shape):
    return pl.BlockSpec(shape, lambda i: (0,) * len(shape), pipeline_mode=pl.Buffered(1))


def _row_tile(row_bytes, resident_bytes):
    for tm in (ROW_TILE, ROW_TILE // 2):
        if 2 * tm * row_bytes + resident_bytes <= TILE_BUDGET:
            return tm
    return ROW_TILE // 4


def _rms_scale(t):
    return lax.rsqrt(jnp.mean(t * t, axis=-1, keepdims=True) + EPS)


def _rms_bwd(n, r, dn):
    return r * (dn - n * jnp.mean(dn * n, axis=-1, keepdims=True))


class _Comm(NamedTuple):
    arrays: tuple
    out_shape: tuple
    sems: tuple
    start: Callable
    finish: Callable


def _call(body, grid, in_specs, out_specs, out_shape, operands, name, scratch_shapes=(), comm=None, aliases=None):
    n_in, n_out, n_scr = len(in_specs), len(out_shape), len(scratch_shapes)
    aliases = dict(aliases or {})
    if comm is None:
        res = pl.pallas_call(body, grid=grid, in_specs=list(in_specs), out_specs=list(out_specs),
                             out_shape=list(out_shape), scratch_shapes=list(scratch_shapes),
                             input_output_aliases=aliases,
                             compiler_params=_params("arbitrary"), name=name)(*operands)
        return list(res), []
    c_in, c_out = len(comm.arrays), len(comm.out_shape)
    hbm = pl.BlockSpec(memory_space=pl.ANY)
    last = grid[0] - 1

    def carried(*refs):
        ins, cins = refs[:n_in], refs[n_in:n_in + c_in]
        o0 = n_in + c_in
        outs, couts = refs[o0:o0 + n_out], refs[o0 + n_out:o0 + n_out + c_out]
        s0 = o0 + n_out + c_out
        scr, sems = refs[s0:s0 + n_scr], refs[s0 + n_scr:]
        pl.when(pl.program_id(0) == 0)(lambda: comm.start(cins, couts, sems))
        body(*ins, *outs, *scr)
        pl.when(pl.program_id(0) == last)(lambda: comm.finish(cins, couts, sems))

    res = pl.pallas_call(carried, grid=grid, in_specs=list(in_specs) + [hbm] * c_in,
                         out_specs=list(out_specs) + [hbm] * c_out, out_shape=list(out_shape) + list(comm.out_shape),
                         scratch_shapes=list(scratch_shapes) + list(comm.sems), input_output_aliases=aliases,
                         compiler_params=_params("arbitrary"), name=name)(*operands, *comm.arrays)
    return list(res[:n_out]), list(res[n_out:])


def _comm_only(comm, name):
    hbm = pl.BlockSpec(memory_space=pl.ANY)
    c_in, c_out = len(comm.arrays), len(comm.out_shape)

    def body(*refs):
        ins, outs, sems = refs[:c_in], refs[c_in:c_in + c_out], refs[c_in + c_out:]
        comm.start(ins, outs, sems)
        comm.finish(ins, outs, sems)

    return pl.pallas_call(body, in_specs=[hbm] * c_in, out_specs=[hbm] * c_out, out_shape=list(comm.out_shape),
                          scratch_shapes=list(comm.sems), name=name)(*comm.arrays)


def _norm_mm(x, g, wt, name, comm=None):
    s, d = x.shape
    n = wt.shape[0]
    tm = _row_tile(4 * d + 4 * n + 2 * d, 2 * n * d)

    def body(x_ref, g_ref, w_ref, o_ref, h_ref):
        xx = x_ref[...]
        h = ((xx * _rms_scale(xx)) * g_ref[...]).astype(BF16)
        h_ref[...] = h
        for n0 in range(0, n, COL_CHUNK):
            o_ref[:, n0:n0 + COL_CHUNK] = _dot_nt(h, w_ref[n0:n0 + COL_CHUNK, :])

    return _call(
        body,
        grid=(s // tm,),
        in_specs=[pl.BlockSpec((tm, d), lambda i: (i, 0)),
                  pl.BlockSpec((1, d), lambda i: (0, 0)),
                  _resident((n, d))],
        out_specs=[pl.BlockSpec((tm, n), lambda i: (i, 0)),
                   pl.BlockSpec((tm, d), lambda i: (i, 0))],
        out_shape=[jax.ShapeDtypeStruct((s, n), F32), jax.ShapeDtypeStruct((s, d), BF16)],
        operands=(x, g, wt), name=name, comm=comm)


def _ff_out_in_proj(a, w2, x1, g, wt, name, comm=None):
    s, f = a.shape
    d = w2.shape[1]
    n = wt.shape[0]
    tm = _row_tile(2 * f + 4 * d + 4 * d + 4 * n + 2 * d, 2 * f * d + 2 * n * d)

    def body(a_ref, w2_ref, x_ref, g_ref, w_ref, x2_ref, z_ref, h_ref):
        x2 = x_ref[...] + _dot_nn(a_ref[...], w2_ref[...])
        x2_ref[...] = x2
        h = ((x2 * _rms_scale(x2)) * g_ref[...]).astype(BF16)
        h_ref[...] = h
        for n0 in range(0, n, COL_CHUNK):
            z_ref[:, n0:n0 + COL_CHUNK] = _dot_nt(h, w_ref[n0:n0 + COL_CHUNK, :])

    rows = lambda w: pl.BlockSpec((tm, w), lambda i: (i, 0))
    return _call(
        body,
        grid=(s // tm,),
        in_specs=[rows(f), _resident((f, d)), rows(d), pl.BlockSpec((1, d), lambda i: (0, 0)), _resident((n, d))],
        out_specs=[rows(d), rows(n), rows(d)],
        out_shape=[jax.ShapeDtypeStruct((s, d), F32), jax.ShapeDtypeStruct((s, n), F32),
                   jax.ShapeDtypeStruct((s, d), BF16)],
        operands=(a, w2, x1, g, wt), name=name, comm=comm)


def _mix_ff_in(ya, yb, yc, gg, wo, x0, g_mlp, wt1, name, comm=None):
    s = ya.shape[0]
    d = wo.shape[1]
    f = wt1.shape[0]
    tm = _row_tile(4 * d + 4 * d + 2 * d + 4 * d + 2 * d + 2 * f, 2 * d * d + 2 * f * d)

    def body(ya_ref, yb_ref, yc_ref, gg_ref, wo_ref, x_ref, g_ref, w1_ref, y_ref, x1_ref, a_ref, h_ref):
        parts = []
        for ref in (ya_ref, yb_ref, yc_ref):
            t = ref[...]
            parts.append(t * _rms_scale(t))
        y = (jnp.concatenate(parts, axis=1) * gg_ref[...]).astype(BF16)
        y_ref[...] = y
        x1 = x_ref[...] + _dot_nn(y, wo_ref[...])
        x1_ref[...] = x1
        h = ((x1 * _rms_scale(x1)) * g_ref[...]).astype(BF16)
        h_ref[...] = h
        for n0 in range(0, f, COL_CHUNK):
            u = _dot_nt(h, w1_ref[n0:n0 + COL_CHUNK, :])
            a_ref[:, n0:n0 + COL_CHUNK] = jnp.square(jnp.maximum(u, 0.0)).astype(BF16)

    rows = lambda w: pl.BlockSpec((tm, w), lambda i: (i, 0))
    vec = pl.BlockSpec((1, d), lambda i: (0, 0))
    return _call(
        body,
        grid=(s // tm,),
        in_specs=[rows(A_WIDTH), rows(CONV_CH), rows(C_WIDTH), vec, _resident((d, d)), rows(d), vec, _resident((f, d))],
        out_specs=[rows(d), rows(d), rows(f), rows(d)],
        out_shape=[jax.ShapeDtypeStruct((s, d), BF16), jax.ShapeDtypeStruct((s, d), F32),
                   jax.ShapeDtypeStruct((s, f), BF16), jax.ShapeDtypeStruct((s, d), BF16)],
        operands=(ya, yb, yc, gg, wo, x0, g_mlp, wt1), name=name, comm=comm)


def _mm_res_loss(a, w2, x1, g, target, name):
    s, f = a.shape
    d = w2.shape[1]
    tm = _row_tile(2 * f + 4 * d + 4 * d + 4 * d + 2 * d, 2 * f * d)

    def body(a_ref, w_ref, x_ref, g_ref, t_ref, loss_ref, dx_ref, dxb_ref, dg_ref):
        @pl.when(pl.program_id(0) == 0)
        def _():
            loss_ref[...] = jnp.zeros_like(loss_ref)
            dg_ref[...] = jnp.zeros_like(dg_ref)

        xx = x_ref[...] + _dot_nn(a_ref[...], w_ref[...])
        r = _rms_scale(xx)
        n = xx * r
        gv = g_ref[...]
        err = n * gv - t_ref[...]
        per_tok = jnp.sum(err * err, axis=1, keepdims=True) * (1.0 / d)
        loss_ref[...] += 0.5 * jnp.sum(per_tok, axis=0, keepdims=True)
        dout = err * (1.0 / d)
        dg_ref[...] += jnp.sum(dout * n, axis=0, keepdims=True)
        dx = _rms_bwd(n, r, dout * gv)
        dx_ref[...] = dx
        dxb_ref[...] = dx.astype(BF16)

    return pl.pallas_call(
        body,
        grid=(s // tm,),
        in_specs=[pl.BlockSpec((tm, f), lambda i: (i, 0)),
                  _resident((f, d)),
                  pl.BlockSpec((tm, d), lambda i: (i, 0)),
                  pl.BlockSpec((1, d), lambda i: (0, 0)),
                  pl.BlockSpec((tm, d), lambda i: (i, 0))],
        out_specs=[pl.BlockSpec((8, LANES), lambda i: (0, 0)),
                   pl.BlockSpec((tm, d), lambda i: (i, 0)),
                   pl.BlockSpec((tm, d), lambda i: (i, 0)),
                   pl.BlockSpec((1, d), lambda i: (0, 0))],
        out_shape=[jax.ShapeDtypeStruct((8, LANES), F32), jax.ShapeDtypeStruct((s, d), F32),
                   jax.ShapeDtypeStruct((s, d), BF16), jax.ShapeDtypeStruct((1, d), F32)],
        compiler_params=_params("arbitrary"),
        name=name,
    )(a, w2, x1, g, target)


def _mlp_bwd_act(dxb, w2, a, name, comm=None):
    s, d = dxb.shape
    f = w2.shape[0]
    tm = _row_tile(2 * d + 2 * f + 2 * f, 2 * f * d)

    def body(dx_ref, w_ref, a_ref, du_ref):
        dx = dx_ref[...]
        for n0 in range(0, f, COL_CHUNK):
            da = _dot_nt(dx, w_ref[n0:n0 + COL_CHUNK, :])
            av = a_ref[:, n0:n0 + COL_CHUNK].astype(F32)
            rl = av * lax.rsqrt(jnp.maximum(av, F32_TINY))
            du_ref[:, n0:n0 + COL_CHUNK] = (da * (2.0 * rl)).astype(BF16)

    return _call(
        body,
        grid=(s // tm,),
        in_specs=[pl.BlockSpec((tm, d), lambda i: (i, 0)),
                  _resident((f, d)),
                  pl.BlockSpec((tm, f), lambda i: (i, 0))],
        out_specs=[pl.BlockSpec((tm, f), lambda i: (i, 0))],
        out_shape=[jax.ShapeDtypeStruct((s, f), BF16)],
        operands=(dxb, w2, a), name=name, comm=comm)


def _mm_tn(a, b, name, comm=None):
    s, n = a.shape
    d = b.shape[1]
    tn = 512

    def body(a_ref, b_ref, o_ref, acc):
        for k0 in range(0, s, ROW_TILE):
            part = _dot_tn(a_ref[k0:k0 + ROW_TILE, :], b_ref[k0:k0 + ROW_TILE, :])
            if k0 == 0:
                acc[...] = part
            else:
                acc[...] += part
        o_ref[...] = acc[...].astype(BF16)

    return _call(
        body,
        grid=(n // tn,),
        in_specs=[pl.BlockSpec((s, tn), lambda j: (0, j)),
                  _resident((s, d))],
        out_specs=[pl.BlockSpec((tn, d), lambda j: (j, 0))],
        out_shape=[jax.ShapeDtypeStruct((n, d), BF16)],
        operands=(a, b), name=name, scratch_shapes=[pltpu.VMEM((tn, d), F32)], comm=comm)


def _mm_nn_normbwd(dact, wt, x, dres, g, name, comm=None):
    s, kdim = dact.shape
    d = wt.shape[1]
    tm = _row_tile(2 * kdim + 4 * d + 4 * d + 4 * d + 2 * d, 2 * kdim * d)

    def body(a_ref, w_ref, x_ref, r_ref, g_ref, o_ref, ob_ref, dg_ref):
        @pl.when(pl.program_id(0) == 0)
        def _():
            dg_ref[...] = jnp.zeros_like(dg_ref)

        dh = _dot_nn(a_ref[...], w_ref[...])
        xx = x_ref[...]
        r = _rms_scale(xx)
        n = xx * r
        dg_ref[...] += jnp.sum(dh * n, axis=0, keepdims=True)
        dx = r_ref[...] + _rms_bwd(n, r, dh * g_ref[...])
        o_ref[...] = dx
        ob_ref[...] = dx.astype(BF16)

    return _call(
        body,
        grid=(s // tm,),
        in_specs=[pl.BlockSpec((tm, kdim), lambda i: (i, 0)),
                  _resident((kdim, d)),
                  pl.BlockSpec((tm, d), lambda i: (i, 0)),
                  pl.BlockSpec((tm, d), lambda i: (i, 0)),
                  pl.BlockSpec((1, d), lambda i: (0, 0))],
        out_specs=[pl.BlockSpec((tm, d), lambda i: (i, 0)),
                   pl.BlockSpec((tm, d), lambda i: (i, 0)),
                   pl.BlockSpec((1, d), lambda i: (0, 0))],
        out_shape=[jax.ShapeDtypeStruct((s, d), F32), jax.ShapeDtypeStruct((s, d), BF16),
                   jax.ShapeDtypeStruct((1, d), F32)],
        operands=(dact, wt, x, dres, g), name=name, comm=comm)


def _ff_in_mix_bwd(du, wt1, x1, dres, g_mlp, wo, ya, yb, yc, gg, name, comm=None):
    s, f = du.shape
    d = wt1.shape[1]
    widths = (A_WIDTH, CONV_CH, C_WIDTH)
    tm = _row_tile(2 * f + 4 * d + 4 * d + 4 * d + 2 * d + 4 * d + 4 * d, 2 * f * d + 2 * d * d)

    def body(du_ref, w1_ref, x_ref, r_ref, g_ref, wo_ref, ya_ref, yb_ref, yc_ref, gg_ref,
             dx_ref, dxb_ref, dg_ref, da_ref, db_ref, dc_ref, dgg_ref):
        @pl.when(pl.program_id(0) == 0)
        def _():
            dg_ref[...] = jnp.zeros_like(dg_ref)
            dgg_ref[...] = jnp.zeros_like(dgg_ref)

        dh = _dot_nn(du_ref[...], w1_ref[...])
        xx = x_ref[...]
        r = _rms_scale(xx)
        n = xx * r
        dg_ref[...] += jnp.sum(dh * n, axis=0, keepdims=True)
        dx = r_ref[...] + _rms_bwd(n, r, dh * g_ref[...])
        dx_ref[...] = dx
        dxb = dx.astype(BF16)
        dxb_ref[...] = dxb

        dy = _dot_nt(dxb, wo_ref[...])
        gv = gg_ref[...]
        off = 0
        dgs = []
        for ref, out, w in zip((ya_ref, yb_ref, yc_ref), (da_ref, db_ref, dc_ref), widths):
            t = ref[...]
            r = _rms_scale(t)
            n = t * r
            dyg = dy[:, off:off + w]
            dgs.append(jnp.sum(dyg * n, axis=0, keepdims=True))
            out[...] = _rms_bwd(n, r, dyg * gv[:, off:off + w])
            off += w
        dgg_ref[...] += jnp.concatenate(dgs, axis=1)

    rows = lambda w: pl.BlockSpec((tm, w), lambda i: (i, 0))
    vec = pl.BlockSpec((1, d), lambda i: (0, 0))
    return _call(
        body,
        grid=(s // tm,),
        in_specs=[rows(f), _resident((f, d)), rows(d), rows(d), vec, _resident((d, d)),
                  rows(A_WIDTH), rows(CONV_CH), rows(C_WIDTH), vec],
        out_specs=[rows(d), rows(d), vec, rows(A_WIDTH), rows(CONV_CH), rows(C_WIDTH), vec],
        out_shape=[jax.ShapeDtypeStruct((s, d), F32), jax.ShapeDtypeStruct((s, d), BF16), jax.ShapeDtypeStruct((1, d), F32),
                   jax.ShapeDtypeStruct((s, A_WIDTH), F32), jax.ShapeDtypeStruct((s, CONV_CH), F32),
                   jax.ShapeDtypeStruct((s, C_WIDTH), F32), jax.ShapeDtypeStruct((1, d), F32)],
        operands=(du, wt1, x1, dres, g_mlp, wo, ya, yb, yc, gg), name=name, comm=comm)


CONV_CHUNK = 256
CONV_HALO = 8


def _conv_fwd(z, cw, name):
    s = z.shape[0]
    nch = s // CONV_CHUNK

    def body(gb_ref, gc_ref, xb_ref, w_ref, o_ref, us):
        us[pl.ds(0, CONV_HALO), :] = jnp.zeros((CONV_HALO, LANES), F32)
        us[pl.ds(CONV_HALO, s), :] = gc_ref[...] * xb_ref[...]
        w0, w1, w2 = w_ref[0:1, :], w_ref[1:2, :], w_ref[2:3, :]

        def chunk(c, carry):
            st = pl.multiple_of(c * CONV_CHUNK, CONV_CHUNK)
            ext = us[pl.ds(st, CONV_CHUNK + CONV_HALO), :]
            y = (w0 * ext[CONV_HALO - 2:CONV_HALO - 2 + CONV_CHUNK]
                 + w1 * ext[CONV_HALO - 1:CONV_HALO - 1 + CONV_CHUNK]
                 + w2 * ext[CONV_HALO:])
            o_ref[pl.ds(st, CONV_CHUNK), :] = gb_ref[pl.ds(st, CONV_CHUNK), :] * y
            return carry

        lax.fori_loop(0, nch, chunk, 0)

    col = lambda blk: pl.BlockSpec((s, LANES), lambda j, blk=blk: (0, blk + j))
    return pl.pallas_call(
        body,
        grid=(CONV_CH // LANES,),
        in_specs=[col(GB_BLK), col(GC_BLK), col(XB_BLK), pl.BlockSpec((3, LANES), lambda j: (0, j))],
        out_specs=pl.BlockSpec((s, LANES), lambda j: (0, j)),
        out_shape=jax.ShapeDtypeStruct((s, CONV_CH), F32),
        scratch_shapes=[pltpu.VMEM((s + CONV_HALO, LANES), F32)],
        compiler_params=_params("parallel"),
        name=name,
    )(z, z, z, cw)


def _conv_bwd(z, cw, dyb, dz, name):
    s = z.shape[0]
    nch = s // CONV_CHUNK
    ncol = CONV_CH // LANES

    def body(gb_ref, gc_ref, xb_ref, w_ref, dy_ref, dz_in, dz_ref, dw_ref, us, ds_, dgb_ref, dgc_ref, dxb_ref, sems):
        j = pl.program_id(0)

        def to_dz(staged, blk, k):
            cols = pl.ds(pl.multiple_of((blk + j) * LANES, LANES), LANES)
            return pltpu.make_async_copy(staged, dz_ref.at[:, cols], sems.at[k])

        copies = [to_dz(dgb_ref, GB_BLK, 0), to_dz(dgc_ref, GC_BLK, 1), to_dz(dxb_ref, XB_BLK, 2)]

        @pl.when(j > 0)
        def _():
            for cp in copies:
                cp.wait()

        us[pl.ds(0, CONV_HALO), :] = jnp.zeros((CONV_HALO, LANES), F32)
        us[pl.ds(CONV_HALO, s), :] = gc_ref[...] * xb_ref[...]
        ds_[pl.ds(s, CONV_HALO), :] = jnp.zeros((CONV_HALO, LANES), F32)
        ds_[pl.ds(0, s), :] = dy_ref[...] * gb_ref[...]
        w0, w1, w2 = w_ref[0:1, :], w_ref[1:2, :], w_ref[2:3, :]
        zero = jnp.zeros((1, LANES), F32)

        def chunk(c, carry):
            a0, a1, a2 = carry
            st = pl.multiple_of(c * CONV_CHUNK, CONV_CHUNK)
            rows = pl.ds(st, CONV_CHUNK)
            ext = us[pl.ds(st, CONV_CHUNK + CONV_HALO), :]
            um2 = ext[CONV_HALO - 2:CONV_HALO - 2 + CONV_CHUNK]
            um1 = ext[CONV_HALO - 1:CONV_HALO - 1 + CONV_CHUNK]
            u0 = ext[CONV_HALO:]
            dext = ds_[pl.ds(st, CONV_CHUNK + CONV_HALO), :]
            dc0 = dext[:CONV_CHUNK]
            du = w2 * dc0 + w1 * dext[1:1 + CONV_CHUNK] + w0 * dext[2:2 + CONV_CHUNK]
            yconv = w0 * um2 + w1 * um1 + w2 * u0
            dgb_ref[rows, :] = (dy_ref[rows, :] * yconv).astype(BF16)
            dgc_ref[rows, :] = (du * xb_ref[rows, :]).astype(BF16)
            dxb_ref[rows, :] = (du * gc_ref[rows, :]).astype(BF16)
            a0 = a0 + jnp.sum(dc0 * um2, axis=0, keepdims=True)
            a1 = a1 + jnp.sum(dc0 * um1, axis=0, keepdims=True)
            a2 = a2 + jnp.sum(dc0 * u0, axis=0, keepdims=True)
            return a0, a1, a2

        a0, a1, a2 = lax.fori_loop(0, nch, chunk, (zero, zero, zero))
        dw_ref[...] = jnp.concatenate([a0, a1, a2, jnp.zeros((5, LANES), F32)], axis=0)
        for cp in copies:
            cp.start()

        @pl.when(j == ncol - 1)
        def _():
            for cp in copies:
                cp.wait()

    col = lambda blk: pl.BlockSpec((s, LANES), lambda j, blk=blk: (0, blk + j))
    hbm = pl.BlockSpec(memory_space=pl.ANY)
    return pl.pallas_call(
        body,
        grid=(ncol,),
        in_specs=[col(GB_BLK), col(GC_BLK), col(XB_BLK), pl.BlockSpec((3, LANES), lambda j: (0, j)),
                  pl.BlockSpec((s, LANES), lambda j: (0, j)), hbm],
        out_specs=[hbm, pl.BlockSpec((8, LANES), lambda j: (0, j))],
        out_shape=[jax.ShapeDtypeStruct(dz.shape, dz.dtype), jax.ShapeDtypeStruct((8, CONV_CH), F32)],
        scratch_shapes=[pltpu.VMEM((s + CONV_HALO, LANES), F32), pltpu.VMEM((s + CONV_HALO, LANES), F32)]
        + [pltpu.VMEM((s, LANES), BF16)] * 3 + [pltpu.SemaphoreType.DMA((3,))],
        input_output_aliases={5: 0},
        compiler_params=_params("arbitrary"),
        name=name,
    )(z, z, z, cw, dyb, dz)


ATTN_ROWS = 512
ATTN_UNROLL = 8


def _band_rows(b, d, r):
    base = pl.multiple_of(b * (BLOCK * d), BLOCK)
    prev = jnp.maximum(base - BLOCK * d, 0)
    if d == 1:
        return pl.ds(base, BLOCK), pl.ds(pl.multiple_of(prev, BLOCK), BLOCK)
    return pl.ds(base + r, BLOCK, stride=d), pl.ds(prev + r, BLOCK, stride=d)


def _write_band_bias(bias_ref, max_dist):
    qi = lax.broadcasted_iota(jnp.int32, (BLOCK, 2 * BLOCK), 0)
    kj = lax.broadcasted_iota(jnp.int32, (BLOCK, 2 * BLOCK), 1)
    dist = BLOCK + qi - kj
    band = (dist >= 0) & (dist <= max_dist)
    bias_ref[0:BLOCK, :] = jnp.where(band, 0.0, -jnp.inf)
    bias_ref[BLOCK:2 * BLOCK, :] = jnp.where(band & (kj >= BLOCK), 0.0, -jnp.inf)


def _band_bias(bias_ref, b):
    bias = bias_ref[pl.ds(pl.multiple_of(jnp.where(b > 0, 0, BLOCK), BLOCK), BLOCK), :]
    return jnp.concatenate([bias, bias], axis=0)


def _kv_halves(pair):
    zero = jnp.zeros((1, LANES), jnp.int32)
    return zero + (pair >> 1), zero + ((pair + 1) >> 1)


def _stack_heads(t, head0, halves=None):
    top, bottom = jnp.where(head0, t, 0.0), jnp.where(head0, 0.0, t)
    if halves is not None:
        top = jnp.where(halves[0] == 1, pltpu.roll(top, HEAD_DIM, 1), top)
        bottom = jnp.where(halves[1] == 0, pltpu.roll(bottom, HEAD_DIM, 1), bottom)
    return jnp.concatenate([top, bottom], axis=0).astype(BF16)


def _unstack_heads(t, head0, halves=None):
    top, bottom = t[:BLOCK], t[BLOCK:]
    if halves is not None:
        top = jnp.where(halves[0] == 1, pltpu.roll(top, HEAD_DIM, 1), top)
        bottom = jnp.where(halves[1] == 0, pltpu.roll(bottom, HEAD_DIM, 1), bottom)
    return jnp.where(head0, top, bottom)


def _block_loops(s, patterns, unroll, one_block):
    for n, d in enumerate(patterns):
        nb = (s // BLOCK) // d
        ur = min(unroll, d)
        ub = unroll // ur
        for r0 in range(0, d, ur):
            def trip(i, carry, n=n, d=d, r0=r0, ur=ur, ub=ub):
                for u in range(ub):
                    for r in range(r0, r0 + ur):
                        one_block(i * ub + u, d, r, n == 0)
                return carry
            lax.fori_loop(0, nb // ub, trip, 0)


def _attn_fwd(z, m_init, l_init, q_blk, k_blk, v_blk, patterns, max_dist, gqa, name, comm=None):
    s = z.shape[0]
    npair = 3

    def body(q_ref, k_ref, v_ref, mi_ref, o_ref, lse0_ref, lse1_ref, bias_scr, m_scr, l_scr, *kv_scr):
        head0 = lax.broadcasted_iota(jnp.int32, (1, LANES), 1) < HEAD_DIM
        _write_band_bias(bias_scr, max_dist)
        ones = jnp.ones((2 * BLOCK, LANES), BF16)
        k_src, v_src = kv_scr if gqa else (k_ref, v_ref)
        if gqa:
            half = (lax.broadcasted_iota(jnp.int32, (1, LANES), 1) >= HEAD_DIM).astype(jnp.int32)
            swap = ((pl.program_id(0) + half) >> 1) != half

            def expand(c, carry):
                rows = pl.ds(pl.multiple_of(c * ATTN_ROWS, ATTN_ROWS), ATTN_ROWS)
                k_src[rows, :] = jnp.where(swap, pltpu.roll(k_ref[rows, :], HEAD_DIM, 1), k_ref[rows, :])
                v_src[rows, :] = jnp.where(swap, pltpu.roll(v_ref[rows, :], HEAD_DIM, 1), v_ref[rows, :])
                return carry

            lax.fori_loop(0, s // ATTN_ROWS, expand, 0)

        def one_block(b, d, r, first):
            rq, rp = _band_rows(b, d, r)
            q2 = _stack_heads(q_ref[rq, :] * SCALE, head0)
            k2 = jnp.concatenate([k_src[rp, :], k_src[rq, :]], axis=0).astype(BF16)
            v2 = jnp.concatenate([v_src[rp, :], v_src[rq, :]], axis=0).astype(BF16)
            sc = _dot_nt(q2, k2) + _band_bias(bias_scr, b)
            mb = jnp.max(sc, axis=1, keepdims=True)
            p = jnp.exp(sc - mb).astype(BF16)
            ob = _dot_nn(p, jnp.concatenate([v2, ones], axis=1))
            m_blk = _unstack_heads(jnp.broadcast_to(mb, (2 * BLOCK, LANES)), head0)
            l_blk = _unstack_heads(ob[:, LANES:], head0)
            o_blk = _unstack_heads(ob[:, :LANES], head0)
            if first and l_init == 0.0:
                m_new, l_new, o_new = m_blk, l_blk, o_blk
            else:
                if first:
                    m_old, l_old, o_old = jnp.broadcast_to(mi_ref[...], (BLOCK, LANES)), l_init, 0.0
                else:
                    m_old, l_old, o_old = m_scr[rq, :], l_scr[rq, :], o_ref[rq, :]
                m_new = jnp.maximum(m_old, m_blk)
                a_old = jnp.exp(m_old - m_new)
                a_blk = jnp.exp(m_blk - m_new)
                l_new = l_old * a_old + l_blk * a_blk
                o_new = o_old * a_old + o_blk * a_blk
            o_ref[rq, :], l_scr[rq, :], m_scr[rq, :] = o_new, l_new, m_new

        _block_loops(s, patterns, ATTN_UNROLL, one_block)

        def fin(c, carry):
            rows = pl.ds(pl.multiple_of(c * ATTN_ROWS, ATTN_ROWS), ATTN_ROWS)
            l = l_scr[rows, :]
            o_ref[rows, :] = o_ref[rows, :] / l
            lse = m_scr[rows, :] + jnp.log(l)
            swapped = pltpu.roll(lse, HEAD_DIM, 1)
            lse0_ref[rows, :] = jnp.where(head0, lse, swapped)
            lse1_ref[rows, :] = jnp.where(head0, swapped, lse)
            return carry

        lax.fori_loop(0, s // ATTN_ROWS, fin, 0)

    kv = (lambda blk: pl.BlockSpec((s, LANES), lambda j, blk=blk: (0, blk), pipeline_mode=pl.Buffered(1))) if gqa \
        else (lambda blk: pl.BlockSpec((s, LANES), lambda j, blk=blk: (0, blk + j)))
    own = pl.BlockSpec((s, LANES), lambda j: (0, j))
    return _call(
        body,
        grid=(npair,),
        in_specs=[pl.BlockSpec((s, LANES), lambda j: (0, q_blk + j)), kv(k_blk), kv(v_blk),
                  pl.BlockSpec((1, LANES), lambda j: (0, j))],
        out_specs=[own, own, own],
        out_shape=[jax.ShapeDtypeStruct((s, npair * LANES), F32)] * 3,
        operands=(z, z, z, m_init), name=name,
        scratch_shapes=[pltpu.VMEM((2 * BLOCK, 2 * BLOCK), F32)] + [pltpu.VMEM((s, LANES), F32)] * (4 if gqa else 2),
        comm=comm)


def _attn_bwd(z, do, o, lse, m_init, dz, q_blk, k_blk, v_blk, patterns, max_dist, gqa, name, comm=None):
    s = z.shape[0]
    npair = 3
    n_dz_in = 0 if dz is None else 1

    def body(q_ref, k_ref, v_ref, do_ref, o_ref, lse0_ref, lse1_ref, mi_ref, *rest):
        (dz_ref, dm_ref, dq_acc, dk_acc, dv_acc, dl0_scr, dl1_scr, bias_scr,
         dq_out, dk_out, dv_out, out_sems) = rest[n_dz_in:]
        pair = pl.program_id(0)
        head0 = lax.broadcasted_iota(jnp.int32, (1, LANES), 1) < HEAD_DIM
        halves = _kv_halves(pair) if gqa else None
        _write_band_bias(bias_scr, max_dist)

        def zero_kv():
            def f(c, carry):
                rows = pl.ds(pl.multiple_of(c * ATTN_ROWS, ATTN_ROWS), ATTN_ROWS)
                dk_acc[rows, :] = jnp.zeros((ATTN_ROWS, LANES), F32)
                dv_acc[rows, :] = jnp.zeros((ATTN_ROWS, LANES), F32)
                return carry
            lax.fori_loop(0, s // ATTN_ROWS, f, 0)

        if gqa:
            pl.when(pair == 0)(zero_kv)
        else:
            zero_kv()

        def prep(c, dm):
            rows = pl.ds(pl.multiple_of(c * ATTN_ROWS, ATTN_ROWS), ATTN_ROWS)
            dq_acc[rows, :] = jnp.zeros((ATTN_ROWS, LANES), F32)
            prod = do_ref[rows, :] * o_ref[rows, :]
            d0 = jnp.sum(jnp.where(head0, prod, 0.0), axis=1, keepdims=True)
            d1 = jnp.sum(jnp.where(head0, 0.0, prod), axis=1, keepdims=True)
            dl0_scr[rows, :] = jnp.broadcast_to(d0, (ATTN_ROWS, LANES))
            dl1_scr[rows, :] = jnp.broadcast_to(d1, (ATTN_ROWS, LANES))
            lse_own = jnp.where(head0, lse0_ref[rows, :], lse1_ref[rows, :])
            psink = jnp.exp(mi_ref[...] - lse_own)
            return dm - jnp.sum(psink * jnp.where(head0, d0, d1), axis=0, keepdims=True)

        dm_ref[...] = lax.fori_loop(0, s // ATTN_ROWS, prep, jnp.zeros((1, LANES), F32))

        def one_block(b, d, r, first):
            rq, rp = _band_rows(b, d, r)
            q2 = _stack_heads(q_ref[rq, :] * SCALE, head0, halves)
            do2 = _stack_heads(do_ref[rq, :], head0, halves)
            k2 = jnp.concatenate([k_ref[rp, :], k_ref[rq, :]], axis=0).astype(BF16)
            v2 = jnp.concatenate([v_ref[rp, :], v_ref[rq, :]], axis=0).astype(BF16)
            lse2 = jnp.concatenate([lse0_ref[rq, :], lse1_ref[rq, :]], axis=0)
            dl2 = jnp.concatenate([dl0_scr[rq, :], dl1_scr[rq, :]], axis=0)
            lse2 = jnp.concatenate([lse2, lse2], axis=1)
            dl2 = jnp.concatenate([dl2, dl2], axis=1)
            p = jnp.exp(_dot_nt(q2, k2) + _band_bias(bias_scr, b) - lse2)
            dp = _dot_nt(do2, v2)
            dsc = (p * (dp - dl2)).astype(BF16)
            dq2 = _unstack_heads(_dot_nn(dsc, k2), head0, halves)
            dk2 = _dot_tn(dsc, q2)
            dv2 = _dot_tn(p.astype(BF16), do2)
            dq_acc[rq, :] += dq2 * SCALE
            dk_acc[rp, :] += dk2[:BLOCK]
            dk_acc[rq, :] += dk2[BLOCK:]
            dv_acc[rp, :] += dv2[:BLOCK]
            dv_acc[rq, :] += dv2[BLOCK:]

        _block_loops(s, patterns, ATTN_UNROLL, one_block)

        def to_dz(staged, blk, k):
            cols = pl.ds(pl.multiple_of(blk * LANES, LANES), LANES)
            return pltpu.make_async_copy(staged, dz_ref.at[:, cols], out_sems.at[k])

        last_pair = pair == npair - 1
        q_copy = to_dz(dq_out, q_blk + pair, 0)
        kv_copies = [to_dz(dk_out, k_blk + (0 if gqa else pair), 1), to_dz(dv_out, v_blk + (0 if gqa else pair), 2)]

        @pl.when(pair > 0)
        def _():
            for cp in [q_copy] + ([] if gqa else kv_copies):
                cp.wait()

        def stage(acc, out):
            def f(c, carry):
                rows = pl.ds(pl.multiple_of(c * ATTN_ROWS, ATTN_ROWS), ATTN_ROWS)
                out[rows, :] = acc[rows, :].astype(BF16)
                return carry
            lax.fori_loop(0, s // ATTN_ROWS, f, 0)

        def stage_kv():
            stage(dk_acc, dk_out)
            stage(dv_acc, dv_out)
            for cp in kv_copies:
                cp.start()

        stage(dq_acc, dq_out)
        q_copy.start()
        if gqa:
            pl.when(last_pair)(stage_kv)
        else:
            stage_kv()

        @pl.when(last_pair)
        def _():
            for cp in [q_copy] + kv_copies:
                cp.wait()

    own = pl.BlockSpec((s, LANES), lambda j: (0, j))
    hbm = pl.BlockSpec(memory_space=pl.ANY)
    if gqa:
        kv = lambda blk: pl.BlockSpec((s, LANES), lambda j, blk=blk: (0, blk), pipeline_mode=pl.Buffered(1))
    else:
        kv = lambda blk: pl.BlockSpec((s, LANES), lambda j, blk=blk: (0, blk + j))
    in_specs = [pl.BlockSpec((s, LANES), lambda j: (0, q_blk + j)), kv(k_blk), kv(v_blk), own, own, own, own,
                pl.BlockSpec((1, LANES), lambda j: (0, j))]
    operands = (z, z, z, do, o, lse[0], lse[1], m_init)
    return _call(
        body,
        grid=(npair,),
        in_specs=in_specs + [hbm] * n_dz_in,
        out_specs=[hbm, pl.BlockSpec((1, LANES), lambda j: (0, j))],
        out_shape=[jax.ShapeDtypeStruct((s, IN_WIDTH), BF16), jax.ShapeDtypeStruct((1, npair * LANES), F32)],
        operands=operands + (() if dz is None else (dz,)), name=name,
        scratch_shapes=[pltpu.VMEM((s, LANES), F32)] * 5 + [pltpu.VMEM((2 * BLOCK, 2 * BLOCK), F32)]
        + [pltpu.VMEM((s, LANES), BF16)] * 3 + [pltpu.SemaphoreType.DMA((3,))],
        comm=comm, aliases={} if dz is None else {len(in_specs): 0})


def _adamw_math(w, g, m, v):
    m = ADAM_B1 * m + (1.0 - ADAM_B1) * g
    v = ADAM_B2 * v + (1.0 - ADAM_B2) * (g * g)
    m_hat = m / (1.0 - ADAM_B1 ** ADAM_STEP)
    v_hat = v / (1.0 - ADAM_B2 ** ADAM_STEP)
    delta = -ADAM_LR * (m_hat / (jnp.sqrt(v_hat) + ADAM_EPS) + ADAM_WD * w)
    return delta, m, v


def _adamw(w, g, m, v, name):
    rows, cols = w.shape
    tr = min(rows, 256)

    def body(w_ref, g_ref, m_ref, v_ref, d_ref, nm_ref, nv_ref):
        d_ref[...], nm_ref[...], nv_ref[...] = _adamw_math(w_ref[...], g_ref[...], m_ref[...], v_ref[...])

    spec = pl.BlockSpec((tr, cols), lambda i: (i, 0))
    return pl.pallas_call(
        body,
        grid=(rows // tr,),
        in_specs=[spec] * 4,
        out_specs=[spec] * 3,
        out_shape=[jax.ShapeDtypeStruct((rows, cols), F32)] * 3,
        compiler_params=_params("parallel"),
        name=name,
    )(w, g, m, v)


def _sum_adamw(parts, w, m, v, pos, transpose, name):
    assert len(parts) == DEPTH == 2
    (p0, r0), (p1, r1) = parts
    _, rows, cols = p0.shape
    tr = 256 if rows % 256 == 0 else rows
    nt = rows // tr

    def body(pos_ref, p0_ref, r0_ref, p1_ref, r1_ref, w_ref, m_ref, v_ref, g_ref, d_ref, nm_ref, nv_ref):
        def run(p_ref, r_ref):
            g = ((p_ref[...].astype(F32) + r_ref[0].astype(F32)) + r_ref[1].astype(F32)) + r_ref[2].astype(F32)
            if transpose:
                g = g.T
            g_ref[...] = g
            d_ref[...], nm_ref[...], nv_ref[...] = _adamw_math(w_ref[...], g, m_ref[...], v_ref[...])

        layer0 = pl.program_id(0) < nt
        pl.when(layer0)(lambda: run(p0_ref, r0_ref))
        pl.when(jnp.logical_not(layer0))(lambda: run(p1_ref, r1_ref))

    def tile0(i):
        return jnp.minimum(i, nt - 1)

    def tile1(i):
        return jnp.maximum(i - nt, 0)

    if transpose:
        w_spec = pl.BlockSpec((None, cols, tr), lambda i, q: (i // nt, 0, i % nt))
    else:
        w_spec = pl.BlockSpec((None, tr, cols), lambda i, q: (i // nt, i % nt, 0))
    return pl.pallas_call(
        body,
        grid_spec=pltpu.PrefetchScalarGridSpec(
            num_scalar_prefetch=1,
            grid=(DEPTH * nt,),
            in_specs=[pl.BlockSpec((None, tr, cols), lambda i, q: (q[0], tile0(i), 0)),
                      pl.BlockSpec((3, tr, cols), lambda i, q: (0, tile0(i), 0)),
                      pl.BlockSpec((None, tr, cols), lambda i, q: (q[0], tile1(i), 0)),
                      pl.BlockSpec((3, tr, cols), lambda i, q: (0, tile1(i), 0)),
                      w_spec, w_spec, w_spec],
            out_specs=[w_spec] * 4,
        ),
        out_shape=[jax.ShapeDtypeStruct(w.shape, F32)] * 4,
        compiler_params=_params("arbitrary"),
        name=name,
    )(pos, p0, r0, p1, r1, w, m, v)


def _small_sum_adamw(gathered, params, name):
    _, rows, cols = gathered.shape
    n = len(params)

    def body(ga_ref, *refs):
        ins, outs, (g_scr,) = refs[:3 * n], refs[3 * n:7 * n + 2], refs[7 * n + 2:]
        g = ga_ref[0]
        for i in range(1, N_DEV):
            g = g + ga_ref[i]
        g_scr[...] = g
        for k, (row0, w, _, _) in enumerate(params):
            w_ref, m_ref, v_ref = ins[3 * k:3 * k + 3]
            gk = g_scr[row0:row0 + w.shape[0], :]
            outs[4 * k][...] = gk
            outs[4 * k + 1][...], outs[4 * k + 2][...], outs[4 * k + 3][...] = _adamw_math(
                w_ref[...], gk, m_ref[...], v_ref[...])
        outs[4 * n][...] = g_scr[CONV_ROW:CONV_ROW + 8, :]
        outs[4 * n + 1][...] = g_scr[LOSS_ROW:LOSS_ROW + 1, :]

    out_shape = []
    for _, w, _, _ in params:
        out_shape += [jax.ShapeDtypeStruct(w.shape, F32)] * 4
    out_shape += [jax.ShapeDtypeStruct((8, cols), F32), jax.ShapeDtypeStruct((1, cols), F32)]
    res = pl.pallas_call(
        body,
        out_shape=out_shape,
        scratch_shapes=[pltpu.VMEM((rows, cols), F32)],
        name=name,
    )(gathered, *[t for _, w, m, v in params for t in (w, m, v)])
    return [res[4 * k:4 * k + 4] for k in range(n)], res[4 * n], res[4 * n + 1]


def _pair_sum(g4, r1, pos, name):
    _, _, rows, cols = g4.shape
    tr = min(rows, 512)

    def body(pos_ref, g_ref, r_ref, o_ref):
        o_ref[...] = (g_ref[...].astype(F32) + r_ref[...].astype(F32)).astype(BF16)

    return pl.pallas_call(
        body,
        grid_spec=pltpu.PrefetchScalarGridSpec(
            num_scalar_prefetch=1,
            grid=(4, rows // tr),
            in_specs=[pl.BlockSpec((None, None, tr, cols), lambda i, j, p: (i, p[1], j, 0)),
                      pl.BlockSpec((None, tr, cols), lambda i, j, p: (i, j, 0))],
            out_specs=pl.BlockSpec((None, tr, cols), lambda i, j, p: (i, j, 0)),
        ),
        out_shape=jax.ShapeDtypeStruct((4, rows, cols), BF16),
        compiler_params=_params("parallel", "parallel"),
        name=name,
    )(pos, g4, r1)


def _place():
    return lax.axis_index("x"), lax.axis_index("y"), lax.axis_index("c")


def _gather_comm(shards):
    na = len(shards)

    def plan(ins, outs, sems):
        send_sems, recv_sems, local_sems = sems
        x, y, c = _place()
        me, sibling = (x, y, c), (x, y, 1 - c)
        chips = [(1 - x, y), (x, 1 - y), (1 - x, 1 - y)]

        def rows(a, px, py, pc):
            m = ins[a].shape[0]
            return outs[a].at[pl.ds((4 * px + 2 * py + pc) * m, m), :]

        def copy(a, k, block, to, src=None):
            return pltpu.make_async_remote_copy(
                src_ref=rows(a, *block) if src is None else src, dst_ref=rows(a, *block),
                send_sem=send_sems.at[a, k], recv_sem=recv_sems.at[a, k], device_id=to, device_id_type=MESH)

        mine = [pltpu.make_async_copy(ins[a], rows(a, *me), local_sems.at[a]) for a in range(na)]
        first = []
        for a in range(na):
            first.append(copy(a, 0, me, sibling, src=ins[a]))
            first += [copy(a, 1 + j, me, (*chip, c), src=ins[a]) for j, chip in enumerate(chips)]
        return me, sibling, chips, c, copy, mine, first

    def start(ins, outs, sems):
        *_, mine, first = plan(ins, outs, sems)
        for cp in mine + first:
            cp.start()

    def finish(ins, outs, sems):
        me, sibling, chips, c, copy, mine, first = plan(ins, outs, sems)
        passed = []
        for j, chip in enumerate(chips):
            for a in range(na):
                copy(a, 1 + j, (*chip, c), me).wait_recv()
                cp = copy(a, 4 + j, (*chip, c), sibling)
                cp.start()
                passed.append(cp)
        for a in range(na):
            copy(a, 0, sibling, me).wait_recv()
            for j, chip in enumerate(chips):
                copy(a, 4 + j, (*chip, 1 - c), me).wait_recv()
        for cp in first + passed:
            cp.wait_send()
        for cp in mine:
            cp.wait()

    return _Comm(tuple(shards),
                 tuple(jax.ShapeDtypeStruct((N_DEV * t.shape[0], t.shape[1]), t.dtype) for t in shards),
                 (pltpu.SemaphoreType.DMA((na, 7)), pltpu.SemaphoreType.DMA((na, 7)), pltpu.SemaphoreType.DMA((na,))),
                 start, finish)


def _exchange_comm(arrays, out_shape, n_copies, copies_of):
    na = len(arrays)

    def every(ins, outs, sems):
        send_sems, recv_sems = sems
        return [cp for a in range(na) for cp in copies_of(ins, outs, a, send_sems, recv_sems)]

    def start(ins, outs, sems):
        for cp in every(ins, outs, sems):
            cp.start()

    def finish(ins, outs, sems):
        for cp in every(ins, outs, sems):
            cp.wait()

    return _Comm(tuple(arrays), tuple(out_shape),
                 (pltpu.SemaphoreType.DMA((na, n_copies)), pltpu.SemaphoreType.DMA((na, n_copies))), start, finish)


def _sibling_comm(grads):
    def copies_of(ins, outs, a, send_sems, recv_sems):
        x, y, c = _place()
        return [pltpu.make_async_remote_copy(
            src_ref=ins[a].at[chip, 1 - c], dst_ref=outs[a].at[chip],
            send_sem=send_sems.at[a, chip], recv_sem=recv_sems.at[a, chip],
            device_id=(x, y, 1 - c), device_id_type=MESH) for chip in range(4)]

    return _exchange_comm(grads, [jax.ShapeDtypeStruct((4,) + t.shape[2:], t.dtype) for t in grads], 4, copies_of)


def _chip_comm(partials):
    def copies_of(ins, outs, a, send_sems, recv_sems):
        x, y, c = _place()
        chips = [(1 - x, y), (x, 1 - y), (1 - x, 1 - y)]
        return [pltpu.make_async_remote_copy(
            src_ref=ins[a].at[2 * cx + cy], dst_ref=outs[a].at[k],
            send_sem=send_sems.at[a, k], recv_sem=recv_sems.at[a, k],
            device_id=(cx, cy, c), device_id_type=MESH) for k, (cx, cy) in enumerate(chips)]

    return _exchange_comm(partials, [jax.ShapeDtypeStruct((3,) + t.shape[1:], t.dtype) for t in partials], 3, copies_of)


def _pad_rows(t, rows):
    return jnp.pad(t, ((0, rows - t.shape[0]), (0, D_MODEL - t.shape[1])))


MIX_ROW, GROUP_ROW, MLP_ROW, FINAL_ROW, CONV_ROW, SINK_ROW = 0, 8, 16, 24, 32, 40
LOSS_ROW = FINAL_ROW + 1


def _pack_small(g_mix, g_group, g_mlp, g_final, conv, sinks, loss):
    final_and_loss = jnp.concatenate([g_final.reshape(1, D_MODEL), _pad_rows(loss, 1)], axis=0)
    return jnp.concatenate([
        _pad_rows(g_mix, 8), _pad_rows(g_group, 8), _pad_rows(g_mlp, 8), _pad_rows(final_and_loss, 8),
        _pad_rows(conv.reshape(DEPTH * 3, CONV_CH), 8), _pad_rows(sinks.reshape(1, DEPTH * 6), 8)], axis=0)


def kernel(x, w_in, conv_w, sinks, g_mix, g_group, w_o, g_mlp, w_ff_in, w_ff_out, g_final, loss_target, m_w_in, m_conv_w, m_sinks, m_g_mix, m_g_group, m_w_o, m_g_mlp, m_w_ff_in, m_w_ff_out, m_g_final, v_w_in, v_conv_w, v_sinks, v_g_mix, v_g_group, v_w_o, v_g_mlp, v_w_ff_in, v_w_ff_out, v_g_final):
    ax, ay, ac = _place()
    chip = 2 * ax + ay
    dev = 4 * ax + 2 * ay + ac
    pos = jnp.stack([chip, ac]).astype(jnp.int32)

    x0 = x.reshape(SEQ, D_MODEL)
    target = loss_target.reshape(SEQ, D_MODEL)

    shards = {}
    for l in range(DEPTH):
        shards[l, 0], shards[l, 1] = w_in[l].T.astype(BF16), w_o[l].astype(BF16)
        shards[l, 2], shards[l, 3] = w_ff_in[l].T.astype(BF16), w_ff_out[l].astype(BF16)
    conv_tile = jnp.pad(conv_w.reshape(DEPTH * 3, CONV_CH // N_DEV), ((0, 2), (0, LANES - CONV_CH // N_DEV)))
    wt_in0, conv_all = _comm_only(_gather_comm([shards[0, 0], conv_tile]), "gather_first")
    conv_full = conv_all.reshape(N_DEV, 8, LANES)[:, :DEPTH * 3, :CONV_CH // N_DEV]
    conv_full = conv_full.transpose(1, 0, 2).reshape(DEPTH, 3, CONV_CH)

    dx, parts, small = _step(x0, target, shards, wt_in0, conv_full, sinks, g_mix, g_group, g_mlp, g_final, pos)
    return _finish(dx, parts, small, pos, dev, w_in, conv_w, sinks, g_mix, g_group, w_o, g_mlp, w_ff_in, w_ff_out, g_final, m_w_in, m_conv_w, m_sinks, m_g_mix, m_g_group, m_w_o, m_g_mlp, m_w_ff_in, m_w_ff_out, m_g_final, v_w_in, v_conv_w, v_sinks, v_g_mix, v_g_group, v_w_o, v_g_mlp, v_w_ff_in, v_w_ff_out, v_g_final)


FWD_CARRY = {(0, "in_proj"): ((0, 1),), (0, "window"): ((1, 0),), (0, "dilated"): ((0, 2),),
             (0, "mix_ff_in"): ((1, 1), (0, 3)), (0, "ff_out_in_proj"): ((1, 3),),
             (1, "dilated"): ((1, 2),)}


def _step(x0, target, shards, wt_in0, conv_full, sinks, g_mix, g_group, g_mlp, g_final, pos):
    sink_lanes = jnp.repeat(sinks.reshape(DEPTH, 6), HEAD_DIM, axis=1)
    no_sink = jnp.full((1, A_WIDTH), NEG_BIG, F32)
    full = {(0, 0): wt_in0}

    def gather(stage, l):
        keys = FWD_CARRY.get((l, stage), ())
        return keys, (_gather_comm([shards[k] for k in keys]) if keys else None)

    def landed(keys, got):
        full.update(zip(keys, got))

    saved = []
    xc = x0
    keys, comm = gather("in_proj", 0)
    (z, h), got = _norm_mm(xc, g_mix[0:1], full[0, 0], "in_proj_0", comm)
    landed(keys, got)
    for l in range(DEPTH):
        sink_l = sink_lanes[l:l + 1]
        keys, comm = gather("window", l)
        (yc, *lse_c), got = _attn_fwd(z, sink_l, 1.0, QC_BLK, KC_BLK, VC_BLK, (1,), C_MAX_DIST, True,
                                     f"window_attn_{l}", comm)
        landed(keys, got)
        yb = _conv_fwd(z, conv_full[l], f"conv_{l}")
        keys, comm = gather("dilated", l)
        (ya, *lse_a), got = _attn_fwd(z, no_sink, 0.0, QA_BLK, KA_BLK, VA_BLK, DILATED_PATTERNS, A_MAX_DIST, False,
                                     f"dilated_attn_{l}", comm)
        landed(keys, got)
        keys, comm = gather("mix_ff_in", l)
        (y, x1, a, h2), got = _mix_ff_in(ya, yb, yc, g_group[l:l + 1], full[l, 1], xc, g_mlp[l:l + 1], full[l, 2],
                                         f"mix_ff_in_{l}", comm)
        landed(keys, got)
        saved.append((xc, z, h, ya, lse_a, yb, yc, lse_c, sink_l, y, x1, a, h2))
        if l + 1 < DEPTH:
            keys, comm = gather("ff_out_in_proj", l)
            (xc, z, h), got = _ff_out_in_proj(a, full[l, 3], x1, g_mix[l + 1:l + 2], full[l + 1, 0],
                                              f"ff_out_{l}_in_proj_{l + 1}", comm)
            landed(keys, got)

    loss_slab, dx, dxb, dg_final = _mm_res_loss(a, full[DEPTH - 1, 3], x1, g_final.reshape(1, D_MODEL), target,
                                                f"ff_out_{DEPTH - 1}_loss")

    def by_owner(t):
        return t.reshape(4, 2, t.shape[0] // N_DEV, D_MODEL)

    def pair(key, g, r1):
        return _pair_sum(g, r1, pos, f"grad_pair_sum_{key[0]}_{key[1]}")

    partial, r2 = {}, {}
    dg_mix, dg_group, dg_mlp, dconv, dsinks = [None] * DEPTH, [None] * DEPTH, [None] * DEPTH, [None] * DEPTH, [None] * DEPTH
    for l in reversed(range(DEPTH)):
        xin, z, h, ya, lse_a, yb, yc, lse_c, sink_l, y, x1, a, h2 = saved[l]
        late = [(l + 1, 1), (l + 1, 0)] if l + 1 < DEPTH else []
        (du,), got = _mlp_bwd_act(dxb, full[l, 3], a, f"ff_out_bwd_{l}",
                                  _chip_comm([partial[k] for k in late]) if late else None)
        r2.update(zip(late, got))
        (g3,), _ = _mm_tn(a, dxb, f"grad_w_ff_out_{l}")
        (g2,), _ = _mm_tn(du, h2, f"grad_w_ff_in_{l}")
        g3, g2 = by_owner(g3), by_owner(g2)
        (dx1, dx1b, dg_mlp[l], dya, dyb, dyc, dg_group[l]), got = _ff_in_mix_bwd(
            du, full[l, 2], x1, dx, g_mlp[l:l + 1], full[l, 1], ya, yb, yc, g_group[l:l + 1],
            f"ff_in_mix_bwd_{l}", _sibling_comm([g3, g2]))
        partial[l, 3], partial[l, 2] = pair((l, 3), g3, got[0]), pair((l, 2), g2, got[1])
        (g1,), _ = _mm_tn(y, dx1b, f"grad_w_o_{l}")
        g1 = by_owner(g1)
        early = [(l, 3), (l, 2)]
        (dz, _), got = _attn_bwd(z, dya, ya, lse_a, no_sink, None, QA_BLK, KA_BLK, VA_BLK, DILATED_PATTERNS,
                                 A_MAX_DIST, False, f"dilated_attn_bwd_{l}", _chip_comm([partial[k] for k in early]))
        r2.update(zip(early, got))
        dz, dcw = _conv_bwd(z, conv_full[l], dyb, dz, f"conv_bwd_{l}")
        (dz, dsink), _ = _attn_bwd(z, dyc, yc, lse_c, sink_l, dz, QC_BLK, KC_BLK, VC_BLK, (1,), C_MAX_DIST,
                                   True, f"window_attn_bwd_{l}")
        (g0,), _ = _mm_tn(dz, h, f"grad_w_in_{l}")
        g0 = by_owner(g0)
        if l > 0:
            (dx, dxb, dg_mix[l]), got = _mm_nn_normbwd(dz, full[l, 0], xin, dx1, g_mix[l:l + 1], f"in_proj_bwd_{l}",
                                                      _sibling_comm([g1, g0]))
            partial[l, 1], partial[l, 0] = pair((l, 1), g1, got[0]), pair((l, 0), g0, got[1])
        else:
            got = _comm_only(_sibling_comm([g1, g0]), "grad_sibling_exchange_last")
            partial[l, 1], partial[l, 0] = pair((l, 1), g1, got[0]), pair((l, 0), g0, got[1])
            (dx, dxb, dg_mix[l]), got = _mm_nn_normbwd(dz, full[l, 0], xin, dx1, g_mix[l:l + 1], f"in_proj_bwd_{l}",
                                                      _chip_comm([partial[l, 1], partial[l, 0]]))
            r2[l, 1], r2[l, 0] = got
        dconv[l] = dcw[:3]
        dsinks[l] = dsink[0, ::HEAD_DIM]
    parts = {key: (partial[key], r2[key]) for key in partial}
    small = _pack_small(jnp.concatenate(dg_mix), jnp.concatenate(dg_group), jnp.concatenate(dg_mlp),
                        dg_final, jnp.stack(dconv), jnp.stack(dsinks), loss_slab[0:1])
    return dx, parts, small


def _finish(dx, parts, small, pos, dev, w_in, conv_w, sinks, g_mix, g_group, w_o, g_mlp, w_ff_in, w_ff_out, g_final, m_w_in, m_conv_w, m_sinks, m_g_mix, m_g_group, m_w_o, m_g_mlp, m_w_ff_in, m_w_ff_out, m_g_final, v_w_in, v_conv_w, v_sinks, v_g_mix, v_g_group, v_w_o, v_g_mlp, v_w_ff_in, v_w_ff_out, v_g_final):
    grad_x = dx.reshape(1, SEQ, D_MODEL)

    (small_all,) = _comm_only(_gather_comm([small]), "gather_small_grads")
    row = lambda t: t.reshape(1, D_MODEL)
    sink_row = lambda t: _pad_rows(t.reshape(1, DEPTH * 6), 1)
    params = [(MIX_ROW, g_mix, m_g_mix, v_g_mix), (GROUP_ROW, g_group, m_g_group, v_g_group),
              (MLP_ROW, g_mlp, m_g_mlp, v_g_mlp), (FINAL_ROW, row(g_final), row(m_g_final), row(v_g_final)),
              (SINK_ROW, sink_row(sinks), sink_row(m_sinks), sink_row(v_sinks))]
    updated, conv_rows, loss_row = _small_sum_adamw(small_all.reshape(N_DEV, SMALL_ROWS, D_MODEL), params, "small_adamw")
    loss = loss_row[0, 0]
    (grad_g_mix, delta_g_mix, new_m_g_mix, new_v_g_mix), (grad_g_group, delta_g_group, new_m_g_group, new_v_g_group), \
        (grad_g_mlp, delta_g_mlp, new_m_g_mlp, new_v_g_mlp), final4, sinks4 = updated
    grad_g_final, delta_g_final, new_m_g_final, new_v_g_final = [t.reshape(D_MODEL) for t in final4]
    grad_sinks, delta_sinks, new_m_sinks, new_v_sinks = [t[0, :DEPTH * 6].reshape(DEPTH, 2, 3) for t in sinks4]
    conv_grad_full = conv_rows[:DEPTH * 3, :CONV_CH].reshape(DEPTH, 3, CONV_CH)
    cs = CONV_CH // N_DEV
    grad_conv_w = lax.dynamic_slice_in_dim(conv_grad_full, dev * cs, cs, axis=2)

    def tile_of(t):
        return jnp.pad(t.reshape(1, DEPTH * 3 * cs), ((0, 7), (0, 256 - DEPTH * 3 * cs)))

    cd, cm, cv = _adamw(tile_of(conv_w), tile_of(grad_conv_w), tile_of(m_conv_w), tile_of(v_conv_w), "conv_adamw")
    untile = lambda t: t[0, :DEPTH * 3 * cs].reshape(DEPTH, 3, cs)
    delta_conv_w, new_m_conv_w, new_v_conv_w = untile(cd), untile(cm), untile(cv)

    def big(kind, w, m, v, transpose, name):
        return _sum_adamw([parts[l, kind] for l in range(DEPTH)], w, m, v, pos, transpose, name)

    grad_w_in, delta_w_in, new_m_w_in, new_v_w_in = big(0, w_in, m_w_in, v_w_in, True, "adamw_w_in")
    grad_w_o, delta_w_o, new_m_w_o, new_v_w_o = big(1, w_o, m_w_o, v_w_o, False, "adamw_w_o")
    grad_w_ff_in, delta_w_ff_in, new_m_w_ff_in, new_v_w_ff_in = big(2, w_ff_in, m_w_ff_in, v_w_ff_in, True, "adamw_w_ff_in")
    grad_w_ff_out, delta_w_ff_out, new_m_w_ff_out, new_v_w_ff_out = big(3, w_ff_out, m_w_ff_out, v_w_ff_out, False,
                                                                         "adamw_w_ff_out")

    return (loss, grad_x, grad_w_in, grad_conv_w, grad_sinks, grad_g_mix, grad_g_group, grad_w_o, grad_g_mlp,
            grad_w_ff_in, grad_w_ff_out, grad_g_final,
            delta_w_in, delta_conv_w, delta_sinks, delta_g_mix, delta_g_group, delta_w_o, delta_g_mlp,
            delta_w_ff_in, delta_w_ff_out, delta_g_final,
            new_m_w_in, new_m_conv_w, new_m_sinks, new_m_g_mix, new_m_g_group, new_m_w_o, new_m_g_mlp,
            new_m_w_ff_in, new_m_w_ff_out, new_m_g_final,
            new_v_w_in, new_v_conv_w, new_v_sinks, new_v_g_mix, new_v_g_group, new_v_w_o, new_v_g_mlp,
            new_v_w_ff_in, new_v_w_ff_out, new_v_g_final)
```

```python
from typing import Callable, NamedTuple

import jax
import jax.numpy as jnp
from jax import lax
from jax.experimental import pallas as pl
from jax.experimental.pallas import tpu as pltpu

F32 = jnp.float32
BF16 = jnp.bfloat16
MESH = pl.DeviceIdType.MESH

N_DEV = 8
SEQ = 4096
D_MODEL = 1024
DEPTH = 2
HEAD_DIM = 64
LANES = 128
A_WIDTH = 384
CONV_CH = 256
C_WIDTH = 384
KV_WIDTH = 128
IN_WIDTH = 2560
D_FF = 4096
BLOCK = 128
DILATED_PATTERNS = (1, 4, 16)
A_MAX_DIST = 128
C_MAX_DIST = 127
EPS = 1e-6
SCALE = HEAD_DIM ** -0.5
NEG_BIG = -1e30
F32_TINY = 1.1754944e-38

QA_BLK, KA_BLK, VA_BLK = 0, 3, 6
GB_BLK, GC_BLK, XB_BLK = 9, 11, 13
QC_BLK, KC_BLK, VC_BLK = 15, 18, 19

ADAM_LR = 0.001
ADAM_B1 = 0.9
ADAM_B2 = 0.999
ADAM_EPS = 1e-08
ADAM_WD = 0.01
ADAM_STEP = 10

VMEM_LIMIT = 56 * 1024 * 1024
TILE_BUDGET = 46 * 1024 * 1024
ROW_TILE = 512
COL_CHUNK = 512
SMALL_ROWS = 48


def _dot_nn(a, b):
    return lax.dot_general(a, b, (((1,), (0,)), ((), ())), preferred_element_type=F32)


def _dot_nt(a, b):
    return lax.dot_general(a, b, (((1,), (1,)), ((), ())), preferred_element_type=F32)


def _dot_tn(a, b):
    return lax.dot_general(a, b, (((0,), (0,)), ((), ())), preferred_element_type=F32)


def _params(*sem):
    return pltpu.CompilerParams(dimension_semantics=sem, vmem_limit_bytes=VMEM_LIMIT)


def _resident(shape):
    return pl.BlockSpec(shape, lambda i: (0,) * len(shape), pipeline_mode=pl.Buffered(1))


def _row_tile(row_bytes, resident_bytes):
    for tm in (ROW_TILE, ROW_TILE // 2):
        if 2 * tm * row_bytes + resident_bytes <= TILE_BUDGET:
            return tm
    return ROW_TILE // 4


def _rms_scale(t):
    return lax.rsqrt(jnp.mean(t * t, axis=-1, keepdims=True) + EPS)


def _rms_bwd(n, r, dn):
    return r * (dn - n * jnp.mean(dn * n, axis=-1, keepdims=True))


class _Comm(NamedTuple):
    arrays: tuple
    out_shape: tuple
    sems: tuple
    start: Callable
    finish: Callable


def _call(body, grid, in_specs, out_specs, out_shape, operands, name, scratch_shapes=(), comm=None, aliases=None):
    n_in, n_out, n_scr = len(in_specs), len(out_shape), len(scratch_shapes)
    aliases = dict(aliases or {})
    if comm is None:
        res = pl.pallas_call(body, grid=grid, in_specs=list(in_specs), out_specs=list(out_specs),
                             out_shape=list(out_shape), scratch_shapes=list(scratch_shapes),
                             input_output_aliases=aliases,
                             compiler_params=_params("arbitrary"), name=name)(*operands)
        return list(res), []
    c_in, c_out = len(comm.arrays), len(comm.out_shape)
    hbm = pl.BlockSpec(memory_space=pl.ANY)
    last = grid[0] - 1

    def carried(*refs):
        ins, cins = refs[:n_in], refs[n_in:n_in + c_in]
        o0 = n_in + c_in
        outs, couts = refs[o0:o0 + n_out], refs[o0 + n_out:o0 + n_out + c_out]
        s0 = o0 + n_out + c_out
        scr, sems = refs[s0:s0 + n_scr], refs[s0 + n_scr:]
        pl.when(pl.program_id(0) == 0)(lambda: comm.start(cins, couts, sems))
        body(*ins, *outs, *scr)
        pl.when(pl.program_id(0) == last)(lambda: comm.finish(cins, couts, sems))

    res = pl.pallas_call(carried, grid=grid, in_specs=list(in_specs) + [hbm] * c_in,
                         out_specs=list(out_specs) + [hbm] * c_out, out_shape=list(out_shape) + list(comm.out_shape),
                         scratch_shapes=list(scratch_shapes) + list(comm.sems), input_output_aliases=aliases,
                         compiler_params=_params("arbitrary"), name=name)(*operands, *comm.arrays)
    return list(res[:n_out]), list(res[n_out:])


def _comm_only(comm, name):
    hbm = pl.BlockSpec(memory_space=pl.ANY)
    c_in, c_out = len(comm.arrays), len(comm.out_shape)

    def body(*refs):
        ins, outs, sems = refs[:c_in], refs[c_in:c_in + c_out], refs[c_in + c_out:]
        comm.start(ins, outs, sems)
        comm.finish(ins, outs, sems)

    return pl.pallas_call(body, in_specs=[hbm] * c_in, out_specs=[hbm] * c_out, out_shape=list(comm.out_shape),
                          scratch_shapes=list(comm.sems), name=name)(*comm.arrays)


def _norm_mm(x, g, wt, name, comm=None):
    s, d = x.shape
    n = wt.shape[0]
    tm = _row_tile(4 * d + 4 * n + 2 * d, 2 * n * d)

    def body(x_ref, g_ref, w_ref, o_ref, h_ref):
        xx = x_ref[...]
        h = ((xx * _rms_scale(xx)) * g_ref[...]).astype(BF16)
        h_ref[...] = h
        for n0 in range(0, n, COL_CHUNK):
            o_ref[:, n0:n0 + COL_CHUNK] = _dot_nt(h, w_ref[n0:n0 + COL_CHUNK, :])

    return _call(
        body,
        grid=(s // tm,),
        in_specs=[pl.BlockSpec((tm, d), lambda i: (i, 0)),
                  pl.BlockSpec((1, d), lambda i: (0, 0)),
                  _resident((n, d))],
        out_specs=[pl.BlockSpec((tm, n), lambda i: (i, 0)),
                   pl.BlockSpec((tm, d), lambda i: (i, 0))],
        out_shape=[jax.ShapeDtypeStruct((s, n), F32), jax.ShapeDtypeStruct((s, d), BF16)],
        operands=(x, g, wt), name=name, comm=comm)


def _ff_out_in_proj(a, w2, x1, g, wt, name, comm=None):
    s, f = a.shape
    d = w2.shape[1]
    n = wt.shape[0]
    tm = _row_tile(2 * f + 4 * d + 4 * d + 4 * n + 2 * d, 2 * f * d + 2 * n * d)

    def body(a_ref, w2_ref, x_ref, g_ref, w_ref, x2_ref, z_ref, h_ref):
        x2 = x_ref[...] + _dot_nn(a_ref[...], w2_ref[...])
        x2_ref[...] = x2
        h = ((x2 * _rms_scale(x2)) * g_ref[...]).astype(BF16)
        h_ref[...] = h
        for n0 in range(0, n, COL_CHUNK):
            z_ref[:, n0:n0 + COL_CHUNK] = _dot_nt(h, w_ref[n0:n0 + COL_CHUNK, :])

    rows = lambda w: pl.BlockSpec((tm, w), lambda i: (i, 0))
    return _call(
        body,
        grid=(s // tm,),
        in_specs=[rows(f), _resident((f, d)), rows(d), pl.BlockSpec((1, d), lambda i: (0, 0)), _resident((n, d))],
        out_specs=[rows(d), rows(n), rows(d)],
        out_shape=[jax.ShapeDtypeStruct((s, d), F32), jax.ShapeDtypeStruct((s, n), F32),
                   jax.ShapeDtypeStruct((s, d), BF16)],
        operands=(a, w2, x1, g, wt), name=name, comm=comm)


def _mix_ff_in(ya, yb, yc, gg, wo, x0, g_mlp, wt1, name, comm=None):
    s = ya.shape[0]
    d = wo.shape[1]
    f = wt1.shape[0]
    tm = _row_tile(4 * d + 4 * d + 2 * d + 4 * d + 2 * d + 2 * f, 2 * d * d + 2 * f * d)

    def body(ya_ref, yb_ref, yc_ref, gg_ref, wo_ref, x_ref, g_ref, w1_ref, y_ref, x1_ref, a_ref, h_ref):
        parts = []
        for ref in (ya_ref, yb_ref, yc_ref):
            t = ref[...]
            parts.append(t * _rms_scale(t))
        y = (jnp.concatenate(parts, axis=1) * gg_ref[...]).astype(BF16)
        y_ref[...] = y
        x1 = x_ref[...] + _dot_nn(y, wo_ref[...])
        x1_ref[...] = x1
        h = ((x1 * _rms_scale(x1)) * g_ref[...]).astype(BF16)
        h_ref[...] = h
        for n0 in range(0, f, COL_CHUNK):
            u = _dot_nt(h, w1_ref[n0:n0 + COL_CHUNK, :])
            a_ref[:, n0:n0 + COL_CHUNK] = jnp.square(jnp.maximum(u, 0.0)).astype(BF16)

    rows = lambda w: pl.BlockSpec((tm, w), lambda i: (i, 0))
    vec = pl.BlockSpec((1, d), lambda i: (0, 0))
    return _call(
        body,
        grid=(s // tm,),
        in_specs=[rows(A_WIDTH), rows(CONV_CH), rows(C_WIDTH), vec, _resident((d, d)), rows(d), vec, _resident((f, d))],
        out_specs=[rows(d), rows(d), rows(f), rows(d)],
        out_shape=[jax.ShapeDtypeStruct((s, d), BF16), jax.ShapeDtypeStruct((s, d), F32),
                   jax.ShapeDtypeStruct((s, f), BF16), jax.ShapeDtypeStruct((s, d), BF16)],
        operands=(ya, yb, yc, gg, wo, x0, g_mlp, wt1), name=name, comm=comm)


def _mm_res_loss(a, w2, x1, g, target, name):
    s, f = a.shape
    d = w2.shape[1]
    tm = _row_tile(2 * f + 4 * d + 4 * d + 4 * d + 2 * d, 2 * f * d)

    def body(a_ref, w_ref, x_ref, g_ref, t_ref, loss_ref, dx_ref, dxb_ref, dg_ref):
        @pl.when(pl.program_id(0) == 0)
        def _():
            loss_ref[...] = jnp.zeros_like(loss_ref)
            dg_ref[...] = jnp.zeros_like(dg_ref)

        xx = x_ref[...] + _dot_nn(a_ref[...], w_ref[...])
        r = _rms_scale(xx)
        n = xx * r
        gv = g_ref[...]
        err = n * gv - t_ref[...]
        per_tok = jnp.sum(err * err, axis=1, keepdims=True) * (1.0 / d)
        loss_ref[...] += 0.5 * jnp.sum(per_tok, axis=0, keepdims=True)
        dout = err * (1.0 / d)
        dg_ref[...] += jnp.sum(dout * n, axis=0, keepdims=True)
        dx = _rms_bwd(n, r, dout * gv)
        dx_ref[...] = dx
        dxb_ref[...] = dx.astype(BF16)

    return pl.pallas_call(
        body,
        grid=(s // tm,),
        in_specs=[pl.BlockSpec((tm, f), lambda i: (i, 0)),
                  _resident((f, d)),
                  pl.BlockSpec((tm, d), lambda i: (i, 0)),
                  pl.BlockSpec((1, d), lambda i: (0, 0)),
                  pl.BlockSpec((tm, d), lambda i: (i, 0))],
        out_specs=[pl.BlockSpec((8, LANES), lambda i: (0, 0)),
                   pl.BlockSpec((tm, d), lambda i: (i, 0)),
                   pl.BlockSpec((tm, d), lambda i: (i, 0)),
                   pl.BlockSpec((1, d), lambda i: (0, 0))],
        out_shape=[jax.ShapeDtypeStruct((8, LANES), F32), jax.ShapeDtypeStruct((s, d), F32),
                   jax.ShapeDtypeStruct((s, d), BF16), jax.ShapeDtypeStruct((1, d), F32)],
        compiler_params=_params("arbitrary"),
        name=name,
    )(a, w2, x1, g, target)


def _mlp_bwd_act(dxb, w2, a, name, comm=None):
    s, d = dxb.shape
    f = w2.shape[0]
    tm = _row_tile(2 * d + 2 * f + 2 * f, 2 * f * d)

    def body(dx_ref, w_ref, a_ref, du_ref):
        dx = dx_ref[...]
        for n0 in range(0, f, COL_CHUNK):
            da = _dot_nt(dx, w_ref[n0:n0 + COL_CHUNK, :])
            av = a_ref[:, n0:n0 + COL_CHUNK].astype(F32)
            rl = av * lax.rsqrt(jnp.maximum(av, F32_TINY))
            du_ref[:, n0:n0 + COL_CHUNK] = (da * (2.0 * rl)).astype(BF16)

    return _call(
        body,
        grid=(s // tm,),
        in_specs=[pl.BlockSpec((tm, d), lambda i: (i, 0)),
                  _resident((f, d)),
                  pl.BlockSpec((tm, f), lambda i: (i, 0))],
        out_specs=[pl.BlockSpec((tm, f), lambda i: (i, 0))],
        out_shape=[jax.ShapeDtypeStruct((s, f), BF16)],
        operands=(dxb, w2, a), name=name, comm=comm)


def _mm_tn(a, b, name, comm=None):
    s, n = a.shape
    d = b.shape[1]
    tn = 512

    def body(a_ref, b_ref, o_ref, acc):
        for k0 in range(0, s, ROW_TILE):
            part = _dot_tn(a_ref[k0:k0 + ROW_TILE, :], b_ref[k0:k0 + ROW_TILE, :])
            if k0 == 0:
                acc[...] = part
            else:
                acc[...] += part
        o_ref[...] = acc[...].astype(BF16)

    return _call(
        body,
        grid=(n // tn,),
        in_specs=[pl.BlockSpec((s, tn), lambda j: (0, j)),
                  _resident((s, d))],
        out_specs=[pl.BlockSpec((tn, d), lambda j: (j, 0))],
        out_shape=[jax.ShapeDtypeStruct((n, d), BF16)],
        operands=(a, b), name=name, scratch_shapes=[pltpu.VMEM((tn, d), F32)], comm=comm)


def _mm_nn_normbwd(dact, wt, x, dres, g, name, comm=None):
    s, kdim = dact.shape
    d = wt.shape[1]
    tm = _row_tile(2 * kdim + 4 * d + 4 * d + 4 * d + 2 * d, 2 * kdim * d)

    def body(a_ref, w_ref, x_ref, r_ref, g_ref, o_ref, ob_ref, dg_ref):
        @pl.when(pl.program_id(0) == 0)
        def _():
            dg_ref[...] = jnp.zeros_like(dg_ref)

        dh = _dot_nn(a_ref[...], w_ref[...])
        xx = x_ref[...]
        r = _rms_scale(xx)
        n = xx * r
        dg_ref[...] += jnp.sum(dh * n, axis=0, keepdims=True)
        dx = r_ref[...] + _rms_bwd(n, r, dh * g_ref[...])
        o_ref[...] = dx
        ob_ref[...] = dx.astype(BF16)

    return _call(
        body,
        grid=(s // tm,),
        in_specs=[pl.BlockSpec((tm, kdim), lambda i: (i, 0)),
                  _resident((kdim, d)),
                  pl.BlockSpec((tm, d), lambda i: (i, 0)),
                  pl.BlockSpec((tm, d), lambda i: (i, 0)),
                  pl.BlockSpec((1, d), lambda i: (0, 0))],
        out_specs=[pl.BlockSpec((tm, d), lambda i: (i, 0)),
                   pl.BlockSpec((tm, d), lambda i: (i, 0)),
                   pl.BlockSpec((1, d), lambda i: (0, 0))],
        out_shape=[jax.ShapeDtypeStruct((s, d), F32), jax.ShapeDtypeStruct((s, d), BF16),
                   jax.ShapeDtypeStruct((1, d), F32)],
        operands=(dact, wt, x, dres, g), name=name, comm=comm)


def _ff_in_mix_bwd(du, wt1, x1, dres, g_mlp, wo, ya, yb, yc, gg, name, comm=None):
    s, f = du.shape
    d = wt1.shape[1]
    widths = (A_WIDTH, CONV_CH, C_WIDTH)
    tm = _row_tile(2 * f + 4 * d + 4 * d + 4 * d + 2 * d + 4 * d + 4 * d, 2 * f * d + 2 * d * d)

    def body(du_ref, w1_ref, x_ref, r_ref, g_ref, wo_ref, ya_ref, yb_ref, yc_ref, gg_ref,
             dx_ref, dxb_ref, dg_ref, da_ref, db_ref, dc_ref, dgg_ref):
        @pl.when(pl.program_id(0) == 0)
        def _():
            dg_ref[...] = jnp.zeros_like(dg_ref)
            dgg_ref[...] = jnp.zeros_like(dgg_ref)

        dh = _dot_nn(du_ref[...], w1_ref[...])
        xx = x_ref[...]
        r = _rms_scale(xx)
        n = xx * r
        dg_ref[...] += jnp.sum(dh * n, axis=0, keepdims=True)
        dx = r_ref[...] + _rms_bwd(n, r, dh * g_ref[...])
        dx_ref[...] = dx
        dxb = dx.astype(BF16)
        dxb_ref[...] = dxb

        dy = _dot_nt(dxb, wo_ref[...])
        gv = gg_ref[...]
        off = 0
        dgs = []
        for ref, out, w in zip((ya_ref, yb_ref, yc_ref), (da_ref, db_ref, dc_ref), widths):
            t = ref[...]
            r = _rms_scale(t)
            n = t * r
            dyg = dy[:, off:off + w]
            dgs.append(jnp.sum(dyg * n, axis=0, keepdims=True))
            out[...] = _rms_bwd(n, r, dyg * gv[:, off:off + w])
            off += w
        dgg_ref[...] += jnp.concatenate(dgs, axis=1)

    rows = lambda w: pl.BlockSpec((tm, w), lambda i: (i, 0))
    vec = pl.BlockSpec((1, d), lambda i: (0, 0))
    return _call(
        body,
        grid=(s // tm,),
        in_specs=[rows(f), _resident((f, d)), rows(d), rows(d), vec, _resident((d, d)),
                  rows(A_WIDTH), rows(CONV_CH), rows(C_WIDTH), vec],
        out_specs=[rows(d), rows(d), vec, rows(A_WIDTH), rows(CONV_CH), rows(C_WIDTH), vec],
        out_shape=[jax.ShapeDtypeStruct((s, d), F32), jax.ShapeDtypeStruct((s, d), BF16), jax.ShapeDtypeStruct((1, d), F32),
                   jax.ShapeDtypeStruct((s, A_WIDTH), F32), jax.ShapeDtypeStruct((s, CONV_CH), F32),
                   jax.ShapeDtypeStruct((s, C_WIDTH), F32), jax.ShapeDtypeStruct((1, d), F32)],
        operands=(du, wt1, x1, dres, g_mlp, wo, ya, yb, yc, gg), name=name, comm=comm)


CONV_CHUNK = 256
CONV_HALO = 8


def _conv_fwd(z, cw, name):
    s = z.shape[0]
    nch = s // CONV_CHUNK

    def body(gb_ref, gc_ref, xb_ref, w_ref, o_ref, us):
        us[pl.ds(0, CONV_HALO), :] = jnp.zeros((CONV_HALO, LANES), F32)
        us[pl.ds(CONV_HALO, s), :] = gc_ref[...] * xb_ref[...]
        w0, w1, w2 = w_ref[0:1, :], w_ref[1:2, :], w_ref[2:3, :]

        def chunk(c, carry):
            st = pl.multiple_of(c * CONV_CHUNK, CONV_CHUNK)
            ext = us[pl.ds(st, CONV_CHUNK + CONV_HALO), :]
            y = (w0 * ext[CONV_HALO - 2:CONV_HALO - 2 + CONV_CHUNK]
                 + w1 * ext[CONV_HALO - 1:CONV_HALO - 1 + CONV_CHUNK]
                 + w2 * ext[CONV_HALO:])
            o_ref[pl.ds(st, CONV_CHUNK), :] = gb_ref[pl.ds(st, CONV_CHUNK), :] * y
            return carry

        lax.fori_loop(0, nch, chunk, 0)

    col = lambda blk: pl.BlockSpec((s, LANES), lambda j, blk=blk: (0, blk + j))
    return pl.pallas_call(
        body,
        grid=(CONV_CH // LANES,),
        in_specs=[col(GB_BLK), col(GC_BLK), col(XB_BLK), pl.BlockSpec((3, LANES), lambda j: (0, j))],
        out_specs=pl.BlockSpec((s, LANES), lambda j: (0, j)),
        out_shape=jax.ShapeDtypeStruct((s, CONV_CH), F32),
        scratch_shapes=[pltpu.VMEM((s + CONV_HALO, LANES), F32)],
        compiler_params=_params("parallel"),
        name=name,
    )(z, z, z, cw)


def _conv_bwd(z, cw, dyb, dz, name):
    s = z.shape[0]
    nch = s // CONV_CHUNK
    ncol = CONV_CH // LANES

    def body(gb_ref, gc_ref, xb_ref, w_ref, dy_ref, dz_in, dz_ref, dw_ref, us, ds_, dgb_ref, dgc_ref, dxb_ref, sems):
        j = pl.program_id(0)

        def to_dz(staged, blk, k):
            cols = pl.ds(pl.multiple_of((blk + j) * LANES, LANES), LANES)
            return pltpu.make_async_copy(staged, dz_ref.at[:, cols], sems.at[k])

        copies = [to_dz(dgb_ref, GB_BLK, 0), to_dz(dgc_ref, GC_BLK, 1), to_dz(dxb_ref, XB_BLK, 2)]

        @pl.when(j > 0)
        def _():
            for cp in copies:
                cp.wait()

        us[pl.ds(0, CONV_HALO), :] = jnp.zeros((CONV_HALO, LANES), F32)
        us[pl.ds(CONV_HALO, s), :] = gc_ref[...] * xb_ref[...]
        ds_[pl.ds(s, CONV_HALO), :] = jnp.zeros((CONV_HALO, LANES), F32)
        ds_[pl.ds(0, s), :] = dy_ref[...] * gb_ref[...]
        w0, w1, w2 = w_ref[0:1, :], w_ref[1:2, :], w_ref[2:3, :]
        zero = jnp.zeros((1, LANES), F32)

        def chunk(c, carry):
            a0, a1, a2 = carry
            st = pl.multiple_of(c * CONV_CHUNK, CONV_CHUNK)
            rows = pl.ds(st, CONV_CHUNK)
            ext = us[pl.ds(st, CONV_CHUNK + CONV_HALO), :]
            um2 = ext[CONV_HALO - 2:CONV_HALO - 2 + CONV_CHUNK]
            um1 = ext[CONV_HALO - 1:CONV_HALO - 1 + CONV_CHUNK]
            u0 = ext[CONV_HALO:]
            dext = ds_[pl.ds(st, CONV_CHUNK + CONV_HALO), :]
            dc0 = dext[:CONV_CHUNK]
            du = w2 * dc0 + w1 * dext[1:1 + CONV_CHUNK] + w0 * dext[2:2 + CONV_CHUNK]
            yconv = w0 * um2 + w1 * um1 + w2 * u0
            dgb_ref[rows, :] = (dy_ref[rows, :] * yconv).astype(BF16)
            dgc_ref[rows, :] = (du * xb_ref[rows, :]).astype(BF16)
            dxb_ref[rows, :] = (du * gc_ref[rows, :]).astype(BF16)
            a0 = a0 + jnp.sum(dc0 * um2, axis=0, keepdims=True)
            a1 = a1 + jnp.sum(dc0 * um1, axis=0, keepdims=True)
            a2 = a2 + jnp.sum(dc0 * u0, axis=0, keepdims=True)
            return a0, a1, a2

        a0, a1, a2 = lax.fori_loop(0, nch, chunk, (zero, zero, zero))
        dw_ref[...] = jnp.concatenate([a0, a1, a2, jnp.zeros((5, LANES), F32)], axis=0)
        for cp in copies:
            cp.start()

        @pl.when(j == ncol - 1)
        def _():
            for cp in copies:
                cp.wait()

    col = lambda blk: pl.BlockSpec((s, LANES), lambda j, blk=blk: (0, blk + j))
    hbm = pl.BlockSpec(memory_space=pl.ANY)
    return pl.pallas_call(
        body,
        grid=(ncol,),
        in_specs=[col(GB_BLK), col(GC_BLK), col(XB_BLK), pl.BlockSpec((3, LANES), lambda j: (0, j)),
                  pl.BlockSpec((s, LANES), lambda j: (0, j)), hbm],
        out_specs=[hbm, pl.BlockSpec((8, LANES), lambda j: (0, j))],
        out_shape=[jax.ShapeDtypeStruct(dz.shape, dz.dtype), jax.ShapeDtypeStruct((8, CONV_CH), F32)],
        scratch_shapes=[pltpu.VMEM((s + CONV_HALO, LANES), F32), pltpu.VMEM((s + CONV_HALO, LANES), F32)]
        + [pltpu.VMEM((s, LANES), BF16)] * 3 + [pltpu.SemaphoreType.DMA((3,))],
        input_output_aliases={5: 0},
        compiler_params=_params("arbitrary"),
        name=name,
    )(z, z, z, cw, dyb, dz)


ATTN_ROWS = 512
ATTN_UNROLL = 8


def _band_rows(b, d, r):
    base = pl.multiple_of(b * (BLOCK * d), BLOCK)
    prev = jnp.maximum(base - BLOCK * d, 0)
    if d == 1:
        return pl.ds(base, BLOCK), pl.ds(pl.multiple_of(prev, BLOCK), BLOCK)
    return pl.ds(base + r, BLOCK, stride=d), pl.ds(prev + r, BLOCK, stride=d)


def _write_band_bias(bias_ref, max_dist):
    qi = lax.broadcasted_iota(jnp.int32, (BLOCK, 2 * BLOCK), 0)
    kj = lax.broadcasted_iota(jnp.int32, (BLOCK, 2 * BLOCK), 1)
    dist = BLOCK + qi - kj
    band = (dist >= 0) & (dist <= max_dist)
    bias_ref[0:BLOCK, :] = jnp.where(band, 0.0, -jnp.inf)
    bias_ref[BLOCK:2 * BLOCK, :] = jnp.where(band & (kj >= BLOCK), 0.0, -jnp.inf)


def _band_bias(bias_ref, b):
    bias = bias_ref[pl.ds(pl.multiple_of(jnp.where(b > 0, 0, BLOCK), BLOCK), BLOCK), :]
    return jnp.concatenate([bias, bias], axis=0)


def _kv_halves(pair):
    zero = jnp.zeros((1, LANES), jnp.int32)
    return zero + (pair >> 1), zero + ((pair + 1) >> 1)


def _stack_heads(t, head0, halves=None):
    top, bottom = jnp.where(head0, t, 0.0), jnp.where(head0, 0.0, t)
    if halves is not None:
        top = jnp.where(halves[0] == 1, pltpu.roll(top, HEAD_DIM, 1), top)
        bottom = jnp.where(halves[1] == 0, pltpu.roll(bottom, HEAD_DIM, 1), bottom)
    return jnp.concatenate([top, bottom], axis=0).astype(BF16)


def _unstack_heads(t, head0, halves=None):
    top, bottom = t[:BLOCK], t[BLOCK:]
    if halves is not None:
        top = jnp.where(halves[0] == 1, pltpu.roll(top, HEAD_DIM, 1), top)
        bottom = jnp.where(halves[1] == 0, pltpu.roll(bottom, HEAD_DIM, 1), bottom)
    return jnp.where(head0, top, bottom)


def _block_loops(s, patterns, unroll, one_block):
    for n, d in enumerate(patterns):
        nb = (s // BLOCK) // d
        ur = min(unroll, d)
        ub = unroll // ur
        for r0 in range(0, d, ur):
            def trip(i, carry, n=n, d=d, r0=r0, ur=ur, ub=ub):
                for u in range(ub):
                    for r in range(r0, r0 + ur):
                        one_block(i * ub + u, d, r, n == 0)
                return carry
            lax.fori_loop(0, nb // ub, trip, 0)


def _attn_fwd(z, m_init, l_init, q_blk, k_blk, v_blk, patterns, max_dist, gqa, name, comm=None):
    s = z.shape[0]
    npair = 3

    def body(q_ref, k_ref, v_ref, mi_ref, o_ref, lse0_ref, lse1_ref, bias_scr, m_scr, l_scr, *kv_scr):
        head0 = lax.broadcasted_iota(jnp.int32, (1, LANES), 1) < HEAD_DIM
        _write_band_bias(bias_scr, max_dist)
        ones = jnp.ones((2 * BLOCK, LANES), BF16)
        k_src, v_src = kv_scr if gqa else (k_ref, v_ref)
        if gqa:
            half = (lax.broadcasted_iota(jnp.int32, (1, LANES), 1) >= HEAD_DIM).astype(jnp.int32)
            swap = ((pl.program_id(0) + half) >> 1) != half

            def expand(c, carry):
                rows = pl.ds(pl.multiple_of(c * ATTN_ROWS, ATTN_ROWS), ATTN_ROWS)
                k_src[rows, :] = jnp.where(swap, pltpu.roll(k_ref[rows, :], HEAD_DIM, 1), k_ref[rows, :])
                v_src[rows, :] = jnp.where(swap, pltpu.roll(v_ref[rows, :], HEAD_DIM, 1), v_ref[rows, :])
                return carry

            lax.fori_loop(0, s // ATTN_ROWS, expand, 0)

        def one_block(b, d, r, first):
            rq, rp = _band_rows(b, d, r)
            q2 = _stack_heads(q_ref[rq, :] * SCALE, head0)
            k2 = jnp.concatenate([k_src[rp, :], k_src[rq, :]], axis=0).astype(BF16)
            v2 = jnp.concatenate([v_src[rp, :], v_src[rq, :]], axis=0).astype(BF16)
            sc = _dot_nt(q2, k2) + _band_bias(bias_scr, b)
            mb = jnp.max(sc, axis=1, keepdims=True)
            p = jnp.exp(sc - mb).astype(BF16)
            ob = _dot_nn(p, jnp.concatenate([v2, ones], axis=1))
            m_blk = _unstack_heads(jnp.broadcast_to(mb, (2 * BLOCK, LANES)), head0)
            l_blk = _unstack_heads(ob[:, LANES:], head0)
            o_blk = _unstack_heads(ob[:, :LANES], head0)
            if first and l_init == 0.0:
                m_new, l_new, o_new = m_blk, l_blk, o_blk
            else:
                if first:
                    m_old, l_old, o_old = jnp.broadcast_to(mi_ref[...], (BLOCK, LANES)), l_init, 0.0
                else:
                    m_old, l_old, o_old = m_scr[rq, :], l_scr[rq, :], o_ref[rq, :]
                m_new = jnp.maximum(m_old, m_blk)
                a_old = jnp.exp(m_old - m_new)
                a_blk = jnp.exp(m_blk - m_new)
                l_new = l_old * a_old + l_blk * a_blk
                o_new = o_old * a_old + o_blk * a_blk
            o_ref[rq, :], l_scr[rq, :], m_scr[rq, :] = o_new, l_new, m_new

        _block_loops(s, patterns, ATTN_UNROLL, one_block)

        def fin(c, carry):
            rows = pl.ds(pl.multiple_of(c * ATTN_ROWS, ATTN_ROWS), ATTN_ROWS)
            l = l_scr[rows, :]
            o_ref[rows, :] = o_ref[rows, :] / l
            lse = m_scr[rows, :] + jnp.log(l)
            swapped = pltpu.roll(lse, HEAD_DIM, 1)
            lse0_ref[rows, :] = jnp.where(head0, lse, swapped)
            lse1_ref[rows, :] = jnp.where(head0, swapped, lse)
            return carry

        lax.fori_loop(0, s // ATTN_ROWS, fin, 0)

    kv = (lambda blk: pl.BlockSpec((s, LANES), lambda j, blk=blk: (0, blk), pipeline_mode=pl.Buffered(1))) if gqa \
        else (lambda blk: pl.BlockSpec((s, LANES), lambda j, blk=blk: (0, blk + j)))
    own = pl.BlockSpec((s, LANES), lambda j: (0, j))
    return _call(
        body,
        grid=(npair,),
        in_specs=[pl.BlockSpec((s, LANES), lambda j: (0, q_blk + j)), kv(k_blk), kv(v_blk),
                  pl.BlockSpec((1, LANES), lambda j: (0, j))],
        out_specs=[own, own, own],
        out_shape=[jax.ShapeDtypeStruct((s, npair * LANES), F32)] * 3,
        operands=(z, z, z, m_init), name=name,
        scratch_shapes=[pltpu.VMEM((2 * BLOCK, 2 * BLOCK), F32)] + [pltpu.VMEM((s, LANES), F32)] * (4 if gqa else 2),
        comm=comm)


def _attn_bwd(z, do, o, lse, m_init, dz, q_blk, k_blk, v_blk, patterns, max_dist, gqa, name, comm=None):
    s = z.shape[0]
    npair = 3
    n_dz_in = 0 if dz is None else 1

    def body(q_ref, k_ref, v_ref, do_ref, o_ref, lse0_ref, lse1_ref, mi_ref, *rest):
        (dz_ref, dm_ref, dq_acc, dk_acc, dv_acc, dl0_scr, dl1_scr, bias_scr,
         dq_out, dk_out, dv_out, out_sems) = rest[n_dz_in:]
        pair = pl.program_id(0)
        head0 = lax.broadcasted_iota(jnp.int32, (1, LANES), 1) < HEAD_DIM
        halves = _kv_halves(pair) if gqa else None
        _write_band_bias(bias_scr, max_dist)

        def zero_kv():
            def f(c, carry):
                rows = pl.ds(pl.multiple_of(c * ATTN_ROWS, ATTN_ROWS), ATTN_ROWS)
                dk_acc[rows, :] = jnp.zeros((ATTN_ROWS, LANES), F32)
                dv_acc[rows, :] = jnp.zeros((ATTN_ROWS, LANES), F32)
                return carry
            lax.fori_loop(0, s // ATTN_ROWS, f, 0)

        if gqa:
            pl.when(pair == 0)(zero_kv)
        else:
            zero_kv()

        def prep(c, dm):
            rows = pl.ds(pl.multiple_of(c * ATTN_ROWS, ATTN_ROWS), ATTN_ROWS)
            dq_acc[rows, :] = jnp.zeros((ATTN_ROWS, LANES), F32)
            prod = do_ref[rows, :] * o_ref[rows, :]
            d0 = jnp.sum(jnp.where(head0, prod, 0.0), axis=1, keepdims=True)
            d1 = jnp.sum(jnp.where(head0, 0.0, prod), axis=1, keepdims=True)
            dl0_scr[rows, :] = jnp.broadcast_to(d0, (ATTN_ROWS, LANES))
            dl1_scr[rows, :] = jnp.broadcast_to(d1, (ATTN_ROWS, LANES))
            lse_own = jnp.where(head0, lse0_ref[rows, :], lse1_ref[rows, :])
            psink = jnp.exp(mi_ref[...] - lse_own)
            return dm - jnp.sum(psink * jnp.where(head0, d0, d1), axis=0, keepdims=True)

        dm_ref[...] = lax.fori_loop(0, s // ATTN_ROWS, prep, jnp.zeros((1, LANES), F32))

        def one_block(b, d, r, first):
            rq, rp = _band_rows(b, d, r)
            q2 = _stack_heads(q_ref[rq, :] * SCALE, head0, halves)
            do2 = _stack_heads(do_ref[rq, :], head0, halves)
            k2 = jnp.concatenate([k_ref[rp, :], k_ref[rq, :]], axis=0).astype(BF16)
            v2 = jnp.concatenate([v_ref[rp, :], v_ref[rq, :]], axis=0).astype(BF16)
            lse2 = jnp.concatenate([lse0_ref[rq, :], lse1_ref[rq, :]], axis=0)
            dl2 = jnp.concatenate([dl0_scr[rq, :], dl1_scr[rq, :]], axis=0)
            lse2 = jnp.concatenate([lse2, lse2], axis=1)
            dl2 = jnp.concatenate([dl2, dl2], axis=1)
            p = jnp.exp(_dot_nt(q2, k2) + _band_bias(bias_scr, b) - lse2)
            dp = _dot_nt(do2, v2)
            dsc = (p * (dp - dl2)).astype(BF16)
            dq2 = _unstack_heads(_dot_nn(dsc, k2), head0, halves)
            dk2 = _dot_tn(dsc, q2)
            dv2 = _dot_tn(p.astype(BF16), do2)
            dq_acc[rq, :] += dq2 * SCALE
            dk_acc[rp, :] += dk2[:BLOCK]
            dk_acc[rq, :] += dk2[BLOCK:]
            dv_acc[rp, :] += dv2[:BLOCK]
            dv_acc[rq, :] += dv2[BLOCK:]

        _block_loops(s, patterns, ATTN_UNROLL, one_block)

        def to_dz(staged, blk, k):
            cols = pl.ds(pl.multiple_of(blk * LANES, LANES), LANES)
            return pltpu.make_async_copy(staged, dz_ref.at[:, cols], out_sems.at[k])

        last_pair = pair == npair - 1
        q_copy = to_dz(dq_out, q_blk + pair, 0)
        kv_copies = [to_dz(dk_out, k_blk + (0 if gqa else pair), 1), to_dz(dv_out, v_blk + (0 if gqa else pair), 2)]

        @pl.when(pair > 0)
        def _():
            for cp in [q_copy] + ([] if gqa else kv_copies):
                cp.wait()

        def stage(acc, out):
            def f(c, carry):
                rows = pl.ds(pl.multiple_of(c * ATTN_ROWS, ATTN_ROWS), ATTN_ROWS)
                out[rows, :] = acc[rows, :].astype(BF16)
                return carry
            lax.fori_loop(0, s // ATTN_ROWS, f, 0)

        def stage_kv():
            stage(dk_acc, dk_out)
            stage(dv_acc, dv_out)
            for cp in kv_copies:
                cp.start()

        stage(dq_acc, dq_out)
        q_copy.start()
        if gqa:
            pl.when(last_pair)(stage_kv)
        else:
            stage_kv()

        @pl.when(last_pair)
        def _():
            for cp in [q_copy] + kv_copies:
                cp.wait()

    own = pl.BlockSpec((s, LANES), lambda j: (0, j))
    hbm = pl.BlockSpec(memory_space=pl.ANY)
    if gqa:
        kv = lambda blk: pl.BlockSpec((s, LANES), lambda j, blk=blk: (0, blk), pipeline_mode=pl.Buffered(1))
    else:
        kv = lambda blk: pl.BlockSpec((s, LANES), lambda j, blk=blk: (0, blk + j))
    in_specs = [pl.BlockSpec((s, LANES), lambda j: (0, q_blk + j)), kv(k_blk), kv(v_blk), own, own, own, own,
                pl.BlockSpec((1, LANES), lambda j: (0, j))]
    operands = (z, z, z, do, o, lse[0], lse[1], m_init)
    return _call(
        body,
        grid=(npair,),
        in_specs=in_specs + [hbm] * n_dz_in,
        out_specs=[hbm, pl.BlockSpec((1, LANES), lambda j: (0, j))],
        out_shape=[jax.ShapeDtypeStruct((s, IN_WIDTH), BF16), jax.ShapeDtypeStruct((1, npair * LANES), F32)],
        operands=operands + (() if dz is None else (dz,)), name=name,
        scratch_shapes=[pltpu.VMEM((s, LANES), F32)] * 5 + [pltpu.VMEM((2 * BLOCK, 2 * BLOCK), F32)]
        + [pltpu.VMEM((s, LANES), BF16)] * 3 + [pltpu.SemaphoreType.DMA((3,))],
        comm=comm, aliases={} if dz is None else {len(in_specs): 0})


def _adamw_math(w, g, m, v):
    m = ADAM_B1 * m + (1.0 - ADAM_B1) * g
    v = ADAM_B2 * v + (1.0 - ADAM_B2) * (g * g)
    m_hat = m / (1.0 - ADAM_B1 ** ADAM_STEP)
    v_hat = v / (1.0 - ADAM_B2 ** ADAM_STEP)
    delta = -ADAM_LR * (m_hat / (jnp.sqrt(v_hat) + ADAM_EPS) + ADAM_WD * w)
    return delta, m, v


def _adamw(w, g, m, v, name):
    rows, cols = w.shape
    tr = min(rows, 256)

    def body(w_ref, g_ref, m_ref, v_ref, d_ref, nm_ref, nv_ref):
        d_ref[...], nm_ref[...], nv_ref[...] = _adamw_math(w_ref[...], g_ref[...], m_ref[...], v_ref[...])

    spec = pl.BlockSpec((tr, cols), lambda i: (i, 0))
    return pl.pallas_call(
        body,
        grid=(rows // tr,),
        in_specs=[spec] * 4,
        out_specs=[spec] * 3,
        out_shape=[jax.ShapeDtypeStruct((rows, cols), F32)] * 3,
        compiler_params=_params("parallel"),
        name=name,
    )(w, g, m, v)


def _sum_adamw(parts, w, m, v, pos, transpose, name):
    assert len(parts) == DEPTH == 2
    (p0, r0), (p1, r1) = parts
    _, rows, cols = p0.shape
    tr = 256 if rows % 256 == 0 else rows
    nt = rows // tr

    def body(pos_ref, p0_ref, r0_ref, p1_ref, r1_ref, w_ref, m_ref, v_ref, g_ref, d_ref, nm_ref, nv_ref):
        def run(p_ref, r_ref):
            g = ((p_ref[...].astype(F32) + r_ref[0].astype(F32)) + r_ref[1].astype(F32)) + r_ref[2].astype(F32)
            if transpose:
                g = g.T
            g_ref[...] = g
            d_ref[...], nm_ref[...], nv_ref[...] = _adamw_math(w_ref[...], g, m_ref[...], v_ref[...])

        layer0 = pl.program_id(0) < nt
        pl.when(layer0)(lambda: run(p0_ref, r0_ref))
        pl.when(jnp.logical_not(layer0))(lambda: run(p1_ref, r1_ref))

    def tile0(i):
        return jnp.minimum(i, nt - 1)

    def tile1(i):
        return jnp.maximum(i - nt, 0)

    if transpose:
        w_spec = pl.BlockSpec((None, cols, tr), lambda i, q: (i // nt, 0, i % nt))
    else:
        w_spec = pl.BlockSpec((None, tr, cols), lambda i, q: (i // nt, i % nt, 0))
    return pl.pallas_call(
        body,
        grid_spec=pltpu.PrefetchScalarGridSpec(
            num_scalar_prefetch=1,
            grid=(DEPTH * nt,),
            in_specs=[pl.BlockSpec((None, tr, cols), lambda i, q: (q[0], tile0(i), 0)),
                      pl.BlockSpec((3, tr, cols), lambda i, q: (0, tile0(i), 0)),
                      pl.BlockSpec((None, tr, cols), lambda i, q: (q[0], tile1(i), 0)),
                      pl.BlockSpec((3, tr, cols), lambda i, q: (0, tile1(i), 0)),
                      w_spec, w_spec, w_spec],
            out_specs=[w_spec] * 4,
        ),
        out_shape=[jax.ShapeDtypeStruct(w.shape, F32)] * 4,
        compiler_params=_params("arbitrary"),
        name=name,
    )(pos, p0, r0, p1, r1, w, m, v)


def _small_sum_adamw(gathered, params, name):
    _, rows, cols = gathered.shape
    n = len(params)

    def body(ga_ref, *refs):
        ins, outs, (g_scr,) = refs[:3 * n], refs[3 * n:7 * n + 2], refs[7 * n + 2:]
        g = ga_ref[0]
        for i in range(1, N_DEV):
            g = g + ga_ref[i]
        g_scr[...] = g
        for k, (row0, w, _, _) in enumerate(params):
            w_ref, m_ref, v_ref = ins[3 * k:3 * k + 3]
            gk = g_scr[row0:row0 + w.shape[0], :]
            outs[4 * k][...] = gk
            outs[4 * k + 1][...], outs[4 * k + 2][...], outs[4 * k + 3][...] = _adamw_math(
                w_ref[...], gk, m_ref[...], v_ref[...])
        outs[4 * n][...] = g_scr[CONV_ROW:CONV_ROW + 8, :]
        outs[4 * n + 1][...] = g_scr[LOSS_ROW:LOSS_ROW + 1, :]

    out_shape = []
    for _, w, _, _ in params:
        out_shape += [jax.ShapeDtypeStruct(w.shape, F32)] * 4
    out_shape += [jax.ShapeDtypeStruct((8, cols), F32), jax.ShapeDtypeStruct((1, cols), F32)]
    res = pl.pallas_call(
        body,
        out_shape=out_shape,
        scratch_shapes=[pltpu.VMEM((rows, cols), F32)],
        name=name,
    )(gathered, *[t for _, w, m, v in params for t in (w, m, v)])
    return [res[4 * k:4 * k + 4] for k in range(n)], res[4 * n], res[4 * n + 1]


def _pair_sum(g4, r1, pos, name):
    _, _, rows, cols = g4.shape
    tr = min(rows, 512)

    def body(pos_ref, g_ref, r_ref, o_ref):
        o_ref[...] = (g_ref[...].astype(F32) + r_ref[...].astype(F32)).astype(BF16)

    return pl.pallas_call(
        body,
        grid_spec=pltpu.PrefetchScalarGridSpec(
            num_scalar_prefetch=1,
            grid=(4, rows // tr),
            in_specs=[pl.BlockSpec((None, None, tr, cols), lambda i, j, p: (i, p[1], j, 0)),
                      pl.BlockSpec((None, tr, cols), lambda i, j, p: (i, j, 0))],
            out_specs=pl.BlockSpec((None, tr, cols), lambda i, j, p: (i, j, 0)),
        ),
        out_shape=jax.ShapeDtypeStruct((4, rows, cols), BF16),
        compiler_params=_params("parallel", "parallel"),
        name=name,
    )(pos, g4, r1)


def _place():
    return lax.axis_index("x"), lax.axis_index("y"), lax.axis_index("c")


def _gather_comm(shards):
    na = len(shards)

    def plan(ins, outs, sems):
        send_sems, recv_sems, local_sems = sems
        x, y, c = _place()
        me, sibling = (x, y, c), (x, y, 1 - c)
        chips = [(1 - x, y), (x, 1 - y), (1 - x, 1 - y)]

        def rows(a, px, py, pc):
            m = ins[a].shape[0]
            return outs[a].at[pl.ds((4 * px + 2 * py + pc) * m, m), :]

        def copy(a, k, block, to, src=None):
            return pltpu.make_async_remote_copy(
                src_ref=rows(a, *block) if src is None else src, dst_ref=rows(a, *block),
                send_sem=send_sems.at[a, k], recv_sem=recv_sems.at[a, k], device_id=to, device_id_type=MESH)

        mine = [pltpu.make_async_copy(ins[a], rows(a, *me), local_sems.at[a]) for a in range(na)]
        first = []
        for a in range(na):
            first.append(copy(a, 0, me, sibling, src=ins[a]))
            first += [copy(a, 1 + j, me, (*chip, c), src=ins[a]) for j, chip in enumerate(chips)]
        return me, sibling, chips, c, copy, mine, first

    def start(ins, outs, sems):
        *_, mine, first = plan(ins, outs, sems)
        for cp in mine + first:
            cp.start()

    def finish(ins, outs, sems):
        me, sibling, chips, c, copy, mine, first = plan(ins, outs, sems)
        passed = []
        for j, chip in enumerate(chips):
            for a in range(na):
                copy(a, 1 + j, (*chip, c), me).wait_recv()
                cp = copy(a, 4 + j, (*chip, c), sibling)
                cp.start()
                passed.append(cp)
        for a in range(na):
            copy(a, 0, sibling, me).wait_recv()
            for j, chip in enumerate(chips):
                copy(a, 4 + j, (*chip, 1 - c), me).wait_recv()
        for cp in first + passed:
            cp.wait_send()
        for cp in mine:
            cp.wait()

    return _Comm(tuple(shards),
                 tuple(jax.ShapeDtypeStruct((N_DEV * t.shape[0], t.shape[1]), t.dtype) for t in shards),
                 (pltpu.SemaphoreType.DMA((na, 7)), pltpu.SemaphoreType.DMA((na, 7)), pltpu.SemaphoreType.DMA((na,))),
                 start, finish)


def _exchange_comm(arrays, out_shape, n_copies, copies_of):
    na = len(arrays)

    def every(ins, outs, sems):
        send_sems, recv_sems = sems
        return [cp for a in range(na) for cp in copies_of(ins, outs, a, send_sems, recv_sems)]

    def start(ins, outs, sems):
        for cp in every(ins, outs, sems):
            cp.start()

    def finish(ins, outs, sems):
        for cp in every(ins, outs, sems):
            cp.wait()

    return _Comm(tuple(arrays), tuple(out_shape),
                 (pltpu.SemaphoreType.DMA((na, n_copies)), pltpu.SemaphoreType.DMA((na, n_copies))), start, finish)


def _sibling_comm(grads):
    def copies_of(ins, outs, a, send_sems, recv_sems):
        x, y, c = _place()
        return [pltpu.make_async_remote_copy(
            src_ref=ins[a].at[chip, 1 - c], dst_ref=outs[a].at[chip],
            send_sem=send_sems.at[a, chip], recv_sem=recv_sems.at[a, chip],
            device_id=(x, y, 1 - c), device_id_type=MESH) for chip in range(4)]

    return _exchange_comm(grads, [jax.ShapeDtypeStruct((4,) + t.shape[2:], t.dtype) for t in grads], 4, copies_of)


def _chip_comm(partials):
    def copies_of(ins, outs, a, send_sems, recv_sems):
        x, y, c = _place()
        chips = [(1 - x, y), (x, 1 - y), (1 - x, 1 - y)]
        return [pltpu.make_async_remote_copy(
            src_ref=ins[a].at[2 * cx + cy], dst_ref=outs[a].at[k],
            send_sem=send_sems.at[a, k], recv_sem=recv_sems.at[a, k],
            device_id=(cx, cy, c), device_id_type=MESH) for k, (cx, cy) in enumerate(chips)]

    return _exchange_comm(partials, [jax.ShapeDtypeStruct((3,) + t.shape[1:], t.dtype) for t in partials], 3, copies_of)


def _pad_rows(t, rows):
    return jnp.pad(t, ((0, rows - t.shape[0]), (0, D_MODEL - t.shape[1])))


MIX_ROW, GROUP_ROW, MLP_ROW, FINAL_ROW, CONV_ROW, SINK_ROW = 0, 8, 16, 24, 32, 40
LOSS_ROW = FINAL_ROW + 1


def _pack_small(g_mix, g_group, g_mlp, g_final, conv, sinks, loss):
    final_and_loss = jnp.concatenate([g_final.reshape(1, D_MODEL), _pad_rows(loss, 1)], axis=0)
    return jnp.concatenate([
        _pad_rows(g_mix, 8), _pad_rows(g_group, 8), _pad_rows(g_mlp, 8), _pad_rows(final_and_loss, 8),
        _pad_rows(conv.reshape(DEPTH * 3, CONV_CH), 8), _pad_rows(sinks.reshape(1, DEPTH * 6), 8)], axis=0)


def kernel(x, w_in, conv_w, sinks, g_mix, g_group, w_o, g_mlp, w_ff_in, w_ff_out, g_final, loss_target, m_w_in, m_conv_w, m_sinks, m_g_mix, m_g_group, m_w_o, m_g_mlp, m_w_ff_in, m_w_ff_out, m_g_final, v_w_in, v_conv_w, v_sinks, v_g_mix, v_g_group, v_w_o, v_g_mlp, v_w_ff_in, v_w_ff_out, v_g_final):
    ax, ay, ac = _place()
    chip = 2 * ax + ay
    dev = 4 * ax + 2 * ay + ac
    pos = jnp.stack([chip, ac]).astype(jnp.int32)

    x0 = x.reshape(SEQ, D_MODEL)
    target = loss_target.reshape(SEQ, D_MODEL)

    shards = {}
    for l in range(DEPTH):
        shards[l, 0], shards[l, 1] = w_in[l].T.astype(BF16), w_o[l].astype(BF16)
        shards[l, 2], shards[l, 3] = w_ff_in[l].T.astype(BF16), w_ff_out[l].astype(BF16)
    conv_tile = jnp.pad(conv_w.reshape(DEPTH * 3, CONV_CH // N_DEV), ((0, 2), (0, LANES - CONV_CH // N_DEV)))
    wt_in0, conv_all = _comm_only(_gather_comm([shards[0, 0], conv_tile]), "gather_first")
    conv_full = conv_all.reshape(N_DEV, 8, LANES)[:, :DEPTH * 3, :CONV_CH // N_DEV]
    conv_full = conv_full.transpose(1, 0, 2).reshape(DEPTH, 3, CONV_CH)

    dx, parts, small = _step(x0, target, shards, wt_in0, conv_full, sinks, g_mix, g_group, g_mlp, g_final, pos)
    return _finish(dx, parts, small, pos, dev, w_in, conv_w, sinks, g_mix, g_group, w_o, g_mlp, w_ff_in, w_ff_out, g_final, m_w_in, m_conv_w, m_sinks, m_g_mix, m_g_group, m_w_o, m_g_mlp, m_w_ff_in, m_w_ff_out, m_g_final, v_w_in, v_conv_w, v_sinks, v_g_mix, v_g_group, v_w_o, v_g_mlp, v_w_ff_in, v_w_ff_out, v_g_final)


FWD_CARRY = {(0, "in_proj"): ((1, 0),), (0, "window"): ((0, 1),), (0, "dilated"): ((0, 2),),
             (0, "mix_ff_in"): ((0, 3),), (0, "ff_out_in_proj"): ((1, 3),),
             (1, "window"): ((1, 1),), (1, "dilated"): ((1, 2),)}


def _step(x0, target, shards, wt_in0, conv_full, sinks, g_mix, g_group, g_mlp, g_final, pos):
    sink_lanes = jnp.repeat(sinks.reshape(DEPTH, 6), HEAD_DIM, axis=1)
    no_sink = jnp.full((1, A_WIDTH), NEG_BIG, F32)
    full = {(0, 0): wt_in0}

    def gather(stage, l):
        keys = FWD_CARRY.get((l, stage), ())
        return keys, (_gather_comm([shards[k] for k in keys]) if keys else None)

    def landed(keys, got):
        full.update(zip(keys, got))

    saved = []
    xc = x0
    keys, comm = gather("in_proj", 0)
    (z, h), got = _norm_mm(xc, g_mix[0:1], full[0, 0], "in_proj_0", comm)
    landed(keys, got)
    for l in range(DEPTH):
        sink_l = sink_lanes[l:l + 1]
        keys, comm = gather("window", l)
        (yc, *lse_c), got = _attn_fwd(z, sink_l, 1.0, QC_BLK, KC_BLK, VC_BLK, (1,), C_MAX_DIST, True,
                                     f"window_attn_{l}", comm)
        landed(keys, got)
        yb = _conv_fwd(z, conv_full[l], f"conv_{l}")
        keys, comm = gather("dilated", l)
        (ya, *lse_a), got = _attn_fwd(z, no_sink, 0.0, QA_BLK, KA_BLK, VA_BLK, DILATED_PATTERNS, A_MAX_DIST, False,
                                     f"dilated_attn_{l}", comm)
        landed(keys, got)
        keys, comm = gather("mix_ff_in", l)
        (y, x1, a, h2), got = _mix_ff_in(ya, yb, yc, g_group[l:l + 1], full[l, 1], xc, g_mlp[l:l + 1], full[l, 2],
                                         f"mix_ff_in_{l}", comm)
        landed(keys, got)
        saved.append((xc, z, h, ya, lse_a, yb, yc, lse_c, sink_l, y, x1, a, h2))
        if l + 1 < DEPTH:
            keys, comm = gather("ff_out_in_proj", l)
            (xc, z, h), got = _ff_out_in_proj(a, full[l, 3], x1, g_mix[l + 1:l + 2], full[l + 1, 0],
                                              f"ff_out_{l}_in_proj_{l + 1}", comm)
            landed(keys, got)

    loss_slab, dx, dxb, dg_final = _mm_res_loss(a, full[DEPTH - 1, 3], x1, g_final.reshape(1, D_MODEL), target,
                                                f"ff_out_{DEPTH - 1}_loss")

    def by_owner(t):
        return t.reshape(4, 2, t.shape[0] // N_DEV, D_MODEL)

    def pair(key, g, r1):
        return _pair_sum(g, r1, pos, f"grad_pair_sum_{key[0]}_{key[1]}")

    partial, r2 = {}, {}
    dg_mix, dg_group, dg_mlp, dconv, dsinks = [None] * DEPTH, [None] * DEPTH, [None] * DEPTH, [None] * DEPTH, [None] * DEPTH
    for l in reversed(range(DEPTH)):
        xin, z, h, ya, lse_a, yb, yc, lse_c, sink_l, y, x1, a, h2 = saved[l]
        late = [(l + 1, 1), (l + 1, 0)] if l + 1 < DEPTH else []
        (du,), got = _mlp_bwd_act(dxb, full[l, 3], a, f"ff_out_bwd_{l}",
                                  _chip_comm([partial[k] for k in late]) if late else None)
        r2.update(zip(late, got))
        (g3,), _ = _mm_tn(a, dxb, f"grad_w_ff_out_{l}")
        (g2,), _ = _mm_tn(du, h2, f"grad_w_ff_in_{l}")
        g3, g2 = by_owner(g3), by_owner(g2)
        (dx1, dx1b, dg_mlp[l], dya, dyb, dyc, dg_group[l]), got = _ff_in_mix_bwd(
            du, full[l, 2], x1, dx, g_mlp[l:l + 1], full[l, 1], ya, yb, yc, g_group[l:l + 1],
            f"ff_in_mix_bwd_{l}", _sibling_comm([g3, g2]))
        partial[l, 3], partial[l, 2] = pair((l, 3), g3, got[0]), pair((l, 2), g2, got[1])
        (g1,), _ = _mm_tn(y, dx1b, f"grad_w_o_{l}")
        g1 = by_owner(g1)
        early = [(l, 3), (l, 2)]
        (dz, _), got = _attn_bwd(z, dya, ya, lse_a, no_sink, None, QA_BLK, KA_BLK, VA_BLK, DILATED_PATTERNS,
                                 A_MAX_DIST, False, f"dilated_attn_bwd_{l}", _chip_comm([partial[k] for k in early]))
        r2.update(zip(early, got))
        dz, dcw = _conv_bwd(z, conv_full[l], dyb, dz, f"conv_bwd_{l}")
        (dz, dsink), _ = _attn_bwd(z, dyc, yc, lse_c, sink_l, dz, QC_BLK, KC_BLK, VC_BLK, (1,), C_MAX_DIST,
                                   True, f"window_attn_bwd_{l}")
        (g0,), _ = _mm_tn(dz, h, f"grad_w_in_{l}")
        g0 = by_owner(g0)
        if l > 0:
            (dx, dxb, dg_mix[l]), got = _mm_nn_normbwd(dz, full[l, 0], xin, dx1, g_mix[l:l + 1], f"in_proj_bwd_{l}",
                                                      _sibling_comm([g1, g0]))
            partial[l, 1], partial[l, 0] = pair((l, 1), g1, got[0]), pair((l, 0), g0, got[1])
        else:
            got = _comm_only(_sibling_comm([g1, g0]), "grad_sibling_exchange_last")
            partial[l, 1], partial[l, 0] = pair((l, 1), g1, got[0]), pair((l, 0), g0, got[1])
            (dx, dxb, dg_mix[l]), got = _mm_nn_normbwd(dz, full[l, 0], xin, dx1, g_mix[l:l + 1], f"in_proj_bwd_{l}",
                                                      _chip_comm([partial[l, 1], partial[l, 0]]))
            r2[l, 1], r2[l, 0] = got
        dconv[l] = dcw[:3]
        dsinks[l] = dsink[0, ::HEAD_DIM]
    parts = {key: (partial[key], r2[key]) for key in partial}
    small = _pack_small(jnp.concatenate(dg_mix), jnp.concatenate(dg_group), jnp.concatenate(dg_mlp),
                        dg_final, jnp.stack(dconv), jnp.stack(dsinks), loss_slab[0:1])
    return dx, parts, small


def _finish(dx, parts, small, pos, dev, w_in, conv_w, sinks, g_mix, g_group, w_o, g_mlp, w_ff_in, w_ff_out, g_final, m_w_in, m_conv_w, m_sinks, m_g_mix, m_g_group, m_w_o, m_g_mlp, m_w_ff_in, m_w_ff_out, m_g_final, v_w_in, v_conv_w, v_sinks, v_g_mix, v_g_group, v_w_o, v_g_mlp, v_w_ff_in, v_w_ff_out, v_g_final):
    grad_x = dx.reshape(1, SEQ, D_MODEL)

    (small_all,) = _comm_only(_gather_comm([small]), "gather_small_grads")
    row = lambda t: t.reshape(1, D_MODEL)
    sink_row = lambda t: _pad_rows(t.reshape(1, DEPTH * 6), 1)
    params = [(MIX_ROW, g_mix, m_g_mix, v_g_mix), (GROUP_ROW, g_group, m_g_group, v_g_group),
              (MLP_ROW, g_mlp, m_g_mlp, v_g_mlp), (FINAL_ROW, row(g_final), row(m_g_final), row(v_g_final)),
              (SINK_ROW, sink_row(sinks), sink_row(m_sinks), sink_row(v_sinks))]
    updated, conv_rows, loss_row = _small_sum_adamw(small_all.reshape(N_DEV, SMALL_ROWS, D_MODEL), params, "small_adamw")
    loss = loss_row[0, 0]
    (grad_g_mix, delta_g_mix, new_m_g_mix, new_v_g_mix), (grad_g_group, delta_g_group, new_m_g_group, new_v_g_group), \
        (grad_g_mlp, delta_g_mlp, new_m_g_mlp, new_v_g_mlp), final4, sinks4 = updated
    grad_g_final, delta_g_final, new_m_g_final, new_v_g_final = [t.reshape(D_MODEL) for t in final4]
    grad_sinks, delta_sinks, new_m_sinks, new_v_sinks = [t[0, :DEPTH * 6].reshape(DEPTH, 2, 3) for t in sinks4]
    conv_grad_full = conv_rows[:DEPTH * 3, :CONV_CH].reshape(DEPTH, 3, CONV_CH)
    cs = CONV_CH // N_DEV
    grad_conv_w = lax.dynamic_slice_in_dim(conv_grad_full, dev * cs, cs, axis=2)

    def tile_of(t):
        return jnp.pad(t.reshape(1, DEPTH * 3 * cs), ((0, 7), (0, 256 - DEPTH * 3 * cs)))

    cd, cm, cv = _adamw(tile_of(conv_w), tile_of(grad_conv_w), tile_of(m_conv_w), tile_of(v_conv_w), "conv_adamw")
    untile = lambda t: t[0, :DEPTH * 3 * cs].reshape(DEPTH, 3, cs)
    delta_conv_w, new_m_conv_w, new_v_conv_w = untile(cd), untile(cm), untile(cv)

    def big(kind, w, m, v, transpose, name):
        return _sum_adamw([parts[l, kind] for l in range(DEPTH)], w, m, v, pos, transpose, name)

    grad_w_in, delta_w_in, new_m_w_in, new_v_w_in = big(0, w_in, m_w_in, v_w_in, True, "adamw_w_in")
    grad_w_o, delta_w_o, new_m_w_o, new_v_w_o = big(1, w_o, m_w_o, v_w_o, False, "adamw_w_o")
    grad_w_ff_in, delta_w_ff_in, new_m_w_ff_in, new_v_w_ff_in = big(2, w_ff_in, m_w_ff_in, v_w_ff_in, True, "adamw_w_ff_in")
    grad_w_ff_out, delta_w_ff_out, new_m_w_ff_out, new_v_w_ff_out = big(3, w_ff_out, m_w_ff_out, v_w_ff_out, False,
                                                                         "adamw_w_ff_out")

    return (loss, grad_x, grad_w_in, grad_conv_w, grad_sinks, grad_g_mix, grad_g_group, grad_w_o, grad_g_mlp,
            grad_w_ff_in, grad_w_ff_out, grad_g_final,
            delta_w_in, delta_conv_w, delta_sinks, delta_g_mix, delta_g_group, delta_w_o, delta_g_mlp,
            delta_w_ff_in, delta_w_ff_out, delta_g_final,
            new_m_w_in, new_m_conv_w, new_m_sinks, new_m_g_mix, new_m_g_group, new_m_w_o, new_m_g_mlp,
            new_m_w_ff_in, new_m_w_ff_out, new_m_g_final,
            new_v_w_in, new_v_conv_w, new_v_sinks, new_v_g_mix, new_v_g_group, new_v_w_o, new_v_g_mlp,
            new_v_w_ff_in, new_v_w_ff_out, new_v_g_final)
```

```python
from typing import Callable, NamedTuple

import jax
import jax.numpy as jnp
from jax import lax
from jax.experimental import pallas as pl
from jax.experimental.pallas import tpu as pltpu

F32 = jnp.float32
BF16 = jnp.bfloat16
MESH = pl.DeviceIdType.MESH

N_DEV = 8
SEQ = 4096
D_MODEL = 1024
DEPTH = 2
HEAD_DIM = 64
LANES = 128
A_WIDTH = 384
CONV_CH = 256
C_WIDTH = 384
KV_WIDTH = 128
IN_WIDTH = 2560
D_FF = 4096
BLOCK = 128
DILATED_PATTERNS = (1, 4, 16)
A_MAX_DIST = 128
C_MAX_DIST = 127
EPS = 1e-6
SCALE = HEAD_DIM ** -0.5
NEG_BIG = -1e30
F32_TINY = 1.1754944e-38

QA_BLK, KA_BLK, VA_BLK = 0, 3, 6
GB_BLK, GC_BLK, XB_BLK = 9, 11, 13
QC_BLK, KC_BLK, VC_BLK = 15, 18, 19

ADAM_LR = 0.001
ADAM_B1 = 0.9
ADAM_B2 = 0.999
ADAM_EPS = 1e-08
ADAM_WD = 0.01
ADAM_STEP = 10

VMEM_LIMIT = 56 * 1024 * 1024
TILE_BUDGET = 46 * 1024 * 1024
ROW_TILE = 512
COL_CHUNK = 512
SMALL_ROWS = 48


def _dot_nn(a, b):
    return lax.dot_general(a, b, (((1,), (0,)), ((), ())), preferred_element_type=F32)


def _dot_nt(a, b):
    return lax.dot_general(a, b, (((1,), (1,)), ((), ())), preferred_element_type=F32)


def _dot_tn(a, b):
    return lax.dot_general(a, b, (((0,), (0,)), ((), ())), preferred_element_type=F32)


def _params(*sem):
    return pltpu.CompilerParams(dimension_semantics=sem, vmem_limit_bytes=VMEM_LIMIT)


def _resident(shape):
    return pl.BlockSpec(shape, lambda i: (0,) * len(shape), pipeline_mode=pl.Buffered(1))


def _row_tile(row_bytes, resident_bytes):
    for tm in (ROW_TILE, ROW_TILE // 2):
        if 2 * tm * row_bytes + resident_bytes <= TILE_BUDGET:
            return tm
    return ROW_TILE // 4


def _rms_scale(t):
    return lax.rsqrt(jnp.mean(t * t, axis=-1, keepdims=True) + EPS)


def _rms_bwd(n, r, dn):
    return r * (dn - n * jnp.mean(dn * n, axis=-1, keepdims=True))


class _Comm(NamedTuple):
    arrays: tuple
    out_shape: tuple
    sems: tuple
    start: Callable
    finish: Callable


def _call(body, grid, in_specs, out_specs, out_shape, operands, name, scratch_shapes=(), comm=None, aliases=None):
    n_in, n_out, n_scr = len(in_specs), len(out_shape), len(scratch_shapes)
    aliases = dict(aliases or {})
    if comm is None:
        res = pl.pallas_call(body, grid=grid, in_specs=list(in_specs), out_specs=list(out_specs),
                             out_shape=list(out_shape), scratch_shapes=list(scratch_shapes),
                             input_output_aliases=aliases,
                             compiler_params=_params("arbitrary"), name=name)(*operands)
        return list(res), []
    c_in, c_out = len(comm.arrays), len(comm.out_shape)
    hbm = pl.BlockSpec(memory_space=pl.ANY)
    last = grid[0] - 1

    def carried(*refs):
        ins, cins = refs[:n_in], refs[n_in:n_in + c_in]
        o0 = n_in + c_in
        outs, couts = refs[o0:o0 + n_out], refs[o0 + n_out:o0 + n_out + c_out]
        s0 = o0 + n_out + c_out
        scr, sems = refs[s0:s0 + n_scr], refs[s0 + n_scr:]
        pl.when(pl.program_id(0) == 0)(lambda: comm.start(cins, couts, sems))
        body(*ins, *outs, *scr)
        pl.when(pl.program_id(0) == last)(lambda: comm.finish(cins, couts, sems))

    res = pl.pallas_call(carried, grid=grid, in_specs=list(in_specs) + [hbm] * c_in,
                         out_specs=list(out_specs) + [hbm] * c_out, out_shape=list(out_shape) + list(comm.out_shape),
                         scratch_shapes=list(scratch_shapes) + list(comm.sems), input_output_aliases=aliases,
                         compiler_params=_params("arbitrary"), name=name)(*operands, *comm.arrays)
    return list(res[:n_out]), list(res[n_out:])


def _comm_only(comm, name):
    hbm = pl.BlockSpec(memory_space=pl.ANY)
    c_in, c_out = len(comm.arrays), len(comm.out_shape)

    def body(*refs):
        ins, outs, sems = refs[:c_in], refs[c_in:c_in + c_out], refs[c_in + c_out:]
        comm.start(ins, outs, sems)
        comm.finish(ins, outs, sems)

    return pl.pallas_call(body, in_specs=[hbm] * c_in, out_specs=[hbm] * c_out, out_shape=list(comm.out_shape),
                          scratch_shapes=list(comm.sems), name=name)(*comm.arrays)


def _norm_mm(x, g, wt, name, comm=None):
    s, d = x.shape
    n = wt.shape[0]
    tm = _row_tile(4 * d + 4 * n + 2 * d, 2 * n * d)

    def body(x_ref, g_ref, w_ref, o_ref, h_ref):
        xx = x_ref[...]
        h = ((xx * _rms_scale(xx)) * g_ref[...]).astype(BF16)
        h_ref[...] = h
        for n0 in range(0, n, COL_CHUNK):
            o_ref[:, n0:n0 + COL_CHUNK] = _dot_nt(h, w_ref[n0:n0 + COL_CHUNK, :])

    return _call(
        body,
        grid=(s // tm,),
        in_specs=[pl.BlockSpec((tm, d), lambda i: (i, 0)),
                  pl.BlockSpec((1, d), lambda i: (0, 0)),
                  _resident((n, d))],
        out_specs=[pl.BlockSpec((tm, n), lambda i: (i, 0)),
                   pl.BlockSpec((tm, d), lambda i: (i, 0))],
        out_shape=[jax.ShapeDtypeStruct((s, n), F32), jax.ShapeDtypeStruct((s, d), BF16)],
        operands=(x, g, wt), name=name, comm=comm)


def _ff_out_in_proj(a, w2, x1, g, wt, name, comm=None):
    s, f = a.shape
    d = w2.shape[1]
    n = wt.shape[0]
    tm = _row_tile(2 * f + 4 * d + 4 * d + 4 * n + 2 * d, 2 * f * d + 2 * n * d)

    def body(a_ref, w2_ref, x_ref, g_ref, w_ref, x2_ref, z_ref, h_ref):
        x2 = x_ref[...] + _dot_nn(a_ref[...], w2_ref[...])
        x2_ref[...] = x2
        h = ((x2 * _rms_scale(x2)) * g_ref[...]).astype(BF16)
        h_ref[...] = h
        for n0 in range(0, n, COL_CHUNK):
            z_ref[:, n0:n0 + COL_CHUNK] = _dot_nt(h, w_ref[n0:n0 + COL_CHUNK, :])

    rows = lambda w: pl.BlockSpec((tm, w), lambda i: (i, 0))
    return _call(
        body,
        grid=(s // tm,),
        in_specs=[rows(f), _resident((f, d)), rows(d), pl.BlockSpec((1, d), lambda i: (0, 0)), _resident((n, d))],
        out_specs=[rows(d), rows(n), rows(d)],
        out_shape=[jax.ShapeDtypeStruct((s, d), F32), jax.ShapeDtypeStruct((s, n), F32),
                   jax.ShapeDtypeStruct((s, d), BF16)],
        operands=(a, w2, x1, g, wt), name=name, comm=comm)


def _mix_ff_in(ya, yb, yc, gg, wo, x0, g_mlp, wt1, name, comm=None):
    s = ya.shape[0]
    d = wo.shape[1]
    f = wt1.shape[0]
    tm = _row_tile(4 * d + 4 * d + 2 * d + 4 * d + 2 * d + 2 * f, 2 * d * d + 2 * f * d)

    def body(ya_ref, yb_ref, yc_ref, gg_ref, wo_ref, x_ref, g_ref, w1_ref, y_ref, x1_ref, a_ref, h_ref):
        parts = []
        for ref in (ya_ref, yb_ref, yc_ref):
            t = ref[...]
            parts.append(t * _rms_scale(t))
        y = (jnp.concatenate(parts, axis=1) * gg_ref[...]).astype(BF16)
        y_ref[...] = y
        x1 = x_ref[...] + _dot_nn(y, wo_ref[...])
        x1_ref[...] = x1
        h = ((x1 * _rms_scale(x1)) * g_ref[...]).astype(BF16)
        h_ref[...] = h
        for n0 in range(0, f, COL_CHUNK):
            u = _dot_nt(h, w1_ref[n0:n0 + COL_CHUNK, :])
            a_ref[:, n0:n0 + COL_CHUNK] = jnp.square(jnp.maximum(u, 0.0)).astype(BF16)

    rows = lambda w: pl.BlockSpec((tm, w), lambda i: (i, 0))
    vec = pl.BlockSpec((1, d), lambda i: (0, 0))
    return _call(
        body,
        grid=(s // tm,),
        in_specs=[rows(A_WIDTH), rows(CONV_CH), rows(C_WIDTH), vec, _resident((d, d)), rows(d), vec, _resident((f, d))],
        out_specs=[rows(d), rows(d), rows(f), rows(d)],
        out_shape=[jax.ShapeDtypeStruct((s, d), BF16), jax.ShapeDtypeStruct((s, d), F32),
                   jax.ShapeDtypeStruct((s, f), BF16), jax.ShapeDtypeStruct((s, d), BF16)],
        operands=(ya, yb, yc, gg, wo, x0, g_mlp, wt1), name=name, comm=comm)


def _mm_res_loss(a, w2, x1, g, target, name):
    s, f = a.shape
    d = w2.shape[1]
    tm = _row_tile(2 * f + 4 * d + 4 * d + 4 * d + 2 * d, 2 * f * d)

    def body(a_ref, w_ref, x_ref, g_ref, t_ref, loss_ref, dx_ref, dxb_ref, dg_ref):
        @pl.when(pl.program_id(0) == 0)
        def _():
            loss_ref[...] = jnp.zeros_like(loss_ref)
            dg_ref[...] = jnp.zeros_like(dg_ref)

        xx = x_ref[...] + _dot_nn(a_ref[...], w_ref[...])
        r = _rms_scale(xx)
        n = xx * r
        gv = g_ref[...]
        err = n * gv - t_ref[...]
        per_tok = jnp.sum(err * err, axis=1, keepdims=True) * (1.0 / d)
        loss_ref[...] += 0.5 * jnp.sum(per_tok, axis=0, keepdims=True)
        dout = err * (1.0 / d)
        dg_ref[...] += jnp.sum(dout * n, axis=0, keepdims=True)
        dx = _rms_bwd(n, r, dout * gv)
        dx_ref[...] = dx
        dxb_ref[...] = dx.astype(BF16)

    return pl.pallas_call(
        body,
        grid=(s // tm,),
        in_specs=[pl.BlockSpec((tm, f), lambda i: (i, 0)),
                  _resident((f, d)),
                  pl.BlockSpec((tm, d), lambda i: (i, 0)),
                  pl.BlockSpec((1, d), lambda i: (0, 0)),
                  pl.BlockSpec((tm, d), lambda i: (i, 0))],
        out_specs=[pl.BlockSpec((8, LANES), lambda i: (0, 0)),
                   pl.BlockSpec((tm, d), lambda i: (i, 0)),
                   pl.BlockSpec((tm, d), lambda i: (i, 0)),
                   pl.BlockSpec((1, d), lambda i: (0, 0))],
        out_shape=[jax.ShapeDtypeStruct((8, LANES), F32), jax.ShapeDtypeStruct((s, d), F32),
                   jax.ShapeDtypeStruct((s, d), BF16), jax.ShapeDtypeStruct((1, d), F32)],
        compiler_params=_params("arbitrary"),
        name=name,
    )(a, w2, x1, g, target)


def _mlp_bwd_act(dxb, w2, a, name, comm=None):
    s, d = dxb.shape
    f = w2.shape[0]
    tm = _row_tile(2 * d + 2 * f + 2 * f, 2 * f * d)

    def body(dx_ref, w_ref, a_ref, du_ref):
        dx = dx_ref[...]
        for n0 in range(0, f, COL_CHUNK):
            da = _dot_nt(dx, w_ref[n0:n0 + COL_CHUNK, :])
            av = a_ref[:, n0:n0 + COL_CHUNK].astype(F32)
            rl = av * lax.rsqrt(jnp.maximum(av, F32_TINY))
            du_ref[:, n0:n0 + COL_CHUNK] = (da * (2.0 * rl)).astype(BF16)

    return _call(
        body,
        grid=(s // tm,),
        in_specs=[pl.BlockSpec((tm, d), lambda i: (i, 0)),
                  _resident((f, d)),
                  pl.BlockSpec((tm, f), lambda i: (i, 0))],
        out_specs=[pl.BlockSpec((tm, f), lambda i: (i, 0))],
        out_shape=[jax.ShapeDtypeStruct((s, f), BF16)],
        operands=(dxb, w2, a), name=name, comm=comm)


def _mm_tn(pairs, name, comm=None):
    s, d = pairs[0][1].shape
    tn = 512
    tiles = [a.shape[1] // tn for a, _ in pairs]
    starts = [sum(tiles[:k]) for k in range(len(pairs))]

    def body(*refs):
        ins, outs, acc = refs[:2 * len(pairs)], refs[2 * len(pairs):3 * len(pairs)], refs[3 * len(pairs)]
        j = pl.program_id(0)
        for k in range(len(pairs)):
            def run(a_ref=ins[2 * k], b_ref=ins[2 * k + 1], o_ref=outs[k]):
                for k0 in range(0, s, ROW_TILE):
                    part = _dot_tn(a_ref[k0:k0 + ROW_TILE, :], b_ref[k0:k0 + ROW_TILE, :])
                    if k0 == 0:
                        acc[...] = part
                    else:
                        acc[...] += part
                o_ref[...] = acc[...].astype(BF16)

            pl.when((j >= starts[k]) & (j < starts[k] + tiles[k]))(run)

    def tile_of(k):
        return lambda j: jnp.clip(j - starts[k], 0, tiles[k] - 1)

    in_specs, out_specs = [], []
    for k in range(len(pairs)):
        in_specs += [pl.BlockSpec((s, tn), lambda j, t=tile_of(k): (0, t(j))), _resident((s, d))]
        out_specs.append(pl.BlockSpec((tn, d), lambda j, t=tile_of(k): (t(j), 0)))
    return _call(
        body,
        grid=(sum(tiles),),
        in_specs=in_specs,
        out_specs=out_specs,
        out_shape=[jax.ShapeDtypeStruct((a.shape[1], d), BF16) for a, _ in pairs],
        operands=tuple(t for pair in pairs for t in pair), name=name,
        scratch_shapes=[pltpu.VMEM((tn, d), F32)], comm=comm)


def _mm_nn_normbwd(dact, wt, x, dres, g, name, comm=None):
    s, kdim = dact.shape
    d = wt.shape[1]
    tm = _row_tile(2 * kdim + 4 * d + 4 * d + 4 * d + 2 * d, 2 * kdim * d)

    def body(a_ref, w_ref, x_ref, r_ref, g_ref, o_ref, ob_ref, dg_ref):
        @pl.when(pl.program_id(0) == 0)
        def _():
            dg_ref[...] = jnp.zeros_like(dg_ref)

        dh = _dot_nn(a_ref[...], w_ref[...])
        xx = x_ref[...]
        r = _rms_scale(xx)
        n = xx * r
        dg_ref[...] += jnp.sum(dh * n, axis=0, keepdims=True)
        dx = r_ref[...] + _rms_bwd(n, r, dh * g_ref[...])
        o_ref[...] = dx
        ob_ref[...] = dx.astype(BF16)

    return _call(
        body,
        grid=(s // tm,),
        in_specs=[pl.BlockSpec((tm, kdim), lambda i: (i, 0)),
                  _resident((kdim, d)),
                  pl.BlockSpec((tm, d), lambda i: (i, 0)),
                  pl.BlockSpec((tm, d), lambda i: (i, 0)),
                  pl.BlockSpec((1, d), lambda i: (0, 0))],
        out_specs=[pl.BlockSpec((tm, d), lambda i: (i, 0)),
                   pl.BlockSpec((tm, d), lambda i: (i, 0)),
                   pl.BlockSpec((1, d), lambda i: (0, 0))],
        out_shape=[jax.ShapeDtypeStruct((s, d), F32), jax.ShapeDtypeStruct((s, d), BF16),
                   jax.ShapeDtypeStruct((1, d), F32)],
        operands=(dact, wt, x, dres, g), name=name, comm=comm)


def _ff_in_mix_bwd(du, wt1, x1, dres, g_mlp, wo, ya, yb, yc, gg, name, comm=None):
    s, f = du.shape
    d = wt1.shape[1]
    widths = (A_WIDTH, CONV_CH, C_WIDTH)
    tm = _row_tile(2 * f + 4 * d + 4 * d + 4 * d + 2 * d + 4 * d + 4 * d, 2 * f * d + 2 * d * d)

    def body(du_ref, w1_ref, x_ref, r_ref, g_ref, wo_ref, ya_ref, yb_ref, yc_ref, gg_ref,
             dx_ref, dxb_ref, dg_ref, da_ref, db_ref, dc_ref, dgg_ref):
        @pl.when(pl.program_id(0) == 0)
        def _():
            dg_ref[...] = jnp.zeros_like(dg_ref)
            dgg_ref[...] = jnp.zeros_like(dgg_ref)

        dh = _dot_nn(du_ref[...], w1_ref[...])
        xx = x_ref[...]
        r = _rms_scale(xx)
        n = xx * r
        dg_ref[...] += jnp.sum(dh * n, axis=0, keepdims=True)
        dx = r_ref[...] + _rms_bwd(n, r, dh * g_ref[...])
        dx_ref[...] = dx
        dxb = dx.astype(BF16)
        dxb_ref[...] = dxb

        dy = _dot_nt(dxb, wo_ref[...])
        gv = gg_ref[...]
        off = 0
        dgs = []
        for ref, out, w in zip((ya_ref, yb_ref, yc_ref), (da_ref, db_ref, dc_ref), widths):
            t = ref[...]
            r = _rms_scale(t)
            n = t * r
            dyg = dy[:, off:off + w]
            dgs.append(jnp.sum(dyg * n, axis=0, keepdims=True))
            out[...] = _rms_bwd(n, r, dyg * gv[:, off:off + w])
            off += w
        dgg_ref[...] += jnp.concatenate(dgs, axis=1)

    rows = lambda w: pl.BlockSpec((tm, w), lambda i: (i, 0))
    vec = pl.BlockSpec((1, d), lambda i: (0, 0))
    return _call(
        body,
        grid=(s // tm,),
        in_specs=[rows(f), _resident((f, d)), rows(d), rows(d), vec, _resident((d, d)),
                  rows(A_WIDTH), rows(CONV_CH), rows(C_WIDTH), vec],
        out_specs=[rows(d), rows(d), vec, rows(A_WIDTH), rows(CONV_CH), rows(C_WIDTH), vec],
        out_shape=[jax.ShapeDtypeStruct((s, d), F32), jax.ShapeDtypeStruct((s, d), BF16), jax.ShapeDtypeStruct((1, d), F32),
                   jax.ShapeDtypeStruct((s, A_WIDTH), F32), jax.ShapeDtypeStruct((s, CONV_CH), F32),
                   jax.ShapeDtypeStruct((s, C_WIDTH), F32), jax.ShapeDtypeStruct((1, d), F32)],
        operands=(du, wt1, x1, dres, g_mlp, wo, ya, yb, yc, gg), name=name, comm=comm)


CONV_CHUNK = 256
CONV_HALO = 8


def _conv_fwd(z, cw, name):
    s = z.shape[0]
    nch = s // CONV_CHUNK

    def body(gb_ref, gc_ref, xb_ref, w_ref, o_ref, us):
        us[pl.ds(0, CONV_HALO), :] = jnp.zeros((CONV_HALO, LANES), F32)
        us[pl.ds(CONV_HALO, s), :] = gc_ref[...] * xb_ref[...]
        w0, w1, w2 = w_ref[0:1, :], w_ref[1:2, :], w_ref[2:3, :]

        def chunk(c, carry):
            st = pl.multiple_of(c * CONV_CHUNK, CONV_CHUNK)
            ext = us[pl.ds(st, CONV_CHUNK + CONV_HALO), :]
            y = (w0 * ext[CONV_HALO - 2:CONV_HALO - 2 + CONV_CHUNK]
                 + w1 * ext[CONV_HALO - 1:CONV_HALO - 1 + CONV_CHUNK]
                 + w2 * ext[CONV_HALO:])
            o_ref[pl.ds(st, CONV_CHUNK), :] = gb_ref[pl.ds(st, CONV_CHUNK), :] * y
            return carry

        lax.fori_loop(0, nch, chunk, 0)

    col = lambda blk: pl.BlockSpec((s, LANES), lambda j, blk=blk: (0, blk + j))
    return pl.pallas_call(
        body,
        grid=(CONV_CH // LANES,),
        in_specs=[col(GB_BLK), col(GC_BLK), col(XB_BLK), pl.BlockSpec((3, LANES), lambda j: (0, j))],
        out_specs=pl.BlockSpec((s, LANES), lambda j: (0, j)),
        out_shape=jax.ShapeDtypeStruct((s, CONV_CH), F32),
        scratch_shapes=[pltpu.VMEM((s + CONV_HALO, LANES), F32)],
        compiler_params=_params("parallel"),
        name=name,
    )(z, z, z, cw)


def _conv_bwd(z, cw, dyb, dz, name):
    s = z.shape[0]
    nch = s // CONV_CHUNK
    ncol = CONV_CH // LANES

    def body(gb_ref, gc_ref, xb_ref, w_ref, dy_ref, dz_in, dz_ref, dw_ref, us, ds_, dgb_ref, dgc_ref, dxb_ref, sems):
        j = pl.program_id(0)

        def to_dz(staged, blk, k):
            cols = pl.ds(pl.multiple_of((blk + j) * LANES, LANES), LANES)
            return pltpu.make_async_copy(staged, dz_ref.at[:, cols], sems.at[k])

        copies = [to_dz(dgb_ref, GB_BLK, 0), to_dz(dgc_ref, GC_BLK, 1), to_dz(dxb_ref, XB_BLK, 2)]

        @pl.when(j > 0)
        def _():
            for cp in copies:
                cp.wait()

        us[pl.ds(0, CONV_HALO), :] = jnp.zeros((CONV_HALO, LANES), F32)
        us[pl.ds(CONV_HALO, s), :] = gc_ref[...] * xb_ref[...]
        ds_[pl.ds(s, CONV_HALO), :] = jnp.zeros((CONV_HALO, LANES), F32)
        ds_[pl.ds(0, s), :] = dy_ref[...] * gb_ref[...]
        w0, w1, w2 = w_ref[0:1, :], w_ref[1:2, :], w_ref[2:3, :]
        zero = jnp.zeros((1, LANES), F32)

        def chunk(c, carry):
            a0, a1, a2 = carry
            st = pl.multiple_of(c * CONV_CHUNK, CONV_CHUNK)
            rows = pl.ds(st, CONV_CHUNK)
            ext = us[pl.ds(st, CONV_CHUNK + CONV_HALO), :]
            um2 = ext[CONV_HALO - 2:CONV_HALO - 2 + CONV_CHUNK]
            um1 = ext[CONV_HALO - 1:CONV_HALO - 1 + CONV_CHUNK]
            u0 = ext[CONV_HALO:]
            dext = ds_[pl.ds(st, CONV_CHUNK + CONV_HALO), :]
            dc0 = dext[:CONV_CHUNK]
            du = w2 * dc0 + w1 * dext[1:1 + CONV_CHUNK] + w0 * dext[2:2 + CONV_CHUNK]
            yconv = w0 * um2 + w1 * um1 + w2 * u0
            dgb_ref[rows, :] = (dy_ref[rows, :] * yconv).astype(BF16)
            dgc_ref[rows, :] = (du * xb_ref[rows, :]).astype(BF16)
            dxb_ref[rows, :] = (du * gc_ref[rows, :]).astype(BF16)
            a0 = a0 + jnp.sum(dc0 * um2, axis=0, keepdims=True)
            a1 = a1 + jnp.sum(dc0 * um1, axis=0, keepdims=True)
            a2 = a2 + jnp.sum(dc0 * u0, axis=0, keepdims=True)
            return a0, a1, a2

        a0, a1, a2 = lax.fori_loop(0, nch, chunk, (zero, zero, zero))
        dw_ref[...] = jnp.concatenate([a0, a1, a2, jnp.zeros((5, LANES), F32)], axis=0)
        for cp in copies:
            cp.start()

        @pl.when(j == ncol - 1)
        def _():
            for cp in copies:
                cp.wait()

    col = lambda blk: pl.BlockSpec((s, LANES), lambda j, blk=blk: (0, blk + j))
    hbm = pl.BlockSpec(memory_space=pl.ANY)
    return pl.pallas_call(
        body,
        grid=(ncol,),
        in_specs=[col(GB_BLK), col(GC_BLK), col(XB_BLK), pl.BlockSpec((3, LANES), lambda j: (0, j)),
                  pl.BlockSpec((s, LANES), lambda j: (0, j)), hbm],
        out_specs=[hbm, pl.BlockSpec((8, LANES), lambda j: (0, j))],
        out_shape=[jax.ShapeDtypeStruct(dz.shape, dz.dtype), jax.ShapeDtypeStruct((8, CONV_CH), F32)],
        scratch_shapes=[pltpu.VMEM((s + CONV_HALO, LANES), F32), pltpu.VMEM((s + CONV_HALO, LANES), F32)]
        + [pltpu.VMEM((s, LANES), BF16)] * 3 + [pltpu.SemaphoreType.DMA((3,))],
        input_output_aliases={5: 0},
        compiler_params=_params("arbitrary"),
        name=name,
    )(z, z, z, cw, dyb, dz)


ATTN_ROWS = 512
ATTN_UNROLL = 8


def _band_rows(b, d, r):
    base = pl.multiple_of(b * (BLOCK * d), BLOCK)
    prev = jnp.maximum(base - BLOCK * d, 0)
    if d == 1:
        return pl.ds(base, BLOCK), pl.ds(pl.multiple_of(prev, BLOCK), BLOCK)
    return pl.ds(base + r, BLOCK, stride=d), pl.ds(prev + r, BLOCK, stride=d)


def _write_band_bias(bias_ref, max_dist):
    qi = lax.broadcasted_iota(jnp.int32, (BLOCK, 2 * BLOCK), 0)
    kj = lax.broadcasted_iota(jnp.int32, (BLOCK, 2 * BLOCK), 1)
    dist = BLOCK + qi - kj
    band = (dist >= 0) & (dist <= max_dist)
    bias_ref[0:BLOCK, :] = jnp.where(band, 0.0, -jnp.inf)
    bias_ref[BLOCK:2 * BLOCK, :] = jnp.where(band & (kj >= BLOCK), 0.0, -jnp.inf)


def _band_bias(bias_ref, b):
    bias = bias_ref[pl.ds(pl.multiple_of(jnp.where(b > 0, 0, BLOCK), BLOCK), BLOCK), :]
    return jnp.concatenate([bias, bias], axis=0)


def _kv_halves(pair):
    zero = jnp.zeros((1, LANES), jnp.int32)
    return zero + (pair >> 1), zero + ((pair + 1) >> 1)


def _stack_heads(t, head0, halves=None):
    top, bottom = jnp.where(head0, t, 0.0), jnp.where(head0, 0.0, t)
    if halves is not None:
        top = jnp.where(halves[0] == 1, pltpu.roll(top, HEAD_DIM, 1), top)
        bottom = jnp.where(halves[1] == 0, pltpu.roll(bottom, HEAD_DIM, 1), bottom)
    return jnp.concatenate([top, bottom], axis=0).astype(BF16)


def _unstack_heads(t, head0, halves=None):
    top, bottom = t[:BLOCK], t[BLOCK:]
    if halves is not None:
        top = jnp.where(halves[0] == 1, pltpu.roll(top, HEAD_DIM, 1), top)
        bottom = jnp.where(halves[1] == 0, pltpu.roll(bottom, HEAD_DIM, 1), bottom)
    return jnp.where(head0, top, bottom)


def _block_loops(s, patterns, unroll, one_block):
    for n, d in enumerate(patterns):
        nb = (s // BLOCK) // d
        ur = min(unroll, d)
        ub = unroll // ur
        for r0 in range(0, d, ur):
            def trip(i, carry, n=n, d=d, r0=r0, ur=ur, ub=ub):
                for u in range(ub):
                    for r in range(r0, r0 + ur):
                        one_block(i * ub + u, d, r, n == 0)
                return carry
            lax.fori_loop(0, nb // ub, trip, 0)


def _attn_fwd(z, m_init, l_init, q_blk, k_blk, v_blk, patterns, max_dist, gqa, name, comm=None):
    s = z.shape[0]
    npair = 3

    def body(q_ref, k_ref, v_ref, mi_ref, o_ref, lse0_ref, lse1_ref, bias_scr, m_scr, l_scr, *kv_scr):
        head0 = lax.broadcasted_iota(jnp.int32, (1, LANES), 1) < HEAD_DIM
        _write_band_bias(bias_scr, max_dist)
        ones = jnp.ones((2 * BLOCK, LANES), BF16)
        k_src, v_src = kv_scr if gqa else (k_ref, v_ref)
        if gqa:
            half = (lax.broadcasted_iota(jnp.int32, (1, LANES), 1) >= HEAD_DIM).astype(jnp.int32)
            swap = ((pl.program_id(0) + half) >> 1) != half

            def expand(c, carry):
                rows = pl.ds(pl.multiple_of(c * ATTN_ROWS, ATTN_ROWS), ATTN_ROWS)
                k_src[rows, :] = jnp.where(swap, pltpu.roll(k_ref[rows, :], HEAD_DIM, 1), k_ref[rows, :])
                v_src[rows, :] = jnp.where(swap, pltpu.roll(v_ref[rows, :], HEAD_DIM, 1), v_ref[rows, :])
                return carry

            lax.fori_loop(0, s // ATTN_ROWS, expand, 0)

        def one_block(b, d, r, first):
            rq, rp = _band_rows(b, d, r)
            q2 = _stack_heads(q_ref[rq, :] * SCALE, head0)
            k2 = jnp.concatenate([k_src[rp, :], k_src[rq, :]], axis=0).astype(BF16)
            v2 = jnp.concatenate([v_src[rp, :], v_src[rq, :]], axis=0).astype(BF16)
            sc = _dot_nt(q2, k2) + _band_bias(bias_scr, b)
            mb = jnp.max(sc, axis=1, keepdims=True)
            p = jnp.exp(sc - mb).astype(BF16)
            ob = _dot_nn(p, jnp.concatenate([v2, ones], axis=1))
            m_blk = _unstack_heads(jnp.broadcast_to(mb, (2 * BLOCK, LANES)), head0)
            l_blk = _unstack_heads(ob[:, LANES:], head0)
            o_blk = _unstack_heads(ob[:, :LANES], head0)
            if first and l_init == 0.0:
                m_new, l_new, o_new = m_blk, l_blk, o_blk
            else:
                if first:
                    m_old, l_old, o_old = jnp.broadcast_to(mi_ref[...], (BLOCK, LANES)), l_init, 0.0
                else:
                    m_old, l_old, o_old = m_scr[rq, :], l_scr[rq, :], o_ref[rq, :]
                m_new = jnp.maximum(m_old, m_blk)
                a_old = jnp.exp(m_old - m_new)
                a_blk = jnp.exp(m_blk - m_new)
                l_new = l_old * a_old + l_blk * a_blk
                o_new = o_old * a_old + o_blk * a_blk
            o_ref[rq, :], l_scr[rq, :], m_scr[rq, :] = o_new, l_new, m_new

        _block_loops(s, patterns, ATTN_UNROLL, one_block)

        def fin(c, carry):
            rows = pl.ds(pl.multiple_of(c * ATTN_ROWS, ATTN_ROWS), ATTN_ROWS)
            l = l_scr[rows, :]
            o_ref[rows, :] = o_ref[rows, :] / l
            lse = m_scr[rows, :] + jnp.log(l)
            swapped = pltpu.roll(lse, HEAD_DIM, 1)
            lse0_ref[rows, :] = jnp.where(head0, lse, swapped)
            lse1_ref[rows, :] = jnp.where(head0, swapped, lse)
            return carry

        lax.fori_loop(0, s // ATTN_ROWS, fin, 0)

    kv = (lambda blk: pl.BlockSpec((s, LANES), lambda j, blk=blk: (0, blk), pipeline_mode=pl.Buffered(1))) if gqa \
        else (lambda blk: pl.BlockSpec((s, LANES), lambda j, blk=blk: (0, blk + j)))
    own = pl.BlockSpec((s, LANES), lambda j: (0, j))
    return _call(
        body,
        grid=(npair,),
        in_specs=[pl.BlockSpec((s, LANES), lambda j: (0, q_blk + j)), kv(k_blk), kv(v_blk),
                  pl.BlockSpec((1, LANES), lambda j: (0, j))],
        out_specs=[own, own, own],
        out_shape=[jax.ShapeDtypeStruct((s, npair * LANES), F32)] * 3,
        operands=(z, z, z, m_init), name=name,
        scratch_shapes=[pltpu.VMEM((2 * BLOCK, 2 * BLOCK), F32)] + [pltpu.VMEM((s, LANES), F32)] * (4 if gqa else 2),
        comm=comm)


def _attn_bwd(z, do, o, lse, m_init, dz, q_blk, k_blk, v_blk, patterns, max_dist, gqa, name, comm=None):
    s = z.shape[0]
    npair = 3
    n_dz_in = 0 if dz is None else 1

    def body(q_ref, k_ref, v_ref, do_ref, o_ref, lse0_ref, lse1_ref, mi_ref, *rest):
        (dz_ref, dm_ref, dq_acc, dk_acc, dv_acc, dl0_scr, dl1_scr, bias_scr,
         dq_out, dk_out, dv_out, out_sems) = rest[n_dz_in:]
        pair = pl.program_id(0)
        head0 = lax.broadcasted_iota(jnp.int32, (1, LANES), 1) < HEAD_DIM
        halves = _kv_halves(pair) if gqa else None
        _write_band_bias(bias_scr, max_dist)

        def zero_kv():
            def f(c, carry):
                rows = pl.ds(pl.multiple_of(c * ATTN_ROWS, ATTN_ROWS), ATTN_ROWS)
                dk_acc[rows, :] = jnp.zeros((ATTN_ROWS, LANES), F32)
                dv_acc[rows, :] = jnp.zeros((ATTN_ROWS, LANES), F32)
                return carry
            lax.fori_loop(0, s // ATTN_ROWS, f, 0)

        if gqa:
            pl.when(pair == 0)(zero_kv)
        else:
            zero_kv()

        def prep(c, dm):
            rows = pl.ds(pl.multiple_of(c * ATTN_ROWS, ATTN_ROWS), ATTN_ROWS)
            dq_acc[rows, :] = jnp.zeros((ATTN_ROWS, LANES), F32)
            prod = do_ref[rows, :] * o_ref[rows, :]
            d0 = jnp.sum(jnp.where(head0, prod, 0.0), axis=1, keepdims=True)
            d1 = jnp.sum(jnp.where(head0, 0.0, prod), axis=1, keepdims=True)
            dl0_scr[rows, :] = jnp.broadcast_to(d0, (ATTN_ROWS, LANES))
            dl1_scr[rows, :] = jnp.broadcast_to(d1, (ATTN_ROWS, LANES))
            lse_own = jnp.where(head0, lse0_ref[rows, :], lse1_ref[rows, :])
            psink = jnp.exp(mi_ref[...] - lse_own)
            return dm - jnp.sum(psink * jnp.where(head0, d0, d1), axis=0, keepdims=True)

        dm_ref[...] = lax.fori_loop(0, s // ATTN_ROWS, prep, jnp.zeros((1, LANES), F32))

        def one_block(b, d, r, first):
            rq, rp = _band_rows(b, d, r)
            q2 = _stack_heads(q_ref[rq, :] * SCALE, head0, halves)
            do2 = _stack_heads(do_ref[rq, :], head0, halves)
            k2 = jnp.concatenate([k_ref[rp, :], k_ref[rq, :]], axis=0).astype(BF16)
            v2 = jnp.concatenate([v_ref[rp, :], v_ref[rq, :]], axis=0).astype(BF16)
            lse2 = jnp.concatenate([lse0_ref[rq, :], lse1_ref[rq, :]], axis=0)
            dl2 = jnp.concatenate([dl0_scr[rq, :], dl1_scr[rq, :]], axis=0)
            lse2 = jnp.concatenate([lse2, lse2], axis=1)
            dl2 = jnp.concatenate([dl2, dl2], axis=1)
            p = jnp.exp(_dot_nt(q2, k2) + _band_bias(bias_scr, b) - lse2)
            dp = _dot_nt(do2, v2)
            dsc = (p * (dp - dl2)).astype(BF16)
            dq2 = _unstack_heads(_dot_nn(dsc, k2), head0, halves)
            dk2 = _dot_tn(dsc, q2)
            dv2 = _dot_tn(p.astype(BF16), do2)
            dq_acc[rq, :] += dq2 * SCALE
            dk_acc[rp, :] += dk2[:BLOCK]
            dk_acc[rq, :] += dk2[BLOCK:]
            dv_acc[rp, :] += dv2[:BLOCK]
            dv_acc[rq, :] += dv2[BLOCK:]

        _block_loops(s, patterns, ATTN_UNROLL, one_block)

        def to_dz(staged, blk, k):
            cols = pl.ds(pl.multiple_of(blk * LANES, LANES), LANES)
            return pltpu.make_async_copy(staged, dz_ref.at[:, cols], out_sems.at[k])

        last_pair = pair == npair - 1
        q_copy = to_dz(dq_out, q_blk + pair, 0)
        kv_copies = [to_dz(dk_out, k_blk + (0 if gqa else pair), 1), to_dz(dv_out, v_blk + (0 if gqa else pair), 2)]

        @pl.when(pair > 0)
        def _():
            for cp in [q_copy] + ([] if gqa else kv_copies):
                cp.wait()

        def stage(acc, out):
            def f(c, carry):
                rows = pl.ds(pl.multiple_of(c * ATTN_ROWS, ATTN_ROWS), ATTN_ROWS)
                out[rows, :] = acc[rows, :].astype(BF16)
                return carry
            lax.fori_loop(0, s // ATTN_ROWS, f, 0)

        def stage_kv():
            stage(dk_acc, dk_out)
            stage(dv_acc, dv_out)
            for cp in kv_copies:
                cp.start()

        stage(dq_acc, dq_out)
        q_copy.start()
        if gqa:
            pl.when(last_pair)(stage_kv)
        else:
            stage_kv()

        @pl.when(last_pair)
        def _():
            for cp in [q_copy] + kv_copies:
                cp.wait()

    own = pl.BlockSpec((s, LANES), lambda j: (0, j))
    hbm = pl.BlockSpec(memory_space=pl.ANY)
    if gqa:
        kv = lambda blk: pl.BlockSpec((s, LANES), lambda j, blk=blk: (0, blk), pipeline_mode=pl.Buffered(1))
    else:
        kv = lambda blk: pl.BlockSpec((s, LANES), lambda j, blk=blk: (0, blk + j))
    in_specs = [pl.BlockSpec((s, LANES), lambda j: (0, q_blk + j)), kv(k_blk), kv(v_blk), own, own, own, own,
                pl.BlockSpec((1, LANES), lambda j: (0, j))]
    operands = (z, z, z, do, o, lse[0], lse[1], m_init)
    return _call(
        body,
        grid=(npair,),
        in_specs=in_specs + [hbm] * n_dz_in,
        out_specs=[hbm, pl.BlockSpec((1, LANES), lambda j: (0, j))],
        out_shape=[jax.ShapeDtypeStruct((s, IN_WIDTH), BF16), jax.ShapeDtypeStruct((1, npair * LANES), F32)],
        operands=operands + (() if dz is None else (dz,)), name=name,
        scratch_shapes=[pltpu.VMEM((s, LANES), F32)] * 5 + [pltpu.VMEM((2 * BLOCK, 2 * BLOCK), F32)]
        + [pltpu.VMEM((s, LANES), BF16)] * 3 + [pltpu.SemaphoreType.DMA((3,))],
        comm=comm, aliases={} if dz is None else {len(in_specs): 0})


def _adamw_math(w, g, m, v):
    m = ADAM_B1 * m + (1.0 - ADAM_B1) * g
    v = ADAM_B2 * v + (1.0 - ADAM_B2) * (g * g)
    m_hat = m / (1.0 - ADAM_B1 ** ADAM_STEP)
    v_hat = v / (1.0 - ADAM_B2 ** ADAM_STEP)
    delta = -ADAM_LR * (m_hat / (jnp.sqrt(v_hat) + ADAM_EPS) + ADAM_WD * w)
    return delta, m, v


def _adamw(w, g, m, v, name):
    rows, cols = w.shape
    tr = min(rows, 256)

    def body(w_ref, g_ref, m_ref, v_ref, d_ref, nm_ref, nv_ref):
        d_ref[...], nm_ref[...], nv_ref[...] = _adamw_math(w_ref[...], g_ref[...], m_ref[...], v_ref[...])

    spec = pl.BlockSpec((tr, cols), lambda i: (i, 0))
    return pl.pallas_call(
        body,
        grid=(rows // tr,),
        in_specs=[spec] * 4,
        out_specs=[spec] * 3,
        out_shape=[jax.ShapeDtypeStruct((rows, cols), F32)] * 3,
        compiler_params=_params("parallel"),
        name=name,
    )(w, g, m, v)


def _sum_adamw(parts, w, m, v, pos, transpose, name):
    assert len(parts) == DEPTH == 2
    (p0, r0), (p1, r1) = parts
    _, rows, cols = p0.shape
    tr = 256 if rows % 256 == 0 else rows
    nt = rows // tr

    def body(pos_ref, p0_ref, r0_ref, p1_ref, r1_ref, w_ref, m_ref, v_ref, g_ref, d_ref, nm_ref, nv_ref):
        def run(p_ref, r_ref):
            g = ((p_ref[...].astype(F32) + r_ref[0].astype(F32)) + r_ref[1].astype(F32)) + r_ref[2].astype(F32)
            if transpose:
                g = g.T
            g_ref[...] = g
            d_ref[...], nm_ref[...], nv_ref[...] = _adamw_math(w_ref[...], g, m_ref[...], v_ref[...])

        layer0 = pl.program_id(0) < nt
        pl.when(layer0)(lambda: run(p0_ref, r0_ref))
        pl.when(jnp.logical_not(layer0))(lambda: run(p1_ref, r1_ref))

    def tile0(i):
        return jnp.minimum(i, nt - 1)

    def tile1(i):
        return jnp.maximum(i - nt, 0)

    if transpose:
        w_spec = pl.BlockSpec((None, cols, tr), lambda i, q: (i // nt, 0, i % nt))
    else:
        w_spec = pl.BlockSpec((None, tr, cols), lambda i, q: (i // nt, i % nt, 0))
    return pl.pallas_call(
        body,
        grid_spec=pltpu.PrefetchScalarGridSpec(
            num_scalar_prefetch=1,
            grid=(DEPTH * nt,),
            in_specs=[pl.BlockSpec((None, tr, cols), lambda i, q: (q[0], tile0(i), 0)),
                      pl.BlockSpec((3, tr, cols), lambda i, q: (0, tile0(i), 0)),
                      pl.BlockSpec((None, tr, cols), lambda i, q: (q[0], tile1(i), 0)),
                      pl.BlockSpec((3, tr, cols), lambda i, q: (0, tile1(i), 0)),
                      w_spec, w_spec, w_spec],
            out_specs=[w_spec] * 4,
        ),
        out_shape=[jax.ShapeDtypeStruct(w.shape, F32)] * 4,
        compiler_params=_params("arbitrary"),
        name=name,
    )(pos, p0, r0, p1, r1, w, m, v)


def _small_sum_adamw(gathered, params, name):
    _, rows, cols = gathered.shape
    n = len(params)

    def body(ga_ref, *refs):
        ins, outs, (g_scr,) = refs[:3 * n], refs[3 * n:7 * n + 2], refs[7 * n + 2:]
        g = ga_ref[0]
        for i in range(1, N_DEV):
            g = g + ga_ref[i]
        g_scr[...] = g
        for k, (row0, w, _, _) in enumerate(params):
            w_ref, m_ref, v_ref = ins[3 * k:3 * k + 3]
            gk = g_scr[row0:row0 + w.shape[0], :]
            outs[4 * k][...] = gk
            outs[4 * k + 1][...], outs[4 * k + 2][...], outs[4 * k + 3][...] = _adamw_math(
                w_ref[...], gk, m_ref[...], v_ref[...])
        outs[4 * n][...] = g_scr[CONV_ROW:CONV_ROW + 8, :]
        outs[4 * n + 1][...] = g_scr[LOSS_ROW:LOSS_ROW + 1, :]

    out_shape = []
    for _, w, _, _ in params:
        out_shape += [jax.ShapeDtypeStruct(w.shape, F32)] * 4
    out_shape += [jax.ShapeDtypeStruct((8, cols), F32), jax.ShapeDtypeStruct((1, cols), F32)]
    res = pl.pallas_call(
        body,
        out_shape=out_shape,
        scratch_shapes=[pltpu.VMEM((rows, cols), F32)],
        name=name,
    )(gathered, *[t for _, w, m, v in params for t in (w, m, v)])
    return [res[4 * k:4 * k + 4] for k in range(n)], res[4 * n], res[4 * n + 1]


def _pair_sum(g4, r1, pos, name):
    _, _, rows, cols = g4.shape
    tr = min(rows, 512)

    def body(pos_ref, g_ref, r_ref, o_ref):
        o_ref[...] = (g_ref[...].astype(F32) + r_ref[...].astype(F32)).astype(BF16)

    return pl.pallas_call(
        body,
        grid_spec=pltpu.PrefetchScalarGridSpec(
            num_scalar_prefetch=1,
            grid=(4, rows // tr),
            in_specs=[pl.BlockSpec((None, None, tr, cols), lambda i, j, p: (i, p[1], j, 0)),
                      pl.BlockSpec((None, tr, cols), lambda i, j, p: (i, j, 0))],
            out_specs=pl.BlockSpec((None, tr, cols), lambda i, j, p: (i, j, 0)),
        ),
        out_shape=jax.ShapeDtypeStruct((4, rows, cols), BF16),
        compiler_params=_params("parallel", "parallel"),
        name=name,
    )(pos, g4, r1)


def _place():
    return lax.axis_index("x"), lax.axis_index("y"), lax.axis_index("c")


def _gather_comm(shards):
    na = len(shards)

    def plan(ins, outs, sems):
        send_sems, recv_sems, local_sems = sems
        x, y, c = _place()
        me, sibling = (x, y, c), (x, y, 1 - c)
        chips = [(1 - x, y), (x, 1 - y), (1 - x, 1 - y)]

        def rows(a, px, py, pc):
            m = ins[a].shape[0]
            return outs[a].at[pl.ds((4 * px + 2 * py + pc) * m, m), :]

        def copy(a, k, block, to, src=None):
            return pltpu.make_async_remote_copy(
                src_ref=rows(a, *block) if src is None else src, dst_ref=rows(a, *block),
                send_sem=send_sems.at[a, k], recv_sem=recv_sems.at[a, k], device_id=to, device_id_type=MESH)

        mine = [pltpu.make_async_copy(ins[a], rows(a, *me), local_sems.at[a]) for a in range(na)]
        first = []
        for a in range(na):
            first.append(copy(a, 0, me, sibling, src=ins[a]))
            first += [copy(a, 1 + j, me, (*chip, c), src=ins[a]) for j, chip in enumerate(chips)]
        return me, sibling, chips, c, copy, mine, first

    def start(ins, outs, sems):
        *_, mine, first = plan(ins, outs, sems)
        for cp in mine + first:
            cp.start()

    def finish(ins, outs, sems):
        me, sibling, chips, c, copy, mine, first = plan(ins, outs, sems)
        passed = []
        for j, chip in enumerate(chips):
            for a in range(na):
                copy(a, 1 + j, (*chip, c), me).wait_recv()
                cp = copy(a, 4 + j, (*chip, c), sibling)
                cp.start()
                passed.append(cp)
        for a in range(na):
            copy(a, 0, sibling, me).wait_recv()
            for j, chip in enumerate(chips):
                copy(a, 4 + j, (*chip, 1 - c), me).wait_recv()
        for cp in first + passed:
            cp.wait_send()
        for cp in mine:
            cp.wait()

    return _Comm(tuple(shards),
                 tuple(jax.ShapeDtypeStruct((N_DEV * t.shape[0], t.shape[1]), t.dtype) for t in shards),
                 (pltpu.SemaphoreType.DMA((na, 7)), pltpu.SemaphoreType.DMA((na, 7)), pltpu.SemaphoreType.DMA((na,))),
                 start, finish)


def _exchange_comm(arrays, out_shape, n_copies, copies_of):
    na = len(arrays)

    def every(ins, outs, sems):
        send_sems, recv_sems = sems
        return [cp for a in range(na) for cp in copies_of(ins, outs, a, send_sems, recv_sems)]

    def start(ins, outs, sems):
        for cp in every(ins, outs, sems):
            cp.start()

    def finish(ins, outs, sems):
        for cp in every(ins, outs, sems):
            cp.wait()

    return _Comm(tuple(arrays), tuple(out_shape),
                 (pltpu.SemaphoreType.DMA((na, n_copies)), pltpu.SemaphoreType.DMA((na, n_copies))), start, finish)


def _sibling_comm(grads):
    def copies_of(ins, outs, a, send_sems, recv_sems):
        x, y, c = _place()
        return [pltpu.make_async_remote_copy(
            src_ref=ins[a].at[chip, 1 - c], dst_ref=outs[a].at[chip],
            send_sem=send_sems.at[a, chip], recv_sem=recv_sems.at[a, chip],
            device_id=(x, y, 1 - c), device_id_type=MESH) for chip in range(4)]

    return _exchange_comm(grads, [jax.ShapeDtypeStruct((4,) + t.shape[2:], t.dtype) for t in grads], 4, copies_of)


def _chip_comm(partials):
    def copies_of(ins, outs, a, send_sems, recv_sems):
        x, y, c = _place()
        chips = [(1 - x, y), (x, 1 - y), (1 - x, 1 - y)]
        return [pltpu.make_async_remote_copy(
            src_ref=ins[a].at[2 * cx + cy], dst_ref=outs[a].at[k],
            send_sem=send_sems.at[a, k], recv_sem=recv_sems.at[a, k],
            device_id=(cx, cy, c), device_id_type=MESH) for k, (cx, cy) in enumerate(chips)]

    return _exchange_comm(partials, [jax.ShapeDtypeStruct((3,) + t.shape[1:], t.dtype) for t in partials], 3, copies_of)


def _pad_rows(t, rows):
    return jnp.pad(t, ((0, rows - t.shape[0]), (0, D_MODEL - t.shape[1])))


MIX_ROW, GROUP_ROW, MLP_ROW, FINAL_ROW, CONV_ROW, SINK_ROW = 0, 8, 16, 24, 32, 40
LOSS_ROW = FINAL_ROW + 1


def _pack_small(g_mix, g_group, g_mlp, g_final, conv, sinks, loss):
    final_and_loss = jnp.concatenate([g_final.reshape(1, D_MODEL), _pad_rows(loss, 1)], axis=0)
    return jnp.concatenate([
        _pad_rows(g_mix, 8), _pad_rows(g_group, 8), _pad_rows(g_mlp, 8), _pad_rows(final_and_loss, 8),
        _pad_rows(conv.reshape(DEPTH * 3, CONV_CH), 8), _pad_rows(sinks.reshape(1, DEPTH * 6), 8)], axis=0)


def kernel(x, w_in, conv_w, sinks, g_mix, g_group, w_o, g_mlp, w_ff_in, w_ff_out, g_final, loss_target, m_w_in, m_conv_w, m_sinks, m_g_mix, m_g_group, m_w_o, m_g_mlp, m_w_ff_in, m_w_ff_out, m_g_final, v_w_in, v_conv_w, v_sinks, v_g_mix, v_g_group, v_w_o, v_g_mlp, v_w_ff_in, v_w_ff_out, v_g_final):
    ax, ay, ac = _place()
    chip = 2 * ax + ay
    dev = 4 * ax + 2 * ay + ac
    pos = jnp.stack([chip, ac]).astype(jnp.int32)

    x0 = x.reshape(SEQ, D_MODEL)
    target = loss_target.reshape(SEQ, D_MODEL)

    shards = {}
    for l in range(DEPTH):
        shards[l, 0], shards[l, 1] = w_in[l].T.astype(BF16), w_o[l].astype(BF16)
        shards[l, 2], shards[l, 3] = w_ff_in[l].T.astype(BF16), w_ff_out[l].astype(BF16)
    conv_tile = jnp.pad(conv_w.reshape(DEPTH * 3, CONV_CH // N_DEV), ((0, 2), (0, LANES - CONV_CH // N_DEV)))
    wt_in0, conv_all = _comm_only(_gather_comm([shards[0, 0], conv_tile]), "gather_first")
    conv_full = conv_all.reshape(N_DEV, 8, LANES)[:, :DEPTH * 3, :CONV_CH // N_DEV]
    conv_full = conv_full.transpose(1, 0, 2).reshape(DEPTH, 3, CONV_CH)

    dx, parts, small = _step(x0, target, shards, wt_in0, conv_full, sinks, g_mix, g_group, g_mlp, g_final, pos)
    return _finish(dx, parts, small, pos, dev, w_in, conv_w, sinks, g_mix, g_group, w_o, g_mlp, w_ff_in, w_ff_out, g_final, m_w_in, m_conv_w, m_sinks, m_g_mix, m_g_group, m_w_o, m_g_mlp, m_w_ff_in, m_w_ff_out, m_g_final, v_w_in, v_conv_w, v_sinks, v_g_mix, v_g_group, v_w_o, v_g_mlp, v_w_ff_in, v_w_ff_out, v_g_final)


FWD_CARRY = {(0, "in_proj"): ((1, 0),), (0, "window"): ((0, 1),), (0, "dilated"): ((0, 2),),
             (0, "mix_ff_in"): ((0, 3),), (0, "ff_out_in_proj"): ((1, 3),),
             (1, "window"): ((1, 1),), (1, "dilated"): ((1, 2),)}


def _step(x0, target, shards, wt_in0, conv_full, sinks, g_mix, g_group, g_mlp, g_final, pos):
    sink_lanes = jnp.repeat(sinks.reshape(DEPTH, 6), HEAD_DIM, axis=1)
    no_sink = jnp.full((1, A_WIDTH), NEG_BIG, F32)
    full = {(0, 0): wt_in0}

    def gather(stage, l):
        keys = FWD_CARRY.get((l, stage), ())
        return keys, (_gather_comm([shards[k] for k in keys]) if keys else None)

    def landed(keys, got):
        full.update(zip(keys, got))

    saved = []
    xc = x0
    keys, comm = gather("in_proj", 0)
    (z, h), got = _norm_mm(xc, g_mix[0:1], full[0, 0], "in_proj_0", comm)
    landed(keys, got)
    for l in range(DEPTH):
        sink_l = sink_lanes[l:l + 1]
        keys, comm = gather("window", l)
        (yc, *lse_c), got = _attn_fwd(z, sink_l, 1.0, QC_BLK, KC_BLK, VC_BLK, (1,), C_MAX_DIST, True,
                                     f"window_attn_{l}", comm)
        landed(keys, got)
        yb = _conv_fwd(z, conv_full[l], f"conv_{l}")
        keys, comm = gather("dilated", l)
        (ya, *lse_a), got = _attn_fwd(z, no_sink, 0.0, QA_BLK, KA_BLK, VA_BLK, DILATED_PATTERNS, A_MAX_DIST, False,
                                     f"dilated_attn_{l}", comm)
        landed(keys, got)
        keys, comm = gather("mix_ff_in", l)
        (y, x1, a, h2), got = _mix_ff_in(ya, yb, yc, g_group[l:l + 1], full[l, 1], xc, g_mlp[l:l + 1], full[l, 2],
                                         f"mix_ff_in_{l}", comm)
        landed(keys, got)
        saved.append((xc, z, h, ya, lse_a, yb, yc, lse_c, sink_l, y, x1, a, h2))
        if l + 1 < DEPTH:
            keys, comm = gather("ff_out_in_proj", l)
            (xc, z, h), got = _ff_out_in_proj(a, full[l, 3], x1, g_mix[l + 1:l + 2], full[l + 1, 0],
                                              f"ff_out_{l}_in_proj_{l + 1}", comm)
            landed(keys, got)

    loss_slab, dx, dxb, dg_final = _mm_res_loss(a, full[DEPTH - 1, 3], x1, g_final.reshape(1, D_MODEL), target,
                                                f"ff_out_{DEPTH - 1}_loss")

    def by_owner(t):
        return t.reshape(4, 2, t.shape[0] // N_DEV, D_MODEL)

    def pair(key, g, r1):
        return _pair_sum(g, r1, pos, f"grad_pair_sum_{key[0]}_{key[1]}")

    partial, r2 = {}, {}
    dg_mix, dg_group, dg_mlp, dconv, dsinks = [None] * DEPTH, [None] * DEPTH, [None] * DEPTH, [None] * DEPTH, [None] * DEPTH
    for l in reversed(range(DEPTH)):
        xin, z, h, ya, lse_a, yb, yc, lse_c, sink_l, y, x1, a, h2 = saved[l]
        late = [(l + 1, 1), (l + 1, 0)] if l + 1 < DEPTH else []
        (du,), got = _mlp_bwd_act(dxb, full[l, 3], a, f"ff_out_bwd_{l}",
                                  _chip_comm([partial[k] for k in late]) if late else None)
        r2.update(zip(late, got))
        (g3, g2), _ = _mm_tn([(a, dxb), (du, h2)], f"grad_w_ff_{l}")
        g3, g2 = by_owner(g3), by_owner(g2)
        (dx1, dx1b, dg_mlp[l], dya, dyb, dyc, dg_group[l]), got = _ff_in_mix_bwd(
            du, full[l, 2], x1, dx, g_mlp[l:l + 1], full[l, 1], ya, yb, yc, g_group[l:l + 1],
            f"ff_in_mix_bwd_{l}", _sibling_comm([g3, g2]))
        partial[l, 3], partial[l, 2] = pair((l, 3), g3, got[0]), pair((l, 2), g2, got[1])
        early = [(l, 3), (l, 2)]
        (dz, _), got = _attn_bwd(z, dya, ya, lse_a, no_sink, None, QA_BLK, KA_BLK, VA_BLK, DILATED_PATTERNS,
                                 A_MAX_DIST, False, f"dilated_attn_bwd_{l}", _chip_comm([partial[k] for k in early]))
        r2.update(zip(early, got))
        dz, dcw = _conv_bwd(z, conv_full[l], dyb, dz, f"conv_bwd_{l}")
        (dz, dsink), _ = _attn_bwd(z, dyc, yc, lse_c, sink_l, dz, QC_BLK, KC_BLK, VC_BLK, (1,), C_MAX_DIST,
                                   True, f"window_attn_bwd_{l}")
        (g1, g0), _ = _mm_tn([(y, dx1b), (dz, h)], f"grad_w_o_in_{l}")
        g1, g0 = by_owner(g1), by_owner(g0)
        if l > 0:
            (dx, dxb, dg_mix[l]), got = _mm_nn_normbwd(dz, full[l, 0], xin, dx1, g_mix[l:l + 1], f"in_proj_bwd_{l}",
                                                      _sibling_comm([g1, g0]))
            partial[l, 1], partial[l, 0] = pair((l, 1), g1, got[0]), pair((l, 0), g0, got[1])
        else:
            got = _comm_only(_sibling_comm([g1, g0]), "grad_sibling_exchange_last")
            partial[l, 1], partial[l, 0] = pair((l, 1), g1, got[0]), pair((l, 0), g0, got[1])
            (dx, dxb, dg_mix[l]), got = _mm_nn_normbwd(dz, full[l, 0], xin, dx1, g_mix[l:l + 1], f"in_proj_bwd_{l}",
                                                      _chip_comm([partial[l, 1], partial[l, 0]]))
            r2[l, 1], r2[l, 0] = got
        dconv[l] = dcw[:3]
        dsinks[l] = dsink[0, ::HEAD_DIM]
    parts = {key: (partial[key], r2[key]) for key in partial}
    small = _pack_small(jnp.concatenate(dg_mix), jnp.concatenate(dg_group), jnp.concatenate(dg_mlp),
                        dg_final, jnp.stack(dconv), jnp.stack(dsinks), loss_slab[0:1])
    return dx, parts, small


def _finish(dx, parts, small, pos, dev, w_in, conv_w, sinks, g_mix, g_group, w_o, g_mlp, w_ff_in, w_ff_out, g_final, m_w_in, m_conv_w, m_sinks, m_g_mix, m_g_group, m_w_o, m_g_mlp, m_w_ff_in, m_w_ff_out, m_g_final, v_w_in, v_conv_w, v_sinks, v_g_mix, v_g_group, v_w_o, v_g_mlp, v_w_ff_in, v_w_ff_out, v_g_final):
    grad_x = dx.reshape(1, SEQ, D_MODEL)

    (small_all,) = _comm_only(_gather_comm([small]), "gather_small_grads")
    row = lambda t: t.reshape(1, D_MODEL)
    sink_row = lambda t: _pad_rows(t.reshape(1, DEPTH * 6), 1)
    params = [(MIX_ROW, g_mix, m_g_mix, v_g_mix), (GROUP_ROW, g_group, m_g_group, v_g_group),
              (MLP_ROW, g_mlp, m_g_mlp, v_g_mlp), (FINAL_ROW, row(g_final), row(m_g_final), row(v_g_final)),
              (SINK_ROW, sink_row(sinks), sink_row(m_sinks), sink_row(v_sinks))]
    updated, conv_rows, loss_row = _small_sum_adamw(small_all.reshape(N_DEV, SMALL_ROWS, D_MODEL), params, "small_adamw")
    loss = loss_row[0, 0]
    (grad_g_mix, delta_g_mix, new_m_g_mix, new_v_g_mix), (grad_g_group, delta_g_group, new_m_g_group, new_v_g_group), \
        (grad_g_mlp, delta_g_mlp, new_m_g_mlp, new_v_g_mlp), final4, sinks4 = updated
    grad_g_final, delta_g_final, new_m_g_final, new_v_g_final = [t.reshape(D_MODEL) for t in final4]
    grad_sinks, delta_sinks, new_m_sinks, new_v_sinks = [t[0, :DEPTH * 6].reshape(DEPTH, 2, 3) for t in sinks4]
    conv_grad_full = conv_rows[:DEPTH * 3, :CONV_CH].reshape(DEPTH, 3, CONV_CH)
    cs = CONV_CH // N_DEV
    grad_conv_w = lax.dynamic_slice_in_dim(conv_grad_full, dev * cs, cs, axis=2)

    def tile_of(t):
        return jnp.pad(t.reshape(1, DEPTH * 3 * cs), ((0, 7), (0, 256 - DEPTH * 3 * cs)))

    cd, cm, cv = _adamw(tile_of(conv_w), tile_of(grad_conv_w), tile_of(m_conv_w), tile_of(v_conv_w), "conv_adamw")
    untile = lambda t: t[0, :DEPTH * 3 * cs].reshape(DEPTH, 3, cs)
    delta_conv_w, new_m_conv_w, new_v_conv_w = untile(cd), untile(cm), untile(cv)

    def big(kind, w, m, v, transpose, name):
        return _sum_adamw([parts[l, kind] for l in range(DEPTH)], w, m, v, pos, transpose, name)

    grad_w_in, delta_w_in, new_m_w_in, new_v_w_in = big(0, w_in, m_w_in, v_w_in, True, "adamw_w_in")
    grad_w_o, delta_w_o, new_m_w_o, new_v_w_o = big(1, w_o, m_w_o, v_w_o, False, "adamw_w_o")
    grad_w_ff_in, delta_w_ff_in, new_m_w_ff_in, new_v_w_ff_in = big(2, w_ff_in, m_w_ff_in, v_w_ff_in, True, "adamw_w_ff_in")
    grad_w_ff_out, delta_w_ff_out, new_m_w_ff_out, new_v_w_ff_out = big(3, w_ff_out, m_w_ff_out, v_w_ff_out, False,
                                                                         "adamw_w_ff_out")

    return (loss, grad_x, grad_w_in, grad_conv_w, grad_sinks, grad_g_mix, grad_g_group, grad_w_o, grad_g_mlp,
            grad_w_ff_in, grad_w_ff_out, grad_g_final,
            delta_w_in, delta_conv_w, delta_sinks, delta_g_mix, delta_g_group, delta_w_o, delta_g_mlp,
            delta_w_ff_in, delta_w_ff_out, delta_g_final,
            new_m_w_in, new_m_conv_w, new_m_sinks, new_m_g_mix, new_m_g_group, new_m_w_o, new_m_g_mlp,
            new_m_w_ff_in, new_m_w_ff_out, new_m_g_final,
            new_v_w_in, new_v_conv_w, new_v_sinks, new_v_g_mix, new_v_g_group, new_v_w_o, new_v_g_mlp,
            new_v_w_ff_in, new_v_w_ff_out, new_v_g_final)
```

```python
from typing import Callable, NamedTuple

import jax
import jax.numpy as jnp
from jax import lax
from jax.experimental import pallas as pl
from jax.experimental.pallas import tpu as pltpu

F32 = jnp.float32
BF16 = jnp.bfloat16
MESH = pl.DeviceIdType.MESH

N_DEV = 8
SEQ = 4096
D_MODEL = 1024
DEPTH = 2
HEAD_DIM = 64
LANES = 128
A_WIDTH = 384
CONV_CH = 256
C_WIDTH = 384
KV_WIDTH = 128
IN_WIDTH = 2560
D_FF = 4096
BLOCK = 128
DILATED_PATTERNS = (1, 4, 16)
A_MAX_DIST = 128
C_MAX_DIST = 127
EPS = 1e-6
SCALE = HEAD_DIM ** -0.5
NEG_BIG = -1e30
F32_TINY = 1.1754944e-38

QA_BLK, KA_BLK, VA_BLK = 0, 3, 6
GB_BLK, GC_BLK, XB_BLK = 9, 11, 13
QC_BLK, KC_BLK, VC_BLK = 15, 18, 19

ADAM_LR = 0.001
ADAM_B1 = 0.9
ADAM_B2 = 0.999
ADAM_EPS = 1e-08
ADAM_WD = 0.01
ADAM_STEP = 10

VMEM_LIMIT = 56 * 1024 * 1024
TILE_BUDGET = 46 * 1024 * 1024
ROW_TILE = 512
COL_CHUNK = 512
SMALL_ROWS = 48


def _dot_nn(a, b):
    return lax.dot_general(a, b, (((1,), (0,)), ((), ())), preferred_element_type=F32)


def _dot_nt(a, b):
    return lax.dot_general(a, b, (((1,), (1,)), ((), ())), preferred_element_type=F32)


def _dot_tn(a, b):
    return lax.dot_general(a, b, (((0,), (0,)), ((), ())), preferred_element_type=F32)


def _params(*sem):
    return pltpu.CompilerParams(dimension_semantics=sem, vmem_limit_bytes=VMEM_LIMIT)


def _resident(shape):
    return pl.BlockSpec(shape, lambda i: (0,) * len(shape), pipeline_mode=pl.Buffered(1))


def _row_tile(row_bytes, resident_bytes):
    for tm in (ROW_TILE, ROW_TILE // 2):
        if 2 * tm * row_bytes + resident_bytes <= TILE_BUDGET:
            return tm
    return ROW_TILE // 4


def _rms_scale(t):
    return lax.rsqrt(jnp.mean(t * t, axis=-1, keepdims=True) + EPS)


def _rms_bwd(n, r, dn):
    return r * (dn - n * jnp.mean(dn * n, axis=-1, keepdims=True))


class _Comm(NamedTuple):
    arrays: tuple
    out_shape: tuple
    sems: tuple
    start: Callable
    finish: Callable


def _call(body, grid, in_specs, out_specs, out_shape, operands, name, scratch_shapes=(), comm=None, aliases=None):
    n_in, n_out, n_scr = len(in_specs), len(out_shape), len(scratch_shapes)
    aliases = dict(aliases or {})
    if comm is None:
        res = pl.pallas_call(body, grid=grid, in_specs=list(in_specs), out_specs=list(out_specs),
                             out_shape=list(out_shape), scratch_shapes=list(scratch_shapes),
                             input_output_aliases=aliases,
                             compiler_params=_params("arbitrary"), name=name)(*operands)
        return list(res), []
    c_in, c_out = len(comm.arrays), len(comm.out_shape)
    hbm = pl.BlockSpec(memory_space=pl.ANY)
    last = grid[0] - 1

    def carried(*refs):
        ins, cins = refs[:n_in], refs[n_in:n_in + c_in]
        o0 = n_in + c_in
        outs, couts = refs[o0:o0 + n_out], refs[o0 + n_out:o0 + n_out + c_out]
        s0 = o0 + n_out + c_out
        scr, sems = refs[s0:s0 + n_scr], refs[s0 + n_scr:]
        pl.when(pl.program_id(0) == 0)(lambda: comm.start(cins, couts, sems))
        body(*ins, *outs, *scr)
        pl.when(pl.program_id(0) == last)(lambda: comm.finish(cins, couts, sems))

    res = pl.pallas_call(carried, grid=grid, in_specs=list(in_specs) + [hbm] * c_in,
                         out_specs=list(out_specs) + [hbm] * c_out, out_shape=list(out_shape) + list(comm.out_shape),
                         scratch_shapes=list(scratch_shapes) + list(comm.sems), input_output_aliases=aliases,
                         compiler_params=_params("arbitrary"), name=name)(*operands, *comm.arrays)
    return list(res[:n_out]), list(res[n_out:])


def _comm_only(comm, name):
    hbm = pl.BlockSpec(memory_space=pl.ANY)
    c_in, c_out = len(comm.arrays), len(comm.out_shape)

    def body(*refs):
        ins, outs, sems = refs[:c_in], refs[c_in:c_in + c_out], refs[c_in + c_out:]
        comm.start(ins, outs, sems)
        comm.finish(ins, outs, sems)

    return pl.pallas_call(body, in_specs=[hbm] * c_in, out_specs=[hbm] * c_out, out_shape=list(comm.out_shape),
                          scratch_shapes=list(comm.sems), name=name)(*comm.arrays)


def _norm_mm(x, g, wt, name, comm=None):
    s, d = x.shape
    n = wt.shape[0]
    tm = _row_tile(4 * d + 4 * n + 2 * d, 2 * n * d)

    def body(x_ref, g_ref, w_ref, o_ref, h_ref):
        xx = x_ref[...]
        h = ((xx * _rms_scale(xx)) * g_ref[...]).astype(BF16)
        h_ref[...] = h
        for n0 in range(0, n, COL_CHUNK):
            o_ref[:, n0:n0 + COL_CHUNK] = _dot_nt(h, w_ref[n0:n0 + COL_CHUNK, :])

    return _call(
        body,
        grid=(s // tm,),
        in_specs=[pl.BlockSpec((tm, d), lambda i: (i, 0)),
                  pl.BlockSpec((1, d), lambda i: (0, 0)),
                  _resident((n, d))],
        out_specs=[pl.BlockSpec((tm, n), lambda i: (i, 0)),
                   pl.BlockSpec((tm, d), lambda i: (i, 0))],
        out_shape=[jax.ShapeDtypeStruct((s, n), F32), jax.ShapeDtypeStruct((s, d), BF16)],
        operands=(x, g, wt), name=name, comm=comm)


def _ff_out_in_proj(a, w2, x1, g, wt, name, comm=None):
    s, f = a.shape
    d = w2.shape[1]
    n = wt.shape[0]
    tm = _row_tile(2 * f + 4 * d + 4 * d + 4 * n + 2 * d, 2 * f * d + 2 * n * d)

    def body(a_ref, w2_ref, x_ref, g_ref, w_ref, x2_ref, z_ref, h_ref):
        x2 = x_ref[...] + _dot_nn(a_ref[...], w2_ref[...])
        x2_ref[...] = x2
        h = ((x2 * _rms_scale(x2)) * g_ref[...]).astype(BF16)
        h_ref[...] = h
        for n0 in range(0, n, COL_CHUNK):
            z_ref[:, n0:n0 + COL_CHUNK] = _dot_nt(h, w_ref[n0:n0 + COL_CHUNK, :])

    rows = lambda w: pl.BlockSpec((tm, w), lambda i: (i, 0))
    return _call(
        body,
        grid=(s // tm,),
        in_specs=[rows(f), _resident((f, d)), rows(d), pl.BlockSpec((1, d), lambda i: (0, 0)), _resident((n, d))],
        out_specs=[rows(d), rows(n), rows(d)],
        out_shape=[jax.ShapeDtypeStruct((s, d), F32), jax.ShapeDtypeStruct((s, n), F32),
                   jax.ShapeDtypeStruct((s, d), BF16)],
        operands=(a, w2, x1, g, wt), name=name, comm=comm)


def _mix_ff_in(ya, yb, yc, gg, wo, x0, g_mlp, wt1, name, comm=None):
    s = ya.shape[0]
    d = wo.shape[1]
    f = wt1.shape[0]
    tm = _row_tile(4 * d + 4 * d + 2 * d + 4 * d + 2 * d + 2 * f, 2 * d * d + 2 * f * d)

    def body(ya_ref, yb_ref, yc_ref, gg_ref, wo_ref, x_ref, g_ref, w1_ref, y_ref, x1_ref, a_ref, h_ref):
        parts = []
        for ref in (ya_ref, yb_ref, yc_ref):
            t = ref[...]
            parts.append(t * _rms_scale(t))
        y = (jnp.concatenate(parts, axis=1) * gg_ref[...]).astype(BF16)
        y_ref[...] = y
        x1 = x_ref[...] + _dot_nn(y, wo_ref[...])
        x1_ref[...] = x1
        h = ((x1 * _rms_scale(x1)) * g_ref[...]).astype(BF16)
        h_ref[...] = h
        for n0 in range(0, f, COL_CHUNK):
            u = _dot_nt(h, w1_ref[n0:n0 + COL_CHUNK, :])
            a_ref[:, n0:n0 + COL_CHUNK] = jnp.square(jnp.maximum(u, 0.0)).astype(BF16)

    rows = lambda w: pl.BlockSpec((tm, w), lambda i: (i, 0))
    vec = pl.BlockSpec((1, d), lambda i: (0, 0))
    return _call(
        body,
        grid=(s // tm,),
        in_specs=[rows(A_WIDTH), rows(CONV_CH), rows(C_WIDTH), vec, _resident((d, d)), rows(d), vec, _resident((f, d))],
        out_specs=[rows(d), rows(d), rows(f), rows(d)],
        out_shape=[jax.ShapeDtypeStruct((s, d), BF16), jax.ShapeDtypeStruct((s, d), F32),
                   jax.ShapeDtypeStruct((s, f), BF16), jax.ShapeDtypeStruct((s, d), BF16)],
        operands=(ya, yb, yc, gg, wo, x0, g_mlp, wt1), name=name, comm=comm)


def _relu_from_square(av):
    return av * lax.rsqrt(jnp.maximum(av, F32_TINY))


def _mm_res_loss(a, w2, x1, g, target, name):
    s, f = a.shape
    d = w2.shape[1]
    tm = _row_tile(2 * f + 4 * d + 4 * d + 4 * d + 2 * d + 2 * f, 2 * f * d)

    def body(a_ref, w_ref, x_ref, g_ref, t_ref, loss_ref, dx_ref, dxb_ref, dg_ref, du_ref):
        @pl.when(pl.program_id(0) == 0)
        def _():
            loss_ref[...] = jnp.zeros_like(loss_ref)
            dg_ref[...] = jnp.zeros_like(dg_ref)

        xx = x_ref[...] + _dot_nn(a_ref[...], w_ref[...])
        r = _rms_scale(xx)
        n = xx * r
        gv = g_ref[...]
        err = n * gv - t_ref[...]
        per_tok = jnp.sum(err * err, axis=1, keepdims=True) * (1.0 / d)
        loss_ref[...] += 0.5 * jnp.sum(per_tok, axis=0, keepdims=True)
        dout = err * (1.0 / d)
        dg_ref[...] += jnp.sum(dout * n, axis=0, keepdims=True)
        dx = _rms_bwd(n, r, dout * gv)
        dx_ref[...] = dx
        dxb = dx.astype(BF16)
        dxb_ref[...] = dxb
        for n0 in range(0, f, COL_CHUNK):
            da = _dot_nt(dxb, w_ref[n0:n0 + COL_CHUNK, :])
            rl = _relu_from_square(a_ref[:, n0:n0 + COL_CHUNK].astype(F32))
            du_ref[:, n0:n0 + COL_CHUNK] = (da * (2.0 * rl)).astype(BF16)

    rows = lambda w: pl.BlockSpec((tm, w), lambda i: (i, 0))
    vec = pl.BlockSpec((1, d), lambda i: (0, 0))
    return pl.pallas_call(
        body,
        grid=(s // tm,),
        in_specs=[rows(f), _resident((f, d)), rows(d), vec, rows(d)],
        out_specs=[pl.BlockSpec((8, LANES), lambda i: (0, 0)), rows(d), rows(d), vec, rows(f)],
        out_shape=[jax.ShapeDtypeStruct((8, LANES), F32), jax.ShapeDtypeStruct((s, d), F32),
                   jax.ShapeDtypeStruct((s, d), BF16), jax.ShapeDtypeStruct((1, d), F32),
                   jax.ShapeDtypeStruct((s, f), BF16)],
        compiler_params=_params("arbitrary"),
        name=name,
    )(a, w2, x1, g, target)


def _mlp_bwd_act(dxb, w2, a, name, comm=None):
    s, d = dxb.shape
    f = w2.shape[0]
    tm = _row_tile(2 * d + 2 * f + 2 * f, 2 * f * d)

    def body(dx_ref, w_ref, a_ref, du_ref):
        dx = dx_ref[...]
        for n0 in range(0, f, COL_CHUNK):
            da = _dot_nt(dx, w_ref[n0:n0 + COL_CHUNK, :])
            rl = _relu_from_square(a_ref[:, n0:n0 + COL_CHUNK].astype(F32))
            du_ref[:, n0:n0 + COL_CHUNK] = (da * (2.0 * rl)).astype(BF16)

    return _call(
        body,
        grid=(s // tm,),
        in_specs=[pl.BlockSpec((tm, d), lambda i: (i, 0)),
                  _resident((f, d)),
                  pl.BlockSpec((tm, f), lambda i: (i, 0))],
        out_specs=[pl.BlockSpec((tm, f), lambda i: (i, 0))],
        out_shape=[jax.ShapeDtypeStruct((s, f), BF16)],
        operands=(dxb, w2, a), name=name, comm=comm)


def _mm_tn(pairs, name, comm=None):
    s, d = pairs[0][1].shape
    tn = 512
    tiles = [a.shape[1] // tn for a, _ in pairs]
    starts = [sum(tiles[:k]) for k in range(len(pairs))]

    def body(*refs):
        ins, outs, acc = refs[:2 * len(pairs)], refs[2 * len(pairs):3 * len(pairs)], refs[3 * len(pairs)]
        j = pl.program_id(0)
        for k in range(len(pairs)):
            def run(a_ref=ins[2 * k], b_ref=ins[2 * k + 1], o_ref=outs[k]):
                for k0 in range(0, s, ROW_TILE):
                    part = _dot_tn(a_ref[k0:k0 + ROW_TILE, :], b_ref[k0:k0 + ROW_TILE, :])
                    if k0 == 0:
                        acc[...] = part
                    else:
                        acc[...] += part
                o_ref[...] = acc[...].astype(BF16)

            pl.when((j >= starts[k]) & (j < starts[k] + tiles[k]))(run)

    def tile_of(k):
        return lambda j: jnp.clip(j - starts[k], 0, tiles[k] - 1)

    in_specs, out_specs = [], []
    for k in range(len(pairs)):
        in_specs += [pl.BlockSpec((s, tn), lambda j, t=tile_of(k): (0, t(j))), _resident((s, d))]
        out_specs.append(pl.BlockSpec((tn, d), lambda j, t=tile_of(k): (t(j), 0)))
    return _call(
        body,
        grid=(sum(tiles),),
        in_specs=in_specs,
        out_specs=out_specs,
        out_shape=[jax.ShapeDtypeStruct((a.shape[1], d), BF16) for a, _ in pairs],
        operands=tuple(t for pair in pairs for t in pair), name=name,
        scratch_shapes=[pltpu.VMEM((tn, d), F32)], comm=comm)


def _mm_nn_normbwd(dact, wt, x, dres, g, name, comm=None):
    s, kdim = dact.shape
    d = wt.shape[1]
    tm = _row_tile(2 * kdim + 4 * d + 4 * d + 4 * d + 2 * d, 2 * kdim * d)

    def body(a_ref, w_ref, x_ref, r_ref, g_ref, o_ref, ob_ref, dg_ref):
        @pl.when(pl.program_id(0) == 0)
        def _():
            dg_ref[...] = jnp.zeros_like(dg_ref)

        dh = _dot_nn(a_ref[...], w_ref[...])
        xx = x_ref[...]
        r = _rms_scale(xx)
        n = xx * r
        dg_ref[...] += jnp.sum(dh * n, axis=0, keepdims=True)
        dx = r_ref[...] + _rms_bwd(n, r, dh * g_ref[...])
        o_ref[...] = dx
        ob_ref[...] = dx.astype(BF16)

    return _call(
        body,
        grid=(s // tm,),
        in_specs=[pl.BlockSpec((tm, kdim), lambda i: (i, 0)),
                  _resident((kdim, d)),
                  pl.BlockSpec((tm, d), lambda i: (i, 0)),
                  pl.BlockSpec((tm, d), lambda i: (i, 0)),
                  pl.BlockSpec((1, d), lambda i: (0, 0))],
        out_specs=[pl.BlockSpec((tm, d), lambda i: (i, 0)),
                   pl.BlockSpec((tm, d), lambda i: (i, 0)),
                   pl.BlockSpec((1, d), lambda i: (0, 0))],
        out_shape=[jax.ShapeDtypeStruct((s, d), F32), jax.ShapeDtypeStruct((s, d), BF16),
                   jax.ShapeDtypeStruct((1, d), F32)],
        operands=(dact, wt, x, dres, g), name=name, comm=comm)


def _ff_in_mix_bwd(du, wt1, x1, dres, g_mlp, wo, ya, yb, yc, gg, name, comm=None):
    s, f = du.shape
    d = wt1.shape[1]
    widths = (A_WIDTH, CONV_CH, C_WIDTH)
    tm = _row_tile(2 * f + 4 * d + 4 * d + 4 * d + 2 * d + 4 * d + 4 * d, 2 * f * d + 2 * d * d)

    def body(du_ref, w1_ref, x_ref, r_ref, g_ref, wo_ref, ya_ref, yb_ref, yc_ref, gg_ref,
             dx_ref, dxb_ref, dg_ref, da_ref, db_ref, dc_ref, dgg_ref):
        @pl.when(pl.program_id(0) == 0)
        def _():
            dg_ref[...] = jnp.zeros_like(dg_ref)
            dgg_ref[...] = jnp.zeros_like(dgg_ref)

        dh = _dot_nn(du_ref[...], w1_ref[...])
        xx = x_ref[...]
        r = _rms_scale(xx)
        n = xx * r
        dg_ref[...] += jnp.sum(dh * n, axis=0, keepdims=True)
        dx = r_ref[...] + _rms_bwd(n, r, dh * g_ref[...])
        dx_ref[...] = dx
        dxb = dx.astype(BF16)
        dxb_ref[...] = dxb

        dy = _dot_nt(dxb, wo_ref[...])
        gv = gg_ref[...]
        off = 0
        dgs = []
        for ref, out, w in zip((ya_ref, yb_ref, yc_ref), (da_ref, db_ref, dc_ref), widths):
            t = ref[...]
            r = _rms_scale(t)
            n = t * r
            dyg = dy[:, off:off + w]
            dgs.append(jnp.sum(dyg * n, axis=0, keepdims=True))
            out[...] = _rms_bwd(n, r, dyg * gv[:, off:off + w])
            off += w
        dgg_ref[...] += jnp.concatenate(dgs, axis=1)

    rows = lambda w: pl.BlockSpec((tm, w), lambda i: (i, 0))
    vec = pl.BlockSpec((1, d), lambda i: (0, 0))
    return _call(
        body,
        grid=(s // tm,),
        in_specs=[rows(f), _resident((f, d)), rows(d), rows(d), vec, _resident((d, d)),
                  rows(A_WIDTH), rows(CONV_CH), rows(C_WIDTH), vec],
        out_specs=[rows(d), rows(d), vec, rows(A_WIDTH), rows(CONV_CH), rows(C_WIDTH), vec],
        out_shape=[jax.ShapeDtypeStruct((s, d), F32), jax.ShapeDtypeStruct((s, d), BF16), jax.ShapeDtypeStruct((1, d), F32),
                   jax.ShapeDtypeStruct((s, A_WIDTH), F32), jax.ShapeDtypeStruct((s, CONV_CH), F32),
                   jax.ShapeDtypeStruct((s, C_WIDTH), F32), jax.ShapeDtypeStruct((1, d), F32)],
        operands=(du, wt1, x1, dres, g_mlp, wo, ya, yb, yc, gg), name=name, comm=comm)


CONV_CHUNK = 256
CONV_HALO = 8


def _conv_fwd(z, cw, name):
    s = z.shape[0]
    nch = s // CONV_CHUNK

    def body(gb_ref, gc_ref, xb_ref, w_ref, o_ref, us):
        us[pl.ds(0, CONV_HALO), :] = jnp.zeros((CONV_HALO, LANES), F32)
        us[pl.ds(CONV_HALO, s), :] = gc_ref[...] * xb_ref[...]
        w0, w1, w2 = w_ref[0:1, :], w_ref[1:2, :], w_ref[2:3, :]

        def chunk(c, carry):
            st = pl.multiple_of(c * CONV_CHUNK, CONV_CHUNK)
            ext = us[pl.ds(st, CONV_CHUNK + CONV_HALO), :]
            y = (w0 * ext[CONV_HALO - 2:CONV_HALO - 2 + CONV_CHUNK]
                 + w1 * ext[CONV_HALO - 1:CONV_HALO - 1 + CONV_CHUNK]
                 + w2 * ext[CONV_HALO:])
            o_ref[pl.ds(st, CONV_CHUNK), :] = gb_ref[pl.ds(st, CONV_CHUNK), :] * y
            return carry

        lax.fori_loop(0, nch, chunk, 0)

    col = lambda blk: pl.BlockSpec((s, LANES), lambda j, blk=blk: (0, blk + j))
    return pl.pallas_call(
        body,
        grid=(CONV_CH // LANES,),
        in_specs=[col(GB_BLK), col(GC_BLK), col(XB_BLK), pl.BlockSpec((3, LANES), lambda j: (0, j))],
        out_specs=pl.BlockSpec((s, LANES), lambda j: (0, j)),
        out_shape=jax.ShapeDtypeStruct((s, CONV_CH), F32),
        scratch_shapes=[pltpu.VMEM((s + CONV_HALO, LANES), F32)],
        compiler_params=_params("parallel"),
        name=name,
    )(z, z, z, cw)


def _conv_bwd(z, cw, dyb, dz, name):
    s = z.shape[0]
    nch = s // CONV_CHUNK
    ncol = CONV_CH // LANES

    def body(gb_ref, gc_ref, xb_ref, w_ref, dy_ref, dz_in, dz_ref, dw_ref, us, ds_, dgb_ref, dgc_ref, dxb_ref, sems):
        j = pl.program_id(0)

        def to_dz(staged, blk, k):
            cols = pl.ds(pl.multiple_of((blk + j) * LANES, LANES), LANES)
            return pltpu.make_async_copy(staged, dz_ref.at[:, cols], sems.at[k])

        copies = [to_dz(dgb_ref, GB_BLK, 0), to_dz(dgc_ref, GC_BLK, 1), to_dz(dxb_ref, XB_BLK, 2)]

        @pl.when(j > 0)
        def _():
            for cp in copies:
                cp.wait()

        us[pl.ds(0, CONV_HALO), :] = jnp.zeros((CONV_HALO, LANES), F32)
        us[pl.ds(CONV_HALO, s), :] = gc_ref[...] * xb_ref[...]
        ds_[pl.ds(s, CONV_HALO), :] = jnp.zeros((CONV_HALO, LANES), F32)
        ds_[pl.ds(0, s), :] = dy_ref[...] * gb_ref[...]
        w0, w1, w2 = w_ref[0:1, :], w_ref[1:2, :], w_ref[2:3, :]
        zero = jnp.zeros((1, LANES), F32)

        def chunk(c, carry):
            a0, a1, a2 = carry
            st = pl.multiple_of(c * CONV_CHUNK, CONV_CHUNK)
            rows = pl.ds(st, CONV_CHUNK)
            ext = us[pl.ds(st, CONV_CHUNK + CONV_HALO), :]
            um2 = ext[CONV_HALO - 2:CONV_HALO - 2 + CONV_CHUNK]
            um1 = ext[CONV_HALO - 1:CONV_HALO - 1 + CONV_CHUNK]
            u0 = ext[CONV_HALO:]
            dext = ds_[pl.ds(st, CONV_CHUNK + CONV_HALO), :]
            dc0 = dext[:CONV_CHUNK]
            du = w2 * dc0 + w1 * dext[1:1 + CONV_CHUNK] + w0 * dext[2:2 + CONV_CHUNK]
            yconv = w0 * um2 + w1 * um1 + w2 * u0
            dgb_ref[rows, :] = (dy_ref[rows, :] * yconv).astype(BF16)
            dgc_ref[rows, :] = (du * xb_ref[rows, :]).astype(BF16)
            dxb_ref[rows, :] = (du * gc_ref[rows, :]).astype(BF16)
            a0 = a0 + jnp.sum(dc0 * um2, axis=0, keepdims=True)
            a1 = a1 + jnp.sum(dc0 * um1, axis=0, keepdims=True)
            a2 = a2 + jnp.sum(dc0 * u0, axis=0, keepdims=True)
            return a0, a1, a2

        a0, a1, a2 = lax.fori_loop(0, nch, chunk, (zero, zero, zero))
        dw_ref[...] = jnp.concatenate([a0, a1, a2, jnp.zeros((5, LANES), F32)], axis=0)
        for cp in copies:
            cp.start()

        @pl.when(j == ncol - 1)
        def _():
            for cp in copies:
                cp.wait()

    col = lambda blk: pl.BlockSpec((s, LANES), lambda j, blk=blk: (0, blk + j))
    hbm = pl.BlockSpec(memory_space=pl.ANY)
    return pl.pallas_call(
        body,
        grid=(ncol,),
        in_specs=[col(GB_BLK), col(GC_BLK), col(XB_BLK), pl.BlockSpec((3, LANES), lambda j: (0, j)),
                  pl.BlockSpec((s, LANES), lambda j: (0, j)), hbm],
        out_specs=[hbm, pl.BlockSpec((8, LANES), lambda j: (0, j))],
        out_shape=[jax.ShapeDtypeStruct(dz.shape, dz.dtype), jax.ShapeDtypeStruct((8, CONV_CH), F32)],
        scratch_shapes=[pltpu.VMEM((s + CONV_HALO, LANES), F32), pltpu.VMEM((s + CONV_HALO, LANES), F32)]
        + [pltpu.VMEM((s, LANES), BF16)] * 3 + [pltpu.SemaphoreType.DMA((3,))],
        input_output_aliases={5: 0},
        compiler_params=_params("arbitrary"),
        name=name,
    )(z, z, z, cw, dyb, dz)


ATTN_ROWS = 512
ATTN_UNROLL = 8


def _band_rows(b, d, r):
    base = pl.multiple_of(b * (BLOCK * d), BLOCK)
    prev = jnp.maximum(base - BLOCK * d, 0)
    if d == 1:
        return pl.ds(base, BLOCK), pl.ds(pl.multiple_of(prev, BLOCK), BLOCK)
    return pl.ds(base + r, BLOCK, stride=d), pl.ds(prev + r, BLOCK, stride=d)


def _write_band_bias(bias_ref, max_dist):
    qi = lax.broadcasted_iota(jnp.int32, (BLOCK, 2 * BLOCK), 0)
    kj = lax.broadcasted_iota(jnp.int32, (BLOCK, 2 * BLOCK), 1)
    dist = BLOCK + qi - kj
    band = (dist >= 0) & (dist <= max_dist)
    bias_ref[0:BLOCK, :] = jnp.where(band, 0.0, -jnp.inf)
    bias_ref[BLOCK:2 * BLOCK, :] = jnp.where(band & (kj >= BLOCK), 0.0, -jnp.inf)


def _band_bias(bias_ref, b):
    bias = bias_ref[pl.ds(pl.multiple_of(jnp.where(b > 0, 0, BLOCK), BLOCK), BLOCK), :]
    return jnp.concatenate([bias, bias], axis=0)


def _kv_halves(pair):
    zero = jnp.zeros((1, LANES), jnp.int32)
    return zero + (pair >> 1), zero + ((pair + 1) >> 1)


def _stack_heads(t, head0, halves=None):
    top, bottom = jnp.where(head0, t, 0.0), jnp.where(head0, 0.0, t)
    if halves is not None:
        top = jnp.where(halves[0] == 1, pltpu.roll(top, HEAD_DIM, 1), top)
        bottom = jnp.where(halves[1] == 0, pltpu.roll(bottom, HEAD_DIM, 1), bottom)
    return jnp.concatenate([top, bottom], axis=0).astype(BF16)


def _unstack_heads(t, head0, halves=None):
    top, bottom = t[:BLOCK], t[BLOCK:]
    if halves is not None:
        top = jnp.where(halves[0] == 1, pltpu.roll(top, HEAD_DIM, 1), top)
        bottom = jnp.where(halves[1] == 0, pltpu.roll(bottom, HEAD_DIM, 1), bottom)
    return jnp.where(head0, top, bottom)


def _block_loops(s, patterns, unroll, one_block):
    for n, d in enumerate(patterns):
        nb = (s // BLOCK) // d
        ur = min(unroll, d)
        ub = unroll // ur
        for r0 in range(0, d, ur):
            def trip(i, carry, n=n, d=d, r0=r0, ur=ur, ub=ub):
                for u in range(ub):
                    for r in range(r0, r0 + ur):
                        one_block(i * ub + u, d, r, n == 0)
                return carry
            lax.fori_loop(0, nb // ub, trip, 0)


def _attn_fwd(z, m_init, l_init, q_blk, k_blk, v_blk, patterns, max_dist, gqa, name, comm=None):
    s = z.shape[0]
    npair = 3

    def body(q_ref, k_ref, v_ref, mi_ref, o_ref, lse0_ref, lse1_ref, bias_scr, m_scr, l_scr, *kv_scr):
        head0 = lax.broadcasted_iota(jnp.int32, (1, LANES), 1) < HEAD_DIM
        _write_band_bias(bias_scr, max_dist)
        ones = jnp.ones((2 * BLOCK, LANES), BF16)
        k_src, v_src = kv_scr if gqa else (k_ref, v_ref)
        if gqa:
            half = (lax.broadcasted_iota(jnp.int32, (1, LANES), 1) >= HEAD_DIM).astype(jnp.int32)
            swap = ((pl.program_id(0) + half) >> 1) != half

            def expand(c, carry):
                rows = pl.ds(pl.multiple_of(c * ATTN_ROWS, ATTN_ROWS), ATTN_ROWS)
                k_src[rows, :] = jnp.where(swap, pltpu.roll(k_ref[rows, :], HEAD_DIM, 1), k_ref[rows, :])
                v_src[rows, :] = jnp.where(swap, pltpu.roll(v_ref[rows, :], HEAD_DIM, 1), v_ref[rows, :])
                return carry

            lax.fori_loop(0, s // ATTN_ROWS, expand, 0)

        def one_block(b, d, r, first):
            rq, rp = _band_rows(b, d, r)
            q2 = _stack_heads(q_ref[rq, :] * SCALE, head0)
            k2 = jnp.concatenate([k_src[rp, :], k_src[rq, :]], axis=0).astype(BF16)
            v2 = jnp.concatenate([v_src[rp, :], v_src[rq, :]], axis=0).astype(BF16)
            sc = _dot_nt(q2, k2) + _band_bias(bias_scr, b)
            mb = jnp.max(sc, axis=1, keepdims=True)
            p = jnp.exp(sc - mb).astype(BF16)
            ob = _dot_nn(p, jnp.concatenate([v2, ones], axis=1))
            m_blk = _unstack_heads(jnp.broadcast_to(mb, (2 * BLOCK, LANES)), head0)
            l_blk = _unstack_heads(ob[:, LANES:], head0)
            o_blk = _unstack_heads(ob[:, :LANES], head0)
            if first and l_init == 0.0:
                m_new, l_new, o_new = m_blk, l_blk, o_blk
            else:
                if first:
                    m_old, l_old, o_old = jnp.broadcast_to(mi_ref[...], (BLOCK, LANES)), l_init, 0.0
                else:
                    m_old, l_old, o_old = m_scr[rq, :], l_scr[rq, :], o_ref[rq, :]
                m_new = jnp.maximum(m_old, m_blk)
                a_old = jnp.exp(m_old - m_new)
                a_blk = jnp.exp(m_blk - m_new)
                l_new = l_old * a_old + l_blk * a_blk
                o_new = o_old * a_old + o_blk * a_blk
            o_ref[rq, :], l_scr[rq, :], m_scr[rq, :] = o_new, l_new, m_new

        _block_loops(s, patterns, ATTN_UNROLL, one_block)

        def fin(c, carry):
            rows = pl.ds(pl.multiple_of(c * ATTN_ROWS, ATTN_ROWS), ATTN_ROWS)
            l = l_scr[rows, :]
            o_ref[rows, :] = o_ref[rows, :] / l
            lse = m_scr[rows, :] + jnp.log(l)
            swapped = pltpu.roll(lse, HEAD_DIM, 1)
            lse0_ref[rows, :] = jnp.where(head0, lse, swapped)
            lse1_ref[rows, :] = jnp.where(head0, swapped, lse)
            return carry

        lax.fori_loop(0, s // ATTN_ROWS, fin, 0)

    kv = (lambda blk: pl.BlockSpec((s, LANES), lambda j, blk=blk: (0, blk), pipeline_mode=pl.Buffered(1))) if gqa \
        else (lambda blk: pl.BlockSpec((s, LANES), lambda j, blk=blk: (0, blk + j)))
    own = pl.BlockSpec((s, LANES), lambda j: (0, j))
    return _call(
        body,
        grid=(npair,),
        in_specs=[pl.BlockSpec((s, LANES), lambda j: (0, q_blk + j)), kv(k_blk), kv(v_blk),
                  pl.BlockSpec((1, LANES), lambda j: (0, j))],
        out_specs=[own, own, own],
        out_shape=[jax.ShapeDtypeStruct((s, npair * LANES), F32)] * 3,
        operands=(z, z, z, m_init), name=name,
        scratch_shapes=[pltpu.VMEM((2 * BLOCK, 2 * BLOCK), F32)] + [pltpu.VMEM((s, LANES), F32)] * (4 if gqa else 2),
        comm=comm)


def _attn_bwd(z, do, o, lse, m_init, dz, q_blk, k_blk, v_blk, patterns, max_dist, gqa, name, comm=None):
    s = z.shape[0]
    npair = 3
    n_dz_in = 0 if dz is None else 1

    def body(q_ref, k_ref, v_ref, do_ref, o_ref, lse0_ref, lse1_ref, mi_ref, *rest):
        (dz_ref, dm_ref, dq_acc, dk_acc, dv_acc, dl0_scr, dl1_scr, bias_scr,
         dq_out, dk_out, dv_out, out_sems) = rest[n_dz_in:]
        pair = pl.program_id(0)
        head0 = lax.broadcasted_iota(jnp.int32, (1, LANES), 1) < HEAD_DIM
        halves = _kv_halves(pair) if gqa else None
        _write_band_bias(bias_scr, max_dist)

        def zero_kv():
            def f(c, carry):
                rows = pl.ds(pl.multiple_of(c * ATTN_ROWS, ATTN_ROWS), ATTN_ROWS)
                dk_acc[rows, :] = jnp.zeros((ATTN_ROWS, LANES), F32)
                dv_acc[rows, :] = jnp.zeros((ATTN_ROWS, LANES), F32)
                return carry
            lax.fori_loop(0, s // ATTN_ROWS, f, 0)

        if gqa:
            pl.when(pair == 0)(zero_kv)
        else:
            zero_kv()

        def prep(c, dm):
            rows = pl.ds(pl.multiple_of(c * ATTN_ROWS, ATTN_ROWS), ATTN_ROWS)
            dq_acc[rows, :] = jnp.zeros((ATTN_ROWS, LANES), F32)
            prod = do_ref[rows, :] * o_ref[rows, :]
            d0 = jnp.sum(jnp.where(head0, prod, 0.0), axis=1, keepdims=True)
            d1 = jnp.sum(jnp.where(head0, 0.0, prod), axis=1, keepdims=True)
            dl0_scr[rows, :] = jnp.broadcast_to(d0, (ATTN_ROWS, LANES))
            dl1_scr[rows, :] = jnp.broadcast_to(d1, (ATTN_ROWS, LANES))
            lse_own = jnp.where(head0, lse0_ref[rows, :], lse1_ref[rows, :])
            psink = jnp.exp(mi_ref[...] - lse_own)
            return dm - jnp.sum(psink * jnp.where(head0, d0, d1), axis=0, keepdims=True)

        dm_ref[...] = lax.fori_loop(0, s // ATTN_ROWS, prep, jnp.zeros((1, LANES), F32))

        def one_block(b, d, r, first):
            rq, rp = _band_rows(b, d, r)
            q2 = _stack_heads(q_ref[rq, :] * SCALE, head0, halves)
            do2 = _stack_heads(do_ref[rq, :], head0, halves)
            k2 = jnp.concatenate([k_ref[rp, :], k_ref[rq, :]], axis=0).astype(BF16)
            v2 = jnp.concatenate([v_ref[rp, :], v_ref[rq, :]], axis=0).astype(BF16)
            lse2 = jnp.concatenate([lse0_ref[rq, :], lse1_ref[rq, :]], axis=0)
            dl2 = jnp.concatenate([dl0_scr[rq, :], dl1_scr[rq, :]], axis=0)
            lse2 = jnp.concatenate([lse2, lse2], axis=1)
            dl2 = jnp.concatenate([dl2, dl2], axis=1)
            p = jnp.exp(_dot_nt(q2, k2) + _band_bias(bias_scr, b) - lse2)
            dp = _dot_nt(do2, v2)
            dsc = (p * (dp - dl2)).astype(BF16)
            dq2 = _unstack_heads(_dot_nn(dsc, k2), head0, halves)
            dk2 = _dot_tn(dsc, q2)
            dv2 = _dot_tn(p.astype(BF16), do2)
            dq_acc[rq, :] += dq2 * SCALE
            dk_acc[rp, :] += dk2[:BLOCK]
            dk_acc[rq, :] += dk2[BLOCK:]
            dv_acc[rp, :] += dv2[:BLOCK]
            dv_acc[rq, :] += dv2[BLOCK:]

        _block_loops(s, patterns, ATTN_UNROLL, one_block)

        def to_dz(staged, blk, k):
            cols = pl.ds(pl.multiple_of(blk * LANES, LANES), LANES)
            return pltpu.make_async_copy(staged, dz_ref.at[:, cols], out_sems.at[k])

        last_pair = pair == npair - 1
        q_copy = to_dz(dq_out, q_blk + pair, 0)
        kv_copies = [to_dz(dk_out, k_blk + (0 if gqa else pair), 1), to_dz(dv_out, v_blk + (0 if gqa else pair), 2)]

        @pl.when(pair > 0)
        def _():
            for cp in [q_copy] + ([] if gqa else kv_copies):
                cp.wait()

        def stage(acc, out):
            def f(c, carry):
                rows = pl.ds(pl.multiple_of(c * ATTN_ROWS, ATTN_ROWS), ATTN_ROWS)
                out[rows, :] = acc[rows, :].astype(BF16)
                return carry
            lax.fori_loop(0, s // ATTN_ROWS, f, 0)

        def stage_kv():
            stage(dk_acc, dk_out)
            stage(dv_acc, dv_out)
            for cp in kv_copies:
                cp.start()

        stage(dq_acc, dq_out)
        q_copy.start()
        if gqa:
            pl.when(last_pair)(stage_kv)
        else:
            stage_kv()

        @pl.when(last_pair)
        def _():
            for cp in [q_copy] + kv_copies:
                cp.wait()

    own = pl.BlockSpec((s, LANES), lambda j: (0, j))
    hbm = pl.BlockSpec(memory_space=pl.ANY)
    if gqa:
        kv = lambda blk: pl.BlockSpec((s, LANES), lambda j, blk=blk: (0, blk), pipeline_mode=pl.Buffered(1))
    else:
        kv = lambda blk: pl.BlockSpec((s, LANES), lambda j, blk=blk: (0, blk + j))
    in_specs = [pl.BlockSpec((s, LANES), lambda j: (0, q_blk + j)), kv(k_blk), kv(v_blk), own, own, own, own,
                pl.BlockSpec((1, LANES), lambda j: (0, j))]
    operands = (z, z, z, do, o, lse[0], lse[1], m_init)
    return _call(
        body,
        grid=(npair,),
        in_specs=in_specs + [hbm] * n_dz_in,
        out_specs=[hbm, pl.BlockSpec((1, LANES), lambda j: (0, j))],
        out_shape=[jax.ShapeDtypeStruct((s, IN_WIDTH), BF16), jax.ShapeDtypeStruct((1, npair * LANES), F32)],
        operands=operands + (() if dz is None else (dz,)), name=name,
        scratch_shapes=[pltpu.VMEM((s, LANES), F32)] * 5 + [pltpu.VMEM((2 * BLOCK, 2 * BLOCK), F32)]
        + [pltpu.VMEM((s, LANES), BF16)] * 3 + [pltpu.SemaphoreType.DMA((3,))],
        comm=comm, aliases={} if dz is None else {len(in_specs): 0})


def _adamw_math(w, g, m, v):
    m = ADAM_B1 * m + (1.0 - ADAM_B1) * g
    v = ADAM_B2 * v + (1.0 - ADAM_B2) * (g * g)
    m_hat = m / (1.0 - ADAM_B1 ** ADAM_STEP)
    v_hat = v / (1.0 - ADAM_B2 ** ADAM_STEP)
    delta = -ADAM_LR * (m_hat / (jnp.sqrt(v_hat) + ADAM_EPS) + ADAM_WD * w)
    return delta, m, v


def _adamw(w, g, m, v, name):
    rows, cols = w.shape
    tr = min(rows, 256)

    def body(w_ref, g_ref, m_ref, v_ref, d_ref, nm_ref, nv_ref):
        d_ref[...], nm_ref[...], nv_ref[...] = _adamw_math(w_ref[...], g_ref[...], m_ref[...], v_ref[...])

    spec = pl.BlockSpec((tr, cols), lambda i: (i, 0))
    return pl.pallas_call(
        body,
        grid=(rows // tr,),
        in_specs=[spec] * 4,
        out_specs=[spec] * 3,
        out_shape=[jax.ShapeDtypeStruct((rows, cols), F32)] * 3,
        compiler_params=_params("parallel"),
        name=name,
    )(w, g, m, v)


def _sum_adamw(parts, w, m, v, pos, transpose, name):
    assert len(parts) == DEPTH == 2
    (p0, r0), (p1, r1) = parts
    _, rows, cols = p0.shape
    tr = 256 if rows % 256 == 0 else rows
    nt = rows // tr

    def body(pos_ref, p0_ref, r0_ref, p1_ref, r1_ref, w_ref, m_ref, v_ref, g_ref, d_ref, nm_ref, nv_ref):
        def run(p_ref, r_ref):
            g = ((p_ref[...].astype(F32) + r_ref[0].astype(F32)) + r_ref[1].astype(F32)) + r_ref[2].astype(F32)
            if transpose:
                g = g.T
            g_ref[...] = g
            d_ref[...], nm_ref[...], nv_ref[...] = _adamw_math(w_ref[...], g, m_ref[...], v_ref[...])

        layer0 = pl.program_id(0) < nt
        pl.when(layer0)(lambda: run(p0_ref, r0_ref))
        pl.when(jnp.logical_not(layer0))(lambda: run(p1_ref, r1_ref))

    def tile0(i):
        return jnp.minimum(i, nt - 1)

    def tile1(i):
        return jnp.maximum(i - nt, 0)

    if transpose:
        w_spec = pl.BlockSpec((None, cols, tr), lambda i, q: (i // nt, 0, i % nt))
    else:
        w_spec = pl.BlockSpec((None, tr, cols), lambda i, q: (i // nt, i % nt, 0))
    return pl.pallas_call(
        body,
        grid_spec=pltpu.PrefetchScalarGridSpec(
            num_scalar_prefetch=1,
            grid=(DEPTH * nt,),
            in_specs=[pl.BlockSpec((None, tr, cols), lambda i, q: (q[0], tile0(i), 0)),
                      pl.BlockSpec((3, tr, cols), lambda i, q: (0, tile0(i), 0)),
                      pl.BlockSpec((None, tr, cols), lambda i, q: (q[0], tile1(i), 0)),
                      pl.BlockSpec((3, tr, cols), lambda i, q: (0, tile1(i), 0)),
                      w_spec, w_spec, w_spec],
            out_specs=[w_spec] * 4,
        ),
        out_shape=[jax.ShapeDtypeStruct(w.shape, F32)] * 4,
        compiler_params=_params("arbitrary"),
        name=name,
    )(pos, p0, r0, p1, r1, w, m, v)


def _small_sum_adamw(gathered, params, name):
    _, rows, cols = gathered.shape
    n = len(params)

    def body(ga_ref, *refs):
        ins, outs, (g_scr,) = refs[:3 * n], refs[3 * n:7 * n + 2], refs[7 * n + 2:]
        g = ga_ref[0]
        for i in range(1, N_DEV):
            g = g + ga_ref[i]
        g_scr[...] = g
        for k, (row0, w, _, _) in enumerate(params):
            w_ref, m_ref, v_ref = ins[3 * k:3 * k + 3]
            gk = g_scr[row0:row0 + w.shape[0], :]
            outs[4 * k][...] = gk
            outs[4 * k + 1][...], outs[4 * k + 2][...], outs[4 * k + 3][...] = _adamw_math(
                w_ref[...], gk, m_ref[...], v_ref[...])
        outs[4 * n][...] = g_scr[CONV_ROW:CONV_ROW + 8, :]
        outs[4 * n + 1][...] = g_scr[LOSS_ROW:LOSS_ROW + 1, :]

    out_shape = []
    for _, w, _, _ in params:
        out_shape += [jax.ShapeDtypeStruct(w.shape, F32)] * 4
    out_shape += [jax.ShapeDtypeStruct((8, cols), F32), jax.ShapeDtypeStruct((1, cols), F32)]
    res = pl.pallas_call(
        body,
        out_shape=out_shape,
        scratch_shapes=[pltpu.VMEM((rows, cols), F32)],
        name=name,
    )(gathered, *[t for _, w, m, v in params for t in (w, m, v)])
    return [res[4 * k:4 * k + 4] for k in range(n)], res[4 * n], res[4 * n + 1]


def _pair_sum(g4, r1, pos, name):
    _, _, rows, cols = g4.shape
    tr = min(rows, 512)

    def body(pos_ref, g_ref, r_ref, o_ref):
        o_ref[...] = (g_ref[...].astype(F32) + r_ref[...].astype(F32)).astype(BF16)

    return pl.pallas_call(
        body,
        grid_spec=pltpu.PrefetchScalarGridSpec(
            num_scalar_prefetch=1,
            grid=(4, rows // tr),
            in_specs=[pl.BlockSpec((None, None, tr, cols), lambda i, j, p: (i, p[1], j, 0)),
                      pl.BlockSpec((None, tr, cols), lambda i, j, p: (i, j, 0))],
            out_specs=pl.BlockSpec((None, tr, cols), lambda i, j, p: (i, j, 0)),
        ),
        out_shape=jax.ShapeDtypeStruct((4, rows, cols), BF16),
        compiler_params=_params("parallel", "parallel"),
        name=name,
    )(pos, g4, r1)


def _place():
    return lax.axis_index("x"), lax.axis_index("y"), lax.axis_index("c")


def _gather_comm(shards):
    na = len(shards)

    def plan(ins, outs, sems):
        send_sems, recv_sems, local_sems = sems
        x, y, c = _place()
        me, sibling = (x, y, c), (x, y, 1 - c)
        chips = [(1 - x, y), (x, 1 - y), (1 - x, 1 - y)]

        def rows(a, px, py, pc):
            m = ins[a].shape[0]
            return outs[a].at[pl.ds((4 * px + 2 * py + pc) * m, m), :]

        def copy(a, k, block, to, src=None):
            return pltpu.make_async_remote_copy(
                src_ref=rows(a, *block) if src is None else src, dst_ref=rows(a, *block),
                send_sem=send_sems.at[a, k], recv_sem=recv_sems.at[a, k], device_id=to, device_id_type=MESH)

        mine = [pltpu.make_async_copy(ins[a], rows(a, *me), local_sems.at[a]) for a in range(na)]
        first = []
        for a in range(na):
            first.append(copy(a, 0, me, sibling, src=ins[a]))
            first += [copy(a, 1 + j, me, (*chip, c), src=ins[a]) for j, chip in enumerate(chips)]
        return me, sibling, chips, c, copy, mine, first

    def start(ins, outs, sems):
        *_, mine, first = plan(ins, outs, sems)
        for cp in mine + first:
            cp.start()

    def finish(ins, outs, sems):
        me, sibling, chips, c, copy, mine, first = plan(ins, outs, sems)
        passed = []
        for j, chip in enumerate(chips):
            for a in range(na):
                copy(a, 1 + j, (*chip, c), me).wait_recv()
                cp = copy(a, 4 + j, (*chip, c), sibling)
                cp.start()
                passed.append(cp)
        for a in range(na):
            copy(a, 0, sibling, me).wait_recv()
            for j, chip in enumerate(chips):
                copy(a, 4 + j, (*chip, 1 - c), me).wait_recv()
        for cp in first + passed:
            cp.wait_send()
        for cp in mine:
            cp.wait()

    return _Comm(tuple(shards),
                 tuple(jax.ShapeDtypeStruct((N_DEV * t.shape[0], t.shape[1]), t.dtype) for t in shards),
                 (pltpu.SemaphoreType.DMA((na, 7)), pltpu.SemaphoreType.DMA((na, 7)), pltpu.SemaphoreType.DMA((na,))),
                 start, finish)


def _exchange_comm(arrays, out_shape, n_copies, copies_of):
    na = len(arrays)

    def every(ins, outs, sems):
        send_sems, recv_sems = sems
        return [cp for a in range(na) for cp in copies_of(ins, outs, a, send_sems, recv_sems)]

    def start(ins, outs, sems):
        for cp in every(ins, outs, sems):
            cp.start()

    def finish(ins, outs, sems):
        for cp in every(ins, outs, sems):
            cp.wait()

    return _Comm(tuple(arrays), tuple(out_shape),
                 (pltpu.SemaphoreType.DMA((na, n_copies)), pltpu.SemaphoreType.DMA((na, n_copies))), start, finish)


def _sibling_comm(grads):
    def copies_of(ins, outs, a, send_sems, recv_sems):
        x, y, c = _place()
        return [pltpu.make_async_remote_copy(
            src_ref=ins[a].at[chip, 1 - c], dst_ref=outs[a].at[chip],
            send_sem=send_sems.at[a, chip], recv_sem=recv_sems.at[a, chip],
            device_id=(x, y, 1 - c), device_id_type=MESH) for chip in range(4)]

    return _exchange_comm(grads, [jax.ShapeDtypeStruct((4,) + t.shape[2:], t.dtype) for t in grads], 4, copies_of)


def _chip_comm(partials):
    def copies_of(ins, outs, a, send_sems, recv_sems):
        x, y, c = _place()
        chips = [(1 - x, y), (x, 1 - y), (1 - x, 1 - y)]
        return [pltpu.make_async_remote_copy(
            src_ref=ins[a].at[2 * cx + cy], dst_ref=outs[a].at[k],
            send_sem=send_sems.at[a, k], recv_sem=recv_sems.at[a, k],
            device_id=(cx, cy, c), device_id_type=MESH) for k, (cx, cy) in enumerate(chips)]

    return _exchange_comm(partials, [jax.ShapeDtypeStruct((3,) + t.shape[1:], t.dtype) for t in partials], 3, copies_of)


def _pad_rows(t, rows):
    return jnp.pad(t, ((0, rows - t.shape[0]), (0, D_MODEL - t.shape[1])))


MIX_ROW, GROUP_ROW, MLP_ROW, FINAL_ROW, CONV_ROW, SINK_ROW = 0, 8, 16, 24, 32, 40
LOSS_ROW = FINAL_ROW + 1


def _pack_small(g_mix, g_group, g_mlp, g_final, conv, sinks, loss):
    final_and_loss = jnp.concatenate([g_final.reshape(1, D_MODEL), _pad_rows(loss, 1)], axis=0)
    return jnp.concatenate([
        _pad_rows(g_mix, 8), _pad_rows(g_group, 8), _pad_rows(g_mlp, 8), _pad_rows(final_and_loss, 8),
        _pad_rows(conv.reshape(DEPTH * 3, CONV_CH), 8), _pad_rows(sinks.reshape(1, DEPTH * 6), 8)], axis=0)


def kernel(x, w_in, conv_w, sinks, g_mix, g_group, w_o, g_mlp, w_ff_in, w_ff_out, g_final, loss_target, m_w_in, m_conv_w, m_sinks, m_g_mix, m_g_group, m_w_o, m_g_mlp, m_w_ff_in, m_w_ff_out, m_g_final, v_w_in, v_conv_w, v_sinks, v_g_mix, v_g_group, v_w_o, v_g_mlp, v_w_ff_in, v_w_ff_out, v_g_final):
    ax, ay, ac = _place()
    chip = 2 * ax + ay
    dev = 4 * ax + 2 * ay + ac
    pos = jnp.stack([chip, ac]).astype(jnp.int32)

    x0 = x.reshape(SEQ, D_MODEL)
    target = loss_target.reshape(SEQ, D_MODEL)

    shards = {}
    for l in range(DEPTH):
        shards[l, 0], shards[l, 1] = w_in[l].T.astype(BF16), w_o[l].astype(BF16)
        shards[l, 2], shards[l, 3] = w_ff_in[l].T.astype(BF16), w_ff_out[l].astype(BF16)
    conv_tile = jnp.pad(conv_w.reshape(DEPTH * 3, CONV_CH // N_DEV), ((0, 2), (0, LANES - CONV_CH // N_DEV)))
    wt_in0, conv_all = _comm_only(_gather_comm([shards[0, 0], conv_tile]), "gather_first")
    conv_full = conv_all.reshape(N_DEV, 8, LANES)[:, :DEPTH * 3, :CONV_CH // N_DEV]
    conv_full = conv_full.transpose(1, 0, 2).reshape(DEPTH, 3, CONV_CH)

    dx, parts, small = _step(x0, target, shards, wt_in0, conv_full, sinks, g_mix, g_group, g_mlp, g_final, pos)
    return _finish(dx, parts, small, pos, dev, w_in, conv_w, sinks, g_mix, g_group, w_o, g_mlp, w_ff_in, w_ff_out, g_final, m_w_in, m_conv_w, m_sinks, m_g_mix, m_g_group, m_w_o, m_g_mlp, m_w_ff_in, m_w_ff_out, m_g_final, v_w_in, v_conv_w, v_sinks, v_g_mix, v_g_group, v_w_o, v_g_mlp, v_w_ff_in, v_w_ff_out, v_g_final)


FWD_CARRY = {(0, "in_proj"): ((1, 0),), (0, "window"): ((0, 1),), (0, "dilated"): ((0, 2),),
             (0, "mix_ff_in"): ((0, 3),), (0, "ff_out_in_proj"): ((1, 3),),
             (1, "window"): ((1, 1),), (1, "dilated"): ((1, 2),)}


def _step(x0, target, shards, wt_in0, conv_full, sinks, g_mix, g_group, g_mlp, g_final, pos):
    sink_lanes = jnp.repeat(sinks.reshape(DEPTH, 6), HEAD_DIM, axis=1)
    no_sink = jnp.full((1, A_WIDTH), NEG_BIG, F32)
    full = {(0, 0): wt_in0}

    def gather(stage, l):
        keys = FWD_CARRY.get((l, stage), ())
        return keys, (_gather_comm([shards[k] for k in keys]) if keys else None)

    def landed(keys, got):
        full.update(zip(keys, got))

    saved = []
    xc = x0
    keys, comm = gather("in_proj", 0)
    (z, h), got = _norm_mm(xc, g_mix[0:1], full[0, 0], "in_proj_0", comm)
    landed(keys, got)
    for l in range(DEPTH):
        sink_l = sink_lanes[l:l + 1]
        keys, comm = gather("window", l)
        (yc, *lse_c), got = _attn_fwd(z, sink_l, 1.0, QC_BLK, KC_BLK, VC_BLK, (1,), C_MAX_DIST, True,
                                     f"window_attn_{l}", comm)
        landed(keys, got)
        yb = _conv_fwd(z, conv_full[l], f"conv_{l}")
        keys, comm = gather("dilated", l)
        (ya, *lse_a), got = _attn_fwd(z, no_sink, 0.0, QA_BLK, KA_BLK, VA_BLK, DILATED_PATTERNS, A_MAX_DIST, False,
                                     f"dilated_attn_{l}", comm)
        landed(keys, got)
        keys, comm = gather("mix_ff_in", l)
        (y, x1, a, h2), got = _mix_ff_in(ya, yb, yc, g_group[l:l + 1], full[l, 1], xc, g_mlp[l:l + 1], full[l, 2],
                                         f"mix_ff_in_{l}", comm)
        landed(keys, got)
        saved.append((xc, z, h, ya, lse_a, yb, yc, lse_c, sink_l, y, x1, a, h2))
        if l + 1 < DEPTH:
            keys, comm = gather("ff_out_in_proj", l)
            (xc, z, h), got = _ff_out_in_proj(a, full[l, 3], x1, g_mix[l + 1:l + 2], full[l + 1, 0],
                                              f"ff_out_{l}_in_proj_{l + 1}", comm)
            landed(keys, got)

    loss_slab, dx, dxb, dg_final, du = _mm_res_loss(a, full[DEPTH - 1, 3], x1, g_final.reshape(1, D_MODEL), target,
                                                    f"ff_out_{DEPTH - 1}_loss")

    def by_owner(t):
        return t.reshape(4, 2, t.shape[0] // N_DEV, D_MODEL)

    def pair(key, g, r1):
        return _pair_sum(g, r1, pos, f"grad_pair_sum_{key[0]}_{key[1]}")

    partial, r2 = {}, {}
    dg_mix, dg_group, dg_mlp, dconv, dsinks = [None] * DEPTH, [None] * DEPTH, [None] * DEPTH, [None] * DEPTH, [None] * DEPTH
    for l in reversed(range(DEPTH)):
        xin, z, h, ya, lse_a, yb, yc, lse_c, sink_l, y, x1, a, h2 = saved[l]
        if l + 1 < DEPTH:
            late = [(l + 1, 1), (l + 1, 0)]
            (du,), got = _mlp_bwd_act(dxb, full[l, 3], a, f"ff_out_bwd_{l}", _chip_comm([partial[k] for k in late]))
            r2.update(zip(late, got))
        (g3, g2), _ = _mm_tn([(a, dxb), (du, h2)], f"grad_w_ff_{l}")
        g3, g2 = by_owner(g3), by_owner(g2)
        (dx1, dx1b, dg_mlp[l], dya, dyb, dyc, dg_group[l]), got = _ff_in_mix_bwd(
            du, full[l, 2], x1, dx, g_mlp[l:l + 1], full[l, 1], ya, yb, yc, g_group[l:l + 1],
            f"ff_in_mix_bwd_{l}", _sibling_comm([g3, g2]))
        partial[l, 3], partial[l, 2] = pair((l, 3), g3, got[0]), pair((l, 2), g2, got[1])
        early = [(l, 3), (l, 2)]
        (dz, _), got = _attn_bwd(z, dya, ya, lse_a, no_sink, None, QA_BLK, KA_BLK, VA_BLK, DILATED_PATTERNS,
                                 A_MAX_DIST, False, f"dilated_attn_bwd_{l}", _chip_comm([partial[k] for k in early]))
        r2.update(zip(early, got))
        dz, dcw = _conv_bwd(z, conv_full[l], dyb, dz, f"conv_bwd_{l}")
        (dz, dsink), _ = _attn_bwd(z, dyc, yc, lse_c, sink_l, dz, QC_BLK, KC_BLK, VC_BLK, (1,), C_MAX_DIST,
                                   True, f"window_attn_bwd_{l}")
        (g1, g0), _ = _mm_tn([(y, dx1b), (dz, h)], f"grad_w_o_in_{l}")
        g1, g0 = by_owner(g1), by_owner(g0)
        if l > 0:
            (dx, dxb, dg_mix[l]), got = _mm_nn_normbwd(dz, full[l, 0], xin, dx1, g_mix[l:l + 1], f"in_proj_bwd_{l}",
                                                      _sibling_comm([g1, g0]))
            partial[l, 1], partial[l, 0] = pair((l, 1), g1, got[0]), pair((l, 0), g0, got[1])
        else:
            got = _comm_only(_sibling_comm([g1, g0]), "grad_sibling_exchange_last")
            partial[l, 1], partial[l, 0] = pair((l, 1), g1, got[0]), pair((l, 0), g0, got[1])
            (dx, dxb, dg_mix[l]), got = _mm_nn_normbwd(dz, full[l, 0], xin, dx1, g_mix[l:l + 1], f"in_proj_bwd_{l}",
                                                      _chip_comm([partial[l, 1], partial[l, 0]]))
            r2[l, 1], r2[l, 0] = got
        dconv[l] = dcw[:3]
        dsinks[l] = dsink[0, ::HEAD_DIM]
    parts = {key: (partial[key], r2[key]) for key in partial}
    small = _pack_small(jnp.concatenate(dg_mix), jnp.concatenate(dg_group), jnp.concatenate(dg_mlp),
                        dg_final, jnp.stack(dconv), jnp.stack(dsinks), loss_slab[0:1])
    return dx, parts, small


def _finish(dx, parts, small, pos, dev, w_in, conv_w, sinks, g_mix, g_group, w_o, g_mlp, w_ff_in, w_ff_out, g_final, m_w_in, m_conv_w, m_sinks, m_g_mix, m_g_group, m_w_o, m_g_mlp, m_w_ff_in, m_w_ff_out, m_g_final, v_w_in, v_conv_w, v_sinks, v_g_mix, v_g_group, v_w_o, v_g_mlp, v_w_ff_in, v_w_ff_out, v_g_final):
    grad_x = dx.reshape(1, SEQ, D_MODEL)

    (small_all,) = _comm_only(_gather_comm([small]), "gather_small_grads")
    row = lambda t: t.reshape(1, D_MODEL)
    sink_row = lambda t: _pad_rows(t.reshape(1, DEPTH * 6), 1)
    params = [(MIX_ROW, g_mix, m_g_mix, v_g_mix), (GROUP_ROW, g_group, m_g_group, v_g_group),
              (MLP_ROW, g_mlp, m_g_mlp, v_g_mlp), (FINAL_ROW, row(g_final), row(m_g_final), row(v_g_final)),
              (SINK_ROW, sink_row(sinks), sink_row(m_sinks), sink_row(v_sinks))]
    updated, conv_rows, loss_row = _small_sum_adamw(small_all.reshape(N_DEV, SMALL_ROWS, D_MODEL), params, "small_adamw")
    loss = loss_row[0, 0]
    (grad_g_mix, delta_g_mix, new_m_g_mix, new_v_g_mix), (grad_g_group, delta_g_group, new_m_g_group, new_v_g_group), \
        (grad_g_mlp, delta_g_mlp, new_m_g_mlp, new_v_g_mlp), final4, sinks4 = updated
    grad_g_final, delta_g_final, new_m_g_final, new_v_g_final = [t.reshape(D_MODEL) for t in final4]
    grad_sinks, delta_sinks, new_m_sinks, new_v_sinks = [t[0, :DEPTH * 6].reshape(DEPTH, 2, 3) for t in sinks4]
    conv_grad_full = conv_rows[:DEPTH * 3, :CONV_CH].reshape(DEPTH, 3, CONV_CH)
    cs = CONV_CH // N_DEV
    grad_conv_w = lax.dynamic_slice_in_dim(conv_grad_full, dev * cs, cs, axis=2)

    def tile_of(t):
        return jnp.pad(t.reshape(1, DEPTH * 3 * cs), ((0, 7), (0, 256 - DEPTH * 3 * cs)))

    cd, cm, cv = _adamw(tile_of(conv_w), tile_of(grad_conv_w), tile_of(m_conv_w), tile_of(v_conv_w), "conv_adamw")
    untile = lambda t: t[0, :DEPTH * 3 * cs].reshape(DEPTH, 3, cs)
    delta_conv_w, new_m_conv_w, new_v_conv_w = untile(cd), untile(cm), untile(cv)

    def big(kind, w, m, v, transpose, name):
        return _sum_adamw([parts[l, kind] for l in range(DEPTH)], w, m, v, pos, transpose, name)

    grad_w_in, delta_w_in, new_m_w_in, new_v_w_in = big(0, w_in, m_w_in, v_w_in, True, "adamw_w_in")
    grad_w_o, delta_w_o, new_m_w_o, new_v_w_o = big(1, w_o, m_w_o, v_w_o, False, "adamw_w_o")
    grad_w_ff_in, delta_w_ff_in, new_m_w_ff_in, new_v_w_ff_in = big(2, w_ff_in, m_w_ff_in, v_w_ff_in, True, "adamw_w_ff_in")
    grad_w_ff_out, delta_w_ff_out, new_m_w_ff_out, new_v_w_ff_out = big(3, w_ff_out, m_w_ff_out, v_w_ff_out, False,
                                                                         "adamw_w_ff_out")

    return (loss, grad_x, grad_w_in, grad_conv_w, grad_sinks, grad_g_mix, grad_g_group, grad_w_o, grad_g_mlp,
            grad_w_ff_in, grad_w_ff_out, grad_g_final,
            delta_w_in, delta_conv_w, delta_sinks, delta_g_mix, delta_g_group, delta_w_o, delta_g_mlp,
            delta_w_ff_in, delta_w_ff_out, delta_g_final,
            new_m_w_in, new_m_conv_w, new_m_sinks, new_m_g_mix, new_m_g_group, new_m_w_o, new_m_g_mlp,
            new_m_w_ff_in, new_m_w_ff_out, new_m_g_final,
            new_v_w_in, new_v_conv_w, new_v_sinks, new_v_g_mix, new_v_g_group, new_v_w_o, new_v_g_mlp,
            new_v_w_ff_in, new_v_w_ff_out, new_v_g_final)
```

```python
from typing import Callable, NamedTuple

import jax
import jax.numpy as jnp
from jax import lax
from jax.experimental import pallas as pl
from jax.experimental.pallas import tpu as pltpu

F32 = jnp.float32
BF16 = jnp.bfloat16
MESH = pl.DeviceIdType.MESH

N_DEV = 8
SEQ = 4096
D_MODEL = 1024
DEPTH = 2
HEAD_DIM = 64
LANES = 128
A_WIDTH = 384
CONV_CH = 256
C_WIDTH = 384
IN_WIDTH = 2560
BLOCK = 128
DILATED_PATTERNS = (1, 4, 16)
A_MAX_DIST = 128
C_MAX_DIST = 127
EPS = 1e-6
SCALE = HEAD_DIM ** -0.5
LOG2E = 1.4426950408889634
LN2 = 0.6931471805599453
NEG_BIG = -1e30
F32_TINY = 1.1754944e-38

QA_BLK, KA_BLK, VA_BLK = 0, 3, 6
GB_BLK, GC_BLK, XB_BLK = 9, 11, 13
QC_BLK, KC_BLK, VC_BLK = 15, 18, 19

ADAM_LR = 0.001
ADAM_B1 = 0.9
ADAM_B2 = 0.999
ADAM_EPS = 1e-08
ADAM_WD = 0.01
ADAM_STEP = 10

VMEM_LIMIT = 56 * 1024 * 1024
TILE_BUDGET = 46 * 1024 * 1024
ROW_TILE = 512
COL_CHUNK = 512
SMALL_ROWS = 48


def _dot_nn(a, b):
    return lax.dot_general(a, b, (((1,), (0,)), ((), ())), preferred_element_type=F32)


def _dot_nt(a, b):
    return lax.dot_general(a, b, (((1,), (1,)), ((), ())), preferred_element_type=F32)


def _dot_tn(a, b):
    return lax.dot_general(a, b, (((0,), (0,)), ((), ())), preferred_element_type=F32)


def _params(*sem):
    return pltpu.CompilerParams(dimension_semantics=sem, vmem_limit_bytes=VMEM_LIMIT)


def _resident(shape):
    return pl.BlockSpec(shape, lambda i: (0,) * len(shape), pipeline_mode=pl.Buffered(1))


def _row_tile(row_bytes, resident_bytes):
    for tm in (ROW_TILE, ROW_TILE // 2):
        if 2 * tm * row_bytes + resident_bytes <= TILE_BUDGET:
            return tm
    return ROW_TILE // 4


def _rms_scale(t):
    return lax.rsqrt(jnp.mean(t * t, axis=-1, keepdims=True) + EPS)


def _rms_bwd(n, r, dn):
    return r * (dn - n * jnp.mean(dn * n, axis=-1, keepdims=True))


class _Comm(NamedTuple):
    arrays: tuple
    out_shape: tuple
    sems: tuple
    start: Callable
    finish: Callable


def _call(body, grid, in_specs, out_specs, out_shape, operands, name, scratch_shapes=(), comm=None, aliases=None):
    n_in, n_out, n_scr = len(in_specs), len(out_shape), len(scratch_shapes)
    aliases = dict(aliases or {})
    if comm is None:
        res = pl.pallas_call(body, grid=grid, in_specs=list(in_specs), out_specs=list(out_specs),
                             out_shape=list(out_shape), scratch_shapes=list(scratch_shapes),
                             input_output_aliases=aliases,
                             compiler_params=_params("arbitrary"), name=name)(*operands)
        return list(res), []
    c_in, c_out = len(comm.arrays), len(comm.out_shape)
    hbm = pl.BlockSpec(memory_space=pl.ANY)
    last = grid[0] - 1

    def carried(*refs):
        ins, cins = refs[:n_in], refs[n_in:n_in + c_in]
        o0 = n_in + c_in
        outs, couts = refs[o0:o0 + n_out], refs[o0 + n_out:o0 + n_out + c_out]
        s0 = o0 + n_out + c_out
        scr, sems = refs[s0:s0 + n_scr], refs[s0 + n_scr:]
        pl.when(pl.program_id(0) == 0)(lambda: comm.start(cins, couts, sems))
        body(*ins, *outs, *scr)
        pl.when(pl.program_id(0) == last)(lambda: comm.finish(cins, couts, sems))

    res = pl.pallas_call(carried, grid=grid, in_specs=list(in_specs) + [hbm] * c_in,
                         out_specs=list(out_specs) + [hbm] * c_out, out_shape=list(out_shape) + list(comm.out_shape),
                         scratch_shapes=list(scratch_shapes) + list(comm.sems), input_output_aliases=aliases,
                         compiler_params=_params("arbitrary"), name=name)(*operands, *comm.arrays)
    return list(res[:n_out]), list(res[n_out:])


def _comm_only(comm, name):
    hbm = pl.BlockSpec(memory_space=pl.ANY)
    c_in, c_out = len(comm.arrays), len(comm.out_shape)

    def body(*refs):
        ins, outs, sems = refs[:c_in], refs[c_in:c_in + c_out], refs[c_in + c_out:]
        comm.start(ins, outs, sems)
        comm.finish(ins, outs, sems)

    return pl.pallas_call(body, in_specs=[hbm] * c_in, out_specs=[hbm] * c_out, out_shape=list(comm.out_shape),
                          scratch_shapes=list(comm.sems), name=name)(*comm.arrays)


def _norm_mm(x, g, wt, name, comm=None):
    s, d = x.shape
    n = wt.shape[0]
    tm = _row_tile(4 * d + 4 * n + 2 * d, 2 * n * d)

    def body(x_ref, g_ref, w_ref, o_ref, h_ref):
        xx = x_ref[...]
        h = ((xx * _rms_scale(xx)) * g_ref[...]).astype(BF16)
        h_ref[...] = h
        for n0 in range(0, n, COL_CHUNK):
            o_ref[:, n0:n0 + COL_CHUNK] = _dot_nt(h, w_ref[n0:n0 + COL_CHUNK, :])

    return _call(
        body,
        grid=(s // tm,),
        in_specs=[pl.BlockSpec((tm, d), lambda i: (i, 0)),
                  pl.BlockSpec((1, d), lambda i: (0, 0)),
                  _resident((n, d))],
        out_specs=[pl.BlockSpec((tm, n), lambda i: (i, 0)),
                   pl.BlockSpec((tm, d), lambda i: (i, 0))],
        out_shape=[jax.ShapeDtypeStruct((s, n), F32), jax.ShapeDtypeStruct((s, d), BF16)],
        operands=(x, g, wt), name=name, comm=comm)


def _ff_out_in_proj(a, w2, x1, g, wt, name, comm=None):
    s, f = a.shape
    d = w2.shape[1]
    n = wt.shape[0]
    tm = _row_tile(2 * f + 4 * d + 4 * d + 4 * n + 2 * d, 2 * f * d + 2 * n * d)

    def body(a_ref, w2_ref, x_ref, g_ref, w_ref, x2_ref, z_ref, h_ref):
        x2 = x_ref[...] + _dot_nn(a_ref[...], w2_ref[...])
        x2_ref[...] = x2
        h = ((x2 * _rms_scale(x2)) * g_ref[...]).astype(BF16)
        h_ref[...] = h
        for n0 in range(0, n, COL_CHUNK):
            z_ref[:, n0:n0 + COL_CHUNK] = _dot_nt(h, w_ref[n0:n0 + COL_CHUNK, :])

    rows = lambda w: pl.BlockSpec((tm, w), lambda i: (i, 0))
    return _call(
        body,
        grid=(s // tm,),
        in_specs=[rows(f), _resident((f, d)), rows(d), pl.BlockSpec((1, d), lambda i: (0, 0)), _resident((n, d))],
        out_specs=[rows(d), rows(n), rows(d)],
        out_shape=[jax.ShapeDtypeStruct((s, d), F32), jax.ShapeDtypeStruct((s, n), F32),
                   jax.ShapeDtypeStruct((s, d), BF16)],
        operands=(a, w2, x1, g, wt), name=name, comm=comm)


def _mix_ff_in(ya, yb, yc, gg, wo, x0, g_mlp, wt1, name, comm=None):
    s = ya.shape[0]
    d = wo.shape[1]
    f = wt1.shape[0]
    tm = _row_tile(4 * d + 4 * d + 2 * d + 4 * d + 2 * d + 2 * f, 2 * d * d + 2 * f * d)

    def body(ya_ref, yb_ref, yc_ref, gg_ref, wo_ref, x_ref, g_ref, w1_ref, y_ref, x1_ref, a_ref, h_ref):
        parts = []
        for ref in (ya_ref, yb_ref, yc_ref):
            t = ref[...]
            parts.append(t * _rms_scale(t))
        y = (jnp.concatenate(parts, axis=1) * gg_ref[...]).astype(BF16)
        y_ref[...] = y
        x1 = x_ref[...] + _dot_nn(y, wo_ref[...])
        x1_ref[...] = x1
        h = ((x1 * _rms_scale(x1)) * g_ref[...]).astype(BF16)
        h_ref[...] = h
        for n0 in range(0, f, COL_CHUNK):
            u = _dot_nt(h, w1_ref[n0:n0 + COL_CHUNK, :])
            a_ref[:, n0:n0 + COL_CHUNK] = jnp.square(jnp.maximum(u, 0.0)).astype(BF16)

    rows = lambda w: pl.BlockSpec((tm, w), lambda i: (i, 0))
    vec = pl.BlockSpec((1, d), lambda i: (0, 0))
    return _call(
        body,
        grid=(s // tm,),
        in_specs=[rows(A_WIDTH), rows(CONV_CH), rows(C_WIDTH), vec, _resident((d, d)), rows(d), vec, _resident((f, d))],
        out_specs=[rows(d), rows(d), rows(f), rows(d)],
        out_shape=[jax.ShapeDtypeStruct((s, d), BF16), jax.ShapeDtypeStruct((s, d), F32),
                   jax.ShapeDtypeStruct((s, f), BF16), jax.ShapeDtypeStruct((s, d), BF16)],
        operands=(ya, yb, yc, gg, wo, x0, g_mlp, wt1), name=name, comm=comm)


def _relu_from_square(av):
    return av * lax.rsqrt(jnp.maximum(av, F32_TINY))


def _mm_res_loss(a, w2, x1, g, target, name):
    s, f = a.shape
    d = w2.shape[1]
    tm = _row_tile(2 * f + 4 * d + 4 * d + 4 * d + 2 * d + 2 * f, 2 * f * d)

    def body(a_ref, w_ref, x_ref, g_ref, t_ref, loss_ref, dx_ref, dxb_ref, dg_ref, du_ref):
        @pl.when(pl.program_id(0) == 0)
        def _():
            loss_ref[...] = jnp.zeros_like(loss_ref)
            dg_ref[...] = jnp.zeros_like(dg_ref)

        xx = x_ref[...] + _dot_nn(a_ref[...], w_ref[...])
        r = _rms_scale(xx)
        n = xx * r
        gv = g_ref[...]
        err = n * gv - t_ref[...]
        per_tok = jnp.sum(err * err, axis=1, keepdims=True) * (1.0 / d)
        loss_ref[...] += 0.5 * jnp.sum(per_tok, axis=0, keepdims=True)
        dout = err * (1.0 / d)
        dg_ref[...] += jnp.sum(dout * n, axis=0, keepdims=True)
        dx = _rms_bwd(n, r, dout * gv)
        dx_ref[...] = dx
        dxb = dx.astype(BF16)
        dxb_ref[...] = dxb
        for n0 in range(0, f, COL_CHUNK):
            da = _dot_nt(dxb, w_ref[n0:n0 + COL_CHUNK, :])
            rl = _relu_from_square(a_ref[:, n0:n0 + COL_CHUNK].astype(F32))
            du_ref[:, n0:n0 + COL_CHUNK] = (da * (2.0 * rl)).astype(BF16)

    rows = lambda w: pl.BlockSpec((tm, w), lambda i: (i, 0))
    vec = pl.BlockSpec((1, d), lambda i: (0, 0))
    return pl.pallas_call(
        body,
        grid=(s // tm,),
        in_specs=[rows(f), _resident((f, d)), rows(d), vec, rows(d)],
        out_specs=[pl.BlockSpec((8, LANES), lambda i: (0, 0)), rows(d), rows(d), vec, rows(f)],
        out_shape=[jax.ShapeDtypeStruct((8, LANES), F32), jax.ShapeDtypeStruct((s, d), F32),
                   jax.ShapeDtypeStruct((s, d), BF16), jax.ShapeDtypeStruct((1, d), F32),
                   jax.ShapeDtypeStruct((s, f), BF16)],
        compiler_params=_params("arbitrary"),
        name=name,
    )(a, w2, x1, g, target)


def _mlp_bwd_act(dxb, w2, a, name, comm=None):
    s, d = dxb.shape
    f = w2.shape[0]
    tm = _row_tile(2 * d + 2 * f + 2 * f, 2 * f * d)

    def body(dx_ref, w_ref, a_ref, du_ref):
        dx = dx_ref[...]
        for n0 in range(0, f, COL_CHUNK):
            da = _dot_nt(dx, w_ref[n0:n0 + COL_CHUNK, :])
            rl = _relu_from_square(a_ref[:, n0:n0 + COL_CHUNK].astype(F32))
            du_ref[:, n0:n0 + COL_CHUNK] = (da * (2.0 * rl)).astype(BF16)

    return _call(
        body,
        grid=(s // tm,),
        in_specs=[pl.BlockSpec((tm, d), lambda i: (i, 0)),
                  _resident((f, d)),
                  pl.BlockSpec((tm, f), lambda i: (i, 0))],
        out_specs=[pl.BlockSpec((tm, f), lambda i: (i, 0))],
        out_shape=[jax.ShapeDtypeStruct((s, f), BF16)],
        operands=(dxb, w2, a), name=name, comm=comm)


def _mm_tn(pairs, name, comm=None):
    s, d = pairs[0][1].shape
    tn = 512
    tiles = [a.shape[1] // tn for a, _ in pairs]
    starts = [sum(tiles[:k]) for k in range(len(pairs))]

    def body(*refs):
        ins, outs, acc = refs[:2 * len(pairs)], refs[2 * len(pairs):3 * len(pairs)], refs[3 * len(pairs)]
        j = pl.program_id(0)
        for k in range(len(pairs)):
            def run(a_ref=ins[2 * k], b_ref=ins[2 * k + 1], o_ref=outs[k]):
                for k0 in range(0, s, ROW_TILE):
                    part = _dot_tn(a_ref[k0:k0 + ROW_TILE, :], b_ref[k0:k0 + ROW_TILE, :])
                    if k0 == 0:
                        acc[...] = part
                    else:
                        acc[...] += part
                o_ref[...] = acc[...].astype(BF16)

            pl.when((j >= starts[k]) & (j < starts[k] + tiles[k]))(run)

    def tile_of(k):
        return lambda j: jnp.clip(j - starts[k], 0, tiles[k] - 1)

    in_specs, out_specs = [], []
    for k in range(len(pairs)):
        in_specs += [pl.BlockSpec((s, tn), lambda j, t=tile_of(k): (0, t(j))), _resident((s, d))]
        out_specs.append(pl.BlockSpec((tn, d), lambda j, t=tile_of(k): (t(j), 0)))
    return _call(
        body,
        grid=(sum(tiles),),
        in_specs=in_specs,
        out_specs=out_specs,
        out_shape=[jax.ShapeDtypeStruct((a.shape[1], d), BF16) for a, _ in pairs],
        operands=tuple(t for pair in pairs for t in pair), name=name,
        scratch_shapes=[pltpu.VMEM((tn, d), F32)], comm=comm)


def _mm_nn_normbwd(dact, wt, x, dres, g, name, comm=None):
    s, kdim = dact.shape
    d = wt.shape[1]
    tm = _row_tile(2 * kdim + 4 * d + 4 * d + 4 * d + 2 * d, 2 * kdim * d)

    def body(a_ref, w_ref, x_ref, r_ref, g_ref, o_ref, ob_ref, dg_ref):
        @pl.when(pl.program_id(0) == 0)
        def _():
            dg_ref[...] = jnp.zeros_like(dg_ref)

        dh = _dot_nn(a_ref[...], w_ref[...])
        xx = x_ref[...]
        r = _rms_scale(xx)
        n = xx * r
        dg_ref[...] += jnp.sum(dh * n, axis=0, keepdims=True)
        dx = r_ref[...] + _rms_bwd(n, r, dh * g_ref[...])
        o_ref[...] = dx
        ob_ref[...] = dx.astype(BF16)

    return _call(
        body,
        grid=(s // tm,),
        in_specs=[pl.BlockSpec((tm, kdim), lambda i: (i, 0)),
                  _resident((kdim, d)),
                  pl.BlockSpec((tm, d), lambda i: (i, 0)),
                  pl.BlockSpec((tm, d), lambda i: (i, 0)),
                  pl.BlockSpec((1, d), lambda i: (0, 0))],
        out_specs=[pl.BlockSpec((tm, d), lambda i: (i, 0)),
                   pl.BlockSpec((tm, d), lambda i: (i, 0)),
                   pl.BlockSpec((1, d), lambda i: (0, 0))],
        out_shape=[jax.ShapeDtypeStruct((s, d), F32), jax.ShapeDtypeStruct((s, d), BF16),
                   jax.ShapeDtypeStruct((1, d), F32)],
        operands=(dact, wt, x, dres, g), name=name, comm=comm)


def _ff_in_mix_bwd(du, wt1, x1, dres, g_mlp, wo, ya, yb, yc, gg, name, comm=None):
    s, f = du.shape
    d = wt1.shape[1]
    widths = (A_WIDTH, CONV_CH, C_WIDTH)
    tm = _row_tile(2 * f + 4 * d + 4 * d + 4 * d + 2 * d + 4 * d + 4 * d, 2 * f * d + 2 * d * d)

    def body(du_ref, w1_ref, x_ref, r_ref, g_ref, wo_ref, ya_ref, yb_ref, yc_ref, gg_ref,
             dx_ref, dxb_ref, dg_ref, da_ref, db_ref, dc_ref, dgg_ref):
        @pl.when(pl.program_id(0) == 0)
        def _():
            dg_ref[...] = jnp.zeros_like(dg_ref)
            dgg_ref[...] = jnp.zeros_like(dgg_ref)

        dh = _dot_nn(du_ref[...], w1_ref[...])
        xx = x_ref[...]
        r = _rms_scale(xx)
        n = xx * r
        dg_ref[...] += jnp.sum(dh * n, axis=0, keepdims=True)
        dx = r_ref[...] + _rms_bwd(n, r, dh * g_ref[...])
        dx_ref[...] = dx
        dxb = dx.astype(BF16)
        dxb_ref[...] = dxb

        dy = _dot_nt(dxb, wo_ref[...])
        gv = gg_ref[...]
        off = 0
        dgs = []
        for ref, out, w in zip((ya_ref, yb_ref, yc_ref), (da_ref, db_ref, dc_ref), widths):
            t = ref[...]
            r = _rms_scale(t)
            n = t * r
            dyg = dy[:, off:off + w]
            dgs.append(jnp.sum(dyg * n, axis=0, keepdims=True))
            out[...] = _rms_bwd(n, r, dyg * gv[:, off:off + w])
            off += w
        dgg_ref[...] += jnp.concatenate(dgs, axis=1)

    rows = lambda w: pl.BlockSpec((tm, w), lambda i: (i, 0))
    vec = pl.BlockSpec((1, d), lambda i: (0, 0))
    return _call(
        body,
        grid=(s // tm,),
        in_specs=[rows(f), _resident((f, d)), rows(d), rows(d), vec, _resident((d, d)),
                  rows(A_WIDTH), rows(CONV_CH), rows(C_WIDTH), vec],
        out_specs=[rows(d), rows(d), vec, rows(A_WIDTH), rows(CONV_CH), rows(C_WIDTH), vec],
        out_shape=[jax.ShapeDtypeStruct((s, d), F32), jax.ShapeDtypeStruct((s, d), BF16), jax.ShapeDtypeStruct((1, d), F32),
                   jax.ShapeDtypeStruct((s, A_WIDTH), F32), jax.ShapeDtypeStruct((s, CONV_CH), F32),
                   jax.ShapeDtypeStruct((s, C_WIDTH), F32), jax.ShapeDtypeStruct((1, d), F32)],
        operands=(du, wt1, x1, dres, g_mlp, wo, ya, yb, yc, gg), name=name, comm=comm)


CONV_CHUNK = 256
CONV_HALO = 8


def _conv_fwd(z, cw, name):
    s = z.shape[0]
    nch = s // CONV_CHUNK

    def body(gb_ref, gc_ref, xb_ref, w_ref, o_ref, us):
        us[pl.ds(0, CONV_HALO), :] = jnp.zeros((CONV_HALO, LANES), F32)
        us[pl.ds(CONV_HALO, s), :] = gc_ref[...] * xb_ref[...]
        w0, w1, w2 = w_ref[0:1, :], w_ref[1:2, :], w_ref[2:3, :]

        def chunk(c, carry):
            st = pl.multiple_of(c * CONV_CHUNK, CONV_CHUNK)
            ext = us[pl.ds(st, CONV_CHUNK + CONV_HALO), :]
            y = (w0 * ext[CONV_HALO - 2:CONV_HALO - 2 + CONV_CHUNK]
                 + w1 * ext[CONV_HALO - 1:CONV_HALO - 1 + CONV_CHUNK]
                 + w2 * ext[CONV_HALO:])
            o_ref[pl.ds(st, CONV_CHUNK), :] = gb_ref[pl.ds(st, CONV_CHUNK), :] * y
            return carry

        lax.fori_loop(0, nch, chunk, 0)

    col = lambda blk: pl.BlockSpec((s, LANES), lambda j, blk=blk: (0, blk + j))
    return pl.pallas_call(
        body,
        grid=(CONV_CH // LANES,),
        in_specs=[col(GB_BLK), col(GC_BLK), col(XB_BLK), pl.BlockSpec((3, LANES), lambda j: (0, j))],
        out_specs=pl.BlockSpec((s, LANES), lambda j: (0, j)),
        out_shape=jax.ShapeDtypeStruct((s, CONV_CH), F32),
        scratch_shapes=[pltpu.VMEM((s + CONV_HALO, LANES), F32)],
        compiler_params=_params("parallel"),
        name=name,
    )(z, z, z, cw)


def _conv_bwd(z, cw, dyb, dz, name):
    s = z.shape[0]
    nch = s // CONV_CHUNK
    ncol = CONV_CH // LANES

    def body(gb_ref, gc_ref, xb_ref, w_ref, dy_ref, dz_in, dz_ref, dw_ref, us, ds_, dgb_ref, dgc_ref, dxb_ref, sems):
        j = pl.program_id(0)

        def to_dz(staged, blk, k):
            cols = pl.ds(pl.multiple_of((blk + j) * LANES, LANES), LANES)
            return pltpu.make_async_copy(staged, dz_ref.at[:, cols], sems.at[k])

        copies = [to_dz(dgb_ref, GB_BLK, 0), to_dz(dgc_ref, GC_BLK, 1), to_dz(dxb_ref, XB_BLK, 2)]

        @pl.when(j > 0)
        def _():
            for cp in copies:
                cp.wait()

        us[pl.ds(0, CONV_HALO), :] = jnp.zeros((CONV_HALO, LANES), F32)
        us[pl.ds(CONV_HALO, s), :] = gc_ref[...] * xb_ref[...]
        ds_[pl.ds(s, CONV_HALO), :] = jnp.zeros((CONV_HALO, LANES), F32)
        ds_[pl.ds(0, s), :] = dy_ref[...] * gb_ref[...]
        w0, w1, w2 = w_ref[0:1, :], w_ref[1:2, :], w_ref[2:3, :]
        zero = jnp.zeros((1, LANES), F32)

        def chunk(c, carry):
            a0, a1, a2 = carry
            st = pl.multiple_of(c * CONV_CHUNK, CONV_CHUNK)
            rows = pl.ds(st, CONV_CHUNK)
            ext = us[pl.ds(st, CONV_CHUNK + CONV_HALO), :]
            um2 = ext[CONV_HALO - 2:CONV_HALO - 2 + CONV_CHUNK]
            um1 = ext[CONV_HALO - 1:CONV_HALO - 1 + CONV_CHUNK]
            u0 = ext[CONV_HALO:]
            dext = ds_[pl.ds(st, CONV_CHUNK + CONV_HALO), :]
            dc0 = dext[:CONV_CHUNK]
            du = w2 * dc0 + w1 * dext[1:1 + CONV_CHUNK] + w0 * dext[2:2 + CONV_CHUNK]
            yconv = w0 * um2 + w1 * um1 + w2 * u0
            dgb_ref[rows, :] = (dy_ref[rows, :] * yconv).astype(BF16)
            dgc_ref[rows, :] = (du * xb_ref[rows, :]).astype(BF16)
            dxb_ref[rows, :] = (du * gc_ref[rows, :]).astype(BF16)
            a0 = a0 + jnp.sum(dc0 * um2, axis=0, keepdims=True)
            a1 = a1 + jnp.sum(dc0 * um1, axis=0, keepdims=True)
            a2 = a2 + jnp.sum(dc0 * u0, axis=0, keepdims=True)
            return a0, a1, a2

        a0, a1, a2 = lax.fori_loop(0, nch, chunk, (zero, zero, zero))
        dw_ref[...] = jnp.concatenate([a0, a1, a2, jnp.zeros((5, LANES), F32)], axis=0)
        for cp in copies:
            cp.start()

        @pl.when(j == ncol - 1)
        def _():
            for cp in copies:
                cp.wait()

    col = lambda blk: pl.BlockSpec((s, LANES), lambda j, blk=blk: (0, blk + j))
    hbm = pl.BlockSpec(memory_space=pl.ANY)
    return pl.pallas_call(
        body,
        grid=(ncol,),
        in_specs=[col(GB_BLK), col(GC_BLK), col(XB_BLK), pl.BlockSpec((3, LANES), lambda j: (0, j)),
                  pl.BlockSpec((s, LANES), lambda j: (0, j)), hbm],
        out_specs=[hbm, pl.BlockSpec((8, LANES), lambda j: (0, j))],
        out_shape=[jax.ShapeDtypeStruct(dz.shape, dz.dtype), jax.ShapeDtypeStruct((8, CONV_CH), F32)],
        scratch_shapes=[pltpu.VMEM((s + CONV_HALO, LANES), F32), pltpu.VMEM((s + CONV_HALO, LANES), F32)]
        + [pltpu.VMEM((s, LANES), BF16)] * 3 + [pltpu.SemaphoreType.DMA((3,))],
        input_output_aliases={5: 0},
        compiler_params=_params("arbitrary"),
        name=name,
    )(z, z, z, cw, dyb, dz)


ATTN_ROWS = 512
ATTN_UNROLL = 8


def _band_rows(b, d, r):
    base = pl.multiple_of(b * (BLOCK * d), BLOCK)
    prev = jnp.maximum(base - BLOCK * d, 0)
    if d == 1:
        return pl.ds(base, BLOCK), pl.ds(pl.multiple_of(prev, BLOCK), BLOCK)
    return pl.ds(base + r, BLOCK, stride=d), pl.ds(prev + r, BLOCK, stride=d)


def _write_band_bias(bias_ref, max_dist):
    qi = lax.broadcasted_iota(jnp.int32, (BLOCK, 2 * BLOCK), 0)
    kj = lax.broadcasted_iota(jnp.int32, (BLOCK, 2 * BLOCK), 1)
    dist = BLOCK + qi - kj
    band = (dist >= 0) & (dist <= max_dist)
    bias_ref[0:BLOCK, :] = jnp.where(band, 0.0, -jnp.inf)
    bias_ref[BLOCK:2 * BLOCK, :] = jnp.where(band & (kj >= BLOCK), 0.0, -jnp.inf)


def _band_bias(bias_ref, b):
    bias = bias_ref[pl.ds(pl.multiple_of(jnp.where(b > 0, 0, BLOCK), BLOCK), BLOCK), :]
    return jnp.concatenate([bias, bias], axis=0)


def _kv_halves(pair):
    zero = jnp.zeros((1, LANES), jnp.int32)
    return zero + (pair >> 1), zero + ((pair + 1) >> 1)


def _stack_heads(t, head0, halves=None):
    top, bottom = jnp.where(head0, t, 0.0), jnp.where(head0, 0.0, t)
    if halves is not None:
        top = jnp.where(halves[0] == 1, pltpu.roll(top, HEAD_DIM, 1), top)
        bottom = jnp.where(halves[1] == 0, pltpu.roll(bottom, HEAD_DIM, 1), bottom)
    return jnp.concatenate([top, bottom], axis=0).astype(BF16)


def _unstack_heads(t, head0, halves=None):
    top, bottom = t[:BLOCK], t[BLOCK:]
    if halves is not None:
        top = jnp.where(halves[0] == 1, pltpu.roll(top, HEAD_DIM, 1), top)
        bottom = jnp.where(halves[1] == 0, pltpu.roll(bottom, HEAD_DIM, 1), bottom)
    return jnp.where(head0, top, bottom)


def _block_loops(s, patterns, unroll, one_block):
    for n, d in enumerate(patterns):
        nb = (s // BLOCK) // d
        ur = min(unroll, d)
        ub = unroll // ur
        for r0 in range(0, d, ur):
            def trip(i, carry, n=n, d=d, r0=r0, ur=ur, ub=ub):
                for u in range(ub):
                    for r in range(r0, r0 + ur):
                        one_block(i * ub + u, d, r, n == 0)
                return carry
            lax.fori_loop(0, nb // ub, trip, 0)


def _attn_fwd(z, m_init, l_init, q_blk, k_blk, v_blk, patterns, max_dist, gqa, name, comm=None):
    s = z.shape[0]
    npair = 3

    def body(q_ref, k_ref, v_ref, mi_ref, o_ref, lse0_ref, lse1_ref, bias_scr, m_scr, l_scr, *kv_scr):
        head0 = lax.broadcasted_iota(jnp.int32, (1, LANES), 1) < HEAD_DIM
        _write_band_bias(bias_scr, max_dist)
        ones = jnp.ones((2 * BLOCK, LANES), BF16)
        k_src, v_src = kv_scr if gqa else (k_ref, v_ref)
        if gqa:
            half = (lax.broadcasted_iota(jnp.int32, (1, LANES), 1) >= HEAD_DIM).astype(jnp.int32)
            swap = ((pl.program_id(0) + half) >> 1) != half

            def expand(c, carry):
                rows = pl.ds(pl.multiple_of(c * ATTN_ROWS, ATTN_ROWS), ATTN_ROWS)
                k_src[rows, :] = jnp.where(swap, pltpu.roll(k_ref[rows, :], HEAD_DIM, 1), k_ref[rows, :])
                v_src[rows, :] = jnp.where(swap, pltpu.roll(v_ref[rows, :], HEAD_DIM, 1), v_ref[rows, :])
                return carry

            lax.fori_loop(0, s // ATTN_ROWS, expand, 0)

        def one_block(b, d, r, first):
            rq, rp = _band_rows(b, d, r)
            q2 = _stack_heads(q_ref[rq, :] * (SCALE * LOG2E), head0)
            k2 = jnp.concatenate([k_src[rp, :], k_src[rq, :]], axis=0).astype(BF16)
            v2 = jnp.concatenate([v_src[rp, :], v_src[rq, :]], axis=0).astype(BF16)
            sc = _dot_nt(q2, k2) + _band_bias(bias_scr, b)
            mb = jnp.max(sc, axis=1, keepdims=True)
            p = jnp.exp2(sc - mb).astype(BF16)
            ob = _dot_nn(p, jnp.concatenate([v2, ones], axis=1))
            m_blk = _unstack_heads(jnp.broadcast_to(mb, (2 * BLOCK, LANES)), head0)
            l_blk = _unstack_heads(ob[:, LANES:], head0)
            o_blk = _unstack_heads(ob[:, :LANES], head0)
            if first and l_init == 0.0:
                m_new, l_new, o_new = m_blk, l_blk, o_blk
            else:
                if first:
                    m_old, l_old, o_old = jnp.broadcast_to(mi_ref[...], (BLOCK, LANES)), l_init, 0.0
                else:
                    m_old, l_old, o_old = m_scr[rq, :], l_scr[rq, :], o_ref[rq, :]
                m_new = jnp.maximum(m_old, m_blk)
                a_old = jnp.exp2(m_old - m_new)
                a_blk = jnp.exp2(m_blk - m_new)
                l_new = l_old * a_old + l_blk * a_blk
                o_new = o_old * a_old + o_blk * a_blk
            o_ref[rq, :], l_scr[rq, :], m_scr[rq, :] = o_new, l_new, m_new

        _block_loops(s, patterns, ATTN_UNROLL, one_block)

        def fin(c, carry):
            rows = pl.ds(pl.multiple_of(c * ATTN_ROWS, ATTN_ROWS), ATTN_ROWS)
            l = l_scr[rows, :]
            o_ref[rows, :] = o_ref[rows, :] / l
            lse = m_scr[rows, :] + jnp.log2(l)
            swapped = pltpu.roll(lse, HEAD_DIM, 1)
            lse0_ref[rows, :] = jnp.where(head0, lse, swapped)
            lse1_ref[rows, :] = jnp.where(head0, swapped, lse)
            return carry

        lax.fori_loop(0, s // ATTN_ROWS, fin, 0)

    kv = (lambda blk: pl.BlockSpec((s, LANES), lambda j, blk=blk: (0, blk), pipeline_mode=pl.Buffered(1))) if gqa \
        else (lambda blk: pl.BlockSpec((s, LANES), lambda j, blk=blk: (0, blk + j)))
    own = pl.BlockSpec((s, LANES), lambda j: (0, j))
    return _call(
        body,
        grid=(npair,),
        in_specs=[pl.BlockSpec((s, LANES), lambda j: (0, q_blk + j)), kv(k_blk), kv(v_blk),
                  pl.BlockSpec((1, LANES), lambda j: (0, j))],
        out_specs=[own, own, own],
        out_shape=[jax.ShapeDtypeStruct((s, npair * LANES), F32)] * 3,
        operands=(z, z, z, m_init * LOG2E), name=name,
        scratch_shapes=[pltpu.VMEM((2 * BLOCK, 2 * BLOCK), F32)] + [pltpu.VMEM((s, LANES), F32)] * (4 if gqa else 2),
        comm=comm)


def _attn_bwd(z, do, o, lse, m_init, dz, q_blk, k_blk, v_blk, patterns, max_dist, gqa, name, comm=None):
    s = z.shape[0]
    npair = 3
    n_dz_in = 0 if dz is None else 1

    def body(q_ref, k_ref, v_ref, do_ref, o_ref, lse0_ref, lse1_ref, mi_ref, *rest):
        (dz_ref, dm_ref, dq_acc, dk_acc, dv_acc, dl0_scr, dl1_scr, bias_scr,
         dq_out, dk_out, dv_out, out_sems) = rest[n_dz_in:]
        pair = pl.program_id(0)
        head0 = lax.broadcasted_iota(jnp.int32, (1, LANES), 1) < HEAD_DIM
        halves = _kv_halves(pair) if gqa else None
        _write_band_bias(bias_scr, max_dist)

        def zero_kv():
            def f(c, carry):
                rows = pl.ds(pl.multiple_of(c * ATTN_ROWS, ATTN_ROWS), ATTN_ROWS)
                dk_acc[rows, :] = jnp.zeros((ATTN_ROWS, LANES), F32)
                dv_acc[rows, :] = jnp.zeros((ATTN_ROWS, LANES), F32)
                return carry
            lax.fori_loop(0, s // ATTN_ROWS, f, 0)

        if gqa:
            pl.when(pair == 0)(zero_kv)
        else:
            zero_kv()

        def prep(c, dm):
            rows = pl.ds(pl.multiple_of(c * ATTN_ROWS, ATTN_ROWS), ATTN_ROWS)
            dq_acc[rows, :] = jnp.zeros((ATTN_ROWS, LANES), F32)
            prod = do_ref[rows, :] * o_ref[rows, :]
            d0 = jnp.sum(jnp.where(head0, prod, 0.0), axis=1, keepdims=True)
            d1 = jnp.sum(jnp.where(head0, 0.0, prod), axis=1, keepdims=True)
            dl0_scr[rows, :] = jnp.broadcast_to(d0, (ATTN_ROWS, LANES))
            dl1_scr[rows, :] = jnp.broadcast_to(d1, (ATTN_ROWS, LANES))
            lse_own = jnp.where(head0, lse0_ref[rows, :], lse1_ref[rows, :])
            psink = jnp.exp2(mi_ref[...] - lse_own)
            return dm - jnp.sum(psink * jnp.where(head0, d0, d1), axis=0, keepdims=True)

        dm_ref[...] = lax.fori_loop(0, s // ATTN_ROWS, prep, jnp.zeros((1, LANES), F32))

        def one_block(b, d, r, first):
            rq, rp = _band_rows(b, d, r)
            q2 = _stack_heads(q_ref[rq, :] * (SCALE * LOG2E), head0, halves)
            do2 = _stack_heads(do_ref[rq, :], head0, halves)
            k2 = jnp.concatenate([k_ref[rp, :], k_ref[rq, :]], axis=0).astype(BF16)
            v2 = jnp.concatenate([v_ref[rp, :], v_ref[rq, :]], axis=0).astype(BF16)
            lse2 = jnp.concatenate([lse0_ref[rq, :], lse1_ref[rq, :]], axis=0)
            dl2 = jnp.concatenate([dl0_scr[rq, :], dl1_scr[rq, :]], axis=0)
            lse2 = jnp.concatenate([lse2, lse2], axis=1)
            dl2 = jnp.concatenate([dl2, dl2], axis=1)
            p = jnp.exp2(_dot_nt(q2, k2) + _band_bias(bias_scr, b) - lse2)
            dp = _dot_nt(do2, v2)
            dsc = (p * (dp - dl2)).astype(BF16)
            dq2 = _unstack_heads(_dot_nn(dsc, k2), head0, halves)
            dk2 = _dot_tn(dsc, q2)
            dv2 = _dot_tn(p.astype(BF16), do2)
            dq_acc[rq, :] += dq2 * SCALE
            dk_acc[rp, :] += dk2[:BLOCK]
            dk_acc[rq, :] += dk2[BLOCK:]
            dv_acc[rp, :] += dv2[:BLOCK]
            dv_acc[rq, :] += dv2[BLOCK:]

        _block_loops(s, patterns, ATTN_UNROLL, one_block)

        def to_dz(staged, blk, k):
            cols = pl.ds(pl.multiple_of(blk * LANES, LANES), LANES)
            return pltpu.make_async_copy(staged, dz_ref.at[:, cols], out_sems.at[k])

        last_pair = pair == npair - 1
        q_copy = to_dz(dq_out, q_blk + pair, 0)
        kv_copies = [to_dz(dk_out, k_blk + (0 if gqa else pair), 1), to_dz(dv_out, v_blk + (0 if gqa else pair), 2)]

        @pl.when(pair > 0)
        def _():
            for cp in [q_copy] + ([] if gqa else kv_copies):
                cp.wait()

        def stage(acc, out, factor=None):
            def f(c, carry):
                rows = pl.ds(pl.multiple_of(c * ATTN_ROWS, ATTN_ROWS), ATTN_ROWS)
                t = acc[rows, :]
                out[rows, :] = (t if factor is None else t * factor).astype(BF16)
                return carry
            lax.fori_loop(0, s // ATTN_ROWS, f, 0)

        def stage_kv():
            stage(dk_acc, dk_out, LN2)
            stage(dv_acc, dv_out)
            for cp in kv_copies:
                cp.start()

        stage(dq_acc, dq_out)
        q_copy.start()
        if gqa:
            pl.when(last_pair)(stage_kv)
        else:
            stage_kv()

        @pl.when(last_pair)
        def _():
            for cp in [q_copy] + kv_copies:
                cp.wait()

    own = pl.BlockSpec((s, LANES), lambda j: (0, j))
    hbm = pl.BlockSpec(memory_space=pl.ANY)
    if gqa:
        kv = lambda blk: pl.BlockSpec((s, LANES), lambda j, blk=blk: (0, blk), pipeline_mode=pl.Buffered(1))
    else:
        kv = lambda blk: pl.BlockSpec((s, LANES), lambda j, blk=blk: (0, blk + j))
    in_specs = [pl.BlockSpec((s, LANES), lambda j: (0, q_blk + j)), kv(k_blk), kv(v_blk), own, own, own, own,
                pl.BlockSpec((1, LANES), lambda j: (0, j))]
    operands = (z, z, z, do, o, lse[0], lse[1], m_init * LOG2E)
    return _call(
        body,
        grid=(npair,),
        in_specs=in_specs + [hbm] * n_dz_in,
        out_specs=[hbm, pl.BlockSpec((1, LANES), lambda j: (0, j))],
        out_shape=[jax.ShapeDtypeStruct((s, IN_WIDTH), BF16), jax.ShapeDtypeStruct((1, npair * LANES), F32)],
        operands=operands + (() if dz is None else (dz,)), name=name,
        scratch_shapes=[pltpu.VMEM((s, LANES), F32)] * 5 + [pltpu.VMEM((2 * BLOCK, 2 * BLOCK), F32)]
        + [pltpu.VMEM((s, LANES), BF16)] * 3 + [pltpu.SemaphoreType.DMA((3,))],
        comm=comm, aliases={} if dz is None else {len(in_specs): 0})


def _adamw_math(w, g, m, v):
    m = ADAM_B1 * m + (1.0 - ADAM_B1) * g
    v = ADAM_B2 * v + (1.0 - ADAM_B2) * (g * g)
    m_hat = m / (1.0 - ADAM_B1 ** ADAM_STEP)
    v_hat = v / (1.0 - ADAM_B2 ** ADAM_STEP)
    delta = -ADAM_LR * (m_hat / (jnp.sqrt(v_hat) + ADAM_EPS) + ADAM_WD * w)
    return delta, m, v


def _adamw(w, g, m, v, name):
    rows, cols = w.shape
    tr = min(rows, 256)

    def body(w_ref, g_ref, m_ref, v_ref, d_ref, nm_ref, nv_ref):
        d_ref[...], nm_ref[...], nv_ref[...] = _adamw_math(w_ref[...], g_ref[...], m_ref[...], v_ref[...])

    spec = pl.BlockSpec((tr, cols), lambda i: (i, 0))
    return pl.pallas_call(
        body,
        grid=(rows // tr,),
        in_specs=[spec] * 4,
        out_specs=[spec] * 3,
        out_shape=[jax.ShapeDtypeStruct((rows, cols), F32)] * 3,
        compiler_params=_params("parallel"),
        name=name,
    )(w, g, m, v)


def _sum_adamw(parts, w, m, v, pos, transpose, name):
    assert len(parts) == DEPTH == 2
    (p0, r0), (p1, r1) = parts
    _, rows, cols = p0.shape
    tr = 256 if rows % 256 == 0 else rows
    nt = rows // tr

    def body(pos_ref, p0_ref, r0_ref, p1_ref, r1_ref, w_ref, m_ref, v_ref, g_ref, d_ref, nm_ref, nv_ref):
        def run(p_ref, r_ref):
            g = ((p_ref[...].astype(F32) + r_ref[0].astype(F32)) + r_ref[1].astype(F32)) + r_ref[2].astype(F32)
            if transpose:
                g = g.T
            g_ref[...] = g
            d_ref[...], nm_ref[...], nv_ref[...] = _adamw_math(w_ref[...], g, m_ref[...], v_ref[...])

        layer0 = pl.program_id(0) < nt
        pl.when(layer0)(lambda: run(p0_ref, r0_ref))
        pl.when(jnp.logical_not(layer0))(lambda: run(p1_ref, r1_ref))

    def tile0(i):
        return jnp.minimum(i, nt - 1)

    def tile1(i):
        return jnp.maximum(i - nt, 0)

    if transpose:
        w_spec = pl.BlockSpec((None, cols, tr), lambda i, q: (i // nt, 0, i % nt))
    else:
        w_spec = pl.BlockSpec((None, tr, cols), lambda i, q: (i // nt, i % nt, 0))
    return pl.pallas_call(
        body,
        grid_spec=pltpu.PrefetchScalarGridSpec(
            num_scalar_prefetch=1,
            grid=(DEPTH * nt,),
            in_specs=[pl.BlockSpec((None, tr, cols), lambda i, q: (q[0], tile0(i), 0)),
                      pl.BlockSpec((3, tr, cols), lambda i, q: (0, tile0(i), 0)),
                      pl.BlockSpec((None, tr, cols), lambda i, q: (q[0], tile1(i), 0)),
                      pl.BlockSpec((3, tr, cols), lambda i, q: (0, tile1(i), 0)),
                      w_spec, w_spec, w_spec],
            out_specs=[w_spec] * 4,
        ),
        out_shape=[jax.ShapeDtypeStruct(w.shape, F32)] * 4,
        compiler_params=_params("arbitrary"),
        name=name,
    )(pos, p0, r0, p1, r1, w, m, v)


def _small_sum_adamw(gathered, params, name):
    _, rows, cols = gathered.shape
    n = len(params)

    def body(ga_ref, *refs):
        ins, outs, (g_scr,) = refs[:3 * n], refs[3 * n:7 * n + 2], refs[7 * n + 2:]
        g = ga_ref[0]
        for i in range(1, N_DEV):
            g = g + ga_ref[i]
        g_scr[...] = g
        for k, (row0, w, _, _) in enumerate(params):
            w_ref, m_ref, v_ref = ins[3 * k:3 * k + 3]
            gk = g_scr[row0:row0 + w.shape[0], :]
            outs[4 * k][...] = gk
            outs[4 * k + 1][...], outs[4 * k + 2][...], outs[4 * k + 3][...] = _adamw_math(
                w_ref[...], gk, m_ref[...], v_ref[...])
        outs[4 * n][...] = g_scr[CONV_ROW:CONV_ROW + 8, :]
        outs[4 * n + 1][...] = g_scr[LOSS_ROW:LOSS_ROW + 1, :]

    out_shape = []
    for _, w, _, _ in params:
        out_shape += [jax.ShapeDtypeStruct(w.shape, F32)] * 4
    out_shape += [jax.ShapeDtypeStruct((8, cols), F32), jax.ShapeDtypeStruct((1, cols), F32)]
    res = pl.pallas_call(
        body,
        out_shape=out_shape,
        scratch_shapes=[pltpu.VMEM((rows, cols), F32)],
        name=name,
    )(gathered, *[t for _, w, m, v in params for t in (w, m, v)])
    return [res[4 * k:4 * k + 4] for k in range(n)], res[4 * n], res[4 * n + 1]


def _pair_sum(g4, r1, pos, name):
    _, _, rows, cols = g4.shape
    tr = min(rows, 512)

    def body(pos_ref, g_ref, r_ref, o_ref):
        o_ref[...] = (g_ref[...].astype(F32) + r_ref[...].astype(F32)).astype(BF16)

    return pl.pallas_call(
        body,
        grid_spec=pltpu.PrefetchScalarGridSpec(
            num_scalar_prefetch=1,
            grid=(4, rows // tr),
            in_specs=[pl.BlockSpec((None, None, tr, cols), lambda i, j, p: (i, p[1], j, 0)),
                      pl.BlockSpec((None, tr, cols), lambda i, j, p: (i, j, 0))],
            out_specs=pl.BlockSpec((None, tr, cols), lambda i, j, p: (i, j, 0)),
        ),
        out_shape=jax.ShapeDtypeStruct((4, rows, cols), BF16),
        compiler_params=_params("parallel", "parallel"),
        name=name,
    )(pos, g4, r1)


def _place():
    return lax.axis_index("x"), lax.axis_index("y"), lax.axis_index("c")


def _gather_comm(shards):
    na = len(shards)

    def plan(ins, outs, sems):
        send_sems, recv_sems, local_sems = sems
        x, y, c = _place()
        me, sibling = (x, y, c), (x, y, 1 - c)
        chips = [(1 - x, y), (x, 1 - y), (1 - x, 1 - y)]

        def rows(a, px, py, pc):
            m = ins[a].shape[0]
            return outs[a].at[pl.ds((4 * px + 2 * py + pc) * m, m), :]

        def copy(a, k, block, to, src=None):
            return pltpu.make_async_remote_copy(
                src_ref=rows(a, *block) if src is None else src, dst_ref=rows(a, *block),
                send_sem=send_sems.at[a, k], recv_sem=recv_sems.at[a, k], device_id=to, device_id_type=MESH)

        mine = [pltpu.make_async_copy(ins[a], rows(a, *me), local_sems.at[a]) for a in range(na)]
        first = []
        for a in range(na):
            first.append(copy(a, 0, me, sibling, src=ins[a]))
            first += [copy(a, 1 + j, me, (*chip, c), src=ins[a]) for j, chip in enumerate(chips)]
        return me, sibling, chips, c, copy, mine, first

    def start(ins, outs, sems):
        *_, mine, first = plan(ins, outs, sems)
        for cp in mine + first:
            cp.start()

    def finish(ins, outs, sems):
        me, sibling, chips, c, copy, mine, first = plan(ins, outs, sems)
        passed = []
        for j, chip in enumerate(chips):
            for a in range(na):
                copy(a, 1 + j, (*chip, c), me).wait_recv()
                cp = copy(a, 4 + j, (*chip, c), sibling)
                cp.start()
                passed.append(cp)
        for a in range(na):
            copy(a, 0, sibling, me).wait_recv()
            for j, chip in enumerate(chips):
                copy(a, 4 + j, (*chip, 1 - c), me).wait_recv()
        for cp in first + passed:
            cp.wait_send()
        for cp in mine:
            cp.wait()

    return _Comm(tuple(shards),
                 tuple(jax.ShapeDtypeStruct((N_DEV * t.shape[0], t.shape[1]), t.dtype) for t in shards),
                 (pltpu.SemaphoreType.DMA((na, 7)), pltpu.SemaphoreType.DMA((na, 7)), pltpu.SemaphoreType.DMA((na,))),
                 start, finish)


def _exchange_comm(arrays, out_shape, n_copies, copies_of):
    na = len(arrays)

    def every(ins, outs, sems):
        send_sems, recv_sems = sems
        return [cp for a in range(na) for cp in copies_of(ins, outs, a, send_sems, recv_sems)]

    def start(ins, outs, sems):
        for cp in every(ins, outs, sems):
            cp.start()

    def finish(ins, outs, sems):
        for cp in every(ins, outs, sems):
            cp.wait()

    return _Comm(tuple(arrays), tuple(out_shape),
                 (pltpu.SemaphoreType.DMA((na, n_copies)), pltpu.SemaphoreType.DMA((na, n_copies))), start, finish)


def _sibling_comm(grads):
    def copies_of(ins, outs, a, send_sems, recv_sems):
        x, y, c = _place()
        return [pltpu.make_async_remote_copy(
            src_ref=ins[a].at[chip, 1 - c], dst_ref=outs[a].at[chip],
            send_sem=send_sems.at[a, chip], recv_sem=recv_sems.at[a, chip],
            device_id=(x, y, 1 - c), device_id_type=MESH) for chip in range(4)]

    return _exchange_comm(grads, [jax.ShapeDtypeStruct((4,) + t.shape[2:], t.dtype) for t in grads], 4, copies_of)


def _chip_comm(partials):
    def copies_of(ins, outs, a, send_sems, recv_sems):
        x, y, c = _place()
        chips = [(1 - x, y), (x, 1 - y), (1 - x, 1 - y)]
        return [pltpu.make_async_remote_copy(
            src_ref=ins[a].at[2 * cx + cy], dst_ref=outs[a].at[k],
            send_sem=send_sems.at[a, k], recv_sem=recv_sems.at[a, k],
            device_id=(cx, cy, c), device_id_type=MESH) for k, (cx, cy) in enumerate(chips)]

    return _exchange_comm(partials, [jax.ShapeDtypeStruct((3,) + t.shape[1:], t.dtype) for t in partials], 3, copies_of)


def _pad_rows(t, rows):
    return jnp.pad(t, ((0, rows - t.shape[0]), (0, D_MODEL - t.shape[1])))


MIX_ROW, GROUP_ROW, MLP_ROW, FINAL_ROW, CONV_ROW, SINK_ROW = 0, 8, 16, 24, 32, 40
LOSS_ROW = FINAL_ROW + 1


def _pack_small(g_mix, g_group, g_mlp, g_final, conv, sinks, loss):
    final_and_loss = jnp.concatenate([g_final.reshape(1, D_MODEL), _pad_rows(loss, 1)], axis=0)
    return jnp.concatenate([
        _pad_rows(g_mix, 8), _pad_rows(g_group, 8), _pad_rows(g_mlp, 8), _pad_rows(final_and_loss, 8),
        _pad_rows(conv.reshape(DEPTH * 3, CONV_CH), 8), _pad_rows(sinks.reshape(1, DEPTH * 6), 8)], axis=0)


def kernel(x, w_in, conv_w, sinks, g_mix, g_group, w_o, g_mlp, w_ff_in, w_ff_out, g_final, loss_target, m_w_in, m_conv_w, m_sinks, m_g_mix, m_g_group, m_w_o, m_g_mlp, m_w_ff_in, m_w_ff_out, m_g_final, v_w_in, v_conv_w, v_sinks, v_g_mix, v_g_group, v_w_o, v_g_mlp, v_w_ff_in, v_w_ff_out, v_g_final):
    ax, ay, ac = _place()
    chip = 2 * ax + ay
    dev = 4 * ax + 2 * ay + ac
    pos = jnp.stack([chip, ac]).astype(jnp.int32)

    x0 = x.reshape(SEQ, D_MODEL)
    target = loss_target.reshape(SEQ, D_MODEL)

    shards = {}
    for l in range(DEPTH):
        shards[l, 0], shards[l, 1] = w_in[l].T.astype(BF16), w_o[l].astype(BF16)
        shards[l, 2], shards[l, 3] = w_ff_in[l].T.astype(BF16), w_ff_out[l].astype(BF16)
    conv_tile = jnp.pad(conv_w.reshape(DEPTH * 3, CONV_CH // N_DEV), ((0, 2), (0, LANES - CONV_CH // N_DEV)))
    wt_in0, conv_all = _comm_only(_gather_comm([shards[0, 0], conv_tile]), "gather_first")
    conv_full = conv_all.reshape(N_DEV, 8, LANES)[:, :DEPTH * 3, :CONV_CH // N_DEV]
    conv_full = conv_full.transpose(1, 0, 2).reshape(DEPTH, 3, CONV_CH)

    dx, parts, small = _step(x0, target, shards, wt_in0, conv_full, sinks, g_mix, g_group, g_mlp, g_final, pos)
    return _finish(dx, parts, small, pos, dev, w_in, conv_w, sinks, g_mix, g_group, w_o, g_mlp, w_ff_in, w_ff_out, g_final, m_w_in, m_conv_w, m_sinks, m_g_mix, m_g_group, m_w_o, m_g_mlp, m_w_ff_in, m_w_ff_out, m_g_final, v_w_in, v_conv_w, v_sinks, v_g_mix, v_g_group, v_w_o, v_g_mlp, v_w_ff_in, v_w_ff_out, v_g_final)


FWD_CARRY = {(0, "in_proj"): ((1, 0),), (0, "window"): ((0, 1),), (0, "dilated"): ((0, 2),),
             (0, "mix_ff_in"): ((0, 3),), (0, "ff_out_in_proj"): ((1, 3),),
             (1, "window"): ((1, 1),), (1, "dilated"): ((1, 2),)}


def _step(x0, target, shards, wt_in0, conv_full, sinks, g_mix, g_group, g_mlp, g_final, pos):
    sink_lanes = jnp.repeat(sinks.reshape(DEPTH, 6), HEAD_DIM, axis=1)
    no_sink = jnp.full((1, A_WIDTH), NEG_BIG, F32)
    full = {(0, 0): wt_in0}

    def gather(stage, l):
        keys = FWD_CARRY.get((l, stage), ())
        return keys, (_gather_comm([shards[k] for k in keys]) if keys else None)

    def landed(keys, got):
        full.update(zip(keys, got))

    saved = []
    xc = x0
    keys, comm = gather("in_proj", 0)
    (z, h), got = _norm_mm(xc, g_mix[0:1], full[0, 0], "in_proj_0", comm)
    landed(keys, got)
    for l in range(DEPTH):
        sink_l = sink_lanes[l:l + 1]
        keys, comm = gather("window", l)
        (yc, *lse_c), got = _attn_fwd(z, sink_l, 1.0, QC_BLK, KC_BLK, VC_BLK, (1,), C_MAX_DIST, True,
                                     f"window_attn_{l}", comm)
        landed(keys, got)
        yb = _conv_fwd(z, conv_full[l], f"conv_{l}")
        keys, comm = gather("dilated", l)
        (ya, *lse_a), got = _attn_fwd(z, no_sink, 0.0, QA_BLK, KA_BLK, VA_BLK, DILATED_PATTERNS, A_MAX_DIST, False,
                                     f"dilated_attn_{l}", comm)
        landed(keys, got)
        keys, comm = gather("mix_ff_in", l)
        (y, x1, a, h2), got = _mix_ff_in(ya, yb, yc, g_group[l:l + 1], full[l, 1], xc, g_mlp[l:l + 1], full[l, 2],
                                         f"mix_ff_in_{l}", comm)
        landed(keys, got)
        saved.append((xc, z, h, ya, lse_a, yb, yc, lse_c, sink_l, y, x1, a, h2))
        if l + 1 < DEPTH:
            keys, comm = gather("ff_out_in_proj", l)
            (xc, z, h), got = _ff_out_in_proj(a, full[l, 3], x1, g_mix[l + 1:l + 2], full[l + 1, 0],
                                              f"ff_out_{l}_in_proj_{l + 1}", comm)
            landed(keys, got)

    loss_slab, dx, dxb, dg_final, du = _mm_res_loss(a, full[DEPTH - 1, 3], x1, g_final.reshape(1, D_MODEL), target,
                                                    f"ff_out_{DEPTH - 1}_loss")

    def by_owner(t):
        return t.reshape(4, 2, t.shape[0] // N_DEV, D_MODEL)

    def pair(key, g, r1):
        return _pair_sum(g, r1, pos, f"grad_pair_sum_{key[0]}_{key[1]}")

    partial, r2 = {}, {}
    dg_mix, dg_group, dg_mlp, dconv, dsinks = [None] * DEPTH, [None] * DEPTH, [None] * DEPTH, [None] * DEPTH, [None] * DEPTH
    for l in reversed(range(DEPTH)):
        xin, z, h, ya, lse_a, yb, yc, lse_c, sink_l, y, x1, a, h2 = saved[l]
        if l + 1 < DEPTH:
            late = [(l + 1, 1), (l + 1, 0)]
            (du,), got = _mlp_bwd_act(dxb, full[l, 3], a, f"ff_out_bwd_{l}", _chip_comm([partial[k] for k in late]))
            r2.update(zip(late, got))
        (g3, g2), _ = _mm_tn([(a, dxb), (du, h2)], f"grad_w_ff_{l}")
        g3, g2 = by_owner(g3), by_owner(g2)
        (dx1, dx1b, dg_mlp[l], dya, dyb, dyc, dg_group[l]), got = _ff_in_mix_bwd(
            du, full[l, 2], x1, dx, g_mlp[l:l + 1], full[l, 1], ya, yb, yc, g_group[l:l + 1],
            f"ff_in_mix_bwd_{l}", _sibling_comm([g3, g2]))
        partial[l, 3], partial[l, 2] = pair((l, 3), g3, got[0]), pair((l, 2), g2, got[1])
        early = [(l, 3), (l, 2)]
        (dz, _), got = _attn_bwd(z, dya, ya, lse_a, no_sink, None, QA_BLK, KA_BLK, VA_BLK, DILATED_PATTERNS,
                                 A_MAX_DIST, False, f"dilated_attn_bwd_{l}", _chip_comm([partial[k] for k in early]))
        r2.update(zip(early, got))
        dz, dcw = _conv_bwd(z, conv_full[l], dyb, dz, f"conv_bwd_{l}")
        (dz, dsink), _ = _attn_bwd(z, dyc, yc, lse_c, sink_l, dz, QC_BLK, KC_BLK, VC_BLK, (1,), C_MAX_DIST,
                                   True, f"window_attn_bwd_{l}")
        (g1, g0), _ = _mm_tn([(y, dx1b), (dz, h)], f"grad_w_o_in_{l}")
        g1, g0 = by_owner(g1), by_owner(g0)
        if l > 0:
            (dx, dxb, dg_mix[l]), got = _mm_nn_normbwd(dz, full[l, 0], xin, dx1, g_mix[l:l + 1], f"in_proj_bwd_{l}",
                                                      _sibling_comm([g1, g0]))
            partial[l, 1], partial[l, 0] = pair((l, 1), g1, got[0]), pair((l, 0), g0, got[1])
        else:
            got = _comm_only(_sibling_comm([g1, g0]), "grad_sibling_exchange_last")
            partial[l, 1], partial[l, 0] = pair((l, 1), g1, got[0]), pair((l, 0), g0, got[1])
            (dx, dxb, dg_mix[l]), got = _mm_nn_normbwd(dz, full[l, 0], xin, dx1, g_mix[l:l + 1], f"in_proj_bwd_{l}",
                                                      _chip_comm([partial[l, 1], partial[l, 0]]))
            r2[l, 1], r2[l, 0] = got
        dconv[l] = dcw[:3]
        dsinks[l] = dsink[0, ::HEAD_DIM]
    parts = {key: (partial[key], r2[key]) for key in partial}
    small = _pack_small(jnp.concatenate(dg_mix), jnp.concatenate(dg_group), jnp.concatenate(dg_mlp),
                        dg_final, jnp.stack(dconv), jnp.stack(dsinks), loss_slab[0:1])
    return dx, parts, small


def _finish(dx, parts, small, pos, dev, w_in, conv_w, sinks, g_mix, g_group, w_o, g_mlp, w_ff_in, w_ff_out, g_final, m_w_in, m_conv_w, m_sinks, m_g_mix, m_g_group, m_w_o, m_g_mlp, m_w_ff_in, m_w_ff_out, m_g_final, v_w_in, v_conv_w, v_sinks, v_g_mix, v_g_group, v_w_o, v_g_mlp, v_w_ff_in, v_w_ff_out, v_g_final):
    grad_x = dx.reshape(1, SEQ, D_MODEL)

    (small_all,) = _comm_only(_gather_comm([small]), "gather_small_grads")
    row = lambda t: t.reshape(1, D_MODEL)
    sink_row = lambda t: _pad_rows(t.reshape(1, DEPTH * 6), 1)
    params = [(MIX_ROW, g_mix, m_g_mix, v_g_mix), (GROUP_ROW, g_group, m_g_group, v_g_group),
              (MLP_ROW, g_mlp, m_g_mlp, v_g_mlp), (FINAL_ROW, row(g_final), row(m_g_final), row(v_g_final)),
              (SINK_ROW, sink_row(sinks), sink_row(m_sinks), sink_row(v_sinks))]
    updated, conv_rows, loss_row = _small_sum_adamw(small_all.reshape(N_DEV, SMALL_ROWS, D_MODEL), params, "small_adamw")
    loss = loss_row[0, 0]
    (grad_g_mix, delta_g_mix, new_m_g_mix, new_v_g_mix), (grad_g_group, delta_g_group, new_m_g_group, new_v_g_group), \
        (grad_g_mlp, delta_g_mlp, new_m_g_mlp, new_v_g_mlp), final4, sinks4 = updated
    grad_g_final, delta_g_final, new_m_g_final, new_v_g_final = [t.reshape(D_MODEL) for t in final4]
    grad_sinks, delta_sinks, new_m_sinks, new_v_sinks = [t[0, :DEPTH * 6].reshape(DEPTH, 2, 3) for t in sinks4]
    conv_grad_full = conv_rows[:DEPTH * 3, :CONV_CH].reshape(DEPTH, 3, CONV_CH)
    cs = CONV_CH // N_DEV
    grad_conv_w = lax.dynamic_slice_in_dim(conv_grad_full, dev * cs, cs, axis=2)

    def tile_of(t):
        return jnp.pad(t.reshape(1, DEPTH * 3 * cs), ((0, 7), (0, 256 - DEPTH * 3 * cs)))

    cd, cm, cv = _adamw(tile_of(conv_w), tile_of(grad_conv_w), tile_of(m_conv_w), tile_of(v_conv_w), "conv_adamw")
    untile = lambda t: t[0, :DEPTH * 3 * cs].reshape(DEPTH, 3, cs)
    delta_conv_w, new_m_conv_w, new_v_conv_w = untile(cd), untile(cm), untile(cv)

    def big(kind, w, m, v, transpose, name):
        return _sum_adamw([parts[l, kind] for l in range(DEPTH)], w, m, v, pos, transpose, name)

    grad_w_in, delta_w_in, new_m_w_in, new_v_w_in = big(0, w_in, m_w_in, v_w_in, True, "adamw_w_in")
    grad_w_o, delta_w_o, new_m_w_o, new_v_w_o = big(1, w_o, m_w_o, v_w_o, False, "adamw_w_o")
    grad_w_ff_in, delta_w_ff_in, new_m_w_ff_in, new_v_w_ff_in = big(2, w_ff_in, m_w_ff_in, v_w_ff_in, True, "adamw_w_ff_in")
    grad_w_ff_out, delta_w_ff_out, new_m_w_ff_out, new_v_w_ff_out = big(3, w_ff_out, m_w_ff_out, v_w_ff_out, False,
                                                                         "adamw_w_ff_out")

    return (loss, grad_x, grad_w_in, grad_conv_w, grad_sinks, grad_g_mix, grad_g_group, grad_w_o, grad_g_mlp,
            grad_w_ff_in, grad_w_ff_out, grad_g_final,
            delta_w_in, delta_conv_w, delta_sinks, delta_g_mix, delta_g_group, delta_w_o, delta_g_mlp,
            delta_w_ff_in, delta_w_ff_out, delta_g_final,
            new_m_w_in, new_m_conv_w, new_m_sinks, new_m_g_mix, new_m_g_group, new_m_w_o, new_m_g_mlp,
            new_m_w_ff_in, new_m_w_ff_out, new_m_g_final,
            new_v_w_in, new_v_conv_w, new_v_sinks, new_v_g_mix, new_v_g_group, new_v_w_o, new_v_g_mlp,
            new_v_w_ff_in, new_v_w_ff_out, new_v_g_final)
```

```python
from typing import Callable, NamedTuple

import jax
import jax.numpy as jnp
from jax import lax
from jax.experimental import pallas as pl
from jax.experimental.pallas import tpu as pltpu

F32 = jnp.float32
BF16 = jnp.bfloat16
MESH = pl.DeviceIdType.MESH

N_DEV = 8
SEQ = 4096
D_MODEL = 1024
DEPTH = 2
HEAD_DIM = 64
LANES = 128
A_WIDTH = 384
CONV_CH = 256
C_WIDTH = 384
IN_WIDTH = 2560
BLOCK = 128
DILATED_PATTERNS = (1, 4, 16)
A_MAX_DIST = 128
C_MAX_DIST = 127
EPS = 1e-6
SCALE = HEAD_DIM ** -0.5
NEG_BIG = -1e30
F32_TINY = 1.1754944e-38

QA_BLK, KA_BLK, VA_BLK = 0, 3, 6
GB_BLK, GC_BLK, XB_BLK = 9, 11, 13
QC_BLK, KC_BLK, VC_BLK = 15, 18, 19

ADAM_LR = 0.001
ADAM_B1 = 0.9
ADAM_B2 = 0.999
ADAM_EPS = 1e-08
ADAM_WD = 0.01
ADAM_STEP = 10

VMEM_LIMIT = 56 * 1024 * 1024
TILE_BUDGET = 46 * 1024 * 1024
ROW_TILE = 512
COL_CHUNK = 512
SMALL_ROWS = 48


def _dot_nn(a, b):
    return lax.dot_general(a, b, (((1,), (0,)), ((), ())), preferred_element_type=F32)


def _dot_nt(a, b):
    return lax.dot_general(a, b, (((1,), (1,)), ((), ())), preferred_element_type=F32)


def _dot_tn(a, b):
    return lax.dot_general(a, b, (((0,), (0,)), ((), ())), preferred_element_type=F32)


def _params(*sem):
    return pltpu.CompilerParams(dimension_semantics=sem, vmem_limit_bytes=VMEM_LIMIT)


def _resident(shape):
    return pl.BlockSpec(shape, lambda i: (0,) * len(shape), pipeline_mode=pl.Buffered(1))


def _row_tile(row_bytes, resident_bytes):
    for tm in (ROW_TILE, ROW_TILE // 2):
        if 2 * tm * row_bytes + resident_bytes <= TILE_BUDGET:
            return tm
    return ROW_TILE // 4


def _rms_scale(t):
    return lax.rsqrt(jnp.mean(t * t, axis=-1, keepdims=True) + EPS)


def _rms_bwd(n, r, dn):
    return r * (dn - n * jnp.mean(dn * n, axis=-1, keepdims=True))


class _Comm(NamedTuple):
    arrays: tuple
    out_shape: tuple
    sems: tuple
    start: Callable
    finish: Callable


def _call(body, grid, in_specs, out_specs, out_shape, operands, name, scratch_shapes=(), comm=None, aliases=None):
    n_in, n_out, n_scr = len(in_specs), len(out_shape), len(scratch_shapes)
    aliases = dict(aliases or {})
    if comm is None:
        res = pl.pallas_call(body, grid=grid, in_specs=list(in_specs), out_specs=list(out_specs),
                             out_shape=list(out_shape), scratch_shapes=list(scratch_shapes),
                             input_output_aliases=aliases,
                             compiler_params=_params("arbitrary"), name=name)(*operands)
        return list(res), []
    c_in, c_out = len(comm.arrays), len(comm.out_shape)
    hbm = pl.BlockSpec(memory_space=pl.ANY)
    last = grid[0] - 1

    def carried(*refs):
        ins, cins = refs[:n_in], refs[n_in:n_in + c_in]
        o0 = n_in + c_in
        outs, couts = refs[o0:o0 + n_out], refs[o0 + n_out:o0 + n_out + c_out]
        s0 = o0 + n_out + c_out
        scr, sems = refs[s0:s0 + n_scr], refs[s0 + n_scr:]
        pl.when(pl.program_id(0) == 0)(lambda: comm.start(cins, couts, sems))
        body(*ins, *outs, *scr)
        pl.when(pl.program_id(0) == last)(lambda: comm.finish(cins, couts, sems))

    res = pl.pallas_call(carried, grid=grid, in_specs=list(in_specs) + [hbm] * c_in,
                         out_specs=list(out_specs) + [hbm] * c_out, out_shape=list(out_shape) + list(comm.out_shape),
                         scratch_shapes=list(scratch_shapes) + list(comm.sems), input_output_aliases=aliases,
                         compiler_params=_params("arbitrary"), name=name)(*operands, *comm.arrays)
    return list(res[:n_out]), list(res[n_out:])


def _comm_only(comm, name):
    hbm = pl.BlockSpec(memory_space=pl.ANY)
    c_in, c_out = len(comm.arrays), len(comm.out_shape)

    def body(*refs):
        ins, outs, sems = refs[:c_in], refs[c_in:c_in + c_out], refs[c_in + c_out:]
        comm.start(ins, outs, sems)
        comm.finish(ins, outs, sems)

    return pl.pallas_call(body, in_specs=[hbm] * c_in, out_specs=[hbm] * c_out, out_shape=list(comm.out_shape),
                          scratch_shapes=list(comm.sems), name=name)(*comm.arrays)


def _norm_mm(x, g, wt, name, comm=None):
    s, d = x.shape
    n = wt.shape[0]
    tm = _row_tile(4 * d + 4 * n + 2 * d, 2 * n * d)

    def body(x_ref, g_ref, w_ref, o_ref, h_ref):
        xx = x_ref[...]
        h = ((xx * _rms_scale(xx)) * g_ref[...]).astype(BF16)
        h_ref[...] = h
        for n0 in range(0, n, COL_CHUNK):
            o_ref[:, n0:n0 + COL_CHUNK] = _dot_nt(h, w_ref[n0:n0 + COL_CHUNK, :])

    return _call(
        body,
        grid=(s // tm,),
        in_specs=[pl.BlockSpec((tm, d), lambda i: (i, 0)),
                  pl.BlockSpec((1, d), lambda i: (0, 0)),
                  _resident((n, d))],
        out_specs=[pl.BlockSpec((tm, n), lambda i: (i, 0)),
                   pl.BlockSpec((tm, d), lambda i: (i, 0))],
        out_shape=[jax.ShapeDtypeStruct((s, n), F32), jax.ShapeDtypeStruct((s, d), BF16)],
        operands=(x, g, wt), name=name, comm=comm)


def _ff_out_in_proj(a, w2, x1, g, wt, name, comm=None):
    s, f = a.shape
    d = w2.shape[1]
    n = wt.shape[0]
    tm = _row_tile(2 * f + 4 * d + 4 * d + 4 * n + 2 * d, 2 * f * d + 2 * n * d)

    def body(a_ref, w2_ref, x_ref, g_ref, w_ref, x2_ref, z_ref, h_ref):
        x2 = x_ref[...] + _dot_nn(a_ref[...], w2_ref[...])
        x2_ref[...] = x2
        h = ((x2 * _rms_scale(x2)) * g_ref[...]).astype(BF16)
        h_ref[...] = h
        for n0 in range(0, n, COL_CHUNK):
            z_ref[:, n0:n0 + COL_CHUNK] = _dot_nt(h, w_ref[n0:n0 + COL_CHUNK, :])

    rows = lambda w: pl.BlockSpec((tm, w), lambda i: (i, 0))
    return _call(
        body,
        grid=(s // tm,),
        in_specs=[rows(f), _resident((f, d)), rows(d), pl.BlockSpec((1, d), lambda i: (0, 0)), _resident((n, d))],
        out_specs=[rows(d), rows(n), rows(d)],
        out_shape=[jax.ShapeDtypeStruct((s, d), F32), jax.ShapeDtypeStruct((s, n), F32),
                   jax.ShapeDtypeStruct((s, d), BF16)],
        operands=(a, w2, x1, g, wt), name=name, comm=comm)


def _mix_ff_in(ya, yb, yc, gg, wo, x0, g_mlp, wt1, name, comm=None):
    s = ya.shape[0]
    d = wo.shape[1]
    f = wt1.shape[0]
    tm = _row_tile(4 * d + 4 * d + 2 * d + 4 * d + 2 * d + 2 * f, 2 * d * d + 2 * f * d)

    def body(ya_ref, yb_ref, yc_ref, gg_ref, wo_ref, x_ref, g_ref, w1_ref, y_ref, x1_ref, a_ref, h_ref):
        parts = []
        for ref in (ya_ref, yb_ref, yc_ref):
            t = ref[...]
            parts.append(t * _rms_scale(t))
        y = (jnp.concatenate(parts, axis=1) * gg_ref[...]).astype(BF16)
        y_ref[...] = y
        x1 = x_ref[...] + _dot_nn(y, wo_ref[...])
        x1_ref[...] = x1
        h = ((x1 * _rms_scale(x1)) * g_ref[...]).astype(BF16)
        h_ref[...] = h
        for n0 in range(0, f, COL_CHUNK):
            u = _dot_nt(h, w1_ref[n0:n0 + COL_CHUNK, :])
            a_ref[:, n0:n0 + COL_CHUNK] = jnp.square(jnp.maximum(u, 0.0)).astype(BF16)

    rows = lambda w: pl.BlockSpec((tm, w), lambda i: (i, 0))
    vec = pl.BlockSpec((1, d), lambda i: (0, 0))
    return _call(
        body,
        grid=(s // tm,),
        in_specs=[rows(A_WIDTH), rows(CONV_CH), rows(C_WIDTH), vec, _resident((d, d)), rows(d), vec, _resident((f, d))],
        out_specs=[rows(d), rows(d), rows(f), rows(d)],
        out_shape=[jax.ShapeDtypeStruct((s, d), BF16), jax.ShapeDtypeStruct((s, d), F32),
                   jax.ShapeDtypeStruct((s, f), BF16), jax.ShapeDtypeStruct((s, d), BF16)],
        operands=(ya, yb, yc, gg, wo, x0, g_mlp, wt1), name=name, comm=comm)


def _relu_from_square(av):
    return av * lax.rsqrt(jnp.maximum(av, F32_TINY))


def _mm_res_loss(a, w2, x1, g, target, name):
    s, f = a.shape
    d = w2.shape[1]
    tm = _row_tile(2 * f + 4 * d + 4 * d + 4 * d + 2 * d + 2 * f, 2 * f * d)

    def body(a_ref, w_ref, x_ref, g_ref, t_ref, loss_ref, dx_ref, dxb_ref, dg_ref, du_ref):
        @pl.when(pl.program_id(0) == 0)
        def _():
            loss_ref[...] = jnp.zeros_like(loss_ref)
            dg_ref[...] = jnp.zeros_like(dg_ref)

        xx = x_ref[...] + _dot_nn(a_ref[...], w_ref[...])
        r = _rms_scale(xx)
        n = xx * r
        gv = g_ref[...]
        err = n * gv - t_ref[...]
        per_tok = jnp.sum(err * err, axis=1, keepdims=True) * (1.0 / d)
        loss_ref[...] += 0.5 * jnp.sum(per_tok, axis=0, keepdims=True)
        dout = err * (1.0 / d)
        dg_ref[...] += jnp.sum(dout * n, axis=0, keepdims=True)
        dx = _rms_bwd(n, r, dout * gv)
        dx_ref[...] = dx
        dxb = dx.astype(BF16)
        dxb_ref[...] = dxb
        for n0 in range(0, f, COL_CHUNK):
            da = _dot_nt(dxb, w_ref[n0:n0 + COL_CHUNK, :])
            rl = _relu_from_square(a_ref[:, n0:n0 + COL_CHUNK].astype(F32))
            du_ref[:, n0:n0 + COL_CHUNK] = (da * (2.0 * rl)).astype(BF16)

    rows = lambda w: pl.BlockSpec((tm, w), lambda i: (i, 0))
    vec = pl.BlockSpec((1, d), lambda i: (0, 0))
    return pl.pallas_call(
        body,
        grid=(s // tm,),
        in_specs=[rows(f), _resident((f, d)), rows(d), vec, rows(d)],
        out_specs=[pl.BlockSpec((8, LANES), lambda i: (0, 0)), rows(d), rows(d), vec, rows(f)],
        out_shape=[jax.ShapeDtypeStruct((8, LANES), F32), jax.ShapeDtypeStruct((s, d), F32),
                   jax.ShapeDtypeStruct((s, d), BF16), jax.ShapeDtypeStruct((1, d), F32),
                   jax.ShapeDtypeStruct((s, f), BF16)],
        compiler_params=_params("arbitrary"),
        name=name,
    )(a, w2, x1, g, target)


def _mlp_bwd_act(dxb, w2, a, name, comm=None):
    s, d = dxb.shape
    f = w2.shape[0]
    tm = _row_tile(2 * d + 2 * f + 2 * f, 2 * f * d)

    def body(dx_ref, w_ref, a_ref, du_ref):
        dx = dx_ref[...]
        for n0 in range(0, f, COL_CHUNK):
            da = _dot_nt(dx, w_ref[n0:n0 + COL_CHUNK, :])
            rl = _relu_from_square(a_ref[:, n0:n0 + COL_CHUNK].astype(F32))
            du_ref[:, n0:n0 + COL_CHUNK] = (da * (2.0 * rl)).astype(BF16)

    return _call(
        body,
        grid=(s // tm,),
        in_specs=[pl.BlockSpec((tm, d), lambda i: (i, 0)),
                  _resident((f, d)),
                  pl.BlockSpec((tm, f), lambda i: (i, 0))],
        out_specs=[pl.BlockSpec((tm, f), lambda i: (i, 0))],
        out_shape=[jax.ShapeDtypeStruct((s, f), BF16)],
        operands=(dxb, w2, a), name=name, comm=comm)


def _mm_tn(pairs, name, comm=None):
    s, d = pairs[0][1].shape
    tn = 512
    tiles = [a.shape[1] // tn for a, _ in pairs]
    starts = [sum(tiles[:k]) for k in range(len(pairs))]

    def body(*refs):
        ins, outs, acc = refs[:2 * len(pairs)], refs[2 * len(pairs):3 * len(pairs)], refs[3 * len(pairs)]
        j = pl.program_id(0)
        for k in range(len(pairs)):
            def run(a_ref=ins[2 * k], b_ref=ins[2 * k + 1], o_ref=outs[k]):
                for k0 in range(0, s, ROW_TILE):
                    part = _dot_tn(a_ref[k0:k0 + ROW_TILE, :], b_ref[k0:k0 + ROW_TILE, :])
                    if k0 == 0:
                        acc[...] = part
                    else:
                        acc[...] += part
                o_ref[...] = acc[...].astype(BF16)

            pl.when((j >= starts[k]) & (j < starts[k] + tiles[k]))(run)

    def tile_of(k):
        return lambda j: jnp.clip(j - starts[k], 0, tiles[k] - 1)

    in_specs, out_specs = [], []
    for k in range(len(pairs)):
        in_specs += [pl.BlockSpec((s, tn), lambda j, t=tile_of(k): (0, t(j))), _resident((s, d))]
        out_specs.append(pl.BlockSpec((tn, d), lambda j, t=tile_of(k): (t(j), 0)))
    return _call(
        body,
        grid=(sum(tiles),),
        in_specs=in_specs,
        out_specs=out_specs,
        out_shape=[jax.ShapeDtypeStruct((a.shape[1], d), BF16) for a, _ in pairs],
        operands=tuple(t for pair in pairs for t in pair), name=name,
        scratch_shapes=[pltpu.VMEM((tn, d), F32)], comm=comm)


def _mm_nn_normbwd(dact, wt, x, dres, g, name, comm=None):
    s, kdim = dact.shape
    d = wt.shape[1]
    tm = _row_tile(2 * kdim + 4 * d + 4 * d + 4 * d + 2 * d, 2 * kdim * d)

    def body(a_ref, w_ref, x_ref, r_ref, g_ref, o_ref, ob_ref, dg_ref):
        @pl.when(pl.program_id(0) == 0)
        def _():
            dg_ref[...] = jnp.zeros_like(dg_ref)

        dh = _dot_nn(a_ref[...], w_ref[...])
        xx = x_ref[...]
        r = _rms_scale(xx)
        n = xx * r
        dg_ref[...] += jnp.sum(dh * n, axis=0, keepdims=True)
        dx = r_ref[...] + _rms_bwd(n, r, dh * g_ref[...])
        o_ref[...] = dx
        ob_ref[...] = dx.astype(BF16)

    return _call(
        body,
        grid=(s // tm,),
        in_specs=[pl.BlockSpec((tm, kdim), lambda i: (i, 0)),
                  _resident((kdim, d)),
                  pl.BlockSpec((tm, d), lambda i: (i, 0)),
                  pl.BlockSpec((tm, d), lambda i: (i, 0)),
                  pl.BlockSpec((1, d), lambda i: (0, 0))],
        out_specs=[pl.BlockSpec((tm, d), lambda i: (i, 0)),
                   pl.BlockSpec((tm, d), lambda i: (i, 0)),
                   pl.BlockSpec((1, d), lambda i: (0, 0))],
        out_shape=[jax.ShapeDtypeStruct((s, d), F32), jax.ShapeDtypeStruct((s, d), BF16),
                   jax.ShapeDtypeStruct((1, d), F32)],
        operands=(dact, wt, x, dres, g), name=name, comm=comm)


def _ff_in_mix_bwd(du, wt1, x1, dres, g_mlp, wo, ya, yb, yc, gg, name, comm=None):
    s, f = du.shape
    d = wt1.shape[1]
    widths = (A_WIDTH, CONV_CH, C_WIDTH)
    tm = _row_tile(2 * f + 4 * d + 4 * d + 4 * d + 2 * d + 4 * d + 4 * d, 2 * f * d + 2 * d * d)

    def body(du_ref, w1_ref, x_ref, r_ref, g_ref, wo_ref, ya_ref, yb_ref, yc_ref, gg_ref,
             dx_ref, dxb_ref, dg_ref, da_ref, db_ref, dc_ref, dgg_ref):
        @pl.when(pl.program_id(0) == 0)
        def _():
            dg_ref[...] = jnp.zeros_like(dg_ref)
            dgg_ref[...] = jnp.zeros_like(dgg_ref)

        dh = _dot_nn(du_ref[...], w1_ref[...])
        xx = x_ref[...]
        r = _rms_scale(xx)
        n = xx * r
        dg_ref[...] += jnp.sum(dh * n, axis=0, keepdims=True)
        dx = r_ref[...] + _rms_bwd(n, r, dh * g_ref[...])
        dx_ref[...] = dx
        dxb = dx.astype(BF16)
        dxb_ref[...] = dxb

        dy = _dot_nt(dxb, wo_ref[...])
        gv = gg_ref[...]
        off = 0
        dgs = []
        for ref, out, w in zip((ya_ref, yb_ref, yc_ref), (da_ref, db_ref, dc_ref), widths):
            t = ref[...]
            r = _rms_scale(t)
            n = t * r
            dyg = dy[:, off:off + w]
            dgs.append(jnp.sum(dyg * n, axis=0, keepdims=True))
            out[...] = _rms_bwd(n, r, dyg * gv[:, off:off + w])
            off += w
        dgg_ref[...] += jnp.concatenate(dgs, axis=1)

    rows = lambda w: pl.BlockSpec((tm, w), lambda i: (i, 0))
    vec = pl.BlockSpec((1, d), lambda i: (0, 0))
    return _call(
        body,
        grid=(s // tm,),
        in_specs=[rows(f), _resident((f, d)), rows(d), rows(d), vec, _resident((d, d)),
                  rows(A_WIDTH), rows(CONV_CH), rows(C_WIDTH), vec],
        out_specs=[rows(d), rows(d), vec, rows(A_WIDTH), rows(CONV_CH), rows(C_WIDTH), vec],
        out_shape=[jax.ShapeDtypeStruct((s, d), F32), jax.ShapeDtypeStruct((s, d), BF16), jax.ShapeDtypeStruct((1, d), F32),
                   jax.ShapeDtypeStruct((s, A_WIDTH), F32), jax.ShapeDtypeStruct((s, CONV_CH), F32),
                   jax.ShapeDtypeStruct((s, C_WIDTH), F32), jax.ShapeDtypeStruct((1, d), F32)],
        operands=(du, wt1, x1, dres, g_mlp, wo, ya, yb, yc, gg), name=name, comm=comm)


CONV_CHUNK = 256
CONV_HALO = 8


def _conv_fwd(z, cw, name):
    s = z.shape[0]
    nch = s // CONV_CHUNK

    def body(gb_ref, gc_ref, xb_ref, w_ref, o_ref, us):
        us[pl.ds(0, CONV_HALO), :] = jnp.zeros((CONV_HALO, LANES), F32)
        us[pl.ds(CONV_HALO, s), :] = gc_ref[...] * xb_ref[...]
        w0, w1, w2 = w_ref[0:1, :], w_ref[1:2, :], w_ref[2:3, :]

        def chunk(c, carry):
            st = pl.multiple_of(c * CONV_CHUNK, CONV_CHUNK)
            ext = us[pl.ds(st, CONV_CHUNK + CONV_HALO), :]
            y = (w0 * ext[CONV_HALO - 2:CONV_HALO - 2 + CONV_CHUNK]
                 + w1 * ext[CONV_HALO - 1:CONV_HALO - 1 + CONV_CHUNK]
                 + w2 * ext[CONV_HALO:])
            o_ref[pl.ds(st, CONV_CHUNK), :] = gb_ref[pl.ds(st, CONV_CHUNK), :] * y
            return carry

        lax.fori_loop(0, nch, chunk, 0)

    col = lambda blk: pl.BlockSpec((s, LANES), lambda j, blk=blk: (0, blk + j))
    return pl.pallas_call(
        body,
        grid=(CONV_CH // LANES,),
        in_specs=[col(GB_BLK), col(GC_BLK), col(XB_BLK), pl.BlockSpec((3, LANES), lambda j: (0, j))],
        out_specs=pl.BlockSpec((s, LANES), lambda j: (0, j)),
        out_shape=jax.ShapeDtypeStruct((s, CONV_CH), F32),
        scratch_shapes=[pltpu.VMEM((s + CONV_HALO, LANES), F32)],
        compiler_params=_params("parallel"),
        name=name,
    )(z, z, z, cw)


def _conv_bwd(z, cw, dyb, dz, name):
    s = z.shape[0]
    nch = s // CONV_CHUNK
    ncol = CONV_CH // LANES

    def body(gb_ref, gc_ref, xb_ref, w_ref, dy_ref, dz_in, dz_ref, dw_ref, us, ds_, dgb_ref, dgc_ref, dxb_ref, sems):
        j = pl.program_id(0)

        def to_dz(staged, blk, k):
            cols = pl.ds(pl.multiple_of((blk + j) * LANES, LANES), LANES)
            return pltpu.make_async_copy(staged, dz_ref.at[:, cols], sems.at[k])

        copies = [to_dz(dgb_ref, GB_BLK, 0), to_dz(dgc_ref, GC_BLK, 1), to_dz(dxb_ref, XB_BLK, 2)]

        @pl.when(j > 0)
        def _():
            for cp in copies:
                cp.wait()

        us[pl.ds(0, CONV_HALO), :] = jnp.zeros((CONV_HALO, LANES), F32)
        us[pl.ds(CONV_HALO, s), :] = gc_ref[...] * xb_ref[...]
        ds_[pl.ds(s, CONV_HALO), :] = jnp.zeros((CONV_HALO, LANES), F32)
        ds_[pl.ds(0, s), :] = dy_ref[...] * gb_ref[...]
        w0, w1, w2 = w_ref[0:1, :], w_ref[1:2, :], w_ref[2:3, :]
        zero = jnp.zeros((1, LANES), F32)

        def chunk(c, carry):
            a0, a1, a2 = carry
            st = pl.multiple_of(c * CONV_CHUNK, CONV_CHUNK)
            rows = pl.ds(st, CONV_CHUNK)
            ext = us[pl.ds(st, CONV_CHUNK + CONV_HALO), :]
            um2 = ext[CONV_HALO - 2:CONV_HALO - 2 + CONV_CHUNK]
            um1 = ext[CONV_HALO - 1:CONV_HALO - 1 + CONV_CHUNK]
            u0 = ext[CONV_HALO:]
            dext = ds_[pl.ds(st, CONV_CHUNK + CONV_HALO), :]
            dc0 = dext[:CONV_CHUNK]
            du = w2 * dc0 + w1 * dext[1:1 + CONV_CHUNK] + w0 * dext[2:2 + CONV_CHUNK]
            yconv = w0 * um2 + w1 * um1 + w2 * u0
            dgb_ref[rows, :] = (dy_ref[rows, :] * yconv).astype(BF16)
            dgc_ref[rows, :] = (du * xb_ref[rows, :]).astype(BF16)
            dxb_ref[rows, :] = (du * gc_ref[rows, :]).astype(BF16)
            a0 = a0 + jnp.sum(dc0 * um2, axis=0, keepdims=True)
            a1 = a1 + jnp.sum(dc0 * um1, axis=0, keepdims=True)
            a2 = a2 + jnp.sum(dc0 * u0, axis=0, keepdims=True)
            return a0, a1, a2

        a0, a1, a2 = lax.fori_loop(0, nch, chunk, (zero, zero, zero))
        dw_ref[...] = jnp.concatenate([a0, a1, a2, jnp.zeros((5, LANES), F32)], axis=0)
        for cp in copies:
            cp.start()

        @pl.when(j == ncol - 1)
        def _():
            for cp in copies:
                cp.wait()

    col = lambda blk: pl.BlockSpec((s, LANES), lambda j, blk=blk: (0, blk + j))
    hbm = pl.BlockSpec(memory_space=pl.ANY)
    return pl.pallas_call(
        body,
        grid=(ncol,),
        in_specs=[col(GB_BLK), col(GC_BLK), col(XB_BLK), pl.BlockSpec((3, LANES), lambda j: (0, j)),
                  pl.BlockSpec((s, LANES), lambda j: (0, j)), hbm],
        out_specs=[hbm, pl.BlockSpec((8, LANES), lambda j: (0, j))],
        out_shape=[jax.ShapeDtypeStruct(dz.shape, dz.dtype), jax.ShapeDtypeStruct((8, CONV_CH), F32)],
        scratch_shapes=[pltpu.VMEM((s + CONV_HALO, LANES), F32), pltpu.VMEM((s + CONV_HALO, LANES), F32)]
        + [pltpu.VMEM((s, LANES), BF16)] * 3 + [pltpu.SemaphoreType.DMA((3,))],
        input_output_aliases={5: 0},
        compiler_params=_params("arbitrary"),
        name=name,
    )(z, z, z, cw, dyb, dz)


ATTN_ROWS = 512
ATTN_UNROLL = 8


def _band_rows(b, d, r):
    base = pl.multiple_of(b * (BLOCK * d), BLOCK)
    prev = jnp.maximum(base - BLOCK * d, 0)
    if d == 1:
        return pl.ds(base, BLOCK), pl.ds(pl.multiple_of(prev, BLOCK), BLOCK)
    return pl.ds(base + r, BLOCK, stride=d), pl.ds(prev + r, BLOCK, stride=d)


def _write_band_bias(bias_ref, max_dist):
    qi = lax.broadcasted_iota(jnp.int32, (BLOCK, 2 * BLOCK), 0)
    kj = lax.broadcasted_iota(jnp.int32, (BLOCK, 2 * BLOCK), 1)
    dist = BLOCK + qi - kj
    band = (dist >= 0) & (dist <= max_dist)
    bias_ref[0:BLOCK, :] = jnp.where(band, 0.0, -jnp.inf)
    bias_ref[BLOCK:2 * BLOCK, :] = jnp.where(band & (kj >= BLOCK), 0.0, -jnp.inf)


def _band_bias(bias_ref, b):
    bias = bias_ref[pl.ds(pl.multiple_of(jnp.where(b > 0, 0, BLOCK), BLOCK), BLOCK), :]
    return jnp.concatenate([bias, bias], axis=0)


def _kv_halves(pair):
    zero = jnp.zeros((1, LANES), jnp.int32)
    return zero + (pair >> 1), zero + ((pair + 1) >> 1)


def _stack_heads(t, head0, halves=None):
    top, bottom = jnp.where(head0, t, 0.0), jnp.where(head0, 0.0, t)
    if halves is not None:
        top = jnp.where(halves[0] == 1, pltpu.roll(top, HEAD_DIM, 1), top)
        bottom = jnp.where(halves[1] == 0, pltpu.roll(bottom, HEAD_DIM, 1), bottom)
    return jnp.concatenate([top, bottom], axis=0).astype(BF16)


def _unstack_heads(t, head0, halves=None):
    top, bottom = t[:BLOCK], t[BLOCK:]
    if halves is not None:
        top = jnp.where(halves[0] == 1, pltpu.roll(top, HEAD_DIM, 1), top)
        bottom = jnp.where(halves[1] == 0, pltpu.roll(bottom, HEAD_DIM, 1), bottom)
    return jnp.where(head0, top, bottom)


def _block_loops(s, patterns, unroll, one_block):
    for n, d in enumerate(patterns):
        nb = (s // BLOCK) // d
        ur = min(unroll, d)
        ub = unroll // ur
        for r0 in range(0, d, ur):
            def trip(i, carry, n=n, d=d, r0=r0, ur=ur, ub=ub):
                for u in range(ub):
                    for r in range(r0, r0 + ur):
                        one_block(i * ub + u, d, r, n == 0)
                return carry
            lax.fori_loop(0, nb // ub, trip, 0)


def _attn_fwd(z, m_init, l_init, q_blk, k_blk, v_blk, patterns, max_dist, gqa, name, comm=None):
    s = z.shape[0]
    npair = 3

    def body(q_ref, k_ref, v_ref, mi_ref, o_ref, lse0_ref, lse1_ref, bias_scr, m_scr, l_scr, *kv_scr):
        head0 = lax.broadcasted_iota(jnp.int32, (1, LANES), 1) < HEAD_DIM
        _write_band_bias(bias_scr, max_dist)
        ones = jnp.ones((2 * BLOCK, LANES), BF16)
        k_src, v_src = kv_scr if gqa else (k_ref, v_ref)
        if gqa:
            half = (lax.broadcasted_iota(jnp.int32, (1, LANES), 1) >= HEAD_DIM).astype(jnp.int32)
            swap = ((pl.program_id(0) + half) >> 1) != half

            def expand(c, carry):
                rows = pl.ds(pl.multiple_of(c * ATTN_ROWS, ATTN_ROWS), ATTN_ROWS)
                k_src[rows, :] = jnp.where(swap, pltpu.roll(k_ref[rows, :], HEAD_DIM, 1), k_ref[rows, :])
                v_src[rows, :] = jnp.where(swap, pltpu.roll(v_ref[rows, :], HEAD_DIM, 1), v_ref[rows, :])
                return carry

            lax.fori_loop(0, s // ATTN_ROWS, expand, 0)

        def one_block(b, d, r, first):
            rq, rp = _band_rows(b, d, r)
            q2 = _stack_heads(q_ref[rq, :] * SCALE, head0)
            k2 = jnp.concatenate([k_src[rp, :], k_src[rq, :]], axis=0).astype(BF16)
            v2 = jnp.concatenate([v_src[rp, :], v_src[rq, :]], axis=0).astype(BF16)
            sc = _dot_nt(q2, k2) + _band_bias(bias_scr, b)
            mb = jnp.max(sc, axis=1, keepdims=True)
            p = jnp.exp(sc - mb).astype(BF16)
            ob = _dot_nn(p, jnp.concatenate([v2, ones], axis=1))
            m_blk = _unstack_heads(jnp.broadcast_to(mb, (2 * BLOCK, LANES)), head0)
            l_blk = _unstack_heads(ob[:, LANES:], head0)
            o_blk = _unstack_heads(ob[:, :LANES], head0)
            if first and l_init == 0.0:
                m_new, l_new, o_new = m_blk, l_blk, o_blk
            else:
                if first:
                    m_old, l_old, o_old = jnp.broadcast_to(mi_ref[...], (BLOCK, LANES)), l_init, 0.0
                else:
                    m_old, l_old, o_old = m_scr[rq, :], l_scr[rq, :], o_ref[rq, :]
                m_new = jnp.maximum(m_old, m_blk)
                a_old = jnp.exp(m_old - m_new)
                a_blk = jnp.exp(m_blk - m_new)
                l_new = l_old * a_old + l_blk * a_blk
                o_new = o_old * a_old + o_blk * a_blk
            o_ref[rq, :], l_scr[rq, :], m_scr[rq, :] = o_new, l_new, m_new

        _block_loops(s, patterns, ATTN_UNROLL, one_block)

        def fin(c, carry):
            rows = pl.ds(pl.multiple_of(c * ATTN_ROWS, ATTN_ROWS), ATTN_ROWS)
            l = l_scr[rows, :]
            o_ref[rows, :] = o_ref[rows, :] / l
            lse = m_scr[rows, :] + jnp.log(l)
            swapped = pltpu.roll(lse, HEAD_DIM, 1)
            lse0_ref[rows, :] = jnp.where(head0, lse, swapped)
            lse1_ref[rows, :] = jnp.where(head0, swapped, lse)
            return carry

        lax.fori_loop(0, s // ATTN_ROWS, fin, 0)

    kv = (lambda blk: pl.BlockSpec((s, LANES), lambda j, blk=blk: (0, blk), pipeline_mode=pl.Buffered(1))) if gqa \
        else (lambda blk: pl.BlockSpec((s, LANES), lambda j, blk=blk: (0, blk + j)))
    own = pl.BlockSpec((s, LANES), lambda j: (0, j))
    return _call(
        body,
        grid=(npair,),
        in_specs=[pl.BlockSpec((s, LANES), lambda j: (0, q_blk + j)), kv(k_blk), kv(v_blk),
                  pl.BlockSpec((1, LANES), lambda j: (0, j))],
        out_specs=[own, own, own],
        out_shape=[jax.ShapeDtypeStruct((s, npair * LANES), F32)] * 3,
        operands=(z, z, z, m_init), name=name,
        scratch_shapes=[pltpu.VMEM((2 * BLOCK, 2 * BLOCK), F32)] + [pltpu.VMEM((s, LANES), F32)] * (4 if gqa else 2),
        comm=comm)


def _attn_bwd(z, do, o, lse, m_init, dz, q_blk, k_blk, v_blk, patterns, max_dist, gqa, name, comm=None):
    s = z.shape[0]
    npair = 3
    n_dz_in = 0 if dz is None else 1

    def body(q_ref, k_ref, v_ref, do_ref, o_ref, lse0_ref, lse1_ref, mi_ref, *rest):
        (dz_ref, dm_ref, dq_acc, dk_acc, dv_acc, dl0_scr, dl1_scr, bias_scr,
         dq_out, dk_out, dv_out, out_sems) = rest[n_dz_in:]
        pair = pl.program_id(0)
        head0 = lax.broadcasted_iota(jnp.int32, (1, LANES), 1) < HEAD_DIM
        halves = _kv_halves(pair) if gqa else None
        _write_band_bias(bias_scr, max_dist)

        def zero_kv():
            def f(c, carry):
                rows = pl.ds(pl.multiple_of(c * ATTN_ROWS, ATTN_ROWS), ATTN_ROWS)
                dk_acc[rows, :] = jnp.zeros((ATTN_ROWS, LANES), F32)
                dv_acc[rows, :] = jnp.zeros((ATTN_ROWS, LANES), F32)
                return carry
            lax.fori_loop(0, s // ATTN_ROWS, f, 0)

        if gqa:
            pl.when(pair == 0)(zero_kv)
        else:
            zero_kv()

        def prep(c, dm):
            rows = pl.ds(pl.multiple_of(c * ATTN_ROWS, ATTN_ROWS), ATTN_ROWS)
            dq_acc[rows, :] = jnp.zeros((ATTN_ROWS, LANES), F32)
            prod = do_ref[rows, :] * o_ref[rows, :]
            d0 = jnp.sum(jnp.where(head0, prod, 0.0), axis=1, keepdims=True)
            d1 = jnp.sum(jnp.where(head0, 0.0, prod), axis=1, keepdims=True)
            dl0_scr[rows, :] = jnp.broadcast_to(d0, (ATTN_ROWS, LANES))
            dl1_scr[rows, :] = jnp.broadcast_to(d1, (ATTN_ROWS, LANES))
            lse_own = jnp.where(head0, lse0_ref[rows, :], lse1_ref[rows, :])
            psink = jnp.exp(mi_ref[...] - lse_own)
            return dm - jnp.sum(psink * jnp.where(head0, d0, d1), axis=0, keepdims=True)

        dm_ref[...] = lax.fori_loop(0, s // ATTN_ROWS, prep, jnp.zeros((1, LANES), F32))

        def one_block(b, d, r, first):
            rq, rp = _band_rows(b, d, r)
            q2 = _stack_heads(q_ref[rq, :] * SCALE, head0, halves)
            do2 = _stack_heads(do_ref[rq, :], head0, halves)
            k2 = jnp.concatenate([k_ref[rp, :], k_ref[rq, :]], axis=0).astype(BF16)
            v2 = jnp.concatenate([v_ref[rp, :], v_ref[rq, :]], axis=0).astype(BF16)
            lse2 = jnp.concatenate([lse0_ref[rq, :], lse1_ref[rq, :]], axis=0)
            dl2 = jnp.concatenate([dl0_scr[rq, :], dl1_scr[rq, :]], axis=0)
            lse2 = jnp.concatenate([lse2, lse2], axis=1)
            dl2 = jnp.concatenate([dl2, dl2], axis=1)
            p = jnp.exp(_dot_nt(q2, k2) + _band_bias(bias_scr, b) - lse2)
            dp = _dot_nt(do2, v2)
            dsc = (p * (dp - dl2)).astype(BF16)
            dq2 = _unstack_heads(_dot_nn(dsc, k2), head0, halves)
            dk2 = _dot_tn(dsc, q2)
            dv2 = _dot_tn(p.astype(BF16), do2)
            dq_acc[rq, :] += dq2 * SCALE
            dk_acc[rp, :] += dk2[:BLOCK]
            dk_acc[rq, :] += dk2[BLOCK:]
            dv_acc[rp, :] += dv2[:BLOCK]
            dv_acc[rq, :] += dv2[BLOCK:]

        _block_loops(s, patterns, ATTN_UNROLL, one_block)

        def to_dz(staged, blk, k):
            cols = pl.ds(pl.multiple_of(blk * LANES, LANES), LANES)
            return pltpu.make_async_copy(staged, dz_ref.at[:, cols], out_sems.at[k])

        last_pair = pair == npair - 1
        q_copy = to_dz(dq_out, q_blk + pair, 0)
        kv_copies = [to_dz(dk_out, k_blk + (0 if gqa else pair), 1), to_dz(dv_out, v_blk + (0 if gqa else pair), 2)]

        @pl.when(pair > 0)
        def _():
            for cp in [q_copy] + ([] if gqa else kv_copies):
                cp.wait()

        def stage(acc, out):
            def f(c, carry):
                rows = pl.ds(pl.multiple_of(c * ATTN_ROWS, ATTN_ROWS), ATTN_ROWS)
                out[rows, :] = acc[rows, :].astype(BF16)
                return carry
            lax.fori_loop(0, s // ATTN_ROWS, f, 0)

        def stage_kv():
            stage(dk_acc, dk_out)
            stage(dv_acc, dv_out)
            for cp in kv_copies:
                cp.start()

        stage(dq_acc, dq_out)
        q_copy.start()
        if gqa:
            pl.when(last_pair)(stage_kv)
        else:
            stage_kv()

        @pl.when(last_pair)
        def _():
            for cp in [q_copy] + kv_copies:
                cp.wait()

    own = pl.BlockSpec((s, LANES), lambda j: (0, j))
    hbm = pl.BlockSpec(memory_space=pl.ANY)
    if gqa:
        kv = lambda blk: pl.BlockSpec((s, LANES), lambda j, blk=blk: (0, blk), pipeline_mode=pl.Buffered(1))
    else:
        kv = lambda blk: pl.BlockSpec((s, LANES), lambda j, blk=blk: (0, blk + j))
    in_specs = [pl.BlockSpec((s, LANES), lambda j: (0, q_blk + j)), kv(k_blk), kv(v_blk), own, own, own, own,
                pl.BlockSpec((1, LANES), lambda j: (0, j))]
    operands = (z, z, z, do, o, lse[0], lse[1], m_init)
    return _call(
        body,
        grid=(npair,),
        in_specs=in_specs + [hbm] * n_dz_in,
        out_specs=[hbm, pl.BlockSpec((1, LANES), lambda j: (0, j))],
        out_shape=[jax.ShapeDtypeStruct((s, IN_WIDTH), BF16), jax.ShapeDtypeStruct((1, npair * LANES), F32)],
        operands=operands + (() if dz is None else (dz,)), name=name,
        scratch_shapes=[pltpu.VMEM((s, LANES), F32)] * 5 + [pltpu.VMEM((2 * BLOCK, 2 * BLOCK), F32)]
        + [pltpu.VMEM((s, LANES), BF16)] * 3 + [pltpu.SemaphoreType.DMA((3,))],
        comm=comm, aliases={} if dz is None else {len(in_specs): 0})


def _adamw_math(w, g, m, v):
    m = ADAM_B1 * m + (1.0 - ADAM_B1) * g
    v = ADAM_B2 * v + (1.0 - ADAM_B2) * (g * g)
    m_hat = m / (1.0 - ADAM_B1 ** ADAM_STEP)
    v_hat = v / (1.0 - ADAM_B2 ** ADAM_STEP)
    delta = -ADAM_LR * (m_hat / (jnp.sqrt(v_hat) + ADAM_EPS) + ADAM_WD * w)
    return delta, m, v


def _adamw(w, g, m, v, name):
    rows, cols = w.shape
    tr = min(rows, 256)

    def body(w_ref, g_ref, m_ref, v_ref, d_ref, nm_ref, nv_ref):
        d_ref[...], nm_ref[...], nv_ref[...] = _adamw_math(w_ref[...], g_ref[...], m_ref[...], v_ref[...])

    spec = pl.BlockSpec((tr, cols), lambda i: (i, 0))
    return pl.pallas_call(
        body,
        grid=(rows // tr,),
        in_specs=[spec] * 4,
        out_specs=[spec] * 3,
        out_shape=[jax.ShapeDtypeStruct((rows, cols), F32)] * 3,
        compiler_params=_params("parallel"),
        name=name,
    )(w, g, m, v)


def _sum_adamw(parts, w, m, v, pos, transpose, name):
    assert len(parts) == DEPTH == 2
    (p0, r0), (p1, r1) = parts
    _, rows, cols = p0.shape
    tr = 256 if rows % 256 == 0 else rows
    nt = rows // tr

    def body(pos_ref, p0_ref, r0_ref, p1_ref, r1_ref, w_ref, m_ref, v_ref, g_ref, d_ref, nm_ref, nv_ref):
        def run(p_ref, r_ref):
            g = ((p_ref[...].astype(F32) + r_ref[0].astype(F32)) + r_ref[1].astype(F32)) + r_ref[2].astype(F32)
            if transpose:
                g = g.T
            g_ref[...] = g
            d_ref[...], nm_ref[...], nv_ref[...] = _adamw_math(w_ref[...], g, m_ref[...], v_ref[...])

        layer0 = pl.program_id(0) < nt
        pl.when(layer0)(lambda: run(p0_ref, r0_ref))
        pl.when(jnp.logical_not(layer0))(lambda: run(p1_ref, r1_ref))

    def tile0(i):
        return jnp.minimum(i, nt - 1)

    def tile1(i):
        return jnp.maximum(i - nt, 0)

    if transpose:
        w_spec = pl.BlockSpec((None, cols, tr), lambda i, q: (i // nt, 0, i % nt))
    else:
        w_spec = pl.BlockSpec((None, tr, cols), lambda i, q: (i // nt, i % nt, 0))
    return pl.pallas_call(
        body,
        grid_spec=pltpu.PrefetchScalarGridSpec(
            num_scalar_prefetch=1,
            grid=(DEPTH * nt,),
            in_specs=[pl.BlockSpec((None, tr, cols), lambda i, q: (q[0], tile0(i), 0)),
                      pl.BlockSpec((3, tr, cols), lambda i, q: (0, tile0(i), 0)),
                      pl.BlockSpec((None, tr, cols), lambda i, q: (q[0], tile1(i), 0)),
                      pl.BlockSpec((3, tr, cols), lambda i, q: (0, tile1(i), 0)),
                      w_spec, w_spec, w_spec],
            out_specs=[w_spec] * 4,
        ),
        out_shape=[jax.ShapeDtypeStruct(w.shape, F32)] * 4,
        compiler_params=_params("arbitrary"),
        name=name,
    )(pos, p0, r0, p1, r1, w, m, v)


def _small_sum_adamw(gathered, params, name):
    _, rows, cols = gathered.shape
    n = len(params)

    def body(ga_ref, *refs):
        ins, outs, (g_scr,) = refs[:3 * n], refs[3 * n:7 * n + 2], refs[7 * n + 2:]
        g = ga_ref[0]
        for i in range(1, N_DEV):
            g = g + ga_ref[i]
        g_scr[...] = g
        for k, (row0, w, _, _) in enumerate(params):
            w_ref, m_ref, v_ref = ins[3 * k:3 * k + 3]
            gk = g_scr[row0:row0 + w.shape[0], :]
            outs[4 * k][...] = gk
            outs[4 * k + 1][...], outs[4 * k + 2][...], outs[4 * k + 3][...] = _adamw_math(
                w_ref[...], gk, m_ref[...], v_ref[...])
        outs[4 * n][...] = g_scr[CONV_ROW:CONV_ROW + 8, :]
        outs[4 * n + 1][...] = g_scr[LOSS_ROW:LOSS_ROW + 1, :]

    out_shape = []
    for _, w, _, _ in params:
        out_shape += [jax.ShapeDtypeStruct(w.shape, F32)] * 4
    out_shape += [jax.ShapeDtypeStruct((8, cols), F32), jax.ShapeDtypeStruct((1, cols), F32)]
    res = pl.pallas_call(
        body,
        out_shape=out_shape,
        scratch_shapes=[pltpu.VMEM((rows, cols), F32)],
        name=name,
    )(gathered, *[t for _, w, m, v in params for t in (w, m, v)])
    return [res[4 * k:4 * k + 4] for k in range(n)], res[4 * n], res[4 * n + 1]


def _pair_sum(g4, r1, pos, name):
    _, _, rows, cols = g4.shape
    tr = min(rows, 512)

    def body(pos_ref, g_ref, r_ref, o_ref):
        o_ref[...] = (g_ref[...].astype(F32) + r_ref[...].astype(F32)).astype(BF16)

    return pl.pallas_call(
        body,
        grid_spec=pltpu.PrefetchScalarGridSpec(
            num_scalar_prefetch=1,
            grid=(4, rows // tr),
            in_specs=[pl.BlockSpec((None, None, tr, cols), lambda i, j, p: (i, p[1], j, 0)),
                      pl.BlockSpec((None, tr, cols), lambda i, j, p: (i, j, 0))],
            out_specs=pl.BlockSpec((None, tr, cols), lambda i, j, p: (i, j, 0)),
        ),
        out_shape=jax.ShapeDtypeStruct((4, rows, cols), BF16),
        compiler_params=_params("parallel", "parallel"),
        name=name,
    )(pos, g4, r1)


def _place():
    return lax.axis_index("x"), lax.axis_index("y"), lax.axis_index("c")


def _gather_comm(shards):
    na = len(shards)

    def plan(ins, outs, sems):
        send_sems, recv_sems, local_sems = sems
        x, y, c = _place()
        me, sibling = (x, y, c), (x, y, 1 - c)
        chips = [(1 - x, y), (x, 1 - y), (1 - x, 1 - y)]

        def rows(a, px, py, pc):
            m = ins[a].shape[0]
            return outs[a].at[pl.ds((4 * px + 2 * py + pc) * m, m), :]

        def copy(a, k, block, to, src=None):
            return pltpu.make_async_remote_copy(
                src_ref=rows(a, *block) if src is None else src, dst_ref=rows(a, *block),
                send_sem=send_sems.at[a, k], recv_sem=recv_sems.at[a, k], device_id=to, device_id_type=MESH)

        mine = [pltpu.make_async_copy(ins[a], rows(a, *me), local_sems.at[a]) for a in range(na)]
        first = []
        for a in range(na):
            first.append(copy(a, 0, me, sibling, src=ins[a]))
            first += [copy(a, 1 + j, me, (*chip, c), src=ins[a]) for j, chip in enumerate(chips)]
        return me, sibling, chips, c, copy, mine, first

    def start(ins, outs, sems):
        *_, mine, first = plan(ins, outs, sems)
        for cp in mine + first:
            cp.start()

    def finish(ins, outs, sems):
        me, sibling, chips, c, copy, mine, first = plan(ins, outs, sems)
        passed = []
        for j, chip in enumerate(chips):
            for a in range(na):
                copy(a, 1 + j, (*chip, c), me).wait_recv()
                cp = copy(a, 4 + j, (*chip, c), sibling)
                cp.start()
                passed.append(cp)
        for a in range(na):
            copy(a, 0, sibling, me).wait_recv()
            for j, chip in enumerate(chips):
                copy(a, 4 + j, (*chip, 1 - c), me).wait_recv()
        for cp in first + passed:
            cp.wait_send()
        for cp in mine:
            cp.wait()

    return _Comm(tuple(shards),
                 tuple(jax.ShapeDtypeStruct((N_DEV * t.shape[0], t.shape[1]), t.dtype) for t in shards),
                 (pltpu.SemaphoreType.DMA((na, 7)), pltpu.SemaphoreType.DMA((na, 7)), pltpu.SemaphoreType.DMA((na,))),
                 start, finish)


def _exchange_comm(arrays, out_shape, n_copies, copies_of):
    na = len(arrays)

    def every(ins, outs, sems):
        send_sems, recv_sems = sems
        return [cp for a in range(na) for cp in copies_of(ins, outs, a, send_sems, recv_sems)]

    def start(ins, outs, sems):
        for cp in every(ins, outs, sems):
            cp.start()

    def finish(ins, outs, sems):
        for cp in every(ins, outs, sems):
            cp.wait()

    return _Comm(tuple(arrays), tuple(out_shape),
                 (pltpu.SemaphoreType.DMA((na, n_copies)), pltpu.SemaphoreType.DMA((na, n_copies))), start, finish)


def _sibling_comm(grads):
    def copies_of(ins, outs, a, send_sems, recv_sems):
        x, y, c = _place()
        return [pltpu.make_async_remote_copy(
            src_ref=ins[a].at[chip, 1 - c], dst_ref=outs[a].at[chip],
            send_sem=send_sems.at[a, chip], recv_sem=recv_sems.at[a, chip],
            device_id=(x, y, 1 - c), device_id_type=MESH) for chip in range(4)]

    return _exchange_comm(grads, [jax.ShapeDtypeStruct((4,) + t.shape[2:], t.dtype) for t in grads], 4, copies_of)


def _chip_comm(partials):
    def copies_of(ins, outs, a, send_sems, recv_sems):
        x, y, c = _place()
        chips = [(1 - x, y), (x, 1 - y), (1 - x, 1 - y)]
        return [pltpu.make_async_remote_copy(
            src_ref=ins[a].at[2 * cx + cy], dst_ref=outs[a].at[k],
            send_sem=send_sems.at[a, k], recv_sem=recv_sems.at[a, k],
            device_id=(cx, cy, c), device_id_type=MESH) for k, (cx, cy) in enumerate(chips)]

    return _exchange_comm(partials, [jax.ShapeDtypeStruct((3,) + t.shape[1:], t.dtype) for t in partials], 3, copies_of)


def _pad_rows(t, rows):
    return jnp.pad(t, ((0, rows - t.shape[0]), (0, D_MODEL - t.shape[1])))


MIX_ROW, GROUP_ROW, MLP_ROW, FINAL_ROW, CONV_ROW, SINK_ROW = 0, 8, 16, 24, 32, 40
LOSS_ROW = FINAL_ROW + 1


def _pack_small(g_mix, g_group, g_mlp, g_final, conv, sinks, loss):
    final_and_loss = jnp.concatenate([g_final.reshape(1, D_MODEL), _pad_rows(loss, 1)], axis=0)
    return jnp.concatenate([
        _pad_rows(g_mix, 8), _pad_rows(g_group, 8), _pad_rows(g_mlp, 8), _pad_rows(final_and_loss, 8),
        _pad_rows(conv.reshape(DEPTH * 3, CONV_CH), 8), _pad_rows(sinks.reshape(1, DEPTH * 6), 8)], axis=0)


def kernel(x, w_in, conv_w, sinks, g_mix, g_group, w_o, g_mlp, w_ff_in, w_ff_out, g_final, loss_target, m_w_in, m_conv_w, m_sinks, m_g_mix, m_g_group, m_w_o, m_g_mlp, m_w_ff_in, m_w_ff_out, m_g_final, v_w_in, v_conv_w, v_sinks, v_g_mix, v_g_group, v_w_o, v_g_mlp, v_w_ff_in, v_w_ff_out, v_g_final):
    ax, ay, ac = _place()
    chip = 2 * ax + ay
    dev = 4 * ax + 2 * ay + ac
    pos = jnp.stack([chip, ac]).astype(jnp.int32)

    x0 = x.reshape(SEQ, D_MODEL)
    target = loss_target.reshape(SEQ, D_MODEL)

    shards = {}
    for l in range(DEPTH):
        shards[l, 0], shards[l, 1] = w_in[l].T.astype(BF16), w_o[l].astype(BF16)
        shards[l, 2], shards[l, 3] = w_ff_in[l].T.astype(BF16), w_ff_out[l].astype(BF16)
    conv_tile = jnp.pad(conv_w.reshape(DEPTH * 3, CONV_CH // N_DEV), ((0, 2), (0, LANES - CONV_CH // N_DEV)))
    wt_in0, conv_all = _comm_only(_gather_comm([shards[0, 0], conv_tile]), "gather_first")
    conv_full = conv_all.reshape(N_DEV, 8, LANES)[:, :DEPTH * 3, :CONV_CH // N_DEV]
    conv_full = conv_full.transpose(1, 0, 2).reshape(DEPTH, 3, CONV_CH)

    dx, parts, small = _step(x0, target, shards, wt_in0, conv_full, sinks, g_mix, g_group, g_mlp, g_final, pos)
    return _finish(dx, parts, small, pos, dev, w_in, conv_w, sinks, g_mix, g_group, w_o, g_mlp, w_ff_in, w_ff_out, g_final, m_w_in, m_conv_w, m_sinks, m_g_mix, m_g_group, m_w_o, m_g_mlp, m_w_ff_in, m_w_ff_out, m_g_final, v_w_in, v_conv_w, v_sinks, v_g_mix, v_g_group, v_w_o, v_g_mlp, v_w_ff_in, v_w_ff_out, v_g_final)


FWD_CARRY = {(0, "in_proj"): ((1, 0),), (0, "window"): ((0, 1),), (0, "dilated"): ((0, 2),),
             (0, "mix_ff_in"): ((0, 3),), (0, "ff_out_in_proj"): ((1, 1), (1, 3)),
             (1, "dilated"): ((1, 2),)}


def _step(x0, target, shards, wt_in0, conv_full, sinks, g_mix, g_group, g_mlp, g_final, pos):
    sink_lanes = jnp.repeat(sinks.reshape(DEPTH, 6), HEAD_DIM, axis=1)
    no_sink = jnp.full((1, A_WIDTH), NEG_BIG, F32)
    full = {(0, 0): wt_in0}

    def gather(stage, l):
        keys = FWD_CARRY.get((l, stage), ())
        return keys, (_gather_comm([shards[k] for k in keys]) if keys else None)

    def landed(keys, got):
        full.update(zip(keys, got))

    saved = []
    xc = x0
    keys, comm = gather("in_proj", 0)
    (z, h), got = _norm_mm(xc, g_mix[0:1], full[0, 0], "in_proj_0", comm)
    landed(keys, got)
    for l in range(DEPTH):
        sink_l = sink_lanes[l:l + 1]
        keys, comm = gather("window", l)
        (yc, *lse_c), got = _attn_fwd(z, sink_l, 1.0, QC_BLK, KC_BLK, VC_BLK, (1,), C_MAX_DIST, True,
                                     f"window_attn_{l}", comm)
        landed(keys, got)
        yb = _conv_fwd(z, conv_full[l], f"conv_{l}")
        keys, comm = gather("dilated", l)
        (ya, *lse_a), got = _attn_fwd(z, no_sink, 0.0, QA_BLK, KA_BLK, VA_BLK, DILATED_PATTERNS, A_MAX_DIST, False,
                                     f"dilated_attn_{l}", comm)
        landed(keys, got)
        keys, comm = gather("mix_ff_in", l)
        (y, x1, a, h2), got = _mix_ff_in(ya, yb, yc, g_group[l:l + 1], full[l, 1], xc, g_mlp[l:l + 1], full[l, 2],
                                         f"mix_ff_in_{l}", comm)
        landed(keys, got)
        saved.append((xc, z, h, ya, lse_a, yb, yc, lse_c, sink_l, y, x1, a, h2))
        if l + 1 < DEPTH:
            keys, comm = gather("ff_out_in_proj", l)
            (xc, z, h), got = _ff_out_in_proj(a, full[l, 3], x1, g_mix[l + 1:l + 2], full[l + 1, 0],
                                              f"ff_out_{l}_in_proj_{l + 1}", comm)
            landed(keys, got)

    loss_slab, dx, dxb, dg_final, du = _mm_res_loss(a, full[DEPTH - 1, 3], x1, g_final.reshape(1, D_MODEL), target,
                                                    f"ff_out_{DEPTH - 1}_loss")

    def by_owner(t):
        return t.reshape(4, 2, t.shape[0] // N_DEV, D_MODEL)

    def pair(key, g, r1):
        return _pair_sum(g, r1, pos, f"grad_pair_sum_{key[0]}_{key[1]}")

    partial, r2 = {}, {}
    dg_mix, dg_group, dg_mlp, dconv, dsinks = [None] * DEPTH, [None] * DEPTH, [None] * DEPTH, [None] * DEPTH, [None] * DEPTH
    for l in reversed(range(DEPTH)):
        xin, z, h, ya, lse_a, yb, yc, lse_c, sink_l, y, x1, a, h2 = saved[l]
        if l + 1 < DEPTH:
            late = [(l + 1, 1), (l + 1, 0)]
            (du,), got = _mlp_bwd_act(dxb, full[l, 3], a, f"ff_out_bwd_{l}", _chip_comm([partial[k] for k in late]))
            r2.update(zip(late, got))
        (g3, g2), _ = _mm_tn([(a, dxb), (du, h2)], f"grad_w_ff_{l}")
        g3, g2 = by_owner(g3), by_owner(g2)
        (dx1, dx1b, dg_mlp[l], dya, dyb, dyc, dg_group[l]), got = _ff_in_mix_bwd(
            du, full[l, 2], x1, dx, g_mlp[l:l + 1], full[l, 1], ya, yb, yc, g_group[l:l + 1],
            f"ff_in_mix_bwd_{l}", _sibling_comm([g3, g2]))
        partial[l, 3], partial[l, 2] = pair((l, 3), g3, got[0]), pair((l, 2), g2, got[1])
        early = [(l, 3), (l, 2)]
        (dz, _), got = _attn_bwd(z, dya, ya, lse_a, no_sink, None, QA_BLK, KA_BLK, VA_BLK, DILATED_PATTERNS,
                                 A_MAX_DIST, False, f"dilated_attn_bwd_{l}", _chip_comm([partial[k] for k in early]))
        r2.update(zip(early, got))
        dz, dcw = _conv_bwd(z, conv_full[l], dyb, dz, f"conv_bwd_{l}")
        (dz, dsink), _ = _attn_bwd(z, dyc, yc, lse_c, sink_l, dz, QC_BLK, KC_BLK, VC_BLK, (1,), C_MAX_DIST,
                                   True, f"window_attn_bwd_{l}")
        (g1, g0), _ = _mm_tn([(y, dx1b), (dz, h)], f"grad_w_o_in_{l}")
        g1, g0 = by_owner(g1), by_owner(g0)
        if l > 0:
            (dx, dxb, dg_mix[l]), got = _mm_nn_normbwd(dz, full[l, 0], xin, dx1, g_mix[l:l + 1], f"in_proj_bwd_{l}",
                                                      _sibling_comm([g1, g0]))
            partial[l, 1], partial[l, 0] = pair((l, 1), g1, got[0]), pair((l, 0), g0, got[1])
        else:
            got = _comm_only(_sibling_comm([g1, g0]), "grad_sibling_exchange_last")
            partial[l, 1], partial[l, 0] = pair((l, 1), g1, got[0]), pair((l, 0), g0, got[1])
            (dx, dxb, dg_mix[l]), got = _mm_nn_normbwd(dz, full[l, 0], xin, dx1, g_mix[l:l + 1], f"in_proj_bwd_{l}",
                                                      _chip_comm([partial[l, 1], partial[l, 0]]))
            r2[l, 1], r2[l, 0] = got
        dconv[l] = dcw[:3]
        dsinks[l] = dsink[0, ::HEAD_DIM]
    parts = {key: (partial[key], r2[key]) for key in partial}
    small = _pack_small(jnp.concatenate(dg_mix), jnp.concatenate(dg_group), jnp.concatenate(dg_mlp),
                        dg_final, jnp.stack(dconv), jnp.stack(dsinks), loss_slab[0:1])
    return dx, parts, small


def _finish(dx, parts, small, pos, dev, w_in, conv_w, sinks, g_mix, g_group, w_o, g_mlp, w_ff_in, w_ff_out, g_final, m_w_in, m_conv_w, m_sinks, m_g_mix, m_g_group, m_w_o, m_g_mlp, m_w_ff_in, m_w_ff_out, m_g_final, v_w_in, v_conv_w, v_sinks, v_g_mix, v_g_group, v_w_o, v_g_mlp, v_w_ff_in, v_w_ff_out, v_g_final):
    grad_x = dx.reshape(1, SEQ, D_MODEL)

    (small_all,) = _comm_only(_gather_comm([small]), "gather_small_grads")
    row = lambda t: t.reshape(1, D_MODEL)
    sink_row = lambda t: _pad_rows(t.reshape(1, DEPTH * 6), 1)
    params = [(MIX_ROW, g_mix, m_g_mix, v_g_mix), (GROUP_ROW, g_group, m_g_group, v_g_group),
              (MLP_ROW, g_mlp, m_g_mlp, v_g_mlp), (FINAL_ROW, row(g_final), row(m_g_final), row(v_g_final)),
              (SINK_ROW, sink_row(sinks), sink_row(m_sinks), sink_row(v_sinks))]
    updated, conv_rows, loss_row = _small_sum_adamw(small_all.reshape(N_DEV, SMALL_ROWS, D_MODEL), params, "small_adamw")
    loss = loss_row[0, 0]
    (grad_g_mix, delta_g_mix, new_m_g_mix, new_v_g_mix), (grad_g_group, delta_g_group, new_m_g_group, new_v_g_group), \
        (grad_g_mlp, delta_g_mlp, new_m_g_mlp, new_v_g_mlp), final4, sinks4 = updated
    grad_g_final, delta_g_final, new_m_g_final, new_v_g_final = [t.reshape(D_MODEL) for t in final4]
    grad_sinks, delta_sinks, new_m_sinks, new_v_sinks = [t[0, :DEPTH * 6].reshape(DEPTH, 2, 3) for t in sinks4]
    conv_grad_full = conv_rows[:DEPTH * 3, :CONV_CH].reshape(DEPTH, 3, CONV_CH)
    cs = CONV_CH // N_DEV
    grad_conv_w = lax.dynamic_slice_in_dim(conv_grad_full, dev * cs, cs, axis=2)

    def tile_of(t):
        return jnp.pad(t.reshape(1, DEPTH * 3 * cs), ((0, 7), (0, 256 - DEPTH * 3 * cs)))

    cd, cm, cv = _adamw(tile_of(conv_w), tile_of(grad_conv_w), tile_of(m_conv_w), tile_of(v_conv_w), "conv_adamw")
    untile = lambda t: t[0, :DEPTH * 3 * cs].reshape(DEPTH, 3, cs)
    delta_conv_w, new_m_conv_w, new_v_conv_w = untile(cd), untile(cm), untile(cv)

    def big(kind, w, m, v, transpose, name):
        return _sum_adamw([parts[l, kind] for l in range(DEPTH)], w, m, v, pos, transpose, name)

    grad_w_in, delta_w_in, new_m_w_in, new_v_w_in = big(0, w_in, m_w_in, v_w_in, True, "adamw_w_in")
    grad_w_o, delta_w_o, new_m_w_o, new_v_w_o = big(1, w_o, m_w_o, v_w_o, False, "adamw_w_o")
    grad_w_ff_in, delta_w_ff_in, new_m_w_ff_in, new_v_w_ff_in = big(2, w_ff_in, m_w_ff_in, v_w_ff_in, True, "adamw_w_ff_in")
    grad_w_ff_out, delta_w_ff_out, new_m_w_ff_out, new_v_w_ff_out = big(3, w_ff_out, m_w_ff_out, v_w_ff_out, False,
                                                                         "adamw_w_ff_out")

    return (loss, grad_x, grad_w_in, grad_conv_w, grad_sinks, grad_g_mix, grad_g_group, grad_w_o, grad_g_mlp,
            grad_w_ff_in, grad_w_ff_out, grad_g_final,
            delta_w_in, delta_conv_w, delta_sinks, delta_g_mix, delta_g_group, delta_w_o, delta_g_mlp,
            delta_w_ff_in, delta_w_ff_out, delta_g_final,
            new_m_w_in, new_m_conv_w, new_m_sinks, new_m_g_mix, new_m_g_group, new_m_w_o, new_m_g_mlp,
            new_m_w_ff_in, new_m_w_ff_out, new_m_g_final,
            new_v_w_in, new_v_conv_w, new_v_sinks, new_v_g_mix, new_v_g_group, new_v_w_o, new_v_g_mlp,
            new_v_w_ff_in, new_v_w_ff_out, new_v_g_final)
```

```python
from typing import Callable, NamedTuple

import jax
import jax.numpy as jnp
from jax import lax
from jax.experimental import pallas as pl
from jax.experimental.pallas import tpu as pltpu

F32 = jnp.float32
BF16 = jnp.bfloat16
MESH = pl.DeviceIdType.MESH

N_DEV = 8
SEQ = 4096
D_MODEL = 1024
DEPTH = 2
HEAD_DIM = 64
LANES = 128
A_WIDTH = 384
CONV_CH = 256
C_WIDTH = 384
IN_WIDTH = 2560
BLOCK = 128
DILATED_PATTERNS = (1, 4, 16)
A_MAX_DIST = 128
C_MAX_DIST = 127
EPS = 1e-6
SCALE = HEAD_DIM ** -0.5
NEG_BIG = -1e30
F32_TINY = 1.1754944e-38

QA_BLK, KA_BLK, VA_BLK = 0, 3, 6
GB_BLK, GC_BLK, XB_BLK = 9, 11, 13
QC_BLK, KC_BLK, VC_BLK = 15, 18, 19

ADAM_LR = 0.001
ADAM_B1 = 0.9
ADAM_B2 = 0.999
ADAM_EPS = 1e-08
ADAM_WD = 0.01
ADAM_STEP = 10

VMEM_LIMIT = 56 * 1024 * 1024
TILE_BUDGET = 46 * 1024 * 1024
ROW_TILE = 512
COL_CHUNK = 512
SMALL_ROWS = 48


def _dot_nn(a, b):
    return lax.dot_general(a, b, (((1,), (0,)), ((), ())), preferred_element_type=F32)


def _dot_nt(a, b):
    return lax.dot_general(a, b, (((1,), (1,)), ((), ())), preferred_element_type=F32)


def _dot_tn(a, b):
    return lax.dot_general(a, b, (((0,), (0,)), ((), ())), preferred_element_type=F32)


def _params(*sem, collective_id=None):
    return pltpu.CompilerParams(dimension_semantics=sem, vmem_limit_bytes=VMEM_LIMIT, collective_id=collective_id)


def _resident(shape):
    return pl.BlockSpec(shape, lambda i: (0,) * len(shape), pipeline_mode=pl.Buffered(1))


def _row_tile(row_bytes, resident_bytes):
    for tm in (ROW_TILE, ROW_TILE // 2):
        if 2 * tm * row_bytes + resident_bytes <= TILE_BUDGET:
            return tm
    return ROW_TILE // 4


def _rms_scale(t):
    return lax.rsqrt(jnp.mean(t * t, axis=-1, keepdims=True) + EPS)


def _rms_bwd(n, r, dn):
    return r * (dn - n * jnp.mean(dn * n, axis=-1, keepdims=True))


class _Comm(NamedTuple):
    arrays: tuple
    out_shape: tuple
    sems: tuple
    start: Callable
    finish: Callable
    peers: Callable
    collective_id: int


def _handshake(comm):
    barrier = pltpu.get_barrier_semaphore()
    peers = comm.peers()
    for peer in peers:
        pl.semaphore_signal(barrier, inc=1, device_id=peer, device_id_type=MESH)
    pl.semaphore_wait(barrier, len(peers))


def _call(body, grid, in_specs, out_specs, out_shape, operands, name, scratch_shapes=(), comm=None, aliases=None):
    n_in, n_out, n_scr = len(in_specs), len(out_shape), len(scratch_shapes)
    aliases = dict(aliases or {})
    if comm is None:
        res = pl.pallas_call(body, grid=grid, in_specs=list(in_specs), out_specs=list(out_specs),
                             out_shape=list(out_shape), scratch_shapes=list(scratch_shapes),
                             input_output_aliases=aliases,
                             compiler_params=_params("arbitrary"), name=name)(*operands)
        return list(res), []
    c_in, c_out = len(comm.arrays), len(comm.out_shape)
    hbm = pl.BlockSpec(memory_space=pl.ANY)
    last = grid[0] - 1

    def carried(*refs):
        ins, cins = refs[:n_in], refs[n_in:n_in + c_in]
        o0 = n_in + c_in
        outs, couts = refs[o0:o0 + n_out], refs[o0 + n_out:o0 + n_out + c_out]
        s0 = o0 + n_out + c_out
        scr, sems = refs[s0:s0 + n_scr], refs[s0 + n_scr:]
        @pl.when(pl.program_id(0) == 0)
        def _():
            _handshake(comm)
            comm.start(cins, couts, sems)

        body(*ins, *outs, *scr)
        pl.when(pl.program_id(0) == last)(lambda: comm.finish(cins, couts, sems))

    res = pl.pallas_call(carried, grid=grid, in_specs=list(in_specs) + [hbm] * c_in,
                         out_specs=list(out_specs) + [hbm] * c_out, out_shape=list(out_shape) + list(comm.out_shape),
                         scratch_shapes=list(scratch_shapes) + list(comm.sems), input_output_aliases=aliases,
                         compiler_params=_params("arbitrary", collective_id=comm.collective_id),
                         name=name)(*operands, *comm.arrays)
    return list(res[:n_out]), list(res[n_out:])


def _comm_only(comm, name):
    hbm = pl.BlockSpec(memory_space=pl.ANY)
    c_in, c_out = len(comm.arrays), len(comm.out_shape)

    def body(*refs):
        ins, outs, sems = refs[:c_in], refs[c_in:c_in + c_out], refs[c_in + c_out:]
        _handshake(comm)
        comm.start(ins, outs, sems)
        comm.finish(ins, outs, sems)

    return pl.pallas_call(body, in_specs=[hbm] * c_in, out_specs=[hbm] * c_out, out_shape=list(comm.out_shape),
                          scratch_shapes=list(comm.sems),
                          compiler_params=pltpu.CompilerParams(collective_id=comm.collective_id),
                          name=name)(*comm.arrays)


def _norm_mm(x, g, wt, name, comm=None):
    s, d = x.shape
    n = wt.shape[0]
    tm = _row_tile(4 * d + 4 * n + 2 * d, 2 * n * d)

    def body(x_ref, g_ref, w_ref, o_ref, h_ref):
        xx = x_ref[...]
        h = ((xx * _rms_scale(xx)) * g_ref[...]).astype(BF16)
        h_ref[...] = h
        for n0 in range(0, n, COL_CHUNK):
            o_ref[:, n0:n0 + COL_CHUNK] = _dot_nt(h, w_ref[n0:n0 + COL_CHUNK, :])

    return _call(
        body,
        grid=(s // tm,),
        in_specs=[pl.BlockSpec((tm, d), lambda i: (i, 0)),
                  pl.BlockSpec((1, d), lambda i: (0, 0)),
                  _resident((n, d))],
        out_specs=[pl.BlockSpec((tm, n), lambda i: (i, 0)),
                   pl.BlockSpec((tm, d), lambda i: (i, 0))],
        out_shape=[jax.ShapeDtypeStruct((s, n), F32), jax.ShapeDtypeStruct((s, d), BF16)],
        operands=(x, g, wt), name=name, comm=comm)


def _ff_out_in_proj(a, w2, x1, g, wt, name, comm=None):
    s, f = a.shape
    d = w2.shape[1]
    n = wt.shape[0]
    tm = _row_tile(2 * f + 4 * d + 4 * d + 4 * n + 2 * d, 2 * f * d + 2 * n * d)

    def body(a_ref, w2_ref, x_ref, g_ref, w_ref, x2_ref, z_ref, h_ref):
        x2 = x_ref[...] + _dot_nn(a_ref[...], w2_ref[...])
        x2_ref[...] = x2
        h = ((x2 * _rms_scale(x2)) * g_ref[...]).astype(BF16)
        h_ref[...] = h
        for n0 in range(0, n, COL_CHUNK):
            z_ref[:, n0:n0 + COL_CHUNK] = _dot_nt(h, w_ref[n0:n0 + COL_CHUNK, :])

    rows = lambda w: pl.BlockSpec((tm, w), lambda i: (i, 0))
    return _call(
        body,
        grid=(s // tm,),
        in_specs=[rows(f), _resident((f, d)), rows(d), pl.BlockSpec((1, d), lambda i: (0, 0)), _resident((n, d))],
        out_specs=[rows(d), rows(n), rows(d)],
        out_shape=[jax.ShapeDtypeStruct((s, d), F32), jax.ShapeDtypeStruct((s, n), F32),
                   jax.ShapeDtypeStruct((s, d), BF16)],
        operands=(a, w2, x1, g, wt), name=name, comm=comm)


def _mix_ff_in(ya, yb, yc, gg, wo, x0, g_mlp, wt1, name, comm=None):
    s = ya.shape[0]
    d = wo.shape[1]
    f = wt1.shape[0]
    tm = _row_tile(4 * d + 4 * d + 2 * d + 4 * d + 2 * d + 2 * f, 2 * d * d + 2 * f * d)

    def body(ya_ref, yb_ref, yc_ref, gg_ref, wo_ref, x_ref, g_ref, w1_ref, y_ref, x1_ref, a_ref, h_ref):
        parts = []
        for ref in (ya_ref, yb_ref, yc_ref):
            t = ref[...]
            parts.append(t * _rms_scale(t))
        y = (jnp.concatenate(parts, axis=1) * gg_ref[...]).astype(BF16)
        y_ref[...] = y
        x1 = x_ref[...] + _dot_nn(y, wo_ref[...])
        x1_ref[...] = x1
        h = ((x1 * _rms_scale(x1)) * g_ref[...]).astype(BF16)
        h_ref[...] = h
        for n0 in range(0, f, COL_CHUNK):
            u = _dot_nt(h, w1_ref[n0:n0 + COL_CHUNK, :])
            a_ref[:, n0:n0 + COL_CHUNK] = jnp.square(jnp.maximum(u, 0.0)).astype(BF16)

    rows = lambda w: pl.BlockSpec((tm, w), lambda i: (i, 0))
    vec = pl.BlockSpec((1, d), lambda i: (0, 0))
    return _call(
        body,
        grid=(s // tm,),
        in_specs=[rows(A_WIDTH), rows(CONV_CH), rows(C_WIDTH), vec, _resident((d, d)), rows(d), vec, _resident((f, d))],
        out_specs=[rows(d), rows(d), rows(f), rows(d)],
        out_shape=[jax.ShapeDtypeStruct((s, d), BF16), jax.ShapeDtypeStruct((s, d), F32),
                   jax.ShapeDtypeStruct((s, f), BF16), jax.ShapeDtypeStruct((s, d), BF16)],
        operands=(ya, yb, yc, gg, wo, x0, g_mlp, wt1), name=name, comm=comm)


def _relu_from_square(av):
    return av * lax.rsqrt(jnp.maximum(av, F32_TINY))


def _mm_res_loss(a, w2, x1, g, target, name):
    s, f = a.shape
    d = w2.shape[1]
    tm = _row_tile(2 * f + 4 * d + 4 * d + 4 * d + 2 * d + 2 * f, 2 * f * d)

    def body(a_ref, w_ref, x_ref, g_ref, t_ref, loss_ref, dx_ref, dxb_ref, dg_ref, du_ref):
        @pl.when(pl.program_id(0) == 0)
        def _():
            loss_ref[...] = jnp.zeros_like(loss_ref)
            dg_ref[...] = jnp.zeros_like(dg_ref)

        xx = x_ref[...] + _dot_nn(a_ref[...], w_ref[...])
        r = _rms_scale(xx)
        n = xx * r
        gv = g_ref[...]
        err = n * gv - t_ref[...]
        per_tok = jnp.sum(err * err, axis=1, keepdims=True) * (1.0 / d)
        loss_ref[...] += 0.5 * jnp.sum(per_tok, axis=0, keepdims=True)
        dout = err * (1.0 / d)
        dg_ref[...] += jnp.sum(dout * n, axis=0, keepdims=True)
        dx = _rms_bwd(n, r, dout * gv)
        dx_ref[...] = dx
        dxb = dx.astype(BF16)
        dxb_ref[...] = dxb
        for n0 in range(0, f, COL_CHUNK):
            da = _dot_nt(dxb, w_ref[n0:n0 + COL_CHUNK, :])
            rl = _relu_from_square(a_ref[:, n0:n0 + COL_CHUNK].astype(F32))
            du_ref[:, n0:n0 + COL_CHUNK] = (da * (2.0 * rl)).astype(BF16)

    rows = lambda w: pl.BlockSpec((tm, w), lambda i: (i, 0))
    vec = pl.BlockSpec((1, d), lambda i: (0, 0))
    return pl.pallas_call(
        body,
        grid=(s // tm,),
        in_specs=[rows(f), _resident((f, d)), rows(d), vec, rows(d)],
        out_specs=[pl.BlockSpec((8, LANES), lambda i: (0, 0)), rows(d), rows(d), vec, rows(f)],
        out_shape=[jax.ShapeDtypeStruct((8, LANES), F32), jax.ShapeDtypeStruct((s, d), F32),
                   jax.ShapeDtypeStruct((s, d), BF16), jax.ShapeDtypeStruct((1, d), F32),
                   jax.ShapeDtypeStruct((s, f), BF16)],
        compiler_params=_params("arbitrary"),
        name=name,
    )(a, w2, x1, g, target)


def _mlp_bwd_act(dxb, w2, a, name, comm=None):
    s, d = dxb.shape
    f = w2.shape[0]
    tm = _row_tile(2 * d + 2 * f + 2 * f, 2 * f * d)

    def body(dx_ref, w_ref, a_ref, du_ref):
        dx = dx_ref[...]
        for n0 in range(0, f, COL_CHUNK):
            da = _dot_nt(dx, w_ref[n0:n0 + COL_CHUNK, :])
            rl = _relu_from_square(a_ref[:, n0:n0 + COL_CHUNK].astype(F32))
            du_ref[:, n0:n0 + COL_CHUNK] = (da * (2.0 * rl)).astype(BF16)

    return _call(
        body,
        grid=(s // tm,),
        in_specs=[pl.BlockSpec((tm, d), lambda i: (i, 0)),
                  _resident((f, d)),
                  pl.BlockSpec((tm, f), lambda i: (i, 0))],
        out_specs=[pl.BlockSpec((tm, f), lambda i: (i, 0))],
        out_shape=[jax.ShapeDtypeStruct((s, f), BF16)],
        operands=(dxb, w2, a), name=name, comm=comm)


def _mm_tn(pairs, name, comm=None):
    s, d = pairs[0][1].shape
    tn = 512
    tiles = [a.shape[1] // tn for a, _ in pairs]
    starts = [sum(tiles[:k]) for k in range(len(pairs))]

    def body(*refs):
        ins, outs, acc = refs[:2 * len(pairs)], refs[2 * len(pairs):3 * len(pairs)], refs[3 * len(pairs)]
        j = pl.program_id(0)
        for k in range(len(pairs)):
            def run(a_ref=ins[2 * k], b_ref=ins[2 * k + 1], o_ref=outs[k]):
                for k0 in range(0, s, ROW_TILE):
                    part = _dot_tn(a_ref[k0:k0 + ROW_TILE, :], b_ref[k0:k0 + ROW_TILE, :])
                    if k0 == 0:
                        acc[...] = part
                    else:
                        acc[...] += part
                o_ref[...] = acc[...].astype(BF16)

            pl.when((j >= starts[k]) & (j < starts[k] + tiles[k]))(run)

    def tile_of(k):
        return lambda j: jnp.clip(j - starts[k], 0, tiles[k] - 1)

    in_specs, out_specs = [], []
    for k in range(len(pairs)):
        in_specs += [pl.BlockSpec((s, tn), lambda j, t=tile_of(k): (0, t(j))), _resident((s, d))]
        out_specs.append(pl.BlockSpec((tn, d), lambda j, t=tile_of(k): (t(j), 0)))
    return _call(
        body,
        grid=(sum(tiles),),
        in_specs=in_specs,
        out_specs=out_specs,
        out_shape=[jax.ShapeDtypeStruct((a.shape[1], d), BF16) for a, _ in pairs],
        operands=tuple(t for pair in pairs for t in pair), name=name,
        scratch_shapes=[pltpu.VMEM((tn, d), F32)], comm=comm)


def _mm_nn_normbwd(dact, wt, x, dres, g, name, comm=None):
    s, kdim = dact.shape
    d = wt.shape[1]
    tm = _row_tile(2 * kdim + 4 * d + 4 * d + 4 * d + 2 * d, 2 * kdim * d)

    def body(a_ref, w_ref, x_ref, r_ref, g_ref, o_ref, ob_ref, dg_ref):
        @pl.when(pl.program_id(0) == 0)
        def _():
            dg_ref[...] = jnp.zeros_like(dg_ref)

        dh = _dot_nn(a_ref[...], w_ref[...])
        xx = x_ref[...]
        r = _rms_scale(xx)
        n = xx * r
        dg_ref[...] += jnp.sum(dh * n, axis=0, keepdims=True)
        dx = r_ref[...] + _rms_bwd(n, r, dh * g_ref[...])
        o_ref[...] = dx
        ob_ref[...] = dx.astype(BF16)

    return _call(
        body,
        grid=(s // tm,),
        in_specs=[pl.BlockSpec((tm, kdim), lambda i: (i, 0)),
                  _resident((kdim, d)),
                  pl.BlockSpec((tm, d), lambda i: (i, 0)),
                  pl.BlockSpec((tm, d), lambda i: (i, 0)),
                  pl.BlockSpec((1, d), lambda i: (0, 0))],
        out_specs=[pl.BlockSpec((tm, d), lambda i: (i, 0)),
                   pl.BlockSpec((tm, d), lambda i: (i, 0)),
                   pl.BlockSpec((1, d), lambda i: (0, 0))],
        out_shape=[jax.ShapeDtypeStruct((s, d), F32), jax.ShapeDtypeStruct((s, d), BF16),
                   jax.ShapeDtypeStruct((1, d), F32)],
        operands=(dact, wt, x, dres, g), name=name, comm=comm)


def _ff_in_mix_bwd(du, wt1, x1, dres, g_mlp, wo, ya, yb, yc, gg, name, comm=None):
    s, f = du.shape
    d = wt1.shape[1]
    widths = (A_WIDTH, CONV_CH, C_WIDTH)
    tm = _row_tile(2 * f + 4 * d + 4 * d + 4 * d + 2 * d + 4 * d + 4 * d, 2 * f * d + 2 * d * d)

    def body(du_ref, w1_ref, x_ref, r_ref, g_ref, wo_ref, ya_ref, yb_ref, yc_ref, gg_ref,
             dx_ref, dxb_ref, dg_ref, da_ref, db_ref, dc_ref, dgg_ref):
        @pl.when(pl.program_id(0) == 0)
        def _():
            dg_ref[...] = jnp.zeros_like(dg_ref)
            dgg_ref[...] = jnp.zeros_like(dgg_ref)

        dh = _dot_nn(du_ref[...], w1_ref[...])
        xx = x_ref[...]
        r = _rms_scale(xx)
        n = xx * r
        dg_ref[...] += jnp.sum(dh * n, axis=0, keepdims=True)
        dx = r_ref[...] + _rms_bwd(n, r, dh * g_ref[...])
        dx_ref[...] = dx
        dxb = dx.astype(BF16)
        dxb_ref[...] = dxb

        dy = _dot_nt(dxb, wo_ref[...])
        gv = gg_ref[...]
        off = 0
        dgs = []
        for ref, out, w in zip((ya_ref, yb_ref, yc_ref), (da_ref, db_ref, dc_ref), widths):
            t = ref[...]
            r = _rms_scale(t)
            n = t * r
            dyg = dy[:, off:off + w]
            dgs.append(jnp.sum(dyg * n, axis=0, keepdims=True))
            out[...] = _rms_bwd(n, r, dyg * gv[:, off:off + w])
            off += w
        dgg_ref[...] += jnp.concatenate(dgs, axis=1)

    rows = lambda w: pl.BlockSpec((tm, w), lambda i: (i, 0))
    vec = pl.BlockSpec((1, d), lambda i: (0, 0))
    return _call(
        body,
        grid=(s // tm,),
        in_specs=[rows(f), _resident((f, d)), rows(d), rows(d), vec, _resident((d, d)),
                  rows(A_WIDTH), rows(CONV_CH), rows(C_WIDTH), vec],
        out_specs=[rows(d), rows(d), vec, rows(A_WIDTH), rows(CONV_CH), rows(C_WIDTH), vec],
        out_shape=[jax.ShapeDtypeStruct((s, d), F32), jax.ShapeDtypeStruct((s, d), BF16), jax.ShapeDtypeStruct((1, d), F32),
                   jax.ShapeDtypeStruct((s, A_WIDTH), F32), jax.ShapeDtypeStruct((s, CONV_CH), F32),
                   jax.ShapeDtypeStruct((s, C_WIDTH), F32), jax.ShapeDtypeStruct((1, d), F32)],
        operands=(du, wt1, x1, dres, g_mlp, wo, ya, yb, yc, gg), name=name, comm=comm)


CONV_CHUNK = 256
CONV_HALO = 8


def _conv_fwd(z, cw, name):
    s = z.shape[0]
    nch = s // CONV_CHUNK

    def body(gb_ref, gc_ref, xb_ref, w_ref, o_ref, us):
        us[pl.ds(0, CONV_HALO), :] = jnp.zeros((CONV_HALO, LANES), F32)
        us[pl.ds(CONV_HALO, s), :] = gc_ref[...] * xb_ref[...]
        w0, w1, w2 = w_ref[0:1, :], w_ref[1:2, :], w_ref[2:3, :]

        def chunk(c, carry):
            st = pl.multiple_of(c * CONV_CHUNK, CONV_CHUNK)
            ext = us[pl.ds(st, CONV_CHUNK + CONV_HALO), :]
            y = (w0 * ext[CONV_HALO - 2:CONV_HALO - 2 + CONV_CHUNK]
                 + w1 * ext[CONV_HALO - 1:CONV_HALO - 1 + CONV_CHUNK]
                 + w2 * ext[CONV_HALO:])
            o_ref[pl.ds(st, CONV_CHUNK), :] = gb_ref[pl.ds(st, CONV_CHUNK), :] * y
            return carry

        lax.fori_loop(0, nch, chunk, 0)

    col = lambda blk: pl.BlockSpec((s, LANES), lambda j, blk=blk: (0, blk + j))
    return pl.pallas_call(
        body,
        grid=(CONV_CH // LANES,),
        in_specs=[col(GB_BLK), col(GC_BLK), col(XB_BLK), pl.BlockSpec((3, LANES), lambda j: (0, j))],
        out_specs=pl.BlockSpec((s, LANES), lambda j: (0, j)),
        out_shape=jax.ShapeDtypeStruct((s, CONV_CH), F32),
        scratch_shapes=[pltpu.VMEM((s + CONV_HALO, LANES), F32)],
        compiler_params=_params("parallel"),
        name=name,
    )(z, z, z, cw)


def _conv_bwd(z, cw, dyb, dz, name):
    s = z.shape[0]
    nch = s // CONV_CHUNK
    ncol = CONV_CH // LANES

    def body(gb_ref, gc_ref, xb_ref, w_ref, dy_ref, dz_in, dz_ref, dw_ref, us, ds_, dgb_ref, dgc_ref, dxb_ref, sems):
        j = pl.program_id(0)

        def to_dz(staged, blk, k):
            cols = pl.ds(pl.multiple_of((blk + j) * LANES, LANES), LANES)
            return pltpu.make_async_copy(staged, dz_ref.at[:, cols], sems.at[k])

        copies = [to_dz(dgb_ref, GB_BLK, 0), to_dz(dgc_ref, GC_BLK, 1), to_dz(dxb_ref, XB_BLK, 2)]

        @pl.when(j > 0)
        def _():
            for cp in copies:
                cp.wait()

        us[pl.ds(0, CONV_HALO), :] = jnp.zeros((CONV_HALO, LANES), F32)
        us[pl.ds(CONV_HALO, s), :] = gc_ref[...] * xb_ref[...]
        ds_[pl.ds(s, CONV_HALO), :] = jnp.zeros((CONV_HALO, LANES), F32)
        ds_[pl.ds(0, s), :] = dy_ref[...] * gb_ref[...]
        w0, w1, w2 = w_ref[0:1, :], w_ref[1:2, :], w_ref[2:3, :]
        zero = jnp.zeros((1, LANES), F32)

        def chunk(c, carry):
            a0, a1, a2 = carry
            st = pl.multiple_of(c * CONV_CHUNK, CONV_CHUNK)
            rows = pl.ds(st, CONV_CHUNK)
            ext = us[pl.ds(st, CONV_CHUNK + CONV_HALO), :]
            um2 = ext[CONV_HALO - 2:CONV_HALO - 2 + CONV_CHUNK]
            um1 = ext[CONV_HALO - 1:CONV_HALO - 1 + CONV_CHUNK]
            u0 = ext[CONV_HALO:]
            dext = ds_[pl.ds(st, CONV_CHUNK + CONV_HALO), :]
            dc0 = dext[:CONV_CHUNK]
            du = w2 * dc0 + w1 * dext[1:1 + CONV_CHUNK] + w0 * dext[2:2 + CONV_CHUNK]
            yconv = w0 * um2 + w1 * um1 + w2 * u0
            dgb_ref[rows, :] = (dy_ref[rows, :] * yconv).astype(BF16)
            dgc_ref[rows, :] = (du * xb_ref[rows, :]).astype(BF16)
            dxb_ref[rows, :] = (du * gc_ref[rows, :]).astype(BF16)
            a0 = a0 + jnp.sum(dc0 * um2, axis=0, keepdims=True)
            a1 = a1 + jnp.sum(dc0 * um1, axis=0, keepdims=True)
            a2 = a2 + jnp.sum(dc0 * u0, axis=0, keepdims=True)
            return a0, a1, a2

        a0, a1, a2 = lax.fori_loop(0, nch, chunk, (zero, zero, zero))
        dw_ref[...] = jnp.concatenate([a0, a1, a2, jnp.zeros((5, LANES), F32)], axis=0)
        for cp in copies:
            cp.start()

        @pl.when(j == ncol - 1)
        def _():
            for cp in copies:
                cp.wait()

    col = lambda blk: pl.BlockSpec((s, LANES), lambda j, blk=blk: (0, blk + j))
    hbm = pl.BlockSpec(memory_space=pl.ANY)
    return pl.pallas_call(
        body,
        grid=(ncol,),
        in_specs=[col(GB_BLK), col(GC_BLK), col(XB_BLK), pl.BlockSpec((3, LANES), lambda j: (0, j)),
                  pl.BlockSpec((s, LANES), lambda j: (0, j)), hbm],
        out_specs=[hbm, pl.BlockSpec((8, LANES), lambda j: (0, j))],
        out_shape=[jax.ShapeDtypeStruct(dz.shape, dz.dtype), jax.ShapeDtypeStruct((8, CONV_CH), F32)],
        scratch_shapes=[pltpu.VMEM((s + CONV_HALO, LANES), F32), pltpu.VMEM((s + CONV_HALO, LANES), F32)]
        + [pltpu.VMEM((s, LANES), BF16)] * 3 + [pltpu.SemaphoreType.DMA((3,))],
        input_output_aliases={5: 0},
        compiler_params=_params("arbitrary"),
        name=name,
    )(z, z, z, cw, dyb, dz)


ATTN_ROWS = 512
ATTN_UNROLL = 8


def _band_rows(b, d, r):
    base = pl.multiple_of(b * (BLOCK * d), BLOCK)
    prev = jnp.maximum(base - BLOCK * d, 0)
    if d == 1:
        return pl.ds(base, BLOCK), pl.ds(pl.multiple_of(prev, BLOCK), BLOCK)
    return pl.ds(base + r, BLOCK, stride=d), pl.ds(prev + r, BLOCK, stride=d)


def _write_band_bias(bias_ref, max_dist):
    qi = lax.broadcasted_iota(jnp.int32, (BLOCK, 2 * BLOCK), 0)
    kj = lax.broadcasted_iota(jnp.int32, (BLOCK, 2 * BLOCK), 1)
    dist = BLOCK + qi - kj
    band = (dist >= 0) & (dist <= max_dist)
    bias_ref[0:BLOCK, :] = jnp.where(band, 0.0, -jnp.inf)
    bias_ref[BLOCK:2 * BLOCK, :] = jnp.where(band & (kj >= BLOCK), 0.0, -jnp.inf)


def _band_bias(bias_ref, b):
    bias = bias_ref[pl.ds(pl.multiple_of(jnp.where(b > 0, 0, BLOCK), BLOCK), BLOCK), :]
    return jnp.concatenate([bias, bias], axis=0)


def _kv_halves(pair):
    zero = jnp.zeros((1, LANES), jnp.int32)
    return zero + (pair >> 1), zero + ((pair + 1) >> 1)


def _stack_heads(t, head0, halves=None):
    top, bottom = jnp.where(head0, t, 0.0), jnp.where(head0, 0.0, t)
    if halves is not None:
        top = jnp.where(halves[0] == 1, pltpu.roll(top, HEAD_DIM, 1), top)
        bottom = jnp.where(halves[1] == 0, pltpu.roll(bottom, HEAD_DIM, 1), bottom)
    return jnp.concatenate([top, bottom], axis=0).astype(BF16)


def _unstack_heads(t, head0, halves=None):
    top, bottom = t[:BLOCK], t[BLOCK:]
    if halves is not None:
        top = jnp.where(halves[0] == 1, pltpu.roll(top, HEAD_DIM, 1), top)
        bottom = jnp.where(halves[1] == 0, pltpu.roll(bottom, HEAD_DIM, 1), bottom)
    return jnp.where(head0, top, bottom)


def _block_loops(s, patterns, unroll, one_block):
    for n, d in enumerate(patterns):
        nb = (s // BLOCK) // d
        ur = min(unroll, d)
        ub = unroll // ur
        for r0 in range(0, d, ur):
            def trip(i, carry, n=n, d=d, r0=r0, ur=ur, ub=ub):
                for u in range(ub):
                    for r in range(r0, r0 + ur):
                        one_block(i * ub + u, d, r, n == 0)
                return carry
            lax.fori_loop(0, nb // ub, trip, 0)


def _attn_fwd(z, m_init, l_init, q_blk, k_blk, v_blk, patterns, max_dist, gqa, name, comm=None):
    s = z.shape[0]
    npair = 3

    def body(q_ref, k_ref, v_ref, mi_ref, o_ref, lse0_ref, lse1_ref, bias_scr, m_scr, l_scr, *kv_scr):
        head0 = lax.broadcasted_iota(jnp.int32, (1, LANES), 1) < HEAD_DIM
        _write_band_bias(bias_scr, max_dist)
        ones = jnp.ones((2 * BLOCK, LANES), BF16)
        k_src, v_src = kv_scr if gqa else (k_ref, v_ref)
        if gqa:
            half = (lax.broadcasted_iota(jnp.int32, (1, LANES), 1) >= HEAD_DIM).astype(jnp.int32)
            swap = ((pl.program_id(0) + half) >> 1) != half

            def expand(c, carry):
                rows = pl.ds(pl.multiple_of(c * ATTN_ROWS, ATTN_ROWS), ATTN_ROWS)
                k_src[rows, :] = jnp.where(swap, pltpu.roll(k_ref[rows, :], HEAD_DIM, 1), k_ref[rows, :])
                v_src[rows, :] = jnp.where(swap, pltpu.roll(v_ref[rows, :], HEAD_DIM, 1), v_ref[rows, :])
                return carry

            lax.fori_loop(0, s // ATTN_ROWS, expand, 0)

        def one_block(b, d, r, first):
            rq, rp = _band_rows(b, d, r)
            q2 = _stack_heads(q_ref[rq, :] * SCALE, head0)
            k2 = jnp.concatenate([k_src[rp, :], k_src[rq, :]], axis=0).astype(BF16)
            v2 = jnp.concatenate([v_src[rp, :], v_src[rq, :]], axis=0).astype(BF16)
            sc = _dot_nt(q2, k2) + _band_bias(bias_scr, b)
            mb = jnp.max(sc, axis=1, keepdims=True)
            p = jnp.exp(sc - mb).astype(BF16)
            ob = _dot_nn(p, jnp.concatenate([v2, ones], axis=1))
            m_blk = _unstack_heads(jnp.broadcast_to(mb, (2 * BLOCK, LANES)), head0)
            l_blk = _unstack_heads(ob[:, LANES:], head0)
            o_blk = _unstack_heads(ob[:, :LANES], head0)
            if first and l_init == 0.0:
                m_new, l_new, o_new = m_blk, l_blk, o_blk
            else:
                if first:
                    m_old, l_old, o_old = jnp.broadcast_to(mi_ref[...], (BLOCK, LANES)), l_init, 0.0
                else:
                    m_old, l_old, o_old = m_scr[rq, :], l_scr[rq, :], o_ref[rq, :]
                m_new = jnp.maximum(m_old, m_blk)
                a_old = jnp.exp(m_old - m_new)
                a_blk = jnp.exp(m_blk - m_new)
                l_new = l_old * a_old + l_blk * a_blk
                o_new = o_old * a_old + o_blk * a_blk
            o_ref[rq, :], l_scr[rq, :], m_scr[rq, :] = o_new, l_new, m_new

        _block_loops(s, patterns, ATTN_UNROLL, one_block)

        def fin(c, carry):
            rows = pl.ds(pl.multiple_of(c * ATTN_ROWS, ATTN_ROWS), ATTN_ROWS)
            l = l_scr[rows, :]
            o_ref[rows, :] = o_ref[rows, :] / l
            lse = m_scr[rows, :] + jnp.log(l)
            swapped = pltpu.roll(lse, HEAD_DIM, 1)
            lse0_ref[rows, :] = jnp.where(head0, lse, swapped)
            lse1_ref[rows, :] = jnp.where(head0, swapped, lse)
            return carry

        lax.fori_loop(0, s // ATTN_ROWS, fin, 0)

    kv = (lambda blk: pl.BlockSpec((s, LANES), lambda j, blk=blk: (0, blk), pipeline_mode=pl.Buffered(1))) if gqa \
        else (lambda blk: pl.BlockSpec((s, LANES), lambda j, blk=blk: (0, blk + j)))
    own = pl.BlockSpec((s, LANES), lambda j: (0, j))
    return _call(
        body,
        grid=(npair,),
        in_specs=[pl.BlockSpec((s, LANES), lambda j: (0, q_blk + j)), kv(k_blk), kv(v_blk),
                  pl.BlockSpec((1, LANES), lambda j: (0, j))],
        out_specs=[own, own, own],
        out_shape=[jax.ShapeDtypeStruct((s, npair * LANES), F32)] * 3,
        operands=(z, z, z, m_init), name=name,
        scratch_shapes=[pltpu.VMEM((2 * BLOCK, 2 * BLOCK), F32)] + [pltpu.VMEM((s, LANES), F32)] * (4 if gqa else 2),
        comm=comm)


def _attn_bwd(z, do, o, lse, m_init, dz, q_blk, k_blk, v_blk, patterns, max_dist, gqa, name, comm=None):
    s = z.shape[0]
    npair = 3
    n_dz_in = 0 if dz is None else 1

    def body(q_ref, k_ref, v_ref, do_ref, o_ref, lse0_ref, lse1_ref, mi_ref, *rest):
        (dz_ref, dm_ref, dq_acc, dk_acc, dv_acc, dl0_scr, dl1_scr, bias_scr,
         dq_out, dk_out, dv_out, out_sems) = rest[n_dz_in:]
        pair = pl.program_id(0)
        head0 = lax.broadcasted_iota(jnp.int32, (1, LANES), 1) < HEAD_DIM
        halves = _kv_halves(pair) if gqa else None
        _write_band_bias(bias_scr, max_dist)

        def zero_kv():
            def f(c, carry):
                rows = pl.ds(pl.multiple_of(c * ATTN_ROWS, ATTN_ROWS), ATTN_ROWS)
                dk_acc[rows, :] = jnp.zeros((ATTN_ROWS, LANES), F32)
                dv_acc[rows, :] = jnp.zeros((ATTN_ROWS, LANES), F32)
                return carry
            lax.fori_loop(0, s // ATTN_ROWS, f, 0)

        if gqa:
            pl.when(pair == 0)(zero_kv)
        else:
            zero_kv()

        def prep(c, dm):
            rows = pl.ds(pl.multiple_of(c * ATTN_ROWS, ATTN_ROWS), ATTN_ROWS)
            dq_acc[rows, :] = jnp.zeros((ATTN_ROWS, LANES), F32)
            prod = do_ref[rows, :] * o_ref[rows, :]
            d0 = jnp.sum(jnp.where(head0, prod, 0.0), axis=1, keepdims=True)
            d1 = jnp.sum(jnp.where(head0, 0.0, prod), axis=1, keepdims=True)
            dl0_scr[rows, :] = jnp.broadcast_to(d0, (ATTN_ROWS, LANES))
            dl1_scr[rows, :] = jnp.broadcast_to(d1, (ATTN_ROWS, LANES))
            lse_own = jnp.where(head0, lse0_ref[rows, :], lse1_ref[rows, :])
            psink = jnp.exp(mi_ref[...] - lse_own)
            return dm - jnp.sum(psink * jnp.where(head0, d0, d1), axis=0, keepdims=True)

        dm_ref[...] = lax.fori_loop(0, s // ATTN_ROWS, prep, jnp.zeros((1, LANES), F32))

        def one_block(b, d, r, first):
            rq, rp = _band_rows(b, d, r)
            q2 = _stack_heads(q_ref[rq, :] * SCALE, head0, halves)
            do2 = _stack_heads(do_ref[rq, :], head0, halves)
            k2 = jnp.concatenate([k_ref[rp, :], k_ref[rq, :]], axis=0).astype(BF16)
            v2 = jnp.concatenate([v_ref[rp, :], v_ref[rq, :]], axis=0).astype(BF16)
            lse2 = jnp.concatenate([lse0_ref[rq, :], lse1_ref[rq, :]], axis=0)
            dl2 = jnp.concatenate([dl0_scr[rq, :], dl1_scr[rq, :]], axis=0)
            lse2 = jnp.concatenate([lse2, lse2], axis=1)
            dl2 = jnp.concatenate([dl2, dl2], axis=1)
            p = jnp.exp(_dot_nt(q2, k2) + _band_bias(bias_scr, b) - lse2)
            dp = _dot_nt(do2, v2)
            dsc = (p * (dp - dl2)).astype(BF16)
            dq2 = _unstack_heads(_dot_nn(dsc, k2), head0, halves)
            dk2 = _dot_tn(dsc, q2)
            dv2 = _dot_tn(p.astype(BF16), do2)
            dq_acc[rq, :] += dq2 * SCALE
            dk_acc[rp, :] += dk2[:BLOCK]
            dk_acc[rq, :] += dk2[BLOCK:]
            dv_acc[rp, :] += dv2[:BLOCK]
            dv_acc[rq, :] += dv2[BLOCK:]

        _block_loops(s, patterns, ATTN_UNROLL, one_block)

        def to_dz(staged, blk, k):
            cols = pl.ds(pl.multiple_of(blk * LANES, LANES), LANES)
            return pltpu.make_async_copy(staged, dz_ref.at[:, cols], out_sems.at[k])

        last_pair = pair == npair - 1
        q_copy = to_dz(dq_out, q_blk + pair, 0)
        kv_copies = [to_dz(dk_out, k_blk + (0 if gqa else pair), 1), to_dz(dv_out, v_blk + (0 if gqa else pair), 2)]

        @pl.when(pair > 0)
        def _():
            for cp in [q_copy] + ([] if gqa else kv_copies):
                cp.wait()

        def stage(acc, out):
            def f(c, carry):
                rows = pl.ds(pl.multiple_of(c * ATTN_ROWS, ATTN_ROWS), ATTN_ROWS)
                out[rows, :] = acc[rows, :].astype(BF16)
                return carry
            lax.fori_loop(0, s // ATTN_ROWS, f, 0)

        def stage_kv():
            stage(dk_acc, dk_out)
            stage(dv_acc, dv_out)
            for cp in kv_copies:
                cp.start()

        stage(dq_acc, dq_out)
        q_copy.start()
        if gqa:
            pl.when(last_pair)(stage_kv)
        else:
            stage_kv()

        @pl.when(last_pair)
        def _():
            for cp in [q_copy] + kv_copies:
                cp.wait()

    own = pl.BlockSpec((s, LANES), lambda j: (0, j))
    hbm = pl.BlockSpec(memory_space=pl.ANY)
    if gqa:
        kv = lambda blk: pl.BlockSpec((s, LANES), lambda j, blk=blk: (0, blk), pipeline_mode=pl.Buffered(1))
    else:
        kv = lambda blk: pl.BlockSpec((s, LANES), lambda j, blk=blk: (0, blk + j))
    in_specs = [pl.BlockSpec((s, LANES), lambda j: (0, q_blk + j)), kv(k_blk), kv(v_blk), own, own, own, own,
                pl.BlockSpec((1, LANES), lambda j: (0, j))]
    operands = (z, z, z, do, o, lse[0], lse[1], m_init)
    return _call(
        body,
        grid=(npair,),
        in_specs=in_specs + [hbm] * n_dz_in,
        out_specs=[hbm, pl.BlockSpec((1, LANES), lambda j: (0, j))],
        out_shape=[jax.ShapeDtypeStruct((s, IN_WIDTH), BF16), jax.ShapeDtypeStruct((1, npair * LANES), F32)],
        operands=operands + (() if dz is None else (dz,)), name=name,
        scratch_shapes=[pltpu.VMEM((s, LANES), F32)] * 5 + [pltpu.VMEM((2 * BLOCK, 2 * BLOCK), F32)]
        + [pltpu.VMEM((s, LANES), BF16)] * 3 + [pltpu.SemaphoreType.DMA((3,))],
        comm=comm, aliases={} if dz is None else {len(in_specs): 0})


def _adamw_math(w, g, m, v):
    m = ADAM_B1 * m + (1.0 - ADAM_B1) * g
    v = ADAM_B2 * v + (1.0 - ADAM_B2) * (g * g)
    m_hat = m / (1.0 - ADAM_B1 ** ADAM_STEP)
    v_hat = v / (1.0 - ADAM_B2 ** ADAM_STEP)
    delta = -ADAM_LR * (m_hat / (jnp.sqrt(v_hat) + ADAM_EPS) + ADAM_WD * w)
    return delta, m, v


def _adamw(w, g, m, v, name):
    rows, cols = w.shape
    tr = min(rows, 256)

    def body(w_ref, g_ref, m_ref, v_ref, d_ref, nm_ref, nv_ref):
        d_ref[...], nm_ref[...], nv_ref[...] = _adamw_math(w_ref[...], g_ref[...], m_ref[...], v_ref[...])

    spec = pl.BlockSpec((tr, cols), lambda i: (i, 0))
    return pl.pallas_call(
        body,
        grid=(rows // tr,),
        in_specs=[spec] * 4,
        out_specs=[spec] * 3,
        out_shape=[jax.ShapeDtypeStruct((rows, cols), F32)] * 3,
        compiler_params=_params("parallel"),
        name=name,
    )(w, g, m, v)


def _sum_adamw(parts, w, m, v, pos, transpose, name):
    assert len(parts) == DEPTH == 2
    (p0, r0), (p1, r1) = parts
    _, rows, cols = p0.shape
    tr = 256 if rows % 256 == 0 else rows
    nt = rows // tr

    def body(pos_ref, p0_ref, r0_ref, p1_ref, r1_ref, w_ref, m_ref, v_ref, g_ref, d_ref, nm_ref, nv_ref):
        def run(p_ref, r_ref):
            g = ((p_ref[...].astype(F32) + r_ref[0].astype(F32)) + r_ref[1].astype(F32)) + r_ref[2].astype(F32)
            if transpose:
                g = g.T
            g_ref[...] = g
            d_ref[...], nm_ref[...], nv_ref[...] = _adamw_math(w_ref[...], g, m_ref[...], v_ref[...])

        layer0 = pl.program_id(0) < nt
        pl.when(layer0)(lambda: run(p0_ref, r0_ref))
        pl.when(jnp.logical_not(layer0))(lambda: run(p1_ref, r1_ref))

    def tile0(i):
        return jnp.minimum(i, nt - 1)

    def tile1(i):
        return jnp.maximum(i - nt, 0)

    if transpose:
        w_spec = pl.BlockSpec((None, cols, tr), lambda i, q: (i // nt, 0, i % nt))
    else:
        w_spec = pl.BlockSpec((None, tr, cols), lambda i, q: (i // nt, i % nt, 0))
    return pl.pallas_call(
        body,
        grid_spec=pltpu.PrefetchScalarGridSpec(
            num_scalar_prefetch=1,
            grid=(DEPTH * nt,),
            in_specs=[pl.BlockSpec((None, tr, cols), lambda i, q: (q[0], tile0(i), 0)),
                      pl.BlockSpec((3, tr, cols), lambda i, q: (0, tile0(i), 0)),
                      pl.BlockSpec((None, tr, cols), lambda i, q: (q[0], tile1(i), 0)),
                      pl.BlockSpec((3, tr, cols), lambda i, q: (0, tile1(i), 0)),
                      w_spec, w_spec, w_spec],
            out_specs=[w_spec] * 4,
        ),
        out_shape=[jax.ShapeDtypeStruct(w.shape, F32)] * 4,
        compiler_params=_params("arbitrary"),
        name=name,
    )(pos, p0, r0, p1, r1, w, m, v)


def _small_sum_adamw(gathered, params, name):
    _, rows, cols = gathered.shape
    n = len(params)

    def body(ga_ref, *refs):
        ins, outs, (g_scr,) = refs[:3 * n], refs[3 * n:7 * n + 2], refs[7 * n + 2:]
        g = ga_ref[0]
        for i in range(1, N_DEV):
            g = g + ga_ref[i]
        g_scr[...] = g
        for k, (row0, w, _, _) in enumerate(params):
            w_ref, m_ref, v_ref = ins[3 * k:3 * k + 3]
            gk = g_scr[row0:row0 + w.shape[0], :]
            outs[4 * k][...] = gk
            outs[4 * k + 1][...], outs[4 * k + 2][...], outs[4 * k + 3][...] = _adamw_math(
                w_ref[...], gk, m_ref[...], v_ref[...])
        outs[4 * n][...] = g_scr[CONV_ROW:CONV_ROW + 8, :]
        outs[4 * n + 1][...] = g_scr[LOSS_ROW:LOSS_ROW + 1, :]

    out_shape = []
    for _, w, _, _ in params:
        out_shape += [jax.ShapeDtypeStruct(w.shape, F32)] * 4
    out_shape += [jax.ShapeDtypeStruct((8, cols), F32), jax.ShapeDtypeStruct((1, cols), F32)]
    res = pl.pallas_call(
        body,
        out_shape=out_shape,
        scratch_shapes=[pltpu.VMEM((rows, cols), F32)],
        name=name,
    )(gathered, *[t for _, w, m, v in params for t in (w, m, v)])
    return [res[4 * k:4 * k + 4] for k in range(n)], res[4 * n], res[4 * n + 1]


def _pair_sum(g4, r1, pos, name):
    _, _, rows, cols = g4.shape
    tr = min(rows, 512)

    def body(pos_ref, g_ref, r_ref, o_ref):
        o_ref[...] = (g_ref[...].astype(F32) + r_ref[...].astype(F32)).astype(BF16)

    return pl.pallas_call(
        body,
        grid_spec=pltpu.PrefetchScalarGridSpec(
            num_scalar_prefetch=1,
            grid=(4, rows // tr),
            in_specs=[pl.BlockSpec((None, None, tr, cols), lambda i, j, p: (i, p[1], j, 0)),
                      pl.BlockSpec((None, tr, cols), lambda i, j, p: (i, j, 0))],
            out_specs=pl.BlockSpec((None, tr, cols), lambda i, j, p: (i, j, 0)),
        ),
        out_shape=jax.ShapeDtypeStruct((4, rows, cols), BF16),
        compiler_params=_params("parallel", "parallel"),
        name=name,
    )(pos, g4, r1)


GATHER_ID, CHIP_ID, SIBLING_ID = 0, 1, 2


def _place():
    return lax.axis_index("x"), lax.axis_index("y"), lax.axis_index("c")


def _sibling():
    x, y, c = _place()
    return (x, y, 1 - c)


def _same_core_of_other_chips():
    x, y, c = _place()
    return [(1 - x, y, c), (x, 1 - y, c), (1 - x, 1 - y, c)]


def _gather_comm(shards):
    na = len(shards)

    def plan(ins, outs, sems):
        send_sems, recv_sems, local_sems = sems
        x, y, c = _place()
        me, sibling = (x, y, c), (x, y, 1 - c)
        chips = [(1 - x, y), (x, 1 - y), (1 - x, 1 - y)]

        def rows(a, px, py, pc):
            m = ins[a].shape[0]
            return outs[a].at[pl.ds((4 * px + 2 * py + pc) * m, m), :]

        def copy(a, k, block, to, src=None):
            return pltpu.make_async_remote_copy(
                src_ref=rows(a, *block) if src is None else src, dst_ref=rows(a, *block),
                send_sem=send_sems.at[a, k], recv_sem=recv_sems.at[a, k], device_id=to, device_id_type=MESH)

        mine = [pltpu.make_async_copy(ins[a], rows(a, *me), local_sems.at[a]) for a in range(na)]
        first = []
        for a in range(na):
            first.append(copy(a, 0, me, sibling, src=ins[a]))
            first += [copy(a, 1 + j, me, (*chip, c), src=ins[a]) for j, chip in enumerate(chips)]
        return me, sibling, chips, c, copy, mine, first

    def start(ins, outs, sems):
        *_, mine, first = plan(ins, outs, sems)
        for cp in mine + first:
            cp.start()

    def finish(ins, outs, sems):
        me, sibling, chips, c, copy, mine, first = plan(ins, outs, sems)
        passed = []
        for j, chip in enumerate(chips):
            for a in range(na):
                copy(a, 1 + j, (*chip, c), me).wait_recv()
                cp = copy(a, 4 + j, (*chip, c), sibling)
                cp.start()
                passed.append(cp)
        for a in range(na):
            copy(a, 0, sibling, me).wait_recv()
            for j, chip in enumerate(chips):
                copy(a, 4 + j, (*chip, 1 - c), me).wait_recv()
        for cp in first + passed:
            cp.wait_send()
        for cp in mine:
            cp.wait()

    return _Comm(tuple(shards),
                 tuple(jax.ShapeDtypeStruct((N_DEV * t.shape[0], t.shape[1]), t.dtype) for t in shards),
                 (pltpu.SemaphoreType.DMA((na, 7)), pltpu.SemaphoreType.DMA((na, 7)), pltpu.SemaphoreType.DMA((na,))),
                 start, finish, lambda: [_sibling()] + _same_core_of_other_chips(), GATHER_ID)


def _exchange_comm(arrays, out_shape, n_copies, copies_of, peers, collective_id):
    na = len(arrays)

    def every(ins, outs, sems):
        send_sems, recv_sems = sems
        return [cp for a in range(na) for cp in copies_of(ins, outs, a, send_sems, recv_sems)]

    def start(ins, outs, sems):
        for cp in every(ins, outs, sems):
            cp.start()

    def finish(ins, outs, sems):
        for cp in every(ins, outs, sems):
            cp.wait()

    return _Comm(tuple(arrays), tuple(out_shape),
                 (pltpu.SemaphoreType.DMA((na, n_copies)), pltpu.SemaphoreType.DMA((na, n_copies))), start, finish,
                 peers, collective_id)


def _sibling_comm(grads):
    def copies_of(ins, outs, a, send_sems, recv_sems):
        x, y, c = _place()
        return [pltpu.make_async_remote_copy(
            src_ref=ins[a].at[chip, 1 - c], dst_ref=outs[a].at[chip],
            send_sem=send_sems.at[a, chip], recv_sem=recv_sems.at[a, chip],
            device_id=(x, y, 1 - c), device_id_type=MESH) for chip in range(4)]

    return _exchange_comm(grads, [jax.ShapeDtypeStruct((4,) + t.shape[2:], t.dtype) for t in grads], 4, copies_of,
                          lambda: [_sibling()], SIBLING_ID)


def _chip_comm(partials):
    def copies_of(ins, outs, a, send_sems, recv_sems):
        x, y, c = _place()
        chips = [(1 - x, y), (x, 1 - y), (1 - x, 1 - y)]
        return [pltpu.make_async_remote_copy(
            src_ref=ins[a].at[2 * cx + cy], dst_ref=outs[a].at[k],
            send_sem=send_sems.at[a, k], recv_sem=recv_sems.at[a, k],
            device_id=(cx, cy, c), device_id_type=MESH) for k, (cx, cy) in enumerate(chips)]

    return _exchange_comm(partials, [jax.ShapeDtypeStruct((3,) + t.shape[1:], t.dtype) for t in partials], 3, copies_of,
                          _same_core_of_other_chips, CHIP_ID)


def _pad_rows(t, rows):
    return jnp.pad(t, ((0, rows - t.shape[0]), (0, D_MODEL - t.shape[1])))


MIX_ROW, GROUP_ROW, MLP_ROW, FINAL_ROW, CONV_ROW, SINK_ROW = 0, 8, 16, 24, 32, 40
LOSS_ROW = FINAL_ROW + 1


def _pack_small(g_mix, g_group, g_mlp, g_final, conv, sinks, loss):
    final_and_loss = jnp.concatenate([g_final.reshape(1, D_MODEL), _pad_rows(loss, 1)], axis=0)
    return jnp.concatenate([
        _pad_rows(g_mix, 8), _pad_rows(g_group, 8), _pad_rows(g_mlp, 8), _pad_rows(final_and_loss, 8),
        _pad_rows(conv.reshape(DEPTH * 3, CONV_CH), 8), _pad_rows(sinks.reshape(1, DEPTH * 6), 8)], axis=0)


def kernel(x, w_in, conv_w, sinks, g_mix, g_group, w_o, g_mlp, w_ff_in, w_ff_out, g_final, loss_target, m_w_in, m_conv_w, m_sinks, m_g_mix, m_g_group, m_w_o, m_g_mlp, m_w_ff_in, m_w_ff_out, m_g_final, v_w_in, v_conv_w, v_sinks, v_g_mix, v_g_group, v_w_o, v_g_mlp, v_w_ff_in, v_w_ff_out, v_g_final):
    ax, ay, ac = _place()
    chip = 2 * ax + ay
    dev = 4 * ax + 2 * ay + ac
    pos = jnp.stack([chip, ac]).astype(jnp.int32)

    x0 = x.reshape(SEQ, D_MODEL)
    target = loss_target.reshape(SEQ, D_MODEL)

    shards = {}
    for l in range(DEPTH):
        shards[l, 0], shards[l, 1] = w_in[l].T.astype(BF16), w_o[l].astype(BF16)
        shards[l, 2], shards[l, 3] = w_ff_in[l].T.astype(BF16), w_ff_out[l].astype(BF16)
    conv_tile = jnp.pad(conv_w.reshape(DEPTH * 3, CONV_CH // N_DEV), ((0, 2), (0, LANES - CONV_CH // N_DEV)))
    wt_in0, conv_all = _comm_only(_gather_comm([shards[0, 0], conv_tile]), "gather_first")
    conv_full = conv_all.reshape(N_DEV, 8, LANES)[:, :DEPTH * 3, :CONV_CH // N_DEV]
    conv_full = conv_full.transpose(1, 0, 2).reshape(DEPTH, 3, CONV_CH)

    dx, parts, small = _step(x0, target, shards, wt_in0, conv_full, sinks, g_mix, g_group, g_mlp, g_final, pos)
    return _finish(dx, parts, small, pos, dev, w_in, conv_w, sinks, g_mix, g_group, w_o, g_mlp, w_ff_in, w_ff_out, g_final, m_w_in, m_conv_w, m_sinks, m_g_mix, m_g_group, m_w_o, m_g_mlp, m_w_ff_in, m_w_ff_out, m_g_final, v_w_in, v_conv_w, v_sinks, v_g_mix, v_g_group, v_w_o, v_g_mlp, v_w_ff_in, v_w_ff_out, v_g_final)


FWD_CARRY = {(0, "in_proj"): ((1, 0),), (0, "window"): ((0, 1),), (0, "dilated"): ((0, 2),),
             (0, "mix_ff_in"): ((0, 3),), (0, "ff_out_in_proj"): ((1, 1), (1, 3)),
             (1, "dilated"): ((1, 2),)}


def _step(x0, target, shards, wt_in0, conv_full, sinks, g_mix, g_group, g_mlp, g_final, pos):
    sink_lanes = jnp.repeat(sinks.reshape(DEPTH, 6), HEAD_DIM, axis=1)
    no_sink = jnp.full((1, A_WIDTH), NEG_BIG, F32)
    full = {(0, 0): wt_in0}

    def gather(stage, l):
        keys = FWD_CARRY.get((l, stage), ())
        return keys, (_gather_comm([shards[k] for k in keys]) if keys else None)

    def landed(keys, got):
        full.update(zip(keys, got))

    saved = []
    xc = x0
    keys, comm = gather("in_proj", 0)
    (z, h), got = _norm_mm(xc, g_mix[0:1], full[0, 0], "in_proj_0", comm)
    landed(keys, got)
    for l in range(DEPTH):
        sink_l = sink_lanes[l:l + 1]
        keys, comm = gather("window", l)
        (yc, *lse_c), got = _attn_fwd(z, sink_l, 1.0, QC_BLK, KC_BLK, VC_BLK, (1,), C_MAX_DIST, True,
                                     f"window_attn_{l}", comm)
        landed(keys, got)
        yb = _conv_fwd(z, conv_full[l], f"conv_{l}")
        keys, comm = gather("dilated", l)
        (ya, *lse_a), got = _attn_fwd(z, no_sink, 0.0, QA_BLK, KA_BLK, VA_BLK, DILATED_PATTERNS, A_MAX_DIST, False,
                                     f"dilated_attn_{l}", comm)
        landed(keys, got)
        keys, comm = gather("mix_ff_in", l)
        (y, x1, a, h2), got = _mix_ff_in(ya, yb, yc, g_group[l:l + 1], full[l, 1], xc, g_mlp[l:l + 1], full[l, 2],
                                         f"mix_ff_in_{l}", comm)
        landed(keys, got)
        saved.append((xc, z, h, ya, lse_a, yb, yc, lse_c, sink_l, y, x1, a, h2))
        if l + 1 < DEPTH:
            keys, comm = gather("ff_out_in_proj", l)
            (xc, z, h), got = _ff_out_in_proj(a, full[l, 3], x1, g_mix[l + 1:l + 2], full[l + 1, 0],
                                              f"ff_out_{l}_in_proj_{l + 1}", comm)
            landed(keys, got)

    loss_slab, dx, dxb, dg_final, du = _mm_res_loss(a, full[DEPTH - 1, 3], x1, g_final.reshape(1, D_MODEL), target,
                                                    f"ff_out_{DEPTH - 1}_loss")

    def by_owner(t):
        return t.reshape(4, 2, t.shape[0] // N_DEV, D_MODEL)

    def pair(key, g, r1):
        return _pair_sum(g, r1, pos, f"grad_pair_sum_{key[0]}_{key[1]}")

    partial, r2 = {}, {}
    dg_mix, dg_group, dg_mlp, dconv, dsinks = [None] * DEPTH, [None] * DEPTH, [None] * DEPTH, [None] * DEPTH, [None] * DEPTH
    for l in reversed(range(DEPTH)):
        xin, z, h, ya, lse_a, yb, yc, lse_c, sink_l, y, x1, a, h2 = saved[l]
        if l + 1 < DEPTH:
            late = [(l + 1, 1), (l + 1, 0)]
            (du,), got = _mlp_bwd_act(dxb, full[l, 3], a, f"ff_out_bwd_{l}", _chip_comm([partial[k] for k in late]))
            r2.update(zip(late, got))
        (g3, g2), _ = _mm_tn([(a, dxb), (du, h2)], f"grad_w_ff_{l}")
        g3, g2 = by_owner(g3), by_owner(g2)
        (dx1, dx1b, dg_mlp[l], dya, dyb, dyc, dg_group[l]), got = _ff_in_mix_bwd(
            du, full[l, 2], x1, dx, g_mlp[l:l + 1], full[l, 1], ya, yb, yc, g_group[l:l + 1],
            f"ff_in_mix_bwd_{l}", _sibling_comm([g3, g2]))
        partial[l, 3], partial[l, 2] = pair((l, 3), g3, got[0]), pair((l, 2), g2, got[1])
        early = [(l, 3), (l, 2)]
        (dz, _), got = _attn_bwd(z, dya, ya, lse_a, no_sink, None, QA_BLK, KA_BLK, VA_BLK, DILATED_PATTERNS,
                                 A_MAX_DIST, False, f"dilated_attn_bwd_{l}", _chip_comm([partial[k] for k in early]))
        r2.update(zip(early, got))
        dz, dcw = _conv_bwd(z, conv_full[l], dyb, dz, f"conv_bwd_{l}")
        (dz, dsink), _ = _attn_bwd(z, dyc, yc, lse_c, sink_l, dz, QC_BLK, KC_BLK, VC_BLK, (1,), C_MAX_DIST,
                                   True, f"window_attn_bwd_{l}")
        (g1, g0), _ = _mm_tn([(y, dx1b), (dz, h)], f"grad_w_o_in_{l}")
        g1, g0 = by_owner(g1), by_owner(g0)
        if l > 0:
            (dx, dxb, dg_mix[l]), got = _mm_nn_normbwd(dz, full[l, 0], xin, dx1, g_mix[l:l + 1], f"in_proj_bwd_{l}",
                                                      _sibling_comm([g1, g0]))
            partial[l, 1], partial[l, 0] = pair((l, 1), g1, got[0]), pair((l, 0), g0, got[1])
        else:
            got = _comm_only(_sibling_comm([g1, g0]), "grad_sibling_exchange_last")
            partial[l, 1], partial[l, 0] = pair((l, 1), g1, got[0]), pair((l, 0), g0, got[1])
            (dx, dxb, dg_mix[l]), got = _mm_nn_normbwd(dz, full[l, 0], xin, dx1, g_mix[l:l + 1], f"in_proj_bwd_{l}",
                                                      _chip_comm([partial[l, 1], partial[l, 0]]))
            r2[l, 1], r2[l, 0] = got
        dconv[l] = dcw[:3]
        dsinks[l] = dsink[0, ::HEAD_DIM]
    parts = {key: (partial[key], r2[key]) for key in partial}
    small = _pack_small(jnp.concatenate(dg_mix), jnp.concatenate(dg_group), jnp.concatenate(dg_mlp),
                        dg_final, jnp.stack(dconv), jnp.stack(dsinks), loss_slab[0:1])
    return dx, parts, small


def _finish(dx, parts, small, pos, dev, w_in, conv_w, sinks, g_mix, g_group, w_o, g_mlp, w_ff_in, w_ff_out, g_final, m_w_in, m_conv_w, m_sinks, m_g_mix, m_g_group, m_w_o, m_g_mlp, m_w_ff_in, m_w_ff_out, m_g_final, v_w_in, v_conv_w, v_sinks, v_g_mix, v_g_group, v_w_o, v_g_mlp, v_w_ff_in, v_w_ff_out, v_g_final):
    grad_x = dx.reshape(1, SEQ, D_MODEL)

    (small_all,) = _comm_only(_gather_comm([small]), "gather_small_grads")
    row = lambda t: t.reshape(1, D_MODEL)
    sink_row = lambda t: _pad_rows(t.reshape(1, DEPTH * 6), 1)
    params = [(MIX_ROW, g_mix, m_g_mix, v_g_mix), (GROUP_ROW, g_group, m_g_group, v_g_group),
              (MLP_ROW, g_mlp, m_g_mlp, v_g_mlp), (FINAL_ROW, row(g_final), row(m_g_final), row(v_g_final)),
              (SINK_ROW, sink_row(sinks), sink_row(m_sinks), sink_row(v_sinks))]
    updated, conv_rows, loss_row = _small_sum_adamw(small_all.reshape(N_DEV, SMALL_ROWS, D_MODEL), params, "small_adamw")
    loss = loss_row[0, 0]
    (grad_g_mix, delta_g_mix, new_m_g_mix, new_v_g_mix), (grad_g_group, delta_g_group, new_m_g_group, new_v_g_group), \
        (grad_g_mlp, delta_g_mlp, new_m_g_mlp, new_v_g_mlp), final4, sinks4 = updated
    grad_g_final, delta_g_final, new_m_g_final, new_v_g_final = [t.reshape(D_MODEL) for t in final4]
    grad_sinks, delta_sinks, new_m_sinks, new_v_sinks = [t[0, :DEPTH * 6].reshape(DEPTH, 2, 3) for t in sinks4]
    conv_grad_full = conv_rows[:DEPTH * 3, :CONV_CH].reshape(DEPTH, 3, CONV_CH)
    cs = CONV_CH // N_DEV
    grad_conv_w = lax.dynamic_slice_in_dim(conv_grad_full, dev * cs, cs, axis=2)

    def tile_of(t):
        return jnp.pad(t.reshape(1, DEPTH * 3 * cs), ((0, 7), (0, 256 - DEPTH * 3 * cs)))

    cd, cm, cv = _adamw(tile_of(conv_w), tile_of(grad_conv_w), tile_of(m_conv_w), tile_of(v_conv_w), "conv_adamw")
    untile = lambda t: t[0, :DEPTH * 3 * cs].reshape(DEPTH, 3, cs)
    delta_conv_w, new_m_conv_w, new_v_conv_w = untile(cd), untile(cm), untile(cv)

    def big(kind, w, m, v, transpose, name):
        return _sum_adamw([parts[l, kind] for l in range(DEPTH)], w, m, v, pos, transpose, name)

    grad_w_in, delta_w_in, new_m_w_in, new_v_w_in = big(0, w_in, m_w_in, v_w_in, True, "adamw_w_in")
    grad_w_o, delta_w_o, new_m_w_o, new_v_w_o = big(1, w_o, m_w_o, v_w_o, False, "adamw_w_o")
    grad_w_ff_in, delta_w_ff_in, new_m_w_ff_in, new_v_w_ff_in = big(2, w_ff_in, m_w_ff_in, v_w_ff_in, True, "adamw_w_ff_in")
    grad_w_ff_out, delta_w_ff_out, new_m_w_ff_out, new_v_w_ff_out = big(3, w_ff_out, m_w_ff_out, v_w_ff_out, False,
                                                                         "adamw_w_ff_out")

    return (loss, grad_x, grad_w_in, grad_conv_w, grad_sinks, grad_g_mix, grad_g_group, grad_w_o, grad_g_mlp,
            grad_w_ff_in, grad_w_ff_out, grad_g_final,
            delta_w_in, delta_conv_w, delta_sinks, delta_g_mix, delta_g_group, delta_w_o, delta_g_mlp,
            delta_w_ff_in, delta_w_ff_out, delta_g_final,
            new_m_w_in, new_m_conv_w, new_m_sinks, new_m_g_mix, new_m_g_group, new_m_w_o, new_m_g_mlp,
            new_m_w_ff_in, new_m_w_ff_out, new_m_g_final,
            new_v_w_in, new_v_conv_w, new_v_sinks, new_v_g_mix, new_v_g_group, new_v_w_o, new_v_g_mlp,
            new_v_w_ff_in, new_v_w_ff_out, new_v_g_final)
```

```python
from typing import Callable, NamedTuple

import jax
import jax.numpy as jnp
from jax import lax
from jax.experimental import pallas as pl
from jax.experimental.pallas import tpu as pltpu

F32 = jnp.float32
BF16 = jnp.bfloat16
MESH = pl.DeviceIdType.MESH

N_DEV = 8
SEQ = 4096
D_MODEL = 1024
DEPTH = 2
HEAD_DIM = 64
LANES = 128
A_WIDTH = 384
CONV_CH = 256
C_WIDTH = 384
IN_WIDTH = 2560
BLOCK = 128
DILATED_PATTERNS = (1, 4, 16)
A_MAX_DIST = 128
C_MAX_DIST = 127
EPS = 1e-6
SCALE = HEAD_DIM ** -0.5
NEG_BIG = -1e30
F32_TINY = 1.1754944e-38

QA_BLK, KA_BLK, VA_BLK = 0, 3, 6
GB_BLK, GC_BLK, XB_BLK = 9, 11, 13
QC_BLK, KC_BLK, VC_BLK = 15, 18, 19

ADAM_LR = 0.001
ADAM_B1 = 0.9
ADAM_B2 = 0.999
ADAM_EPS = 1e-08
ADAM_WD = 0.01
ADAM_STEP = 10

VMEM_LIMIT = 56 * 1024 * 1024
TILE_BUDGET = 46 * 1024 * 1024
ROW_TILE = 512
COL_CHUNK = 512
SMALL_ROWS = 48


def _dot_nn(a, b):
    return lax.dot_general(a, b, (((1,), (0,)), ((), ())), preferred_element_type=F32)


def _dot_nt(a, b):
    return lax.dot_general(a, b, (((1,), (1,)), ((), ())), preferred_element_type=F32)


def _dot_tn(a, b):
    return lax.dot_general(a, b, (((0,), (0,)), ((), ())), preferred_element_type=F32)


def _params(*sem, collective_id=None):
    return pltpu.CompilerParams(dimension_semantics=sem, vmem_limit_bytes=VMEM_LIMIT, collective_id=collective_id)


def _resident(shape):
    return pl.BlockSpec(shape, lambda i: (0,) * len(shape), pipeline_mode=pl.Buffered(1))


def _row_tile(row_bytes, resident_bytes):
    for tm in (ROW_TILE, ROW_TILE // 2):
        if 2 * tm * row_bytes + resident_bytes <= TILE_BUDGET:
            return tm
    return ROW_TILE // 4


def _rms_scale(t):
    return lax.rsqrt(jnp.mean(t * t, axis=-1, keepdims=True) + EPS)


def _rms_bwd(n, r, dn):
    return r * (dn - n * jnp.mean(dn * n, axis=-1, keepdims=True))


class _Comm(NamedTuple):
    arrays: tuple
    out_shape: tuple
    sems: tuple
    start: Callable
    finish: Callable
    peers: Callable
    collective_id: int


def _handshake(comm):
    barrier = pltpu.get_barrier_semaphore()
    peers = comm.peers()
    for peer in peers:
        pl.semaphore_signal(barrier, inc=1, device_id=peer, device_id_type=MESH)
    pl.semaphore_wait(barrier, len(peers))


def _call(body, grid, in_specs, out_specs, out_shape, operands, name, scratch_shapes=(), comm=None, aliases=None):
    n_in, n_out, n_scr = len(in_specs), len(out_shape), len(scratch_shapes)
    aliases = dict(aliases or {})
    if comm is None:
        res = pl.pallas_call(body, grid=grid, in_specs=list(in_specs), out_specs=list(out_specs),
                             out_shape=list(out_shape), scratch_shapes=list(scratch_shapes),
                             input_output_aliases=aliases,
                             compiler_params=_params("arbitrary"), name=name)(*operands)
        return list(res), []
    c_in, c_out = len(comm.arrays), len(comm.out_shape)
    hbm = pl.BlockSpec(memory_space=pl.ANY)
    last = grid[0] - 1

    def carried(*refs):
        ins, cins = refs[:n_in], refs[n_in:n_in + c_in]
        o0 = n_in + c_in
        outs, couts = refs[o0:o0 + n_out], refs[o0 + n_out:o0 + n_out + c_out]
        s0 = o0 + n_out + c_out
        scr, sems = refs[s0:s0 + n_scr], refs[s0 + n_scr:]
        @pl.when(pl.program_id(0) == 0)
        def _():
            _handshake(comm)
            comm.start(cins, couts, sems)

        body(*ins, *outs, *scr)
        pl.when(pl.program_id(0) == last)(lambda: comm.finish(cins, couts, sems))

    res = pl.pallas_call(carried, grid=grid, in_specs=list(in_specs) + [hbm] * c_in,
                         out_specs=list(out_specs) + [hbm] * c_out, out_shape=list(out_shape) + list(comm.out_shape),
                         scratch_shapes=list(scratch_shapes) + list(comm.sems), input_output_aliases=aliases,
                         compiler_params=_params("arbitrary", collective_id=comm.collective_id),
                         name=name)(*operands, *comm.arrays)
    return list(res[:n_out]), list(res[n_out:])


def _comm_only(comm, name):
    hbm = pl.BlockSpec(memory_space=pl.ANY)
    c_in, c_out = len(comm.arrays), len(comm.out_shape)

    def body(*refs):
        ins, outs, sems = refs[:c_in], refs[c_in:c_in + c_out], refs[c_in + c_out:]
        _handshake(comm)
        comm.start(ins, outs, sems)
        comm.finish(ins, outs, sems)

    return pl.pallas_call(body, in_specs=[hbm] * c_in, out_specs=[hbm] * c_out, out_shape=list(comm.out_shape),
                          scratch_shapes=list(comm.sems),
                          compiler_params=pltpu.CompilerParams(collective_id=comm.collective_id),
                          name=name)(*comm.arrays)


def _norm_mm(x, g, wt, name, comm=None):
    s, d = x.shape
    n = wt.shape[0]
    tm = _row_tile(4 * d + 4 * n + 2 * d, 2 * n * d)

    def body(x_ref, g_ref, w_ref, o_ref, h_ref):
        xx = x_ref[...]
        h = ((xx * _rms_scale(xx)) * g_ref[...]).astype(BF16)
        h_ref[...] = h
        for n0 in range(0, n, COL_CHUNK):
            o_ref[:, n0:n0 + COL_CHUNK] = _dot_nt(h, w_ref[n0:n0 + COL_CHUNK, :])

    return _call(
        body,
        grid=(s // tm,),
        in_specs=[pl.BlockSpec((tm, d), lambda i: (i, 0)),
                  pl.BlockSpec((1, d), lambda i: (0, 0)),
                  _resident((n, d))],
        out_specs=[pl.BlockSpec((tm, n), lambda i: (i, 0)),
                   pl.BlockSpec((tm, d), lambda i: (i, 0))],
        out_shape=[jax.ShapeDtypeStruct((s, n), F32), jax.ShapeDtypeStruct((s, d), BF16)],
        operands=(x, g, wt), name=name, comm=comm)


def _ff_out_in_proj(a, w2, x1, g, wt, name, comm=None):
    s, f = a.shape
    d = w2.shape[1]
    n = wt.shape[0]
    tm = _row_tile(2 * f + 4 * d + 4 * d + 4 * n + 2 * d, 2 * f * d + 2 * n * d)

    def body(a_ref, w2_ref, x_ref, g_ref, w_ref, x2_ref, z_ref, h_ref):
        x2 = x_ref[...] + _dot_nn(a_ref[...], w2_ref[...])
        x2_ref[...] = x2
        h = ((x2 * _rms_scale(x2)) * g_ref[...]).astype(BF16)
        h_ref[...] = h
        for n0 in range(0, n, COL_CHUNK):
            z_ref[:, n0:n0 + COL_CHUNK] = _dot_nt(h, w_ref[n0:n0 + COL_CHUNK, :])

    rows = lambda w: pl.BlockSpec((tm, w), lambda i: (i, 0))
    return _call(
        body,
        grid=(s // tm,),
        in_specs=[rows(f), _resident((f, d)), rows(d), pl.BlockSpec((1, d), lambda i: (0, 0)), _resident((n, d))],
        out_specs=[rows(d), rows(n), rows(d)],
        out_shape=[jax.ShapeDtypeStruct((s, d), F32), jax.ShapeDtypeStruct((s, n), F32),
                   jax.ShapeDtypeStruct((s, d), BF16)],
        operands=(a, w2, x1, g, wt), name=name, comm=comm)


def _mix_ff_in(ya, yb, yc, gg, wo, x0, g_mlp, wt1, name, comm=None):
    s = ya.shape[0]
    d = wo.shape[1]
    f = wt1.shape[0]
    tm = _row_tile(4 * d + 4 * d + 2 * d + 4 * d + 2 * d + 2 * f, 2 * d * d + 2 * f * d)

    def body(ya_ref, yb_ref, yc_ref, gg_ref, wo_ref, x_ref, g_ref, w1_ref, y_ref, x1_ref, a_ref, h_ref):
        parts = []
        for ref in (ya_ref, yb_ref, yc_ref):
            t = ref[...]
            parts.append(t * _rms_scale(t))
        y = (jnp.concatenate(parts, axis=1) * gg_ref[...]).astype(BF16)
        y_ref[...] = y
        x1 = x_ref[...] + _dot_nn(y, wo_ref[...])
        x1_ref[...] = x1
        h = ((x1 * _rms_scale(x1)) * g_ref[...]).astype(BF16)
        h_ref[...] = h
        for n0 in range(0, f, COL_CHUNK):
            u = _dot_nt(h, w1_ref[n0:n0 + COL_CHUNK, :])
            a_ref[:, n0:n0 + COL_CHUNK] = jnp.square(jnp.maximum(u, 0.0)).astype(BF16)

    rows = lambda w: pl.BlockSpec((tm, w), lambda i: (i, 0))
    vec = pl.BlockSpec((1, d), lambda i: (0, 0))
    return _call(
        body,
        grid=(s // tm,),
        in_specs=[rows(A_WIDTH), rows(CONV_CH), rows(C_WIDTH), vec, _resident((d, d)), rows(d), vec, _resident((f, d))],
        out_specs=[rows(d), rows(d), rows(f), rows(d)],
        out_shape=[jax.ShapeDtypeStruct((s, d), BF16), jax.ShapeDtypeStruct((s, d), F32),
                   jax.ShapeDtypeStruct((s, f), BF16), jax.ShapeDtypeStruct((s, d), BF16)],
        operands=(ya, yb, yc, gg, wo, x0, g_mlp, wt1), name=name, comm=comm)


def _relu_from_square(av):
    return av * lax.rsqrt(jnp.maximum(av, F32_TINY))


def _mm_res_loss(a, w2, x1, g, target, name):
    s, f = a.shape
    d = w2.shape[1]
    tm = _row_tile(2 * f + 4 * d + 4 * d + 4 * d + 2 * d + 2 * f, 2 * f * d)

    def body(a_ref, w_ref, x_ref, g_ref, t_ref, loss_ref, dx_ref, dxb_ref, dg_ref, du_ref):
        @pl.when(pl.program_id(0) == 0)
        def _():
            loss_ref[...] = jnp.zeros_like(loss_ref)
            dg_ref[...] = jnp.zeros_like(dg_ref)

        xx = x_ref[...] + _dot_nn(a_ref[...], w_ref[...])
        r = _rms_scale(xx)
        n = xx * r
        gv = g_ref[...]
        err = n * gv - t_ref[...]
        per_tok = jnp.sum(err * err, axis=1, keepdims=True) * (1.0 / d)
        loss_ref[...] += 0.5 * jnp.sum(per_tok, axis=0, keepdims=True)
        dout = err * (1.0 / d)
        dg_ref[...] += jnp.sum(dout * n, axis=0, keepdims=True)
        dx = _rms_bwd(n, r, dout * gv)
        dx_ref[...] = dx
        dxb = dx.astype(BF16)
        dxb_ref[...] = dxb
        for n0 in range(0, f, COL_CHUNK):
            da = _dot_nt(dxb, w_ref[n0:n0 + COL_CHUNK, :])
            rl = _relu_from_square(a_ref[:, n0:n0 + COL_CHUNK].astype(F32))
            du_ref[:, n0:n0 + COL_CHUNK] = (da * (2.0 * rl)).astype(BF16)

    rows = lambda w: pl.BlockSpec((tm, w), lambda i: (i, 0))
    vec = pl.BlockSpec((1, d), lambda i: (0, 0))
    return pl.pallas_call(
        body,
        grid=(s // tm,),
        in_specs=[rows(f), _resident((f, d)), rows(d), vec, rows(d)],
        out_specs=[pl.BlockSpec((8, LANES), lambda i: (0, 0)), rows(d), rows(d), vec, rows(f)],
        out_shape=[jax.ShapeDtypeStruct((8, LANES), F32), jax.ShapeDtypeStruct((s, d), F32),
                   jax.ShapeDtypeStruct((s, d), BF16), jax.ShapeDtypeStruct((1, d), F32),
                   jax.ShapeDtypeStruct((s, f), BF16)],
        compiler_params=_params("arbitrary"),
        name=name,
    )(a, w2, x1, g, target)


def _mlp_bwd_act(dxb, w2, a, name, comm=None):
    s, d = dxb.shape
    f = w2.shape[0]
    tm = _row_tile(2 * d + 2 * f + 2 * f, 2 * f * d)

    def body(dx_ref, w_ref, a_ref, du_ref):
        dx = dx_ref[...]
        for n0 in range(0, f, COL_CHUNK):
            da = _dot_nt(dx, w_ref[n0:n0 + COL_CHUNK, :])
            rl = _relu_from_square(a_ref[:, n0:n0 + COL_CHUNK].astype(F32))
            du_ref[:, n0:n0 + COL_CHUNK] = (da * (2.0 * rl)).astype(BF16)

    return _call(
        body,
        grid=(s // tm,),
        in_specs=[pl.BlockSpec((tm, d), lambda i: (i, 0)),
                  _resident((f, d)),
                  pl.BlockSpec((tm, f), lambda i: (i, 0))],
        out_specs=[pl.BlockSpec((tm, f), lambda i: (i, 0))],
        out_shape=[jax.ShapeDtypeStruct((s, f), BF16)],
        operands=(dxb, w2, a), name=name, comm=comm)


def _mm_tn(pairs, name, comm=None):
    s, d = pairs[0][1].shape
    tn = 512
    tiles = [a.shape[1] // tn for a, _ in pairs]
    starts = [sum(tiles[:k]) for k in range(len(pairs))]

    def body(*refs):
        ins, outs, acc = refs[:2 * len(pairs)], refs[2 * len(pairs):3 * len(pairs)], refs[3 * len(pairs)]
        j = pl.program_id(0)
        for k in range(len(pairs)):
            def run(a_ref=ins[2 * k], b_ref=ins[2 * k + 1], o_ref=outs[k]):
                for k0 in range(0, s, ROW_TILE):
                    part = _dot_tn(a_ref[k0:k0 + ROW_TILE, :], b_ref[k0:k0 + ROW_TILE, :])
                    if k0 == 0:
                        acc[...] = part
                    else:
                        acc[...] += part
                o_ref[...] = acc[...].astype(BF16)

            pl.when((j >= starts[k]) & (j < starts[k] + tiles[k]))(run)

    def tile_of(k):
        return lambda j: jnp.clip(j - starts[k], 0, tiles[k] - 1)

    in_specs, out_specs = [], []
    for k in range(len(pairs)):
        in_specs += [pl.BlockSpec((s, tn), lambda j, t=tile_of(k): (0, t(j))), _resident((s, d))]
        out_specs.append(pl.BlockSpec((tn, d), lambda j, t=tile_of(k): (t(j), 0)))
    return _call(
        body,
        grid=(sum(tiles),),
        in_specs=in_specs,
        out_specs=out_specs,
        out_shape=[jax.ShapeDtypeStruct((a.shape[1], d), BF16) for a, _ in pairs],
        operands=tuple(t for pair in pairs for t in pair), name=name,
        scratch_shapes=[pltpu.VMEM((tn, d), F32)], comm=comm)


def _mm_nn_normbwd(dact, wt, x, dres, g, name, comm=None):
    s, kdim = dact.shape
    d = wt.shape[1]
    tm = _row_tile(2 * kdim + 4 * d + 4 * d + 4 * d + 2 * d, 2 * kdim * d)

    def body(a_ref, w_ref, x_ref, r_ref, g_ref, o_ref, ob_ref, dg_ref):
        @pl.when(pl.program_id(0) == 0)
        def _():
            dg_ref[...] = jnp.zeros_like(dg_ref)

        dh = _dot_nn(a_ref[...], w_ref[...])
        xx = x_ref[...]
        r = _rms_scale(xx)
        n = xx * r
        dg_ref[...] += jnp.sum(dh * n, axis=0, keepdims=True)
        dx = r_ref[...] + _rms_bwd(n, r, dh * g_ref[...])
        o_ref[...] = dx
        ob_ref[...] = dx.astype(BF16)

    return _call(
        body,
        grid=(s // tm,),
        in_specs=[pl.BlockSpec((tm, kdim), lambda i: (i, 0)),
                  _resident((kdim, d)),
                  pl.BlockSpec((tm, d), lambda i: (i, 0)),
                  pl.BlockSpec((tm, d), lambda i: (i, 0)),
                  pl.BlockSpec((1, d), lambda i: (0, 0))],
        out_specs=[pl.BlockSpec((tm, d), lambda i: (i, 0)),
                   pl.BlockSpec((tm, d), lambda i: (i, 0)),
                   pl.BlockSpec((1, d), lambda i: (0, 0))],
        out_shape=[jax.ShapeDtypeStruct((s, d), F32), jax.ShapeDtypeStruct((s, d), BF16),
                   jax.ShapeDtypeStruct((1, d), F32)],
        operands=(dact, wt, x, dres, g), name=name, comm=comm)


def _ff_in_mix_bwd(du, wt1, x1, dres, g_mlp, wo, ya, yb, yc, gg, name, comm=None):
    s, f = du.shape
    d = wt1.shape[1]
    widths = (A_WIDTH, CONV_CH, C_WIDTH)
    tm = _row_tile(2 * f + 4 * d + 4 * d + 4 * d + 2 * d + 4 * d + 4 * d, 2 * f * d + 2 * d * d)

    def body(du_ref, w1_ref, x_ref, r_ref, g_ref, wo_ref, ya_ref, yb_ref, yc_ref, gg_ref,
             dx_ref, dxb_ref, dg_ref, da_ref, db_ref, dc_ref, dgg_ref):
        @pl.when(pl.program_id(0) == 0)
        def _():
            dg_ref[...] = jnp.zeros_like(dg_ref)
            dgg_ref[...] = jnp.zeros_like(dgg_ref)

        dh = _dot_nn(du_ref[...], w1_ref[...])
        xx = x_ref[...]
        r = _rms_scale(xx)
        n = xx * r
        dg_ref[...] += jnp.sum(dh * n, axis=0, keepdims=True)
        dx = r_ref[...] + _rms_bwd(n, r, dh * g_ref[...])
        dx_ref[...] = dx
        dxb = dx.astype(BF16)
        dxb_ref[...] = dxb

        dy = _dot_nt(dxb, wo_ref[...])
        gv = gg_ref[...]
        off = 0
        dgs = []
        for ref, out, w in zip((ya_ref, yb_ref, yc_ref), (da_ref, db_ref, dc_ref), widths):
            t = ref[...]
            r = _rms_scale(t)
            n = t * r
            dyg = dy[:, off:off + w]
            dgs.append(jnp.sum(dyg * n, axis=0, keepdims=True))
            out[...] = _rms_bwd(n, r, dyg * gv[:, off:off + w])
            off += w
        dgg_ref[...] += jnp.concatenate(dgs, axis=1)

    rows = lambda w: pl.BlockSpec((tm, w), lambda i: (i, 0))
    vec = pl.BlockSpec((1, d), lambda i: (0, 0))
    return _call(
        body,
        grid=(s // tm,),
        in_specs=[rows(f), _resident((f, d)), rows(d), rows(d), vec, _resident((d, d)),
                  rows(A_WIDTH), rows(CONV_CH), rows(C_WIDTH), vec],
        out_specs=[rows(d), rows(d), vec, rows(A_WIDTH), rows(CONV_CH), rows(C_WIDTH), vec],
        out_shape=[jax.ShapeDtypeStruct((s, d), F32), jax.ShapeDtypeStruct((s, d), BF16), jax.ShapeDtypeStruct((1, d), F32),
                   jax.ShapeDtypeStruct((s, A_WIDTH), F32), jax.ShapeDtypeStruct((s, CONV_CH), F32),
                   jax.ShapeDtypeStruct((s, C_WIDTH), F32), jax.ShapeDtypeStruct((1, d), F32)],
        operands=(du, wt1, x1, dres, g_mlp, wo, ya, yb, yc, gg), name=name, comm=comm)


CONV_CHUNK = 256
CONV_HALO = 8


def _conv_fwd(z, cw, name):
    s = z.shape[0]
    nch = s // CONV_CHUNK

    def body(gb_ref, gc_ref, xb_ref, w_ref, o_ref, us):
        us[pl.ds(0, CONV_HALO), :] = jnp.zeros((CONV_HALO, LANES), F32)
        us[pl.ds(CONV_HALO, s), :] = gc_ref[...] * xb_ref[...]
        w0, w1, w2 = w_ref[0:1, :], w_ref[1:2, :], w_ref[2:3, :]

        def chunk(c, carry):
            st = pl.multiple_of(c * CONV_CHUNK, CONV_CHUNK)
            ext = us[pl.ds(st, CONV_CHUNK + CONV_HALO), :]
            y = (w0 * ext[CONV_HALO - 2:CONV_HALO - 2 + CONV_CHUNK]
                 + w1 * ext[CONV_HALO - 1:CONV_HALO - 1 + CONV_CHUNK]
                 + w2 * ext[CONV_HALO:])
            o_ref[pl.ds(st, CONV_CHUNK), :] = gb_ref[pl.ds(st, CONV_CHUNK), :] * y
            return carry

        lax.fori_loop(0, nch, chunk, 0)

    col = lambda blk: pl.BlockSpec((s, LANES), lambda j, blk=blk: (0, blk + j))
    return pl.pallas_call(
        body,
        grid=(CONV_CH // LANES,),
        in_specs=[col(GB_BLK), col(GC_BLK), col(XB_BLK), pl.BlockSpec((3, LANES), lambda j: (0, j))],
        out_specs=pl.BlockSpec((s, LANES), lambda j: (0, j)),
        out_shape=jax.ShapeDtypeStruct((s, CONV_CH), F32),
        scratch_shapes=[pltpu.VMEM((s + CONV_HALO, LANES), F32)],
        compiler_params=_params("parallel"),
        name=name,
    )(z, z, z, cw)


def _conv_bwd(z, cw, dyb, dz, name):
    s = z.shape[0]
    nch = s // CONV_CHUNK
    ncol = CONV_CH // LANES

    def body(gb_ref, gc_ref, xb_ref, w_ref, dy_ref, dz_in, dz_ref, dw_ref, us, ds_, dgb_ref, dgc_ref, dxb_ref, sems):
        j = pl.program_id(0)

        def to_dz(staged, blk, k):
            cols = pl.ds(pl.multiple_of((blk + j) * LANES, LANES), LANES)
            return pltpu.make_async_copy(staged, dz_ref.at[:, cols], sems.at[k])

        copies = [to_dz(dgb_ref, GB_BLK, 0), to_dz(dgc_ref, GC_BLK, 1), to_dz(dxb_ref, XB_BLK, 2)]

        @pl.when(j > 0)
        def _():
            for cp in copies:
                cp.wait()

        us[pl.ds(0, CONV_HALO), :] = jnp.zeros((CONV_HALO, LANES), F32)
        us[pl.ds(CONV_HALO, s), :] = gc_ref[...] * xb_ref[...]
        ds_[pl.ds(s, CONV_HALO), :] = jnp.zeros((CONV_HALO, LANES), F32)
        ds_[pl.ds(0, s), :] = dy_ref[...] * gb_ref[...]
        w0, w1, w2 = w_ref[0:1, :], w_ref[1:2, :], w_ref[2:3, :]
        zero = jnp.zeros((1, LANES), F32)

        def chunk(c, carry):
            a0, a1, a2 = carry
            st = pl.multiple_of(c * CONV_CHUNK, CONV_CHUNK)
            rows = pl.ds(st, CONV_CHUNK)
            ext = us[pl.ds(st, CONV_CHUNK + CONV_HALO), :]
            um2 = ext[CONV_HALO - 2:CONV_HALO - 2 + CONV_CHUNK]
            um1 = ext[CONV_HALO - 1:CONV_HALO - 1 + CONV_CHUNK]
            u0 = ext[CONV_HALO:]
            dext = ds_[pl.ds(st, CONV_CHUNK + CONV_HALO), :]
            dc0 = dext[:CONV_CHUNK]
            du = w2 * dc0 + w1 * dext[1:1 + CONV_CHUNK] + w0 * dext[2:2 + CONV_CHUNK]
            yconv = w0 * um2 + w1 * um1 + w2 * u0
            dgb_ref[rows, :] = (dy_ref[rows, :] * yconv).astype(BF16)
            dgc_ref[rows, :] = (du * xb_ref[rows, :]).astype(BF16)
            dxb_ref[rows, :] = (du * gc_ref[rows, :]).astype(BF16)
            a0 = a0 + jnp.sum(dc0 * um2, axis=0, keepdims=True)
            a1 = a1 + jnp.sum(dc0 * um1, axis=0, keepdims=True)
            a2 = a2 + jnp.sum(dc0 * u0, axis=0, keepdims=True)
            return a0, a1, a2

        a0, a1, a2 = lax.fori_loop(0, nch, chunk, (zero, zero, zero))
        dw_ref[...] = jnp.concatenate([a0, a1, a2, jnp.zeros((5, LANES), F32)], axis=0)
        for cp in copies:
            cp.start()

        @pl.when(j == ncol - 1)
        def _():
            for cp in copies:
                cp.wait()

    col = lambda blk: pl.BlockSpec((s, LANES), lambda j, blk=blk: (0, blk + j))
    hbm = pl.BlockSpec(memory_space=pl.ANY)
    return pl.pallas_call(
        body,
        grid=(ncol,),
        in_specs=[col(GB_BLK), col(GC_BLK), col(XB_BLK), pl.BlockSpec((3, LANES), lambda j: (0, j)),
                  pl.BlockSpec((s, LANES), lambda j: (0, j)), hbm],
        out_specs=[hbm, pl.BlockSpec((8, LANES), lambda j: (0, j))],
        out_shape=[jax.ShapeDtypeStruct(dz.shape, dz.dtype), jax.ShapeDtypeStruct((8, CONV_CH), F32)],
        scratch_shapes=[pltpu.VMEM((s + CONV_HALO, LANES), F32), pltpu.VMEM((s + CONV_HALO, LANES), F32)]
        + [pltpu.VMEM((s, LANES), BF16)] * 3 + [pltpu.SemaphoreType.DMA((3,))],
        input_output_aliases={5: 0},
        compiler_params=_params("arbitrary"),
        name=name,
    )(z, z, z, cw, dyb, dz)


ATTN_ROWS = 512
ATTN_UNROLL = 8


def _band_rows(b, d, r):
    base = pl.multiple_of(b * (BLOCK * d), BLOCK)
    prev = jnp.maximum(base - BLOCK * d, 0)
    if d == 1:
        return pl.ds(base, BLOCK), pl.ds(pl.multiple_of(prev, BLOCK), BLOCK)
    return pl.ds(base + r, BLOCK, stride=d), pl.ds(prev + r, BLOCK, stride=d)


def _write_band_bias(bias_ref, max_dist):
    qi = lax.broadcasted_iota(jnp.int32, (BLOCK, 2 * BLOCK), 0)
    kj = lax.broadcasted_iota(jnp.int32, (BLOCK, 2 * BLOCK), 1)
    dist = BLOCK + qi - kj
    band = (dist >= 0) & (dist <= max_dist)
    bias_ref[0:BLOCK, :] = jnp.where(band, 0.0, -jnp.inf)
    bias_ref[BLOCK:2 * BLOCK, :] = jnp.where(band & (kj >= BLOCK), 0.0, -jnp.inf)


def _band_bias(bias_ref, b):
    bias = bias_ref[pl.ds(pl.multiple_of(jnp.where(b > 0, 0, BLOCK), BLOCK), BLOCK), :]
    return jnp.concatenate([bias, bias], axis=0)


def _kv_halves(pair):
    zero = jnp.zeros((1, LANES), jnp.int32)
    return zero + (pair >> 1), zero + ((pair + 1) >> 1)


def _stack_heads(t, head0, halves=None):
    top, bottom = jnp.where(head0, t, 0.0), jnp.where(head0, 0.0, t)
    if halves is not None:
        top = jnp.where(halves[0] == 1, pltpu.roll(top, HEAD_DIM, 1), top)
        bottom = jnp.where(halves[1] == 0, pltpu.roll(bottom, HEAD_DIM, 1), bottom)
    return jnp.concatenate([top, bottom], axis=0).astype(BF16)


def _unstack_heads(t, head0, halves=None):
    top, bottom = t[:BLOCK], t[BLOCK:]
    if halves is not None:
        top = jnp.where(halves[0] == 1, pltpu.roll(top, HEAD_DIM, 1), top)
        bottom = jnp.where(halves[1] == 0, pltpu.roll(bottom, HEAD_DIM, 1), bottom)
    return jnp.where(head0, top, bottom)


def _block_loops(s, patterns, unroll, one_block):
    for n, d in enumerate(patterns):
        nb = (s // BLOCK) // d
        ur = min(unroll, d)
        ub = unroll // ur
        for r0 in range(0, d, ur):
            def trip(i, carry, n=n, d=d, r0=r0, ur=ur, ub=ub):
                for u in range(ub):
                    for r in range(r0, r0 + ur):
                        one_block(i * ub + u, d, r, n == 0)
                return carry
            lax.fori_loop(0, nb // ub, trip, 0)


def _attn_fwd(z, m_init, l_init, q_blk, k_blk, v_blk, patterns, max_dist, gqa, name, comm=None):
    s = z.shape[0]
    npair = 3

    def body(q_ref, k_ref, v_ref, mi_ref, o_ref, lse0_ref, lse1_ref, bias_scr, m_scr, l_scr, *kv_scr):
        head0 = lax.broadcasted_iota(jnp.int32, (1, LANES), 1) < HEAD_DIM
        _write_band_bias(bias_scr, max_dist)
        ones = jnp.ones((2 * BLOCK, LANES), BF16)
        k_src, v_src = kv_scr if gqa else (k_ref, v_ref)
        if gqa:
            half = (lax.broadcasted_iota(jnp.int32, (1, LANES), 1) >= HEAD_DIM).astype(jnp.int32)
            swap = ((pl.program_id(0) + half) >> 1) != half

            def expand(c, carry):
                rows = pl.ds(pl.multiple_of(c * ATTN_ROWS, ATTN_ROWS), ATTN_ROWS)
                k_src[rows, :] = jnp.where(swap, pltpu.roll(k_ref[rows, :], HEAD_DIM, 1), k_ref[rows, :])
                v_src[rows, :] = jnp.where(swap, pltpu.roll(v_ref[rows, :], HEAD_DIM, 1), v_ref[rows, :])
                return carry

            lax.fori_loop(0, s // ATTN_ROWS, expand, 0)

        def one_block(b, d, r, first):
            rq, rp = _band_rows(b, d, r)
            q2 = _stack_heads(q_ref[rq, :] * SCALE, head0)
            k2 = jnp.concatenate([k_src[rp, :], k_src[rq, :]], axis=0).astype(BF16)
            v2 = jnp.concatenate([v_src[rp, :], v_src[rq, :]], axis=0).astype(BF16)
            sc = _dot_nt(q2, k2) + _band_bias(bias_scr, b)
            mb = jnp.max(sc, axis=1, keepdims=True)
            p = jnp.exp(sc - mb).astype(BF16)
            ob = _dot_nn(p, jnp.concatenate([v2, ones], axis=1))
            m_blk = _unstack_heads(jnp.broadcast_to(mb, (2 * BLOCK, LANES)), head0)
            l_blk = _unstack_heads(ob[:, LANES:], head0)
            o_blk = _unstack_heads(ob[:, :LANES], head0)
            if first and l_init == 0.0:
                m_new, l_new, o_new = m_blk, l_blk, o_blk
            else:
                if first:
                    m_old, l_old, o_old = jnp.broadcast_to(mi_ref[...], (BLOCK, LANES)), l_init, 0.0
                else:
                    m_old, l_old, o_old = m_scr[rq, :], l_scr[rq, :], o_ref[rq, :]
                m_new = jnp.maximum(m_old, m_blk)
                a_old = jnp.exp(m_old - m_new)
                a_blk = jnp.exp(m_blk - m_new)
                l_new = l_old * a_old + l_blk * a_blk
                o_new = o_old * a_old + o_blk * a_blk
            o_ref[rq, :], l_scr[rq, :], m_scr[rq, :] = o_new, l_new, m_new

        _block_loops(s, patterns, ATTN_UNROLL, one_block)

        def fin(c, carry):
            rows = pl.ds(pl.multiple_of(c * ATTN_ROWS, ATTN_ROWS), ATTN_ROWS)
            l = l_scr[rows, :]
            o_ref[rows, :] = o_ref[rows, :] / l
            lse = m_scr[rows, :] + jnp.log(l)
            swapped = pltpu.roll(lse, HEAD_DIM, 1)
            lse0_ref[rows, :] = jnp.where(head0, lse, swapped)
            lse1_ref[rows, :] = jnp.where(head0, swapped, lse)
            return carry

        lax.fori_loop(0, s // ATTN_ROWS, fin, 0)

    kv = (lambda blk: pl.BlockSpec((s, LANES), lambda j, blk=blk: (0, blk), pipeline_mode=pl.Buffered(1))) if gqa \
        else (lambda blk: pl.BlockSpec((s, LANES), lambda j, blk=blk: (0, blk + j)))
    own = pl.BlockSpec((s, LANES), lambda j: (0, j))
    return _call(
        body,
        grid=(npair,),
        in_specs=[pl.BlockSpec((s, LANES), lambda j: (0, q_blk + j)), kv(k_blk), kv(v_blk),
                  pl.BlockSpec((1, LANES), lambda j: (0, j))],
        out_specs=[own, own, own],
        out_shape=[jax.ShapeDtypeStruct((s, npair * LANES), F32)] * 3,
        operands=(z, z, z, m_init), name=name,
        scratch_shapes=[pltpu.VMEM((2 * BLOCK, 2 * BLOCK), F32)] + [pltpu.VMEM((s, LANES), F32)] * (4 if gqa else 2),
        comm=comm)


def _attn_bwd(z, do, o, lse, m_init, dz, q_blk, k_blk, v_blk, patterns, max_dist, gqa, name, comm=None):
    s = z.shape[0]
    npair = 3
    n_dz_in = 0 if dz is None else 1

    def body(q_ref, k_ref, v_ref, do_ref, o_ref, lse0_ref, lse1_ref, mi_ref, *rest):
        (dz_ref, dm_ref, dq_acc, dk_acc, dv_acc, dl0_scr, dl1_scr, bias_scr,
         dq_out, dk_out, dv_out, out_sems) = rest[n_dz_in:]
        pair = pl.program_id(0)
        head0 = lax.broadcasted_iota(jnp.int32, (1, LANES), 1) < HEAD_DIM
        halves = _kv_halves(pair) if gqa else None
        _write_band_bias(bias_scr, max_dist)

        def zero_kv():
            def f(c, carry):
                rows = pl.ds(pl.multiple_of(c * ATTN_ROWS, ATTN_ROWS), ATTN_ROWS)
                dk_acc[rows, :] = jnp.zeros((ATTN_ROWS, LANES), F32)
                dv_acc[rows, :] = jnp.zeros((ATTN_ROWS, LANES), F32)
                return carry
            lax.fori_loop(0, s // ATTN_ROWS, f, 0)

        if gqa:
            pl.when(pair == 0)(zero_kv)
        else:
            zero_kv()

        def prep(c, dm):
            rows = pl.ds(pl.multiple_of(c * ATTN_ROWS, ATTN_ROWS), ATTN_ROWS)
            dq_acc[rows, :] = jnp.zeros((ATTN_ROWS, LANES), F32)
            prod = do_ref[rows, :] * o_ref[rows, :]
            d0 = jnp.sum(jnp.where(head0, prod, 0.0), axis=1, keepdims=True)
            d1 = jnp.sum(jnp.where(head0, 0.0, prod), axis=1, keepdims=True)
            dl0_scr[rows, :] = jnp.broadcast_to(d0, (ATTN_ROWS, LANES))
            dl1_scr[rows, :] = jnp.broadcast_to(d1, (ATTN_ROWS, LANES))
            lse_own = jnp.where(head0, lse0_ref[rows, :], lse1_ref[rows, :])
            psink = jnp.exp(mi_ref[...] - lse_own)
            return dm - jnp.sum(psink * jnp.where(head0, d0, d1), axis=0, keepdims=True)

        dm_ref[...] = lax.fori_loop(0, s // ATTN_ROWS, prep, jnp.zeros((1, LANES), F32))

        def one_block(b, d, r, first):
            rq, rp = _band_rows(b, d, r)
            q2 = _stack_heads(q_ref[rq, :] * SCALE, head0, halves)
            do2 = _stack_heads(do_ref[rq, :], head0, halves)
            k2 = jnp.concatenate([k_ref[rp, :], k_ref[rq, :]], axis=0).astype(BF16)
            v2 = jnp.concatenate([v_ref[rp, :], v_ref[rq, :]], axis=0).astype(BF16)
            lse2 = jnp.concatenate([lse0_ref[rq, :], lse1_ref[rq, :]], axis=0)
            dl2 = jnp.concatenate([dl0_scr[rq, :], dl1_scr[rq, :]], axis=0)
            lse2 = jnp.concatenate([lse2, lse2], axis=1)
            dl2 = jnp.concatenate([dl2, dl2], axis=1)
            p = jnp.exp(_dot_nt(q2, k2) + _band_bias(bias_scr, b) - lse2)
            dp = _dot_nt(do2, v2)
            dsc = (p * (dp - dl2)).astype(BF16)
            dq2 = _unstack_heads(_dot_nn(dsc, k2), head0, halves)
            dk2 = _dot_tn(dsc, q2)
            dv2 = _dot_tn(p.astype(BF16), do2)
            dq_acc[rq, :] += dq2 * SCALE
            dk_acc[rp, :] += dk2[:BLOCK]
            dk_acc[rq, :] += dk2[BLOCK:]
            dv_acc[rp, :] += dv2[:BLOCK]
            dv_acc[rq, :] += dv2[BLOCK:]

        _block_loops(s, patterns, ATTN_UNROLL, one_block)

        def to_dz(staged, blk, k):
            cols = pl.ds(pl.multiple_of(blk * LANES, LANES), LANES)
            return pltpu.make_async_copy(staged, dz_ref.at[:, cols], out_sems.at[k])

        last_pair = pair == npair - 1
        q_copy = to_dz(dq_out, q_blk + pair, 0)
        kv_copies = [to_dz(dk_out, k_blk + (0 if gqa else pair), 1), to_dz(dv_out, v_blk + (0 if gqa else pair), 2)]

        @pl.when(pair > 0)
        def _():
            for cp in [q_copy] + ([] if gqa else kv_copies):
                cp.wait()

        def stage(acc, out):
            def f(c, carry):
                rows = pl.ds(pl.multiple_of(c * ATTN_ROWS, ATTN_ROWS), ATTN_ROWS)
                out[rows, :] = acc[rows, :].astype(BF16)
                return carry
            lax.fori_loop(0, s // ATTN_ROWS, f, 0)

        def stage_kv():
            stage(dk_acc, dk_out)
            stage(dv_acc, dv_out)
            for cp in kv_copies:
                cp.start()

        stage(dq_acc, dq_out)
        q_copy.start()
        if gqa:
            pl.when(last_pair)(stage_kv)
        else:
            stage_kv()

        @pl.when(last_pair)
        def _():
            for cp in [q_copy] + kv_copies:
                cp.wait()

    own = pl.BlockSpec((s, LANES), lambda j: (0, j))
    hbm = pl.BlockSpec(memory_space=pl.ANY)
    if gqa:
        kv = lambda blk: pl.BlockSpec((s, LANES), lambda j, blk=blk: (0, blk), pipeline_mode=pl.Buffered(1))
    else:
        kv = lambda blk: pl.BlockSpec((s, LANES), lambda j, blk=blk: (0, blk + j))
    in_specs = [pl.BlockSpec((s, LANES), lambda j: (0, q_blk + j)), kv(k_blk), kv(v_blk), own, own, own, own,
                pl.BlockSpec((1, LANES), lambda j: (0, j))]
    operands = (z, z, z, do, o, lse[0], lse[1], m_init)
    return _call(
        body,
        grid=(npair,),
        in_specs=in_specs + [hbm] * n_dz_in,
        out_specs=[hbm, pl.BlockSpec((1, LANES), lambda j: (0, j))],
        out_shape=[jax.ShapeDtypeStruct((s, IN_WIDTH), BF16), jax.ShapeDtypeStruct((1, npair * LANES), F32)],
        operands=operands + (() if dz is None else (dz,)), name=name,
        scratch_shapes=[pltpu.VMEM((s, LANES), F32)] * 5 + [pltpu.VMEM((2 * BLOCK, 2 * BLOCK), F32)]
        + [pltpu.VMEM((s, LANES), BF16)] * 3 + [pltpu.SemaphoreType.DMA((3,))],
        comm=comm, aliases={} if dz is None else {len(in_specs): 0})


def _adamw_math(w, g, m, v):
    m = ADAM_B1 * m + (1.0 - ADAM_B1) * g
    v = ADAM_B2 * v + (1.0 - ADAM_B2) * (g * g)
    m_hat = m / (1.0 - ADAM_B1 ** ADAM_STEP)
    v_hat = v / (1.0 - ADAM_B2 ** ADAM_STEP)
    delta = -ADAM_LR * (m_hat / (jnp.sqrt(v_hat) + ADAM_EPS) + ADAM_WD * w)
    return delta, m, v


def _adamw(w, g, m, v, name):
    rows, cols = w.shape
    tr = min(rows, 256)

    def body(w_ref, g_ref, m_ref, v_ref, d_ref, nm_ref, nv_ref):
        d_ref[...], nm_ref[...], nv_ref[...] = _adamw_math(w_ref[...], g_ref[...], m_ref[...], v_ref[...])

    spec = pl.BlockSpec((tr, cols), lambda i: (i, 0))
    return pl.pallas_call(
        body,
        grid=(rows // tr,),
        in_specs=[spec] * 4,
        out_specs=[spec] * 3,
        out_shape=[jax.ShapeDtypeStruct((rows, cols), F32)] * 3,
        compiler_params=_params("parallel"),
        name=name,
    )(w, g, m, v)


def _sum_adamw(parts, w, m, v, pos, transpose, name):
    assert len(parts) == DEPTH == 2
    (p0, r0), (p1, r1) = parts
    _, rows, cols = p0.shape
    tr = 256 if rows % 256 == 0 else rows
    nt = rows // tr

    def body(pos_ref, p0_ref, r0_ref, p1_ref, r1_ref, w_ref, m_ref, v_ref, g_ref, d_ref, nm_ref, nv_ref):
        def run(p_ref, r_ref):
            g = ((p_ref[...].astype(F32) + r_ref[0].astype(F32)) + r_ref[1].astype(F32)) + r_ref[2].astype(F32)
            if transpose:
                g = g.T
            g_ref[...] = g
            d_ref[...], nm_ref[...], nv_ref[...] = _adamw_math(w_ref[...], g, m_ref[...], v_ref[...])

        layer0 = pl.program_id(0) < nt
        pl.when(layer0)(lambda: run(p0_ref, r0_ref))
        pl.when(jnp.logical_not(layer0))(lambda: run(p1_ref, r1_ref))

    def tile0(i):
        return jnp.minimum(i, nt - 1)

    def tile1(i):
        return jnp.maximum(i - nt, 0)

    if transpose:
        w_spec = pl.BlockSpec((None, cols, tr), lambda i, q: (i // nt, 0, i % nt))
    else:
        w_spec = pl.BlockSpec((None, tr, cols), lambda i, q: (i // nt, i % nt, 0))
    return pl.pallas_call(
        body,
        grid_spec=pltpu.PrefetchScalarGridSpec(
            num_scalar_prefetch=1,
            grid=(DEPTH * nt,),
            in_specs=[pl.BlockSpec((None, tr, cols), lambda i, q: (q[0], tile0(i), 0)),
                      pl.BlockSpec((3, tr, cols), lambda i, q: (0, tile0(i), 0)),
                      pl.BlockSpec((None, tr, cols), lambda i, q: (q[0], tile1(i), 0)),
                      pl.BlockSpec((3, tr, cols), lambda i, q: (0, tile1(i), 0)),
                      w_spec, w_spec, w_spec],
            out_specs=[w_spec] * 4,
        ),
        out_shape=[jax.ShapeDtypeStruct(w.shape, F32)] * 4,
        compiler_params=_params("arbitrary"),
        name=name,
    )(pos, p0, r0, p1, r1, w, m, v)


def _small_sum_adamw(gathered, params, name):
    _, rows, cols = gathered.shape
    n = len(params)

    def body(ga_ref, *refs):
        ins, outs, (g_scr,) = refs[:3 * n], refs[3 * n:7 * n + 2], refs[7 * n + 2:]
        g = ga_ref[0]
        for i in range(1, N_DEV):
            g = g + ga_ref[i]
        g_scr[...] = g
        for k, (row0, w, _, _) in enumerate(params):
            w_ref, m_ref, v_ref = ins[3 * k:3 * k + 3]
            gk = g_scr[row0:row0 + w.shape[0], :]
            outs[4 * k][...] = gk
            outs[4 * k + 1][...], outs[4 * k + 2][...], outs[4 * k + 3][...] = _adamw_math(
                w_ref[...], gk, m_ref[...], v_ref[...])
        outs[4 * n][...] = g_scr[CONV_ROW:CONV_ROW + 8, :]
        outs[4 * n + 1][...] = g_scr[LOSS_ROW:LOSS_ROW + 1, :]

    out_shape = []
    for _, w, _, _ in params:
        out_shape += [jax.ShapeDtypeStruct(w.shape, F32)] * 4
    out_shape += [jax.ShapeDtypeStruct((8, cols), F32), jax.ShapeDtypeStruct((1, cols), F32)]
    res = pl.pallas_call(
        body,
        out_shape=out_shape,
        scratch_shapes=[pltpu.VMEM((rows, cols), F32)],
        name=name,
    )(gathered, *[t for _, w, m, v in params for t in (w, m, v)])
    return [res[4 * k:4 * k + 4] for k in range(n)], res[4 * n], res[4 * n + 1]


def _pair_sum(g4, r1, pos, name):
    _, _, rows, cols = g4.shape
    tr = min(rows, 512)

    def body(pos_ref, g_ref, r_ref, o_ref):
        o_ref[...] = (g_ref[...].astype(F32) + r_ref[...].astype(F32)).astype(BF16)

    return pl.pallas_call(
        body,
        grid_spec=pltpu.PrefetchScalarGridSpec(
            num_scalar_prefetch=1,
            grid=(4, rows // tr),
            in_specs=[pl.BlockSpec((None, None, tr, cols), lambda i, j, p: (i, p[1], j, 0)),
                      pl.BlockSpec((None, tr, cols), lambda i, j, p: (i, j, 0))],
            out_specs=pl.BlockSpec((None, tr, cols), lambda i, j, p: (i, j, 0)),
        ),
        out_shape=jax.ShapeDtypeStruct((4, rows, cols), BF16),
        compiler_params=_params("parallel", "parallel"),
        name=name,
    )(pos, g4, r1)


GATHER_ID, CHIP_ID, SIBLING_ID = 0, 1, 2


def _place():
    return lax.axis_index("x"), lax.axis_index("y"), lax.axis_index("c")


def _sibling():
    x, y, c = _place()
    return (x, y, 1 - c)


def _same_core_of_other_chips():
    x, y, c = _place()
    return [(1 - x, y, c), (x, 1 - y, c), (1 - x, 1 - y, c)]


def _gather_comm(shards):
    na = len(shards)

    def plan(ins, outs, sems):
        send_sems, recv_sems, local_sems = sems
        x, y, c = _place()
        me, sibling = (x, y, c), (x, y, 1 - c)
        chips = [(1 - x, y), (x, 1 - y), (1 - x, 1 - y)]

        def rows(a, px, py, pc):
            m = ins[a].shape[0]
            return outs[a].at[pl.ds((4 * px + 2 * py + pc) * m, m), :]

        def copy(a, k, block, to, src=None):
            return pltpu.make_async_remote_copy(
                src_ref=rows(a, *block) if src is None else src, dst_ref=rows(a, *block),
                send_sem=send_sems.at[a, k], recv_sem=recv_sems.at[a, k], device_id=to, device_id_type=MESH)

        mine = [pltpu.make_async_copy(ins[a], rows(a, *me), local_sems.at[a]) for a in range(na)]
        first = []
        for a in range(na):
            first.append(copy(a, 0, me, sibling, src=ins[a]))
            first += [copy(a, 1 + j, me, (*chip, c), src=ins[a]) for j, chip in enumerate(chips)]
        return me, sibling, chips, c, copy, mine, first

    def start(ins, outs, sems):
        *_, mine, first = plan(ins, outs, sems)
        for cp in mine + first:
            cp.start()

    def finish(ins, outs, sems):
        me, sibling, chips, c, copy, mine, first = plan(ins, outs, sems)
        passed = []
        for j, chip in enumerate(chips):
            for a in range(na):
                copy(a, 1 + j, (*chip, c), me).wait_recv()
                cp = copy(a, 4 + j, (*chip, c), sibling)
                cp.start()
                passed.append(cp)
        for a in range(na):
            copy(a, 0, sibling, me).wait_recv()
            for j, chip in enumerate(chips):
                copy(a, 4 + j, (*chip, 1 - c), me).wait_recv()
        for cp in first + passed:
            cp.wait_send()
        for cp in mine:
            cp.wait()

    return _Comm(tuple(shards),
                 tuple(jax.ShapeDtypeStruct((N_DEV * t.shape[0], t.shape[1]), t.dtype) for t in shards),
                 (pltpu.SemaphoreType.DMA((na, 7)), pltpu.SemaphoreType.DMA((na, 7)), pltpu.SemaphoreType.DMA((na,))),
                 start, finish, lambda: [_sibling()] + _same_core_of_other_chips(), GATHER_ID)


def _exchange_comm(arrays, out_shape, n_copies, copies_of, peers, collective_id):
    na = len(arrays)

    def every(ins, outs, sems):
        send_sems, recv_sems = sems
        return [cp for a in range(na) for cp in copies_of(ins, outs, a, send_sems, recv_sems)]

    def start(ins, outs, sems):
        for cp in every(ins, outs, sems):
            cp.start()

    def finish(ins, outs, sems):
        for cp in every(ins, outs, sems):
            cp.wait()

    return _Comm(tuple(arrays), tuple(out_shape),
                 (pltpu.SemaphoreType.DMA((na, n_copies)), pltpu.SemaphoreType.DMA((na, n_copies))), start, finish,
                 peers, collective_id)


def _sibling_comm(grads):
    def copies_of(ins, outs, a, send_sems, recv_sems):
        x, y, c = _place()
        return [pltpu.make_async_remote_copy(
            src_ref=ins[a].at[chip, 1 - c], dst_ref=outs[a].at[chip],
            send_sem=send_sems.at[a, chip], recv_sem=recv_sems.at[a, chip],
            device_id=(x, y, 1 - c), device_id_type=MESH) for chip in range(4)]

    return _exchange_comm(grads, [jax.ShapeDtypeStruct((4,) + t.shape[2:], t.dtype) for t in grads], 4, copies_of,
                          lambda: [_sibling()], SIBLING_ID)


def _chip_comm(partials):
    def copies_of(ins, outs, a, send_sems, recv_sems):
        x, y, c = _place()
        chips = [(1 - x, y), (x, 1 - y), (1 - x, 1 - y)]
        return [pltpu.make_async_remote_copy(
            src_ref=ins[a].at[2 * cx + cy], dst_ref=outs[a].at[k],
            send_sem=send_sems.at[a, k], recv_sem=recv_sems.at[a, k],
            device_id=(cx, cy, c), device_id_type=MESH) for k, (cx, cy) in enumerate(chips)]

    return _exchange_comm(partials, [jax.ShapeDtypeStruct((3,) + t.shape[1:], t.dtype) for t in partials], 3, copies_of,
                          _same_core_of_other_chips, CHIP_ID)


def _pad_rows(t, rows):
    return jnp.pad(t, ((0, rows - t.shape[0]), (0, D_MODEL - t.shape[1])))


MIX_ROW, GROUP_ROW, MLP_ROW, FINAL_ROW, CONV_ROW, SINK_ROW = 0, 8, 16, 24, 32, 40
LOSS_ROW = FINAL_ROW + 1


def _pack_small(g_mix, g_group, g_mlp, g_final, conv, sinks, loss):
    final_and_loss = jnp.concatenate([g_final.reshape(1, D_MODEL), _pad_rows(loss, 1)], axis=0)
    return jnp.concatenate([
        _pad_rows(g_mix, 8), _pad_rows(g_group, 8), _pad_rows(g_mlp, 8), _pad_rows(final_and_loss, 8),
        _pad_rows(conv.reshape(DEPTH * 3, CONV_CH), 8), _pad_rows(sinks.reshape(1, DEPTH * 6), 8)], axis=0)


def kernel(x, w_in, conv_w, sinks, g_mix, g_group, w_o, g_mlp, w_ff_in, w_ff_out, g_final, loss_target, m_w_in, m_conv_w, m_sinks, m_g_mix, m_g_group, m_w_o, m_g_mlp, m_w_ff_in, m_w_ff_out, m_g_final, v_w_in, v_conv_w, v_sinks, v_g_mix, v_g_group, v_w_o, v_g_mlp, v_w_ff_in, v_w_ff_out, v_g_final):
    ax, ay, ac = _place()
    chip = 2 * ax + ay
    dev = 4 * ax + 2 * ay + ac
    pos = jnp.stack([chip, ac]).astype(jnp.int32)

    x0 = x.reshape(SEQ, D_MODEL)
    target = loss_target.reshape(SEQ, D_MODEL)

    shards = {}
    for l in range(DEPTH):
        shards[l, 0], shards[l, 1] = w_in[l].T.astype(BF16), w_o[l].astype(BF16)
        shards[l, 2], shards[l, 3] = w_ff_in[l].T.astype(BF16), w_ff_out[l].astype(BF16)
    conv_tile = jnp.pad(conv_w.reshape(DEPTH * 3, CONV_CH // N_DEV), ((0, 2), (0, LANES - CONV_CH // N_DEV)))
    wt_in0, conv_all = _comm_only(_gather_comm([shards[0, 0], conv_tile]), "gather_first")
    conv_full = conv_all.reshape(N_DEV, 8, LANES)[:, :DEPTH * 3, :CONV_CH // N_DEV]
    conv_full = conv_full.transpose(1, 0, 2).reshape(DEPTH, 3, CONV_CH)

    dx, parts, small = _step(x0, target, shards, wt_in0, conv_full, sinks, g_mix, g_group, g_mlp, g_final, pos)
    return _finish(dx, parts, small, pos, dev, w_in, conv_w, sinks, g_mix, g_group, w_o, g_mlp, w_ff_in, w_ff_out, g_final, m_w_in, m_conv_w, m_sinks, m_g_mix, m_g_group, m_w_o, m_g_mlp, m_w_ff_in, m_w_ff_out, m_g_final, v_w_in, v_conv_w, v_sinks, v_g_mix, v_g_group, v_w_o, v_g_mlp, v_w_ff_in, v_w_ff_out, v_g_final)


FWD_CARRY = {(0, "in_proj"): ((1, 0),), (0, "window"): ((0, 1),), (0, "dilated"): ((0, 2),),
             (0, "mix_ff_in"): ((0, 3),), (0, "ff_out_in_proj"): ((1, 1), (1, 3)),
             (1, "dilated"): ((1, 2),)}


def _step(x0, target, shards, wt_in0, conv_full, sinks, g_mix, g_group, g_mlp, g_final, pos):
    sink_lanes = jnp.repeat(sinks.reshape(DEPTH, 6), HEAD_DIM, axis=1)
    no_sink = jnp.full((1, A_WIDTH), NEG_BIG, F32)
    full = {(0, 0): wt_in0}

    def gather(stage, l):
        keys = FWD_CARRY.get((l, stage), ())
        return keys, (_gather_comm([shards[k] for k in keys]) if keys else None)

    def landed(keys, got):
        full.update(zip(keys, got))

    saved = []
    xc = x0
    keys, comm = gather("in_proj", 0)
    (z, h), got = _norm_mm(xc, g_mix[0:1], full[0, 0], "in_proj_0", comm)
    landed(keys, got)
    for l in range(DEPTH):
        sink_l = sink_lanes[l:l + 1]
        keys, comm = gather("window", l)
        (yc, *lse_c), got = _attn_fwd(z, sink_l, 1.0, QC_BLK, KC_BLK, VC_BLK, (1,), C_MAX_DIST, True,
                                     f"window_attn_{l}", comm)
        landed(keys, got)
        yb = _conv_fwd(z, conv_full[l], f"conv_{l}")
        keys, comm = gather("dilated", l)
        (ya, *lse_a), got = _attn_fwd(z, no_sink, 0.0, QA_BLK, KA_BLK, VA_BLK, DILATED_PATTERNS, A_MAX_DIST, False,
                                     f"dilated_attn_{l}", comm)
        landed(keys, got)
        keys, comm = gather("mix_ff_in", l)
        (y, x1, a, h2), got = _mix_ff_in(ya, yb, yc, g_group[l:l + 1], full[l, 1], xc, g_mlp[l:l + 1], full[l, 2],
                                         f"mix_ff_in_{l}", comm)
        landed(keys, got)
        saved.append((xc, z, h, ya, lse_a, yb, yc, lse_c, sink_l, y, x1, a, h2))
        if l + 1 < DEPTH:
            keys, comm = gather("ff_out_in_proj", l)
            (xc, z, h), got = _ff_out_in_proj(a, full[l, 3], x1, g_mix[l + 1:l + 2], full[l + 1, 0],
                                              f"ff_out_{l}_in_proj_{l + 1}", comm)
            landed(keys, got)

    loss_slab, dx, dxb, dg_final, du = _mm_res_loss(a, full[DEPTH - 1, 3], x1, g_final.reshape(1, D_MODEL), target,
                                                    f"ff_out_{DEPTH - 1}_loss")

    def by_owner(t):
        return t.reshape(4, 2, t.shape[0] // N_DEV, D_MODEL)

    def pair(key, g, r1):
        return _pair_sum(g, r1, pos, f"grad_pair_sum_{key[0]}_{key[1]}")

    partial, r2 = {}, {}
    dg_mix, dg_group, dg_mlp, dconv, dsinks = [None] * DEPTH, [None] * DEPTH, [None] * DEPTH, [None] * DEPTH, [None] * DEPTH
    for l in reversed(range(DEPTH)):
        xin, z, h, ya, lse_a, yb, yc, lse_c, sink_l, y, x1, a, h2 = saved[l]
        if l + 1 < DEPTH:
            late = [(l + 1, 1), (l + 1, 0)]
            (du,), got = _mlp_bwd_act(dxb, full[l, 3], a, f"ff_out_bwd_{l}", _chip_comm([partial[k] for k in late]))
            r2.update(zip(late, got))
        (g3, g2), _ = _mm_tn([(a, dxb), (du, h2)], f"grad_w_ff_{l}")
        g3, g2 = by_owner(g3), by_owner(g2)
        (dx1, dx1b, dg_mlp[l], dya, dyb, dyc, dg_group[l]), got = _ff_in_mix_bwd(
            du, full[l, 2], x1, dx, g_mlp[l:l + 1], full[l, 1], ya, yb, yc, g_group[l:l + 1],
            f"ff_in_mix_bwd_{l}", _sibling_comm([g3, g2]))
        partial[l, 3], partial[l, 2] = pair((l, 3), g3, got[0]), pair((l, 2), g2, got[1])
        early = [(l, 3), (l, 2)]
        (dz, _), got = _attn_bwd(z, dya, ya, lse_a, no_sink, None, QA_BLK, KA_BLK, VA_BLK, DILATED_PATTERNS,
                                 A_MAX_DIST, False, f"dilated_attn_bwd_{l}", _chip_comm([partial[k] for k in early]))
        r2.update(zip(early, got))
        dz, dcw = _conv_bwd(z, conv_full[l], dyb, dz, f"conv_bwd_{l}")
        (dz, dsink), _ = _attn_bwd(z, dyc, yc, lse_c, sink_l, dz, QC_BLK, KC_BLK, VC_BLK, (1,), C_MAX_DIST,
                                   True, f"window_attn_bwd_{l}")
        (g1, g0), _ = _mm_tn([(y, dx1b), (dz, h)], f"grad_w_o_in_{l}")
        g1, g0 = by_owner(g1), by_owner(g0)
        if l > 0:
            (dx, dxb, dg_mix[l]), got = _mm_nn_normbwd(dz, full[l, 0], xin, dx1, g_mix[l:l + 1], f"in_proj_bwd_{l}",
                                                      _sibling_comm([g1, g0]))
            partial[l, 1], partial[l, 0] = pair((l, 1), g1, got[0]), pair((l, 0), g0, got[1])
        else:
            got = _comm_only(_sibling_comm([g1, g0]), "grad_sibling_exchange_last")
            partial[l, 1], partial[l, 0] = pair((l, 1), g1, got[0]), pair((l, 0), g0, got[1])
            (dx, dxb, dg_mix[l]), got = _mm_nn_normbwd(dz, full[l, 0], xin, dx1, g_mix[l:l + 1], f"in_proj_bwd_{l}",
                                                      _chip_comm([partial[l, 1], partial[l, 0]]))
            r2[l, 1], r2[l, 0] = got
        dconv[l] = dcw[:3]
        dsinks[l] = dsink[0, ::HEAD_DIM]
    parts = {key: (partial[key], r2[key]) for key in partial}
    small = _pack_small(jnp.concatenate(dg_mix), jnp.concatenate(dg_group), jnp.concatenate(dg_mlp),
                        dg_final, jnp.stack(dconv), jnp.stack(dsinks), loss_slab[0:1])
    return dx, parts, small


def _finish(dx, parts, small, pos, dev, w_in, conv_w, sinks, g_mix, g_group, w_o, g_mlp, w_ff_in, w_ff_out, g_final, m_w_in, m_conv_w, m_sinks, m_g_mix, m_g_group, m_w_o, m_g_mlp, m_w_ff_in, m_w_ff_out, m_g_final, v_w_in, v_conv_w, v_sinks, v_g_mix, v_g_group, v_w_o, v_g_mlp, v_w_ff_in, v_w_ff_out, v_g_final):
    grad_x = dx.reshape(1, SEQ, D_MODEL)

    (small_all,) = _comm_only(_gather_comm([small]), "gather_small_grads")
    row = lambda t: t.reshape(1, D_MODEL)
    sink_row = lambda t: _pad_rows(t.reshape(1, DEPTH * 6), 1)
    params = [(MIX_ROW, g_mix, m_g_mix, v_g_mix), (GROUP_ROW, g_group, m_g_group, v_g_group),
              (MLP_ROW, g_mlp, m_g_mlp, v_g_mlp), (FINAL_ROW, row(g_final), row(m_g_final), row(v_g_final)),
              (SINK_ROW, sink_row(sinks), sink_row(m_sinks), sink_row(v_sinks))]
    updated, conv_rows, loss_row = _small_sum_adamw(small_all.reshape(N_DEV, SMALL_ROWS, D_MODEL), params, "small_adamw")
    loss = loss_row[0, 0]
    (grad_g_mix, delta_g_mix, new_m_g_mix, new_v_g_mix), (grad_g_group, delta_g_group, new_m_g_group, new_v_g_group), \
        (grad_g_mlp, delta_g_mlp, new_m_g_mlp, new_v_g_mlp), final4, sinks4 = updated
    grad_g_final, delta_g_final, new_m_g_final, new_v_g_final = [t.reshape(D_MODEL) for t in final4]
    grad_sinks, delta_sinks, new_m_sinks, new_v_sinks = [t[0, :DEPTH * 6].reshape(DEPTH, 2, 3) for t in sinks4]
    conv_grad_full = conv_rows[:DEPTH * 3, :CONV_CH].reshape(DEPTH, 3, CONV_CH)
    cs = CONV_CH // N_DEV
    grad_conv_w = lax.dynamic_slice_in_dim(conv_grad_full, dev * cs, cs, axis=2)

    def tile_of(t):
        return jnp.pad(t.reshape(1, DEPTH * 3 * cs), ((0, 7), (0, 256 - DEPTH * 3 * cs)))

    cd, cm, cv = _adamw(tile_of(conv_w), tile_of(grad_conv_w), tile_of(m_conv_w), tile_of(v_conv_w), "conv_adamw")
    untile = lambda t: t[0, :DEPTH * 3 * cs].reshape(DEPTH, 3, cs)
    delta_conv_w, new_m_conv_w, new_v_conv_w = untile(cd), untile(cm), untile(cv)

    def big(kind, w, m, v, transpose, name):
        return _sum_adamw([parts[l, kind] for l in range(DEPTH)], w, m, v, pos, transpose, name)

    swap = lambda t: jnp.swapaxes(t, 1, 2)
    grad_w_in, delta_w_in, new_m_w_in, new_v_w_in = [
        swap(t) for t in big(0, swap(w_in), swap(m_w_in), swap(v_w_in), False, "adamw_w_in")]
    grad_w_o, delta_w_o, new_m_w_o, new_v_w_o = big(1, w_o, m_w_o, v_w_o, False, "adamw_w_o")
    grad_w_ff_in, delta_w_ff_in, new_m_w_ff_in, new_v_w_ff_in = big(2, w_ff_in, m_w_ff_in, v_w_ff_in, True, "adamw_w_ff_in")
    grad_w_ff_out, delta_w_ff_out, new_m_w_ff_out, new_v_w_ff_out = big(3, w_ff_out, m_w_ff_out, v_w_ff_out, False,
                                                                         "adamw_w_ff_out")

    return (loss, grad_x, grad_w_in, grad_conv_w, grad_sinks, grad_g_mix, grad_g_group, grad_w_o, grad_g_mlp,
            grad_w_ff_in, grad_w_ff_out, grad_g_final,
            delta_w_in, delta_conv_w, delta_sinks, delta_g_mix, delta_g_group, delta_w_o, delta_g_mlp,
            delta_w_ff_in, delta_w_ff_out, delta_g_final,
            new_m_w_in, new_m_conv_w, new_m_sinks, new_m_g_mix, new_m_g_group, new_m_w_o, new_m_g_mlp,
            new_m_w_ff_in, new_m_w_ff_out, new_m_g_final,
            new_v_w_in, new_v_conv_w, new_v_sinks, new_v_g_mix, new_v_g_group, new_v_w_o, new_v_g_mlp,
            new_v_w_ff_in, new_v_w_ff_out, new_v_g_final)
```

```python
from typing import Callable, NamedTuple

import jax
import jax.numpy as jnp
from jax import lax
from jax.experimental import pallas as pl
from jax.experimental.pallas import tpu as pltpu

F32 = jnp.float32
BF16 = jnp.bfloat16
MESH = pl.DeviceIdType.MESH

N_DEV = 8
SEQ = 4096
D_MODEL = 1024
DEPTH = 2
HEAD_DIM = 64
LANES = 128
A_WIDTH = 384
CONV_CH = 256
C_WIDTH = 384
IN_WIDTH = 2560
BLOCK = 128
DILATED_PATTERNS = (1, 4, 16)
A_MAX_DIST = 128
C_MAX_DIST = 127
EPS = 1e-6
SCALE = HEAD_DIM ** -0.5
NEG_BIG = -1e30
F32_TINY = 1.1754944e-38

QA_BLK, KA_BLK, VA_BLK = 0, 3, 6
GB_BLK, GC_BLK, XB_BLK = 9, 11, 13
QC_BLK, KC_BLK, VC_BLK = 15, 18, 19

ADAM_LR = 0.001
ADAM_B1 = 0.9
ADAM_B2 = 0.999
ADAM_EPS = 1e-08
ADAM_WD = 0.01
ADAM_STEP = 10

VMEM_LIMIT = 56 * 1024 * 1024
TILE_BUDGET = 46 * 1024 * 1024
ROW_TILE = 512
COL_CHUNK = 512
SMALL_ROWS = 48


def _dot_nn(a, b):
    return lax.dot_general(a, b, (((1,), (0,)), ((), ())), preferred_element_type=F32)


def _dot_nt(a, b):
    return lax.dot_general(a, b, (((1,), (1,)), ((), ())), preferred_element_type=F32)


def _dot_tn(a, b):
    return lax.dot_general(a, b, (((0,), (0,)), ((), ())), preferred_element_type=F32)


def _params(*sem, collective_id=None):
    return pltpu.CompilerParams(dimension_semantics=sem, vmem_limit_bytes=VMEM_LIMIT, collective_id=collective_id)


def _resident(shape):
    return pl.BlockSpec(shape, lambda i: (0,) * len(shape), pipeline_mode=pl.Buffered(1))


def _row_tile(row_bytes, resident_bytes):
    for tm in (ROW_TILE, ROW_TILE // 2):
        if 2 * tm * row_bytes + resident_bytes <= TILE_BUDGET:
            return tm
    return ROW_TILE // 4


def _rms_scale(t):
    return lax.rsqrt(jnp.mean(t * t, axis=-1, keepdims=True) + EPS)


def _rms_bwd(n, r, dn):
    return r * (dn - n * jnp.mean(dn * n, axis=-1, keepdims=True))


class _Comm(NamedTuple):
    arrays: tuple
    out_shape: tuple
    sems: tuple
    start: Callable
    finish: Callable
    peers: Callable
    collective_id: int


def _handshake(comm):
    barrier = pltpu.get_barrier_semaphore()
    peers = comm.peers()
    for peer in peers:
        pl.semaphore_signal(barrier, inc=1, device_id=peer, device_id_type=MESH)
    pl.semaphore_wait(barrier, len(peers))


def _call(body, grid, in_specs, out_specs, out_shape, operands, name, scratch_shapes=(), comm=None, aliases=None):
    n_in, n_out, n_scr = len(in_specs), len(out_shape), len(scratch_shapes)
    aliases = dict(aliases or {})
    if comm is None:
        res = pl.pallas_call(body, grid=grid, in_specs=list(in_specs), out_specs=list(out_specs),
                             out_shape=list(out_shape), scratch_shapes=list(scratch_shapes),
                             input_output_aliases=aliases,
                             compiler_params=_params("arbitrary"), name=name)(*operands)
        return list(res), []
    c_in, c_out = len(comm.arrays), len(comm.out_shape)
    hbm = pl.BlockSpec(memory_space=pl.ANY)
    last = grid[0] - 1

    def carried(*refs):
        ins, cins = refs[:n_in], refs[n_in:n_in + c_in]
        o0 = n_in + c_in
        outs, couts = refs[o0:o0 + n_out], refs[o0 + n_out:o0 + n_out + c_out]
        s0 = o0 + n_out + c_out
        scr, sems = refs[s0:s0 + n_scr], refs[s0 + n_scr:]
        @pl.when(pl.program_id(0) == 0)
        def _():
            _handshake(comm)
            comm.start(cins, couts, sems)

        body(*ins, *outs, *scr)
        pl.when(pl.program_id(0) == last)(lambda: comm.finish(cins, couts, sems))

    res = pl.pallas_call(carried, grid=grid, in_specs=list(in_specs) + [hbm] * c_in,
                         out_specs=list(out_specs) + [hbm] * c_out, out_shape=list(out_shape) + list(comm.out_shape),
                         scratch_shapes=list(scratch_shapes) + list(comm.sems), input_output_aliases=aliases,
                         compiler_params=_params("arbitrary", collective_id=comm.collective_id),
                         name=name)(*operands, *comm.arrays)
    return list(res[:n_out]), list(res[n_out:])


def _comm_only(comm, name):
    hbm = pl.BlockSpec(memory_space=pl.ANY)
    c_in, c_out = len(comm.arrays), len(comm.out_shape)

    def body(*refs):
        ins, outs, sems = refs[:c_in], refs[c_in:c_in + c_out], refs[c_in + c_out:]
        _handshake(comm)
        comm.start(ins, outs, sems)
        comm.finish(ins, outs, sems)

    return pl.pallas_call(body, in_specs=[hbm] * c_in, out_specs=[hbm] * c_out, out_shape=list(comm.out_shape),
                          scratch_shapes=list(comm.sems),
                          compiler_params=pltpu.CompilerParams(collective_id=comm.collective_id),
                          name=name)(*comm.arrays)


def _norm_mm(x, g, wt, name, comm=None):
    s, d = x.shape
    n = wt.shape[0]
    tm = _row_tile(4 * d + 4 * n + 2 * d, 2 * n * d)

    def body(x_ref, g_ref, w_ref, o_ref, h_ref):
        xx = x_ref[...]
        h = ((xx * _rms_scale(xx)) * g_ref[...]).astype(BF16)
        h_ref[...] = h
        for n0 in range(0, n, COL_CHUNK):
            o_ref[:, n0:n0 + COL_CHUNK] = _dot_nt(h, w_ref[n0:n0 + COL_CHUNK, :])

    return _call(
        body,
        grid=(s // tm,),
        in_specs=[pl.BlockSpec((tm, d), lambda i: (i, 0)),
                  pl.BlockSpec((1, d), lambda i: (0, 0)),
                  _resident((n, d))],
        out_specs=[pl.BlockSpec((tm, n), lambda i: (i, 0)),
                   pl.BlockSpec((tm, d), lambda i: (i, 0))],
        out_shape=[jax.ShapeDtypeStruct((s, n), F32), jax.ShapeDtypeStruct((s, d), BF16)],
        operands=(x, g, wt), name=name, comm=comm)


def _ff_out_in_proj(a, w2, x1, g, wt, name, comm=None):
    s, f = a.shape
    d = w2.shape[1]
    n = wt.shape[0]
    tm = _row_tile(2 * f + 4 * d + 4 * d + 4 * n + 2 * d, 2 * f * d + 2 * n * d)

    def body(a_ref, w2_ref, x_ref, g_ref, w_ref, x2_ref, z_ref, h_ref):
        x2 = x_ref[...] + _dot_nn(a_ref[...], w2_ref[...])
        x2_ref[...] = x2
        h = ((x2 * _rms_scale(x2)) * g_ref[...]).astype(BF16)
        h_ref[...] = h
        for n0 in range(0, n, COL_CHUNK):
            z_ref[:, n0:n0 + COL_CHUNK] = _dot_nt(h, w_ref[n0:n0 + COL_CHUNK, :])

    rows = lambda w: pl.BlockSpec((tm, w), lambda i: (i, 0))
    return _call(
        body,
        grid=(s // tm,),
        in_specs=[rows(f), _resident((f, d)), rows(d), pl.BlockSpec((1, d), lambda i: (0, 0)), _resident((n, d))],
        out_specs=[rows(d), rows(n), rows(d)],
        out_shape=[jax.ShapeDtypeStruct((s, d), F32), jax.ShapeDtypeStruct((s, n), F32),
                   jax.ShapeDtypeStruct((s, d), BF16)],
        operands=(a, w2, x1, g, wt), name=name, comm=comm)


def _mix_ff_in(ya, yb, yc, gg, wo, x0, g_mlp, wt1, name, comm=None):
    s = ya.shape[0]
    d = wo.shape[1]
    f = wt1.shape[0]
    tm = _row_tile(4 * d + 4 * d + 2 * d + 4 * d + 2 * d + 2 * f, 2 * d * d + 2 * f * d)

    def body(ya_ref, yb_ref, yc_ref, gg_ref, wo_ref, x_ref, g_ref, w1_ref, y_ref, x1_ref, a_ref, h_ref):
        parts = []
        for ref in (ya_ref, yb_ref, yc_ref):
            t = ref[...]
            parts.append(t * _rms_scale(t))
        y = (jnp.concatenate(parts, axis=1) * gg_ref[...]).astype(BF16)
        y_ref[...] = y
        x1 = x_ref[...] + _dot_nn(y, wo_ref[...])
        x1_ref[...] = x1
        h = ((x1 * _rms_scale(x1)) * g_ref[...]).astype(BF16)
        h_ref[...] = h
        for n0 in range(0, f, COL_CHUNK):
            u = _dot_nt(h, w1_ref[n0:n0 + COL_CHUNK, :])
            a_ref[:, n0:n0 + COL_CHUNK] = jnp.square(jnp.maximum(u, 0.0)).astype(BF16)

    rows = lambda w: pl.BlockSpec((tm, w), lambda i: (i, 0))
    vec = pl.BlockSpec((1, d), lambda i: (0, 0))
    return _call(
        body,
        grid=(s // tm,),
        in_specs=[rows(A_WIDTH), rows(CONV_CH), rows(C_WIDTH), vec, _resident((d, d)), rows(d), vec, _resident((f, d))],
        out_specs=[rows(d), rows(d), rows(f), rows(d)],
        out_shape=[jax.ShapeDtypeStruct((s, d), BF16), jax.ShapeDtypeStruct((s, d), F32),
                   jax.ShapeDtypeStruct((s, f), BF16), jax.ShapeDtypeStruct((s, d), BF16)],
        operands=(ya, yb, yc, gg, wo, x0, g_mlp, wt1), name=name, comm=comm)


def _relu_from_square(av):
    return av * lax.rsqrt(jnp.maximum(av, F32_TINY))


def _mm_res_loss(a, w2, x1, g, target, name):
    s, f = a.shape
    d = w2.shape[1]
    tm = _row_tile(2 * f + 4 * d + 4 * d + 4 * d + 2 * d + 2 * f, 2 * f * d)

    def body(a_ref, w_ref, x_ref, g_ref, t_ref, loss_ref, dx_ref, dxb_ref, dg_ref, du_ref):
        @pl.when(pl.program_id(0) == 0)
        def _():
            loss_ref[...] = jnp.zeros_like(loss_ref)
            dg_ref[...] = jnp.zeros_like(dg_ref)

        xx = x_ref[...] + _dot_nn(a_ref[...], w_ref[...])
        r = _rms_scale(xx)
        n = xx * r
        gv = g_ref[...]
        err = n * gv - t_ref[...]
        per_tok = jnp.sum(err * err, axis=1, keepdims=True) * (1.0 / d)
        loss_ref[...] += 0.5 * jnp.sum(per_tok, axis=0, keepdims=True)
        dout = err * (1.0 / d)
        dg_ref[...] += jnp.sum(dout * n, axis=0, keepdims=True)
        dx = _rms_bwd(n, r, dout * gv)
        dx_ref[...] = dx
        dxb = dx.astype(BF16)
        dxb_ref[...] = dxb
        for n0 in range(0, f, COL_CHUNK):
            da = _dot_nt(dxb, w_ref[n0:n0 + COL_CHUNK, :])
            rl = _relu_from_square(a_ref[:, n0:n0 + COL_CHUNK].astype(F32))
            du_ref[:, n0:n0 + COL_CHUNK] = (da * (2.0 * rl)).astype(BF16)

    rows = lambda w: pl.BlockSpec((tm, w), lambda i: (i, 0))
    vec = pl.BlockSpec((1, d), lambda i: (0, 0))
    return pl.pallas_call(
        body,
        grid=(s // tm,),
        in_specs=[rows(f), _resident((f, d)), rows(d), vec, rows(d)],
        out_specs=[pl.BlockSpec((8, LANES), lambda i: (0, 0)), rows(d), rows(d), vec, rows(f)],
        out_shape=[jax.ShapeDtypeStruct((8, LANES), F32), jax.ShapeDtypeStruct((s, d), F32),
                   jax.ShapeDtypeStruct((s, d), BF16), jax.ShapeDtypeStruct((1, d), F32),
                   jax.ShapeDtypeStruct((s, f), BF16)],
        compiler_params=_params("arbitrary"),
        name=name,
    )(a, w2, x1, g, target)


def _mlp_bwd_act(dxb, w2, a, name, comm=None):
    s, d = dxb.shape
    f = w2.shape[0]
    tm = _row_tile(2 * d + 2 * f + 2 * f, 2 * f * d)

    def body(dx_ref, w_ref, a_ref, du_ref):
        dx = dx_ref[...]
        for n0 in range(0, f, COL_CHUNK):
            da = _dot_nt(dx, w_ref[n0:n0 + COL_CHUNK, :])
            rl = _relu_from_square(a_ref[:, n0:n0 + COL_CHUNK].astype(F32))
            du_ref[:, n0:n0 + COL_CHUNK] = (da * (2.0 * rl)).astype(BF16)

    return _call(
        body,
        grid=(s // tm,),
        in_specs=[pl.BlockSpec((tm, d), lambda i: (i, 0)),
                  _resident((f, d)),
                  pl.BlockSpec((tm, f), lambda i: (i, 0))],
        out_specs=[pl.BlockSpec((tm, f), lambda i: (i, 0))],
        out_shape=[jax.ShapeDtypeStruct((s, f), BF16)],
        operands=(dxb, w2, a), name=name, comm=comm)


def _mm_tn(pairs, name, comm=None):
    s, d = pairs[0][1].shape
    tn = 512
    tiles = [a.shape[1] // tn for a, _ in pairs]
    starts = [sum(tiles[:k]) for k in range(len(pairs))]

    def body(*refs):
        ins, outs, acc = refs[:2 * len(pairs)], refs[2 * len(pairs):3 * len(pairs)], refs[3 * len(pairs)]
        j = pl.program_id(0)
        for k in range(len(pairs)):
            def run(a_ref=ins[2 * k], b_ref=ins[2 * k + 1], o_ref=outs[k]):
                for k0 in range(0, s, ROW_TILE):
                    part = _dot_tn(a_ref[k0:k0 + ROW_TILE, :], b_ref[k0:k0 + ROW_TILE, :])
                    if k0 == 0:
                        acc[...] = part
                    else:
                        acc[...] += part
                o_ref[...] = acc[...].astype(BF16)

            pl.when((j >= starts[k]) & (j < starts[k] + tiles[k]))(run)

    def tile_of(k):
        return lambda j: jnp.clip(j - starts[k], 0, tiles[k] - 1)

    in_specs, out_specs = [], []
    for k in range(len(pairs)):
        in_specs += [pl.BlockSpec((s, tn), lambda j, t=tile_of(k): (0, t(j))), _resident((s, d))]
        out_specs.append(pl.BlockSpec((tn, d), lambda j, t=tile_of(k): (t(j), 0)))
    return _call(
        body,
        grid=(sum(tiles),),
        in_specs=in_specs,
        out_specs=out_specs,
        out_shape=[jax.ShapeDtypeStruct((a.shape[1], d), BF16) for a, _ in pairs],
        operands=tuple(t for pair in pairs for t in pair), name=name,
        scratch_shapes=[pltpu.VMEM((tn, d), F32)], comm=comm)


def _mm_nn_normbwd(dact, wt, x, dres, g, name, comm=None):
    s, kdim = dact.shape
    d = wt.shape[1]
    tm = _row_tile(2 * kdim + 4 * d + 4 * d + 4 * d + 2 * d, 2 * kdim * d)

    def body(a_ref, w_ref, x_ref, r_ref, g_ref, o_ref, ob_ref, dg_ref):
        @pl.when(pl.program_id(0) == 0)
        def _():
            dg_ref[...] = jnp.zeros_like(dg_ref)

        dh = _dot_nn(a_ref[...], w_ref[...])
        xx = x_ref[...]
        r = _rms_scale(xx)
        n = xx * r
        dg_ref[...] += jnp.sum(dh * n, axis=0, keepdims=True)
        dx = r_ref[...] + _rms_bwd(n, r, dh * g_ref[...])
        o_ref[...] = dx
        ob_ref[...] = dx.astype(BF16)

    return _call(
        body,
        grid=(s // tm,),
        in_specs=[pl.BlockSpec((tm, kdim), lambda i: (i, 0)),
                  _resident((kdim, d)),
                  pl.BlockSpec((tm, d), lambda i: (i, 0)),
                  pl.BlockSpec((tm, d), lambda i: (i, 0)),
                  pl.BlockSpec((1, d), lambda i: (0, 0))],
        out_specs=[pl.BlockSpec((tm, d), lambda i: (i, 0)),
                   pl.BlockSpec((tm, d), lambda i: (i, 0)),
                   pl.BlockSpec((1, d), lambda i: (0, 0))],
        out_shape=[jax.ShapeDtypeStruct((s, d), F32), jax.ShapeDtypeStruct((s, d), BF16),
                   jax.ShapeDtypeStruct((1, d), F32)],
        operands=(dact, wt, x, dres, g), name=name, comm=comm)


def _ff_in_mix_bwd(du, wt1, x1, dres, g_mlp, wo, ya, yb, yc, gg, name, comm=None):
    s, f = du.shape
    d = wt1.shape[1]
    widths = (A_WIDTH, CONV_CH, C_WIDTH)
    tm = _row_tile(2 * f + 4 * d + 4 * d + 4 * d + 2 * d + 4 * d + 4 * d, 2 * f * d + 2 * d * d)

    def body(du_ref, w1_ref, x_ref, r_ref, g_ref, wo_ref, ya_ref, yb_ref, yc_ref, gg_ref,
             dx_ref, dxb_ref, dg_ref, da_ref, db_ref, dc_ref, dgg_ref):
        @pl.when(pl.program_id(0) == 0)
        def _():
            dg_ref[...] = jnp.zeros_like(dg_ref)
            dgg_ref[...] = jnp.zeros_like(dgg_ref)

        dh = _dot_nn(du_ref[...], w1_ref[...])
        xx = x_ref[...]
        r = _rms_scale(xx)
        n = xx * r
        dg_ref[...] += jnp.sum(dh * n, axis=0, keepdims=True)
        dx = r_ref[...] + _rms_bwd(n, r, dh * g_ref[...])
        dx_ref[...] = dx
        dxb = dx.astype(BF16)
        dxb_ref[...] = dxb

        dy = _dot_nt(dxb, wo_ref[...])
        gv = gg_ref[...]
        off = 0
        dgs = []
        for ref, out, w in zip((ya_ref, yb_ref, yc_ref), (da_ref, db_ref, dc_ref), widths):
            t = ref[...]
            r = _rms_scale(t)
            n = t * r
            dyg = dy[:, off:off + w]
            dgs.append(jnp.sum(dyg * n, axis=0, keepdims=True))
            out[...] = _rms_bwd(n, r, dyg * gv[:, off:off + w])
            off += w
        dgg_ref[...] += jnp.concatenate(dgs, axis=1)

    rows = lambda w: pl.BlockSpec((tm, w), lambda i: (i, 0))
    vec = pl.BlockSpec((1, d), lambda i: (0, 0))
    return _call(
        body,
        grid=(s // tm,),
        in_specs=[rows(f), _resident((f, d)), rows(d), rows(d), vec, _resident((d, d)),
                  rows(A_WIDTH), rows(CONV_CH), rows(C_WIDTH), vec],
        out_specs=[rows(d), rows(d), vec, rows(A_WIDTH), rows(CONV_CH), rows(C_WIDTH), vec],
        out_shape=[jax.ShapeDtypeStruct((s, d), F32), jax.ShapeDtypeStruct((s, d), BF16), jax.ShapeDtypeStruct((1, d), F32),
                   jax.ShapeDtypeStruct((s, A_WIDTH), F32), jax.ShapeDtypeStruct((s, CONV_CH), F32),
                   jax.ShapeDtypeStruct((s, C_WIDTH), F32), jax.ShapeDtypeStruct((1, d), F32)],
        operands=(du, wt1, x1, dres, g_mlp, wo, ya, yb, yc, gg), name=name, comm=comm)


CONV_CHUNK = 256
CONV_HALO = 8


def _conv_fwd(z, cw, name):
    s = z.shape[0]
    nch = s // CONV_CHUNK

    def body(gb_ref, gc_ref, xb_ref, w_ref, o_ref, us):
        us[pl.ds(0, CONV_HALO), :] = jnp.zeros((CONV_HALO, LANES), F32)
        us[pl.ds(CONV_HALO, s), :] = gc_ref[...] * xb_ref[...]
        w0, w1, w2 = w_ref[0:1, :], w_ref[1:2, :], w_ref[2:3, :]

        def chunk(c, carry):
            st = pl.multiple_of(c * CONV_CHUNK, CONV_CHUNK)
            ext = us[pl.ds(st, CONV_CHUNK + CONV_HALO), :]
            y = (w0 * ext[CONV_HALO - 2:CONV_HALO - 2 + CONV_CHUNK]
                 + w1 * ext[CONV_HALO - 1:CONV_HALO - 1 + CONV_CHUNK]
                 + w2 * ext[CONV_HALO:])
            o_ref[pl.ds(st, CONV_CHUNK), :] = gb_ref[pl.ds(st, CONV_CHUNK), :] * y
            return carry

        lax.fori_loop(0, nch, chunk, 0)

    col = lambda blk: pl.BlockSpec((s, LANES), lambda j, blk=blk: (0, blk + j))
    return pl.pallas_call(
        body,
        grid=(CONV_CH // LANES,),
        in_specs=[col(GB_BLK), col(GC_BLK), col(XB_BLK), pl.BlockSpec((3, LANES), lambda j: (0, j))],
        out_specs=pl.BlockSpec((s, LANES), lambda j: (0, j)),
        out_shape=jax.ShapeDtypeStruct((s, CONV_CH), F32),
        scratch_shapes=[pltpu.VMEM((s + CONV_HALO, LANES), F32)],
        compiler_params=_params("parallel"),
        name=name,
    )(z, z, z, cw)


def _conv_bwd(z, cw, dyb, dz, name):
    s = z.shape[0]
    nch = s // CONV_CHUNK
    ncol = CONV_CH // LANES

    def body(gb_ref, gc_ref, xb_ref, w_ref, dy_ref, dz_in, dz_ref, dw_ref, us, ds_, dgb_ref, dgc_ref, dxb_ref, sems):
        j = pl.program_id(0)

        def to_dz(staged, blk, k):
            cols = pl.ds(pl.multiple_of((blk + j) * LANES, LANES), LANES)
            return pltpu.make_async_copy(staged, dz_ref.at[:, cols], sems.at[k])

        copies = [to_dz(dgb_ref, GB_BLK, 0), to_dz(dgc_ref, GC_BLK, 1), to_dz(dxb_ref, XB_BLK, 2)]

        @pl.when(j > 0)
        def _():
            for cp in copies:
                cp.wait()

        us[pl.ds(0, CONV_HALO), :] = jnp.zeros((CONV_HALO, LANES), F32)
        us[pl.ds(CONV_HALO, s), :] = gc_ref[...] * xb_ref[...]
        ds_[pl.ds(s, CONV_HALO), :] = jnp.zeros((CONV_HALO, LANES), F32)
        ds_[pl.ds(0, s), :] = dy_ref[...] * gb_ref[...]
        w0, w1, w2 = w_ref[0:1, :], w_ref[1:2, :], w_ref[2:3, :]
        zero = jnp.zeros((1, LANES), F32)

        def chunk(c, carry):
            a0, a1, a2 = carry
            st = pl.multiple_of(c * CONV_CHUNK, CONV_CHUNK)
            rows = pl.ds(st, CONV_CHUNK)
            ext = us[pl.ds(st, CONV_CHUNK + CONV_HALO), :]
            um2 = ext[CONV_HALO - 2:CONV_HALO - 2 + CONV_CHUNK]
            um1 = ext[CONV_HALO - 1:CONV_HALO - 1 + CONV_CHUNK]
            u0 = ext[CONV_HALO:]
            dext = ds_[pl.ds(st, CONV_CHUNK + CONV_HALO), :]
            dc0 = dext[:CONV_CHUNK]
            du = w2 * dc0 + w1 * dext[1:1 + CONV_CHUNK] + w0 * dext[2:2 + CONV_CHUNK]
            yconv = w0 * um2 + w1 * um1 + w2 * u0
            dgb_ref[rows, :] = (dy_ref[rows, :] * yconv).astype(BF16)
            dgc_ref[rows, :] = (du * xb_ref[rows, :]).astype(BF16)
            dxb_ref[rows, :] = (du * gc_ref[rows, :]).astype(BF16)
            a0 = a0 + jnp.sum(dc0 * um2, axis=0, keepdims=True)
            a1 = a1 + jnp.sum(dc0 * um1, axis=0, keepdims=True)
            a2 = a2 + jnp.sum(dc0 * u0, axis=0, keepdims=True)
            return a0, a1, a2

        a0, a1, a2 = lax.fori_loop(0, nch, chunk, (zero, zero, zero))
        dw_ref[...] = jnp.concatenate([a0, a1, a2, jnp.zeros((5, LANES), F32)], axis=0)
        for cp in copies:
            cp.start()

        @pl.when(j == ncol - 1)
        def _():
            for cp in copies:
                cp.wait()

    col = lambda blk: pl.BlockSpec((s, LANES), lambda j, blk=blk: (0, blk + j))
    hbm = pl.BlockSpec(memory_space=pl.ANY)
    return pl.pallas_call(
        body,
        grid=(ncol,),
        in_specs=[col(GB_BLK), col(GC_BLK), col(XB_BLK), pl.BlockSpec((3, LANES), lambda j: (0, j)),
                  pl.BlockSpec((s, LANES), lambda j: (0, j)), hbm],
        out_specs=[hbm, pl.BlockSpec((8, LANES), lambda j: (0, j))],
        out_shape=[jax.ShapeDtypeStruct(dz.shape, dz.dtype), jax.ShapeDtypeStruct((8, CONV_CH), F32)],
        scratch_shapes=[pltpu.VMEM((s + CONV_HALO, LANES), F32), pltpu.VMEM((s + CONV_HALO, LANES), F32)]
        + [pltpu.VMEM((s, LANES), BF16)] * 3 + [pltpu.SemaphoreType.DMA((3,))],
        input_output_aliases={5: 0},
        compiler_params=_params("arbitrary"),
        name=name,
    )(z, z, z, cw, dyb, dz)


ATTN_ROWS = 512
ATTN_UNROLL = 8


def _band_rows(b, d, r):
    base = pl.multiple_of(b * (BLOCK * d), BLOCK)
    prev = jnp.maximum(base - BLOCK * d, 0)
    if d == 1:
        return pl.ds(base, BLOCK), pl.ds(pl.multiple_of(prev, BLOCK), BLOCK)
    return pl.ds(base + r, BLOCK, stride=d), pl.ds(prev + r, BLOCK, stride=d)


def _write_band_bias(bias_ref, max_dist):
    qi = lax.broadcasted_iota(jnp.int32, (BLOCK, 2 * BLOCK), 0)
    kj = lax.broadcasted_iota(jnp.int32, (BLOCK, 2 * BLOCK), 1)
    dist = BLOCK + qi - kj
    band = (dist >= 0) & (dist <= max_dist)
    bias_ref[0:BLOCK, :] = jnp.where(band, 0.0, -jnp.inf)
    bias_ref[BLOCK:2 * BLOCK, :] = jnp.where(band & (kj >= BLOCK), 0.0, -jnp.inf)


def _band_bias(bias_ref, b):
    bias = bias_ref[pl.ds(pl.multiple_of(jnp.where(b > 0, 0, BLOCK), BLOCK), BLOCK), :]
    return jnp.concatenate([bias, bias], axis=0)


def _kv_halves(pair):
    zero = jnp.zeros((1, LANES), jnp.int32)
    return zero + (pair >> 1), zero + ((pair + 1) >> 1)


def _stack_heads(t, head0, halves=None):
    top, bottom = jnp.where(head0, t, 0.0), jnp.where(head0, 0.0, t)
    if halves is not None:
        top = jnp.where(halves[0] == 1, pltpu.roll(top, HEAD_DIM, 1), top)
        bottom = jnp.where(halves[1] == 0, pltpu.roll(bottom, HEAD_DIM, 1), bottom)
    return jnp.concatenate([top, bottom], axis=0).astype(BF16)


def _unstack_heads(t, head0, halves=None):
    top, bottom = t[:BLOCK], t[BLOCK:]
    if halves is not None:
        top = jnp.where(halves[0] == 1, pltpu.roll(top, HEAD_DIM, 1), top)
        bottom = jnp.where(halves[1] == 0, pltpu.roll(bottom, HEAD_DIM, 1), bottom)
    return jnp.where(head0, top, bottom)


def _block_loops(s, patterns, unroll, one_block):
    for n, d in enumerate(patterns):
        nb = (s // BLOCK) // d
        ur = min(unroll, d)
        ub = unroll // ur
        for r0 in range(0, d, ur):
            def trip(i, carry, n=n, d=d, r0=r0, ur=ur, ub=ub):
                for u in range(ub):
                    for r in range(r0, r0 + ur):
                        one_block(i * ub + u, d, r, n == 0)
                return carry
            lax.fori_loop(0, nb // ub, trip, 0)


def _attn_fwd(z, m_init, l_init, q_blk, k_blk, v_blk, patterns, max_dist, gqa, name, comm=None):
    s = z.shape[0]
    npair = 3

    def body(q_ref, k_ref, v_ref, mi_ref, o_ref, lse0_ref, lse1_ref, bias_scr, m_scr, l_scr, *kv_scr):
        head0 = lax.broadcasted_iota(jnp.int32, (1, LANES), 1) < HEAD_DIM
        _write_band_bias(bias_scr, max_dist)
        ones = jnp.ones((2 * BLOCK, LANES), BF16)
        k_src, v_src = kv_scr if gqa else (k_ref, v_ref)
        if gqa:
            half = (lax.broadcasted_iota(jnp.int32, (1, LANES), 1) >= HEAD_DIM).astype(jnp.int32)
            swap = ((pl.program_id(0) + half) >> 1) != half

            def expand(c, carry):
                rows = pl.ds(pl.multiple_of(c * ATTN_ROWS, ATTN_ROWS), ATTN_ROWS)
                k_src[rows, :] = jnp.where(swap, pltpu.roll(k_ref[rows, :], HEAD_DIM, 1), k_ref[rows, :])
                v_src[rows, :] = jnp.where(swap, pltpu.roll(v_ref[rows, :], HEAD_DIM, 1), v_ref[rows, :])
                return carry

            lax.fori_loop(0, s // ATTN_ROWS, expand, 0)

        def one_block(b, d, r, first):
            rq, rp = _band_rows(b, d, r)
            q2 = _stack_heads(q_ref[rq, :] * SCALE, head0)
            k2 = jnp.concatenate([k_src[rp, :], k_src[rq, :]], axis=0).astype(BF16)
            v2 = jnp.concatenate([v_src[rp, :], v_src[rq, :]], axis=0).astype(BF16)
            sc = _dot_nt(q2, k2) + _band_bias(bias_scr, b)
            mb = jnp.max(sc, axis=1, keepdims=True)
            p = jnp.exp(sc - mb).astype(BF16)
            ob = _dot_nn(p, jnp.concatenate([v2, ones], axis=1))
            m_blk = _unstack_heads(jnp.broadcast_to(mb, (2 * BLOCK, LANES)), head0)
            l_blk = _unstack_heads(ob[:, LANES:], head0)
            o_blk = _unstack_heads(ob[:, :LANES], head0)
            if first and l_init == 0.0:
                m_new, l_new, o_new = m_blk, l_blk, o_blk
            else:
                if first:
                    m_old, l_old, o_old = jnp.broadcast_to(mi_ref[...], (BLOCK, LANES)), l_init, 0.0
                else:
                    m_old, l_old, o_old = m_scr[rq, :], l_scr[rq, :], o_ref[rq, :]
                m_new = jnp.maximum(m_old, m_blk)
                a_old = jnp.exp(m_old - m_new)
                a_blk = jnp.exp(m_blk - m_new)
                l_new = l_old * a_old + l_blk * a_blk
                o_new = o_old * a_old + o_blk * a_blk
            o_ref[rq, :], l_scr[rq, :], m_scr[rq, :] = o_new, l_new, m_new

        _block_loops(s, patterns, ATTN_UNROLL, one_block)

        def fin(c, carry):
            rows = pl.ds(pl.multiple_of(c * ATTN_ROWS, ATTN_ROWS), ATTN_ROWS)
            l = l_scr[rows, :]
            o_ref[rows, :] = o_ref[rows, :] / l
            lse = m_scr[rows, :] + jnp.log(l)
            swapped = pltpu.roll(lse, HEAD_DIM, 1)
            lse0_ref[rows, :] = jnp.where(head0, lse, swapped)
            lse1_ref[rows, :] = jnp.where(head0, swapped, lse)
            return carry

        lax.fori_loop(0, s // ATTN_ROWS, fin, 0)

    kv = (lambda blk: pl.BlockSpec((s, LANES), lambda j, blk=blk: (0, blk), pipeline_mode=pl.Buffered(1))) if gqa \
        else (lambda blk: pl.BlockSpec((s, LANES), lambda j, blk=blk: (0, blk + j)))
    own = pl.BlockSpec((s, LANES), lambda j: (0, j))
    return _call(
        body,
        grid=(npair,),
        in_specs=[pl.BlockSpec((s, LANES), lambda j: (0, q_blk + j)), kv(k_blk), kv(v_blk),
                  pl.BlockSpec((1, LANES), lambda j: (0, j))],
        out_specs=[own, own, own],
        out_shape=[jax.ShapeDtypeStruct((s, npair * LANES), F32)] * 3,
        operands=(z, z, z, m_init), name=name,
        scratch_shapes=[pltpu.VMEM((2 * BLOCK, 2 * BLOCK), F32)] + [pltpu.VMEM((s, LANES), F32)] * (4 if gqa else 2),
        comm=comm)


def _attn_bwd(z, do, o, lse, m_init, dz, q_blk, k_blk, v_blk, patterns, max_dist, gqa, name, comm=None):
    s = z.shape[0]
    npair = 3
    n_dz_in = 0 if dz is None else 1

    def body(q_ref, k_ref, v_ref, do_ref, o_ref, lse0_ref, lse1_ref, mi_ref, *rest):
        (dz_ref, dm_ref, dq_acc, dk_acc, dv_acc, dl0_scr, dl1_scr, bias_scr,
         dq_out, dk_out, dv_out, out_sems) = rest[n_dz_in:]
        pair = pl.program_id(0)
        head0 = lax.broadcasted_iota(jnp.int32, (1, LANES), 1) < HEAD_DIM
        halves = _kv_halves(pair) if gqa else None
        _write_band_bias(bias_scr, max_dist)

        def zero_kv():
            def f(c, carry):
                rows = pl.ds(pl.multiple_of(c * ATTN_ROWS, ATTN_ROWS), ATTN_ROWS)
                dk_acc[rows, :] = jnp.zeros((ATTN_ROWS, LANES), F32)
                dv_acc[rows, :] = jnp.zeros((ATTN_ROWS, LANES), F32)
                return carry
            lax.fori_loop(0, s // ATTN_ROWS, f, 0)

        if gqa:
            pl.when(pair == 0)(zero_kv)
        else:
            zero_kv()

        def prep(c, dm):
            rows = pl.ds(pl.multiple_of(c * ATTN_ROWS, ATTN_ROWS), ATTN_ROWS)
            dq_acc[rows, :] = jnp.zeros((ATTN_ROWS, LANES), F32)
            prod = do_ref[rows, :] * o_ref[rows, :]
            d0 = jnp.sum(jnp.where(head0, prod, 0.0), axis=1, keepdims=True)
            d1 = jnp.sum(jnp.where(head0, 0.0, prod), axis=1, keepdims=True)
            dl0_scr[rows, :] = jnp.broadcast_to(d0, (ATTN_ROWS, LANES))
            dl1_scr[rows, :] = jnp.broadcast_to(d1, (ATTN_ROWS, LANES))
            lse_own = jnp.where(head0, lse0_ref[rows, :], lse1_ref[rows, :])
            psink = jnp.exp(mi_ref[...] - lse_own)
            return dm - jnp.sum(psink * jnp.where(head0, d0, d1), axis=0, keepdims=True)

        dm_ref[...] = lax.fori_loop(0, s // ATTN_ROWS, prep, jnp.zeros((1, LANES), F32))

        def one_block(b, d, r, first):
            rq, rp = _band_rows(b, d, r)
            q2 = _stack_heads(q_ref[rq, :] * SCALE, head0, halves)
            do2 = _stack_heads(do_ref[rq, :], head0, halves)
            k2 = jnp.concatenate([k_ref[rp, :], k_ref[rq, :]], axis=0).astype(BF16)
            v2 = jnp.concatenate([v_ref[rp, :], v_ref[rq, :]], axis=0).astype(BF16)
            lse2 = jnp.concatenate([lse0_ref[rq, :], lse1_ref[rq, :]], axis=0)
            dl2 = jnp.concatenate([dl0_scr[rq, :], dl1_scr[rq, :]], axis=0)
            lse2 = jnp.concatenate([lse2, lse2], axis=1)
            dl2 = jnp.concatenate([dl2, dl2], axis=1)
            p = jnp.exp(_dot_nt(q2, k2) + _band_bias(bias_scr, b) - lse2)
            dp = _dot_nt(do2, v2)
            dsc = (p * (dp - dl2)).astype(BF16)
            dq2 = _unstack_heads(_dot_nn(dsc, k2), head0, halves)
            dk2 = _dot_tn(dsc, q2)
            dv2 = _dot_tn(p.astype(BF16), do2)
            dq_acc[rq, :] += dq2 * SCALE
            dk_acc[rp, :] += dk2[:BLOCK]
            dk_acc[rq, :] += dk2[BLOCK:]
            dv_acc[rp, :] += dv2[:BLOCK]
            dv_acc[rq, :] += dv2[BLOCK:]

        _block_loops(s, patterns, ATTN_UNROLL, one_block)

        def to_dz(staged, blk, k):
            cols = pl.ds(pl.multiple_of(blk * LANES, LANES), LANES)
            return pltpu.make_async_copy(staged, dz_ref.at[:, cols], out_sems.at[k])

        last_pair = pair == npair - 1
        q_copy = to_dz(dq_out, q_blk + pair, 0)
        kv_copies = [to_dz(dk_out, k_blk + (0 if gqa else pair), 1), to_dz(dv_out, v_blk + (0 if gqa else pair), 2)]

        @pl.when(pair > 0)
        def _():
            for cp in [q_copy] + ([] if gqa else kv_copies):
                cp.wait()

        def stage(acc, out):
            def f(c, carry):
                rows = pl.ds(pl.multiple_of(c * ATTN_ROWS, ATTN_ROWS), ATTN_ROWS)
                out[rows, :] = acc[rows, :].astype(BF16)
                return carry
            lax.fori_loop(0, s // ATTN_ROWS, f, 0)

        def stage_kv():
            stage(dk_acc, dk_out)
            stage(dv_acc, dv_out)
            for cp in kv_copies:
                cp.start()

        stage(dq_acc, dq_out)
        q_copy.start()
        if gqa:
            pl.when(last_pair)(stage_kv)
        else:
            stage_kv()

        @pl.when(last_pair)
        def _():
            for cp in [q_copy] + kv_copies:
                cp.wait()

    own = pl.BlockSpec((s, LANES), lambda j: (0, j))
    hbm = pl.BlockSpec(memory_space=pl.ANY)
    if gqa:
        kv = lambda blk: pl.BlockSpec((s, LANES), lambda j, blk=blk: (0, blk), pipeline_mode=pl.Buffered(1))
    else:
        kv = lambda blk: pl.BlockSpec((s, LANES), lambda j, blk=blk: (0, blk + j))
    in_specs = [pl.BlockSpec((s, LANES), lambda j: (0, q_blk + j)), kv(k_blk), kv(v_blk), own, own, own, own,
                pl.BlockSpec((1, LANES), lambda j: (0, j))]
    operands = (z, z, z, do, o, lse[0], lse[1], m_init)
    return _call(
        body,
        grid=(npair,),
        in_specs=in_specs + [hbm] * n_dz_in,
        out_specs=[hbm, pl.BlockSpec((1, LANES), lambda j: (0, j))],
        out_shape=[jax.ShapeDtypeStruct((s, IN_WIDTH), BF16), jax.ShapeDtypeStruct((1, npair * LANES), F32)],
        operands=operands + (() if dz is None else (dz,)), name=name,
        scratch_shapes=[pltpu.VMEM((s, LANES), F32)] * 5 + [pltpu.VMEM((2 * BLOCK, 2 * BLOCK), F32)]
        + [pltpu.VMEM((s, LANES), BF16)] * 3 + [pltpu.SemaphoreType.DMA((3,))],
        comm=comm, aliases={} if dz is None else {len(in_specs): 0})


def _adamw_math(w, g, m, v):
    m = ADAM_B1 * m + (1.0 - ADAM_B1) * g
    v = ADAM_B2 * v + (1.0 - ADAM_B2) * (g * g)
    m_hat = m / (1.0 - ADAM_B1 ** ADAM_STEP)
    v_hat = v / (1.0 - ADAM_B2 ** ADAM_STEP)
    delta = -ADAM_LR * (m_hat / (jnp.sqrt(v_hat) + ADAM_EPS) + ADAM_WD * w)
    return delta, m, v


def _adamw(w, g, m, v, name):
    rows, cols = w.shape
    tr = min(rows, 256)

    def body(w_ref, g_ref, m_ref, v_ref, d_ref, nm_ref, nv_ref):
        d_ref[...], nm_ref[...], nv_ref[...] = _adamw_math(w_ref[...], g_ref[...], m_ref[...], v_ref[...])

    spec = pl.BlockSpec((tr, cols), lambda i: (i, 0))
    return pl.pallas_call(
        body,
        grid=(rows // tr,),
        in_specs=[spec] * 4,
        out_specs=[spec] * 3,
        out_shape=[jax.ShapeDtypeStruct((rows, cols), F32)] * 3,
        compiler_params=_params("parallel"),
        name=name,
    )(w, g, m, v)


def _sum_adamw(parts, w, m, v, pos, transpose, name):
    assert len(parts) == DEPTH == 2
    (p0, r0), (p1, r1) = parts
    _, rows, cols = p0.shape
    tr = 256 if rows % 256 == 0 else rows
    nt = rows // tr

    def body(pos_ref, p0_ref, r0_ref, p1_ref, r1_ref, w_ref, m_ref, v_ref, g_ref, d_ref, nm_ref, nv_ref):
        def run(p_ref, r_ref):
            g = ((p_ref[...].astype(F32) + r_ref[0].astype(F32)) + r_ref[1].astype(F32)) + r_ref[2].astype(F32)
            if transpose:
                g = g.T
            g_ref[...] = g
            d_ref[...], nm_ref[...], nv_ref[...] = _adamw_math(w_ref[...], g, m_ref[...], v_ref[...])

        layer0 = pl.program_id(0) < nt
        pl.when(layer0)(lambda: run(p0_ref, r0_ref))
        pl.when(jnp.logical_not(layer0))(lambda: run(p1_ref, r1_ref))

    def tile0(i):
        return jnp.minimum(i, nt - 1)

    def tile1(i):
        return jnp.maximum(i - nt, 0)

    if transpose:
        w_spec = pl.BlockSpec((None, cols, tr), lambda i, q: (i // nt, 0, i % nt))
    else:
        w_spec = pl.BlockSpec((None, tr, cols), lambda i, q: (i // nt, i % nt, 0))
    return pl.pallas_call(
        body,
        grid_spec=pltpu.PrefetchScalarGridSpec(
            num_scalar_prefetch=1,
            grid=(DEPTH * nt,),
            in_specs=[pl.BlockSpec((None, tr, cols), lambda i, q: (q[0], tile0(i), 0)),
                      pl.BlockSpec((3, tr, cols), lambda i, q: (0, tile0(i), 0)),
                      pl.BlockSpec((None, tr, cols), lambda i, q: (q[0], tile1(i), 0)),
                      pl.BlockSpec((3, tr, cols), lambda i, q: (0, tile1(i), 0)),
                      w_spec, w_spec, w_spec],
            out_specs=[w_spec] * 4,
        ),
        out_shape=[jax.ShapeDtypeStruct(w.shape, F32)] * 4,
        compiler_params=_params("arbitrary"),
        name=name,
    )(pos, p0, r0, p1, r1, w, m, v)


def _small_sum_adamw(gathered, params, name):
    _, rows, cols = gathered.shape
    n = len(params)

    def body(ga_ref, *refs):
        ins, outs, (g_scr,) = refs[:3 * n], refs[3 * n:7 * n + 2], refs[7 * n + 2:]
        g = ga_ref[0]
        for i in range(1, N_DEV):
            g = g + ga_ref[i]
        g_scr[...] = g
        for k, (row0, w, _, _) in enumerate(params):
            w_ref, m_ref, v_ref = ins[3 * k:3 * k + 3]
            gk = g_scr[row0:row0 + w.shape[0], :]
            outs[4 * k][...] = gk
            outs[4 * k + 1][...], outs[4 * k + 2][...], outs[4 * k + 3][...] = _adamw_math(
                w_ref[...], gk, m_ref[...], v_ref[...])
        outs[4 * n][...] = g_scr[CONV_ROW:CONV_ROW + 8, :]
        outs[4 * n + 1][...] = g_scr[LOSS_ROW:LOSS_ROW + 1, :]

    out_shape = []
    for _, w, _, _ in params:
        out_shape += [jax.ShapeDtypeStruct(w.shape, F32)] * 4
    out_shape += [jax.ShapeDtypeStruct((8, cols), F32), jax.ShapeDtypeStruct((1, cols), F32)]
    res = pl.pallas_call(
        body,
        out_shape=out_shape,
        scratch_shapes=[pltpu.VMEM((rows, cols), F32)],
        name=name,
    )(gathered, *[t for _, w, m, v in params for t in (w, m, v)])
    return [res[4 * k:4 * k + 4] for k in range(n)], res[4 * n], res[4 * n + 1]


def _pair_sums(g4s, r1s, pos, name):
    n = len(g4s)

    def body(pos_ref, *refs):
        for g_ref, r_ref, o_ref in zip(refs[:n], refs[n:2 * n], refs[2 * n:]):
            o_ref[...] = (g_ref[...].astype(F32) + r_ref[...].astype(F32)).astype(BF16)

    block = lambda t: pl.BlockSpec((None,) + t.shape[1:], lambda i, p: (i, 0, 0))
    return pl.pallas_call(
        body,
        grid_spec=pltpu.PrefetchScalarGridSpec(
            num_scalar_prefetch=1,
            grid=(4,),
            in_specs=[pl.BlockSpec((None, None) + g.shape[2:], lambda i, p: (i, p[1], 0, 0)) for g in g4s]
            + [block(r) for r in r1s],
            out_specs=[block(r) for r in r1s],
        ),
        out_shape=[jax.ShapeDtypeStruct(r.shape, BF16) for r in r1s],
        compiler_params=_params("parallel"),
        name=name,
    )(pos, *g4s, *r1s)


GATHER_ID, CHIP_ID, SIBLING_ID = 0, 1, 2


def _place():
    return lax.axis_index("x"), lax.axis_index("y"), lax.axis_index("c")


def _sibling():
    x, y, c = _place()
    return (x, y, 1 - c)


def _same_core_of_other_chips():
    x, y, c = _place()
    return [(1 - x, y, c), (x, 1 - y, c), (1 - x, 1 - y, c)]


def _gather_comm(shards):
    na = len(shards)
    stacks, index = zip(*shards)

    def plan(ins, outs, sems):
        send_sems, recv_sems, local_sems = sems
        x, y, c = _place()
        me, sibling = (x, y, c), (x, y, 1 - c)
        chips = [(1 - x, y), (x, 1 - y), (1 - x, 1 - y)]
        shard = [ins[a].at[index[a]] for a in range(na)]

        def rows(a, px, py, pc):
            m = shard[a].shape[0]
            return outs[a].at[pl.ds((4 * px + 2 * py + pc) * m, m), :]

        def copy(a, k, block, to, src=None):
            return pltpu.make_async_remote_copy(
                src_ref=rows(a, *block) if src is None else src, dst_ref=rows(a, *block),
                send_sem=send_sems.at[a, k], recv_sem=recv_sems.at[a, k], device_id=to, device_id_type=MESH)

        mine = [pltpu.make_async_copy(shard[a], rows(a, *me), local_sems.at[a]) for a in range(na)]
        first = []
        for a in range(na):
            first.append(copy(a, 0, me, sibling, src=shard[a]))
            first += [copy(a, 1 + j, me, (*chip, c), src=shard[a]) for j, chip in enumerate(chips)]
        return me, sibling, chips, c, copy, mine, first

    def start(ins, outs, sems):
        *_, mine, first = plan(ins, outs, sems)
        for cp in mine + first:
            cp.start()

    def finish(ins, outs, sems):
        me, sibling, chips, c, copy, mine, first = plan(ins, outs, sems)
        passed = []
        for j, chip in enumerate(chips):
            for a in range(na):
                copy(a, 1 + j, (*chip, c), me).wait_recv()
                cp = copy(a, 4 + j, (*chip, c), sibling)
                cp.start()
                passed.append(cp)
        for a in range(na):
            copy(a, 0, sibling, me).wait_recv()
            for j, chip in enumerate(chips):
                copy(a, 4 + j, (*chip, 1 - c), me).wait_recv()
        for cp in first + passed:
            cp.wait_send()
        for cp in mine:
            cp.wait()

    return _Comm(tuple(stacks),
                 tuple(jax.ShapeDtypeStruct((N_DEV * t.shape[1], t.shape[2]), t.dtype) for t in stacks),
                 (pltpu.SemaphoreType.DMA((na, 7)), pltpu.SemaphoreType.DMA((na, 7)), pltpu.SemaphoreType.DMA((na,))),
                 start, finish, lambda: [_sibling()] + _same_core_of_other_chips(), GATHER_ID)


def _exchange_comm(arrays, out_shape, n_copies, copies_of, peers, collective_id):
    na = len(arrays)

    def every(ins, outs, sems):
        send_sems, recv_sems = sems
        return [cp for a in range(na) for cp in copies_of(ins, outs, a, send_sems, recv_sems)]

    def start(ins, outs, sems):
        for cp in every(ins, outs, sems):
            cp.start()

    def finish(ins, outs, sems):
        for cp in every(ins, outs, sems):
            cp.wait()

    return _Comm(tuple(arrays), tuple(out_shape),
                 (pltpu.SemaphoreType.DMA((na, n_copies)), pltpu.SemaphoreType.DMA((na, n_copies))), start, finish,
                 peers, collective_id)


def _sibling_comm(grads):
    def copies_of(ins, outs, a, send_sems, recv_sems):
        x, y, c = _place()
        return [pltpu.make_async_remote_copy(
            src_ref=ins[a].at[chip, 1 - c], dst_ref=outs[a].at[chip],
            send_sem=send_sems.at[a, chip], recv_sem=recv_sems.at[a, chip],
            device_id=(x, y, 1 - c), device_id_type=MESH) for chip in range(4)]

    return _exchange_comm(grads, [jax.ShapeDtypeStruct((4,) + t.shape[2:], t.dtype) for t in grads], 4, copies_of,
                          lambda: [_sibling()], SIBLING_ID)


def _chip_comm(partials):
    def copies_of(ins, outs, a, send_sems, recv_sems):
        x, y, c = _place()
        chips = [(1 - x, y), (x, 1 - y), (1 - x, 1 - y)]
        return [pltpu.make_async_remote_copy(
            src_ref=ins[a].at[2 * cx + cy], dst_ref=outs[a].at[k],
            send_sem=send_sems.at[a, k], recv_sem=recv_sems.at[a, k],
            device_id=(cx, cy, c), device_id_type=MESH) for k, (cx, cy) in enumerate(chips)]

    return _exchange_comm(partials, [jax.ShapeDtypeStruct((3,) + t.shape[1:], t.dtype) for t in partials], 3, copies_of,
                          _same_core_of_other_chips, CHIP_ID)


def _pad_rows(t, rows):
    return jnp.pad(t, ((0, rows - t.shape[0]), (0, D_MODEL - t.shape[1])))


MIX_ROW, GROUP_ROW, MLP_ROW, FINAL_ROW, CONV_ROW, SINK_ROW = 0, 8, 16, 24, 32, 40
LOSS_ROW = FINAL_ROW + 1


def _pack_small(g_mix, g_group, g_mlp, g_final, conv, sinks, loss):
    final_and_loss = jnp.concatenate([g_final.reshape(1, D_MODEL), _pad_rows(loss, 1)], axis=0)
    return jnp.concatenate([
        _pad_rows(g_mix, 8), _pad_rows(g_group, 8), _pad_rows(g_mlp, 8), _pad_rows(final_and_loss, 8),
        _pad_rows(conv.reshape(DEPTH * 3, CONV_CH), 8), _pad_rows(sinks.reshape(1, DEPTH * 6), 8)], axis=0)


def kernel(x, w_in, conv_w, sinks, g_mix, g_group, w_o, g_mlp, w_ff_in, w_ff_out, g_final, loss_target, m_w_in, m_conv_w, m_sinks, m_g_mix, m_g_group, m_w_o, m_g_mlp, m_w_ff_in, m_w_ff_out, m_g_final, v_w_in, v_conv_w, v_sinks, v_g_mix, v_g_group, v_w_o, v_g_mlp, v_w_ff_in, v_w_ff_out, v_g_final):
    ax, ay, ac = _place()
    chip = 2 * ax + ay
    dev = 4 * ax + 2 * ay + ac
    pos = jnp.stack([chip, ac]).astype(jnp.int32)

    x0 = x.reshape(SEQ, D_MODEL)
    target = loss_target.reshape(SEQ, D_MODEL)

    stacks = [jnp.swapaxes(w_in, 1, 2).astype(BF16), w_o.astype(BF16),
              jnp.swapaxes(w_ff_in, 1, 2).astype(BF16), w_ff_out.astype(BF16)]
    shards = {(l, kind): (stack, l) for kind, stack in enumerate(stacks) for l in range(DEPTH)}
    conv_tile = jnp.pad(conv_w.reshape(DEPTH * 3, CONV_CH // N_DEV), ((0, 2), (0, LANES - CONV_CH // N_DEV)))
    wt_in0, conv_all = _comm_only(_gather_comm([shards[0, 0], (conv_tile[None], 0)]), "gather_first")
    conv_full = conv_all.reshape(N_DEV, 8, LANES)[:, :DEPTH * 3, :CONV_CH // N_DEV]
    conv_full = conv_full.transpose(1, 0, 2).reshape(DEPTH, 3, CONV_CH)

    dx, parts, small = _step(x0, target, shards, wt_in0, conv_full, sinks, g_mix, g_group, g_mlp, g_final, pos)
    return _finish(dx, parts, small, pos, dev, w_in, conv_w, sinks, g_mix, g_group, w_o, g_mlp, w_ff_in, w_ff_out, g_final, m_w_in, m_conv_w, m_sinks, m_g_mix, m_g_group, m_w_o, m_g_mlp, m_w_ff_in, m_w_ff_out, m_g_final, v_w_in, v_conv_w, v_sinks, v_g_mix, v_g_group, v_w_o, v_g_mlp, v_w_ff_in, v_w_ff_out, v_g_final)


FWD_CARRY = {(0, "in_proj"): ((1, 0),), (0, "window"): ((0, 1),), (0, "dilated"): ((0, 2),),
             (0, "mix_ff_in"): ((0, 3),), (0, "ff_out_in_proj"): ((1, 1), (1, 3)),
             (1, "dilated"): ((1, 2),)}


def _step(x0, target, shards, wt_in0, conv_full, sinks, g_mix, g_group, g_mlp, g_final, pos):
    sink_lanes = jnp.repeat(sinks.reshape(DEPTH, 6), HEAD_DIM, axis=1)
    no_sink = jnp.full((1, A_WIDTH), NEG_BIG, F32)
    full = {(0, 0): wt_in0}

    def gather(stage, l):
        keys = FWD_CARRY.get((l, stage), ())
        return keys, (_gather_comm([shards[k] for k in keys]) if keys else None)

    def landed(keys, got):
        full.update(zip(keys, got))

    saved = []
    xc = x0
    keys, comm = gather("in_proj", 0)
    (z, h), got = _norm_mm(xc, g_mix[0:1], full[0, 0], "in_proj_0", comm)
    landed(keys, got)
    for l in range(DEPTH):
        sink_l = sink_lanes[l:l + 1]
        keys, comm = gather("window", l)
        (yc, *lse_c), got = _attn_fwd(z, sink_l, 1.0, QC_BLK, KC_BLK, VC_BLK, (1,), C_MAX_DIST, True,
                                     f"window_attn_{l}", comm)
        landed(keys, got)
        yb = _conv_fwd(z, conv_full[l], f"conv_{l}")
        keys, comm = gather("dilated", l)
        (ya, *lse_a), got = _attn_fwd(z, no_sink, 0.0, QA_BLK, KA_BLK, VA_BLK, DILATED_PATTERNS, A_MAX_DIST, False,
                                     f"dilated_attn_{l}", comm)
        landed(keys, got)
        keys, comm = gather("mix_ff_in", l)
        (y, x1, a, h2), got = _mix_ff_in(ya, yb, yc, g_group[l:l + 1], full[l, 1], xc, g_mlp[l:l + 1], full[l, 2],
                                         f"mix_ff_in_{l}", comm)
        landed(keys, got)
        saved.append((xc, z, h, ya, lse_a, yb, yc, lse_c, sink_l, y, x1, a, h2))
        if l + 1 < DEPTH:
            keys, comm = gather("ff_out_in_proj", l)
            (xc, z, h), got = _ff_out_in_proj(a, full[l, 3], x1, g_mix[l + 1:l + 2], full[l + 1, 0],
                                              f"ff_out_{l}_in_proj_{l + 1}", comm)
            landed(keys, got)

    loss_slab, dx, dxb, dg_final, du = _mm_res_loss(a, full[DEPTH - 1, 3], x1, g_final.reshape(1, D_MODEL), target,
                                                    f"ff_out_{DEPTH - 1}_loss")

    def by_owner(t):
        return t.reshape(4, 2, t.shape[0] // N_DEV, D_MODEL)

    def pair(l, kinds, grads, received):
        sums = _pair_sums(grads, received, pos, f"grad_pair_sums_{l}_{kinds[0]}{kinds[1]}")
        partial.update({(l, kind): t for kind, t in zip(kinds, sums)})

    partial, r2 = {}, {}
    dg_mix, dg_group, dg_mlp, dconv, dsinks = [None] * DEPTH, [None] * DEPTH, [None] * DEPTH, [None] * DEPTH, [None] * DEPTH
    for l in reversed(range(DEPTH)):
        xin, z, h, ya, lse_a, yb, yc, lse_c, sink_l, y, x1, a, h2 = saved[l]
        if l + 1 < DEPTH:
            late = [(l + 1, 1), (l + 1, 0)]
            (du,), got = _mlp_bwd_act(dxb, full[l, 3], a, f"ff_out_bwd_{l}", _chip_comm([partial[k] for k in late]))
            r2.update(zip(late, got))
        (g3, g2), _ = _mm_tn([(a, dxb), (du, h2)], f"grad_w_ff_{l}")
        g3, g2 = by_owner(g3), by_owner(g2)
        (dx1, dx1b, dg_mlp[l], dya, dyb, dyc, dg_group[l]), got = _ff_in_mix_bwd(
            du, full[l, 2], x1, dx, g_mlp[l:l + 1], full[l, 1], ya, yb, yc, g_group[l:l + 1],
            f"ff_in_mix_bwd_{l}", _sibling_comm([g3, g2]))
        pair(l, (3, 2), [g3, g2], got)
        early = [(l, 3), (l, 2)]
        (dz, _), got = _attn_bwd(z, dya, ya, lse_a, no_sink, None, QA_BLK, KA_BLK, VA_BLK, DILATED_PATTERNS,
                                 A_MAX_DIST, False, f"dilated_attn_bwd_{l}", _chip_comm([partial[k] for k in early]))
        r2.update(zip(early, got))
        dz, dcw = _conv_bwd(z, conv_full[l], dyb, dz, f"conv_bwd_{l}")
        (dz, dsink), _ = _attn_bwd(z, dyc, yc, lse_c, sink_l, dz, QC_BLK, KC_BLK, VC_BLK, (1,), C_MAX_DIST,
                                   True, f"window_attn_bwd_{l}")
        (g1, g0), _ = _mm_tn([(y, dx1b), (dz, h)], f"grad_w_o_in_{l}")
        g1, g0 = by_owner(g1), by_owner(g0)
        if l > 0:
            (dx, dxb, dg_mix[l]), got = _mm_nn_normbwd(dz, full[l, 0], xin, dx1, g_mix[l:l + 1], f"in_proj_bwd_{l}",
                                                      _sibling_comm([g1, g0]))
            pair(l, (1, 0), [g1, g0], got)
        else:
            got = _comm_only(_sibling_comm([g1, g0]), "grad_sibling_exchange_last")
            pair(l, (1, 0), [g1, g0], got)
            (dx, dxb, dg_mix[l]), got = _mm_nn_normbwd(dz, full[l, 0], xin, dx1, g_mix[l:l + 1], f"in_proj_bwd_{l}",
                                                      _chip_comm([partial[l, 1], partial[l, 0]]))
            r2[l, 1], r2[l, 0] = got
        dconv[l] = dcw[:3]
        dsinks[l] = dsink[0, ::HEAD_DIM]
    parts = {key: (partial[key], r2[key]) for key in partial}
    small = _pack_small(jnp.concatenate(dg_mix), jnp.concatenate(dg_group), jnp.concatenate(dg_mlp),
                        dg_final, jnp.stack(dconv), jnp.stack(dsinks), loss_slab[0:1])
    return dx, parts, small


def _finish(dx, parts, small, pos, dev, w_in, conv_w, sinks, g_mix, g_group, w_o, g_mlp, w_ff_in, w_ff_out, g_final, m_w_in, m_conv_w, m_sinks, m_g_mix, m_g_group, m_w_o, m_g_mlp, m_w_ff_in, m_w_ff_out, m_g_final, v_w_in, v_conv_w, v_sinks, v_g_mix, v_g_group, v_w_o, v_g_mlp, v_w_ff_in, v_w_ff_out, v_g_final):
    grad_x = dx.reshape(1, SEQ, D_MODEL)

    (small_all,) = _comm_only(_gather_comm([(small[None], 0)]), "gather_small_grads")
    row = lambda t: t.reshape(1, D_MODEL)
    sink_row = lambda t: _pad_rows(t.reshape(1, DEPTH * 6), 1)
    params = [(MIX_ROW, g_mix, m_g_mix, v_g_mix), (GROUP_ROW, g_group, m_g_group, v_g_group),
              (MLP_ROW, g_mlp, m_g_mlp, v_g_mlp), (FINAL_ROW, row(g_final), row(m_g_final), row(v_g_final)),
              (SINK_ROW, sink_row(sinks), sink_row(m_sinks), sink_row(v_sinks))]
    updated, conv_rows, loss_row = _small_sum_adamw(small_all.reshape(N_DEV, SMALL_ROWS, D_MODEL), params, "small_adamw")
    loss = loss_row[0, 0]
    (grad_g_mix, delta_g_mix, new_m_g_mix, new_v_g_mix), (grad_g_group, delta_g_group, new_m_g_group, new_v_g_group), \
        (grad_g_mlp, delta_g_mlp, new_m_g_mlp, new_v_g_mlp), final4, sinks4 = updated
    grad_g_final, delta_g_final, new_m_g_final, new_v_g_final = [t.reshape(D_MODEL) for t in final4]
    grad_sinks, delta_sinks, new_m_sinks, new_v_sinks = [t[0, :DEPTH * 6].reshape(DEPTH, 2, 3) for t in sinks4]
    conv_grad_full = conv_rows[:DEPTH * 3, :CONV_CH].reshape(DEPTH, 3, CONV_CH)
    cs = CONV_CH // N_DEV
    grad_conv_w = lax.dynamic_slice_in_dim(conv_grad_full, dev * cs, cs, axis=2)

    def tile_of(t):
        return jnp.pad(t.reshape(1, DEPTH * 3 * cs), ((0, 7), (0, 256 - DEPTH * 3 * cs)))

    cd, cm, cv = _adamw(tile_of(conv_w), tile_of(grad_conv_w), tile_of(m_conv_w), tile_of(v_conv_w), "conv_adamw")
    untile = lambda t: t[0, :DEPTH * 3 * cs].reshape(DEPTH, 3, cs)
    delta_conv_w, new_m_conv_w, new_v_conv_w = untile(cd), untile(cm), untile(cv)

    def big(kind, w, m, v, transpose, name):
        return _sum_adamw([parts[l, kind] for l in range(DEPTH)], w, m, v, pos, transpose, name)

    swap = lambda t: jnp.swapaxes(t, 1, 2)
    grad_w_in, delta_w_in, new_m_w_in, new_v_w_in = [
        swap(t) for t in big(0, swap(w_in), swap(m_w_in), swap(v_w_in), False, "adamw_w_in")]
    grad_w_o, delta_w_o, new_m_w_o, new_v_w_o = big(1, w_o, m_w_o, v_w_o, False, "adamw_w_o")
    grad_w_ff_in, delta_w_ff_in, new_m_w_ff_in, new_v_w_ff_in = big(2, w_ff_in, m_w_ff_in, v_w_ff_in, True, "adamw_w_ff_in")
    grad_w_ff_out, delta_w_ff_out, new_m_w_ff_out, new_v_w_ff_out = big(3, w_ff_out, m_w_ff_out, v_w_ff_out, False,
                                                                         "adamw_w_ff_out")

    return (loss, grad_x, grad_w_in, grad_conv_w, grad_sinks, grad_g_mix, grad_g_group, grad_w_o, grad_g_mlp,
            grad_w_ff_in, grad_w_ff_out, grad_g_final,
            delta_w_in, delta_conv_w, delta_sinks, delta_g_mix, delta_g_group, delta_w_o, delta_g_mlp,
            delta_w_ff_in, delta_w_ff_out, delta_g_final,
            new_m_w_in, new_m_conv_w, new_m_sinks, new_m_g_mix, new_m_g_group, new_m_w_o, new_m_g_mlp,
            new_m_w_ff_in, new_m_w_ff_out, new_m_g_final,
            new_v_w_in, new_v_conv_w, new_v_sinks, new_v_g_mix, new_v_g_group, new_v_w_o, new_v_g_mlp,
            new_v_w_ff_in, new_v_w_ff_out, new_v_g_final)
```

```python
from typing import Callable, NamedTuple

import jax
import jax.numpy as jnp
from jax import lax
from jax.experimental import pallas as pl
from jax.experimental.pallas import tpu as pltpu

F32 = jnp.float32
BF16 = jnp.bfloat16
MESH = pl.DeviceIdType.MESH

N_DEV = 8
SEQ = 4096
D_MODEL = 1024
DEPTH = 2
HEAD_DIM = 64
LANES = 128
A_WIDTH = 384
CONV_CH = 256
C_WIDTH = 384
IN_WIDTH = 2560
BLOCK = 128
DILATED_PATTERNS = (1, 4, 16)
A_MAX_DIST = 128
C_MAX_DIST = 127
EPS = 1e-6
SCALE = HEAD_DIM ** -0.5
NEG_BIG = -1e30
F32_TINY = 1.1754944e-38

QA_BLK, KA_BLK, VA_BLK = 0, 3, 6
GB_BLK, GC_BLK, XB_BLK = 9, 11, 13
QC_BLK, KC_BLK, VC_BLK = 15, 18, 19

ADAM_LR = 0.001
ADAM_B1 = 0.9
ADAM_B2 = 0.999
ADAM_EPS = 1e-08
ADAM_WD = 0.01
ADAM_STEP = 10

VMEM_LIMIT = 56 * 1024 * 1024
TILE_BUDGET = 46 * 1024 * 1024
ROW_TILE = 512
COL_CHUNK = 512
SMALL_ROWS = 48


def _dot_nn(a, b):
    return lax.dot_general(a, b, (((1,), (0,)), ((), ())), preferred_element_type=F32)


def _dot_nt(a, b):
    return lax.dot_general(a, b, (((1,), (1,)), ((), ())), preferred_element_type=F32)


def _dot_tn(a, b):
    return lax.dot_general(a, b, (((0,), (0,)), ((), ())), preferred_element_type=F32)


def _params(*sem, collective_id=None):
    return pltpu.CompilerParams(dimension_semantics=sem, vmem_limit_bytes=VMEM_LIMIT, collective_id=collective_id)


def _resident(shape):
    return pl.BlockSpec(shape, lambda i: (0,) * len(shape), pipeline_mode=pl.Buffered(1))


class _Late(NamedTuple):
    hbm: object
    vmem: object
    sem: object

    def fetch(self, needed_at_step):
        copy = pltpu.make_async_copy(self.hbm, self.vmem, self.sem)
        pl.when(pl.program_id(0) == 0)(copy.start)
        return lambda: pl.when(pl.program_id(0) == needed_at_step)(copy.wait)


LATE_SPEC = pl.BlockSpec(memory_space=pl.ANY)


def _late_scratch(t):
    return [pltpu.VMEM(t.shape, t.dtype), pltpu.SemaphoreType.DMA(())]


def _row_tile(row_bytes, resident_bytes):
    for tm in (ROW_TILE, ROW_TILE // 2):
        if 2 * tm * row_bytes + resident_bytes <= TILE_BUDGET:
            return tm
    return ROW_TILE // 4


def _rms_scale(t):
    return lax.rsqrt(jnp.mean(t * t, axis=-1, keepdims=True) + EPS)


def _rms_bwd(n, r, dn):
    return r * (dn - n * jnp.mean(dn * n, axis=-1, keepdims=True))


class _Comm(NamedTuple):
    arrays: tuple
    out_shape: tuple
    sems: tuple
    start: Callable
    finish: Callable
    peers: Callable
    collective_id: int


def _handshake(comm):
    barrier = pltpu.get_barrier_semaphore()
    peers = comm.peers()
    for peer in peers:
        pl.semaphore_signal(barrier, inc=1, device_id=peer, device_id_type=MESH)
    pl.semaphore_wait(barrier, len(peers))


def _call(body, grid, in_specs, out_specs, out_shape, operands, name, scratch_shapes=(), comm=None, aliases=None):
    n_in, n_out, n_scr = len(in_specs), len(out_shape), len(scratch_shapes)
    aliases = dict(aliases or {})
    if comm is None:
        res = pl.pallas_call(body, grid=grid, in_specs=list(in_specs), out_specs=list(out_specs),
                             out_shape=list(out_shape), scratch_shapes=list(scratch_shapes),
                             input_output_aliases=aliases,
                             compiler_params=_params("arbitrary"), name=name)(*operands)
        return list(res), []
    c_in, c_out = len(comm.arrays), len(comm.out_shape)
    hbm = pl.BlockSpec(memory_space=pl.ANY)
    last = grid[0] - 1

    def carried(*refs):
        ins, cins = refs[:n_in], refs[n_in:n_in + c_in]
        o0 = n_in + c_in
        outs, couts = refs[o0:o0 + n_out], refs[o0 + n_out:o0 + n_out + c_out]
        s0 = o0 + n_out + c_out
        scr, sems = refs[s0:s0 + n_scr], refs[s0 + n_scr:]
        @pl.when(pl.program_id(0) == 0)
        def _():
            _handshake(comm)
            comm.start(cins, couts, sems)

        body(*ins, *outs, *scr)
        pl.when(pl.program_id(0) == last)(lambda: comm.finish(cins, couts, sems))

    res = pl.pallas_call(carried, grid=grid, in_specs=list(in_specs) + [hbm] * c_in,
                         out_specs=list(out_specs) + [hbm] * c_out, out_shape=list(out_shape) + list(comm.out_shape),
                         scratch_shapes=list(scratch_shapes) + list(comm.sems), input_output_aliases=aliases,
                         compiler_params=_params("arbitrary", collective_id=comm.collective_id),
                         name=name)(*operands, *comm.arrays)
    return list(res[:n_out]), list(res[n_out:])


def _comm_only(comm, name):
    hbm = pl.BlockSpec(memory_space=pl.ANY)
    c_in, c_out = len(comm.arrays), len(comm.out_shape)

    def body(*refs):
        ins, outs, sems = refs[:c_in], refs[c_in:c_in + c_out], refs[c_in + c_out:]
        _handshake(comm)
        comm.start(ins, outs, sems)
        comm.finish(ins, outs, sems)

    return pl.pallas_call(body, in_specs=[hbm] * c_in, out_specs=[hbm] * c_out, out_shape=list(comm.out_shape),
                          scratch_shapes=list(comm.sems),
                          compiler_params=pltpu.CompilerParams(collective_id=comm.collective_id),
                          name=name)(*comm.arrays)


def _norm_mm(x, g, wt, name, comm=None):
    s, d = x.shape
    n = wt.shape[0]
    tm = _row_tile(4 * d + 4 * n + 2 * d, 2 * n * d)

    def body(x_ref, g_ref, w_ref, o_ref, h_ref):
        xx = x_ref[...]
        h = ((xx * _rms_scale(xx)) * g_ref[...]).astype(BF16)
        h_ref[...] = h
        for n0 in range(0, n, COL_CHUNK):
            o_ref[:, n0:n0 + COL_CHUNK] = _dot_nt(h, w_ref[n0:n0 + COL_CHUNK, :])

    return _call(
        body,
        grid=(s // tm,),
        in_specs=[pl.BlockSpec((tm, d), lambda i: (i, 0)),
                  pl.BlockSpec((1, d), lambda i: (0, 0)),
                  _resident((n, d))],
        out_specs=[pl.BlockSpec((tm, n), lambda i: (i, 0)),
                   pl.BlockSpec((tm, d), lambda i: (i, 0))],
        out_shape=[jax.ShapeDtypeStruct((s, n), F32), jax.ShapeDtypeStruct((s, d), BF16)],
        operands=(x, g, wt), name=name, comm=comm)


def _ff_out_in_proj(a, w2, x1, g, wt, name, comm=None):
    s, f = a.shape
    d = w2.shape[1]
    n = wt.shape[0]
    tm = _row_tile(2 * f + 4 * d + 4 * d + 4 * n + 2 * d, 2 * f * d + 2 * n * d)

    def body(a_ref, w2_ref, x_ref, g_ref, w_ref, x2_ref, z_ref, h_ref):
        x2 = x_ref[...] + _dot_nn(a_ref[...], w2_ref[...])
        x2_ref[...] = x2
        h = ((x2 * _rms_scale(x2)) * g_ref[...]).astype(BF16)
        h_ref[...] = h
        for n0 in range(0, n, COL_CHUNK):
            z_ref[:, n0:n0 + COL_CHUNK] = _dot_nt(h, w_ref[n0:n0 + COL_CHUNK, :])

    rows = lambda w: pl.BlockSpec((tm, w), lambda i: (i, 0))
    return _call(
        body,
        grid=(s // tm,),
        in_specs=[rows(f), _resident((f, d)), rows(d), pl.BlockSpec((1, d), lambda i: (0, 0)), _resident((n, d))],
        out_specs=[rows(d), rows(n), rows(d)],
        out_shape=[jax.ShapeDtypeStruct((s, d), F32), jax.ShapeDtypeStruct((s, n), F32),
                   jax.ShapeDtypeStruct((s, d), BF16)],
        operands=(a, w2, x1, g, wt), name=name, comm=comm)


def _mix_ff_in(ya, yb, yc, gg, wo, x0, g_mlp, wt1, name, comm=None):
    s = ya.shape[0]
    d = wo.shape[1]
    f = wt1.shape[0]
    tm = _row_tile(4 * d + 4 * d + 2 * d + 4 * d + 2 * d + 2 * f, 2 * d * d + 2 * f * d)

    def body(ya_ref, yb_ref, yc_ref, gg_ref, wo_ref, x_ref, g_ref, w1_ref, y_ref, x1_ref, a_ref, h_ref):
        parts = []
        for ref in (ya_ref, yb_ref, yc_ref):
            t = ref[...]
            parts.append(t * _rms_scale(t))
        y = (jnp.concatenate(parts, axis=1) * gg_ref[...]).astype(BF16)
        y_ref[...] = y
        x1 = x_ref[...] + _dot_nn(y, wo_ref[...])
        x1_ref[...] = x1
        h = ((x1 * _rms_scale(x1)) * g_ref[...]).astype(BF16)
        h_ref[...] = h
        for n0 in range(0, f, COL_CHUNK):
            u = _dot_nt(h, w1_ref[n0:n0 + COL_CHUNK, :])
            a_ref[:, n0:n0 + COL_CHUNK] = jnp.square(jnp.maximum(u, 0.0)).astype(BF16)

    rows = lambda w: pl.BlockSpec((tm, w), lambda i: (i, 0))
    vec = pl.BlockSpec((1, d), lambda i: (0, 0))
    return _call(
        body,
        grid=(s // tm,),
        in_specs=[rows(A_WIDTH), rows(CONV_CH), rows(C_WIDTH), vec, _resident((d, d)), rows(d), vec, _resident((f, d))],
        out_specs=[rows(d), rows(d), rows(f), rows(d)],
        out_shape=[jax.ShapeDtypeStruct((s, d), BF16), jax.ShapeDtypeStruct((s, d), F32),
                   jax.ShapeDtypeStruct((s, f), BF16), jax.ShapeDtypeStruct((s, d), BF16)],
        operands=(ya, yb, yc, gg, wo, x0, g_mlp, wt1), name=name, comm=comm)


def _relu_from_square(av):
    return av * lax.rsqrt(jnp.maximum(av, F32_TINY))


def _mm_res_loss(a, w2, x1, g, target, name):
    s, f = a.shape
    d = w2.shape[1]
    tm = _row_tile(2 * f + 4 * d + 4 * d + 4 * d + 2 * d + 2 * f, 2 * f * d)

    def body(a_ref, w_ref, x_ref, g_ref, t_ref, loss_ref, dx_ref, dxb_ref, dg_ref, du_ref):
        @pl.when(pl.program_id(0) == 0)
        def _():
            loss_ref[...] = jnp.zeros_like(loss_ref)
            dg_ref[...] = jnp.zeros_like(dg_ref)

        xx = x_ref[...] + _dot_nn(a_ref[...], w_ref[...])
        r = _rms_scale(xx)
        n = xx * r
        gv = g_ref[...]
        err = n * gv - t_ref[...]
        per_tok = jnp.sum(err * err, axis=1, keepdims=True) * (1.0 / d)
        loss_ref[...] += 0.5 * jnp.sum(per_tok, axis=0, keepdims=True)
        dout = err * (1.0 / d)
        dg_ref[...] += jnp.sum(dout * n, axis=0, keepdims=True)
        dx = _rms_bwd(n, r, dout * gv)
        dx_ref[...] = dx
        dxb = dx.astype(BF16)
        dxb_ref[...] = dxb
        for n0 in range(0, f, COL_CHUNK):
            da = _dot_nt(dxb, w_ref[n0:n0 + COL_CHUNK, :])
            rl = _relu_from_square(a_ref[:, n0:n0 + COL_CHUNK].astype(F32))
            du_ref[:, n0:n0 + COL_CHUNK] = (da * (2.0 * rl)).astype(BF16)

    rows = lambda w: pl.BlockSpec((tm, w), lambda i: (i, 0))
    vec = pl.BlockSpec((1, d), lambda i: (0, 0))
    return pl.pallas_call(
        body,
        grid=(s // tm,),
        in_specs=[rows(f), _resident((f, d)), rows(d), vec, rows(d)],
        out_specs=[pl.BlockSpec((8, LANES), lambda i: (0, 0)), rows(d), rows(d), vec, rows(f)],
        out_shape=[jax.ShapeDtypeStruct((8, LANES), F32), jax.ShapeDtypeStruct((s, d), F32),
                   jax.ShapeDtypeStruct((s, d), BF16), jax.ShapeDtypeStruct((1, d), F32),
                   jax.ShapeDtypeStruct((s, f), BF16)],
        compiler_params=_params("arbitrary"),
        name=name,
    )(a, w2, x1, g, target)


def _mlp_bwd_act(dxb, w2, a, name, comm=None):
    s, d = dxb.shape
    f = w2.shape[0]
    tm = _row_tile(2 * d + 2 * f + 2 * f, 2 * f * d)

    def body(dx_ref, w_ref, a_ref, du_ref):
        dx = dx_ref[...]
        for n0 in range(0, f, COL_CHUNK):
            da = _dot_nt(dx, w_ref[n0:n0 + COL_CHUNK, :])
            rl = _relu_from_square(a_ref[:, n0:n0 + COL_CHUNK].astype(F32))
            du_ref[:, n0:n0 + COL_CHUNK] = (da * (2.0 * rl)).astype(BF16)

    return _call(
        body,
        grid=(s // tm,),
        in_specs=[pl.BlockSpec((tm, d), lambda i: (i, 0)),
                  _resident((f, d)),
                  pl.BlockSpec((tm, f), lambda i: (i, 0))],
        out_specs=[pl.BlockSpec((tm, f), lambda i: (i, 0))],
        out_shape=[jax.ShapeDtypeStruct((s, f), BF16)],
        operands=(dxb, w2, a), name=name, comm=comm)


def _mm_tn(pairs, name, comm=None):
    s, d = pairs[0][1].shape
    tn = 512
    tiles = [a.shape[1] // tn for a, _ in pairs]
    starts = [sum(tiles[:k]) for k in range(len(pairs))]

    def body(*refs):
        ins, outs = refs[:2 * len(pairs)], refs[2 * len(pairs):3 * len(pairs)]
        acc, late = refs[3 * len(pairs)], refs[3 * len(pairs) + 1:]
        j = pl.program_id(0)
        b_refs = [ins[1]]
        for k in range(1, len(pairs)):
            b_late = _Late(ins[2 * k + 1], late[2 * k - 2], late[2 * k - 1])
            b_late.fetch(starts[k])()
            b_refs.append(b_late.vmem)
        for k in range(len(pairs)):
            def run(a_ref=ins[2 * k], b_ref=b_refs[k], o_ref=outs[k]):
                for k0 in range(0, s, ROW_TILE):
                    part = _dot_tn(a_ref[k0:k0 + ROW_TILE, :], b_ref[k0:k0 + ROW_TILE, :])
                    if k0 == 0:
                        acc[...] = part
                    else:
                        acc[...] += part
                o_ref[...] = acc[...].astype(BF16)

            pl.when((j >= starts[k]) & (j < starts[k] + tiles[k]))(run)

    def tile_of(k):
        return lambda j: jnp.clip(j - starts[k], 0, tiles[k] - 1)

    in_specs, out_specs = [], []
    for k in range(len(pairs)):
        in_specs += [pl.BlockSpec((s, tn), lambda j, t=tile_of(k): (0, t(j))), _resident((s, d)) if k == 0 else LATE_SPEC]
        out_specs.append(pl.BlockSpec((tn, d), lambda j, t=tile_of(k): (t(j), 0)))
    late = [scratch for _, b in pairs[1:] for scratch in _late_scratch(b)]
    return _call(
        body,
        grid=(sum(tiles),),
        in_specs=in_specs,
        out_specs=out_specs,
        out_shape=[jax.ShapeDtypeStruct((a.shape[1], d), BF16) for a, _ in pairs],
        operands=tuple(t for pair in pairs for t in pair), name=name,
        scratch_shapes=[pltpu.VMEM((tn, d), F32)] + late, comm=comm)


def _mm_nn_normbwd(dact, wt, x, dres, g, name, comm=None):
    s, kdim = dact.shape
    d = wt.shape[1]
    tm = _row_tile(2 * kdim + 4 * d + 4 * d + 4 * d + 2 * d, 2 * kdim * d)

    def body(a_ref, w_ref, x_ref, r_ref, g_ref, o_ref, ob_ref, dg_ref):
        @pl.when(pl.program_id(0) == 0)
        def _():
            dg_ref[...] = jnp.zeros_like(dg_ref)

        dh = _dot_nn(a_ref[...], w_ref[...])
        xx = x_ref[...]
        r = _rms_scale(xx)
        n = xx * r
        dg_ref[...] += jnp.sum(dh * n, axis=0, keepdims=True)
        dx = r_ref[...] + _rms_bwd(n, r, dh * g_ref[...])
        o_ref[...] = dx
        ob_ref[...] = dx.astype(BF16)

    return _call(
        body,
        grid=(s // tm,),
        in_specs=[pl.BlockSpec((tm, kdim), lambda i: (i, 0)),
                  _resident((kdim, d)),
                  pl.BlockSpec((tm, d), lambda i: (i, 0)),
                  pl.BlockSpec((tm, d), lambda i: (i, 0)),
                  pl.BlockSpec((1, d), lambda i: (0, 0))],
        out_specs=[pl.BlockSpec((tm, d), lambda i: (i, 0)),
                   pl.BlockSpec((tm, d), lambda i: (i, 0)),
                   pl.BlockSpec((1, d), lambda i: (0, 0))],
        out_shape=[jax.ShapeDtypeStruct((s, d), F32), jax.ShapeDtypeStruct((s, d), BF16),
                   jax.ShapeDtypeStruct((1, d), F32)],
        operands=(dact, wt, x, dres, g), name=name, comm=comm)


def _ff_in_mix_bwd(du, wt1, x1, dres, g_mlp, wo, ya, yb, yc, gg, name, comm=None):
    s, f = du.shape
    d = wt1.shape[1]
    widths = (A_WIDTH, CONV_CH, C_WIDTH)
    tm = _row_tile(2 * f + 4 * d + 4 * d + 4 * d + 2 * d + 4 * d + 4 * d, 2 * f * d + 2 * d * d)

    def body(du_ref, w1_ref, x_ref, r_ref, g_ref, wo_ref, ya_ref, yb_ref, yc_ref, gg_ref,
             dx_ref, dxb_ref, dg_ref, da_ref, db_ref, dc_ref, dgg_ref):
        @pl.when(pl.program_id(0) == 0)
        def _():
            dg_ref[...] = jnp.zeros_like(dg_ref)
            dgg_ref[...] = jnp.zeros_like(dgg_ref)

        dh = _dot_nn(du_ref[...], w1_ref[...])
        xx = x_ref[...]
        r = _rms_scale(xx)
        n = xx * r
        dg_ref[...] += jnp.sum(dh * n, axis=0, keepdims=True)
        dx = r_ref[...] + _rms_bwd(n, r, dh * g_ref[...])
        dx_ref[...] = dx
        dxb = dx.astype(BF16)
        dxb_ref[...] = dxb

        dy = _dot_nt(dxb, wo_ref[...])
        gv = gg_ref[...]
        off = 0
        dgs = []
        for ref, out, w in zip((ya_ref, yb_ref, yc_ref), (da_ref, db_ref, dc_ref), widths):
            t = ref[...]
            r = _rms_scale(t)
            n = t * r
            dyg = dy[:, off:off + w]
            dgs.append(jnp.sum(dyg * n, axis=0, keepdims=True))
            out[...] = _rms_bwd(n, r, dyg * gv[:, off:off + w])
            off += w
        dgg_ref[...] += jnp.concatenate(dgs, axis=1)

    rows = lambda w: pl.BlockSpec((tm, w), lambda i: (i, 0))
    vec = pl.BlockSpec((1, d), lambda i: (0, 0))
    return _call(
        body,
        grid=(s // tm,),
        in_specs=[rows(f), _resident((f, d)), rows(d), rows(d), vec, _resident((d, d)),
                  rows(A_WIDTH), rows(CONV_CH), rows(C_WIDTH), vec],
        out_specs=[rows(d), rows(d), vec, rows(A_WIDTH), rows(CONV_CH), rows(C_WIDTH), vec],
        out_shape=[jax.ShapeDtypeStruct((s, d), F32), jax.ShapeDtypeStruct((s, d), BF16), jax.ShapeDtypeStruct((1, d), F32),
                   jax.ShapeDtypeStruct((s, A_WIDTH), F32), jax.ShapeDtypeStruct((s, CONV_CH), F32),
                   jax.ShapeDtypeStruct((s, C_WIDTH), F32), jax.ShapeDtypeStruct((1, d), F32)],
        operands=(du, wt1, x1, dres, g_mlp, wo, ya, yb, yc, gg), name=name, comm=comm)


CONV_CHUNK = 256
CONV_HALO = 8


def _conv_fwd(z, cw, name):
    s = z.shape[0]
    nch = s // CONV_CHUNK

    def body(gb_ref, gc_ref, xb_ref, w_ref, o_ref, us):
        us[pl.ds(0, CONV_HALO), :] = jnp.zeros((CONV_HALO, LANES), F32)
        us[pl.ds(CONV_HALO, s), :] = gc_ref[...] * xb_ref[...]
        w0, w1, w2 = w_ref[0:1, :], w_ref[1:2, :], w_ref[2:3, :]

        def chunk(c, carry):
            st = pl.multiple_of(c * CONV_CHUNK, CONV_CHUNK)
            ext = us[pl.ds(st, CONV_CHUNK + CONV_HALO), :]
            y = (w0 * ext[CONV_HALO - 2:CONV_HALO - 2 + CONV_CHUNK]
                 + w1 * ext[CONV_HALO - 1:CONV_HALO - 1 + CONV_CHUNK]
                 + w2 * ext[CONV_HALO:])
            o_ref[pl.ds(st, CONV_CHUNK), :] = gb_ref[pl.ds(st, CONV_CHUNK), :] * y
            return carry

        lax.fori_loop(0, nch, chunk, 0)

    col = lambda blk: pl.BlockSpec((s, LANES), lambda j, blk=blk: (0, blk + j))
    return pl.pallas_call(
        body,
        grid=(CONV_CH // LANES,),
        in_specs=[col(GB_BLK), col(GC_BLK), col(XB_BLK), pl.BlockSpec((3, LANES), lambda j: (0, j))],
        out_specs=pl.BlockSpec((s, LANES), lambda j: (0, j)),
        out_shape=jax.ShapeDtypeStruct((s, CONV_CH), F32),
        scratch_shapes=[pltpu.VMEM((s + CONV_HALO, LANES), F32)],
        compiler_params=_params("parallel"),
        name=name,
    )(z, z, z, cw)


def _conv_bwd(z, cw, dyb, dz, name):
    s = z.shape[0]
    nch = s // CONV_CHUNK
    ncol = CONV_CH // LANES

    def body(gb_ref, gc_ref, xb_ref, w_ref, dy_ref, dz_in, dz_ref, dw_ref, us, ds_, dgb_ref, dgc_ref, dxb_ref, sems):
        j = pl.program_id(0)

        def to_dz(staged, blk, k):
            cols = pl.ds(pl.multiple_of((blk + j) * LANES, LANES), LANES)
            return pltpu.make_async_copy(staged, dz_ref.at[:, cols], sems.at[k])

        copies = [to_dz(dgb_ref, GB_BLK, 0), to_dz(dgc_ref, GC_BLK, 1), to_dz(dxb_ref, XB_BLK, 2)]

        @pl.when(j > 0)
        def _():
            for cp in copies:
                cp.wait()

        us[pl.ds(0, CONV_HALO), :] = jnp.zeros((CONV_HALO, LANES), F32)
        us[pl.ds(CONV_HALO, s), :] = gc_ref[...] * xb_ref[...]
        ds_[pl.ds(s, CONV_HALO), :] = jnp.zeros((CONV_HALO, LANES), F32)
        ds_[pl.ds(0, s), :] = dy_ref[...] * gb_ref[...]
        w0, w1, w2 = w_ref[0:1, :], w_ref[1:2, :], w_ref[2:3, :]
        zero = jnp.zeros((1, LANES), F32)

        def chunk(c, carry):
            a0, a1, a2 = carry
            st = pl.multiple_of(c * CONV_CHUNK, CONV_CHUNK)
            rows = pl.ds(st, CONV_CHUNK)
            ext = us[pl.ds(st, CONV_CHUNK + CONV_HALO), :]
            um2 = ext[CONV_HALO - 2:CONV_HALO - 2 + CONV_CHUNK]
            um1 = ext[CONV_HALO - 1:CONV_HALO - 1 + CONV_CHUNK]
            u0 = ext[CONV_HALO:]
            dext = ds_[pl.ds(st, CONV_CHUNK + CONV_HALO), :]
            dc0 = dext[:CONV_CHUNK]
            du = w2 * dc0 + w1 * dext[1:1 + CONV_CHUNK] + w0 * dext[2:2 + CONV_CHUNK]
            yconv = w0 * um2 + w1 * um1 + w2 * u0
            dgb_ref[rows, :] = (dy_ref[rows, :] * yconv).astype(BF16)
            dgc_ref[rows, :] = (du * xb_ref[rows, :]).astype(BF16)
            dxb_ref[rows, :] = (du * gc_ref[rows, :]).astype(BF16)
            a0 = a0 + jnp.sum(dc0 * um2, axis=0, keepdims=True)
            a1 = a1 + jnp.sum(dc0 * um1, axis=0, keepdims=True)
            a2 = a2 + jnp.sum(dc0 * u0, axis=0, keepdims=True)
            return a0, a1, a2

        a0, a1, a2 = lax.fori_loop(0, nch, chunk, (zero, zero, zero))
        dw_ref[...] = jnp.concatenate([a0, a1, a2, jnp.zeros((5, LANES), F32)], axis=0)
        for cp in copies:
            cp.start()

        @pl.when(j == ncol - 1)
        def _():
            for cp in copies:
                cp.wait()

    col = lambda blk: pl.BlockSpec((s, LANES), lambda j, blk=blk: (0, blk + j))
    hbm = pl.BlockSpec(memory_space=pl.ANY)
    return pl.pallas_call(
        body,
        grid=(ncol,),
        in_specs=[col(GB_BLK), col(GC_BLK), col(XB_BLK), pl.BlockSpec((3, LANES), lambda j: (0, j)),
                  pl.BlockSpec((s, LANES), lambda j: (0, j)), hbm],
        out_specs=[hbm, pl.BlockSpec((8, LANES), lambda j: (0, j))],
        out_shape=[jax.ShapeDtypeStruct(dz.shape, dz.dtype), jax.ShapeDtypeStruct((8, CONV_CH), F32)],
        scratch_shapes=[pltpu.VMEM((s + CONV_HALO, LANES), F32), pltpu.VMEM((s + CONV_HALO, LANES), F32)]
        + [pltpu.VMEM((s, LANES), BF16)] * 3 + [pltpu.SemaphoreType.DMA((3,))],
        input_output_aliases={5: 0},
        compiler_params=_params("arbitrary"),
        name=name,
    )(z, z, z, cw, dyb, dz)


ATTN_ROWS = 512
ATTN_UNROLL = 8


def _band_rows(b, d, r):
    base = pl.multiple_of(b * (BLOCK * d), BLOCK)
    prev = jnp.maximum(base - BLOCK * d, 0)
    if d == 1:
        return pl.ds(base, BLOCK), pl.ds(pl.multiple_of(prev, BLOCK), BLOCK)
    return pl.ds(base + r, BLOCK, stride=d), pl.ds(prev + r, BLOCK, stride=d)


def _write_band_bias(bias_ref, max_dist):
    qi = lax.broadcasted_iota(jnp.int32, (BLOCK, 2 * BLOCK), 0)
    kj = lax.broadcasted_iota(jnp.int32, (BLOCK, 2 * BLOCK), 1)
    dist = BLOCK + qi - kj
    band = (dist >= 0) & (dist <= max_dist)
    bias_ref[0:BLOCK, :] = jnp.where(band, 0.0, -jnp.inf)
    bias_ref[BLOCK:2 * BLOCK, :] = jnp.where(band & (kj >= BLOCK), 0.0, -jnp.inf)


def _band_bias(bias_ref, b):
    bias = bias_ref[pl.ds(pl.multiple_of(jnp.where(b > 0, 0, BLOCK), BLOCK), BLOCK), :]
    return jnp.concatenate([bias, bias], axis=0)


def _kv_halves(pair):
    zero = jnp.zeros((1, LANES), jnp.int32)
    return zero + (pair >> 1), zero + ((pair + 1) >> 1)


def _stack_heads(t, head0, halves=None):
    top, bottom = jnp.where(head0, t, 0.0), jnp.where(head0, 0.0, t)
    if halves is not None:
        top = jnp.where(halves[0] == 1, pltpu.roll(top, HEAD_DIM, 1), top)
        bottom = jnp.where(halves[1] == 0, pltpu.roll(bottom, HEAD_DIM, 1), bottom)
    return jnp.concatenate([top, bottom], axis=0).astype(BF16)


def _unstack_heads(t, head0, halves=None):
    top, bottom = t[:BLOCK], t[BLOCK:]
    if halves is not None:
        top = jnp.where(halves[0] == 1, pltpu.roll(top, HEAD_DIM, 1), top)
        bottom = jnp.where(halves[1] == 0, pltpu.roll(bottom, HEAD_DIM, 1), bottom)
    return jnp.where(head0, top, bottom)


def _block_loops(s, patterns, unroll, one_block):
    for n, d in enumerate(patterns):
        nb = (s // BLOCK) // d
        ur = min(unroll, d)
        ub = unroll // ur
        for r0 in range(0, d, ur):
            def trip(i, carry, n=n, d=d, r0=r0, ur=ur, ub=ub):
                for u in range(ub):
                    for r in range(r0, r0 + ur):
                        one_block(i * ub + u, d, r, n == 0)
                return carry
            lax.fori_loop(0, nb // ub, trip, 0)


def _attn_fwd(z, m_init, l_init, q_blk, k_blk, v_blk, patterns, max_dist, gqa, name, comm=None):
    s = z.shape[0]
    npair = 3

    def body(q_ref, k_ref, v_ref, mi_ref, o_ref, lse0_ref, lse1_ref, bias_scr, m_scr, l_scr, *kv_scr):
        head0 = lax.broadcasted_iota(jnp.int32, (1, LANES), 1) < HEAD_DIM
        _write_band_bias(bias_scr, max_dist)
        ones = jnp.ones((2 * BLOCK, LANES), BF16)
        k_src, v_src = kv_scr if gqa else (k_ref, v_ref)
        if gqa:
            half = (lax.broadcasted_iota(jnp.int32, (1, LANES), 1) >= HEAD_DIM).astype(jnp.int32)
            swap = ((pl.program_id(0) + half) >> 1) != half

            def expand(c, carry):
                rows = pl.ds(pl.multiple_of(c * ATTN_ROWS, ATTN_ROWS), ATTN_ROWS)
                k_src[rows, :] = jnp.where(swap, pltpu.roll(k_ref[rows, :], HEAD_DIM, 1), k_ref[rows, :])
                v_src[rows, :] = jnp.where(swap, pltpu.roll(v_ref[rows, :], HEAD_DIM, 1), v_ref[rows, :])
                return carry

            lax.fori_loop(0, s // ATTN_ROWS, expand, 0)

        def one_block(b, d, r, first):
            rq, rp = _band_rows(b, d, r)
            q2 = _stack_heads(q_ref[rq, :] * SCALE, head0)
            k2 = jnp.concatenate([k_src[rp, :], k_src[rq, :]], axis=0).astype(BF16)
            v2 = jnp.concatenate([v_src[rp, :], v_src[rq, :]], axis=0).astype(BF16)
            sc = _dot_nt(q2, k2) + _band_bias(bias_scr, b)
            mb = jnp.max(sc, axis=1, keepdims=True)
            p = jnp.exp(sc - mb).astype(BF16)
            ob = _dot_nn(p, jnp.concatenate([v2, ones], axis=1))
            m_blk = _unstack_heads(jnp.broadcast_to(mb, (2 * BLOCK, LANES)), head0)
            l_blk = _unstack_heads(ob[:, LANES:], head0)
            o_blk = _unstack_heads(ob[:, :LANES], head0)
            if first and l_init == 0.0:
                m_new, l_new, o_new = m_blk, l_blk, o_blk
            else:
                if first:
                    m_old, l_old, o_old = jnp.broadcast_to(mi_ref[...], (BLOCK, LANES)), l_init, 0.0
                else:
                    m_old, l_old, o_old = m_scr[rq, :], l_scr[rq, :], o_ref[rq, :]
                m_new = jnp.maximum(m_old, m_blk)
                a_old = jnp.exp(m_old - m_new)
                a_blk = jnp.exp(m_blk - m_new)
                l_new = l_old * a_old + l_blk * a_blk
                o_new = o_old * a_old + o_blk * a_blk
            o_ref[rq, :], l_scr[rq, :], m_scr[rq, :] = o_new, l_new, m_new

        _block_loops(s, patterns, ATTN_UNROLL, one_block)

        def fin(c, carry):
            rows = pl.ds(pl.multiple_of(c * ATTN_ROWS, ATTN_ROWS), ATTN_ROWS)
            l = l_scr[rows, :]
            o_ref[rows, :] = o_ref[rows, :] / l
            lse = m_scr[rows, :] + jnp.log(l)
            swapped = pltpu.roll(lse, HEAD_DIM, 1)
            lse0_ref[rows, :] = jnp.where(head0, lse, swapped)
            lse1_ref[rows, :] = jnp.where(head0, swapped, lse)
            return carry

        lax.fori_loop(0, s // ATTN_ROWS, fin, 0)

    kv = (lambda blk: pl.BlockSpec((s, LANES), lambda j, blk=blk: (0, blk), pipeline_mode=pl.Buffered(1))) if gqa \
        else (lambda blk: pl.BlockSpec((s, LANES), lambda j, blk=blk: (0, blk + j)))
    own = pl.BlockSpec((s, LANES), lambda j: (0, j))
    return _call(
        body,
        grid=(npair,),
        in_specs=[pl.BlockSpec((s, LANES), lambda j: (0, q_blk + j)), kv(k_blk), kv(v_blk),
                  pl.BlockSpec((1, LANES), lambda j: (0, j))],
        out_specs=[own, own, own],
        out_shape=[jax.ShapeDtypeStruct((s, npair * LANES), F32)] * 3,
        operands=(z, z, z, m_init), name=name,
        scratch_shapes=[pltpu.VMEM((2 * BLOCK, 2 * BLOCK), F32)] + [pltpu.VMEM((s, LANES), F32)] * (4 if gqa else 2),
        comm=comm)


def _attn_bwd(z, do, o, lse, m_init, dz, q_blk, k_blk, v_blk, patterns, max_dist, gqa, name, comm=None):
    s = z.shape[0]
    npair = 3
    n_dz_in = 0 if dz is None else 1

    def body(q_ref, k_ref, v_ref, do_ref, o_ref, lse0_ref, lse1_ref, mi_ref, *rest):
        (dz_ref, dm_ref, dq_acc, dk_acc, dv_acc, dl0_scr, dl1_scr, bias_scr,
         dq_out, dk_out, dv_out, out_sems) = rest[n_dz_in:]
        pair = pl.program_id(0)
        head0 = lax.broadcasted_iota(jnp.int32, (1, LANES), 1) < HEAD_DIM
        halves = _kv_halves(pair) if gqa else None
        _write_band_bias(bias_scr, max_dist)

        def zero_kv():
            def f(c, carry):
                rows = pl.ds(pl.multiple_of(c * ATTN_ROWS, ATTN_ROWS), ATTN_ROWS)
                dk_acc[rows, :] = jnp.zeros((ATTN_ROWS, LANES), F32)
                dv_acc[rows, :] = jnp.zeros((ATTN_ROWS, LANES), F32)
                return carry
            lax.fori_loop(0, s // ATTN_ROWS, f, 0)

        if gqa:
            pl.when(pair == 0)(zero_kv)
        else:
            zero_kv()

        def prep(c, dm):
            rows = pl.ds(pl.multiple_of(c * ATTN_ROWS, ATTN_ROWS), ATTN_ROWS)
            dq_acc[rows, :] = jnp.zeros((ATTN_ROWS, LANES), F32)
            prod = do_ref[rows, :] * o_ref[rows, :]
            d0 = jnp.sum(jnp.where(head0, prod, 0.0), axis=1, keepdims=True)
            d1 = jnp.sum(jnp.where(head0, 0.0, prod), axis=1, keepdims=True)
            dl0_scr[rows, :] = jnp.broadcast_to(d0, (ATTN_ROWS, LANES))
            dl1_scr[rows, :] = jnp.broadcast_to(d1, (ATTN_ROWS, LANES))
            lse_own = jnp.where(head0, lse0_ref[rows, :], lse1_ref[rows, :])
            psink = jnp.exp(mi_ref[...] - lse_own)
            return dm - jnp.sum(psink * jnp.where(head0, d0, d1), axis=0, keepdims=True)

        dm_ref[...] = lax.fori_loop(0, s // ATTN_ROWS, prep, jnp.zeros((1, LANES), F32))

        def one_block(b, d, r, first):
            rq, rp = _band_rows(b, d, r)
            q2 = _stack_heads(q_ref[rq, :] * SCALE, head0, halves)
            do2 = _stack_heads(do_ref[rq, :], head0, halves)
            k2 = jnp.concatenate([k_ref[rp, :], k_ref[rq, :]], axis=0).astype(BF16)
            v2 = jnp.concatenate([v_ref[rp, :], v_ref[rq, :]], axis=0).astype(BF16)
            lse2 = jnp.concatenate([lse0_ref[rq, :], lse1_ref[rq, :]], axis=0)
            dl2 = jnp.concatenate([dl0_scr[rq, :], dl1_scr[rq, :]], axis=0)
            lse2 = jnp.concatenate([lse2, lse2], axis=1)
            dl2 = jnp.concatenate([dl2, dl2], axis=1)
            p = jnp.exp(_dot_nt(q2, k2) + _band_bias(bias_scr, b) - lse2)
            dp = _dot_nt(do2, v2)
            dsc = (p * (dp - dl2)).astype(BF16)
            dq2 = _unstack_heads(_dot_nn(dsc, k2), head0, halves)
            dk2 = _dot_tn(dsc, q2)
            dv2 = _dot_tn(p.astype(BF16), do2)
            dq_acc[rq, :] += dq2 * SCALE
            dk_acc[rp, :] += dk2[:BLOCK]
            dk_acc[rq, :] += dk2[BLOCK:]
            dv_acc[rp, :] += dv2[:BLOCK]
            dv_acc[rq, :] += dv2[BLOCK:]

        _block_loops(s, patterns, ATTN_UNROLL, one_block)

        def to_dz(staged, blk, k):
            cols = pl.ds(pl.multiple_of(blk * LANES, LANES), LANES)
            return pltpu.make_async_copy(staged, dz_ref.at[:, cols], out_sems.at[k])

        last_pair = pair == npair - 1
        q_copy = to_dz(dq_out, q_blk + pair, 0)
        kv_copies = [to_dz(dk_out, k_blk + (0 if gqa else pair), 1), to_dz(dv_out, v_blk + (0 if gqa else pair), 2)]

        @pl.when(pair > 0)
        def _():
            for cp in [q_copy] + ([] if gqa else kv_copies):
                cp.wait()

        def stage(acc, out):
            def f(c, carry):
                rows = pl.ds(pl.multiple_of(c * ATTN_ROWS, ATTN_ROWS), ATTN_ROWS)
                out[rows, :] = acc[rows, :].astype(BF16)
                return carry
            lax.fori_loop(0, s // ATTN_ROWS, f, 0)

        def stage_kv():
            stage(dk_acc, dk_out)
            stage(dv_acc, dv_out)
            for cp in kv_copies:
                cp.start()

        stage(dq_acc, dq_out)
        q_copy.start()
        if gqa:
            pl.when(last_pair)(stage_kv)
        else:
            stage_kv()

        @pl.when(last_pair)
        def _():
            for cp in [q_copy] + kv_copies:
                cp.wait()

    own = pl.BlockSpec((s, LANES), lambda j: (0, j))
    hbm = pl.BlockSpec(memory_space=pl.ANY)
    if gqa:
        kv = lambda blk: pl.BlockSpec((s, LANES), lambda j, blk=blk: (0, blk), pipeline_mode=pl.Buffered(1))
    else:
        kv = lambda blk: pl.BlockSpec((s, LANES), lambda j, blk=blk: (0, blk + j))
    in_specs = [pl.BlockSpec((s, LANES), lambda j: (0, q_blk + j)), kv(k_blk), kv(v_blk), own, own, own, own,
                pl.BlockSpec((1, LANES), lambda j: (0, j))]
    operands = (z, z, z, do, o, lse[0], lse[1], m_init)
    return _call(
        body,
        grid=(npair,),
        in_specs=in_specs + [hbm] * n_dz_in,
        out_specs=[hbm, pl.BlockSpec((1, LANES), lambda j: (0, j))],
        out_shape=[jax.ShapeDtypeStruct((s, IN_WIDTH), BF16), jax.ShapeDtypeStruct((1, npair * LANES), F32)],
        operands=operands + (() if dz is None else (dz,)), name=name,
        scratch_shapes=[pltpu.VMEM((s, LANES), F32)] * 5 + [pltpu.VMEM((2 * BLOCK, 2 * BLOCK), F32)]
        + [pltpu.VMEM((s, LANES), BF16)] * 3 + [pltpu.SemaphoreType.DMA((3,))],
        comm=comm, aliases={} if dz is None else {len(in_specs): 0})


def _adamw_math(w, g, m, v):
    m = ADAM_B1 * m + (1.0 - ADAM_B1) * g
    v = ADAM_B2 * v + (1.0 - ADAM_B2) * (g * g)
    m_hat = m / (1.0 - ADAM_B1 ** ADAM_STEP)
    v_hat = v / (1.0 - ADAM_B2 ** ADAM_STEP)
    delta = -ADAM_LR * (m_hat / (jnp.sqrt(v_hat) + ADAM_EPS) + ADAM_WD * w)
    return delta, m, v


def _adamw(w, g, m, v, name):
    rows, cols = w.shape
    tr = min(rows, 256)

    def body(w_ref, g_ref, m_ref, v_ref, d_ref, nm_ref, nv_ref):
        d_ref[...], nm_ref[...], nv_ref[...] = _adamw_math(w_ref[...], g_ref[...], m_ref[...], v_ref[...])

    spec = pl.BlockSpec((tr, cols), lambda i: (i, 0))
    return pl.pallas_call(
        body,
        grid=(rows // tr,),
        in_specs=[spec] * 4,
        out_specs=[spec] * 3,
        out_shape=[jax.ShapeDtypeStruct((rows, cols), F32)] * 3,
        compiler_params=_params("parallel"),
        name=name,
    )(w, g, m, v)


def _sum_adamw(parts, w, m, v, pos, transpose, name):
    assert len(parts) == DEPTH == 2
    (p0, r0), (p1, r1) = parts
    _, rows, cols = p0.shape
    tr = 256 if rows % 256 == 0 else rows
    nt = rows // tr

    def body(pos_ref, p0_ref, r0_ref, p1_ref, r1_ref, w_ref, m_ref, v_ref, g_ref, d_ref, nm_ref, nv_ref):
        def run(p_ref, r_ref):
            g = ((p_ref[...].astype(F32) + r_ref[0].astype(F32)) + r_ref[1].astype(F32)) + r_ref[2].astype(F32)
            if transpose:
                g = g.T
            g_ref[...] = g
            d_ref[...], nm_ref[...], nv_ref[...] = _adamw_math(w_ref[...], g, m_ref[...], v_ref[...])

        layer0 = pl.program_id(0) < nt
        pl.when(layer0)(lambda: run(p0_ref, r0_ref))
        pl.when(jnp.logical_not(layer0))(lambda: run(p1_ref, r1_ref))

    def tile0(i):
        return jnp.minimum(i, nt - 1)

    def tile1(i):
        return jnp.maximum(i - nt, 0)

    if transpose:
        w_spec = pl.BlockSpec((None, cols, tr), lambda i, q: (i // nt, 0, i % nt))
    else:
        w_spec = pl.BlockSpec((None, tr, cols), lambda i, q: (i // nt, i % nt, 0))
    return pl.pallas_call(
        body,
        grid_spec=pltpu.PrefetchScalarGridSpec(
            num_scalar_prefetch=1,
            grid=(DEPTH * nt,),
            in_specs=[pl.BlockSpec((None, tr, cols), lambda i, q: (q[0], tile0(i), 0)),
                      pl.BlockSpec((3, tr, cols), lambda i, q: (0, tile0(i), 0)),
                      pl.BlockSpec((None, tr, cols), lambda i, q: (q[0], tile1(i), 0)),
                      pl.BlockSpec((3, tr, cols), lambda i, q: (0, tile1(i), 0)),
                      w_spec, w_spec, w_spec],
            out_specs=[w_spec] * 4,
        ),
        out_shape=[jax.ShapeDtypeStruct(w.shape, F32)] * 4,
        compiler_params=_params("arbitrary"),
        name=name,
    )(pos, p0, r0, p1, r1, w, m, v)


def _small_sum_adamw(gathered, params, name):
    _, rows, cols = gathered.shape
    n = len(params)

    def body(ga_ref, *refs):
        ins, outs, (g_scr,) = refs[:3 * n], refs[3 * n:7 * n + 2], refs[7 * n + 2:]
        g = ga_ref[0]
        for i in range(1, N_DEV):
            g = g + ga_ref[i]
        g_scr[...] = g
        for k, (row0, w, _, _) in enumerate(params):
            w_ref, m_ref, v_ref = ins[3 * k:3 * k + 3]
            gk = g_scr[row0:row0 + w.shape[0], :]
            outs[4 * k][...] = gk
            outs[4 * k + 1][...], outs[4 * k + 2][...], outs[4 * k + 3][...] = _adamw_math(
                w_ref[...], gk, m_ref[...], v_ref[...])
        outs[4 * n][...] = g_scr[CONV_ROW:CONV_ROW + 8, :]
        outs[4 * n + 1][...] = g_scr[LOSS_ROW:LOSS_ROW + 1, :]

    out_shape = []
    for _, w, _, _ in params:
        out_shape += [jax.ShapeDtypeStruct(w.shape, F32)] * 4
    out_shape += [jax.ShapeDtypeStruct((8, cols), F32), jax.ShapeDtypeStruct((1, cols), F32)]
    res = pl.pallas_call(
        body,
        out_shape=out_shape,
        scratch_shapes=[pltpu.VMEM((rows, cols), F32)],
        name=name,
    )(gathered, *[t for _, w, m, v in params for t in (w, m, v)])
    return [res[4 * k:4 * k + 4] for k in range(n)], res[4 * n], res[4 * n + 1]


def _pair_sums(g4s, r1s, pos, name):
    n = len(g4s)

    def body(pos_ref, *refs):
        for g_ref, r_ref, o_ref in zip(refs[:n], refs[n:2 * n], refs[2 * n:]):
            o_ref[...] = (g_ref[...].astype(F32) + r_ref[...].astype(F32)).astype(BF16)

    block = lambda t: pl.BlockSpec((None,) + t.shape[1:], lambda i, p: (i, 0, 0))
    return pl.pallas_call(
        body,
        grid_spec=pltpu.PrefetchScalarGridSpec(
            num_scalar_prefetch=1,
            grid=(4,),
            in_specs=[pl.BlockSpec((None, None) + g.shape[2:], lambda i, p: (i, p[1], 0, 0)) for g in g4s]
            + [block(r) for r in r1s],
            out_specs=[block(r) for r in r1s],
        ),
        out_shape=[jax.ShapeDtypeStruct(r.shape, BF16) for r in r1s],
        compiler_params=_params("parallel"),
        name=name,
    )(pos, *g4s, *r1s)


GATHER_ID, CHIP_ID, SIBLING_ID = 0, 1, 2


def _place():
    return lax.axis_index("x"), lax.axis_index("y"), lax.axis_index("c")


def _sibling():
    x, y, c = _place()
    return (x, y, 1 - c)


def _same_core_of_other_chips():
    x, y, c = _place()
    return [(1 - x, y, c), (x, 1 - y, c), (1 - x, 1 - y, c)]


def _gather_comm(shards):
    na = len(shards)
    stacks, index = zip(*shards)

    def plan(ins, outs, sems):
        send_sems, recv_sems, local_sems = sems
        x, y, c = _place()
        me, sibling = (x, y, c), (x, y, 1 - c)
        chips = [(1 - x, y), (x, 1 - y), (1 - x, 1 - y)]
        shard = [ins[a].at[index[a]] for a in range(na)]

        def rows(a, px, py, pc):
            m = shard[a].shape[0]
            return outs[a].at[pl.ds((4 * px + 2 * py + pc) * m, m), :]

        def copy(a, k, block, to, src=None):
            return pltpu.make_async_remote_copy(
                src_ref=rows(a, *block) if src is None else src, dst_ref=rows(a, *block),
                send_sem=send_sems.at[a, k], recv_sem=recv_sems.at[a, k], device_id=to, device_id_type=MESH)

        mine = [pltpu.make_async_copy(shard[a], rows(a, *me), local_sems.at[a]) for a in range(na)]
        first = []
        for a in range(na):
            first.append(copy(a, 0, me, sibling, src=shard[a]))
            first += [copy(a, 1 + j, me, (*chip, c), src=shard[a]) for j, chip in enumerate(chips)]
        return me, sibling, chips, c, copy, mine, first

    def start(ins, outs, sems):
        *_, mine, first = plan(ins, outs, sems)
        for cp in mine + first:
            cp.start()

    def finish(ins, outs, sems):
        me, sibling, chips, c, copy, mine, first = plan(ins, outs, sems)
        passed = []
        for j, chip in enumerate(chips):
            for a in range(na):
                copy(a, 1 + j, (*chip, c), me).wait_recv()
                cp = copy(a, 4 + j, (*chip, c), sibling)
                cp.start()
                passed.append(cp)
        for a in range(na):
            copy(a, 0, sibling, me).wait_recv()
            for j, chip in enumerate(chips):
                copy(a, 4 + j, (*chip, 1 - c), me).wait_recv()
        for cp in first + passed:
            cp.wait_send()
        for cp in mine:
            cp.wait()

    return _Comm(tuple(stacks),
                 tuple(jax.ShapeDtypeStruct((N_DEV * t.shape[1], t.shape[2]), t.dtype) for t in stacks),
                 (pltpu.SemaphoreType.DMA((na, 7)), pltpu.SemaphoreType.DMA((na, 7)), pltpu.SemaphoreType.DMA((na,))),
                 start, finish, lambda: [_sibling()] + _same_core_of_other_chips(), GATHER_ID)


def _exchange_comm(arrays, out_shape, n_copies, copies_of, peers, collective_id):
    na = len(arrays)

    def every(ins, outs, sems):
        send_sems, recv_sems = sems
        return [cp for a in range(na) for cp in copies_of(ins, outs, a, send_sems, recv_sems)]

    def start(ins, outs, sems):
        for cp in every(ins, outs, sems):
            cp.start()

    def finish(ins, outs, sems):
        for cp in every(ins, outs, sems):
            cp.wait()

    return _Comm(tuple(arrays), tuple(out_shape),
                 (pltpu.SemaphoreType.DMA((na, n_copies)), pltpu.SemaphoreType.DMA((na, n_copies))), start, finish,
                 peers, collective_id)


def _sibling_comm(grads):
    def copies_of(ins, outs, a, send_sems, recv_sems):
        x, y, c = _place()
        return [pltpu.make_async_remote_copy(
            src_ref=ins[a].at[chip, 1 - c], dst_ref=outs[a].at[chip],
            send_sem=send_sems.at[a, chip], recv_sem=recv_sems.at[a, chip],
            device_id=(x, y, 1 - c), device_id_type=MESH) for chip in range(4)]

    return _exchange_comm(grads, [jax.ShapeDtypeStruct((4,) + t.shape[2:], t.dtype) for t in grads], 4, copies_of,
                          lambda: [_sibling()], SIBLING_ID)


def _chip_comm(partials):
    def copies_of(ins, outs, a, send_sems, recv_sems):
        x, y, c = _place()
        chips = [(1 - x, y), (x, 1 - y), (1 - x, 1 - y)]
        return [pltpu.make_async_remote_copy(
            src_ref=ins[a].at[2 * cx + cy], dst_ref=outs[a].at[k],
            send_sem=send_sems.at[a, k], recv_sem=recv_sems.at[a, k],
            device_id=(cx, cy, c), device_id_type=MESH) for k, (cx, cy) in enumerate(chips)]

    return _exchange_comm(partials, [jax.ShapeDtypeStruct((3,) + t.shape[1:], t.dtype) for t in partials], 3, copies_of,
                          _same_core_of_other_chips, CHIP_ID)


def _pad_rows(t, rows):
    return jnp.pad(t, ((0, rows - t.shape[0]), (0, D_MODEL - t.shape[1])))


MIX_ROW, GROUP_ROW, MLP_ROW, FINAL_ROW, CONV_ROW, SINK_ROW = 0, 8, 16, 24, 32, 40
LOSS_ROW = FINAL_ROW + 1


def _pack_small(g_mix, g_group, g_mlp, g_final, conv, sinks, loss):
    final_and_loss = jnp.concatenate([g_final.reshape(1, D_MODEL), _pad_rows(loss, 1)], axis=0)
    return jnp.concatenate([
        _pad_rows(g_mix, 8), _pad_rows(g_group, 8), _pad_rows(g_mlp, 8), _pad_rows(final_and_loss, 8),
        _pad_rows(conv.reshape(DEPTH * 3, CONV_CH), 8), _pad_rows(sinks.reshape(1, DEPTH * 6), 8)], axis=0)


def kernel(x, w_in, conv_w, sinks, g_mix, g_group, w_o, g_mlp, w_ff_in, w_ff_out, g_final, loss_target, m_w_in, m_conv_w, m_sinks, m_g_mix, m_g_group, m_w_o, m_g_mlp, m_w_ff_in, m_w_ff_out, m_g_final, v_w_in, v_conv_w, v_sinks, v_g_mix, v_g_group, v_w_o, v_g_mlp, v_w_ff_in, v_w_ff_out, v_g_final):
    ax, ay, ac = _place()
    chip = 2 * ax + ay
    dev = 4 * ax + 2 * ay + ac
    pos = jnp.stack([chip, ac]).astype(jnp.int32)

    x0 = x.reshape(SEQ, D_MODEL)
    target = loss_target.reshape(SEQ, D_MODEL)

    stacks = [jnp.swapaxes(w_in, 1, 2).astype(BF16), w_o.astype(BF16),
              jnp.swapaxes(w_ff_in, 1, 2).astype(BF16), w_ff_out.astype(BF16)]
    shards = {(l, kind): (stack, l) for kind, stack in enumerate(stacks) for l in range(DEPTH)}
    conv_tile = jnp.pad(conv_w.reshape(DEPTH * 3, CONV_CH // N_DEV), ((0, 2), (0, LANES - CONV_CH // N_DEV)))
    wt_in0, conv_all = _comm_only(_gather_comm([shards[0, 0], (conv_tile[None], 0)]), "gather_first")
    conv_full = conv_all.reshape(N_DEV, 8, LANES)[:, :DEPTH * 3, :CONV_CH // N_DEV]
    conv_full = conv_full.transpose(1, 0, 2).reshape(DEPTH, 3, CONV_CH)

    dx, parts, small = _step(x0, target, shards, wt_in0, conv_full, sinks, g_mix, g_group, g_mlp, g_final, pos)
    return _finish(dx, parts, small, pos, dev, w_in, conv_w, sinks, g_mix, g_group, w_o, g_mlp, w_ff_in, w_ff_out, g_final, m_w_in, m_conv_w, m_sinks, m_g_mix, m_g_group, m_w_o, m_g_mlp, m_w_ff_in, m_w_ff_out, m_g_final, v_w_in, v_conv_w, v_sinks, v_g_mix, v_g_group, v_w_o, v_g_mlp, v_w_ff_in, v_w_ff_out, v_g_final)


FWD_CARRY = {(0, "in_proj"): ((1, 0),), (0, "window"): ((0, 1),), (0, "dilated"): ((0, 2),),
             (0, "mix_ff_in"): ((0, 3),), (0, "ff_out_in_proj"): ((1, 1), (1, 3)),
             (1, "dilated"): ((1, 2),)}


def _step(x0, target, shards, wt_in0, conv_full, sinks, g_mix, g_group, g_mlp, g_final, pos):
    sink_lanes = jnp.repeat(sinks.reshape(DEPTH, 6), HEAD_DIM, axis=1)
    no_sink = jnp.full((1, A_WIDTH), NEG_BIG, F32)
    full = {(0, 0): wt_in0}

    def gather(stage, l):
        keys = FWD_CARRY.get((l, stage), ())
        return keys, (_gather_comm([shards[k] for k in keys]) if keys else None)

    def landed(keys, got):
        full.update(zip(keys, got))

    saved = []
    xc = x0
    keys, comm = gather("in_proj", 0)
    (z, h), got = _norm_mm(xc, g_mix[0:1], full[0, 0], "in_proj_0", comm)
    landed(keys, got)
    for l in range(DEPTH):
        sink_l = sink_lanes[l:l + 1]
        keys, comm = gather("window", l)
        (yc, *lse_c), got = _attn_fwd(z, sink_l, 1.0, QC_BLK, KC_BLK, VC_BLK, (1,), C_MAX_DIST, True,
                                     f"window_attn_{l}", comm)
        landed(keys, got)
        yb = _conv_fwd(z, conv_full[l], f"conv_{l}")
        keys, comm = gather("dilated", l)
        (ya, *lse_a), got = _attn_fwd(z, no_sink, 0.0, QA_BLK, KA_BLK, VA_BLK, DILATED_PATTERNS, A_MAX_DIST, False,
                                     f"dilated_attn_{l}", comm)
        landed(keys, got)
        keys, comm = gather("mix_ff_in", l)
        (y, x1, a, h2), got = _mix_ff_in(ya, yb, yc, g_group[l:l + 1], full[l, 1], xc, g_mlp[l:l + 1], full[l, 2],
                                         f"mix_ff_in_{l}", comm)
        landed(keys, got)
        saved.append((xc, z, h, ya, lse_a, yb, yc, lse_c, sink_l, y, x1, a, h2))
        if l + 1 < DEPTH:
            keys, comm = gather("ff_out_in_proj", l)
            (xc, z, h), got = _ff_out_in_proj(a, full[l, 3], x1, g_mix[l + 1:l + 2], full[l + 1, 0],
                                              f"ff_out_{l}_in_proj_{l + 1}", comm)
            landed(keys, got)

    loss_slab, dx, dxb, dg_final, du = _mm_res_loss(a, full[DEPTH - 1, 3], x1, g_final.reshape(1, D_MODEL), target,
                                                    f"ff_out_{DEPTH - 1}_loss")

    def by_owner(t):
        return t.reshape(4, 2, t.shape[0] // N_DEV, D_MODEL)

    def pair(l, kinds, grads, received):
        sums = _pair_sums(grads, received, pos, f"grad_pair_sums_{l}_{kinds[0]}{kinds[1]}")
        partial.update({(l, kind): t for kind, t in zip(kinds, sums)})

    partial, r2 = {}, {}
    dg_mix, dg_group, dg_mlp, dconv, dsinks = [None] * DEPTH, [None] * DEPTH, [None] * DEPTH, [None] * DEPTH, [None] * DEPTH
    for l in reversed(range(DEPTH)):
        xin, z, h, ya, lse_a, yb, yc, lse_c, sink_l, y, x1, a, h2 = saved[l]
        if l + 1 < DEPTH:
            late = [(l + 1, 1), (l + 1, 0)]
            (du,), got = _mlp_bwd_act(dxb, full[l, 3], a, f"ff_out_bwd_{l}", _chip_comm([partial[k] for k in late]))
            r2.update(zip(late, got))
        (g3, g2), _ = _mm_tn([(a, dxb), (du, h2)], f"grad_w_ff_{l}")
        g3, g2 = by_owner(g3), by_owner(g2)
        (dx1, dx1b, dg_mlp[l], dya, dyb, dyc, dg_group[l]), got = _ff_in_mix_bwd(
            du, full[l, 2], x1, dx, g_mlp[l:l + 1], full[l, 1], ya, yb, yc, g_group[l:l + 1],
            f"ff_in_mix_bwd_{l}", _sibling_comm([g3, g2]))
        pair(l, (3, 2), [g3, g2], got)
        early = [(l, 3), (l, 2)]
        (dz, _), got = _attn_bwd(z, dya, ya, lse_a, no_sink, None, QA_BLK, KA_BLK, VA_BLK, DILATED_PATTERNS,
                                 A_MAX_DIST, False, f"dilated_attn_bwd_{l}", _chip_comm([partial[k] for k in early]))
        r2.update(zip(early, got))
        dz, dcw = _conv_bwd(z, conv_full[l], dyb, dz, f"conv_bwd_{l}")
        (dz, dsink), _ = _attn_bwd(z, dyc, yc, lse_c, sink_l, dz, QC_BLK, KC_BLK, VC_BLK, (1,), C_MAX_DIST,
                                   True, f"window_attn_bwd_{l}")
        (g1, g0), _ = _mm_tn([(y, dx1b), (dz, h)], f"grad_w_o_in_{l}")
        g1, g0 = by_owner(g1), by_owner(g0)
        if l > 0:
            (dx, dxb, dg_mix[l]), got = _mm_nn_normbwd(dz, full[l, 0], xin, dx1, g_mix[l:l + 1], f"in_proj_bwd_{l}",
                                                      _sibling_comm([g1, g0]))
            pair(l, (1, 0), [g1, g0], got)
        else:
            got = _comm_only(_sibling_comm([g1, g0]), "grad_sibling_exchange_last")
            pair(l, (1, 0), [g1, g0], got)
            (dx, dxb, dg_mix[l]), got = _mm_nn_normbwd(dz, full[l, 0], xin, dx1, g_mix[l:l + 1], f"in_proj_bwd_{l}",
                                                      _chip_comm([partial[l, 1], partial[l, 0]]))
            r2[l, 1], r2[l, 0] = got
        dconv[l] = dcw[:3]
        dsinks[l] = dsink[0, ::HEAD_DIM]
    parts = {key: (partial[key], r2[key]) for key in partial}
    small = _pack_small(jnp.concatenate(dg_mix), jnp.concatenate(dg_group), jnp.concatenate(dg_mlp),
                        dg_final, jnp.stack(dconv), jnp.stack(dsinks), loss_slab[0:1])
    return dx, parts, small


def _finish(dx, parts, small, pos, dev, w_in, conv_w, sinks, g_mix, g_group, w_o, g_mlp, w_ff_in, w_ff_out, g_final, m_w_in, m_conv_w, m_sinks, m_g_mix, m_g_group, m_w_o, m_g_mlp, m_w_ff_in, m_w_ff_out, m_g_final, v_w_in, v_conv_w, v_sinks, v_g_mix, v_g_group, v_w_o, v_g_mlp, v_w_ff_in, v_w_ff_out, v_g_final):
    grad_x = dx.reshape(1, SEQ, D_MODEL)

    (small_all,) = _comm_only(_gather_comm([(small[None], 0)]), "gather_small_grads")
    row = lambda t: t.reshape(1, D_MODEL)
    sink_row = lambda t: _pad_rows(t.reshape(1, DEPTH * 6), 1)
    params = [(MIX_ROW, g_mix, m_g_mix, v_g_mix), (GROUP_ROW, g_group, m_g_group, v_g_group),
              (MLP_ROW, g_mlp, m_g_mlp, v_g_mlp), (FINAL_ROW, row(g_final), row(m_g_final), row(v_g_final)),
              (SINK_ROW, sink_row(sinks), sink_row(m_sinks), sink_row(v_sinks))]
    updated, conv_rows, loss_row = _small_sum_adamw(small_all.reshape(N_DEV, SMALL_ROWS, D_MODEL), params, "small_adamw")
    loss = loss_row[0, 0]
    (grad_g_mix, delta_g_mix, new_m_g_mix, new_v_g_mix), (grad_g_group, delta_g_group, new_m_g_group, new_v_g_group), \
        (grad_g_mlp, delta_g_mlp, new_m_g_mlp, new_v_g_mlp), final4, sinks4 = updated
    grad_g_final, delta_g_final, new_m_g_final, new_v_g_final = [t.reshape(D_MODEL) for t in final4]
    grad_sinks, delta_sinks, new_m_sinks, new_v_sinks = [t[0, :DEPTH * 6].reshape(DEPTH, 2, 3) for t in sinks4]
    conv_grad_full = conv_rows[:DEPTH * 3, :CONV_CH].reshape(DEPTH, 3, CONV_CH)
    cs = CONV_CH // N_DEV
    grad_conv_w = lax.dynamic_slice_in_dim(conv_grad_full, dev * cs, cs, axis=2)

    def tile_of(t):
        return jnp.pad(t.reshape(1, DEPTH * 3 * cs), ((0, 7), (0, 256 - DEPTH * 3 * cs)))

    cd, cm, cv = _adamw(tile_of(conv_w), tile_of(grad_conv_w), tile_of(m_conv_w), tile_of(v_conv_w), "conv_adamw")
    untile = lambda t: t[0, :DEPTH * 3 * cs].reshape(DEPTH, 3, cs)
    delta_conv_w, new_m_conv_w, new_v_conv_w = untile(cd), untile(cm), untile(cv)

    def big(kind, w, m, v, transpose, name):
        return _sum_adamw([parts[l, kind] for l in range(DEPTH)], w, m, v, pos, transpose, name)

    swap = lambda t: jnp.swapaxes(t, 1, 2)
    grad_w_in, delta_w_in, new_m_w_in, new_v_w_in = [
        swap(t) for t in big(0, swap(w_in), swap(m_w_in), swap(v_w_in), False, "adamw_w_in")]
    grad_w_o, delta_w_o, new_m_w_o, new_v_w_o = big(1, w_o, m_w_o, v_w_o, False, "adamw_w_o")
    grad_w_ff_in, delta_w_ff_in, new_m_w_ff_in, new_v_w_ff_in = big(2, w_ff_in, m_w_ff_in, v_w_ff_in, True, "adamw_w_ff_in")
    grad_w_ff_out, delta_w_ff_out, new_m_w_ff_out, new_v_w_ff_out = big(3, w_ff_out, m_w_ff_out, v_w_ff_out, False,
                                                                         "adamw_w_ff_out")

    return (loss, grad_x, grad_w_in, grad_conv_w, grad_sinks, grad_g_mix, grad_g_group, grad_w_o, grad_g_mlp,
            grad_w_ff_in, grad_w_ff_out, grad_g_final,
            delta_w_in, delta_conv_w, delta_sinks, delta_g_mix, delta_g_group, delta_w_o, delta_g_mlp,
            delta_w_ff_in, delta_w_ff_out, delta_g_final,
            new_m_w_in, new_m_conv_w, new_m_sinks, new_m_g_mix, new_m_g_group, new_m_w_o, new_m_g_mlp,
            new_m_w_ff_in, new_m_w_ff_out, new_m_g_final,
            new_v_w_in, new_v_conv_w, new_v_sinks, new_v_g_mix, new_v_g_group, new_v_w_o, new_v_g_mlp,
            new_v_w_ff_in, new_v_w_ff_out, new_v_g_final)
```

```python
from typing import Callable, NamedTuple

import jax
import jax.numpy as jnp
from jax import lax
from jax.experimental import pallas as pl
from jax.experimental.pallas import tpu as pltpu

F32 = jnp.float32
BF16 = jnp.bfloat16
MESH = pl.DeviceIdType.MESH

N_DEV = 8
SEQ = 4096
D_MODEL = 1024
DEPTH = 2
HEAD_DIM = 64
LANES = 128
A_WIDTH = 384
CONV_CH = 256
C_WIDTH = 384
IN_WIDTH = 2560
BLOCK = 128
DILATED_PATTERNS = (1, 4, 16)
A_MAX_DIST = 128
C_MAX_DIST = 127
EPS = 1e-6
SCALE = HEAD_DIM ** -0.5
NEG_BIG = -1e30
F32_TINY = 1.1754944e-38

QA_BLK, KA_BLK, VA_BLK = 0, 3, 6
GB_BLK, GC_BLK, XB_BLK = 9, 11, 13
QC_BLK, KC_BLK, VC_BLK = 15, 18, 19

ADAM_LR = 0.001
ADAM_B1 = 0.9
ADAM_B2 = 0.999
ADAM_EPS = 1e-08
ADAM_WD = 0.01
ADAM_STEP = 10

VMEM_LIMIT = 56 * 1024 * 1024
TILE_BUDGET = 46 * 1024 * 1024
ROW_TILE = 512
COL_CHUNK = 512
SMALL_ROWS = 48


def _dot_nn(a, b):
    return lax.dot_general(a, b, (((1,), (0,)), ((), ())), preferred_element_type=F32)


def _dot_nt(a, b):
    return lax.dot_general(a, b, (((1,), (1,)), ((), ())), preferred_element_type=F32)


def _dot_tn(a, b):
    return lax.dot_general(a, b, (((0,), (0,)), ((), ())), preferred_element_type=F32)


def _params(*sem, collective_id=None):
    return pltpu.CompilerParams(dimension_semantics=sem, vmem_limit_bytes=VMEM_LIMIT, collective_id=collective_id)


def _resident(shape):
    return pl.BlockSpec(shape, lambda i: (0,) * len(shape), pipeline_mode=pl.Buffered(1))


class _Late(NamedTuple):
    hbm: object
    vmem: object
    sem: object

    def fetch(self, needed_at_step):
        copy = pltpu.make_async_copy(self.hbm, self.vmem, self.sem)
        pl.when(pl.program_id(0) == 0)(copy.start)
        return lambda: pl.when(pl.program_id(0) == needed_at_step)(copy.wait)


LATE_SPEC = pl.BlockSpec(memory_space=pl.ANY)


def _late_scratch(t):
    return [pltpu.VMEM(t.shape, t.dtype), pltpu.SemaphoreType.DMA(())]


def _row_tile(row_bytes, resident_bytes):
    for tm in (ROW_TILE, ROW_TILE // 2):
        if 2 * tm * row_bytes + resident_bytes <= TILE_BUDGET:
            return tm
    return ROW_TILE // 4


def _rms_scale(t):
    return lax.rsqrt(jnp.mean(t * t, axis=-1, keepdims=True) + EPS)


def _rms_bwd(n, r, dn):
    return r * (dn - n * jnp.mean(dn * n, axis=-1, keepdims=True))


class _Comm(NamedTuple):
    arrays: tuple
    out_shape: tuple
    sems: tuple
    start: Callable
    finish: Callable
    peers: Callable
    collective_id: int


def _handshake(comm):
    barrier = pltpu.get_barrier_semaphore()
    peers = comm.peers()
    for peer in peers:
        pl.semaphore_signal(barrier, inc=1, device_id=peer, device_id_type=MESH)
    pl.semaphore_wait(barrier, len(peers))


def _call(body, grid, in_specs, out_specs, out_shape, operands, name, scratch_shapes=(), comm=None, aliases=None):
    n_in, n_out, n_scr = len(in_specs), len(out_shape), len(scratch_shapes)
    aliases = dict(aliases or {})
    if comm is None:
        res = pl.pallas_call(body, grid=grid, in_specs=list(in_specs), out_specs=list(out_specs),
                             out_shape=list(out_shape), scratch_shapes=list(scratch_shapes),
                             input_output_aliases=aliases,
                             compiler_params=_params("arbitrary"), name=name)(*operands)
        return list(res), []
    c_in, c_out = len(comm.arrays), len(comm.out_shape)
    hbm = pl.BlockSpec(memory_space=pl.ANY)
    last = grid[0] - 1

    def carried(*refs):
        ins, cins = refs[:n_in], refs[n_in:n_in + c_in]
        o0 = n_in + c_in
        outs, couts = refs[o0:o0 + n_out], refs[o0 + n_out:o0 + n_out + c_out]
        s0 = o0 + n_out + c_out
        scr, sems = refs[s0:s0 + n_scr], refs[s0 + n_scr:]
        @pl.when(pl.program_id(0) == 0)
        def _():
            _handshake(comm)
            comm.start(cins, couts, sems)

        body(*ins, *outs, *scr)
        pl.when(pl.program_id(0) == last)(lambda: comm.finish(cins, couts, sems))

    res = pl.pallas_call(carried, grid=grid, in_specs=list(in_specs) + [hbm] * c_in,
                         out_specs=list(out_specs) + [hbm] * c_out, out_shape=list(out_shape) + list(comm.out_shape),
                         scratch_shapes=list(scratch_shapes) + list(comm.sems), input_output_aliases=aliases,
                         compiler_params=_params("arbitrary", collective_id=comm.collective_id),
                         name=name)(*operands, *comm.arrays)
    return list(res[:n_out]), list(res[n_out:])


def _comm_only(comm, name):
    hbm = pl.BlockSpec(memory_space=pl.ANY)
    c_in, c_out = len(comm.arrays), len(comm.out_shape)

    def body(*refs):
        ins, outs, sems = refs[:c_in], refs[c_in:c_in + c_out], refs[c_in + c_out:]
        _handshake(comm)
        comm.start(ins, outs, sems)
        comm.finish(ins, outs, sems)

    return pl.pallas_call(body, in_specs=[hbm] * c_in, out_specs=[hbm] * c_out, out_shape=list(comm.out_shape),
                          scratch_shapes=list(comm.sems),
                          compiler_params=pltpu.CompilerParams(collective_id=comm.collective_id),
                          name=name)(*comm.arrays)


def _norm_mm(x, g, wt, name, comm=None):
    s, d = x.shape
    n = wt.shape[0]
    tm = _row_tile(4 * d + 4 * n + 2 * d, 2 * n * d)

    def body(x_ref, g_ref, w_ref, o_ref, h_ref):
        xx = x_ref[...]
        h = ((xx * _rms_scale(xx)) * g_ref[...]).astype(BF16)
        h_ref[...] = h
        for n0 in range(0, n, COL_CHUNK):
            o_ref[:, n0:n0 + COL_CHUNK] = _dot_nt(h, w_ref[n0:n0 + COL_CHUNK, :])

    return _call(
        body,
        grid=(s // tm,),
        in_specs=[pl.BlockSpec((tm, d), lambda i: (i, 0)),
                  pl.BlockSpec((1, d), lambda i: (0, 0)),
                  _resident((n, d))],
        out_specs=[pl.BlockSpec((tm, n), lambda i: (i, 0)),
                   pl.BlockSpec((tm, d), lambda i: (i, 0))],
        out_shape=[jax.ShapeDtypeStruct((s, n), F32), jax.ShapeDtypeStruct((s, d), BF16)],
        operands=(x, g, wt), name=name, comm=comm)


def _ff_out_in_proj(a, w2, x1, g, wt, name, comm=None):
    s, f = a.shape
    d = w2.shape[1]
    n = wt.shape[0]
    tm = _row_tile(2 * f + 4 * d + 4 * d + 4 * n + 2 * d, 2 * f * d + 2 * n * d)

    def body(a_ref, w2_ref, x_ref, g_ref, w_ref, x2_ref, z_ref, h_ref):
        x2 = x_ref[...] + _dot_nn(a_ref[...], w2_ref[...])
        x2_ref[...] = x2
        h = ((x2 * _rms_scale(x2)) * g_ref[...]).astype(BF16)
        h_ref[...] = h
        for n0 in range(0, n, COL_CHUNK):
            z_ref[:, n0:n0 + COL_CHUNK] = _dot_nt(h, w_ref[n0:n0 + COL_CHUNK, :])

    rows = lambda w: pl.BlockSpec((tm, w), lambda i: (i, 0))
    return _call(
        body,
        grid=(s // tm,),
        in_specs=[rows(f), _resident((f, d)), rows(d), pl.BlockSpec((1, d), lambda i: (0, 0)), _resident((n, d))],
        out_specs=[rows(d), rows(n), rows(d)],
        out_shape=[jax.ShapeDtypeStruct((s, d), F32), jax.ShapeDtypeStruct((s, n), F32),
                   jax.ShapeDtypeStruct((s, d), BF16)],
        operands=(a, w2, x1, g, wt), name=name, comm=comm)


def _mix_ff_in(ya, yb, yc, gg, wo, x0, g_mlp, wt1, name, comm=None):
    s = ya.shape[0]
    d = wo.shape[1]
    f = wt1.shape[0]
    tm = _row_tile(4 * d + 4 * d + 2 * d + 4 * d + 2 * d + 2 * f, 2 * d * d + 2 * f * d)

    def body(ya_ref, yb_ref, yc_ref, gg_ref, wo_ref, x_ref, g_ref, w1_ref, y_ref, x1_ref, a_ref, h_ref):
        parts = []
        for ref in (ya_ref, yb_ref, yc_ref):
            t = ref[...]
            parts.append(t * _rms_scale(t))
        y = (jnp.concatenate(parts, axis=1) * gg_ref[...]).astype(BF16)
        y_ref[...] = y
        x1 = x_ref[...] + _dot_nn(y, wo_ref[...])
        x1_ref[...] = x1
        h = ((x1 * _rms_scale(x1)) * g_ref[...]).astype(BF16)
        h_ref[...] = h
        for n0 in range(0, f, COL_CHUNK):
            u = _dot_nt(h, w1_ref[n0:n0 + COL_CHUNK, :])
            a_ref[:, n0:n0 + COL_CHUNK] = jnp.square(jnp.maximum(u, 0.0)).astype(BF16)

    rows = lambda w: pl.BlockSpec((tm, w), lambda i: (i, 0))
    vec = pl.BlockSpec((1, d), lambda i: (0, 0))
    return _call(
        body,
        grid=(s // tm,),
        in_specs=[rows(A_WIDTH), rows(CONV_CH), rows(C_WIDTH), vec, _resident((d, d)), rows(d), vec, _resident((f, d))],
        out_specs=[rows(d), rows(d), rows(f), rows(d)],
        out_shape=[jax.ShapeDtypeStruct((s, d), BF16), jax.ShapeDtypeStruct((s, d), F32),
                   jax.ShapeDtypeStruct((s, f), BF16), jax.ShapeDtypeStruct((s, d), BF16)],
        operands=(ya, yb, yc, gg, wo, x0, g_mlp, wt1), name=name, comm=comm)


def _relu_from_square(av):
    return av * lax.rsqrt(jnp.maximum(av, F32_TINY))


def _mm_res_loss(a, w2, x1, g, target, name):
    s, f = a.shape
    d = w2.shape[1]
    tm = _row_tile(2 * f + 4 * d + 4 * d + 4 * d + 2 * d + 2 * f, 2 * f * d)

    def body(a_ref, w_ref, x_ref, g_ref, t_ref, loss_ref, dx_ref, dxb_ref, dg_ref, du_ref):
        @pl.when(pl.program_id(0) == 0)
        def _():
            loss_ref[...] = jnp.zeros_like(loss_ref)
            dg_ref[...] = jnp.zeros_like(dg_ref)

        xx = x_ref[...] + _dot_nn(a_ref[...], w_ref[...])
        r = _rms_scale(xx)
        n = xx * r
        gv = g_ref[...]
        err = n * gv - t_ref[...]
        per_tok = jnp.sum(err * err, axis=1, keepdims=True) * (1.0 / d)
        loss_ref[...] += 0.5 * jnp.sum(per_tok, axis=0, keepdims=True)
        dout = err * (1.0 / d)
        dg_ref[...] += jnp.sum(dout * n, axis=0, keepdims=True)
        dx = _rms_bwd(n, r, dout * gv)
        dx_ref[...] = dx
        dxb = dx.astype(BF16)
        dxb_ref[...] = dxb
        for n0 in range(0, f, COL_CHUNK):
            da = _dot_nt(dxb, w_ref[n0:n0 + COL_CHUNK, :])
            rl = _relu_from_square(a_ref[:, n0:n0 + COL_CHUNK].astype(F32))
            du_ref[:, n0:n0 + COL_CHUNK] = (da * (2.0 * rl)).astype(BF16)

    rows = lambda w: pl.BlockSpec((tm, w), lambda i: (i, 0))
    vec = pl.BlockSpec((1, d), lambda i: (0, 0))
    return pl.pallas_call(
        body,
        grid=(s // tm,),
        in_specs=[rows(f), _resident((f, d)), rows(d), vec, rows(d)],
        out_specs=[pl.BlockSpec((8, LANES), lambda i: (0, 0)), rows(d), rows(d), vec, rows(f)],
        out_shape=[jax.ShapeDtypeStruct((8, LANES), F32), jax.ShapeDtypeStruct((s, d), F32),
                   jax.ShapeDtypeStruct((s, d), BF16), jax.ShapeDtypeStruct((1, d), F32),
                   jax.ShapeDtypeStruct((s, f), BF16)],
        compiler_params=_params("arbitrary"),
        name=name,
    )(a, w2, x1, g, target)


def _mlp_bwd_act(dxb, w2, a, name, comm=None):
    s, d = dxb.shape
    f = w2.shape[0]
    tm = _row_tile(2 * d + 2 * f + 2 * f, 2 * f * d)

    def body(dx_ref, w_ref, a_ref, du_ref):
        dx = dx_ref[...]
        for n0 in range(0, f, COL_CHUNK):
            da = _dot_nt(dx, w_ref[n0:n0 + COL_CHUNK, :])
            rl = _relu_from_square(a_ref[:, n0:n0 + COL_CHUNK].astype(F32))
            du_ref[:, n0:n0 + COL_CHUNK] = (da * (2.0 * rl)).astype(BF16)

    return _call(
        body,
        grid=(s // tm,),
        in_specs=[pl.BlockSpec((tm, d), lambda i: (i, 0)),
                  _resident((f, d)),
                  pl.BlockSpec((tm, f), lambda i: (i, 0))],
        out_specs=[pl.BlockSpec((tm, f), lambda i: (i, 0))],
        out_shape=[jax.ShapeDtypeStruct((s, f), BF16)],
        operands=(dxb, w2, a), name=name, comm=comm)


def _mm_tn(pairs, name, comm=None):
    s, d = pairs[0][1].shape
    tn = 512
    tiles = [a.shape[1] // tn for a, _ in pairs]
    starts = [sum(tiles[:k]) for k in range(len(pairs))]

    def body(*refs):
        ins, outs = refs[:2 * len(pairs)], refs[2 * len(pairs):3 * len(pairs)]
        acc, late = refs[3 * len(pairs)], refs[3 * len(pairs) + 1:]
        j = pl.program_id(0)
        b_refs = [ins[1]]
        for k in range(1, len(pairs)):
            b_late = _Late(ins[2 * k + 1], late[2 * k - 2], late[2 * k - 1])
            b_late.fetch(starts[k])()
            b_refs.append(b_late.vmem)
        for k in range(len(pairs)):
            def run(a_ref=ins[2 * k], b_ref=b_refs[k], o_ref=outs[k]):
                for k0 in range(0, s, ROW_TILE):
                    part = _dot_tn(a_ref[k0:k0 + ROW_TILE, :], b_ref[k0:k0 + ROW_TILE, :])
                    if k0 == 0:
                        acc[...] = part
                    else:
                        acc[...] += part
                o_ref[...] = acc[...].astype(BF16)

            pl.when((j >= starts[k]) & (j < starts[k] + tiles[k]))(run)

    def tile_of(k):
        return lambda j: jnp.clip(j - starts[k], 0, tiles[k] - 1)

    in_specs, out_specs = [], []
    for k in range(len(pairs)):
        in_specs += [pl.BlockSpec((s, tn), lambda j, t=tile_of(k): (0, t(j))), _resident((s, d)) if k == 0 else LATE_SPEC]
        out_specs.append(pl.BlockSpec((tn, d), lambda j, t=tile_of(k): (t(j), 0)))
    late = [scratch for _, b in pairs[1:] for scratch in _late_scratch(b)]
    return _call(
        body,
        grid=(sum(tiles),),
        in_specs=in_specs,
        out_specs=out_specs,
        out_shape=[jax.ShapeDtypeStruct((a.shape[1], d), BF16) for a, _ in pairs],
        operands=tuple(t for pair in pairs for t in pair), name=name,
        scratch_shapes=[pltpu.VMEM((tn, d), F32)] + late, comm=comm)


def _mm_nn_normbwd(dact, wt, x, dres, g, name, comm=None):
    s, kdim = dact.shape
    d = wt.shape[1]
    tm = _row_tile(2 * kdim + 4 * d + 4 * d + 4 * d + 2 * d, 2 * kdim * d)

    def body(a_ref, w_ref, x_ref, r_ref, g_ref, o_ref, ob_ref, dg_ref):
        @pl.when(pl.program_id(0) == 0)
        def _():
            dg_ref[...] = jnp.zeros_like(dg_ref)

        dh = _dot_nn(a_ref[...], w_ref[...])
        xx = x_ref[...]
        r = _rms_scale(xx)
        n = xx * r
        dg_ref[...] += jnp.sum(dh * n, axis=0, keepdims=True)
        dx = r_ref[...] + _rms_bwd(n, r, dh * g_ref[...])
        o_ref[...] = dx
        ob_ref[...] = dx.astype(BF16)

    return _call(
        body,
        grid=(s // tm,),
        in_specs=[pl.BlockSpec((tm, kdim), lambda i: (i, 0)),
                  _resident((kdim, d)),
                  pl.BlockSpec((tm, d), lambda i: (i, 0)),
                  pl.BlockSpec((tm, d), lambda i: (i, 0)),
                  pl.BlockSpec((1, d), lambda i: (0, 0))],
        out_specs=[pl.BlockSpec((tm, d), lambda i: (i, 0)),
                   pl.BlockSpec((tm, d), lambda i: (i, 0)),
                   pl.BlockSpec((1, d), lambda i: (0, 0))],
        out_shape=[jax.ShapeDtypeStruct((s, d), F32), jax.ShapeDtypeStruct((s, d), BF16),
                   jax.ShapeDtypeStruct((1, d), F32)],
        operands=(dact, wt, x, dres, g), name=name, comm=comm)


def _ff_in_mix_bwd(du, wt1, x1, dres, g_mlp, wo, ya, yb, yc, gg, name, comm=None):
    s, f = du.shape
    d = wt1.shape[1]
    widths = (A_WIDTH, CONV_CH, C_WIDTH)
    tm = _row_tile(2 * f + 4 * d + 4 * d + 4 * d + 2 * d + 4 * d + 4 * d, 2 * f * d + 2 * d * d)

    def body(du_ref, w1_ref, x_ref, r_ref, g_ref, wo_ref, ya_ref, yb_ref, yc_ref, gg_ref,
             dx_ref, dxb_ref, dg_ref, da_ref, db_ref, dc_ref, dgg_ref):
        @pl.when(pl.program_id(0) == 0)
        def _():
            dg_ref[...] = jnp.zeros_like(dg_ref)
            dgg_ref[...] = jnp.zeros_like(dgg_ref)

        dh = _dot_nn(du_ref[...], w1_ref[...])
        xx = x_ref[...]
        r = _rms_scale(xx)
        n = xx * r
        dg_ref[...] += jnp.sum(dh * n, axis=0, keepdims=True)
        dx = r_ref[...] + _rms_bwd(n, r, dh * g_ref[...])
        dx_ref[...] = dx
        dxb = dx.astype(BF16)
        dxb_ref[...] = dxb

        dy = _dot_nt(dxb, wo_ref[...])
        gv = gg_ref[...]
        off = 0
        dgs = []
        for ref, out, w in zip((ya_ref, yb_ref, yc_ref), (da_ref, db_ref, dc_ref), widths):
            t = ref[...]
            r = _rms_scale(t)
            n = t * r
            dyg = dy[:, off:off + w]
            dgs.append(jnp.sum(dyg * n, axis=0, keepdims=True))
            out[...] = _rms_bwd(n, r, dyg * gv[:, off:off + w])
            off += w
        dgg_ref[...] += jnp.concatenate(dgs, axis=1)

    rows = lambda w: pl.BlockSpec((tm, w), lambda i: (i, 0))
    vec = pl.BlockSpec((1, d), lambda i: (0, 0))
    return _call(
        body,
        grid=(s // tm,),
        in_specs=[rows(f), _resident((f, d)), rows(d), rows(d), vec, _resident((d, d)),
                  rows(A_WIDTH), rows(CONV_CH), rows(C_WIDTH), vec],
        out_specs=[rows(d), rows(d), vec, rows(A_WIDTH), rows(CONV_CH), rows(C_WIDTH), vec],
        out_shape=[jax.ShapeDtypeStruct((s, d), F32), jax.ShapeDtypeStruct((s, d), BF16), jax.ShapeDtypeStruct((1, d), F32),
                   jax.ShapeDtypeStruct((s, A_WIDTH), F32), jax.ShapeDtypeStruct((s, CONV_CH), F32),
                   jax.ShapeDtypeStruct((s, C_WIDTH), F32), jax.ShapeDtypeStruct((1, d), F32)],
        operands=(du, wt1, x1, dres, g_mlp, wo, ya, yb, yc, gg), name=name, comm=comm)


CONV_CHUNK = 256
CONV_HALO = 8


def _conv_fwd(z, cw, name):
    s = z.shape[0]
    nch = s // CONV_CHUNK

    def body(gb_ref, gc_ref, xb_ref, w_ref, o_ref, us):
        us[pl.ds(0, CONV_HALO), :] = jnp.zeros((CONV_HALO, LANES), F32)
        us[pl.ds(CONV_HALO, s), :] = gc_ref[...] * xb_ref[...]
        w0, w1, w2 = w_ref[0:1, :], w_ref[1:2, :], w_ref[2:3, :]

        def chunk(c, carry):
            st = pl.multiple_of(c * CONV_CHUNK, CONV_CHUNK)
            ext = us[pl.ds(st, CONV_CHUNK + CONV_HALO), :]
            y = (w0 * ext[CONV_HALO - 2:CONV_HALO - 2 + CONV_CHUNK]
                 + w1 * ext[CONV_HALO - 1:CONV_HALO - 1 + CONV_CHUNK]
                 + w2 * ext[CONV_HALO:])
            o_ref[pl.ds(st, CONV_CHUNK), :] = gb_ref[pl.ds(st, CONV_CHUNK), :] * y
            return carry

        lax.fori_loop(0, nch, chunk, 0)

    col = lambda blk: pl.BlockSpec((s, LANES), lambda j, blk=blk: (0, blk + j))
    return pl.pallas_call(
        body,
        grid=(CONV_CH // LANES,),
        in_specs=[col(GB_BLK), col(GC_BLK), col(XB_BLK), pl.BlockSpec((3, LANES), lambda j: (0, j))],
        out_specs=pl.BlockSpec((s, LANES), lambda j: (0, j)),
        out_shape=jax.ShapeDtypeStruct((s, CONV_CH), F32),
        scratch_shapes=[pltpu.VMEM((s + CONV_HALO, LANES), F32)],
        compiler_params=_params("parallel"),
        name=name,
    )(z, z, z, cw)


def _conv_bwd(z, cw, dyb, dz, name):
    s = z.shape[0]
    nch = s // CONV_CHUNK
    ncol = CONV_CH // LANES

    def body(gb_ref, gc_ref, xb_ref, w_ref, dy_ref, dz_in, dz_ref, dw_ref, us, ds_, dgb_ref, dgc_ref, dxb_ref, sems):
        j = pl.program_id(0)

        def to_dz(staged, blk, k):
            cols = pl.ds(pl.multiple_of((blk + j) * LANES, LANES), LANES)
            return pltpu.make_async_copy(staged, dz_ref.at[:, cols], sems.at[k])

        copies = [to_dz(dgb_ref, GB_BLK, 0), to_dz(dgc_ref, GC_BLK, 1), to_dz(dxb_ref, XB_BLK, 2)]

        @pl.when(j > 0)
        def _():
            for cp in copies:
                cp.wait()

        us[pl.ds(0, CONV_HALO), :] = jnp.zeros((CONV_HALO, LANES), F32)
        us[pl.ds(CONV_HALO, s), :] = gc_ref[...] * xb_ref[...]
        ds_[pl.ds(s, CONV_HALO), :] = jnp.zeros((CONV_HALO, LANES), F32)
        ds_[pl.ds(0, s), :] = dy_ref[...] * gb_ref[...]
        w0, w1, w2 = w_ref[0:1, :], w_ref[1:2, :], w_ref[2:3, :]
        zero = jnp.zeros((1, LANES), F32)

        def chunk(c, carry):
            a0, a1, a2 = carry
            st = pl.multiple_of(c * CONV_CHUNK, CONV_CHUNK)
            rows = pl.ds(st, CONV_CHUNK)
            ext = us[pl.ds(st, CONV_CHUNK + CONV_HALO), :]
            um2 = ext[CONV_HALO - 2:CONV_HALO - 2 + CONV_CHUNK]
            um1 = ext[CONV_HALO - 1:CONV_HALO - 1 + CONV_CHUNK]
            u0 = ext[CONV_HALO:]
            dext = ds_[pl.ds(st, CONV_CHUNK + CONV_HALO), :]
            dc0 = dext[:CONV_CHUNK]
            du = w2 * dc0 + w1 * dext[1:1 + CONV_CHUNK] + w0 * dext[2:2 + CONV_CHUNK]
            yconv = w0 * um2 + w1 * um1 + w2 * u0
            dgb_ref[rows, :] = (dy_ref[rows, :] * yconv).astype(BF16)
            dgc_ref[rows, :] = (du * xb_ref[rows, :]).astype(BF16)
            dxb_ref[rows, :] = (du * gc_ref[rows, :]).astype(BF16)
            a0 = a0 + jnp.sum(dc0 * um2, axis=0, keepdims=True)
            a1 = a1 + jnp.sum(dc0 * um1, axis=0, keepdims=True)
            a2 = a2 + jnp.sum(dc0 * u0, axis=0, keepdims=True)
            return a0, a1, a2

        a0, a1, a2 = lax.fori_loop(0, nch, chunk, (zero, zero, zero))
        dw_ref[...] = jnp.concatenate([a0, a1, a2, jnp.zeros((5, LANES), F32)], axis=0)
        for cp in copies:
            cp.start()

        @pl.when(j == ncol - 1)
        def _():
            for cp in copies:
                cp.wait()

    col = lambda blk: pl.BlockSpec((s, LANES), lambda j, blk=blk: (0, blk + j))
    hbm = pl.BlockSpec(memory_space=pl.ANY)
    return pl.pallas_call(
        body,
        grid=(ncol,),
        in_specs=[col(GB_BLK), col(GC_BLK), col(XB_BLK), pl.BlockSpec((3, LANES), lambda j: (0, j)),
                  pl.BlockSpec((s, LANES), lambda j: (0, j)), hbm],
        out_specs=[hbm, pl.BlockSpec((8, LANES), lambda j: (0, j))],
        out_shape=[jax.ShapeDtypeStruct(dz.shape, dz.dtype), jax.ShapeDtypeStruct((8, CONV_CH), F32)],
        scratch_shapes=[pltpu.VMEM((s + CONV_HALO, LANES), F32), pltpu.VMEM((s + CONV_HALO, LANES), F32)]
        + [pltpu.VMEM((s, LANES), BF16)] * 3 + [pltpu.SemaphoreType.DMA((3,))],
        input_output_aliases={5: 0},
        compiler_params=_params("arbitrary"),
        name=name,
    )(z, z, z, cw, dyb, dz)


ATTN_ROWS = 512
ATTN_UNROLL = 8


def _band_rows(b, d, r):
    base = pl.multiple_of(b * (BLOCK * d), BLOCK)
    prev = jnp.maximum(base - BLOCK * d, 0)
    if d == 1:
        return pl.ds(base, BLOCK), pl.ds(pl.multiple_of(prev, BLOCK), BLOCK)
    return pl.ds(base + r, BLOCK, stride=d), pl.ds(prev + r, BLOCK, stride=d)


def _write_band_bias(bias_ref, max_dist):
    qi = lax.broadcasted_iota(jnp.int32, (BLOCK, 2 * BLOCK), 0)
    kj = lax.broadcasted_iota(jnp.int32, (BLOCK, 2 * BLOCK), 1)
    dist = BLOCK + qi - kj
    band = (dist >= 0) & (dist <= max_dist)
    bias_ref[0:BLOCK, :] = jnp.where(band, 0.0, -jnp.inf)
    bias_ref[BLOCK:2 * BLOCK, :] = jnp.where(band & (kj >= BLOCK), 0.0, -jnp.inf)


def _band_bias(bias_ref, b):
    bias = bias_ref[pl.ds(pl.multiple_of(jnp.where(b > 0, 0, BLOCK), BLOCK), BLOCK), :]
    return jnp.concatenate([bias, bias], axis=0)


def _kv_halves(pair):
    zero = jnp.zeros((1, LANES), jnp.int32)
    return zero + (pair >> 1), zero + ((pair + 1) >> 1)


def _stack_heads(t, head0, halves=None):
    top, bottom = jnp.where(head0, t, 0.0), jnp.where(head0, 0.0, t)
    if halves is not None:
        top = jnp.where(halves[0] == 1, pltpu.roll(top, HEAD_DIM, 1), top)
        bottom = jnp.where(halves[1] == 0, pltpu.roll(bottom, HEAD_DIM, 1), bottom)
    return jnp.concatenate([top, bottom], axis=0).astype(BF16)


def _unstack_heads(t, head0, halves=None):
    top, bottom = t[:BLOCK], t[BLOCK:]
    if halves is not None:
        top = jnp.where(halves[0] == 1, pltpu.roll(top, HEAD_DIM, 1), top)
        bottom = jnp.where(halves[1] == 0, pltpu.roll(bottom, HEAD_DIM, 1), bottom)
    return jnp.where(head0, top, bottom)


def _block_loops(s, patterns, unroll, one_block):
    for n, d in enumerate(patterns):
        nb = (s // BLOCK) // d
        ur = min(unroll, d)
        ub = unroll // ur
        for r0 in range(0, d, ur):
            def trip(i, carry, n=n, d=d, r0=r0, ur=ur, ub=ub):
                for u in range(ub):
                    for r in range(r0, r0 + ur):
                        one_block(i * ub + u, d, r, n == 0)
                return carry
            lax.fori_loop(0, nb // ub, trip, 0)


def _attn_fwd(z, m_init, l_init, q_blk, k_blk, v_blk, patterns, max_dist, gqa, name, comm=None):
    s = z.shape[0]
    npair = 3

    def body(q_ref, k_ref, v_ref, mi_ref, o_ref, lse0_ref, lse1_ref, bias_scr, m_scr, l_scr, *kv_scr):
        head0 = lax.broadcasted_iota(jnp.int32, (1, LANES), 1) < HEAD_DIM
        _write_band_bias(bias_scr, max_dist)
        ones = jnp.ones((2 * BLOCK, LANES), BF16)
        k_src, v_src = kv_scr if gqa else (k_ref, v_ref)
        if gqa:
            half = (lax.broadcasted_iota(jnp.int32, (1, LANES), 1) >= HEAD_DIM).astype(jnp.int32)
            swap = ((pl.program_id(0) + half) >> 1) != half

            def expand(c, carry):
                rows = pl.ds(pl.multiple_of(c * ATTN_ROWS, ATTN_ROWS), ATTN_ROWS)
                k_src[rows, :] = jnp.where(swap, pltpu.roll(k_ref[rows, :], HEAD_DIM, 1), k_ref[rows, :])
                v_src[rows, :] = jnp.where(swap, pltpu.roll(v_ref[rows, :], HEAD_DIM, 1), v_ref[rows, :])
                return carry

            lax.fori_loop(0, s // ATTN_ROWS, expand, 0)

        def one_block(b, d, r, first):
            rq, rp = _band_rows(b, d, r)
            q2 = _stack_heads(q_ref[rq, :] * SCALE, head0)
            k2 = jnp.concatenate([k_src[rp, :], k_src[rq, :]], axis=0).astype(BF16)
            v2 = jnp.concatenate([v_src[rp, :], v_src[rq, :]], axis=0).astype(BF16)
            sc = _dot_nt(q2, k2) + _band_bias(bias_scr, b)
            mb = jnp.max(sc, axis=1, keepdims=True)
            p = jnp.exp(sc - mb).astype(BF16)
            ob = _dot_nn(p, jnp.concatenate([v2, ones], axis=1))
            m_blk = _unstack_heads(jnp.broadcast_to(mb, (2 * BLOCK, LANES)), head0)
            l_blk = _unstack_heads(ob[:, LANES:], head0)
            o_blk = _unstack_heads(ob[:, :LANES], head0)
            if first and l_init == 0.0:
                m_new, l_new, o_new = m_blk, l_blk, o_blk
            else:
                if first:
                    m_old, l_old, o_old = jnp.broadcast_to(mi_ref[...], (BLOCK, LANES)), l_init, 0.0
                else:
                    m_old, l_old, o_old = m_scr[rq, :], l_scr[rq, :], o_ref[rq, :]
                m_new = jnp.maximum(m_old, m_blk)
                a_old = jnp.exp(m_old - m_new)
                a_blk = jnp.exp(m_blk - m_new)
                l_new = l_old * a_old + l_blk * a_blk
                o_new = o_old * a_old + o_blk * a_blk
            o_ref[rq, :], l_scr[rq, :], m_scr[rq, :] = o_new, l_new, m_new

        _block_loops(s, patterns, ATTN_UNROLL, one_block)

        def fin(c, carry):
            rows = pl.ds(pl.multiple_of(c * ATTN_ROWS, ATTN_ROWS), ATTN_ROWS)
            l = l_scr[rows, :]
            o_ref[rows, :] = o_ref[rows, :] / l
            lse = m_scr[rows, :] + jnp.log(l)
            swapped = pltpu.roll(lse, HEAD_DIM, 1)
            lse0_ref[rows, :] = jnp.where(head0, lse, swapped)
            lse1_ref[rows, :] = jnp.where(head0, swapped, lse)
            return carry

        lax.fori_loop(0, s // ATTN_ROWS, fin, 0)

    kv = (lambda blk: pl.BlockSpec((s, LANES), lambda j, blk=blk: (0, blk), pipeline_mode=pl.Buffered(1))) if gqa \
        else (lambda blk: pl.BlockSpec((s, LANES), lambda j, blk=blk: (0, blk + j)))
    own = pl.BlockSpec((s, LANES), lambda j: (0, j))
    return _call(
        body,
        grid=(npair,),
        in_specs=[pl.BlockSpec((s, LANES), lambda j: (0, q_blk + j)), kv(k_blk), kv(v_blk),
                  pl.BlockSpec((1, LANES), lambda j: (0, j))],
        out_specs=[own, own, own],
        out_shape=[jax.ShapeDtypeStruct((s, npair * LANES), F32)] * 3,
        operands=(z, z, z, m_init), name=name,
        scratch_shapes=[pltpu.VMEM((2 * BLOCK, 2 * BLOCK), F32)] + [pltpu.VMEM((s, LANES), F32)] * (4 if gqa else 2),
        comm=comm)


def _attn_bwd(z, do, o, lse, m_init, dz, q_blk, k_blk, v_blk, patterns, max_dist, gqa, name, comm=None):
    s = z.shape[0]
    npair = 3
    n_dz_in = 0 if dz is None else 1

    def body(q_ref, k_ref, v_ref, do_ref, o_ref, lse0_ref, lse1_ref, mi_ref, *rest):
        (dz_ref, dm_ref, dq_acc, dk_acc, dv_acc, dl0_scr, dl1_scr, bias_scr,
         dq_out, dk_out, dv_out, out_sems) = rest[n_dz_in:]
        pair = pl.program_id(0)
        head0 = lax.broadcasted_iota(jnp.int32, (1, LANES), 1) < HEAD_DIM
        halves = _kv_halves(pair) if gqa else None
        _write_band_bias(bias_scr, max_dist)

        def zero_kv():
            def f(c, carry):
                rows = pl.ds(pl.multiple_of(c * ATTN_ROWS, ATTN_ROWS), ATTN_ROWS)
                dk_acc[rows, :] = jnp.zeros((ATTN_ROWS, LANES), F32)
                dv_acc[rows, :] = jnp.zeros((ATTN_ROWS, LANES), F32)
                return carry
            lax.fori_loop(0, s // ATTN_ROWS, f, 0)

        if gqa:
            pl.when(pair == 0)(zero_kv)
        else:
            zero_kv()

        def prep(c, dm):
            rows = pl.ds(pl.multiple_of(c * ATTN_ROWS, ATTN_ROWS), ATTN_ROWS)
            dq_acc[rows, :] = jnp.zeros((ATTN_ROWS, LANES), F32)
            prod = do_ref[rows, :] * o_ref[rows, :]
            d0 = jnp.sum(jnp.where(head0, prod, 0.0), axis=1, keepdims=True)
            d1 = jnp.sum(jnp.where(head0, 0.0, prod), axis=1, keepdims=True)
            dl0_scr[rows, :] = jnp.broadcast_to(d0, (ATTN_ROWS, LANES))
            dl1_scr[rows, :] = jnp.broadcast_to(d1, (ATTN_ROWS, LANES))
            lse_own = jnp.where(head0, lse0_ref[rows, :], lse1_ref[rows, :])
            psink = jnp.exp(mi_ref[...] - lse_own)
            return dm - jnp.sum(psink * jnp.where(head0, d0, d1), axis=0, keepdims=True)

        dm_ref[...] = lax.fori_loop(0, s // ATTN_ROWS, prep, jnp.zeros((1, LANES), F32))

        def one_block(b, d, r, first):
            rq, rp = _band_rows(b, d, r)
            q2 = _stack_heads(q_ref[rq, :] * SCALE, head0, halves)
            do2 = _stack_heads(do_ref[rq, :], head0, halves)
            k2 = jnp.concatenate([k_ref[rp, :], k_ref[rq, :]], axis=0).astype(BF16)
            v2 = jnp.concatenate([v_ref[rp, :], v_ref[rq, :]], axis=0).astype(BF16)
            lse2 = jnp.concatenate([lse0_ref[rq, :], lse1_ref[rq, :]], axis=0)
            dl2 = jnp.concatenate([dl0_scr[rq, :], dl1_scr[rq, :]], axis=0)
            lse2 = jnp.concatenate([lse2, lse2], axis=1)
            dl2 = jnp.concatenate([dl2, dl2], axis=1)
            p = jnp.exp(_dot_nt(q2, k2) + _band_bias(bias_scr, b) - lse2)
            dp = _dot_nt(do2, v2)
            dsc = (p * (dp - dl2)).astype(BF16)
            dq2 = _unstack_heads(_dot_nn(dsc, k2), head0, halves)
            dk2 = _dot_tn(dsc, q2)
            dv2 = _dot_tn(p.astype(BF16), do2)
            dq_acc[rq, :] += dq2 * SCALE
            dk_acc[rp, :] += dk2[:BLOCK]
            dk_acc[rq, :] += dk2[BLOCK:]
            dv_acc[rp, :] += dv2[:BLOCK]
            dv_acc[rq, :] += dv2[BLOCK:]

        _block_loops(s, patterns, ATTN_UNROLL, one_block)

        def to_dz(staged, blk, k):
            cols = pl.ds(pl.multiple_of(blk * LANES, LANES), LANES)
            return pltpu.make_async_copy(staged, dz_ref.at[:, cols], out_sems.at[k])

        last_pair = pair == npair - 1
        q_copy = to_dz(dq_out, q_blk + pair, 0)
        kv_copies = [to_dz(dk_out, k_blk + (0 if gqa else pair), 1), to_dz(dv_out, v_blk + (0 if gqa else pair), 2)]

        @pl.when(pair > 0)
        def _():
            for cp in [q_copy] + ([] if gqa else kv_copies):
                cp.wait()

        def stage(acc, out):
            def f(c, carry):
                rows = pl.ds(pl.multiple_of(c * ATTN_ROWS, ATTN_ROWS), ATTN_ROWS)
                out[rows, :] = acc[rows, :].astype(BF16)
                return carry
            lax.fori_loop(0, s // ATTN_ROWS, f, 0)

        def stage_kv():
            stage(dk_acc, dk_out)
            stage(dv_acc, dv_out)
            for cp in kv_copies:
                cp.start()

        stage(dq_acc, dq_out)
        q_copy.start()
        if gqa:
            pl.when(last_pair)(stage_kv)
        else:
            stage_kv()

        @pl.when(last_pair)
        def _():
            for cp in [q_copy] + kv_copies:
                cp.wait()

    own = pl.BlockSpec((s, LANES), lambda j: (0, j))
    hbm = pl.BlockSpec(memory_space=pl.ANY)
    if gqa:
        kv = lambda blk: pl.BlockSpec((s, LANES), lambda j, blk=blk: (0, blk), pipeline_mode=pl.Buffered(1))
    else:
        kv = lambda blk: pl.BlockSpec((s, LANES), lambda j, blk=blk: (0, blk + j))
    in_specs = [pl.BlockSpec((s, LANES), lambda j: (0, q_blk + j)), kv(k_blk), kv(v_blk), own, own, own, own,
                pl.BlockSpec((1, LANES), lambda j: (0, j))]
    operands = (z, z, z, do, o, lse[0], lse[1], m_init)
    return _call(
        body,
        grid=(npair,),
        in_specs=in_specs + [hbm] * n_dz_in,
        out_specs=[hbm, pl.BlockSpec((1, LANES), lambda j: (0, j))],
        out_shape=[jax.ShapeDtypeStruct((s, IN_WIDTH), BF16), jax.ShapeDtypeStruct((1, npair * LANES), F32)],
        operands=operands + (() if dz is None else (dz,)), name=name,
        scratch_shapes=[pltpu.VMEM((s, LANES), F32)] * 5 + [pltpu.VMEM((2 * BLOCK, 2 * BLOCK), F32)]
        + [pltpu.VMEM((s, LANES), BF16)] * 3 + [pltpu.SemaphoreType.DMA((3,))],
        comm=comm, aliases={} if dz is None else {len(in_specs): 0})


def _adamw_math(w, g, m, v):
    m = ADAM_B1 * m + (1.0 - ADAM_B1) * g
    v = ADAM_B2 * v + (1.0 - ADAM_B2) * (g * g)
    m_hat = m / (1.0 - ADAM_B1 ** ADAM_STEP)
    v_hat = v / (1.0 - ADAM_B2 ** ADAM_STEP)
    delta = -ADAM_LR * (m_hat / (jnp.sqrt(v_hat) + ADAM_EPS) + ADAM_WD * w)
    return delta, m, v


def _adamw(w, g, m, v, name):
    rows, cols = w.shape
    tr = min(rows, 256)

    def body(w_ref, g_ref, m_ref, v_ref, d_ref, nm_ref, nv_ref):
        d_ref[...], nm_ref[...], nv_ref[...] = _adamw_math(w_ref[...], g_ref[...], m_ref[...], v_ref[...])

    spec = pl.BlockSpec((tr, cols), lambda i: (i, 0))
    return pl.pallas_call(
        body,
        grid=(rows // tr,),
        in_specs=[spec] * 4,
        out_specs=[spec] * 3,
        out_shape=[jax.ShapeDtypeStruct((rows, cols), F32)] * 3,
        compiler_params=_params("parallel"),
        name=name,
    )(w, g, m, v)


def _sum_adamw(parts, w, m, v, pos, transpose, name):
    assert len(parts) == DEPTH == 2
    (p0, r0), (p1, r1) = parts
    _, rows, cols = p0.shape
    tr = 256 if rows % 256 == 0 else rows
    nt = rows // tr

    def body(pos_ref, p0_ref, r0_ref, p1_ref, r1_ref, w_ref, m_ref, v_ref, g_ref, d_ref, nm_ref, nv_ref):
        def run(p_ref, r_ref):
            g = ((p_ref[...].astype(F32) + r_ref[0].astype(F32)) + r_ref[1].astype(F32)) + r_ref[2].astype(F32)
            if transpose:
                g = g.T
            g_ref[...] = g
            d_ref[...], nm_ref[...], nv_ref[...] = _adamw_math(w_ref[...], g, m_ref[...], v_ref[...])

        layer0 = pl.program_id(0) < nt
        pl.when(layer0)(lambda: run(p0_ref, r0_ref))
        pl.when(jnp.logical_not(layer0))(lambda: run(p1_ref, r1_ref))

    def tile0(i):
        return jnp.minimum(i, nt - 1)

    def tile1(i):
        return jnp.maximum(i - nt, 0)

    if transpose:
        w_spec = pl.BlockSpec((None, cols, tr), lambda i, q: (i // nt, 0, i % nt))
    else:
        w_spec = pl.BlockSpec((None, tr, cols), lambda i, q: (i // nt, i % nt, 0))
    return pl.pallas_call(
        body,
        grid_spec=pltpu.PrefetchScalarGridSpec(
            num_scalar_prefetch=1,
            grid=(DEPTH * nt,),
            in_specs=[pl.BlockSpec((None, tr, cols), lambda i, q: (q[0], tile0(i), 0)),
                      pl.BlockSpec((3, tr, cols), lambda i, q: (0, tile0(i), 0)),
                      pl.BlockSpec((None, tr, cols), lambda i, q: (q[0], tile1(i), 0)),
                      pl.BlockSpec((3, tr, cols), lambda i, q: (0, tile1(i), 0)),
                      w_spec, w_spec, w_spec],
            out_specs=[w_spec] * 4,
        ),
        out_shape=[jax.ShapeDtypeStruct(w.shape, F32)] * 4,
        compiler_params=_params("arbitrary"),
        name=name,
    )(pos, p0, r0, p1, r1, w, m, v)


def _small_sum_adamw(gathered, params, name):
    _, rows, cols = gathered.shape
    n = len(params)

    def body(ga_ref, *refs):
        ins, outs, (g_scr,) = refs[:3 * n], refs[3 * n:7 * n + 2], refs[7 * n + 2:]
        g = ga_ref[0]
        for i in range(1, N_DEV):
            g = g + ga_ref[i]
        g_scr[...] = g
        for k, (row0, w, _, _) in enumerate(params):
            w_ref, m_ref, v_ref = ins[3 * k:3 * k + 3]
            gk = g_scr[row0:row0 + w.shape[0], :w.shape[1]]
            outs[4 * k][...] = gk
            outs[4 * k + 1][...], outs[4 * k + 2][...], outs[4 * k + 3][...] = _adamw_math(
                w_ref[...], gk, m_ref[...], v_ref[...])
        outs[4 * n][...] = g_scr[CONV_ROW:CONV_ROW + 8, :]
        outs[4 * n + 1][...] = g_scr[LOSS_ROW:LOSS_ROW + 1, :]

    out_shape = []
    for _, w, _, _ in params:
        out_shape += [jax.ShapeDtypeStruct(w.shape, F32)] * 4
    out_shape += [jax.ShapeDtypeStruct((8, cols), F32), jax.ShapeDtypeStruct((1, cols), F32)]
    res = pl.pallas_call(
        body,
        out_shape=out_shape,
        scratch_shapes=[pltpu.VMEM((rows, cols), F32)],
        name=name,
    )(gathered, *[t for _, w, m, v in params for t in (w, m, v)])
    return [res[4 * k:4 * k + 4] for k in range(n)], res[4 * n], res[4 * n + 1]


def _pair_sums(g4s, r1s, pos, name):
    n = len(g4s)

    def body(pos_ref, *refs):
        for g_ref, r_ref, o_ref in zip(refs[:n], refs[n:2 * n], refs[2 * n:]):
            o_ref[...] = (g_ref[...].astype(F32) + r_ref[...].astype(F32)).astype(BF16)

    block = lambda t: pl.BlockSpec((None,) + t.shape[1:], lambda i, p: (i, 0, 0))
    return pl.pallas_call(
        body,
        grid_spec=pltpu.PrefetchScalarGridSpec(
            num_scalar_prefetch=1,
            grid=(4,),
            in_specs=[pl.BlockSpec((None, None) + g.shape[2:], lambda i, p: (i, p[1], 0, 0)) for g in g4s]
            + [block(r) for r in r1s],
            out_specs=[block(r) for r in r1s],
        ),
        out_shape=[jax.ShapeDtypeStruct(r.shape, BF16) for r in r1s],
        compiler_params=_params("parallel"),
        name=name,
    )(pos, *g4s, *r1s)


GATHER_ID, CHIP_ID, SIBLING_ID = 0, 1, 2


def _place():
    return lax.axis_index("x"), lax.axis_index("y"), lax.axis_index("c")


def _sibling():
    x, y, c = _place()
    return (x, y, 1 - c)


def _same_core_of_other_chips():
    x, y, c = _place()
    return [(1 - x, y, c), (x, 1 - y, c), (1 - x, 1 - y, c)]


def _gather_comm(shards):
    na = len(shards)
    stacks, index = zip(*shards)

    def plan(ins, outs, sems):
        send_sems, recv_sems, local_sems = sems
        x, y, c = _place()
        me, sibling = (x, y, c), (x, y, 1 - c)
        chips = [(1 - x, y), (x, 1 - y), (1 - x, 1 - y)]
        shard = [ins[a].at[index[a]] for a in range(na)]

        def rows(a, px, py, pc):
            m = shard[a].shape[0]
            return outs[a].at[pl.ds((4 * px + 2 * py + pc) * m, m), :]

        def copy(a, k, block, to, src=None):
            return pltpu.make_async_remote_copy(
                src_ref=rows(a, *block) if src is None else src, dst_ref=rows(a, *block),
                send_sem=send_sems.at[a, k], recv_sem=recv_sems.at[a, k], device_id=to, device_id_type=MESH)

        mine = [pltpu.make_async_copy(shard[a], rows(a, *me), local_sems.at[a]) for a in range(na)]
        first = []
        for a in range(na):
            first.append(copy(a, 0, me, sibling, src=shard[a]))
            first += [copy(a, 1 + j, me, (*chip, c), src=shard[a]) for j, chip in enumerate(chips)]
        return me, sibling, chips, c, copy, mine, first

    def start(ins, outs, sems):
        *_, mine, first = plan(ins, outs, sems)
        for cp in mine + first:
            cp.start()

    def finish(ins, outs, sems):
        me, sibling, chips, c, copy, mine, first = plan(ins, outs, sems)
        passed = []
        for j, chip in enumerate(chips):
            for a in range(na):
                copy(a, 1 + j, (*chip, c), me).wait_recv()
                cp = copy(a, 4 + j, (*chip, c), sibling)
                cp.start()
                passed.append(cp)
        for a in range(na):
            copy(a, 0, sibling, me).wait_recv()
            for j, chip in enumerate(chips):
                copy(a, 4 + j, (*chip, 1 - c), me).wait_recv()
        for cp in first + passed:
            cp.wait_send()
        for cp in mine:
            cp.wait()

    return _Comm(tuple(stacks),
                 tuple(jax.ShapeDtypeStruct((N_DEV * t.shape[1], t.shape[2]), t.dtype) for t in stacks),
                 (pltpu.SemaphoreType.DMA((na, 7)), pltpu.SemaphoreType.DMA((na, 7)), pltpu.SemaphoreType.DMA((na,))),
                 start, finish, lambda: [_sibling()] + _same_core_of_other_chips(), GATHER_ID)


def _exchange_comm(arrays, out_shape, n_copies, copies_of, peers, collective_id):
    na = len(arrays)

    def every(ins, outs, sems):
        send_sems, recv_sems = sems
        return [cp for a in range(na) for cp in copies_of(ins, outs, a, send_sems, recv_sems)]

    def start(ins, outs, sems):
        for cp in every(ins, outs, sems):
            cp.start()

    def finish(ins, outs, sems):
        for cp in every(ins, outs, sems):
            cp.wait()

    return _Comm(tuple(arrays), tuple(out_shape),
                 (pltpu.SemaphoreType.DMA((na, n_copies)), pltpu.SemaphoreType.DMA((na, n_copies))), start, finish,
                 peers, collective_id)


def _sibling_comm(grads):
    def copies_of(ins, outs, a, send_sems, recv_sems):
        x, y, c = _place()
        return [pltpu.make_async_remote_copy(
            src_ref=ins[a].at[chip, 1 - c], dst_ref=outs[a].at[chip],
            send_sem=send_sems.at[a, chip], recv_sem=recv_sems.at[a, chip],
            device_id=(x, y, 1 - c), device_id_type=MESH) for chip in range(4)]

    return _exchange_comm(grads, [jax.ShapeDtypeStruct((4,) + t.shape[2:], t.dtype) for t in grads], 4, copies_of,
                          lambda: [_sibling()], SIBLING_ID)


def _chip_comm(partials):
    def copies_of(ins, outs, a, send_sems, recv_sems):
        x, y, c = _place()
        chips = [(1 - x, y), (x, 1 - y), (1 - x, 1 - y)]
        return [pltpu.make_async_remote_copy(
            src_ref=ins[a].at[2 * cx + cy], dst_ref=outs[a].at[k],
            send_sem=send_sems.at[a, k], recv_sem=recv_sems.at[a, k],
            device_id=(cx, cy, c), device_id_type=MESH) for k, (cx, cy) in enumerate(chips)]

    return _exchange_comm(partials, [jax.ShapeDtypeStruct((3,) + t.shape[1:], t.dtype) for t in partials], 3, copies_of,
                          _same_core_of_other_chips, CHIP_ID)


def _pad_rows(t, rows):
    return jnp.pad(t, ((0, rows - t.shape[0]), (0, D_MODEL - t.shape[1])))


MIX_ROW, GROUP_ROW, MLP_ROW, FINAL_ROW, CONV_ROW, SINK_ROW = 0, 8, 16, 24, 32, 40
LOSS_ROW = FINAL_ROW + 1


def _pack_small(g_mix, g_group, g_mlp, g_final, conv, sinks, loss):
    final_and_loss = jnp.concatenate([g_final.reshape(1, D_MODEL), _pad_rows(loss, 1)], axis=0)
    return jnp.concatenate([
        _pad_rows(g_mix, 8), _pad_rows(g_group, 8), _pad_rows(g_mlp, 8), _pad_rows(final_and_loss, 8),
        _pad_rows(conv.reshape(DEPTH * 3, CONV_CH), 8), _pad_rows(sinks.reshape(1, DEPTH * 6), 8)], axis=0)


def kernel(x, w_in, conv_w, sinks, g_mix, g_group, w_o, g_mlp, w_ff_in, w_ff_out, g_final, loss_target, m_w_in, m_conv_w, m_sinks, m_g_mix, m_g_group, m_w_o, m_g_mlp, m_w_ff_in, m_w_ff_out, m_g_final, v_w_in, v_conv_w, v_sinks, v_g_mix, v_g_group, v_w_o, v_g_mlp, v_w_ff_in, v_w_ff_out, v_g_final):
    ax, ay, ac = _place()
    chip = 2 * ax + ay
    dev = 4 * ax + 2 * ay + ac
    pos = jnp.stack([chip, ac]).astype(jnp.int32)

    x0 = x.reshape(SEQ, D_MODEL)
    target = loss_target.reshape(SEQ, D_MODEL)

    stacks = [jnp.swapaxes(w_in, 1, 2).astype(BF16), w_o.astype(BF16),
              jnp.swapaxes(w_ff_in, 1, 2).astype(BF16), w_ff_out.astype(BF16)]
    shards = {(l, kind): (stack, l) for kind, stack in enumerate(stacks) for l in range(DEPTH)}
    conv_tile = jnp.pad(conv_w.reshape(DEPTH * 3, CONV_CH // N_DEV), ((0, 2), (0, LANES - CONV_CH // N_DEV)))
    wt_in0, conv_all = _comm_only(_gather_comm([shards[0, 0], (conv_tile[None], 0)]), "gather_first")
    conv_full = conv_all.reshape(N_DEV, 8, LANES)[:, :DEPTH * 3, :CONV_CH // N_DEV]
    conv_full = conv_full.transpose(1, 0, 2).reshape(DEPTH, 3, CONV_CH)

    dx, parts, small = _step(x0, target, shards, wt_in0, conv_full, sinks, g_mix, g_group, g_mlp, g_final, pos)
    return _finish(dx, parts, small, pos, dev, w_in, conv_w, sinks, g_mix, g_group, w_o, g_mlp, w_ff_in, w_ff_out, g_final, m_w_in, m_conv_w, m_sinks, m_g_mix, m_g_group, m_w_o, m_g_mlp, m_w_ff_in, m_w_ff_out, m_g_final, v_w_in, v_conv_w, v_sinks, v_g_mix, v_g_group, v_w_o, v_g_mlp, v_w_ff_in, v_w_ff_out, v_g_final)


FWD_CARRY = {(0, "in_proj"): ((1, 0),), (0, "window"): ((0, 1),), (0, "dilated"): ((0, 2),),
             (0, "mix_ff_in"): ((0, 3),), (0, "ff_out_in_proj"): ((1, 1), (1, 3)),
             (1, "dilated"): ((1, 2),)}


def _step(x0, target, shards, wt_in0, conv_full, sinks, g_mix, g_group, g_mlp, g_final, pos):
    sink_lanes = jnp.repeat(sinks.reshape(DEPTH, 6), HEAD_DIM, axis=1)
    no_sink = jnp.full((1, A_WIDTH), NEG_BIG, F32)
    full = {(0, 0): wt_in0}

    def gather(stage, l):
        keys = FWD_CARRY.get((l, stage), ())
        return keys, (_gather_comm([shards[k] for k in keys]) if keys else None)

    def landed(keys, got):
        full.update(zip(keys, got))

    saved = []
    xc = x0
    keys, comm = gather("in_proj", 0)
    (z, h), got = _norm_mm(xc, g_mix[0:1], full[0, 0], "in_proj_0", comm)
    landed(keys, got)
    for l in range(DEPTH):
        sink_l = sink_lanes[l:l + 1]
        keys, comm = gather("window", l)
        (yc, *lse_c), got = _attn_fwd(z, sink_l, 1.0, QC_BLK, KC_BLK, VC_BLK, (1,), C_MAX_DIST, True,
                                     f"window_attn_{l}", comm)
        landed(keys, got)
        yb = _conv_fwd(z, conv_full[l], f"conv_{l}")
        keys, comm = gather("dilated", l)
        (ya, *lse_a), got = _attn_fwd(z, no_sink, 0.0, QA_BLK, KA_BLK, VA_BLK, DILATED_PATTERNS, A_MAX_DIST, False,
                                     f"dilated_attn_{l}", comm)
        landed(keys, got)
        keys, comm = gather("mix_ff_in", l)
        (y, x1, a, h2), got = _mix_ff_in(ya, yb, yc, g_group[l:l + 1], full[l, 1], xc, g_mlp[l:l + 1], full[l, 2],
                                         f"mix_ff_in_{l}", comm)
        landed(keys, got)
        saved.append((xc, z, h, ya, lse_a, yb, yc, lse_c, sink_l, y, x1, a, h2))
        if l + 1 < DEPTH:
            keys, comm = gather("ff_out_in_proj", l)
            (xc, z, h), got = _ff_out_in_proj(a, full[l, 3], x1, g_mix[l + 1:l + 2], full[l + 1, 0],
                                              f"ff_out_{l}_in_proj_{l + 1}", comm)
            landed(keys, got)

    loss_slab, dx, dxb, dg_final, du = _mm_res_loss(a, full[DEPTH - 1, 3], x1, g_final.reshape(1, D_MODEL), target,
                                                    f"ff_out_{DEPTH - 1}_loss")

    def by_owner(t):
        return t.reshape(4, 2, t.shape[0] // N_DEV, D_MODEL)

    def pair(l, kinds, grads, received):
        sums = _pair_sums(grads, received, pos, f"grad_pair_sums_{l}_{kinds[0]}{kinds[1]}")
        partial.update({(l, kind): t for kind, t in zip(kinds, sums)})

    partial, r2 = {}, {}
    dg_mix, dg_group, dg_mlp, dconv, dsinks = [None] * DEPTH, [None] * DEPTH, [None] * DEPTH, [None] * DEPTH, [None] * DEPTH
    for l in reversed(range(DEPTH)):
        xin, z, h, ya, lse_a, yb, yc, lse_c, sink_l, y, x1, a, h2 = saved[l]
        if l + 1 < DEPTH:
            late = [(l + 1, 1), (l + 1, 0)]
            (du,), got = _mlp_bwd_act(dxb, full[l, 3], a, f"ff_out_bwd_{l}", _chip_comm([partial[k] for k in late]))
            r2.update(zip(late, got))
        (g3, g2), _ = _mm_tn([(a, dxb), (du, h2)], f"grad_w_ff_{l}")
        g3, g2 = by_owner(g3), by_owner(g2)
        (dx1, dx1b, dg_mlp[l], dya, dyb, dyc, dg_group[l]), got = _ff_in_mix_bwd(
            du, full[l, 2], x1, dx, g_mlp[l:l + 1], full[l, 1], ya, yb, yc, g_group[l:l + 1],
            f"ff_in_mix_bwd_{l}", _sibling_comm([g3, g2]))
        pair(l, (3, 2), [g3, g2], got)
        early = [(l, 3), (l, 2)]
        (dz, _), got = _attn_bwd(z, dya, ya, lse_a, no_sink, None, QA_BLK, KA_BLK, VA_BLK, DILATED_PATTERNS,
                                 A_MAX_DIST, False, f"dilated_attn_bwd_{l}", _chip_comm([partial[k] for k in early]))
        r2.update(zip(early, got))
        dz, dcw = _conv_bwd(z, conv_full[l], dyb, dz, f"conv_bwd_{l}")
        (dz, dsink), _ = _attn_bwd(z, dyc, yc, lse_c, sink_l, dz, QC_BLK, KC_BLK, VC_BLK, (1,), C_MAX_DIST,
                                   True, f"window_attn_bwd_{l}")
        (g1, g0), _ = _mm_tn([(y, dx1b), (dz, h)], f"grad_w_o_in_{l}")
        g1, g0 = by_owner(g1), by_owner(g0)
        if l > 0:
            (dx, dxb, dg_mix[l]), got = _mm_nn_normbwd(dz, full[l, 0], xin, dx1, g_mix[l:l + 1], f"in_proj_bwd_{l}",
                                                      _sibling_comm([g1, g0]))
            pair(l, (1, 0), [g1, g0], got)
        else:
            got = _comm_only(_sibling_comm([g1, g0]), "grad_sibling_exchange_last")
            pair(l, (1, 0), [g1, g0], got)
            (dx, dxb, dg_mix[l]), got = _mm_nn_normbwd(dz, full[l, 0], xin, dx1, g_mix[l:l + 1], f"in_proj_bwd_{l}",
                                                      _chip_comm([partial[l, 1], partial[l, 0]]))
            r2[l, 1], r2[l, 0] = got
        dconv[l] = dcw[:3]
        dsinks[l] = dsink[0, ::HEAD_DIM]
    parts = {key: (partial[key], r2[key]) for key in partial}
    small = _pack_small(jnp.concatenate(dg_mix), jnp.concatenate(dg_group), jnp.concatenate(dg_mlp),
                        dg_final, jnp.stack(dconv), jnp.stack(dsinks), loss_slab[0:1])
    return dx, parts, small


def _finish(dx, parts, small, pos, dev, w_in, conv_w, sinks, g_mix, g_group, w_o, g_mlp, w_ff_in, w_ff_out, g_final, m_w_in, m_conv_w, m_sinks, m_g_mix, m_g_group, m_w_o, m_g_mlp, m_w_ff_in, m_w_ff_out, m_g_final, v_w_in, v_conv_w, v_sinks, v_g_mix, v_g_group, v_w_o, v_g_mlp, v_w_ff_in, v_w_ff_out, v_g_final):
    grad_x = dx.reshape(1, SEQ, D_MODEL)

    (small_all,) = _comm_only(_gather_comm([(small[None], 0)]), "gather_small_grads")
    row = lambda t: t.reshape(1, D_MODEL)
    sink_row = lambda t: t.reshape(1, DEPTH * 6)
    params = [(MIX_ROW, g_mix, m_g_mix, v_g_mix), (GROUP_ROW, g_group, m_g_group, v_g_group),
              (MLP_ROW, g_mlp, m_g_mlp, v_g_mlp), (FINAL_ROW, row(g_final), row(m_g_final), row(v_g_final)),
              (SINK_ROW, sink_row(sinks), sink_row(m_sinks), sink_row(v_sinks))]
    updated, conv_rows, loss_row = _small_sum_adamw(small_all.reshape(N_DEV, SMALL_ROWS, D_MODEL), params, "small_adamw")
    loss = loss_row[0, 0]
    (grad_g_mix, delta_g_mix, new_m_g_mix, new_v_g_mix), (grad_g_group, delta_g_group, new_m_g_group, new_v_g_group), \
        (grad_g_mlp, delta_g_mlp, new_m_g_mlp, new_v_g_mlp), final4, sinks4 = updated
    grad_g_final, delta_g_final, new_m_g_final, new_v_g_final = [t.reshape(D_MODEL) for t in final4]
    grad_sinks, delta_sinks, new_m_sinks, new_v_sinks = [t.reshape(DEPTH, 2, 3) for t in sinks4]
    conv_grad_full = conv_rows[:DEPTH * 3, :CONV_CH].reshape(DEPTH, 3, CONV_CH)
    cs = CONV_CH // N_DEV
    grad_conv_w = lax.dynamic_slice_in_dim(conv_grad_full, dev * cs, cs, axis=2)

    tile_of = lambda t: t.reshape(1, DEPTH * 3 * cs)

    cd, cm, cv = _adamw(tile_of(conv_w), tile_of(grad_conv_w), tile_of(m_conv_w), tile_of(v_conv_w), "conv_adamw")
    untile = lambda t: t.reshape(DEPTH, 3, cs)
    delta_conv_w, new_m_conv_w, new_v_conv_w = untile(cd), untile(cm), untile(cv)

    def big(kind, w, m, v, transpose, name):
        return _sum_adamw([parts[l, kind] for l in range(DEPTH)], w, m, v, pos, transpose, name)

    swap = lambda t: jnp.swapaxes(t, 1, 2)
    grad_w_in, delta_w_in, new_m_w_in, new_v_w_in = [
        swap(t) for t in big(0, swap(w_in), swap(m_w_in), swap(v_w_in), False, "adamw_w_in")]
    grad_w_o, delta_w_o, new_m_w_o, new_v_w_o = big(1, w_o, m_w_o, v_w_o, False, "adamw_w_o")
    grad_w_ff_in, delta_w_ff_in, new_m_w_ff_in, new_v_w_ff_in = big(2, w_ff_in, m_w_ff_in, v_w_ff_in, True, "adamw_w_ff_in")
    grad_w_ff_out, delta_w_ff_out, new_m_w_ff_out, new_v_w_ff_out = big(3, w_ff_out, m_w_ff_out, v_w_ff_out, False,
                                                                         "adamw_w_ff_out")

    return (loss, grad_x, grad_w_in, grad_conv_w, grad_sinks, grad_g_mix, grad_g_group, grad_w_o, grad_g_mlp,
            grad_w_ff_in, grad_w_ff_out, grad_g_final,
            delta_w_in, delta_conv_w, delta_sinks, delta_g_mix, delta_g_group, delta_w_o, delta_g_mlp,
            delta_w_ff_in, delta_w_ff_out, delta_g_final,
            new_m_w_in, new_m_conv_w, new_m_sinks, new_m_g_mix, new_m_g_group, new_m_w_o, new_m_g_mlp,
            new_m_w_ff_in, new_m_w_ff_out, new_m_g_final,
            new_v_w_in, new_v_conv_w, new_v_sinks, new_v_g_mix, new_v_g_group, new_v_w_o, new_v_g_mlp,
            new_v_w_ff_in, new_v_w_ff_out, new_v_g_final)
```

```python
from typing import Callable, NamedTuple

import jax
import jax.numpy as jnp
from jax import lax
from jax.experimental import pallas as pl
from jax.experimental.pallas import tpu as pltpu

F32 = jnp.float32
BF16 = jnp.bfloat16
MESH = pl.DeviceIdType.MESH

N_DEV = 8
SEQ = 4096
D_MODEL = 1024
DEPTH = 2
HEAD_DIM = 64
LANES = 128
A_WIDTH = 384
CONV_CH = 256
C_WIDTH = 384
IN_WIDTH = 2560
BLOCK = 128
DILATED_PATTERNS = (1, 4, 16)
A_MAX_DIST = 128
C_MAX_DIST = 127
EPS = 1e-6
SCALE = HEAD_DIM ** -0.5
NEG_BIG = -1e30
F32_TINY = 1.1754944e-38

QA_BLK, KA_BLK, VA_BLK = 0, 3, 6
GB_BLK, GC_BLK, XB_BLK = 9, 11, 13
QC_BLK, KC_BLK, VC_BLK = 15, 18, 19

ADAM_LR = 0.001
ADAM_B1 = 0.9
ADAM_B2 = 0.999
ADAM_EPS = 1e-08
ADAM_WD = 0.01
ADAM_STEP = 10

VMEM_LIMIT = 56 * 1024 * 1024
TILE_BUDGET = 46 * 1024 * 1024
ROW_TILE = 512
COL_CHUNK = 512
SMALL_ROWS = 48


def _dot_nn(a, b):
    return lax.dot_general(a, b, (((1,), (0,)), ((), ())), preferred_element_type=F32)


def _dot_nt(a, b):
    return lax.dot_general(a, b, (((1,), (1,)), ((), ())), preferred_element_type=F32)


def _dot_tn(a, b):
    return lax.dot_general(a, b, (((0,), (0,)), ((), ())), preferred_element_type=F32)


def _params(*sem, collective_id=None):
    return pltpu.CompilerParams(dimension_semantics=sem, vmem_limit_bytes=VMEM_LIMIT, collective_id=collective_id)


def _resident(shape):
    return pl.BlockSpec(shape, lambda i: (0,) * len(shape), pipeline_mode=pl.Buffered(1))


class _Late(NamedTuple):
    hbm: object
    vmem: object
    sem: object

    def fetch(self, needed_at_step):
        copy = pltpu.make_async_copy(self.hbm, self.vmem, self.sem)
        pl.when(pl.program_id(0) == 0)(copy.start)
        return lambda: pl.when(pl.program_id(0) == needed_at_step)(copy.wait)


LATE_SPEC = pl.BlockSpec(memory_space=pl.ANY)


def _late_scratch(t):
    return [pltpu.VMEM(t.shape, t.dtype), pltpu.SemaphoreType.DMA(())]


def _row_tile(row_bytes, resident_bytes):
    for tm in (ROW_TILE, ROW_TILE // 2):
        if 2 * tm * row_bytes + resident_bytes <= TILE_BUDGET:
            return tm
    return ROW_TILE // 4


def _rms_scale(t):
    return lax.rsqrt(jnp.mean(t * t, axis=-1, keepdims=True) + EPS)


def _rms_bwd(n, r, dn):
    return r * (dn - n * jnp.mean(dn * n, axis=-1, keepdims=True))


class _Comm(NamedTuple):
    arrays: tuple
    out_shape: tuple
    sems: tuple
    start: Callable
    finish: Callable
    peers: Callable
    collective_id: int


def _handshake(comm):
    barrier = pltpu.get_barrier_semaphore()
    peers = comm.peers()
    for peer in peers:
        pl.semaphore_signal(barrier, inc=1, device_id=peer, device_id_type=MESH)
    pl.semaphore_wait(barrier, len(peers))


def _call(body, grid, in_specs, out_specs, out_shape, operands, name, scratch_shapes=(), comm=None, aliases=None):
    n_in, n_out, n_scr = len(in_specs), len(out_shape), len(scratch_shapes)
    aliases = dict(aliases or {})
    if comm is None:
        res = pl.pallas_call(body, grid=grid, in_specs=list(in_specs), out_specs=list(out_specs),
                             out_shape=list(out_shape), scratch_shapes=list(scratch_shapes),
                             input_output_aliases=aliases,
                             compiler_params=_params("arbitrary"), name=name)(*operands)
        return list(res), []
    c_in, c_out = len(comm.arrays), len(comm.out_shape)
    hbm = pl.BlockSpec(memory_space=pl.ANY)
    last = grid[0] - 1

    def carried(*refs):
        ins, cins = refs[:n_in], refs[n_in:n_in + c_in]
        o0 = n_in + c_in
        outs, couts = refs[o0:o0 + n_out], refs[o0 + n_out:o0 + n_out + c_out]
        s0 = o0 + n_out + c_out
        scr, sems = refs[s0:s0 + n_scr], refs[s0 + n_scr:]
        @pl.when(pl.program_id(0) == 0)
        def _():
            _handshake(comm)
            comm.start(cins, couts, sems)

        body(*ins, *outs, *scr)
        pl.when(pl.program_id(0) == last)(lambda: comm.finish(cins, couts, sems))

    res = pl.pallas_call(carried, grid=grid, in_specs=list(in_specs) + [hbm] * c_in,
                         out_specs=list(out_specs) + [hbm] * c_out, out_shape=list(out_shape) + list(comm.out_shape),
                         scratch_shapes=list(scratch_shapes) + list(comm.sems), input_output_aliases=aliases,
                         compiler_params=_params("arbitrary", collective_id=comm.collective_id),
                         name=name)(*operands, *comm.arrays)
    return list(res[:n_out]), list(res[n_out:])


def _comm_only(comm, name):
    hbm = pl.BlockSpec(memory_space=pl.ANY)
    c_in, c_out = len(comm.arrays), len(comm.out_shape)

    def body(*refs):
        ins, outs, sems = refs[:c_in], refs[c_in:c_in + c_out], refs[c_in + c_out:]
        _handshake(comm)
        comm.start(ins, outs, sems)
        comm.finish(ins, outs, sems)

    return pl.pallas_call(body, in_specs=[hbm] * c_in, out_specs=[hbm] * c_out, out_shape=list(comm.out_shape),
                          scratch_shapes=list(comm.sems),
                          compiler_params=pltpu.CompilerParams(collective_id=comm.collective_id),
                          name=name)(*comm.arrays)


def _norm_mm(x, g, wt, name, comm=None):
    s, d = x.shape
    n = wt.shape[0]
    tm = _row_tile(4 * d + 4 * n + 2 * d, 2 * n * d)

    def body(x_ref, g_ref, w_ref, o_ref, h_ref):
        xx = x_ref[...]
        h = ((xx * _rms_scale(xx)) * g_ref[...]).astype(BF16)
        h_ref[...] = h
        for n0 in range(0, n, COL_CHUNK):
            o_ref[:, n0:n0 + COL_CHUNK] = _dot_nt(h, w_ref[n0:n0 + COL_CHUNK, :])

    return _call(
        body,
        grid=(s // tm,),
        in_specs=[pl.BlockSpec((tm, d), lambda i: (i, 0)),
                  pl.BlockSpec((1, d), lambda i: (0, 0)),
                  _resident((n, d))],
        out_specs=[pl.BlockSpec((tm, n), lambda i: (i, 0)),
                   pl.BlockSpec((tm, d), lambda i: (i, 0))],
        out_shape=[jax.ShapeDtypeStruct((s, n), F32), jax.ShapeDtypeStruct((s, d), BF16)],
        operands=(x, g, wt), name=name, comm=comm)


def _ff_out_in_proj(a, w2, x1, g, wt, name, comm=None):
    s, f = a.shape
    d = w2.shape[1]
    n = wt.shape[0]
    tm = _row_tile(2 * f + 4 * d + 4 * d + 4 * n + 2 * d, 2 * f * d + 2 * n * d)

    def body(a_ref, w2_ref, x_ref, g_ref, w_ref, x2_ref, z_ref, h_ref):
        x2 = x_ref[...] + _dot_nn(a_ref[...], w2_ref[...])
        x2_ref[...] = x2
        h = ((x2 * _rms_scale(x2)) * g_ref[...]).astype(BF16)
        h_ref[...] = h
        for n0 in range(0, n, COL_CHUNK):
            z_ref[:, n0:n0 + COL_CHUNK] = _dot_nt(h, w_ref[n0:n0 + COL_CHUNK, :])

    rows = lambda w: pl.BlockSpec((tm, w), lambda i: (i, 0))
    return _call(
        body,
        grid=(s // tm,),
        in_specs=[rows(f), _resident((f, d)), rows(d), pl.BlockSpec((1, d), lambda i: (0, 0)), _resident((n, d))],
        out_specs=[rows(d), rows(n), rows(d)],
        out_shape=[jax.ShapeDtypeStruct((s, d), F32), jax.ShapeDtypeStruct((s, n), F32),
                   jax.ShapeDtypeStruct((s, d), BF16)],
        operands=(a, w2, x1, g, wt), name=name, comm=comm)


def _mix_ff_in(ya, yb, yc, gg, wo, x0, g_mlp, wt1, name, comm=None):
    s = ya.shape[0]
    d = wo.shape[1]
    f = wt1.shape[0]
    tm = _row_tile(4 * d + 4 * d + 2 * d + 4 * d + 2 * d + 2 * f, 2 * d * d + 2 * f * d)

    def body(ya_ref, yb_ref, yc_ref, gg_ref, wo_ref, x_ref, g_ref, w1_ref, y_ref, x1_ref, a_ref, h_ref):
        parts = []
        for ref in (ya_ref, yb_ref, yc_ref):
            t = ref[...]
            parts.append(t * _rms_scale(t))
        y = (jnp.concatenate(parts, axis=1) * gg_ref[...]).astype(BF16)
        y_ref[...] = y
        x1 = x_ref[...] + _dot_nn(y, wo_ref[...])
        x1_ref[...] = x1
        h = ((x1 * _rms_scale(x1)) * g_ref[...]).astype(BF16)
        h_ref[...] = h
        for n0 in range(0, f, COL_CHUNK):
            u = _dot_nt(h, w1_ref[n0:n0 + COL_CHUNK, :])
            a_ref[:, n0:n0 + COL_CHUNK] = jnp.square(jnp.maximum(u, 0.0)).astype(BF16)

    rows = lambda w: pl.BlockSpec((tm, w), lambda i: (i, 0))
    vec = pl.BlockSpec((1, d), lambda i: (0, 0))
    return _call(
        body,
        grid=(s // tm,),
        in_specs=[rows(A_WIDTH), rows(CONV_CH), rows(C_WIDTH), vec, _resident((d, d)), rows(d), vec, _resident((f, d))],
        out_specs=[rows(d), rows(d), rows(f), rows(d)],
        out_shape=[jax.ShapeDtypeStruct((s, d), BF16), jax.ShapeDtypeStruct((s, d), F32),
                   jax.ShapeDtypeStruct((s, f), BF16), jax.ShapeDtypeStruct((s, d), BF16)],
        operands=(ya, yb, yc, gg, wo, x0, g_mlp, wt1), name=name, comm=comm)


def _relu_from_square(av):
    return av * lax.rsqrt(jnp.maximum(av, F32_TINY))


def _mm_res_loss(a, w2, x1, g, target, name):
    s, f = a.shape
    d = w2.shape[1]
    tm = _row_tile(2 * f + 4 * d + 4 * d + 4 * d + 2 * d + 2 * f, 2 * f * d)

    def body(a_ref, w_ref, x_ref, g_ref, t_ref, loss_ref, dx_ref, dxb_ref, dg_ref, du_ref):
        @pl.when(pl.program_id(0) == 0)
        def _():
            loss_ref[...] = jnp.zeros_like(loss_ref)
            dg_ref[...] = jnp.zeros_like(dg_ref)

        xx = x_ref[...] + _dot_nn(a_ref[...], w_ref[...])
        r = _rms_scale(xx)
        n = xx * r
        gv = g_ref[...]
        err = n * gv - t_ref[...]
        per_tok = jnp.sum(err * err, axis=1, keepdims=True) * (1.0 / d)
        loss_ref[...] += 0.5 * jnp.sum(per_tok, axis=0, keepdims=True)
        dout = err * (1.0 / d)
        dg_ref[...] += jnp.sum(dout * n, axis=0, keepdims=True)
        dx = _rms_bwd(n, r, dout * gv)
        dx_ref[...] = dx
        dxb = dx.astype(BF16)
        dxb_ref[...] = dxb
        for n0 in range(0, f, COL_CHUNK):
            da = _dot_nt(dxb, w_ref[n0:n0 + COL_CHUNK, :])
            rl = _relu_from_square(a_ref[:, n0:n0 + COL_CHUNK].astype(F32))
            du_ref[:, n0:n0 + COL_CHUNK] = (da * (2.0 * rl)).astype(BF16)

    rows = lambda w: pl.BlockSpec((tm, w), lambda i: (i, 0))
    vec = pl.BlockSpec((1, d), lambda i: (0, 0))
    return pl.pallas_call(
        body,
        grid=(s // tm,),
        in_specs=[rows(f), _resident((f, d)), rows(d), vec, rows(d)],
        out_specs=[pl.BlockSpec((8, LANES), lambda i: (0, 0)), rows(d), rows(d), vec, rows(f)],
        out_shape=[jax.ShapeDtypeStruct((8, LANES), F32), jax.ShapeDtypeStruct((s, d), F32),
                   jax.ShapeDtypeStruct((s, d), BF16), jax.ShapeDtypeStruct((1, d), F32),
                   jax.ShapeDtypeStruct((s, f), BF16)],
        compiler_params=_params("arbitrary"),
        name=name,
    )(a, w2, x1, g, target)


def _mlp_bwd_act(dxb, w2, a, name, comm=None):
    s, d = dxb.shape
    f = w2.shape[0]
    tm = _row_tile(2 * d + 2 * f + 2 * f, 2 * f * d)

    def body(dx_ref, w_ref, a_ref, du_ref):
        dx = dx_ref[...]
        for n0 in range(0, f, COL_CHUNK):
            da = _dot_nt(dx, w_ref[n0:n0 + COL_CHUNK, :])
            rl = _relu_from_square(a_ref[:, n0:n0 + COL_CHUNK].astype(F32))
            du_ref[:, n0:n0 + COL_CHUNK] = (da * (2.0 * rl)).astype(BF16)

    return _call(
        body,
        grid=(s // tm,),
        in_specs=[pl.BlockSpec((tm, d), lambda i: (i, 0)),
                  _resident((f, d)),
                  pl.BlockSpec((tm, f), lambda i: (i, 0))],
        out_specs=[pl.BlockSpec((tm, f), lambda i: (i, 0))],
        out_shape=[jax.ShapeDtypeStruct((s, f), BF16)],
        operands=(dxb, w2, a), name=name, comm=comm)


def _mm_tn(pairs, name, comm=None):
    s, d = pairs[0][1].shape
    tn = 512
    tiles = [a.shape[1] // tn for a, _ in pairs]
    starts = [sum(tiles[:k]) for k in range(len(pairs))]

    def body(*refs):
        ins, outs = refs[:2 * len(pairs)], refs[2 * len(pairs):3 * len(pairs)]
        acc, late = refs[3 * len(pairs)], refs[3 * len(pairs) + 1:]
        j = pl.program_id(0)
        b_refs = [ins[1]]
        for k in range(1, len(pairs)):
            b_late = _Late(ins[2 * k + 1], late[2 * k - 2], late[2 * k - 1])
            b_late.fetch(starts[k])()
            b_refs.append(b_late.vmem)
        for k in range(len(pairs)):
            def run(a_ref=ins[2 * k], b_ref=b_refs[k], o_ref=outs[k]):
                for k0 in range(0, s, ROW_TILE):
                    part = _dot_tn(a_ref[k0:k0 + ROW_TILE, :], b_ref[k0:k0 + ROW_TILE, :])
                    if k0 == 0:
                        acc[...] = part
                    else:
                        acc[...] += part
                o_ref[...] = acc[...].astype(BF16)

            pl.when((j >= starts[k]) & (j < starts[k] + tiles[k]))(run)

    def tile_of(k):
        return lambda j: jnp.clip(j - starts[k], 0, tiles[k] - 1)

    in_specs, out_specs = [], []
    for k in range(len(pairs)):
        in_specs += [pl.BlockSpec((s, tn), lambda j, t=tile_of(k): (0, t(j))), _resident((s, d)) if k == 0 else LATE_SPEC]
        out_specs.append(pl.BlockSpec((tn, d), lambda j, t=tile_of(k): (t(j), 0)))
    late = [scratch for _, b in pairs[1:] for scratch in _late_scratch(b)]
    return _call(
        body,
        grid=(sum(tiles),),
        in_specs=in_specs,
        out_specs=out_specs,
        out_shape=[jax.ShapeDtypeStruct((a.shape[1], d), BF16) for a, _ in pairs],
        operands=tuple(t for pair in pairs for t in pair), name=name,
        scratch_shapes=[pltpu.VMEM((tn, d), F32)] + late, comm=comm)


def _mm_nn_normbwd(dact, wt, x, dres, g, name, comm=None):
    s, kdim = dact.shape
    d = wt.shape[1]
    tm = _row_tile(2 * kdim + 4 * d + 4 * d + 4 * d + 2 * d, 2 * kdim * d)

    def body(a_ref, w_ref, x_ref, r_ref, g_ref, o_ref, ob_ref, dg_ref):
        @pl.when(pl.program_id(0) == 0)
        def _():
            dg_ref[...] = jnp.zeros_like(dg_ref)

        dh = _dot_nn(a_ref[...], w_ref[...])
        xx = x_ref[...]
        r = _rms_scale(xx)
        n = xx * r
        dg_ref[...] += jnp.sum(dh * n, axis=0, keepdims=True)
        dx = r_ref[...] + _rms_bwd(n, r, dh * g_ref[...])
        o_ref[...] = dx
        ob_ref[...] = dx.astype(BF16)

    return _call(
        body,
        grid=(s // tm,),
        in_specs=[pl.BlockSpec((tm, kdim), lambda i: (i, 0)),
                  _resident((kdim, d)),
                  pl.BlockSpec((tm, d), lambda i: (i, 0)),
                  pl.BlockSpec((tm, d), lambda i: (i, 0)),
                  pl.BlockSpec((1, d), lambda i: (0, 0))],
        out_specs=[pl.BlockSpec((tm, d), lambda i: (i, 0)),
                   pl.BlockSpec((tm, d), lambda i: (i, 0)),
                   pl.BlockSpec((1, d), lambda i: (0, 0))],
        out_shape=[jax.ShapeDtypeStruct((s, d), F32), jax.ShapeDtypeStruct((s, d), BF16),
                   jax.ShapeDtypeStruct((1, d), F32)],
        operands=(dact, wt, x, dres, g), name=name, comm=comm)


def _ff_in_mix_bwd(du, wt1, x1, dres, g_mlp, wo, ya, yb, yc, gg, name, comm=None):
    s, f = du.shape
    d = wt1.shape[1]
    widths = (A_WIDTH, CONV_CH, C_WIDTH)
    tm = _row_tile(2 * f + 4 * d + 4 * d + 4 * d + 2 * d + 4 * d + 4 * d, 2 * f * d + 2 * d * d)

    def body(du_ref, w1_ref, x_ref, r_ref, g_ref, wo_ref, ya_ref, yb_ref, yc_ref, gg_ref,
             dx_ref, dxb_ref, dg_ref, da_ref, db_ref, dc_ref, dgg_ref):
        @pl.when(pl.program_id(0) == 0)
        def _():
            dg_ref[...] = jnp.zeros_like(dg_ref)
            dgg_ref[...] = jnp.zeros_like(dgg_ref)

        dh = _dot_nn(du_ref[...], w1_ref[...])
        xx = x_ref[...]
        r = _rms_scale(xx)
        n = xx * r
        dg_ref[...] += jnp.sum(dh * n, axis=0, keepdims=True)
        dx = r_ref[...] + _rms_bwd(n, r, dh * g_ref[...])
        dx_ref[...] = dx
        dxb = dx.astype(BF16)
        dxb_ref[...] = dxb

        dy = _dot_nt(dxb, wo_ref[...])
        gv = gg_ref[...]
        off = 0
        dgs = []
        for ref, out, w in zip((ya_ref, yb_ref, yc_ref), (da_ref, db_ref, dc_ref), widths):
            t = ref[...]
            r = _rms_scale(t)
            n = t * r
            dyg = dy[:, off:off + w]
            dgs.append(jnp.sum(dyg * n, axis=0, keepdims=True))
            out[...] = _rms_bwd(n, r, dyg * gv[:, off:off + w])
            off += w
        dgg_ref[...] += jnp.concatenate(dgs, axis=1)

    rows = lambda w: pl.BlockSpec((tm, w), lambda i: (i, 0))
    vec = pl.BlockSpec((1, d), lambda i: (0, 0))
    return _call(
        body,
        grid=(s // tm,),
        in_specs=[rows(f), _resident((f, d)), rows(d), rows(d), vec, _resident((d, d)),
                  rows(A_WIDTH), rows(CONV_CH), rows(C_WIDTH), vec],
        out_specs=[rows(d), rows(d), vec, rows(A_WIDTH), rows(CONV_CH), rows(C_WIDTH), vec],
        out_shape=[jax.ShapeDtypeStruct((s, d), F32), jax.ShapeDtypeStruct((s, d), BF16), jax.ShapeDtypeStruct((1, d), F32),
                   jax.ShapeDtypeStruct((s, A_WIDTH), F32), jax.ShapeDtypeStruct((s, CONV_CH), F32),
                   jax.ShapeDtypeStruct((s, C_WIDTH), F32), jax.ShapeDtypeStruct((1, d), F32)],
        operands=(du, wt1, x1, dres, g_mlp, wo, ya, yb, yc, gg), name=name, comm=comm)


CONV_CHUNK = 256
CONV_HALO = 8


def _conv_fwd(z, cw, name):
    s = z.shape[0]
    nch = s // CONV_CHUNK

    def body(gb_ref, gc_ref, xb_ref, w_ref, o_ref, us):
        us[pl.ds(0, CONV_HALO), :] = jnp.zeros((CONV_HALO, LANES), F32)
        us[pl.ds(CONV_HALO, s), :] = gc_ref[...] * xb_ref[...]
        w0, w1, w2 = w_ref[0:1, :], w_ref[1:2, :], w_ref[2:3, :]

        def chunk(c, carry):
            st = pl.multiple_of(c * CONV_CHUNK, CONV_CHUNK)
            ext = us[pl.ds(st, CONV_CHUNK + CONV_HALO), :]
            y = (w0 * ext[CONV_HALO - 2:CONV_HALO - 2 + CONV_CHUNK]
                 + w1 * ext[CONV_HALO - 1:CONV_HALO - 1 + CONV_CHUNK]
                 + w2 * ext[CONV_HALO:])
            o_ref[pl.ds(st, CONV_CHUNK), :] = gb_ref[pl.ds(st, CONV_CHUNK), :] * y
            return carry

        lax.fori_loop(0, nch, chunk, 0)

    col = lambda blk: pl.BlockSpec((s, LANES), lambda j, blk=blk: (0, blk + j))
    return pl.pallas_call(
        body,
        grid=(CONV_CH // LANES,),
        in_specs=[col(GB_BLK), col(GC_BLK), col(XB_BLK), pl.BlockSpec((3, LANES), lambda j: (0, j))],
        out_specs=pl.BlockSpec((s, LANES), lambda j: (0, j)),
        out_shape=jax.ShapeDtypeStruct((s, CONV_CH), F32),
        scratch_shapes=[pltpu.VMEM((s + CONV_HALO, LANES), F32)],
        compiler_params=_params("parallel"),
        name=name,
    )(z, z, z, cw)


def _conv_bwd(z, cw, dyb, dz, name):
    s = z.shape[0]
    nch = s // CONV_CHUNK
    ncol = CONV_CH // LANES

    def body(gb_ref, gc_ref, xb_ref, w_ref, dy_ref, dz_in, dz_ref, dw_ref, us, ds_, dgb_ref, dgc_ref, dxb_ref, sems):
        j = pl.program_id(0)

        def to_dz(staged, blk, k):
            cols = pl.ds(pl.multiple_of((blk + j) * LANES, LANES), LANES)
            return pltpu.make_async_copy(staged, dz_ref.at[:, cols], sems.at[k])

        copies = [to_dz(dgb_ref, GB_BLK, 0), to_dz(dgc_ref, GC_BLK, 1), to_dz(dxb_ref, XB_BLK, 2)]

        @pl.when(j > 0)
        def _():
            for cp in copies:
                cp.wait()

        us[pl.ds(0, CONV_HALO), :] = jnp.zeros((CONV_HALO, LANES), F32)
        us[pl.ds(CONV_HALO, s), :] = gc_ref[...] * xb_ref[...]
        ds_[pl.ds(s, CONV_HALO), :] = jnp.zeros((CONV_HALO, LANES), F32)
        ds_[pl.ds(0, s), :] = dy_ref[...] * gb_ref[...]
        w0, w1, w2 = w_ref[0:1, :], w_ref[1:2, :], w_ref[2:3, :]
        zero = jnp.zeros((1, LANES), F32)

        def chunk(c, carry):
            a0, a1, a2 = carry
            st = pl.multiple_of(c * CONV_CHUNK, CONV_CHUNK)
            rows = pl.ds(st, CONV_CHUNK)
            ext = us[pl.ds(st, CONV_CHUNK + CONV_HALO), :]
            um2 = ext[CONV_HALO - 2:CONV_HALO - 2 + CONV_CHUNK]
            um1 = ext[CONV_HALO - 1:CONV_HALO - 1 + CONV_CHUNK]
            u0 = ext[CONV_HALO:]
            dext = ds_[pl.ds(st, CONV_CHUNK + CONV_HALO), :]
            dc0 = dext[:CONV_CHUNK]
            du = w2 * dc0 + w1 * dext[1:1 + CONV_CHUNK] + w0 * dext[2:2 + CONV_CHUNK]
            yconv = w0 * um2 + w1 * um1 + w2 * u0
            dgb_ref[rows, :] = (dy_ref[rows, :] * yconv).astype(BF16)
            dgc_ref[rows, :] = (du * xb_ref[rows, :]).astype(BF16)
            dxb_ref[rows, :] = (du * gc_ref[rows, :]).astype(BF16)
            a0 = a0 + jnp.sum(dc0 * um2, axis=0, keepdims=True)
            a1 = a1 + jnp.sum(dc0 * um1, axis=0, keepdims=True)
            a2 = a2 + jnp.sum(dc0 * u0, axis=0, keepdims=True)
            return a0, a1, a2

        a0, a1, a2 = lax.fori_loop(0, nch, chunk, (zero, zero, zero))
        dw_ref[...] = jnp.concatenate([a0, a1, a2, jnp.zeros((5, LANES), F32)], axis=0)
        for cp in copies:
            cp.start()

        @pl.when(j == ncol - 1)
        def _():
            for cp in copies:
                cp.wait()

    col = lambda blk: pl.BlockSpec((s, LANES), lambda j, blk=blk: (0, blk + j))
    hbm = pl.BlockSpec(memory_space=pl.ANY)
    return pl.pallas_call(
        body,
        grid=(ncol,),
        in_specs=[col(GB_BLK), col(GC_BLK), col(XB_BLK), pl.BlockSpec((3, LANES), lambda j: (0, j)),
                  pl.BlockSpec((s, LANES), lambda j: (0, j)), hbm],
        out_specs=[hbm, pl.BlockSpec((8, LANES), lambda j: (0, j))],
        out_shape=[jax.ShapeDtypeStruct(dz.shape, dz.dtype), jax.ShapeDtypeStruct((8, CONV_CH), F32)],
        scratch_shapes=[pltpu.VMEM((s + CONV_HALO, LANES), F32), pltpu.VMEM((s + CONV_HALO, LANES), F32)]
        + [pltpu.VMEM((s, LANES), BF16)] * 3 + [pltpu.SemaphoreType.DMA((3,))],
        input_output_aliases={5: 0},
        compiler_params=_params("arbitrary"),
        name=name,
    )(z, z, z, cw, dyb, dz)


ATTN_ROWS = 512
ATTN_UNROLL = 8


def _band_rows(b, d, r):
    base = pl.multiple_of(b * (BLOCK * d), BLOCK)
    prev = jnp.maximum(base - BLOCK * d, 0)
    if d == 1:
        return pl.ds(base, BLOCK), pl.ds(pl.multiple_of(prev, BLOCK), BLOCK)
    return pl.ds(base + r, BLOCK, stride=d), pl.ds(prev + r, BLOCK, stride=d)


def _write_band_bias(bias_ref, max_dist):
    qi = lax.broadcasted_iota(jnp.int32, (BLOCK, 2 * BLOCK), 0)
    kj = lax.broadcasted_iota(jnp.int32, (BLOCK, 2 * BLOCK), 1)
    dist = BLOCK + qi - kj
    band = (dist >= 0) & (dist <= max_dist)
    bias_ref[0:BLOCK, :] = jnp.where(band, 0.0, -jnp.inf)
    bias_ref[BLOCK:2 * BLOCK, :] = jnp.where(band & (kj >= BLOCK), 0.0, -jnp.inf)


def _band_bias(bias_ref, b):
    bias = bias_ref[pl.ds(pl.multiple_of(jnp.where(b > 0, 0, BLOCK), BLOCK), BLOCK), :]
    return jnp.concatenate([bias, bias], axis=0)


def _kv_halves(pair):
    zero = jnp.zeros((1, LANES), jnp.int32)
    return zero + (pair >> 1), zero + ((pair + 1) >> 1)


def _stack_heads(t, head0, halves=None):
    top, bottom = jnp.where(head0, t, 0.0), jnp.where(head0, 0.0, t)
    if halves is not None:
        top = jnp.where(halves[0] == 1, pltpu.roll(top, HEAD_DIM, 1), top)
        bottom = jnp.where(halves[1] == 0, pltpu.roll(bottom, HEAD_DIM, 1), bottom)
    return jnp.concatenate([top, bottom], axis=0).astype(BF16)


def _unstack_heads(t, head0, halves=None):
    top, bottom = t[:BLOCK], t[BLOCK:]
    if halves is not None:
        top = jnp.where(halves[0] == 1, pltpu.roll(top, HEAD_DIM, 1), top)
        bottom = jnp.where(halves[1] == 0, pltpu.roll(bottom, HEAD_DIM, 1), bottom)
    return jnp.where(head0, top, bottom)


def _block_loops(s, patterns, unroll, one_block):
    for n, d in enumerate(patterns):
        nb = (s // BLOCK) // d
        ur = min(unroll, d)
        ub = unroll // ur
        for r0 in range(0, d, ur):
            def trip(i, carry, n=n, d=d, r0=r0, ur=ur, ub=ub):
                for u in range(ub):
                    for r in range(r0, r0 + ur):
                        one_block(i * ub + u, d, r, n == 0)
                return carry
            lax.fori_loop(0, nb // ub, trip, 0)


def _attn_fwd(z, m_init, l_init, q_blk, k_blk, v_blk, patterns, max_dist, gqa, name, comm=None):
    s = z.shape[0]
    npair = 3

    def body(q_ref, k_ref, v_ref, mi_ref, o_ref, lse0_ref, lse1_ref, bias_scr, m_scr, l_scr, *kv_scr):
        head0 = lax.broadcasted_iota(jnp.int32, (1, LANES), 1) < HEAD_DIM
        _write_band_bias(bias_scr, max_dist)
        ones = jnp.ones((2 * BLOCK, LANES), BF16)
        k_src, v_src = kv_scr if gqa else (k_ref, v_ref)
        if gqa:
            half = (lax.broadcasted_iota(jnp.int32, (1, LANES), 1) >= HEAD_DIM).astype(jnp.int32)
            swap = ((pl.program_id(0) + half) >> 1) != half

            def expand(c, carry):
                rows = pl.ds(pl.multiple_of(c * ATTN_ROWS, ATTN_ROWS), ATTN_ROWS)
                k_src[rows, :] = jnp.where(swap, pltpu.roll(k_ref[rows, :], HEAD_DIM, 1), k_ref[rows, :])
                v_src[rows, :] = jnp.where(swap, pltpu.roll(v_ref[rows, :], HEAD_DIM, 1), v_ref[rows, :])
                return carry

            lax.fori_loop(0, s // ATTN_ROWS, expand, 0)

        def one_block(b, d, r, first):
            rq, rp = _band_rows(b, d, r)
            q2 = _stack_heads(q_ref[rq, :] * SCALE, head0)
            k2 = jnp.concatenate([k_src[rp, :], k_src[rq, :]], axis=0).astype(BF16)
            v2 = jnp.concatenate([v_src[rp, :], v_src[rq, :]], axis=0).astype(BF16)
            sc = _dot_nt(q2, k2) + _band_bias(bias_scr, b)
            mb = jnp.max(sc, axis=1, keepdims=True)
            p = jnp.exp(sc - mb).astype(BF16)
            ob = _dot_nn(p, jnp.concatenate([v2, ones], axis=1))
            m_blk = _unstack_heads(jnp.broadcast_to(mb, (2 * BLOCK, LANES)), head0)
            l_blk = _unstack_heads(ob[:, LANES:], head0)
            o_blk = _unstack_heads(ob[:, :LANES], head0)
            if first and l_init == 0.0:
                m_new, l_new, o_new = m_blk, l_blk, o_blk
            else:
                if first:
                    m_old, l_old, o_old = jnp.broadcast_to(mi_ref[...], (BLOCK, LANES)), l_init, 0.0
                else:
                    m_old, l_old, o_old = m_scr[rq, :], l_scr[rq, :], o_ref[rq, :]
                m_new = jnp.maximum(m_old, m_blk)
                a_old = jnp.exp(m_old - m_new)
                a_blk = jnp.exp(m_blk - m_new)
                l_new = l_old * a_old + l_blk * a_blk
                o_new = o_old * a_old + o_blk * a_blk
            o_ref[rq, :], l_scr[rq, :], m_scr[rq, :] = o_new, l_new, m_new

        _block_loops(s, patterns, ATTN_UNROLL, one_block)

        def fin(c, carry):
            rows = pl.ds(pl.multiple_of(c * ATTN_ROWS, ATTN_ROWS), ATTN_ROWS)
            l = l_scr[rows, :]
            o_ref[rows, :] = o_ref[rows, :] / l
            lse = m_scr[rows, :] + jnp.log(l)
            swapped = pltpu.roll(lse, HEAD_DIM, 1)
            lse0_ref[rows, :] = jnp.where(head0, lse, swapped)
            lse1_ref[rows, :] = jnp.where(head0, swapped, lse)
            return carry

        lax.fori_loop(0, s // ATTN_ROWS, fin, 0)

    kv = (lambda blk: pl.BlockSpec((s, LANES), lambda j, blk=blk: (0, blk), pipeline_mode=pl.Buffered(1))) if gqa \
        else (lambda blk: pl.BlockSpec((s, LANES), lambda j, blk=blk: (0, blk + j)))
    own = pl.BlockSpec((s, LANES), lambda j: (0, j))
    return _call(
        body,
        grid=(npair,),
        in_specs=[pl.BlockSpec((s, LANES), lambda j: (0, q_blk + j)), kv(k_blk), kv(v_blk),
                  pl.BlockSpec((1, LANES), lambda j: (0, j))],
        out_specs=[own, own, own],
        out_shape=[jax.ShapeDtypeStruct((s, npair * LANES), F32)] * 3,
        operands=(z, z, z, m_init), name=name,
        scratch_shapes=[pltpu.VMEM((2 * BLOCK, 2 * BLOCK), F32)] + [pltpu.VMEM((s, LANES), F32)] * (4 if gqa else 2),
        comm=comm)


def _attn_bwd(z, do, o, lse, m_init, dz, q_blk, k_blk, v_blk, patterns, max_dist, gqa, name, comm=None):
    s = z.shape[0]
    npair = 3
    n_dz_in = 0 if dz is None else 1

    def body(q_ref, k_ref, v_ref, do_ref, o_ref, lse0_ref, lse1_ref, mi_ref, *rest):
        (dz_ref, dm_ref, dq_acc, dk_acc, dv_acc, dl0_scr, dl1_scr, bias_scr,
         dq_out, dk_out, dv_out, out_sems) = rest[n_dz_in:]
        pair = pl.program_id(0)
        head0 = lax.broadcasted_iota(jnp.int32, (1, LANES), 1) < HEAD_DIM
        halves = _kv_halves(pair) if gqa else None
        _write_band_bias(bias_scr, max_dist)

        def zero_kv():
            def f(c, carry):
                rows = pl.ds(pl.multiple_of(c * ATTN_ROWS, ATTN_ROWS), ATTN_ROWS)
                dk_acc[rows, :] = jnp.zeros((ATTN_ROWS, LANES), F32)
                dv_acc[rows, :] = jnp.zeros((ATTN_ROWS, LANES), F32)
                return carry
            lax.fori_loop(0, s // ATTN_ROWS, f, 0)

        if gqa:
            pl.when(pair == 0)(zero_kv)
        else:
            zero_kv()

        def prep(c, dm):
            rows = pl.ds(pl.multiple_of(c * ATTN_ROWS, ATTN_ROWS), ATTN_ROWS)
            dq_acc[rows, :] = jnp.zeros((ATTN_ROWS, LANES), F32)
            prod = do_ref[rows, :] * o_ref[rows, :]
            d0 = jnp.sum(jnp.where(head0, prod, 0.0), axis=1, keepdims=True)
            d1 = jnp.sum(jnp.where(head0, 0.0, prod), axis=1, keepdims=True)
            dl0_scr[rows, :] = jnp.broadcast_to(d0, (ATTN_ROWS, LANES))
            dl1_scr[rows, :] = jnp.broadcast_to(d1, (ATTN_ROWS, LANES))
            lse_own = jnp.where(head0, lse0_ref[rows, :], lse1_ref[rows, :])
            psink = jnp.exp(mi_ref[...] - lse_own)
            return dm - jnp.sum(psink * jnp.where(head0, d0, d1), axis=0, keepdims=True)

        dm_ref[...] = lax.fori_loop(0, s // ATTN_ROWS, prep, jnp.zeros((1, LANES), F32))

        def one_block(b, d, r, first):
            rq, rp = _band_rows(b, d, r)
            q2 = _stack_heads(q_ref[rq, :] * SCALE, head0, halves)
            do2 = _stack_heads(do_ref[rq, :], head0, halves)
            k2 = jnp.concatenate([k_ref[rp, :], k_ref[rq, :]], axis=0).astype(BF16)
            v2 = jnp.concatenate([v_ref[rp, :], v_ref[rq, :]], axis=0).astype(BF16)
            lse2 = jnp.concatenate([lse0_ref[rq, :], lse1_ref[rq, :]], axis=0)
            dl2 = jnp.concatenate([dl0_scr[rq, :], dl1_scr[rq, :]], axis=0)
            lse2 = jnp.concatenate([lse2, lse2], axis=1)
            dl2 = jnp.concatenate([dl2, dl2], axis=1)
            p = jnp.exp(_dot_nt(q2, k2) + _band_bias(bias_scr, b) - lse2)
            dp = _dot_nt(do2, v2)
            dsc = (p * (dp - dl2)).astype(BF16)
            dq2 = _unstack_heads(_dot_nn(dsc, k2), head0, halves)
            dk2 = _dot_tn(dsc, q2)
            dv2 = _dot_tn(p.astype(BF16), do2)
            dq_acc[rq, :] += dq2 * SCALE
            dk_acc[rp, :] += dk2[:BLOCK]
            dk_acc[rq, :] += dk2[BLOCK:]
            dv_acc[rp, :] += dv2[:BLOCK]
            dv_acc[rq, :] += dv2[BLOCK:]

        _block_loops(s, patterns, ATTN_UNROLL, one_block)

        def to_dz(staged, blk, k):
            cols = pl.ds(pl.multiple_of(blk * LANES, LANES), LANES)
            return pltpu.make_async_copy(staged, dz_ref.at[:, cols], out_sems.at[k])

        last_pair = pair == npair - 1
        q_copy = to_dz(dq_out, q_blk + pair, 0)
        kv_copies = [to_dz(dk_out, k_blk + (0 if gqa else pair), 1), to_dz(dv_out, v_blk + (0 if gqa else pair), 2)]

        @pl.when(pair > 0)
        def _():
            for cp in [q_copy] + ([] if gqa else kv_copies):
                cp.wait()

        def stage(acc, out):
            def f(c, carry):
                rows = pl.ds(pl.multiple_of(c * ATTN_ROWS, ATTN_ROWS), ATTN_ROWS)
                out[rows, :] = acc[rows, :].astype(BF16)
                return carry
            lax.fori_loop(0, s // ATTN_ROWS, f, 0)

        def stage_kv():
            stage(dk_acc, dk_out)
            stage(dv_acc, dv_out)
            for cp in kv_copies:
                cp.start()

        stage(dq_acc, dq_out)
        q_copy.start()
        if gqa:
            pl.when(last_pair)(stage_kv)
        else:
            stage_kv()

        @pl.when(last_pair)
        def _():
            for cp in [q_copy] + kv_copies:
                cp.wait()

    own = pl.BlockSpec((s, LANES), lambda j: (0, j))
    hbm = pl.BlockSpec(memory_space=pl.ANY)
    if gqa:
        kv = lambda blk: pl.BlockSpec((s, LANES), lambda j, blk=blk: (0, blk), pipeline_mode=pl.Buffered(1))
    else:
        kv = lambda blk: pl.BlockSpec((s, LANES), lambda j, blk=blk: (0, blk + j))
    in_specs = [pl.BlockSpec((s, LANES), lambda j: (0, q_blk + j)), kv(k_blk), kv(v_blk), own, own, own, own,
                pl.BlockSpec((1, LANES), lambda j: (0, j))]
    operands = (z, z, z, do, o, lse[0], lse[1], m_init)
    return _call(
        body,
        grid=(npair,),
        in_specs=in_specs + [hbm] * n_dz_in,
        out_specs=[hbm, pl.BlockSpec((1, LANES), lambda j: (0, j))],
        out_shape=[jax.ShapeDtypeStruct((s, IN_WIDTH), BF16), jax.ShapeDtypeStruct((1, npair * LANES), F32)],
        operands=operands + (() if dz is None else (dz,)), name=name,
        scratch_shapes=[pltpu.VMEM((s, LANES), F32)] * 5 + [pltpu.VMEM((2 * BLOCK, 2 * BLOCK), F32)]
        + [pltpu.VMEM((s, LANES), BF16)] * 3 + [pltpu.SemaphoreType.DMA((3,))],
        comm=comm, aliases={} if dz is None else {len(in_specs): 0})


def _adamw_math(w, g, m, v):
    m = ADAM_B1 * m + (1.0 - ADAM_B1) * g
    v = ADAM_B2 * v + (1.0 - ADAM_B2) * (g * g)
    m_hat = m / (1.0 - ADAM_B1 ** ADAM_STEP)
    v_hat = v / (1.0 - ADAM_B2 ** ADAM_STEP)
    delta = -ADAM_LR * (m_hat / (jnp.sqrt(v_hat) + ADAM_EPS) + ADAM_WD * w)
    return delta, m, v


def _adamw(w, g, m, v, name):
    rows, cols = w.shape
    tr = min(rows, 256)

    def body(w_ref, g_ref, m_ref, v_ref, d_ref, nm_ref, nv_ref):
        d_ref[...], nm_ref[...], nv_ref[...] = _adamw_math(w_ref[...], g_ref[...], m_ref[...], v_ref[...])

    spec = pl.BlockSpec((tr, cols), lambda i: (i, 0))
    return pl.pallas_call(
        body,
        grid=(rows // tr,),
        in_specs=[spec] * 4,
        out_specs=[spec] * 3,
        out_shape=[jax.ShapeDtypeStruct((rows, cols), F32)] * 3,
        compiler_params=_params("parallel"),
        name=name,
    )(w, g, m, v)


def _sum_adamw(parts, w, m, v, pos, transpose, name):
    assert len(parts) == DEPTH == 2
    (p0, r0), (p1, r1) = parts
    _, rows, cols = p0.shape
    tr = 256 if rows % 256 == 0 else rows
    nt = rows // tr

    def body(pos_ref, p0_ref, r0_ref, p1_ref, r1_ref, w_ref, m_ref, v_ref, g_ref, d_ref, nm_ref, nv_ref):
        def run(p_ref, r_ref):
            g = ((p_ref[...].astype(F32) + r_ref[0].astype(F32)) + r_ref[1].astype(F32)) + r_ref[2].astype(F32)
            if transpose:
                g = g.T
            g_ref[...] = g
            d_ref[...], nm_ref[...], nv_ref[...] = _adamw_math(w_ref[...], g, m_ref[...], v_ref[...])

        layer0 = pl.program_id(0) < nt
        pl.when(layer0)(lambda: run(p0_ref, r0_ref))
        pl.when(jnp.logical_not(layer0))(lambda: run(p1_ref, r1_ref))

    def tile0(i):
        return jnp.minimum(i, nt - 1)

    def tile1(i):
        return jnp.maximum(i - nt, 0)

    if transpose:
        w_spec = pl.BlockSpec((None, cols, tr), lambda i, q: (i // nt, 0, i % nt))
    else:
        w_spec = pl.BlockSpec((None, tr, cols), lambda i, q: (i // nt, i % nt, 0))
    return pl.pallas_call(
        body,
        grid_spec=pltpu.PrefetchScalarGridSpec(
            num_scalar_prefetch=1,
            grid=(DEPTH * nt,),
            in_specs=[pl.BlockSpec((None, tr, cols), lambda i, q: (q[0], tile0(i), 0)),
                      pl.BlockSpec((3, tr, cols), lambda i, q: (0, tile0(i), 0)),
                      pl.BlockSpec((None, tr, cols), lambda i, q: (q[0], tile1(i), 0)),
                      pl.BlockSpec((3, tr, cols), lambda i, q: (0, tile1(i), 0)),
                      w_spec, w_spec, w_spec],
            out_specs=[w_spec] * 4,
        ),
        out_shape=[jax.ShapeDtypeStruct(w.shape, F32)] * 4,
        compiler_params=_params("arbitrary"),
        name=name,
    )(pos, p0, r0, p1, r1, w, m, v)


def _small_sum_adamw(gathered, params, name):
    _, rows, cols = gathered.shape
    n = len(params)

    def body(ga_ref, *refs):
        ins, outs, (g_scr,) = refs[:3 * n], refs[3 * n:7 * n + 2], refs[7 * n + 2:]
        g = ga_ref[0]
        for i in range(1, N_DEV):
            g = g + ga_ref[i]
        g_scr[...] = g
        for k, (row0, w, _, _) in enumerate(params):
            w_ref, m_ref, v_ref = ins[3 * k:3 * k + 3]
            gk = g_scr[row0:row0 + w.shape[0], :]
            outs[4 * k][...] = gk
            outs[4 * k + 1][...], outs[4 * k + 2][...], outs[4 * k + 3][...] = _adamw_math(
                w_ref[...], gk, m_ref[...], v_ref[...])
        outs[4 * n][...] = g_scr[CONV_ROW:CONV_ROW + 8, :]
        outs[4 * n + 1][...] = g_scr[LOSS_ROW:LOSS_ROW + 1, :]

    out_shape = []
    for _, w, _, _ in params:
        out_shape += [jax.ShapeDtypeStruct(w.shape, F32)] * 4
    out_shape += [jax.ShapeDtypeStruct((8, cols), F32), jax.ShapeDtypeStruct((1, cols), F32)]
    res = pl.pallas_call(
        body,
        out_shape=out_shape,
        scratch_shapes=[pltpu.VMEM((rows, cols), F32)],
        name=name,
    )(gathered, *[t for _, w, m, v in params for t in (w, m, v)])
    return [res[4 * k:4 * k + 4] for k in range(n)], res[4 * n], res[4 * n + 1]


def _pair_sums(g4s, r1s, pos, name):
    n = len(g4s)

    def body(pos_ref, *refs):
        for g_ref, r_ref, o_ref in zip(refs[:n], refs[n:2 * n], refs[2 * n:]):
            o_ref[...] = (g_ref[...].astype(F32) + r_ref[...].astype(F32)).astype(BF16)

    block = lambda t: pl.BlockSpec((None,) + t.shape[1:], lambda i, p: (i, 0, 0))
    return pl.pallas_call(
        body,
        grid_spec=pltpu.PrefetchScalarGridSpec(
            num_scalar_prefetch=1,
            grid=(4,),
            in_specs=[pl.BlockSpec((None, None) + g.shape[2:], lambda i, p: (i, p[1], 0, 0)) for g in g4s]
            + [block(r) for r in r1s],
            out_specs=[block(r) for r in r1s],
        ),
        out_shape=[jax.ShapeDtypeStruct(r.shape, BF16) for r in r1s],
        compiler_params=_params("parallel"),
        name=name,
    )(pos, *g4s, *r1s)


GATHER_ID, CHIP_ID, SIBLING_ID = 0, 1, 2


def _place():
    return lax.axis_index("x"), lax.axis_index("y"), lax.axis_index("c")


def _sibling():
    x, y, c = _place()
    return (x, y, 1 - c)


def _same_core_of_other_chips():
    x, y, c = _place()
    return [(1 - x, y, c), (x, 1 - y, c), (1 - x, 1 - y, c)]


def _gather_comm(shards):
    na = len(shards)
    stacks, index = zip(*shards)

    def plan(ins, outs, sems):
        send_sems, recv_sems, local_sems = sems
        x, y, c = _place()
        me, sibling = (x, y, c), (x, y, 1 - c)
        chips = [(1 - x, y), (x, 1 - y), (1 - x, 1 - y)]
        shard = [ins[a].at[index[a]] for a in range(na)]

        def rows(a, px, py, pc):
            m = shard[a].shape[0]
            return outs[a].at[pl.ds((4 * px + 2 * py + pc) * m, m), :]

        def copy(a, k, block, to, src=None):
            return pltpu.make_async_remote_copy(
                src_ref=rows(a, *block) if src is None else src, dst_ref=rows(a, *block),
                send_sem=send_sems.at[a, k], recv_sem=recv_sems.at[a, k], device_id=to, device_id_type=MESH)

        mine = [pltpu.make_async_copy(shard[a], rows(a, *me), local_sems.at[a]) for a in range(na)]
        first = []
        for a in range(na):
            first.append(copy(a, 0, me, sibling, src=shard[a]))
            first += [copy(a, 1 + j, me, (*chip, c), src=shard[a]) for j, chip in enumerate(chips)]
        return me, sibling, chips, c, copy, mine, first

    def start(ins, outs, sems):
        *_, mine, first = plan(ins, outs, sems)
        for cp in mine + first:
            cp.start()

    def finish(ins, outs, sems):
        me, sibling, chips, c, copy, mine, first = plan(ins, outs, sems)
        passed = []
        for j, chip in enumerate(chips):
            for a in range(na):
                copy(a, 1 + j, (*chip, c), me).wait_recv()
                cp = copy(a, 4 + j, (*chip, c), sibling)
                cp.start()
                passed.append(cp)
        for a in range(na):
            copy(a, 0, sibling, me).wait_recv()
            for j, chip in enumerate(chips):
                copy(a, 4 + j, (*chip, 1 - c), me).wait_recv()
        for cp in first + passed:
            cp.wait_send()
        for cp in mine:
            cp.wait()

    return _Comm(tuple(stacks),
                 tuple(jax.ShapeDtypeStruct((N_DEV * t.shape[1], t.shape[2]), t.dtype) for t in stacks),
                 (pltpu.SemaphoreType.DMA((na, 7)), pltpu.SemaphoreType.DMA((na, 7)), pltpu.SemaphoreType.DMA((na,))),
                 start, finish, lambda: [_sibling()] + _same_core_of_other_chips(), GATHER_ID)


def _exchange_comm(arrays, out_shape, n_copies, copies_of, peers, collective_id):
    na = len(arrays)

    def every(ins, outs, sems):
        send_sems, recv_sems = sems
        return [cp for a in range(na) for cp in copies_of(ins, outs, a, send_sems, recv_sems)]

    def start(ins, outs, sems):
        for cp in every(ins, outs, sems):
            cp.start()

    def finish(ins, outs, sems):
        for cp in every(ins, outs, sems):
            cp.wait()

    return _Comm(tuple(arrays), tuple(out_shape),
                 (pltpu.SemaphoreType.DMA((na, n_copies)), pltpu.SemaphoreType.DMA((na, n_copies))), start, finish,
                 peers, collective_id)


def _sibling_comm(grads):
    def copies_of(ins, outs, a, send_sems, recv_sems):
        x, y, c = _place()
        return [pltpu.make_async_remote_copy(
            src_ref=ins[a].at[chip, 1 - c], dst_ref=outs[a].at[chip],
            send_sem=send_sems.at[a, chip], recv_sem=recv_sems.at[a, chip],
            device_id=(x, y, 1 - c), device_id_type=MESH) for chip in range(4)]

    return _exchange_comm(grads, [jax.ShapeDtypeStruct((4,) + t.shape[2:], t.dtype) for t in grads], 4, copies_of,
                          lambda: [_sibling()], SIBLING_ID)


def _chip_comm(partials):
    def copies_of(ins, outs, a, send_sems, recv_sems):
        x, y, c = _place()
        chips = [(1 - x, y), (x, 1 - y), (1 - x, 1 - y)]
        return [pltpu.make_async_remote_copy(
            src_ref=ins[a].at[2 * cx + cy], dst_ref=outs[a].at[k],
            send_sem=send_sems.at[a, k], recv_sem=recv_sems.at[a, k],
            device_id=(cx, cy, c), device_id_type=MESH) for k, (cx, cy) in enumerate(chips)]

    return _exchange_comm(partials, [jax.ShapeDtypeStruct((3,) + t.shape[1:], t.dtype) for t in partials], 3, copies_of,
                          _same_core_of_other_chips, CHIP_ID)


def _pad_rows(t, rows):
    return jnp.pad(t, ((0, rows - t.shape[0]), (0, D_MODEL - t.shape[1])))


MIX_ROW, GROUP_ROW, MLP_ROW, FINAL_ROW, CONV_ROW, SINK_ROW = 0, 8, 16, 24, 32, 40
LOSS_ROW = FINAL_ROW + 1


def _pack_small(g_mix, g_group, g_mlp, g_final, conv, sinks, loss):
    final_and_loss = jnp.concatenate([g_final.reshape(1, D_MODEL), _pad_rows(loss, 1)], axis=0)
    return jnp.concatenate([
        _pad_rows(g_mix, 8), _pad_rows(g_group, 8), _pad_rows(g_mlp, 8), _pad_rows(final_and_loss, 8),
        _pad_rows(conv.reshape(DEPTH * 3, CONV_CH), 8), _pad_rows(sinks.reshape(1, DEPTH * 6), 8)], axis=0)


def kernel(x, w_in, conv_w, sinks, g_mix, g_group, w_o, g_mlp, w_ff_in, w_ff_out, g_final, loss_target, m_w_in, m_conv_w, m_sinks, m_g_mix, m_g_group, m_w_o, m_g_mlp, m_w_ff_in, m_w_ff_out, m_g_final, v_w_in, v_conv_w, v_sinks, v_g_mix, v_g_group, v_w_o, v_g_mlp, v_w_ff_in, v_w_ff_out, v_g_final):
    ax, ay, ac = _place()
    chip = 2 * ax + ay
    dev = 4 * ax + 2 * ay + ac
    pos = jnp.stack([chip, ac]).astype(jnp.int32)

    x0 = x.reshape(SEQ, D_MODEL)
    target = loss_target.reshape(SEQ, D_MODEL)

    stacks = [jnp.swapaxes(w_in, 1, 2).astype(BF16), w_o.astype(BF16),
              jnp.swapaxes(w_ff_in, 1, 2).astype(BF16), w_ff_out.astype(BF16)]
    shards = {(l, kind): (stack, l) for kind, stack in enumerate(stacks) for l in range(DEPTH)}
    conv_tile = jnp.pad(conv_w.reshape(DEPTH * 3, CONV_CH // N_DEV), ((0, 2), (0, LANES - CONV_CH // N_DEV)))
    wt_in0, conv_all = _comm_only(_gather_comm([shards[0, 0], (conv_tile[None], 0)]), "gather_first")
    conv_full = conv_all.reshape(N_DEV, 8, LANES)[:, :DEPTH * 3, :CONV_CH // N_DEV]
    conv_full = conv_full.transpose(1, 0, 2).reshape(DEPTH, 3, CONV_CH)

    dx, parts, small = _step(x0, target, shards, wt_in0, conv_full, sinks, g_mix, g_group, g_mlp, g_final, pos)
    return _finish(dx, parts, small, pos, dev, w_in, conv_w, sinks, g_mix, g_group, w_o, g_mlp, w_ff_in, w_ff_out, g_final, m_w_in, m_conv_w, m_sinks, m_g_mix, m_g_group, m_w_o, m_g_mlp, m_w_ff_in, m_w_ff_out, m_g_final, v_w_in, v_conv_w, v_sinks, v_g_mix, v_g_group, v_w_o, v_g_mlp, v_w_ff_in, v_w_ff_out, v_g_final)


FWD_CARRY = {(0, "in_proj"): ((1, 0),), (0, "window"): ((0, 1),), (0, "dilated"): ((0, 2),),
             (0, "mix_ff_in"): ((0, 3),), (0, "ff_out_in_proj"): ((1, 1), (1, 3)),
             (1, "dilated"): ((1, 2),)}


def _step(x0, target, shards, wt_in0, conv_full, sinks, g_mix, g_group, g_mlp, g_final, pos):
    sink_lanes = jnp.repeat(sinks.reshape(DEPTH, 6), HEAD_DIM, axis=1)
    no_sink = jnp.full((1, A_WIDTH), NEG_BIG, F32)
    full = {(0, 0): wt_in0}

    def gather(stage, l):
        keys = FWD_CARRY.get((l, stage), ())
        return keys, (_gather_comm([shards[k] for k in keys]) if keys else None)

    def landed(keys, got):
        full.update(zip(keys, got))

    saved = []
    xc = x0
    keys, comm = gather("in_proj", 0)
    (z, h), got = _norm_mm(xc, g_mix[0:1], full[0, 0], "in_proj_0", comm)
    landed(keys, got)
    for l in range(DEPTH):
        sink_l = sink_lanes[l:l + 1]
        keys, comm = gather("window", l)
        (yc, *lse_c), got = _attn_fwd(z, sink_l, 1.0, QC_BLK, KC_BLK, VC_BLK, (1,), C_MAX_DIST, True,
                                     f"window_attn_{l}", comm)
        landed(keys, got)
        yb = _conv_fwd(z, conv_full[l], f"conv_{l}")
        keys, comm = gather("dilated", l)
        (ya, *lse_a), got = _attn_fwd(z, no_sink, 0.0, QA_BLK, KA_BLK, VA_BLK, DILATED_PATTERNS, A_MAX_DIST, False,
                                     f"dilated_attn_{l}", comm)
        landed(keys, got)
        keys, comm = gather("mix_ff_in", l)
        (y, x1, a, h2), got = _mix_ff_in(ya, yb, yc, g_group[l:l + 1], full[l, 1], xc, g_mlp[l:l + 1], full[l, 2],
                                         f"mix_ff_in_{l}", comm)
        landed(keys, got)
        saved.append((xc, z, h, ya, lse_a, yb, yc, lse_c, sink_l, y, x1, a, h2))
        if l + 1 < DEPTH:
            keys, comm = gather("ff_out_in_proj", l)
            (xc, z, h), got = _ff_out_in_proj(a, full[l, 3], x1, g_mix[l + 1:l + 2], full[l + 1, 0],
                                              f"ff_out_{l}_in_proj_{l + 1}", comm)
            landed(keys, got)

    loss_slab, dx, dxb, dg_final, du = _mm_res_loss(a, full[DEPTH - 1, 3], x1, g_final.reshape(1, D_MODEL), target,
                                                    f"ff_out_{DEPTH - 1}_loss")

    def by_owner(t):
        return t.reshape(4, 2, t.shape[0] // N_DEV, D_MODEL)

    def pair(l, kinds, grads, received):
        sums = _pair_sums(grads, received, pos, f"grad_pair_sums_{l}_{kinds[0]}{kinds[1]}")
        partial.update({(l, kind): t for kind, t in zip(kinds, sums)})

    partial, r2 = {}, {}
    dg_mix, dg_group, dg_mlp, dconv, dsinks = [None] * DEPTH, [None] * DEPTH, [None] * DEPTH, [None] * DEPTH, [None] * DEPTH
    for l in reversed(range(DEPTH)):
        xin, z, h, ya, lse_a, yb, yc, lse_c, sink_l, y, x1, a, h2 = saved[l]
        late = [(l + 1, 1), (l + 1, 0)] if l + 1 < DEPTH else []
        if l + 1 < DEPTH:
            (du,), _ = _mlp_bwd_act(dxb, full[l, 3], a, f"ff_out_bwd_{l}")
        (g3, g2), got = _mm_tn([(a, dxb), (du, h2)], f"grad_w_ff_{l}",
                               _chip_comm([partial[k] for k in late]) if late else None)
        r2.update(zip(late, got))
        g3, g2 = by_owner(g3), by_owner(g2)
        (dx1, dx1b, dg_mlp[l], dya, dyb, dyc, dg_group[l]), got = _ff_in_mix_bwd(
            du, full[l, 2], x1, dx, g_mlp[l:l + 1], full[l, 1], ya, yb, yc, g_group[l:l + 1],
            f"ff_in_mix_bwd_{l}", _sibling_comm([g3, g2]))
        pair(l, (3, 2), [g3, g2], got)
        early = [(l, 3), (l, 2)]
        (dz, _), got = _attn_bwd(z, dya, ya, lse_a, no_sink, None, QA_BLK, KA_BLK, VA_BLK, DILATED_PATTERNS,
                                 A_MAX_DIST, False, f"dilated_attn_bwd_{l}", _chip_comm([partial[k] for k in early]))
        r2.update(zip(early, got))
        dz, dcw = _conv_bwd(z, conv_full[l], dyb, dz, f"conv_bwd_{l}")
        (dz, dsink), _ = _attn_bwd(z, dyc, yc, lse_c, sink_l, dz, QC_BLK, KC_BLK, VC_BLK, (1,), C_MAX_DIST,
                                   True, f"window_attn_bwd_{l}")
        (g1, g0), _ = _mm_tn([(y, dx1b), (dz, h)], f"grad_w_o_in_{l}")
        g1, g0 = by_owner(g1), by_owner(g0)
        if l > 0:
            (dx, dxb, dg_mix[l]), got = _mm_nn_normbwd(dz, full[l, 0], xin, dx1, g_mix[l:l + 1], f"in_proj_bwd_{l}",
                                                      _sibling_comm([g1, g0]))
            pair(l, (1, 0), [g1, g0], got)
        else:
            got = _comm_only(_sibling_comm([g1, g0]), "grad_sibling_exchange_last")
            pair(l, (1, 0), [g1, g0], got)
            (dx, dxb, dg_mix[l]), got = _mm_nn_normbwd(dz, full[l, 0], xin, dx1, g_mix[l:l + 1], f"in_proj_bwd_{l}",
                                                      _chip_comm([partial[l, 1], partial[l, 0]]))
            r2[l, 1], r2[l, 0] = got
        dconv[l] = dcw[:3]
        dsinks[l] = dsink[0, ::HEAD_DIM]
    parts = {key: (partial[key], r2[key]) for key in partial}
    small = _pack_small(jnp.concatenate(dg_mix), jnp.concatenate(dg_group), jnp.concatenate(dg_mlp),
                        dg_final, jnp.stack(dconv), jnp.stack(dsinks), loss_slab[0:1])
    return dx, parts, small


def _finish(dx, parts, small, pos, dev, w_in, conv_w, sinks, g_mix, g_group, w_o, g_mlp, w_ff_in, w_ff_out, g_final, m_w_in, m_conv_w, m_sinks, m_g_mix, m_g_group, m_w_o, m_g_mlp, m_w_ff_in, m_w_ff_out, m_g_final, v_w_in, v_conv_w, v_sinks, v_g_mix, v_g_group, v_w_o, v_g_mlp, v_w_ff_in, v_w_ff_out, v_g_final):
    grad_x = dx.reshape(1, SEQ, D_MODEL)

    (small_all,) = _comm_only(_gather_comm([(small[None], 0)]), "gather_small_grads")
    row = lambda t: t.reshape(1, D_MODEL)
    sink_row = lambda t: _pad_rows(t.reshape(1, DEPTH * 6), 1)
    params = [(MIX_ROW, g_mix, m_g_mix, v_g_mix), (GROUP_ROW, g_group, m_g_group, v_g_group),
              (MLP_ROW, g_mlp, m_g_mlp, v_g_mlp), (FINAL_ROW, row(g_final), row(m_g_final), row(v_g_final)),
              (SINK_ROW, sink_row(sinks), sink_row(m_sinks), sink_row(v_sinks))]
    updated, conv_rows, loss_row = _small_sum_adamw(small_all.reshape(N_DEV, SMALL_ROWS, D_MODEL), params, "small_adamw")
    loss = loss_row[0, 0]
    (grad_g_mix, delta_g_mix, new_m_g_mix, new_v_g_mix), (grad_g_group, delta_g_group, new_m_g_group, new_v_g_group), \
        (grad_g_mlp, delta_g_mlp, new_m_g_mlp, new_v_g_mlp), final4, sinks4 = updated
    grad_g_final, delta_g_final, new_m_g_final, new_v_g_final = [t.reshape(D_MODEL) for t in final4]
    grad_sinks, delta_sinks, new_m_sinks, new_v_sinks = [t[0, :DEPTH * 6].reshape(DEPTH, 2, 3) for t in sinks4]
    conv_grad_full = conv_rows[:DEPTH * 3, :CONV_CH].reshape(DEPTH, 3, CONV_CH)
    cs = CONV_CH // N_DEV
    grad_conv_w = lax.dynamic_slice_in_dim(conv_grad_full, dev * cs, cs, axis=2)

    def tile_of(t):
        return jnp.pad(t.reshape(1, DEPTH * 3 * cs), ((0, 7), (0, 256 - DEPTH * 3 * cs)))

    cd, cm, cv = _adamw(tile_of(conv_w), tile_of(grad_conv_w), tile_of(m_conv_w), tile_of(v_conv_w), "conv_adamw")
    untile = lambda t: t[0, :DEPTH * 3 * cs].reshape(DEPTH, 3, cs)
    delta_conv_w, new_m_conv_w, new_v_conv_w = untile(cd), untile(cm), untile(cv)

    def big(kind, w, m, v, transpose, name):
        return _sum_adamw([parts[l, kind] for l in range(DEPTH)], w, m, v, pos, transpose, name)

    swap = lambda t: jnp.swapaxes(t, 1, 2)
    grad_w_in, delta_w_in, new_m_w_in, new_v_w_in = [
        swap(t) for t in big(0, swap(w_in), swap(m_w_in), swap(v_w_in), False, "adamw_w_in")]
    grad_w_o, delta_w_o, new_m_w_o, new_v_w_o = big(1, w_o, m_w_o, v_w_o, False, "adamw_w_o")
    grad_w_ff_in, delta_w_ff_in, new_m_w_ff_in, new_v_w_ff_in = big(2, w_ff_in, m_w_ff_in, v_w_ff_in, True, "adamw_w_ff_in")
    grad_w_ff_out, delta_w_ff_out, new_m_w_ff_out, new_v_w_ff_out = big(3, w_ff_out, m_w_ff_out, v_w_ff_out, False,
                                                                         "adamw_w_ff_out")

    return (loss, grad_x, grad_w_in, grad_conv_w, grad_sinks, grad_g_mix, grad_g_group, grad_w_o, grad_g_mlp,
            grad_w_ff_in, grad_w_ff_out, grad_g_final,
            delta_w_in, delta_conv_w, delta_sinks, delta_g_mix, delta_g_group, delta_w_o, delta_g_mlp,
            delta_w_ff_in, delta_w_ff_out, delta_g_final,
            new_m_w_in, new_m_conv_w, new_m_sinks, new_m_g_mix, new_m_g_group, new_m_w_o, new_m_g_mlp,
            new_m_w_ff_in, new_m_w_ff_out, new_m_g_final,
            new_v_w_in, new_v_conv_w, new_v_sinks, new_v_g_mix, new_v_g_group, new_v_w_o, new_v_g_mlp,
            new_v_w_ff_in, new_v_w_ff_out, new_v_g_final)
```

```python
from typing import Callable, NamedTuple

import jax
import jax.numpy as jnp
from jax import lax
from jax.experimental import pallas as pl
from jax.experimental.pallas import tpu as pltpu

F32 = jnp.float32
BF16 = jnp.bfloat16
MESH = pl.DeviceIdType.MESH

N_DEV = 8
SEQ = 4096
D_MODEL = 1024
DEPTH = 2
HEAD_DIM = 64
LANES = 128
A_WIDTH = 384
CONV_CH = 256
C_WIDTH = 384
IN_WIDTH = 2560
BLOCK = 128
DILATED_PATTERNS = (1, 4, 16)
A_MAX_DIST = 128
C_MAX_DIST = 127
EPS = 1e-6
SCALE = HEAD_DIM ** -0.5
NEG_BIG = -1e30
F32_TINY = 1.1754944e-38

QA_BLK, KA_BLK, VA_BLK = 0, 3, 6
GB_BLK, GC_BLK, XB_BLK = 9, 11, 13
QC_BLK, KC_BLK, VC_BLK = 15, 18, 19

ADAM_LR = 0.001
ADAM_B1 = 0.9
ADAM_B2 = 0.999
ADAM_EPS = 1e-08
ADAM_WD = 0.01
ADAM_STEP = 10

VMEM_LIMIT = 56 * 1024 * 1024
TILE_BUDGET = 46 * 1024 * 1024
ROW_TILE = 512
COL_CHUNK = 512
SMALL_ROWS = 48


def _dot_nn(a, b):
    return lax.dot_general(a, b, (((1,), (0,)), ((), ())), preferred_element_type=F32)


def _dot_nt(a, b):
    return lax.dot_general(a, b, (((1,), (1,)), ((), ())), preferred_element_type=F32)


def _dot_tn(a, b):
    return lax.dot_general(a, b, (((0,), (0,)), ((), ())), preferred_element_type=F32)


def _params(*sem, collective_id=None):
    return pltpu.CompilerParams(dimension_semantics=sem, vmem_limit_bytes=VMEM_LIMIT, collective_id=collective_id)


def _resident(shape):
    return pl.BlockSpec(shape, lambda i: (0,) * len(shape), pipeline_mode=pl.Buffered(1))


class _Late(NamedTuple):
    hbm: object
    vmem: object
    sem: object

    def fetch(self, needed_at_step):
        copy = pltpu.make_async_copy(self.hbm, self.vmem, self.sem)
        pl.when(pl.program_id(0) == 0)(copy.start)
        return lambda: pl.when(pl.program_id(0) == needed_at_step)(copy.wait)


LATE_SPEC = pl.BlockSpec(memory_space=pl.ANY)


def _late_scratch(t):
    return [pltpu.VMEM(t.shape, t.dtype), pltpu.SemaphoreType.DMA(())]


def _row_tile(row_bytes, resident_bytes):
    for tm in (ROW_TILE, ROW_TILE // 2):
        if 2 * tm * row_bytes + resident_bytes <= TILE_BUDGET:
            return tm
    return ROW_TILE // 4


def _rms_scale(t):
    return lax.rsqrt(jnp.mean(t * t, axis=-1, keepdims=True) + EPS)


def _rms_bwd(n, r, dn):
    return r * (dn - n * jnp.mean(dn * n, axis=-1, keepdims=True))


class _Comm(NamedTuple):
    arrays: tuple
    out_shape: tuple
    sems: tuple
    start: Callable
    relay: Callable
    finish: Callable
    peers: Callable
    collective_id: int


def _handshake(comm):
    barrier = pltpu.get_barrier_semaphore()
    peers = comm.peers()
    for peer in peers:
        pl.semaphore_signal(barrier, inc=1, device_id=peer, device_id_type=MESH)
    pl.semaphore_wait(barrier, len(peers))


def _call(body, grid, in_specs, out_specs, out_shape, operands, name, scratch_shapes=(), comm=None, aliases=None):
    n_in, n_out, n_scr = len(in_specs), len(out_shape), len(scratch_shapes)
    aliases = dict(aliases or {})
    if comm is None:
        res = pl.pallas_call(body, grid=grid, in_specs=list(in_specs), out_specs=list(out_specs),
                             out_shape=list(out_shape), scratch_shapes=list(scratch_shapes),
                             input_output_aliases=aliases,
                             compiler_params=_params("arbitrary"), name=name)(*operands)
        return list(res), []
    c_in, c_out = len(comm.arrays), len(comm.out_shape)
    hbm = pl.BlockSpec(memory_space=pl.ANY)
    last = grid[0] - 1

    def carried(*refs):
        ins, cins = refs[:n_in], refs[n_in:n_in + c_in]
        o0 = n_in + c_in
        outs, couts = refs[o0:o0 + n_out], refs[o0 + n_out:o0 + n_out + c_out]
        s0 = o0 + n_out + c_out
        scr, sems = refs[s0:s0 + n_scr], refs[s0 + n_scr:]
        @pl.when(pl.program_id(0) == 0)
        def _():
            _handshake(comm)
            comm.start(cins, couts, sems)

        pl.when(pl.program_id(0) == last)(lambda: comm.relay(cins, couts, sems))
        body(*ins, *outs, *scr)
        pl.when(pl.program_id(0) == last)(lambda: comm.finish(cins, couts, sems))

    res = pl.pallas_call(carried, grid=grid, in_specs=list(in_specs) + [hbm] * c_in,
                         out_specs=list(out_specs) + [hbm] * c_out, out_shape=list(out_shape) + list(comm.out_shape),
                         scratch_shapes=list(scratch_shapes) + list(comm.sems), input_output_aliases=aliases,
                         compiler_params=_params("arbitrary", collective_id=comm.collective_id),
                         name=name)(*operands, *comm.arrays)
    return list(res[:n_out]), list(res[n_out:])


def _comm_only(comm, name):
    hbm = pl.BlockSpec(memory_space=pl.ANY)
    c_in, c_out = len(comm.arrays), len(comm.out_shape)

    def body(*refs):
        ins, outs, sems = refs[:c_in], refs[c_in:c_in + c_out], refs[c_in + c_out:]
        _handshake(comm)
        comm.start(ins, outs, sems)
        comm.relay(ins, outs, sems)
        comm.finish(ins, outs, sems)

    return pl.pallas_call(body, in_specs=[hbm] * c_in, out_specs=[hbm] * c_out, out_shape=list(comm.out_shape),
                          scratch_shapes=list(comm.sems),
                          compiler_params=pltpu.CompilerParams(collective_id=comm.collective_id),
                          name=name)(*comm.arrays)


def _norm_mm(x, g, wt, name, comm=None):
    s, d = x.shape
    n = wt.shape[0]
    tm = _row_tile(4 * d + 4 * n + 2 * d, 2 * n * d)

    def body(x_ref, g_ref, w_ref, o_ref, h_ref):
        xx = x_ref[...]
        h = ((xx * _rms_scale(xx)) * g_ref[...]).astype(BF16)
        h_ref[...] = h
        for n0 in range(0, n, COL_CHUNK):
            o_ref[:, n0:n0 + COL_CHUNK] = _dot_nt(h, w_ref[n0:n0 + COL_CHUNK, :])

    return _call(
        body,
        grid=(s // tm,),
        in_specs=[pl.BlockSpec((tm, d), lambda i: (i, 0)),
                  pl.BlockSpec((1, d), lambda i: (0, 0)),
                  _resident((n, d))],
        out_specs=[pl.BlockSpec((tm, n), lambda i: (i, 0)),
                   pl.BlockSpec((tm, d), lambda i: (i, 0))],
        out_shape=[jax.ShapeDtypeStruct((s, n), F32), jax.ShapeDtypeStruct((s, d), BF16)],
        operands=(x, g, wt), name=name, comm=comm)


def _ff_out_in_proj(a, w2, x1, g, wt, name, comm=None):
    s, f = a.shape
    d = w2.shape[1]
    n = wt.shape[0]
    tm = _row_tile(2 * f + 4 * d + 4 * d + 4 * n + 2 * d, 2 * f * d + 2 * n * d)

    def body(a_ref, w2_ref, x_ref, g_ref, w_ref, x2_ref, z_ref, h_ref):
        x2 = x_ref[...] + _dot_nn(a_ref[...], w2_ref[...])
        x2_ref[...] = x2
        h = ((x2 * _rms_scale(x2)) * g_ref[...]).astype(BF16)
        h_ref[...] = h
        for n0 in range(0, n, COL_CHUNK):
            z_ref[:, n0:n0 + COL_CHUNK] = _dot_nt(h, w_ref[n0:n0 + COL_CHUNK, :])

    rows = lambda w: pl.BlockSpec((tm, w), lambda i: (i, 0))
    return _call(
        body,
        grid=(s // tm,),
        in_specs=[rows(f), _resident((f, d)), rows(d), pl.BlockSpec((1, d), lambda i: (0, 0)), _resident((n, d))],
        out_specs=[rows(d), rows(n), rows(d)],
        out_shape=[jax.ShapeDtypeStruct((s, d), F32), jax.ShapeDtypeStruct((s, n), F32),
                   jax.ShapeDtypeStruct((s, d), BF16)],
        operands=(a, w2, x1, g, wt), name=name, comm=comm)


def _mix_ff_in(ya, yb, yc, gg, wo, x0, g_mlp, wt1, name, comm=None):
    s = ya.shape[0]
    d = wo.shape[1]
    f = wt1.shape[0]
    tm = _row_tile(4 * d + 4 * d + 2 * d + 4 * d + 2 * d + 2 * f, 2 * d * d + 2 * f * d)

    def body(ya_ref, yb_ref, yc_ref, gg_ref, wo_ref, x_ref, g_ref, w1_ref, y_ref, x1_ref, a_ref, h_ref):
        parts = []
        for ref in (ya_ref, yb_ref, yc_ref):
            t = ref[...]
            parts.append(t * _rms_scale(t))
        y = (jnp.concatenate(parts, axis=1) * gg_ref[...]).astype(BF16)
        y_ref[...] = y
        x1 = x_ref[...] + _dot_nn(y, wo_ref[...])
        x1_ref[...] = x1
        h = ((x1 * _rms_scale(x1)) * g_ref[...]).astype(BF16)
        h_ref[...] = h
        for n0 in range(0, f, COL_CHUNK):
            u = _dot_nt(h, w1_ref[n0:n0 + COL_CHUNK, :])
            a_ref[:, n0:n0 + COL_CHUNK] = jnp.square(jnp.maximum(u, 0.0)).astype(BF16)

    rows = lambda w: pl.BlockSpec((tm, w), lambda i: (i, 0))
    vec = pl.BlockSpec((1, d), lambda i: (0, 0))
    return _call(
        body,
        grid=(s // tm,),
        in_specs=[rows(A_WIDTH), rows(CONV_CH), rows(C_WIDTH), vec, _resident((d, d)), rows(d), vec, _resident((f, d))],
        out_specs=[rows(d), rows(d), rows(f), rows(d)],
        out_shape=[jax.ShapeDtypeStruct((s, d), BF16), jax.ShapeDtypeStruct((s, d), F32),
                   jax.ShapeDtypeStruct((s, f), BF16), jax.ShapeDtypeStruct((s, d), BF16)],
        operands=(ya, yb, yc, gg, wo, x0, g_mlp, wt1), name=name, comm=comm)


def _relu_from_square(av):
    return av * lax.rsqrt(jnp.maximum(av, F32_TINY))


def _mm_res_loss(a, w2, x1, g, target, name):
    s, f = a.shape
    d = w2.shape[1]
    tm = _row_tile(2 * f + 4 * d + 4 * d + 4 * d + 2 * d + 2 * f, 2 * f * d)

    def body(a_ref, w_ref, x_ref, g_ref, t_ref, loss_ref, dx_ref, dxb_ref, dg_ref, du_ref):
        @pl.when(pl.program_id(0) == 0)
        def _():
            loss_ref[...] = jnp.zeros_like(loss_ref)
            dg_ref[...] = jnp.zeros_like(dg_ref)

        xx = x_ref[...] + _dot_nn(a_ref[...], w_ref[...])
        r = _rms_scale(xx)
        n = xx * r
        gv = g_ref[...]
        err = n * gv - t_ref[...]
        per_tok = jnp.sum(err * err, axis=1, keepdims=True) * (1.0 / d)
        loss_ref[...] += 0.5 * jnp.sum(per_tok, axis=0, keepdims=True)
        dout = err * (1.0 / d)
        dg_ref[...] += jnp.sum(dout * n, axis=0, keepdims=True)
        dx = _rms_bwd(n, r, dout * gv)
        dx_ref[...] = dx
        dxb = dx.astype(BF16)
        dxb_ref[...] = dxb
        for n0 in range(0, f, COL_CHUNK):
            da = _dot_nt(dxb, w_ref[n0:n0 + COL_CHUNK, :])
            rl = _relu_from_square(a_ref[:, n0:n0 + COL_CHUNK].astype(F32))
            du_ref[:, n0:n0 + COL_CHUNK] = (da * (2.0 * rl)).astype(BF16)

    rows = lambda w: pl.BlockSpec((tm, w), lambda i: (i, 0))
    vec = pl.BlockSpec((1, d), lambda i: (0, 0))
    return pl.pallas_call(
        body,
        grid=(s // tm,),
        in_specs=[rows(f), _resident((f, d)), rows(d), vec, rows(d)],
        out_specs=[pl.BlockSpec((8, LANES), lambda i: (0, 0)), rows(d), rows(d), vec, rows(f)],
        out_shape=[jax.ShapeDtypeStruct((8, LANES), F32), jax.ShapeDtypeStruct((s, d), F32),
                   jax.ShapeDtypeStruct((s, d), BF16), jax.ShapeDtypeStruct((1, d), F32),
                   jax.ShapeDtypeStruct((s, f), BF16)],
        compiler_params=_params("arbitrary"),
        name=name,
    )(a, w2, x1, g, target)


def _mlp_bwd_act(dxb, w2, a, name, comm=None):
    s, d = dxb.shape
    f = w2.shape[0]
    tm = _row_tile(2 * d + 2 * f + 2 * f, 2 * f * d)

    def body(dx_ref, w_ref, a_ref, du_ref):
        dx = dx_ref[...]
        for n0 in range(0, f, COL_CHUNK):
            da = _dot_nt(dx, w_ref[n0:n0 + COL_CHUNK, :])
            rl = _relu_from_square(a_ref[:, n0:n0 + COL_CHUNK].astype(F32))
            du_ref[:, n0:n0 + COL_CHUNK] = (da * (2.0 * rl)).astype(BF16)

    return _call(
        body,
        grid=(s // tm,),
        in_specs=[pl.BlockSpec((tm, d), lambda i: (i, 0)),
                  _resident((f, d)),
                  pl.BlockSpec((tm, f), lambda i: (i, 0))],
        out_specs=[pl.BlockSpec((tm, f), lambda i: (i, 0))],
        out_shape=[jax.ShapeDtypeStruct((s, f), BF16)],
        operands=(dxb, w2, a), name=name, comm=comm)


def _mm_tn(pairs, name, comm=None):
    s, d = pairs[0][1].shape
    tn = 512
    tiles = [a.shape[1] // tn for a, _ in pairs]
    starts = [sum(tiles[:k]) for k in range(len(pairs))]

    def body(*refs):
        ins, outs = refs[:2 * len(pairs)], refs[2 * len(pairs):3 * len(pairs)]
        acc, late = refs[3 * len(pairs)], refs[3 * len(pairs) + 1:]
        j = pl.program_id(0)
        b_refs = [ins[1]]
        for k in range(1, len(pairs)):
            b_late = _Late(ins[2 * k + 1], late[2 * k - 2], late[2 * k - 1])
            b_late.fetch(starts[k])()
            b_refs.append(b_late.vmem)
        for k in range(len(pairs)):
            def run(a_ref=ins[2 * k], b_ref=b_refs[k], o_ref=outs[k]):
                for k0 in range(0, s, ROW_TILE):
                    part = _dot_tn(a_ref[k0:k0 + ROW_TILE, :], b_ref[k0:k0 + ROW_TILE, :])
                    if k0 == 0:
                        acc[...] = part
                    else:
                        acc[...] += part
                o_ref[...] = acc[...].astype(BF16)

            pl.when((j >= starts[k]) & (j < starts[k] + tiles[k]))(run)

    def tile_of(k):
        return lambda j: jnp.clip(j - starts[k], 0, tiles[k] - 1)

    in_specs, out_specs = [], []
    for k in range(len(pairs)):
        in_specs += [pl.BlockSpec((s, tn), lambda j, t=tile_of(k): (0, t(j))), _resident((s, d)) if k == 0 else LATE_SPEC]
        out_specs.append(pl.BlockSpec((tn, d), lambda j, t=tile_of(k): (t(j), 0)))
    late = [scratch for _, b in pairs[1:] for scratch in _late_scratch(b)]
    return _call(
        body,
        grid=(sum(tiles),),
        in_specs=in_specs,
        out_specs=out_specs,
        out_shape=[jax.ShapeDtypeStruct((a.shape[1], d), BF16) for a, _ in pairs],
        operands=tuple(t for pair in pairs for t in pair), name=name,
        scratch_shapes=[pltpu.VMEM((tn, d), F32)] + late, comm=comm)


def _mm_nn_normbwd(dact, wt, x, dres, g, name, comm=None):
    s, kdim = dact.shape
    d = wt.shape[1]
    tm = _row_tile(2 * kdim + 4 * d + 4 * d + 4 * d + 2 * d, 2 * kdim * d)

    def body(a_ref, w_ref, x_ref, r_ref, g_ref, o_ref, ob_ref, dg_ref):
        @pl.when(pl.program_id(0) == 0)
        def _():
            dg_ref[...] = jnp.zeros_like(dg_ref)

        dh = _dot_nn(a_ref[...], w_ref[...])
        xx = x_ref[...]
        r = _rms_scale(xx)
        n = xx * r
        dg_ref[...] += jnp.sum(dh * n, axis=0, keepdims=True)
        dx = r_ref[...] + _rms_bwd(n, r, dh * g_ref[...])
        o_ref[...] = dx
        ob_ref[...] = dx.astype(BF16)

    return _call(
        body,
        grid=(s // tm,),
        in_specs=[pl.BlockSpec((tm, kdim), lambda i: (i, 0)),
                  _resident((kdim, d)),
                  pl.BlockSpec((tm, d), lambda i: (i, 0)),
                  pl.BlockSpec((tm, d), lambda i: (i, 0)),
                  pl.BlockSpec((1, d), lambda i: (0, 0))],
        out_specs=[pl.BlockSpec((tm, d), lambda i: (i, 0)),
                   pl.BlockSpec((tm, d), lambda i: (i, 0)),
                   pl.BlockSpec((1, d), lambda i: (0, 0))],
        out_shape=[jax.ShapeDtypeStruct((s, d), F32), jax.ShapeDtypeStruct((s, d), BF16),
                   jax.ShapeDtypeStruct((1, d), F32)],
        operands=(dact, wt, x, dres, g), name=name, comm=comm)


def _ff_in_mix_bwd(du, wt1, x1, dres, g_mlp, wo, ya, yb, yc, gg, name, comm=None):
    s, f = du.shape
    d = wt1.shape[1]
    widths = (A_WIDTH, CONV_CH, C_WIDTH)
    tm = _row_tile(2 * f + 4 * d + 4 * d + 4 * d + 2 * d + 4 * d + 4 * d, 2 * f * d + 2 * d * d)

    def body(du_ref, w1_ref, x_ref, r_ref, g_ref, wo_ref, ya_ref, yb_ref, yc_ref, gg_ref,
             dx_ref, dxb_ref, dg_ref, da_ref, db_ref, dc_ref, dgg_ref):
        @pl.when(pl.program_id(0) == 0)
        def _():
            dg_ref[...] = jnp.zeros_like(dg_ref)
            dgg_ref[...] = jnp.zeros_like(dgg_ref)

        dh = _dot_nn(du_ref[...], w1_ref[...])
        xx = x_ref[...]
        r = _rms_scale(xx)
        n = xx * r
        dg_ref[...] += jnp.sum(dh * n, axis=0, keepdims=True)
        dx = r_ref[...] + _rms_bwd(n, r, dh * g_ref[...])
        dx_ref[...] = dx
        dxb = dx.astype(BF16)
        dxb_ref[...] = dxb

        dy = _dot_nt(dxb, wo_ref[...])
        gv = gg_ref[...]
        off = 0
        dgs = []
        for ref, out, w in zip((ya_ref, yb_ref, yc_ref), (da_ref, db_ref, dc_ref), widths):
            t = ref[...]
            r = _rms_scale(t)
            n = t * r
            dyg = dy[:, off:off + w]
            dgs.append(jnp.sum(dyg * n, axis=0, keepdims=True))
            out[...] = _rms_bwd(n, r, dyg * gv[:, off:off + w])
            off += w
        dgg_ref[...] += jnp.concatenate(dgs, axis=1)

    rows = lambda w: pl.BlockSpec((tm, w), lambda i: (i, 0))
    vec = pl.BlockSpec((1, d), lambda i: (0, 0))
    return _call(
        body,
        grid=(s // tm,),
        in_specs=[rows(f), _resident((f, d)), rows(d), rows(d), vec, _resident((d, d)),
                  rows(A_WIDTH), rows(CONV_CH), rows(C_WIDTH), vec],
        out_specs=[rows(d), rows(d), vec, rows(A_WIDTH), rows(CONV_CH), rows(C_WIDTH), vec],
        out_shape=[jax.ShapeDtypeStruct((s, d), F32), jax.ShapeDtypeStruct((s, d), BF16), jax.ShapeDtypeStruct((1, d), F32),
                   jax.ShapeDtypeStruct((s, A_WIDTH), F32), jax.ShapeDtypeStruct((s, CONV_CH), F32),
                   jax.ShapeDtypeStruct((s, C_WIDTH), F32), jax.ShapeDtypeStruct((1, d), F32)],
        operands=(du, wt1, x1, dres, g_mlp, wo, ya, yb, yc, gg), name=name, comm=comm)


CONV_CHUNK = 256
CONV_HALO = 8


def _conv_fwd(z, cw, name):
    s = z.shape[0]
    nch = s // CONV_CHUNK

    def body(gb_ref, gc_ref, xb_ref, w_ref, o_ref, us):
        us[pl.ds(0, CONV_HALO), :] = jnp.zeros((CONV_HALO, LANES), F32)
        us[pl.ds(CONV_HALO, s), :] = gc_ref[...] * xb_ref[...]
        w0, w1, w2 = w_ref[0:1, :], w_ref[1:2, :], w_ref[2:3, :]

        def chunk(c, carry):
            st = pl.multiple_of(c * CONV_CHUNK, CONV_CHUNK)
            ext = us[pl.ds(st, CONV_CHUNK + CONV_HALO), :]
            y = (w0 * ext[CONV_HALO - 2:CONV_HALO - 2 + CONV_CHUNK]
                 + w1 * ext[CONV_HALO - 1:CONV_HALO - 1 + CONV_CHUNK]
                 + w2 * ext[CONV_HALO:])
            o_ref[pl.ds(st, CONV_CHUNK), :] = gb_ref[pl.ds(st, CONV_CHUNK), :] * y
            return carry

        lax.fori_loop(0, nch, chunk, 0)

    col = lambda blk: pl.BlockSpec((s, LANES), lambda j, blk=blk: (0, blk + j))
    return pl.pallas_call(
        body,
        grid=(CONV_CH // LANES,),
        in_specs=[col(GB_BLK), col(GC_BLK), col(XB_BLK), pl.BlockSpec((3, LANES), lambda j: (0, j))],
        out_specs=pl.BlockSpec((s, LANES), lambda j: (0, j)),
        out_shape=jax.ShapeDtypeStruct((s, CONV_CH), F32),
        scratch_shapes=[pltpu.VMEM((s + CONV_HALO, LANES), F32)],
        compiler_params=_params("parallel"),
        name=name,
    )(z, z, z, cw)


def _conv_bwd(z, cw, dyb, dz, name):
    s = z.shape[0]
    nch = s // CONV_CHUNK
    ncol = CONV_CH // LANES

    def body(gb_ref, gc_ref, xb_ref, w_ref, dy_ref, dz_in, dz_ref, dw_ref, us, ds_, dgb_ref, dgc_ref, dxb_ref, sems):
        j = pl.program_id(0)

        def to_dz(staged, blk, k):
            cols = pl.ds(pl.multiple_of((blk + j) * LANES, LANES), LANES)
            return pltpu.make_async_copy(staged, dz_ref.at[:, cols], sems.at[k])

        copies = [to_dz(dgb_ref, GB_BLK, 0), to_dz(dgc_ref, GC_BLK, 1), to_dz(dxb_ref, XB_BLK, 2)]

        @pl.when(j > 0)
        def _():
            for cp in copies:
                cp.wait()

        us[pl.ds(0, CONV_HALO), :] = jnp.zeros((CONV_HALO, LANES), F32)
        us[pl.ds(CONV_HALO, s), :] = gc_ref[...] * xb_ref[...]
        ds_[pl.ds(s, CONV_HALO), :] = jnp.zeros((CONV_HALO, LANES), F32)
        ds_[pl.ds(0, s), :] = dy_ref[...] * gb_ref[...]
        w0, w1, w2 = w_ref[0:1, :], w_ref[1:2, :], w_ref[2:3, :]
        zero = jnp.zeros((1, LANES), F32)

        def chunk(c, carry):
            a0, a1, a2 = carry
            st = pl.multiple_of(c * CONV_CHUNK, CONV_CHUNK)
            rows = pl.ds(st, CONV_CHUNK)
            ext = us[pl.ds(st, CONV_CHUNK + CONV_HALO), :]
            um2 = ext[CONV_HALO - 2:CONV_HALO - 2 + CONV_CHUNK]
            um1 = ext[CONV_HALO - 1:CONV_HALO - 1 + CONV_CHUNK]
            u0 = ext[CONV_HALO:]
            dext = ds_[pl.ds(st, CONV_CHUNK + CONV_HALO), :]
            dc0 = dext[:CONV_CHUNK]
            du = w2 * dc0 + w1 * dext[1:1 + CONV_CHUNK] + w0 * dext[2:2 + CONV_CHUNK]
            yconv = w0 * um2 + w1 * um1 + w2 * u0
            dgb_ref[rows, :] = (dy_ref[rows, :] * yconv).astype(BF16)
            dgc_ref[rows, :] = (du * xb_ref[rows, :]).astype(BF16)
            dxb_ref[rows, :] = (du * gc_ref[rows, :]).astype(BF16)
            a0 = a0 + jnp.sum(dc0 * um2, axis=0, keepdims=True)
            a1 = a1 + jnp.sum(dc0 * um1, axis=0, keepdims=True)
            a2 = a2 + jnp.sum(dc0 * u0, axis=0, keepdims=True)
            return a0, a1, a2

        a0, a1, a2 = lax.fori_loop(0, nch, chunk, (zero, zero, zero))
        dw_ref[...] = jnp.concatenate([a0, a1, a2, jnp.zeros((5, LANES), F32)], axis=0)
        for cp in copies:
            cp.start()

        @pl.when(j == ncol - 1)
        def _():
            for cp in copies:
                cp.wait()

    col = lambda blk: pl.BlockSpec((s, LANES), lambda j, blk=blk: (0, blk + j))
    hbm = pl.BlockSpec(memory_space=pl.ANY)
    return pl.pallas_call(
        body,
        grid=(ncol,),
        in_specs=[col(GB_BLK), col(GC_BLK), col(XB_BLK), pl.BlockSpec((3, LANES), lambda j: (0, j)),
                  pl.BlockSpec((s, LANES), lambda j: (0, j)), hbm],
        out_specs=[hbm, pl.BlockSpec((8, LANES), lambda j: (0, j))],
        out_shape=[jax.ShapeDtypeStruct(dz.shape, dz.dtype), jax.ShapeDtypeStruct((8, CONV_CH), F32)],
        scratch_shapes=[pltpu.VMEM((s + CONV_HALO, LANES), F32), pltpu.VMEM((s + CONV_HALO, LANES), F32)]
        + [pltpu.VMEM((s, LANES), BF16)] * 3 + [pltpu.SemaphoreType.DMA((3,))],
        input_output_aliases={5: 0},
        compiler_params=_params("arbitrary"),
        name=name,
    )(z, z, z, cw, dyb, dz)


ATTN_ROWS = 512
ATTN_UNROLL = 8


def _band_rows(b, d, r):
    base = pl.multiple_of(b * (BLOCK * d), BLOCK)
    prev = jnp.maximum(base - BLOCK * d, 0)
    if d == 1:
        return pl.ds(base, BLOCK), pl.ds(pl.multiple_of(prev, BLOCK), BLOCK)
    return pl.ds(base + r, BLOCK, stride=d), pl.ds(prev + r, BLOCK, stride=d)


def _write_band_bias(bias_ref, max_dist):
    qi = lax.broadcasted_iota(jnp.int32, (BLOCK, 2 * BLOCK), 0)
    kj = lax.broadcasted_iota(jnp.int32, (BLOCK, 2 * BLOCK), 1)
    dist = BLOCK + qi - kj
    band = (dist >= 0) & (dist <= max_dist)
    bias_ref[0:BLOCK, :] = jnp.where(band, 0.0, -jnp.inf)
    bias_ref[BLOCK:2 * BLOCK, :] = jnp.where(band & (kj >= BLOCK), 0.0, -jnp.inf)


def _band_bias(bias_ref, b):
    bias = bias_ref[pl.ds(pl.multiple_of(jnp.where(b > 0, 0, BLOCK), BLOCK), BLOCK), :]
    return jnp.concatenate([bias, bias], axis=0)


def _kv_halves(pair):
    zero = jnp.zeros((1, LANES), jnp.int32)
    return zero + (pair >> 1), zero + ((pair + 1) >> 1)


def _stack_heads(t, head0, halves=None):
    top, bottom = jnp.where(head0, t, 0.0), jnp.where(head0, 0.0, t)
    if halves is not None:
        top = jnp.where(halves[0] == 1, pltpu.roll(top, HEAD_DIM, 1), top)
        bottom = jnp.where(halves[1] == 0, pltpu.roll(bottom, HEAD_DIM, 1), bottom)
    return jnp.concatenate([top, bottom], axis=0).astype(BF16)


def _unstack_heads(t, head0, halves=None):
    top, bottom = t[:BLOCK], t[BLOCK:]
    if halves is not None:
        top = jnp.where(halves[0] == 1, pltpu.roll(top, HEAD_DIM, 1), top)
        bottom = jnp.where(halves[1] == 0, pltpu.roll(bottom, HEAD_DIM, 1), bottom)
    return jnp.where(head0, top, bottom)


def _block_loops(s, patterns, unroll, one_block):
    for n, d in enumerate(patterns):
        nb = (s // BLOCK) // d
        ur = min(unroll, d)
        ub = unroll // ur
        for r0 in range(0, d, ur):
            def trip(i, carry, n=n, d=d, r0=r0, ur=ur, ub=ub):
                for u in range(ub):
                    for r in range(r0, r0 + ur):
                        one_block(i * ub + u, d, r, n == 0)
                return carry
            lax.fori_loop(0, nb // ub, trip, 0)


def _attn_fwd(z, m_init, l_init, q_blk, k_blk, v_blk, patterns, max_dist, gqa, name, comm=None):
    s = z.shape[0]
    npair = 3

    def body(q_ref, k_ref, v_ref, mi_ref, o_ref, lse0_ref, lse1_ref, bias_scr, m_scr, l_scr, *kv_scr):
        head0 = lax.broadcasted_iota(jnp.int32, (1, LANES), 1) < HEAD_DIM
        _write_band_bias(bias_scr, max_dist)
        ones = jnp.ones((2 * BLOCK, LANES), BF16)
        k_src, v_src = kv_scr if gqa else (k_ref, v_ref)
        if gqa:
            half = (lax.broadcasted_iota(jnp.int32, (1, LANES), 1) >= HEAD_DIM).astype(jnp.int32)
            swap = ((pl.program_id(0) + half) >> 1) != half

            def expand(c, carry):
                rows = pl.ds(pl.multiple_of(c * ATTN_ROWS, ATTN_ROWS), ATTN_ROWS)
                k_src[rows, :] = jnp.where(swap, pltpu.roll(k_ref[rows, :], HEAD_DIM, 1), k_ref[rows, :])
                v_src[rows, :] = jnp.where(swap, pltpu.roll(v_ref[rows, :], HEAD_DIM, 1), v_ref[rows, :])
                return carry

            lax.fori_loop(0, s // ATTN_ROWS, expand, 0)

        def one_block(b, d, r, first):
            rq, rp = _band_rows(b, d, r)
            q2 = _stack_heads(q_ref[rq, :] * SCALE, head0)
            k2 = jnp.concatenate([k_src[rp, :], k_src[rq, :]], axis=0).astype(BF16)
            v2 = jnp.concatenate([v_src[rp, :], v_src[rq, :]], axis=0).astype(BF16)
            sc = _dot_nt(q2, k2) + _band_bias(bias_scr, b)
            mb = jnp.max(sc, axis=1, keepdims=True)
            p = jnp.exp(sc - mb).astype(BF16)
            ob = _dot_nn(p, jnp.concatenate([v2, ones], axis=1))
            m_blk = _unstack_heads(jnp.broadcast_to(mb, (2 * BLOCK, LANES)), head0)
            l_blk = _unstack_heads(ob[:, LANES:], head0)
            o_blk = _unstack_heads(ob[:, :LANES], head0)
            if first and l_init == 0.0:
                m_new, l_new, o_new = m_blk, l_blk, o_blk
            else:
                if first:
                    m_old, l_old, o_old = jnp.broadcast_to(mi_ref[...], (BLOCK, LANES)), l_init, 0.0
                else:
                    m_old, l_old, o_old = m_scr[rq, :], l_scr[rq, :], o_ref[rq, :]
                m_new = jnp.maximum(m_old, m_blk)
                a_old = jnp.exp(m_old - m_new)
                a_blk = jnp.exp(m_blk - m_new)
                l_new = l_old * a_old + l_blk * a_blk
                o_new = o_old * a_old + o_blk * a_blk
            o_ref[rq, :], l_scr[rq, :], m_scr[rq, :] = o_new, l_new, m_new

        _block_loops(s, patterns, ATTN_UNROLL, one_block)

        def fin(c, carry):
            rows = pl.ds(pl.multiple_of(c * ATTN_ROWS, ATTN_ROWS), ATTN_ROWS)
            l = l_scr[rows, :]
            o_ref[rows, :] = o_ref[rows, :] / l
            lse = m_scr[rows, :] + jnp.log(l)
            swapped = pltpu.roll(lse, HEAD_DIM, 1)
            lse0_ref[rows, :] = jnp.where(head0, lse, swapped)
            lse1_ref[rows, :] = jnp.where(head0, swapped, lse)
            return carry

        lax.fori_loop(0, s // ATTN_ROWS, fin, 0)

    kv = (lambda blk: pl.BlockSpec((s, LANES), lambda j, blk=blk: (0, blk), pipeline_mode=pl.Buffered(1))) if gqa \
        else (lambda blk: pl.BlockSpec((s, LANES), lambda j, blk=blk: (0, blk + j)))
    own = pl.BlockSpec((s, LANES), lambda j: (0, j))
    return _call(
        body,
        grid=(npair,),
        in_specs=[pl.BlockSpec((s, LANES), lambda j: (0, q_blk + j)), kv(k_blk), kv(v_blk),
                  pl.BlockSpec((1, LANES), lambda j: (0, j))],
        out_specs=[own, own, own],
        out_shape=[jax.ShapeDtypeStruct((s, npair * LANES), F32)] * 3,
        operands=(z, z, z, m_init), name=name,
        scratch_shapes=[pltpu.VMEM((2 * BLOCK, 2 * BLOCK), F32)] + [pltpu.VMEM((s, LANES), F32)] * (4 if gqa else 2),
        comm=comm)


def _attn_bwd(z, do, o, lse, m_init, dz, q_blk, k_blk, v_blk, patterns, max_dist, gqa, name, comm=None):
    s = z.shape[0]
    npair = 3
    n_dz_in = 0 if dz is None else 1

    def body(q_ref, k_ref, v_ref, do_ref, o_ref, lse0_ref, lse1_ref, mi_ref, *rest):
        (dz_ref, dm_ref, dq_acc, dk_acc, dv_acc, dl0_scr, dl1_scr, bias_scr,
         dq_out, dk_out, dv_out, out_sems) = rest[n_dz_in:]
        pair = pl.program_id(0)
        head0 = lax.broadcasted_iota(jnp.int32, (1, LANES), 1) < HEAD_DIM
        halves = _kv_halves(pair) if gqa else None
        _write_band_bias(bias_scr, max_dist)

        def zero_kv():
            def f(c, carry):
                rows = pl.ds(pl.multiple_of(c * ATTN_ROWS, ATTN_ROWS), ATTN_ROWS)
                dk_acc[rows, :] = jnp.zeros((ATTN_ROWS, LANES), F32)
                dv_acc[rows, :] = jnp.zeros((ATTN_ROWS, LANES), F32)
                return carry
            lax.fori_loop(0, s // ATTN_ROWS, f, 0)

        if gqa:
            pl.when(pair == 0)(zero_kv)
        else:
            zero_kv()

        def prep(c, dm):
            rows = pl.ds(pl.multiple_of(c * ATTN_ROWS, ATTN_ROWS), ATTN_ROWS)
            dq_acc[rows, :] = jnp.zeros((ATTN_ROWS, LANES), F32)
            prod = do_ref[rows, :] * o_ref[rows, :]
            d0 = jnp.sum(jnp.where(head0, prod, 0.0), axis=1, keepdims=True)
            d1 = jnp.sum(jnp.where(head0, 0.0, prod), axis=1, keepdims=True)
            dl0_scr[rows, :] = jnp.broadcast_to(d0, (ATTN_ROWS, LANES))
            dl1_scr[rows, :] = jnp.broadcast_to(d1, (ATTN_ROWS, LANES))
            lse_own = jnp.where(head0, lse0_ref[rows, :], lse1_ref[rows, :])
            psink = jnp.exp(mi_ref[...] - lse_own)
            return dm - jnp.sum(psink * jnp.where(head0, d0, d1), axis=0, keepdims=True)

        dm_ref[...] = lax.fori_loop(0, s // ATTN_ROWS, prep, jnp.zeros((1, LANES), F32))

        def one_block(b, d, r, first):
            rq, rp = _band_rows(b, d, r)
            q2 = _stack_heads(q_ref[rq, :] * SCALE, head0, halves)
            do2 = _stack_heads(do_ref[rq, :], head0, halves)
            k2 = jnp.concatenate([k_ref[rp, :], k_ref[rq, :]], axis=0).astype(BF16)
            v2 = jnp.concatenate([v_ref[rp, :], v_ref[rq, :]], axis=0).astype(BF16)
            lse2 = jnp.concatenate([lse0_ref[rq, :], lse1_ref[rq, :]], axis=0)
            dl2 = jnp.concatenate([dl0_scr[rq, :], dl1_scr[rq, :]], axis=0)
            lse2 = jnp.concatenate([lse2, lse2], axis=1)
            dl2 = jnp.concatenate([dl2, dl2], axis=1)
            p = jnp.exp(_dot_nt(q2, k2) + _band_bias(bias_scr, b) - lse2)
            dp = _dot_nt(do2, v2)
            dsc = (p * (dp - dl2)).astype(BF16)
            dq2 = _unstack_heads(_dot_nn(dsc, k2), head0, halves)
            dk2 = _dot_tn(dsc, q2)
            dv2 = _dot_tn(p.astype(BF16), do2)
            dq_acc[rq, :] += dq2 * SCALE
            dk_acc[rp, :] += dk2[:BLOCK]
            dk_acc[rq, :] += dk2[BLOCK:]
            dv_acc[rp, :] += dv2[:BLOCK]
            dv_acc[rq, :] += dv2[BLOCK:]

        _block_loops(s, patterns, ATTN_UNROLL, one_block)

        def to_dz(staged, blk, k):
            cols = pl.ds(pl.multiple_of(blk * LANES, LANES), LANES)
            return pltpu.make_async_copy(staged, dz_ref.at[:, cols], out_sems.at[k])

        last_pair = pair == npair - 1
        q_copy = to_dz(dq_out, q_blk + pair, 0)
        kv_copies = [to_dz(dk_out, k_blk + (0 if gqa else pair), 1), to_dz(dv_out, v_blk + (0 if gqa else pair), 2)]

        @pl.when(pair > 0)
        def _():
            for cp in [q_copy] + ([] if gqa else kv_copies):
                cp.wait()

        def stage(acc, out):
            def f(c, carry):
                rows = pl.ds(pl.multiple_of(c * ATTN_ROWS, ATTN_ROWS), ATTN_ROWS)
                out[rows, :] = acc[rows, :].astype(BF16)
                return carry
            lax.fori_loop(0, s // ATTN_ROWS, f, 0)

        def stage_kv():
            stage(dk_acc, dk_out)
            stage(dv_acc, dv_out)
            for cp in kv_copies:
                cp.start()

        stage(dq_acc, dq_out)
        q_copy.start()
        if gqa:
            pl.when(last_pair)(stage_kv)
        else:
            stage_kv()

        @pl.when(last_pair)
        def _():
            for cp in [q_copy] + kv_copies:
                cp.wait()

    own = pl.BlockSpec((s, LANES), lambda j: (0, j))
    hbm = pl.BlockSpec(memory_space=pl.ANY)
    if gqa:
        kv = lambda blk: pl.BlockSpec((s, LANES), lambda j, blk=blk: (0, blk), pipeline_mode=pl.Buffered(1))
    else:
        kv = lambda blk: pl.BlockSpec((s, LANES), lambda j, blk=blk: (0, blk + j))
    in_specs = [pl.BlockSpec((s, LANES), lambda j: (0, q_blk + j)), kv(k_blk), kv(v_blk), own, own, own, own,
                pl.BlockSpec((1, LANES), lambda j: (0, j))]
    operands = (z, z, z, do, o, lse[0], lse[1], m_init)
    return _call(
        body,
        grid=(npair,),
        in_specs=in_specs + [hbm] * n_dz_in,
        out_specs=[hbm, pl.BlockSpec((1, LANES), lambda j: (0, j))],
        out_shape=[jax.ShapeDtypeStruct((s, IN_WIDTH), BF16), jax.ShapeDtypeStruct((1, npair * LANES), F32)],
        operands=operands + (() if dz is None else (dz,)), name=name,
        scratch_shapes=[pltpu.VMEM((s, LANES), F32)] * 5 + [pltpu.VMEM((2 * BLOCK, 2 * BLOCK), F32)]
        + [pltpu.VMEM((s, LANES), BF16)] * 3 + [pltpu.SemaphoreType.DMA((3,))],
        comm=comm, aliases={} if dz is None else {len(in_specs): 0})


def _adamw_math(w, g, m, v):
    m = ADAM_B1 * m + (1.0 - ADAM_B1) * g
    v = ADAM_B2 * v + (1.0 - ADAM_B2) * (g * g)
    m_hat = m / (1.0 - ADAM_B1 ** ADAM_STEP)
    v_hat = v / (1.0 - ADAM_B2 ** ADAM_STEP)
    delta = -ADAM_LR * (m_hat / (jnp.sqrt(v_hat) + ADAM_EPS) + ADAM_WD * w)
    return delta, m, v


def _adamw(w, g, m, v, name):
    rows, cols = w.shape
    tr = min(rows, 256)

    def body(w_ref, g_ref, m_ref, v_ref, d_ref, nm_ref, nv_ref):
        d_ref[...], nm_ref[...], nv_ref[...] = _adamw_math(w_ref[...], g_ref[...], m_ref[...], v_ref[...])

    spec = pl.BlockSpec((tr, cols), lambda i: (i, 0))
    return pl.pallas_call(
        body,
        grid=(rows // tr,),
        in_specs=[spec] * 4,
        out_specs=[spec] * 3,
        out_shape=[jax.ShapeDtypeStruct((rows, cols), F32)] * 3,
        compiler_params=_params("parallel"),
        name=name,
    )(w, g, m, v)


def _sum_adamw(parts, w, m, v, pos, transpose, name):
    assert len(parts) == DEPTH == 2
    (p0, r0), (p1, r1) = parts
    _, rows, cols = p0.shape
    tr = 256 if rows % 256 == 0 else rows
    nt = rows // tr

    def body(pos_ref, p0_ref, r0_ref, p1_ref, r1_ref, w_ref, m_ref, v_ref, g_ref, d_ref, nm_ref, nv_ref):
        def run(p_ref, r_ref):
            g = ((p_ref[...].astype(F32) + r_ref[0].astype(F32)) + r_ref[1].astype(F32)) + r_ref[2].astype(F32)
            if transpose:
                g = g.T
            g_ref[...] = g
            d_ref[...], nm_ref[...], nv_ref[...] = _adamw_math(w_ref[...], g, m_ref[...], v_ref[...])

        layer0 = pl.program_id(0) < nt
        pl.when(layer0)(lambda: run(p0_ref, r0_ref))
        pl.when(jnp.logical_not(layer0))(lambda: run(p1_ref, r1_ref))

    def tile0(i):
        return jnp.minimum(i, nt - 1)

    def tile1(i):
        return jnp.maximum(i - nt, 0)

    if transpose:
        w_spec = pl.BlockSpec((None, cols, tr), lambda i, q: (i // nt, 0, i % nt))
    else:
        w_spec = pl.BlockSpec((None, tr, cols), lambda i, q: (i // nt, i % nt, 0))
    return pl.pallas_call(
        body,
        grid_spec=pltpu.PrefetchScalarGridSpec(
            num_scalar_prefetch=1,
            grid=(DEPTH * nt,),
            in_specs=[pl.BlockSpec((None, tr, cols), lambda i, q: (q[0], tile0(i), 0)),
                      pl.BlockSpec((3, tr, cols), lambda i, q: (0, tile0(i), 0)),
                      pl.BlockSpec((None, tr, cols), lambda i, q: (q[0], tile1(i), 0)),
                      pl.BlockSpec((3, tr, cols), lambda i, q: (0, tile1(i), 0)),
                      w_spec, w_spec, w_spec],
            out_specs=[w_spec] * 4,
        ),
        out_shape=[jax.ShapeDtypeStruct(w.shape, F32)] * 4,
        compiler_params=_params("arbitrary"),
        name=name,
    )(pos, p0, r0, p1, r1, w, m, v)


def _small_sum_adamw(gathered, params, name):
    _, rows, cols = gathered.shape
    n = len(params)

    def body(ga_ref, *refs):
        ins, outs, (g_scr,) = refs[:3 * n], refs[3 * n:7 * n + 2], refs[7 * n + 2:]
        g = ga_ref[0]
        for i in range(1, N_DEV):
            g = g + ga_ref[i]
        g_scr[...] = g
        for k, (row0, w, _, _) in enumerate(params):
            w_ref, m_ref, v_ref = ins[3 * k:3 * k + 3]
            gk = g_scr[row0:row0 + w.shape[0], :]
            outs[4 * k][...] = gk
            outs[4 * k + 1][...], outs[4 * k + 2][...], outs[4 * k + 3][...] = _adamw_math(
                w_ref[...], gk, m_ref[...], v_ref[...])
        outs[4 * n][...] = g_scr[CONV_ROW:CONV_ROW + 8, :]
        outs[4 * n + 1][...] = g_scr[LOSS_ROW:LOSS_ROW + 1, :]

    out_shape = []
    for _, w, _, _ in params:
        out_shape += [jax.ShapeDtypeStruct(w.shape, F32)] * 4
    out_shape += [jax.ShapeDtypeStruct((8, cols), F32), jax.ShapeDtypeStruct((1, cols), F32)]
    res = pl.pallas_call(
        body,
        out_shape=out_shape,
        scratch_shapes=[pltpu.VMEM((rows, cols), F32)],
        name=name,
    )(gathered, *[t for _, w, m, v in params for t in (w, m, v)])
    return [res[4 * k:4 * k + 4] for k in range(n)], res[4 * n], res[4 * n + 1]


def _pair_sums(g4s, r1s, pos, name):
    n = len(g4s)

    def body(pos_ref, *refs):
        for g_ref, r_ref, o_ref in zip(refs[:n], refs[n:2 * n], refs[2 * n:]):
            o_ref[...] = (g_ref[...].astype(F32) + r_ref[...].astype(F32)).astype(BF16)

    block = lambda t: pl.BlockSpec((None,) + t.shape[1:], lambda i, p: (i, 0, 0))
    return pl.pallas_call(
        body,
        grid_spec=pltpu.PrefetchScalarGridSpec(
            num_scalar_prefetch=1,
            grid=(4,),
            in_specs=[pl.BlockSpec((None, None) + g.shape[2:], lambda i, p: (i, p[1], 0, 0)) for g in g4s]
            + [block(r) for r in r1s],
            out_specs=[block(r) for r in r1s],
        ),
        out_shape=[jax.ShapeDtypeStruct(r.shape, BF16) for r in r1s],
        compiler_params=_params("parallel"),
        name=name,
    )(pos, *g4s, *r1s)


GATHER_ID, CHIP_ID, SIBLING_ID = 0, 1, 2


def _place():
    return lax.axis_index("x"), lax.axis_index("y"), lax.axis_index("c")


def _sibling():
    x, y, c = _place()
    return (x, y, 1 - c)


def _same_core_of_other_chips():
    x, y, c = _place()
    return [(1 - x, y, c), (x, 1 - y, c), (1 - x, 1 - y, c)]


def _gather_comm(shards):
    na = len(shards)
    stacks, index = zip(*shards)

    def plan(ins, outs, sems):
        send_sems, recv_sems, local_sems = sems
        x, y, c = _place()
        me, sibling = (x, y, c), (x, y, 1 - c)
        chips = [(1 - x, y), (x, 1 - y), (1 - x, 1 - y)]
        shard = [ins[a].at[index[a]] for a in range(na)]

        def rows(a, px, py, pc):
            m = shard[a].shape[0]
            return outs[a].at[pl.ds((4 * px + 2 * py + pc) * m, m), :]

        def copy(a, k, block, to, src=None):
            return pltpu.make_async_remote_copy(
                src_ref=rows(a, *block) if src is None else src, dst_ref=rows(a, *block),
                send_sem=send_sems.at[a, k], recv_sem=recv_sems.at[a, k], device_id=to, device_id_type=MESH)

        def mine():
            return [pltpu.make_async_copy(shard[a], rows(a, *me), local_sems.at[a]) for a in range(na)]

        def first():
            return [copy(a, k, me, to, src=shard[a]) for a in range(na)
                    for k, to in enumerate([sibling] + [(*chip, c) for chip in chips])]

        def passed_on():
            return [copy(a, 4 + j, (*chip, c), sibling) for j, chip in enumerate(chips) for a in range(na)]

        return me, sibling, chips, c, copy, mine, first, passed_on

    def start(ins, outs, sems):
        *_, mine, first, _ = plan(ins, outs, sems)
        for cp in mine() + first():
            cp.start()

    def relay(ins, outs, sems):
        me, _, chips, c, copy, _, _, passed_on = plan(ins, outs, sems)
        for cp, (j, a) in zip(passed_on(), [(j, a) for j in range(3) for a in range(na)]):
            copy(a, 1 + j, (*chips[j], c), me).wait_recv()
            cp.start()

    def finish(ins, outs, sems):
        me, sibling, chips, c, copy, mine, first, passed_on = plan(ins, outs, sems)
        for a in range(na):
            copy(a, 0, sibling, me).wait_recv()
            for j, chip in enumerate(chips):
                copy(a, 4 + j, (*chip, 1 - c), me).wait_recv()
        for cp in first() + passed_on():
            cp.wait_send()
        for cp in mine():
            cp.wait()

    return _Comm(tuple(stacks),
                 tuple(jax.ShapeDtypeStruct((N_DEV * t.shape[1], t.shape[2]), t.dtype) for t in stacks),
                 (pltpu.SemaphoreType.DMA((na, 7)), pltpu.SemaphoreType.DMA((na, 7)), pltpu.SemaphoreType.DMA((na,))),
                 start, relay, finish, lambda: [_sibling()] + _same_core_of_other_chips(), GATHER_ID)


def _exchange_comm(arrays, out_shape, n_copies, copies_of, peers, collective_id):
    na = len(arrays)

    def every(ins, outs, sems):
        send_sems, recv_sems = sems
        return [cp for a in range(na) for cp in copies_of(ins, outs, a, send_sems, recv_sems)]

    def start(ins, outs, sems):
        for cp in every(ins, outs, sems):
            cp.start()

    def finish(ins, outs, sems):
        for cp in every(ins, outs, sems):
            cp.wait()

    return _Comm(tuple(arrays), tuple(out_shape),
                 (pltpu.SemaphoreType.DMA((na, n_copies)), pltpu.SemaphoreType.DMA((na, n_copies))), start,
                 lambda ins, outs, sems: None, finish, peers, collective_id)


def _sibling_comm(grads):
    def copies_of(ins, outs, a, send_sems, recv_sems):
        x, y, c = _place()
        return [pltpu.make_async_remote_copy(
            src_ref=ins[a].at[chip, 1 - c], dst_ref=outs[a].at[chip],
            send_sem=send_sems.at[a, chip], recv_sem=recv_sems.at[a, chip],
            device_id=(x, y, 1 - c), device_id_type=MESH) for chip in range(4)]

    return _exchange_comm(grads, [jax.ShapeDtypeStruct((4,) + t.shape[2:], t.dtype) for t in grads], 4, copies_of,
                          lambda: [_sibling()], SIBLING_ID)


def _chip_comm(partials):
    def copies_of(ins, outs, a, send_sems, recv_sems):
        x, y, c = _place()
        chips = [(1 - x, y), (x, 1 - y), (1 - x, 1 - y)]
        return [pltpu.make_async_remote_copy(
            src_ref=ins[a].at[2 * cx + cy], dst_ref=outs[a].at[k],
            send_sem=send_sems.at[a, k], recv_sem=recv_sems.at[a, k],
            device_id=(cx, cy, c), device_id_type=MESH) for k, (cx, cy) in enumerate(chips)]

    return _exchange_comm(partials, [jax.ShapeDtypeStruct((3,) + t.shape[1:], t.dtype) for t in partials], 3, copies_of,
                          _same_core_of_other_chips, CHIP_ID)


def _pad_rows(t, rows):
    return jnp.pad(t, ((0, rows - t.shape[0]), (0, D_MODEL - t.shape[1])))


MIX_ROW, GROUP_ROW, MLP_ROW, FINAL_ROW, CONV_ROW, SINK_ROW = 0, 8, 16, 24, 32, 40
LOSS_ROW = FINAL_ROW + 1


def _pack_small(g_mix, g_group, g_mlp, g_final, conv, sinks, loss):
    final_and_loss = jnp.concatenate([g_final.reshape(1, D_MODEL), _pad_rows(loss, 1)], axis=0)
    return jnp.concatenate([
        _pad_rows(g_mix, 8), _pad_rows(g_group, 8), _pad_rows(g_mlp, 8), _pad_rows(final_and_loss, 8),
        _pad_rows(conv.reshape(DEPTH * 3, CONV_CH), 8), _pad_rows(sinks.reshape(1, DEPTH * 6), 8)], axis=0)


def kernel(x, w_in, conv_w, sinks, g_mix, g_group, w_o, g_mlp, w_ff_in, w_ff_out, g_final, loss_target, m_w_in, m_conv_w, m_sinks, m_g_mix, m_g_group, m_w_o, m_g_mlp, m_w_ff_in, m_w_ff_out, m_g_final, v_w_in, v_conv_w, v_sinks, v_g_mix, v_g_group, v_w_o, v_g_mlp, v_w_ff_in, v_w_ff_out, v_g_final):
    ax, ay, ac = _place()
    chip = 2 * ax + ay
    dev = 4 * ax + 2 * ay + ac
    pos = jnp.stack([chip, ac]).astype(jnp.int32)

    x0 = x.reshape(SEQ, D_MODEL)
    target = loss_target.reshape(SEQ, D_MODEL)

    stacks = [jnp.swapaxes(w_in, 1, 2).astype(BF16), w_o.astype(BF16),
              jnp.swapaxes(w_ff_in, 1, 2).astype(BF16), w_ff_out.astype(BF16)]
    shards = {(l, kind): (stack, l) for kind, stack in enumerate(stacks) for l in range(DEPTH)}
    conv_tile = jnp.pad(conv_w.reshape(DEPTH * 3, CONV_CH // N_DEV), ((0, 2), (0, LANES - CONV_CH // N_DEV)))
    wt_in0, conv_all = _comm_only(_gather_comm([shards[0, 0], (conv_tile[None], 0)]), "gather_first")
    conv_full = conv_all.reshape(N_DEV, 8, LANES)[:, :DEPTH * 3, :CONV_CH // N_DEV]
    conv_full = conv_full.transpose(1, 0, 2).reshape(DEPTH, 3, CONV_CH)

    dx, parts, small = _step(x0, target, shards, wt_in0, conv_full, sinks, g_mix, g_group, g_mlp, g_final, pos)
    return _finish(dx, parts, small, pos, dev, w_in, conv_w, sinks, g_mix, g_group, w_o, g_mlp, w_ff_in, w_ff_out, g_final, m_w_in, m_conv_w, m_sinks, m_g_mix, m_g_group, m_w_o, m_g_mlp, m_w_ff_in, m_w_ff_out, m_g_final, v_w_in, v_conv_w, v_sinks, v_g_mix, v_g_group, v_w_o, v_g_mlp, v_w_ff_in, v_w_ff_out, v_g_final)


FWD_CARRY = {(0, "in_proj"): ((1, 0),), (0, "window"): ((0, 1),), (0, "dilated"): ((0, 2),),
             (0, "mix_ff_in"): ((0, 3),), (0, "ff_out_in_proj"): ((1, 1), (1, 3)),
             (1, "dilated"): ((1, 2),)}


def _step(x0, target, shards, wt_in0, conv_full, sinks, g_mix, g_group, g_mlp, g_final, pos):
    sink_lanes = jnp.repeat(sinks.reshape(DEPTH, 6), HEAD_DIM, axis=1)
    no_sink = jnp.full((1, A_WIDTH), NEG_BIG, F32)
    full = {(0, 0): wt_in0}

    def gather(stage, l):
        keys = FWD_CARRY.get((l, stage), ())
        return keys, (_gather_comm([shards[k] for k in keys]) if keys else None)

    def landed(keys, got):
        full.update(zip(keys, got))

    saved = []
    xc = x0
    keys, comm = gather("in_proj", 0)
    (z, h), got = _norm_mm(xc, g_mix[0:1], full[0, 0], "in_proj_0", comm)
    landed(keys, got)
    for l in range(DEPTH):
        sink_l = sink_lanes[l:l + 1]
        keys, comm = gather("window", l)
        (yc, *lse_c), got = _attn_fwd(z, sink_l, 1.0, QC_BLK, KC_BLK, VC_BLK, (1,), C_MAX_DIST, True,
                                     f"window_attn_{l}", comm)
        landed(keys, got)
        yb = _conv_fwd(z, conv_full[l], f"conv_{l}")
        keys, comm = gather("dilated", l)
        (ya, *lse_a), got = _attn_fwd(z, no_sink, 0.0, QA_BLK, KA_BLK, VA_BLK, DILATED_PATTERNS, A_MAX_DIST, False,
                                     f"dilated_attn_{l}", comm)
        landed(keys, got)
        keys, comm = gather("mix_ff_in", l)
        (y, x1, a, h2), got = _mix_ff_in(ya, yb, yc, g_group[l:l + 1], full[l, 1], xc, g_mlp[l:l + 1], full[l, 2],
                                         f"mix_ff_in_{l}", comm)
        landed(keys, got)
        saved.append((xc, z, h, ya, lse_a, yb, yc, lse_c, sink_l, y, x1, a, h2))
        if l + 1 < DEPTH:
            keys, comm = gather("ff_out_in_proj", l)
            (xc, z, h), got = _ff_out_in_proj(a, full[l, 3], x1, g_mix[l + 1:l + 2], full[l + 1, 0],
                                              f"ff_out_{l}_in_proj_{l + 1}", comm)
            landed(keys, got)

    loss_slab, dx, dxb, dg_final, du = _mm_res_loss(a, full[DEPTH - 1, 3], x1, g_final.reshape(1, D_MODEL), target,
                                                    f"ff_out_{DEPTH - 1}_loss")

    def by_owner(t):
        return t.reshape(4, 2, t.shape[0] // N_DEV, D_MODEL)

    def pair(l, kinds, grads, received):
        sums = _pair_sums(grads, received, pos, f"grad_pair_sums_{l}_{kinds[0]}{kinds[1]}")
        partial.update({(l, kind): t for kind, t in zip(kinds, sums)})

    partial, r2 = {}, {}
    dg_mix, dg_group, dg_mlp, dconv, dsinks = [None] * DEPTH, [None] * DEPTH, [None] * DEPTH, [None] * DEPTH, [None] * DEPTH
    for l in reversed(range(DEPTH)):
        xin, z, h, ya, lse_a, yb, yc, lse_c, sink_l, y, x1, a, h2 = saved[l]
        late = [(l + 1, 1), (l + 1, 0)] if l + 1 < DEPTH else []
        if l + 1 < DEPTH:
            (du,), _ = _mlp_bwd_act(dxb, full[l, 3], a, f"ff_out_bwd_{l}")
        (g3, g2), got = _mm_tn([(a, dxb), (du, h2)], f"grad_w_ff_{l}",
                               _chip_comm([partial[k] for k in late]) if late else None)
        r2.update(zip(late, got))
        g3, g2 = by_owner(g3), by_owner(g2)
        (dx1, dx1b, dg_mlp[l], dya, dyb, dyc, dg_group[l]), got = _ff_in_mix_bwd(
            du, full[l, 2], x1, dx, g_mlp[l:l + 1], full[l, 1], ya, yb, yc, g_group[l:l + 1],
            f"ff_in_mix_bwd_{l}", _sibling_comm([g3, g2]))
        pair(l, (3, 2), [g3, g2], got)
        early = [(l, 3), (l, 2)]
        (dz, _), got = _attn_bwd(z, dya, ya, lse_a, no_sink, None, QA_BLK, KA_BLK, VA_BLK, DILATED_PATTERNS,
                                 A_MAX_DIST, False, f"dilated_attn_bwd_{l}", _chip_comm([partial[k] for k in early]))
        r2.update(zip(early, got))
        dz, dcw = _conv_bwd(z, conv_full[l], dyb, dz, f"conv_bwd_{l}")
        (dz, dsink), _ = _attn_bwd(z, dyc, yc, lse_c, sink_l, dz, QC_BLK, KC_BLK, VC_BLK, (1,), C_MAX_DIST,
                                   True, f"window_attn_bwd_{l}")
        (g1, g0), _ = _mm_tn([(y, dx1b), (dz, h)], f"grad_w_o_in_{l}")
        g1, g0 = by_owner(g1), by_owner(g0)
        if l > 0:
            (dx, dxb, dg_mix[l]), got = _mm_nn_normbwd(dz, full[l, 0], xin, dx1, g_mix[l:l + 1], f"in_proj_bwd_{l}",
                                                      _sibling_comm([g1, g0]))
            pair(l, (1, 0), [g1, g0], got)
        else:
            got = _comm_only(_sibling_comm([g1, g0]), "grad_sibling_exchange_last")
            pair(l, (1, 0), [g1, g0], got)
            (dx, dxb, dg_mix[l]), got = _mm_nn_normbwd(dz, full[l, 0], xin, dx1, g_mix[l:l + 1], f"in_proj_bwd_{l}",
                                                      _chip_comm([partial[l, 1], partial[l, 0]]))
            r2[l, 1], r2[l, 0] = got
        dconv[l] = dcw[:3]
        dsinks[l] = dsink[0, ::HEAD_DIM]
    parts = {key: (partial[key], r2[key]) for key in partial}
    small = _pack_small(jnp.concatenate(dg_mix), jnp.concatenate(dg_group), jnp.concatenate(dg_mlp),
                        dg_final, jnp.stack(dconv), jnp.stack(dsinks), loss_slab[0:1])
    return dx, parts, small


def _finish(dx, parts, small, pos, dev, w_in, conv_w, sinks, g_mix, g_group, w_o, g_mlp, w_ff_in, w_ff_out, g_final, m_w_in, m_conv_w, m_sinks, m_g_mix, m_g_group, m_w_o, m_g_mlp, m_w_ff_in, m_w_ff_out, m_g_final, v_w_in, v_conv_w, v_sinks, v_g_mix, v_g_group, v_w_o, v_g_mlp, v_w_ff_in, v_w_ff_out, v_g_final):
    grad_x = dx.reshape(1, SEQ, D_MODEL)

    (small_all,) = _comm_only(_gather_comm([(small[None], 0)]), "gather_small_grads")
    row = lambda t: t.reshape(1, D_MODEL)
    sink_row = lambda t: _pad_rows(t.reshape(1, DEPTH * 6), 1)
    params = [(MIX_ROW, g_mix, m_g_mix, v_g_mix), (GROUP_ROW, g_group, m_g_group, v_g_group),
              (MLP_ROW, g_mlp, m_g_mlp, v_g_mlp), (FINAL_ROW, row(g_final), row(m_g_final), row(v_g_final)),
              (SINK_ROW, sink_row(sinks), sink_row(m_sinks), sink_row(v_sinks))]
    updated, conv_rows, loss_row = _small_sum_adamw(small_all.reshape(N_DEV, SMALL_ROWS, D_MODEL), params, "small_adamw")
    loss = loss_row[0, 0]
    (grad_g_mix, delta_g_mix, new_m_g_mix, new_v_g_mix), (grad_g_group, delta_g_group, new_m_g_group, new_v_g_group), \
        (grad_g_mlp, delta_g_mlp, new_m_g_mlp, new_v_g_mlp), final4, sinks4 = updated
    grad_g_final, delta_g_final, new_m_g_final, new_v_g_final = [t.reshape(D_MODEL) for t in final4]
    grad_sinks, delta_sinks, new_m_sinks, new_v_sinks = [t[0, :DEPTH * 6].reshape(DEPTH, 2, 3) for t in sinks4]
    conv_grad_full = conv_rows[:DEPTH * 3, :CONV_CH].reshape(DEPTH, 3, CONV_CH)
    cs = CONV_CH // N_DEV
    grad_conv_w = lax.dynamic_slice_in_dim(conv_grad_full, dev * cs, cs, axis=2)

    def tile_of(t):
        return jnp.pad(t.reshape(1, DEPTH * 3 * cs), ((0, 7), (0, 256 - DEPTH * 3 * cs)))

    cd, cm, cv = _adamw(tile_of(conv_w), tile_of(grad_conv_w), tile_of(m_conv_w), tile_of(v_conv_w), "conv_adamw")
    untile = lambda t: t[0, :DEPTH * 3 * cs].reshape(DEPTH, 3, cs)
    delta_conv_w, new_m_conv_w, new_v_conv_w = untile(cd), untile(cm), untile(cv)

    def big(kind, w, m, v, transpose, name):
        return _sum_adamw([parts[l, kind] for l in range(DEPTH)], w, m, v, pos, transpose, name)

    swap = lambda t: jnp.swapaxes(t, 1, 2)
    grad_w_in, delta_w_in, new_m_w_in, new_v_w_in = [
        swap(t) for t in big(0, swap(w_in), swap(m_w_in), swap(v_w_in), False, "adamw_w_in")]
    grad_w_o, delta_w_o, new_m_w_o, new_v_w_o = big(1, w_o, m_w_o, v_w_o, False, "adamw_w_o")
    grad_w_ff_in, delta_w_ff_in, new_m_w_ff_in, new_v_w_ff_in = big(2, w_ff_in, m_w_ff_in, v_w_ff_in, True, "adamw_w_ff_in")
    grad_w_ff_out, delta_w_ff_out, new_m_w_ff_out, new_v_w_ff_out = big(3, w_ff_out, m_w_ff_out, v_w_ff_out, False,
                                                                         "adamw_w_ff_out")

    return (loss, grad_x, grad_w_in, grad_conv_w, grad_sinks, grad_g_mix, grad_g_group, grad_w_o, grad_g_mlp,
            grad_w_ff_in, grad_w_ff_out, grad_g_final,
            delta_w_in, delta_conv_w, delta_sinks, delta_g_mix, delta_g_group, delta_w_o, delta_g_mlp,
            delta_w_ff_in, delta_w_ff_out, delta_g_final,
            new_m_w_in, new_m_conv_w, new_m_sinks, new_m_g_mix, new_m_g_group, new_m_w_o, new_m_g_mlp,
            new_m_w_ff_in, new_m_w_ff_out, new_m_g_final,
            new_v_w_in, new_v_conv_w, new_v_sinks, new_v_g_mix, new_v_g_group, new_v_w_o, new_v_g_mlp,
            new_v_w_ff_in, new_v_w_ff_out, new_v_g_final)
```

```python
from typing import Callable, NamedTuple

import jax
import jax.numpy as jnp
from jax import lax
from jax.experimental import pallas as pl
from jax.experimental.pallas import tpu as pltpu

F32 = jnp.float32
BF16 = jnp.bfloat16
MESH = pl.DeviceIdType.MESH

N_DEV = 8
SEQ = 4096
D_MODEL = 1024
DEPTH = 2
HEAD_DIM = 64
LANES = 128
A_WIDTH = 384
CONV_CH = 256
C_WIDTH = 384
IN_WIDTH = 2560
BLOCK = 128
DILATED_PATTERNS = (1, 4, 16)
A_MAX_DIST = 128
C_MAX_DIST = 127
EPS = 1e-6
SCALE = HEAD_DIM ** -0.5
NEG_BIG = -1e30
F32_TINY = 1.1754944e-38

QA_BLK, KA_BLK, VA_BLK = 0, 3, 6
GB_BLK, GC_BLK, XB_BLK = 9, 11, 13
QC_BLK, KC_BLK, VC_BLK = 15, 18, 19

ADAM_LR = 0.001
ADAM_B1 = 0.9
ADAM_B2 = 0.999
ADAM_EPS = 1e-08
ADAM_WD = 0.01
ADAM_STEP = 10

VMEM_LIMIT = 56 * 1024 * 1024
TILE_BUDGET = 46 * 1024 * 1024
ROW_TILE = 512
COL_CHUNK = 512
SMALL_ROWS = 48


def _dot_nn(a, b):
    return lax.dot_general(a, b, (((1,), (0,)), ((), ())), preferred_element_type=F32)


def _dot_nt(a, b):
    return lax.dot_general(a, b, (((1,), (1,)), ((), ())), preferred_element_type=F32)


def _dot_tn(a, b):
    return lax.dot_general(a, b, (((0,), (0,)), ((), ())), preferred_element_type=F32)


def _params(*sem, collective_id=None):
    return pltpu.CompilerParams(dimension_semantics=sem, vmem_limit_bytes=VMEM_LIMIT, collective_id=collective_id)


def _resident(shape):
    return pl.BlockSpec(shape, lambda i: (0,) * len(shape), pipeline_mode=pl.Buffered(1))


class _Late(NamedTuple):
    hbm: object
    vmem: object
    sem: object

    def fetch(self, needed_at_step):
        copy = pltpu.make_async_copy(self.hbm, self.vmem, self.sem)
        pl.when(pl.program_id(0) == 0)(copy.start)
        return lambda: pl.when(pl.program_id(0) == needed_at_step)(copy.wait)


LATE_SPEC = pl.BlockSpec(memory_space=pl.ANY)


def _late_scratch(t):
    return [pltpu.VMEM(t.shape, t.dtype), pltpu.SemaphoreType.DMA(())]


def _row_tile(row_bytes, resident_bytes):
    for tm in (ROW_TILE, ROW_TILE // 2):
        if 2 * tm * row_bytes + resident_bytes <= TILE_BUDGET:
            return tm
    return ROW_TILE // 4


def _rms_scale(t):
    return lax.rsqrt(jnp.mean(t * t, axis=-1, keepdims=True) + EPS)


def _rms_bwd(n, r, dn):
    return r * (dn - n * jnp.mean(dn * n, axis=-1, keepdims=True))


class _Comm(NamedTuple):
    arrays: tuple
    out_shape: tuple
    sems: tuple
    start: Callable
    relay: Callable
    finish: Callable
    peers: Callable
    collective_id: int
    relay_early: bool


def _handshake(comm):
    barrier = pltpu.get_barrier_semaphore()
    peers = comm.peers()
    for peer in peers:
        pl.semaphore_signal(barrier, inc=1, device_id=peer, device_id_type=MESH)
    pl.semaphore_wait(barrier, len(peers))


def _call(body, grid, in_specs, out_specs, out_shape, operands, name, scratch_shapes=(), comm=None, aliases=None):
    n_in, n_out, n_scr = len(in_specs), len(out_shape), len(scratch_shapes)
    aliases = dict(aliases or {})
    if comm is None:
        res = pl.pallas_call(body, grid=grid, in_specs=list(in_specs), out_specs=list(out_specs),
                             out_shape=list(out_shape), scratch_shapes=list(scratch_shapes),
                             input_output_aliases=aliases,
                             compiler_params=_params("arbitrary"), name=name)(*operands)
        return list(res), []
    c_in, c_out = len(comm.arrays), len(comm.out_shape)
    hbm = pl.BlockSpec(memory_space=pl.ANY)
    last = grid[0] - 1

    def carried(*refs):
        ins, cins = refs[:n_in], refs[n_in:n_in + c_in]
        o0 = n_in + c_in
        outs, couts = refs[o0:o0 + n_out], refs[o0 + n_out:o0 + n_out + c_out]
        s0 = o0 + n_out + c_out
        scr, sems = refs[s0:s0 + n_scr], refs[s0 + n_scr:]
        @pl.when(pl.program_id(0) == 0)
        def _():
            _handshake(comm)
            comm.start(cins, couts, sems)

        if comm.relay_early:
            pl.when(pl.program_id(0) == last)(lambda: comm.relay(cins, couts, sems))
        body(*ins, *outs, *scr)

        @pl.when(pl.program_id(0) == last)
        def _():
            if not comm.relay_early:
                comm.relay(cins, couts, sems)
            comm.finish(cins, couts, sems)

    res = pl.pallas_call(carried, grid=grid, in_specs=list(in_specs) + [hbm] * c_in,
                         out_specs=list(out_specs) + [hbm] * c_out, out_shape=list(out_shape) + list(comm.out_shape),
                         scratch_shapes=list(scratch_shapes) + list(comm.sems), input_output_aliases=aliases,
                         compiler_params=_params("arbitrary", collective_id=comm.collective_id),
                         name=name)(*operands, *comm.arrays)
    return list(res[:n_out]), list(res[n_out:])


def _comm_only(comm, name):
    hbm = pl.BlockSpec(memory_space=pl.ANY)
    c_in, c_out = len(comm.arrays), len(comm.out_shape)

    def body(*refs):
        ins, outs, sems = refs[:c_in], refs[c_in:c_in + c_out], refs[c_in + c_out:]
        _handshake(comm)
        comm.start(ins, outs, sems)
        comm.relay(ins, outs, sems)
        comm.finish(ins, outs, sems)

    return pl.pallas_call(body, in_specs=[hbm] * c_in, out_specs=[hbm] * c_out, out_shape=list(comm.out_shape),
                          scratch_shapes=list(comm.sems),
                          compiler_params=pltpu.CompilerParams(collective_id=comm.collective_id),
                          name=name)(*comm.arrays)


def _norm_mm(x, g, wt, name, comm=None):
    s, d = x.shape
    n = wt.shape[0]
    tm = _row_tile(4 * d + 4 * n + 2 * d, 2 * n * d)

    def body(x_ref, g_ref, w_ref, o_ref, h_ref):
        xx = x_ref[...]
        h = ((xx * _rms_scale(xx)) * g_ref[...]).astype(BF16)
        h_ref[...] = h
        for n0 in range(0, n, COL_CHUNK):
            o_ref[:, n0:n0 + COL_CHUNK] = _dot_nt(h, w_ref[n0:n0 + COL_CHUNK, :])

    return _call(
        body,
        grid=(s // tm,),
        in_specs=[pl.BlockSpec((tm, d), lambda i: (i, 0)),
                  pl.BlockSpec((1, d), lambda i: (0, 0)),
                  _resident((n, d))],
        out_specs=[pl.BlockSpec((tm, n), lambda i: (i, 0)),
                   pl.BlockSpec((tm, d), lambda i: (i, 0))],
        out_shape=[jax.ShapeDtypeStruct((s, n), F32), jax.ShapeDtypeStruct((s, d), BF16)],
        operands=(x, g, wt), name=name, comm=comm)


def _ff_out_in_proj(a, w2, x1, g, wt, name, comm=None):
    s, f = a.shape
    d = w2.shape[1]
    n = wt.shape[0]
    tm = _row_tile(2 * f + 4 * d + 4 * d + 4 * n + 2 * d, 2 * f * d + 2 * n * d)

    def body(a_ref, w2_ref, x_ref, g_ref, w_ref, x2_ref, z_ref, h_ref):
        x2 = x_ref[...] + _dot_nn(a_ref[...], w2_ref[...])
        x2_ref[...] = x2
        h = ((x2 * _rms_scale(x2)) * g_ref[...]).astype(BF16)
        h_ref[...] = h
        for n0 in range(0, n, COL_CHUNK):
            z_ref[:, n0:n0 + COL_CHUNK] = _dot_nt(h, w_ref[n0:n0 + COL_CHUNK, :])

    rows = lambda w: pl.BlockSpec((tm, w), lambda i: (i, 0))
    return _call(
        body,
        grid=(s // tm,),
        in_specs=[rows(f), _resident((f, d)), rows(d), pl.BlockSpec((1, d), lambda i: (0, 0)), _resident((n, d))],
        out_specs=[rows(d), rows(n), rows(d)],
        out_shape=[jax.ShapeDtypeStruct((s, d), F32), jax.ShapeDtypeStruct((s, n), F32),
                   jax.ShapeDtypeStruct((s, d), BF16)],
        operands=(a, w2, x1, g, wt), name=name, comm=comm)


def _mix_ff_in(ya, yb, yc, gg, wo, x0, g_mlp, wt1, name, comm=None):
    s = ya.shape[0]
    d = wo.shape[1]
    f = wt1.shape[0]
    tm = _row_tile(4 * d + 4 * d + 2 * d + 4 * d + 2 * d + 2 * f, 2 * d * d + 2 * f * d)

    def body(ya_ref, yb_ref, yc_ref, gg_ref, wo_ref, x_ref, g_ref, w1_ref, y_ref, x1_ref, a_ref, h_ref):
        parts = []
        for ref in (ya_ref, yb_ref, yc_ref):
            t = ref[...]
            parts.append(t * _rms_scale(t))
        y = (jnp.concatenate(parts, axis=1) * gg_ref[...]).astype(BF16)
        y_ref[...] = y
        x1 = x_ref[...] + _dot_nn(y, wo_ref[...])
        x1_ref[...] = x1
        h = ((x1 * _rms_scale(x1)) * g_ref[...]).astype(BF16)
        h_ref[...] = h
        for n0 in range(0, f, COL_CHUNK):
            u = _dot_nt(h, w1_ref[n0:n0 + COL_CHUNK, :])
            a_ref[:, n0:n0 + COL_CHUNK] = jnp.square(jnp.maximum(u, 0.0)).astype(BF16)

    rows = lambda w: pl.BlockSpec((tm, w), lambda i: (i, 0))
    vec = pl.BlockSpec((1, d), lambda i: (0, 0))
    return _call(
        body,
        grid=(s // tm,),
        in_specs=[rows(A_WIDTH), rows(CONV_CH), rows(C_WIDTH), vec, _resident((d, d)), rows(d), vec, _resident((f, d))],
        out_specs=[rows(d), rows(d), rows(f), rows(d)],
        out_shape=[jax.ShapeDtypeStruct((s, d), BF16), jax.ShapeDtypeStruct((s, d), F32),
                   jax.ShapeDtypeStruct((s, f), BF16), jax.ShapeDtypeStruct((s, d), BF16)],
        operands=(ya, yb, yc, gg, wo, x0, g_mlp, wt1), name=name, comm=comm)


def _relu_from_square(av):
    return av * lax.rsqrt(jnp.maximum(av, F32_TINY))


def _mm_res_loss(a, w2, x1, g, target, name):
    s, f = a.shape
    d = w2.shape[1]
    tm = _row_tile(2 * f + 4 * d + 4 * d + 4 * d + 2 * d + 2 * f, 2 * f * d)

    def body(a_ref, w_ref, x_ref, g_ref, t_ref, loss_ref, dx_ref, dxb_ref, dg_ref, du_ref):
        @pl.when(pl.program_id(0) == 0)
        def _():
            loss_ref[...] = jnp.zeros_like(loss_ref)
            dg_ref[...] = jnp.zeros_like(dg_ref)

        xx = x_ref[...] + _dot_nn(a_ref[...], w_ref[...])
        r = _rms_scale(xx)
        n = xx * r
        gv = g_ref[...]
        err = n * gv - t_ref[...]
        per_tok = jnp.sum(err * err, axis=1, keepdims=True) * (1.0 / d)
        loss_ref[...] += 0.5 * jnp.sum(per_tok, axis=0, keepdims=True)
        dout = err * (1.0 / d)
        dg_ref[...] += jnp.sum(dout * n, axis=0, keepdims=True)
        dx = _rms_bwd(n, r, dout * gv)
        dx_ref[...] = dx
        dxb = dx.astype(BF16)
        dxb_ref[...] = dxb
        for n0 in range(0, f, COL_CHUNK):
            da = _dot_nt(dxb, w_ref[n0:n0 + COL_CHUNK, :])
            rl = _relu_from_square(a_ref[:, n0:n0 + COL_CHUNK].astype(F32))
            du_ref[:, n0:n0 + COL_CHUNK] = (da * (2.0 * rl)).astype(BF16)

    rows = lambda w: pl.BlockSpec((tm, w), lambda i: (i, 0))
    vec = pl.BlockSpec((1, d), lambda i: (0, 0))
    return pl.pallas_call(
        body,
        grid=(s // tm,),
        in_specs=[rows(f), _resident((f, d)), rows(d), vec, rows(d)],
        out_specs=[pl.BlockSpec((8, LANES), lambda i: (0, 0)), rows(d), rows(d), vec, rows(f)],
        out_shape=[jax.ShapeDtypeStruct((8, LANES), F32), jax.ShapeDtypeStruct((s, d), F32),
                   jax.ShapeDtypeStruct((s, d), BF16), jax.ShapeDtypeStruct((1, d), F32),
                   jax.ShapeDtypeStruct((s, f), BF16)],
        compiler_params=_params("arbitrary"),
        name=name,
    )(a, w2, x1, g, target)


def _mlp_bwd_act(dxb, w2, a, name, comm=None):
    s, d = dxb.shape
    f = w2.shape[0]
    tm = _row_tile(2 * d + 2 * f + 2 * f, 2 * f * d)

    def body(dx_ref, w_ref, a_ref, du_ref):
        dx = dx_ref[...]
        for n0 in range(0, f, COL_CHUNK):
            da = _dot_nt(dx, w_ref[n0:n0 + COL_CHUNK, :])
            rl = _relu_from_square(a_ref[:, n0:n0 + COL_CHUNK].astype(F32))
            du_ref[:, n0:n0 + COL_CHUNK] = (da * (2.0 * rl)).astype(BF16)

    return _call(
        body,
        grid=(s // tm,),
        in_specs=[pl.BlockSpec((tm, d), lambda i: (i, 0)),
                  _resident((f, d)),
                  pl.BlockSpec((tm, f), lambda i: (i, 0))],
        out_specs=[pl.BlockSpec((tm, f), lambda i: (i, 0))],
        out_shape=[jax.ShapeDtypeStruct((s, f), BF16)],
        operands=(dxb, w2, a), name=name, comm=comm)


def _mm_tn(pairs, name, comm=None):
    s, d = pairs[0][1].shape
    tn = 512
    tiles = [a.shape[1] // tn for a, _ in pairs]
    starts = [sum(tiles[:k]) for k in range(len(pairs))]

    def body(*refs):
        ins, outs = refs[:2 * len(pairs)], refs[2 * len(pairs):3 * len(pairs)]
        acc, late = refs[3 * len(pairs)], refs[3 * len(pairs) + 1:]
        j = pl.program_id(0)
        b_refs = [ins[1]]
        for k in range(1, len(pairs)):
            b_late = _Late(ins[2 * k + 1], late[2 * k - 2], late[2 * k - 1])
            b_late.fetch(starts[k])()
            b_refs.append(b_late.vmem)
        for k in range(len(pairs)):
            def run(a_ref=ins[2 * k], b_ref=b_refs[k], o_ref=outs[k]):
                for k0 in range(0, s, ROW_TILE):
                    part = _dot_tn(a_ref[k0:k0 + ROW_TILE, :], b_ref[k0:k0 + ROW_TILE, :])
                    if k0 == 0:
                        acc[...] = part
                    else:
                        acc[...] += part
                o_ref[...] = acc[...].astype(BF16)

            pl.when((j >= starts[k]) & (j < starts[k] + tiles[k]))(run)

    def tile_of(k):
        return lambda j: jnp.clip(j - starts[k], 0, tiles[k] - 1)

    in_specs, out_specs = [], []
    for k in range(len(pairs)):
        in_specs += [pl.BlockSpec((s, tn), lambda j, t=tile_of(k): (0, t(j))), _resident((s, d)) if k == 0 else LATE_SPEC]
        out_specs.append(pl.BlockSpec((tn, d), lambda j, t=tile_of(k): (t(j), 0)))
    late = [scratch for _, b in pairs[1:] for scratch in _late_scratch(b)]
    return _call(
        body,
        grid=(sum(tiles),),
        in_specs=in_specs,
        out_specs=out_specs,
        out_shape=[jax.ShapeDtypeStruct((a.shape[1], d), BF16) for a, _ in pairs],
        operands=tuple(t for pair in pairs for t in pair), name=name,
        scratch_shapes=[pltpu.VMEM((tn, d), F32)] + late, comm=comm)


def _mm_nn_normbwd(dact, wt, x, dres, g, name, comm=None):
    s, kdim = dact.shape
    d = wt.shape[1]
    tm = _row_tile(2 * kdim + 4 * d + 4 * d + 4 * d + 2 * d, 2 * kdim * d)

    def body(a_ref, w_ref, x_ref, r_ref, g_ref, o_ref, ob_ref, dg_ref):
        @pl.when(pl.program_id(0) == 0)
        def _():
            dg_ref[...] = jnp.zeros_like(dg_ref)

        dh = _dot_nn(a_ref[...], w_ref[...])
        xx = x_ref[...]
        r = _rms_scale(xx)
        n = xx * r
        dg_ref[...] += jnp.sum(dh * n, axis=0, keepdims=True)
        dx = r_ref[...] + _rms_bwd(n, r, dh * g_ref[...])
        o_ref[...] = dx
        ob_ref[...] = dx.astype(BF16)

    return _call(
        body,
        grid=(s // tm,),
        in_specs=[pl.BlockSpec((tm, kdim), lambda i: (i, 0)),
                  _resident((kdim, d)),
                  pl.BlockSpec((tm, d), lambda i: (i, 0)),
                  pl.BlockSpec((tm, d), lambda i: (i, 0)),
                  pl.BlockSpec((1, d), lambda i: (0, 0))],
        out_specs=[pl.BlockSpec((tm, d), lambda i: (i, 0)),
                   pl.BlockSpec((tm, d), lambda i: (i, 0)),
                   pl.BlockSpec((1, d), lambda i: (0, 0))],
        out_shape=[jax.ShapeDtypeStruct((s, d), F32), jax.ShapeDtypeStruct((s, d), BF16),
                   jax.ShapeDtypeStruct((1, d), F32)],
        operands=(dact, wt, x, dres, g), name=name, comm=comm)


def _ff_in_mix_bwd(du, wt1, x1, dres, g_mlp, wo, ya, yb, yc, gg, name, comm=None):
    s, f = du.shape
    d = wt1.shape[1]
    widths = (A_WIDTH, CONV_CH, C_WIDTH)
    tm = _row_tile(2 * f + 4 * d + 4 * d + 4 * d + 2 * d + 4 * d + 4 * d, 2 * f * d + 2 * d * d)

    def body(du_ref, w1_ref, x_ref, r_ref, g_ref, wo_ref, ya_ref, yb_ref, yc_ref, gg_ref,
             dx_ref, dxb_ref, dg_ref, da_ref, db_ref, dc_ref, dgg_ref):
        @pl.when(pl.program_id(0) == 0)
        def _():
            dg_ref[...] = jnp.zeros_like(dg_ref)
            dgg_ref[...] = jnp.zeros_like(dgg_ref)

        dh = _dot_nn(du_ref[...], w1_ref[...])
        xx = x_ref[...]
        r = _rms_scale(xx)
        n = xx * r
        dg_ref[...] += jnp.sum(dh * n, axis=0, keepdims=True)
        dx = r_ref[...] + _rms_bwd(n, r, dh * g_ref[...])
        dx_ref[...] = dx
        dxb = dx.astype(BF16)
        dxb_ref[...] = dxb

        dy = _dot_nt(dxb, wo_ref[...])
        gv = gg_ref[...]
        off = 0
        dgs = []
        for ref, out, w in zip((ya_ref, yb_ref, yc_ref), (da_ref, db_ref, dc_ref), widths):
            t = ref[...]
            r = _rms_scale(t)
            n = t * r
            dyg = dy[:, off:off + w]
            dgs.append(jnp.sum(dyg * n, axis=0, keepdims=True))
            out[...] = _rms_bwd(n, r, dyg * gv[:, off:off + w])
            off += w
        dgg_ref[...] += jnp.concatenate(dgs, axis=1)

    rows = lambda w: pl.BlockSpec((tm, w), lambda i: (i, 0))
    vec = pl.BlockSpec((1, d), lambda i: (0, 0))
    return _call(
        body,
        grid=(s // tm,),
        in_specs=[rows(f), _resident((f, d)), rows(d), rows(d), vec, _resident((d, d)),
                  rows(A_WIDTH), rows(CONV_CH), rows(C_WIDTH), vec],
        out_specs=[rows(d), rows(d), vec, rows(A_WIDTH), rows(CONV_CH), rows(C_WIDTH), vec],
        out_shape=[jax.ShapeDtypeStruct((s, d), F32), jax.ShapeDtypeStruct((s, d), BF16), jax.ShapeDtypeStruct((1, d), F32),
                   jax.ShapeDtypeStruct((s, A_WIDTH), F32), jax.ShapeDtypeStruct((s, CONV_CH), F32),
                   jax.ShapeDtypeStruct((s, C_WIDTH), F32), jax.ShapeDtypeStruct((1, d), F32)],
        operands=(du, wt1, x1, dres, g_mlp, wo, ya, yb, yc, gg), name=name, comm=comm)


CONV_CHUNK = 256
CONV_HALO = 8


def _conv_fwd(z, cw, name):
    s = z.shape[0]
    nch = s // CONV_CHUNK

    def body(gb_ref, gc_ref, xb_ref, w_ref, o_ref, us):
        us[pl.ds(0, CONV_HALO), :] = jnp.zeros((CONV_HALO, LANES), F32)
        us[pl.ds(CONV_HALO, s), :] = gc_ref[...] * xb_ref[...]
        w0, w1, w2 = w_ref[0:1, :], w_ref[1:2, :], w_ref[2:3, :]

        def chunk(c, carry):
            st = pl.multiple_of(c * CONV_CHUNK, CONV_CHUNK)
            ext = us[pl.ds(st, CONV_CHUNK + CONV_HALO), :]
            y = (w0 * ext[CONV_HALO - 2:CONV_HALO - 2 + CONV_CHUNK]
                 + w1 * ext[CONV_HALO - 1:CONV_HALO - 1 + CONV_CHUNK]
                 + w2 * ext[CONV_HALO:])
            o_ref[pl.ds(st, CONV_CHUNK), :] = gb_ref[pl.ds(st, CONV_CHUNK), :] * y
            return carry

        lax.fori_loop(0, nch, chunk, 0)

    col = lambda blk: pl.BlockSpec((s, LANES), lambda j, blk=blk: (0, blk + j))
    return pl.pallas_call(
        body,
        grid=(CONV_CH // LANES,),
        in_specs=[col(GB_BLK), col(GC_BLK), col(XB_BLK), pl.BlockSpec((3, LANES), lambda j: (0, j))],
        out_specs=pl.BlockSpec((s, LANES), lambda j: (0, j)),
        out_shape=jax.ShapeDtypeStruct((s, CONV_CH), F32),
        scratch_shapes=[pltpu.VMEM((s + CONV_HALO, LANES), F32)],
        compiler_params=_params("parallel"),
        name=name,
    )(z, z, z, cw)


def _conv_bwd(z, cw, dyb, dz, name):
    s = z.shape[0]
    nch = s // CONV_CHUNK
    ncol = CONV_CH // LANES

    def body(gb_ref, gc_ref, xb_ref, w_ref, dy_ref, dz_in, dz_ref, dw_ref, us, ds_, dgb_ref, dgc_ref, dxb_ref, sems):
        j = pl.program_id(0)

        def to_dz(staged, blk, k):
            cols = pl.ds(pl.multiple_of((blk + j) * LANES, LANES), LANES)
            return pltpu.make_async_copy(staged, dz_ref.at[:, cols], sems.at[k])

        copies = [to_dz(dgb_ref, GB_BLK, 0), to_dz(dgc_ref, GC_BLK, 1), to_dz(dxb_ref, XB_BLK, 2)]

        @pl.when(j > 0)
        def _():
            for cp in copies:
                cp.wait()

        us[pl.ds(0, CONV_HALO), :] = jnp.zeros((CONV_HALO, LANES), F32)
        us[pl.ds(CONV_HALO, s), :] = gc_ref[...] * xb_ref[...]
        ds_[pl.ds(s, CONV_HALO), :] = jnp.zeros((CONV_HALO, LANES), F32)
        ds_[pl.ds(0, s), :] = dy_ref[...] * gb_ref[...]
        w0, w1, w2 = w_ref[0:1, :], w_ref[1:2, :], w_ref[2:3, :]
        zero = jnp.zeros((1, LANES), F32)

        def chunk(c, carry):
            a0, a1, a2 = carry
            st = pl.multiple_of(c * CONV_CHUNK, CONV_CHUNK)
            rows = pl.ds(st, CONV_CHUNK)
            ext = us[pl.ds(st, CONV_CHUNK + CONV_HALO), :]
            um2 = ext[CONV_HALO - 2:CONV_HALO - 2 + CONV_CHUNK]
            um1 = ext[CONV_HALO - 1:CONV_HALO - 1 + CONV_CHUNK]
            u0 = ext[CONV_HALO:]
            dext = ds_[pl.ds(st, CONV_CHUNK + CONV_HALO), :]
            dc0 = dext[:CONV_CHUNK]
            du = w2 * dc0 + w1 * dext[1:1 + CONV_CHUNK] + w0 * dext[2:2 + CONV_CHUNK]
            yconv = w0 * um2 + w1 * um1 + w2 * u0
            dgb_ref[rows, :] = (dy_ref[rows, :] * yconv).astype(BF16)
            dgc_ref[rows, :] = (du * xb_ref[rows, :]).astype(BF16)
            dxb_ref[rows, :] = (du * gc_ref[rows, :]).astype(BF16)
            a0 = a0 + jnp.sum(dc0 * um2, axis=0, keepdims=True)
            a1 = a1 + jnp.sum(dc0 * um1, axis=0, keepdims=True)
            a2 = a2 + jnp.sum(dc0 * u0, axis=0, keepdims=True)
            return a0, a1, a2

        a0, a1, a2 = lax.fori_loop(0, nch, chunk, (zero, zero, zero))
        dw_ref[...] = jnp.concatenate([a0, a1, a2, jnp.zeros((5, LANES), F32)], axis=0)
        for cp in copies:
            cp.start()

        @pl.when(j == ncol - 1)
        def _():
            for cp in copies:
                cp.wait()

    col = lambda blk: pl.BlockSpec((s, LANES), lambda j, blk=blk: (0, blk + j))
    hbm = pl.BlockSpec(memory_space=pl.ANY)
    return pl.pallas_call(
        body,
        grid=(ncol,),
        in_specs=[col(GB_BLK), col(GC_BLK), col(XB_BLK), pl.BlockSpec((3, LANES), lambda j: (0, j)),
                  pl.BlockSpec((s, LANES), lambda j: (0, j)), hbm],
        out_specs=[hbm, pl.BlockSpec((8, LANES), lambda j: (0, j))],
        out_shape=[jax.ShapeDtypeStruct(dz.shape, dz.dtype), jax.ShapeDtypeStruct((8, CONV_CH), F32)],
        scratch_shapes=[pltpu.VMEM((s + CONV_HALO, LANES), F32), pltpu.VMEM((s + CONV_HALO, LANES), F32)]
        + [pltpu.VMEM((s, LANES), BF16)] * 3 + [pltpu.SemaphoreType.DMA((3,))],
        input_output_aliases={5: 0},
        compiler_params=_params("arbitrary"),
        name=name,
    )(z, z, z, cw, dyb, dz)


ATTN_ROWS = 512
ATTN_UNROLL = 8


def _band_rows(b, d, r):
    base = pl.multiple_of(b * (BLOCK * d), BLOCK)
    prev = jnp.maximum(base - BLOCK * d, 0)
    if d == 1:
        return pl.ds(base, BLOCK), pl.ds(pl.multiple_of(prev, BLOCK), BLOCK)
    return pl.ds(base + r, BLOCK, stride=d), pl.ds(prev + r, BLOCK, stride=d)


def _write_band_bias(bias_ref, max_dist):
    qi = lax.broadcasted_iota(jnp.int32, (BLOCK, 2 * BLOCK), 0)
    kj = lax.broadcasted_iota(jnp.int32, (BLOCK, 2 * BLOCK), 1)
    dist = BLOCK + qi - kj
    band = (dist >= 0) & (dist <= max_dist)
    bias_ref[0:BLOCK, :] = jnp.where(band, 0.0, -jnp.inf)
    bias_ref[BLOCK:2 * BLOCK, :] = jnp.where(band & (kj >= BLOCK), 0.0, -jnp.inf)


def _band_bias(bias_ref, b):
    bias = bias_ref[pl.ds(pl.multiple_of(jnp.where(b > 0, 0, BLOCK), BLOCK), BLOCK), :]
    return jnp.concatenate([bias, bias], axis=0)


def _kv_halves(pair):
    zero = jnp.zeros((1, LANES), jnp.int32)
    return zero + (pair >> 1), zero + ((pair + 1) >> 1)


def _stack_heads(t, head0, halves=None):
    top, bottom = jnp.where(head0, t, 0.0), jnp.where(head0, 0.0, t)
    if halves is not None:
        top = jnp.where(halves[0] == 1, pltpu.roll(top, HEAD_DIM, 1), top)
        bottom = jnp.where(halves[1] == 0, pltpu.roll(bottom, HEAD_DIM, 1), bottom)
    return jnp.concatenate([top, bottom], axis=0).astype(BF16)


def _unstack_heads(t, head0, halves=None):
    top, bottom = t[:BLOCK], t[BLOCK:]
    if halves is not None:
        top = jnp.where(halves[0] == 1, pltpu.roll(top, HEAD_DIM, 1), top)
        bottom = jnp.where(halves[1] == 0, pltpu.roll(bottom, HEAD_DIM, 1), bottom)
    return jnp.where(head0, top, bottom)


def _block_loops(s, patterns, unroll, one_block):
    for n, d in enumerate(patterns):
        nb = (s // BLOCK) // d
        ur = min(unroll, d)
        ub = unroll // ur
        for r0 in range(0, d, ur):
            def trip(i, carry, n=n, d=d, r0=r0, ur=ur, ub=ub):
                for u in range(ub):
                    for r in range(r0, r0 + ur):
                        one_block(i * ub + u, d, r, n == 0)
                return carry
            lax.fori_loop(0, nb // ub, trip, 0)


def _attn_fwd(z, m_init, l_init, q_blk, k_blk, v_blk, patterns, max_dist, gqa, name, comm=None):
    s = z.shape[0]
    npair = 3

    def body(q_ref, k_ref, v_ref, mi_ref, o_ref, lse0_ref, lse1_ref, bias_scr, m_scr, l_scr, *kv_scr):
        head0 = lax.broadcasted_iota(jnp.int32, (1, LANES), 1) < HEAD_DIM
        _write_band_bias(bias_scr, max_dist)
        ones = jnp.ones((2 * BLOCK, LANES), BF16)
        k_src, v_src = kv_scr if gqa else (k_ref, v_ref)
        if gqa:
            half = (lax.broadcasted_iota(jnp.int32, (1, LANES), 1) >= HEAD_DIM).astype(jnp.int32)
            swap = ((pl.program_id(0) + half) >> 1) != half

            def expand(c, carry):
                rows = pl.ds(pl.multiple_of(c * ATTN_ROWS, ATTN_ROWS), ATTN_ROWS)
                k_src[rows, :] = jnp.where(swap, pltpu.roll(k_ref[rows, :], HEAD_DIM, 1), k_ref[rows, :])
                v_src[rows, :] = jnp.where(swap, pltpu.roll(v_ref[rows, :], HEAD_DIM, 1), v_ref[rows, :])
                return carry

            lax.fori_loop(0, s // ATTN_ROWS, expand, 0)

        def one_block(b, d, r, first):
            rq, rp = _band_rows(b, d, r)
            q2 = _stack_heads(q_ref[rq, :] * SCALE, head0)
            k2 = jnp.concatenate([k_src[rp, :], k_src[rq, :]], axis=0).astype(BF16)
            v2 = jnp.concatenate([v_src[rp, :], v_src[rq, :]], axis=0).astype(BF16)
            sc = _dot_nt(q2, k2) + _band_bias(bias_scr, b)
            mb = jnp.max(sc, axis=1, keepdims=True)
            p = jnp.exp(sc - mb).astype(BF16)
            ob = _dot_nn(p, jnp.concatenate([v2, ones], axis=1))
            m_blk = _unstack_heads(jnp.broadcast_to(mb, (2 * BLOCK, LANES)), head0)
            l_blk = _unstack_heads(ob[:, LANES:], head0)
            o_blk = _unstack_heads(ob[:, :LANES], head0)
            if first and l_init == 0.0:
                m_new, l_new, o_new = m_blk, l_blk, o_blk
            else:
                if first:
                    m_old, l_old, o_old = jnp.broadcast_to(mi_ref[...], (BLOCK, LANES)), l_init, 0.0
                else:
                    m_old, l_old, o_old = m_scr[rq, :], l_scr[rq, :], o_ref[rq, :]
                m_new = jnp.maximum(m_old, m_blk)
                a_old = jnp.exp(m_old - m_new)
                a_blk = jnp.exp(m_blk - m_new)
                l_new = l_old * a_old + l_blk * a_blk
                o_new = o_old * a_old + o_blk * a_blk
            o_ref[rq, :], l_scr[rq, :], m_scr[rq, :] = o_new, l_new, m_new

        _block_loops(s, patterns, ATTN_UNROLL, one_block)

        def fin(c, carry):
            rows = pl.ds(pl.multiple_of(c * ATTN_ROWS, ATTN_ROWS), ATTN_ROWS)
            l = l_scr[rows, :]
            o_ref[rows, :] = o_ref[rows, :] / l
            lse = m_scr[rows, :] + jnp.log(l)
            swapped = pltpu.roll(lse, HEAD_DIM, 1)
            lse0_ref[rows, :] = jnp.where(head0, lse, swapped)
            lse1_ref[rows, :] = jnp.where(head0, swapped, lse)
            return carry

        lax.fori_loop(0, s // ATTN_ROWS, fin, 0)

    kv = (lambda blk: pl.BlockSpec((s, LANES), lambda j, blk=blk: (0, blk), pipeline_mode=pl.Buffered(1))) if gqa \
        else (lambda blk: pl.BlockSpec((s, LANES), lambda j, blk=blk: (0, blk + j)))
    own = pl.BlockSpec((s, LANES), lambda j: (0, j))
    return _call(
        body,
        grid=(npair,),
        in_specs=[pl.BlockSpec((s, LANES), lambda j: (0, q_blk + j)), kv(k_blk), kv(v_blk),
                  pl.BlockSpec((1, LANES), lambda j: (0, j))],
        out_specs=[own, own, own],
        out_shape=[jax.ShapeDtypeStruct((s, npair * LANES), F32)] * 3,
        operands=(z, z, z, m_init), name=name,
        scratch_shapes=[pltpu.VMEM((2 * BLOCK, 2 * BLOCK), F32)] + [pltpu.VMEM((s, LANES), F32)] * (4 if gqa else 2),
        comm=comm)


def _attn_bwd(z, do, o, lse, m_init, dz, q_blk, k_blk, v_blk, patterns, max_dist, gqa, name, comm=None):
    s = z.shape[0]
    npair = 3
    n_dz_in = 0 if dz is None else 1

    def body(q_ref, k_ref, v_ref, do_ref, o_ref, lse0_ref, lse1_ref, mi_ref, *rest):
        (dz_ref, dm_ref, dq_acc, dk_acc, dv_acc, dl0_scr, dl1_scr, bias_scr,
         dq_out, dk_out, dv_out, out_sems) = rest[n_dz_in:]
        pair = pl.program_id(0)
        head0 = lax.broadcasted_iota(jnp.int32, (1, LANES), 1) < HEAD_DIM
        halves = _kv_halves(pair) if gqa else None
        _write_band_bias(bias_scr, max_dist)

        def zero_kv():
            def f(c, carry):
                rows = pl.ds(pl.multiple_of(c * ATTN_ROWS, ATTN_ROWS), ATTN_ROWS)
                dk_acc[rows, :] = jnp.zeros((ATTN_ROWS, LANES), F32)
                dv_acc[rows, :] = jnp.zeros((ATTN_ROWS, LANES), F32)
                return carry
            lax.fori_loop(0, s // ATTN_ROWS, f, 0)

        if gqa:
            pl.when(pair == 0)(zero_kv)
        else:
            zero_kv()

        def prep(c, dm):
            rows = pl.ds(pl.multiple_of(c * ATTN_ROWS, ATTN_ROWS), ATTN_ROWS)
            dq_acc[rows, :] = jnp.zeros((ATTN_ROWS, LANES), F32)
            prod = do_ref[rows, :] * o_ref[rows, :]
            d0 = jnp.sum(jnp.where(head0, prod, 0.0), axis=1, keepdims=True)
            d1 = jnp.sum(jnp.where(head0, 0.0, prod), axis=1, keepdims=True)
            dl0_scr[rows, :] = jnp.broadcast_to(d0, (ATTN_ROWS, LANES))
            dl1_scr[rows, :] = jnp.broadcast_to(d1, (ATTN_ROWS, LANES))
            lse_own = jnp.where(head0, lse0_ref[rows, :], lse1_ref[rows, :])
            psink = jnp.exp(mi_ref[...] - lse_own)
            return dm - jnp.sum(psink * jnp.where(head0, d0, d1), axis=0, keepdims=True)

        dm_ref[...] = lax.fori_loop(0, s // ATTN_ROWS, prep, jnp.zeros((1, LANES), F32))

        def one_block(b, d, r, first):
            rq, rp = _band_rows(b, d, r)
            q2 = _stack_heads(q_ref[rq, :] * SCALE, head0, halves)
            do2 = _stack_heads(do_ref[rq, :], head0, halves)
            k2 = jnp.concatenate([k_ref[rp, :], k_ref[rq, :]], axis=0).astype(BF16)
            v2 = jnp.concatenate([v_ref[rp, :], v_ref[rq, :]], axis=0).astype(BF16)
            lse2 = jnp.concatenate([lse0_ref[rq, :], lse1_ref[rq, :]], axis=0)
            dl2 = jnp.concatenate([dl0_scr[rq, :], dl1_scr[rq, :]], axis=0)
            lse2 = jnp.concatenate([lse2, lse2], axis=1)
            dl2 = jnp.concatenate([dl2, dl2], axis=1)
            p = jnp.exp(_dot_nt(q2, k2) + _band_bias(bias_scr, b) - lse2)
            dp = _dot_nt(do2, v2)
            dsc = (p * (dp - dl2)).astype(BF16)
            dq2 = _unstack_heads(_dot_nn(dsc, k2), head0, halves)
            dk2 = _dot_tn(dsc, q2)
            dv2 = _dot_tn(p.astype(BF16), do2)
            dq_acc[rq, :] += dq2 * SCALE
            dk_acc[rp, :] += dk2[:BLOCK]
            dk_acc[rq, :] += dk2[BLOCK:]
            dv_acc[rp, :] += dv2[:BLOCK]
            dv_acc[rq, :] += dv2[BLOCK:]

        _block_loops(s, patterns, ATTN_UNROLL, one_block)

        def to_dz(staged, blk, k):
            cols = pl.ds(pl.multiple_of(blk * LANES, LANES), LANES)
            return pltpu.make_async_copy(staged, dz_ref.at[:, cols], out_sems.at[k])

        last_pair = pair == npair - 1
        q_copy = to_dz(dq_out, q_blk + pair, 0)
        kv_copies = [to_dz(dk_out, k_blk + (0 if gqa else pair), 1), to_dz(dv_out, v_blk + (0 if gqa else pair), 2)]

        @pl.when(pair > 0)
        def _():
            for cp in [q_copy] + ([] if gqa else kv_copies):
                cp.wait()

        def stage(acc, out):
            def f(c, carry):
                rows = pl.ds(pl.multiple_of(c * ATTN_ROWS, ATTN_ROWS), ATTN_ROWS)
                out[rows, :] = acc[rows, :].astype(BF16)
                return carry
            lax.fori_loop(0, s // ATTN_ROWS, f, 0)

        def stage_kv():
            stage(dk_acc, dk_out)
            stage(dv_acc, dv_out)
            for cp in kv_copies:
                cp.start()

        stage(dq_acc, dq_out)
        q_copy.start()
        if gqa:
            pl.when(last_pair)(stage_kv)
        else:
            stage_kv()

        @pl.when(last_pair)
        def _():
            for cp in [q_copy] + kv_copies:
                cp.wait()

    own = pl.BlockSpec((s, LANES), lambda j: (0, j))
    hbm = pl.BlockSpec(memory_space=pl.ANY)
    if gqa:
        kv = lambda blk: pl.BlockSpec((s, LANES), lambda j, blk=blk: (0, blk), pipeline_mode=pl.Buffered(1))
    else:
        kv = lambda blk: pl.BlockSpec((s, LANES), lambda j, blk=blk: (0, blk + j))
    in_specs = [pl.BlockSpec((s, LANES), lambda j: (0, q_blk + j)), kv(k_blk), kv(v_blk), own, own, own, own,
                pl.BlockSpec((1, LANES), lambda j: (0, j))]
    operands = (z, z, z, do, o, lse[0], lse[1], m_init)
    return _call(
        body,
        grid=(npair,),
        in_specs=in_specs + [hbm] * n_dz_in,
        out_specs=[hbm, pl.BlockSpec((1, LANES), lambda j: (0, j))],
        out_shape=[jax.ShapeDtypeStruct((s, IN_WIDTH), BF16), jax.ShapeDtypeStruct((1, npair * LANES), F32)],
        operands=operands + (() if dz is None else (dz,)), name=name,
        scratch_shapes=[pltpu.VMEM((s, LANES), F32)] * 5 + [pltpu.VMEM((2 * BLOCK, 2 * BLOCK), F32)]
        + [pltpu.VMEM((s, LANES), BF16)] * 3 + [pltpu.SemaphoreType.DMA((3,))],
        comm=comm, aliases={} if dz is None else {len(in_specs): 0})


def _adamw_math(w, g, m, v):
    m = ADAM_B1 * m + (1.0 - ADAM_B1) * g
    v = ADAM_B2 * v + (1.0 - ADAM_B2) * (g * g)
    m_hat = m / (1.0 - ADAM_B1 ** ADAM_STEP)
    v_hat = v / (1.0 - ADAM_B2 ** ADAM_STEP)
    delta = -ADAM_LR * (m_hat / (jnp.sqrt(v_hat) + ADAM_EPS) + ADAM_WD * w)
    return delta, m, v


def _adamw(w, g, m, v, name):
    rows, cols = w.shape
    tr = min(rows, 256)

    def body(w_ref, g_ref, m_ref, v_ref, d_ref, nm_ref, nv_ref):
        d_ref[...], nm_ref[...], nv_ref[...] = _adamw_math(w_ref[...], g_ref[...], m_ref[...], v_ref[...])

    spec = pl.BlockSpec((tr, cols), lambda i: (i, 0))
    return pl.pallas_call(
        body,
        grid=(rows // tr,),
        in_specs=[spec] * 4,
        out_specs=[spec] * 3,
        out_shape=[jax.ShapeDtypeStruct((rows, cols), F32)] * 3,
        compiler_params=_params("parallel"),
        name=name,
    )(w, g, m, v)


def _sum_adamw(parts, w, m, v, pos, transpose, name):
    assert len(parts) == DEPTH == 2
    (p0, r0), (p1, r1) = parts
    _, rows, cols = p0.shape
    tr = 256 if rows % 256 == 0 else rows
    nt = rows // tr

    def body(pos_ref, p0_ref, r0_ref, p1_ref, r1_ref, w_ref, m_ref, v_ref, g_ref, d_ref, nm_ref, nv_ref):
        def run(p_ref, r_ref):
            g = ((p_ref[...].astype(F32) + r_ref[0].astype(F32)) + r_ref[1].astype(F32)) + r_ref[2].astype(F32)
            if transpose:
                g = g.T
            g_ref[...] = g
            d_ref[...], nm_ref[...], nv_ref[...] = _adamw_math(w_ref[...], g, m_ref[...], v_ref[...])

        layer0 = pl.program_id(0) < nt
        pl.when(layer0)(lambda: run(p0_ref, r0_ref))
        pl.when(jnp.logical_not(layer0))(lambda: run(p1_ref, r1_ref))

    def tile0(i):
        return jnp.minimum(i, nt - 1)

    def tile1(i):
        return jnp.maximum(i - nt, 0)

    if transpose:
        w_spec = pl.BlockSpec((None, cols, tr), lambda i, q: (i // nt, 0, i % nt))
    else:
        w_spec = pl.BlockSpec((None, tr, cols), lambda i, q: (i // nt, i % nt, 0))
    return pl.pallas_call(
        body,
        grid_spec=pltpu.PrefetchScalarGridSpec(
            num_scalar_prefetch=1,
            grid=(DEPTH * nt,),
            in_specs=[pl.BlockSpec((None, tr, cols), lambda i, q: (q[0], tile0(i), 0)),
                      pl.BlockSpec((3, tr, cols), lambda i, q: (0, tile0(i), 0)),
                      pl.BlockSpec((None, tr, cols), lambda i, q: (q[0], tile1(i), 0)),
                      pl.BlockSpec((3, tr, cols), lambda i, q: (0, tile1(i), 0)),
                      w_spec, w_spec, w_spec],
            out_specs=[w_spec] * 4,
        ),
        out_shape=[jax.ShapeDtypeStruct(w.shape, F32)] * 4,
        compiler_params=_params("arbitrary"),
        name=name,
    )(pos, p0, r0, p1, r1, w, m, v)


def _small_sum_adamw(gathered, params, name):
    _, rows, cols = gathered.shape
    n = len(params)

    def body(ga_ref, *refs):
        ins, outs, (g_scr,) = refs[:3 * n], refs[3 * n:7 * n + 2], refs[7 * n + 2:]
        g = ga_ref[0]
        for i in range(1, N_DEV):
            g = g + ga_ref[i]
        g_scr[...] = g
        for k, (row0, w, _, _) in enumerate(params):
            w_ref, m_ref, v_ref = ins[3 * k:3 * k + 3]
            gk = g_scr[row0:row0 + w.shape[0], :]
            outs[4 * k][...] = gk
            outs[4 * k + 1][...], outs[4 * k + 2][...], outs[4 * k + 3][...] = _adamw_math(
                w_ref[...], gk, m_ref[...], v_ref[...])
        outs[4 * n][...] = g_scr[CONV_ROW:CONV_ROW + 8, :]
        outs[4 * n + 1][...] = g_scr[LOSS_ROW:LOSS_ROW + 1, :]

    out_shape = []
    for _, w, _, _ in params:
        out_shape += [jax.ShapeDtypeStruct(w.shape, F32)] * 4
    out_shape += [jax.ShapeDtypeStruct((8, cols), F32), jax.ShapeDtypeStruct((1, cols), F32)]
    res = pl.pallas_call(
        body,
        out_shape=out_shape,
        scratch_shapes=[pltpu.VMEM((rows, cols), F32)],
        name=name,
    )(gathered, *[t for _, w, m, v in params for t in (w, m, v)])
    return [res[4 * k:4 * k + 4] for k in range(n)], res[4 * n], res[4 * n + 1]


def _pair_sums(g4s, r1s, pos, name):
    n = len(g4s)

    def body(pos_ref, *refs):
        for g_ref, r_ref, o_ref in zip(refs[:n], refs[n:2 * n], refs[2 * n:]):
            o_ref[...] = (g_ref[...].astype(F32) + r_ref[...].astype(F32)).astype(BF16)

    block = lambda t: pl.BlockSpec((None,) + t.shape[1:], lambda i, p: (i, 0, 0))
    return pl.pallas_call(
        body,
        grid_spec=pltpu.PrefetchScalarGridSpec(
            num_scalar_prefetch=1,
            grid=(4,),
            in_specs=[pl.BlockSpec((None, None) + g.shape[2:], lambda i, p: (i, p[1], 0, 0)) for g in g4s]
            + [block(r) for r in r1s],
            out_specs=[block(r) for r in r1s],
        ),
        out_shape=[jax.ShapeDtypeStruct(r.shape, BF16) for r in r1s],
        compiler_params=_params("parallel"),
        name=name,
    )(pos, *g4s, *r1s)


GATHER_ID, CHIP_ID, SIBLING_ID = 0, 1, 2


def _place():
    return lax.axis_index("x"), lax.axis_index("y"), lax.axis_index("c")


def _sibling():
    x, y, c = _place()
    return (x, y, 1 - c)


def _same_core_of_other_chips():
    x, y, c = _place()
    return [(1 - x, y, c), (x, 1 - y, c), (1 - x, 1 - y, c)]


def _gather_comm(shards, relay_early=False):
    na = len(shards)
    stacks, index = zip(*shards)

    def plan(ins, outs, sems):
        send_sems, recv_sems, local_sems = sems
        x, y, c = _place()
        me, sibling = (x, y, c), (x, y, 1 - c)
        chips = [(1 - x, y), (x, 1 - y), (1 - x, 1 - y)]
        shard = [ins[a].at[index[a]] for a in range(na)]

        def rows(a, px, py, pc):
            m = shard[a].shape[0]
            return outs[a].at[pl.ds((4 * px + 2 * py + pc) * m, m), :]

        def copy(a, k, block, to, src=None):
            return pltpu.make_async_remote_copy(
                src_ref=rows(a, *block) if src is None else src, dst_ref=rows(a, *block),
                send_sem=send_sems.at[a, k], recv_sem=recv_sems.at[a, k], device_id=to, device_id_type=MESH)

        def mine():
            return [pltpu.make_async_copy(shard[a], rows(a, *me), local_sems.at[a]) for a in range(na)]

        def first():
            return [copy(a, k, me, to, src=shard[a]) for a in range(na)
                    for k, to in enumerate([sibling] + [(*chip, c) for chip in chips])]

        def passed_on():
            return [copy(a, 4 + j, (*chip, c), sibling) for j, chip in enumerate(chips) for a in range(na)]

        return me, sibling, chips, c, copy, mine, first, passed_on

    def start(ins, outs, sems):
        *_, mine, first, _ = plan(ins, outs, sems)
        for cp in mine() + first():
            cp.start()

    def relay(ins, outs, sems):
        me, _, chips, c, copy, _, _, passed_on = plan(ins, outs, sems)
        for cp, (j, a) in zip(passed_on(), [(j, a) for j in range(3) for a in range(na)]):
            copy(a, 1 + j, (*chips[j], c), me).wait_recv()
            cp.start()

    def finish(ins, outs, sems):
        me, sibling, chips, c, copy, mine, first, passed_on = plan(ins, outs, sems)
        for a in range(na):
            copy(a, 0, sibling, me).wait_recv()
            for j, chip in enumerate(chips):
                copy(a, 4 + j, (*chip, 1 - c), me).wait_recv()
        for cp in first() + passed_on():
            cp.wait_send()
        for cp in mine():
            cp.wait()

    return _Comm(tuple(stacks),
                 tuple(jax.ShapeDtypeStruct((N_DEV * t.shape[1], t.shape[2]), t.dtype) for t in stacks),
                 (pltpu.SemaphoreType.DMA((na, 7)), pltpu.SemaphoreType.DMA((na, 7)), pltpu.SemaphoreType.DMA((na,))),
                 start, relay, finish, lambda: [_sibling()] + _same_core_of_other_chips(), GATHER_ID, relay_early)


def _exchange_comm(arrays, out_shape, n_copies, copies_of, peers, collective_id):
    na = len(arrays)

    def every(ins, outs, sems):
        send_sems, recv_sems = sems
        return [cp for a in range(na) for cp in copies_of(ins, outs, a, send_sems, recv_sems)]

    def start(ins, outs, sems):
        for cp in every(ins, outs, sems):
            cp.start()

    def finish(ins, outs, sems):
        for cp in every(ins, outs, sems):
            cp.wait()

    return _Comm(tuple(arrays), tuple(out_shape),
                 (pltpu.SemaphoreType.DMA((na, n_copies)), pltpu.SemaphoreType.DMA((na, n_copies))), start,
                 lambda ins, outs, sems: None, finish, peers, collective_id, False)


def _sibling_comm(grads):
    def copies_of(ins, outs, a, send_sems, recv_sems):
        x, y, c = _place()
        return [pltpu.make_async_remote_copy(
            src_ref=ins[a].at[chip, 1 - c], dst_ref=outs[a].at[chip],
            send_sem=send_sems.at[a, chip], recv_sem=recv_sems.at[a, chip],
            device_id=(x, y, 1 - c), device_id_type=MESH) for chip in range(4)]

    return _exchange_comm(grads, [jax.ShapeDtypeStruct((4,) + t.shape[2:], t.dtype) for t in grads], 4, copies_of,
                          lambda: [_sibling()], SIBLING_ID)


def _chip_comm(partials):
    def copies_of(ins, outs, a, send_sems, recv_sems):
        x, y, c = _place()
        chips = [(1 - x, y), (x, 1 - y), (1 - x, 1 - y)]
        return [pltpu.make_async_remote_copy(
            src_ref=ins[a].at[2 * cx + cy], dst_ref=outs[a].at[k],
            send_sem=send_sems.at[a, k], recv_sem=recv_sems.at[a, k],
            device_id=(cx, cy, c), device_id_type=MESH) for k, (cx, cy) in enumerate(chips)]

    return _exchange_comm(partials, [jax.ShapeDtypeStruct((3,) + t.shape[1:], t.dtype) for t in partials], 3, copies_of,
                          _same_core_of_other_chips, CHIP_ID)


def _pad_rows(t, rows):
    return jnp.pad(t, ((0, rows - t.shape[0]), (0, D_MODEL - t.shape[1])))


MIX_ROW, GROUP_ROW, MLP_ROW, FINAL_ROW, CONV_ROW, SINK_ROW = 0, 8, 16, 24, 32, 40
LOSS_ROW = FINAL_ROW + 1


def _pack_small(g_mix, g_group, g_mlp, g_final, conv, sinks, loss):
    final_and_loss = jnp.concatenate([g_final.reshape(1, D_MODEL), _pad_rows(loss, 1)], axis=0)
    return jnp.concatenate([
        _pad_rows(g_mix, 8), _pad_rows(g_group, 8), _pad_rows(g_mlp, 8), _pad_rows(final_and_loss, 8),
        _pad_rows(conv.reshape(DEPTH * 3, CONV_CH), 8), _pad_rows(sinks.reshape(1, DEPTH * 6), 8)], axis=0)


def kernel(x, w_in, conv_w, sinks, g_mix, g_group, w_o, g_mlp, w_ff_in, w_ff_out, g_final, loss_target, m_w_in, m_conv_w, m_sinks, m_g_mix, m_g_group, m_w_o, m_g_mlp, m_w_ff_in, m_w_ff_out, m_g_final, v_w_in, v_conv_w, v_sinks, v_g_mix, v_g_group, v_w_o, v_g_mlp, v_w_ff_in, v_w_ff_out, v_g_final):
    ax, ay, ac = _place()
    chip = 2 * ax + ay
    dev = 4 * ax + 2 * ay + ac
    pos = jnp.stack([chip, ac]).astype(jnp.int32)

    x0 = x.reshape(SEQ, D_MODEL)
    target = loss_target.reshape(SEQ, D_MODEL)

    stacks = [jnp.swapaxes(w_in, 1, 2).astype(BF16), w_o.astype(BF16),
              jnp.swapaxes(w_ff_in, 1, 2).astype(BF16), w_ff_out.astype(BF16)]
    shards = {(l, kind): (stack, l) for kind, stack in enumerate(stacks) for l in range(DEPTH)}
    conv_tile = jnp.pad(conv_w.reshape(DEPTH * 3, CONV_CH // N_DEV), ((0, 2), (0, LANES - CONV_CH // N_DEV)))
    wt_in0, conv_all = _comm_only(_gather_comm([shards[0, 0], (conv_tile[None], 0)]), "gather_first")
    conv_full = conv_all.reshape(N_DEV, 8, LANES)[:, :DEPTH * 3, :CONV_CH // N_DEV]
    conv_full = conv_full.transpose(1, 0, 2).reshape(DEPTH, 3, CONV_CH)

    dx, parts, small = _step(x0, target, shards, wt_in0, conv_full, sinks, g_mix, g_group, g_mlp, g_final, pos)
    return _finish(dx, parts, small, pos, dev, w_in, conv_w, sinks, g_mix, g_group, w_o, g_mlp, w_ff_in, w_ff_out, g_final, m_w_in, m_conv_w, m_sinks, m_g_mix, m_g_group, m_w_o, m_g_mlp, m_w_ff_in, m_w_ff_out, m_g_final, v_w_in, v_conv_w, v_sinks, v_g_mix, v_g_group, v_w_o, v_g_mlp, v_w_ff_in, v_w_ff_out, v_g_final)


FWD_CARRY = {(0, "in_proj"): ((1, 0),), (0, "window"): ((0, 1),), (0, "dilated"): ((0, 2),),
             (0, "mix_ff_in"): ((0, 3),), (0, "ff_out_in_proj"): ((1, 1), (1, 3)),
             (1, "dilated"): ((1, 2),)}
FWD_LATE_RELAY = {(0, "in_proj"), (0, "dilated"), (1, "dilated")}


def _step(x0, target, shards, wt_in0, conv_full, sinks, g_mix, g_group, g_mlp, g_final, pos):
    sink_lanes = jnp.repeat(sinks.reshape(DEPTH, 6), HEAD_DIM, axis=1)
    no_sink = jnp.full((1, A_WIDTH), NEG_BIG, F32)
    full = {(0, 0): wt_in0}

    def gather(stage, l):
        keys = FWD_CARRY.get((l, stage), ())
        return keys, (_gather_comm([shards[k] for k in keys], (l, stage) not in FWD_LATE_RELAY) if keys else None)

    def landed(keys, got):
        full.update(zip(keys, got))

    saved = []
    xc = x0
    keys, comm = gather("in_proj", 0)
    (z, h), got = _norm_mm(xc, g_mix[0:1], full[0, 0], "in_proj_0", comm)
    landed(keys, got)
    for l in range(DEPTH):
        sink_l = sink_lanes[l:l + 1]
        keys, comm = gather("window", l)
        (yc, *lse_c), got = _attn_fwd(z, sink_l, 1.0, QC_BLK, KC_BLK, VC_BLK, (1,), C_MAX_DIST, True,
                                     f"window_attn_{l}", comm)
        landed(keys, got)
        yb = _conv_fwd(z, conv_full[l], f"conv_{l}")
        keys, comm = gather("dilated", l)
        (ya, *lse_a), got = _attn_fwd(z, no_sink, 0.0, QA_BLK, KA_BLK, VA_BLK, DILATED_PATTERNS, A_MAX_DIST, False,
                                     f"dilated_attn_{l}", comm)
        landed(keys, got)
        keys, comm = gather("mix_ff_in", l)
        (y, x1, a, h2), got = _mix_ff_in(ya, yb, yc, g_group[l:l + 1], full[l, 1], xc, g_mlp[l:l + 1], full[l, 2],
                                         f"mix_ff_in_{l}", comm)
        landed(keys, got)
        saved.append((xc, z, h, ya, lse_a, yb, yc, lse_c, sink_l, y, x1, a, h2))
        if l + 1 < DEPTH:
            keys, comm = gather("ff_out_in_proj", l)
            (xc, z, h), got = _ff_out_in_proj(a, full[l, 3], x1, g_mix[l + 1:l + 2], full[l + 1, 0],
                                              f"ff_out_{l}_in_proj_{l + 1}", comm)
            landed(keys, got)

    loss_slab, dx, dxb, dg_final, du = _mm_res_loss(a, full[DEPTH - 1, 3], x1, g_final.reshape(1, D_MODEL), target,
                                                    f"ff_out_{DEPTH - 1}_loss")

    def by_owner(t):
        return t.reshape(4, 2, t.shape[0] // N_DEV, D_MODEL)

    def pair(l, kinds, grads, received):
        sums = _pair_sums(grads, received, pos, f"grad_pair_sums_{l}_{kinds[0]}{kinds[1]}")
        partial.update({(l, kind): t for kind, t in zip(kinds, sums)})

    partial, r2 = {}, {}
    dg_mix, dg_group, dg_mlp, dconv, dsinks = [None] * DEPTH, [None] * DEPTH, [None] * DEPTH, [None] * DEPTH, [None] * DEPTH
    for l in reversed(range(DEPTH)):
        xin, z, h, ya, lse_a, yb, yc, lse_c, sink_l, y, x1, a, h2 = saved[l]
        late = [(l + 1, 1), (l + 1, 0)] if l + 1 < DEPTH else []
        if l + 1 < DEPTH:
            (du,), _ = _mlp_bwd_act(dxb, full[l, 3], a, f"ff_out_bwd_{l}")
        (g3, g2), got = _mm_tn([(a, dxb), (du, h2)], f"grad_w_ff_{l}",
                               _chip_comm([partial[k] for k in late]) if late else None)
        r2.update(zip(late, got))
        g3, g2 = by_owner(g3), by_owner(g2)
        (dx1, dx1b, dg_mlp[l], dya, dyb, dyc, dg_group[l]), got = _ff_in_mix_bwd(
            du, full[l, 2], x1, dx, g_mlp[l:l + 1], full[l, 1], ya, yb, yc, g_group[l:l + 1],
            f"ff_in_mix_bwd_{l}", _sibling_comm([g3, g2]))
        pair(l, (3, 2), [g3, g2], got)
        early = [(l, 3), (l, 2)]
        (dz, _), got = _attn_bwd(z, dya, ya, lse_a, no_sink, None, QA_BLK, KA_BLK, VA_BLK, DILATED_PATTERNS,
                                 A_MAX_DIST, False, f"dilated_attn_bwd_{l}", _chip_comm([partial[k] for k in early]))
        r2.update(zip(early, got))
        dz, dcw = _conv_bwd(z, conv_full[l], dyb, dz, f"conv_bwd_{l}")
        (dz, dsink), _ = _attn_bwd(z, dyc, yc, lse_c, sink_l, dz, QC_BLK, KC_BLK, VC_BLK, (1,), C_MAX_DIST,
                                   True, f"window_attn_bwd_{l}")
        (g1, g0), _ = _mm_tn([(y, dx1b), (dz, h)], f"grad_w_o_in_{l}")
        g1, g0 = by_owner(g1), by_owner(g0)
        if l > 0:
            (dx, dxb, dg_mix[l]), got = _mm_nn_normbwd(dz, full[l, 0], xin, dx1, g_mix[l:l + 1], f"in_proj_bwd_{l}",
                                                      _sibling_comm([g1, g0]))
            pair(l, (1, 0), [g1, g0], got)
        else:
            got = _comm_only(_sibling_comm([g1, g0]), "grad_sibling_exchange_last")
            pair(l, (1, 0), [g1, g0], got)
            (dx, dxb, dg_mix[l]), got = _mm_nn_normbwd(dz, full[l, 0], xin, dx1, g_mix[l:l + 1], f"in_proj_bwd_{l}",
                                                      _chip_comm([partial[l, 1], partial[l, 0]]))
            r2[l, 1], r2[l, 0] = got
        dconv[l] = dcw[:3]
        dsinks[l] = dsink[0, ::HEAD_DIM]
    parts = {key: (partial[key], r2[key]) for key in partial}
    small = _pack_small(jnp.concatenate(dg_mix), jnp.concatenate(dg_group), jnp.concatenate(dg_mlp),
                        dg_final, jnp.stack(dconv), jnp.stack(dsinks), loss_slab[0:1])
    return dx, parts, small


def _finish(dx, parts, small, pos, dev, w_in, conv_w, sinks, g_mix, g_group, w_o, g_mlp, w_ff_in, w_ff_out, g_final, m_w_in, m_conv_w, m_sinks, m_g_mix, m_g_group, m_w_o, m_g_mlp, m_w_ff_in, m_w_ff_out, m_g_final, v_w_in, v_conv_w, v_sinks, v_g_mix, v_g_group, v_w_o, v_g_mlp, v_w_ff_in, v_w_ff_out, v_g_final):
    grad_x = dx.reshape(1, SEQ, D_MODEL)

    (small_all,) = _comm_only(_gather_comm([(small[None], 0)]), "gather_small_grads")
    row = lambda t: t.reshape(1, D_MODEL)
    sink_row = lambda t: _pad_rows(t.reshape(1, DEPTH * 6), 1)
    params = [(MIX_ROW, g_mix, m_g_mix, v_g_mix), (GROUP_ROW, g_group, m_g_group, v_g_group),
              (MLP_ROW, g_mlp, m_g_mlp, v_g_mlp), (FINAL_ROW, row(g_final), row(m_g_final), row(v_g_final)),
              (SINK_ROW, sink_row(sinks), sink_row(m_sinks), sink_row(v_sinks))]
    updated, conv_rows, loss_row = _small_sum_adamw(small_all.reshape(N_DEV, SMALL_ROWS, D_MODEL), params, "small_adamw")
    loss = loss_row[0, 0]
    (grad_g_mix, delta_g_mix, new_m_g_mix, new_v_g_mix), (grad_g_group, delta_g_group, new_m_g_group, new_v_g_group), \
        (grad_g_mlp, delta_g_mlp, new_m_g_mlp, new_v_g_mlp), final4, sinks4 = updated
    grad_g_final, delta_g_final, new_m_g_final, new_v_g_final = [t.reshape(D_MODEL) for t in final4]
    grad_sinks, delta_sinks, new_m_sinks, new_v_sinks = [t[0, :DEPTH * 6].reshape(DEPTH, 2, 3) for t in sinks4]
    conv_grad_full = conv_rows[:DEPTH * 3, :CONV_CH].reshape(DEPTH, 3, CONV_CH)
    cs = CONV_CH // N_DEV
    grad_conv_w = lax.dynamic_slice_in_dim(conv_grad_full, dev * cs, cs, axis=2)

    def tile_of(t):
        return jnp.pad(t.reshape(1, DEPTH * 3 * cs), ((0, 7), (0, 256 - DEPTH * 3 * cs)))

    cd, cm, cv = _adamw(tile_of(conv_w), tile_of(grad_conv_w), tile_of(m_conv_w), tile_of(v_conv_w), "conv_adamw")
    untile = lambda t: t[0, :DEPTH * 3 * cs].reshape(DEPTH, 3, cs)
    delta_conv_w, new_m_conv_w, new_v_conv_w = untile(cd), untile(cm), untile(cv)

    def big(kind, w, m, v, transpose, name):
        return _sum_adamw([parts[l, kind] for l in range(DEPTH)], w, m, v, pos, transpose, name)

    swap = lambda t: jnp.swapaxes(t, 1, 2)
    grad_w_in, delta_w_in, new_m_w_in, new_v_w_in = [
        swap(t) for t in big(0, swap(w_in), swap(m_w_in), swap(v_w_in), False, "adamw_w_in")]
    grad_w_o, delta_w_o, new_m_w_o, new_v_w_o = big(1, w_o, m_w_o, v_w_o, False, "adamw_w_o")
    grad_w_ff_in, delta_w_ff_in, new_m_w_ff_in, new_v_w_ff_in = big(2, w_ff_in, m_w_ff_in, v_w_ff_in, True, "adamw_w_ff_in")
    grad_w_ff_out, delta_w_ff_out, new_m_w_ff_out, new_v_w_ff_out = big(3, w_ff_out, m_w_ff_out, v_w_ff_out, False,
                                                                         "adamw_w_ff_out")

    return (loss, grad_x, grad_w_in, grad_conv_w, grad_sinks, grad_g_mix, grad_g_group, grad_w_o, grad_g_mlp,
            grad_w_ff_in, grad_w_ff_out, grad_g_final,
            delta_w_in, delta_conv_w, delta_sinks, delta_g_mix, delta_g_group, delta_w_o, delta_g_mlp,
            delta_w_ff_in, delta_w_ff_out, delta_g_final,
            new_m_w_in, new_m_conv_w, new_m_sinks, new_m_g_mix, new_m_g_group, new_m_w_o, new_m_g_mlp,
            new_m_w_ff_in, new_m_w_ff_out, new_m_g_final,
            new_v_w_in, new_v_conv_w, new_v_sinks, new_v_g_mix, new_v_g_group, new_v_w_o, new_v_g_mlp,
            new_v_w_ff_in, new_v_w_ff_out, new_v_g_final)
```

```python
from typing import Callable, NamedTuple

import jax
import jax.numpy as jnp
from jax import lax
from jax.experimental import pallas as pl
from jax.experimental.pallas import tpu as pltpu

F32 = jnp.float32
BF16 = jnp.bfloat16
MESH = pl.DeviceIdType.MESH

N_DEV = 8
SEQ = 4096
D_MODEL = 1024
DEPTH = 2
HEAD_DIM = 64
LANES = 128
A_WIDTH = 384
CONV_CH = 256
C_WIDTH = 384
IN_WIDTH = 2560
BLOCK = 128
DILATED_PATTERNS = (1, 4, 16)
A_MAX_DIST = 128
C_MAX_DIST = 127
EPS = 1e-6
SCALE = HEAD_DIM ** -0.5
NEG_BIG = -1e30
F32_TINY = 1.1754944e-38

QA_BLK, KA_BLK, VA_BLK = 0, 3, 6
GB_BLK, GC_BLK, XB_BLK = 9, 11, 13
QC_BLK, KC_BLK, VC_BLK = 15, 18, 19

ADAM_LR = 0.001
ADAM_B1 = 0.9
ADAM_B2 = 0.999
ADAM_EPS = 1e-08
ADAM_WD = 0.01
ADAM_STEP = 10

VMEM_LIMIT = 56 * 1024 * 1024
TILE_BUDGET = 46 * 1024 * 1024
ROW_TILE = 512
COL_CHUNK = 512
SMALL_ROWS = 48


def _dot_nn(a, b):
    return lax.dot_general(a, b, (((1,), (0,)), ((), ())), preferred_element_type=F32)


def _dot_nt(a, b):
    return lax.dot_general(a, b, (((1,), (1,)), ((), ())), preferred_element_type=F32)


def _dot_tn(a, b):
    return lax.dot_general(a, b, (((0,), (0,)), ((), ())), preferred_element_type=F32)


def _params(*sem, collective_id=None):
    return pltpu.CompilerParams(dimension_semantics=sem, vmem_limit_bytes=VMEM_LIMIT, collective_id=collective_id)


def _resident(shape):
    return pl.BlockSpec(shape, lambda i: (0,) * len(shape), pipeline_mode=pl.Buffered(1))


class _Late(NamedTuple):
    hbm: object
    vmem: object
    sem: object

    def fetch(self, needed_at_step):
        copy = pltpu.make_async_copy(self.hbm, self.vmem, self.sem)
        pl.when(pl.program_id(0) == 0)(copy.start)
        return lambda: pl.when(pl.program_id(0) == needed_at_step)(copy.wait)


RING = 3


def _ring_scratch(tm, t):
    return [pltpu.VMEM((RING, tm, t.shape[1]), t.dtype), pltpu.SemaphoreType.DMA((RING,))]


def _ring_tile(hbm_ref, buf, sems, tm, steps):
    s = pl.program_id(0)

    def copy(step):
        slot = step % RING
        first = step * tm if isinstance(step, int) else pl.multiple_of(step * tm, tm)
        return pltpu.make_async_copy(hbm_ref.at[pl.ds(first, tm), :], buf.at[slot], sems.at[slot])

    @pl.when(s == 0)
    def _():
        for step in range(min(RING - 1, steps)):
            copy(step).start()

    pl.when(s + RING - 1 < steps)(lambda: copy(s + RING - 1).start())
    copy(s).wait()
    return buf.at[s % RING]


LATE_SPEC = pl.BlockSpec(memory_space=pl.ANY)


def _late_scratch(t):
    return [pltpu.VMEM(t.shape, t.dtype), pltpu.SemaphoreType.DMA(())]


def _row_tile(row_bytes, resident_bytes):
    for tm in (ROW_TILE, ROW_TILE // 2):
        if 2 * tm * row_bytes + resident_bytes <= TILE_BUDGET:
            return tm
    return ROW_TILE // 4


def _rms_scale(t):
    return lax.rsqrt(jnp.mean(t * t, axis=-1, keepdims=True) + EPS)


def _rms_bwd(n, r, dn):
    return r * (dn - n * jnp.mean(dn * n, axis=-1, keepdims=True))


class _Comm(NamedTuple):
    arrays: tuple
    out_shape: tuple
    sems: tuple
    start: Callable
    relay: Callable
    finish: Callable
    peers: Callable
    collective_id: int
    relay_early: bool


def _handshake(comm):
    barrier = pltpu.get_barrier_semaphore()
    peers = comm.peers()
    for peer in peers:
        pl.semaphore_signal(barrier, inc=1, device_id=peer, device_id_type=MESH)
    pl.semaphore_wait(barrier, len(peers))


def _call(body, grid, in_specs, out_specs, out_shape, operands, name, scratch_shapes=(), comm=None, aliases=None):
    n_in, n_out, n_scr = len(in_specs), len(out_shape), len(scratch_shapes)
    aliases = dict(aliases or {})
    if comm is None:
        res = pl.pallas_call(body, grid=grid, in_specs=list(in_specs), out_specs=list(out_specs),
                             out_shape=list(out_shape), scratch_shapes=list(scratch_shapes),
                             input_output_aliases=aliases,
                             compiler_params=_params("arbitrary"), name=name)(*operands)
        return list(res), []
    c_in, c_out = len(comm.arrays), len(comm.out_shape)
    hbm = pl.BlockSpec(memory_space=pl.ANY)
    last = grid[0] - 1

    def carried(*refs):
        ins, cins = refs[:n_in], refs[n_in:n_in + c_in]
        o0 = n_in + c_in
        outs, couts = refs[o0:o0 + n_out], refs[o0 + n_out:o0 + n_out + c_out]
        s0 = o0 + n_out + c_out
        scr, sems = refs[s0:s0 + n_scr], refs[s0 + n_scr:]
        @pl.when(pl.program_id(0) == 0)
        def _():
            _handshake(comm)
            comm.start(cins, couts, sems)

        if comm.relay_early:
            pl.when(pl.program_id(0) == last)(lambda: comm.relay(cins, couts, sems))
        body(*ins, *outs, *scr)

        @pl.when(pl.program_id(0) == last)
        def _():
            if not comm.relay_early:
                comm.relay(cins, couts, sems)
            comm.finish(cins, couts, sems)

    res = pl.pallas_call(carried, grid=grid, in_specs=list(in_specs) + [hbm] * c_in,
                         out_specs=list(out_specs) + [hbm] * c_out, out_shape=list(out_shape) + list(comm.out_shape),
                         scratch_shapes=list(scratch_shapes) + list(comm.sems), input_output_aliases=aliases,
                         compiler_params=_params("arbitrary", collective_id=comm.collective_id),
                         name=name)(*operands, *comm.arrays)
    return list(res[:n_out]), list(res[n_out:])


def _comm_only(comm, name):
    hbm = pl.BlockSpec(memory_space=pl.ANY)
    c_in, c_out = len(comm.arrays), len(comm.out_shape)

    def body(*refs):
        ins, outs, sems = refs[:c_in], refs[c_in:c_in + c_out], refs[c_in + c_out:]
        _handshake(comm)
        comm.start(ins, outs, sems)
        comm.relay(ins, outs, sems)
        comm.finish(ins, outs, sems)

    return pl.pallas_call(body, in_specs=[hbm] * c_in, out_specs=[hbm] * c_out, out_shape=list(comm.out_shape),
                          scratch_shapes=list(comm.sems),
                          compiler_params=pltpu.CompilerParams(collective_id=comm.collective_id),
                          name=name)(*comm.arrays)


def _norm_mm(x, g, wt, name, comm=None):
    s, d = x.shape
    n = wt.shape[0]
    tm = _row_tile(4 * d + 4 * n + 2 * d, 2 * n * d)

    def body(x_hbm, g_ref, w_ref, o_ref, h_ref, x_ring, x_sems):
        xx = _ring_tile(x_hbm, x_ring, x_sems, tm, s // tm)[...]
        h = ((xx * _rms_scale(xx)) * g_ref[...]).astype(BF16)
        h_ref[...] = h
        for n0 in range(0, n, COL_CHUNK):
            o_ref[:, n0:n0 + COL_CHUNK] = _dot_nt(h, w_ref[n0:n0 + COL_CHUNK, :])

    return _call(
        body,
        grid=(s // tm,),
        in_specs=[LATE_SPEC,
                  pl.BlockSpec((1, d), lambda i: (0, 0)),
                  _resident((n, d))],
        out_specs=[pl.BlockSpec((tm, n), lambda i: (i, 0)),
                   pl.BlockSpec((tm, d), lambda i: (i, 0))],
        out_shape=[jax.ShapeDtypeStruct((s, n), F32), jax.ShapeDtypeStruct((s, d), BF16)],
        operands=(x, g, wt), name=name, scratch_shapes=_ring_scratch(tm, x), comm=comm)


def _ff_out_in_proj(a, w2, x1, g, wt, name, comm=None):
    s, f = a.shape
    d = w2.shape[1]
    n = wt.shape[0]
    tm = _row_tile(2 * f + 4 * d + 4 * d + 4 * n + 2 * d, 2 * f * d + 2 * n * d)

    def body(a_ref, w2_ref, x_ref, g_ref, w_ref, x2_ref, z_ref, h_ref):
        x2 = x_ref[...] + _dot_nn(a_ref[...], w2_ref[...])
        x2_ref[...] = x2
        h = ((x2 * _rms_scale(x2)) * g_ref[...]).astype(BF16)
        h_ref[...] = h
        for n0 in range(0, n, COL_CHUNK):
            z_ref[:, n0:n0 + COL_CHUNK] = _dot_nt(h, w_ref[n0:n0 + COL_CHUNK, :])

    rows = lambda w: pl.BlockSpec((tm, w), lambda i: (i, 0))
    return _call(
        body,
        grid=(s // tm,),
        in_specs=[rows(f), _resident((f, d)), rows(d), pl.BlockSpec((1, d), lambda i: (0, 0)), _resident((n, d))],
        out_specs=[rows(d), rows(n), rows(d)],
        out_shape=[jax.ShapeDtypeStruct((s, d), F32), jax.ShapeDtypeStruct((s, n), F32),
                   jax.ShapeDtypeStruct((s, d), BF16)],
        operands=(a, w2, x1, g, wt), name=name, comm=comm)


def _mix_ff_in(ya, yb, yc, gg, wo, x0, g_mlp, wt1, name, comm=None):
    s = ya.shape[0]
    d = wo.shape[1]
    f = wt1.shape[0]
    tm = _row_tile(4 * d + 4 * d + 2 * d + 4 * d + 2 * d + 2 * f, 2 * d * d + 2 * f * d)

    def body(ya_ref, yb_ref, yc_ref, gg_ref, wo_ref, x_ref, g_ref, w1_ref, y_ref, x1_ref, a_ref, h_ref):
        parts = []
        for ref in (ya_ref, yb_ref, yc_ref):
            t = ref[...]
            parts.append(t * _rms_scale(t))
        y = (jnp.concatenate(parts, axis=1) * gg_ref[...]).astype(BF16)
        y_ref[...] = y
        x1 = x_ref[...] + _dot_nn(y, wo_ref[...])
        x1_ref[...] = x1
        h = ((x1 * _rms_scale(x1)) * g_ref[...]).astype(BF16)
        h_ref[...] = h
        for n0 in range(0, f, COL_CHUNK):
            u = _dot_nt(h, w1_ref[n0:n0 + COL_CHUNK, :])
            a_ref[:, n0:n0 + COL_CHUNK] = jnp.square(jnp.maximum(u, 0.0)).astype(BF16)

    rows = lambda w: pl.BlockSpec((tm, w), lambda i: (i, 0))
    vec = pl.BlockSpec((1, d), lambda i: (0, 0))
    return _call(
        body,
        grid=(s // tm,),
        in_specs=[rows(A_WIDTH), rows(CONV_CH), rows(C_WIDTH), vec, _resident((d, d)), rows(d), vec, _resident((f, d))],
        out_specs=[rows(d), rows(d), rows(f), rows(d)],
        out_shape=[jax.ShapeDtypeStruct((s, d), BF16), jax.ShapeDtypeStruct((s, d), F32),
                   jax.ShapeDtypeStruct((s, f), BF16), jax.ShapeDtypeStruct((s, d), BF16)],
        operands=(ya, yb, yc, gg, wo, x0, g_mlp, wt1), name=name, comm=comm)


def _relu_from_square(av):
    return av * lax.rsqrt(jnp.maximum(av, F32_TINY))


def _mm_res_loss(a, w2, x1, g, target, name):
    s, f = a.shape
    d = w2.shape[1]
    tm = _row_tile(2 * f + 4 * d + 4 * d + 4 * d + 2 * d + 2 * f, 2 * f * d)

    def body(a_ref, w_ref, x_ref, g_ref, t_ref, loss_ref, dx_ref, dxb_ref, dg_ref, du_ref):
        @pl.when(pl.program_id(0) == 0)
        def _():
            loss_ref[...] = jnp.zeros_like(loss_ref)
            dg_ref[...] = jnp.zeros_like(dg_ref)

        xx = x_ref[...] + _dot_nn(a_ref[...], w_ref[...])
        r = _rms_scale(xx)
        n = xx * r
        gv = g_ref[...]
        err = n * gv - t_ref[...]
        per_tok = jnp.sum(err * err, axis=1, keepdims=True) * (1.0 / d)
        loss_ref[...] += 0.5 * jnp.sum(per_tok, axis=0, keepdims=True)
        dout = err * (1.0 / d)
        dg_ref[...] += jnp.sum(dout * n, axis=0, keepdims=True)
        dx = _rms_bwd(n, r, dout * gv)
        dx_ref[...] = dx
        dxb = dx.astype(BF16)
        dxb_ref[...] = dxb
        for n0 in range(0, f, COL_CHUNK):
            da = _dot_nt(dxb, w_ref[n0:n0 + COL_CHUNK, :])
            rl = _relu_from_square(a_ref[:, n0:n0 + COL_CHUNK].astype(F32))
            du_ref[:, n0:n0 + COL_CHUNK] = (da * (2.0 * rl)).astype(BF16)

    rows = lambda w: pl.BlockSpec((tm, w), lambda i: (i, 0))
    vec = pl.BlockSpec((1, d), lambda i: (0, 0))
    return pl.pallas_call(
        body,
        grid=(s // tm,),
        in_specs=[rows(f), _resident((f, d)), rows(d), vec, rows(d)],
        out_specs=[pl.BlockSpec((8, LANES), lambda i: (0, 0)), rows(d), rows(d), vec, rows(f)],
        out_shape=[jax.ShapeDtypeStruct((8, LANES), F32), jax.ShapeDtypeStruct((s, d), F32),
                   jax.ShapeDtypeStruct((s, d), BF16), jax.ShapeDtypeStruct((1, d), F32),
                   jax.ShapeDtypeStruct((s, f), BF16)],
        compiler_params=_params("arbitrary"),
        name=name,
    )(a, w2, x1, g, target)


def _mlp_bwd_act(dxb, w2, a, name, comm=None):
    s, d = dxb.shape
    f = w2.shape[0]
    tm = _row_tile(2 * d + 2 * f + 2 * f, 2 * f * d)

    def body(dx_ref, w_ref, a_ref, du_ref):
        dx = dx_ref[...]
        for n0 in range(0, f, COL_CHUNK):
            da = _dot_nt(dx, w_ref[n0:n0 + COL_CHUNK, :])
            rl = _relu_from_square(a_ref[:, n0:n0 + COL_CHUNK].astype(F32))
            du_ref[:, n0:n0 + COL_CHUNK] = (da * (2.0 * rl)).astype(BF16)

    return _call(
        body,
        grid=(s // tm,),
        in_specs=[pl.BlockSpec((tm, d), lambda i: (i, 0)),
                  _resident((f, d)),
                  pl.BlockSpec((tm, f), lambda i: (i, 0))],
        out_specs=[pl.BlockSpec((tm, f), lambda i: (i, 0))],
        out_shape=[jax.ShapeDtypeStruct((s, f), BF16)],
        operands=(dxb, w2, a), name=name, comm=comm)


def _mm_tn(pairs, name, comm=None):
    s, d = pairs[0][1].shape
    tn = 512
    tiles = [a.shape[1] // tn for a, _ in pairs]
    starts = [sum(tiles[:k]) for k in range(len(pairs))]

    def body(*refs):
        ins, outs = refs[:2 * len(pairs)], refs[2 * len(pairs):3 * len(pairs)]
        acc, late = refs[3 * len(pairs)], refs[3 * len(pairs) + 1:]
        j = pl.program_id(0)
        b_refs = [ins[1]]
        for k in range(1, len(pairs)):
            b_late = _Late(ins[2 * k + 1], late[2 * k - 2], late[2 * k - 1])
            b_late.fetch(starts[k])()
            b_refs.append(b_late.vmem)
        for k in range(len(pairs)):
            def run(a_ref=ins[2 * k], b_ref=b_refs[k], o_ref=outs[k]):
                for k0 in range(0, s, ROW_TILE):
                    part = _dot_tn(a_ref[k0:k0 + ROW_TILE, :], b_ref[k0:k0 + ROW_TILE, :])
                    if k0 == 0:
                        acc[...] = part
                    else:
                        acc[...] += part
                o_ref[...] = acc[...].astype(BF16)

            pl.when((j >= starts[k]) & (j < starts[k] + tiles[k]))(run)

    def tile_of(k):
        return lambda j: jnp.clip(j - starts[k], 0, tiles[k] - 1)

    in_specs, out_specs = [], []
    for k in range(len(pairs)):
        in_specs += [pl.BlockSpec((s, tn), lambda j, t=tile_of(k): (0, t(j))), _resident((s, d)) if k == 0 else LATE_SPEC]
        out_specs.append(pl.BlockSpec((tn, d), lambda j, t=tile_of(k): (t(j), 0)))
    late = [scratch for _, b in pairs[1:] for scratch in _late_scratch(b)]
    return _call(
        body,
        grid=(sum(tiles),),
        in_specs=in_specs,
        out_specs=out_specs,
        out_shape=[jax.ShapeDtypeStruct((a.shape[1], d), BF16) for a, _ in pairs],
        operands=tuple(t for pair in pairs for t in pair), name=name,
        scratch_shapes=[pltpu.VMEM((tn, d), F32)] + late, comm=comm)


def _mm_nn_normbwd(dact, wt, x, dres, g, name, comm=None):
    s, kdim = dact.shape
    d = wt.shape[1]
    tm = _row_tile(2 * kdim + 4 * d + 4 * d + 4 * d + 2 * d, 2 * kdim * d)

    def body(a_ref, w_ref, x_hbm, r_hbm, g_ref, o_ref, ob_ref, dg_ref, x_ring, x_sems, r_ring, r_sems):
        @pl.when(pl.program_id(0) == 0)
        def _():
            dg_ref[...] = jnp.zeros_like(dg_ref)

        x_ref = _ring_tile(x_hbm, x_ring, x_sems, tm, s // tm)
        r_ref = _ring_tile(r_hbm, r_ring, r_sems, tm, s // tm)
        dh = _dot_nn(a_ref[...], w_ref[...])
        xx = x_ref[...]
        r = _rms_scale(xx)
        n = xx * r
        dg_ref[...] += jnp.sum(dh * n, axis=0, keepdims=True)
        dx = r_ref[...] + _rms_bwd(n, r, dh * g_ref[...])
        o_ref[...] = dx
        ob_ref[...] = dx.astype(BF16)

    return _call(
        body,
        grid=(s // tm,),
        in_specs=[pl.BlockSpec((tm, kdim), lambda i: (i, 0)),
                  _resident((kdim, d)),
                  LATE_SPEC,
                  LATE_SPEC,
                  pl.BlockSpec((1, d), lambda i: (0, 0))],
        out_specs=[pl.BlockSpec((tm, d), lambda i: (i, 0)),
                   pl.BlockSpec((tm, d), lambda i: (i, 0)),
                   pl.BlockSpec((1, d), lambda i: (0, 0))],
        out_shape=[jax.ShapeDtypeStruct((s, d), F32), jax.ShapeDtypeStruct((s, d), BF16),
                   jax.ShapeDtypeStruct((1, d), F32)],
        operands=(dact, wt, x, dres, g), name=name, scratch_shapes=_ring_scratch(tm, x) + _ring_scratch(tm, dres),
        comm=comm)


def _ff_in_mix_bwd(du, wt1, x1, dres, g_mlp, wo, ya, yb, yc, gg, name, comm=None):
    s, f = du.shape
    d = wt1.shape[1]
    widths = (A_WIDTH, CONV_CH, C_WIDTH)
    tm = _row_tile(2 * f + 4 * d + 4 * d + 4 * d + 2 * d + 4 * d + 4 * d, 2 * f * d + 2 * d * d)

    def body(du_ref, w1_ref, x_ref, r_ref, g_ref, wo_ref, ya_ref, yb_ref, yc_ref, gg_ref,
             dx_ref, dxb_ref, dg_ref, da_ref, db_ref, dc_ref, dgg_ref):
        @pl.when(pl.program_id(0) == 0)
        def _():
            dg_ref[...] = jnp.zeros_like(dg_ref)
            dgg_ref[...] = jnp.zeros_like(dgg_ref)

        dh = _dot_nn(du_ref[...], w1_ref[...])
        xx = x_ref[...]
        r = _rms_scale(xx)
        n = xx * r
        dg_ref[...] += jnp.sum(dh * n, axis=0, keepdims=True)
        dx = r_ref[...] + _rms_bwd(n, r, dh * g_ref[...])
        dx_ref[...] = dx
        dxb = dx.astype(BF16)
        dxb_ref[...] = dxb

        dy = _dot_nt(dxb, wo_ref[...])
        gv = gg_ref[...]
        off = 0
        dgs = []
        for ref, out, w in zip((ya_ref, yb_ref, yc_ref), (da_ref, db_ref, dc_ref), widths):
            t = ref[...]
            r = _rms_scale(t)
            n = t * r
            dyg = dy[:, off:off + w]
            dgs.append(jnp.sum(dyg * n, axis=0, keepdims=True))
            out[...] = _rms_bwd(n, r, dyg * gv[:, off:off + w])
            off += w
        dgg_ref[...] += jnp.concatenate(dgs, axis=1)

    rows = lambda w: pl.BlockSpec((tm, w), lambda i: (i, 0))
    vec = pl.BlockSpec((1, d), lambda i: (0, 0))
    return _call(
        body,
        grid=(s // tm,),
        in_specs=[rows(f), _resident((f, d)), rows(d), rows(d), vec, _resident((d, d)),
                  rows(A_WIDTH), rows(CONV_CH), rows(C_WIDTH), vec],
        out_specs=[rows(d), rows(d), vec, rows(A_WIDTH), rows(CONV_CH), rows(C_WIDTH), vec],
        out_shape=[jax.ShapeDtypeStruct((s, d), F32), jax.ShapeDtypeStruct((s, d), BF16), jax.ShapeDtypeStruct((1, d), F32),
                   jax.ShapeDtypeStruct((s, A_WIDTH), F32), jax.ShapeDtypeStruct((s, CONV_CH), F32),
                   jax.ShapeDtypeStruct((s, C_WIDTH), F32), jax.ShapeDtypeStruct((1, d), F32)],
        operands=(du, wt1, x1, dres, g_mlp, wo, ya, yb, yc, gg), name=name, comm=comm)


CONV_CHUNK = 256
CONV_HALO = 8


def _conv_fwd(z, cw, name):
    s = z.shape[0]
    nch = s // CONV_CHUNK

    def body(gb_ref, gc_ref, xb_ref, w_ref, o_ref, us):
        us[pl.ds(0, CONV_HALO), :] = jnp.zeros((CONV_HALO, LANES), F32)
        us[pl.ds(CONV_HALO, s), :] = gc_ref[...] * xb_ref[...]
        w0, w1, w2 = w_ref[0:1, :], w_ref[1:2, :], w_ref[2:3, :]

        def chunk(c, carry):
            st = pl.multiple_of(c * CONV_CHUNK, CONV_CHUNK)
            ext = us[pl.ds(st, CONV_CHUNK + CONV_HALO), :]
            y = (w0 * ext[CONV_HALO - 2:CONV_HALO - 2 + CONV_CHUNK]
                 + w1 * ext[CONV_HALO - 1:CONV_HALO - 1 + CONV_CHUNK]
                 + w2 * ext[CONV_HALO:])
            o_ref[pl.ds(st, CONV_CHUNK), :] = gb_ref[pl.ds(st, CONV_CHUNK), :] * y
            return carry

        lax.fori_loop(0, nch, chunk, 0)

    col = lambda blk: pl.BlockSpec((s, LANES), lambda j, blk=blk: (0, blk + j))
    return pl.pallas_call(
        body,
        grid=(CONV_CH // LANES,),
        in_specs=[col(GB_BLK), col(GC_BLK), col(XB_BLK), pl.BlockSpec((3, LANES), lambda j: (0, j))],
        out_specs=pl.BlockSpec((s, LANES), lambda j: (0, j)),
        out_shape=jax.ShapeDtypeStruct((s, CONV_CH), F32),
        scratch_shapes=[pltpu.VMEM((s + CONV_HALO, LANES), F32)],
        compiler_params=_params("parallel"),
        name=name,
    )(z, z, z, cw)


def _conv_bwd(z, cw, dyb, dz, name):
    s = z.shape[0]
    nch = s // CONV_CHUNK
    ncol = CONV_CH // LANES

    def body(gb_ref, gc_ref, xb_ref, w_ref, dy_ref, dz_in, dz_ref, dw_ref, us, ds_, dgb_ref, dgc_ref, dxb_ref, sems):
        j = pl.program_id(0)

        def to_dz(staged, blk, k):
            cols = pl.ds(pl.multiple_of((blk + j) * LANES, LANES), LANES)
            return pltpu.make_async_copy(staged, dz_ref.at[:, cols], sems.at[k])

        copies = [to_dz(dgb_ref, GB_BLK, 0), to_dz(dgc_ref, GC_BLK, 1), to_dz(dxb_ref, XB_BLK, 2)]

        @pl.when(j > 0)
        def _():
            for cp in copies:
                cp.wait()

        us[pl.ds(0, CONV_HALO), :] = jnp.zeros((CONV_HALO, LANES), F32)
        us[pl.ds(CONV_HALO, s), :] = gc_ref[...] * xb_ref[...]
        ds_[pl.ds(s, CONV_HALO), :] = jnp.zeros((CONV_HALO, LANES), F32)
        ds_[pl.ds(0, s), :] = dy_ref[...] * gb_ref[...]
        w0, w1, w2 = w_ref[0:1, :], w_ref[1:2, :], w_ref[2:3, :]
        zero = jnp.zeros((1, LANES), F32)

        def chunk(c, carry):
            a0, a1, a2 = carry
            st = pl.multiple_of(c * CONV_CHUNK, CONV_CHUNK)
            rows = pl.ds(st, CONV_CHUNK)
            ext = us[pl.ds(st, CONV_CHUNK + CONV_HALO), :]
            um2 = ext[CONV_HALO - 2:CONV_HALO - 2 + CONV_CHUNK]
            um1 = ext[CONV_HALO - 1:CONV_HALO - 1 + CONV_CHUNK]
            u0 = ext[CONV_HALO:]
            dext = ds_[pl.ds(st, CONV_CHUNK + CONV_HALO), :]
            dc0 = dext[:CONV_CHUNK]
            du = w2 * dc0 + w1 * dext[1:1 + CONV_CHUNK] + w0 * dext[2:2 + CONV_CHUNK]
            yconv = w0 * um2 + w1 * um1 + w2 * u0
            dgb_ref[rows, :] = (dy_ref[rows, :] * yconv).astype(BF16)
            dgc_ref[rows, :] = (du * xb_ref[rows, :]).astype(BF16)
            dxb_ref[rows, :] = (du * gc_ref[rows, :]).astype(BF16)
            a0 = a0 + jnp.sum(dc0 * um2, axis=0, keepdims=True)
            a1 = a1 + jnp.sum(dc0 * um1, axis=0, keepdims=True)
            a2 = a2 + jnp.sum(dc0 * u0, axis=0, keepdims=True)
            return a0, a1, a2

        a0, a1, a2 = lax.fori_loop(0, nch, chunk, (zero, zero, zero))
        dw_ref[...] = jnp.concatenate([a0, a1, a2, jnp.zeros((5, LANES), F32)], axis=0)
        for cp in copies:
            cp.start()

        @pl.when(j == ncol - 1)
        def _():
            for cp in copies:
                cp.wait()

    col = lambda blk: pl.BlockSpec((s, LANES), lambda j, blk=blk: (0, blk + j))
    hbm = pl.BlockSpec(memory_space=pl.ANY)
    return pl.pallas_call(
        body,
        grid=(ncol,),
        in_specs=[col(GB_BLK), col(GC_BLK), col(XB_BLK), pl.BlockSpec((3, LANES), lambda j: (0, j)),
                  pl.BlockSpec((s, LANES), lambda j: (0, j)), hbm],
        out_specs=[hbm, pl.BlockSpec((8, LANES), lambda j: (0, j))],
        out_shape=[jax.ShapeDtypeStruct(dz.shape, dz.dtype), jax.ShapeDtypeStruct((8, CONV_CH), F32)],
        scratch_shapes=[pltpu.VMEM((s + CONV_HALO, LANES), F32), pltpu.VMEM((s + CONV_HALO, LANES), F32)]
        + [pltpu.VMEM((s, LANES), BF16)] * 3 + [pltpu.SemaphoreType.DMA((3,))],
        input_output_aliases={5: 0},
        compiler_params=_params("arbitrary"),
        name=name,
    )(z, z, z, cw, dyb, dz)


ATTN_ROWS = 512
ATTN_UNROLL = 8


def _band_rows(b, d, r):
    base = pl.multiple_of(b * (BLOCK * d), BLOCK)
    prev = jnp.maximum(base - BLOCK * d, 0)
    if d == 1:
        return pl.ds(base, BLOCK), pl.ds(pl.multiple_of(prev, BLOCK), BLOCK)
    return pl.ds(base + r, BLOCK, stride=d), pl.ds(prev + r, BLOCK, stride=d)


def _write_band_bias(bias_ref, max_dist):
    qi = lax.broadcasted_iota(jnp.int32, (BLOCK, 2 * BLOCK), 0)
    kj = lax.broadcasted_iota(jnp.int32, (BLOCK, 2 * BLOCK), 1)
    dist = BLOCK + qi - kj
    band = (dist >= 0) & (dist <= max_dist)
    bias_ref[0:BLOCK, :] = jnp.where(band, 0.0, -jnp.inf)
    bias_ref[BLOCK:2 * BLOCK, :] = jnp.where(band & (kj >= BLOCK), 0.0, -jnp.inf)


def _band_bias(bias_ref, b):
    bias = bias_ref[pl.ds(pl.multiple_of(jnp.where(b > 0, 0, BLOCK), BLOCK), BLOCK), :]
    return jnp.concatenate([bias, bias], axis=0)


def _kv_halves(pair):
    zero = jnp.zeros((1, LANES), jnp.int32)
    return zero + (pair >> 1), zero + ((pair + 1) >> 1)


def _stack_heads(t, head0, halves=None):
    top, bottom = jnp.where(head0, t, 0.0), jnp.where(head0, 0.0, t)
    if halves is not None:
        top = jnp.where(halves[0] == 1, pltpu.roll(top, HEAD_DIM, 1), top)
        bottom = jnp.where(halves[1] == 0, pltpu.roll(bottom, HEAD_DIM, 1), bottom)
    return jnp.concatenate([top, bottom], axis=0).astype(BF16)


def _unstack_heads(t, head0, halves=None):
    top, bottom = t[:BLOCK], t[BLOCK:]
    if halves is not None:
        top = jnp.where(halves[0] == 1, pltpu.roll(top, HEAD_DIM, 1), top)
        bottom = jnp.where(halves[1] == 0, pltpu.roll(bottom, HEAD_DIM, 1), bottom)
    return jnp.where(head0, top, bottom)


def _block_loops(s, patterns, unroll, one_block):
    for n, d in enumerate(patterns):
        nb = (s // BLOCK) // d
        ur = min(unroll, d)
        ub = unroll // ur
        for r0 in range(0, d, ur):
            def trip(i, carry, n=n, d=d, r0=r0, ur=ur, ub=ub):
                for u in range(ub):
                    for r in range(r0, r0 + ur):
                        one_block(i * ub + u, d, r, n == 0)
                return carry
            lax.fori_loop(0, nb // ub, trip, 0)


def _attn_fwd(z, m_init, l_init, q_blk, k_blk, v_blk, patterns, max_dist, gqa, name, comm=None):
    s = z.shape[0]
    npair = 3

    def body(q_ref, k_ref, v_ref, mi_ref, o_ref, lse0_ref, lse1_ref, bias_scr, m_scr, l_scr, *kv_scr):
        head0 = lax.broadcasted_iota(jnp.int32, (1, LANES), 1) < HEAD_DIM
        _write_band_bias(bias_scr, max_dist)
        ones = jnp.ones((2 * BLOCK, LANES), BF16)
        k_src, v_src = kv_scr if gqa else (k_ref, v_ref)
        if gqa:
            half = (lax.broadcasted_iota(jnp.int32, (1, LANES), 1) >= HEAD_DIM).astype(jnp.int32)
            swap = ((pl.program_id(0) + half) >> 1) != half

            def expand(c, carry):
                rows = pl.ds(pl.multiple_of(c * ATTN_ROWS, ATTN_ROWS), ATTN_ROWS)
                k_src[rows, :] = jnp.where(swap, pltpu.roll(k_ref[rows, :], HEAD_DIM, 1), k_ref[rows, :])
                v_src[rows, :] = jnp.where(swap, pltpu.roll(v_ref[rows, :], HEAD_DIM, 1), v_ref[rows, :])
                return carry

            lax.fori_loop(0, s // ATTN_ROWS, expand, 0)

        def one_block(b, d, r, first):
            rq, rp = _band_rows(b, d, r)
            q2 = _stack_heads(q_ref[rq, :] * SCALE, head0)
            k2 = jnp.concatenate([k_src[rp, :], k_src[rq, :]], axis=0).astype(BF16)
            v2 = jnp.concatenate([v_src[rp, :], v_src[rq, :]], axis=0).astype(BF16)
            sc = _dot_nt(q2, k2) + _band_bias(bias_scr, b)
            mb = jnp.max(sc, axis=1, keepdims=True)
            p = jnp.exp(sc - mb).astype(BF16)
            ob = _dot_nn(p, jnp.concatenate([v2, ones], axis=1))
            m_blk = _unstack_heads(jnp.broadcast_to(mb, (2 * BLOCK, LANES)), head0)
            l_blk = _unstack_heads(ob[:, LANES:], head0)
            o_blk = _unstack_heads(ob[:, :LANES], head0)
            if first and l_init == 0.0:
                m_new, l_new, o_new = m_blk, l_blk, o_blk
            else:
                if first:
                    m_old, l_old, o_old = jnp.broadcast_to(mi_ref[...], (BLOCK, LANES)), l_init, 0.0
                else:
                    m_old, l_old, o_old = m_scr[rq, :], l_scr[rq, :], o_ref[rq, :]
                m_new = jnp.maximum(m_old, m_blk)
                a_old = jnp.exp(m_old - m_new)
                a_blk = jnp.exp(m_blk - m_new)
                l_new = l_old * a_old + l_blk * a_blk
                o_new = o_old * a_old + o_blk * a_blk
            o_ref[rq, :], l_scr[rq, :], m_scr[rq, :] = o_new, l_new, m_new

        _block_loops(s, patterns, ATTN_UNROLL, one_block)

        def fin(c, carry):
            rows = pl.ds(pl.multiple_of(c * ATTN_ROWS, ATTN_ROWS), ATTN_ROWS)
            l = l_scr[rows, :]
            o_ref[rows, :] = o_ref[rows, :] / l
            lse = m_scr[rows, :] + jnp.log(l)
            swapped = pltpu.roll(lse, HEAD_DIM, 1)
            lse0_ref[rows, :] = jnp.where(head0, lse, swapped)
            lse1_ref[rows, :] = jnp.where(head0, swapped, lse)
            return carry

        lax.fori_loop(0, s // ATTN_ROWS, fin, 0)

    kv = (lambda blk: pl.BlockSpec((s, LANES), lambda j, blk=blk: (0, blk), pipeline_mode=pl.Buffered(1))) if gqa \
        else (lambda blk: pl.BlockSpec((s, LANES), lambda j, blk=blk: (0, blk + j)))
    own = pl.BlockSpec((s, LANES), lambda j: (0, j))
    return _call(
        body,
        grid=(npair,),
        in_specs=[pl.BlockSpec((s, LANES), lambda j: (0, q_blk + j)), kv(k_blk), kv(v_blk),
                  pl.BlockSpec((1, LANES), lambda j: (0, j))],
        out_specs=[own, own, own],
        out_shape=[jax.ShapeDtypeStruct((s, npair * LANES), F32)] * 3,
        operands=(z, z, z, m_init), name=name,
        scratch_shapes=[pltpu.VMEM((2 * BLOCK, 2 * BLOCK), F32)] + [pltpu.VMEM((s, LANES), F32)] * (4 if gqa else 2),
        comm=comm)


def _attn_bwd(z, do, o, lse, m_init, dz, q_blk, k_blk, v_blk, patterns, max_dist, gqa, name, comm=None):
    s = z.shape[0]
    npair = 3
    n_dz_in = 0 if dz is None else 1

    def body(q_ref, k_ref, v_ref, do_ref, o_ref, lse0_ref, lse1_ref, mi_ref, *rest):
        (dz_ref, dm_ref, dq_acc, dk_acc, dv_acc, dl0_scr, dl1_scr, bias_scr,
         dq_out, dk_out, dv_out, out_sems) = rest[n_dz_in:]
        pair = pl.program_id(0)
        head0 = lax.broadcasted_iota(jnp.int32, (1, LANES), 1) < HEAD_DIM
        halves = _kv_halves(pair) if gqa else None
        _write_band_bias(bias_scr, max_dist)

        def zero_kv():
            def f(c, carry):
                rows = pl.ds(pl.multiple_of(c * ATTN_ROWS, ATTN_ROWS), ATTN_ROWS)
                dk_acc[rows, :] = jnp.zeros((ATTN_ROWS, LANES), F32)
                dv_acc[rows, :] = jnp.zeros((ATTN_ROWS, LANES), F32)
                return carry
            lax.fori_loop(0, s // ATTN_ROWS, f, 0)

        if gqa:
            pl.when(pair == 0)(zero_kv)
        else:
            zero_kv()

        def prep(c, dm):
            rows = pl.ds(pl.multiple_of(c * ATTN_ROWS, ATTN_ROWS), ATTN_ROWS)
            dq_acc[rows, :] = jnp.zeros((ATTN_ROWS, LANES), F32)
            prod = do_ref[rows, :] * o_ref[rows, :]
            d0 = jnp.sum(jnp.where(head0, prod, 0.0), axis=1, keepdims=True)
            d1 = jnp.sum(jnp.where(head0, 0.0, prod), axis=1, keepdims=True)
            dl0_scr[rows, :] = jnp.broadcast_to(d0, (ATTN_ROWS, LANES))
            dl1_scr[rows, :] = jnp.broadcast_to(d1, (ATTN_ROWS, LANES))
            lse_own = jnp.where(head0, lse0_ref[rows, :], lse1_ref[rows, :])
            psink = jnp.exp(mi_ref[...] - lse_own)
            return dm - jnp.sum(psink * jnp.where(head0, d0, d1), axis=0, keepdims=True)

        dm_ref[...] = lax.fori_loop(0, s // ATTN_ROWS, prep, jnp.zeros((1, LANES), F32))

        def one_block(b, d, r, first):
            rq, rp = _band_rows(b, d, r)
            q2 = _stack_heads(q_ref[rq, :] * SCALE, head0, halves)
            do2 = _stack_heads(do_ref[rq, :], head0, halves)
            k2 = jnp.concatenate([k_ref[rp, :], k_ref[rq, :]], axis=0).astype(BF16)
            v2 = jnp.concatenate([v_ref[rp, :], v_ref[rq, :]], axis=0).astype(BF16)
            lse2 = jnp.concatenate([lse0_ref[rq, :], lse1_ref[rq, :]], axis=0)
            dl2 = jnp.concatenate([dl0_scr[rq, :], dl1_scr[rq, :]], axis=0)
            lse2 = jnp.concatenate([lse2, lse2], axis=1)
            dl2 = jnp.concatenate([dl2, dl2], axis=1)
            p = jnp.exp(_dot_nt(q2, k2) + _band_bias(bias_scr, b) - lse2)
            dp = _dot_nt(do2, v2)
            dsc = (p * (dp - dl2)).astype(BF16)
            dq2 = _unstack_heads(_dot_nn(dsc, k2), head0, halves)
            dk2 = _dot_tn(dsc, q2)
            dv2 = _dot_tn(p.astype(BF16), do2)
            dq_acc[rq, :] += dq2 * SCALE
            dk_acc[rp, :] += dk2[:BLOCK]
            dk_acc[rq, :] += dk2[BLOCK:]
            dv_acc[rp, :] += dv2[:BLOCK]
            dv_acc[rq, :] += dv2[BLOCK:]

        _block_loops(s, patterns, ATTN_UNROLL, one_block)

        def to_dz(staged, blk, k):
            cols = pl.ds(pl.multiple_of(blk * LANES, LANES), LANES)
            return pltpu.make_async_copy(staged, dz_ref.at[:, cols], out_sems.at[k])

        last_pair = pair == npair - 1
        q_copy = to_dz(dq_out, q_blk + pair, 0)
        kv_copies = [to_dz(dk_out, k_blk + (0 if gqa else pair), 1), to_dz(dv_out, v_blk + (0 if gqa else pair), 2)]

        @pl.when(pair > 0)
        def _():
            for cp in [q_copy] + ([] if gqa else kv_copies):
                cp.wait()

        def stage(acc, out):
            def f(c, carry):
                rows = pl.ds(pl.multiple_of(c * ATTN_ROWS, ATTN_ROWS), ATTN_ROWS)
                out[rows, :] = acc[rows, :].astype(BF16)
                return carry
            lax.fori_loop(0, s // ATTN_ROWS, f, 0)

        def stage_kv():
            stage(dk_acc, dk_out)
            stage(dv_acc, dv_out)
            for cp in kv_copies:
                cp.start()

        stage(dq_acc, dq_out)
        q_copy.start()
        if gqa:
            pl.when(last_pair)(stage_kv)
        else:
            stage_kv()

        @pl.when(last_pair)
        def _():
            for cp in [q_copy] + kv_copies:
                cp.wait()

    own = pl.BlockSpec((s, LANES), lambda j: (0, j))
    hbm = pl.BlockSpec(memory_space=pl.ANY)
    if gqa:
        kv = lambda blk: pl.BlockSpec((s, LANES), lambda j, blk=blk: (0, blk), pipeline_mode=pl.Buffered(1))
    else:
        kv = lambda blk: pl.BlockSpec((s, LANES), lambda j, blk=blk: (0, blk + j))
    in_specs = [pl.BlockSpec((s, LANES), lambda j: (0, q_blk + j)), kv(k_blk), kv(v_blk), own, own, own, own,
                pl.BlockSpec((1, LANES), lambda j: (0, j))]
    operands = (z, z, z, do, o, lse[0], lse[1], m_init)
    return _call(
        body,
        grid=(npair,),
        in_specs=in_specs + [hbm] * n_dz_in,
        out_specs=[hbm, pl.BlockSpec((1, LANES), lambda j: (0, j))],
        out_shape=[jax.ShapeDtypeStruct((s, IN_WIDTH), BF16), jax.ShapeDtypeStruct((1, npair * LANES), F32)],
        operands=operands + (() if dz is None else (dz,)), name=name,
        scratch_shapes=[pltpu.VMEM((s, LANES), F32)] * 5 + [pltpu.VMEM((2 * BLOCK, 2 * BLOCK), F32)]
        + [pltpu.VMEM((s, LANES), BF16)] * 3 + [pltpu.SemaphoreType.DMA((3,))],
        comm=comm, aliases={} if dz is None else {len(in_specs): 0})


def _adamw_math(w, g, m, v):
    m = ADAM_B1 * m + (1.0 - ADAM_B1) * g
    v = ADAM_B2 * v + (1.0 - ADAM_B2) * (g * g)
    m_hat = m / (1.0 - ADAM_B1 ** ADAM_STEP)
    v_hat = v / (1.0 - ADAM_B2 ** ADAM_STEP)
    delta = -ADAM_LR * (m_hat / (jnp.sqrt(v_hat) + ADAM_EPS) + ADAM_WD * w)
    return delta, m, v


def _adamw(w, g, m, v, name):
    rows, cols = w.shape
    tr = min(rows, 256)

    def body(w_ref, g_ref, m_ref, v_ref, d_ref, nm_ref, nv_ref):
        d_ref[...], nm_ref[...], nv_ref[...] = _adamw_math(w_ref[...], g_ref[...], m_ref[...], v_ref[...])

    spec = pl.BlockSpec((tr, cols), lambda i: (i, 0))
    return pl.pallas_call(
        body,
        grid=(rows // tr,),
        in_specs=[spec] * 4,
        out_specs=[spec] * 3,
        out_shape=[jax.ShapeDtypeStruct((rows, cols), F32)] * 3,
        compiler_params=_params("parallel"),
        name=name,
    )(w, g, m, v)


def _sum_adamw(parts, w, m, v, pos, transpose, name):
    assert len(parts) == DEPTH == 2
    (p0, r0), (p1, r1) = parts
    _, rows, cols = p0.shape
    tr = 256 if rows % 256 == 0 else rows
    nt = rows // tr

    def body(pos_ref, p0_ref, r0_ref, p1_ref, r1_ref, w_ref, m_ref, v_ref, g_ref, d_ref, nm_ref, nv_ref):
        def run(p_ref, r_ref):
            g = ((p_ref[...].astype(F32) + r_ref[0].astype(F32)) + r_ref[1].astype(F32)) + r_ref[2].astype(F32)
            if transpose:
                g = g.T
            g_ref[...] = g
            d_ref[...], nm_ref[...], nv_ref[...] = _adamw_math(w_ref[...], g, m_ref[...], v_ref[...])

        layer0 = pl.program_id(0) < nt
        pl.when(layer0)(lambda: run(p0_ref, r0_ref))
        pl.when(jnp.logical_not(layer0))(lambda: run(p1_ref, r1_ref))

    def tile0(i):
        return jnp.minimum(i, nt - 1)

    def tile1(i):
        return jnp.maximum(i - nt, 0)

    if transpose:
        w_spec = pl.BlockSpec((None, cols, tr), lambda i, q: (i // nt, 0, i % nt))
    else:
        w_spec = pl.BlockSpec((None, tr, cols), lambda i, q: (i // nt, i % nt, 0))
    return pl.pallas_call(
        body,
        grid_spec=pltpu.PrefetchScalarGridSpec(
            num_scalar_prefetch=1,
            grid=(DEPTH * nt,),
            in_specs=[pl.BlockSpec((None, tr, cols), lambda i, q: (q[0], tile0(i), 0)),
                      pl.BlockSpec((3, tr, cols), lambda i, q: (0, tile0(i), 0)),
                      pl.BlockSpec((None, tr, cols), lambda i, q: (q[0], tile1(i), 0)),
                      pl.BlockSpec((3, tr, cols), lambda i, q: (0, tile1(i), 0)),
                      w_spec, w_spec, w_spec],
            out_specs=[w_spec] * 4,
        ),
        out_shape=[jax.ShapeDtypeStruct(w.shape, F32)] * 4,
        compiler_params=_params("arbitrary"),
        name=name,
    )(pos, p0, r0, p1, r1, w, m, v)


def _small_sum_adamw(gathered, params, name):
    _, rows, cols = gathered.shape
    n = len(params)

    def body(ga_ref, *refs):
        ins, outs, (g_scr,) = refs[:3 * n], refs[3 * n:7 * n + 2], refs[7 * n + 2:]
        g = ga_ref[0]
        for i in range(1, N_DEV):
            g = g + ga_ref[i]
        g_scr[...] = g
        for k, (row0, w, _, _) in enumerate(params):
            w_ref, m_ref, v_ref = ins[3 * k:3 * k + 3]
            gk = g_scr[row0:row0 + w.shape[0], :]
            outs[4 * k][...] = gk
            outs[4 * k + 1][...], outs[4 * k + 2][...], outs[4 * k + 3][...] = _adamw_math(
                w_ref[...], gk, m_ref[...], v_ref[...])
        outs[4 * n][...] = g_scr[CONV_ROW:CONV_ROW + 8, :]
        outs[4 * n + 1][...] = g_scr[LOSS_ROW:LOSS_ROW + 1, :]

    out_shape = []
    for _, w, _, _ in params:
        out_shape += [jax.ShapeDtypeStruct(w.shape, F32)] * 4
    out_shape += [jax.ShapeDtypeStruct((8, cols), F32), jax.ShapeDtypeStruct((1, cols), F32)]
    res = pl.pallas_call(
        body,
        out_shape=out_shape,
        scratch_shapes=[pltpu.VMEM((rows, cols), F32)],
        name=name,
    )(gathered, *[t for _, w, m, v in params for t in (w, m, v)])
    return [res[4 * k:4 * k + 4] for k in range(n)], res[4 * n], res[4 * n + 1]


def _pair_sums(g4s, r1s, pos, name):
    n = len(g4s)

    def body(pos_ref, *refs):
        for g_ref, r_ref, o_ref in zip(refs[:n], refs[n:2 * n], refs[2 * n:]):
            o_ref[...] = (g_ref[...].astype(F32) + r_ref[...].astype(F32)).astype(BF16)

    block = lambda t: pl.BlockSpec((None,) + t.shape[1:], lambda i, p: (i, 0, 0))
    return pl.pallas_call(
        body,
        grid_spec=pltpu.PrefetchScalarGridSpec(
            num_scalar_prefetch=1,
            grid=(4,),
            in_specs=[pl.BlockSpec((None, None) + g.shape[2:], lambda i, p: (i, p[1], 0, 0)) for g in g4s]
            + [block(r) for r in r1s],
            out_specs=[block(r) for r in r1s],
        ),
        out_shape=[jax.ShapeDtypeStruct(r.shape, BF16) for r in r1s],
        compiler_params=_params("parallel"),
        name=name,
    )(pos, *g4s, *r1s)


GATHER_ID, CHIP_ID, SIBLING_ID = 0, 1, 2


def _place():
    return lax.axis_index("x"), lax.axis_index("y"), lax.axis_index("c")


def _sibling():
    x, y, c = _place()
    return (x, y, 1 - c)


def _same_core_of_other_chips():
    x, y, c = _place()
    return [(1 - x, y, c), (x, 1 - y, c), (1 - x, 1 - y, c)]


def _gather_comm(shards, relay_early=False):
    na = len(shards)
    stacks, index = zip(*shards)

    def plan(ins, outs, sems):
        send_sems, recv_sems, local_sems = sems
        x, y, c = _place()
        me, sibling = (x, y, c), (x, y, 1 - c)
        chips = [(1 - x, y), (x, 1 - y), (1 - x, 1 - y)]
        shard = [ins[a].at[index[a]] for a in range(na)]

        def rows(a, px, py, pc):
            m = shard[a].shape[0]
            return outs[a].at[pl.ds((4 * px + 2 * py + pc) * m, m), :]

        def copy(a, k, block, to, src=None):
            return pltpu.make_async_remote_copy(
                src_ref=rows(a, *block) if src is None else src, dst_ref=rows(a, *block),
                send_sem=send_sems.at[a, k], recv_sem=recv_sems.at[a, k], device_id=to, device_id_type=MESH)

        def mine():
            return [pltpu.make_async_copy(shard[a], rows(a, *me), local_sems.at[a]) for a in range(na)]

        def first():
            return [copy(a, k, me, to, src=shard[a]) for a in range(na)
                    for k, to in enumerate([sibling] + [(*chip, c) for chip in chips])]

        def passed_on():
            return [copy(a, 4 + j, (*chip, c), sibling) for j, chip in enumerate(chips) for a in range(na)]

        return me, sibling, chips, c, copy, mine, first, passed_on

    def start(ins, outs, sems):
        *_, mine, first, _ = plan(ins, outs, sems)
        for cp in mine() + first():
            cp.start()

    def relay(ins, outs, sems):
        me, _, chips, c, copy, _, _, passed_on = plan(ins, outs, sems)
        for cp, (j, a) in zip(passed_on(), [(j, a) for j in range(3) for a in range(na)]):
            copy(a, 1 + j, (*chips[j], c), me).wait_recv()
            cp.start()

    def finish(ins, outs, sems):
        me, sibling, chips, c, copy, mine, first, passed_on = plan(ins, outs, sems)
        for a in range(na):
            copy(a, 0, sibling, me).wait_recv()
            for j, chip in enumerate(chips):
                copy(a, 4 + j, (*chip, 1 - c), me).wait_recv()
        for cp in first() + passed_on():
            cp.wait_send()
        for cp in mine():
            cp.wait()

    return _Comm(tuple(stacks),
                 tuple(jax.ShapeDtypeStruct((N_DEV * t.shape[1], t.shape[2]), t.dtype) for t in stacks),
                 (pltpu.SemaphoreType.DMA((na, 7)), pltpu.SemaphoreType.DMA((na, 7)), pltpu.SemaphoreType.DMA((na,))),
                 start, relay, finish, lambda: [_sibling()] + _same_core_of_other_chips(), GATHER_ID, relay_early)


def _exchange_comm(arrays, out_shape, n_copies, copies_of, peers, collective_id):
    na = len(arrays)

    def every(ins, outs, sems):
        send_sems, recv_sems = sems
        return [cp for a in range(na) for cp in copies_of(ins, outs, a, send_sems, recv_sems)]

    def start(ins, outs, sems):
        for cp in every(ins, outs, sems):
            cp.start()

    def finish(ins, outs, sems):
        for cp in every(ins, outs, sems):
            cp.wait()

    return _Comm(tuple(arrays), tuple(out_shape),
                 (pltpu.SemaphoreType.DMA((na, n_copies)), pltpu.SemaphoreType.DMA((na, n_copies))), start,
                 lambda ins, outs, sems: None, finish, peers, collective_id, False)


def _sibling_comm(grads):
    def copies_of(ins, outs, a, send_sems, recv_sems):
        x, y, c = _place()
        return [pltpu.make_async_remote_copy(
            src_ref=ins[a].at[chip, 1 - c], dst_ref=outs[a].at[chip],
            send_sem=send_sems.at[a, chip], recv_sem=recv_sems.at[a, chip],
            device_id=(x, y, 1 - c), device_id_type=MESH) for chip in range(4)]

    return _exchange_comm(grads, [jax.ShapeDtypeStruct((4,) + t.shape[2:], t.dtype) for t in grads], 4, copies_of,
                          lambda: [_sibling()], SIBLING_ID)


def _chip_comm(partials):
    def copies_of(ins, outs, a, send_sems, recv_sems):
        x, y, c = _place()
        chips = [(1 - x, y), (x, 1 - y), (1 - x, 1 - y)]
        return [pltpu.make_async_remote_copy(
            src_ref=ins[a].at[2 * cx + cy], dst_ref=outs[a].at[k],
            send_sem=send_sems.at[a, k], recv_sem=recv_sems.at[a, k],
            device_id=(cx, cy, c), device_id_type=MESH) for k, (cx, cy) in enumerate(chips)]

    return _exchange_comm(partials, [jax.ShapeDtypeStruct((3,) + t.shape[1:], t.dtype) for t in partials], 3, copies_of,
                          _same_core_of_other_chips, CHIP_ID)


def _pad_rows(t, rows):
    return jnp.pad(t, ((0, rows - t.shape[0]), (0, D_MODEL - t.shape[1])))


MIX_ROW, GROUP_ROW, MLP_ROW, FINAL_ROW, CONV_ROW, SINK_ROW = 0, 8, 16, 24, 32, 40
LOSS_ROW = FINAL_ROW + 1


def _pack_small(g_mix, g_group, g_mlp, g_final, conv, sinks, loss):
    final_and_loss = jnp.concatenate([g_final.reshape(1, D_MODEL), _pad_rows(loss, 1)], axis=0)
    return jnp.concatenate([
        _pad_rows(g_mix, 8), _pad_rows(g_group, 8), _pad_rows(g_mlp, 8), _pad_rows(final_and_loss, 8),
        _pad_rows(conv.reshape(DEPTH * 3, CONV_CH), 8), _pad_rows(sinks.reshape(1, DEPTH * 6), 8)], axis=0)


def kernel(x, w_in, conv_w, sinks, g_mix, g_group, w_o, g_mlp, w_ff_in, w_ff_out, g_final, loss_target, m_w_in, m_conv_w, m_sinks, m_g_mix, m_g_group, m_w_o, m_g_mlp, m_w_ff_in, m_w_ff_out, m_g_final, v_w_in, v_conv_w, v_sinks, v_g_mix, v_g_group, v_w_o, v_g_mlp, v_w_ff_in, v_w_ff_out, v_g_final):
    ax, ay, ac = _place()
    chip = 2 * ax + ay
    dev = 4 * ax + 2 * ay + ac
    pos = jnp.stack([chip, ac]).astype(jnp.int32)

    x0 = x.reshape(SEQ, D_MODEL)
    target = loss_target.reshape(SEQ, D_MODEL)

    stacks = [jnp.swapaxes(w_in, 1, 2).astype(BF16), w_o.astype(BF16),
              jnp.swapaxes(w_ff_in, 1, 2).astype(BF16), w_ff_out.astype(BF16)]
    shards = {(l, kind): (stack, l) for kind, stack in enumerate(stacks) for l in range(DEPTH)}
    conv_tile = jnp.pad(conv_w.reshape(DEPTH * 3, CONV_CH // N_DEV), ((0, 2), (0, LANES - CONV_CH // N_DEV)))
    wt_in0, conv_all = _comm_only(_gather_comm([shards[0, 0], (conv_tile[None], 0)]), "gather_first")
    conv_full = conv_all.reshape(N_DEV, 8, LANES)[:, :DEPTH * 3, :CONV_CH // N_DEV]
    conv_full = conv_full.transpose(1, 0, 2).reshape(DEPTH, 3, CONV_CH)

    dx, parts, small = _step(x0, target, shards, wt_in0, conv_full, sinks, g_mix, g_group, g_mlp, g_final, pos)
    return _finish(dx, parts, small, pos, dev, w_in, conv_w, sinks, g_mix, g_group, w_o, g_mlp, w_ff_in, w_ff_out, g_final, m_w_in, m_conv_w, m_sinks, m_g_mix, m_g_group, m_w_o, m_g_mlp, m_w_ff_in, m_w_ff_out, m_g_final, v_w_in, v_conv_w, v_sinks, v_g_mix, v_g_group, v_w_o, v_g_mlp, v_w_ff_in, v_w_ff_out, v_g_final)


FWD_CARRY = {(0, "in_proj"): ((1, 0),), (0, "window"): ((0, 1),), (0, "dilated"): ((0, 2),),
             (0, "mix_ff_in"): ((0, 3),), (0, "ff_out_in_proj"): ((1, 1), (1, 3)),
             (1, "dilated"): ((1, 2),)}
FWD_LATE_RELAY = {(0, "in_proj"), (0, "dilated"), (1, "dilated")}


def _step(x0, target, shards, wt_in0, conv_full, sinks, g_mix, g_group, g_mlp, g_final, pos):
    sink_lanes = jnp.repeat(sinks.reshape(DEPTH, 6), HEAD_DIM, axis=1)
    no_sink = jnp.full((1, A_WIDTH), NEG_BIG, F32)
    full = {(0, 0): wt_in0}

    def gather(stage, l):
        keys = FWD_CARRY.get((l, stage), ())
        return keys, (_gather_comm([shards[k] for k in keys], (l, stage) not in FWD_LATE_RELAY) if keys else None)

    def landed(keys, got):
        full.update(zip(keys, got))

    saved = []
    xc = x0
    keys, comm = gather("in_proj", 0)
    (z, h), got = _norm_mm(xc, g_mix[0:1], full[0, 0], "in_proj_0", comm)
    landed(keys, got)
    for l in range(DEPTH):
        sink_l = sink_lanes[l:l + 1]
        keys, comm = gather("window", l)
        (yc, *lse_c), got = _attn_fwd(z, sink_l, 1.0, QC_BLK, KC_BLK, VC_BLK, (1,), C_MAX_DIST, True,
                                     f"window_attn_{l}", comm)
        landed(keys, got)
        yb = _conv_fwd(z, conv_full[l], f"conv_{l}")
        keys, comm = gather("dilated", l)
        (ya, *lse_a), got = _attn_fwd(z, no_sink, 0.0, QA_BLK, KA_BLK, VA_BLK, DILATED_PATTERNS, A_MAX_DIST, False,
                                     f"dilated_attn_{l}", comm)
        landed(keys, got)
        keys, comm = gather("mix_ff_in", l)
        (y, x1, a, h2), got = _mix_ff_in(ya, yb, yc, g_group[l:l + 1], full[l, 1], xc, g_mlp[l:l + 1], full[l, 2],
                                         f"mix_ff_in_{l}", comm)
        landed(keys, got)
        saved.append((xc, z, h, ya, lse_a, yb, yc, lse_c, sink_l, y, x1, a, h2))
        if l + 1 < DEPTH:
            keys, comm = gather("ff_out_in_proj", l)
            (xc, z, h), got = _ff_out_in_proj(a, full[l, 3], x1, g_mix[l + 1:l + 2], full[l + 1, 0],
                                              f"ff_out_{l}_in_proj_{l + 1}", comm)
            landed(keys, got)

    loss_slab, dx, dxb, dg_final, du = _mm_res_loss(a, full[DEPTH - 1, 3], x1, g_final.reshape(1, D_MODEL), target,
                                                    f"ff_out_{DEPTH - 1}_loss")

    def by_owner(t):
        return t.reshape(4, 2, t.shape[0] // N_DEV, D_MODEL)

    def pair(l, kinds, grads, received):
        sums = _pair_sums(grads, received, pos, f"grad_pair_sums_{l}_{kinds[0]}{kinds[1]}")
        partial.update({(l, kind): t for kind, t in zip(kinds, sums)})

    partial, r2 = {}, {}
    dg_mix, dg_group, dg_mlp, dconv, dsinks = [None] * DEPTH, [None] * DEPTH, [None] * DEPTH, [None] * DEPTH, [None] * DEPTH
    for l in reversed(range(DEPTH)):
        xin, z, h, ya, lse_a, yb, yc, lse_c, sink_l, y, x1, a, h2 = saved[l]
        late = [(l + 1, 1), (l + 1, 0)] if l + 1 < DEPTH else []
        if l + 1 < DEPTH:
            (du,), _ = _mlp_bwd_act(dxb, full[l, 3], a, f"ff_out_bwd_{l}")
        (g3, g2), got = _mm_tn([(a, dxb), (du, h2)], f"grad_w_ff_{l}",
                               _chip_comm([partial[k] for k in late]) if late else None)
        r2.update(zip(late, got))
        g3, g2 = by_owner(g3), by_owner(g2)
        (dx1, dx1b, dg_mlp[l], dya, dyb, dyc, dg_group[l]), got = _ff_in_mix_bwd(
            du, full[l, 2], x1, dx, g_mlp[l:l + 1], full[l, 1], ya, yb, yc, g_group[l:l + 1],
            f"ff_in_mix_bwd_{l}", _sibling_comm([g3, g2]))
        pair(l, (3, 2), [g3, g2], got)
        early = [(l, 3), (l, 2)]
        (dz, _), got = _attn_bwd(z, dya, ya, lse_a, no_sink, None, QA_BLK, KA_BLK, VA_BLK, DILATED_PATTERNS,
                                 A_MAX_DIST, False, f"dilated_attn_bwd_{l}", _chip_comm([partial[k] for k in early]))
        r2.update(zip(early, got))
        dz, dcw = _conv_bwd(z, conv_full[l], dyb, dz, f"conv_bwd_{l}")
        (dz, dsink), _ = _attn_bwd(z, dyc, yc, lse_c, sink_l, dz, QC_BLK, KC_BLK, VC_BLK, (1,), C_MAX_DIST,
                                   True, f"window_attn_bwd_{l}")
        (g1, g0), _ = _mm_tn([(y, dx1b), (dz, h)], f"grad_w_o_in_{l}")
        g1, g0 = by_owner(g1), by_owner(g0)
        if l > 0:
            (dx, dxb, dg_mix[l]), got = _mm_nn_normbwd(dz, full[l, 0], xin, dx1, g_mix[l:l + 1], f"in_proj_bwd_{l}",
                                                      _sibling_comm([g1, g0]))
            pair(l, (1, 0), [g1, g0], got)
        else:
            got = _comm_only(_sibling_comm([g1, g0]), "grad_sibling_exchange_last")
            pair(l, (1, 0), [g1, g0], got)
            (dx, dxb, dg_mix[l]), got = _mm_nn_normbwd(dz, full[l, 0], xin, dx1, g_mix[l:l + 1], f"in_proj_bwd_{l}",
                                                      _chip_comm([partial[l, 1], partial[l, 0]]))
            r2[l, 1], r2[l, 0] = got
        dconv[l] = dcw[:3]
        dsinks[l] = dsink[0, ::HEAD_DIM]
    parts = {key: (partial[key], r2[key]) for key in partial}
    small = _pack_small(jnp.concatenate(dg_mix), jnp.concatenate(dg_group), jnp.concatenate(dg_mlp),
                        dg_final, jnp.stack(dconv), jnp.stack(dsinks), loss_slab[0:1])
    return dx, parts, small


def _finish(dx, parts, small, pos, dev, w_in, conv_w, sinks, g_mix, g_group, w_o, g_mlp, w_ff_in, w_ff_out, g_final, m_w_in, m_conv_w, m_sinks, m_g_mix, m_g_group, m_w_o, m_g_mlp, m_w_ff_in, m_w_ff_out, m_g_final, v_w_in, v_conv_w, v_sinks, v_g_mix, v_g_group, v_w_o, v_g_mlp, v_w_ff_in, v_w_ff_out, v_g_final):
    grad_x = dx.reshape(1, SEQ, D_MODEL)

    (small_all,) = _comm_only(_gather_comm([(small[None], 0)]), "gather_small_grads")
    row = lambda t: t.reshape(1, D_MODEL)
    sink_row = lambda t: _pad_rows(t.reshape(1, DEPTH * 6), 1)
    params = [(MIX_ROW, g_mix, m_g_mix, v_g_mix), (GROUP_ROW, g_group, m_g_group, v_g_group),
              (MLP_ROW, g_mlp, m_g_mlp, v_g_mlp), (FINAL_ROW, row(g_final), row(m_g_final), row(v_g_final)),
              (SINK_ROW, sink_row(sinks), sink_row(m_sinks), sink_row(v_sinks))]
    updated, conv_rows, loss_row = _small_sum_adamw(small_all.reshape(N_DEV, SMALL_ROWS, D_MODEL), params, "small_adamw")
    loss = loss_row[0, 0]
    (grad_g_mix, delta_g_mix, new_m_g_mix, new_v_g_mix), (grad_g_group, delta_g_group, new_m_g_group, new_v_g_group), \
        (grad_g_mlp, delta_g_mlp, new_m_g_mlp, new_v_g_mlp), final4, sinks4 = updated
    grad_g_final, delta_g_final, new_m_g_final, new_v_g_final = [t.reshape(D_MODEL) for t in final4]
    grad_sinks, delta_sinks, new_m_sinks, new_v_sinks = [t[0, :DEPTH * 6].reshape(DEPTH, 2, 3) for t in sinks4]
    conv_grad_full = conv_rows[:DEPTH * 3, :CONV_CH].reshape(DEPTH, 3, CONV_CH)
    cs = CONV_CH // N_DEV
    grad_conv_w = lax.dynamic_slice_in_dim(conv_grad_full, dev * cs, cs, axis=2)

    def tile_of(t):
        return jnp.pad(t.reshape(1, DEPTH * 3 * cs), ((0, 7), (0, 256 - DEPTH * 3 * cs)))

    cd, cm, cv = _adamw(tile_of(conv_w), tile_of(grad_conv_w), tile_of(m_conv_w), tile_of(v_conv_w), "conv_adamw")
    untile = lambda t: t[0, :DEPTH * 3 * cs].reshape(DEPTH, 3, cs)
    delta_conv_w, new_m_conv_w, new_v_conv_w = untile(cd), untile(cm), untile(cv)

    def big(kind, w, m, v, transpose, name):
        return _sum_adamw([parts[l, kind] for l in range(DEPTH)], w, m, v, pos, transpose, name)

    swap = lambda t: jnp.swapaxes(t, 1, 2)
    grad_w_in, delta_w_in, new_m_w_in, new_v_w_in = [
        swap(t) for t in big(0, swap(w_in), swap(m_w_in), swap(v_w_in), False, "adamw_w_in")]
    grad_w_o, delta_w_o, new_m_w_o, new_v_w_o = big(1, w_o, m_w_o, v_w_o, False, "adamw_w_o")
    grad_w_ff_in, delta_w_ff_in, new_m_w_ff_in, new_v_w_ff_in = big(2, w_ff_in, m_w_ff_in, v_w_ff_in, True, "adamw_w_ff_in")
    grad_w_ff_out, delta_w_ff_out, new_m_w_ff_out, new_v_w_ff_out = big(3, w_ff_out, m_w_ff_out, v_w_ff_out, False,
                                                                         "adamw_w_ff_out")

    return (loss, grad_x, grad_w_in, grad_conv_w, grad_sinks, grad_g_mix, grad_g_group, grad_w_o, grad_g_mlp,
            grad_w_ff_in, grad_w_ff_out, grad_g_final,
            delta_w_in, delta_conv_w, delta_sinks, delta_g_mix, delta_g_group, delta_w_o, delta_g_mlp,
            delta_w_ff_in, delta_w_ff_out, delta_g_final,
            new_m_w_in, new_m_conv_w, new_m_sinks, new_m_g_mix, new_m_g_group, new_m_w_o, new_m_g_mlp,
            new_m_w_ff_in, new_m_w_ff_out, new_m_g_final,
            new_v_w_in, new_v_conv_w, new_v_sinks, new_v_g_mix, new_v_g_group, new_v_w_o, new_v_g_mlp,
            new_v_w_ff_in, new_v_w_ff_out, new_v_g_final)
```
